```python
import math
import jax, jax.numpy as jnp
from jax import lax
import numpy as np


D_MODEL = 1024
BATCH = 16
SEQ = 2048
DEPTH = 1

HEAD_DIM = 64
FOX_HEADS = 12
DIL_HEADS = 12
MEM_HEADS = 4
MEM_HEAD_DIM = 128
MEM_LEN = 256
FOX_W = FOX_HEADS * HEAD_DIM
DIL_W = DIL_HEADS * HEAD_DIM
MEM_W = MEM_HEADS * MEM_HEAD_DIM
MIX_W = FOX_W + DIL_W + MEM_W
DILATIONS = ((128, 1), (512, 4), (2048, 16))
BLOCK = 128
ROPE_THETA = 500000.0
ROPE_DIM = HEAD_DIM // 4
RMS_EPS = 1e-6
NEG_INF = -1e30
IN_SIZES = [FOX_W] * 4 + [FOX_HEADS] + [DIL_W] * 4 + [MEM_W] * 2
IN_W = sum(IN_SIZES)

kernel_name = 'hymba_fox_dilated_memory_block'


def rmsnorm(x, g):
    xf = x.astype(jnp.float32)
    y = xf * lax.rsqrt(jnp.mean(xf * xf, axis=-1, keepdims=True) + RMS_EPS)
    return (y * g.astype(jnp.float32)).astype(x.dtype)


def rope_partial(t, pos):
    half = ROPE_DIM // 2
    inv_freq = 1.0 / (ROPE_THETA ** (jnp.arange(0, ROPE_DIM, 2, dtype=jnp.float32) / ROPE_DIM))
    ang = pos[:, None] * inv_freq[None, :]
    cos = jnp.cos(ang)[None, :, None, :]
    sin = jnp.sin(ang)[None, :, None, :]
    tr = t[..., :ROPE_DIM].astype(jnp.float32)
    t1, t2 = tr[..., :half], tr[..., half:]
    rot = jnp.concatenate([t1 * cos - t2 * sin, t2 * cos + t1 * sin], axis=-1)
    return jnp.concatenate([rot.astype(t.dtype), t[..., ROPE_DIM:]], axis=-1)


def forgetting_attention(q, k, v, logf):
    B, S, H, E = q.shape
    scale = 1.0 / math.sqrt(E)
    c = jnp.cumsum(logf, axis=1).transpose(0, 2, 1)
    vf = v.astype(jnp.float32)
    outs = []
    for i in range(S // BLOCK):
        q0, q1 = i * BLOCK, (i + 1) * BLOCK
        s = jnp.einsum('bqhe,bkhe->bhqk', q[:, q0:q1], k[:, :q1]).astype(jnp.float32) * scale
        s = s + c[:, :, q0:q1, None] - c[:, :, None, :q1]
        mask = (q0 + jnp.arange(BLOCK))[:, None] >= jnp.arange(q1)[None, :]
        s = jnp.where(mask[None, None], s, NEG_INF)
        p = jax.nn.softmax(s, axis=-1)
        outs.append(jnp.einsum('bhqk,bkhe->bqhe', p, vf[:, :q1]))
    return jnp.concatenate(outs, axis=1)


def dilated_pattern(q, k, v, dilation, n_steps):
    B, S, H, E = q.shape
    L = S // dilation
    nb = -(-L // BLOCK)
    Lp = nb * BLOCK
    scale = 1.0 / math.sqrt(E)

    def to_blocks(t):
        t = t.reshape(B, L, dilation, H, E)
        t = jnp.pad(t, ((0, 0), (0, Lp - L), (0, 0), (0, 0), (0, 0)))
        return t.reshape(B, nb, BLOCK, dilation, H, E)

    def with_prev(t):
        prev = jnp.pad(t, ((0, 0), (1, 0), (0, 0), (0, 0), (0, 0), (0, 0)))[:, :nb]
        return jnp.concatenate([prev, t], axis=2)

    qb = to_blocks(q)
    kc = with_prev(to_blocks(k))
    vc = with_prev(to_blocks(v)).astype(jnp.float32)
    s = jnp.einsum('bnqrhe,bnkrhe->bnrhqk', qb, kc).astype(jnp.float32) * scale
    lq = jnp.arange(nb)[:, None] * BLOCK + jnp.arange(BLOCK)[None, :]
    lk = (jnp.arange(nb)[:, None] - 1) * BLOCK + jnp.arange(2 * BLOCK)[None, :]
    delta = lq[:, :, None] - lk[:, None, :]
    mask = (delta >= 0) & (delta <= n_steps) & (lk[:, None, :] >= 0)
    s = jnp.where(mask[None, :, None, None], s, NEG_INF)
    m = jnp.max(s, axis=-1, keepdims=True)
    e = jnp.exp(s - m)
    den = jnp.sum(e, axis=-1)
    num = jnp.einsum('bnrhqk,bnkrhe->bnqrhe', e, vc)
    num = num.reshape(B, Lp, dilation, H, E)[:, :L].reshape(B, S, H, E)

    def rows(t):
        t = t.transpose(0, 1, 4, 2, 3).reshape(B, Lp, dilation, H)
        return t[:, :L].reshape(B, S, H)

    return num, rows(den), rows(m[..., 0])


def dilated_attention(q, k, v):
    parts = [dilated_pattern(q, k, v, d, w // d) for (w, d) in DILATIONS]
    m_all = parts[0][2]
    for p in parts[1:]:
        m_all = jnp.maximum(m_all, p[2])
    num_tot = 0.0
    den_tot = 0.0
    for num, den, m in parts:
        w = jnp.exp(m - m_all)
        num_tot = num_tot + num * w[..., None]
        den_tot = den_tot + den * w
    return num_tot / den_tot[..., None]


def memory_attention(q, mk, mv):
    scale = 1.0 / math.sqrt(q.shape[-1])
    s = jnp.einsum('bqhe,bkhe->bhqk', q, mk).astype(jnp.float32) * scale
    p = jax.nn.softmax(s, axis=-1)
    return jnp.einsum('bhqk,bkhe->bqhe', p, mv.astype(jnp.float32))


def _fwd_setup_inputs(seed: int = 0) -> dict:
    key = jax.random.key(seed)
    ks = jax.random.split(key, 10)
    f32 = jnp.float32
    x = jax.random.normal(ks[0], (BATCH, SEQ, D_MODEL), f32)
    mem = jax.random.normal(ks[1], (BATCH, MEM_LEN, D_MODEL), f32)
    norm_g = 1.0 + 0.02 * jax.random.normal(ks[2], (DEPTH, D_MODEL), f32)
    w_in = jax.random.normal(ks[3], (DEPTH, D_MODEL, IN_W), f32) * D_MODEL ** -0.5
    b_forget = jax.random.uniform(ks[4], (DEPTH, FOX_HEADS), f32, 1.0, 4.0)
    mem_norm_g = 1.0 + 0.02 * jax.random.normal(ks[5], (DEPTH, D_MODEL), f32)
    w_mem_kv = jax.random.normal(ks[6], (DEPTH, D_MODEL, 2 * MEM_W), f32) * D_MODEL ** -0.5
    w_out = jax.random.normal(ks[7], (DEPTH, MIX_W, D_MODEL), f32) * MIX_W ** -0.5
    final_norm_g = 1.0 + 0.02 * jax.random.normal(ks[8], (D_MODEL,), f32)
    return {'x': x, 'mem': mem, 'norm_g': norm_g, 'w_in': w_in, 'b_forget': b_forget,
            'mem_norm_g': mem_norm_g, 'w_mem_kv': w_mem_kv, 'w_out': w_out,
            'final_norm_g': final_norm_g}


def _fwd_reference(x, mem, norm_g, w_in, b_forget, mem_norm_g, w_mem_kv, w_out, final_norm_g):
    B, S, _ = x.shape
    pos = jnp.arange(S, dtype=jnp.float32)
    split_idx = np.cumsum(IN_SIZES)[:-1].tolist()
    for l in range(DEPTH):
        h = rmsnorm(x, norm_g[l])
        proj = h @ w_in[l]
        (fq, fk, fv, fg, flog, dq, dk, dv, dg, mq, mg) = jnp.split(proj, split_idx, axis=-1)

        logf = jax.nn.log_sigmoid((flog + b_forget[l]).astype(jnp.float32))
        hs = (B, S, FOX_HEADS, HEAD_DIM)
        fox = forgetting_attention(fq.reshape(hs), fk.reshape(hs), fv.reshape(hs), logf)
        fox = fox.reshape(B, S, FOX_W).astype(x.dtype)

        hs = (B, S, DIL_HEADS, HEAD_DIM)
        dqr = rope_partial(dq.reshape(hs), pos)
        dkr = rope_partial(dk.reshape(hs), pos)
        dil = dilated_attention(dqr, dkr, dv.reshape(hs)).reshape(B, S, DIL_W).astype(x.dtype)

        mh = rmsnorm(mem, mem_norm_g[l])
        mk, mv = jnp.split(mh @ w_mem_kv[l], 2, axis=-1)
        ms = (B, mem.shape[1], MEM_HEADS, MEM_HEAD_DIM)
        memo = memory_attention(mq.reshape(B, S, MEM_HEADS, MEM_HEAD_DIM), mk.reshape(ms), mv.reshape(ms))
        memo = memo.reshape(B, S, MEM_W).astype(x.dtype)

        y = jnp.concatenate([fox * jax.nn.silu(fg), dil * jax.nn.silu(dg), memo * jax.nn.silu(mg)], axis=-1)
        x = x + y @ w_out[l]
    return rmsnorm(x, final_norm_g)


import jax as _jax
import jax.numpy as _jnp

TWIN_FORMAT = 'train_step'
FWD_PARAMS = ['x', 'mem', 'norm_g', 'w_in', 'b_forget', 'mem_norm_g', 'w_mem_kv', 'w_out', 'final_norm_g']
TWIN_WEIGHTS = ['norm_g', 'w_in', 'b_forget', 'mem_norm_g', 'w_mem_kv', 'w_out', 'final_norm_g']
TWIN_DIFF_INPUT = 'x'
TWIN_INPUTS = ['x', 'mem', 'norm_g', 'w_in', 'b_forget', 'mem_norm_g', 'w_mem_kv', 'w_out', 'final_norm_g', 'loss_target', 'm_norm_g', 'm_w_in', 'm_b_forget', 'm_mem_norm_g', 'm_w_mem_kv', 'm_w_out', 'm_final_norm_g', 'v_norm_g', 'v_w_in', 'v_b_forget', 'v_mem_norm_g', 'v_w_mem_kv', 'v_w_out', 'v_final_norm_g']
TWIN_OUTPUTS = ['loss', 'grad_x', 'grad_norm_g', 'grad_w_in', 'grad_b_forget', 'grad_mem_norm_g', 'grad_w_mem_kv', 'grad_w_out', 'grad_final_norm_g', 'delta_norm_g', 'delta_w_in', 'delta_b_forget', 'delta_mem_norm_g', 'delta_w_mem_kv', 'delta_w_out', 'delta_final_norm_g', 'new_m_norm_g', 'new_m_w_in', 'new_m_b_forget', 'new_m_mem_norm_g', 'new_m_w_mem_kv', 'new_m_w_out', 'new_m_final_norm_g', 'new_v_norm_g', 'new_v_w_in', 'new_v_b_forget', 'new_v_mem_norm_g', 'new_v_w_mem_kv', 'new_v_w_out', 'new_v_final_norm_g']
TWIN_LEAF_KINDS = {'loss': 'loss', 'grad_x': 'grad_x', 'grad_norm_g': 'grad_w', 'grad_w_in': 'grad_w', 'grad_b_forget': 'grad_w', 'grad_mem_norm_g': 'grad_w', 'grad_w_mem_kv': 'grad_w', 'grad_w_out': 'grad_w', 'grad_final_norm_g': 'grad_w', 'delta_norm_g': 'delta_w', 'delta_w_in': 'delta_w', 'delta_b_forget': 'delta_w', 'delta_mem_norm_g': 'delta_w', 'delta_w_mem_kv': 'delta_w', 'delta_w_out': 'delta_w', 'delta_final_norm_g': 'delta_w', 'new_m_norm_g': 'new_m', 'new_m_w_in': 'new_m', 'new_m_b_forget': 'new_m', 'new_m_mem_norm_g': 'new_m', 'new_m_w_mem_kv': 'new_m', 'new_m_w_out': 'new_m', 'new_m_final_norm_g': 'new_m', 'new_v_norm_g': 'new_v', 'new_v_w_in': 'new_v', 'new_v_b_forget': 'new_v', 'new_v_mem_norm_g': 'new_v', 'new_v_w_mem_kv': 'new_v', 'new_v_w_out': 'new_v', 'new_v_final_norm_g': 'new_v'}


def _forward(args):
    return _fwd_reference(*[args[k] for k in FWD_PARAMS])


def _output_shape():
    out = _jax.eval_shape(lambda: _forward(_fwd_setup_inputs(0)))
    return out.shape, out.dtype

N_MICROBATCH = 1
ADAM_LR = 0.001
ADAM_B1 = 0.9
ADAM_B2 = 0.999
ADAM_EPS = 1e-08
ADAM_WD = 0.01
ADAM_STEP = 10
PER_EXAMPLE_BATCH_AXIS = {'x': 0, 'mem': 0, 'loss_target': 0}
SHARED_INPUTS = []
_WEIGHT_DTYPES = {'norm_g': _jnp.float32, 'w_in': _jnp.float32, 'b_forget': _jnp.float32, 'mem_norm_g': _jnp.float32, 'w_mem_kv': _jnp.float32, 'w_out': _jnp.float32, 'final_norm_g': _jnp.float32}
MOMENT_SCALE = {'norm_g': 5.304995e-02, 'w_in': 2.074066e-02, 'b_forget': 2.075598e-01, 'mem_norm_g': 8.806458e-03, 'w_mem_kv': 7.884067e-03, 'w_out': 3.051722e-02, 'final_norm_g': 3.197048e+01}


def _to_microbatches(a, axis):
    t = _jnp.moveaxis(a, axis, 0)
    t = t.reshape((N_MICROBATCH, t.shape[0] // N_MICROBATCH) + t.shape[1:])
    return _jnp.moveaxis(t, 1, axis + 1)


def setup_inputs(seed: int = 0) -> dict:
    inp = _fwd_setup_inputs(seed)
    key = _jax.random.fold_in(_jax.random.key(seed), 7919)
    shape, _ = _output_shape()
    out = dict(inp)
    out["loss_target"] = _jax.random.normal(_jax.random.fold_in(key, 0), shape, _jnp.float32)
    for i, name in enumerate(TWIN_WEIGHTS):
        w = inp[name].astype(_jnp.float32)
        if MOMENT_SCALE is None:
            s = _jnp.sqrt(_jnp.mean(_jnp.square(w)) + 1e-30)
        else:
            s = MOMENT_SCALE[name]
        km, kv = _jax.random.split(_jax.random.fold_in(key, i + 1))
        out[name] = w
        out["m_" + name] = s * _jax.random.normal(km, w.shape, _jnp.float32)
        out["v_" + name] = (s * s) * _jax.random.uniform(kv, w.shape, _jnp.float32, 0.5, 1.5)
    if N_MICROBATCH > 1:
        for name, axis in PER_EXAMPLE_BATCH_AXIS.items():
            out[name] = _to_microbatches(out[name], axis)
    return {'x': out['x'], 'mem': out['mem'], 'norm_g': out['norm_g'], 'w_in': out['w_in'], 'b_forget': out['b_forget'], 'mem_norm_g': out['mem_norm_g'], 'w_mem_kv': out['w_mem_kv'], 'w_out': out['w_out'], 'final_norm_g': out['final_norm_g'], 'loss_target': out['loss_target'], 'm_norm_g': out['m_norm_g'], 'm_w_in': out['m_w_in'], 'm_b_forget': out['m_b_forget'], 'm_mem_norm_g': out['m_mem_norm_g'], 'm_w_mem_kv': out['m_w_mem_kv'], 'm_w_out': out['m_w_out'], 'm_final_norm_g': out['m_final_norm_g'], 'v_norm_g': out['v_norm_g'], 'v_w_in': out['v_w_in'], 'v_b_forget': out['v_b_forget'], 'v_mem_norm_g': out['v_mem_norm_g'], 'v_w_mem_kv': out['v_w_mem_kv'], 'v_w_out': out['v_w_out'], 'v_final_norm_g': out['v_final_norm_g']}


def _loss(weights, diff, rest, loss_target):
    with _jax.named_scope("forward"):
        args = {**rest, TWIN_DIFF_INPUT: diff, **{k: w.astype(_WEIGHT_DTYPES[k]) for k, w in weights.items()}}
        y = _forward(args)
    with _jax.named_scope("loss_head"):
        err = _jnp.square(y.astype(_jnp.float32) - loss_target)
        return 0.5 * _jnp.sum(_jnp.mean(err, axis=-1)) if err.ndim else 0.5 * err


def _adamw(w, g, m, v):
    m = ADAM_B1 * m + (1.0 - ADAM_B1) * g
    v = ADAM_B2 * v + (1.0 - ADAM_B2) * _jnp.square(g)
    m_hat = m / (1.0 - ADAM_B1 ** ADAM_STEP)
    v_hat = v / (1.0 - ADAM_B2 ** ADAM_STEP)
    delta = -ADAM_LR * (m_hat / (_jnp.sqrt(v_hat) + ADAM_EPS) + ADAM_WD * w)
    return delta, m, v


def reference(x, mem, norm_g, w_in, b_forget, mem_norm_g, w_mem_kv, w_out, final_norm_g, loss_target, m_norm_g, m_w_in, m_b_forget, m_mem_norm_g, m_w_mem_kv, m_w_out, m_final_norm_g, v_norm_g, v_w_in, v_b_forget, v_mem_norm_g, v_w_mem_kv, v_w_out, v_final_norm_g):
    given = dict(x=x, mem=mem, norm_g=norm_g, w_in=w_in, b_forget=b_forget, mem_norm_g=mem_norm_g, w_mem_kv=w_mem_kv, w_out=w_out, final_norm_g=final_norm_g, loss_target=loss_target, m_norm_g=m_norm_g, m_w_in=m_w_in, m_b_forget=m_b_forget, m_mem_norm_g=m_mem_norm_g, m_w_mem_kv=m_w_mem_kv, m_w_out=m_w_out, m_final_norm_g=m_final_norm_g, v_norm_g=v_norm_g, v_w_in=v_w_in, v_b_forget=v_b_forget, v_mem_norm_g=v_mem_norm_g, v_w_mem_kv=v_w_mem_kv, v_w_out=v_w_out, v_final_norm_g=v_final_norm_g)
    weights = {n: given[n] for n in TWIN_WEIGHTS}
    shared = {n: given[n] for n in SHARED_INPUTS}
    per_example = {n: given[n] for n in ['x', 'mem']}
    grad_fn = _jax.value_and_grad(_loss, argnums=(0, 1))

    def one_microbatch(ex, loss_target):
        ex = dict(ex)
        diff = ex.pop(TWIN_DIFF_INPUT)
        return grad_fn(weights, diff, {**shared, **ex}, loss_target)

    if N_MICROBATCH == 1:
        loss, (grad_w, grad_x) = one_microbatch(per_example, given["loss_target"])
    else:
        def body(carry, xs):
            loss_sum, grad_sum = carry
            l_k, (gw_k, gx_k) = one_microbatch(xs[0], xs[1])
            with _jax.named_scope("update"):
                return (loss_sum + l_k, _jax.tree.map(_jnp.add, grad_sum, gw_k)), gx_k

        init = (_jnp.zeros((), _jnp.float32), _jax.tree.map(_jnp.zeros_like, weights))
        (loss, grad_w), grad_x = _jax.lax.scan(body, init, (per_example, given["loss_target"]))
    with _jax.named_scope("update"):
        delta_w, new_m, new_v = {}, {}, {}
        for n in TWIN_WEIGHTS:
            delta_w[n], new_m[n], new_v[n] = _adamw(weights[n], grad_w[n], given["m_" + n], given["v_" + n])
    return (loss, grad_x, *[grad_w[n] for n in TWIN_WEIGHTS], *[delta_w[n] for n in TWIN_WEIGHTS],
            *[new_m[n] for n in TWIN_WEIGHTS], *[new_v[n] for n in TWIN_WEIGHTS])
```

```python
import functools
import math

import numpy as np
import jax
import jax.numpy as jnp
from jax import lax
from jax.experimental import pallas as pl
from jax.experimental.pallas import tpu as pltpu

F32 = jnp.float32
BF16 = jnp.bfloat16

D_MODEL = 1024
HEAD_DIM = 64
FOX_HEADS = 12
DIL_HEADS = 12
MEM_HEADS = 4
MEM_HEAD_DIM = 128
MEM_LEN = 256
FOX_W = FOX_HEADS * HEAD_DIM
DIL_W = DIL_HEADS * HEAD_DIM
MEM_W = MEM_HEADS * MEM_HEAD_DIM
MIX_W = FOX_W + DIL_W + MEM_W
DILATIONS = ((128, 1), (512, 4), (2048, 16))
ROPE_THETA = 500000.0
ROPE_DIM = HEAD_DIM // 4
RMS_EPS = 1e-6
NEG_INF = -1e30
IN_W = 4 * FOX_W + FOX_HEADS + 4 * DIL_W + 2 * MEM_W

ADAM_LR = 0.001
ADAM_B1 = 0.9
ADAM_B2 = 0.999
ADAM_EPS = 1e-08
ADAM_WD = 0.01
ADAM_STEP = 10

N_DEV = 8
LANES = 128
PAIR_W = 3 * LANES
TQ = 256
TK = 256

O_FQ, O_FK, O_FV, O_FG = 0, FOX_W, 2 * FOX_W, 3 * FOX_W
O_FLOG = 4 * FOX_W
O_DQ = O_FLOG + FOX_HEADS
O_DK, O_DV, O_DG = O_DQ + DIL_W, O_DQ + 2 * DIL_W, O_DQ + 3 * DIL_W
O_MQ = O_DQ + 4 * DIL_W
O_MG = O_MQ + MEM_W
P_FOX = 0
P_FG = P_FOX + 3 * FOX_W
P_DIL = P_FG + FOX_W
P_DG = P_DIL + 3 * DIL_W
P_MQ = P_DG + DIL_W
P_MG = P_MQ + MEM_W
P_FLOG = P_MG + MEM_W
PW = P_FLOG + LANES

VMEM_LIMIT = 56 * 1024 * 1024


def _pack_pieces():
    pieces = []
    for base in (O_FQ, O_DQ):
        seg = []
        for hp in range(FOX_HEADS // 2):
            for part in range(3):
                seg.append((base + part * FOX_W + hp * LANES, LANES))
        pieces.append(seg)
    fox, dil = pieces
    return fox + [(O_FG, FOX_W)] + dil + [(O_DG, DIL_W), (O_MQ, MEM_W), (O_MG, MEM_W), (O_FLOG, FOX_HEADS)]


def _pack_cols(w):
    parts = [w[..., s:s + n] for s, n in _pack_pieces()]
    parts.append(jnp.zeros(w.shape[:-1] + (LANES - FOX_HEADS,), w.dtype))
    return jnp.concatenate(parts, axis=-1)


def _unpack_cols(g):
    runs = []
    pos = 0
    for s, n in _pack_pieces():
        runs.append((s, n, pos))
        pos += n
    runs.sort()
    return jnp.concatenate([g[..., p:p + n] for s, n, p in runs], axis=-1)


def _params(sem=None, **kw):
    return pltpu.CompilerParams(dimension_semantics=sem, vmem_limit_bytes=VMEM_LIMIT, **kw)


def _mesh_pos():
    return lax.axis_index("x"), lax.axis_index("y"), lax.axis_index("c")


def _flip(v, d):
    return 1 - v if d else v


_RELATIONS = [(dx, dy, dc) for dx in (0, 1) for dy in (0, 1) for dc in (0, 1)][1:]


def weight_gather(shards):
    n_arr = len(shards)
    rows = [s.shape[0] for s in shards]

    def body(*refs):
        in_refs = refs[:n_arr]
        out_refs = refs[n_arr:2 * n_arr]
        send_sems, recv_sems, local_sems = refs[2 * n_arr:]
        x, y, c = _mesh_pos()
        me, sibling = (x, y, c), (x, y, 1 - c)
        chips = [(1 - x, y), (x, 1 - y), (1 - x, 1 - y)]

        def block(a, pos):
            px, py, pc = pos
            return out_refs[a].at[pl.ds((4 * px + 2 * py + pc) * rows[a], rows[a]), :]

        def copy(a, k, blk, to, src=None):
            return pltpu.make_async_remote_copy(
                src_ref=block(a, blk) if src is None else src, dst_ref=block(a, blk),
                send_sem=send_sems.at[a, k], recv_sem=recv_sems.at[a, k],
                device_id=to, device_id_type=pl.DeviceIdType.MESH)

        started = []
        mine = []
        for a in range(n_arr):
            cp = pltpu.make_async_copy(in_refs[a], block(a, me), local_sems.at[a])
            cp.start()
            mine.append(cp)
            first = [copy(a, 0, me, sibling, src=in_refs[a])]
            first += [copy(a, 1 + j, me, (*chip, c), src=in_refs[a]) for j, chip in enumerate(chips)]
            for cp in first:
                cp.start()
            started += first
        for a in range(n_arr):
            for j, chip in enumerate(chips):
                copy(a, 1 + j, (*chip, c), me).wait_recv()
                passed = copy(a, 4 + j, (*chip, c), sibling)
                passed.start()
                started.append(passed)
        for a in range(n_arr):
            copy(a, 0, sibling, me).wait_recv()
            for j, chip in enumerate(chips):
                copy(a, 4 + j, (*chip, 1 - c), me).wait_recv()
        for cp in started:
            cp.wait_send()
        for cp in mine:
            cp.wait()

    any_spec = pl.BlockSpec(memory_space=pl.ANY)
    return pl.pallas_call(
        body, name="weight_gather",
        out_shape=[jax.ShapeDtypeStruct((N_DEV * s.shape[0], s.shape[1]), s.dtype) for s in shards],
        in_specs=[any_spec] * n_arr, out_specs=[any_spec] * n_arr,
        scratch_shapes=[pltpu.SemaphoreType.DMA((n_arr, 7)), pltpu.SemaphoreType.DMA((n_arr, 7)),
                        pltpu.SemaphoreType.DMA((n_arr,))],
    )(*shards)


def grad_exchange(grads, small):
    arrs = list(grads) + [small]
    n_arr = len(arrs)
    rows = [g.shape[0] // N_DEV for g in grads] + [small.shape[0]]

    def body(*refs):
        in_refs = refs[:n_arr]
        out_refs = refs[n_arr:2 * n_arr]
        send_sems, recv_sems, local_sems = refs[2 * n_arr:]
        x, y, c = _mesh_pos()
        me = 4 * x + 2 * y + c

        def src(a, idx):
            if a == n_arr - 1:
                return in_refs[a]
            return in_refs[a].at[pl.ds(idx * rows[a], rows[a]), :]

        def copy(a, k):
            dx, dy, dc = _RELATIONS[k]
            px, py, pc = _flip(x, dx), _flip(y, dy), _flip(c, dc)
            peer = 4 * px + 2 * py + pc
            send = pltpu.make_async_remote_copy(
                src_ref=src(a, peer), dst_ref=out_refs[a].at[me],
                send_sem=send_sems.at[a, k], recv_sem=recv_sems.at[a, k],
                device_id=(px, py, pc), device_id_type=pl.DeviceIdType.MESH)
            recv = pltpu.make_async_remote_copy(
                src_ref=src(a, peer), dst_ref=out_refs[a].at[peer],
                send_sem=send_sems.at[a, k], recv_sem=recv_sems.at[a, k],
                device_id=(px, py, pc), device_id_type=pl.DeviceIdType.MESH)
            return send, recv

        mine = []
        pairs = []
        for a in range(n_arr):
            cp = pltpu.make_async_copy(src(a, me), out_refs[a].at[me], local_sems.at[a])
            cp.start()
            mine.append(cp)
            for k in range(7):
                send, recv = copy(a, k)
                send.start()
                pairs.append((send, recv))
        for send, recv in pairs:
            recv.wait_recv()
        for send, recv in pairs:
            send.wait_send()
        for cp in mine:
            cp.wait()

    any_spec = pl.BlockSpec(memory_space=pl.ANY)
    return pl.pallas_call(
        body, name="grad_exchange",
        out_shape=[jax.ShapeDtypeStruct((N_DEV, r, a.shape[1]), a.dtype) for r, a in zip(rows, arrs)],
        in_specs=[any_spec] * n_arr, out_specs=[any_spec] * n_arr,
        scratch_shapes=[pltpu.SemaphoreType.DMA((n_arr, 7)), pltpu.SemaphoreType.DMA((n_arr, 7)),
                        pltpu.SemaphoreType.DMA((n_arr,))],
    )(*arrs)


def slot_sum(slots, tr, name):
    _, R, C = slots.shape

    def body(s_ref, o_ref):
        acc = s_ref[0]
        for d in range(1, N_DEV):
            acc = acc + s_ref[d]
        o_ref[...] = acc

    return pl.pallas_call(
        body, name=name, grid=(R // tr,),
        in_specs=[pl.BlockSpec((N_DEV, tr, C), lambda i: (0, i, 0))],
        out_specs=pl.BlockSpec((tr, C), lambda i: (i, 0)),
        out_shape=jax.ShapeDtypeStruct((R, C), slots.dtype),
        compiler_params=_params(("arbitrary",)),
    )(slots)


def rms_fwd(x, g, tm, name):
    M, K = x.shape

    def body(x_ref, g_ref, o_ref):
        xv = x_ref[...]
        r = lax.rsqrt(jnp.mean(xv * xv, axis=-1, keepdims=True) + RMS_EPS)
        o_ref[...] = ((xv * r) * g_ref[...]).astype(BF16)

    return pl.pallas_call(
        body, name=name, grid=(M // tm,),
        in_specs=[pl.BlockSpec((tm, K), lambda i: (i, 0)), pl.BlockSpec((1, K), lambda i: (0, 0))],
        out_specs=pl.BlockSpec((tm, K), lambda i: (i, 0)),
        out_shape=jax.ShapeDtypeStruct((M, K), BF16),
        compiler_params=_params(("arbitrary",)),
    )(x, g)


def rms_bwd(x, g, dh, dres, tm, name):
    M, K = x.shape
    has_res = dres is not None

    def body(*refs):
        if has_res:
            x_ref, g_ref, dh_ref, dres_ref, dx_ref, dg_ref = refs
        else:
            x_ref, g_ref, dh_ref, dx_ref, dg_ref = refs
        xv = x_ref[...]
        r = lax.rsqrt(jnp.mean(xv * xv, axis=-1, keepdims=True) + RMS_EPS)
        xn = xv * r
        dhv = dh_ref[...]
        dxn = dhv * g_ref[...]
        dx = r * (dxn - xn * jnp.mean(dxn * xn, axis=-1, keepdims=True))
        if has_res:
            dx = dx + dres_ref[...]
        dx_ref[...] = dx
        part = jnp.sum(dhv * xn, axis=0, keepdims=True)
        row = lax.broadcasted_iota(jnp.int32, (8, K), 0)
        upd = jnp.where(row == 0, part, 0.0)

        @pl.when(pl.program_id(0) == 0)
        def _():
            dg_ref[...] = upd

        @pl.when(pl.program_id(0) != 0)
        def _():
            dg_ref[...] += upd

    row_spec = pl.BlockSpec((tm, K), lambda i: (i, 0))
    ins = [x, g, dh] + ([dres] if has_res else [])
    in_specs = [row_spec, pl.BlockSpec((1, K), lambda i: (0, 0)), row_spec] + ([row_spec] if has_res else [])
    return pl.pallas_call(
        body, name=name, grid=(M // tm,),
        in_specs=in_specs,
        out_specs=[row_spec, pl.BlockSpec((8, K), lambda i: (0, 0))],
        out_shape=[jax.ShapeDtypeStruct((M, K), F32), jax.ShapeDtypeStruct((8, K), F32)],
        compiler_params=_params(("arbitrary",)),
    )(*ins)


def mm_nn(a, b, tm, tn, name):
    M, K = a.shape
    N = b.shape[1]

    def body(a_ref, b_ref, o_ref):
        o_ref[...] = jnp.dot(a_ref[...], b_ref[...], preferred_element_type=F32)

    return pl.pallas_call(
        body, name=name, grid=(N // tn, M // tm),
        in_specs=[pl.BlockSpec((tm, K), lambda j, i: (i, 0)), pl.BlockSpec((K, tn), lambda j, i: (0, j))],
        out_specs=pl.BlockSpec((tm, tn), lambda j, i: (i, j)),
        out_shape=jax.ShapeDtypeStruct((M, N), F32),
        compiler_params=_params(("arbitrary", "arbitrary")),
    )(a, b)


def mm_nt(a, b, tm, tk, name):
    M, K = a.shape
    N = b.shape[0]

    def body(a_ref, b_ref, o_ref):
        part = lax.dot_general(a_ref[...], b_ref[...], (((1,), (1,)), ((), ())), preferred_element_type=F32)

        @pl.when(pl.program_id(1) == 0)
        def _():
            o_ref[...] = part

        @pl.when(pl.program_id(1) != 0)
        def _():
            o_ref[...] += part

    return pl.pallas_call(
        body, name=name, grid=(M // tm, K // tk),
        in_specs=[pl.BlockSpec((tm, tk), lambda i, k: (i, k)), pl.BlockSpec((N, tk), lambda i, k: (0, k))],
        out_specs=pl.BlockSpec((tm, N), lambda i, k: (i, 0)),
        out_shape=jax.ShapeDtypeStruct((M, N), F32),
        compiler_params=_params(("arbitrary", "arbitrary")),
    )(a, b)


def mm_tn(a, b, tt, tn, name):
    T, K = a.shape
    N = b.shape[1]

    def body(a_ref, b_ref, o_ref):
        part = lax.dot_general(a_ref[...], b_ref[...], (((0,), (0,)), ((), ())), preferred_element_type=F32)

        @pl.when(pl.program_id(1) == 0)
        def _():
            o_ref[...] = part

        @pl.when(pl.program_id(1) != 0)
        def _():
            o_ref[...] += part

    return pl.pallas_call(
        body, name=name, grid=(N // tn, T // tt),
        in_specs=[pl.BlockSpec((tt, K), lambda j, t: (t, 0)), pl.BlockSpec((tt, tn), lambda j, t: (t, j))],
        out_specs=pl.BlockSpec((K, tn), lambda j, t: (0, j)),
        out_shape=jax.ShapeDtypeStruct((K, N), F32),
        compiler_params=_params(("arbitrary", "arbitrary")),
    )(a, b)


def _log_sigmoid(z):
    return jnp.minimum(z, 0.0) - jnp.log(1.0 + jnp.exp(-jnp.abs(z)))


def _tri(n, lower):
    r = lax.broadcasted_iota(jnp.int32, (n, n), 0)
    c = lax.broadcasted_iota(jnp.int32, (n, n), 1)
    return jnp.where((r >= c) if lower else (r <= c), 1.0, 0.0).astype(F32)


def fox_gate(proj3, b_pad):
    B, S, _ = proj3.shape
    nblk = S // TK

    def body(f_ref, b_ref, o_ref):
        tri = _tri(TK, True)
        carry = jnp.zeros((1, LANES), F32)
        for n in range(nblk):
            z = f_ref[0, n * TK:(n + 1) * TK, :] + b_ref[...]
            logf = _log_sigmoid(z)
            cs = jnp.dot(tri, logf, preferred_element_type=F32, precision=lax.Precision.HIGHEST) + carry
            carry = cs[TK - 1:TK, :]
            o_ref[0, :, n * TK:(n + 1) * TK] = (-cs).T

    return pl.pallas_call(
        body, name="fox_gate", grid=(B,),
        in_specs=[pl.BlockSpec((1, S, LANES), lambda b: (b, 0, P_FLOG // LANES)),
                  pl.BlockSpec((1, LANES), lambda b: (0, 0))],
        out_specs=pl.BlockSpec((1, LANES, S), lambda b: (b, 0, 0)),
        out_shape=jax.ShapeDtypeStruct((B, LANES, S), F32),
        compiler_params=_params(("arbitrary",)),
    )(proj3, b_pad)


def fox_gate_bwd(dnegc, drow, proj3, b_pad):
    B, S, _ = proj3.shape
    nblk = S // TK

    def body(d_ref, r_ref, f_ref, b_ref, o_ref, db_ref):
        tri = _tri(TK, False)
        lane = lax.broadcasted_iota(jnp.int32, (TK, LANES), 1)
        er = lax.broadcasted_iota(jnp.int32, (FOX_W, LANES), 0)
        ec = lax.broadcasted_iota(jnp.int32, (FOX_W, LANES), 1)
        pick = jnp.where(er == HEAD_DIM * ec, 1.0, 0.0).astype(F32)
        carry = jnp.zeros((1, LANES), F32)
        dbsum = jnp.zeros((1, LANES), F32)
        for n in reversed(range(nblk)):
            dr = jnp.dot(r_ref[0, n * TK:(n + 1) * TK, :], pick, preferred_element_type=F32,
                         precision=lax.Precision.HIGHEST)
            dc = jnp.where(lane < FOX_HEADS, dr - d_ref[0, :, n * TK:(n + 1) * TK].T, 0.0)
            rs = jnp.dot(tri, dc, preferred_element_type=F32, precision=lax.Precision.HIGHEST) + carry
            carry = rs[0:1, :]
            z = f_ref[0, n * TK:(n + 1) * TK, :] + b_ref[...]
            dz = rs * (1.0 / (1.0 + jnp.exp(z)))
            o_ref[0, n * TK:(n + 1) * TK, :] = dz.astype(BF16)
            dbsum = dbsum + jnp.sum(dz, axis=0, keepdims=True)
        row = lax.broadcasted_iota(jnp.int32, (8, LANES), 0)
        upd = jnp.where(row == 0, dbsum, 0.0)

        @pl.when(pl.program_id(0) == 0)
        def _():
            db_ref[...] = upd

        @pl.when(pl.program_id(0) != 0)
        def _():
            db_ref[...] += upd

    return pl.pallas_call(
        body, name="fox_gate_bwd", grid=(B,),
        in_specs=[pl.BlockSpec((1, LANES, S), lambda b: (b, 0, 0)),
                  pl.BlockSpec((1, S, FOX_W), lambda b: (b, 0, 0)),
                  pl.BlockSpec((1, S, LANES), lambda b: (b, 0, P_FLOG // LANES)),
                  pl.BlockSpec((1, LANES), lambda b: (0, 0))],
        out_specs=[pl.BlockSpec((1, S, LANES), lambda b: (b, 0, 0)), pl.BlockSpec((8, LANES), lambda b: (0, 0))],
        out_shape=[jax.ShapeDtypeStruct((B, S, LANES), BF16), jax.ShapeDtypeStruct((8, LANES), F32)],
        compiler_params=_params(("arbitrary",)),
    )(dnegc, drow, proj3, b_pad)


def _mult_masks(S, kind):
    nd = S // TQ
    a = np.arange(TQ)[:, None]
    b = np.arange(TK)[None, :]
    out = np.zeros((nd, TQ, TK), np.float32)
    for d in range(nd):
        delta = d * TQ + a - b
        if kind == "causal":
            out[d] = delta >= 0
        else:
            m = np.zeros((TQ, TK), np.float32)
            for w, dil in DILATIONS:
                m += (delta >= 0) & (delta % dil == 0) & (delta <= w)
            out[d] = m
    return jnp.asarray(out)


def _rope_tables(S):
    half = ROPE_DIM // 2
    pos = jnp.arange(S, dtype=F32)
    inv_freq = 1.0 / (ROPE_THETA ** (jnp.arange(0, ROPE_DIM, 2, dtype=F32) / ROPE_DIM))
    ang = pos[:, None] * inv_freq[None, :]
    cos, sin = jnp.cos(ang), jnp.sin(ang)
    one = jnp.ones((S, HEAD_DIM - ROPE_DIM), F32)
    zero = jnp.zeros((S, HEAD_DIM - ROPE_DIM), F32)
    zh = jnp.zeros((S, half), F32)
    c = jnp.concatenate([cos, cos, one], axis=1)
    s1 = jnp.concatenate([-sin, zh, zero], axis=1)
    s2 = jnp.concatenate([zh, sin, zero], axis=1)
    return tuple(jnp.concatenate([t, t], axis=1) for t in (c, s1, s2))


def _rope(t, c, s1, s2):
    return t * c + pltpu.roll(t, LANES - half_rope(), 1) * s1 + pltpu.roll(t, half_rope(), 1) * s2


def half_rope():
    return ROPE_DIM // 2


def _rope_bwd(d, c, s1, s2):
    return d * c + pltpu.roll(d * s1, half_rope(), 1) + pltpu.roll(d * s2, LANES - half_rope(), 1)


def _scale_parts(scale):
    m, _ = math.frexp(scale)
    return (scale, None) if m == 0.5 else (None, scale)


def attn_fwd(kind, src, S, *, negc=None, mask=None, rope=None, kv=None):
    B = src.shape[0]
    pair = kind != "mem"
    col0 = {"fox": P_FOX, "dil": P_DIL, "mem": P_MQ}[kind]
    n_blocks = FOX_HEADS // 2 if pair else MEM_HEADS
    e_dim = HEAD_DIM if pair else MEM_HEAD_DIM
    q_fold, s_scale = _scale_parts(1.0 / math.sqrt(e_dim))
    Sk = S if pair else MEM_LEN
    nh = 2 if pair else 1
    has_bias = negc is not None
    has_rope = rope is not None
    nq = S // TQ

    def body(*refs):
        refs = list(refs)
        if pair:
            qkv_ref = refs.pop(0)
        else:
            q_ref, k_ref, v_ref = refs.pop(0), refs.pop(0), refs.pop(0)
        negc_ref = refs.pop(0) if has_bias else None
        mask_ref = refs.pop(0) if pair else None
        rope_refs = [refs.pop(0) for _ in range(3)] if has_rope else None
        o_ref, lse_ref, qs, ks, vs = refs
        lane = lax.broadcasted_iota(jnp.int32, (1, LANES), 1)

        def prep_q(n, _):
            r0 = pl.multiple_of(n * TQ, TQ)
            rows = pl.ds(r0, TQ)
            q = qkv_ref[0, rows, 0:LANES] if pair else q_ref[0, rows, :]
            if has_rope:
                q = _rope(q, *[t[rows, :] for t in rope_refs])
            if q_fold is not None:
                q = q * q_fold
            qs[rows, :] = q.astype(BF16)
            return 0

        def prep_kv(n, _):
            r0 = pl.multiple_of(n * TK, TK)
            rows = pl.ds(r0, TK)
            k = qkv_ref[0, rows, LANES:2 * LANES] if pair else k_ref[0, rows, :]
            v = qkv_ref[0, rows, 2 * LANES:3 * LANES] if pair else v_ref[0, rows, :]
            if has_rope:
                k = _rope(k, *[t[rows, :] for t in rope_refs])
            ks[rows, :] = k.astype(BF16)
            vs[rows, :] = v.astype(BF16)
            return 0

        lax.fori_loop(0, nq, prep_q, 0)
        lax.fori_loop(0, Sk // TK, prep_kv, 0)

        def q_loop(i, _):
            r0 = pl.multiple_of(i * TQ, TQ)
            q = qs[pl.ds(r0, TQ), :]
            res = []
            for hh in range(nh):
                hmask = (lane >= HEAD_DIM * hh) & (lane < HEAD_DIM * (hh + 1))
                qh = jnp.where(hmask, q, jnp.zeros_like(q)) if pair else q

                def kv_loop(j, carry, qh=qh, hh=hh):
                    m, l, acc = carry
                    c0 = pl.multiple_of(j * TK, TK)
                    k = ks[pl.ds(c0, TK), :]
                    v = vs[pl.ds(c0, TK), :]
                    s = lax.dot_general(qh, k, (((1,), (1,)), ((), ())), preferred_element_type=F32)
                    if s_scale is not None:
                        s = s * s_scale
                    if has_bias:
                        s = s + negc_ref[0, 0, pl.ds(hh, 1), pl.ds(c0, TK)]
                    if pair:
                        mult = mask_ref[i - j]
                        s = jnp.where(mult > 0.0, s, NEG_INF)
                    m_new = jnp.maximum(m, jnp.max(s, axis=1, keepdims=True))
                    p = jnp.exp(s - m_new)
                    if pair:
                        p = p * mult
                    alpha = jnp.exp(m - m_new)
                    l = alpha * l + jnp.sum(p, axis=1, keepdims=True)
                    acc = acc * alpha + jnp.dot(p.astype(BF16), v, preferred_element_type=F32)
                    return m_new, l, acc

                init = (jnp.full((TQ, 1), NEG_INF, F32), jnp.zeros((TQ, 1), F32), jnp.zeros((TQ, LANES), F32))
                m, l, acc = lax.fori_loop(0, (i + 1) if pair else Sk // TK, kv_loop, init)
                res.append((acc / l, m + jnp.log(l)))
            if pair:
                o = jnp.where(lane < HEAD_DIM, res[0][0], res[1][0])
                lse = jnp.where(lane < HEAD_DIM, res[0][1], res[1][1])
            else:
                o = res[0][0]
                lse = jnp.broadcast_to(res[0][1], (TQ, LANES))
            o_ref[0, pl.ds(r0, TQ), :] = o
            lse_ref[0, pl.ds(r0, TQ), :] = lse
            return 0

        lax.fori_loop(0, nq, q_loop, 0)

    ins, in_specs = [], []
    if pair:
        ins.append(src)
        in_specs.append(pl.BlockSpec((1, S, PAIR_W), lambda b, h: (b, 0, col0 // PAIR_W + h)))
    else:
        ins += [src, kv, kv]
        in_specs += [pl.BlockSpec((1, S, LANES), lambda b, h: (b, 0, col0 // LANES + h)),
                     pl.BlockSpec((1, MEM_LEN, LANES), lambda b, h: (b, 0, h)),
                     pl.BlockSpec((1, MEM_LEN, LANES), lambda b, h: (b, 0, MEM_HEADS + h))]
    if has_bias:
        ins.append(negc)
        in_specs.append(pl.BlockSpec((1, 1, 2, S), lambda b, h: (b, h, 0, 0)))
    if pair:
        ins.append(mask)
        in_specs.append(pl.BlockSpec(mask.shape, lambda b, h: (0, 0, 0)))
    if has_rope:
        ins += list(rope)
        in_specs += [pl.BlockSpec((S, LANES), lambda b, h: (0, 0))] * 3
    W = n_blocks * LANES
    out_spec = pl.BlockSpec((1, S, LANES), lambda b, h: (b, 0, h))
    return pl.pallas_call(
        body, name=kind + "_attn_fwd", grid=(B, n_blocks),
        in_specs=in_specs, out_specs=[out_spec, out_spec],
        out_shape=[jax.ShapeDtypeStruct((B, S, W), F32)] * 2,
        scratch_shapes=[pltpu.VMEM((S, LANES), BF16), pltpu.VMEM((Sk, LANES), BF16), pltpu.VMEM((Sk, LANES), BF16)],
        compiler_params=_params(("arbitrary", "arbitrary")),
    )(*ins)


def attn_bwd(kind, src, do, o, lse, S, *, negc=None, mask=None, rope=None, kv=None):
    B = src.shape[0]
    pair = kind != "mem"
    col0 = {"fox": P_FOX, "dil": P_DIL, "mem": P_MQ}[kind]
    n_blocks = FOX_HEADS // 2 if pair else MEM_HEADS
    e_dim = HEAD_DIM if pair else MEM_HEAD_DIM
    scale = 1.0 / math.sqrt(e_dim)
    q_fold, s_scale = _scale_parts(scale)
    Sk = S if pair else MEM_LEN
    nh = 2 if pair else 1
    has_bias = negc is not None
    has_rope = rope is not None
    nq = S // TQ
    nk = Sk // TK

    def body(*refs):
        refs = list(refs)
        if pair:
            qkv_ref = refs.pop(0)
        else:
            q_ref, k_ref, v_ref = refs.pop(0), refs.pop(0), refs.pop(0)
        do_ref, o_ref, lse_ref = refs.pop(0), refs.pop(0), refs.pop(0)
        negc_ref = refs.pop(0) if has_bias else None
        mask_ref = refs.pop(0) if pair else None
        rope_refs = [refs.pop(0) for _ in range(3)] if has_rope else None
        if pair:
            dqkv_ref = refs.pop(0)
            dnegc_ref = refs.pop(0) if has_bias else None
            drow_ref = refs.pop(0) if has_bias else None
        else:
            dq_ref, dk_ref, dv_ref = refs.pop(0), refs.pop(0), refs.pop(0)
        qs, ks, vs, dos, delta_s, dq_acc = refs[:6]
        drow_acc = refs[6] if has_bias else None
        lane = lax.broadcasted_iota(jnp.int32, (1, LANES), 1)

        def prep_q(n, _):
            r0 = pl.multiple_of(n * TQ, TQ)
            rows = pl.ds(r0, TQ)
            q = qkv_ref[0, rows, 0:LANES] if pair else q_ref[0, rows, :]
            if has_rope:
                q = _rope(q, *[t[rows, :] for t in rope_refs])
            if q_fold is not None:
                q = q * q_fold
            qs[rows, :] = q.astype(BF16)
            dov = do_ref[0, rows, :]
            dob = dov.astype(BF16)
            dos[rows, :] = dob
            prod = dob.astype(F32) * o_ref[0, rows, :]
            if pair:
                d0 = jnp.sum(jnp.where(lane < HEAD_DIM, prod, 0.0), axis=1, keepdims=True)
                d1 = jnp.sum(jnp.where(lane < HEAD_DIM, 0.0, prod), axis=1, keepdims=True)
                delta_s[rows, :] = jnp.where(lane < HEAD_DIM, d0, d1)
            else:
                delta_s[rows, :] = jnp.broadcast_to(jnp.sum(prod, axis=1, keepdims=True), (TQ, LANES))
            dq_acc[rows, :] = jnp.zeros((TQ, LANES), F32)
            if has_bias:
                drow_acc[rows, :] = jnp.zeros((TQ, LANES), F32)
            return 0

        def prep_kv(n, _):
            r0 = pl.multiple_of(n * TK, TK)
            rows = pl.ds(r0, TK)
            k = qkv_ref[0, rows, LANES:2 * LANES] if pair else k_ref[0, rows, :]
            v = qkv_ref[0, rows, 2 * LANES:3 * LANES] if pair else v_ref[0, rows, :]
            if has_rope:
                k = _rope(k, *[t[rows, :] for t in rope_refs])
            ks[rows, :] = k.astype(BF16)
            vs[rows, :] = v.astype(BF16)
            return 0

        lax.fori_loop(0, nq, prep_q, 0)
        lax.fori_loop(0, nk, prep_kv, 0)

        def kv_loop(j, _):
            c0 = pl.multiple_of(j * TK, TK)
            kt = ks[pl.ds(c0, TK), :]
            vt = vs[pl.ds(c0, TK), :]
            res = []
            for hh in range(nh):
                hmask = (lane >= HEAD_DIM * hh) & (lane < HEAD_DIM * (hh + 1))
                kh = jnp.where(hmask, kt, jnp.zeros_like(kt)) if pair else kt
                vh = jnp.where(hmask, vt, jnp.zeros_like(vt)) if pair else vt

                def q_loop(i, carry, kh=kh, vh=vh, hh=hh, hmask=hmask):
                    dk, dv, dneg = carry
                    r0 = pl.multiple_of(i * TQ, TQ)
                    rows = pl.ds(r0, TQ)
                    q = qs[rows, :]
                    dot = dos[rows, :]
                    lse_i = lse_ref[0, rows, hh * HEAD_DIM:hh * HEAD_DIM + 1]
                    delta_i = delta_s[rows, hh * HEAD_DIM:hh * HEAD_DIM + 1]
                    s = lax.dot_general(q, kh, (((1,), (1,)), ((), ())), preferred_element_type=F32)
                    if s_scale is not None:
                        s = s * s_scale
                    if has_bias:
                        s = s + negc_ref[0, 0, pl.ds(hh, 1), pl.ds(c0, TK)]
                    if pair:
                        mult = mask_ref[i - j]
                        s = jnp.where(mult > 0.0, s, NEG_INF)
                    p = jnp.exp(s - lse_i)
                    if pair:
                        p = p * mult
                    dv = dv + lax.dot_general(p.astype(BF16), dot, (((0,), (0,)), ((), ())),
                                              preferred_element_type=F32)
                    dp = lax.dot_general(dot, vh, (((1,), (1,)), ((), ())), preferred_element_type=F32)
                    ds = p * (dp - delta_i)
                    if has_bias:
                        dneg = dneg + jnp.sum(ds, axis=0, keepdims=True)
                        drow_acc[rows, :] += jnp.where(hmask, jnp.sum(ds, axis=1, keepdims=True), 0.0)
                    if s_scale is not None:
                        ds = ds * s_scale
                    dsb = ds.astype(BF16)
                    dk = dk + lax.dot_general(dsb, q, (((0,), (0,)), ((), ())), preferred_element_type=F32)
                    dq = jnp.dot(dsb, kh, preferred_element_type=F32)
                    dq_acc[rows, :] += dq
                    return dk, dv, dneg

                init = (jnp.zeros((TK, LANES), F32), jnp.zeros((TK, LANES), F32), jnp.zeros((1, TK), F32))
                dk, dv, dneg = lax.fori_loop(j if pair else 0, nq, q_loop, init)
                if has_bias:
                    dnegc_ref[0, 0, pl.ds(hh, 1), pl.ds(c0, TK)] = dneg
                res.append((dk, dv))
            if pair:
                dk = jnp.where(lane < HEAD_DIM, res[0][0], res[1][0])
                dv = jnp.where(lane < HEAD_DIM, res[0][1], res[1][1])
                if has_rope:
                    dk = _rope_bwd(dk, *[t[pl.ds(c0, TK), :] for t in rope_refs])
                dqkv_ref[0, pl.ds(c0, TK), LANES:2 * LANES] = dk.astype(BF16)
                dqkv_ref[0, pl.ds(c0, TK), 2 * LANES:3 * LANES] = dv.astype(BF16)
            else:
                dk_ref[0, pl.ds(c0, TK), :] = res[0][0].astype(BF16)
                dv_ref[0, pl.ds(c0, TK), :] = res[0][1].astype(BF16)
            return 0

        lax.fori_loop(0, nk, kv_loop, 0)

        def fin_q(n, _):
            r0 = pl.multiple_of(n * TQ, TQ)
            rows = pl.ds(r0, TQ)
            dq = dq_acc[rows, :]
            if q_fold is not None:
                dq = dq * q_fold
            if has_rope:
                dq = _rope_bwd(dq, *[t[rows, :] for t in rope_refs])
            if pair:
                dqkv_ref[0, rows, 0:LANES] = dq.astype(BF16)
            else:
                dq_ref[0, rows, :] = dq.astype(BF16)
            if has_bias:
                drow_ref[0, rows, :] = drow_acc[rows, :]
            return 0

        lax.fori_loop(0, nq, fin_q, 0)

    ins, in_specs = [], []
    if pair:
        ins.append(src)
        in_specs.append(pl.BlockSpec((1, S, PAIR_W), lambda b, h: (b, 0, col0 // PAIR_W + h)))
    else:
        ins += [src, kv, kv]
        in_specs += [pl.BlockSpec((1, S, LANES), lambda b, h: (b, 0, col0 // LANES + h)),
                     pl.BlockSpec((1, MEM_LEN, LANES), lambda b, h: (b, 0, h)),
                     pl.BlockSpec((1, MEM_LEN, LANES), lambda b, h: (b, 0, MEM_HEADS + h))]
    row_spec = pl.BlockSpec((1, S, LANES), lambda b, h: (b, 0, h))
    ins += [do, o, lse]
    in_specs += [row_spec] * 3
    if has_bias:
        ins.append(negc)
        in_specs.append(pl.BlockSpec((1, 1, 2, S), lambda b, h: (b, h, 0, 0)))
    if pair:
        ins.append(mask)
        in_specs.append(pl.BlockSpec(mask.shape, lambda b, h: (0, 0, 0)))
    if has_rope:
        ins += list(rope)
        in_specs += [pl.BlockSpec((S, LANES), lambda b, h: (0, 0))] * 3
    W = n_blocks * LANES
    if pair:
        out_specs = [pl.BlockSpec((1, S, PAIR_W), lambda b, h: (b, 0, h))]
        out_shape = [jax.ShapeDtypeStruct((B, S, 3 * W), BF16)]
        if has_bias:
            out_specs.append(pl.BlockSpec((1, 1, 2, S), lambda b, h: (b, h, 0, 0)))
            out_shape.append(jax.ShapeDtypeStruct((B, LANES // 2, 2, S), F32))
            out_specs.append(row_spec)
            out_shape.append(jax.ShapeDtypeStruct((B, S, W), F32))
    else:
        kv_spec = pl.BlockSpec((1, MEM_LEN, LANES), lambda b, h: (b, 0, h))
        out_specs = [row_spec, kv_spec, kv_spec]
        out_shape = [jax.ShapeDtypeStruct((B, S, W), BF16)] + [jax.ShapeDtypeStruct((B, MEM_LEN, W), BF16)] * 2
    return pl.pallas_call(
        body, name=kind + "_attn_bwd", grid=(B, n_blocks),
        in_specs=in_specs, out_specs=out_specs, out_shape=out_shape,
        scratch_shapes=[pltpu.VMEM((S, LANES), BF16), pltpu.VMEM((Sk, LANES), BF16), pltpu.VMEM((Sk, LANES), BF16),
                        pltpu.VMEM((S, LANES), BF16), pltpu.VMEM((S, LANES), F32), pltpu.VMEM((S, LANES), F32)]
        + ([pltpu.VMEM((S, LANES), F32)] if has_bias else []),
        compiler_params=_params(("arbitrary", "arbitrary")),
    )(*ins)


def _sigmoid(g):
    return 1.0 / (1.0 + jnp.exp(-g))


def out_fwd(proj, o_fox, o_dil, o_mem, w_out, x, target, gf, tm):
    T = x.shape[0]

    def body(fg_ref, dg_ref, mg_ref, of_ref, od_ref, om_ref, w_ref, x_ref, t_ref, gf_ref,
             y_ref, dx_ref, dxb_ref, sm_ref):
        parts = []
        for g_ref, o_ref in ((fg_ref, of_ref), (dg_ref, od_ref), (mg_ref, om_ref)):
            g = g_ref[...]
            parts.append((o_ref[...] * (g * _sigmoid(g))).astype(BF16))
        ymix = jnp.concatenate(parts, axis=1)
        y_ref[...] = ymix
        x2 = x_ref[...] + jnp.dot(ymix, w_ref[...], preferred_element_type=F32)
        r = lax.rsqrt(jnp.mean(x2 * x2, axis=-1, keepdims=True) + RMS_EPS)
        yn = x2 * r
        err = yn * gf_ref[...] - t_ref[...]
        loss = 0.5 * jnp.sum(jnp.sum(err * err, axis=-1, keepdims=True) / D_MODEL, axis=0, keepdims=True)
        dyf = err / D_MODEL
        dgf = jnp.sum(dyf * yn, axis=0, keepdims=True)
        dyn = dyf * gf_ref[...]
        dx2 = r * (dyn - yn * jnp.mean(dyn * yn, axis=-1, keepdims=True))
        dx_ref[...] = dx2
        dxb_ref[...] = dx2.astype(BF16)
        row = lax.broadcasted_iota(jnp.int32, (8, D_MODEL), 0)
        upd = jnp.where(row == 0, dgf, jnp.where(row == 1, loss, 0.0))

        @pl.when(pl.program_id(0) == 0)
        def _():
            sm_ref[...] = upd

        @pl.when(pl.program_id(0) != 0)
        def _():
            sm_ref[...] += upd

    def rows(w, col=0):
        return pl.BlockSpec((tm, w), lambda i: (i, col))

    return pl.pallas_call(
        body, name="out_fwd", grid=(T // tm,),
        in_specs=[rows(FOX_W, P_FG // FOX_W), rows(DIL_W, P_DG // DIL_W), rows(MEM_W, P_MG // MEM_W),
                  rows(FOX_W), rows(DIL_W), rows(MEM_W),
                  pl.BlockSpec((MIX_W, D_MODEL), lambda i: (0, 0)),
                  rows(D_MODEL), rows(D_MODEL), pl.BlockSpec((1, D_MODEL), lambda i: (0, 0))],
        out_specs=[rows(MIX_W), rows(D_MODEL), rows(D_MODEL), pl.BlockSpec((8, D_MODEL), lambda i: (0, 0))],
        out_shape=[jax.ShapeDtypeStruct((T, MIX_W), BF16), jax.ShapeDtypeStruct((T, D_MODEL), F32),
                   jax.ShapeDtypeStruct((T, D_MODEL), BF16), jax.ShapeDtypeStruct((8, D_MODEL), F32)],
        compiler_params=_params(("arbitrary",)),
    )(proj, proj, proj, o_fox, o_dil, o_mem, w_out, x, target, gf)


def out_bwd(proj, o_fox, o_dil, o_mem, w_out, dx2b, tm):
    T = dx2b.shape[0]

    def body(fg_ref, dg_ref, mg_ref, of_ref, od_ref, om_ref, w_ref, dx_ref,
             dof_ref, dod_ref, dom_ref, dfg_ref, ddg_ref, dmg_ref):
        dmix = lax.dot_general(dx_ref[...], w_ref[...], (((1,), (1,)), ((), ())), preferred_element_type=F32)
        col = 0
        for g_ref, o_ref, do_ref, dgate_ref in ((fg_ref, of_ref, dof_ref, dfg_ref), (dg_ref, od_ref, dod_ref, ddg_ref),
                                                 (mg_ref, om_ref, dom_ref, dmg_ref)):
            w = g_ref.shape[1]
            d = dmix[:, col:col + w]
            col += w
            g = g_ref[...]
            sg = _sigmoid(g)
            do_ref[...] = d * (g * sg)
            dgate_ref[...] = (d * o_ref[...] * (sg * (1.0 + g * (1.0 - sg)))).astype(BF16)

    def rows(w, col=0):
        return pl.BlockSpec((tm, w), lambda i: (i, col))

    return pl.pallas_call(
        body, name="out_bwd", grid=(T // tm,),
        in_specs=[rows(FOX_W, P_FG // FOX_W), rows(DIL_W, P_DG // DIL_W), rows(MEM_W, P_MG // MEM_W),
                  rows(FOX_W), rows(DIL_W), rows(MEM_W),
                  pl.BlockSpec((MIX_W, D_MODEL), lambda i: (0, 0)), rows(D_MODEL)],
        out_specs=[rows(FOX_W), rows(DIL_W), rows(MEM_W), rows(FOX_W), rows(DIL_W), rows(MEM_W)],
        out_shape=[jax.ShapeDtypeStruct((T, FOX_W), F32), jax.ShapeDtypeStruct((T, DIL_W), F32),
                   jax.ShapeDtypeStruct((T, MEM_W), F32), jax.ShapeDtypeStruct((T, FOX_W), BF16),
                   jax.ShapeDtypeStruct((T, DIL_W), BF16), jax.ShapeDtypeStruct((T, MEM_W), BF16)],
        compiler_params=_params(("arbitrary",)),
    )(proj, proj, proj, o_fox, o_dil, o_mem, w_out, dx2b)


def adamw(w, g, m, v, tr, name):
    R, C = w.shape

    def body(w_ref, g_ref, m_ref, v_ref, d_ref, mo_ref, vo_ref):
        gv = g_ref[...]
        mn = ADAM_B1 * m_ref[...] + (1.0 - ADAM_B1) * gv
        vn = ADAM_B2 * v_ref[...] + (1.0 - ADAM_B2) * jnp.square(gv)
        m_hat = mn / (1.0 - ADAM_B1 ** ADAM_STEP)
        v_hat = vn / (1.0 - ADAM_B2 ** ADAM_STEP)
        d_ref[...] = -ADAM_LR * (m_hat / (jnp.sqrt(v_hat) + ADAM_EPS) + ADAM_WD * w_ref[...])
        mo_ref[...] = mn
        vo_ref[...] = vn

    spec = pl.BlockSpec((tr, C), lambda i: (i, 0))
    return pl.pallas_call(
        body, name=name, grid=(R // tr,),
        in_specs=[spec] * 4, out_specs=[spec] * 3,
        out_shape=[jax.ShapeDtypeStruct((R, C), F32)] * 3,
        compiler_params=_params(("arbitrary",)),
    )(w, g, m, v)


def _pad_row(v, width):
    return jnp.concatenate([v, jnp.zeros((1, width - v.shape[1]), v.dtype)], axis=1)


def local_grads(x, mem, norm_g, b_forget, mem_norm_g, final_norm_g, loss_target, w_in_p, w_kv, w_out):
    B, S, D = x.shape
    T = B * S
    xt = x.reshape(T, D)
    memt = mem.reshape(B * MEM_LEN, D)
    b_pad = _pad_row(b_forget, LANES)

    h = rms_fwd(xt, norm_g, 512, "rms_x")
    proj = mm_nn(h, w_in_p, 512, PW // 3, "in_proj")
    proj3 = proj.reshape(B, S, PW)
    mh = rms_fwd(memt, mem_norm_g, B * MEM_LEN, "rms_mem")
    mkv = mm_nn(mh, w_kv, B * MEM_LEN, 2 * MEM_W, "mem_kv_proj")
    mkv3 = mkv.reshape(B, MEM_LEN, 2 * MEM_W)

    negc = fox_gate(proj3, b_pad).reshape(B, LANES // 2, 2, S)
    causal = _mult_masks(S, "causal")
    dilated = _mult_masks(S, "dilated")
    rope = _rope_tables(S)

    o_fox, lse_fox = attn_fwd("fox", proj3, S, negc=negc, mask=causal)
    o_dil, lse_dil = attn_fwd("dil", proj3, S, mask=dilated, rope=rope)
    o_mem, lse_mem = attn_fwd("mem", proj3, S, kv=mkv3)

    ymix, dx2, dx2b, small_out = out_fwd(
        proj, o_fox.reshape(T, FOX_W), o_dil.reshape(T, DIL_W), o_mem.reshape(T, MEM_W), w_out,
        xt, loss_target.reshape(T, D), final_norm_g.reshape(1, D), 256)
    do_fox, do_dil, do_mem, dfg, ddg, dmg = out_bwd(
        proj, o_fox.reshape(T, FOX_W), o_dil.reshape(T, DIL_W), o_mem.reshape(T, MEM_W), w_out, dx2b, 256)
    g_out = mm_tn(ymix, dx2b, 512, D, "w_out_grad")

    dqkv_fox, dnegc, drow = attn_bwd("fox", proj3, do_fox.reshape(B, S, FOX_W), o_fox, lse_fox, S, negc=negc, mask=causal)
    (dqkv_dil,) = attn_bwd("dil", proj3, do_dil.reshape(B, S, DIL_W), o_dil, lse_dil, S, mask=dilated, rope=rope)
    dmq, dmk, dmv = attn_bwd("mem", proj3, do_mem.reshape(B, S, MEM_W), o_mem, lse_mem, S, kv=mkv3)
    dflog, db_part = fox_gate_bwd(dnegc.reshape(B, LANES, S), drow, proj3, b_pad)

    dproj = jnp.concatenate([dqkv_fox.reshape(T, 3 * FOX_W), dfg, dqkv_dil.reshape(T, 3 * DIL_W), ddg,
                             dmq.reshape(T, MEM_W), dmg, dflog.reshape(T, LANES)], axis=1)
    g_in = mm_tn(h, dproj, 512, PW // 3, "w_in_grad")
    dh = mm_nt(dproj, w_in_p, 1024, PW // 3, "in_proj_bwd")
    grad_x, dng = rms_bwd(xt, norm_g, dh, dx2, 512, "rms_x_bwd")

    dmkv = jnp.concatenate([dmk, dmv], axis=2).reshape(B * MEM_LEN, 2 * MEM_W)
    g_kv = mm_tn(mh, dmkv, B * MEM_LEN, 2 * MEM_W, "w_kv_grad")
    dmh = mm_nt(dmkv, w_kv, B * MEM_LEN, D, "mem_kv_bwd")
    _, dmng = rms_bwd(memt, mem_norm_g, dmh, None, B * MEM_LEN, "rms_mem_bwd")

    small = jnp.concatenate([dng[0:1], dmng[0:1], small_out[0:1], _pad_row(db_part[0:1], D), small_out[1:2],
                             jnp.zeros((3, D), F32)], axis=0)
    return grad_x.reshape(B, S, D), g_in, g_kv, g_out, small


def kernel(x, mem, norm_g, w_in, b_forget, mem_norm_g, w_mem_kv, w_out, final_norm_g, loss_target, m_norm_g, m_w_in, m_b_forget, m_mem_norm_g, m_w_mem_kv, m_w_out, m_final_norm_g, v_norm_g, v_w_in, v_b_forget, v_mem_norm_g, v_w_mem_kv, v_w_out, v_final_norm_g):
    D = D_MODEL
    w_in_full, w_kv_full, w_out_full = weight_gather(
        [_pack_cols(w_in[0]).astype(BF16), w_mem_kv[0].astype(BF16), w_out[0].astype(BF16)])
    grad_x, g_in, g_kv, g_out, small = local_grads(
        x, mem, norm_g, b_forget, mem_norm_g, final_norm_g, loss_target, w_in_full, w_kv_full, w_out_full)

    s_in, s_kv, s_out, s_small = grad_exchange([g_in, g_kv, g_out], small)
    gw_in = _unpack_cols(slot_sum(s_in, 16, "sum_w_in"))
    gw_kv = slot_sum(s_kv, 128, "sum_w_kv")
    gw_out = slot_sum(s_out, 256, "sum_w_out")
    tot = slot_sum(s_small, 8, "sum_small")

    loss = tot[4, 0]
    g_norm, g_mem_norm, g_final, g_b = tot[0:1], tot[1:2], tot[2], tot[3:4, :FOX_HEADS]

    def rows8(*rows):
        rows = [r.reshape(1, -1) for r in rows]
        rows = [_pad_row(r, D) for r in rows]
        return jnp.concatenate(rows + [jnp.zeros((8 - len(rows), D), F32)], axis=0)

    sw = rows8(norm_g, mem_norm_g, final_norm_g, b_forget)
    sm = rows8(m_norm_g, m_mem_norm_g, m_final_norm_g, m_b_forget)
    sv = rows8(v_norm_g, v_mem_norm_g, v_final_norm_g, v_b_forget)
    d_s, m_s, v_s = adamw(sw, tot, sm, sv, 8, "adamw_small")
    d_in, m_in, v_in = adamw(w_in[0], gw_in, m_w_in[0], v_w_in[0], 32, "adamw_w_in")
    d_kv, m_kv, v_kv = adamw(w_mem_kv[0], gw_kv, m_w_mem_kv[0], v_w_mem_kv[0], 128, "adamw_w_kv")
    d_out, m_out, v_out = adamw(w_out[0], gw_out, m_w_out[0], v_w_out[0], 256, "adamw_w_out")

    def small_outs(t):
        return t[0:1], t[3:4, :FOX_HEADS], t[1:2], t[2]

    grads = (g_norm, gw_in[None], g_b, g_mem_norm, gw_kv[None], gw_out[None], g_final)
    outs = []
    for t, big in ((d_s, (d_in, d_kv, d_out)), (m_s, (m_in, m_kv, m_out)), (v_s, (v_in, v_kv, v_out))):
        n, b, mn, f = small_outs(t)
        outs += [n, big[0][None], b, mn, big[1][None], big[2][None], f]
    return (loss, grad_x, *grads, *outs)
```

```python
import functools
import math

import numpy as np
import jax
import jax.numpy as jnp
from jax import lax
from jax.experimental import pallas as pl
from jax.experimental.pallas import tpu as pltpu

F32 = jnp.float32
BF16 = jnp.bfloat16

D_MODEL = 1024
HEAD_DIM = 64
FOX_HEADS = 12
DIL_HEADS = 12
MEM_HEADS = 4
MEM_HEAD_DIM = 128
MEM_LEN = 256
FOX_W = FOX_HEADS * HEAD_DIM
DIL_W = DIL_HEADS * HEAD_DIM
MEM_W = MEM_HEADS * MEM_HEAD_DIM
MIX_W = FOX_W + DIL_W + MEM_W
DILATIONS = ((128, 1), (512, 4), (2048, 16))
ROPE_THETA = 500000.0
ROPE_DIM = HEAD_DIM // 4
RMS_EPS = 1e-6
NEG_INF = -1e30
IN_W = 4 * FOX_W + FOX_HEADS + 4 * DIL_W + 2 * MEM_W

ADAM_LR = 0.001
ADAM_B1 = 0.9
ADAM_B2 = 0.999
ADAM_EPS = 1e-08
ADAM_WD = 0.01
ADAM_STEP = 10

N_DEV = 8
LANES = 128
PAIR_W = 3 * LANES
TQ = 256
TK = 256

O_FQ, O_FK, O_FV, O_FG = 0, FOX_W, 2 * FOX_W, 3 * FOX_W
O_FLOG = 4 * FOX_W
O_DQ = O_FLOG + FOX_HEADS
O_DK, O_DV, O_DG = O_DQ + DIL_W, O_DQ + 2 * DIL_W, O_DQ + 3 * DIL_W
O_MQ = O_DQ + 4 * DIL_W
O_MG = O_MQ + MEM_W
P_FOX = 0
P_FG = P_FOX + 3 * FOX_W
P_DIL = P_FG + FOX_W
P_DG = P_DIL + 3 * DIL_W
P_MQ = P_DG + DIL_W
P_MG = P_MQ + MEM_W
P_FLOG = P_MG + MEM_W
PW = P_FLOG + LANES

VMEM_LIMIT = 56 * 1024 * 1024


def _pack_pieces():
    pieces = []
    for base in (O_FQ, O_DQ):
        seg = []
        for hp in range(FOX_HEADS // 2):
            for part in range(3):
                seg.append((base + part * FOX_W + hp * LANES, LANES))
        pieces.append(seg)
    fox, dil = pieces
    return fox + [(O_FG, FOX_W)] + dil + [(O_DG, DIL_W), (O_MQ, MEM_W), (O_MG, MEM_W), (O_FLOG, FOX_HEADS)]


def _pack_cols(w):
    parts = [w[..., s:s + n] for s, n in _pack_pieces()]
    parts.append(jnp.zeros(w.shape[:-1] + (LANES - FOX_HEADS,), w.dtype))
    return jnp.concatenate(parts, axis=-1)


def _unpack_cols(g):
    runs = []
    pos = 0
    for s, n in _pack_pieces():
        runs.append((s, n, pos))
        pos += n
    runs.sort()
    return jnp.concatenate([g[..., p:p + n] for s, n, p in runs], axis=-1)


def _params(sem=None, **kw):
    return pltpu.CompilerParams(dimension_semantics=sem, vmem_limit_bytes=VMEM_LIMIT, **kw)


def _mesh_pos():
    return lax.axis_index("x"), lax.axis_index("y"), lax.axis_index("c")


def _flip(v, d):
    return 1 - v if d else v


_RELATIONS = [(dx, dy, dc) for dx in (0, 1) for dy in (0, 1) for dc in (0, 1)][1:]


def weight_gather(shards):
    n_arr = len(shards)
    rows = [s.shape[0] for s in shards]

    def body(*refs):
        in_refs = refs[:n_arr]
        out_refs = refs[n_arr:2 * n_arr]
        send_sems, recv_sems, local_sems = refs[2 * n_arr:]
        x, y, c = _mesh_pos()
        me, sibling = (x, y, c), (x, y, 1 - c)
        chips = [(1 - x, y), (x, 1 - y), (1 - x, 1 - y)]

        def block(a, pos):
            px, py, pc = pos
            return out_refs[a].at[pl.ds((4 * px + 2 * py + pc) * rows[a], rows[a]), :]

        def copy(a, k, blk, to, src=None):
            return pltpu.make_async_remote_copy(
                src_ref=block(a, blk) if src is None else src, dst_ref=block(a, blk),
                send_sem=send_sems.at[a, k], recv_sem=recv_sems.at[a, k],
                device_id=to, device_id_type=pl.DeviceIdType.MESH)

        started = []
        mine = []
        for a in range(n_arr):
            cp = pltpu.make_async_copy(in_refs[a], block(a, me), local_sems.at[a])
            cp.start()
            mine.append(cp)
            first = [copy(a, 0, me, sibling, src=in_refs[a])]
            first += [copy(a, 1 + j, me, (*chip, c), src=in_refs[a]) for j, chip in enumerate(chips)]
            for cp in first:
                cp.start()
            started += first
        for a in range(n_arr):
            for j, chip in enumerate(chips):
                copy(a, 1 + j, (*chip, c), me).wait_recv()
                passed = copy(a, 4 + j, (*chip, c), sibling)
                passed.start()
                started.append(passed)
        for a in range(n_arr):
            copy(a, 0, sibling, me).wait_recv()
            for j, chip in enumerate(chips):
                copy(a, 4 + j, (*chip, 1 - c), me).wait_recv()
        for cp in started:
            cp.wait_send()
        for cp in mine:
            cp.wait()

    any_spec = pl.BlockSpec(memory_space=pl.ANY)
    return pl.pallas_call(
        body, name="weight_gather",
        out_shape=[jax.ShapeDtypeStruct((N_DEV * s.shape[0], s.shape[1]), s.dtype) for s in shards],
        in_specs=[any_spec] * n_arr, out_specs=[any_spec] * n_arr,
        scratch_shapes=[pltpu.SemaphoreType.DMA((n_arr, 7)), pltpu.SemaphoreType.DMA((n_arr, 7)),
                        pltpu.SemaphoreType.DMA((n_arr,))],
    )(*shards)


def grad_exchange(grads, small):
    arrs = list(grads) + [small]
    n_arr = len(arrs)
    rows = [g.shape[0] // N_DEV for g in grads] + [small.shape[0]]

    def body(*refs):
        in_refs = refs[:n_arr]
        out_refs = refs[n_arr:2 * n_arr]
        send_sems, recv_sems, local_sems = refs[2 * n_arr:]
        x, y, c = _mesh_pos()
        me = 4 * x + 2 * y + c

        def src(a, idx):
            if a == n_arr - 1:
                return in_refs[a]
            return in_refs[a].at[pl.ds(idx * rows[a], rows[a]), :]

        def copy(a, k):
            dx, dy, dc = _RELATIONS[k]
            px, py, pc = _flip(x, dx), _flip(y, dy), _flip(c, dc)
            peer = 4 * px + 2 * py + pc
            send = pltpu.make_async_remote_copy(
                src_ref=src(a, peer), dst_ref=out_refs[a].at[me],
                send_sem=send_sems.at[a, k], recv_sem=recv_sems.at[a, k],
                device_id=(px, py, pc), device_id_type=pl.DeviceIdType.MESH)
            recv = pltpu.make_async_remote_copy(
                src_ref=src(a, peer), dst_ref=out_refs[a].at[peer],
                send_sem=send_sems.at[a, k], recv_sem=recv_sems.at[a, k],
                device_id=(px, py, pc), device_id_type=pl.DeviceIdType.MESH)
            return send, recv

        mine = []
        pairs = []
        for a in range(n_arr):
            cp = pltpu.make_async_copy(src(a, me), out_refs[a].at[me], local_sems.at[a])
            cp.start()
            mine.append(cp)
            for k in range(7):
                send, recv = copy(a, k)
                send.start()
                pairs.append((send, recv))
        for send, recv in pairs:
            recv.wait_recv()
        for send, recv in pairs:
            send.wait_send()
        for cp in mine:
            cp.wait()

    any_spec = pl.BlockSpec(memory_space=pl.ANY)
    return pl.pallas_call(
        body, name="grad_exchange",
        out_shape=[jax.ShapeDtypeStruct((N_DEV, r, a.shape[1]), a.dtype) for r, a in zip(rows, arrs)],
        in_specs=[any_spec] * n_arr, out_specs=[any_spec] * n_arr,
        scratch_shapes=[pltpu.SemaphoreType.DMA((n_arr, 7)), pltpu.SemaphoreType.DMA((n_arr, 7)),
                        pltpu.SemaphoreType.DMA((n_arr,))],
    )(*arrs)


def slot_sum(slots, tr, name):
    _, R, C = slots.shape

    def body(s_ref, o_ref):
        acc = s_ref[0]
        for d in range(1, N_DEV):
            acc = acc + s_ref[d]
        o_ref[...] = acc

    return pl.pallas_call(
        body, name=name, grid=(R // tr,),
        in_specs=[pl.BlockSpec((N_DEV, tr, C), lambda i: (0, i, 0))],
        out_specs=pl.BlockSpec((tr, C), lambda i: (i, 0)),
        out_shape=jax.ShapeDtypeStruct((R, C), slots.dtype),
        compiler_params=_params(("arbitrary",)),
    )(slots)


def rms_fwd(x, g, tm, name):
    M, K = x.shape

    def body(x_ref, g_ref, o_ref):
        xv = x_ref[...]
        r = lax.rsqrt(jnp.mean(xv * xv, axis=-1, keepdims=True) + RMS_EPS)
        o_ref[...] = ((xv * r) * g_ref[...]).astype(BF16)

    return pl.pallas_call(
        body, name=name, grid=(M // tm,),
        in_specs=[pl.BlockSpec((tm, K), lambda i: (i, 0)), pl.BlockSpec((1, K), lambda i: (0, 0))],
        out_specs=pl.BlockSpec((tm, K), lambda i: (i, 0)),
        out_shape=jax.ShapeDtypeStruct((M, K), BF16),
        compiler_params=_params(("arbitrary",)),
    )(x, g)


def rms_bwd(x, g, dh, dres, tm, name):
    M, K = x.shape
    has_res = dres is not None

    def body(*refs):
        if has_res:
            x_ref, g_ref, dh_ref, dres_ref, dx_ref, dg_ref = refs
        else:
            x_ref, g_ref, dh_ref, dx_ref, dg_ref = refs
        xv = x_ref[...]
        r = lax.rsqrt(jnp.mean(xv * xv, axis=-1, keepdims=True) + RMS_EPS)
        xn = xv * r
        dhv = dh_ref[...]
        dxn = dhv * g_ref[...]
        dx = r * (dxn - xn * jnp.mean(dxn * xn, axis=-1, keepdims=True))
        if has_res:
            dx = dx + dres_ref[...]
        dx_ref[...] = dx
        part = jnp.sum(dhv * xn, axis=0, keepdims=True)
        row = lax.broadcasted_iota(jnp.int32, (8, K), 0)
        upd = jnp.where(row == 0, part, 0.0)

        @pl.when(pl.program_id(0) == 0)
        def _():
            dg_ref[...] = upd

        @pl.when(pl.program_id(0) != 0)
        def _():
            dg_ref[...] += upd

    row_spec = pl.BlockSpec((tm, K), lambda i: (i, 0))
    ins = [x, g, dh] + ([dres] if has_res else [])
    in_specs = [row_spec, pl.BlockSpec((1, K), lambda i: (0, 0)), row_spec] + ([row_spec] if has_res else [])
    return pl.pallas_call(
        body, name=name, grid=(M // tm,),
        in_specs=in_specs,
        out_specs=[row_spec, pl.BlockSpec((8, K), lambda i: (0, 0))],
        out_shape=[jax.ShapeDtypeStruct((M, K), F32), jax.ShapeDtypeStruct((8, K), F32)],
        compiler_params=_params(("arbitrary",)),
    )(*ins)


def mm_nn(a, b, tm, tn, name):
    M, K = a.shape
    N = b.shape[1]

    def body(a_ref, b_ref, o_ref):
        o_ref[...] = jnp.dot(a_ref[...], b_ref[...], preferred_element_type=F32)

    return pl.pallas_call(
        body, name=name, grid=(N // tn, M // tm),
        in_specs=[pl.BlockSpec((tm, K), lambda j, i: (i, 0)), pl.BlockSpec((K, tn), lambda j, i: (0, j))],
        out_specs=pl.BlockSpec((tm, tn), lambda j, i: (i, j)),
        out_shape=jax.ShapeDtypeStruct((M, N), F32),
        compiler_params=_params(("arbitrary", "arbitrary")),
    )(a, b)


def mm_nt(a, b, tm, tk, name):
    M, K = a.shape
    N = b.shape[0]

    def body(a_ref, b_ref, o_ref):
        part = lax.dot_general(a_ref[...], b_ref[...], (((1,), (1,)), ((), ())), preferred_element_type=F32)

        @pl.when(pl.program_id(1) == 0)
        def _():
            o_ref[...] = part

        @pl.when(pl.program_id(1) != 0)
        def _():
            o_ref[...] += part

    return pl.pallas_call(
        body, name=name, grid=(M // tm, K // tk),
        in_specs=[pl.BlockSpec((tm, tk), lambda i, k: (i, k)), pl.BlockSpec((N, tk), lambda i, k: (0, k))],
        out_specs=pl.BlockSpec((tm, N), lambda i, k: (i, 0)),
        out_shape=jax.ShapeDtypeStruct((M, N), F32),
        compiler_params=_params(("arbitrary", "arbitrary")),
    )(a, b)


def mm_tn(a, b, tt, tn, name):
    T, K = a.shape
    N = b.shape[1]

    def body(a_ref, b_ref, o_ref):
        part = lax.dot_general(a_ref[...], b_ref[...], (((0,), (0,)), ((), ())), preferred_element_type=F32)

        @pl.when(pl.program_id(1) == 0)
        def _():
            o_ref[...] = part

        @pl.when(pl.program_id(1) != 0)
        def _():
            o_ref[...] += part

    return pl.pallas_call(
        body, name=name, grid=(N // tn, T // tt),
        in_specs=[pl.BlockSpec((tt, K), lambda j, t: (t, 0)), pl.BlockSpec((tt, tn), lambda j, t: (t, j))],
        out_specs=pl.BlockSpec((K, tn), lambda j, t: (0, j)),
        out_shape=jax.ShapeDtypeStruct((K, N), F32),
        compiler_params=_params(("arbitrary", "arbitrary")),
    )(a, b)


def _log_sigmoid(z):
    return jnp.minimum(z, 0.0) - jnp.log(1.0 + jnp.exp(-jnp.abs(z)))


def _tri(n, lower):
    r = lax.broadcasted_iota(jnp.int32, (n, n), 0)
    c = lax.broadcasted_iota(jnp.int32, (n, n), 1)
    return jnp.where((r >= c) if lower else (r <= c), 1.0, 0.0).astype(F32)


def fox_gate(proj3, b_pad):
    B, S, _ = proj3.shape
    nblk = S // TK

    def body(f_ref, b_ref, o_ref):
        tri = _tri(TK, True)
        carry = jnp.zeros((1, LANES), F32)
        for n in range(nblk):
            z = f_ref[0, n * TK:(n + 1) * TK, :] + b_ref[...]
            logf = _log_sigmoid(z)
            cs = jnp.dot(tri, logf, preferred_element_type=F32, precision=lax.Precision.HIGHEST) + carry
            carry = cs[TK - 1:TK, :]
            o_ref[0, n * TK:(n + 1) * TK, :] = -cs

    return pl.pallas_call(
        body, name="fox_gate", grid=(B,),
        in_specs=[pl.BlockSpec((1, S, LANES), lambda b: (b, 0, P_FLOG // LANES)),
                  pl.BlockSpec((1, LANES), lambda b: (0, 0))],
        out_specs=pl.BlockSpec((1, S, LANES), lambda b: (b, 0, 0)),
        out_shape=jax.ShapeDtypeStruct((B, S, LANES), F32),
        compiler_params=_params(("arbitrary",)),
    )(proj3, b_pad)


def fox_gate_bwd(drow, dneg, proj3, b_pad):
    B, S, _ = proj3.shape
    nblk = S // TK

    def body(d_ref, r_ref, f_ref, b_ref, o_ref, db_ref):
        tri = _tri(TK, False)
        lane = lax.broadcasted_iota(jnp.int32, (TK, LANES), 1)
        er = lax.broadcasted_iota(jnp.int32, (FOX_W, LANES), 0)
        ec = lax.broadcasted_iota(jnp.int32, (FOX_W, LANES), 1)
        pick = jnp.where(er == LANES * (ec >> 1) + (ec & 1), 1.0, 0.0).astype(F32)
        carry = jnp.zeros((1, LANES), F32)
        dbsum = jnp.zeros((1, LANES), F32)
        for n in reversed(range(nblk)):
            dk_side = jnp.dot(r_ref[0, n * TK:(n + 1) * TK, :], pick, preferred_element_type=F32,
                              precision=lax.Precision.HIGHEST)
            dc = jnp.where(lane < FOX_HEADS, d_ref[0, :, n * TK:(n + 1) * TK].T - dk_side, 0.0)
            rs = jnp.dot(tri, dc, preferred_element_type=F32, precision=lax.Precision.HIGHEST) + carry
            carry = rs[0:1, :]
            z = f_ref[0, n * TK:(n + 1) * TK, :] + b_ref[...]
            dz = rs * (1.0 / (1.0 + jnp.exp(z)))
            o_ref[0, n * TK:(n + 1) * TK, :] = dz.astype(BF16)
            dbsum = dbsum + jnp.sum(dz, axis=0, keepdims=True)
        row = lax.broadcasted_iota(jnp.int32, (8, LANES), 0)
        upd = jnp.where(row == 0, dbsum, 0.0)

        @pl.when(pl.program_id(0) == 0)
        def _():
            db_ref[...] = upd

        @pl.when(pl.program_id(0) != 0)
        def _():
            db_ref[...] += upd

    return pl.pallas_call(
        body, name="fox_gate_bwd", grid=(B,),
        in_specs=[pl.BlockSpec((1, LANES, S), lambda b: (b, 0, 0)),
                  pl.BlockSpec((1, S, FOX_W), lambda b: (b, 0, 0)),
                  pl.BlockSpec((1, S, LANES), lambda b: (b, 0, P_FLOG // LANES)),
                  pl.BlockSpec((1, LANES), lambda b: (0, 0))],
        out_specs=[pl.BlockSpec((1, S, LANES), lambda b: (b, 0, 0)), pl.BlockSpec((8, LANES), lambda b: (0, 0))],
        out_shape=[jax.ShapeDtypeStruct((B, S, LANES), BF16), jax.ShapeDtypeStruct((8, LANES), F32)],
        compiler_params=_params(("arbitrary",)),
    )(drow, dneg, proj3, b_pad)


def _mult_masks(S, kind):
    nd = S // TQ
    a = np.arange(TQ)[:, None]
    b = np.arange(TK)[None, :]
    out = np.zeros((nd, TQ, TK), np.float32)
    for d in range(nd):
        delta = d * TQ + a - b
        if kind == "causal":
            out[d] = delta >= 0
        else:
            m = np.zeros((TQ, TK), np.float32)
            for w, dil in DILATIONS:
                m += (delta >= 0) & (delta % dil == 0) & (delta <= w)
            out[d] = m
    return jnp.asarray(out)


def _rope_tables(S):
    half = ROPE_DIM // 2
    pos = jnp.arange(S, dtype=F32)
    inv_freq = 1.0 / (ROPE_THETA ** (jnp.arange(0, ROPE_DIM, 2, dtype=F32) / ROPE_DIM))
    ang = pos[:, None] * inv_freq[None, :]
    cos, sin = jnp.cos(ang), jnp.sin(ang)
    one = jnp.ones((S, HEAD_DIM - ROPE_DIM), F32)
    zero = jnp.zeros((S, HEAD_DIM - ROPE_DIM), F32)
    zh = jnp.zeros((S, half), F32)
    c = jnp.concatenate([cos, cos, one], axis=1)
    s1 = jnp.concatenate([-sin, zh, zero], axis=1)
    s2 = jnp.concatenate([zh, sin, zero], axis=1)
    return tuple(jnp.concatenate([t, t], axis=1) for t in (c, s1, s2))


def _rope(t, c, s1, s2):
    return t * c + pltpu.roll(t, LANES - half_rope(), 1) * s1 + pltpu.roll(t, half_rope(), 1) * s2


def half_rope():
    return ROPE_DIM // 2


def _rope_bwd(d, c, s1, s2):
    return d * c + pltpu.roll(d * s1, half_rope(), 1) + pltpu.roll(d * s2, LANES - half_rope(), 1)


def _scale_parts(scale):
    m, _ = math.frexp(scale)
    return (scale, None) if m == 0.5 else (None, scale)


def attn_fwd(kind, src, S, *, negc=None, mask=None, rope=None, kv=None):
    B = src.shape[0]
    pair = kind != "mem"
    col0 = {"fox": P_FOX, "dil": P_DIL, "mem": P_MQ}[kind]
    n_blocks = FOX_HEADS // 2 if pair else MEM_HEADS
    e_dim = HEAD_DIM if pair else MEM_HEAD_DIM
    q_fold, s_scale = _scale_parts(1.0 / math.sqrt(e_dim))
    Sk = S if pair else MEM_LEN
    nh = 2 if pair else 1
    has_bias = negc is not None
    has_rope = rope is not None
    nq = S // TQ

    def body(*refs):
        refs = list(refs)
        if pair:
            qkv_ref = refs.pop(0)
        else:
            q_ref, k_ref, v_ref = refs.pop(0), refs.pop(0), refs.pop(0)
        negc_ref = refs.pop(0) if has_bias else None
        mask_ref = refs.pop(0) if pair else None
        rope_refs = [refs.pop(0) for _ in range(3)] if has_rope else None
        o_ref, lse_ref, qs, ks, vs = refs
        lane = lax.broadcasted_iota(jnp.int32, (1, LANES), 1)

        def prep_q(n, _):
            r0 = pl.multiple_of(n * TQ, TQ)
            rows = pl.ds(r0, TQ)
            q = qkv_ref[0, rows, 0:LANES] if pair else q_ref[0, rows, :]
            if has_rope:
                q = _rope(q, *[t[rows, :] for t in rope_refs])
            if q_fold is not None:
                q = q * q_fold
            qs[rows, :] = q.astype(BF16)
            return 0

        def prep_kv(n, _):
            r0 = pl.multiple_of(n * TK, TK)
            rows = pl.ds(r0, TK)
            k = qkv_ref[0, rows, LANES:2 * LANES] if pair else k_ref[0, rows, :]
            v = qkv_ref[0, rows, 2 * LANES:3 * LANES] if pair else v_ref[0, rows, :]
            if has_rope:
                k = _rope(k, *[t[rows, :] for t in rope_refs])
            ks[rows, :] = k.astype(BF16)
            vs[rows, :] = v.astype(BF16)
            return 0

        lax.fori_loop(0, nq, prep_q, 0)
        lax.fori_loop(0, Sk // TK, prep_kv, 0)

        def q_loop(i, _):
            r0 = pl.multiple_of(i * TQ, TQ)
            q = qs[pl.ds(r0, TQ), :]
            res = []
            for hh in range(nh):
                hmask = (lane >= HEAD_DIM * hh) & (lane < HEAD_DIM * (hh + 1))
                qh = jnp.where(hmask, q, jnp.zeros_like(q)) if pair else q

                def kv_loop(j, carry, qh=qh, hh=hh):
                    m, l, acc = carry
                    c0 = pl.multiple_of(j * TK, TK)
                    k = ks[pl.ds(c0, TK), :]
                    v = vs[pl.ds(c0, TK), :]
                    s = lax.dot_general(qh, k, (((1,), (1,)), ((), ())), preferred_element_type=F32)
                    if s_scale is not None:
                        s = s * s_scale
                    if has_bias:
                        s = s + negc_ref[0, 0, pl.ds(hh, 1), pl.ds(c0, TK)]
                    if pair:
                        mult = mask_ref[i - j]
                        s = jnp.where(mult > 0.0, s, NEG_INF)
                    m_new = jnp.maximum(m, jnp.max(s, axis=1, keepdims=True))
                    p = jnp.exp(s - m_new)
                    if pair:
                        p = p * mult
                    alpha = jnp.exp(m - m_new)
                    l = alpha * l + jnp.sum(p, axis=1, keepdims=True)
                    acc = acc * alpha + jnp.dot(p.astype(BF16), v, preferred_element_type=F32)
                    return m_new, l, acc

                init = (jnp.full((TQ, 1), NEG_INF, F32), jnp.zeros((TQ, 1), F32), jnp.zeros((TQ, LANES), F32))
                m, l, acc = lax.fori_loop(0, (i + 1) if pair else Sk // TK, kv_loop, init)
                res.append((acc / l, m + jnp.log(l)))
            if pair:
                o = jnp.where(lane < HEAD_DIM, res[0][0], res[1][0])
                lse = jnp.where(lane < HEAD_DIM, res[0][1], res[1][1])
            else:
                o = res[0][0]
                lse = jnp.broadcast_to(res[0][1], (TQ, LANES))
            o_ref[0, pl.ds(r0, TQ), :] = o
            lse_ref[0, pl.ds(r0, TQ), :] = lse
            return 0

        lax.fori_loop(0, nq, q_loop, 0)

    ins, in_specs = [], []
    if pair:
        ins.append(src)
        in_specs.append(pl.BlockSpec((1, S, PAIR_W), lambda b, h: (b, 0, col0 // PAIR_W + h)))
    else:
        ins += [src, kv, kv]
        in_specs += [pl.BlockSpec((1, S, LANES), lambda b, h: (b, 0, col0 // LANES + h)),
                     pl.BlockSpec((1, MEM_LEN, LANES), lambda b, h: (b, 0, h)),
                     pl.BlockSpec((1, MEM_LEN, LANES), lambda b, h: (b, 0, MEM_HEADS + h))]
    if has_bias:
        ins.append(negc)
        in_specs.append(pl.BlockSpec((1, 1, 2, S), lambda b, h: (b, h, 0, 0)))
    if pair:
        ins.append(mask)
        in_specs.append(pl.BlockSpec(mask.shape, lambda b, h: (0, 0, 0)))
    if has_rope:
        ins += list(rope)
        in_specs += [pl.BlockSpec((S, LANES), lambda b, h: (0, 0))] * 3
    W = n_blocks * LANES
    out_spec = pl.BlockSpec((1, S, LANES), lambda b, h: (b, 0, h))
    return pl.pallas_call(
        body, name=kind + "_attn_fwd", grid=(B, n_blocks),
        in_specs=in_specs, out_specs=[out_spec, out_spec],
        out_shape=[jax.ShapeDtypeStruct((B, S, W), F32)] * 2,
        scratch_shapes=[pltpu.VMEM((S, LANES), BF16), pltpu.VMEM((Sk, LANES), BF16), pltpu.VMEM((Sk, LANES), BF16)],
        compiler_params=_params(("arbitrary", "arbitrary")),
    )(*ins)


def attn_bwd(kind, src, do, o, lse, S, *, negc=None, mask=None, rope=None, kv=None):
    B = src.shape[0]
    pair = kind != "mem"
    col0 = {"fox": P_FOX, "dil": P_DIL, "mem": P_MQ}[kind]
    n_blocks = FOX_HEADS // 2 if pair else MEM_HEADS
    e_dim = HEAD_DIM if pair else MEM_HEAD_DIM
    scale = 1.0 / math.sqrt(e_dim)
    q_fold, s_scale = _scale_parts(scale)
    Sk = S if pair else MEM_LEN
    nh = 2 if pair else 1
    has_bias = negc is not None
    has_rope = rope is not None
    nq = S // TQ
    nk = Sk // TK

    def body(*refs):
        refs = list(refs)
        if pair:
            qkv_ref = refs.pop(0)
        else:
            q_ref, k_ref, v_ref = refs.pop(0), refs.pop(0), refs.pop(0)
        do_ref, o_ref, lse_ref = refs.pop(0), refs.pop(0), refs.pop(0)
        negc_ref = refs.pop(0) if has_bias else None
        mask_ref = refs.pop(0) if pair else None
        rope_refs = [refs.pop(0) for _ in range(3)] if has_rope else None
        if pair:
            dqkv_ref = refs.pop(0)
            dnegc_ref = refs.pop(0) if has_bias else None
            drow_ref = refs.pop(0) if has_bias else None
        else:
            dq_ref, dk_ref, dv_ref = refs.pop(0), refs.pop(0), refs.pop(0)
        qs, ks, vs, dos, delta_s, dq_acc = refs[:6]
        drow_acc = refs[6] if has_bias else None
        lane = lax.broadcasted_iota(jnp.int32, (1, LANES), 1)

        def prep_q(n, _):
            r0 = pl.multiple_of(n * TQ, TQ)
            rows = pl.ds(r0, TQ)
            q = qkv_ref[0, rows, 0:LANES] if pair else q_ref[0, rows, :]
            if has_rope:
                q = _rope(q, *[t[rows, :] for t in rope_refs])
            if q_fold is not None:
                q = q * q_fold
            qs[rows, :] = q.astype(BF16)
            dov = do_ref[0, rows, :]
            dob = dov.astype(BF16)
            dos[rows, :] = dob
            prod = dob.astype(F32) * o_ref[0, rows, :]
            if pair:
                d0 = jnp.sum(jnp.where(lane < HEAD_DIM, prod, 0.0), axis=1, keepdims=True)
                d1 = jnp.sum(jnp.where(lane < HEAD_DIM, 0.0, prod), axis=1, keepdims=True)
                delta_s[rows, :] = jnp.where(lane < HEAD_DIM, d0, d1)
            else:
                delta_s[rows, :] = jnp.broadcast_to(jnp.sum(prod, axis=1, keepdims=True), (TQ, LANES))
            dq_acc[rows, :] = jnp.zeros((TQ, LANES), F32)
            if has_bias:
                drow_acc[rows, :] = jnp.zeros((TQ, LANES), F32)
            return 0

        def prep_kv(n, _):
            r0 = pl.multiple_of(n * TK, TK)
            rows = pl.ds(r0, TK)
            k = qkv_ref[0, rows, LANES:2 * LANES] if pair else k_ref[0, rows, :]
            v = qkv_ref[0, rows, 2 * LANES:3 * LANES] if pair else v_ref[0, rows, :]
            if has_rope:
                k = _rope(k, *[t[rows, :] for t in rope_refs])
            ks[rows, :] = k.astype(BF16)
            vs[rows, :] = v.astype(BF16)
            return 0

        lax.fori_loop(0, nq, prep_q, 0)
        lax.fori_loop(0, nk, prep_kv, 0)

        def kv_loop(j, _):
            c0 = pl.multiple_of(j * TK, TK)
            kt = ks[pl.ds(c0, TK), :]
            vt = vs[pl.ds(c0, TK), :]
            res = []
            for hh in range(nh):
                hmask = (lane >= HEAD_DIM * hh) & (lane < HEAD_DIM * (hh + 1))
                kh = jnp.where(hmask, kt, jnp.zeros_like(kt)) if pair else kt
                vh = jnp.where(hmask, vt, jnp.zeros_like(vt)) if pair else vt

                def q_loop(i, carry, kh=kh, vh=vh, hh=hh, hmask=hmask):
                    dk, dv, dneg = carry
                    r0 = pl.multiple_of(i * TQ, TQ)
                    rows = pl.ds(r0, TQ)
                    q = qs[rows, :]
                    dot = dos[rows, :]
                    lse_i = lse_ref[0, rows, hh * HEAD_DIM:hh * HEAD_DIM + 1]
                    delta_i = delta_s[rows, hh * HEAD_DIM:hh * HEAD_DIM + 1]
                    s = lax.dot_general(q, kh, (((1,), (1,)), ((), ())), preferred_element_type=F32)
                    if s_scale is not None:
                        s = s * s_scale
                    if has_bias:
                        s = s + negc_ref[0, 0, pl.ds(hh, 1), pl.ds(c0, TK)]
                    if pair:
                        mult = mask_ref[i - j]
                        s = jnp.where(mult > 0.0, s, NEG_INF)
                    p = jnp.exp(s - lse_i)
                    if pair:
                        p = p * mult
                    dv = dv + lax.dot_general(p.astype(BF16), dot, (((0,), (0,)), ((), ())),
                                              preferred_element_type=F32)
                    dp = lax.dot_general(dot, vh, (((1,), (1,)), ((), ())), preferred_element_type=F32)
                    ds = p * (dp - delta_i)
                    if has_bias:
                        dneg = dneg + jnp.sum(ds, axis=0, keepdims=True)
                        drow_acc[rows, :] += jnp.where(hmask, jnp.sum(ds, axis=1, keepdims=True), 0.0)
                    if s_scale is not None:
                        ds = ds * s_scale
                    dsb = ds.astype(BF16)
                    dk = dk + lax.dot_general(dsb, q, (((0,), (0,)), ((), ())), preferred_element_type=F32)
                    dq = jnp.dot(dsb, kh, preferred_element_type=F32)
                    dq_acc[rows, :] += dq
                    return dk, dv, dneg

                init = (jnp.zeros((TK, LANES), F32), jnp.zeros((TK, LANES), F32), jnp.zeros((1, TK), F32))
                dk, dv, dneg = lax.fori_loop(j if pair else 0, nq, q_loop, init)
                if has_bias:
                    dnegc_ref[0, 0, pl.ds(hh, 1), pl.ds(c0, TK)] = dneg
                res.append((dk, dv))
            if pair:
                dk = jnp.where(lane < HEAD_DIM, res[0][0], res[1][0])
                dv = jnp.where(lane < HEAD_DIM, res[0][1], res[1][1])
                if has_rope:
                    dk = _rope_bwd(dk, *[t[pl.ds(c0, TK), :] for t in rope_refs])
                dqkv_ref[0, pl.ds(c0, TK), LANES:2 * LANES] = dk.astype(BF16)
                dqkv_ref[0, pl.ds(c0, TK), 2 * LANES:3 * LANES] = dv.astype(BF16)
            else:
                dk_ref[0, pl.ds(c0, TK), :] = res[0][0].astype(BF16)
                dv_ref[0, pl.ds(c0, TK), :] = res[0][1].astype(BF16)
            return 0

        lax.fori_loop(0, nk, kv_loop, 0)

        def fin_q(n, _):
            r0 = pl.multiple_of(n * TQ, TQ)
            rows = pl.ds(r0, TQ)
            dq = dq_acc[rows, :]
            if q_fold is not None:
                dq = dq * q_fold
            if has_rope:
                dq = _rope_bwd(dq, *[t[rows, :] for t in rope_refs])
            if pair:
                dqkv_ref[0, rows, 0:LANES] = dq.astype(BF16)
            else:
                dq_ref[0, rows, :] = dq.astype(BF16)
            if has_bias:
                drow_ref[0, rows, :] = drow_acc[rows, :]
            return 0

        lax.fori_loop(0, nq, fin_q, 0)

    ins, in_specs = [], []
    if pair:
        ins.append(src)
        in_specs.append(pl.BlockSpec((1, S, PAIR_W), lambda b, h: (b, 0, col0 // PAIR_W + h)))
    else:
        ins += [src, kv, kv]
        in_specs += [pl.BlockSpec((1, S, LANES), lambda b, h: (b, 0, col0 // LANES + h)),
                     pl.BlockSpec((1, MEM_LEN, LANES), lambda b, h: (b, 0, h)),
                     pl.BlockSpec((1, MEM_LEN, LANES), lambda b, h: (b, 0, MEM_HEADS + h))]
    row_spec = pl.BlockSpec((1, S, LANES), lambda b, h: (b, 0, h))
    ins += [do, o, lse]
    in_specs += [row_spec] * 3
    if has_bias:
        ins.append(negc)
        in_specs.append(pl.BlockSpec((1, 1, 2, S), lambda b, h: (b, h, 0, 0)))
    if pair:
        ins.append(mask)
        in_specs.append(pl.BlockSpec(mask.shape, lambda b, h: (0, 0, 0)))
    if has_rope:
        ins += list(rope)
        in_specs += [pl.BlockSpec((S, LANES), lambda b, h: (0, 0))] * 3
    W = n_blocks * LANES
    if pair:
        out_specs = [pl.BlockSpec((1, S, PAIR_W), lambda b, h: (b, 0, h))]
        out_shape = [jax.ShapeDtypeStruct((B, S, 3 * W), BF16)]
        if has_bias:
            out_specs.append(pl.BlockSpec((1, 1, 2, S), lambda b, h: (b, h, 0, 0)))
            out_shape.append(jax.ShapeDtypeStruct((B, LANES // 2, 2, S), F32))
            out_specs.append(row_spec)
            out_shape.append(jax.ShapeDtypeStruct((B, S, W), F32))
    else:
        kv_spec = pl.BlockSpec((1, MEM_LEN, LANES), lambda b, h: (b, 0, h))
        out_specs = [row_spec, kv_spec, kv_spec]
        out_shape = [jax.ShapeDtypeStruct((B, S, W), BF16)] + [jax.ShapeDtypeStruct((B, MEM_LEN, W), BF16)] * 2
    return pl.pallas_call(
        body, name=kind + "_attn_bwd", grid=(B, n_blocks),
        in_specs=in_specs, out_specs=out_specs, out_shape=out_shape,
        scratch_shapes=[pltpu.VMEM((S, LANES), BF16), pltpu.VMEM((Sk, LANES), BF16), pltpu.VMEM((Sk, LANES), BF16),
                        pltpu.VMEM((S, LANES), BF16), pltpu.VMEM((S, LANES), F32), pltpu.VMEM((S, LANES), F32)]
        + ([pltpu.VMEM((S, LANES), F32)] if has_bias else []),
        compiler_params=_params(("arbitrary", "arbitrary")),
    )(*ins)


def _log_masks(S, kind):
    nd = 1 if kind == "causal" else S // TQ
    a = np.arange(TQ)[:, None]
    b = np.arange(TK)[None, :]
    out = np.zeros((nd, TQ, TK), np.float32)
    for d in range(nd):
        delta = d * TQ + a - b
        if kind == "causal":
            m = (delta >= 0).astype(np.float64)
        else:
            m = sum(((delta >= 0) & (delta % dil == 0) & (delta <= w)).astype(np.float64) for w, dil in DILATIONS)
        out[d] = np.where(m > 0, np.log(np.maximum(m, 1.0)), NEG_INF)
    return jnp.asarray(out)


def _attn_setup(kind):
    pair = kind != "mem"
    e_dim = HEAD_DIM if pair else MEM_HEAD_DIM
    q_fold, s_scale = _scale_parts(1.0 / math.sqrt(e_dim))
    return dict(pair=pair, col0={"fox": P_FOX, "dil": P_DIL, "mem": P_MQ}[kind],
                n_blocks=FOX_HEADS // 2 if pair else MEM_HEADS, q_fold=q_fold, s_scale=s_scale,
                nh=2 if pair else 1)


def _attn_inputs(kind, src, S, negc, mask, rope, kv, extra):
    cfg = _attn_setup(kind)
    col0 = cfg["col0"]
    ins, in_specs = [], []
    if cfg["pair"]:
        ins.append(src)
        in_specs.append(pl.BlockSpec((1, S, PAIR_W), lambda b, h: (b, 0, col0 // PAIR_W + h)))
    else:
        ins += [src, kv, kv]
        in_specs += [pl.BlockSpec((1, S, LANES), lambda b, h: (b, 0, col0 // LANES + h)),
                     pl.BlockSpec((1, MEM_LEN, LANES), lambda b, h: (b, 0, h)),
                     pl.BlockSpec((1, MEM_LEN, LANES), lambda b, h: (b, 0, MEM_HEADS + h))]
    ins += list(extra)
    in_specs += [pl.BlockSpec((1, S, LANES), lambda b, h: (b, 0, h))] * len(extra)
    if negc is not None:
        ins.append(negc)
        in_specs.append(pl.BlockSpec((1, 1, 2, S), lambda b, h: (b, h, 0, 0)))
    if mask is not None:
        ins.append(mask)
        in_specs.append(pl.BlockSpec(mask.shape, lambda b, h: (0, 0, 0)))
    if rope is not None:
        ins += list(rope)
        in_specs += [pl.BlockSpec((S, LANES), lambda b, h: (0, 0))] * 3
    return ins, in_specs


def _prep_rows(cfg, rope_refs, lane, load_q, load_kv, qs2, ks, vs, S, Sk):
    nh = cfg["nh"]
    R = nh * TQ

    def prep_q(n, _):
        rows = pl.ds(pl.multiple_of(n * TQ, TQ), TQ)
        q = load_q(rows)
        if rope_refs is not None:
            q = _rope(q, *[t[rows, :] for t in rope_refs])
        if cfg["q_fold"] is not None:
            q = q * cfg["q_fold"]
        _store_stacked(cfg, lane, qs2, n, q.astype(BF16))
        return 0

    def prep_kv(n, _):
        rows = pl.ds(pl.multiple_of(n * TK, TK), TK)
        k, v = load_kv(rows)
        if rope_refs is not None:
            k = _rope(k, *[t[rows, :] for t in rope_refs])
        ks[rows, :] = k.astype(BF16)
        vs[rows, :] = v.astype(BF16)
        return 0

    lax.fori_loop(0, S // TQ, prep_q, 0)
    lax.fori_loop(0, Sk // TK, prep_kv, 0)


def _store_stacked(cfg, lane, dst, n, val):
    nh = cfg["nh"]
    R = nh * TQ
    if nh == 1:
        dst[pl.ds(pl.multiple_of(n * R, R), TQ), :] = val
        return
    for hh in range(nh):
        hmask = (lane >= HEAD_DIM * hh) & (lane < HEAD_DIM * (hh + 1))
        dst[pl.ds(pl.multiple_of(n * R + hh * TQ, TQ), TQ), :] = jnp.where(hmask, val, jnp.zeros_like(val))


def _cat(parts, axis):
    return parts[0] if len(parts) == 1 else jnp.concatenate(parts, axis=axis)


def attn_fwd2(kind, src, S, *, negc=None, mask=None, rope=None, kv=None):
    B = src.shape[0]
    cfg = _attn_setup(kind)
    pair, nh, s_scale = cfg["pair"], cfg["nh"], cfg["s_scale"]
    Sk = S if pair else MEM_LEN
    has_bias, has_rope = negc is not None, rope is not None
    R = nh * TQ

    def body(*refs):
        refs = list(refs)
        if pair:
            qkv_ref = refs.pop(0)
        else:
            q_ref, k_ref, v_ref = refs.pop(0), refs.pop(0), refs.pop(0)
        negc_ref = refs.pop(0) if has_bias else None
        mask_ref = refs.pop(0) if mask is not None else None
        rope_refs = [refs.pop(0) for _ in range(3)] if has_rope else None
        o_ref, lse_ref, qs2, ks, vs = refs
        lane = lax.broadcasted_iota(jnp.int32, (1, LANES), 1)

        if pair:
            load_q = lambda rows: qkv_ref[0, rows, 0:LANES]
            load_kv = lambda rows: (qkv_ref[0, rows, LANES:2 * LANES], qkv_ref[0, rows, 2 * LANES:3 * LANES])
        else:
            load_q = lambda rows: q_ref[0, rows, :]
            load_kv = lambda rows: (k_ref[0, rows, :], v_ref[0, rows, :])
        _prep_rows(cfg, rope_refs, lane, load_q, load_kv, qs2, ks, vs, S, Sk)

        def q_loop(i, _):
            q2 = qs2[pl.ds(pl.multiple_of(i * R, R), R), :]

            def step(j, carry, midx):
                ms, ls, acc = carry
                c0 = pl.multiple_of(j * TK, TK)
                k = ks[pl.ds(c0, TK), :]
                v = vs[pl.ds(c0, TK), :]
                s2 = lax.dot_general(q2, k, (((1,), (1,)), ((), ())), preferred_element_type=F32)
                if s_scale is not None:
                    s2 = s2 * s_scale
                new_m, new_l, ps, alphas = [], [], [], []
                for hh in range(nh):
                    s = s2[hh * TQ:(hh + 1) * TQ]
                    if has_bias:
                        s = s + negc_ref[0, 0, pl.ds(hh, 1), pl.ds(c0, TK)]
                    if midx is not None:
                        s = s + mask_ref[midx]
                    m_new = jnp.maximum(ms[hh], jnp.max(s, axis=1, keepdims=True))
                    p = jnp.exp(s - m_new)
                    alpha = jnp.exp(ms[hh] - m_new)
                    new_l.append(alpha * ls[hh] + jnp.sum(p, axis=1, keepdims=True))
                    new_m.append(m_new)
                    ps.append(p.astype(BF16))
                    alphas.append(alpha)
                acc = acc * _cat(alphas, 0) + jnp.dot(_cat(ps, 0), v, preferred_element_type=F32)
                return tuple(new_m), tuple(new_l), acc

            init = (tuple(jnp.full((TQ, 1), NEG_INF, F32) for _ in range(nh)),
                    tuple(jnp.zeros((TQ, 1), F32) for _ in range(nh)), jnp.zeros((R, LANES), F32))
            if kind == "fox":
                carry = lax.fori_loop(0, i, lambda j, c: step(j, c, None), init)
                carry = step(i, carry, 0)
            elif kind == "dil":
                carry = lax.fori_loop(0, i + 1, lambda j, c: step(j, c, i - j), init)
            else:
                carry = lax.fori_loop(0, Sk // TK, lambda j, c: step(j, c, None), init)
            ms, ls, acc = carry
            outs = [acc[hh * TQ:(hh + 1) * TQ] / ls[hh] for hh in range(nh)]
            lses = [ms[hh] + jnp.log(ls[hh]) for hh in range(nh)]
            rows = pl.ds(pl.multiple_of(i * TQ, TQ), TQ)
            if pair:
                o_ref[0, rows, :] = jnp.where(lane < HEAD_DIM, outs[0], outs[1])
                lse_ref[0, rows, :] = jnp.where(lane < HEAD_DIM, lses[0], lses[1])
            else:
                o_ref[0, rows, :] = outs[0]
                lse_ref[0, rows, :] = jnp.broadcast_to(lses[0], (TQ, LANES))
            return 0

        lax.fori_loop(0, S // TQ, q_loop, 0)

    ins, in_specs = _attn_inputs(kind, src, S, negc, mask, rope, kv, ())
    W = cfg["n_blocks"] * LANES
    out_spec = pl.BlockSpec((1, S, LANES), lambda b, h: (b, 0, h))
    return pl.pallas_call(
        body, name=kind + "_attn_fwd", grid=(B, cfg["n_blocks"]),
        in_specs=in_specs, out_specs=[out_spec, out_spec],
        out_shape=[jax.ShapeDtypeStruct((B, S, W), F32)] * 2,
        scratch_shapes=[pltpu.VMEM((nh * S, LANES), BF16), pltpu.VMEM((Sk, LANES), BF16),
                        pltpu.VMEM((Sk, LANES), BF16)],
        compiler_params=_params(("arbitrary", "arbitrary")),
    )(*ins)


def attn_bwd2(kind, src, do, o, lse, S, *, negc=None, mask=None, rope=None, kv=None):
    B = src.shape[0]
    cfg = _attn_setup(kind)
    pair, nh, s_scale, q_fold = cfg["pair"], cfg["nh"], cfg["s_scale"], cfg["q_fold"]
    Sk = S if pair else MEM_LEN
    has_bias, has_rope = negc is not None, rope is not None
    R = nh * TQ
    nq, nk = S // TQ, Sk // TK

    def body(*refs):
        refs = list(refs)
        if pair:
            qkv_ref = refs.pop(0)
        else:
            q_ref, k_ref, v_ref = refs.pop(0), refs.pop(0), refs.pop(0)
        do_ref, o_ref, lse_ref = refs.pop(0), refs.pop(0), refs.pop(0)
        negc_ref = refs.pop(0) if has_bias else None
        mask_ref = refs.pop(0) if mask is not None else None
        rope_refs = [refs.pop(0) for _ in range(3)] if has_rope else None
        if pair:
            dqkv_ref = refs.pop(0)
            dnegc_ref = refs.pop(0) if has_bias else None
            drow_ref = refs.pop(0) if has_bias else None
        else:
            dq_ref, dk_ref, dv_ref = refs.pop(0), refs.pop(0), refs.pop(0)
        qs2, ks, vs, dos2, lse_s, delta_s, dk_acc, dv_acc = refs[:8]
        dneg_acc = refs[8] if has_bias else None
        lane = lax.broadcasted_iota(jnp.int32, (1, LANES), 1)

        if pair:
            load_q = lambda rows: qkv_ref[0, rows, 0:LANES]
            load_kv = lambda rows: (qkv_ref[0, rows, LANES:2 * LANES], qkv_ref[0, rows, 2 * LANES:3 * LANES])
        else:
            load_q = lambda rows: q_ref[0, rows, :]
            load_kv = lambda rows: (k_ref[0, rows, :], v_ref[0, rows, :])
        _prep_rows(cfg, rope_refs, lane, load_q, load_kv, qs2, ks, vs, S, Sk)

        def prep_do(n, _):
            rows = pl.ds(pl.multiple_of(n * TQ, TQ), TQ)
            dob = do_ref[0, rows, :].astype(BF16)
            _store_stacked(cfg, lane, dos2, n, dob)
            prod = dob.astype(F32) * o_ref[0, rows, :]
            lse_blk = lse_ref[0, rows, :]
            for hh in range(nh):
                dst = pl.ds(pl.multiple_of(n * R + hh * TQ, TQ), TQ)
                if pair:
                    hmask = (lane >= HEAD_DIM * hh) & (lane < HEAD_DIM * (hh + 1))
                    d = jnp.sum(jnp.where(hmask, prod, 0.0), axis=1, keepdims=True)
                    lse_s[dst, :] = jnp.broadcast_to(lse_blk[:, hh * HEAD_DIM:hh * HEAD_DIM + 1], (TQ, LANES))
                else:
                    d = jnp.sum(prod, axis=1, keepdims=True)
                    lse_s[dst, :] = lse_blk
                delta_s[dst, :] = jnp.broadcast_to(d, (TQ, LANES))
            return 0

        def zero_kv(n, _):
            rows = pl.ds(pl.multiple_of(n * TK, TK), TK)
            dk_acc[rows, :] = jnp.zeros((TK, LANES), F32)
            dv_acc[rows, :] = jnp.zeros((TK, LANES), F32)
            return 0

        lax.fori_loop(0, nq, prep_do, 0)
        lax.fori_loop(0, nk, zero_kv, 0)
        if has_bias:
            dneg_acc[...] = jnp.zeros(dneg_acc.shape, F32)

        def q_loop(i, _):
            rows2 = pl.ds(pl.multiple_of(i * R, R), R)
            q2 = qs2[rows2, :]
            do2 = dos2[rows2, :]
            lse2 = lse_s[rows2, :]
            delta2 = delta_s[rows2, :]
            wide = lambda t: jnp.concatenate([t] * (TK // LANES), axis=1)

            def step(j, carry, midx):
                dq2, drow = carry
                c0 = pl.multiple_of(j * TK, TK)
                kcols = pl.ds(c0, TK)
                k = ks[kcols, :]
                v = vs[kcols, :]
                s2 = lax.dot_general(q2, k, (((1,), (1,)), ((), ())), preferred_element_type=F32)
                if s_scale is not None:
                    s2 = s2 * s_scale
                if has_bias or midx is not None:
                    halves = []
                    for hh in range(nh):
                        s = s2[hh * TQ:(hh + 1) * TQ]
                        if has_bias:
                            s = s + negc_ref[0, 0, pl.ds(hh, 1), kcols]
                        if midx is not None:
                            s = s + mask_ref[midx]
                        halves.append(s)
                    s2 = _cat(halves, 0)
                p2 = jnp.exp(s2 - wide(lse2))
                dp2 = lax.dot_general(do2, v, (((1,), (1,)), ((), ())), preferred_element_type=F32)
                ds2 = p2 * (dp2 - wide(delta2))
                if has_bias:
                    drow = drow + jnp.sum(ds2, axis=1, keepdims=True)
                    for hh in range(nh):
                        dneg_acc[pl.ds(hh, 1), kcols] += jnp.sum(ds2[hh * TQ:(hh + 1) * TQ], axis=0, keepdims=True)
                if s_scale is not None:
                    ds2 = ds2 * s_scale
                dsb = ds2.astype(BF16)
                dv_acc[kcols, :] += lax.dot_general(p2.astype(BF16), do2, (((0,), (0,)), ((), ())),
                                                    preferred_element_type=F32)
                dk_acc[kcols, :] += lax.dot_general(dsb, q2, (((0,), (0,)), ((), ())), preferred_element_type=F32)
                dq2 = dq2 + jnp.dot(dsb, k, preferred_element_type=F32)
                return dq2, drow

            init = (jnp.zeros((R, LANES), F32), jnp.zeros((R, 1), F32))
            if kind == "fox":
                carry = lax.fori_loop(0, i, lambda j, c: step(j, c, None), init)
                carry = step(i, carry, 0)
            elif kind == "dil":
                carry = lax.fori_loop(0, i + 1, lambda j, c: step(j, c, i - j), init)
            else:
                carry = lax.fori_loop(0, nk, lambda j, c: step(j, c, None), init)
            dq2, drow = carry
            rows = pl.ds(pl.multiple_of(i * TQ, TQ), TQ)
            dq = jnp.where(lane < HEAD_DIM, dq2[0:TQ], dq2[TQ:2 * TQ]) if pair else dq2
            if q_fold is not None:
                dq = dq * q_fold
            if has_rope:
                dq = _rope_bwd(dq, *[t[rows, :] for t in rope_refs])
            if pair:
                dqkv_ref[0, rows, 0:LANES] = dq.astype(BF16)
            else:
                dq_ref[0, rows, :] = dq.astype(BF16)
            if has_bias:
                drow_ref[0, rows, :] = jnp.where(lane < HEAD_DIM, drow[0:TQ], drow[TQ:2 * TQ])
            return 0

        lax.fori_loop(0, nq, q_loop, 0)

        def fin_kv(n, _):
            rows = pl.ds(pl.multiple_of(n * TK, TK), TK)
            dk = dk_acc[rows, :]
            if has_rope:
                dk = _rope_bwd(dk, *[t[rows, :] for t in rope_refs])
            if pair:
                dqkv_ref[0, rows, LANES:2 * LANES] = dk.astype(BF16)
                dqkv_ref[0, rows, 2 * LANES:3 * LANES] = dv_acc[rows, :].astype(BF16)
            else:
                dk_ref[0, rows, :] = dk.astype(BF16)
                dv_ref[0, rows, :] = dv_acc[rows, :].astype(BF16)
            return 0

        lax.fori_loop(0, nk, fin_kv, 0)
        if has_bias:
            dnegc_ref[0, 0] = dneg_acc[...]

    ins, in_specs = _attn_inputs(kind, src, S, negc, mask, rope, kv, (do, o, lse))
    W = cfg["n_blocks"] * LANES
    row_spec = pl.BlockSpec((1, S, LANES), lambda b, h: (b, 0, h))
    if pair:
        out_specs = [pl.BlockSpec((1, S, PAIR_W), lambda b, h: (b, 0, h))]
        out_shape = [jax.ShapeDtypeStruct((B, S, 3 * W), BF16)]
        if has_bias:
            out_specs += [pl.BlockSpec((1, 1, 2, S), lambda b, h: (b, h, 0, 0)), row_spec]
            out_shape += [jax.ShapeDtypeStruct((B, LANES // 2, 2, S), F32), jax.ShapeDtypeStruct((B, S, W), F32)]
    else:
        kv_spec = pl.BlockSpec((1, MEM_LEN, LANES), lambda b, h: (b, 0, h))
        out_specs = [row_spec, kv_spec, kv_spec]
        out_shape = [jax.ShapeDtypeStruct((B, S, W), BF16)] + [jax.ShapeDtypeStruct((B, MEM_LEN, W), BF16)] * 2
    scratch = [pltpu.VMEM((nh * S, LANES), BF16), pltpu.VMEM((Sk, LANES), BF16), pltpu.VMEM((Sk, LANES), BF16),
               pltpu.VMEM((nh * S, LANES), BF16), pltpu.VMEM((nh * S, LANES), F32), pltpu.VMEM((nh * S, LANES), F32),
               pltpu.VMEM((Sk, LANES), F32), pltpu.VMEM((Sk, LANES), F32)]
    if has_bias:
        scratch.append(pltpu.VMEM((2, S), F32))
    return pl.pallas_call(
        body, name=kind + "_attn_bwd", grid=(B, cfg["n_blocks"]),
        in_specs=in_specs, out_specs=out_specs, out_shape=out_shape, scratch_shapes=scratch,
        compiler_params=_params(("arbitrary", "arbitrary")),
    )(*ins)


def _log_masks_t(S, kind):
    return jnp.swapaxes(_log_masks(S, kind), 1, 2)


def _head_rows(hh, pair):
    row = lax.broadcasted_iota(jnp.int32, (LANES, 1), 0)
    if not pair:
        return row >= 0
    return (row >= HEAD_DIM * hh) & (row < HEAD_DIM * (hh + 1))


def _attn_t_inputs(kind, src, S, negc_cols, mask, rope, kv):
    cfg = _attn_setup(kind)
    col0 = cfg["col0"]
    ins, in_specs = [], []
    if cfg["pair"]:
        ins.append(src)
        in_specs.append(pl.BlockSpec((1, S, PAIR_W), lambda b, h: (b, 0, col0 // PAIR_W + h)))
    else:
        ins += [src, kv, kv]
        in_specs += [pl.BlockSpec((1, S, LANES), lambda b, h: (b, 0, col0 // LANES + h)),
                     pl.BlockSpec((1, MEM_LEN, LANES), lambda b, h: (b, 0, h)),
                     pl.BlockSpec((1, MEM_LEN, LANES), lambda b, h: (b, 0, MEM_HEADS + h))]
    if negc_cols is not None:
        ins.append(negc_cols)
        in_specs.append(pl.BlockSpec((1, S, LANES), lambda b, h: (b, 0, 0)))
    if mask is not None:
        ins.append(mask)
        in_specs.append(pl.BlockSpec(mask.shape, lambda b, h: (0, 0, 0)))
    if rope is not None:
        ins += list(rope)
        in_specs += [pl.BlockSpec((S, LANES), lambda b, h: (0, 0))] * 3
    return ins, in_specs


def _attn_t_prep(cfg, refs, S, Sk, *, qT2s, ks, q2s=None, vs=None, vTs=None, kTs=None, nb=None):
    pair, nh = cfg["pair"], cfg["nh"]
    lane = lax.broadcasted_iota(jnp.int32, (1, LANES), 1)
    rope_refs = refs["rope"]

    def prep_q(n, _):
        rows = pl.ds(pl.multiple_of(n * TQ, TQ), TQ)
        q = refs["load_q"](rows)
        if rope_refs is not None:
            q = _rope(q, *[t[rows, :] for t in rope_refs])
        if cfg["q_fold"] is not None:
            q = q * cfg["q_fold"]
        qb = q.astype(BF16)
        if q2s is not None:
            _store_stacked(cfg, lane, q2s, n, qb)
        qtb = qb.astype(F32).T.astype(BF16)
        for hh in range(nh):
            qT2s[n, :, hh * TQ:(hh + 1) * TQ] = jnp.where(_head_rows(hh, pair), qtb, jnp.zeros_like(qtb))
        return 0

    def prep_kv(n, _):
        rows = pl.ds(pl.multiple_of(n * TK, TK), TK)
        k, v = refs["load_kv"](rows)
        if rope_refs is not None:
            k = _rope(k, *[t[rows, :] for t in rope_refs])
        kb = k.astype(BF16)
        vb = v.astype(BF16)
        ks[rows, :] = kb
        if vs is not None:
            vs[rows, :] = vb
        if vTs is not None:
            vTs[n] = vb.astype(F32).T.astype(BF16)
        if kTs is not None:
            kTs[n] = kb.astype(F32).T.astype(BF16)
        if nb is not None:
            blk = refs["negc"][0, rows, :]
            for hh in range(nh):
                h = 2 * refs["block"] + hh
                col = jnp.sum(jnp.where(lane == h, blk, 0.0), axis=1, keepdims=True)
                nb[hh, rows, :] = jnp.broadcast_to(col, (TK, LANES))
        return 0

    lax.fori_loop(0, S // TQ, prep_q, 0)
    lax.fori_loop(0, Sk // TK, prep_kv, 0)


def _scores_t(cfg, kind, k, qT2, nb, mask_ref, kc, midx):
    nh, s_scale = cfg["nh"], cfg["s_scale"]
    sT = jnp.dot(k, qT2, preferred_element_type=F32)
    if s_scale is not None:
        sT = sT * s_scale
    if nb is None and midx is None:
        return sT
    parts = []
    for hh in range(nh):
        t = sT[:, hh * TQ:(hh + 1) * TQ]
        if nb is not None:
            t = t + jnp.concatenate([nb[hh, kc, :]] * (TQ // LANES), axis=1)
        if midx is not None:
            t = t + mask_ref[midx]
        parts.append(t)
    return _cat(parts, 1)


def _kv_sweep(kind, i, nk, step, init):
    if kind == "fox":
        carry = lax.fori_loop(0, i, lambda j, c: step(j, c, None), init)
        return step(i, carry, 0)
    if kind == "dil":
        return lax.fori_loop(0, i + 1, lambda j, c: step(j, c, i - j), init)
    return lax.fori_loop(0, nk, lambda j, c: step(j, c, None), init)


def attn_fwd3(kind, src, S, *, negc_cols=None, mask=None, rope=None, kv=None):
    B = src.shape[0]
    cfg = _attn_setup(kind)
    pair, nh = cfg["pair"], cfg["nh"]
    Sk = S if pair else MEM_LEN
    has_bias, has_rope = negc_cols is not None, rope is not None
    R = nh * TQ
    nq, nk = S // TQ, Sk // TK

    def body(*refs):
        refs = list(refs)
        if pair:
            qkv_ref = refs.pop(0)
            load_q = lambda rows: qkv_ref[0, rows, 0:LANES]
            load_kv = lambda rows: (qkv_ref[0, rows, LANES:2 * LANES], qkv_ref[0, rows, 2 * LANES:3 * LANES])
        else:
            q_ref, k_ref, v_ref = refs.pop(0), refs.pop(0), refs.pop(0)
            load_q = lambda rows: q_ref[0, rows, :]
            load_kv = lambda rows: (k_ref[0, rows, :], v_ref[0, rows, :])
        negc_ref = refs.pop(0) if has_bias else None
        mask_ref = refs.pop(0) if mask is not None else None
        rope_refs = [refs.pop(0) for _ in range(3)] if has_rope else None
        o_ref, lse_ref, qT2s, ks, vTs = refs[:5]
        nb = refs[5] if has_bias else None
        _attn_t_prep(cfg, dict(load_q=load_q, load_kv=load_kv, rope=rope_refs, negc=negc_ref,
                               block=pl.program_id(1)), S, Sk,
                     qT2s=qT2s, ks=ks, vTs=vTs, nb=nb)

        def q_loop(i, _):
            qT2 = qT2s[i]

            def step(j, carry, midx):
                m, l, accT = carry
                kc = pl.ds(pl.multiple_of(j * TK, TK), TK)
                sT = _scores_t(cfg, kind, ks[kc, :], qT2, nb, mask_ref, kc, midx)
                m_new = jnp.maximum(m, jnp.max(sT, axis=0, keepdims=True))
                p = jnp.exp(sT - m_new)
                alpha = jnp.exp(m - m_new)
                l = alpha * l + jnp.sum(p, axis=0, keepdims=True)
                accT = accT * alpha + jnp.dot(vTs[j], p.astype(BF16), preferred_element_type=F32)
                return m_new, l, accT

            init = (jnp.full((1, R), NEG_INF, F32), jnp.zeros((1, R), F32), jnp.zeros((LANES, R), F32))
            m, l, accT = _kv_sweep(kind, i, nk, step, init)
            oT2 = accT / l
            oT = jnp.where(_head_rows(0, True), oT2[:, 0:TQ], oT2[:, TQ:2 * TQ]) if pair else oT2
            o_ref[0, pl.ds(pl.multiple_of(i * TQ, TQ), TQ), :] = oT.T
            lse_ref[0, 0, pl.ds(i, 1), :] = m + jnp.log(l)
            return 0

        lax.fori_loop(0, nq, q_loop, 0)

    ins, in_specs = _attn_t_inputs(kind, src, S, negc_cols, mask, rope, kv)
    W = cfg["n_blocks"] * LANES
    scratch = [pltpu.VMEM((nq, LANES, R), BF16), pltpu.VMEM((Sk, LANES), BF16), pltpu.VMEM((nk, LANES, TK), BF16)]
    if has_bias:
        scratch.append(pltpu.VMEM((nh, Sk, LANES), F32))
    return pl.pallas_call(
        body, name=kind + "_attn_fwd", grid=(B, cfg["n_blocks"]),
        in_specs=in_specs,
        out_specs=[pl.BlockSpec((1, S, LANES), lambda b, h: (b, 0, h)),
                   pl.BlockSpec((1, 1, nq, R), lambda b, h: (b, h, 0, 0))],
        out_shape=[jax.ShapeDtypeStruct((B, S, W), F32), jax.ShapeDtypeStruct((B, cfg["n_blocks"], nq, R), F32)],
        scratch_shapes=scratch,
        compiler_params=_params(("arbitrary", "arbitrary")),
    )(*ins)


def attn_bwd3(kind, src, do, o, lse, S, *, negc_cols=None, mask=None, rope=None, kv=None):
    B = src.shape[0]
    cfg = _attn_setup(kind)
    pair, nh, s_scale, q_fold = cfg["pair"], cfg["nh"], cfg["s_scale"], cfg["q_fold"]
    Sk = S if pair else MEM_LEN
    has_bias, has_rope = negc_cols is not None, rope is not None
    R = nh * TQ
    nq, nk = S // TQ, Sk // TK

    def body(*refs):
        refs = list(refs)
        if pair:
            qkv_ref = refs.pop(0)
            load_q = lambda rows: qkv_ref[0, rows, 0:LANES]
            load_kv = lambda rows: (qkv_ref[0, rows, LANES:2 * LANES], qkv_ref[0, rows, 2 * LANES:3 * LANES])
        else:
            q_ref, k_ref, v_ref = refs.pop(0), refs.pop(0), refs.pop(0)
            load_q = lambda rows: q_ref[0, rows, :]
            load_kv = lambda rows: (k_ref[0, rows, :], v_ref[0, rows, :])
        negc_ref = refs.pop(0) if has_bias else None
        mask_ref = refs.pop(0) if mask is not None else None
        rope_refs = [refs.pop(0) for _ in range(3)] if has_rope else None
        do_ref, o_ref, lse_ref = refs.pop(0), refs.pop(0), refs.pop(0)
        if pair:
            dqkv_ref = refs.pop(0)
            dneg_ref = refs.pop(0) if has_bias else None
            drow_ref = refs.pop(0) if has_bias else None
        else:
            dq_ref, dk_ref, dv_ref = refs.pop(0), refs.pop(0), refs.pop(0)
        qT2s, ks, q2s, vs, kTs, doT2s, do2s, delta_s, dk_acc, dv_acc = refs[:10]
        nb, dneg_acc = (refs[10], refs[11]) if has_bias else (None, None)
        lane = lax.broadcasted_iota(jnp.int32, (1, LANES), 1)
        _attn_t_prep(cfg, dict(load_q=load_q, load_kv=load_kv, rope=rope_refs, negc=negc_ref,
                               block=pl.program_id(1)), S, Sk,
                     qT2s=qT2s, ks=ks, q2s=q2s, vs=vs, kTs=kTs, nb=nb)

        def prep_do(n, _):
            rows = pl.ds(pl.multiple_of(n * TQ, TQ), TQ)
            dob = do_ref[0, rows, :].astype(BF16)
            _store_stacked(cfg, lane, do2s, n, dob)
            doT = dob.astype(F32).T
            prodT = doT * o_ref[0, rows, :].T
            doTb = doT.astype(BF16)
            for hh in range(nh):
                hm = _head_rows(hh, pair)
                doT2s[n, :, hh * TQ:(hh + 1) * TQ] = jnp.where(hm, doTb, jnp.zeros_like(doTb))
                delta_s[pl.ds(n, 1), hh * TQ:(hh + 1) * TQ] = jnp.sum(jnp.where(hm, prodT, 0.0), axis=0, keepdims=True)
            return 0

        def zero_kv(n, _):
            rows = pl.ds(pl.multiple_of(n * TK, TK), TK)
            dk_acc[rows, :] = jnp.zeros((TK, LANES), F32)
            dv_acc[rows, :] = jnp.zeros((TK, LANES), F32)
            if has_bias:
                for hh in range(nh):
                    dneg_acc[hh, rows, :] = jnp.zeros((TK, LANES), F32)
            return 0

        lax.fori_loop(0, nq, prep_do, 0)
        lax.fori_loop(0, nk, zero_kv, 0)

        def q_loop(i, _):
            rows2 = pl.ds(pl.multiple_of(i * R, R), R)
            qT2 = qT2s[i]
            doT2 = doT2s[i]
            q2 = q2s[rows2, :]
            do2 = do2s[rows2, :]
            lse_i = lse_ref[0, 0, pl.ds(i, 1), :]
            delta_i = delta_s[pl.ds(i, 1), :]

            def step(j, carry, midx):
                dqT2, drow = carry
                kc = pl.ds(pl.multiple_of(j * TK, TK), TK)
                sT = _scores_t(cfg, kind, ks[kc, :], qT2, nb, mask_ref, kc, midx)
                pT = jnp.exp(sT - lse_i)
                dpT = jnp.dot(vs[kc, :], doT2, preferred_element_type=F32)
                dsT = pT * (dpT - delta_i)
                if has_bias:
                    drow = drow + jnp.sum(dsT, axis=0, keepdims=True)
                    for hh in range(nh):
                        part = dsT[:, hh * TQ:hh * TQ + LANES]
                        for t in range(1, TQ // LANES):
                            part = part + dsT[:, hh * TQ + t * LANES:hh * TQ + (t + 1) * LANES]
                        dneg_acc[hh, kc, :] += part
                if s_scale is not None:
                    dsT = dsT * s_scale
                dsb = dsT.astype(BF16)
                dv_acc[kc, :] += jnp.dot(pT.astype(BF16), do2, preferred_element_type=F32)
                dk_acc[kc, :] += jnp.dot(dsb, q2, preferred_element_type=F32)
                dqT2 = dqT2 + jnp.dot(kTs[j], dsb, preferred_element_type=F32)
                return dqT2, drow

            init = (jnp.zeros((LANES, R), F32), jnp.zeros((1, R), F32))
            dqT2, drow = _kv_sweep(kind, i, nk, step, init)
            rows = pl.ds(pl.multiple_of(i * TQ, TQ), TQ)
            dqT = jnp.where(_head_rows(0, True), dqT2[:, 0:TQ], dqT2[:, TQ:2 * TQ]) if pair else dqT2
            dq = dqT.T
            if q_fold is not None:
                dq = dq * q_fold
            if has_rope:
                dq = _rope_bwd(dq, *[t[rows, :] for t in rope_refs])
            if pair:
                dqkv_ref[0, rows, 0:LANES] = dq.astype(BF16)
            else:
                dq_ref[0, rows, :] = dq.astype(BF16)
            if has_bias:
                drow_ref[0, 0, pl.ds(i, 1), :] = drow
            return 0

        lax.fori_loop(0, nq, q_loop, 0)

        def fin_kv(n, _):
            rows = pl.ds(pl.multiple_of(n * TK, TK), TK)
            dk = dk_acc[rows, :]
            if has_rope:
                dk = _rope_bwd(dk, *[t[rows, :] for t in rope_refs])
            if pair:
                dqkv_ref[0, rows, LANES:2 * LANES] = dk.astype(BF16)
                dqkv_ref[0, rows, 2 * LANES:3 * LANES] = dv_acc[rows, :].astype(BF16)
            else:
                dk_ref[0, rows, :] = dk.astype(BF16)
                dv_ref[0, rows, :] = dv_acc[rows, :].astype(BF16)
            if has_bias:
                x0 = jnp.sum(dneg_acc[0, rows, :], axis=1, keepdims=True)
                x1 = jnp.sum(dneg_acc[1, rows, :], axis=1, keepdims=True)
                dneg_ref[0, rows, :] = jnp.where(lane == 0, x0, jnp.where(lane == 1, x1, 0.0))
            return 0

        lax.fori_loop(0, nk, fin_kv, 0)

    ins, in_specs = _attn_t_inputs(kind, src, S, negc_cols, mask, rope, kv)
    row_spec = pl.BlockSpec((1, S, LANES), lambda b, h: (b, 0, h))
    vec_spec = pl.BlockSpec((1, 1, nq, R), lambda b, h: (b, h, 0, 0))
    ins += [do, o, lse]
    in_specs += [row_spec, row_spec, vec_spec]
    W = cfg["n_blocks"] * LANES
    if pair:
        out_specs = [pl.BlockSpec((1, S, PAIR_W), lambda b, h: (b, 0, h))]
        out_shape = [jax.ShapeDtypeStruct((B, S, 3 * W), BF16)]
        if has_bias:
            out_specs += [row_spec, vec_spec]
            out_shape += [jax.ShapeDtypeStruct((B, S, W), F32), jax.ShapeDtypeStruct((B, cfg["n_blocks"], nq, R), F32)]
    else:
        kv_spec = pl.BlockSpec((1, MEM_LEN, LANES), lambda b, h: (b, 0, h))
        out_specs = [row_spec, kv_spec, kv_spec]
        out_shape = [jax.ShapeDtypeStruct((B, S, W), BF16)] + [jax.ShapeDtypeStruct((B, MEM_LEN, W), BF16)] * 2
    scratch = [pltpu.VMEM((nq, LANES, R), BF16), pltpu.VMEM((Sk, LANES), BF16), pltpu.VMEM((nh * S, LANES), BF16),
               pltpu.VMEM((Sk, LANES), BF16), pltpu.VMEM((nk, LANES, TK), BF16), pltpu.VMEM((nq, LANES, R), BF16),
               pltpu.VMEM((nh * S, LANES), BF16), pltpu.VMEM((nq, R), F32),
               pltpu.VMEM((Sk, LANES), F32), pltpu.VMEM((Sk, LANES), F32)]
    if has_bias:
        scratch += [pltpu.VMEM((nh, Sk, LANES), F32), pltpu.VMEM((nh, Sk, LANES), F32)]
    return pl.pallas_call(
        body, name=kind + "_attn_bwd", grid=(B, cfg["n_blocks"]),
        in_specs=in_specs, out_specs=out_specs, out_shape=out_shape, scratch_shapes=scratch,
        compiler_params=_params(("arbitrary", "arbitrary")),
    )(*ins)


def _sigmoid(g):
    return 1.0 / (1.0 + jnp.exp(-g))


def out_fwd(proj, o_fox, o_dil, o_mem, w_out, x, target, gf, tm):
    T = x.shape[0]

    def body(fg_ref, dg_ref, mg_ref, of_ref, od_ref, om_ref, w_ref, x_ref, t_ref, gf_ref,
             y_ref, dx_ref, dxb_ref, sm_ref):
        parts = []
        for g_ref, o_ref in ((fg_ref, of_ref), (dg_ref, od_ref), (mg_ref, om_ref)):
            g = g_ref[...]
            parts.append((o_ref[...] * (g * _sigmoid(g))).astype(BF16))
        ymix = jnp.concatenate(parts, axis=1)
        y_ref[...] = ymix
        x2 = x_ref[...] + jnp.dot(ymix, w_ref[...], preferred_element_type=F32)
        r = lax.rsqrt(jnp.mean(x2 * x2, axis=-1, keepdims=True) + RMS_EPS)
        yn = x2 * r
        err = yn * gf_ref[...] - t_ref[...]
        loss = 0.5 * jnp.sum(jnp.sum(err * err, axis=-1, keepdims=True) / D_MODEL, axis=0, keepdims=True)
        dyf = err / D_MODEL
        dgf = jnp.sum(dyf * yn, axis=0, keepdims=True)
        dyn = dyf * gf_ref[...]
        dx2 = r * (dyn - yn * jnp.mean(dyn * yn, axis=-1, keepdims=True))
        dx_ref[...] = dx2
        dxb_ref[...] = dx2.astype(BF16)
        row = lax.broadcasted_iota(jnp.int32, (8, D_MODEL), 0)
        upd = jnp.where(row == 0, dgf, jnp.where(row == 1, loss, 0.0))

        @pl.when(pl.program_id(0) == 0)
        def _():
            sm_ref[...] = upd

        @pl.when(pl.program_id(0) != 0)
        def _():
            sm_ref[...] += upd

    def rows(w, col=0):
        return pl.BlockSpec((tm, w), lambda i: (i, col))

    return pl.pallas_call(
        body, name="out_fwd", grid=(T // tm,),
        in_specs=[rows(FOX_W, P_FG // FOX_W), rows(DIL_W, P_DG // DIL_W), rows(MEM_W, P_MG // MEM_W),
                  rows(FOX_W), rows(DIL_W), rows(MEM_W),
                  pl.BlockSpec((MIX_W, D_MODEL), lambda i: (0, 0)),
                  rows(D_MODEL), rows(D_MODEL), pl.BlockSpec((1, D_MODEL), lambda i: (0, 0))],
        out_specs=[rows(MIX_W), rows(D_MODEL), rows(D_MODEL), pl.BlockSpec((8, D_MODEL), lambda i: (0, 0))],
        out_shape=[jax.ShapeDtypeStruct((T, MIX_W), BF16), jax.ShapeDtypeStruct((T, D_MODEL), F32),
                   jax.ShapeDtypeStruct((T, D_MODEL), BF16), jax.ShapeDtypeStruct((8, D_MODEL), F32)],
        compiler_params=_params(("arbitrary",)),
    )(proj, proj, proj, o_fox, o_dil, o_mem, w_out, x, target, gf)


def out_bwd(proj, o_fox, o_dil, o_mem, w_out, dx2b, tm):
    T = dx2b.shape[0]

    def body(fg_ref, dg_ref, mg_ref, of_ref, od_ref, om_ref, w_ref, dx_ref,
             dof_ref, dod_ref, dom_ref, dfg_ref, ddg_ref, dmg_ref):
        dmix = lax.dot_general(dx_ref[...], w_ref[...], (((1,), (1,)), ((), ())), preferred_element_type=F32)
        col = 0
        for g_ref, o_ref, do_ref, dgate_ref in ((fg_ref, of_ref, dof_ref, dfg_ref), (dg_ref, od_ref, dod_ref, ddg_ref),
                                                 (mg_ref, om_ref, dom_ref, dmg_ref)):
            w = g_ref.shape[1]
            d = dmix[:, col:col + w]
            col += w
            g = g_ref[...]
            sg = _sigmoid(g)
            do_ref[...] = d * (g * sg)
            dgate_ref[...] = (d * o_ref[...] * (sg * (1.0 + g * (1.0 - sg)))).astype(BF16)

    def rows(w, col=0):
        return pl.BlockSpec((tm, w), lambda i: (i, col))

    return pl.pallas_call(
        body, name="out_bwd", grid=(T // tm,),
        in_specs=[rows(FOX_W, P_FG // FOX_W), rows(DIL_W, P_DG // DIL_W), rows(MEM_W, P_MG // MEM_W),
                  rows(FOX_W), rows(DIL_W), rows(MEM_W),
                  pl.BlockSpec((MIX_W, D_MODEL), lambda i: (0, 0)), rows(D_MODEL)],
        out_specs=[rows(FOX_W), rows(DIL_W), rows(MEM_W), rows(FOX_W), rows(DIL_W), rows(MEM_W)],
        out_shape=[jax.ShapeDtypeStruct((T, FOX_W), F32), jax.ShapeDtypeStruct((T, DIL_W), F32),
                   jax.ShapeDtypeStruct((T, MEM_W), F32), jax.ShapeDtypeStruct((T, FOX_W), BF16),
                   jax.ShapeDtypeStruct((T, DIL_W), BF16), jax.ShapeDtypeStruct((T, MEM_W), BF16)],
        compiler_params=_params(("arbitrary",)),
    )(proj, proj, proj, o_fox, o_dil, o_mem, w_out, dx2b)


def adamw(w, g, m, v, tr, name):
    R, C = w.shape

    def body(w_ref, g_ref, m_ref, v_ref, d_ref, mo_ref, vo_ref):
        gv = g_ref[...]
        mn = ADAM_B1 * m_ref[...] + (1.0 - ADAM_B1) * gv
        vn = ADAM_B2 * v_ref[...] + (1.0 - ADAM_B2) * jnp.square(gv)
        m_hat = mn / (1.0 - ADAM_B1 ** ADAM_STEP)
        v_hat = vn / (1.0 - ADAM_B2 ** ADAM_STEP)
        d_ref[...] = -ADAM_LR * (m_hat / (jnp.sqrt(v_hat) + ADAM_EPS) + ADAM_WD * w_ref[...])
        mo_ref[...] = mn
        vo_ref[...] = vn

    spec = pl.BlockSpec((tr, C), lambda i: (i, 0))
    return pl.pallas_call(
        body, name=name, grid=(R // tr,),
        in_specs=[spec] * 4, out_specs=[spec] * 3,
        out_shape=[jax.ShapeDtypeStruct((R, C), F32)] * 3,
        compiler_params=_params(("arbitrary",)),
    )(w, g, m, v)


def _pad_row(v, width):
    return jnp.concatenate([v, jnp.zeros((1, width - v.shape[1]), v.dtype)], axis=1)


def local_grads(x, mem, norm_g, b_forget, mem_norm_g, final_norm_g, loss_target, w_in_p, w_kv, w_out):
    B, S, D = x.shape
    T = B * S
    xt = x.reshape(T, D)
    memt = mem.reshape(B * MEM_LEN, D)
    b_pad = _pad_row(b_forget, LANES)

    h = rms_fwd(xt, norm_g, 512, "rms_x")
    proj = mm_nn(h, w_in_p, 512, PW // 3, "in_proj")
    proj3 = proj.reshape(B, S, PW)
    mh = rms_fwd(memt, mem_norm_g, B * MEM_LEN, "rms_mem")
    mkv = mm_nn(mh, w_kv, B * MEM_LEN, 2 * MEM_W, "mem_kv_proj")
    mkv3 = mkv.reshape(B, MEM_LEN, 2 * MEM_W)

    negc = fox_gate(proj3, b_pad)
    causal = _log_masks_t(S, "causal")
    dilated = _log_masks_t(S, "dilated")
    rope = _rope_tables(S)

    o_fox, lse_fox = attn_fwd3("fox", proj3, S, negc_cols=negc, mask=causal)
    o_dil, lse_dil = attn_fwd3("dil", proj3, S, mask=dilated, rope=rope)
    o_mem, lse_mem = attn_fwd3("mem", proj3, S, kv=mkv3)

    ymix, dx2, dx2b, small_out = out_fwd(
        proj, o_fox.reshape(T, FOX_W), o_dil.reshape(T, DIL_W), o_mem.reshape(T, MEM_W), w_out,
        xt, loss_target.reshape(T, D), final_norm_g.reshape(1, D), 256)
    do_fox, do_dil, do_mem, dfg, ddg, dmg = out_bwd(
        proj, o_fox.reshape(T, FOX_W), o_dil.reshape(T, DIL_W), o_mem.reshape(T, MEM_W), w_out, dx2b, 256)
    g_out = mm_tn(ymix, dx2b, 512, D, "w_out_grad")

    dqkv_fox, dneg, drow = attn_bwd3("fox", proj3, do_fox.reshape(B, S, FOX_W), o_fox, lse_fox, S,
                                     negc_cols=negc, mask=causal)
    (dqkv_dil,) = attn_bwd3("dil", proj3, do_dil.reshape(B, S, DIL_W), o_dil, lse_dil, S, mask=dilated, rope=rope)
    dmq, dmk, dmv = attn_bwd3("mem", proj3, do_mem.reshape(B, S, MEM_W), o_mem, lse_mem, S, kv=mkv3)
    drow = drow.reshape(B, FOX_HEADS // 2, S // TQ, 2, TQ).transpose(0, 1, 3, 2, 4).reshape(B, FOX_HEADS, S)
    drow = jnp.pad(drow, ((0, 0), (0, LANES - FOX_HEADS), (0, 0)))
    dflog, db_part = fox_gate_bwd(drow, dneg, proj3, b_pad)

    dproj = jnp.concatenate([dqkv_fox.reshape(T, 3 * FOX_W), dfg, dqkv_dil.reshape(T, 3 * DIL_W), ddg,
                             dmq.reshape(T, MEM_W), dmg, dflog.reshape(T, LANES)], axis=1)
    g_in = mm_tn(h, dproj, 512, PW // 3, "w_in_grad")
    dh = mm_nt(dproj, w_in_p, 1024, PW // 3, "in_proj_bwd")
    grad_x, dng = rms_bwd(xt, norm_g, dh, dx2, 512, "rms_x_bwd")

    dmkv = jnp.concatenate([dmk, dmv], axis=2).reshape(B * MEM_LEN, 2 * MEM_W)
    g_kv = mm_tn(mh, dmkv, B * MEM_LEN, 2 * MEM_W, "w_kv_grad")
    dmh = mm_nt(dmkv, w_kv, B * MEM_LEN, D, "mem_kv_bwd")
    _, dmng = rms_bwd(memt, mem_norm_g, dmh, None, B * MEM_LEN, "rms_mem_bwd")

    small = jnp.concatenate([dng[0:1], dmng[0:1], small_out[0:1], _pad_row(db_part[0:1], D), small_out[1:2],
                             jnp.zeros((3, D), F32)], axis=0)
    return grad_x.reshape(B, S, D), g_in, g_kv, g_out, small


def kernel(x, mem, norm_g, w_in, b_forget, mem_norm_g, w_mem_kv, w_out, final_norm_g, loss_target, m_norm_g, m_w_in, m_b_forget, m_mem_norm_g, m_w_mem_kv, m_w_out, m_final_norm_g, v_norm_g, v_w_in, v_b_forget, v_mem_norm_g, v_w_mem_kv, v_w_out, v_final_norm_g):
    D = D_MODEL
    w_in_full, w_kv_full, w_out_full = weight_gather(
        [_pack_cols(w_in[0]).astype(BF16), w_mem_kv[0].astype(BF16), w_out[0].astype(BF16)])
    grad_x, g_in, g_kv, g_out, small = local_grads(
        x, mem, norm_g, b_forget, mem_norm_g, final_norm_g, loss_target, w_in_full, w_kv_full, w_out_full)

    s_in, s_kv, s_out, s_small = grad_exchange([g_in, g_kv, g_out], small)
    gw_in = _unpack_cols(slot_sum(s_in, 16, "sum_w_in"))
    gw_kv = slot_sum(s_kv, 128, "sum_w_kv")
    gw_out = slot_sum(s_out, 256, "sum_w_out")
    tot = slot_sum(s_small, 8, "sum_small")

    loss = tot[4, 0]
    g_norm, g_mem_norm, g_final, g_b = tot[0:1], tot[1:2], tot[2], tot[3:4, :FOX_HEADS]

    def rows8(*rows):
        rows = [r.reshape(1, -1) for r in rows]
        rows = [_pad_row(r, D) for r in rows]
        return jnp.concatenate(rows + [jnp.zeros((8 - len(rows), D), F32)], axis=0)

    sw = rows8(norm_g, mem_norm_g, final_norm_g, b_forget)
    sm = rows8(m_norm_g, m_mem_norm_g, m_final_norm_g, m_b_forget)
    sv = rows8(v_norm_g, v_mem_norm_g, v_final_norm_g, v_b_forget)
    d_s, m_s, v_s = adamw(sw, tot, sm, sv, 8, "adamw_small")
    d_in, m_in, v_in = adamw(w_in[0], gw_in, m_w_in[0], v_w_in[0], 32, "adamw_w_in")
    d_kv, m_kv, v_kv = adamw(w_mem_kv[0], gw_kv, m_w_mem_kv[0], v_w_mem_kv[0], 128, "adamw_w_kv")
    d_out, m_out, v_out = adamw(w_out[0], gw_out, m_w_out[0], v_w_out[0], 256, "adamw_w_out")

    def small_outs(t):
        return t[0:1], t[3:4, :FOX_HEADS], t[1:2], t[2]

    grads = (g_norm, gw_in[None], g_b, g_mem_norm, gw_kv[None], gw_out[None], g_final)
    outs = []
    for t, big in ((d_s, (d_in, d_kv, d_out)), (m_s, (m_in, m_kv, m_out)), (v_s, (v_in, v_kv, v_out))):
        n, b, mn, f = small_outs(t)
        outs += [n, big[0][None], b, mn, big[1][None], big[2][None], f]
    return (loss, grad_x, *grads, *outs)
```

```python
import functools
import math

import numpy as np
import jax
import jax.numpy as jnp
from jax import lax
from jax.experimental import pallas as pl
from jax.experimental.pallas import tpu as pltpu

F32 = jnp.float32
BF16 = jnp.bfloat16

D_MODEL = 1024
HEAD_DIM = 64
FOX_HEADS = 12
DIL_HEADS = 12
MEM_HEADS = 4
MEM_HEAD_DIM = 128
MEM_LEN = 256
FOX_W = FOX_HEADS * HEAD_DIM
DIL_W = DIL_HEADS * HEAD_DIM
MEM_W = MEM_HEADS * MEM_HEAD_DIM
MIX_W = FOX_W + DIL_W + MEM_W
DILATIONS = ((128, 1), (512, 4), (2048, 16))
ROPE_THETA = 500000.0
ROPE_DIM = HEAD_DIM // 4
RMS_EPS = 1e-6
NEG_INF = -1e30
IN_W = 4 * FOX_W + FOX_HEADS + 4 * DIL_W + 2 * MEM_W

ADAM_LR = 0.001
ADAM_B1 = 0.9
ADAM_B2 = 0.999
ADAM_EPS = 1e-08
ADAM_WD = 0.01
ADAM_STEP = 10

N_DEV = 8
LANES = 128
PAIR_W = 3 * LANES
TQ = 256
TK = 256

O_FQ, O_FK, O_FV, O_FG = 0, FOX_W, 2 * FOX_W, 3 * FOX_W
O_FLOG = 4 * FOX_W
O_DQ = O_FLOG + FOX_HEADS
O_DK, O_DV, O_DG = O_DQ + DIL_W, O_DQ + 2 * DIL_W, O_DQ + 3 * DIL_W
O_MQ = O_DQ + 4 * DIL_W
O_MG = O_MQ + MEM_W
P_FOX = 0
P_FG = P_FOX + 3 * FOX_W
P_DIL = P_FG + FOX_W
P_DG = P_DIL + 3 * DIL_W
P_MQ = P_DG + DIL_W
P_MG = P_MQ + MEM_W
P_FLOG = P_MG + MEM_W
PW = P_FLOG + LANES

VMEM_LIMIT = 56 * 1024 * 1024


def _pack_pieces():
    pieces = []
    for base in (O_FQ, O_DQ):
        seg = []
        for hp in range(FOX_HEADS // 2):
            for part in range(3):
                seg.append((base + part * FOX_W + hp * LANES, LANES))
        pieces.append(seg)
    fox, dil = pieces
    return fox + [(O_FG, FOX_W)] + dil + [(O_DG, DIL_W), (O_MQ, MEM_W), (O_MG, MEM_W), (O_FLOG, FOX_HEADS)]


def _pack_cols(w):
    parts = [w[..., s:s + n] for s, n in _pack_pieces()]
    parts.append(jnp.zeros(w.shape[:-1] + (LANES - FOX_HEADS,), w.dtype))
    return jnp.concatenate(parts, axis=-1)


def _unpack_cols(g):
    runs = []
    pos = 0
    for s, n in _pack_pieces():
        runs.append((s, n, pos))
        pos += n
    runs.sort()
    return jnp.concatenate([g[..., p:p + n] for s, n, p in runs], axis=-1)


def _params(sem=None, **kw):
    return pltpu.CompilerParams(dimension_semantics=sem, vmem_limit_bytes=VMEM_LIMIT, **kw)


def _mesh_pos():
    return lax.axis_index("x"), lax.axis_index("y"), lax.axis_index("c")


def _flip(v, d):
    return 1 - v if d else v


_RELATIONS = [(dx, dy, dc) for dx in (0, 1) for dy in (0, 1) for dc in (0, 1)][1:]


def weight_gather(shards):
    n_arr = len(shards)
    rows = [s.shape[0] for s in shards]

    def body(*refs):
        in_refs = refs[:n_arr]
        out_refs = refs[n_arr:2 * n_arr]
        send_sems, recv_sems, local_sems = refs[2 * n_arr:]
        x, y, c = _mesh_pos()
        me, sibling = (x, y, c), (x, y, 1 - c)
        chips = [(1 - x, y), (x, 1 - y), (1 - x, 1 - y)]

        def block(a, pos):
            px, py, pc = pos
            return out_refs[a].at[pl.ds((4 * px + 2 * py + pc) * rows[a], rows[a]), :]

        def copy(a, k, blk, to, src=None):
            return pltpu.make_async_remote_copy(
                src_ref=block(a, blk) if src is None else src, dst_ref=block(a, blk),
                send_sem=send_sems.at[a, k], recv_sem=recv_sems.at[a, k],
                device_id=to, device_id_type=pl.DeviceIdType.MESH)

        started = []
        mine = []
        for a in range(n_arr):
            cp = pltpu.make_async_copy(in_refs[a], block(a, me), local_sems.at[a])
            cp.start()
            mine.append(cp)
            first = [copy(a, 0, me, sibling, src=in_refs[a])]
            first += [copy(a, 1 + j, me, (*chip, c), src=in_refs[a]) for j, chip in enumerate(chips)]
            for cp in first:
                cp.start()
            started += first
        for a in range(n_arr):
            for j, chip in enumerate(chips):
                copy(a, 1 + j, (*chip, c), me).wait_recv()
                passed = copy(a, 4 + j, (*chip, c), sibling)
                passed.start()
                started.append(passed)
        for a in range(n_arr):
            copy(a, 0, sibling, me).wait_recv()
            for j, chip in enumerate(chips):
                copy(a, 4 + j, (*chip, 1 - c), me).wait_recv()
        for cp in started:
            cp.wait_send()
        for cp in mine:
            cp.wait()

    any_spec = pl.BlockSpec(memory_space=pl.ANY)
    return pl.pallas_call(
        body, name="weight_gather",
        out_shape=[jax.ShapeDtypeStruct((N_DEV * s.shape[0], s.shape[1]), s.dtype) for s in shards],
        in_specs=[any_spec] * n_arr, out_specs=[any_spec] * n_arr,
        scratch_shapes=[pltpu.SemaphoreType.DMA((n_arr, 7)), pltpu.SemaphoreType.DMA((n_arr, 7)),
                        pltpu.SemaphoreType.DMA((n_arr,))],
    )(*shards)


def grad_exchange(grads, small):
    arrs = list(grads) + [small]
    n_arr = len(arrs)
    rows = [g.shape[0] // N_DEV for g in grads] + [small.shape[0]]

    def body(*refs):
        in_refs = refs[:n_arr]
        out_refs = refs[n_arr:2 * n_arr]
        send_sems, recv_sems, local_sems = refs[2 * n_arr:]
        x, y, c = _mesh_pos()
        me = 4 * x + 2 * y + c

        def src(a, idx):
            if a == n_arr - 1:
                return in_refs[a]
            return in_refs[a].at[pl.ds(idx * rows[a], rows[a]), :]

        def copy(a, k):
            dx, dy, dc = _RELATIONS[k]
            px, py, pc = _flip(x, dx), _flip(y, dy), _flip(c, dc)
            peer = 4 * px + 2 * py + pc
            send = pltpu.make_async_remote_copy(
                src_ref=src(a, peer), dst_ref=out_refs[a].at[me],
                send_sem=send_sems.at[a, k], recv_sem=recv_sems.at[a, k],
                device_id=(px, py, pc), device_id_type=pl.DeviceIdType.MESH)
            recv = pltpu.make_async_remote_copy(
                src_ref=src(a, peer), dst_ref=out_refs[a].at[peer],
                send_sem=send_sems.at[a, k], recv_sem=recv_sems.at[a, k],
                device_id=(px, py, pc), device_id_type=pl.DeviceIdType.MESH)
            return send, recv

        mine = []
        pairs = []
        for a in range(n_arr):
            cp = pltpu.make_async_copy(src(a, me), out_refs[a].at[me], local_sems.at[a])
            cp.start()
            mine.append(cp)
            for k in range(7):
                send, recv = copy(a, k)
                send.start()
                pairs.append((send, recv))
        for send, recv in pairs:
            recv.wait_recv()
        for send, recv in pairs:
            send.wait_send()
        for cp in mine:
            cp.wait()

    any_spec = pl.BlockSpec(memory_space=pl.ANY)
    return pl.pallas_call(
        body, name="grad_exchange",
        out_shape=[jax.ShapeDtypeStruct((N_DEV, r, a.shape[1]), a.dtype) for r, a in zip(rows, arrs)],
        in_specs=[any_spec] * n_arr, out_specs=[any_spec] * n_arr,
        scratch_shapes=[pltpu.SemaphoreType.DMA((n_arr, 7)), pltpu.SemaphoreType.DMA((n_arr, 7)),
                        pltpu.SemaphoreType.DMA((n_arr,))],
    )(*arrs)


def slot_sum(slots, tr, name):
    _, R, C = slots.shape

    def body(s_ref, o_ref):
        acc = s_ref[0]
        for d in range(1, N_DEV):
            acc = acc + s_ref[d]
        o_ref[...] = acc

    return pl.pallas_call(
        body, name=name, grid=(R // tr,),
        in_specs=[pl.BlockSpec((N_DEV, tr, C), lambda i: (0, i, 0))],
        out_specs=pl.BlockSpec((tr, C), lambda i: (i, 0)),
        out_shape=jax.ShapeDtypeStruct((R, C), slots.dtype),
        compiler_params=_params(("arbitrary",)),
    )(slots)


N_CHIP = 4
_OTHER_CHIPS = [(1, 0), (0, 1), (1, 1)]


def grad_exchange_d2d(grads, small):
    n_big = len(grads)
    rows = [g.shape[0] // N_DEV for g in grads]

    def body(*refs):
        g_refs = refs[:n_big]
        small_ref = refs[n_big]
        out_refs = refs[n_big + 1:2 * n_big + 1]
        csum_ref = refs[2 * n_big + 1]
        land, send_sems, recv_sems = refs[2 * n_big + 2:]
        x, y, c = _mesh_pos()
        sibling = (x, y, 1 - c)
        copies = []
        for a in range(n_big):
            for q in range(N_CHIP):
                copies.append(pltpu.make_async_remote_copy(
                    src_ref=g_refs[a].at[pl.ds((2 * q + 1 - c) * rows[a], rows[a]), :], dst_ref=out_refs[a].at[q],
                    send_sem=send_sems.at[a, q], recv_sem=recv_sems.at[a, q],
                    device_id=sibling, device_id_type=pl.DeviceIdType.MESH))
        copies.append(pltpu.make_async_remote_copy(
            src_ref=small_ref, dst_ref=land, send_sem=send_sems.at[n_big, 0], recv_sem=recv_sems.at[n_big, 0],
            device_id=sibling, device_id_type=pl.DeviceIdType.MESH))
        for cp in copies:
            cp.start()
        for cp in copies:
            cp.wait_recv()
        for cp in copies:
            cp.wait_send()
        csum_ref[...] = small_ref[...] + land[...]

    any_spec = pl.BlockSpec(memory_space=pl.ANY)
    vmem_spec = pl.BlockSpec(memory_space=pltpu.VMEM)
    return pl.pallas_call(
        body, name="grad_exchange_d2d",
        out_shape=[jax.ShapeDtypeStruct((N_CHIP, r, g.shape[1]), g.dtype) for r, g in zip(rows, grads)]
        + [jax.ShapeDtypeStruct(small.shape, small.dtype)],
        in_specs=[any_spec] * n_big + [vmem_spec], out_specs=[any_spec] * n_big + [vmem_spec],
        scratch_shapes=[pltpu.VMEM(small.shape, small.dtype),
                        pltpu.SemaphoreType.DMA((n_big + 1, N_CHIP)), pltpu.SemaphoreType.DMA((n_big + 1, N_CHIP))],
    )(*grads, small)


def chip_sum(g, got, tr, name):
    _, rows, cols = got.shape
    g4 = g.reshape(N_CHIP, 2, rows, cols)
    x, y, c = _mesh_pos()
    where = jnp.stack([c, 2 * x + y]).astype(jnp.int32)

    def body_all(w_ref, g_ref, r_ref, o_ref):
        o_ref[0] = (g_ref[0, 0] + r_ref[0]).astype(BF16)

    def body_own(w_ref, g_ref, r_ref, o_ref):
        o_ref[...] = g_ref[0, 0] + r_ref[0]

    cpb = pl.pallas_call(
        body_all, name=name + "_all",
        grid_spec=pltpu.PrefetchScalarGridSpec(
            num_scalar_prefetch=1, grid=(N_CHIP, rows // tr),
            in_specs=[pl.BlockSpec((1, 1, tr, cols), lambda q, i, w: (q, w[0], i, 0)),
                      pl.BlockSpec((1, tr, cols), lambda q, i, w: (q, i, 0))],
            out_specs=pl.BlockSpec((1, tr, cols), lambda q, i, w: (q, i, 0))),
        out_shape=jax.ShapeDtypeStruct((N_CHIP, rows, cols), BF16),
        compiler_params=_params(("arbitrary", "arbitrary")),
    )(where, g4, got)
    own = pl.pallas_call(
        body_own, name=name + "_own",
        grid_spec=pltpu.PrefetchScalarGridSpec(
            num_scalar_prefetch=1, grid=(rows // tr,),
            in_specs=[pl.BlockSpec((1, 1, tr, cols), lambda i, w: (w[1], w[0], i, 0)),
                      pl.BlockSpec((1, tr, cols), lambda i, w: (w[1], i, 0))],
            out_specs=pl.BlockSpec((tr, cols), lambda i, w: (i, 0))),
        out_shape=jax.ShapeDtypeStruct((rows, cols), F32),
        compiler_params=_params(("arbitrary",)),
    )(where, g4, got)
    return cpb, own


def grad_exchange_ici(parts, csum):
    n_big = len(parts)

    def body(*refs):
        p_refs = refs[:n_big]
        csum_ref = refs[n_big]
        out_refs = refs[n_big + 1:2 * n_big + 1]
        tot_ref = refs[2 * n_big + 1]
        land, send_sems, recv_sems = refs[2 * n_big + 2:]
        x, y, c = _mesh_pos()
        q_me = 2 * x + y
        land[q_me] = csum_ref[...]
        sends, recvs = [], []
        for j, (dx, dy) in enumerate(_OTHER_CHIPS):
            px, py = _flip(x, dx), _flip(y, dy)
            q_peer = 2 * px + py
            for a in range(n_big + 1):
                src = p_refs[a].at[q_peer] if a < n_big else csum_ref
                dst = out_refs[a] if a < n_big else land
                common = dict(send_sem=send_sems.at[a, j], recv_sem=recv_sems.at[a, j],
                              device_id=(px, py, c), device_id_type=pl.DeviceIdType.MESH)
                sends.append(pltpu.make_async_remote_copy(src_ref=src, dst_ref=dst.at[q_me], **common))
                recvs.append(pltpu.make_async_remote_copy(src_ref=src, dst_ref=dst.at[q_peer], **common))
        for cp in sends:
            cp.start()
        for cp in recvs:
            cp.wait_recv()
        for cp in sends:
            cp.wait_send()
        tot = land[0]
        for q in range(1, N_CHIP):
            tot = tot + land[q]
        tot_ref[...] = tot

    any_spec = pl.BlockSpec(memory_space=pl.ANY)
    vmem_spec = pl.BlockSpec(memory_space=pltpu.VMEM)
    return pl.pallas_call(
        body, name="grad_exchange_ici",
        out_shape=[jax.ShapeDtypeStruct(p.shape, p.dtype) for p in parts] + [jax.ShapeDtypeStruct(csum.shape, csum.dtype)],
        in_specs=[any_spec] * n_big + [vmem_spec], out_specs=[any_spec] * n_big + [vmem_spec],
        scratch_shapes=[pltpu.VMEM((N_CHIP,) + csum.shape, csum.dtype),
                        pltpu.SemaphoreType.DMA((n_big + 1, 3)), pltpu.SemaphoreType.DMA((n_big + 1, 3))],
    )(*parts, csum)


def final_sum(own, got, tr, name):
    rows, cols = own.shape

    def body(own_ref, got_ref, o_ref):
        q_me = 2 * lax.axis_index("x") + lax.axis_index("y")
        acc = None
        for q in range(N_CHIP):
            term = jnp.where(q == q_me, own_ref[...], got_ref[q].astype(F32))
            acc = term if acc is None else acc + term
        o_ref[...] = acc

    return pl.pallas_call(
        body, name=name, grid=(rows // tr,),
        in_specs=[pl.BlockSpec((tr, cols), lambda i: (i, 0)), pl.BlockSpec((N_CHIP, tr, cols), lambda i: (0, i, 0))],
        out_specs=pl.BlockSpec((tr, cols), lambda i: (i, 0)),
        out_shape=jax.ShapeDtypeStruct((rows, cols), F32),
        compiler_params=_params(("arbitrary",)),
    )(own, got)


def rms_fwd(x, g, tm, name):
    M, K = x.shape

    def body(x_ref, g_ref, o_ref):
        xv = x_ref[...]
        r = lax.rsqrt(jnp.mean(xv * xv, axis=-1, keepdims=True) + RMS_EPS)
        o_ref[...] = ((xv * r) * g_ref[...]).astype(BF16)

    return pl.pallas_call(
        body, name=name, grid=(M // tm,),
        in_specs=[pl.BlockSpec((tm, K), lambda i: (i, 0)), pl.BlockSpec((1, K), lambda i: (0, 0))],
        out_specs=pl.BlockSpec((tm, K), lambda i: (i, 0)),
        out_shape=jax.ShapeDtypeStruct((M, K), BF16),
        compiler_params=_params(("arbitrary",)),
    )(x, g)


def rms_bwd(x, g, dh, dres, tm, name):
    M, K = x.shape
    has_res = dres is not None

    def body(*refs):
        if has_res:
            x_ref, g_ref, dh_ref, dres_ref, dx_ref, dg_ref = refs
        else:
            x_ref, g_ref, dh_ref, dx_ref, dg_ref = refs
        xv = x_ref[...]
        r = lax.rsqrt(jnp.mean(xv * xv, axis=-1, keepdims=True) + RMS_EPS)
        xn = xv * r
        dhv = dh_ref[...]
        dxn = dhv * g_ref[...]
        dx = r * (dxn - xn * jnp.mean(dxn * xn, axis=-1, keepdims=True))
        if has_res:
            dx = dx + dres_ref[...]
        dx_ref[...] = dx
        part = jnp.sum(dhv * xn, axis=0, keepdims=True)
        row = lax.broadcasted_iota(jnp.int32, (8, K), 0)
        upd = jnp.where(row == 0, part, 0.0)

        @pl.when(pl.program_id(0) == 0)
        def _():
            dg_ref[...] = upd

        @pl.when(pl.program_id(0) != 0)
        def _():
            dg_ref[...] += upd

    row_spec = pl.BlockSpec((tm, K), lambda i: (i, 0))
    ins = [x, g, dh] + ([dres] if has_res else [])
    in_specs = [row_spec, pl.BlockSpec((1, K), lambda i: (0, 0)), row_spec] + ([row_spec] if has_res else [])
    return pl.pallas_call(
        body, name=name, grid=(M // tm,),
        in_specs=in_specs,
        out_specs=[row_spec, pl.BlockSpec((8, K), lambda i: (0, 0))],
        out_shape=[jax.ShapeDtypeStruct((M, K), F32), jax.ShapeDtypeStruct((8, K), F32)],
        compiler_params=_params(("arbitrary",)),
    )(*ins)


def mm_nn(a, b, tm, tn, name):
    M, K = a.shape
    N = b.shape[1]

    def body(a_ref, b_ref, o_ref):
        o_ref[...] = jnp.dot(a_ref[...], b_ref[...], preferred_element_type=F32)

    return pl.pallas_call(
        body, name=name, grid=(N // tn, M // tm),
        in_specs=[pl.BlockSpec((tm, K), lambda j, i: (i, 0)), pl.BlockSpec((K, tn), lambda j, i: (0, j))],
        out_specs=pl.BlockSpec((tm, tn), lambda j, i: (i, j)),
        out_shape=jax.ShapeDtypeStruct((M, N), F32),
        compiler_params=_params(("arbitrary", "arbitrary")),
    )(a, b)


def mm_nt(a, b, tm, tk, name):
    M, K = a.shape
    N = b.shape[0]

    def body(a_ref, b_ref, o_ref):
        part = lax.dot_general(a_ref[...], b_ref[...], (((1,), (1,)), ((), ())), preferred_element_type=F32)

        @pl.when(pl.program_id(1) == 0)
        def _():
            o_ref[...] = part

        @pl.when(pl.program_id(1) != 0)
        def _():
            o_ref[...] += part

    return pl.pallas_call(
        body, name=name, grid=(M // tm, K // tk),
        in_specs=[pl.BlockSpec((tm, tk), lambda i, k: (i, k)), pl.BlockSpec((N, tk), lambda i, k: (0, k))],
        out_specs=pl.BlockSpec((tm, N), lambda i, k: (i, 0)),
        out_shape=jax.ShapeDtypeStruct((M, N), F32),
        compiler_params=_params(("arbitrary", "arbitrary")),
    )(a, b)


def mm_tn(a, b, tt, tn, name):
    T, K = a.shape
    N = b.shape[1]

    def body(a_ref, b_ref, o_ref):
        part = lax.dot_general(a_ref[...], b_ref[...], (((0,), (0,)), ((), ())), preferred_element_type=F32)

        @pl.when(pl.program_id(1) == 0)
        def _():
            o_ref[...] = part

        @pl.when(pl.program_id(1) != 0)
        def _():
            o_ref[...] += part

    return pl.pallas_call(
        body, name=name, grid=(N // tn, T // tt),
        in_specs=[pl.BlockSpec((tt, K), lambda j, t: (t, 0)), pl.BlockSpec((tt, tn), lambda j, t: (t, j))],
        out_specs=pl.BlockSpec((K, tn), lambda j, t: (0, j)),
        out_shape=jax.ShapeDtypeStruct((K, N), F32),
        compiler_params=_params(("arbitrary", "arbitrary")),
    )(a, b)


def _log_sigmoid(z):
    return jnp.minimum(z, 0.0) - jnp.log(1.0 + jnp.exp(-jnp.abs(z)))


def _tri(n, lower):
    r = lax.broadcasted_iota(jnp.int32, (n, n), 0)
    c = lax.broadcasted_iota(jnp.int32, (n, n), 1)
    return jnp.where((r >= c) if lower else (r <= c), 1.0, 0.0).astype(F32)


def fox_gate(proj3, b_pad):
    B, S, _ = proj3.shape
    nblk = S // TK

    def body(f_ref, b_ref, o_ref):
        tri = _tri(TK, True)
        carry = jnp.zeros((1, LANES), F32)
        for n in range(nblk):
            z = f_ref[0, n * TK:(n + 1) * TK, :] + b_ref[...]
            logf = _log_sigmoid(z)
            cs = jnp.dot(tri, logf, preferred_element_type=F32, precision=lax.Precision.HIGHEST) + carry
            carry = cs[TK - 1:TK, :]
            o_ref[0, n * TK:(n + 1) * TK, :] = -cs

    return pl.pallas_call(
        body, name="fox_gate", grid=(B,),
        in_specs=[pl.BlockSpec((1, S, LANES), lambda b: (b, 0, P_FLOG // LANES)),
                  pl.BlockSpec((1, LANES), lambda b: (0, 0))],
        out_specs=pl.BlockSpec((1, S, LANES), lambda b: (b, 0, 0)),
        out_shape=jax.ShapeDtypeStruct((B, S, LANES), F32),
        compiler_params=_params(("arbitrary",)),
    )(proj3, b_pad)


def fox_gate_bwd(drow, dneg, proj3, b_pad):
    B, S, _ = proj3.shape
    nblk = S // TK

    def body(d_ref, r_ref, f_ref, b_ref, o_ref, db_ref):
        tri = _tri(TK, False)
        lane = lax.broadcasted_iota(jnp.int32, (TK, LANES), 1)
        er = lax.broadcasted_iota(jnp.int32, (FOX_W, LANES), 0)
        ec = lax.broadcasted_iota(jnp.int32, (FOX_W, LANES), 1)
        pick = jnp.where(er == LANES * (ec >> 1) + (ec & 1), 1.0, 0.0).astype(F32)
        carry = jnp.zeros((1, LANES), F32)
        dbsum = jnp.zeros((1, LANES), F32)
        for n in reversed(range(nblk)):
            dk_side = jnp.dot(r_ref[0, n * TK:(n + 1) * TK, :], pick, preferred_element_type=F32,
                              precision=lax.Precision.HIGHEST)
            dc = jnp.where(lane < FOX_HEADS, d_ref[0, :, n * TK:(n + 1) * TK].T - dk_side, 0.0)
            rs = jnp.dot(tri, dc, preferred_element_type=F32, precision=lax.Precision.HIGHEST) + carry
            carry = rs[0:1, :]
            z = f_ref[0, n * TK:(n + 1) * TK, :] + b_ref[...]
            dz = rs * (1.0 / (1.0 + jnp.exp(z)))
            o_ref[0, n * TK:(n + 1) * TK, :] = dz.astype(BF16)
            dbsum = dbsum + jnp.sum(dz, axis=0, keepdims=True)
        row = lax.broadcasted_iota(jnp.int32, (8, LANES), 0)
        upd = jnp.where(row == 0, dbsum, 0.0)

        @pl.when(pl.program_id(0) == 0)
        def _():
            db_ref[...] = upd

        @pl.when(pl.program_id(0) != 0)
        def _():
            db_ref[...] += upd

    return pl.pallas_call(
        body, name="fox_gate_bwd", grid=(B,),
        in_specs=[pl.BlockSpec((1, LANES, S), lambda b: (b, 0, 0)),
                  pl.BlockSpec((1, S, FOX_W), lambda b: (b, 0, 0)),
                  pl.BlockSpec((1, S, LANES), lambda b: (b, 0, P_FLOG // LANES)),
                  pl.BlockSpec((1, LANES), lambda b: (0, 0))],
        out_specs=[pl.BlockSpec((1, S, LANES), lambda b: (b, 0, 0)), pl.BlockSpec((8, LANES), lambda b: (0, 0))],
        out_shape=[jax.ShapeDtypeStruct((B, S, LANES), BF16), jax.ShapeDtypeStruct((8, LANES), F32)],
        compiler_params=_params(("arbitrary",)),
    )(drow, dneg, proj3, b_pad)


def _mult_masks(S, kind):
    nd = S // TQ
    a = np.arange(TQ)[:, None]
    b = np.arange(TK)[None, :]
    out = np.zeros((nd, TQ, TK), np.float32)
    for d in range(nd):
        delta = d * TQ + a - b
        if kind == "causal":
            out[d] = delta >= 0
        else:
            m = np.zeros((TQ, TK), np.float32)
            for w, dil in DILATIONS:
                m += (delta >= 0) & (delta % dil == 0) & (delta <= w)
            out[d] = m
    return jnp.asarray(out)


def _rope_tables(S):
    half = ROPE_DIM // 2
    pos = jnp.arange(S, dtype=F32)
    inv_freq = 1.0 / (ROPE_THETA ** (jnp.arange(0, ROPE_DIM, 2, dtype=F32) / ROPE_DIM))
    ang = pos[:, None] * inv_freq[None, :]
    cos, sin = jnp.cos(ang), jnp.sin(ang)
    one = jnp.ones((S, HEAD_DIM - ROPE_DIM), F32)
    zero = jnp.zeros((S, HEAD_DIM - ROPE_DIM), F32)
    zh = jnp.zeros((S, half), F32)
    c = jnp.concatenate([cos, cos, one], axis=1)
    s1 = jnp.concatenate([-sin, zh, zero], axis=1)
    s2 = jnp.concatenate([zh, sin, zero], axis=1)
    return tuple(jnp.concatenate([t, t], axis=1) for t in (c, s1, s2))


def _rope(t, c, s1, s2):
    return t * c + pltpu.roll(t, LANES - half_rope(), 1) * s1 + pltpu.roll(t, half_rope(), 1) * s2


def half_rope():
    return ROPE_DIM // 2


def _rope_bwd(d, c, s1, s2):
    return d * c + pltpu.roll(d * s1, half_rope(), 1) + pltpu.roll(d * s2, LANES - half_rope(), 1)


def _scale_parts(scale):
    m, _ = math.frexp(scale)
    return (scale, None) if m == 0.5 else (None, scale)


def attn_fwd(kind, src, S, *, negc=None, mask=None, rope=None, kv=None):
    B = src.shape[0]
    pair = kind != "mem"
    col0 = {"fox": P_FOX, "dil": P_DIL, "mem": P_MQ}[kind]
    n_blocks = FOX_HEADS // 2 if pair else MEM_HEADS
    e_dim = HEAD_DIM if pair else MEM_HEAD_DIM
    q_fold, s_scale = _scale_parts(1.0 / math.sqrt(e_dim))
    Sk = S if pair else MEM_LEN
    nh = 2 if pair else 1
    has_bias = negc is not None
    has_rope = rope is not None
    nq = S // TQ

    def body(*refs):
        refs = list(refs)
        if pair:
            qkv_ref = refs.pop(0)
        else:
            q_ref, k_ref, v_ref = refs.pop(0), refs.pop(0), refs.pop(0)
        negc_ref = refs.pop(0) if has_bias else None
        mask_ref = refs.pop(0) if pair else None
        rope_refs = [refs.pop(0) for _ in range(3)] if has_rope else None
        o_ref, lse_ref, qs, ks, vs = refs
        lane = lax.broadcasted_iota(jnp.int32, (1, LANES), 1)

        def prep_q(n, _):
            r0 = pl.multiple_of(n * TQ, TQ)
            rows = pl.ds(r0, TQ)
            q = qkv_ref[0, rows, 0:LANES] if pair else q_ref[0, rows, :]
            if has_rope:
                q = _rope(q, *[t[rows, :] for t in rope_refs])
            if q_fold is not None:
                q = q * q_fold
            qs[rows, :] = q.astype(BF16)
            return 0

        def prep_kv(n, _):
            r0 = pl.multiple_of(n * TK, TK)
            rows = pl.ds(r0, TK)
            k = qkv_ref[0, rows, LANES:2 * LANES] if pair else k_ref[0, rows, :]
            v = qkv_ref[0, rows, 2 * LANES:3 * LANES] if pair else v_ref[0, rows, :]
            if has_rope:
                k = _rope(k, *[t[rows, :] for t in rope_refs])
            ks[rows, :] = k.astype(BF16)
            vs[rows, :] = v.astype(BF16)
            return 0

        lax.fori_loop(0, nq, prep_q, 0)
        lax.fori_loop(0, Sk // TK, prep_kv, 0)

        def q_loop(i, _):
            r0 = pl.multiple_of(i * TQ, TQ)
            q = qs[pl.ds(r0, TQ), :]
            res = []
            for hh in range(nh):
                hmask = (lane >= HEAD_DIM * hh) & (lane < HEAD_DIM * (hh + 1))
                qh = jnp.where(hmask, q, jnp.zeros_like(q)) if pair else q

                def kv_loop(j, carry, qh=qh, hh=hh):
                    m, l, acc = carry
                    c0 = pl.multiple_of(j * TK, TK)
                    k = ks[pl.ds(c0, TK), :]
                    v = vs[pl.ds(c0, TK), :]
                    s = lax.dot_general(qh, k, (((1,), (1,)), ((), ())), preferred_element_type=F32)
                    if s_scale is not None:
                        s = s * s_scale
                    if has_bias:
                        s = s + negc_ref[0, 0, pl.ds(hh, 1), pl.ds(c0, TK)]
                    if pair:
                        mult = mask_ref[i - j]
                        s = jnp.where(mult > 0.0, s, NEG_INF)
                    m_new = jnp.maximum(m, jnp.max(s, axis=1, keepdims=True))
                    p = jnp.exp(s - m_new)
                    if pair:
                        p = p * mult
                    alpha = jnp.exp(m - m_new)
                    l = alpha * l + jnp.sum(p, axis=1, keepdims=True)
                    acc = acc * alpha + jnp.dot(p.astype(BF16), v, preferred_element_type=F32)
                    return m_new, l, acc

                init = (jnp.full((TQ, 1), NEG_INF, F32), jnp.zeros((TQ, 1), F32), jnp.zeros((TQ, LANES), F32))
                m, l, acc = lax.fori_loop(0, (i + 1) if pair else Sk // TK, kv_loop, init)
                res.append((acc / l, m + jnp.log(l)))
            if pair:
                o = jnp.where(lane < HEAD_DIM, res[0][0], res[1][0])
                lse = jnp.where(lane < HEAD_DIM, res[0][1], res[1][1])
            else:
                o = res[0][0]
                lse = jnp.broadcast_to(res[0][1], (TQ, LANES))
            o_ref[0, pl.ds(r0, TQ), :] = o
            lse_ref[0, pl.ds(r0, TQ), :] = lse
            return 0

        lax.fori_loop(0, nq, q_loop, 0)

    ins, in_specs = [], []
    if pair:
        ins.append(src)
        in_specs.append(pl.BlockSpec((1, S, PAIR_W), lambda b, h: (b, 0, col0 // PAIR_W + h)))
    else:
        ins += [src, kv, kv]
        in_specs += [pl.BlockSpec((1, S, LANES), lambda b, h: (b, 0, col0 // LANES + h)),
                     pl.BlockSpec((1, MEM_LEN, LANES), lambda b, h: (b, 0, h)),
                     pl.BlockSpec((1, MEM_LEN, LANES), lambda b, h: (b, 0, MEM_HEADS + h))]
    if has_bias:
        ins.append(negc)
        in_specs.append(pl.BlockSpec((1, 1, 2, S), lambda b, h: (b, h, 0, 0)))
    if pair:
        ins.append(mask)
        in_specs.append(pl.BlockSpec(mask.shape, lambda b, h: (0, 0, 0)))
    if has_rope:
        ins += list(rope)
        in_specs += [pl.BlockSpec((S, LANES), lambda b, h: (0, 0))] * 3
    W = n_blocks * LANES
    out_spec = pl.BlockSpec((1, S, LANES), lambda b, h: (b, 0, h))
    return pl.pallas_call(
        body, name=kind + "_attn_fwd", grid=(B, n_blocks),
        in_specs=in_specs, out_specs=[out_spec, out_spec],
        out_shape=[jax.ShapeDtypeStruct((B, S, W), F32)] * 2,
        scratch_shapes=[pltpu.VMEM((S, LANES), BF16), pltpu.VMEM((Sk, LANES), BF16), pltpu.VMEM((Sk, LANES), BF16)],
        compiler_params=_params(("arbitrary", "arbitrary")),
    )(*ins)


def attn_bwd(kind, src, do, o, lse, S, *, negc=None, mask=None, rope=None, kv=None):
    B = src.shape[0]
    pair = kind != "mem"
    col0 = {"fox": P_FOX, "dil": P_DIL, "mem": P_MQ}[kind]
    n_blocks = FOX_HEADS // 2 if pair else MEM_HEADS
    e_dim = HEAD_DIM if pair else MEM_HEAD_DIM
    scale = 1.0 / math.sqrt(e_dim)
    q_fold, s_scale = _scale_parts(scale)
    Sk = S if pair else MEM_LEN
    nh = 2 if pair else 1
    has_bias = negc is not None
    has_rope = rope is not None
    nq = S // TQ
    nk = Sk // TK

    def body(*refs):
        refs = list(refs)
        if pair:
            qkv_ref = refs.pop(0)
        else:
            q_ref, k_ref, v_ref = refs.pop(0), refs.pop(0), refs.pop(0)
        do_ref, o_ref, lse_ref = refs.pop(0), refs.pop(0), refs.pop(0)
        negc_ref = refs.pop(0) if has_bias else None
        mask_ref = refs.pop(0) if pair else None
        rope_refs = [refs.pop(0) for _ in range(3)] if has_rope else None
        if pair:
            dqkv_ref = refs.pop(0)
            dnegc_ref = refs.pop(0) if has_bias else None
            drow_ref = refs.pop(0) if has_bias else None
        else:
            dq_ref, dk_ref, dv_ref = refs.pop(0), refs.pop(0), refs.pop(0)
        qs, ks, vs, dos, delta_s, dq_acc = refs[:6]
        drow_acc = refs[6] if has_bias else None
        lane = lax.broadcasted_iota(jnp.int32, (1, LANES), 1)

        def prep_q(n, _):
            r0 = pl.multiple_of(n * TQ, TQ)
            rows = pl.ds(r0, TQ)
            q = qkv_ref[0, rows, 0:LANES] if pair else q_ref[0, rows, :]
            if has_rope:
                q = _rope(q, *[t[rows, :] for t in rope_refs])
            if q_fold is not None:
                q = q * q_fold
            qs[rows, :] = q.astype(BF16)
            dov = do_ref[0, rows, :]
            dob = dov.astype(BF16)
            dos[rows, :] = dob
            prod = dob.astype(F32) * o_ref[0, rows, :]
            if pair:
                d0 = jnp.sum(jnp.where(lane < HEAD_DIM, prod, 0.0), axis=1, keepdims=True)
                d1 = jnp.sum(jnp.where(lane < HEAD_DIM, 0.0, prod), axis=1, keepdims=True)
                delta_s[rows, :] = jnp.where(lane < HEAD_DIM, d0, d1)
            else:
                delta_s[rows, :] = jnp.broadcast_to(jnp.sum(prod, axis=1, keepdims=True), (TQ, LANES))
            dq_acc[rows, :] = jnp.zeros((TQ, LANES), F32)
            if has_bias:
                drow_acc[rows, :] = jnp.zeros((TQ, LANES), F32)
            return 0

        def prep_kv(n, _):
            r0 = pl.multiple_of(n * TK, TK)
            rows = pl.ds(r0, TK)
            k = qkv_ref[0, rows, LANES:2 * LANES] if pair else k_ref[0, rows, :]
            v = qkv_ref[0, rows, 2 * LANES:3 * LANES] if pair else v_ref[0, rows, :]
            if has_rope:
                k = _rope(k, *[t[rows, :] for t in rope_refs])
            ks[rows, :] = k.astype(BF16)
            vs[rows, :] = v.astype(BF16)
            return 0

        lax.fori_loop(0, nq, prep_q, 0)
        lax.fori_loop(0, nk, prep_kv, 0)

        def kv_loop(j, _):
            c0 = pl.multiple_of(j * TK, TK)
            kt = ks[pl.ds(c0, TK), :]
            vt = vs[pl.ds(c0, TK), :]
            res = []
            for hh in range(nh):
                hmask = (lane >= HEAD_DIM * hh) & (lane < HEAD_DIM * (hh + 1))
                kh = jnp.where(hmask, kt, jnp.zeros_like(kt)) if pair else kt
                vh = jnp.where(hmask, vt, jnp.zeros_like(vt)) if pair else vt

                def q_loop(i, carry, kh=kh, vh=vh, hh=hh, hmask=hmask):
                    dk, dv, dneg = carry
                    r0 = pl.multiple_of(i * TQ, TQ)
                    rows = pl.ds(r0, TQ)
                    q = qs[rows, :]
                    dot = dos[rows, :]
                    lse_i = lse_ref[0, rows, hh * HEAD_DIM:hh * HEAD_DIM + 1]
                    delta_i = delta_s[rows, hh * HEAD_DIM:hh * HEAD_DIM + 1]
                    s = lax.dot_general(q, kh, (((1,), (1,)), ((), ())), preferred_element_type=F32)
                    if s_scale is not None:
                        s = s * s_scale
                    if has_bias:
                        s = s + negc_ref[0, 0, pl.ds(hh, 1), pl.ds(c0, TK)]
                    if pair:
                        mult = mask_ref[i - j]
                        s = jnp.where(mult > 0.0, s, NEG_INF)
                    p = jnp.exp(s - lse_i)
                    if pair:
                        p = p * mult
                    dv = dv + lax.dot_general(p.astype(BF16), dot, (((0,), (0,)), ((), ())),
                                              preferred_element_type=F32)
                    dp = lax.dot_general(dot, vh, (((1,), (1,)), ((), ())), preferred_element_type=F32)
                    ds = p * (dp - delta_i)
                    if has_bias:
                        dneg = dneg + jnp.sum(ds, axis=0, keepdims=True)
                        drow_acc[rows, :] += jnp.where(hmask, jnp.sum(ds, axis=1, keepdims=True), 0.0)
                    if s_scale is not None:
                        ds = ds * s_scale
                    dsb = ds.astype(BF16)
                    dk = dk + lax.dot_general(dsb, q, (((0,), (0,)), ((), ())), preferred_element_type=F32)
                    dq = jnp.dot(dsb, kh, preferred_element_type=F32)
                    dq_acc[rows, :] += dq
                    return dk, dv, dneg

                init = (jnp.zeros((TK, LANES), F32), jnp.zeros((TK, LANES), F32), jnp.zeros((1, TK), F32))
                dk, dv, dneg = lax.fori_loop(j if pair else 0, nq, q_loop, init)
                if has_bias:
                    dnegc_ref[0, 0, pl.ds(hh, 1), pl.ds(c0, TK)] = dneg
                res.append((dk, dv))
            if pair:
                dk = jnp.where(lane < HEAD_DIM, res[0][0], res[1][0])
                dv = jnp.where(lane < HEAD_DIM, res[0][1], res[1][1])
                if has_rope:
                    dk = _rope_bwd(dk, *[t[pl.ds(c0, TK), :] for t in rope_refs])
                dqkv_ref[0, pl.ds(c0, TK), LANES:2 * LANES] = dk.astype(BF16)
                dqkv_ref[0, pl.ds(c0, TK), 2 * LANES:3 * LANES] = dv.astype(BF16)
            else:
                dk_ref[0, pl.ds(c0, TK), :] = res[0][0].astype(BF16)
                dv_ref[0, pl.ds(c0, TK), :] = res[0][1].astype(BF16)
            return 0

        lax.fori_loop(0, nk, kv_loop, 0)

        def fin_q(n, _):
            r0 = pl.multiple_of(n * TQ, TQ)
            rows = pl.ds(r0, TQ)
            dq = dq_acc[rows, :]
            if q_fold is not None:
                dq = dq * q_fold
            if has_rope:
                dq = _rope_bwd(dq, *[t[rows, :] for t in rope_refs])
            if pair:
                dqkv_ref[0, rows, 0:LANES] = dq.astype(BF16)
            else:
                dq_ref[0, rows, :] = dq.astype(BF16)
            if has_bias:
                drow_ref[0, rows, :] = drow_acc[rows, :]
            return 0

        lax.fori_loop(0, nq, fin_q, 0)

    ins, in_specs = [], []
    if pair:
        ins.append(src)
        in_specs.append(pl.BlockSpec((1, S, PAIR_W), lambda b, h: (b, 0, col0 // PAIR_W + h)))
    else:
        ins += [src, kv, kv]
        in_specs += [pl.BlockSpec((1, S, LANES), lambda b, h: (b, 0, col0 // LANES + h)),
                     pl.BlockSpec((1, MEM_LEN, LANES), lambda b, h: (b, 0, h)),
                     pl.BlockSpec((1, MEM_LEN, LANES), lambda b, h: (b, 0, MEM_HEADS + h))]
    row_spec = pl.BlockSpec((1, S, LANES), lambda b, h: (b, 0, h))
    ins += [do, o, lse]
    in_specs += [row_spec] * 3
    if has_bias:
        ins.append(negc)
        in_specs.append(pl.BlockSpec((1, 1, 2, S), lambda b, h: (b, h, 0, 0)))
    if pair:
        ins.append(mask)
        in_specs.append(pl.BlockSpec(mask.shape, lambda b, h: (0, 0, 0)))
    if has_rope:
        ins += list(rope)
        in_specs += [pl.BlockSpec((S, LANES), lambda b, h: (0, 0))] * 3
    W = n_blocks * LANES
    if pair:
        out_specs = [pl.BlockSpec((1, S, PAIR_W), lambda b, h: (b, 0, h))]
        out_shape = [jax.ShapeDtypeStruct((B, S, 3 * W), BF16)]
        if has_bias:
            out_specs.append(pl.BlockSpec((1, 1, 2, S), lambda b, h: (b, h, 0, 0)))
            out_shape.append(jax.ShapeDtypeStruct((B, LANES // 2, 2, S), F32))
            out_specs.append(row_spec)
            out_shape.append(jax.ShapeDtypeStruct((B, S, W), F32))
    else:
        kv_spec = pl.BlockSpec((1, MEM_LEN, LANES), lambda b, h: (b, 0, h))
        out_specs = [row_spec, kv_spec, kv_spec]
        out_shape = [jax.ShapeDtypeStruct((B, S, W), BF16)] + [jax.ShapeDtypeStruct((B, MEM_LEN, W), BF16)] * 2
    return pl.pallas_call(
        body, name=kind + "_attn_bwd", grid=(B, n_blocks),
        in_specs=in_specs, out_specs=out_specs, out_shape=out_shape,
        scratch_shapes=[pltpu.VMEM((S, LANES), BF16), pltpu.VMEM((Sk, LANES), BF16), pltpu.VMEM((Sk, LANES), BF16),
                        pltpu.VMEM((S, LANES), BF16), pltpu.VMEM((S, LANES), F32), pltpu.VMEM((S, LANES), F32)]
        + ([pltpu.VMEM((S, LANES), F32)] if has_bias else []),
        compiler_params=_params(("arbitrary", "arbitrary")),
    )(*ins)


def _log_masks(S, kind):
    nd = 1 if kind == "causal" else S // TQ
    a = np.arange(TQ)[:, None]
    b = np.arange(TK)[None, :]
    out = np.zeros((nd, TQ, TK), np.float32)
    for d in range(nd):
        delta = d * TQ + a - b
        if kind == "causal":
            m = (delta >= 0).astype(np.float64)
        else:
            m = sum(((delta >= 0) & (delta % dil == 0) & (delta <= w)).astype(np.float64) for w, dil in DILATIONS)
        out[d] = np.where(m > 0, np.log(np.maximum(m, 1.0)), NEG_INF)
    return jnp.asarray(out)


def _attn_setup(kind):
    pair = kind != "mem"
    e_dim = HEAD_DIM if pair else MEM_HEAD_DIM
    q_fold, s_scale = _scale_parts(1.0 / math.sqrt(e_dim))
    return dict(pair=pair, col0={"fox": P_FOX, "dil": P_DIL, "mem": P_MQ}[kind],
                n_blocks=FOX_HEADS // 2 if pair else MEM_HEADS, q_fold=q_fold, s_scale=s_scale,
                nh=2 if pair else 1)


def _attn_inputs(kind, src, S, negc, mask, rope, kv, extra):
    cfg = _attn_setup(kind)
    col0 = cfg["col0"]
    ins, in_specs = [], []
    if cfg["pair"]:
        ins.append(src)
        in_specs.append(pl.BlockSpec((1, S, PAIR_W), lambda b, h: (b, 0, col0 // PAIR_W + h)))
    else:
        ins += [src, kv, kv]
        in_specs += [pl.BlockSpec((1, S, LANES), lambda b, h: (b, 0, col0 // LANES + h)),
                     pl.BlockSpec((1, MEM_LEN, LANES), lambda b, h: (b, 0, h)),
                     pl.BlockSpec((1, MEM_LEN, LANES), lambda b, h: (b, 0, MEM_HEADS + h))]
    ins += list(extra)
    in_specs += [pl.BlockSpec((1, S, LANES), lambda b, h: (b, 0, h))] * len(extra)
    if negc is not None:
        ins.append(negc)
        in_specs.append(pl.BlockSpec((1, 1, 2, S), lambda b, h: (b, h, 0, 0)))
    if mask is not None:
        ins.append(mask)
        in_specs.append(pl.BlockSpec(mask.shape, lambda b, h: (0, 0, 0)))
    if rope is not None:
        ins += list(rope)
        in_specs += [pl.BlockSpec((S, LANES), lambda b, h: (0, 0))] * 3
    return ins, in_specs


def _prep_rows(cfg, rope_refs, lane, load_q, load_kv, qs2, ks, vs, S, Sk):
    nh = cfg["nh"]
    R = nh * TQ

    def prep_q(n, _):
        rows = pl.ds(pl.multiple_of(n * TQ, TQ), TQ)
        q = load_q(rows)
        if rope_refs is not None:
            q = _rope(q, *[t[rows, :] for t in rope_refs])
        if cfg["q_fold"] is not None:
            q = q * cfg["q_fold"]
        _store_stacked(cfg, lane, qs2, n, q.astype(BF16))
        return 0

    def prep_kv(n, _):
        rows = pl.ds(pl.multiple_of(n * TK, TK), TK)
        k, v = load_kv(rows)
        if rope_refs is not None:
            k = _rope(k, *[t[rows, :] for t in rope_refs])
        ks[rows, :] = k.astype(BF16)
        vs[rows, :] = v.astype(BF16)
        return 0

    lax.fori_loop(0, S // TQ, prep_q, 0)
    lax.fori_loop(0, Sk // TK, prep_kv, 0)


def _store_stacked(cfg, lane, dst, n, val):
    nh = cfg["nh"]
    R = nh * TQ
    if nh == 1:
        dst[pl.ds(pl.multiple_of(n * R, R), TQ), :] = val
        return
    for hh in range(nh):
        hmask = (lane >= HEAD_DIM * hh) & (lane < HEAD_DIM * (hh + 1))
        dst[pl.ds(pl.multiple_of(n * R + hh * TQ, TQ), TQ), :] = jnp.where(hmask, val, jnp.zeros_like(val))


def _cat(parts, axis):
    return parts[0] if len(parts) == 1 else jnp.concatenate(parts, axis=axis)


def attn_fwd2(kind, src, S, *, negc=None, mask=None, rope=None, kv=None):
    B = src.shape[0]
    cfg = _attn_setup(kind)
    pair, nh, s_scale = cfg["pair"], cfg["nh"], cfg["s_scale"]
    Sk = S if pair else MEM_LEN
    has_bias, has_rope = negc is not None, rope is not None
    R = nh * TQ

    def body(*refs):
        refs = list(refs)
        if pair:
            qkv_ref = refs.pop(0)
        else:
            q_ref, k_ref, v_ref = refs.pop(0), refs.pop(0), refs.pop(0)
        negc_ref = refs.pop(0) if has_bias else None
        mask_ref = refs.pop(0) if mask is not None else None
        rope_refs = [refs.pop(0) for _ in range(3)] if has_rope else None
        o_ref, lse_ref, qs2, ks, vs = refs
        lane = lax.broadcasted_iota(jnp.int32, (1, LANES), 1)

        if pair:
            load_q = lambda rows: qkv_ref[0, rows, 0:LANES]
            load_kv = lambda rows: (qkv_ref[0, rows, LANES:2 * LANES], qkv_ref[0, rows, 2 * LANES:3 * LANES])
        else:
            load_q = lambda rows: q_ref[0, rows, :]
            load_kv = lambda rows: (k_ref[0, rows, :], v_ref[0, rows, :])
        _prep_rows(cfg, rope_refs, lane, load_q, load_kv, qs2, ks, vs, S, Sk)

        def q_loop(i, _):
            q2 = qs2[pl.ds(pl.multiple_of(i * R, R), R), :]

            def step(j, carry, midx):
                ms, ls, acc = carry
                c0 = pl.multiple_of(j * TK, TK)
                k = ks[pl.ds(c0, TK), :]
                v = vs[pl.ds(c0, TK), :]
                s2 = lax.dot_general(q2, k, (((1,), (1,)), ((), ())), preferred_element_type=F32)
                if s_scale is not None:
                    s2 = s2 * s_scale
                new_m, new_l, ps, alphas = [], [], [], []
                for hh in range(nh):
                    s = s2[hh * TQ:(hh + 1) * TQ]
                    if has_bias:
                        s = s + negc_ref[0, 0, pl.ds(hh, 1), pl.ds(c0, TK)]
                    if midx is not None:
                        s = s + mask_ref[midx]
                    m_new = jnp.maximum(ms[hh], jnp.max(s, axis=1, keepdims=True))
                    p = jnp.exp(s - m_new)
                    alpha = jnp.exp(ms[hh] - m_new)
                    new_l.append(alpha * ls[hh] + jnp.sum(p, axis=1, keepdims=True))
                    new_m.append(m_new)
                    ps.append(p.astype(BF16))
                    alphas.append(alpha)
                acc = acc * _cat(alphas, 0) + jnp.dot(_cat(ps, 0), v, preferred_element_type=F32)
                return tuple(new_m), tuple(new_l), acc

            init = (tuple(jnp.full((TQ, 1), NEG_INF, F32) for _ in range(nh)),
                    tuple(jnp.zeros((TQ, 1), F32) for _ in range(nh)), jnp.zeros((R, LANES), F32))
            if kind == "fox":
                carry = lax.fori_loop(0, i, lambda j, c: step(j, c, None), init)
                carry = step(i, carry, 0)
            elif kind == "dil":
                carry = lax.fori_loop(0, i + 1, lambda j, c: step(j, c, i - j), init)
            else:
                carry = lax.fori_loop(0, Sk // TK, lambda j, c: step(j, c, None), init)
            ms, ls, acc = carry
            outs = [acc[hh * TQ:(hh + 1) * TQ] / ls[hh] for hh in range(nh)]
            lses = [ms[hh] + jnp.log(ls[hh]) for hh in range(nh)]
            rows = pl.ds(pl.multiple_of(i * TQ, TQ), TQ)
            if pair:
                o_ref[0, rows, :] = jnp.where(lane < HEAD_DIM, outs[0], outs[1])
                lse_ref[0, rows, :] = jnp.where(lane < HEAD_DIM, lses[0], lses[1])
            else:
                o_ref[0, rows, :] = outs[0]
                lse_ref[0, rows, :] = jnp.broadcast_to(lses[0], (TQ, LANES))
            return 0

        lax.fori_loop(0, S // TQ, q_loop, 0)

    ins, in_specs = _attn_inputs(kind, src, S, negc, mask, rope, kv, ())
    W = cfg["n_blocks"] * LANES
    out_spec = pl.BlockSpec((1, S, LANES), lambda b, h: (b, 0, h))
    return pl.pallas_call(
        body, name=kind + "_attn_fwd", grid=(B, cfg["n_blocks"]),
        in_specs=in_specs, out_specs=[out_spec, out_spec],
        out_shape=[jax.ShapeDtypeStruct((B, S, W), F32)] * 2,
        scratch_shapes=[pltpu.VMEM((nh * S, LANES), BF16), pltpu.VMEM((Sk, LANES), BF16),
                        pltpu.VMEM((Sk, LANES), BF16)],
        compiler_params=_params(("arbitrary", "arbitrary")),
    )(*ins)


def attn_bwd2(kind, src, do, o, lse, S, *, negc=None, mask=None, rope=None, kv=None):
    B = src.shape[0]
    cfg = _attn_setup(kind)
    pair, nh, s_scale, q_fold = cfg["pair"], cfg["nh"], cfg["s_scale"], cfg["q_fold"]
    Sk = S if pair else MEM_LEN
    has_bias, has_rope = negc is not None, rope is not None
    R = nh * TQ
    nq, nk = S // TQ, Sk // TK

    def body(*refs):
        refs = list(refs)
        if pair:
            qkv_ref = refs.pop(0)
        else:
            q_ref, k_ref, v_ref = refs.pop(0), refs.pop(0), refs.pop(0)
        do_ref, o_ref, lse_ref = refs.pop(0), refs.pop(0), refs.pop(0)
        negc_ref = refs.pop(0) if has_bias else None
        mask_ref = refs.pop(0) if mask is not None else None
        rope_refs = [refs.pop(0) for _ in range(3)] if has_rope else None
        if pair:
            dqkv_ref = refs.pop(0)
            dnegc_ref = refs.pop(0) if has_bias else None
            drow_ref = refs.pop(0) if has_bias else None
        else:
            dq_ref, dk_ref, dv_ref = refs.pop(0), refs.pop(0), refs.pop(0)
        qs2, ks, vs, dos2, lse_s, delta_s, dk_acc, dv_acc = refs[:8]
        dneg_acc = refs[8] if has_bias else None
        lane = lax.broadcasted_iota(jnp.int32, (1, LANES), 1)

        if pair:
            load_q = lambda rows: qkv_ref[0, rows, 0:LANES]
            load_kv = lambda rows: (qkv_ref[0, rows, LANES:2 * LANES], qkv_ref[0, rows, 2 * LANES:3 * LANES])
        else:
            load_q = lambda rows: q_ref[0, rows, :]
            load_kv = lambda rows: (k_ref[0, rows, :], v_ref[0, rows, :])
        _prep_rows(cfg, rope_refs, lane, load_q, load_kv, qs2, ks, vs, S, Sk)

        def prep_do(n, _):
            rows = pl.ds(pl.multiple_of(n * TQ, TQ), TQ)
            dob = do_ref[0, rows, :].astype(BF16)
            _store_stacked(cfg, lane, dos2, n, dob)
            prod = dob.astype(F32) * o_ref[0, rows, :]
            lse_blk = lse_ref[0, rows, :]
            for hh in range(nh):
                dst = pl.ds(pl.multiple_of(n * R + hh * TQ, TQ), TQ)
                if pair:
                    hmask = (lane >= HEAD_DIM * hh) & (lane < HEAD_DIM * (hh + 1))
                    d = jnp.sum(jnp.where(hmask, prod, 0.0), axis=1, keepdims=True)
                    lse_s[dst, :] = jnp.broadcast_to(lse_blk[:, hh * HEAD_DIM:hh * HEAD_DIM + 1], (TQ, LANES))
                else:
                    d = jnp.sum(prod, axis=1, keepdims=True)
                    lse_s[dst, :] = lse_blk
                delta_s[dst, :] = jnp.broadcast_to(d, (TQ, LANES))
            return 0

        def zero_kv(n, _):
            rows = pl.ds(pl.multiple_of(n * TK, TK), TK)
            dk_acc[rows, :] = jnp.zeros((TK, LANES), F32)
            dv_acc[rows, :] = jnp.zeros((TK, LANES), F32)
            return 0

        lax.fori_loop(0, nq, prep_do, 0)
        lax.fori_loop(0, nk, zero_kv, 0)
        if has_bias:
            dneg_acc[...] = jnp.zeros(dneg_acc.shape, F32)

        def q_loop(i, _):
            rows2 = pl.ds(pl.multiple_of(i * R, R), R)
            q2 = qs2[rows2, :]
            do2 = dos2[rows2, :]
            lse2 = lse_s[rows2, :]
            delta2 = delta_s[rows2, :]
            wide = lambda t: jnp.concatenate([t] * (TK // LANES), axis=1)

            def step(j, carry, midx):
                dq2, drow = carry
                c0 = pl.multiple_of(j * TK, TK)
                kcols = pl.ds(c0, TK)
                k = ks[kcols, :]
                v = vs[kcols, :]
                s2 = lax.dot_general(q2, k, (((1,), (1,)), ((), ())), preferred_element_type=F32)
                if s_scale is not None:
                    s2 = s2 * s_scale
                if has_bias or midx is not None:
                    halves = []
                    for hh in range(nh):
                        s = s2[hh * TQ:(hh + 1) * TQ]
                        if has_bias:
                            s = s + negc_ref[0, 0, pl.ds(hh, 1), kcols]
                        if midx is not None:
                            s = s + mask_ref[midx]
                        halves.append(s)
                    s2 = _cat(halves, 0)
                p2 = jnp.exp(s2 - wide(lse2))
                dp2 = lax.dot_general(do2, v, (((1,), (1,)), ((), ())), preferred_element_type=F32)
                ds2 = p2 * (dp2 - wide(delta2))
                if has_bias:
                    drow = drow + jnp.sum(ds2, axis=1, keepdims=True)
                    for hh in range(nh):
                        dneg_acc[pl.ds(hh, 1), kcols] += jnp.sum(ds2[hh * TQ:(hh + 1) * TQ], axis=0, keepdims=True)
                if s_scale is not None:
                    ds2 = ds2 * s_scale
                dsb = ds2.astype(BF16)
                dv_acc[kcols, :] += lax.dot_general(p2.astype(BF16), do2, (((0,), (0,)), ((), ())),
                                                    preferred_element_type=F32)
                dk_acc[kcols, :] += lax.dot_general(dsb, q2, (((0,), (0,)), ((), ())), preferred_element_type=F32)
                dq2 = dq2 + jnp.dot(dsb, k, preferred_element_type=F32)
                return dq2, drow

            init = (jnp.zeros((R, LANES), F32), jnp.zeros((R, 1), F32))
            if kind == "fox":
                carry = lax.fori_loop(0, i, lambda j, c: step(j, c, None), init)
                carry = step(i, carry, 0)
            elif kind == "dil":
                carry = lax.fori_loop(0, i + 1, lambda j, c: step(j, c, i - j), init)
            else:
                carry = lax.fori_loop(0, nk, lambda j, c: step(j, c, None), init)
            dq2, drow = carry
            rows = pl.ds(pl.multiple_of(i * TQ, TQ), TQ)
            dq = jnp.where(lane < HEAD_DIM, dq2[0:TQ], dq2[TQ:2 * TQ]) if pair else dq2
            if q_fold is not None:
                dq = dq * q_fold
            if has_rope:
                dq = _rope_bwd(dq, *[t[rows, :] for t in rope_refs])
            if pair:
                dqkv_ref[0, rows, 0:LANES] = dq.astype(BF16)
            else:
                dq_ref[0, rows, :] = dq.astype(BF16)
            if has_bias:
                drow_ref[0, rows, :] = jnp.where(lane < HEAD_DIM, drow[0:TQ], drow[TQ:2 * TQ])
            return 0

        lax.fori_loop(0, nq, q_loop, 0)

        def fin_kv(n, _):
            rows = pl.ds(pl.multiple_of(n * TK, TK), TK)
            dk = dk_acc[rows, :]
            if has_rope:
                dk = _rope_bwd(dk, *[t[rows, :] for t in rope_refs])
            if pair:
                dqkv_ref[0, rows, LANES:2 * LANES] = dk.astype(BF16)
                dqkv_ref[0, rows, 2 * LANES:3 * LANES] = dv_acc[rows, :].astype(BF16)
            else:
                dk_ref[0, rows, :] = dk.astype(BF16)
                dv_ref[0, rows, :] = dv_acc[rows, :].astype(BF16)
            return 0

        lax.fori_loop(0, nk, fin_kv, 0)
        if has_bias:
            dnegc_ref[0, 0] = dneg_acc[...]

    ins, in_specs = _attn_inputs(kind, src, S, negc, mask, rope, kv, (do, o, lse))
    W = cfg["n_blocks"] * LANES
    row_spec = pl.BlockSpec((1, S, LANES), lambda b, h: (b, 0, h))
    if pair:
        out_specs = [pl.BlockSpec((1, S, PAIR_W), lambda b, h: (b, 0, h))]
        out_shape = [jax.ShapeDtypeStruct((B, S, 3 * W), BF16)]
        if has_bias:
            out_specs += [pl.BlockSpec((1, 1, 2, S), lambda b, h: (b, h, 0, 0)), row_spec]
            out_shape += [jax.ShapeDtypeStruct((B, LANES // 2, 2, S), F32), jax.ShapeDtypeStruct((B, S, W), F32)]
    else:
        kv_spec = pl.BlockSpec((1, MEM_LEN, LANES), lambda b, h: (b, 0, h))
        out_specs = [row_spec, kv_spec, kv_spec]
        out_shape = [jax.ShapeDtypeStruct((B, S, W), BF16)] + [jax.ShapeDtypeStruct((B, MEM_LEN, W), BF16)] * 2
    scratch = [pltpu.VMEM((nh * S, LANES), BF16), pltpu.VMEM((Sk, LANES), BF16), pltpu.VMEM((Sk, LANES), BF16),
               pltpu.VMEM((nh * S, LANES), BF16), pltpu.VMEM((nh * S, LANES), F32), pltpu.VMEM((nh * S, LANES), F32),
               pltpu.VMEM((Sk, LANES), F32), pltpu.VMEM((Sk, LANES), F32)]
    if has_bias:
        scratch.append(pltpu.VMEM((2, S), F32))
    return pl.pallas_call(
        body, name=kind + "_attn_bwd", grid=(B, cfg["n_blocks"]),
        in_specs=in_specs, out_specs=out_specs, out_shape=out_shape, scratch_shapes=scratch,
        compiler_params=_params(("arbitrary", "arbitrary")),
    )(*ins)


def _log_masks_t(S, kind):
    return jnp.swapaxes(_log_masks(S, kind), 1, 2)


def _head_rows(hh, pair):
    row = lax.broadcasted_iota(jnp.int32, (LANES, 1), 0)
    if not pair:
        return row >= 0
    return (row >= HEAD_DIM * hh) & (row < HEAD_DIM * (hh + 1))


def _attn_t_inputs(kind, src, S, negc_cols, mask, rope, kv):
    cfg = _attn_setup(kind)
    col0 = cfg["col0"]
    ins, in_specs = [], []
    if cfg["pair"]:
        ins.append(src)
        in_specs.append(pl.BlockSpec((1, S, PAIR_W), lambda b, h: (b, 0, col0 // PAIR_W + h)))
    else:
        ins += [src, kv, kv]
        in_specs += [pl.BlockSpec((1, S, LANES), lambda b, h: (b, 0, col0 // LANES + h)),
                     pl.BlockSpec((1, MEM_LEN, LANES), lambda b, h: (b, 0, h)),
                     pl.BlockSpec((1, MEM_LEN, LANES), lambda b, h: (b, 0, MEM_HEADS + h))]
    if negc_cols is not None:
        ins.append(negc_cols)
        in_specs.append(pl.BlockSpec((1, S, LANES), lambda b, h: (b, 0, 0)))
    if mask is not None:
        ins.append(mask)
        in_specs.append(pl.BlockSpec(mask.shape, lambda b, h: (0, 0, 0)))
    if rope is not None:
        ins += list(rope)
        in_specs += [pl.BlockSpec((S, LANES), lambda b, h: (0, 0))] * 3
    return ins, in_specs


def _attn_t_prep(cfg, refs, S, Sk, *, qT2s, ks, q2s=None, vs=None, vTs=None, kTs=None, nb=None):
    pair, nh = cfg["pair"], cfg["nh"]
    lane = lax.broadcasted_iota(jnp.int32, (1, LANES), 1)
    rope_refs = refs["rope"]

    def prep_q(n, _):
        rows = pl.ds(pl.multiple_of(n * TQ, TQ), TQ)
        q = refs["load_q"](rows)
        if rope_refs is not None:
            q = _rope(q, *[t[rows, :] for t in rope_refs])
        if cfg["q_fold"] is not None:
            q = q * cfg["q_fold"]
        qb = q.astype(BF16)
        if q2s is not None:
            _store_stacked(cfg, lane, q2s, n, qb)
        qtb = qb.astype(F32).T.astype(BF16)
        for hh in range(nh):
            qT2s[n, :, hh * TQ:(hh + 1) * TQ] = jnp.where(_head_rows(hh, pair), qtb, jnp.zeros_like(qtb))
        return 0

    def prep_kv(n, _):
        rows = pl.ds(pl.multiple_of(n * TK, TK), TK)
        k, v = refs["load_kv"](rows)
        if rope_refs is not None:
            k = _rope(k, *[t[rows, :] for t in rope_refs])
        kb = k.astype(BF16)
        vb = v.astype(BF16)
        ks[rows, :] = kb
        if vs is not None:
            vs[rows, :] = vb
        if vTs is not None:
            vTs[n] = vb.astype(F32).T.astype(BF16)
        if kTs is not None:
            kTs[n] = kb.astype(F32).T.astype(BF16)
        if nb is not None:
            blk = refs["negc"][0, rows, :]
            for hh in range(nh):
                h = 2 * refs["block"] + hh
                col = jnp.sum(jnp.where(lane == h, blk, 0.0), axis=1, keepdims=True)
                nb[hh, rows, :] = jnp.broadcast_to(col, (TK, LANES))
        return 0

    lax.fori_loop(0, S // TQ, prep_q, 0)
    lax.fori_loop(0, Sk // TK, prep_kv, 0)


def _scores_t(cfg, kind, k, qT2, nb, mask_ref, kc, midx):
    nh, s_scale = cfg["nh"], cfg["s_scale"]
    sT = jnp.dot(k, qT2, preferred_element_type=F32)
    if s_scale is not None:
        sT = sT * s_scale
    if nb is None and midx is None:
        return sT
    parts = []
    for hh in range(nh):
        t = sT[:, hh * TQ:(hh + 1) * TQ]
        if nb is not None:
            t = t + jnp.concatenate([nb[hh, kc, :]] * (TQ // LANES), axis=1)
        if midx is not None:
            t = t + mask_ref[midx]
        parts.append(t)
    return _cat(parts, 1)


def _kv_sweep(kind, i, nk, step, init):
    if kind == "fox":
        carry = lax.fori_loop(0, i, lambda j, c: step(j, c, None), init)
        return step(i, carry, 0)
    if kind == "dil":
        return lax.fori_loop(0, i + 1, lambda j, c: step(j, c, i - j), init)
    return lax.fori_loop(0, nk, lambda j, c: step(j, c, None), init)


def attn_fwd3(kind, src, S, *, negc_cols=None, mask=None, rope=None, kv=None):
    B = src.shape[0]
    cfg = _attn_setup(kind)
    pair, nh = cfg["pair"], cfg["nh"]
    Sk = S if pair else MEM_LEN
    has_bias, has_rope = negc_cols is not None, rope is not None
    R = nh * TQ
    nq, nk = S // TQ, Sk // TK

    def body(*refs):
        refs = list(refs)
        if pair:
            qkv_ref = refs.pop(0)
            load_q = lambda rows: qkv_ref[0, rows, 0:LANES]
            load_kv = lambda rows: (qkv_ref[0, rows, LANES:2 * LANES], qkv_ref[0, rows, 2 * LANES:3 * LANES])
        else:
            q_ref, k_ref, v_ref = refs.pop(0), refs.pop(0), refs.pop(0)
            load_q = lambda rows: q_ref[0, rows, :]
            load_kv = lambda rows: (k_ref[0, rows, :], v_ref[0, rows, :])
        negc_ref = refs.pop(0) if has_bias else None
        mask_ref = refs.pop(0) if mask is not None else None
        rope_refs = [refs.pop(0) for _ in range(3)] if has_rope else None
        o_ref, lse_ref, qT2s, ks, vTs = refs[:5]
        nb = refs[5] if has_bias else None
        _attn_t_prep(cfg, dict(load_q=load_q, load_kv=load_kv, rope=rope_refs, negc=negc_ref,
                               block=pl.program_id(1)), S, Sk,
                     qT2s=qT2s, ks=ks, vTs=vTs, nb=nb)

        def q_loop(i, _):
            qT2 = qT2s[i]

            def step(j, carry, midx):
                m, l, accT = carry
                kc = pl.ds(pl.multiple_of(j * TK, TK), TK)
                sT = _scores_t(cfg, kind, ks[kc, :], qT2, nb, mask_ref, kc, midx)
                m_new = jnp.maximum(m, jnp.max(sT, axis=0, keepdims=True))
                p = jnp.exp(sT - m_new)
                alpha = jnp.exp(m - m_new)
                l = alpha * l + jnp.sum(p, axis=0, keepdims=True)
                accT = accT * alpha + jnp.dot(vTs[j], p.astype(BF16), preferred_element_type=F32)
                return m_new, l, accT

            init = (jnp.full((1, R), NEG_INF, F32), jnp.zeros((1, R), F32), jnp.zeros((LANES, R), F32))
            m, l, accT = _kv_sweep(kind, i, nk, step, init)
            oT2 = accT / l
            oT = jnp.where(_head_rows(0, True), oT2[:, 0:TQ], oT2[:, TQ:2 * TQ]) if pair else oT2
            o_ref[0, pl.ds(pl.multiple_of(i * TQ, TQ), TQ), :] = oT.T
            lse_ref[0, 0, pl.ds(i, 1), :] = m + jnp.log(l)
            return 0

        lax.fori_loop(0, nq, q_loop, 0)

    ins, in_specs = _attn_t_inputs(kind, src, S, negc_cols, mask, rope, kv)
    W = cfg["n_blocks"] * LANES
    scratch = [pltpu.VMEM((nq, LANES, R), BF16), pltpu.VMEM((Sk, LANES), BF16), pltpu.VMEM((nk, LANES, TK), BF16)]
    if has_bias:
        scratch.append(pltpu.VMEM((nh, Sk, LANES), F32))
    return pl.pallas_call(
        body, name=kind + "_attn_fwd", grid=(B, cfg["n_blocks"]),
        in_specs=in_specs,
        out_specs=[pl.BlockSpec((1, S, LANES), lambda b, h: (b, 0, h)),
                   pl.BlockSpec((1, 1, nq, R), lambda b, h: (b, h, 0, 0))],
        out_shape=[jax.ShapeDtypeStruct((B, S, W), F32), jax.ShapeDtypeStruct((B, cfg["n_blocks"], nq, R), F32)],
        scratch_shapes=scratch,
        compiler_params=_params(("arbitrary", "arbitrary")),
    )(*ins)


def attn_bwd3(kind, src, do, o, lse, S, *, negc_cols=None, mask=None, rope=None, kv=None):
    B = src.shape[0]
    cfg = _attn_setup(kind)
    pair, nh, s_scale, q_fold = cfg["pair"], cfg["nh"], cfg["s_scale"], cfg["q_fold"]
    Sk = S if pair else MEM_LEN
    has_bias, has_rope = negc_cols is not None, rope is not None
    R = nh * TQ
    nq, nk = S // TQ, Sk // TK

    def body(*refs):
        refs = list(refs)
        if pair:
            qkv_ref = refs.pop(0)
            load_q = lambda rows: qkv_ref[0, rows, 0:LANES]
            load_kv = lambda rows: (qkv_ref[0, rows, LANES:2 * LANES], qkv_ref[0, rows, 2 * LANES:3 * LANES])
        else:
            q_ref, k_ref, v_ref = refs.pop(0), refs.pop(0), refs.pop(0)
            load_q = lambda rows: q_ref[0, rows, :]
            load_kv = lambda rows: (k_ref[0, rows, :], v_ref[0, rows, :])
        negc_ref = refs.pop(0) if has_bias else None
        mask_ref = refs.pop(0) if mask is not None else None
        rope_refs = [refs.pop(0) for _ in range(3)] if has_rope else None
        do_ref, o_ref, lse_ref = refs.pop(0), refs.pop(0), refs.pop(0)
        if pair:
            dqkv_ref = refs.pop(0)
            dneg_ref = refs.pop(0) if has_bias else None
            drow_ref = refs.pop(0) if has_bias else None
        else:
            dq_ref, dk_ref, dv_ref = refs.pop(0), refs.pop(0), refs.pop(0)
        qT2s, ks, q2s, vs, kTs, doT2s, do2s, delta_s, dk_acc, dv_acc = refs[:10]
        nb, dneg_acc = (refs[10], refs[11]) if has_bias else (None, None)
        lane = lax.broadcasted_iota(jnp.int32, (1, LANES), 1)
        _attn_t_prep(cfg, dict(load_q=load_q, load_kv=load_kv, rope=rope_refs, negc=negc_ref,
                               block=pl.program_id(1)), S, Sk,
                     qT2s=qT2s, ks=ks, q2s=q2s, vs=vs, kTs=kTs, nb=nb)

        def prep_do(n, _):
            rows = pl.ds(pl.multiple_of(n * TQ, TQ), TQ)
            dob = do_ref[0, rows, :].astype(BF16)
            _store_stacked(cfg, lane, do2s, n, dob)
            doT = dob.astype(F32).T
            prodT = doT * o_ref[0, rows, :].T
            doTb = doT.astype(BF16)
            for hh in range(nh):
                hm = _head_rows(hh, pair)
                doT2s[n, :, hh * TQ:(hh + 1) * TQ] = jnp.where(hm, doTb, jnp.zeros_like(doTb))
                delta_s[pl.ds(n, 1), hh * TQ:(hh + 1) * TQ] = jnp.sum(jnp.where(hm, prodT, 0.0), axis=0, keepdims=True)
            return 0

        def zero_kv(n, _):
            rows = pl.ds(pl.multiple_of(n * TK, TK), TK)
            dk_acc[rows, :] = jnp.zeros((TK, LANES), F32)
            dv_acc[rows, :] = jnp.zeros((TK, LANES), F32)
            if has_bias:
                for hh in range(nh):
                    dneg_acc[hh, rows, :] = jnp.zeros((TK, LANES), F32)
            return 0

        lax.fori_loop(0, nq, prep_do, 0)
        lax.fori_loop(0, nk, zero_kv, 0)

        def q_loop(i, _):
            rows2 = pl.ds(pl.multiple_of(i * R, R), R)
            qT2 = qT2s[i]
            doT2 = doT2s[i]
            q2 = q2s[rows2, :]
            do2 = do2s[rows2, :]
            lse_i = lse_ref[0, 0, pl.ds(i, 1), :]
            delta_i = delta_s[pl.ds(i, 1), :]

            def step(j, carry, midx):
                dqT2, drow = carry
                kc = pl.ds(pl.multiple_of(j * TK, TK), TK)
                sT = _scores_t(cfg, kind, ks[kc, :], qT2, nb, mask_ref, kc, midx)
                pT = jnp.exp(sT - lse_i)
                dpT = jnp.dot(vs[kc, :], doT2, preferred_element_type=F32)
                dsT = pT * (dpT - delta_i)
                if has_bias:
                    drow = drow + jnp.sum(dsT, axis=0, keepdims=True)
                    for hh in range(nh):
                        part = dsT[:, hh * TQ:hh * TQ + LANES]
                        for t in range(1, TQ // LANES):
                            part = part + dsT[:, hh * TQ + t * LANES:hh * TQ + (t + 1) * LANES]
                        dneg_acc[hh, kc, :] += part
                if s_scale is not None:
                    dsT = dsT * s_scale
                dsb = dsT.astype(BF16)
                dv_acc[kc, :] += jnp.dot(pT.astype(BF16), do2, preferred_element_type=F32)
                dk_acc[kc, :] += jnp.dot(dsb, q2, preferred_element_type=F32)
                dqT2 = dqT2 + jnp.dot(kTs[j], dsb, preferred_element_type=F32)
                return dqT2, drow

            init = (jnp.zeros((LANES, R), F32), jnp.zeros((1, R), F32))
            dqT2, drow = _kv_sweep(kind, i, nk, step, init)
            rows = pl.ds(pl.multiple_of(i * TQ, TQ), TQ)
            dqT = jnp.where(_head_rows(0, True), dqT2[:, 0:TQ], dqT2[:, TQ:2 * TQ]) if pair else dqT2
            dq = dqT.T
            if q_fold is not None:
                dq = dq * q_fold
            if has_rope:
                dq = _rope_bwd(dq, *[t[rows, :] for t in rope_refs])
            if pair:
                dqkv_ref[0, rows, 0:LANES] = dq.astype(BF16)
            else:
                dq_ref[0, rows, :] = dq.astype(BF16)
            if has_bias:
                drow_ref[0, 0, pl.ds(i, 1), :] = drow
            return 0

        lax.fori_loop(0, nq, q_loop, 0)

        def fin_kv(n, _):
            rows = pl.ds(pl.multiple_of(n * TK, TK), TK)
            dk = dk_acc[rows, :]
            if has_rope:
                dk = _rope_bwd(dk, *[t[rows, :] for t in rope_refs])
            if pair:
                dqkv_ref[0, rows, LANES:2 * LANES] = dk.astype(BF16)
                dqkv_ref[0, rows, 2 * LANES:3 * LANES] = dv_acc[rows, :].astype(BF16)
            else:
                dk_ref[0, rows, :] = dk.astype(BF16)
                dv_ref[0, rows, :] = dv_acc[rows, :].astype(BF16)
            if has_bias:
                x0 = jnp.sum(dneg_acc[0, rows, :], axis=1, keepdims=True)
                x1 = jnp.sum(dneg_acc[1, rows, :], axis=1, keepdims=True)
                dneg_ref[0, rows, :] = jnp.where(lane == 0, x0, jnp.where(lane == 1, x1, 0.0))
            return 0

        lax.fori_loop(0, nk, fin_kv, 0)

    ins, in_specs = _attn_t_inputs(kind, src, S, negc_cols, mask, rope, kv)
    row_spec = pl.BlockSpec((1, S, LANES), lambda b, h: (b, 0, h))
    vec_spec = pl.BlockSpec((1, 1, nq, R), lambda b, h: (b, h, 0, 0))
    ins += [do, o, lse]
    in_specs += [row_spec, row_spec, vec_spec]
    W = cfg["n_blocks"] * LANES
    if pair:
        out_specs = [pl.BlockSpec((1, S, PAIR_W), lambda b, h: (b, 0, h))]
        out_shape = [jax.ShapeDtypeStruct((B, S, 3 * W), BF16)]
        if has_bias:
            out_specs += [row_spec, vec_spec]
            out_shape += [jax.ShapeDtypeStruct((B, S, W), F32), jax.ShapeDtypeStruct((B, cfg["n_blocks"], nq, R), F32)]
    else:
        kv_spec = pl.BlockSpec((1, MEM_LEN, LANES), lambda b, h: (b, 0, h))
        out_specs = [row_spec, kv_spec, kv_spec]
        out_shape = [jax.ShapeDtypeStruct((B, S, W), BF16)] + [jax.ShapeDtypeStruct((B, MEM_LEN, W), BF16)] * 2
    scratch = [pltpu.VMEM((nq, LANES, R), BF16), pltpu.VMEM((Sk, LANES), BF16), pltpu.VMEM((nh * S, LANES), BF16),
               pltpu.VMEM((Sk, LANES), BF16), pltpu.VMEM((nk, LANES, TK), BF16), pltpu.VMEM((nq, LANES, R), BF16),
               pltpu.VMEM((nh * S, LANES), BF16), pltpu.VMEM((nq, R), F32),
               pltpu.VMEM((Sk, LANES), F32), pltpu.VMEM((Sk, LANES), F32)]
    if has_bias:
        scratch += [pltpu.VMEM((nh, Sk, LANES), F32), pltpu.VMEM((nh, Sk, LANES), F32)]
    return pl.pallas_call(
        body, name=kind + "_attn_bwd", grid=(B, cfg["n_blocks"]),
        in_specs=in_specs, out_specs=out_specs, out_shape=out_shape, scratch_shapes=scratch,
        compiler_params=_params(("arbitrary", "arbitrary")),
    )(*ins)


def _sigmoid(g):
    return 1.0 / (1.0 + jnp.exp(-g))


def out_fwd(proj, o_fox, o_dil, o_mem, w_out, x, target, gf, tm):
    T = x.shape[0]

    def body(fg_ref, dg_ref, mg_ref, of_ref, od_ref, om_ref, w_ref, x_ref, t_ref, gf_ref,
             y_ref, dx_ref, dxb_ref, sm_ref):
        parts = []
        for g_ref, o_ref in ((fg_ref, of_ref), (dg_ref, od_ref), (mg_ref, om_ref)):
            g = g_ref[...]
            parts.append((o_ref[...] * (g * _sigmoid(g))).astype(BF16))
        ymix = jnp.concatenate(parts, axis=1)
        y_ref[...] = ymix
        x2 = x_ref[...] + jnp.dot(ymix, w_ref[...], preferred_element_type=F32)
        r = lax.rsqrt(jnp.mean(x2 * x2, axis=-1, keepdims=True) + RMS_EPS)
        yn = x2 * r
        err = yn * gf_ref[...] - t_ref[...]
        loss = 0.5 * jnp.sum(jnp.sum(err * err, axis=-1, keepdims=True) / D_MODEL, axis=0, keepdims=True)
        dyf = err / D_MODEL
        dgf = jnp.sum(dyf * yn, axis=0, keepdims=True)
        dyn = dyf * gf_ref[...]
        dx2 = r * (dyn - yn * jnp.mean(dyn * yn, axis=-1, keepdims=True))
        dx_ref[...] = dx2
        dxb_ref[...] = dx2.astype(BF16)
        row = lax.broadcasted_iota(jnp.int32, (8, D_MODEL), 0)
        upd = jnp.where(row == 0, dgf, jnp.where(row == 1, loss, 0.0))

        @pl.when(pl.program_id(0) == 0)
        def _():
            sm_ref[...] = upd

        @pl.when(pl.program_id(0) != 0)
        def _():
            sm_ref[...] += upd

    def rows(w, col=0):
        return pl.BlockSpec((tm, w), lambda i: (i, col))

    return pl.pallas_call(
        body, name="out_fwd", grid=(T // tm,),
        in_specs=[rows(FOX_W, P_FG // FOX_W), rows(DIL_W, P_DG // DIL_W), rows(MEM_W, P_MG // MEM_W),
                  rows(FOX_W), rows(DIL_W), rows(MEM_W),
                  pl.BlockSpec((MIX_W, D_MODEL), lambda i: (0, 0)),
                  rows(D_MODEL), rows(D_MODEL), pl.BlockSpec((1, D_MODEL), lambda i: (0, 0))],
        out_specs=[rows(MIX_W), rows(D_MODEL), rows(D_MODEL), pl.BlockSpec((8, D_MODEL), lambda i: (0, 0))],
        out_shape=[jax.ShapeDtypeStruct((T, MIX_W), BF16), jax.ShapeDtypeStruct((T, D_MODEL), F32),
                   jax.ShapeDtypeStruct((T, D_MODEL), BF16), jax.ShapeDtypeStruct((8, D_MODEL), F32)],
        compiler_params=_params(("arbitrary",)),
    )(proj, proj, proj, o_fox, o_dil, o_mem, w_out, x, target, gf)


def out_bwd(proj, o_fox, o_dil, o_mem, w_out, dx2b, tm):
    T = dx2b.shape[0]

    def body(fg_ref, dg_ref, mg_ref, of_ref, od_ref, om_ref, w_ref, dx_ref,
             dof_ref, dod_ref, dom_ref, dfg_ref, ddg_ref, dmg_ref):
        dmix = lax.dot_general(dx_ref[...], w_ref[...], (((1,), (1,)), ((), ())), preferred_element_type=F32)
        col = 0
        for g_ref, o_ref, do_ref, dgate_ref in ((fg_ref, of_ref, dof_ref, dfg_ref), (dg_ref, od_ref, dod_ref, ddg_ref),
                                                 (mg_ref, om_ref, dom_ref, dmg_ref)):
            w = g_ref.shape[1]
            d = dmix[:, col:col + w]
            col += w
            g = g_ref[...]
            sg = _sigmoid(g)
            do_ref[...] = d * (g * sg)
            dgate_ref[...] = (d * o_ref[...] * (sg * (1.0 + g * (1.0 - sg)))).astype(BF16)

    def rows(w, col=0):
        return pl.BlockSpec((tm, w), lambda i: (i, col))

    return pl.pallas_call(
        body, name="out_bwd", grid=(T // tm,),
        in_specs=[rows(FOX_W, P_FG // FOX_W), rows(DIL_W, P_DG // DIL_W), rows(MEM_W, P_MG // MEM_W),
                  rows(FOX_W), rows(DIL_W), rows(MEM_W),
                  pl.BlockSpec((MIX_W, D_MODEL), lambda i: (0, 0)), rows(D_MODEL)],
        out_specs=[rows(FOX_W), rows(DIL_W), rows(MEM_W), rows(FOX_W), rows(DIL_W), rows(MEM_W)],
        out_shape=[jax.ShapeDtypeStruct((T, FOX_W), F32), jax.ShapeDtypeStruct((T, DIL_W), F32),
                   jax.ShapeDtypeStruct((T, MEM_W), F32), jax.ShapeDtypeStruct((T, FOX_W), BF16),
                   jax.ShapeDtypeStruct((T, DIL_W), BF16), jax.ShapeDtypeStruct((T, MEM_W), BF16)],
        compiler_params=_params(("arbitrary",)),
    )(proj, proj, proj, o_fox, o_dil, o_mem, w_out, dx2b)


def adamw(w, g, m, v, tr, name):
    R, C = w.shape

    def body(w_ref, g_ref, m_ref, v_ref, d_ref, mo_ref, vo_ref):
        gv = g_ref[...]
        mn = ADAM_B1 * m_ref[...] + (1.0 - ADAM_B1) * gv
        vn = ADAM_B2 * v_ref[...] + (1.0 - ADAM_B2) * jnp.square(gv)
        m_hat = mn / (1.0 - ADAM_B1 ** ADAM_STEP)
        v_hat = vn / (1.0 - ADAM_B2 ** ADAM_STEP)
        d_ref[...] = -ADAM_LR * (m_hat / (jnp.sqrt(v_hat) + ADAM_EPS) + ADAM_WD * w_ref[...])
        mo_ref[...] = mn
        vo_ref[...] = vn

    spec = pl.BlockSpec((tr, C), lambda i: (i, 0))
    return pl.pallas_call(
        body, name=name, grid=(R // tr,),
        in_specs=[spec] * 4, out_specs=[spec] * 3,
        out_shape=[jax.ShapeDtypeStruct((R, C), F32)] * 3,
        compiler_params=_params(("arbitrary",)),
    )(w, g, m, v)


def _pad_row(v, width):
    return jnp.concatenate([v, jnp.zeros((1, width - v.shape[1]), v.dtype)], axis=1)


def local_grads(x, mem, norm_g, b_forget, mem_norm_g, final_norm_g, loss_target, w_in_p, w_kv, w_out):
    B, S, D = x.shape
    T = B * S
    xt = x.reshape(T, D)
    memt = mem.reshape(B * MEM_LEN, D)
    b_pad = _pad_row(b_forget, LANES)

    h = rms_fwd(xt, norm_g, 512, "rms_x")
    proj = mm_nn(h, w_in_p, 512, PW // 3, "in_proj")
    proj3 = proj.reshape(B, S, PW)
    mh = rms_fwd(memt, mem_norm_g, B * MEM_LEN, "rms_mem")
    mkv = mm_nn(mh, w_kv, B * MEM_LEN, 2 * MEM_W, "mem_kv_proj")
    mkv3 = mkv.reshape(B, MEM_LEN, 2 * MEM_W)

    negc = fox_gate(proj3, b_pad)
    causal = _log_masks_t(S, "causal")
    dilated = _log_masks_t(S, "dilated")
    rope = _rope_tables(S)

    o_fox, lse_fox = attn_fwd3("fox", proj3, S, negc_cols=negc, mask=causal)
    o_dil, lse_dil = attn_fwd3("dil", proj3, S, mask=dilated, rope=rope)
    o_mem, lse_mem = attn_fwd3("mem", proj3, S, kv=mkv3)

    ymix, dx2, dx2b, small_out = out_fwd(
        proj, o_fox.reshape(T, FOX_W), o_dil.reshape(T, DIL_W), o_mem.reshape(T, MEM_W), w_out,
        xt, loss_target.reshape(T, D), final_norm_g.reshape(1, D), 256)
    do_fox, do_dil, do_mem, dfg, ddg, dmg = out_bwd(
        proj, o_fox.reshape(T, FOX_W), o_dil.reshape(T, DIL_W), o_mem.reshape(T, MEM_W), w_out, dx2b, 256)
    g_out = mm_tn(ymix, dx2b, 512, D, "w_out_grad")

    dqkv_fox, dneg, drow = attn_bwd3("fox", proj3, do_fox.reshape(B, S, FOX_W), o_fox, lse_fox, S,
                                     negc_cols=negc, mask=causal)
    (dqkv_dil,) = attn_bwd3("dil", proj3, do_dil.reshape(B, S, DIL_W), o_dil, lse_dil, S, mask=dilated, rope=rope)
    dmq, dmk, dmv = attn_bwd3("mem", proj3, do_mem.reshape(B, S, MEM_W), o_mem, lse_mem, S, kv=mkv3)
    drow = drow.reshape(B, FOX_HEADS // 2, S // TQ, 2, TQ).transpose(0, 1, 3, 2, 4).reshape(B, FOX_HEADS, S)
    drow = jnp.pad(drow, ((0, 0), (0, LANES - FOX_HEADS), (0, 0)))
    dflog, db_part = fox_gate_bwd(drow, dneg, proj3, b_pad)

    dproj = jnp.concatenate([dqkv_fox.reshape(T, 3 * FOX_W), dfg, dqkv_dil.reshape(T, 3 * DIL_W), ddg,
                             dmq.reshape(T, MEM_W), dmg, dflog.reshape(T, LANES)], axis=1)
    g_in = mm_tn(h, dproj, 512, PW // 3, "w_in_grad")
    dh = mm_nt(dproj, w_in_p, 1024, PW // 3, "in_proj_bwd")
    grad_x, dng = rms_bwd(xt, norm_g, dh, dx2, 512, "rms_x_bwd")

    dmkv = jnp.concatenate([dmk, dmv], axis=2).reshape(B * MEM_LEN, 2 * MEM_W)
    g_kv = mm_tn(mh, dmkv, B * MEM_LEN, 2 * MEM_W, "w_kv_grad")
    dmh = mm_nt(dmkv, w_kv, B * MEM_LEN, D, "mem_kv_bwd")
    _, dmng = rms_bwd(memt, mem_norm_g, dmh, None, B * MEM_LEN, "rms_mem_bwd")

    small = jnp.concatenate([dng[0:1], dmng[0:1], small_out[0:1], _pad_row(db_part[0:1], D), small_out[1:2],
                             jnp.zeros((3, D), F32)], axis=0)
    return grad_x.reshape(B, S, D), g_in, g_kv, g_out, small


def kernel(x, mem, norm_g, w_in, b_forget, mem_norm_g, w_mem_kv, w_out, final_norm_g, loss_target, m_norm_g, m_w_in, m_b_forget, m_mem_norm_g, m_w_mem_kv, m_w_out, m_final_norm_g, v_norm_g, v_w_in, v_b_forget, v_mem_norm_g, v_w_mem_kv, v_w_out, v_final_norm_g):
    D = D_MODEL
    w_in_full, w_kv_full, w_out_full = weight_gather(
        [_pack_cols(w_in[0]).astype(BF16), w_mem_kv[0].astype(BF16), w_out[0].astype(BF16)])
    grad_x, g_in, g_kv, g_out, small = local_grads(
        x, mem, norm_g, b_forget, mem_norm_g, final_norm_g, loss_target, w_in_full, w_kv_full, w_out_full)

    big = [g_in, g_kv, g_out]
    *from_sibling, csum = grad_exchange_d2d(big, small)
    tiles = (32, 128, 128)
    names = ("w_in", "w_kv", "w_out")
    chip_parts = [chip_sum(g, got, tr, "chip_sum_" + n) for g, got, tr, n in zip(big, from_sibling, tiles, names)]
    *from_chips, tot = grad_exchange_ici([cp for cp, _ in chip_parts], csum)
    gw_in, gw_kv, gw_out = [final_sum(own, got, tr, "final_sum_" + n)
                            for (_, own), got, tr, n in zip(chip_parts, from_chips, tiles, names)]
    gw_in = _unpack_cols(gw_in)

    loss = tot[4, 0]
    g_norm, g_mem_norm, g_final, g_b = tot[0:1], tot[1:2], tot[2], tot[3:4, :FOX_HEADS]

    def rows8(*rows):
        rows = [r.reshape(1, -1) for r in rows]
        rows = [_pad_row(r, D) for r in rows]
        return jnp.concatenate(rows + [jnp.zeros((8 - len(rows), D), F32)], axis=0)

    sw = rows8(norm_g, mem_norm_g, final_norm_g, b_forget)
    sm = rows8(m_norm_g, m_mem_norm_g, m_final_norm_g, m_b_forget)
    sv = rows8(v_norm_g, v_mem_norm_g, v_final_norm_g, v_b_forget)
    d_s, m_s, v_s = adamw(sw, tot, sm, sv, 8, "adamw_small")
    d_in, m_in, v_in = adamw(w_in[0], gw_in, m_w_in[0], v_w_in[0], 32, "adamw_w_in")
    d_kv, m_kv, v_kv = adamw(w_mem_kv[0], gw_kv, m_w_mem_kv[0], v_w_mem_kv[0], 128, "adamw_w_kv")
    d_out, m_out, v_out = adamw(w_out[0], gw_out, m_w_out[0], v_w_out[0], 256, "adamw_w_out")

    def small_outs(t):
        return t[0:1], t[3:4, :FOX_HEADS], t[1:2], t[2]

    grads = (g_norm, gw_in[None], g_b, g_mem_norm, gw_kv[None], gw_out[None], g_final)
    outs = []
    for t, big in ((d_s, (d_in, d_kv, d_out)), (m_s, (m_in, m_kv, m_out)), (v_s, (v_in, v_kv, v_out))):
        n, b, mn, f = small_outs(t)
        outs += [n, big[0][None], b, mn, big[1][None], big[2][None], f]
    return (loss, grad_x, *grads, *outs)
```

```python
import functools
import math

import numpy as np
import jax
import jax.numpy as jnp
from jax import lax
from jax.experimental import pallas as pl
from jax.experimental.pallas import tpu as pltpu

F32 = jnp.float32
BF16 = jnp.bfloat16

D_MODEL = 1024
HEAD_DIM = 64
FOX_HEADS = 12
DIL_HEADS = 12
MEM_HEADS = 4
MEM_HEAD_DIM = 128
MEM_LEN = 256
FOX_W = FOX_HEADS * HEAD_DIM
DIL_W = DIL_HEADS * HEAD_DIM
MEM_W = MEM_HEADS * MEM_HEAD_DIM
MIX_W = FOX_W + DIL_W + MEM_W
DILATIONS = ((128, 1), (512, 4), (2048, 16))
ROPE_THETA = 500000.0
ROPE_DIM = HEAD_DIM // 4
RMS_EPS = 1e-6
NEG_INF = -1e30
IN_W = 4 * FOX_W + FOX_HEADS + 4 * DIL_W + 2 * MEM_W

ADAM_LR = 0.001
ADAM_B1 = 0.9
ADAM_B2 = 0.999
ADAM_EPS = 1e-08
ADAM_WD = 0.01
ADAM_STEP = 10

N_DEV = 8
LANES = 128
PAIR_W = 3 * LANES
TQ = 256
TK = 256

O_FQ, O_FK, O_FV, O_FG = 0, FOX_W, 2 * FOX_W, 3 * FOX_W
O_FLOG = 4 * FOX_W
O_DQ = O_FLOG + FOX_HEADS
O_DK, O_DV, O_DG = O_DQ + DIL_W, O_DQ + 2 * DIL_W, O_DQ + 3 * DIL_W
O_MQ = O_DQ + 4 * DIL_W
O_MG = O_MQ + MEM_W
P_FOX = 0
P_FG = P_FOX + 3 * FOX_W
P_DIL = P_FG + FOX_W
P_DG = P_DIL + 3 * DIL_W
P_MQ = P_DG + DIL_W
P_MG = P_MQ + MEM_W
P_FLOG = P_MG + MEM_W
PW = P_FLOG + LANES

VMEM_LIMIT = 56 * 1024 * 1024


def _pack_pieces():
    pieces = []
    for base in (O_FQ, O_DQ):
        seg = []
        for hp in range(FOX_HEADS // 2):
            for part in range(3):
                seg.append((base + part * FOX_W + hp * LANES, LANES))
        pieces.append(seg)
    fox, dil = pieces
    return fox + [(O_FG, FOX_W)] + dil + [(O_DG, DIL_W), (O_MQ, MEM_W), (O_MG, MEM_W), (O_FLOG, FOX_HEADS)]


def _pack_cols(w):
    parts = [w[..., s:s + n] for s, n in _pack_pieces()]
    parts.append(jnp.zeros(w.shape[:-1] + (LANES - FOX_HEADS,), w.dtype))
    return jnp.concatenate(parts, axis=-1)


def _unpack_cols(g):
    runs = []
    pos = 0
    for s, n in _pack_pieces():
        runs.append((s, n, pos))
        pos += n
    runs.sort()
    return jnp.concatenate([g[..., p:p + n] for s, n, p in runs], axis=-1)


def _params(sem=None, **kw):
    return pltpu.CompilerParams(dimension_semantics=sem, vmem_limit_bytes=VMEM_LIMIT, **kw)


def _mesh_pos():
    return lax.axis_index("x"), lax.axis_index("y"), lax.axis_index("c")


def _flip(v, d):
    return 1 - v if d else v


_RELATIONS = [(dx, dy, dc) for dx in (0, 1) for dy in (0, 1) for dc in (0, 1)][1:]


def weight_gather(shards):
    n_arr = len(shards)
    rows = [s.shape[0] for s in shards]

    def body(*refs):
        in_refs = refs[:n_arr]
        out_refs = refs[n_arr:2 * n_arr]
        send_sems, recv_sems, local_sems = refs[2 * n_arr:]
        x, y, c = _mesh_pos()
        me, sibling = (x, y, c), (x, y, 1 - c)
        chips = [(1 - x, y), (x, 1 - y), (1 - x, 1 - y)]

        def block(a, pos):
            px, py, pc = pos
            return out_refs[a].at[pl.ds((4 * px + 2 * py + pc) * rows[a], rows[a]), :]

        def copy(a, k, blk, to, src=None):
            return pltpu.make_async_remote_copy(
                src_ref=block(a, blk) if src is None else src, dst_ref=block(a, blk),
                send_sem=send_sems.at[a, k], recv_sem=recv_sems.at[a, k],
                device_id=to, device_id_type=pl.DeviceIdType.MESH)

        started = []
        mine = []
        for a in range(n_arr):
            cp = pltpu.make_async_copy(in_refs[a], block(a, me), local_sems.at[a])
            cp.start()
            mine.append(cp)
            first = [copy(a, 0, me, sibling, src=in_refs[a])]
            first += [copy(a, 1 + j, me, (*chip, c), src=in_refs[a]) for j, chip in enumerate(chips)]
            for cp in first:
                cp.start()
            started += first
        for a in range(n_arr):
            for j, chip in enumerate(chips):
                copy(a, 1 + j, (*chip, c), me).wait_recv()
                passed = copy(a, 4 + j, (*chip, c), sibling)
                passed.start()
                started.append(passed)
        for a in range(n_arr):
            copy(a, 0, sibling, me).wait_recv()
            for j, chip in enumerate(chips):
                copy(a, 4 + j, (*chip, 1 - c), me).wait_recv()
        for cp in started:
            cp.wait_send()
        for cp in mine:
            cp.wait()

    any_spec = pl.BlockSpec(memory_space=pl.ANY)
    return pl.pallas_call(
        body, name="weight_gather",
        out_shape=[jax.ShapeDtypeStruct((N_DEV * s.shape[0], s.shape[1]), s.dtype) for s in shards],
        in_specs=[any_spec] * n_arr, out_specs=[any_spec] * n_arr,
        scratch_shapes=[pltpu.SemaphoreType.DMA((n_arr, 7)), pltpu.SemaphoreType.DMA((n_arr, 7)),
                        pltpu.SemaphoreType.DMA((n_arr,))],
    )(*shards)


def grad_exchange(grads, small):
    arrs = list(grads) + [small]
    n_arr = len(arrs)
    rows = [g.shape[0] // N_DEV for g in grads] + [small.shape[0]]

    def body(*refs):
        in_refs = refs[:n_arr]
        out_refs = refs[n_arr:2 * n_arr]
        send_sems, recv_sems, local_sems = refs[2 * n_arr:]
        x, y, c = _mesh_pos()
        me = 4 * x + 2 * y + c

        def src(a, idx):
            if a == n_arr - 1:
                return in_refs[a]
            return in_refs[a].at[pl.ds(idx * rows[a], rows[a]), :]

        def copy(a, k):
            dx, dy, dc = _RELATIONS[k]
            px, py, pc = _flip(x, dx), _flip(y, dy), _flip(c, dc)
            peer = 4 * px + 2 * py + pc
            send = pltpu.make_async_remote_copy(
                src_ref=src(a, peer), dst_ref=out_refs[a].at[me],
                send_sem=send_sems.at[a, k], recv_sem=recv_sems.at[a, k],
                device_id=(px, py, pc), device_id_type=pl.DeviceIdType.MESH)
            recv = pltpu.make_async_remote_copy(
                src_ref=src(a, peer), dst_ref=out_refs[a].at[peer],
                send_sem=send_sems.at[a, k], recv_sem=recv_sems.at[a, k],
                device_id=(px, py, pc), device_id_type=pl.DeviceIdType.MESH)
            return send, recv

        mine = []
        pairs = []
        for a in range(n_arr):
            cp = pltpu.make_async_copy(src(a, me), out_refs[a].at[me], local_sems.at[a])
            cp.start()
            mine.append(cp)
            for k in range(7):
                send, recv = copy(a, k)
                send.start()
                pairs.append((send, recv))
        for send, recv in pairs:
            recv.wait_recv()
        for send, recv in pairs:
            send.wait_send()
        for cp in mine:
            cp.wait()

    any_spec = pl.BlockSpec(memory_space=pl.ANY)
    return pl.pallas_call(
        body, name="grad_exchange",
        out_shape=[jax.ShapeDtypeStruct((N_DEV, r, a.shape[1]), a.dtype) for r, a in zip(rows, arrs)],
        in_specs=[any_spec] * n_arr, out_specs=[any_spec] * n_arr,
        scratch_shapes=[pltpu.SemaphoreType.DMA((n_arr, 7)), pltpu.SemaphoreType.DMA((n_arr, 7)),
                        pltpu.SemaphoreType.DMA((n_arr,))],
    )(*arrs)


def slot_sum(slots, tr, name):
    _, R, C = slots.shape

    def body(s_ref, o_ref):
        acc = s_ref[0]
        for d in range(1, N_DEV):
            acc = acc + s_ref[d]
        o_ref[...] = acc

    return pl.pallas_call(
        body, name=name, grid=(R // tr,),
        in_specs=[pl.BlockSpec((N_DEV, tr, C), lambda i: (0, i, 0))],
        out_specs=pl.BlockSpec((tr, C), lambda i: (i, 0)),
        out_shape=jax.ShapeDtypeStruct((R, C), slots.dtype),
        compiler_params=_params(("arbitrary",)),
    )(slots)


N_CHIP = 4
_OTHER_CHIPS = [(1, 0), (0, 1), (1, 1)]


def grad_exchange_d2d(grads, small):
    n_big = len(grads)
    rows = [g.shape[0] // N_DEV for g in grads]

    def body(*refs):
        g_refs = refs[:n_big]
        small_ref = refs[n_big]
        out_refs = refs[n_big + 1:2 * n_big + 1]
        csum_ref = refs[2 * n_big + 1]
        land, send_sems, recv_sems = refs[2 * n_big + 2:]
        x, y, c = _mesh_pos()
        sibling = (x, y, 1 - c)
        copies = []
        for a in range(n_big):
            for q in range(N_CHIP):
                copies.append(pltpu.make_async_remote_copy(
                    src_ref=g_refs[a].at[pl.ds((2 * q + 1 - c) * rows[a], rows[a]), :], dst_ref=out_refs[a].at[q],
                    send_sem=send_sems.at[a, q], recv_sem=recv_sems.at[a, q],
                    device_id=sibling, device_id_type=pl.DeviceIdType.MESH))
        copies.append(pltpu.make_async_remote_copy(
            src_ref=small_ref, dst_ref=land, send_sem=send_sems.at[n_big, 0], recv_sem=recv_sems.at[n_big, 0],
            device_id=sibling, device_id_type=pl.DeviceIdType.MESH))
        for cp in copies:
            cp.start()
        for cp in copies:
            cp.wait_recv()
        for cp in copies:
            cp.wait_send()
        csum_ref[...] = small_ref[...] + land[...]

    any_spec = pl.BlockSpec(memory_space=pl.ANY)
    vmem_spec = pl.BlockSpec(memory_space=pltpu.VMEM)
    return pl.pallas_call(
        body, name="grad_exchange_d2d",
        out_shape=[jax.ShapeDtypeStruct((N_CHIP, r, g.shape[1]), g.dtype) for r, g in zip(rows, grads)]
        + [jax.ShapeDtypeStruct(small.shape, small.dtype)],
        in_specs=[any_spec] * n_big + [vmem_spec], out_specs=[any_spec] * n_big + [vmem_spec],
        scratch_shapes=[pltpu.VMEM(small.shape, small.dtype),
                        pltpu.SemaphoreType.DMA((n_big + 1, N_CHIP)), pltpu.SemaphoreType.DMA((n_big + 1, N_CHIP))],
    )(*grads, small)


def chip_sum(g, got, tr, name):
    _, rows, cols = got.shape
    g4 = g.reshape(N_CHIP, 2, rows, cols)
    x, y, c = _mesh_pos()
    where = jnp.stack([c, 2 * x + y]).astype(jnp.int32)

    def body_all(w_ref, g_ref, r_ref, o_ref):
        o_ref[0] = (g_ref[0, 0] + r_ref[0]).astype(BF16)

    def body_own(w_ref, g_ref, r_ref, o_ref):
        o_ref[...] = g_ref[0, 0] + r_ref[0]

    cpb = pl.pallas_call(
        body_all, name=name + "_all",
        grid_spec=pltpu.PrefetchScalarGridSpec(
            num_scalar_prefetch=1, grid=(N_CHIP, rows // tr),
            in_specs=[pl.BlockSpec((1, 1, tr, cols), lambda q, i, w: (q, w[0], i, 0)),
                      pl.BlockSpec((1, tr, cols), lambda q, i, w: (q, i, 0))],
            out_specs=pl.BlockSpec((1, tr, cols), lambda q, i, w: (q, i, 0))),
        out_shape=jax.ShapeDtypeStruct((N_CHIP, rows, cols), BF16),
        compiler_params=_params(("arbitrary", "arbitrary")),
    )(where, g4, got)
    own = pl.pallas_call(
        body_own, name=name + "_own",
        grid_spec=pltpu.PrefetchScalarGridSpec(
            num_scalar_prefetch=1, grid=(rows // tr,),
            in_specs=[pl.BlockSpec((1, 1, tr, cols), lambda i, w: (w[1], w[0], i, 0)),
                      pl.BlockSpec((1, tr, cols), lambda i, w: (w[1], i, 0))],
            out_specs=pl.BlockSpec((tr, cols), lambda i, w: (i, 0))),
        out_shape=jax.ShapeDtypeStruct((rows, cols), F32),
        compiler_params=_params(("arbitrary",)),
    )(where, g4, got)
    return cpb, own


def grad_exchange_ici(parts, csum):
    n_big = len(parts)

    def body(*refs):
        p_refs = refs[:n_big]
        csum_ref = refs[n_big]
        out_refs = refs[n_big + 1:2 * n_big + 1]
        tot_ref = refs[2 * n_big + 1]
        land, send_sems, recv_sems = refs[2 * n_big + 2:]
        x, y, c = _mesh_pos()
        q_me = 2 * x + y
        land[q_me] = csum_ref[...]
        sends, recvs = [], []
        for j, (dx, dy) in enumerate(_OTHER_CHIPS):
            px, py = _flip(x, dx), _flip(y, dy)
            q_peer = 2 * px + py
            for a in range(n_big + 1):
                src = p_refs[a].at[q_peer] if a < n_big else csum_ref
                dst = out_refs[a] if a < n_big else land
                common = dict(send_sem=send_sems.at[a, j], recv_sem=recv_sems.at[a, j],
                              device_id=(px, py, c), device_id_type=pl.DeviceIdType.MESH)
                sends.append(pltpu.make_async_remote_copy(src_ref=src, dst_ref=dst.at[q_me], **common))
                recvs.append(pltpu.make_async_remote_copy(src_ref=src, dst_ref=dst.at[q_peer], **common))
        for cp in sends:
            cp.start()
        for cp in recvs:
            cp.wait_recv()
        for cp in sends:
            cp.wait_send()
        tot = land[0]
        for q in range(1, N_CHIP):
            tot = tot + land[q]
        tot_ref[...] = tot

    any_spec = pl.BlockSpec(memory_space=pl.ANY)
    vmem_spec = pl.BlockSpec(memory_space=pltpu.VMEM)
    return pl.pallas_call(
        body, name="grad_exchange_ici",
        out_shape=[jax.ShapeDtypeStruct(p.shape, p.dtype) for p in parts] + [jax.ShapeDtypeStruct(csum.shape, csum.dtype)],
        in_specs=[any_spec] * n_big + [vmem_spec], out_specs=[any_spec] * n_big + [vmem_spec],
        scratch_shapes=[pltpu.VMEM((N_CHIP,) + csum.shape, csum.dtype),
                        pltpu.SemaphoreType.DMA((n_big + 1, 3)), pltpu.SemaphoreType.DMA((n_big + 1, 3))],
    )(*parts, csum)


def final_sum(own, got, tr, name):
    rows, cols = own.shape

    def body(own_ref, got_ref, o_ref):
        q_me = 2 * lax.axis_index("x") + lax.axis_index("y")
        acc = None
        for q in range(N_CHIP):
            term = jnp.where(q == q_me, own_ref[...], got_ref[q].astype(F32))
            acc = term if acc is None else acc + term
        o_ref[...] = acc

    return pl.pallas_call(
        body, name=name, grid=(rows // tr,),
        in_specs=[pl.BlockSpec((tr, cols), lambda i: (i, 0)), pl.BlockSpec((N_CHIP, tr, cols), lambda i: (0, i, 0))],
        out_specs=pl.BlockSpec((tr, cols), lambda i: (i, 0)),
        out_shape=jax.ShapeDtypeStruct((rows, cols), F32),
        compiler_params=_params(("arbitrary",)),
    )(own, got)


def rms_fwd(x, g, tm, name):
    M, K = x.shape

    def body(x_ref, g_ref, o_ref):
        xv = x_ref[...]
        r = lax.rsqrt(jnp.mean(xv * xv, axis=-1, keepdims=True) + RMS_EPS)
        o_ref[...] = ((xv * r) * g_ref[...]).astype(BF16)

    return pl.pallas_call(
        body, name=name, grid=(M // tm,),
        in_specs=[pl.BlockSpec((tm, K), lambda i: (i, 0)), pl.BlockSpec((1, K), lambda i: (0, 0))],
        out_specs=pl.BlockSpec((tm, K), lambda i: (i, 0)),
        out_shape=jax.ShapeDtypeStruct((M, K), BF16),
        compiler_params=_params(("arbitrary",)),
    )(x, g)


def rms_bwd(x, g, dh, dres, tm, name):
    M, K = x.shape
    has_res = dres is not None

    def body(*refs):
        if has_res:
            x_ref, g_ref, dh_ref, dres_ref, dx_ref, dg_ref = refs
        else:
            x_ref, g_ref, dh_ref, dx_ref, dg_ref = refs
        xv = x_ref[...]
        r = lax.rsqrt(jnp.mean(xv * xv, axis=-1, keepdims=True) + RMS_EPS)
        xn = xv * r
        dhv = dh_ref[...]
        dxn = dhv * g_ref[...]
        dx = r * (dxn - xn * jnp.mean(dxn * xn, axis=-1, keepdims=True))
        if has_res:
            dx = dx + dres_ref[...]
        dx_ref[...] = dx
        part = jnp.sum(dhv * xn, axis=0, keepdims=True)
        row = lax.broadcasted_iota(jnp.int32, (8, K), 0)
        upd = jnp.where(row == 0, part, 0.0)

        @pl.when(pl.program_id(0) == 0)
        def _():
            dg_ref[...] = upd

        @pl.when(pl.program_id(0) != 0)
        def _():
            dg_ref[...] += upd

    row_spec = pl.BlockSpec((tm, K), lambda i: (i, 0))
    ins = [x, g, dh] + ([dres] if has_res else [])
    in_specs = [row_spec, pl.BlockSpec((1, K), lambda i: (0, 0)), row_spec] + ([row_spec] if has_res else [])
    return pl.pallas_call(
        body, name=name, grid=(M // tm,),
        in_specs=in_specs,
        out_specs=[row_spec, pl.BlockSpec((8, K), lambda i: (0, 0))],
        out_shape=[jax.ShapeDtypeStruct((M, K), F32), jax.ShapeDtypeStruct((8, K), F32)],
        compiler_params=_params(("arbitrary",)),
    )(*ins)


def mm_nn(a, b, tm, tn, name):
    M, K = a.shape
    N = b.shape[1]

    def body(a_ref, b_ref, o_ref):
        o_ref[...] = jnp.dot(a_ref[...], b_ref[...], preferred_element_type=F32)

    return pl.pallas_call(
        body, name=name, grid=(N // tn, M // tm),
        in_specs=[pl.BlockSpec((tm, K), lambda j, i: (i, 0)), pl.BlockSpec((K, tn), lambda j, i: (0, j))],
        out_specs=pl.BlockSpec((tm, tn), lambda j, i: (i, j)),
        out_shape=jax.ShapeDtypeStruct((M, N), F32),
        compiler_params=_params(("arbitrary", "arbitrary")),
    )(a, b)


def mm_nt(a, b, tm, tk, name):
    M, K = a.shape
    N = b.shape[0]

    def body(a_ref, b_ref, o_ref):
        part = lax.dot_general(a_ref[...], b_ref[...], (((1,), (1,)), ((), ())), preferred_element_type=F32)

        @pl.when(pl.program_id(1) == 0)
        def _():
            o_ref[...] = part

        @pl.when(pl.program_id(1) != 0)
        def _():
            o_ref[...] += part

    return pl.pallas_call(
        body, name=name, grid=(M // tm, K // tk),
        in_specs=[pl.BlockSpec((tm, tk), lambda i, k: (i, k)), pl.BlockSpec((N, tk), lambda i, k: (0, k))],
        out_specs=pl.BlockSpec((tm, N), lambda i, k: (i, 0)),
        out_shape=jax.ShapeDtypeStruct((M, N), F32),
        compiler_params=_params(("arbitrary", "arbitrary")),
    )(a, b)


def mm_tn(a, b, tt, tn, name):
    T, K = a.shape
    N = b.shape[1]

    def body(a_ref, b_ref, o_ref):
        part = lax.dot_general(a_ref[...], b_ref[...], (((0,), (0,)), ((), ())), preferred_element_type=F32)

        @pl.when(pl.program_id(1) == 0)
        def _():
            o_ref[...] = part

        @pl.when(pl.program_id(1) != 0)
        def _():
            o_ref[...] += part

    return pl.pallas_call(
        body, name=name, grid=(N // tn, T // tt),
        in_specs=[pl.BlockSpec((tt, K), lambda j, t: (t, 0)), pl.BlockSpec((tt, tn), lambda j, t: (t, j))],
        out_specs=pl.BlockSpec((K, tn), lambda j, t: (0, j)),
        out_shape=jax.ShapeDtypeStruct((K, N), F32),
        compiler_params=_params(("arbitrary", "arbitrary")),
    )(a, b)


def _log_sigmoid(z):
    return jnp.minimum(z, 0.0) - jnp.log(1.0 + jnp.exp(-jnp.abs(z)))


def _tri(n, lower):
    r = lax.broadcasted_iota(jnp.int32, (n, n), 0)
    c = lax.broadcasted_iota(jnp.int32, (n, n), 1)
    return jnp.where((r >= c) if lower else (r <= c), 1.0, 0.0).astype(F32)


def fox_gate(proj3, b_pad):
    B, S, _ = proj3.shape
    nblk = S // TK

    def body(f_ref, b_ref, o_ref):
        tri = _tri(TK, True)
        carry = jnp.zeros((1, LANES), F32)
        for n in range(nblk):
            z = f_ref[0, n * TK:(n + 1) * TK, :] + b_ref[...]
            logf = _log_sigmoid(z)
            cs = jnp.dot(tri, logf, preferred_element_type=F32, precision=lax.Precision.HIGHEST) + carry
            carry = cs[TK - 1:TK, :]
            o_ref[0, n * TK:(n + 1) * TK, :] = -cs

    return pl.pallas_call(
        body, name="fox_gate", grid=(B,),
        in_specs=[pl.BlockSpec((1, S, LANES), lambda b: (b, 0, P_FLOG // LANES)),
                  pl.BlockSpec((1, LANES), lambda b: (0, 0))],
        out_specs=pl.BlockSpec((1, S, LANES), lambda b: (b, 0, 0)),
        out_shape=jax.ShapeDtypeStruct((B, S, LANES), F32),
        compiler_params=_params(("arbitrary",)),
    )(proj3, b_pad)


def fox_gate_bwd(drow, dneg, proj3, b_pad):
    B, S, _ = proj3.shape
    nblk = S // TK

    def body(d_ref, r_ref, f_ref, b_ref, o_ref, db_ref):
        tri = _tri(TK, False)
        lane = lax.broadcasted_iota(jnp.int32, (TK, LANES), 1)
        er = lax.broadcasted_iota(jnp.int32, (FOX_W, LANES), 0)
        ec = lax.broadcasted_iota(jnp.int32, (FOX_W, LANES), 1)
        pick = jnp.where(er == LANES * (ec >> 1) + (ec & 1), 1.0, 0.0).astype(F32)
        carry = jnp.zeros((1, LANES), F32)
        dbsum = jnp.zeros((1, LANES), F32)
        for n in reversed(range(nblk)):
            dk_side = jnp.dot(r_ref[0, n * TK:(n + 1) * TK, :], pick, preferred_element_type=F32,
                              precision=lax.Precision.HIGHEST)
            dc = jnp.where(lane < FOX_HEADS, d_ref[0, :, n * TK:(n + 1) * TK].T - dk_side, 0.0)
            rs = jnp.dot(tri, dc, preferred_element_type=F32, precision=lax.Precision.HIGHEST) + carry
            carry = rs[0:1, :]
            z = f_ref[0, n * TK:(n + 1) * TK, :] + b_ref[...]
            dz = rs * (1.0 / (1.0 + jnp.exp(z)))
            o_ref[0, n * TK:(n + 1) * TK, :] = dz.astype(BF16)
            dbsum = dbsum + jnp.sum(dz, axis=0, keepdims=True)
        row = lax.broadcasted_iota(jnp.int32, (8, LANES), 0)
        upd = jnp.where(row == 0, dbsum, 0.0)

        @pl.when(pl.program_id(0) == 0)
        def _():
            db_ref[...] = upd

        @pl.when(pl.program_id(0) != 0)
        def _():
            db_ref[...] += upd

    return pl.pallas_call(
        body, name="fox_gate_bwd", grid=(B,),
        in_specs=[pl.BlockSpec((1, LANES, S), lambda b: (b, 0, 0)),
                  pl.BlockSpec((1, S, FOX_W), lambda b: (b, 0, 0)),
                  pl.BlockSpec((1, S, LANES), lambda b: (b, 0, P_FLOG // LANES)),
                  pl.BlockSpec((1, LANES), lambda b: (0, 0))],
        out_specs=[pl.BlockSpec((1, S, LANES), lambda b: (b, 0, 0)), pl.BlockSpec((8, LANES), lambda b: (0, 0))],
        out_shape=[jax.ShapeDtypeStruct((B, S, LANES), BF16), jax.ShapeDtypeStruct((8, LANES), F32)],
        compiler_params=_params(("arbitrary",)),
    )(drow, dneg, proj3, b_pad)


def _mult_masks(S, kind):
    nd = S // TQ
    a = np.arange(TQ)[:, None]
    b = np.arange(TK)[None, :]
    out = np.zeros((nd, TQ, TK), np.float32)
    for d in range(nd):
        delta = d * TQ + a - b
        if kind == "causal":
            out[d] = delta >= 0
        else:
            m = np.zeros((TQ, TK), np.float32)
            for w, dil in DILATIONS:
                m += (delta >= 0) & (delta % dil == 0) & (delta <= w)
            out[d] = m
    return jnp.asarray(out)


def _rope_tables(S):
    half = ROPE_DIM // 2
    pos = jnp.arange(S, dtype=F32)
    inv_freq = 1.0 / (ROPE_THETA ** (jnp.arange(0, ROPE_DIM, 2, dtype=F32) / ROPE_DIM))
    ang = pos[:, None] * inv_freq[None, :]
    cos, sin = jnp.cos(ang), jnp.sin(ang)
    one = jnp.ones((S, HEAD_DIM - ROPE_DIM), F32)
    zero = jnp.zeros((S, HEAD_DIM - ROPE_DIM), F32)
    zh = jnp.zeros((S, half), F32)
    c = jnp.concatenate([cos, cos, one], axis=1)
    s1 = jnp.concatenate([-sin, zh, zero], axis=1)
    s2 = jnp.concatenate([zh, sin, zero], axis=1)
    return tuple(jnp.concatenate([t, t], axis=1) for t in (c, s1, s2))


def _rope(t, c, s1, s2):
    return t * c + pltpu.roll(t, LANES - half_rope(), 1) * s1 + pltpu.roll(t, half_rope(), 1) * s2


def half_rope():
    return ROPE_DIM // 2


def _rope_bwd(d, c, s1, s2):
    return d * c + pltpu.roll(d * s1, half_rope(), 1) + pltpu.roll(d * s2, LANES - half_rope(), 1)


def _scale_parts(scale):
    m, _ = math.frexp(scale)
    return (scale, None) if m == 0.5 else (None, scale)


def attn_fwd(kind, src, S, *, negc=None, mask=None, rope=None, kv=None):
    B = src.shape[0]
    pair = kind != "mem"
    col0 = {"fox": P_FOX, "dil": P_DIL, "mem": P_MQ}[kind]
    n_blocks = FOX_HEADS // 2 if pair else MEM_HEADS
    e_dim = HEAD_DIM if pair else MEM_HEAD_DIM
    q_fold, s_scale = _scale_parts(1.0 / math.sqrt(e_dim))
    Sk = S if pair else MEM_LEN
    nh = 2 if pair else 1
    has_bias = negc is not None
    has_rope = rope is not None
    nq = S // TQ

    def body(*refs):
        refs = list(refs)
        if pair:
            qkv_ref = refs.pop(0)
        else:
            q_ref, k_ref, v_ref = refs.pop(0), refs.pop(0), refs.pop(0)
        negc_ref = refs.pop(0) if has_bias else None
        mask_ref = refs.pop(0) if pair else None
        rope_refs = [refs.pop(0) for _ in range(3)] if has_rope else None
        o_ref, lse_ref, qs, ks, vs = refs
        lane = lax.broadcasted_iota(jnp.int32, (1, LANES), 1)

        def prep_q(n, _):
            r0 = pl.multiple_of(n * TQ, TQ)
            rows = pl.ds(r0, TQ)
            q = qkv_ref[0, rows, 0:LANES] if pair else q_ref[0, rows, :]
            if has_rope:
                q = _rope(q, *[t[rows, :] for t in rope_refs])
            if q_fold is not None:
                q = q * q_fold
            qs[rows, :] = q.astype(BF16)
            return 0

        def prep_kv(n, _):
            r0 = pl.multiple_of(n * TK, TK)
            rows = pl.ds(r0, TK)
            k = qkv_ref[0, rows, LANES:2 * LANES] if pair else k_ref[0, rows, :]
            v = qkv_ref[0, rows, 2 * LANES:3 * LANES] if pair else v_ref[0, rows, :]
            if has_rope:
                k = _rope(k, *[t[rows, :] for t in rope_refs])
            ks[rows, :] = k.astype(BF16)
            vs[rows, :] = v.astype(BF16)
            return 0

        lax.fori_loop(0, nq, prep_q, 0)
        lax.fori_loop(0, Sk // TK, prep_kv, 0)

        def q_loop(i, _):
            r0 = pl.multiple_of(i * TQ, TQ)
            q = qs[pl.ds(r0, TQ), :]
            res = []
            for hh in range(nh):
                hmask = (lane >= HEAD_DIM * hh) & (lane < HEAD_DIM * (hh + 1))
                qh = jnp.where(hmask, q, jnp.zeros_like(q)) if pair else q

                def kv_loop(j, carry, qh=qh, hh=hh):
                    m, l, acc = carry
                    c0 = pl.multiple_of(j * TK, TK)
                    k = ks[pl.ds(c0, TK), :]
                    v = vs[pl.ds(c0, TK), :]
                    s = lax.dot_general(qh, k, (((1,), (1,)), ((), ())), preferred_element_type=F32)
                    if s_scale is not None:
                        s = s * s_scale
                    if has_bias:
                        s = s + negc_ref[0, 0, pl.ds(hh, 1), pl.ds(c0, TK)]
                    if pair:
                        mult = mask_ref[i - j]
                        s = jnp.where(mult > 0.0, s, NEG_INF)
                    m_new = jnp.maximum(m, jnp.max(s, axis=1, keepdims=True))
                    p = jnp.exp(s - m_new)
                    if pair:
                        p = p * mult
                    alpha = jnp.exp(m - m_new)
                    l = alpha * l + jnp.sum(p, axis=1, keepdims=True)
                    acc = acc * alpha + jnp.dot(p.astype(BF16), v, preferred_element_type=F32)
                    return m_new, l, acc

                init = (jnp.full((TQ, 1), NEG_INF, F32), jnp.zeros((TQ, 1), F32), jnp.zeros((TQ, LANES), F32))
                m, l, acc = lax.fori_loop(0, (i + 1) if pair else Sk // TK, kv_loop, init)
                res.append((acc / l, m + jnp.log(l)))
            if pair:
                o = jnp.where(lane < HEAD_DIM, res[0][0], res[1][0])
                lse = jnp.where(lane < HEAD_DIM, res[0][1], res[1][1])
            else:
                o = res[0][0]
                lse = jnp.broadcast_to(res[0][1], (TQ, LANES))
            o_ref[0, pl.ds(r0, TQ), :] = o
            lse_ref[0, pl.ds(r0, TQ), :] = lse
            return 0

        lax.fori_loop(0, nq, q_loop, 0)

    ins, in_specs = [], []
    if pair:
        ins.append(src)
        in_specs.append(pl.BlockSpec((1, S, PAIR_W), lambda b, h: (b, 0, col0 // PAIR_W + h)))
    else:
        ins += [src, kv, kv]
        in_specs += [pl.BlockSpec((1, S, LANES), lambda b, h: (b, 0, col0 // LANES + h)),
                     pl.BlockSpec((1, MEM_LEN, LANES), lambda b, h: (b, 0, h)),
                     pl.BlockSpec((1, MEM_LEN, LANES), lambda b, h: (b, 0, MEM_HEADS + h))]
    if has_bias:
        ins.append(negc)
        in_specs.append(pl.BlockSpec((1, 1, 2, S), lambda b, h: (b, h, 0, 0)))
    if pair:
        ins.append(mask)
        in_specs.append(pl.BlockSpec(mask.shape, lambda b, h: (0, 0, 0)))
    if has_rope:
        ins += list(rope)
        in_specs += [pl.BlockSpec((S, LANES), lambda b, h: (0, 0))] * 3
    W = n_blocks * LANES
    out_spec = pl.BlockSpec((1, S, LANES), lambda b, h: (b, 0, h))
    return pl.pallas_call(
        body, name=kind + "_attn_fwd", grid=(B, n_blocks),
        in_specs=in_specs, out_specs=[out_spec, out_spec],
        out_shape=[jax.ShapeDtypeStruct((B, S, W), F32)] * 2,
        scratch_shapes=[pltpu.VMEM((S, LANES), BF16), pltpu.VMEM((Sk, LANES), BF16), pltpu.VMEM((Sk, LANES), BF16)],
        compiler_params=_params(("arbitrary", "arbitrary")),
    )(*ins)


def attn_bwd(kind, src, do, o, lse, S, *, negc=None, mask=None, rope=None, kv=None):
    B = src.shape[0]
    pair = kind != "mem"
    col0 = {"fox": P_FOX, "dil": P_DIL, "mem": P_MQ}[kind]
    n_blocks = FOX_HEADS // 2 if pair else MEM_HEADS
    e_dim = HEAD_DIM if pair else MEM_HEAD_DIM
    scale = 1.0 / math.sqrt(e_dim)
    q_fold, s_scale = _scale_parts(scale)
    Sk = S if pair else MEM_LEN
    nh = 2 if pair else 1
    has_bias = negc is not None
    has_rope = rope is not None
    nq = S // TQ
    nk = Sk // TK

    def body(*refs):
        refs = list(refs)
        if pair:
            qkv_ref = refs.pop(0)
        else:
            q_ref, k_ref, v_ref = refs.pop(0), refs.pop(0), refs.pop(0)
        do_ref, o_ref, lse_ref = refs.pop(0), refs.pop(0), refs.pop(0)
        negc_ref = refs.pop(0) if has_bias else None
        mask_ref = refs.pop(0) if pair else None
        rope_refs = [refs.pop(0) for _ in range(3)] if has_rope else None
        if pair:
            dqkv_ref = refs.pop(0)
            dnegc_ref = refs.pop(0) if has_bias else None
            drow_ref = refs.pop(0) if has_bias else None
        else:
            dq_ref, dk_ref, dv_ref = refs.pop(0), refs.pop(0), refs.pop(0)
        qs, ks, vs, dos, delta_s, dq_acc = refs[:6]
        drow_acc = refs[6] if has_bias else None
        lane = lax.broadcasted_iota(jnp.int32, (1, LANES), 1)

        def prep_q(n, _):
            r0 = pl.multiple_of(n * TQ, TQ)
            rows = pl.ds(r0, TQ)
            q = qkv_ref[0, rows, 0:LANES] if pair else q_ref[0, rows, :]
            if has_rope:
                q = _rope(q, *[t[rows, :] for t in rope_refs])
            if q_fold is not None:
                q = q * q_fold
            qs[rows, :] = q.astype(BF16)
            dov = do_ref[0, rows, :]
            dob = dov.astype(BF16)
            dos[rows, :] = dob
            prod = dob.astype(F32) * o_ref[0, rows, :]
            if pair:
                d0 = jnp.sum(jnp.where(lane < HEAD_DIM, prod, 0.0), axis=1, keepdims=True)
                d1 = jnp.sum(jnp.where(lane < HEAD_DIM, 0.0, prod), axis=1, keepdims=True)
                delta_s[rows, :] = jnp.where(lane < HEAD_DIM, d0, d1)
            else:
                delta_s[rows, :] = jnp.broadcast_to(jnp.sum(prod, axis=1, keepdims=True), (TQ, LANES))
            dq_acc[rows, :] = jnp.zeros((TQ, LANES), F32)
            if has_bias:
                drow_acc[rows, :] = jnp.zeros((TQ, LANES), F32)
            return 0

        def prep_kv(n, _):
            r0 = pl.multiple_of(n * TK, TK)
            rows = pl.ds(r0, TK)
            k = qkv_ref[0, rows, LANES:2 * LANES] if pair else k_ref[0, rows, :]
            v = qkv_ref[0, rows, 2 * LANES:3 * LANES] if pair else v_ref[0, rows, :]
            if has_rope:
                k = _rope(k, *[t[rows, :] for t in rope_refs])
            ks[rows, :] = k.astype(BF16)
            vs[rows, :] = v.astype(BF16)
            return 0

        lax.fori_loop(0, nq, prep_q, 0)
        lax.fori_loop(0, nk, prep_kv, 0)

        def kv_loop(j, _):
            c0 = pl.multiple_of(j * TK, TK)
            kt = ks[pl.ds(c0, TK), :]
            vt = vs[pl.ds(c0, TK), :]
            res = []
            for hh in range(nh):
                hmask = (lane >= HEAD_DIM * hh) & (lane < HEAD_DIM * (hh + 1))
                kh = jnp.where(hmask, kt, jnp.zeros_like(kt)) if pair else kt
                vh = jnp.where(hmask, vt, jnp.zeros_like(vt)) if pair else vt

                def q_loop(i, carry, kh=kh, vh=vh, hh=hh, hmask=hmask):
                    dk, dv, dneg = carry
                    r0 = pl.multiple_of(i * TQ, TQ)
                    rows = pl.ds(r0, TQ)
                    q = qs[rows, :]
                    dot = dos[rows, :]
                    lse_i = lse_ref[0, rows, hh * HEAD_DIM:hh * HEAD_DIM + 1]
                    delta_i = delta_s[rows, hh * HEAD_DIM:hh * HEAD_DIM + 1]
                    s = lax.dot_general(q, kh, (((1,), (1,)), ((), ())), preferred_element_type=F32)
                    if s_scale is not None:
                        s = s * s_scale
                    if has_bias:
                        s = s + negc_ref[0, 0, pl.ds(hh, 1), pl.ds(c0, TK)]
                    if pair:
                        mult = mask_ref[i - j]
                        s = jnp.where(mult > 0.0, s, NEG_INF)
                    p = jnp.exp(s - lse_i)
                    if pair:
                        p = p * mult
                    dv = dv + lax.dot_general(p.astype(BF16), dot, (((0,), (0,)), ((), ())),
                                              preferred_element_type=F32)
                    dp = lax.dot_general(dot, vh, (((1,), (1,)), ((), ())), preferred_element_type=F32)
                    ds = p * (dp - delta_i)
                    if has_bias:
                        dneg = dneg + jnp.sum(ds, axis=0, keepdims=True)
                        drow_acc[rows, :] += jnp.where(hmask, jnp.sum(ds, axis=1, keepdims=True), 0.0)
                    if s_scale is not None:
                        ds = ds * s_scale
                    dsb = ds.astype(BF16)
                    dk = dk + lax.dot_general(dsb, q, (((0,), (0,)), ((), ())), preferred_element_type=F32)
                    dq = jnp.dot(dsb, kh, preferred_element_type=F32)
                    dq_acc[rows, :] += dq
                    return dk, dv, dneg

                init = (jnp.zeros((TK, LANES), F32), jnp.zeros((TK, LANES), F32), jnp.zeros((1, TK), F32))
                dk, dv, dneg = lax.fori_loop(j if pair else 0, nq, q_loop, init)
                if has_bias:
                    dnegc_ref[0, 0, pl.ds(hh, 1), pl.ds(c0, TK)] = dneg
                res.append((dk, dv))
            if pair:
                dk = jnp.where(lane < HEAD_DIM, res[0][0], res[1][0])
                dv = jnp.where(lane < HEAD_DIM, res[0][1], res[1][1])
                if has_rope:
                    dk = _rope_bwd(dk, *[t[pl.ds(c0, TK), :] for t in rope_refs])
                dqkv_ref[0, pl.ds(c0, TK), LANES:2 * LANES] = dk.astype(BF16)
                dqkv_ref[0, pl.ds(c0, TK), 2 * LANES:3 * LANES] = dv.astype(BF16)
            else:
                dk_ref[0, pl.ds(c0, TK), :] = res[0][0].astype(BF16)
                dv_ref[0, pl.ds(c0, TK), :] = res[0][1].astype(BF16)
            return 0

        lax.fori_loop(0, nk, kv_loop, 0)

        def fin_q(n, _):
            r0 = pl.multiple_of(n * TQ, TQ)
            rows = pl.ds(r0, TQ)
            dq = dq_acc[rows, :]
            if q_fold is not None:
                dq = dq * q_fold
            if has_rope:
                dq = _rope_bwd(dq, *[t[rows, :] for t in rope_refs])
            if pair:
                dqkv_ref[0, rows, 0:LANES] = dq.astype(BF16)
            else:
                dq_ref[0, rows, :] = dq.astype(BF16)
            if has_bias:
                drow_ref[0, rows, :] = drow_acc[rows, :]
            return 0

        lax.fori_loop(0, nq, fin_q, 0)

    ins, in_specs = [], []
    if pair:
        ins.append(src)
        in_specs.append(pl.BlockSpec((1, S, PAIR_W), lambda b, h: (b, 0, col0 // PAIR_W + h)))
    else:
        ins += [src, kv, kv]
        in_specs += [pl.BlockSpec((1, S, LANES), lambda b, h: (b, 0, col0 // LANES + h)),
                     pl.BlockSpec((1, MEM_LEN, LANES), lambda b, h: (b, 0, h)),
                     pl.BlockSpec((1, MEM_LEN, LANES), lambda b, h: (b, 0, MEM_HEADS + h))]
    row_spec = pl.BlockSpec((1, S, LANES), lambda b, h: (b, 0, h))
    ins += [do, o, lse]
    in_specs += [row_spec] * 3
    if has_bias:
        ins.append(negc)
        in_specs.append(pl.BlockSpec((1, 1, 2, S), lambda b, h: (b, h, 0, 0)))
    if pair:
        ins.append(mask)
        in_specs.append(pl.BlockSpec(mask.shape, lambda b, h: (0, 0, 0)))
    if has_rope:
        ins += list(rope)
        in_specs += [pl.BlockSpec((S, LANES), lambda b, h: (0, 0))] * 3
    W = n_blocks * LANES
    if pair:
        out_specs = [pl.BlockSpec((1, S, PAIR_W), lambda b, h: (b, 0, h))]
        out_shape = [jax.ShapeDtypeStruct((B, S, 3 * W), BF16)]
        if has_bias:
            out_specs.append(pl.BlockSpec((1, 1, 2, S), lambda b, h: (b, h, 0, 0)))
            out_shape.append(jax.ShapeDtypeStruct((B, LANES // 2, 2, S), F32))
            out_specs.append(row_spec)
            out_shape.append(jax.ShapeDtypeStruct((B, S, W), F32))
    else:
        kv_spec = pl.BlockSpec((1, MEM_LEN, LANES), lambda b, h: (b, 0, h))
        out_specs = [row_spec, kv_spec, kv_spec]
        out_shape = [jax.ShapeDtypeStruct((B, S, W), BF16)] + [jax.ShapeDtypeStruct((B, MEM_LEN, W), BF16)] * 2
    return pl.pallas_call(
        body, name=kind + "_attn_bwd", grid=(B, n_blocks),
        in_specs=in_specs, out_specs=out_specs, out_shape=out_shape,
        scratch_shapes=[pltpu.VMEM((S, LANES), BF16), pltpu.VMEM((Sk, LANES), BF16), pltpu.VMEM((Sk, LANES), BF16),
                        pltpu.VMEM((S, LANES), BF16), pltpu.VMEM((S, LANES), F32), pltpu.VMEM((S, LANES), F32)]
        + ([pltpu.VMEM((S, LANES), F32)] if has_bias else []),
        compiler_params=_params(("arbitrary", "arbitrary")),
    )(*ins)


def _log_masks(S, kind):
    nd = 1 if kind == "causal" else S // TQ
    a = np.arange(TQ)[:, None]
    b = np.arange(TK)[None, :]
    out = np.zeros((nd, TQ, TK), np.float32)
    for d in range(nd):
        delta = d * TQ + a - b
        if kind == "causal":
            m = (delta >= 0).astype(np.float64)
        else:
            m = sum(((delta >= 0) & (delta % dil == 0) & (delta <= w)).astype(np.float64) for w, dil in DILATIONS)
        out[d] = np.where(m > 0, np.log(np.maximum(m, 1.0)), NEG_INF)
    return jnp.asarray(out)


def _attn_setup(kind):
    pair = kind != "mem"
    e_dim = HEAD_DIM if pair else MEM_HEAD_DIM
    q_fold, s_scale = _scale_parts(1.0 / math.sqrt(e_dim))
    return dict(pair=pair, col0={"fox": P_FOX, "dil": P_DIL, "mem": P_MQ}[kind],
                n_blocks=FOX_HEADS // 2 if pair else MEM_HEADS, q_fold=q_fold, s_scale=s_scale,
                nh=2 if pair else 1)


def _attn_inputs(kind, src, S, negc, mask, rope, kv, extra):
    cfg = _attn_setup(kind)
    col0 = cfg["col0"]
    ins, in_specs = [], []
    if cfg["pair"]:
        ins.append(src)
        in_specs.append(pl.BlockSpec((1, S, PAIR_W), lambda b, h: (b, 0, col0 // PAIR_W + h)))
    else:
        ins += [src, kv, kv]
        in_specs += [pl.BlockSpec((1, S, LANES), lambda b, h: (b, 0, col0 // LANES + h)),
                     pl.BlockSpec((1, MEM_LEN, LANES), lambda b, h: (b, 0, h)),
                     pl.BlockSpec((1, MEM_LEN, LANES), lambda b, h: (b, 0, MEM_HEADS + h))]
    ins += list(extra)
    in_specs += [pl.BlockSpec((1, S, LANES), lambda b, h: (b, 0, h))] * len(extra)
    if negc is not None:
        ins.append(negc)
        in_specs.append(pl.BlockSpec((1, 1, 2, S), lambda b, h: (b, h, 0, 0)))
    if mask is not None:
        ins.append(mask)
        in_specs.append(pl.BlockSpec(mask.shape, lambda b, h: (0, 0, 0)))
    if rope is not None:
        ins += list(rope)
        in_specs += [pl.BlockSpec((S, LANES), lambda b, h: (0, 0))] * 3
    return ins, in_specs


def _prep_rows(cfg, rope_refs, lane, load_q, load_kv, qs2, ks, vs, S, Sk):
    nh = cfg["nh"]
    R = nh * TQ

    def prep_q(n, _):
        rows = pl.ds(pl.multiple_of(n * TQ, TQ), TQ)
        q = load_q(rows)
        if rope_refs is not None:
            q = _rope(q, *[t[rows, :] for t in rope_refs])
        if cfg["q_fold"] is not None:
            q = q * cfg["q_fold"]
        _store_stacked(cfg, lane, qs2, n, q.astype(BF16))
        return 0

    def prep_kv(n, _):
        rows = pl.ds(pl.multiple_of(n * TK, TK), TK)
        k, v = load_kv(rows)
        if rope_refs is not None:
            k = _rope(k, *[t[rows, :] for t in rope_refs])
        ks[rows, :] = k.astype(BF16)
        vs[rows, :] = v.astype(BF16)
        return 0

    lax.fori_loop(0, S // TQ, prep_q, 0)
    lax.fori_loop(0, Sk // TK, prep_kv, 0)


def _store_stacked(cfg, lane, dst, n, val):
    nh = cfg["nh"]
    R = nh * TQ
    if nh == 1:
        dst[pl.ds(pl.multiple_of(n * R, R), TQ), :] = val
        return
    for hh in range(nh):
        hmask = (lane >= HEAD_DIM * hh) & (lane < HEAD_DIM * (hh + 1))
        dst[pl.ds(pl.multiple_of(n * R + hh * TQ, TQ), TQ), :] = jnp.where(hmask, val, jnp.zeros_like(val))


def _cat(parts, axis):
    return parts[0] if len(parts) == 1 else jnp.concatenate(parts, axis=axis)


def attn_fwd2(kind, src, S, *, negc=None, mask=None, rope=None, kv=None):
    B = src.shape[0]
    cfg = _attn_setup(kind)
    pair, nh, s_scale = cfg["pair"], cfg["nh"], cfg["s_scale"]
    Sk = S if pair else MEM_LEN
    has_bias, has_rope = negc is not None, rope is not None
    R = nh * TQ

    def body(*refs):
        refs = list(refs)
        if pair:
            qkv_ref = refs.pop(0)
        else:
            q_ref, k_ref, v_ref = refs.pop(0), refs.pop(0), refs.pop(0)
        negc_ref = refs.pop(0) if has_bias else None
        mask_ref = refs.pop(0) if mask is not None else None
        rope_refs = [refs.pop(0) for _ in range(3)] if has_rope else None
        o_ref, lse_ref, qs2, ks, vs = refs
        lane = lax.broadcasted_iota(jnp.int32, (1, LANES), 1)

        if pair:
            load_q = lambda rows: qkv_ref[0, rows, 0:LANES]
            load_kv = lambda rows: (qkv_ref[0, rows, LANES:2 * LANES], qkv_ref[0, rows, 2 * LANES:3 * LANES])
        else:
            load_q = lambda rows: q_ref[0, rows, :]
            load_kv = lambda rows: (k_ref[0, rows, :], v_ref[0, rows, :])
        _prep_rows(cfg, rope_refs, lane, load_q, load_kv, qs2, ks, vs, S, Sk)

        def q_loop(i, _):
            q2 = qs2[pl.ds(pl.multiple_of(i * R, R), R), :]

            def step(j, carry, midx):
                ms, ls, acc = carry
                c0 = pl.multiple_of(j * TK, TK)
                k = ks[pl.ds(c0, TK), :]
                v = vs[pl.ds(c0, TK), :]
                s2 = lax.dot_general(q2, k, (((1,), (1,)), ((), ())), preferred_element_type=F32)
                if s_scale is not None:
                    s2 = s2 * s_scale
                new_m, new_l, ps, alphas = [], [], [], []
                for hh in range(nh):
                    s = s2[hh * TQ:(hh + 1) * TQ]
                    if has_bias:
                        s = s + negc_ref[0, 0, pl.ds(hh, 1), pl.ds(c0, TK)]
                    if midx is not None:
                        s = s + mask_ref[midx]
                    m_new = jnp.maximum(ms[hh], jnp.max(s, axis=1, keepdims=True))
                    p = jnp.exp(s - m_new)
                    alpha = jnp.exp(ms[hh] - m_new)
                    new_l.append(alpha * ls[hh] + jnp.sum(p, axis=1, keepdims=True))
                    new_m.append(m_new)
                    ps.append(p.astype(BF16))
                    alphas.append(alpha)
                acc = acc * _cat(alphas, 0) + jnp.dot(_cat(ps, 0), v, preferred_element_type=F32)
                return tuple(new_m), tuple(new_l), acc

            init = (tuple(jnp.full((TQ, 1), NEG_INF, F32) for _ in range(nh)),
                    tuple(jnp.zeros((TQ, 1), F32) for _ in range(nh)), jnp.zeros((R, LANES), F32))
            if kind == "fox":
                carry = lax.fori_loop(0, i, lambda j, c: step(j, c, None), init)
                carry = step(i, carry, 0)
            elif kind == "dil":
                carry = lax.fori_loop(0, i + 1, lambda j, c: step(j, c, i - j), init)
            else:
                carry = lax.fori_loop(0, Sk // TK, lambda j, c: step(j, c, None), init)
            ms, ls, acc = carry
            outs = [acc[hh * TQ:(hh + 1) * TQ] / ls[hh] for hh in range(nh)]
            lses = [ms[hh] + jnp.log(ls[hh]) for hh in range(nh)]
            rows = pl.ds(pl.multiple_of(i * TQ, TQ), TQ)
            if pair:
                o_ref[0, rows, :] = jnp.where(lane < HEAD_DIM, outs[0], outs[1])
                lse_ref[0, rows, :] = jnp.where(lane < HEAD_DIM, lses[0], lses[1])
            else:
                o_ref[0, rows, :] = outs[0]
                lse_ref[0, rows, :] = jnp.broadcast_to(lses[0], (TQ, LANES))
            return 0

        lax.fori_loop(0, S // TQ, q_loop, 0)

    ins, in_specs = _attn_inputs(kind, src, S, negc, mask, rope, kv, ())
    W = cfg["n_blocks"] * LANES
    out_spec = pl.BlockSpec((1, S, LANES), lambda b, h: (b, 0, h))
    return pl.pallas_call(
        body, name=kind + "_attn_fwd", grid=(B, cfg["n_blocks"]),
        in_specs=in_specs, out_specs=[out_spec, out_spec],
        out_shape=[jax.ShapeDtypeStruct((B, S, W), F32)] * 2,
        scratch_shapes=[pltpu.VMEM((nh * S, LANES), BF16), pltpu.VMEM((Sk, LANES), BF16),
                        pltpu.VMEM((Sk, LANES), BF16)],
        compiler_params=_params(("arbitrary", "arbitrary")),
    )(*ins)


def attn_bwd2(kind, src, do, o, lse, S, *, negc=None, mask=None, rope=None, kv=None):
    B = src.shape[0]
    cfg = _attn_setup(kind)
    pair, nh, s_scale, q_fold = cfg["pair"], cfg["nh"], cfg["s_scale"], cfg["q_fold"]
    Sk = S if pair else MEM_LEN
    has_bias, has_rope = negc is not None, rope is not None
    R = nh * TQ
    nq, nk = S // TQ, Sk // TK

    def body(*refs):
        refs = list(refs)
        if pair:
            qkv_ref = refs.pop(0)
        else:
            q_ref, k_ref, v_ref = refs.pop(0), refs.pop(0), refs.pop(0)
        do_ref, o_ref, lse_ref = refs.pop(0), refs.pop(0), refs.pop(0)
        negc_ref = refs.pop(0) if has_bias else None
        mask_ref = refs.pop(0) if mask is not None else None
        rope_refs = [refs.pop(0) for _ in range(3)] if has_rope else None
        if pair:
            dqkv_ref = refs.pop(0)
            dnegc_ref = refs.pop(0) if has_bias else None
            drow_ref = refs.pop(0) if has_bias else None
        else:
            dq_ref, dk_ref, dv_ref = refs.pop(0), refs.pop(0), refs.pop(0)
        qs2, ks, vs, dos2, lse_s, delta_s, dk_acc, dv_acc = refs[:8]
        dneg_acc = refs[8] if has_bias else None
        lane = lax.broadcasted_iota(jnp.int32, (1, LANES), 1)

        if pair:
            load_q = lambda rows: qkv_ref[0, rows, 0:LANES]
            load_kv = lambda rows: (qkv_ref[0, rows, LANES:2 * LANES], qkv_ref[0, rows, 2 * LANES:3 * LANES])
        else:
            load_q = lambda rows: q_ref[0, rows, :]
            load_kv = lambda rows: (k_ref[0, rows, :], v_ref[0, rows, :])
        _prep_rows(cfg, rope_refs, lane, load_q, load_kv, qs2, ks, vs, S, Sk)

        def prep_do(n, _):
            rows = pl.ds(pl.multiple_of(n * TQ, TQ), TQ)
            dob = do_ref[0, rows, :].astype(BF16)
            _store_stacked(cfg, lane, dos2, n, dob)
            prod = dob.astype(F32) * o_ref[0, rows, :]
            lse_blk = lse_ref[0, rows, :]
            for hh in range(nh):
                dst = pl.ds(pl.multiple_of(n * R + hh * TQ, TQ), TQ)
                if pair:
                    hmask = (lane >= HEAD_DIM * hh) & (lane < HEAD_DIM * (hh + 1))
                    d = jnp.sum(jnp.where(hmask, prod, 0.0), axis=1, keepdims=True)
                    lse_s[dst, :] = jnp.broadcast_to(lse_blk[:, hh * HEAD_DIM:hh * HEAD_DIM + 1], (TQ, LANES))
                else:
                    d = jnp.sum(prod, axis=1, keepdims=True)
                    lse_s[dst, :] = lse_blk
                delta_s[dst, :] = jnp.broadcast_to(d, (TQ, LANES))
            return 0

        def zero_kv(n, _):
            rows = pl.ds(pl.multiple_of(n * TK, TK), TK)
            dk_acc[rows, :] = jnp.zeros((TK, LANES), F32)
            dv_acc[rows, :] = jnp.zeros((TK, LANES), F32)
            return 0

        lax.fori_loop(0, nq, prep_do, 0)
        lax.fori_loop(0, nk, zero_kv, 0)
        if has_bias:
            dneg_acc[...] = jnp.zeros(dneg_acc.shape, F32)

        def q_loop(i, _):
            rows2 = pl.ds(pl.multiple_of(i * R, R), R)
            q2 = qs2[rows2, :]
            do2 = dos2[rows2, :]
            lse2 = lse_s[rows2, :]
            delta2 = delta_s[rows2, :]
            wide = lambda t: jnp.concatenate([t] * (TK // LANES), axis=1)

            def step(j, carry, midx):
                dq2, drow = carry
                c0 = pl.multiple_of(j * TK, TK)
                kcols = pl.ds(c0, TK)
                k = ks[kcols, :]
                v = vs[kcols, :]
                s2 = lax.dot_general(q2, k, (((1,), (1,)), ((), ())), preferred_element_type=F32)
                if s_scale is not None:
                    s2 = s2 * s_scale
                if has_bias or midx is not None:
                    halves = []
                    for hh in range(nh):
                        s = s2[hh * TQ:(hh + 1) * TQ]
                        if has_bias:
                            s = s + negc_ref[0, 0, pl.ds(hh, 1), kcols]
                        if midx is not None:
                            s = s + mask_ref[midx]
                        halves.append(s)
                    s2 = _cat(halves, 0)
                p2 = jnp.exp(s2 - wide(lse2))
                dp2 = lax.dot_general(do2, v, (((1,), (1,)), ((), ())), preferred_element_type=F32)
                ds2 = p2 * (dp2 - wide(delta2))
                if has_bias:
                    drow = drow + jnp.sum(ds2, axis=1, keepdims=True)
                    for hh in range(nh):
                        dneg_acc[pl.ds(hh, 1), kcols] += jnp.sum(ds2[hh * TQ:(hh + 1) * TQ], axis=0, keepdims=True)
                if s_scale is not None:
                    ds2 = ds2 * s_scale
                dsb = ds2.astype(BF16)
                dv_acc[kcols, :] += lax.dot_general(p2.astype(BF16), do2, (((0,), (0,)), ((), ())),
                                                    preferred_element_type=F32)
                dk_acc[kcols, :] += lax.dot_general(dsb, q2, (((0,), (0,)), ((), ())), preferred_element_type=F32)
                dq2 = dq2 + jnp.dot(dsb, k, preferred_element_type=F32)
                return dq2, drow

            init = (jnp.zeros((R, LANES), F32), jnp.zeros((R, 1), F32))
            if kind == "fox":
                carry = lax.fori_loop(0, i, lambda j, c: step(j, c, None), init)
                carry = step(i, carry, 0)
            elif kind == "dil":
                carry = lax.fori_loop(0, i + 1, lambda j, c: step(j, c, i - j), init)
            else:
                carry = lax.fori_loop(0, nk, lambda j, c: step(j, c, None), init)
            dq2, drow = carry
            rows = pl.ds(pl.multiple_of(i * TQ, TQ), TQ)
            dq = jnp.where(lane < HEAD_DIM, dq2[0:TQ], dq2[TQ:2 * TQ]) if pair else dq2
            if q_fold is not None:
                dq = dq * q_fold
            if has_rope:
                dq = _rope_bwd(dq, *[t[rows, :] for t in rope_refs])
            if pair:
                dqkv_ref[0, rows, 0:LANES] = dq.astype(BF16)
            else:
                dq_ref[0, rows, :] = dq.astype(BF16)
            if has_bias:
                drow_ref[0, rows, :] = jnp.where(lane < HEAD_DIM, drow[0:TQ], drow[TQ:2 * TQ])
            return 0

        lax.fori_loop(0, nq, q_loop, 0)

        def fin_kv(n, _):
            rows = pl.ds(pl.multiple_of(n * TK, TK), TK)
            dk = dk_acc[rows, :]
            if has_rope:
                dk = _rope_bwd(dk, *[t[rows, :] for t in rope_refs])
            if pair:
                dqkv_ref[0, rows, LANES:2 * LANES] = dk.astype(BF16)
                dqkv_ref[0, rows, 2 * LANES:3 * LANES] = dv_acc[rows, :].astype(BF16)
            else:
                dk_ref[0, rows, :] = dk.astype(BF16)
                dv_ref[0, rows, :] = dv_acc[rows, :].astype(BF16)
            return 0

        lax.fori_loop(0, nk, fin_kv, 0)
        if has_bias:
            dnegc_ref[0, 0] = dneg_acc[...]

    ins, in_specs = _attn_inputs(kind, src, S, negc, mask, rope, kv, (do, o, lse))
    W = cfg["n_blocks"] * LANES
    row_spec = pl.BlockSpec((1, S, LANES), lambda b, h: (b, 0, h))
    if pair:
        out_specs = [pl.BlockSpec((1, S, PAIR_W), lambda b, h: (b, 0, h))]
        out_shape = [jax.ShapeDtypeStruct((B, S, 3 * W), BF16)]
        if has_bias:
            out_specs += [pl.BlockSpec((1, 1, 2, S), lambda b, h: (b, h, 0, 0)), row_spec]
            out_shape += [jax.ShapeDtypeStruct((B, LANES // 2, 2, S), F32), jax.ShapeDtypeStruct((B, S, W), F32)]
    else:
        kv_spec = pl.BlockSpec((1, MEM_LEN, LANES), lambda b, h: (b, 0, h))
        out_specs = [row_spec, kv_spec, kv_spec]
        out_shape = [jax.ShapeDtypeStruct((B, S, W), BF16)] + [jax.ShapeDtypeStruct((B, MEM_LEN, W), BF16)] * 2
    scratch = [pltpu.VMEM((nh * S, LANES), BF16), pltpu.VMEM((Sk, LANES), BF16), pltpu.VMEM((Sk, LANES), BF16),
               pltpu.VMEM((nh * S, LANES), BF16), pltpu.VMEM((nh * S, LANES), F32), pltpu.VMEM((nh * S, LANES), F32),
               pltpu.VMEM((Sk, LANES), F32), pltpu.VMEM((Sk, LANES), F32)]
    if has_bias:
        scratch.append(pltpu.VMEM((2, S), F32))
    return pl.pallas_call(
        body, name=kind + "_attn_bwd", grid=(B, cfg["n_blocks"]),
        in_specs=in_specs, out_specs=out_specs, out_shape=out_shape, scratch_shapes=scratch,
        compiler_params=_params(("arbitrary", "arbitrary")),
    )(*ins)


def _log_masks_t(S, kind):
    return jnp.swapaxes(_log_masks(S, kind), 1, 2)


def _head_rows(hh, pair):
    row = lax.broadcasted_iota(jnp.int32, (LANES, 1), 0)
    if not pair:
        return row >= 0
    return (row >= HEAD_DIM * hh) & (row < HEAD_DIM * (hh + 1))


def _attn_t_inputs(kind, src, S, negc_cols, mask, rope, kv):
    cfg = _attn_setup(kind)
    col0 = cfg["col0"]
    ins, in_specs = [], []
    if cfg["pair"]:
        ins.append(src)
        in_specs.append(pl.BlockSpec((1, S, PAIR_W), lambda b, h: (b, 0, col0 // PAIR_W + h)))
    else:
        ins += [src, kv, kv]
        in_specs += [pl.BlockSpec((1, S, LANES), lambda b, h: (b, 0, col0 // LANES + h)),
                     pl.BlockSpec((1, MEM_LEN, LANES), lambda b, h: (b, 0, h)),
                     pl.BlockSpec((1, MEM_LEN, LANES), lambda b, h: (b, 0, MEM_HEADS + h))]
    if negc_cols is not None:
        ins.append(negc_cols)
        in_specs.append(pl.BlockSpec((1, S, LANES), lambda b, h: (b, 0, 0)))
    if mask is not None:
        ins.append(mask)
        in_specs.append(pl.BlockSpec(mask.shape, lambda b, h: (0, 0, 0)))
    if rope is not None:
        ins += list(rope)
        in_specs += [pl.BlockSpec((S, LANES), lambda b, h: (0, 0))] * 3
    return ins, in_specs


def _attn_t_prep(cfg, refs, S, Sk, *, qT2s, ks, q2s=None, vs=None, vTs=None, kTs=None, nb=None):
    pair, nh = cfg["pair"], cfg["nh"]
    lane = lax.broadcasted_iota(jnp.int32, (1, LANES), 1)
    rope_refs = refs["rope"]

    def prep_q(n, _):
        rows = pl.ds(pl.multiple_of(n * TQ, TQ), TQ)
        q = refs["load_q"](rows)
        if rope_refs is not None:
            q = _rope(q, *[t[rows, :] for t in rope_refs])
        if cfg["q_fold"] is not None:
            q = q * cfg["q_fold"]
        qb = q.astype(BF16)
        if q2s is not None:
            _store_stacked(cfg, lane, q2s, n, qb)
        qtb = qb.astype(F32).T.astype(BF16)
        for hh in range(nh):
            qT2s[n, :, hh * TQ:(hh + 1) * TQ] = jnp.where(_head_rows(hh, pair), qtb, jnp.zeros_like(qtb))
        return 0

    def prep_kv(n, _):
        rows = pl.ds(pl.multiple_of(n * TK, TK), TK)
        k, v = refs["load_kv"](rows)
        if rope_refs is not None:
            k = _rope(k, *[t[rows, :] for t in rope_refs])
        kb = k.astype(BF16)
        vb = v.astype(BF16)
        ks[rows, :] = kb
        if vs is not None:
            vs[rows, :] = vb
        if vTs is not None:
            vTs[n] = vb.astype(F32).T.astype(BF16)
        if kTs is not None:
            kTs[n] = kb.astype(F32).T.astype(BF16)
        if nb is not None:
            blk = refs["negc"][0, rows, :]
            for hh in range(nh):
                h = 2 * refs["block"] + hh
                col = jnp.sum(jnp.where(lane == h, blk, 0.0), axis=1, keepdims=True)
                nb[hh, rows, :] = jnp.broadcast_to(col, (TK, LANES))
        return 0

    lax.fori_loop(0, S // TQ, prep_q, 0)
    lax.fori_loop(0, Sk // TK, prep_kv, 0)


def _raw_scores_t(cfg, k, qT2):
    sT = jnp.dot(k, qT2, preferred_element_type=F32)
    if cfg["s_scale"] is not None:
        sT = sT * cfg["s_scale"]
    return sT


def _bias_mask_t(cfg, sT, nb, mask_ref, kc, midx):
    nh = cfg["nh"]
    if nb is None and midx is None:
        return sT
    parts = []
    for hh in range(nh):
        t = sT[:, hh * TQ:(hh + 1) * TQ]
        if nb is not None:
            t = t + jnp.concatenate([nb[hh, kc, :]] * (TQ // LANES), axis=1)
        if midx is not None:
            t = t + mask_ref[midx]
        parts.append(t)
    return _cat(parts, 1)


def _kv_plan(kind, i, nk):
    if kind == "fox":
        return i, (lambda j: None), 0
    if kind == "dil":
        return i, (lambda j: i - j), 0
    return nk - 1, (lambda j: None), None


def attn_fwd3(kind, src, S, *, negc_cols=None, mask=None, rope=None, kv=None):
    B = src.shape[0]
    cfg = _attn_setup(kind)
    pair, nh = cfg["pair"], cfg["nh"]
    Sk = S if pair else MEM_LEN
    has_bias, has_rope = negc_cols is not None, rope is not None
    R = nh * TQ
    nq, nk = S // TQ, Sk // TK

    def body(*refs):
        refs = list(refs)
        if pair:
            qkv_ref = refs.pop(0)
            load_q = lambda rows: qkv_ref[0, rows, 0:LANES]
            load_kv = lambda rows: (qkv_ref[0, rows, LANES:2 * LANES], qkv_ref[0, rows, 2 * LANES:3 * LANES])
        else:
            q_ref, k_ref, v_ref = refs.pop(0), refs.pop(0), refs.pop(0)
            load_q = lambda rows: q_ref[0, rows, :]
            load_kv = lambda rows: (k_ref[0, rows, :], v_ref[0, rows, :])
        negc_ref = refs.pop(0) if has_bias else None
        mask_ref = refs.pop(0) if mask is not None else None
        rope_refs = [refs.pop(0) for _ in range(3)] if has_rope else None
        o_ref, lse_ref, qT2s, ks, vTs = refs[:5]
        nb = refs[5] if has_bias else None
        _attn_t_prep(cfg, dict(load_q=load_q, load_kv=load_kv, rope=rope_refs, negc=negc_ref,
                               block=pl.program_id(1)), S, Sk,
                     qT2s=qT2s, ks=ks, vTs=vTs, nb=nb)

        def q_loop(i, _):
            qT2 = qT2s[i]

            last, mask_of, mask_last = _kv_plan(kind, i, nk)

            def cols(j):
                return pl.ds(pl.multiple_of(j * TK, TK), TK)

            def scores(j):
                return _raw_scores_t(cfg, ks[cols(j), :], qT2)

            def soft(s_raw, j, midx, m, l):
                sT = _bias_mask_t(cfg, s_raw, nb, mask_ref, cols(j), midx)
                m_new = jnp.maximum(m, jnp.max(sT, axis=0, keepdims=True))
                p = jnp.exp(sT - m_new)
                alpha = jnp.exp(m - m_new)
                return m_new, alpha * l + jnp.sum(p, axis=0, keepdims=True), alpha, p.astype(BF16)

            def pv(j, p):
                return jnp.dot(vTs[j], p, preferred_element_type=F32)

            def body(j, carry):
                s_cur, p_prev, m, l, accT = carry
                pv_prev = pv(jnp.maximum(j - 1, 0), p_prev)
                s_next = scores(j + 1)
                m, l, alpha, p = soft(s_cur, j, mask_of(j), m, l)
                return s_next, p, m, l, (accT + pv_prev) * alpha

            init = (scores(0), jnp.zeros((TK, R), BF16), jnp.full((1, R), NEG_INF, F32), jnp.zeros((1, R), F32),
                    jnp.zeros((LANES, R), F32))
            s_cur, p_prev, m, l, accT = lax.fori_loop(0, last, body, init)
            pv_prev = pv(jnp.maximum(last - 1, 0), p_prev)
            m, l, alpha, p = soft(s_cur, last, mask_last, m, l)
            accT = (accT + pv_prev) * alpha + pv(last, p)
            oT2 = accT / l
            oT = jnp.where(_head_rows(0, True), oT2[:, 0:TQ], oT2[:, TQ:2 * TQ]) if pair else oT2
            o_ref[0, pl.ds(pl.multiple_of(i * TQ, TQ), TQ), :] = oT.T
            lse_ref[0, 0, pl.ds(i, 1), :] = m + jnp.log(l)
            return 0

        lax.fori_loop(0, nq, q_loop, 0)

    ins, in_specs = _attn_t_inputs(kind, src, S, negc_cols, mask, rope, kv)
    W = cfg["n_blocks"] * LANES
    scratch = [pltpu.VMEM((nq, LANES, R), BF16), pltpu.VMEM((Sk, LANES), BF16), pltpu.VMEM((nk, LANES, TK), BF16)]
    if has_bias:
        scratch.append(pltpu.VMEM((nh, Sk, LANES), F32))
    return pl.pallas_call(
        body, name=kind + "_attn_fwd", grid=(B, cfg["n_blocks"]),
        in_specs=in_specs,
        out_specs=[pl.BlockSpec((1, S, LANES), lambda b, h: (b, 0, h)),
                   pl.BlockSpec((1, 1, nq, R), lambda b, h: (b, h, 0, 0))],
        out_shape=[jax.ShapeDtypeStruct((B, S, W), F32), jax.ShapeDtypeStruct((B, cfg["n_blocks"], nq, R), F32)],
        scratch_shapes=scratch,
        compiler_params=_params(("arbitrary", "arbitrary")),
    )(*ins)


def attn_bwd3(kind, src, do, o, lse, S, *, negc_cols=None, mask=None, rope=None, kv=None):
    B = src.shape[0]
    cfg = _attn_setup(kind)
    pair, nh, s_scale, q_fold = cfg["pair"], cfg["nh"], cfg["s_scale"], cfg["q_fold"]
    Sk = S if pair else MEM_LEN
    has_bias, has_rope = negc_cols is not None, rope is not None
    R = nh * TQ
    nq, nk = S // TQ, Sk // TK

    def body(*refs):
        refs = list(refs)
        if pair:
            qkv_ref = refs.pop(0)
            load_q = lambda rows: qkv_ref[0, rows, 0:LANES]
            load_kv = lambda rows: (qkv_ref[0, rows, LANES:2 * LANES], qkv_ref[0, rows, 2 * LANES:3 * LANES])
        else:
            q_ref, k_ref, v_ref = refs.pop(0), refs.pop(0), refs.pop(0)
            load_q = lambda rows: q_ref[0, rows, :]
            load_kv = lambda rows: (k_ref[0, rows, :], v_ref[0, rows, :])
        negc_ref = refs.pop(0) if has_bias else None
        mask_ref = refs.pop(0) if mask is not None else None
        rope_refs = [refs.pop(0) for _ in range(3)] if has_rope else None
        do_ref, o_ref, lse_ref = refs.pop(0), refs.pop(0), refs.pop(0)
        if pair:
            dqkv_ref = refs.pop(0)
            dneg_ref = refs.pop(0) if has_bias else None
            drow_ref = refs.pop(0) if has_bias else None
        else:
            dq_ref, dk_ref, dv_ref = refs.pop(0), refs.pop(0), refs.pop(0)
        qT2s, ks, q2s, vs, kTs, doT2s, do2s, delta_s, dk_acc, dv_acc = refs[:10]
        nb, dneg_acc = (refs[10], refs[11]) if has_bias else (None, None)
        lane = lax.broadcasted_iota(jnp.int32, (1, LANES), 1)
        _attn_t_prep(cfg, dict(load_q=load_q, load_kv=load_kv, rope=rope_refs, negc=negc_ref,
                               block=pl.program_id(1)), S, Sk,
                     qT2s=qT2s, ks=ks, q2s=q2s, vs=vs, kTs=kTs, nb=nb)

        def prep_do(n, _):
            rows = pl.ds(pl.multiple_of(n * TQ, TQ), TQ)
            dob = do_ref[0, rows, :].astype(BF16)
            _store_stacked(cfg, lane, do2s, n, dob)
            doT = dob.astype(F32).T
            prodT = doT * o_ref[0, rows, :].T
            doTb = doT.astype(BF16)
            for hh in range(nh):
                hm = _head_rows(hh, pair)
                doT2s[n, :, hh * TQ:(hh + 1) * TQ] = jnp.where(hm, doTb, jnp.zeros_like(doTb))
                delta_s[pl.ds(n, 1), hh * TQ:(hh + 1) * TQ] = jnp.sum(jnp.where(hm, prodT, 0.0), axis=0, keepdims=True)
            return 0

        def zero_kv(n, _):
            rows = pl.ds(pl.multiple_of(n * TK, TK), TK)
            dk_acc[rows, :] = jnp.zeros((TK, LANES), F32)
            dv_acc[rows, :] = jnp.zeros((TK, LANES), F32)
            if has_bias:
                for hh in range(nh):
                    dneg_acc[hh, rows, :] = jnp.zeros((TK, LANES), F32)
            return 0

        lax.fori_loop(0, nq, prep_do, 0)
        lax.fori_loop(0, nk, zero_kv, 0)

        def q_loop(i, _):
            rows2 = pl.ds(pl.multiple_of(i * R, R), R)
            qT2 = qT2s[i]
            doT2 = doT2s[i]
            q2 = q2s[rows2, :]
            do2 = do2s[rows2, :]
            lse_i = lse_ref[0, 0, pl.ds(i, 1), :]
            delta_i = delta_s[pl.ds(i, 1), :]

            last, mask_of, mask_last = _kv_plan(kind, i, nk)

            def cols(j):
                return pl.ds(pl.multiple_of(j * TK, TK), TK)

            def first(j):
                kc = cols(j)
                return (_raw_scores_t(cfg, ks[kc, :], qT2),
                        jnp.dot(vs[kc, :], doT2, preferred_element_type=F32))

            def middle(s_raw, dpT, j, midx, drow):
                kc = cols(j)
                sT = _bias_mask_t(cfg, s_raw, nb, mask_ref, kc, midx)
                pT = jnp.exp(sT - lse_i)
                dsT = pT * (dpT - delta_i)
                if has_bias:
                    drow = drow + jnp.sum(dsT, axis=0, keepdims=True)
                    for hh in range(nh):
                        part = dsT[:, hh * TQ:hh * TQ + LANES]
                        for t in range(1, TQ // LANES):
                            part = part + dsT[:, hh * TQ + t * LANES:hh * TQ + (t + 1) * LANES]
                        dneg_acc[hh, kc, :] += part
                if s_scale is not None:
                    dsT = dsT * s_scale
                return pT.astype(BF16), dsT.astype(BF16), drow

            def second(j, pb, dsb, dqT2):
                kc = cols(j)
                dv_acc[kc, :] += jnp.dot(pb, do2, preferred_element_type=F32)
                dk_acc[kc, :] += jnp.dot(dsb, q2, preferred_element_type=F32)
                return dqT2 + jnp.dot(kTs[j], dsb, preferred_element_type=F32)

            def step(j, carry, midx):
                dqT2, drow = carry
                s_raw, dpT = first(j)
                pb, dsb, drow = middle(s_raw, dpT, j, midx, drow)
                return second(j, pb, dsb, dqT2), drow

            init = (jnp.zeros((LANES, R), F32), jnp.zeros((1, R), F32))
            carry = lax.fori_loop(0, last, lambda j, c: step(j, c, mask_of(j)), init)
            dqT2, drow = step(last, carry, mask_last)
            rows = pl.ds(pl.multiple_of(i * TQ, TQ), TQ)
            dqT = jnp.where(_head_rows(0, True), dqT2[:, 0:TQ], dqT2[:, TQ:2 * TQ]) if pair else dqT2
            dq = dqT.T
            if q_fold is not None:
                dq = dq * q_fold
            if has_rope:
                dq = _rope_bwd(dq, *[t[rows, :] for t in rope_refs])
            if pair:
                dqkv_ref[0, rows, 0:LANES] = dq.astype(BF16)
            else:
                dq_ref[0, rows, :] = dq.astype(BF16)
            if has_bias:
                drow_ref[0, 0, pl.ds(i, 1), :] = drow
            return 0

        lax.fori_loop(0, nq, q_loop, 0)

        def fin_kv(n, _):
            rows = pl.ds(pl.multiple_of(n * TK, TK), TK)
            dk = dk_acc[rows, :]
            if has_rope:
                dk = _rope_bwd(dk, *[t[rows, :] for t in rope_refs])
            if pair:
                dqkv_ref[0, rows, LANES:2 * LANES] = dk.astype(BF16)
                dqkv_ref[0, rows, 2 * LANES:3 * LANES] = dv_acc[rows, :].astype(BF16)
            else:
                dk_ref[0, rows, :] = dk.astype(BF16)
                dv_ref[0, rows, :] = dv_acc[rows, :].astype(BF16)
            if has_bias:
                x0 = jnp.sum(dneg_acc[0, rows, :], axis=1, keepdims=True)
                x1 = jnp.sum(dneg_acc[1, rows, :], axis=1, keepdims=True)
                dneg_ref[0, rows, :] = jnp.where(lane == 0, x0, jnp.where(lane == 1, x1, 0.0))
            return 0

        lax.fori_loop(0, nk, fin_kv, 0)

    ins, in_specs = _attn_t_inputs(kind, src, S, negc_cols, mask, rope, kv)
    row_spec = pl.BlockSpec((1, S, LANES), lambda b, h: (b, 0, h))
    vec_spec = pl.BlockSpec((1, 1, nq, R), lambda b, h: (b, h, 0, 0))
    ins += [do, o, lse]
    in_specs += [row_spec, row_spec, vec_spec]
    W = cfg["n_blocks"] * LANES
    if pair:
        out_specs = [pl.BlockSpec((1, S, PAIR_W), lambda b, h: (b, 0, h))]
        out_shape = [jax.ShapeDtypeStruct((B, S, 3 * W), BF16)]
        if has_bias:
            out_specs += [row_spec, vec_spec]
            out_shape += [jax.ShapeDtypeStruct((B, S, W), F32), jax.ShapeDtypeStruct((B, cfg["n_blocks"], nq, R), F32)]
    else:
        kv_spec = pl.BlockSpec((1, MEM_LEN, LANES), lambda b, h: (b, 0, h))
        out_specs = [row_spec, kv_spec, kv_spec]
        out_shape = [jax.ShapeDtypeStruct((B, S, W), BF16)] + [jax.ShapeDtypeStruct((B, MEM_LEN, W), BF16)] * 2
    scratch = [pltpu.VMEM((nq, LANES, R), BF16), pltpu.VMEM((Sk, LANES), BF16), pltpu.VMEM((nh * S, LANES), BF16),
               pltpu.VMEM((Sk, LANES), BF16), pltpu.VMEM((nk, LANES, TK), BF16), pltpu.VMEM((nq, LANES, R), BF16),
               pltpu.VMEM((nh * S, LANES), BF16), pltpu.VMEM((nq, R), F32),
               pltpu.VMEM((Sk, LANES), F32), pltpu.VMEM((Sk, LANES), F32)]
    if has_bias:
        scratch += [pltpu.VMEM((nh, Sk, LANES), F32), pltpu.VMEM((nh, Sk, LANES), F32)]
    return pl.pallas_call(
        body, name=kind + "_attn_bwd", grid=(B, cfg["n_blocks"]),
        in_specs=in_specs, out_specs=out_specs, out_shape=out_shape, scratch_shapes=scratch,
        compiler_params=_params(("arbitrary", "arbitrary")),
    )(*ins)


def _sigmoid(g):
    return 1.0 / (1.0 + jnp.exp(-g))


def out_fwd(proj, o_fox, o_dil, o_mem, w_out, x, target, gf, tm):
    T = x.shape[0]

    def body(fg_ref, dg_ref, mg_ref, of_ref, od_ref, om_ref, w_ref, x_ref, t_ref, gf_ref,
             y_ref, dx_ref, dxb_ref, sm_ref):
        parts = []
        for g_ref, o_ref in ((fg_ref, of_ref), (dg_ref, od_ref), (mg_ref, om_ref)):
            g = g_ref[...]
            parts.append((o_ref[...] * (g * _sigmoid(g))).astype(BF16))
        ymix = jnp.concatenate(parts, axis=1)
        y_ref[...] = ymix
        x2 = x_ref[...] + jnp.dot(ymix, w_ref[...], preferred_element_type=F32)
        r = lax.rsqrt(jnp.mean(x2 * x2, axis=-1, keepdims=True) + RMS_EPS)
        yn = x2 * r
        err = yn * gf_ref[...] - t_ref[...]
        loss = 0.5 * jnp.sum(jnp.sum(err * err, axis=-1, keepdims=True) / D_MODEL, axis=0, keepdims=True)
        dyf = err / D_MODEL
        dgf = jnp.sum(dyf * yn, axis=0, keepdims=True)
        dyn = dyf * gf_ref[...]
        dx2 = r * (dyn - yn * jnp.mean(dyn * yn, axis=-1, keepdims=True))
        dx_ref[...] = dx2
        dxb_ref[...] = dx2.astype(BF16)
        row = lax.broadcasted_iota(jnp.int32, (8, D_MODEL), 0)
        upd = jnp.where(row == 0, dgf, jnp.where(row == 1, loss, 0.0))

        @pl.when(pl.program_id(0) == 0)
        def _():
            sm_ref[...] = upd

        @pl.when(pl.program_id(0) != 0)
        def _():
            sm_ref[...] += upd

    def rows(w, col=0):
        return pl.BlockSpec((tm, w), lambda i: (i, col))

    return pl.pallas_call(
        body, name="out_fwd", grid=(T // tm,),
        in_specs=[rows(FOX_W, P_FG // FOX_W), rows(DIL_W, P_DG // DIL_W), rows(MEM_W, P_MG // MEM_W),
                  rows(FOX_W), rows(DIL_W), rows(MEM_W),
                  pl.BlockSpec((MIX_W, D_MODEL), lambda i: (0, 0)),
                  rows(D_MODEL), rows(D_MODEL), pl.BlockSpec((1, D_MODEL), lambda i: (0, 0))],
        out_specs=[rows(MIX_W), rows(D_MODEL), rows(D_MODEL), pl.BlockSpec((8, D_MODEL), lambda i: (0, 0))],
        out_shape=[jax.ShapeDtypeStruct((T, MIX_W), BF16), jax.ShapeDtypeStruct((T, D_MODEL), F32),
                   jax.ShapeDtypeStruct((T, D_MODEL), BF16), jax.ShapeDtypeStruct((8, D_MODEL), F32)],
        compiler_params=_params(("arbitrary",)),
    )(proj, proj, proj, o_fox, o_dil, o_mem, w_out, x, target, gf)


def out_bwd(proj, o_fox, o_dil, o_mem, w_out, dx2b, tm):
    T = dx2b.shape[0]

    def body(fg_ref, dg_ref, mg_ref, of_ref, od_ref, om_ref, w_ref, dx_ref,
             dof_ref, dod_ref, dom_ref, dfg_ref, ddg_ref, dmg_ref):
        dmix = lax.dot_general(dx_ref[...], w_ref[...], (((1,), (1,)), ((), ())), preferred_element_type=F32)
        col = 0
        for g_ref, o_ref, do_ref, dgate_ref in ((fg_ref, of_ref, dof_ref, dfg_ref), (dg_ref, od_ref, dod_ref, ddg_ref),
                                                 (mg_ref, om_ref, dom_ref, dmg_ref)):
            w = g_ref.shape[1]
            d = dmix[:, col:col + w]
            col += w
            g = g_ref[...]
            sg = _sigmoid(g)
            do_ref[...] = d * (g * sg)
            dgate_ref[...] = (d * o_ref[...] * (sg * (1.0 + g * (1.0 - sg)))).astype(BF16)

    def rows(w, col=0):
        return pl.BlockSpec((tm, w), lambda i: (i, col))

    return pl.pallas_call(
        body, name="out_bwd", grid=(T // tm,),
        in_specs=[rows(FOX_W, P_FG // FOX_W), rows(DIL_W, P_DG // DIL_W), rows(MEM_W, P_MG // MEM_W),
                  rows(FOX_W), rows(DIL_W), rows(MEM_W),
                  pl.BlockSpec((MIX_W, D_MODEL), lambda i: (0, 0)), rows(D_MODEL)],
        out_specs=[rows(FOX_W), rows(DIL_W), rows(MEM_W), rows(FOX_W), rows(DIL_W), rows(MEM_W)],
        out_shape=[jax.ShapeDtypeStruct((T, FOX_W), F32), jax.ShapeDtypeStruct((T, DIL_W), F32),
                   jax.ShapeDtypeStruct((T, MEM_W), F32), jax.ShapeDtypeStruct((T, FOX_W), BF16),
                   jax.ShapeDtypeStruct((T, DIL_W), BF16), jax.ShapeDtypeStruct((T, MEM_W), BF16)],
        compiler_params=_params(("arbitrary",)),
    )(proj, proj, proj, o_fox, o_dil, o_mem, w_out, dx2b)


def adamw(w, g, m, v, tr, name):
    lead = w.shape[:-2]
    R, C = w.shape[-2:]
    zeros = (0,) * len(lead)

    def body(w_ref, g_ref, m_ref, v_ref, d_ref, mo_ref, vo_ref):
        gv = g_ref[...]
        mn = ADAM_B1 * m_ref[...] + (1.0 - ADAM_B1) * gv
        vn = ADAM_B2 * v_ref[...] + (1.0 - ADAM_B2) * jnp.square(gv)
        m_hat = mn / (1.0 - ADAM_B1 ** ADAM_STEP)
        v_hat = vn / (1.0 - ADAM_B2 ** ADAM_STEP)
        d_ref[...] = -ADAM_LR * (m_hat / (jnp.sqrt(v_hat) + ADAM_EPS) + ADAM_WD * w_ref[...])
        mo_ref[...] = mn
        vo_ref[...] = vn

    spec = pl.BlockSpec((1,) * len(lead) + (tr, C), lambda i: zeros + (i, 0))
    return pl.pallas_call(
        body, name=name, grid=(R // tr,),
        in_specs=[spec] * 4, out_specs=[spec] * 3,
        out_shape=[jax.ShapeDtypeStruct(w.shape, F32)] * 3,
        compiler_params=_params(("arbitrary",)),
    )(w, g, m, v)


def _pad_row(v, width):
    return jnp.concatenate([v, jnp.zeros((1, width - v.shape[1]), v.dtype)], axis=1)


def local_grads(x, mem, norm_g, b_forget, mem_norm_g, final_norm_g, loss_target, w_in_p, w_kv, w_out):
    B, S, D = x.shape
    T = B * S
    xt = x.reshape(T, D)
    memt = mem.reshape(B * MEM_LEN, D)
    b_pad = _pad_row(b_forget, LANES)

    h = rms_fwd(xt, norm_g, 512, "rms_x")
    proj = mm_nn(h, w_in_p, 512, PW // 3, "in_proj")
    proj3 = proj.reshape(B, S, PW)
    mh = rms_fwd(memt, mem_norm_g, B * MEM_LEN, "rms_mem")
    mkv = mm_nn(mh, w_kv, B * MEM_LEN, 2 * MEM_W, "mem_kv_proj")
    mkv3 = mkv.reshape(B, MEM_LEN, 2 * MEM_W)

    negc = fox_gate(proj3, b_pad)
    causal = _log_masks_t(S, "causal")
    dilated = _log_masks_t(S, "dilated")
    rope = _rope_tables(S)

    o_fox, lse_fox = attn_fwd3("fox", proj3, S, negc_cols=negc, mask=causal)
    o_dil, lse_dil = attn_fwd3("dil", proj3, S, mask=dilated, rope=rope)
    o_mem, lse_mem = attn_fwd3("mem", proj3, S, kv=mkv3)

    ymix, dx2, dx2b, small_out = out_fwd(
        proj, o_fox.reshape(T, FOX_W), o_dil.reshape(T, DIL_W), o_mem.reshape(T, MEM_W), w_out,
        xt, loss_target.reshape(T, D), final_norm_g.reshape(1, D), 256)
    do_fox, do_dil, do_mem, dfg, ddg, dmg = out_bwd(
        proj, o_fox.reshape(T, FOX_W), o_dil.reshape(T, DIL_W), o_mem.reshape(T, MEM_W), w_out, dx2b, 256)
    g_out = mm_tn(ymix, dx2b, 512, D, "w_out_grad")

    dqkv_fox, dneg, drow = attn_bwd3("fox", proj3, do_fox.reshape(B, S, FOX_W), o_fox, lse_fox, S,
                                     negc_cols=negc, mask=causal)
    (dqkv_dil,) = attn_bwd3("dil", proj3, do_dil.reshape(B, S, DIL_W), o_dil, lse_dil, S, mask=dilated, rope=rope)
    dmq, dmk, dmv = attn_bwd3("mem", proj3, do_mem.reshape(B, S, MEM_W), o_mem, lse_mem, S, kv=mkv3)
    drow = drow.reshape(B, FOX_HEADS // 2, S // TQ, 2, TQ).transpose(0, 1, 3, 2, 4).reshape(B, FOX_HEADS, S)
    drow = jnp.pad(drow, ((0, 0), (0, LANES - FOX_HEADS), (0, 0)))
    dflog, db_part = fox_gate_bwd(drow, dneg, proj3, b_pad)

    dproj = jnp.concatenate([dqkv_fox.reshape(T, 3 * FOX_W), dfg, dqkv_dil.reshape(T, 3 * DIL_W), ddg,
                             dmq.reshape(T, MEM_W), dmg, dflog.reshape(T, LANES)], axis=1)
    g_in = mm_tn(h, dproj, 512, PW // 3, "w_in_grad")
    dh = mm_nt(dproj, w_in_p, 1024, PW // 3, "in_proj_bwd")
    grad_x, dng = rms_bwd(xt, norm_g, dh, dx2, 512, "rms_x_bwd")

    dmkv = jnp.concatenate([dmk, dmv], axis=2).reshape(B * MEM_LEN, 2 * MEM_W)
    g_kv = mm_tn(mh, dmkv, B * MEM_LEN, 2 * MEM_W, "w_kv_grad")
    dmh = mm_nt(dmkv, w_kv, B * MEM_LEN, D, "mem_kv_bwd")
    _, dmng = rms_bwd(memt, mem_norm_g, dmh, None, B * MEM_LEN, "rms_mem_bwd")

    small = jnp.concatenate([dng[0:1], dmng[0:1], small_out[0:1], _pad_row(db_part[0:1], D), small_out[1:2],
                             jnp.zeros((3, D), F32)], axis=0)
    return grad_x.reshape(B, S, D), g_in, g_kv, g_out, small


def kernel(x, mem, norm_g, w_in, b_forget, mem_norm_g, w_mem_kv, w_out, final_norm_g, loss_target, m_norm_g, m_w_in, m_b_forget, m_mem_norm_g, m_w_mem_kv, m_w_out, m_final_norm_g, v_norm_g, v_w_in, v_b_forget, v_mem_norm_g, v_w_mem_kv, v_w_out, v_final_norm_g):
    D = D_MODEL
    w_in_full, w_kv_full, w_out_full = weight_gather(
        [_pack_cols(w_in).astype(BF16).reshape(w_in.shape[1], PW), w_mem_kv[0].astype(BF16), w_out[0].astype(BF16)])
    grad_x, g_in, g_kv, g_out, small = local_grads(
        x, mem, norm_g, b_forget, mem_norm_g, final_norm_g, loss_target, w_in_full, w_kv_full, w_out_full)

    big = [g_in, g_kv, g_out]
    *from_sibling, csum = grad_exchange_d2d(big, small)
    tiles = (32, 128, 128)
    names = ("w_in", "w_kv", "w_out")
    chip_parts = [chip_sum(g, got, tr, "chip_sum_" + n) for g, got, tr, n in zip(big, from_sibling, tiles, names)]
    *from_chips, tot = grad_exchange_ici([cp for cp, _ in chip_parts], csum)
    gw_in, gw_kv, gw_out = [final_sum(own, got, tr, "final_sum_" + n)
                            for (_, own), got, tr, n in zip(chip_parts, from_chips, tiles, names)]
    gw_in = _unpack_cols(gw_in[None])

    loss = tot[4, 0]
    g_norm, g_mem_norm, g_final, g_b = tot[0:1], tot[1:2], tot[2], tot[3:4, :FOX_HEADS]

    def rows8(*rows):
        rows = [r.reshape(1, -1) for r in rows]
        rows = [_pad_row(r, D) for r in rows]
        return jnp.concatenate(rows + [jnp.zeros((8 - len(rows), D), F32)], axis=0)

    sw = rows8(norm_g, mem_norm_g, final_norm_g, b_forget)
    sm = rows8(m_norm_g, m_mem_norm_g, m_final_norm_g, m_b_forget)
    sv = rows8(v_norm_g, v_mem_norm_g, v_final_norm_g, v_b_forget)
    d_s, m_s, v_s = adamw(sw, tot, sm, sv, 8, "adamw_small")
    d_in, m_in, v_in = adamw(w_in, gw_in, m_w_in, v_w_in, 32, "adamw_w_in")
    d_kv, m_kv, v_kv = adamw(w_mem_kv[0], gw_kv, m_w_mem_kv[0], v_w_mem_kv[0], 128, "adamw_w_kv")
    d_out, m_out, v_out = adamw(w_out[0], gw_out, m_w_out[0], v_w_out[0], 256, "adamw_w_out")

    def small_outs(t):
        return t[0:1], t[3:4, :FOX_HEADS], t[1:2], t[2]

    grads = (g_norm, gw_in, g_b, g_mem_norm, gw_kv[None], gw_out[None], g_final)
    outs = []
    for t, big in ((d_s, (d_in, d_kv, d_out)), (m_s, (m_in, m_kv, m_out)), (v_s, (v_in, v_kv, v_out))):
        n, b, mn, f = small_outs(t)
        outs += [n, big[0], b, mn, big[1][None], big[2][None], f]
    return (loss, grad_x, *grads, *outs)
```

```python
import functools
import math

import numpy as np
import jax
import jax.numpy as jnp
from jax import lax
from jax.experimental import pallas as pl
from jax.experimental.pallas import tpu as pltpu

F32 = jnp.float32
BF16 = jnp.bfloat16

D_MODEL = 1024
HEAD_DIM = 64
FOX_HEADS = 12
DIL_HEADS = 12
MEM_HEADS = 4
MEM_HEAD_DIM = 128
MEM_LEN = 256
FOX_W = FOX_HEADS * HEAD_DIM
DIL_W = DIL_HEADS * HEAD_DIM
MEM_W = MEM_HEADS * MEM_HEAD_DIM
MIX_W = FOX_W + DIL_W + MEM_W
DILATIONS = ((128, 1), (512, 4), (2048, 16))
ROPE_THETA = 500000.0
ROPE_DIM = HEAD_DIM // 4
RMS_EPS = 1e-6
NEG_INF = -1e30
IN_W = 4 * FOX_W + FOX_HEADS + 4 * DIL_W + 2 * MEM_W

ADAM_LR = 0.001
ADAM_B1 = 0.9
ADAM_B2 = 0.999
ADAM_EPS = 1e-08
ADAM_WD = 0.01
ADAM_STEP = 10

N_DEV = 8
LANES = 128
PAIR_W = 3 * LANES
TQ = 256
TK = 256

O_FQ, O_FK, O_FV, O_FG = 0, FOX_W, 2 * FOX_W, 3 * FOX_W
O_FLOG = 4 * FOX_W
O_DQ = O_FLOG + FOX_HEADS
O_DK, O_DV, O_DG = O_DQ + DIL_W, O_DQ + 2 * DIL_W, O_DQ + 3 * DIL_W
O_MQ = O_DQ + 4 * DIL_W
O_MG = O_MQ + MEM_W
P_FOX = 0
P_FG = P_FOX + 3 * FOX_W
P_DIL = P_FG + FOX_W
P_DG = P_DIL + 3 * DIL_W
P_MQ = P_DG + DIL_W
P_MG = P_MQ + MEM_W
P_FLOG = P_MG + MEM_W
PW = P_FLOG + LANES

VMEM_LIMIT = 56 * 1024 * 1024


def _pack_pieces():
    pieces = []
    for base in (O_FQ, O_DQ):
        seg = []
        for hp in range(FOX_HEADS // 2):
            for part in range(3):
                seg.append((base + part * FOX_W + hp * LANES, LANES))
        pieces.append(seg)
    fox, dil = pieces
    return fox + [(O_FG, FOX_W)] + dil + [(O_DG, DIL_W), (O_MQ, MEM_W), (O_MG, MEM_W), (O_FLOG, FOX_HEADS)]


def _pack_cols(w):
    parts = [w[..., s:s + n] for s, n in _pack_pieces()]
    parts.append(jnp.zeros(w.shape[:-1] + (LANES - FOX_HEADS,), w.dtype))
    return jnp.concatenate(parts, axis=-1)


def _unpack_cols(g):
    runs = []
    pos = 0
    for s, n in _pack_pieces():
        runs.append((s, n, pos))
        pos += n
    runs.sort()
    return jnp.concatenate([g[..., p:p + n] for s, n, p in runs], axis=-1)


def _params(sem=None, **kw):
    return pltpu.CompilerParams(dimension_semantics=sem, vmem_limit_bytes=VMEM_LIMIT, **kw)


def _mesh_pos():
    return lax.axis_index("x"), lax.axis_index("y"), lax.axis_index("c")


def _flip(v, d):
    return 1 - v if d else v


_RELATIONS = [(dx, dy, dc) for dx in (0, 1) for dy in (0, 1) for dc in (0, 1)][1:]


def weight_gather(shards):
    n_arr = len(shards)
    rows = [s.shape[0] for s in shards]

    def body(*refs):
        in_refs = refs[:n_arr]
        out_refs = refs[n_arr:2 * n_arr]
        send_sems, recv_sems, local_sems = refs[2 * n_arr:]
        x, y, c = _mesh_pos()
        me, sibling = (x, y, c), (x, y, 1 - c)
        chips = [(1 - x, y), (x, 1 - y), (1 - x, 1 - y)]

        def block(a, pos):
            px, py, pc = pos
            return out_refs[a].at[pl.ds((4 * px + 2 * py + pc) * rows[a], rows[a]), :]

        def copy(a, k, blk, to, src=None):
            return pltpu.make_async_remote_copy(
                src_ref=block(a, blk) if src is None else src, dst_ref=block(a, blk),
                send_sem=send_sems.at[a, k], recv_sem=recv_sems.at[a, k],
                device_id=to, device_id_type=pl.DeviceIdType.MESH)

        started = []
        mine = []
        for a in range(n_arr):
            cp = pltpu.make_async_copy(in_refs[a], block(a, me), local_sems.at[a])
            cp.start()
            mine.append(cp)
            first = [copy(a, 0, me, sibling, src=in_refs[a])]
            first += [copy(a, 1 + j, me, (*chip, c), src=in_refs[a]) for j, chip in enumerate(chips)]
            for cp in first:
                cp.start()
            started += first
        for a in range(n_arr):
            for j, chip in enumerate(chips):
                copy(a, 1 + j, (*chip, c), me).wait_recv()
                passed = copy(a, 4 + j, (*chip, c), sibling)
                passed.start()
                started.append(passed)
        for a in range(n_arr):
            copy(a, 0, sibling, me).wait_recv()
            for j, chip in enumerate(chips):
                copy(a, 4 + j, (*chip, 1 - c), me).wait_recv()
        for cp in started:
            cp.wait_send()
        for cp in mine:
            cp.wait()

    any_spec = pl.BlockSpec(memory_space=pl.ANY)
    return pl.pallas_call(
        body, name="weight_gather",
        out_shape=[jax.ShapeDtypeStruct((N_DEV * s.shape[0], s.shape[1]), s.dtype) for s in shards],
        in_specs=[any_spec] * n_arr, out_specs=[any_spec] * n_arr,
        scratch_shapes=[pltpu.SemaphoreType.DMA((n_arr, 7)), pltpu.SemaphoreType.DMA((n_arr, 7)),
                        pltpu.SemaphoreType.DMA((n_arr,))],
    )(*shards)


def grad_exchange(grads, small):
    arrs = list(grads) + [small]
    n_arr = len(arrs)
    rows = [g.shape[0] // N_DEV for g in grads] + [small.shape[0]]

    def body(*refs):
        in_refs = refs[:n_arr]
        out_refs = refs[n_arr:2 * n_arr]
        send_sems, recv_sems, local_sems = refs[2 * n_arr:]
        x, y, c = _mesh_pos()
        me = 4 * x + 2 * y + c

        def src(a, idx):
            if a == n_arr - 1:
                return in_refs[a]
            return in_refs[a].at[pl.ds(idx * rows[a], rows[a]), :]

        def copy(a, k):
            dx, dy, dc = _RELATIONS[k]
            px, py, pc = _flip(x, dx), _flip(y, dy), _flip(c, dc)
            peer = 4 * px + 2 * py + pc
            send = pltpu.make_async_remote_copy(
                src_ref=src(a, peer), dst_ref=out_refs[a].at[me],
                send_sem=send_sems.at[a, k], recv_sem=recv_sems.at[a, k],
                device_id=(px, py, pc), device_id_type=pl.DeviceIdType.MESH)
            recv = pltpu.make_async_remote_copy(
                src_ref=src(a, peer), dst_ref=out_refs[a].at[peer],
                send_sem=send_sems.at[a, k], recv_sem=recv_sems.at[a, k],
                device_id=(px, py, pc), device_id_type=pl.DeviceIdType.MESH)
            return send, recv

        mine = []
        pairs = []
        for a in range(n_arr):
            cp = pltpu.make_async_copy(src(a, me), out_refs[a].at[me], local_sems.at[a])
            cp.start()
            mine.append(cp)
            for k in range(7):
                send, recv = copy(a, k)
                send.start()
                pairs.append((send, recv))
        for send, recv in pairs:
            recv.wait_recv()
        for send, recv in pairs:
            send.wait_send()
        for cp in mine:
            cp.wait()

    any_spec = pl.BlockSpec(memory_space=pl.ANY)
    return pl.pallas_call(
        body, name="grad_exchange",
        out_shape=[jax.ShapeDtypeStruct((N_DEV, r, a.shape[1]), a.dtype) for r, a in zip(rows, arrs)],
        in_specs=[any_spec] * n_arr, out_specs=[any_spec] * n_arr,
        scratch_shapes=[pltpu.SemaphoreType.DMA((n_arr, 7)), pltpu.SemaphoreType.DMA((n_arr, 7)),
                        pltpu.SemaphoreType.DMA((n_arr,))],
    )(*arrs)


def slot_sum(slots, tr, name):
    _, R, C = slots.shape

    def body(s_ref, o_ref):
        acc = s_ref[0]
        for d in range(1, N_DEV):
            acc = acc + s_ref[d]
        o_ref[...] = acc

    return pl.pallas_call(
        body, name=name, grid=(R // tr,),
        in_specs=[pl.BlockSpec((N_DEV, tr, C), lambda i: (0, i, 0))],
        out_specs=pl.BlockSpec((tr, C), lambda i: (i, 0)),
        out_shape=jax.ShapeDtypeStruct((R, C), slots.dtype),
        compiler_params=_params(("arbitrary",)),
    )(slots)


N_CHIP = 4
_OTHER_CHIPS = [(1, 0), (0, 1), (1, 1)]


def grad_exchange_d2d(grads, small):
    n_big = len(grads)
    rows = [g.shape[0] // N_DEV for g in grads]

    def body(*refs):
        g_refs = refs[:n_big]
        small_ref = refs[n_big]
        out_refs = refs[n_big + 1:2 * n_big + 1]
        csum_ref = refs[2 * n_big + 1]
        land, send_sems, recv_sems = refs[2 * n_big + 2:]
        x, y, c = _mesh_pos()
        sibling = (x, y, 1 - c)
        copies = []
        for a in range(n_big):
            for q in range(N_CHIP):
                copies.append(pltpu.make_async_remote_copy(
                    src_ref=g_refs[a].at[pl.ds((2 * q + 1 - c) * rows[a], rows[a]), :], dst_ref=out_refs[a].at[q],
                    send_sem=send_sems.at[a, q], recv_sem=recv_sems.at[a, q],
                    device_id=sibling, device_id_type=pl.DeviceIdType.MESH))
        copies.append(pltpu.make_async_remote_copy(
            src_ref=small_ref, dst_ref=land, send_sem=send_sems.at[n_big, 0], recv_sem=recv_sems.at[n_big, 0],
            device_id=sibling, device_id_type=pl.DeviceIdType.MESH))
        for cp in copies:
            cp.start()
        for cp in copies:
            cp.wait_recv()
        for cp in copies:
            cp.wait_send()
        csum_ref[...] = small_ref[...] + land[...]

    any_spec = pl.BlockSpec(memory_space=pl.ANY)
    vmem_spec = pl.BlockSpec(memory_space=pltpu.VMEM)
    return pl.pallas_call(
        body, name="grad_exchange_d2d",
        out_shape=[jax.ShapeDtypeStruct((N_CHIP, r, g.shape[1]), g.dtype) for r, g in zip(rows, grads)]
        + [jax.ShapeDtypeStruct(small.shape, small.dtype)],
        in_specs=[any_spec] * n_big + [vmem_spec], out_specs=[any_spec] * n_big + [vmem_spec],
        scratch_shapes=[pltpu.VMEM(small.shape, small.dtype),
                        pltpu.SemaphoreType.DMA((n_big + 1, N_CHIP)), pltpu.SemaphoreType.DMA((n_big + 1, N_CHIP))],
    )(*grads, small)


def chip_sum(g, got, tr, name):
    _, rows, cols = got.shape
    g4 = g.reshape(N_CHIP, 2, rows, cols)
    x, y, c = _mesh_pos()
    where = jnp.stack([c, 2 * x + y]).astype(jnp.int32)

    def body_all(w_ref, g_ref, r_ref, o_ref):
        o_ref[0] = (g_ref[0, 0] + r_ref[0]).astype(BF16)

    def body_own(w_ref, g_ref, r_ref, o_ref):
        o_ref[...] = g_ref[0, 0] + r_ref[0]

    cpb = pl.pallas_call(
        body_all, name=name + "_all",
        grid_spec=pltpu.PrefetchScalarGridSpec(
            num_scalar_prefetch=1, grid=(N_CHIP, rows // tr),
            in_specs=[pl.BlockSpec((1, 1, tr, cols), lambda q, i, w: (q, w[0], i, 0)),
                      pl.BlockSpec((1, tr, cols), lambda q, i, w: (q, i, 0))],
            out_specs=pl.BlockSpec((1, tr, cols), lambda q, i, w: (q, i, 0))),
        out_shape=jax.ShapeDtypeStruct((N_CHIP, rows, cols), BF16),
        compiler_params=_params(("arbitrary", "arbitrary")),
    )(where, g4, got)
    own = pl.pallas_call(
        body_own, name=name + "_own",
        grid_spec=pltpu.PrefetchScalarGridSpec(
            num_scalar_prefetch=1, grid=(rows // tr,),
            in_specs=[pl.BlockSpec((1, 1, tr, cols), lambda i, w: (w[1], w[0], i, 0)),
                      pl.BlockSpec((1, tr, cols), lambda i, w: (w[1], i, 0))],
            out_specs=pl.BlockSpec((tr, cols), lambda i, w: (i, 0))),
        out_shape=jax.ShapeDtypeStruct((rows, cols), F32),
        compiler_params=_params(("arbitrary",)),
    )(where, g4, got)
    return cpb, own


def grad_exchange_ici(parts, csum):
    n_big = len(parts)

    def body(*refs):
        p_refs = refs[:n_big]
        csum_ref = refs[n_big]
        out_refs = refs[n_big + 1:2 * n_big + 1]
        tot_ref = refs[2 * n_big + 1]
        land, send_sems, recv_sems = refs[2 * n_big + 2:]
        x, y, c = _mesh_pos()
        q_me = 2 * x + y
        land[q_me] = csum_ref[...]
        sends, recvs = [], []
        for j, (dx, dy) in enumerate(_OTHER_CHIPS):
            px, py = _flip(x, dx), _flip(y, dy)
            q_peer = 2 * px + py
            for a in range(n_big + 1):
                src = p_refs[a].at[q_peer] if a < n_big else csum_ref
                dst = out_refs[a] if a < n_big else land
                common = dict(send_sem=send_sems.at[a, j], recv_sem=recv_sems.at[a, j],
                              device_id=(px, py, c), device_id_type=pl.DeviceIdType.MESH)
                sends.append(pltpu.make_async_remote_copy(src_ref=src, dst_ref=dst.at[q_me], **common))
                recvs.append(pltpu.make_async_remote_copy(src_ref=src, dst_ref=dst.at[q_peer], **common))
        for cp in sends:
            cp.start()
        for cp in recvs:
            cp.wait_recv()
        for cp in sends:
            cp.wait_send()
        tot = land[0]
        for q in range(1, N_CHIP):
            tot = tot + land[q]
        tot_ref[...] = tot

    any_spec = pl.BlockSpec(memory_space=pl.ANY)
    vmem_spec = pl.BlockSpec(memory_space=pltpu.VMEM)
    return pl.pallas_call(
        body, name="grad_exchange_ici",
        out_shape=[jax.ShapeDtypeStruct(p.shape, p.dtype) for p in parts] + [jax.ShapeDtypeStruct(csum.shape, csum.dtype)],
        in_specs=[any_spec] * n_big + [vmem_spec], out_specs=[any_spec] * n_big + [vmem_spec],
        scratch_shapes=[pltpu.VMEM((N_CHIP,) + csum.shape, csum.dtype),
                        pltpu.SemaphoreType.DMA((n_big + 1, 3)), pltpu.SemaphoreType.DMA((n_big + 1, 3))],
    )(*parts, csum)


def final_sum(own, got, tr, name):
    rows, cols = own.shape

    def body(own_ref, got_ref, o_ref):
        q_me = 2 * lax.axis_index("x") + lax.axis_index("y")
        acc = None
        for q in range(N_CHIP):
            term = jnp.where(q == q_me, own_ref[...], got_ref[q].astype(F32))
            acc = term if acc is None else acc + term
        o_ref[...] = acc

    return pl.pallas_call(
        body, name=name, grid=(rows // tr,),
        in_specs=[pl.BlockSpec((tr, cols), lambda i: (i, 0)), pl.BlockSpec((N_CHIP, tr, cols), lambda i: (0, i, 0))],
        out_specs=pl.BlockSpec((tr, cols), lambda i: (i, 0)),
        out_shape=jax.ShapeDtypeStruct((rows, cols), F32),
        compiler_params=_params(("arbitrary",)),
    )(own, got)


def rms_fwd(x, g, tm, name):
    M, K = x.shape

    def body(x_ref, g_ref, o_ref):
        xv = x_ref[...]
        r = lax.rsqrt(jnp.mean(xv * xv, axis=-1, keepdims=True) + RMS_EPS)
        o_ref[...] = ((xv * r) * g_ref[...]).astype(BF16)

    return pl.pallas_call(
        body, name=name, grid=(M // tm,),
        in_specs=[pl.BlockSpec((tm, K), lambda i: (i, 0)), pl.BlockSpec((1, K), lambda i: (0, 0))],
        out_specs=pl.BlockSpec((tm, K), lambda i: (i, 0)),
        out_shape=jax.ShapeDtypeStruct((M, K), BF16),
        compiler_params=_params(("arbitrary",)),
    )(x, g)


def rms_bwd(x, g, dh, dres, tm, name):
    M, K = x.shape
    has_res = dres is not None

    def body(*refs):
        if has_res:
            x_ref, g_ref, dh_ref, dres_ref, dx_ref, dg_ref = refs
        else:
            x_ref, g_ref, dh_ref, dx_ref, dg_ref = refs
        xv = x_ref[...]
        r = lax.rsqrt(jnp.mean(xv * xv, axis=-1, keepdims=True) + RMS_EPS)
        xn = xv * r
        dhv = dh_ref[...]
        dxn = dhv * g_ref[...]
        dx = r * (dxn - xn * jnp.mean(dxn * xn, axis=-1, keepdims=True))
        if has_res:
            dx = dx + dres_ref[...]
        dx_ref[...] = dx
        part = jnp.sum(dhv * xn, axis=0, keepdims=True)
        row = lax.broadcasted_iota(jnp.int32, (8, K), 0)
        upd = jnp.where(row == 0, part, 0.0)

        @pl.when(pl.program_id(0) == 0)
        def _():
            dg_ref[...] = upd

        @pl.when(pl.program_id(0) != 0)
        def _():
            dg_ref[...] += upd

    row_spec = pl.BlockSpec((tm, K), lambda i: (i, 0))
    ins = [x, g, dh] + ([dres] if has_res else [])
    in_specs = [row_spec, pl.BlockSpec((1, K), lambda i: (0, 0)), row_spec] + ([row_spec] if has_res else [])
    return pl.pallas_call(
        body, name=name, grid=(M // tm,),
        in_specs=in_specs,
        out_specs=[row_spec, pl.BlockSpec((8, K), lambda i: (0, 0))],
        out_shape=[jax.ShapeDtypeStruct((M, K), F32), jax.ShapeDtypeStruct((8, K), F32)],
        compiler_params=_params(("arbitrary",)),
    )(*ins)


def mm_nn(a, b, tm, tn, name):
    M, K = a.shape
    N = b.shape[1]

    def body(a_ref, b_ref, o_ref):
        o_ref[...] = jnp.dot(a_ref[...], b_ref[...], preferred_element_type=F32)

    return pl.pallas_call(
        body, name=name, grid=(N // tn, M // tm),
        in_specs=[pl.BlockSpec((tm, K), lambda j, i: (i, 0)), pl.BlockSpec((K, tn), lambda j, i: (0, j))],
        out_specs=pl.BlockSpec((tm, tn), lambda j, i: (i, j)),
        out_shape=jax.ShapeDtypeStruct((M, N), F32),
        compiler_params=_params(("arbitrary", "arbitrary")),
    )(a, b)


def mm_nt(a, b, tm, tk, name):
    M, K = a.shape
    N = b.shape[0]

    def body(a_ref, b_ref, o_ref):
        part = lax.dot_general(a_ref[...], b_ref[...], (((1,), (1,)), ((), ())), preferred_element_type=F32)

        @pl.when(pl.program_id(1) == 0)
        def _():
            o_ref[...] = part

        @pl.when(pl.program_id(1) != 0)
        def _():
            o_ref[...] += part

    return pl.pallas_call(
        body, name=name, grid=(M // tm, K // tk),
        in_specs=[pl.BlockSpec((tm, tk), lambda i, k: (i, k)), pl.BlockSpec((N, tk), lambda i, k: (0, k))],
        out_specs=pl.BlockSpec((tm, N), lambda i, k: (i, 0)),
        out_shape=jax.ShapeDtypeStruct((M, N), F32),
        compiler_params=_params(("arbitrary", "arbitrary")),
    )(a, b)


def mm_tn(a, b, tt, tn, name):
    T, K = a.shape
    N = b.shape[1]

    def body(a_ref, b_ref, o_ref):
        part = lax.dot_general(a_ref[...], b_ref[...], (((0,), (0,)), ((), ())), preferred_element_type=F32)

        @pl.when(pl.program_id(1) == 0)
        def _():
            o_ref[...] = part

        @pl.when(pl.program_id(1) != 0)
        def _():
            o_ref[...] += part

    return pl.pallas_call(
        body, name=name, grid=(N // tn, T // tt),
        in_specs=[pl.BlockSpec((tt, K), lambda j, t: (t, 0)), pl.BlockSpec((tt, tn), lambda j, t: (t, j))],
        out_specs=pl.BlockSpec((K, tn), lambda j, t: (0, j)),
        out_shape=jax.ShapeDtypeStruct((K, N), F32),
        compiler_params=_params(("arbitrary", "arbitrary")),
    )(a, b)


def _log_sigmoid(z):
    return jnp.minimum(z, 0.0) - jnp.log(1.0 + jnp.exp(-jnp.abs(z)))


def _tri(n, lower):
    r = lax.broadcasted_iota(jnp.int32, (n, n), 0)
    c = lax.broadcasted_iota(jnp.int32, (n, n), 1)
    return jnp.where((r >= c) if lower else (r <= c), 1.0, 0.0).astype(F32)


def fox_gate(proj3, b_pad):
    B, S, _ = proj3.shape
    nblk = S // TK

    def body(f_ref, b_ref, o_ref):
        tri = _tri(TK, True)
        carry = jnp.zeros((1, LANES), F32)
        for n in range(nblk):
            z = f_ref[0, n * TK:(n + 1) * TK, :] + b_ref[...]
            logf = _log_sigmoid(z)
            cs = jnp.dot(tri, logf, preferred_element_type=F32, precision=lax.Precision.HIGHEST) + carry
            carry = cs[TK - 1:TK, :]
            o_ref[0, n * TK:(n + 1) * TK, :] = -cs

    return pl.pallas_call(
        body, name="fox_gate", grid=(B,),
        in_specs=[pl.BlockSpec((1, S, LANES), lambda b: (b, 0, P_FLOG // LANES)),
                  pl.BlockSpec((1, LANES), lambda b: (0, 0))],
        out_specs=pl.BlockSpec((1, S, LANES), lambda b: (b, 0, 0)),
        out_shape=jax.ShapeDtypeStruct((B, S, LANES), F32),
        compiler_params=_params(("arbitrary",)),
    )(proj3, b_pad)


def fox_gate_bwd(drow, dneg, proj3, b_pad):
    B, S, _ = proj3.shape
    nblk = S // TK

    def body(d_ref, r_ref, f_ref, b_ref, o_ref, db_ref):
        tri = _tri(TK, False)
        lane = lax.broadcasted_iota(jnp.int32, (TK, LANES), 1)
        er = lax.broadcasted_iota(jnp.int32, (FOX_W, LANES), 0)
        ec = lax.broadcasted_iota(jnp.int32, (FOX_W, LANES), 1)
        pick = jnp.where(er == LANES * (ec >> 1) + (ec & 1), 1.0, 0.0).astype(F32)
        carry = jnp.zeros((1, LANES), F32)
        dbsum = jnp.zeros((1, LANES), F32)
        for n in reversed(range(nblk)):
            dk_side = jnp.dot(r_ref[0, n * TK:(n + 1) * TK, :], pick, preferred_element_type=F32,
                              precision=lax.Precision.HIGHEST)
            dc = jnp.where(lane < FOX_HEADS, d_ref[0, :, n * TK:(n + 1) * TK].T - dk_side, 0.0)
            rs = jnp.dot(tri, dc, preferred_element_type=F32, precision=lax.Precision.HIGHEST) + carry
            carry = rs[0:1, :]
            z = f_ref[0, n * TK:(n + 1) * TK, :] + b_ref[...]
            dz = rs * (1.0 / (1.0 + jnp.exp(z)))
            o_ref[0, n * TK:(n + 1) * TK, :] = dz.astype(BF16)
            dbsum = dbsum + jnp.sum(dz, axis=0, keepdims=True)
        row = lax.broadcasted_iota(jnp.int32, (8, LANES), 0)
        upd = jnp.where(row == 0, dbsum, 0.0)

        @pl.when(pl.program_id(0) == 0)
        def _():
            db_ref[...] = upd

        @pl.when(pl.program_id(0) != 0)
        def _():
            db_ref[...] += upd

    return pl.pallas_call(
        body, name="fox_gate_bwd", grid=(B,),
        in_specs=[pl.BlockSpec((1, LANES, S), lambda b: (b, 0, 0)),
                  pl.BlockSpec((1, S, FOX_W), lambda b: (b, 0, 0)),
                  pl.BlockSpec((1, S, LANES), lambda b: (b, 0, P_FLOG // LANES)),
                  pl.BlockSpec((1, LANES), lambda b: (0, 0))],
        out_specs=[pl.BlockSpec((1, S, LANES), lambda b: (b, 0, 0)), pl.BlockSpec((8, LANES), lambda b: (0, 0))],
        out_shape=[jax.ShapeDtypeStruct((B, S, LANES), BF16), jax.ShapeDtypeStruct((8, LANES), F32)],
        compiler_params=_params(("arbitrary",)),
    )(drow, dneg, proj3, b_pad)


def _mult_masks(S, kind):
    nd = S // TQ
    a = np.arange(TQ)[:, None]
    b = np.arange(TK)[None, :]
    out = np.zeros((nd, TQ, TK), np.float32)
    for d in range(nd):
        delta = d * TQ + a - b
        if kind == "causal":
            out[d] = delta >= 0
        else:
            m = np.zeros((TQ, TK), np.float32)
            for w, dil in DILATIONS:
                m += (delta >= 0) & (delta % dil == 0) & (delta <= w)
            out[d] = m
    return jnp.asarray(out)


def _rope_tables(S):
    half = ROPE_DIM // 2
    pos = jnp.arange(S, dtype=F32)
    inv_freq = 1.0 / (ROPE_THETA ** (jnp.arange(0, ROPE_DIM, 2, dtype=F32) / ROPE_DIM))
    ang = pos[:, None] * inv_freq[None, :]
    cos, sin = jnp.cos(ang), jnp.sin(ang)
    one = jnp.ones((S, HEAD_DIM - ROPE_DIM), F32)
    zero = jnp.zeros((S, HEAD_DIM - ROPE_DIM), F32)
    zh = jnp.zeros((S, half), F32)
    c = jnp.concatenate([cos, cos, one], axis=1)
    s1 = jnp.concatenate([-sin, zh, zero], axis=1)
    s2 = jnp.concatenate([zh, sin, zero], axis=1)
    return tuple(jnp.concatenate([t, t], axis=1) for t in (c, s1, s2))


def _rope(t, c, s1, s2):
    return t * c + pltpu.roll(t, LANES - half_rope(), 1) * s1 + pltpu.roll(t, half_rope(), 1) * s2


def half_rope():
    return ROPE_DIM // 2


def _rope_bwd(d, c, s1, s2):
    return d * c + pltpu.roll(d * s1, half_rope(), 1) + pltpu.roll(d * s2, LANES - half_rope(), 1)


def _scale_parts(scale):
    m, _ = math.frexp(scale)
    return (scale, None) if m == 0.5 else (None, scale)


def attn_fwd(kind, src, S, *, negc=None, mask=None, rope=None, kv=None):
    B = src.shape[0]
    pair = kind != "mem"
    col0 = {"fox": P_FOX, "dil": P_DIL, "mem": P_MQ}[kind]
    n_blocks = FOX_HEADS // 2 if pair else MEM_HEADS
    e_dim = HEAD_DIM if pair else MEM_HEAD_DIM
    q_fold, s_scale = _scale_parts(1.0 / math.sqrt(e_dim))
    Sk = S if pair else MEM_LEN
    nh = 2 if pair else 1
    has_bias = negc is not None
    has_rope = rope is not None
    nq = S // TQ

    def body(*refs):
        refs = list(refs)
        if pair:
            qkv_ref = refs.pop(0)
        else:
            q_ref, k_ref, v_ref = refs.pop(0), refs.pop(0), refs.pop(0)
        negc_ref = refs.pop(0) if has_bias else None
        mask_ref = refs.pop(0) if pair else None
        rope_refs = [refs.pop(0) for _ in range(3)] if has_rope else None
        o_ref, lse_ref, qs, ks, vs = refs
        lane = lax.broadcasted_iota(jnp.int32, (1, LANES), 1)

        def prep_q(n, _):
            r0 = pl.multiple_of(n * TQ, TQ)
            rows = pl.ds(r0, TQ)
            q = qkv_ref[0, rows, 0:LANES] if pair else q_ref[0, rows, :]
            if has_rope:
                q = _rope(q, *[t[rows, :] for t in rope_refs])
            if q_fold is not None:
                q = q * q_fold
            qs[rows, :] = q.astype(BF16)
            return 0

        def prep_kv(n, _):
            r0 = pl.multiple_of(n * TK, TK)
            rows = pl.ds(r0, TK)
            k = qkv_ref[0, rows, LANES:2 * LANES] if pair else k_ref[0, rows, :]
            v = qkv_ref[0, rows, 2 * LANES:3 * LANES] if pair else v_ref[0, rows, :]
            if has_rope:
                k = _rope(k, *[t[rows, :] for t in rope_refs])
            ks[rows, :] = k.astype(BF16)
            vs[rows, :] = v.astype(BF16)
            return 0

        lax.fori_loop(0, nq, prep_q, 0)
        lax.fori_loop(0, Sk // TK, prep_kv, 0)

        def q_loop(i, _):
            r0 = pl.multiple_of(i * TQ, TQ)
            q = qs[pl.ds(r0, TQ), :]
            res = []
            for hh in range(nh):
                hmask = (lane >= HEAD_DIM * hh) & (lane < HEAD_DIM * (hh + 1))
                qh = jnp.where(hmask, q, jnp.zeros_like(q)) if pair else q

                def kv_loop(j, carry, qh=qh, hh=hh):
                    m, l, acc = carry
                    c0 = pl.multiple_of(j * TK, TK)
                    k = ks[pl.ds(c0, TK), :]
                    v = vs[pl.ds(c0, TK), :]
                    s = lax.dot_general(qh, k, (((1,), (1,)), ((), ())), preferred_element_type=F32)
                    if s_scale is not None:
                        s = s * s_scale
                    if has_bias:
                        s = s + negc_ref[0, 0, pl.ds(hh, 1), pl.ds(c0, TK)]
                    if pair:
                        mult = mask_ref[i - j]
                        s = jnp.where(mult > 0.0, s, NEG_INF)
                    m_new = jnp.maximum(m, jnp.max(s, axis=1, keepdims=True))
                    p = jnp.exp(s - m_new)
                    if pair:
                        p = p * mult
                    alpha = jnp.exp(m - m_new)
                    l = alpha * l + jnp.sum(p, axis=1, keepdims=True)
                    acc = acc * alpha + jnp.dot(p.astype(BF16), v, preferred_element_type=F32)
                    return m_new, l, acc

                init = (jnp.full((TQ, 1), NEG_INF, F32), jnp.zeros((TQ, 1), F32), jnp.zeros((TQ, LANES), F32))
                m, l, acc = lax.fori_loop(0, (i + 1) if pair else Sk // TK, kv_loop, init)
                res.append((acc / l, m + jnp.log(l)))
            if pair:
                o = jnp.where(lane < HEAD_DIM, res[0][0], res[1][0])
                lse = jnp.where(lane < HEAD_DIM, res[0][1], res[1][1])
            else:
                o = res[0][0]
                lse = jnp.broadcast_to(res[0][1], (TQ, LANES))
            o_ref[0, pl.ds(r0, TQ), :] = o
            lse_ref[0, pl.ds(r0, TQ), :] = lse
            return 0

        lax.fori_loop(0, nq, q_loop, 0)

    ins, in_specs = [], []
    if pair:
        ins.append(src)
        in_specs.append(pl.BlockSpec((1, S, PAIR_W), lambda b, h: (b, 0, col0 // PAIR_W + h)))
    else:
        ins += [src, kv, kv]
        in_specs += [pl.BlockSpec((1, S, LANES), lambda b, h: (b, 0, col0 // LANES + h)),
                     pl.BlockSpec((1, MEM_LEN, LANES), lambda b, h: (b, 0, h)),
                     pl.BlockSpec((1, MEM_LEN, LANES), lambda b, h: (b, 0, MEM_HEADS + h))]
    if has_bias:
        ins.append(negc)
        in_specs.append(pl.BlockSpec((1, 1, 2, S), lambda b, h: (b, h, 0, 0)))
    if pair:
        ins.append(mask)
        in_specs.append(pl.BlockSpec(mask.shape, lambda b, h: (0, 0, 0)))
    if has_rope:
        ins += list(rope)
        in_specs += [pl.BlockSpec((S, LANES), lambda b, h: (0, 0))] * 3
    W = n_blocks * LANES
    out_spec = pl.BlockSpec((1, S, LANES), lambda b, h: (b, 0, h))
    return pl.pallas_call(
        body, name=kind + "_attn_fwd", grid=(B, n_blocks),
        in_specs=in_specs, out_specs=[out_spec, out_spec],
        out_shape=[jax.ShapeDtypeStruct((B, S, W), F32)] * 2,
        scratch_shapes=[pltpu.VMEM((S, LANES), BF16), pltpu.VMEM((Sk, LANES), BF16), pltpu.VMEM((Sk, LANES), BF16)],
        compiler_params=_params(("arbitrary", "arbitrary")),
    )(*ins)


def attn_bwd(kind, src, do, o, lse, S, *, negc=None, mask=None, rope=None, kv=None):
    B = src.shape[0]
    pair = kind != "mem"
    col0 = {"fox": P_FOX, "dil": P_DIL, "mem": P_MQ}[kind]
    n_blocks = FOX_HEADS // 2 if pair else MEM_HEADS
    e_dim = HEAD_DIM if pair else MEM_HEAD_DIM
    scale = 1.0 / math.sqrt(e_dim)
    q_fold, s_scale = _scale_parts(scale)
    Sk = S if pair else MEM_LEN
    nh = 2 if pair else 1
    has_bias = negc is not None
    has_rope = rope is not None
    nq = S // TQ
    nk = Sk // TK

    def body(*refs):
        refs = list(refs)
        if pair:
            qkv_ref = refs.pop(0)
        else:
            q_ref, k_ref, v_ref = refs.pop(0), refs.pop(0), refs.pop(0)
        do_ref, o_ref, lse_ref = refs.pop(0), refs.pop(0), refs.pop(0)
        negc_ref = refs.pop(0) if has_bias else None
        mask_ref = refs.pop(0) if pair else None
        rope_refs = [refs.pop(0) for _ in range(3)] if has_rope else None
        if pair:
            dqkv_ref = refs.pop(0)
            dnegc_ref = refs.pop(0) if has_bias else None
            drow_ref = refs.pop(0) if has_bias else None
        else:
            dq_ref, dk_ref, dv_ref = refs.pop(0), refs.pop(0), refs.pop(0)
        qs, ks, vs, dos, delta_s, dq_acc = refs[:6]
        drow_acc = refs[6] if has_bias else None
        lane = lax.broadcasted_iota(jnp.int32, (1, LANES), 1)

        def prep_q(n, _):
            r0 = pl.multiple_of(n * TQ, TQ)
            rows = pl.ds(r0, TQ)
            q = qkv_ref[0, rows, 0:LANES] if pair else q_ref[0, rows, :]
            if has_rope:
                q = _rope(q, *[t[rows, :] for t in rope_refs])
            if q_fold is not None:
                q = q * q_fold
            qs[rows, :] = q.astype(BF16)
            dov = do_ref[0, rows, :]
            dob = dov.astype(BF16)
            dos[rows, :] = dob
            prod = dob.astype(F32) * o_ref[0, rows, :]
            if pair:
                d0 = jnp.sum(jnp.where(lane < HEAD_DIM, prod, 0.0), axis=1, keepdims=True)
                d1 = jnp.sum(jnp.where(lane < HEAD_DIM, 0.0, prod), axis=1, keepdims=True)
                delta_s[rows, :] = jnp.where(lane < HEAD_DIM, d0, d1)
            else:
                delta_s[rows, :] = jnp.broadcast_to(jnp.sum(prod, axis=1, keepdims=True), (TQ, LANES))
            dq_acc[rows, :] = jnp.zeros((TQ, LANES), F32)
            if has_bias:
                drow_acc[rows, :] = jnp.zeros((TQ, LANES), F32)
            return 0

        def prep_kv(n, _):
            r0 = pl.multiple_of(n * TK, TK)
            rows = pl.ds(r0, TK)
            k = qkv_ref[0, rows, LANES:2 * LANES] if pair else k_ref[0, rows, :]
            v = qkv_ref[0, rows, 2 * LANES:3 * LANES] if pair else v_ref[0, rows, :]
            if has_rope:
                k = _rope(k, *[t[rows, :] for t in rope_refs])
            ks[rows, :] = k.astype(BF16)
            vs[rows, :] = v.astype(BF16)
            return 0

        lax.fori_loop(0, nq, prep_q, 0)
        lax.fori_loop(0, nk, prep_kv, 0)

        def kv_loop(j, _):
            c0 = pl.multiple_of(j * TK, TK)
            kt = ks[pl.ds(c0, TK), :]
            vt = vs[pl.ds(c0, TK), :]
            res = []
            for hh in range(nh):
                hmask = (lane >= HEAD_DIM * hh) & (lane < HEAD_DIM * (hh + 1))
                kh = jnp.where(hmask, kt, jnp.zeros_like(kt)) if pair else kt
                vh = jnp.where(hmask, vt, jnp.zeros_like(vt)) if pair else vt

                def q_loop(i, carry, kh=kh, vh=vh, hh=hh, hmask=hmask):
                    dk, dv, dneg = carry
                    r0 = pl.multiple_of(i * TQ, TQ)
                    rows = pl.ds(r0, TQ)
                    q = qs[rows, :]
                    dot = dos[rows, :]
                    lse_i = lse_ref[0, rows, hh * HEAD_DIM:hh * HEAD_DIM + 1]
                    delta_i = delta_s[rows, hh * HEAD_DIM:hh * HEAD_DIM + 1]
                    s = lax.dot_general(q, kh, (((1,), (1,)), ((), ())), preferred_element_type=F32)
                    if s_scale is not None:
                        s = s * s_scale
                    if has_bias:
                        s = s + negc_ref[0, 0, pl.ds(hh, 1), pl.ds(c0, TK)]
                    if pair:
                        mult = mask_ref[i - j]
                        s = jnp.where(mult > 0.0, s, NEG_INF)
                    p = jnp.exp(s - lse_i)
                    if pair:
                        p = p * mult
                    dv = dv + lax.dot_general(p.astype(BF16), dot, (((0,), (0,)), ((), ())),
                                              preferred_element_type=F32)
                    dp = lax.dot_general(dot, vh, (((1,), (1,)), ((), ())), preferred_element_type=F32)
                    ds = p * (dp - delta_i)
                    if has_bias:
                        dneg = dneg + jnp.sum(ds, axis=0, keepdims=True)
                        drow_acc[rows, :] += jnp.where(hmask, jnp.sum(ds, axis=1, keepdims=True), 0.0)
                    if s_scale is not None:
                        ds = ds * s_scale
                    dsb = ds.astype(BF16)
                    dk = dk + lax.dot_general(dsb, q, (((0,), (0,)), ((), ())), preferred_element_type=F32)
                    dq = jnp.dot(dsb, kh, preferred_element_type=F32)
                    dq_acc[rows, :] += dq
                    return dk, dv, dneg

                init = (jnp.zeros((TK, LANES), F32), jnp.zeros((TK, LANES), F32), jnp.zeros((1, TK), F32))
                dk, dv, dneg = lax.fori_loop(j if pair else 0, nq, q_loop, init)
                if has_bias:
                    dnegc_ref[0, 0, pl.ds(hh, 1), pl.ds(c0, TK)] = dneg
                res.append((dk, dv))
            if pair:
                dk = jnp.where(lane < HEAD_DIM, res[0][0], res[1][0])
                dv = jnp.where(lane < HEAD_DIM, res[0][1], res[1][1])
                if has_rope:
                    dk = _rope_bwd(dk, *[t[pl.ds(c0, TK), :] for t in rope_refs])
                dqkv_ref[0, pl.ds(c0, TK), LANES:2 * LANES] = dk.astype(BF16)
                dqkv_ref[0, pl.ds(c0, TK), 2 * LANES:3 * LANES] = dv.astype(BF16)
            else:
                dk_ref[0, pl.ds(c0, TK), :] = res[0][0].astype(BF16)
                dv_ref[0, pl.ds(c0, TK), :] = res[0][1].astype(BF16)
            return 0

        lax.fori_loop(0, nk, kv_loop, 0)

        def fin_q(n, _):
            r0 = pl.multiple_of(n * TQ, TQ)
            rows = pl.ds(r0, TQ)
            dq = dq_acc[rows, :]
            if q_fold is not None:
                dq = dq * q_fold
            if has_rope:
                dq = _rope_bwd(dq, *[t[rows, :] for t in rope_refs])
            if pair:
                dqkv_ref[0, rows, 0:LANES] = dq.astype(BF16)
            else:
                dq_ref[0, rows, :] = dq.astype(BF16)
            if has_bias:
                drow_ref[0, rows, :] = drow_acc[rows, :]
            return 0

        lax.fori_loop(0, nq, fin_q, 0)

    ins, in_specs = [], []
    if pair:
        ins.append(src)
        in_specs.append(pl.BlockSpec((1, S, PAIR_W), lambda b, h: (b, 0, col0 // PAIR_W + h)))
    else:
        ins += [src, kv, kv]
        in_specs += [pl.BlockSpec((1, S, LANES), lambda b, h: (b, 0, col0 // LANES + h)),
                     pl.BlockSpec((1, MEM_LEN, LANES), lambda b, h: (b, 0, h)),
                     pl.BlockSpec((1, MEM_LEN, LANES), lambda b, h: (b, 0, MEM_HEADS + h))]
    row_spec = pl.BlockSpec((1, S, LANES), lambda b, h: (b, 0, h))
    ins += [do, o, lse]
    in_specs += [row_spec] * 3
    if has_bias:
        ins.append(negc)
        in_specs.append(pl.BlockSpec((1, 1, 2, S), lambda b, h: (b, h, 0, 0)))
    if pair:
        ins.append(mask)
        in_specs.append(pl.BlockSpec(mask.shape, lambda b, h: (0, 0, 0)))
    if has_rope:
        ins += list(rope)
        in_specs += [pl.BlockSpec((S, LANES), lambda b, h: (0, 0))] * 3
    W = n_blocks * LANES
    if pair:
        out_specs = [pl.BlockSpec((1, S, PAIR_W), lambda b, h: (b, 0, h))]
        out_shape = [jax.ShapeDtypeStruct((B, S, 3 * W), BF16)]
        if has_bias:
            out_specs.append(pl.BlockSpec((1, 1, 2, S), lambda b, h: (b, h, 0, 0)))
            out_shape.append(jax.ShapeDtypeStruct((B, LANES // 2, 2, S), F32))
            out_specs.append(row_spec)
            out_shape.append(jax.ShapeDtypeStruct((B, S, W), F32))
    else:
        kv_spec = pl.BlockSpec((1, MEM_LEN, LANES), lambda b, h: (b, 0, h))
        out_specs = [row_spec, kv_spec, kv_spec]
        out_shape = [jax.ShapeDtypeStruct((B, S, W), BF16)] + [jax.ShapeDtypeStruct((B, MEM_LEN, W), BF16)] * 2
    return pl.pallas_call(
        body, name=kind + "_attn_bwd", grid=(B, n_blocks),
        in_specs=in_specs, out_specs=out_specs, out_shape=out_shape,
        scratch_shapes=[pltpu.VMEM((S, LANES), BF16), pltpu.VMEM((Sk, LANES), BF16), pltpu.VMEM((Sk, LANES), BF16),
                        pltpu.VMEM((S, LANES), BF16), pltpu.VMEM((S, LANES), F32), pltpu.VMEM((S, LANES), F32)]
        + ([pltpu.VMEM((S, LANES), F32)] if has_bias else []),
        compiler_params=_params(("arbitrary", "arbitrary")),
    )(*ins)


def _log_masks(S, kind):
    nd = 1 if kind == "causal" else S // TQ
    a = np.arange(TQ)[:, None]
    b = np.arange(TK)[None, :]
    out = np.zeros((nd, TQ, TK), np.float32)
    for d in range(nd):
        delta = d * TQ + a - b
        if kind == "causal":
            m = (delta >= 0).astype(np.float64)
        else:
            m = sum(((delta >= 0) & (delta % dil == 0) & (delta <= w)).astype(np.float64) for w, dil in DILATIONS)
        out[d] = np.where(m > 0, np.log(np.maximum(m, 1.0)), NEG_INF)
    return jnp.asarray(out)


def _attn_setup(kind):
    pair = kind != "mem"
    e_dim = HEAD_DIM if pair else MEM_HEAD_DIM
    q_fold, s_scale = _scale_parts(1.0 / math.sqrt(e_dim))
    return dict(pair=pair, col0={"fox": P_FOX, "dil": P_DIL, "mem": P_MQ}[kind],
                n_blocks=FOX_HEADS // 2 if pair else MEM_HEADS, q_fold=q_fold, s_scale=s_scale,
                nh=2 if pair else 1)


def _attn_inputs(kind, src, S, negc, mask, rope, kv, extra):
    cfg = _attn_setup(kind)
    col0 = cfg["col0"]
    ins, in_specs = [], []
    if cfg["pair"]:
        ins.append(src)
        in_specs.append(pl.BlockSpec((1, S, PAIR_W), lambda b, h: (b, 0, col0 // PAIR_W + h)))
    else:
        ins += [src, kv, kv]
        in_specs += [pl.BlockSpec((1, S, LANES), lambda b, h: (b, 0, col0 // LANES + h)),
                     pl.BlockSpec((1, MEM_LEN, LANES), lambda b, h: (b, 0, h)),
                     pl.BlockSpec((1, MEM_LEN, LANES), lambda b, h: (b, 0, MEM_HEADS + h))]
    ins += list(extra)
    in_specs += [pl.BlockSpec((1, S, LANES), lambda b, h: (b, 0, h))] * len(extra)
    if negc is not None:
        ins.append(negc)
        in_specs.append(pl.BlockSpec((1, 1, 2, S), lambda b, h: (b, h, 0, 0)))
    if mask is not None:
        ins.append(mask)
        in_specs.append(pl.BlockSpec(mask.shape, lambda b, h: (0, 0, 0)))
    if rope is not None:
        ins += list(rope)
        in_specs += [pl.BlockSpec((S, LANES), lambda b, h: (0, 0))] * 3
    return ins, in_specs


def _prep_rows(cfg, rope_refs, lane, load_q, load_kv, qs2, ks, vs, S, Sk):
    nh = cfg["nh"]
    R = nh * TQ

    def prep_q(n, _):
        rows = pl.ds(pl.multiple_of(n * TQ, TQ), TQ)
        q = load_q(rows)
        if rope_refs is not None:
            q = _rope(q, *[t[rows, :] for t in rope_refs])
        if cfg["q_fold"] is not None:
            q = q * cfg["q_fold"]
        _store_stacked(cfg, lane, qs2, n, q.astype(BF16))
        return 0

    def prep_kv(n, _):
        rows = pl.ds(pl.multiple_of(n * TK, TK), TK)
        k, v = load_kv(rows)
        if rope_refs is not None:
            k = _rope(k, *[t[rows, :] for t in rope_refs])
        ks[rows, :] = k.astype(BF16)
        vs[rows, :] = v.astype(BF16)
        return 0

    lax.fori_loop(0, S // TQ, prep_q, 0)
    lax.fori_loop(0, Sk // TK, prep_kv, 0)


def _store_stacked(cfg, lane, dst, n, val):
    nh = cfg["nh"]
    R = nh * TQ
    if nh == 1:
        dst[pl.ds(pl.multiple_of(n * R, R), TQ), :] = val
        return
    for hh in range(nh):
        hmask = (lane >= HEAD_DIM * hh) & (lane < HEAD_DIM * (hh + 1))
        dst[pl.ds(pl.multiple_of(n * R + hh * TQ, TQ), TQ), :] = jnp.where(hmask, val, jnp.zeros_like(val))


def _cat(parts, axis):
    return parts[0] if len(parts) == 1 else jnp.concatenate(parts, axis=axis)


def attn_fwd2(kind, src, S, *, negc=None, mask=None, rope=None, kv=None):
    B = src.shape[0]
    cfg = _attn_setup(kind)
    pair, nh, s_scale = cfg["pair"], cfg["nh"], cfg["s_scale"]
    Sk = S if pair else MEM_LEN
    has_bias, has_rope = negc is not None, rope is not None
    R = nh * TQ

    def body(*refs):
        refs = list(refs)
        if pair:
            qkv_ref = refs.pop(0)
        else:
            q_ref, k_ref, v_ref = refs.pop(0), refs.pop(0), refs.pop(0)
        negc_ref = refs.pop(0) if has_bias else None
        mask_ref = refs.pop(0) if mask is not None else None
        rope_refs = [refs.pop(0) for _ in range(3)] if has_rope else None
        o_ref, lse_ref, qs2, ks, vs = refs
        lane = lax.broadcasted_iota(jnp.int32, (1, LANES), 1)

        if pair:
            load_q = lambda rows: qkv_ref[0, rows, 0:LANES]
            load_kv = lambda rows: (qkv_ref[0, rows, LANES:2 * LANES], qkv_ref[0, rows, 2 * LANES:3 * LANES])
        else:
            load_q = lambda rows: q_ref[0, rows, :]
            load_kv = lambda rows: (k_ref[0, rows, :], v_ref[0, rows, :])
        _prep_rows(cfg, rope_refs, lane, load_q, load_kv, qs2, ks, vs, S, Sk)

        def q_loop(i, _):
            q2 = qs2[pl.ds(pl.multiple_of(i * R, R), R), :]

            def step(j, carry, midx):
                ms, ls, acc = carry
                c0 = pl.multiple_of(j * TK, TK)
                k = ks[pl.ds(c0, TK), :]
                v = vs[pl.ds(c0, TK), :]
                s2 = lax.dot_general(q2, k, (((1,), (1,)), ((), ())), preferred_element_type=F32)
                if s_scale is not None:
                    s2 = s2 * s_scale
                new_m, new_l, ps, alphas = [], [], [], []
                for hh in range(nh):
                    s = s2[hh * TQ:(hh + 1) * TQ]
                    if has_bias:
                        s = s + negc_ref[0, 0, pl.ds(hh, 1), pl.ds(c0, TK)]
                    if midx is not None:
                        s = s + mask_ref[midx]
                    m_new = jnp.maximum(ms[hh], jnp.max(s, axis=1, keepdims=True))
                    p = jnp.exp(s - m_new)
                    alpha = jnp.exp(ms[hh] - m_new)
                    new_l.append(alpha * ls[hh] + jnp.sum(p, axis=1, keepdims=True))
                    new_m.append(m_new)
                    ps.append(p.astype(BF16))
                    alphas.append(alpha)
                acc = acc * _cat(alphas, 0) + jnp.dot(_cat(ps, 0), v, preferred_element_type=F32)
                return tuple(new_m), tuple(new_l), acc

            init = (tuple(jnp.full((TQ, 1), NEG_INF, F32) for _ in range(nh)),
                    tuple(jnp.zeros((TQ, 1), F32) for _ in range(nh)), jnp.zeros((R, LANES), F32))
            if kind == "fox":
                carry = lax.fori_loop(0, i, lambda j, c: step(j, c, None), init)
                carry = step(i, carry, 0)
            elif kind == "dil":
                carry = lax.fori_loop(0, i + 1, lambda j, c: step(j, c, i - j), init)
            else:
                carry = lax.fori_loop(0, Sk // TK, lambda j, c: step(j, c, None), init)
            ms, ls, acc = carry
            outs = [acc[hh * TQ:(hh + 1) * TQ] / ls[hh] for hh in range(nh)]
            lses = [ms[hh] + jnp.log(ls[hh]) for hh in range(nh)]
            rows = pl.ds(pl.multiple_of(i * TQ, TQ), TQ)
            if pair:
                o_ref[0, rows, :] = jnp.where(lane < HEAD_DIM, outs[0], outs[1])
                lse_ref[0, rows, :] = jnp.where(lane < HEAD_DIM, lses[0], lses[1])
            else:
                o_ref[0, rows, :] = outs[0]
                lse_ref[0, rows, :] = jnp.broadcast_to(lses[0], (TQ, LANES))
            return 0

        lax.fori_loop(0, S // TQ, q_loop, 0)

    ins, in_specs = _attn_inputs(kind, src, S, negc, mask, rope, kv, ())
    W = cfg["n_blocks"] * LANES
    out_spec = pl.BlockSpec((1, S, LANES), lambda b, h: (b, 0, h))
    return pl.pallas_call(
        body, name=kind + "_attn_fwd", grid=(B, cfg["n_blocks"]),
        in_specs=in_specs, out_specs=[out_spec, out_spec],
        out_shape=[jax.ShapeDtypeStruct((B, S, W), F32)] * 2,
        scratch_shapes=[pltpu.VMEM((nh * S, LANES), BF16), pltpu.VMEM((Sk, LANES), BF16),
                        pltpu.VMEM((Sk, LANES), BF16)],
        compiler_params=_params(("arbitrary", "arbitrary")),
    )(*ins)


def attn_bwd2(kind, src, do, o, lse, S, *, negc=None, mask=None, rope=None, kv=None):
    B = src.shape[0]
    cfg = _attn_setup(kind)
    pair, nh, s_scale, q_fold = cfg["pair"], cfg["nh"], cfg["s_scale"], cfg["q_fold"]
    Sk = S if pair else MEM_LEN
    has_bias, has_rope = negc is not None, rope is not None
    R = nh * TQ
    nq, nk = S // TQ, Sk // TK

    def body(*refs):
        refs = list(refs)
        if pair:
            qkv_ref = refs.pop(0)
        else:
            q_ref, k_ref, v_ref = refs.pop(0), refs.pop(0), refs.pop(0)
        do_ref, o_ref, lse_ref = refs.pop(0), refs.pop(0), refs.pop(0)
        negc_ref = refs.pop(0) if has_bias else None
        mask_ref = refs.pop(0) if mask is not None else None
        rope_refs = [refs.pop(0) for _ in range(3)] if has_rope else None
        if pair:
            dqkv_ref = refs.pop(0)
            dnegc_ref = refs.pop(0) if has_bias else None
            drow_ref = refs.pop(0) if has_bias else None
        else:
            dq_ref, dk_ref, dv_ref = refs.pop(0), refs.pop(0), refs.pop(0)
        qs2, ks, vs, dos2, lse_s, delta_s, dk_acc, dv_acc = refs[:8]
        dneg_acc = refs[8] if has_bias else None
        lane = lax.broadcasted_iota(jnp.int32, (1, LANES), 1)

        if pair:
            load_q = lambda rows: qkv_ref[0, rows, 0:LANES]
            load_kv = lambda rows: (qkv_ref[0, rows, LANES:2 * LANES], qkv_ref[0, rows, 2 * LANES:3 * LANES])
        else:
            load_q = lambda rows: q_ref[0, rows, :]
            load_kv = lambda rows: (k_ref[0, rows, :], v_ref[0, rows, :])
        _prep_rows(cfg, rope_refs, lane, load_q, load_kv, qs2, ks, vs, S, Sk)

        def prep_do(n, _):
            rows = pl.ds(pl.multiple_of(n * TQ, TQ), TQ)
            dob = do_ref[0, rows, :].astype(BF16)
            _store_stacked(cfg, lane, dos2, n, dob)
            prod = dob.astype(F32) * o_ref[0, rows, :]
            lse_blk = lse_ref[0, rows, :]
            for hh in range(nh):
                dst = pl.ds(pl.multiple_of(n * R + hh * TQ, TQ), TQ)
                if pair:
                    hmask = (lane >= HEAD_DIM * hh) & (lane < HEAD_DIM * (hh + 1))
                    d = jnp.sum(jnp.where(hmask, prod, 0.0), axis=1, keepdims=True)
                    lse_s[dst, :] = jnp.broadcast_to(lse_blk[:, hh * HEAD_DIM:hh * HEAD_DIM + 1], (TQ, LANES))
                else:
                    d = jnp.sum(prod, axis=1, keepdims=True)
                    lse_s[dst, :] = lse_blk
                delta_s[dst, :] = jnp.broadcast_to(d, (TQ, LANES))
            return 0

        def zero_kv(n, _):
            rows = pl.ds(pl.multiple_of(n * TK, TK), TK)
            dk_acc[rows, :] = jnp.zeros((TK, LANES), F32)
            dv_acc[rows, :] = jnp.zeros((TK, LANES), F32)
            return 0

        lax.fori_loop(0, nq, prep_do, 0)
        lax.fori_loop(0, nk, zero_kv, 0)
        if has_bias:
            dneg_acc[...] = jnp.zeros(dneg_acc.shape, F32)

        def q_loop(i, _):
            rows2 = pl.ds(pl.multiple_of(i * R, R), R)
            q2 = qs2[rows2, :]
            do2 = dos2[rows2, :]
            lse2 = lse_s[rows2, :]
            delta2 = delta_s[rows2, :]
            wide = lambda t: jnp.concatenate([t] * (TK // LANES), axis=1)

            def step(j, carry, midx):
                dq2, drow = carry
                c0 = pl.multiple_of(j * TK, TK)
                kcols = pl.ds(c0, TK)
                k = ks[kcols, :]
                v = vs[kcols, :]
                s2 = lax.dot_general(q2, k, (((1,), (1,)), ((), ())), preferred_element_type=F32)
                if s_scale is not None:
                    s2 = s2 * s_scale
                if has_bias or midx is not None:
                    halves = []
                    for hh in range(nh):
                        s = s2[hh * TQ:(hh + 1) * TQ]
                        if has_bias:
                            s = s + negc_ref[0, 0, pl.ds(hh, 1), kcols]
                        if midx is not None:
                            s = s + mask_ref[midx]
                        halves.append(s)
                    s2 = _cat(halves, 0)
                p2 = jnp.exp(s2 - wide(lse2))
                dp2 = lax.dot_general(do2, v, (((1,), (1,)), ((), ())), preferred_element_type=F32)
                ds2 = p2 * (dp2 - wide(delta2))
                if has_bias:
                    drow = drow + jnp.sum(ds2, axis=1, keepdims=True)
                    for hh in range(nh):
                        dneg_acc[pl.ds(hh, 1), kcols] += jnp.sum(ds2[hh * TQ:(hh + 1) * TQ], axis=0, keepdims=True)
                if s_scale is not None:
                    ds2 = ds2 * s_scale
                dsb = ds2.astype(BF16)
                dv_acc[kcols, :] += lax.dot_general(p2.astype(BF16), do2, (((0,), (0,)), ((), ())),
                                                    preferred_element_type=F32)
                dk_acc[kcols, :] += lax.dot_general(dsb, q2, (((0,), (0,)), ((), ())), preferred_element_type=F32)
                dq2 = dq2 + jnp.dot(dsb, k, preferred_element_type=F32)
                return dq2, drow

            init = (jnp.zeros((R, LANES), F32), jnp.zeros((R, 1), F32))
            if kind == "fox":
                carry = lax.fori_loop(0, i, lambda j, c: step(j, c, None), init)
                carry = step(i, carry, 0)
            elif kind == "dil":
                carry = lax.fori_loop(0, i + 1, lambda j, c: step(j, c, i - j), init)
            else:
                carry = lax.fori_loop(0, nk, lambda j, c: step(j, c, None), init)
            dq2, drow = carry
            rows = pl.ds(pl.multiple_of(i * TQ, TQ), TQ)
            dq = jnp.where(lane < HEAD_DIM, dq2[0:TQ], dq2[TQ:2 * TQ]) if pair else dq2
            if q_fold is not None:
                dq = dq * q_fold
            if has_rope:
                dq = _rope_bwd(dq, *[t[rows, :] for t in rope_refs])
            if pair:
                dqkv_ref[0, rows, 0:LANES] = dq.astype(BF16)
            else:
                dq_ref[0, rows, :] = dq.astype(BF16)
            if has_bias:
                drow_ref[0, rows, :] = jnp.where(lane < HEAD_DIM, drow[0:TQ], drow[TQ:2 * TQ])
            return 0

        lax.fori_loop(0, nq, q_loop, 0)

        def fin_kv(n, _):
            rows = pl.ds(pl.multiple_of(n * TK, TK), TK)
            dk = dk_acc[rows, :]
            if has_rope:
                dk = _rope_bwd(dk, *[t[rows, :] for t in rope_refs])
            if pair:
                dqkv_ref[0, rows, LANES:2 * LANES] = dk.astype(BF16)
                dqkv_ref[0, rows, 2 * LANES:3 * LANES] = dv_acc[rows, :].astype(BF16)
            else:
                dk_ref[0, rows, :] = dk.astype(BF16)
                dv_ref[0, rows, :] = dv_acc[rows, :].astype(BF16)
            return 0

        lax.fori_loop(0, nk, fin_kv, 0)
        if has_bias:
            dnegc_ref[0, 0] = dneg_acc[...]

    ins, in_specs = _attn_inputs(kind, src, S, negc, mask, rope, kv, (do, o, lse))
    W = cfg["n_blocks"] * LANES
    row_spec = pl.BlockSpec((1, S, LANES), lambda b, h: (b, 0, h))
    if pair:
        out_specs = [pl.BlockSpec((1, S, PAIR_W), lambda b, h: (b, 0, h))]
        out_shape = [jax.ShapeDtypeStruct((B, S, 3 * W), BF16)]
        if has_bias:
            out_specs += [pl.BlockSpec((1, 1, 2, S), lambda b, h: (b, h, 0, 0)), row_spec]
            out_shape += [jax.ShapeDtypeStruct((B, LANES // 2, 2, S), F32), jax.ShapeDtypeStruct((B, S, W), F32)]
    else:
        kv_spec = pl.BlockSpec((1, MEM_LEN, LANES), lambda b, h: (b, 0, h))
        out_specs = [row_spec, kv_spec, kv_spec]
        out_shape = [jax.ShapeDtypeStruct((B, S, W), BF16)] + [jax.ShapeDtypeStruct((B, MEM_LEN, W), BF16)] * 2
    scratch = [pltpu.VMEM((nh * S, LANES), BF16), pltpu.VMEM((Sk, LANES), BF16), pltpu.VMEM((Sk, LANES), BF16),
               pltpu.VMEM((nh * S, LANES), BF16), pltpu.VMEM((nh * S, LANES), F32), pltpu.VMEM((nh * S, LANES), F32),
               pltpu.VMEM((Sk, LANES), F32), pltpu.VMEM((Sk, LANES), F32)]
    if has_bias:
        scratch.append(pltpu.VMEM((2, S), F32))
    return pl.pallas_call(
        body, name=kind + "_attn_bwd", grid=(B, cfg["n_blocks"]),
        in_specs=in_specs, out_specs=out_specs, out_shape=out_shape, scratch_shapes=scratch,
        compiler_params=_params(("arbitrary", "arbitrary")),
    )(*ins)


def _log_masks_t(S, kind):
    return jnp.swapaxes(_log_masks(S, kind), 1, 2)


def _head_rows(hh, pair):
    row = lax.broadcasted_iota(jnp.int32, (LANES, 1), 0)
    if not pair:
        return row >= 0
    return (row >= HEAD_DIM * hh) & (row < HEAD_DIM * (hh + 1))


def _attn_t_inputs(kind, src, S, negc_cols, mask, rope, kv):
    cfg = _attn_setup(kind)
    col0 = cfg["col0"]
    ins, in_specs = [], []
    if cfg["pair"]:
        ins.append(src)
        in_specs.append(pl.BlockSpec((1, S, PAIR_W), lambda b, h: (b, 0, col0 // PAIR_W + h)))
    else:
        ins += [src, kv, kv]
        in_specs += [pl.BlockSpec((1, S, LANES), lambda b, h: (b, 0, col0 // LANES + h)),
                     pl.BlockSpec((1, MEM_LEN, LANES), lambda b, h: (b, 0, h)),
                     pl.BlockSpec((1, MEM_LEN, LANES), lambda b, h: (b, 0, MEM_HEADS + h))]
    if negc_cols is not None:
        ins.append(negc_cols)
        in_specs.append(pl.BlockSpec((1, S, LANES), lambda b, h: (b, 0, 0)))
    if mask is not None:
        ins.append(mask)
        in_specs.append(pl.BlockSpec(mask.shape, lambda b, h: (0, 0, 0)))
    if rope is not None:
        ins += list(rope)
        in_specs += [pl.BlockSpec((S, LANES), lambda b, h: (0, 0))] * 3
    return ins, in_specs


def _attn_t_prep(cfg, refs, S, Sk, *, qT2s, ks, q2s=None, vs=None, vTs=None, kTs=None, nb=None):
    pair, nh = cfg["pair"], cfg["nh"]
    lane = lax.broadcasted_iota(jnp.int32, (1, LANES), 1)
    rope_refs = refs["rope"]

    def prep_q(n, _):
        rows = pl.ds(pl.multiple_of(n * TQ, TQ), TQ)
        q = refs["load_q"](rows)
        if rope_refs is not None:
            q = _rope(q, *[t[rows, :] for t in rope_refs])
        if cfg["q_fold"] is not None:
            q = q * cfg["q_fold"]
        qb = q.astype(BF16)
        if q2s is not None:
            _store_stacked(cfg, lane, q2s, n, qb)
        qtb = qb.astype(F32).T.astype(BF16)
        for hh in range(nh):
            qT2s[n, :, hh * TQ:(hh + 1) * TQ] = jnp.where(_head_rows(hh, pair), qtb, jnp.zeros_like(qtb))
        return 0

    def prep_kv(n, _):
        rows = pl.ds(pl.multiple_of(n * TK, TK), TK)
        k, v = refs["load_kv"](rows)
        if rope_refs is not None:
            k = _rope(k, *[t[rows, :] for t in rope_refs])
        kb = k.astype(BF16)
        vb = v.astype(BF16)
        ks[rows, :] = kb
        if vs is not None:
            vs[rows, :] = vb
        if vTs is not None:
            vTs[n] = vb.astype(F32).T.astype(BF16)
        if kTs is not None:
            kTs[n] = kb.astype(F32).T.astype(BF16)
        if nb is not None:
            blk = refs["negc"][0, rows, :]
            for hh in range(nh):
                h = 2 * refs["block"] + hh
                col = jnp.sum(jnp.where(lane == h, blk, 0.0), axis=1, keepdims=True)
                nb[hh, rows, :] = jnp.broadcast_to(col, (TK, LANES))
        return 0

    lax.fori_loop(0, S // TQ, prep_q, 0)
    lax.fori_loop(0, Sk // TK, prep_kv, 0)


def _raw_scores_t(cfg, k, qT2):
    sT = jnp.dot(k, qT2, preferred_element_type=F32)
    if cfg["s_scale"] is not None:
        sT = sT * cfg["s_scale"]
    return sT


def _bias_mask_t(cfg, sT, nb, mask_ref, kc, midx):
    nh = cfg["nh"]
    if nb is None and midx is None:
        return sT
    parts = []
    for hh in range(nh):
        t = sT[:, hh * TQ:(hh + 1) * TQ]
        if nb is not None:
            t = t + jnp.concatenate([nb[hh, kc, :]] * (TQ // LANES), axis=1)
        if midx is not None:
            t = t + mask_ref[midx]
        parts.append(t)
    return _cat(parts, 1)


def _kv_plan(kind, i, nk):
    if kind == "fox":
        return i, (lambda j: None), 0
    if kind == "dil":
        return i, (lambda j: i - j), 0
    return nk - 1, (lambda j: None), None


def attn_fwd3(kind, src, S, *, negc_cols=None, mask=None, rope=None, kv=None):
    B = src.shape[0]
    cfg = _attn_setup(kind)
    pair, nh = cfg["pair"], cfg["nh"]
    Sk = S if pair else MEM_LEN
    has_bias, has_rope = negc_cols is not None, rope is not None
    R = nh * TQ
    nq, nk = S // TQ, Sk // TK

    def body(*refs):
        refs = list(refs)
        if pair:
            qkv_ref = refs.pop(0)
            load_q = lambda rows: qkv_ref[0, rows, 0:LANES]
            load_kv = lambda rows: (qkv_ref[0, rows, LANES:2 * LANES], qkv_ref[0, rows, 2 * LANES:3 * LANES])
        else:
            q_ref, k_ref, v_ref = refs.pop(0), refs.pop(0), refs.pop(0)
            load_q = lambda rows: q_ref[0, rows, :]
            load_kv = lambda rows: (k_ref[0, rows, :], v_ref[0, rows, :])
        negc_ref = refs.pop(0) if has_bias else None
        mask_ref = refs.pop(0) if mask is not None else None
        rope_refs = [refs.pop(0) for _ in range(3)] if has_rope else None
        o_ref, lse_ref, qT2s, ks, vTs = refs[:5]
        nb = refs[5] if has_bias else None
        _attn_t_prep(cfg, dict(load_q=load_q, load_kv=load_kv, rope=rope_refs, negc=negc_ref,
                               block=pl.program_id(1)), S, Sk,
                     qT2s=qT2s, ks=ks, vTs=vTs, nb=nb)

        def q_loop(i, _):
            qT2 = qT2s[i]

            last, mask_of, mask_last = _kv_plan(kind, i, nk)

            def cols(j):
                return pl.ds(pl.multiple_of(j * TK, TK), TK)

            def scores(j):
                return _raw_scores_t(cfg, ks[cols(j), :], qT2)

            def soft(s_raw, j, midx, m, l):
                sT = _bias_mask_t(cfg, s_raw, nb, mask_ref, cols(j), midx)
                m_new = jnp.maximum(m, jnp.max(sT, axis=0, keepdims=True))
                p = jnp.exp(sT - m_new)
                alpha = jnp.exp(m - m_new)
                return m_new, alpha * l + jnp.sum(p, axis=0, keepdims=True), alpha, p.astype(BF16)

            def pv(j, p):
                return jnp.dot(vTs[j], p, preferred_element_type=F32)

            def body(j, carry):
                s_cur, p_prev, m, l, accT = carry
                pv_prev = pv(jnp.maximum(j - 1, 0), p_prev)
                s_next = scores(j + 1)
                m, l, alpha, p = soft(s_cur, j, mask_of(j), m, l)
                return s_next, p, m, l, (accT + pv_prev) * alpha

            init = (scores(0), jnp.zeros((TK, R), BF16), jnp.full((1, R), NEG_INF, F32), jnp.zeros((1, R), F32),
                    jnp.zeros((LANES, R), F32))
            s_cur, p_prev, m, l, accT = lax.fori_loop(0, last, body, init)
            pv_prev = pv(jnp.maximum(last - 1, 0), p_prev)
            m, l, alpha, p = soft(s_cur, last, mask_last, m, l)
            accT = (accT + pv_prev) * alpha + pv(last, p)
            oT2 = accT / l
            oT = jnp.where(_head_rows(0, True), oT2[:, 0:TQ], oT2[:, TQ:2 * TQ]) if pair else oT2
            o_ref[0, pl.ds(pl.multiple_of(i * TQ, TQ), TQ), :] = oT.T
            lse_ref[0, 0, pl.ds(i, 1), :] = m + jnp.log(l)
            return 0

        lax.fori_loop(0, nq, q_loop, 0)

    ins, in_specs = _attn_t_inputs(kind, src, S, negc_cols, mask, rope, kv)
    W = cfg["n_blocks"] * LANES
    scratch = [pltpu.VMEM((nq, LANES, R), BF16), pltpu.VMEM((Sk, LANES), BF16), pltpu.VMEM((nk, LANES, TK), BF16)]
    if has_bias:
        scratch.append(pltpu.VMEM((nh, Sk, LANES), F32))
    return pl.pallas_call(
        body, name=kind + "_attn_fwd", grid=(B, cfg["n_blocks"]),
        in_specs=in_specs,
        out_specs=[pl.BlockSpec((1, S, LANES), lambda b, h: (b, 0, h)),
                   pl.BlockSpec((1, 1, nq, R), lambda b, h: (b, h, 0, 0))],
        out_shape=[jax.ShapeDtypeStruct((B, S, W), F32), jax.ShapeDtypeStruct((B, cfg["n_blocks"], nq, R), F32)],
        scratch_shapes=scratch,
        compiler_params=_params(("arbitrary", "arbitrary")),
    )(*ins)


def _tile_walk(kind, nq, nk):
    if kind == "mem":
        return nq * nk, (lambda i, j: (jnp.where(j < nk - 1, i, i + 1), jnp.where(j < nk - 1, j + 1, 0))), None
    nxt = lambda i, j: (jnp.where(j < i, i, i + 1), jnp.where(j < i, j + 1, 0))
    if kind == "fox":
        return nq * (nq + 1) // 2, nxt, (lambda i, j: jnp.where(j == i, 0, 1))
    return nq * (nq + 1) // 2, nxt, (lambda i, j: i - j)


def attn_fwd4(kind, src, S, *, negc_cols=None, mask=None, rope=None, kv=None):
    B = src.shape[0]
    cfg = _attn_setup(kind)
    pair, nh = cfg["pair"], cfg["nh"]
    Sk = S if pair else MEM_LEN
    has_bias, has_rope = negc_cols is not None, rope is not None
    R = nh * TQ
    nq, nk = S // TQ, Sk // TK
    n_pairs, successor, mask_index = _tile_walk(kind, nq, nk)
    assert n_pairs % 2 == 0

    def body(*refs):
        refs = list(refs)
        if pair:
            qkv_ref = refs.pop(0)
            load_q = lambda rows: qkv_ref[0, rows, 0:LANES]
            load_kv = lambda rows: (qkv_ref[0, rows, LANES:2 * LANES], qkv_ref[0, rows, 2 * LANES:3 * LANES])
        else:
            q_ref, k_ref, v_ref = refs.pop(0), refs.pop(0), refs.pop(0)
            load_q = lambda rows: q_ref[0, rows, :]
            load_kv = lambda rows: (k_ref[0, rows, :], v_ref[0, rows, :])
        negc_ref = refs.pop(0) if has_bias else None
        mask_ref = refs.pop(0) if mask is not None else None
        rope_refs = [refs.pop(0) for _ in range(3)] if has_rope else None
        o_ref, lse_ref, qT2s, ks, vTs, s_a, s_b, p_a, p_b, acc_all, m_all, l_all = refs[:12]
        nb = refs[12] if has_bias else None
        _attn_t_prep(cfg, dict(load_q=load_q, load_kv=load_kv, rope=rope_refs, negc=negc_ref,
                               block=pl.program_id(1)), S, Sk, qT2s=qT2s, ks=ks, vTs=vTs, nb=nb)

        def cols(j):
            return pl.ds(pl.multiple_of(j * TK, TK), TK)

        def park(i, m, l, accT):
            acc_all[i] = accT
            m_all[pl.ds(i, 1), :] = m
            l_all[pl.ds(i, 1), :] = l

        def finish(i, _):
            l = l_all[pl.ds(i, 1), :]
            oT2 = acc_all[i] / l
            oT = jnp.where(_head_rows(0, True), oT2[:, 0:TQ], oT2[:, TQ:2 * TQ]) if pair else oT2
            o_ref[0, pl.ds(pl.multiple_of(i * TQ, TQ), TQ), :] = oT.T
            lse_ref[0, 0, pl.ds(i, 1), :] = m_all[pl.ds(i, 1), :] + jnp.log(l)
            return 0

        def half(i, j, i_prev, j_prev, s_cur, s_next, p_cur, p_prev, m, l, accT):
            i_n, j_n = successor(i, j)
            acc_full = accT + jnp.dot(vTs[j_prev], p_prev[...], preferred_element_type=F32)
            s_next[...] = _raw_scores_t(cfg, ks[cols(j_n), :], qT2s[jnp.minimum(i_n, nq - 1)])
            park(i_prev, m, l, acc_full)
            first = j == 0
            m = jnp.where(first, NEG_INF, m)
            l = jnp.where(first, 0.0, l)
            sT = _bias_mask_t(cfg, s_cur[...], nb, mask_ref, cols(j), None if mask_index is None else mask_index(i, j))
            m_new = jnp.maximum(m, jnp.max(sT, axis=0, keepdims=True))
            p = jnp.exp(sT - m_new)
            alpha = jnp.exp(m - m_new)
            p_cur[...] = p.astype(BF16)
            return i_n, j_n, i, j, m_new, alpha * l + jnp.sum(p, axis=0, keepdims=True), acc_full * alpha

        def two(_, carry):
            i, j, i_prev, j_prev, m, l, accT = carry
            i, j, i_prev, j_prev, m, l, accT = half(i, j, i_prev, j_prev, s_a, s_b, p_a, p_b, m, l, accT)
            return half(i, j, i_prev, j_prev, s_b, s_a, p_b, p_a, m, l, accT)

        s_a[...] = _raw_scores_t(cfg, ks[cols(0), :], qT2s[0])
        p_b[...] = jnp.zeros((TK, R), BF16)
        zero = jnp.int32(0)
        init = (zero, zero, zero, zero, jnp.full((1, R), NEG_INF, F32), jnp.ones((1, R), F32),
                jnp.zeros((LANES, R), F32))
        _, _, i_prev, j_prev, m, l, accT = lax.fori_loop(0, n_pairs // 2, two, init)
        park(i_prev, m, l, accT + jnp.dot(vTs[j_prev], p_b[...], preferred_element_type=F32))
        lax.fori_loop(0, nq, finish, 0)

    ins, in_specs = _attn_t_inputs(kind, src, S, negc_cols, mask, rope, kv)
    W = cfg["n_blocks"] * LANES
    scratch = [pltpu.VMEM((nq, LANES, R), BF16), pltpu.VMEM((Sk, LANES), BF16), pltpu.VMEM((nk, LANES, TK), BF16),
               pltpu.VMEM((TK, R), F32), pltpu.VMEM((TK, R), F32), pltpu.VMEM((TK, R), BF16), pltpu.VMEM((TK, R), BF16),
               pltpu.VMEM((nq, LANES, R), F32), pltpu.VMEM((nq, R), F32), pltpu.VMEM((nq, R), F32)]
    if has_bias:
        scratch.append(pltpu.VMEM((nh, Sk, LANES), F32))
    return pl.pallas_call(
        body, name=kind + "_attn_fwd", grid=(B, cfg["n_blocks"]),
        in_specs=in_specs,
        out_specs=[pl.BlockSpec((1, S, LANES), lambda b, h: (b, 0, h)),
                   pl.BlockSpec((1, 1, nq, R), lambda b, h: (b, h, 0, 0))],
        out_shape=[jax.ShapeDtypeStruct((B, S, W), F32), jax.ShapeDtypeStruct((B, cfg["n_blocks"], nq, R), F32)],
        scratch_shapes=scratch,
        compiler_params=_params(("arbitrary", "arbitrary")),
    )(*ins)


def attn_bwd3(kind, src, do, o, lse, S, *, negc_cols=None, mask=None, rope=None, kv=None):
    B = src.shape[0]
    cfg = _attn_setup(kind)
    pair, nh, s_scale, q_fold = cfg["pair"], cfg["nh"], cfg["s_scale"], cfg["q_fold"]
    Sk = S if pair else MEM_LEN
    has_bias, has_rope = negc_cols is not None, rope is not None
    R = nh * TQ
    nq, nk = S // TQ, Sk // TK
    n_pairs, successor, mask_index = _tile_walk(kind, nq, nk)
    assert n_pairs % 2 == 0

    def body(*refs):
        refs = list(refs)
        if pair:
            qkv_ref = refs.pop(0)
            load_q = lambda rows: qkv_ref[0, rows, 0:LANES]
            load_kv = lambda rows: (qkv_ref[0, rows, LANES:2 * LANES], qkv_ref[0, rows, 2 * LANES:3 * LANES])
        else:
            q_ref, k_ref, v_ref = refs.pop(0), refs.pop(0), refs.pop(0)
            load_q = lambda rows: q_ref[0, rows, :]
            load_kv = lambda rows: (k_ref[0, rows, :], v_ref[0, rows, :])
        negc_ref = refs.pop(0) if has_bias else None
        mask_ref = refs.pop(0) if mask is not None else None
        rope_refs = [refs.pop(0) for _ in range(3)] if has_rope else None
        do_ref, o_ref, lse_ref = refs.pop(0), refs.pop(0), refs.pop(0)
        if pair:
            dqkv_ref = refs.pop(0)
            dneg_ref = refs.pop(0) if has_bias else None
            drow_ref = refs.pop(0) if has_bias else None
        else:
            dq_ref, dk_ref, dv_ref = refs.pop(0), refs.pop(0), refs.pop(0)
        qT2s, ks, q2s, vs, kTs, doT2s, do2s, delta_s, dk_acc, dv_acc = refs[:10]
        bufs_a, bufs_b, dq_all = refs[10:14], refs[14:18], refs[18]
        nb, dneg_acc, drow_all = (refs[19], refs[20], refs[21]) if has_bias else (None, None, None)
        lane = lax.broadcasted_iota(jnp.int32, (1, LANES), 1)
        _attn_t_prep(cfg, dict(load_q=load_q, load_kv=load_kv, rope=rope_refs, negc=negc_ref,
                               block=pl.program_id(1)), S, Sk,
                     qT2s=qT2s, ks=ks, q2s=q2s, vs=vs, kTs=kTs, nb=nb)

        def prep_do(n, _):
            rows = pl.ds(pl.multiple_of(n * TQ, TQ), TQ)
            dob = do_ref[0, rows, :].astype(BF16)
            _store_stacked(cfg, lane, do2s, n, dob)
            doT = dob.astype(F32).T
            prodT = doT * o_ref[0, rows, :].T
            doTb = doT.astype(BF16)
            for hh in range(nh):
                hm = _head_rows(hh, pair)
                doT2s[n, :, hh * TQ:(hh + 1) * TQ] = jnp.where(hm, doTb, jnp.zeros_like(doTb))
                delta_s[pl.ds(n, 1), hh * TQ:(hh + 1) * TQ] = jnp.sum(jnp.where(hm, prodT, 0.0), axis=0, keepdims=True)
            return 0

        def zero_kv(n, _):
            rows = pl.ds(pl.multiple_of(n * TK, TK), TK)
            dk_acc[rows, :] = jnp.zeros((TK, LANES), F32)
            dv_acc[rows, :] = jnp.zeros((TK, LANES), F32)
            if has_bias:
                for hh in range(nh):
                    dneg_acc[hh, rows, :] = jnp.zeros((TK, LANES), F32)
            return 0

        lax.fori_loop(0, nq, prep_do, 0)
        lax.fori_loop(0, nk, zero_kv, 0)

        def cols(j):
            return pl.ds(pl.multiple_of(j * TK, TK), TK)

        def rows2(i):
            return pl.ds(pl.multiple_of(i * R, R), R)

        def park(i, dqT2, drow):
            dq_all[i] = dqT2
            if has_bias:
                drow_all[pl.ds(i, 1), :] = drow

        def half(i, j, i_prev, j_prev, cur, nxt_bufs, prv, dqT2, drow):
            s_cur, dp_cur, pb_cur, dsb_cur = cur
            s_next, dp_next = nxt_bufs[0], nxt_bufs[1]
            pb_prev, dsb_prev = prv[2], prv[3]
            i_n, j_n = successor(i, j)
            first = j == 0
            if has_bias:
                drow_all[pl.ds(i_prev, 1), :] = drow
            drow = jnp.where(first, 0.0, drow)
            kc = cols(j)
            sT = _bias_mask_t(cfg, s_cur[...], nb, mask_ref, kc, None if mask_index is None else mask_index(i, j))
            pT = jnp.exp(sT - lse_ref[0, 0, pl.ds(i, 1), :])
            dsT = pT * (dp_cur[...] - delta_s[pl.ds(i, 1), :])
            if has_bias:
                drow = drow + jnp.sum(dsT, axis=0, keepdims=True)
                for hh in range(nh):
                    part = dsT[:, hh * TQ:hh * TQ + LANES]
                    for t in range(1, TQ // LANES):
                        part = part + dsT[:, hh * TQ + t * LANES:hh * TQ + (t + 1) * LANES]
                    dneg_acc[hh, kc, :] += part
            if s_scale is not None:
                dsT = dsT * s_scale
            pb_cur[...] = pT.astype(BF16)
            dsb_cur[...] = dsT.astype(BF16)
            kp = cols(j_prev)
            dv_acc[kp, :] += jnp.dot(pb_prev[...], do2s[rows2(i_prev), :], preferred_element_type=F32)
            dk_acc[kp, :] += jnp.dot(dsb_prev[...], q2s[rows2(i_prev), :], preferred_element_type=F32)
            dq_full = dqT2 + jnp.dot(kTs[j_prev], dsb_prev[...], preferred_element_type=F32)
            dq_all[i_prev] = dq_full
            dqT2 = jnp.where(first, 0.0, dq_full)
            i_nc = jnp.minimum(i_n, nq - 1)
            kn = cols(j_n)
            s_next[...] = _raw_scores_t(cfg, ks[kn, :], qT2s[i_nc])
            dp_next[...] = jnp.dot(vs[kn, :], doT2s[i_nc], preferred_element_type=F32)
            return i_n, j_n, i, j, dqT2, drow

        def two(_, carry):
            i, j, i_prev, j_prev, dqT2, drow = carry
            i, j, i_prev, j_prev, dqT2, drow = half(i, j, i_prev, j_prev, bufs_a, bufs_b, bufs_b, dqT2, drow)
            return half(i, j, i_prev, j_prev, bufs_b, bufs_a, bufs_a, dqT2, drow)

        bufs_a[0][...] = _raw_scores_t(cfg, ks[cols(0), :], qT2s[0])
        bufs_a[1][...] = jnp.dot(vs[cols(0), :], doT2s[0], preferred_element_type=F32)
        bufs_b[2][...] = jnp.zeros((TK, R), BF16)
        bufs_b[3][...] = jnp.zeros((TK, R), BF16)
        zero = jnp.int32(0)
        init = (zero, zero, zero, zero, jnp.zeros((LANES, R), F32), jnp.zeros((1, R), F32))
        _, _, i_prev, j_prev, dqT2, drow = lax.fori_loop(0, n_pairs // 2, two, init)
        kp = cols(j_prev)
        dv_acc[kp, :] += jnp.dot(bufs_b[2][...], do2s[rows2(i_prev), :], preferred_element_type=F32)
        dk_acc[kp, :] += jnp.dot(bufs_b[3][...], q2s[rows2(i_prev), :], preferred_element_type=F32)
        park(i_prev, dqT2 + jnp.dot(kTs[j_prev], bufs_b[3][...], preferred_element_type=F32), drow)

        def fin_q(i, _):
            rows = pl.ds(pl.multiple_of(i * TQ, TQ), TQ)
            dqT2 = dq_all[i]
            dqT = jnp.where(_head_rows(0, True), dqT2[:, 0:TQ], dqT2[:, TQ:2 * TQ]) if pair else dqT2
            dq = dqT.T
            if q_fold is not None:
                dq = dq * q_fold
            if has_rope:
                dq = _rope_bwd(dq, *[t[rows, :] for t in rope_refs])
            if pair:
                dqkv_ref[0, rows, 0:LANES] = dq.astype(BF16)
            else:
                dq_ref[0, rows, :] = dq.astype(BF16)
            if has_bias:
                drow_ref[0, 0, pl.ds(i, 1), :] = drow_all[pl.ds(i, 1), :]
            return 0

        lax.fori_loop(0, nq, fin_q, 0)

        def fin_kv(n, _):
            rows = pl.ds(pl.multiple_of(n * TK, TK), TK)
            dk = dk_acc[rows, :]
            if has_rope:
                dk = _rope_bwd(dk, *[t[rows, :] for t in rope_refs])
            if pair:
                dqkv_ref[0, rows, LANES:2 * LANES] = dk.astype(BF16)
                dqkv_ref[0, rows, 2 * LANES:3 * LANES] = dv_acc[rows, :].astype(BF16)
            else:
                dk_ref[0, rows, :] = dk.astype(BF16)
                dv_ref[0, rows, :] = dv_acc[rows, :].astype(BF16)
            if has_bias:
                x0 = jnp.sum(dneg_acc[0, rows, :], axis=1, keepdims=True)
                x1 = jnp.sum(dneg_acc[1, rows, :], axis=1, keepdims=True)
                dneg_ref[0, rows, :] = jnp.where(lane == 0, x0, jnp.where(lane == 1, x1, 0.0))
            return 0

        lax.fori_loop(0, nk, fin_kv, 0)

    ins, in_specs = _attn_t_inputs(kind, src, S, negc_cols, mask, rope, kv)
    row_spec = pl.BlockSpec((1, S, LANES), lambda b, h: (b, 0, h))
    vec_spec = pl.BlockSpec((1, 1, nq, R), lambda b, h: (b, h, 0, 0))
    ins += [do, o, lse]
    in_specs += [row_spec, row_spec, vec_spec]
    W = cfg["n_blocks"] * LANES
    if pair:
        out_specs = [pl.BlockSpec((1, S, PAIR_W), lambda b, h: (b, 0, h))]
        out_shape = [jax.ShapeDtypeStruct((B, S, 3 * W), BF16)]
        if has_bias:
            out_specs += [row_spec, vec_spec]
            out_shape += [jax.ShapeDtypeStruct((B, S, W), F32), jax.ShapeDtypeStruct((B, cfg["n_blocks"], nq, R), F32)]
    else:
        kv_spec = pl.BlockSpec((1, MEM_LEN, LANES), lambda b, h: (b, 0, h))
        out_specs = [row_spec, kv_spec, kv_spec]
        out_shape = [jax.ShapeDtypeStruct((B, S, W), BF16)] + [jax.ShapeDtypeStruct((B, MEM_LEN, W), BF16)] * 2
    scratch = [pltpu.VMEM((nq, LANES, R), BF16), pltpu.VMEM((Sk, LANES), BF16), pltpu.VMEM((nh * S, LANES), BF16),
               pltpu.VMEM((Sk, LANES), BF16), pltpu.VMEM((nk, LANES, TK), BF16), pltpu.VMEM((nq, LANES, R), BF16),
               pltpu.VMEM((nh * S, LANES), BF16), pltpu.VMEM((nq, R), F32),
               pltpu.VMEM((Sk, LANES), F32), pltpu.VMEM((Sk, LANES), F32)]
    pair_bufs = [pltpu.VMEM((TK, R), F32), pltpu.VMEM((TK, R), F32), pltpu.VMEM((TK, R), BF16), pltpu.VMEM((TK, R), BF16)]
    scratch += pair_bufs + pair_bufs + [pltpu.VMEM((nq, LANES, R), F32)]
    if has_bias:
        scratch += [pltpu.VMEM((nh, Sk, LANES), F32), pltpu.VMEM((nh, Sk, LANES), F32), pltpu.VMEM((nq, R), F32)]
    return pl.pallas_call(
        body, name=kind + "_attn_bwd", grid=(B, cfg["n_blocks"]),
        in_specs=in_specs, out_specs=out_specs, out_shape=out_shape, scratch_shapes=scratch,
        compiler_params=_params(("arbitrary", "arbitrary")),
    )(*ins)


def _sigmoid(g):
    return 1.0 / (1.0 + jnp.exp(-g))


def out_fwd(proj, o_fox, o_dil, o_mem, w_out, x, target, gf, tm):
    T = x.shape[0]

    def body(fg_ref, dg_ref, mg_ref, of_ref, od_ref, om_ref, w_ref, x_ref, t_ref, gf_ref,
             y_ref, dx_ref, dxb_ref, sm_ref):
        parts = []
        for g_ref, o_ref in ((fg_ref, of_ref), (dg_ref, od_ref), (mg_ref, om_ref)):
            g = g_ref[...]
            parts.append((o_ref[...] * (g * _sigmoid(g))).astype(BF16))
        ymix = jnp.concatenate(parts, axis=1)
        y_ref[...] = ymix
        x2 = x_ref[...] + jnp.dot(ymix, w_ref[...], preferred_element_type=F32)
        r = lax.rsqrt(jnp.mean(x2 * x2, axis=-1, keepdims=True) + RMS_EPS)
        yn = x2 * r
        err = yn * gf_ref[...] - t_ref[...]
        loss = 0.5 * jnp.sum(jnp.sum(err * err, axis=-1, keepdims=True) / D_MODEL, axis=0, keepdims=True)
        dyf = err / D_MODEL
        dgf = jnp.sum(dyf * yn, axis=0, keepdims=True)
        dyn = dyf * gf_ref[...]
        dx2 = r * (dyn - yn * jnp.mean(dyn * yn, axis=-1, keepdims=True))
        dx_ref[...] = dx2
        dxb_ref[...] = dx2.astype(BF16)
        row = lax.broadcasted_iota(jnp.int32, (8, D_MODEL), 0)
        upd = jnp.where(row == 0, dgf, jnp.where(row == 1, loss, 0.0))

        @pl.when(pl.program_id(0) == 0)
        def _():
            sm_ref[...] = upd

        @pl.when(pl.program_id(0) != 0)
        def _():
            sm_ref[...] += upd

    def rows(w, col=0):
        return pl.BlockSpec((tm, w), lambda i: (i, col))

    return pl.pallas_call(
        body, name="out_fwd", grid=(T // tm,),
        in_specs=[rows(FOX_W, P_FG // FOX_W), rows(DIL_W, P_DG // DIL_W), rows(MEM_W, P_MG // MEM_W),
                  rows(FOX_W), rows(DIL_W), rows(MEM_W),
                  pl.BlockSpec((MIX_W, D_MODEL), lambda i: (0, 0)),
                  rows(D_MODEL), rows(D_MODEL), pl.BlockSpec((1, D_MODEL), lambda i: (0, 0))],
        out_specs=[rows(MIX_W), rows(D_MODEL), rows(D_MODEL), pl.BlockSpec((8, D_MODEL), lambda i: (0, 0))],
        out_shape=[jax.ShapeDtypeStruct((T, MIX_W), BF16), jax.ShapeDtypeStruct((T, D_MODEL), F32),
                   jax.ShapeDtypeStruct((T, D_MODEL), BF16), jax.ShapeDtypeStruct((8, D_MODEL), F32)],
        compiler_params=_params(("arbitrary",)),
    )(proj, proj, proj, o_fox, o_dil, o_mem, w_out, x, target, gf)


def out_bwd(proj, o_fox, o_dil, o_mem, w_out, dx2b, tm):
    T = dx2b.shape[0]

    def body(fg_ref, dg_ref, mg_ref, of_ref, od_ref, om_ref, w_ref, dx_ref,
             dof_ref, dod_ref, dom_ref, dfg_ref, ddg_ref, dmg_ref):
        dmix = lax.dot_general(dx_ref[...], w_ref[...], (((1,), (1,)), ((), ())), preferred_element_type=F32)
        col = 0
        for g_ref, o_ref, do_ref, dgate_ref in ((fg_ref, of_ref, dof_ref, dfg_ref), (dg_ref, od_ref, dod_ref, ddg_ref),
                                                 (mg_ref, om_ref, dom_ref, dmg_ref)):
            w = g_ref.shape[1]
            d = dmix[:, col:col + w]
            col += w
            g = g_ref[...]
            sg = _sigmoid(g)
            do_ref[...] = d * (g * sg)
            dgate_ref[...] = (d * o_ref[...] * (sg * (1.0 + g * (1.0 - sg)))).astype(BF16)

    def rows(w, col=0):
        return pl.BlockSpec((tm, w), lambda i: (i, col))

    return pl.pallas_call(
        body, name="out_bwd", grid=(T // tm,),
        in_specs=[rows(FOX_W, P_FG // FOX_W), rows(DIL_W, P_DG // DIL_W), rows(MEM_W, P_MG // MEM_W),
                  rows(FOX_W), rows(DIL_W), rows(MEM_W),
                  pl.BlockSpec((MIX_W, D_MODEL), lambda i: (0, 0)), rows(D_MODEL)],
        out_specs=[rows(FOX_W), rows(DIL_W), rows(MEM_W), rows(FOX_W), rows(DIL_W), rows(MEM_W)],
        out_shape=[jax.ShapeDtypeStruct((T, FOX_W), F32), jax.ShapeDtypeStruct((T, DIL_W), F32),
                   jax.ShapeDtypeStruct((T, MEM_W), F32), jax.ShapeDtypeStruct((T, FOX_W), BF16),
                   jax.ShapeDtypeStruct((T, DIL_W), BF16), jax.ShapeDtypeStruct((T, MEM_W), BF16)],
        compiler_params=_params(("arbitrary",)),
    )(proj, proj, proj, o_fox, o_dil, o_mem, w_out, dx2b)


def adamw(w, g, m, v, tr, name):
    lead = w.shape[:-2]
    R, C = w.shape[-2:]
    zeros = (0,) * len(lead)

    def body(w_ref, g_ref, m_ref, v_ref, d_ref, mo_ref, vo_ref):
        gv = g_ref[...]
        mn = ADAM_B1 * m_ref[...] + (1.0 - ADAM_B1) * gv
        vn = ADAM_B2 * v_ref[...] + (1.0 - ADAM_B2) * jnp.square(gv)
        m_hat = mn / (1.0 - ADAM_B1 ** ADAM_STEP)
        v_hat = vn / (1.0 - ADAM_B2 ** ADAM_STEP)
        d_ref[...] = -ADAM_LR * (m_hat / (jnp.sqrt(v_hat) + ADAM_EPS) + ADAM_WD * w_ref[...])
        mo_ref[...] = mn
        vo_ref[...] = vn

    spec = pl.BlockSpec((1,) * len(lead) + (tr, C), lambda i: zeros + (i, 0))
    return pl.pallas_call(
        body, name=name, grid=(R // tr,),
        in_specs=[spec] * 4, out_specs=[spec] * 3,
        out_shape=[jax.ShapeDtypeStruct(w.shape, F32)] * 3,
        compiler_params=_params(("arbitrary",)),
    )(w, g, m, v)


def _pad_row(v, width):
    return jnp.concatenate([v, jnp.zeros((1, width - v.shape[1]), v.dtype)], axis=1)


def local_grads(x, mem, norm_g, b_forget, mem_norm_g, final_norm_g, loss_target, w_in_p, w_kv, w_out):
    B, S, D = x.shape
    T = B * S
    xt = x.reshape(T, D)
    memt = mem.reshape(B * MEM_LEN, D)
    b_pad = _pad_row(b_forget, LANES)

    h = rms_fwd(xt, norm_g, 512, "rms_x")
    proj = mm_nn(h, w_in_p, 512, PW // 3, "in_proj")
    proj3 = proj.reshape(B, S, PW)
    mh = rms_fwd(memt, mem_norm_g, B * MEM_LEN, "rms_mem")
    mkv = mm_nn(mh, w_kv, B * MEM_LEN, 2 * MEM_W, "mem_kv_proj")
    mkv3 = mkv.reshape(B, MEM_LEN, 2 * MEM_W)

    negc = fox_gate(proj3, b_pad)
    causal = _log_masks_t(S, "causal")
    causal = jnp.concatenate([causal, jnp.zeros_like(causal)], axis=0)
    dilated = _log_masks_t(S, "dilated")
    rope = _rope_tables(S)

    o_fox, lse_fox = attn_fwd4("fox", proj3, S, negc_cols=negc, mask=causal)
    o_dil, lse_dil = attn_fwd4("dil", proj3, S, mask=dilated, rope=rope)
    o_mem, lse_mem = attn_fwd4("mem", proj3, S, kv=mkv3)

    ymix, dx2, dx2b, small_out = out_fwd(
        proj, o_fox.reshape(T, FOX_W), o_dil.reshape(T, DIL_W), o_mem.reshape(T, MEM_W), w_out,
        xt, loss_target.reshape(T, D), final_norm_g.reshape(1, D), 256)
    do_fox, do_dil, do_mem, dfg, ddg, dmg = out_bwd(
        proj, o_fox.reshape(T, FOX_W), o_dil.reshape(T, DIL_W), o_mem.reshape(T, MEM_W), w_out, dx2b, 256)
    g_out = mm_tn(ymix, dx2b, 512, D, "w_out_grad")

    dqkv_fox, dneg, drow = attn_bwd3("fox", proj3, do_fox.reshape(B, S, FOX_W), o_fox, lse_fox, S,
                                     negc_cols=negc, mask=causal)
    (dqkv_dil,) = attn_bwd3("dil", proj3, do_dil.reshape(B, S, DIL_W), o_dil, lse_dil, S, mask=dilated, rope=rope)
    dmq, dmk, dmv = attn_bwd3("mem", proj3, do_mem.reshape(B, S, MEM_W), o_mem, lse_mem, S, kv=mkv3)
    drow = drow.reshape(B, FOX_HEADS // 2, S // TQ, 2, TQ).transpose(0, 1, 3, 2, 4).reshape(B, FOX_HEADS, S)
    drow = jnp.pad(drow, ((0, 0), (0, LANES - FOX_HEADS), (0, 0)))
    dflog, db_part = fox_gate_bwd(drow, dneg, proj3, b_pad)

    dproj = jnp.concatenate([dqkv_fox.reshape(T, 3 * FOX_W), dfg, dqkv_dil.reshape(T, 3 * DIL_W), ddg,
                             dmq.reshape(T, MEM_W), dmg, dflog.reshape(T, LANES)], axis=1)
    g_in = mm_tn(h, dproj, 512, PW // 3, "w_in_grad")
    dh = mm_nt(dproj, w_in_p, 1024, PW // 3, "in_proj_bwd")
    grad_x, dng = rms_bwd(xt, norm_g, dh, dx2, 512, "rms_x_bwd")

    dmkv = jnp.concatenate([dmk, dmv], axis=2).reshape(B * MEM_LEN, 2 * MEM_W)
    g_kv = mm_tn(mh, dmkv, B * MEM_LEN, 2 * MEM_W, "w_kv_grad")
    dmh = mm_nt(dmkv, w_kv, B * MEM_LEN, D, "mem_kv_bwd")
    _, dmng = rms_bwd(memt, mem_norm_g, dmh, None, B * MEM_LEN, "rms_mem_bwd")

    small = jnp.concatenate([dng[0:1], dmng[0:1], small_out[0:1], _pad_row(db_part[0:1], D), small_out[1:2],
                             jnp.zeros((3, D), F32)], axis=0)
    return grad_x.reshape(B, S, D), g_in, g_kv, g_out, small


def kernel(x, mem, norm_g, w_in, b_forget, mem_norm_g, w_mem_kv, w_out, final_norm_g, loss_target, m_norm_g, m_w_in, m_b_forget, m_mem_norm_g, m_w_mem_kv, m_w_out, m_final_norm_g, v_norm_g, v_w_in, v_b_forget, v_mem_norm_g, v_w_mem_kv, v_w_out, v_final_norm_g):
    D = D_MODEL
    w_in_full, w_kv_full, w_out_full = weight_gather(
        [_pack_cols(w_in).astype(BF16).reshape(w_in.shape[1], PW), w_mem_kv[0].astype(BF16), w_out[0].astype(BF16)])
    grad_x, g_in, g_kv, g_out, small = local_grads(
        x, mem, norm_g, b_forget, mem_norm_g, final_norm_g, loss_target, w_in_full, w_kv_full, w_out_full)

    big = [g_in, g_kv, g_out]
    *from_sibling, csum = grad_exchange_d2d(big, small)
    tiles = (32, 128, 128)
    names = ("w_in", "w_kv", "w_out")
    chip_parts = [chip_sum(g, got, tr, "chip_sum_" + n) for g, got, tr, n in zip(big, from_sibling, tiles, names)]
    *from_chips, tot = grad_exchange_ici([cp for cp, _ in chip_parts], csum)
    gw_in, gw_kv, gw_out = [final_sum(own, got, tr, "final_sum_" + n)
                            for (_, own), got, tr, n in zip(chip_parts, from_chips, tiles, names)]
    gw_in = _unpack_cols(gw_in[None])

    loss = tot[4, 0]
    g_norm, g_mem_norm, g_final, g_b = tot[0:1], tot[1:2], tot[2], tot[3:4, :FOX_HEADS]

    def rows8(*rows):
        rows = [r.reshape(1, -1) for r in rows]
        rows = [_pad_row(r, D) for r in rows]
        return jnp.concatenate(rows + [jnp.zeros((8 - len(rows), D), F32)], axis=0)

    sw = rows8(norm_g, mem_norm_g, final_norm_g, b_forget)
    sm = rows8(m_norm_g, m_mem_norm_g, m_final_norm_g, m_b_forget)
    sv = rows8(v_norm_g, v_mem_norm_g, v_final_norm_g, v_b_forget)
    d_s, m_s, v_s = adamw(sw, tot, sm, sv, 8, "adamw_small")
    d_in, m_in, v_in = adamw(w_in, gw_in, m_w_in, v_w_in, 32, "adamw_w_in")
    d_kv, m_kv, v_kv = adamw(w_mem_kv[0], gw_kv, m_w_mem_kv[0], v_w_mem_kv[0], 128, "adamw_w_kv")
    d_out, m_out, v_out = adamw(w_out[0], gw_out, m_w_out[0], v_w_out[0], 256, "adamw_w_out")

    def small_outs(t):
        return t[0:1], t[3:4, :FOX_HEADS], t[1:2], t[2]

    grads = (g_norm, gw_in, g_b, g_mem_norm, gw_kv[None], gw_out[None], g_final)
    outs = []
    for t, big in ((d_s, (d_in, d_kv, d_out)), (m_s, (m_in, m_kv, m_out)), (v_s, (v_in, v_kv, v_out))):
        n, b, mn, f = small_outs(t)
        outs += [n, big[0], b, mn, big[1][None], big[2][None], f]
    return (loss, grad_x, *grads, *outs)
```

```python
import functools
import math

import numpy as np
import jax
import jax.numpy as jnp
from jax import lax
from jax.experimental import pallas as pl
from jax.experimental.pallas import tpu as pltpu

F32 = jnp.float32
BF16 = jnp.bfloat16

D_MODEL = 1024
HEAD_DIM = 64
FOX_HEADS = 12
DIL_HEADS = 12
MEM_HEADS = 4
MEM_HEAD_DIM = 128
MEM_LEN = 256
FOX_W = FOX_HEADS * HEAD_DIM
DIL_W = DIL_HEADS * HEAD_DIM
MEM_W = MEM_HEADS * MEM_HEAD_DIM
MIX_W = FOX_W + DIL_W + MEM_W
DILATIONS = ((128, 1), (512, 4), (2048, 16))
ROPE_THETA = 500000.0
ROPE_DIM = HEAD_DIM // 4
RMS_EPS = 1e-6
NEG_INF = -1e30
IN_W = 4 * FOX_W + FOX_HEADS + 4 * DIL_W + 2 * MEM_W

ADAM_LR = 0.001
ADAM_B1 = 0.9
ADAM_B2 = 0.999
ADAM_EPS = 1e-08
ADAM_WD = 0.01
ADAM_STEP = 10

N_DEV = 8
LANES = 128
PAIR_W = 3 * LANES
TQ = 256
TK = 256

O_FQ, O_FK, O_FV, O_FG = 0, FOX_W, 2 * FOX_W, 3 * FOX_W
O_FLOG = 4 * FOX_W
O_DQ = O_FLOG + FOX_HEADS
O_DK, O_DV, O_DG = O_DQ + DIL_W, O_DQ + 2 * DIL_W, O_DQ + 3 * DIL_W
O_MQ = O_DQ + 4 * DIL_W
O_MG = O_MQ + MEM_W
P_FOX = 0
P_FG = P_FOX + 3 * FOX_W
P_DIL = P_FG + FOX_W
P_DG = P_DIL + 3 * DIL_W
P_MQ = P_DG + DIL_W
P_MG = P_MQ + MEM_W
P_FLOG = P_MG + MEM_W
PW = P_FLOG + LANES

VMEM_LIMIT = 56 * 1024 * 1024


def _pack_pieces():
    pieces = []
    for base in (O_FQ, O_DQ):
        seg = []
        for hp in range(FOX_HEADS // 2):
            for part in range(3):
                seg.append((base + part * FOX_W + hp * LANES, LANES))
        pieces.append(seg)
    fox, dil = pieces
    return fox + [(O_FG, FOX_W)] + dil + [(O_DG, DIL_W), (O_MQ, MEM_W), (O_MG, MEM_W), (O_FLOG, FOX_HEADS)]


def _pack_cols(w):
    parts = [w[..., s:s + n] for s, n in _pack_pieces()]
    parts.append(jnp.zeros(w.shape[:-1] + (LANES - FOX_HEADS,), w.dtype))
    return jnp.concatenate(parts, axis=-1)


def _unpack_cols(g):
    runs = []
    pos = 0
    for s, n in _pack_pieces():
        runs.append((s, n, pos))
        pos += n
    runs.sort()
    return jnp.concatenate([g[..., p:p + n] for s, n, p in runs], axis=-1)


def _params(sem=None, **kw):
    return pltpu.CompilerParams(dimension_semantics=sem, vmem_limit_bytes=VMEM_LIMIT, **kw)


def _mesh_pos():
    return lax.axis_index("x"), lax.axis_index("y"), lax.axis_index("c")


def _flip(v, d):
    return 1 - v if d else v


_RELATIONS = [(dx, dy, dc) for dx in (0, 1) for dy in (0, 1) for dc in (0, 1)][1:]


def weight_gather(shards):
    n_arr = len(shards)
    rows = [s.shape[0] for s in shards]

    def body(*refs):
        in_refs = refs[:n_arr]
        out_refs = refs[n_arr:2 * n_arr]
        send_sems, recv_sems, local_sems = refs[2 * n_arr:]
        x, y, c = _mesh_pos()
        me, sibling = (x, y, c), (x, y, 1 - c)
        chips = [(1 - x, y), (x, 1 - y), (1 - x, 1 - y)]

        def block(a, pos):
            px, py, pc = pos
            return out_refs[a].at[pl.ds((4 * px + 2 * py + pc) * rows[a], rows[a]), :]

        def copy(a, k, blk, to, src=None):
            return pltpu.make_async_remote_copy(
                src_ref=block(a, blk) if src is None else src, dst_ref=block(a, blk),
                send_sem=send_sems.at[a, k], recv_sem=recv_sems.at[a, k],
                device_id=to, device_id_type=pl.DeviceIdType.MESH)

        started = []
        mine = []
        for a in range(n_arr):
            cp = pltpu.make_async_copy(in_refs[a], block(a, me), local_sems.at[a])
            cp.start()
            mine.append(cp)
            first = [copy(a, 0, me, sibling, src=in_refs[a])]
            first += [copy(a, 1 + j, me, (*chip, c), src=in_refs[a]) for j, chip in enumerate(chips)]
            for cp in first:
                cp.start()
            started += first
        for a in range(n_arr):
            for j, chip in enumerate(chips):
                copy(a, 1 + j, (*chip, c), me).wait_recv()
                passed = copy(a, 4 + j, (*chip, c), sibling)
                passed.start()
                started.append(passed)
        for a in range(n_arr):
            copy(a, 0, sibling, me).wait_recv()
            for j, chip in enumerate(chips):
                copy(a, 4 + j, (*chip, 1 - c), me).wait_recv()
        for cp in started:
            cp.wait_send()
        for cp in mine:
            cp.wait()

    any_spec = pl.BlockSpec(memory_space=pl.ANY)
    return pl.pallas_call(
        body, name="weight_gather",
        out_shape=[jax.ShapeDtypeStruct((N_DEV * s.shape[0], s.shape[1]), s.dtype) for s in shards],
        in_specs=[any_spec] * n_arr, out_specs=[any_spec] * n_arr,
        scratch_shapes=[pltpu.SemaphoreType.DMA((n_arr, 7)), pltpu.SemaphoreType.DMA((n_arr, 7)),
                        pltpu.SemaphoreType.DMA((n_arr,))],
    )(*shards)


def grad_exchange(grads, small):
    arrs = list(grads) + [small]
    n_arr = len(arrs)
    rows = [g.shape[0] // N_DEV for g in grads] + [small.shape[0]]

    def body(*refs):
        in_refs = refs[:n_arr]
        out_refs = refs[n_arr:2 * n_arr]
        send_sems, recv_sems, local_sems = refs[2 * n_arr:]
        x, y, c = _mesh_pos()
        me = 4 * x + 2 * y + c

        def src(a, idx):
            if a == n_arr - 1:
                return in_refs[a]
            return in_refs[a].at[pl.ds(idx * rows[a], rows[a]), :]

        def copy(a, k):
            dx, dy, dc = _RELATIONS[k]
            px, py, pc = _flip(x, dx), _flip(y, dy), _flip(c, dc)
            peer = 4 * px + 2 * py + pc
            send = pltpu.make_async_remote_copy(
                src_ref=src(a, peer), dst_ref=out_refs[a].at[me],
                send_sem=send_sems.at[a, k], recv_sem=recv_sems.at[a, k],
                device_id=(px, py, pc), device_id_type=pl.DeviceIdType.MESH)
            recv = pltpu.make_async_remote_copy(
                src_ref=src(a, peer), dst_ref=out_refs[a].at[peer],
                send_sem=send_sems.at[a, k], recv_sem=recv_sems.at[a, k],
                device_id=(px, py, pc), device_id_type=pl.DeviceIdType.MESH)
            return send, recv

        mine = []
        pairs = []
        for a in range(n_arr):
            cp = pltpu.make_async_copy(src(a, me), out_refs[a].at[me], local_sems.at[a])
            cp.start()
            mine.append(cp)
            for k in range(7):
                send, recv = copy(a, k)
                send.start()
                pairs.append((send, recv))
        for send, recv in pairs:
            recv.wait_recv()
        for send, recv in pairs:
            send.wait_send()
        for cp in mine:
            cp.wait()

    any_spec = pl.BlockSpec(memory_space=pl.ANY)
    return pl.pallas_call(
        body, name="grad_exchange",
        out_shape=[jax.ShapeDtypeStruct((N_DEV, r, a.shape[1]), a.dtype) for r, a in zip(rows, arrs)],
        in_specs=[any_spec] * n_arr, out_specs=[any_spec] * n_arr,
        scratch_shapes=[pltpu.SemaphoreType.DMA((n_arr, 7)), pltpu.SemaphoreType.DMA((n_arr, 7)),
                        pltpu.SemaphoreType.DMA((n_arr,))],
    )(*arrs)


def slot_sum(slots, tr, name):
    _, R, C = slots.shape

    def body(s_ref, o_ref):
        acc = s_ref[0]
        for d in range(1, N_DEV):
            acc = acc + s_ref[d]
        o_ref[...] = acc

    return pl.pallas_call(
        body, name=name, grid=(R // tr,),
        in_specs=[pl.BlockSpec((N_DEV, tr, C), lambda i: (0, i, 0))],
        out_specs=pl.BlockSpec((tr, C), lambda i: (i, 0)),
        out_shape=jax.ShapeDtypeStruct((R, C), slots.dtype),
        compiler_params=_params(("arbitrary",)),
    )(slots)


N_CHIP = 4
_OTHER_CHIPS = [(1, 0), (0, 1), (1, 1)]


def grad_exchange_d2d(grads, small):
    n_big = len(grads)
    rows = [g.shape[0] // N_DEV for g in grads]

    def body(*refs):
        g_refs = refs[:n_big]
        small_ref = refs[n_big]
        out_refs = refs[n_big + 1:2 * n_big + 1]
        csum_ref = refs[2 * n_big + 1]
        land, send_sems, recv_sems = refs[2 * n_big + 2:]
        x, y, c = _mesh_pos()
        sibling = (x, y, 1 - c)
        copies = []
        for a in range(n_big):
            for q in range(N_CHIP):
                copies.append(pltpu.make_async_remote_copy(
                    src_ref=g_refs[a].at[pl.ds((2 * q + 1 - c) * rows[a], rows[a]), :], dst_ref=out_refs[a].at[q],
                    send_sem=send_sems.at[a, q], recv_sem=recv_sems.at[a, q],
                    device_id=sibling, device_id_type=pl.DeviceIdType.MESH))
        copies.append(pltpu.make_async_remote_copy(
            src_ref=small_ref, dst_ref=land, send_sem=send_sems.at[n_big, 0], recv_sem=recv_sems.at[n_big, 0],
            device_id=sibling, device_id_type=pl.DeviceIdType.MESH))
        for cp in copies:
            cp.start()
        for cp in copies:
            cp.wait_recv()
        for cp in copies:
            cp.wait_send()
        csum_ref[...] = small_ref[...] + land[...]

    any_spec = pl.BlockSpec(memory_space=pl.ANY)
    vmem_spec = pl.BlockSpec(memory_space=pltpu.VMEM)
    return pl.pallas_call(
        body, name="grad_exchange_d2d",
        out_shape=[jax.ShapeDtypeStruct((N_CHIP, r, g.shape[1]), g.dtype) for r, g in zip(rows, grads)]
        + [jax.ShapeDtypeStruct(small.shape, small.dtype)],
        in_specs=[any_spec] * n_big + [vmem_spec], out_specs=[any_spec] * n_big + [vmem_spec],
        scratch_shapes=[pltpu.VMEM(small.shape, small.dtype),
                        pltpu.SemaphoreType.DMA((n_big + 1, N_CHIP)), pltpu.SemaphoreType.DMA((n_big + 1, N_CHIP))],
    )(*grads, small)


def chip_sum(g, got, tr, name):
    _, rows, cols = got.shape
    g4 = g.reshape(N_CHIP, 2, rows, cols)
    x, y, c = _mesh_pos()
    where = jnp.stack([c, 2 * x + y]).astype(jnp.int32)

    def body_all(w_ref, g_ref, r_ref, o_ref):
        o_ref[0] = (g_ref[0, 0] + r_ref[0]).astype(BF16)

    def body_own(w_ref, g_ref, r_ref, o_ref):
        o_ref[...] = g_ref[0, 0] + r_ref[0]

    cpb = pl.pallas_call(
        body_all, name=name + "_all",
        grid_spec=pltpu.PrefetchScalarGridSpec(
            num_scalar_prefetch=1, grid=(N_CHIP, rows // tr),
            in_specs=[pl.BlockSpec((1, 1, tr, cols), lambda q, i, w: (q, w[0], i, 0)),
                      pl.BlockSpec((1, tr, cols), lambda q, i, w: (q, i, 0))],
            out_specs=pl.BlockSpec((1, tr, cols), lambda q, i, w: (q, i, 0))),
        out_shape=jax.ShapeDtypeStruct((N_CHIP, rows, cols), BF16),
        compiler_params=_params(("arbitrary", "arbitrary")),
    )(where, g4, got)
    own = pl.pallas_call(
        body_own, name=name + "_own",
        grid_spec=pltpu.PrefetchScalarGridSpec(
            num_scalar_prefetch=1, grid=(rows // tr,),
            in_specs=[pl.BlockSpec((1, 1, tr, cols), lambda i, w: (w[1], w[0], i, 0)),
                      pl.BlockSpec((1, tr, cols), lambda i, w: (w[1], i, 0))],
            out_specs=pl.BlockSpec((tr, cols), lambda i, w: (i, 0))),
        out_shape=jax.ShapeDtypeStruct((rows, cols), F32),
        compiler_params=_params(("arbitrary",)),
    )(where, g4, got)
    return cpb, own


def grad_exchange_ici(parts, csum):
    n_big = len(parts)

    def body(*refs):
        p_refs = refs[:n_big]
        csum_ref = refs[n_big]
        out_refs = refs[n_big + 1:2 * n_big + 1]
        tot_ref = refs[2 * n_big + 1]
        land, send_sems, recv_sems = refs[2 * n_big + 2:]
        x, y, c = _mesh_pos()
        q_me = 2 * x + y
        land[q_me] = csum_ref[...]
        sends, recvs = [], []
        for j, (dx, dy) in enumerate(_OTHER_CHIPS):
            px, py = _flip(x, dx), _flip(y, dy)
            q_peer = 2 * px + py
            for a in range(n_big + 1):
                src = p_refs[a].at[q_peer] if a < n_big else csum_ref
                dst = out_refs[a] if a < n_big else land
                common = dict(send_sem=send_sems.at[a, j], recv_sem=recv_sems.at[a, j],
                              device_id=(px, py, c), device_id_type=pl.DeviceIdType.MESH)
                sends.append(pltpu.make_async_remote_copy(src_ref=src, dst_ref=dst.at[q_me], **common))
                recvs.append(pltpu.make_async_remote_copy(src_ref=src, dst_ref=dst.at[q_peer], **common))
        for cp in sends:
            cp.start()
        for cp in recvs:
            cp.wait_recv()
        for cp in sends:
            cp.wait_send()
        tot = land[0]
        for q in range(1, N_CHIP):
            tot = tot + land[q]
        tot_ref[...] = tot

    any_spec = pl.BlockSpec(memory_space=pl.ANY)
    vmem_spec = pl.BlockSpec(memory_space=pltpu.VMEM)
    return pl.pallas_call(
        body, name="grad_exchange_ici",
        out_shape=[jax.ShapeDtypeStruct(p.shape, p.dtype) for p in parts] + [jax.ShapeDtypeStruct(csum.shape, csum.dtype)],
        in_specs=[any_spec] * n_big + [vmem_spec], out_specs=[any_spec] * n_big + [vmem_spec],
        scratch_shapes=[pltpu.VMEM((N_CHIP,) + csum.shape, csum.dtype),
                        pltpu.SemaphoreType.DMA((n_big + 1, 3)), pltpu.SemaphoreType.DMA((n_big + 1, 3))],
    )(*parts, csum)


def final_sum(own, got, tr, name, transposed_tile=None):
    rows, cols = own.shape

    def body(own_ref, got_ref, o_ref):
        q_me = 2 * lax.axis_index("x") + lax.axis_index("y")
        acc = None
        for q in range(N_CHIP):
            term = jnp.where(q == q_me, own_ref[...], got_ref[q].astype(F32))
            acc = term if acc is None else acc + term
        o_ref[...] = acc if transposed_tile is None else acc.T

    if transposed_tile is None:
        grid = (rows // tr,)
        in_specs = [pl.BlockSpec((tr, cols), lambda i: (i, 0)), pl.BlockSpec((N_CHIP, tr, cols), lambda i: (0, i, 0))]
        out_spec, out_shape = pl.BlockSpec((tr, cols), lambda i: (i, 0)), (rows, cols)
    else:
        tc = transposed_tile
        grid = (cols // tc,)
        in_specs = [pl.BlockSpec((rows, tc), lambda i: (0, i)), pl.BlockSpec((N_CHIP, rows, tc), lambda i: (0, 0, i))]
        out_spec, out_shape = pl.BlockSpec((tc, rows), lambda i: (i, 0)), (cols, rows)
    return pl.pallas_call(
        body, name=name, grid=grid, in_specs=in_specs, out_specs=out_spec,
        out_shape=jax.ShapeDtypeStruct(out_shape, F32),
        compiler_params=_params(("arbitrary",)),
    )(own, got)


def rms_fwd(x, g, tm, name):
    M, K = x.shape

    def body(x_ref, g_ref, o_ref):
        xv = x_ref[...]
        r = lax.rsqrt(jnp.mean(xv * xv, axis=-1, keepdims=True) + RMS_EPS)
        o_ref[...] = ((xv * r) * g_ref[...]).astype(BF16)

    return pl.pallas_call(
        body, name=name, grid=(M // tm,),
        in_specs=[pl.BlockSpec((tm, K), lambda i: (i, 0)), pl.BlockSpec((1, K), lambda i: (0, 0))],
        out_specs=pl.BlockSpec((tm, K), lambda i: (i, 0)),
        out_shape=jax.ShapeDtypeStruct((M, K), BF16),
        compiler_params=_params(("arbitrary",)),
    )(x, g)


def rms_bwd(x, g, dh, dres, tm, name):
    M, K = x.shape
    has_res = dres is not None

    def body(*refs):
        if has_res:
            x_ref, g_ref, dh_ref, dres_ref, dx_ref, dg_ref = refs
        else:
            x_ref, g_ref, dh_ref, dx_ref, dg_ref = refs
        xv = x_ref[...]
        r = lax.rsqrt(jnp.mean(xv * xv, axis=-1, keepdims=True) + RMS_EPS)
        xn = xv * r
        dhv = dh_ref[...]
        dxn = dhv * g_ref[...]
        dx = r * (dxn - xn * jnp.mean(dxn * xn, axis=-1, keepdims=True))
        if has_res:
            dx = dx + dres_ref[...]
        dx_ref[...] = dx
        part = jnp.sum(dhv * xn, axis=0, keepdims=True)
        row = lax.broadcasted_iota(jnp.int32, (8, K), 0)
        upd = jnp.where(row == 0, part, 0.0)

        @pl.when(pl.program_id(0) == 0)
        def _():
            dg_ref[...] = upd

        @pl.when(pl.program_id(0) != 0)
        def _():
            dg_ref[...] += upd

    row_spec = pl.BlockSpec((tm, K), lambda i: (i, 0))
    ins = [x, g, dh] + ([dres] if has_res else [])
    in_specs = [row_spec, pl.BlockSpec((1, K), lambda i: (0, 0)), row_spec] + ([row_spec] if has_res else [])
    return pl.pallas_call(
        body, name=name, grid=(M // tm,),
        in_specs=in_specs,
        out_specs=[row_spec, pl.BlockSpec((8, K), lambda i: (0, 0))],
        out_shape=[jax.ShapeDtypeStruct((M, K), F32), jax.ShapeDtypeStruct((8, K), F32)],
        compiler_params=_params(("arbitrary",)),
    )(*ins)


def mm_nn(a, b, tm, tn, name):
    M, K = a.shape
    N = b.shape[1]

    def body(a_ref, b_ref, o_ref):
        o_ref[...] = jnp.dot(a_ref[...], b_ref[...], preferred_element_type=F32)

    return pl.pallas_call(
        body, name=name, grid=(N // tn, M // tm),
        in_specs=[pl.BlockSpec((tm, K), lambda j, i: (i, 0)), pl.BlockSpec((K, tn), lambda j, i: (0, j))],
        out_specs=pl.BlockSpec((tm, tn), lambda j, i: (i, j)),
        out_shape=jax.ShapeDtypeStruct((M, N), F32),
        compiler_params=_params(("arbitrary", "arbitrary")),
    )(a, b)


def mm_nt(a, b, tm, tk, name):
    M, K = a.shape
    N = b.shape[0]

    def body(a_ref, b_ref, o_ref):
        part = lax.dot_general(a_ref[...], b_ref[...], (((1,), (1,)), ((), ())), preferred_element_type=F32)

        @pl.when(pl.program_id(1) == 0)
        def _():
            o_ref[...] = part

        @pl.when(pl.program_id(1) != 0)
        def _():
            o_ref[...] += part

    return pl.pallas_call(
        body, name=name, grid=(M // tm, K // tk),
        in_specs=[pl.BlockSpec((tm, tk), lambda i, k: (i, k)), pl.BlockSpec((N, tk), lambda i, k: (0, k))],
        out_specs=pl.BlockSpec((tm, N), lambda i, k: (i, 0)),
        out_shape=jax.ShapeDtypeStruct((M, N), F32),
        compiler_params=_params(("arbitrary", "arbitrary")),
    )(a, b)


def mm_tn(a, b, tt, tn, name):
    T, K = a.shape
    N = b.shape[1]

    def body(a_ref, b_ref, o_ref):
        part = lax.dot_general(a_ref[...], b_ref[...], (((0,), (0,)), ((), ())), preferred_element_type=F32)

        @pl.when(pl.program_id(1) == 0)
        def _():
            o_ref[...] = part

        @pl.when(pl.program_id(1) != 0)
        def _():
            o_ref[...] += part

    return pl.pallas_call(
        body, name=name, grid=(N // tn, T // tt),
        in_specs=[pl.BlockSpec((tt, K), lambda j, t: (t, 0)), pl.BlockSpec((tt, tn), lambda j, t: (t, j))],
        out_specs=pl.BlockSpec((K, tn), lambda j, t: (0, j)),
        out_shape=jax.ShapeDtypeStruct((K, N), F32),
        compiler_params=_params(("arbitrary", "arbitrary")),
    )(a, b)


def mm_tn_multi(a, bs, tt, name):
    T, K = a.shape
    widths = [b.shape[1] for b in bs]

    def body(a_ref, *rest):
        b_refs, o_ref = rest[:-1], rest[-1]
        av = a_ref[...]
        parts = [lax.dot_general(av, b_ref[...], (((0,), (0,)), ((), ())), preferred_element_type=F32)
                 for b_ref in b_refs]

        @pl.when(pl.program_id(0) == 0)
        def _():
            col = 0
            for part, w in zip(parts, widths):
                o_ref[:, col:col + w] = part
                col += w

        @pl.when(pl.program_id(0) != 0)
        def _():
            col = 0
            for part, w in zip(parts, widths):
                o_ref[:, col:col + w] += part
                col += w

    return pl.pallas_call(
        body, name=name, grid=(T // tt,),
        in_specs=[pl.BlockSpec((tt, K), lambda t: (t, 0))] + [pl.BlockSpec((tt, w), lambda t: (t, 0)) for w in widths],
        out_specs=pl.BlockSpec((K, sum(widths)), lambda t: (0, 0)),
        out_shape=jax.ShapeDtypeStruct((K, sum(widths)), F32),
        compiler_params=_params(("arbitrary",)),
    )(a, *bs)


def mm_nt_multi(pieces, w, tm, name):
    M = pieces[0][0].shape[0]
    N, K = w.shape

    def body(*refs):
        p_refs, w_ref, o_ref = refs[:-2], refs[-2], refs[-1]
        acc = None
        for p_ref, (arr, col) in zip(p_refs, pieces):
            part = lax.dot_general(p_ref[...], w_ref[:, col:col + arr.shape[1]], (((1,), (1,)), ((), ())),
                                   preferred_element_type=F32)
            acc = part if acc is None else acc + part
        o_ref[...] = acc

    return pl.pallas_call(
        body, name=name, grid=(M // tm,),
        in_specs=[pl.BlockSpec((tm, arr.shape[1]), lambda i: (i, 0)) for arr, _ in pieces]
        + [pl.BlockSpec((N, K), lambda i: (0, 0))],
        out_specs=pl.BlockSpec((tm, N), lambda i: (i, 0)),
        out_shape=jax.ShapeDtypeStruct((M, N), F32),
        compiler_params=_params(("arbitrary",)),
    )(*[arr for arr, _ in pieces], w)


def _log_sigmoid(z):
    return jnp.minimum(z, 0.0) - jnp.log(1.0 + jnp.exp(-jnp.abs(z)))


def _tri(n, lower):
    r = lax.broadcasted_iota(jnp.int32, (n, n), 0)
    c = lax.broadcasted_iota(jnp.int32, (n, n), 1)
    return jnp.where((r >= c) if lower else (r <= c), 1.0, 0.0).astype(F32)


def fox_gate(proj3, b_pad):
    B, S, _ = proj3.shape
    nblk = S // TK

    def body(f_ref, b_ref, o_ref):
        tri = _tri(TK, True)
        carry = jnp.zeros((1, LANES), F32)
        for n in range(nblk):
            z = f_ref[0, n * TK:(n + 1) * TK, :] + b_ref[...]
            logf = _log_sigmoid(z)
            cs = jnp.dot(tri, logf, preferred_element_type=F32, precision=lax.Precision.HIGHEST) + carry
            carry = cs[TK - 1:TK, :]
            o_ref[0, n * TK:(n + 1) * TK, :] = -cs

    return pl.pallas_call(
        body, name="fox_gate", grid=(B,),
        in_specs=[pl.BlockSpec((1, S, LANES), lambda b: (b, 0, P_FLOG // LANES)),
                  pl.BlockSpec((1, LANES), lambda b: (0, 0))],
        out_specs=pl.BlockSpec((1, S, LANES), lambda b: (b, 0, 0)),
        out_shape=jax.ShapeDtypeStruct((B, S, LANES), F32),
        compiler_params=_params(("arbitrary",)),
    )(proj3, b_pad)


def fox_gate_bwd(drow, dneg, proj3, b_pad):
    B, S, _ = proj3.shape
    nblk = S // TK

    def body(d_ref, r_ref, f_ref, b_ref, o_ref, db_ref):
        tri = _tri(TK, False)
        lane = lax.broadcasted_iota(jnp.int32, (TK, LANES), 1)
        er = lax.broadcasted_iota(jnp.int32, (FOX_W, LANES), 0)
        ec = lax.broadcasted_iota(jnp.int32, (FOX_W, LANES), 1)
        pick = jnp.where(er == LANES * (ec >> 1) + (ec & 1), 1.0, 0.0).astype(F32)
        carry = jnp.zeros((1, LANES), F32)
        dbsum = jnp.zeros((1, LANES), F32)
        for n in reversed(range(nblk)):
            dk_side = jnp.dot(r_ref[0, n * TK:(n + 1) * TK, :], pick, preferred_element_type=F32,
                              precision=lax.Precision.HIGHEST)
            dc = jnp.where(lane < FOX_HEADS, d_ref[0, :, n * TK:(n + 1) * TK].T - dk_side, 0.0)
            rs = jnp.dot(tri, dc, preferred_element_type=F32, precision=lax.Precision.HIGHEST) + carry
            carry = rs[0:1, :]
            z = f_ref[0, n * TK:(n + 1) * TK, :] + b_ref[...]
            dz = rs * (1.0 / (1.0 + jnp.exp(z)))
            o_ref[0, n * TK:(n + 1) * TK, :] = dz.astype(BF16)
            dbsum = dbsum + jnp.sum(dz, axis=0, keepdims=True)
        row = lax.broadcasted_iota(jnp.int32, (8, LANES), 0)
        upd = jnp.where(row == 0, dbsum, 0.0)

        @pl.when(pl.program_id(0) == 0)
        def _():
            db_ref[...] = upd

        @pl.when(pl.program_id(0) != 0)
        def _():
            db_ref[...] += upd

    return pl.pallas_call(
        body, name="fox_gate_bwd", grid=(B,),
        in_specs=[pl.BlockSpec((1, LANES, S), lambda b: (b, 0, 0)),
                  pl.BlockSpec((1, S, FOX_W), lambda b: (b, 0, 0)),
                  pl.BlockSpec((1, S, LANES), lambda b: (b, 0, P_FLOG // LANES)),
                  pl.BlockSpec((1, LANES), lambda b: (0, 0))],
        out_specs=[pl.BlockSpec((1, S, LANES), lambda b: (b, 0, 0)), pl.BlockSpec((8, LANES), lambda b: (0, 0))],
        out_shape=[jax.ShapeDtypeStruct((B, S, LANES), BF16), jax.ShapeDtypeStruct((8, LANES), F32)],
        compiler_params=_params(("arbitrary",)),
    )(drow, dneg, proj3, b_pad)


def _mult_masks(S, kind):
    nd = S // TQ
    a = np.arange(TQ)[:, None]
    b = np.arange(TK)[None, :]
    out = np.zeros((nd, TQ, TK), np.float32)
    for d in range(nd):
        delta = d * TQ + a - b
        if kind == "causal":
            out[d] = delta >= 0
        else:
            m = np.zeros((TQ, TK), np.float32)
            for w, dil in DILATIONS:
                m += (delta >= 0) & (delta % dil == 0) & (delta <= w)
            out[d] = m
    return jnp.asarray(out)


def _rope_tables(S):
    half = ROPE_DIM // 2
    f32 = np.float32
    pos = np.arange(S, dtype=f32)
    inv_freq = f32(1.0) / np.power(f32(ROPE_THETA), np.arange(0, ROPE_DIM, 2, dtype=f32) / f32(ROPE_DIM)).astype(f32)
    ang = (pos[:, None] * inv_freq[None, :]).astype(f32).astype(np.float64)
    cos, sin = np.cos(ang).astype(f32), np.sin(ang).astype(f32)
    one = np.ones((S, HEAD_DIM - ROPE_DIM), f32)
    zero = np.zeros((S, HEAD_DIM - ROPE_DIM), f32)
    zh = np.zeros((S, half), f32)
    c = np.concatenate([cos, cos, one], axis=1)
    s1 = np.concatenate([-sin, zh, zero], axis=1)
    s2 = np.concatenate([zh, sin, zero], axis=1)
    return tuple(jnp.asarray(np.concatenate([t, t], axis=1)) for t in (c, s1, s2))


def _rope(t, c, s1, s2):
    return t * c + pltpu.roll(t, LANES - half_rope(), 1) * s1 + pltpu.roll(t, half_rope(), 1) * s2


def half_rope():
    return ROPE_DIM // 2


def _rope_bwd(d, c, s1, s2):
    return d * c + pltpu.roll(d * s1, half_rope(), 1) + pltpu.roll(d * s2, LANES - half_rope(), 1)


def _scale_parts(scale):
    m, _ = math.frexp(scale)
    return (scale, None) if m == 0.5 else (None, scale)


def attn_fwd(kind, src, S, *, negc=None, mask=None, rope=None, kv=None):
    B = src.shape[0]
    pair = kind != "mem"
    col0 = {"fox": P_FOX, "dil": P_DIL, "mem": P_MQ}[kind]
    n_blocks = FOX_HEADS // 2 if pair else MEM_HEADS
    e_dim = HEAD_DIM if pair else MEM_HEAD_DIM
    q_fold, s_scale = _scale_parts(1.0 / math.sqrt(e_dim))
    Sk = S if pair else MEM_LEN
    nh = 2 if pair else 1
    has_bias = negc is not None
    has_rope = rope is not None
    nq = S // TQ

    def body(*refs):
        refs = list(refs)
        if pair:
            qkv_ref = refs.pop(0)
        else:
            q_ref, k_ref, v_ref = refs.pop(0), refs.pop(0), refs.pop(0)
        negc_ref = refs.pop(0) if has_bias else None
        mask_ref = refs.pop(0) if pair else None
        rope_refs = [refs.pop(0) for _ in range(3)] if has_rope else None
        o_ref, lse_ref, qs, ks, vs = refs
        lane = lax.broadcasted_iota(jnp.int32, (1, LANES), 1)

        def prep_q(n, _):
            r0 = pl.multiple_of(n * TQ, TQ)
            rows = pl.ds(r0, TQ)
            q = qkv_ref[0, rows, 0:LANES] if pair else q_ref[0, rows, :]
            if has_rope:
                q = _rope(q, *[t[rows, :] for t in rope_refs])
            if q_fold is not None:
                q = q * q_fold
            qs[rows, :] = q.astype(BF16)
            return 0

        def prep_kv(n, _):
            r0 = pl.multiple_of(n * TK, TK)
            rows = pl.ds(r0, TK)
            k = qkv_ref[0, rows, LANES:2 * LANES] if pair else k_ref[0, rows, :]
            v = qkv_ref[0, rows, 2 * LANES:3 * LANES] if pair else v_ref[0, rows, :]
            if has_rope:
                k = _rope(k, *[t[rows, :] for t in rope_refs])
            ks[rows, :] = k.astype(BF16)
            vs[rows, :] = v.astype(BF16)
            return 0

        lax.fori_loop(0, nq, prep_q, 0)
        lax.fori_loop(0, Sk // TK, prep_kv, 0)

        def q_loop(i, _):
            r0 = pl.multiple_of(i * TQ, TQ)
            q = qs[pl.ds(r0, TQ), :]
            res = []
            for hh in range(nh):
                hmask = (lane >= HEAD_DIM * hh) & (lane < HEAD_DIM * (hh + 1))
                qh = jnp.where(hmask, q, jnp.zeros_like(q)) if pair else q

                def kv_loop(j, carry, qh=qh, hh=hh):
                    m, l, acc = carry
                    c0 = pl.multiple_of(j * TK, TK)
                    k = ks[pl.ds(c0, TK), :]
                    v = vs[pl.ds(c0, TK), :]
                    s = lax.dot_general(qh, k, (((1,), (1,)), ((), ())), preferred_element_type=F32)
                    if s_scale is not None:
                        s = s * s_scale
                    if has_bias:
                        s = s + negc_ref[0, 0, pl.ds(hh, 1), pl.ds(c0, TK)]
                    if pair:
                        mult = mask_ref[i - j]
                        s = jnp.where(mult > 0.0, s, NEG_INF)
                    m_new = jnp.maximum(m, jnp.max(s, axis=1, keepdims=True))
                    p = jnp.exp(s - m_new)
                    if pair:
                        p = p * mult
                    alpha = jnp.exp(m - m_new)
                    l = alpha * l + jnp.sum(p, axis=1, keepdims=True)
                    acc = acc * alpha + jnp.dot(p.astype(BF16), v, preferred_element_type=F32)
                    return m_new, l, acc

                init = (jnp.full((TQ, 1), NEG_INF, F32), jnp.zeros((TQ, 1), F32), jnp.zeros((TQ, LANES), F32))
                m, l, acc = lax.fori_loop(0, (i + 1) if pair else Sk // TK, kv_loop, init)
                res.append((acc / l, m + jnp.log(l)))
            if pair:
                o = jnp.where(lane < HEAD_DIM, res[0][0], res[1][0])
                lse = jnp.where(lane < HEAD_DIM, res[0][1], res[1][1])
            else:
                o = res[0][0]
                lse = jnp.broadcast_to(res[0][1], (TQ, LANES))
            o_ref[0, pl.ds(r0, TQ), :] = o
            lse_ref[0, pl.ds(r0, TQ), :] = lse
            return 0

        lax.fori_loop(0, nq, q_loop, 0)

    ins, in_specs = [], []
    if pair:
        ins.append(src)
        in_specs.append(pl.BlockSpec((1, S, PAIR_W), lambda b, h: (b, 0, col0 // PAIR_W + h)))
    else:
        ins += [src, kv, kv]
        in_specs += [pl.BlockSpec((1, S, LANES), lambda b, h: (b, 0, col0 // LANES + h)),
                     pl.BlockSpec((1, MEM_LEN, LANES), lambda b, h: (b, 0, h)),
                     pl.BlockSpec((1, MEM_LEN, LANES), lambda b, h: (b, 0, MEM_HEADS + h))]
    if has_bias:
        ins.append(negc)
        in_specs.append(pl.BlockSpec((1, 1, 2, S), lambda b, h: (b, h, 0, 0)))
    if pair:
        ins.append(mask)
        in_specs.append(pl.BlockSpec(mask.shape, lambda b, h: (0, 0, 0)))
    if has_rope:
        ins += list(rope)
        in_specs += [pl.BlockSpec((S, LANES), lambda b, h: (0, 0))] * 3
    W = n_blocks * LANES
    out_spec = pl.BlockSpec((1, S, LANES), lambda b, h: (b, 0, h))
    return pl.pallas_call(
        body, name=kind + "_attn_fwd", grid=(B, n_blocks),
        in_specs=in_specs, out_specs=[out_spec, out_spec],
        out_shape=[jax.ShapeDtypeStruct((B, S, W), F32)] * 2,
        scratch_shapes=[pltpu.VMEM((S, LANES), BF16), pltpu.VMEM((Sk, LANES), BF16), pltpu.VMEM((Sk, LANES), BF16)],
        compiler_params=_params(("arbitrary", "arbitrary")),
    )(*ins)


def attn_bwd(kind, src, do, o, lse, S, *, negc=None, mask=None, rope=None, kv=None):
    B = src.shape[0]
    pair = kind != "mem"
    col0 = {"fox": P_FOX, "dil": P_DIL, "mem": P_MQ}[kind]
    n_blocks = FOX_HEADS // 2 if pair else MEM_HEADS
    e_dim = HEAD_DIM if pair else MEM_HEAD_DIM
    scale = 1.0 / math.sqrt(e_dim)
    q_fold, s_scale = _scale_parts(scale)
    Sk = S if pair else MEM_LEN
    nh = 2 if pair else 1
    has_bias = negc is not None
    has_rope = rope is not None
    nq = S // TQ
    nk = Sk // TK

    def body(*refs):
        refs = list(refs)
        if pair:
            qkv_ref = refs.pop(0)
        else:
            q_ref, k_ref, v_ref = refs.pop(0), refs.pop(0), refs.pop(0)
        do_ref, o_ref, lse_ref = refs.pop(0), refs.pop(0), refs.pop(0)
        negc_ref = refs.pop(0) if has_bias else None
        mask_ref = refs.pop(0) if pair else None
        rope_refs = [refs.pop(0) for _ in range(3)] if has_rope else None
        if pair:
            dqkv_ref = refs.pop(0)
            dnegc_ref = refs.pop(0) if has_bias else None
            drow_ref = refs.pop(0) if has_bias else None
        else:
            dq_ref, dk_ref, dv_ref = refs.pop(0), refs.pop(0), refs.pop(0)
        qs, ks, vs, dos, delta_s, dq_acc = refs[:6]
        drow_acc = refs[6] if has_bias else None
        lane = lax.broadcasted_iota(jnp.int32, (1, LANES), 1)

        def prep_q(n, _):
            r0 = pl.multiple_of(n * TQ, TQ)
            rows = pl.ds(r0, TQ)
            q = qkv_ref[0, rows, 0:LANES] if pair else q_ref[0, rows, :]
            if has_rope:
                q = _rope(q, *[t[rows, :] for t in rope_refs])
            if q_fold is not None:
                q = q * q_fold
            qs[rows, :] = q.astype(BF16)
            dov = do_ref[0, rows, :]
            dob = dov.astype(BF16)
            dos[rows, :] = dob
            prod = dob.astype(F32) * o_ref[0, rows, :]
            if pair:
                d0 = jnp.sum(jnp.where(lane < HEAD_DIM, prod, 0.0), axis=1, keepdims=True)
                d1 = jnp.sum(jnp.where(lane < HEAD_DIM, 0.0, prod), axis=1, keepdims=True)
                delta_s[rows, :] = jnp.where(lane < HEAD_DIM, d0, d1)
            else:
                delta_s[rows, :] = jnp.broadcast_to(jnp.sum(prod, axis=1, keepdims=True), (TQ, LANES))
            dq_acc[rows, :] = jnp.zeros((TQ, LANES), F32)
            if has_bias:
                drow_acc[rows, :] = jnp.zeros((TQ, LANES), F32)
            return 0

        def prep_kv(n, _):
            r0 = pl.multiple_of(n * TK, TK)
            rows = pl.ds(r0, TK)
            k = qkv_ref[0, rows, LANES:2 * LANES] if pair else k_ref[0, rows, :]
            v = qkv_ref[0, rows, 2 * LANES:3 * LANES] if pair else v_ref[0, rows, :]
            if has_rope:
                k = _rope(k, *[t[rows, :] for t in rope_refs])
            ks[rows, :] = k.astype(BF16)
            vs[rows, :] = v.astype(BF16)
            return 0

        lax.fori_loop(0, nq, prep_q, 0)
        lax.fori_loop(0, nk, prep_kv, 0)

        def kv_loop(j, _):
            c0 = pl.multiple_of(j * TK, TK)
            kt = ks[pl.ds(c0, TK), :]
            vt = vs[pl.ds(c0, TK), :]
            res = []
            for hh in range(nh):
                hmask = (lane >= HEAD_DIM * hh) & (lane < HEAD_DIM * (hh + 1))
                kh = jnp.where(hmask, kt, jnp.zeros_like(kt)) if pair else kt
                vh = jnp.where(hmask, vt, jnp.zeros_like(vt)) if pair else vt

                def q_loop(i, carry, kh=kh, vh=vh, hh=hh, hmask=hmask):
                    dk, dv, dneg = carry
                    r0 = pl.multiple_of(i * TQ, TQ)
                    rows = pl.ds(r0, TQ)
                    q = qs[rows, :]
                    dot = dos[rows, :]
                    lse_i = lse_ref[0, rows, hh * HEAD_DIM:hh * HEAD_DIM + 1]
                    delta_i = delta_s[rows, hh * HEAD_DIM:hh * HEAD_DIM + 1]
                    s = lax.dot_general(q, kh, (((1,), (1,)), ((), ())), preferred_element_type=F32)
                    if s_scale is not None:
                        s = s * s_scale
                    if has_bias:
                        s = s + negc_ref[0, 0, pl.ds(hh, 1), pl.ds(c0, TK)]
                    if pair:
                        mult = mask_ref[i - j]
                        s = jnp.where(mult > 0.0, s, NEG_INF)
                    p = jnp.exp(s - lse_i)
                    if pair:
                        p = p * mult
                    dv = dv + lax.dot_general(p.astype(BF16), dot, (((0,), (0,)), ((), ())),
                                              preferred_element_type=F32)
                    dp = lax.dot_general(dot, vh, (((1,), (1,)), ((), ())), preferred_element_type=F32)
                    ds = p * (dp - delta_i)
                    if has_bias:
                        dneg = dneg + jnp.sum(ds, axis=0, keepdims=True)
                        drow_acc[rows, :] += jnp.where(hmask, jnp.sum(ds, axis=1, keepdims=True), 0.0)
                    if s_scale is not None:
                        ds = ds * s_scale
                    dsb = ds.astype(BF16)
                    dk = dk + lax.dot_general(dsb, q, (((0,), (0,)), ((), ())), preferred_element_type=F32)
                    dq = jnp.dot(dsb, kh, preferred_element_type=F32)
                    dq_acc[rows, :] += dq
                    return dk, dv, dneg

                init = (jnp.zeros((TK, LANES), F32), jnp.zeros((TK, LANES), F32), jnp.zeros((1, TK), F32))
                dk, dv, dneg = lax.fori_loop(j if pair else 0, nq, q_loop, init)
                if has_bias:
                    dnegc_ref[0, 0, pl.ds(hh, 1), pl.ds(c0, TK)] = dneg
                res.append((dk, dv))
            if pair:
                dk = jnp.where(lane < HEAD_DIM, res[0][0], res[1][0])
                dv = jnp.where(lane < HEAD_DIM, res[0][1], res[1][1])
                if has_rope:
                    dk = _rope_bwd(dk, *[t[pl.ds(c0, TK), :] for t in rope_refs])
                dqkv_ref[0, pl.ds(c0, TK), LANES:2 * LANES] = dk.astype(BF16)
                dqkv_ref[0, pl.ds(c0, TK), 2 * LANES:3 * LANES] = dv.astype(BF16)
            else:
                dk_ref[0, pl.ds(c0, TK), :] = res[0][0].astype(BF16)
                dv_ref[0, pl.ds(c0, TK), :] = res[0][1].astype(BF16)
            return 0

        lax.fori_loop(0, nk, kv_loop, 0)

        def fin_q(n, _):
            r0 = pl.multiple_of(n * TQ, TQ)
            rows = pl.ds(r0, TQ)
            dq = dq_acc[rows, :]
            if q_fold is not None:
                dq = dq * q_fold
            if has_rope:
                dq = _rope_bwd(dq, *[t[rows, :] for t in rope_refs])
            if pair:
                dqkv_ref[0, rows, 0:LANES] = dq.astype(BF16)
            else:
                dq_ref[0, rows, :] = dq.astype(BF16)
            if has_bias:
                drow_ref[0, rows, :] = drow_acc[rows, :]
            return 0

        lax.fori_loop(0, nq, fin_q, 0)

    ins, in_specs = [], []
    if pair:
        ins.append(src)
        in_specs.append(pl.BlockSpec((1, S, PAIR_W), lambda b, h: (b, 0, col0 // PAIR_W + h)))
    else:
        ins += [src, kv, kv]
        in_specs += [pl.BlockSpec((1, S, LANES), lambda b, h: (b, 0, col0 // LANES + h)),
                     pl.BlockSpec((1, MEM_LEN, LANES), lambda b, h: (b, 0, h)),
                     pl.BlockSpec((1, MEM_LEN, LANES), lambda b, h: (b, 0, MEM_HEADS + h))]
    row_spec = pl.BlockSpec((1, S, LANES), lambda b, h: (b, 0, h))
    ins += [do, o, lse]
    in_specs += [row_spec] * 3
    if has_bias:
        ins.append(negc)
        in_specs.append(pl.BlockSpec((1, 1, 2, S), lambda b, h: (b, h, 0, 0)))
    if pair:
        ins.append(mask)
        in_specs.append(pl.BlockSpec(mask.shape, lambda b, h: (0, 0, 0)))
    if has_rope:
        ins += list(rope)
        in_specs += [pl.BlockSpec((S, LANES), lambda b, h: (0, 0))] * 3
    W = n_blocks * LANES
    if pair:
        out_specs = [pl.BlockSpec((1, S, PAIR_W), lambda b, h: (b, 0, h))]
        out_shape = [jax.ShapeDtypeStruct((B, S, 3 * W), BF16)]
        if has_bias:
            out_specs.append(pl.BlockSpec((1, 1, 2, S), lambda b, h: (b, h, 0, 0)))
            out_shape.append(jax.ShapeDtypeStruct((B, LANES // 2, 2, S), F32))
            out_specs.append(row_spec)
            out_shape.append(jax.ShapeDtypeStruct((B, S, W), F32))
    else:
        kv_spec = pl.BlockSpec((1, MEM_LEN, LANES), lambda b, h: (b, 0, h))
        out_specs = [row_spec, kv_spec, kv_spec]
        out_shape = [jax.ShapeDtypeStruct((B, S, W), BF16)] + [jax.ShapeDtypeStruct((B, MEM_LEN, W), BF16)] * 2
    return pl.pallas_call(
        body, name=kind + "_attn_bwd", grid=(B, n_blocks),
        in_specs=in_specs, out_specs=out_specs, out_shape=out_shape,
        scratch_shapes=[pltpu.VMEM((S, LANES), BF16), pltpu.VMEM((Sk, LANES), BF16), pltpu.VMEM((Sk, LANES), BF16),
                        pltpu.VMEM((S, LANES), BF16), pltpu.VMEM((S, LANES), F32), pltpu.VMEM((S, LANES), F32)]
        + ([pltpu.VMEM((S, LANES), F32)] if has_bias else []),
        compiler_params=_params(("arbitrary", "arbitrary")),
    )(*ins)


def _log_masks(S, kind):
    nd = 1 if kind == "causal" else S // TQ
    a = np.arange(TQ)[:, None]
    b = np.arange(TK)[None, :]
    out = np.zeros((nd, TQ, TK), np.float32)
    for d in range(nd):
        delta = d * TQ + a - b
        if kind == "causal":
            m = (delta >= 0).astype(np.float64)
        else:
            m = sum(((delta >= 0) & (delta % dil == 0) & (delta <= w)).astype(np.float64) for w, dil in DILATIONS)
        out[d] = np.where(m > 0, np.log(np.maximum(m, 1.0)), NEG_INF)
    return jnp.asarray(out)


def _attn_setup(kind):
    pair = kind != "mem"
    e_dim = HEAD_DIM if pair else MEM_HEAD_DIM
    q_fold, s_scale = _scale_parts(1.0 / math.sqrt(e_dim))
    return dict(pair=pair, col0={"fox": P_FOX, "dil": P_DIL, "mem": P_MQ}[kind],
                n_blocks=FOX_HEADS // 2 if pair else MEM_HEADS, q_fold=q_fold, s_scale=s_scale,
                nh=2 if pair else 1)


def _attn_inputs(kind, src, S, negc, mask, rope, kv, extra):
    cfg = _attn_setup(kind)
    col0 = cfg["col0"]
    ins, in_specs = [], []
    if cfg["pair"]:
        ins.append(src)
        in_specs.append(pl.BlockSpec((1, S, PAIR_W), lambda b, h: (b, 0, col0 // PAIR_W + h)))
    else:
        ins += [src, kv, kv]
        in_specs += [pl.BlockSpec((1, S, LANES), lambda b, h: (b, 0, col0 // LANES + h)),
                     pl.BlockSpec((1, MEM_LEN, LANES), lambda b, h: (b, 0, h)),
                     pl.BlockSpec((1, MEM_LEN, LANES), lambda b, h: (b, 0, MEM_HEADS + h))]
    ins += list(extra)
    in_specs += [pl.BlockSpec((1, S, LANES), lambda b, h: (b, 0, h))] * len(extra)
    if negc is not None:
        ins.append(negc)
        in_specs.append(pl.BlockSpec((1, 1, 2, S), lambda b, h: (b, h, 0, 0)))
    if mask is not None:
        ins.append(mask)
        in_specs.append(pl.BlockSpec(mask.shape, lambda b, h: (0, 0, 0)))
    if rope is not None:
        ins += list(rope)
        in_specs += [pl.BlockSpec((S, LANES), lambda b, h: (0, 0))] * 3
    return ins, in_specs


def _prep_rows(cfg, rope_refs, lane, load_q, load_kv, qs2, ks, vs, S, Sk):
    nh = cfg["nh"]
    R = nh * TQ

    def prep_q(n, _):
        rows = pl.ds(pl.multiple_of(n * TQ, TQ), TQ)
        q = load_q(rows)
        if rope_refs is not None:
            q = _rope(q, *[t[rows, :] for t in rope_refs])
        if cfg["q_fold"] is not None:
            q = q * cfg["q_fold"]
        _store_stacked(cfg, lane, qs2, n, q.astype(BF16))
        return 0

    def prep_kv(n, _):
        rows = pl.ds(pl.multiple_of(n * TK, TK), TK)
        k, v = load_kv(rows)
        if rope_refs is not None:
            k = _rope(k, *[t[rows, :] for t in rope_refs])
        ks[rows, :] = k.astype(BF16)
        vs[rows, :] = v.astype(BF16)
        return 0

    lax.fori_loop(0, S // TQ, prep_q, 0)
    lax.fori_loop(0, Sk // TK, prep_kv, 0)


def _store_stacked(cfg, lane, dst, n, val):
    nh = cfg["nh"]
    R = nh * TQ
    if nh == 1:
        dst[pl.ds(pl.multiple_of(n * R, R), TQ), :] = val
        return
    for hh in range(nh):
        hmask = (lane >= HEAD_DIM * hh) & (lane < HEAD_DIM * (hh + 1))
        dst[pl.ds(pl.multiple_of(n * R + hh * TQ, TQ), TQ), :] = jnp.where(hmask, val, jnp.zeros_like(val))


def _cat(parts, axis):
    return parts[0] if len(parts) == 1 else jnp.concatenate(parts, axis=axis)


def attn_fwd2(kind, src, S, *, negc=None, mask=None, rope=None, kv=None):
    B = src.shape[0]
    cfg = _attn_setup(kind)
    pair, nh, s_scale = cfg["pair"], cfg["nh"], cfg["s_scale"]
    Sk = S if pair else MEM_LEN
    has_bias, has_rope = negc is not None, rope is not None
    R = nh * TQ

    def body(*refs):
        refs = list(refs)
        if pair:
            qkv_ref = refs.pop(0)
        else:
            q_ref, k_ref, v_ref = refs.pop(0), refs.pop(0), refs.pop(0)
        negc_ref = refs.pop(0) if has_bias else None
        mask_ref = refs.pop(0) if mask is not None else None
        rope_refs = [refs.pop(0) for _ in range(3)] if has_rope else None
        o_ref, lse_ref, qs2, ks, vs = refs
        lane = lax.broadcasted_iota(jnp.int32, (1, LANES), 1)

        if pair:
            load_q = lambda rows: qkv_ref[0, rows, 0:LANES]
            load_kv = lambda rows: (qkv_ref[0, rows, LANES:2 * LANES], qkv_ref[0, rows, 2 * LANES:3 * LANES])
        else:
            load_q = lambda rows: q_ref[0, rows, :]
            load_kv = lambda rows: (k_ref[0, rows, :], v_ref[0, rows, :])
        _prep_rows(cfg, rope_refs, lane, load_q, load_kv, qs2, ks, vs, S, Sk)

        def q_loop(i, _):
            q2 = qs2[pl.ds(pl.multiple_of(i * R, R), R), :]

            def step(j, carry, midx):
                ms, ls, acc = carry
                c0 = pl.multiple_of(j * TK, TK)
                k = ks[pl.ds(c0, TK), :]
                v = vs[pl.ds(c0, TK), :]
                s2 = lax.dot_general(q2, k, (((1,), (1,)), ((), ())), preferred_element_type=F32)
                if s_scale is not None:
                    s2 = s2 * s_scale
                new_m, new_l, ps, alphas = [], [], [], []
                for hh in range(nh):
                    s = s2[hh * TQ:(hh + 1) * TQ]
                    if has_bias:
                        s = s + negc_ref[0, 0, pl.ds(hh, 1), pl.ds(c0, TK)]
                    if midx is not None:
                        s = s + mask_ref[midx]
                    m_new = jnp.maximum(ms[hh], jnp.max(s, axis=1, keepdims=True))
                    p = jnp.exp(s - m_new)
                    alpha = jnp.exp(ms[hh] - m_new)
                    new_l.append(alpha * ls[hh] + jnp.sum(p, axis=1, keepdims=True))
                    new_m.append(m_new)
                    ps.append(p.astype(BF16))
                    alphas.append(alpha)
                acc = acc * _cat(alphas, 0) + jnp.dot(_cat(ps, 0), v, preferred_element_type=F32)
                return tuple(new_m), tuple(new_l), acc

            init = (tuple(jnp.full((TQ, 1), NEG_INF, F32) for _ in range(nh)),
                    tuple(jnp.zeros((TQ, 1), F32) for _ in range(nh)), jnp.zeros((R, LANES), F32))
            if kind == "fox":
                carry = lax.fori_loop(0, i, lambda j, c: step(j, c, None), init)
                carry = step(i, carry, 0)
            elif kind == "dil":
                carry = lax.fori_loop(0, i + 1, lambda j, c: step(j, c, i - j), init)
            else:
                carry = lax.fori_loop(0, Sk // TK, lambda j, c: step(j, c, None), init)
            ms, ls, acc = carry
            outs = [acc[hh * TQ:(hh + 1) * TQ] / ls[hh] for hh in range(nh)]
            lses = [ms[hh] + jnp.log(ls[hh]) for hh in range(nh)]
            rows = pl.ds(pl.multiple_of(i * TQ, TQ), TQ)
            if pair:
                o_ref[0, rows, :] = jnp.where(lane < HEAD_DIM, outs[0], outs[1])
                lse_ref[0, rows, :] = jnp.where(lane < HEAD_DIM, lses[0], lses[1])
            else:
                o_ref[0, rows, :] = outs[0]
                lse_ref[0, rows, :] = jnp.broadcast_to(lses[0], (TQ, LANES))
            return 0

        lax.fori_loop(0, S // TQ, q_loop, 0)

    ins, in_specs = _attn_inputs(kind, src, S, negc, mask, rope, kv, ())
    W = cfg["n_blocks"] * LANES
    out_spec = pl.BlockSpec((1, S, LANES), lambda b, h: (b, 0, h))
    return pl.pallas_call(
        body, name=kind + "_attn_fwd", grid=(B, cfg["n_blocks"]),
        in_specs=in_specs, out_specs=[out_spec, out_spec],
        out_shape=[jax.ShapeDtypeStruct((B, S, W), F32)] * 2,
        scratch_shapes=[pltpu.VMEM((nh * S, LANES), BF16), pltpu.VMEM((Sk, LANES), BF16),
                        pltpu.VMEM((Sk, LANES), BF16)],
        compiler_params=_params(("arbitrary", "arbitrary")),
    )(*ins)


def attn_bwd2(kind, src, do, o, lse, S, *, negc=None, mask=None, rope=None, kv=None):
    B = src.shape[0]
    cfg = _attn_setup(kind)
    pair, nh, s_scale, q_fold = cfg["pair"], cfg["nh"], cfg["s_scale"], cfg["q_fold"]
    Sk = S if pair else MEM_LEN
    has_bias, has_rope = negc is not None, rope is not None
    R = nh * TQ
    nq, nk = S // TQ, Sk // TK

    def body(*refs):
        refs = list(refs)
        if pair:
            qkv_ref = refs.pop(0)
        else:
            q_ref, k_ref, v_ref = refs.pop(0), refs.pop(0), refs.pop(0)
        do_ref, o_ref, lse_ref = refs.pop(0), refs.pop(0), refs.pop(0)
        negc_ref = refs.pop(0) if has_bias else None
        mask_ref = refs.pop(0) if mask is not None else None
        rope_refs = [refs.pop(0) for _ in range(3)] if has_rope else None
        if pair:
            dqkv_ref = refs.pop(0)
            dnegc_ref = refs.pop(0) if has_bias else None
            drow_ref = refs.pop(0) if has_bias else None
        else:
            dq_ref, dk_ref, dv_ref = refs.pop(0), refs.pop(0), refs.pop(0)
        qs2, ks, vs, dos2, lse_s, delta_s, dk_acc, dv_acc = refs[:8]
        dneg_acc = refs[8] if has_bias else None
        lane = lax.broadcasted_iota(jnp.int32, (1, LANES), 1)

        if pair:
            load_q = lambda rows: qkv_ref[0, rows, 0:LANES]
            load_kv = lambda rows: (qkv_ref[0, rows, LANES:2 * LANES], qkv_ref[0, rows, 2 * LANES:3 * LANES])
        else:
            load_q = lambda rows: q_ref[0, rows, :]
            load_kv = lambda rows: (k_ref[0, rows, :], v_ref[0, rows, :])
        _prep_rows(cfg, rope_refs, lane, load_q, load_kv, qs2, ks, vs, S, Sk)

        def prep_do(n, _):
            rows = pl.ds(pl.multiple_of(n * TQ, TQ), TQ)
            dob = do_ref[0, rows, :].astype(BF16)
            _store_stacked(cfg, lane, dos2, n, dob)
            prod = dob.astype(F32) * o_ref[0, rows, :]
            lse_blk = lse_ref[0, rows, :]
            for hh in range(nh):
                dst = pl.ds(pl.multiple_of(n * R + hh * TQ, TQ), TQ)
                if pair:
                    hmask = (lane >= HEAD_DIM * hh) & (lane < HEAD_DIM * (hh + 1))
                    d = jnp.sum(jnp.where(hmask, prod, 0.0), axis=1, keepdims=True)
                    lse_s[dst, :] = jnp.broadcast_to(lse_blk[:, hh * HEAD_DIM:hh * HEAD_DIM + 1], (TQ, LANES))
                else:
                    d = jnp.sum(prod, axis=1, keepdims=True)
                    lse_s[dst, :] = lse_blk
                delta_s[dst, :] = jnp.broadcast_to(d, (TQ, LANES))
            return 0

        def zero_kv(n, _):
            rows = pl.ds(pl.multiple_of(n * TK, TK), TK)
            dk_acc[rows, :] = jnp.zeros((TK, LANES), F32)
            dv_acc[rows, :] = jnp.zeros((TK, LANES), F32)
            return 0

        lax.fori_loop(0, nq, prep_do, 0)
        lax.fori_loop(0, nk, zero_kv, 0)
        if has_bias:
            dneg_acc[...] = jnp.zeros(dneg_acc.shape, F32)

        def q_loop(i, _):
            rows2 = pl.ds(pl.multiple_of(i * R, R), R)
            q2 = qs2[rows2, :]
            do2 = dos2[rows2, :]
            lse2 = lse_s[rows2, :]
            delta2 = delta_s[rows2, :]
            wide = lambda t: jnp.concatenate([t] * (TK // LANES), axis=1)

            def step(j, carry, midx):
                dq2, drow = carry
                c0 = pl.multiple_of(j * TK, TK)
                kcols = pl.ds(c0, TK)
                k = ks[kcols, :]
                v = vs[kcols, :]
                s2 = lax.dot_general(q2, k, (((1,), (1,)), ((), ())), preferred_element_type=F32)
                if s_scale is not None:
                    s2 = s2 * s_scale
                if has_bias or midx is not None:
                    halves = []
                    for hh in range(nh):
                        s = s2[hh * TQ:(hh + 1) * TQ]
                        if has_bias:
                            s = s + negc_ref[0, 0, pl.ds(hh, 1), kcols]
                        if midx is not None:
                            s = s + mask_ref[midx]
                        halves.append(s)
                    s2 = _cat(halves, 0)
                p2 = jnp.exp(s2 - wide(lse2))
                dp2 = lax.dot_general(do2, v, (((1,), (1,)), ((), ())), preferred_element_type=F32)
                ds2 = p2 * (dp2 - wide(delta2))
                if has_bias:
                    drow = drow + jnp.sum(ds2, axis=1, keepdims=True)
                    for hh in range(nh):
                        dneg_acc[pl.ds(hh, 1), kcols] += jnp.sum(ds2[hh * TQ:(hh + 1) * TQ], axis=0, keepdims=True)
                if s_scale is not None:
                    ds2 = ds2 * s_scale
                dsb = ds2.astype(BF16)
                dv_acc[kcols, :] += lax.dot_general(p2.astype(BF16), do2, (((0,), (0,)), ((), ())),
                                                    preferred_element_type=F32)
                dk_acc[kcols, :] += lax.dot_general(dsb, q2, (((0,), (0,)), ((), ())), preferred_element_type=F32)
                dq2 = dq2 + jnp.dot(dsb, k, preferred_element_type=F32)
                return dq2, drow

            init = (jnp.zeros((R, LANES), F32), jnp.zeros((R, 1), F32))
            if kind == "fox":
                carry = lax.fori_loop(0, i, lambda j, c: step(j, c, None), init)
                carry = step(i, carry, 0)
            elif kind == "dil":
                carry = lax.fori_loop(0, i + 1, lambda j, c: step(j, c, i - j), init)
            else:
                carry = lax.fori_loop(0, nk, lambda j, c: step(j, c, None), init)
            dq2, drow = carry
            rows = pl.ds(pl.multiple_of(i * TQ, TQ), TQ)
            dq = jnp.where(lane < HEAD_DIM, dq2[0:TQ], dq2[TQ:2 * TQ]) if pair else dq2
            if q_fold is not None:
                dq = dq * q_fold
            if has_rope:
                dq = _rope_bwd(dq, *[t[rows, :] for t in rope_refs])
            if pair:
                dqkv_ref[0, rows, 0:LANES] = dq.astype(BF16)
            else:
                dq_ref[0, rows, :] = dq.astype(BF16)
            if has_bias:
                drow_ref[0, rows, :] = jnp.where(lane < HEAD_DIM, drow[0:TQ], drow[TQ:2 * TQ])
            return 0

        lax.fori_loop(0, nq, q_loop, 0)

        def fin_kv(n, _):
            rows = pl.ds(pl.multiple_of(n * TK, TK), TK)
            dk = dk_acc[rows, :]
            if has_rope:
                dk = _rope_bwd(dk, *[t[rows, :] for t in rope_refs])
            if pair:
                dqkv_ref[0, rows, LANES:2 * LANES] = dk.astype(BF16)
                dqkv_ref[0, rows, 2 * LANES:3 * LANES] = dv_acc[rows, :].astype(BF16)
            else:
                dk_ref[0, rows, :] = dk.astype(BF16)
                dv_ref[0, rows, :] = dv_acc[rows, :].astype(BF16)
            return 0

        lax.fori_loop(0, nk, fin_kv, 0)
        if has_bias:
            dnegc_ref[0, 0] = dneg_acc[...]

    ins, in_specs = _attn_inputs(kind, src, S, negc, mask, rope, kv, (do, o, lse))
    W = cfg["n_blocks"] * LANES
    row_spec = pl.BlockSpec((1, S, LANES), lambda b, h: (b, 0, h))
    if pair:
        out_specs = [pl.BlockSpec((1, S, PAIR_W), lambda b, h: (b, 0, h))]
        out_shape = [jax.ShapeDtypeStruct((B, S, 3 * W), BF16)]
        if has_bias:
            out_specs += [pl.BlockSpec((1, 1, 2, S), lambda b, h: (b, h, 0, 0)), row_spec]
            out_shape += [jax.ShapeDtypeStruct((B, LANES // 2, 2, S), F32), jax.ShapeDtypeStruct((B, S, W), F32)]
    else:
        kv_spec = pl.BlockSpec((1, MEM_LEN, LANES), lambda b, h: (b, 0, h))
        out_specs = [row_spec, kv_spec, kv_spec]
        out_shape = [jax.ShapeDtypeStruct((B, S, W), BF16)] + [jax.ShapeDtypeStruct((B, MEM_LEN, W), BF16)] * 2
    scratch = [pltpu.VMEM((nh * S, LANES), BF16), pltpu.VMEM((Sk, LANES), BF16), pltpu.VMEM((Sk, LANES), BF16),
               pltpu.VMEM((nh * S, LANES), BF16), pltpu.VMEM((nh * S, LANES), F32), pltpu.VMEM((nh * S, LANES), F32),
               pltpu.VMEM((Sk, LANES), F32), pltpu.VMEM((Sk, LANES), F32)]
    if has_bias:
        scratch.append(pltpu.VMEM((2, S), F32))
    return pl.pallas_call(
        body, name=kind + "_attn_bwd", grid=(B, cfg["n_blocks"]),
        in_specs=in_specs, out_specs=out_specs, out_shape=out_shape, scratch_shapes=scratch,
        compiler_params=_params(("arbitrary", "arbitrary")),
    )(*ins)


def _log_masks_t(S, kind):
    return jnp.swapaxes(_log_masks(S, kind), 1, 2)


def _head_rows(hh, pair):
    row = lax.broadcasted_iota(jnp.int32, (LANES, 1), 0)
    if not pair:
        return row >= 0
    return (row >= HEAD_DIM * hh) & (row < HEAD_DIM * (hh + 1))


def _attn_t_inputs(kind, src, S, negc_cols, mask, rope, kv):
    cfg = _attn_setup(kind)
    col0 = cfg["col0"]
    ins, in_specs = [], []
    if cfg["pair"]:
        ins.append(src)
        in_specs.append(pl.BlockSpec((1, S, PAIR_W), lambda b, h: (b, 0, col0 // PAIR_W + h)))
    else:
        ins += [src, kv, kv]
        in_specs += [pl.BlockSpec((1, S, LANES), lambda b, h: (b, 0, col0 // LANES + h)),
                     pl.BlockSpec((1, MEM_LEN, LANES), lambda b, h: (b, 0, h)),
                     pl.BlockSpec((1, MEM_LEN, LANES), lambda b, h: (b, 0, MEM_HEADS + h))]
    if negc_cols is not None:
        ins.append(negc_cols)
        in_specs.append(pl.BlockSpec((1, S, LANES), lambda b, h: (b, 0, 0)))
    if mask is not None:
        ins.append(mask)
        in_specs.append(pl.BlockSpec(mask.shape, lambda b, h: (0, 0, 0)))
    if rope is not None:
        ins += list(rope)
        in_specs += [pl.BlockSpec((S, LANES), lambda b, h: (0, 0))] * 3
    return ins, in_specs


def _attn_t_prep(cfg, refs, S, Sk, *, qT2s, ks, q2s=None, vs=None, vTs=None, kTs=None, nb=None):
    pair, nh = cfg["pair"], cfg["nh"]
    lane = lax.broadcasted_iota(jnp.int32, (1, LANES), 1)
    rope_refs = refs["rope"]

    def prep_q(n, _):
        rows = pl.ds(pl.multiple_of(n * TQ, TQ), TQ)
        q = refs["load_q"](rows)
        if rope_refs is not None:
            q = _rope(q, *[t[rows, :] for t in rope_refs])
        if cfg["q_fold"] is not None:
            q = q * cfg["q_fold"]
        qb = q.astype(BF16)
        if q2s is not None:
            _store_stacked(cfg, lane, q2s, n, qb)
        qtb = qb.astype(F32).T.astype(BF16)
        for hh in range(nh):
            qT2s[n, :, hh * TQ:(hh + 1) * TQ] = jnp.where(_head_rows(hh, pair), qtb, jnp.zeros_like(qtb))
        return 0

    def prep_kv(n, _):
        rows = pl.ds(pl.multiple_of(n * TK, TK), TK)
        k, v = refs["load_kv"](rows)
        if rope_refs is not None:
            k = _rope(k, *[t[rows, :] for t in rope_refs])
        kb = k.astype(BF16)
        vb = v.astype(BF16)
        ks[rows, :] = kb
        if vs is not None:
            vs[rows, :] = vb
        if vTs is not None:
            vTs[n] = vb.astype(F32).T.astype(BF16)
        if kTs is not None:
            kTs[n] = kb.astype(F32).T.astype(BF16)
        if nb is not None:
            blk = refs["negc"][0, rows, :]
            for hh in range(nh):
                h = 2 * refs["block"] + hh
                col = jnp.sum(jnp.where(lane == h, blk, 0.0), axis=1, keepdims=True)
                nb[hh, rows, :] = jnp.broadcast_to(col, (TK, LANES))
        return 0

    lax.fori_loop(0, S // TQ, prep_q, 0)
    lax.fori_loop(0, Sk // TK, prep_kv, 0)


def _raw_scores_t(cfg, k, qT2):
    sT = jnp.dot(k, qT2, preferred_element_type=F32)
    if cfg["s_scale"] is not None:
        sT = sT * cfg["s_scale"]
    return sT


def _bias_mask_t(cfg, sT, nb, mask_ref, kc, midx):
    nh = cfg["nh"]
    if nb is None and midx is None:
        return sT
    parts = []
    for hh in range(nh):
        t = sT[:, hh * TQ:(hh + 1) * TQ]
        if nb is not None:
            t = t + jnp.concatenate([nb[hh, kc, :]] * (TQ // LANES), axis=1)
        if midx is not None:
            t = t + mask_ref[midx]
        parts.append(t)
    return _cat(parts, 1)


def _kv_plan(kind, i, nk):
    if kind == "fox":
        return i, (lambda j: None), 0
    if kind == "dil":
        return i, (lambda j: i - j), 0
    return nk - 1, (lambda j: None), None


def attn_fwd3(kind, src, S, *, negc_cols=None, mask=None, rope=None, kv=None):
    B = src.shape[0]
    cfg = _attn_setup(kind)
    pair, nh = cfg["pair"], cfg["nh"]
    Sk = S if pair else MEM_LEN
    has_bias, has_rope = negc_cols is not None, rope is not None
    R = nh * TQ
    nq, nk = S // TQ, Sk // TK

    def body(*refs):
        refs = list(refs)
        if pair:
            qkv_ref = refs.pop(0)
            load_q = lambda rows: qkv_ref[0, rows, 0:LANES]
            load_kv = lambda rows: (qkv_ref[0, rows, LANES:2 * LANES], qkv_ref[0, rows, 2 * LANES:3 * LANES])
        else:
            q_ref, k_ref, v_ref = refs.pop(0), refs.pop(0), refs.pop(0)
            load_q = lambda rows: q_ref[0, rows, :]
            load_kv = lambda rows: (k_ref[0, rows, :], v_ref[0, rows, :])
        negc_ref = refs.pop(0) if has_bias else None
        mask_ref = refs.pop(0) if mask is not None else None
        rope_refs = [refs.pop(0) for _ in range(3)] if has_rope else None
        o_ref, lse_ref, qT2s, ks, vTs = refs[:5]
        nb = refs[5] if has_bias else None
        _attn_t_prep(cfg, dict(load_q=load_q, load_kv=load_kv, rope=rope_refs, negc=negc_ref,
                               block=pl.program_id(1)), S, Sk,
                     qT2s=qT2s, ks=ks, vTs=vTs, nb=nb)

        def q_loop(i, _):
            qT2 = qT2s[i]

            last, mask_of, mask_last = _kv_plan(kind, i, nk)

            def cols(j):
                return pl.ds(pl.multiple_of(j * TK, TK), TK)

            def scores(j):
                return _raw_scores_t(cfg, ks[cols(j), :], qT2)

            def soft(s_raw, j, midx, m, l):
                sT = _bias_mask_t(cfg, s_raw, nb, mask_ref, cols(j), midx)
                m_new = jnp.maximum(m, jnp.max(sT, axis=0, keepdims=True))
                p = jnp.exp(sT - m_new)
                alpha = jnp.exp(m - m_new)
                return m_new, alpha * l + jnp.sum(p, axis=0, keepdims=True), alpha, p.astype(BF16)

            def pv(j, p):
                return jnp.dot(vTs[j], p, preferred_element_type=F32)

            def body(j, carry):
                s_cur, p_prev, m, l, accT = carry
                pv_prev = pv(jnp.maximum(j - 1, 0), p_prev)
                s_next = scores(j + 1)
                m, l, alpha, p = soft(s_cur, j, mask_of(j), m, l)
                return s_next, p, m, l, (accT + pv_prev) * alpha

            init = (scores(0), jnp.zeros((TK, R), BF16), jnp.full((1, R), NEG_INF, F32), jnp.zeros((1, R), F32),
                    jnp.zeros((LANES, R), F32))
            s_cur, p_prev, m, l, accT = lax.fori_loop(0, last, body, init)
            pv_prev = pv(jnp.maximum(last - 1, 0), p_prev)
            m, l, alpha, p = soft(s_cur, last, mask_last, m, l)
            accT = (accT + pv_prev) * alpha + pv(last, p)
            oT2 = accT / l
            oT = jnp.where(_head_rows(0, True), oT2[:, 0:TQ], oT2[:, TQ:2 * TQ]) if pair else oT2
            o_ref[0, pl.ds(pl.multiple_of(i * TQ, TQ), TQ), :] = oT.T
            lse_ref[0, 0, pl.ds(i, 1), :] = m + jnp.log(l)
            return 0

        lax.fori_loop(0, nq, q_loop, 0)

    ins, in_specs = _attn_t_inputs(kind, src, S, negc_cols, mask, rope, kv)
    W = cfg["n_blocks"] * LANES
    scratch = [pltpu.VMEM((nq, LANES, R), BF16), pltpu.VMEM((Sk, LANES), BF16), pltpu.VMEM((nk, LANES, TK), BF16)]
    if has_bias:
        scratch.append(pltpu.VMEM((nh, Sk, LANES), F32))
    return pl.pallas_call(
        body, name=kind + "_attn_fwd", grid=(B, cfg["n_blocks"]),
        in_specs=in_specs,
        out_specs=[pl.BlockSpec((1, S, LANES), lambda b, h: (b, 0, h)),
                   pl.BlockSpec((1, 1, nq, R), lambda b, h: (b, h, 0, 0))],
        out_shape=[jax.ShapeDtypeStruct((B, S, W), F32), jax.ShapeDtypeStruct((B, cfg["n_blocks"], nq, R), F32)],
        scratch_shapes=scratch,
        compiler_params=_params(("arbitrary", "arbitrary")),
    )(*ins)


def _tile_walk(kind, nq, nk):
    if kind == "mem":
        return nq * nk, (lambda i, j: (jnp.where(j < nk - 1, i, i + 1), jnp.where(j < nk - 1, j + 1, 0))), None
    nxt = lambda i, j: (jnp.where(j < i, i, i + 1), jnp.where(j < i, j + 1, 0))
    if kind == "fox":
        return nq * (nq + 1) // 2, nxt, (lambda i, j: jnp.where(j == i, 0, 1))
    return nq * (nq + 1) // 2, nxt, (lambda i, j: i - j)


def attn_fwd4(kind, src, S, *, negc_cols=None, mask=None, rope=None, kv=None):
    B = src.shape[0]
    cfg = _attn_setup(kind)
    pair, nh = cfg["pair"], cfg["nh"]
    Sk = S if pair else MEM_LEN
    has_bias, has_rope = negc_cols is not None, rope is not None
    R = nh * TQ
    nq, nk = S // TQ, Sk // TK
    n_pairs, successor, mask_index = _tile_walk(kind, nq, nk)
    assert n_pairs % 2 == 0

    def body(*refs):
        refs = list(refs)
        if pair:
            qkv_ref = refs.pop(0)
            load_q = lambda rows: qkv_ref[0, rows, 0:LANES]
            load_kv = lambda rows: (qkv_ref[0, rows, LANES:2 * LANES], qkv_ref[0, rows, 2 * LANES:3 * LANES])
        else:
            q_ref, k_ref, v_ref = refs.pop(0), refs.pop(0), refs.pop(0)
            load_q = lambda rows: q_ref[0, rows, :]
            load_kv = lambda rows: (k_ref[0, rows, :], v_ref[0, rows, :])
        negc_ref = refs.pop(0) if has_bias else None
        mask_ref = refs.pop(0) if mask is not None else None
        rope_refs = [refs.pop(0) for _ in range(3)] if has_rope else None
        o_ref, lse_ref, qT2s, ks, vTs, s_a, s_b, p_a, p_b, acc_all, m_all, l_all = refs[:12]
        nb = refs[12] if has_bias else None
        _attn_t_prep(cfg, dict(load_q=load_q, load_kv=load_kv, rope=rope_refs, negc=negc_ref,
                               block=pl.program_id(1)), S, Sk, qT2s=qT2s, ks=ks, vTs=vTs, nb=nb)

        def cols(j):
            return pl.ds(pl.multiple_of(j * TK, TK), TK)

        def park(i, m, l, accT):
            acc_all[i] = accT
            m_all[pl.ds(i, 1), :] = m
            l_all[pl.ds(i, 1), :] = l

        def finish(i, _):
            l = l_all[pl.ds(i, 1), :]
            oT2 = acc_all[i] / l
            oT = jnp.where(_head_rows(0, True), oT2[:, 0:TQ], oT2[:, TQ:2 * TQ]) if pair else oT2
            o_ref[0, pl.ds(pl.multiple_of(i * TQ, TQ), TQ), :] = oT.T
            lse_ref[0, 0, pl.ds(i, 1), :] = m_all[pl.ds(i, 1), :] + jnp.log(l)
            return 0

        def half(i, j, i_prev, j_prev, s_cur, s_next, p_cur, p_prev, m, l, accT):
            i_n, j_n = successor(i, j)
            acc_full = accT + jnp.dot(vTs[j_prev], p_prev[...], preferred_element_type=F32)
            s_next[...] = _raw_scores_t(cfg, ks[cols(j_n), :], qT2s[jnp.minimum(i_n, nq - 1)])
            park(i_prev, m, l, acc_full)
            first = j == 0
            m = jnp.where(first, NEG_INF, m)
            l = jnp.where(first, 0.0, l)
            sT = _bias_mask_t(cfg, s_cur[...], nb, mask_ref, cols(j), None if mask_index is None else mask_index(i, j))
            m_new = jnp.maximum(m, jnp.max(sT, axis=0, keepdims=True))
            p = jnp.exp(sT - m_new)
            alpha = jnp.exp(m - m_new)
            p_cur[...] = p.astype(BF16)
            return i_n, j_n, i, j, m_new, alpha * l + jnp.sum(p, axis=0, keepdims=True), acc_full * alpha

        def two(_, carry):
            i, j, i_prev, j_prev, m, l, accT = carry
            i, j, i_prev, j_prev, m, l, accT = half(i, j, i_prev, j_prev, s_a, s_b, p_a, p_b, m, l, accT)
            return half(i, j, i_prev, j_prev, s_b, s_a, p_b, p_a, m, l, accT)

        s_a[...] = _raw_scores_t(cfg, ks[cols(0), :], qT2s[0])
        p_b[...] = jnp.zeros((TK, R), BF16)
        zero = jnp.int32(0)
        init = (zero, zero, zero, zero, jnp.full((1, R), NEG_INF, F32), jnp.ones((1, R), F32),
                jnp.zeros((LANES, R), F32))
        _, _, i_prev, j_prev, m, l, accT = lax.fori_loop(0, n_pairs // 2, two, init)
        park(i_prev, m, l, accT + jnp.dot(vTs[j_prev], p_b[...], preferred_element_type=F32))
        lax.fori_loop(0, nq, finish, 0)

    ins, in_specs = _attn_t_inputs(kind, src, S, negc_cols, mask, rope, kv)
    W = cfg["n_blocks"] * LANES
    scratch = [pltpu.VMEM((nq, LANES, R), BF16), pltpu.VMEM((Sk, LANES), BF16), pltpu.VMEM((nk, LANES, TK), BF16),
               pltpu.VMEM((TK, R), F32), pltpu.VMEM((TK, R), F32), pltpu.VMEM((TK, R), BF16), pltpu.VMEM((TK, R), BF16),
               pltpu.VMEM((nq, LANES, R), F32), pltpu.VMEM((nq, R), F32), pltpu.VMEM((nq, R), F32)]
    if has_bias:
        scratch.append(pltpu.VMEM((nh, Sk, LANES), F32))
    return pl.pallas_call(
        body, name=kind + "_attn_fwd", grid=(B, cfg["n_blocks"]),
        in_specs=in_specs,
        out_specs=[pl.BlockSpec((1, S, LANES), lambda b, h: (b, 0, h)),
                   pl.BlockSpec((1, 1, nq, R), lambda b, h: (b, h, 0, 0))],
        out_shape=[jax.ShapeDtypeStruct((B, S, W), F32), jax.ShapeDtypeStruct((B, cfg["n_blocks"], nq, R), F32)],
        scratch_shapes=scratch,
        compiler_params=_params(("arbitrary", "arbitrary")),
    )(*ins)


def attn_bwd3(kind, src, do, o, lse, S, *, negc_cols=None, mask=None, rope=None, kv=None):
    B = src.shape[0]
    cfg = _attn_setup(kind)
    pair, nh, s_scale, q_fold = cfg["pair"], cfg["nh"], cfg["s_scale"], cfg["q_fold"]
    Sk = S if pair else MEM_LEN
    has_bias, has_rope = negc_cols is not None, rope is not None
    R = nh * TQ
    nq, nk = S // TQ, Sk // TK
    n_pairs, successor, mask_index = _tile_walk(kind, nq, nk)
    assert n_pairs % 2 == 0

    def body(*refs):
        refs = list(refs)
        if pair:
            qkv_ref = refs.pop(0)
            load_q = lambda rows: qkv_ref[0, rows, 0:LANES]
            load_kv = lambda rows: (qkv_ref[0, rows, LANES:2 * LANES], qkv_ref[0, rows, 2 * LANES:3 * LANES])
        else:
            q_ref, k_ref, v_ref = refs.pop(0), refs.pop(0), refs.pop(0)
            load_q = lambda rows: q_ref[0, rows, :]
            load_kv = lambda rows: (k_ref[0, rows, :], v_ref[0, rows, :])
        negc_ref = refs.pop(0) if has_bias else None
        mask_ref = refs.pop(0) if mask is not None else None
        rope_refs = [refs.pop(0) for _ in range(3)] if has_rope else None
        do_ref, o_ref, lse_ref = refs.pop(0), refs.pop(0), refs.pop(0)
        if pair:
            dqkv_ref = refs.pop(0)
            dneg_ref = refs.pop(0) if has_bias else None
            drow_ref = refs.pop(0) if has_bias else None
        else:
            dq_ref, dk_ref, dv_ref = refs.pop(0), refs.pop(0), refs.pop(0)
        qT2s, ks, q2s, vs, kTs, doT2s, do2s, delta_s, dk_acc, dv_acc = refs[:10]
        bufs_a, bufs_b, dq_all = refs[10:14], refs[14:18], refs[18]
        nb, dneg_acc, drow_all = (refs[19], refs[20], refs[21]) if has_bias else (None, None, None)
        lane = lax.broadcasted_iota(jnp.int32, (1, LANES), 1)
        _attn_t_prep(cfg, dict(load_q=load_q, load_kv=load_kv, rope=rope_refs, negc=negc_ref,
                               block=pl.program_id(1)), S, Sk,
                     qT2s=qT2s, ks=ks, q2s=q2s, vs=vs, kTs=kTs, nb=nb)

        def prep_do(n, _):
            rows = pl.ds(pl.multiple_of(n * TQ, TQ), TQ)
            dob = do_ref[0, rows, :].astype(BF16)
            _store_stacked(cfg, lane, do2s, n, dob)
            doT = dob.astype(F32).T
            prodT = doT * o_ref[0, rows, :].T
            doTb = doT.astype(BF16)
            for hh in range(nh):
                hm = _head_rows(hh, pair)
                doT2s[n, :, hh * TQ:(hh + 1) * TQ] = jnp.where(hm, doTb, jnp.zeros_like(doTb))
                delta_s[pl.ds(n, 1), hh * TQ:(hh + 1) * TQ] = jnp.sum(jnp.where(hm, prodT, 0.0), axis=0, keepdims=True)
            return 0

        def zero_kv(n, _):
            rows = pl.ds(pl.multiple_of(n * TK, TK), TK)
            dk_acc[rows, :] = jnp.zeros((TK, LANES), F32)
            dv_acc[rows, :] = jnp.zeros((TK, LANES), F32)
            if has_bias:
                for hh in range(nh):
                    dneg_acc[hh, rows, :] = jnp.zeros((TK, LANES), F32)
            return 0

        lax.fori_loop(0, nq, prep_do, 0)
        lax.fori_loop(0, nk, zero_kv, 0)

        def cols(j):
            return pl.ds(pl.multiple_of(j * TK, TK), TK)

        def rows2(i):
            return pl.ds(pl.multiple_of(i * R, R), R)

        def park(i, dqT2, drow):
            dq_all[i] = dqT2
            if has_bias:
                drow_all[pl.ds(i, 1), :] = drow

        def half(i, j, i_prev, j_prev, cur, nxt_bufs, prv, dqT2, drow):
            s_cur, dp_cur, pb_cur, dsb_cur = cur
            s_next, dp_next = nxt_bufs[0], nxt_bufs[1]
            pb_prev, dsb_prev = prv[2], prv[3]
            i_n, j_n = successor(i, j)
            first = j == 0
            if has_bias:
                drow_all[pl.ds(i_prev, 1), :] = drow
            drow = jnp.where(first, 0.0, drow)
            kc = cols(j)
            sT = _bias_mask_t(cfg, s_cur[...], nb, mask_ref, kc, None if mask_index is None else mask_index(i, j))
            pT = jnp.exp(sT - lse_ref[0, 0, pl.ds(i, 1), :])
            dsT = pT * (dp_cur[...] - delta_s[pl.ds(i, 1), :])
            if has_bias:
                drow = drow + jnp.sum(dsT, axis=0, keepdims=True)
                for hh in range(nh):
                    part = dsT[:, hh * TQ:hh * TQ + LANES]
                    for t in range(1, TQ // LANES):
                        part = part + dsT[:, hh * TQ + t * LANES:hh * TQ + (t + 1) * LANES]
                    dneg_acc[hh, kc, :] += part
            if s_scale is not None:
                dsT = dsT * s_scale
            pb_cur[...] = pT.astype(BF16)
            dsb_cur[...] = dsT.astype(BF16)
            kp = cols(j_prev)
            dv_acc[kp, :] += jnp.dot(pb_prev[...], do2s[rows2(i_prev), :], preferred_element_type=F32)
            dk_acc[kp, :] += jnp.dot(dsb_prev[...], q2s[rows2(i_prev), :], preferred_element_type=F32)
            dq_full = dqT2 + jnp.dot(kTs[j_prev], dsb_prev[...], preferred_element_type=F32)
            dq_all[i_prev] = dq_full
            dqT2 = jnp.where(first, 0.0, dq_full)
            i_nc = jnp.minimum(i_n, nq - 1)
            kn = cols(j_n)
            s_next[...] = _raw_scores_t(cfg, ks[kn, :], qT2s[i_nc])
            dp_next[...] = jnp.dot(vs[kn, :], doT2s[i_nc], preferred_element_type=F32)
            return i_n, j_n, i, j, dqT2, drow

        def two(_, carry):
            i, j, i_prev, j_prev, dqT2, drow = carry
            i, j, i_prev, j_prev, dqT2, drow = half(i, j, i_prev, j_prev, bufs_a, bufs_b, bufs_b, dqT2, drow)
            return half(i, j, i_prev, j_prev, bufs_b, bufs_a, bufs_a, dqT2, drow)

        bufs_a[0][...] = _raw_scores_t(cfg, ks[cols(0), :], qT2s[0])
        bufs_a[1][...] = jnp.dot(vs[cols(0), :], doT2s[0], preferred_element_type=F32)
        bufs_b[2][...] = jnp.zeros((TK, R), BF16)
        bufs_b[3][...] = jnp.zeros((TK, R), BF16)
        zero = jnp.int32(0)
        init = (zero, zero, zero, zero, jnp.zeros((LANES, R), F32), jnp.zeros((1, R), F32))
        _, _, i_prev, j_prev, dqT2, drow = lax.fori_loop(0, n_pairs // 2, two, init)
        kp = cols(j_prev)
        dv_acc[kp, :] += jnp.dot(bufs_b[2][...], do2s[rows2(i_prev), :], preferred_element_type=F32)
        dk_acc[kp, :] += jnp.dot(bufs_b[3][...], q2s[rows2(i_prev), :], preferred_element_type=F32)
        park(i_prev, dqT2 + jnp.dot(kTs[j_prev], bufs_b[3][...], preferred_element_type=F32), drow)

        def fin_q(i, _):
            rows = pl.ds(pl.multiple_of(i * TQ, TQ), TQ)
            dqT2 = dq_all[i]
            dqT = jnp.where(_head_rows(0, True), dqT2[:, 0:TQ], dqT2[:, TQ:2 * TQ]) if pair else dqT2
            dq = dqT.T
            if q_fold is not None:
                dq = dq * q_fold
            if has_rope:
                dq = _rope_bwd(dq, *[t[rows, :] for t in rope_refs])
            if pair:
                dqkv_ref[0, rows, 0:LANES] = dq.astype(BF16)
            else:
                dq_ref[0, rows, :] = dq.astype(BF16)
            if has_bias:
                drow_ref[0, 0, pl.ds(i, 1), :] = drow_all[pl.ds(i, 1), :]
            return 0

        lax.fori_loop(0, nq, fin_q, 0)

        def fin_kv(n, _):
            rows = pl.ds(pl.multiple_of(n * TK, TK), TK)
            dk = dk_acc[rows, :]
            if has_rope:
                dk = _rope_bwd(dk, *[t[rows, :] for t in rope_refs])
            if pair:
                dqkv_ref[0, rows, LANES:2 * LANES] = dk.astype(BF16)
                dqkv_ref[0, rows, 2 * LANES:3 * LANES] = dv_acc[rows, :].astype(BF16)
            else:
                dk_ref[0, rows, :] = dk.astype(BF16)
                dv_ref[0, rows, :] = dv_acc[rows, :].astype(BF16)
            if has_bias:
                x0 = jnp.sum(dneg_acc[0, rows, :], axis=1, keepdims=True)
                x1 = jnp.sum(dneg_acc[1, rows, :], axis=1, keepdims=True)
                dneg_ref[0, rows, :] = jnp.where(lane == 0, x0, jnp.where(lane == 1, x1, 0.0))
            return 0

        lax.fori_loop(0, nk, fin_kv, 0)

    ins, in_specs = _attn_t_inputs(kind, src, S, negc_cols, mask, rope, kv)
    row_spec = pl.BlockSpec((1, S, LANES), lambda b, h: (b, 0, h))
    vec_spec = pl.BlockSpec((1, 1, nq, R), lambda b, h: (b, h, 0, 0))
    ins += [do, o, lse]
    in_specs += [row_spec, row_spec, vec_spec]
    W = cfg["n_blocks"] * LANES
    if pair:
        out_specs = [pl.BlockSpec((1, S, PAIR_W), lambda b, h: (b, 0, h))]
        out_shape = [jax.ShapeDtypeStruct((B, S, 3 * W), BF16)]
        if has_bias:
            out_specs += [row_spec, vec_spec]
            out_shape += [jax.ShapeDtypeStruct((B, S, W), F32), jax.ShapeDtypeStruct((B, cfg["n_blocks"], nq, R), F32)]
    else:
        kv_spec = pl.BlockSpec((1, MEM_LEN, LANES), lambda b, h: (b, 0, h))
        out_specs = [row_spec, kv_spec, kv_spec]
        out_shape = [jax.ShapeDtypeStruct((B, S, W), BF16)] + [jax.ShapeDtypeStruct((B, MEM_LEN, W), BF16)] * 2
    scratch = [pltpu.VMEM((nq, LANES, R), BF16), pltpu.VMEM((Sk, LANES), BF16), pltpu.VMEM((nh * S, LANES), BF16),
               pltpu.VMEM((Sk, LANES), BF16), pltpu.VMEM((nk, LANES, TK), BF16), pltpu.VMEM((nq, LANES, R), BF16),
               pltpu.VMEM((nh * S, LANES), BF16), pltpu.VMEM((nq, R), F32),
               pltpu.VMEM((Sk, LANES), F32), pltpu.VMEM((Sk, LANES), F32)]
    pair_bufs = [pltpu.VMEM((TK, R), F32), pltpu.VMEM((TK, R), F32), pltpu.VMEM((TK, R), BF16), pltpu.VMEM((TK, R), BF16)]
    scratch += pair_bufs + pair_bufs + [pltpu.VMEM((nq, LANES, R), F32)]
    if has_bias:
        scratch += [pltpu.VMEM((nh, Sk, LANES), F32), pltpu.VMEM((nh, Sk, LANES), F32), pltpu.VMEM((nq, R), F32)]
    return pl.pallas_call(
        body, name=kind + "_attn_bwd", grid=(B, cfg["n_blocks"]),
        in_specs=in_specs, out_specs=out_specs, out_shape=out_shape, scratch_shapes=scratch,
        compiler_params=_params(("arbitrary", "arbitrary")),
    )(*ins)


def _sigmoid(g):
    return 1.0 / (1.0 + jnp.exp(-g))


def out_fwd(proj, o_fox, o_dil, o_mem, w_out, x, target, gf, tm):
    T = x.shape[0]

    def body(fg_ref, dg_ref, mg_ref, of_ref, od_ref, om_ref, w_ref, x_ref, t_ref, gf_ref,
             y_ref, dx_ref, dxb_ref, sm_ref):
        parts = []
        for g_ref, o_ref in ((fg_ref, of_ref), (dg_ref, od_ref), (mg_ref, om_ref)):
            g = g_ref[...]
            parts.append((o_ref[...] * (g * _sigmoid(g))).astype(BF16))
        ymix = jnp.concatenate(parts, axis=1)
        y_ref[...] = ymix
        x2 = x_ref[...] + jnp.dot(ymix, w_ref[...], preferred_element_type=F32)
        r = lax.rsqrt(jnp.mean(x2 * x2, axis=-1, keepdims=True) + RMS_EPS)
        yn = x2 * r
        err = yn * gf_ref[...] - t_ref[...]
        loss = 0.5 * jnp.sum(jnp.sum(err * err, axis=-1, keepdims=True) / D_MODEL, axis=0, keepdims=True)
        dyf = err / D_MODEL
        dgf = jnp.sum(dyf * yn, axis=0, keepdims=True)
        dyn = dyf * gf_ref[...]
        dx2 = r * (dyn - yn * jnp.mean(dyn * yn, axis=-1, keepdims=True))
        dx_ref[...] = dx2
        dxb_ref[...] = dx2.astype(BF16)
        row = lax.broadcasted_iota(jnp.int32, (8, D_MODEL), 0)
        upd = jnp.where(row == 0, dgf, jnp.where(row == 1, loss, 0.0))

        @pl.when(pl.program_id(0) == 0)
        def _():
            sm_ref[...] = upd

        @pl.when(pl.program_id(0) != 0)
        def _():
            sm_ref[...] += upd

    def rows(w, col=0):
        return pl.BlockSpec((tm, w), lambda i: (i, col))

    return pl.pallas_call(
        body, name="out_fwd", grid=(T // tm,),
        in_specs=[rows(FOX_W, P_FG // FOX_W), rows(DIL_W, P_DG // DIL_W), rows(MEM_W, P_MG // MEM_W),
                  rows(FOX_W), rows(DIL_W), rows(MEM_W),
                  pl.BlockSpec((MIX_W, D_MODEL), lambda i: (0, 0)),
                  rows(D_MODEL), rows(D_MODEL), pl.BlockSpec((1, D_MODEL), lambda i: (0, 0))],
        out_specs=[rows(MIX_W), rows(D_MODEL), rows(D_MODEL), pl.BlockSpec((8, D_MODEL), lambda i: (0, 0))],
        out_shape=[jax.ShapeDtypeStruct((T, MIX_W), BF16), jax.ShapeDtypeStruct((T, D_MODEL), F32),
                   jax.ShapeDtypeStruct((T, D_MODEL), BF16), jax.ShapeDtypeStruct((8, D_MODEL), F32)],
        compiler_params=_params(("arbitrary",)),
    )(proj, proj, proj, o_fox, o_dil, o_mem, w_out, x, target, gf)


def out_bwd(proj, o_fox, o_dil, o_mem, w_out, dx2b, tm):
    T = dx2b.shape[0]

    def body(fg_ref, dg_ref, mg_ref, of_ref, od_ref, om_ref, w_ref, dx_ref,
             dof_ref, dod_ref, dom_ref, dfg_ref, ddg_ref, dmg_ref):
        dmix = lax.dot_general(dx_ref[...], w_ref[...], (((1,), (1,)), ((), ())), preferred_element_type=F32)
        col = 0
        for g_ref, o_ref, do_ref, dgate_ref in ((fg_ref, of_ref, dof_ref, dfg_ref), (dg_ref, od_ref, dod_ref, ddg_ref),
                                                 (mg_ref, om_ref, dom_ref, dmg_ref)):
            w = g_ref.shape[1]
            d = dmix[:, col:col + w]
            col += w
            g = g_ref[...]
            sg = _sigmoid(g)
            do_ref[...] = d * (g * sg)
            dgate_ref[...] = (d * o_ref[...] * (sg * (1.0 + g * (1.0 - sg)))).astype(BF16)

    def rows(w, col=0):
        return pl.BlockSpec((tm, w), lambda i: (i, col))

    return pl.pallas_call(
        body, name="out_bwd", grid=(T // tm,),
        in_specs=[rows(FOX_W, P_FG // FOX_W), rows(DIL_W, P_DG // DIL_W), rows(MEM_W, P_MG // MEM_W),
                  rows(FOX_W), rows(DIL_W), rows(MEM_W),
                  pl.BlockSpec((MIX_W, D_MODEL), lambda i: (0, 0)), rows(D_MODEL)],
        out_specs=[rows(FOX_W), rows(DIL_W), rows(MEM_W), rows(FOX_W), rows(DIL_W), rows(MEM_W)],
        out_shape=[jax.ShapeDtypeStruct((T, FOX_W), F32), jax.ShapeDtypeStruct((T, DIL_W), F32),
                   jax.ShapeDtypeStruct((T, MEM_W), F32), jax.ShapeDtypeStruct((T, FOX_W), BF16),
                   jax.ShapeDtypeStruct((T, DIL_W), BF16), jax.ShapeDtypeStruct((T, MEM_W), BF16)],
        compiler_params=_params(("arbitrary",)),
    )(proj, proj, proj, o_fox, o_dil, o_mem, w_out, dx2b)


def adamw(w, g, m, v, tr, name):
    lead = w.shape[:-2]
    R, C = w.shape[-2:]
    zeros = (0,) * len(lead)

    def body(w_ref, g_ref, m_ref, v_ref, d_ref, mo_ref, vo_ref):
        gv = g_ref[...]
        mn = ADAM_B1 * m_ref[...] + (1.0 - ADAM_B1) * gv
        vn = ADAM_B2 * v_ref[...] + (1.0 - ADAM_B2) * jnp.square(gv)
        m_hat = mn / (1.0 - ADAM_B1 ** ADAM_STEP)
        v_hat = vn / (1.0 - ADAM_B2 ** ADAM_STEP)
        d_ref[...] = -ADAM_LR * (m_hat / (jnp.sqrt(v_hat) + ADAM_EPS) + ADAM_WD * w_ref[...])
        mo_ref[...] = mn
        vo_ref[...] = vn

    spec = pl.BlockSpec((1,) * len(lead) + (tr, C), lambda i: zeros + (i, 0))
    return pl.pallas_call(
        body, name=name, grid=(pl.cdiv(R, tr),),
        in_specs=[spec] * 4, out_specs=[spec] * 3,
        out_shape=[jax.ShapeDtypeStruct(w.shape, F32)] * 3,
        compiler_params=_params(("arbitrary",)),
    )(w, g, m, v)


def _pad_row(v, width):
    return jnp.concatenate([v, jnp.zeros((1, width - v.shape[1]), v.dtype)], axis=1)


def local_grads(x, mem, norm_g, b_forget, mem_norm_g, final_norm_g, loss_target, w_in_p, w_kv, w_out):
    B, S, D = x.shape
    T = B * S
    xt = x.reshape(T, D)
    memt = mem.reshape(B * MEM_LEN, D)
    b_pad = _pad_row(b_forget, LANES)

    h = rms_fwd(xt, norm_g, 512, "rms_x")
    proj = mm_nn(h, w_in_p, 512, PW // 3, "in_proj")
    proj3 = proj.reshape(B, S, PW)
    mh = rms_fwd(memt, mem_norm_g, B * MEM_LEN, "rms_mem")
    mkv = mm_nn(mh, w_kv, B * MEM_LEN, 2 * MEM_W, "mem_kv_proj")
    mkv3 = mkv.reshape(B, MEM_LEN, 2 * MEM_W)

    negc = fox_gate(proj3, b_pad)
    causal = _log_masks_t(S, "causal")
    causal = jnp.concatenate([causal, jnp.zeros_like(causal)], axis=0)
    dilated = _log_masks_t(S, "dilated")
    rope = _rope_tables(S)

    o_fox, lse_fox = attn_fwd4("fox", proj3, S, negc_cols=negc, mask=causal)
    o_dil, lse_dil = attn_fwd4("dil", proj3, S, mask=dilated, rope=rope)
    o_mem, lse_mem = attn_fwd4("mem", proj3, S, kv=mkv3)

    ymix, dx2, dx2b, small_out = out_fwd(
        proj, o_fox.reshape(T, FOX_W), o_dil.reshape(T, DIL_W), o_mem.reshape(T, MEM_W), w_out,
        xt, loss_target.reshape(T, D), final_norm_g.reshape(1, D), 256)
    do_fox, do_dil, do_mem, dfg, ddg, dmg = out_bwd(
        proj, o_fox.reshape(T, FOX_W), o_dil.reshape(T, DIL_W), o_mem.reshape(T, MEM_W), w_out, dx2b, 256)
    g_out = mm_tn(ymix, dx2b, 512, D, "w_out_grad")

    dqkv_fox, dneg, drow = attn_bwd3("fox", proj3, do_fox.reshape(B, S, FOX_W), o_fox, lse_fox, S,
                                     negc_cols=negc, mask=causal)
    (dqkv_dil,) = attn_bwd3("dil", proj3, do_dil.reshape(B, S, DIL_W), o_dil, lse_dil, S, mask=dilated, rope=rope)
    dmq, dmk, dmv = attn_bwd3("mem", proj3, do_mem.reshape(B, S, MEM_W), o_mem, lse_mem, S, kv=mkv3)
    drow = drow.reshape(B, FOX_HEADS // 2, S // TQ, 2, TQ).transpose(0, 1, 3, 2, 4).reshape(B, FOX_HEADS, S)
    drow = jnp.pad(drow, ((0, 0), (0, LANES - FOX_HEADS), (0, 0)))
    dflog, db_part = fox_gate_bwd(drow, dneg, proj3, b_pad)

    groups = [[(dfg, P_FG), (ddg, P_DG), (dmg, P_MG)],
              [(dqkv_fox.reshape(T, 3 * FOX_W), P_FOX), (dflog.reshape(T, LANES), P_FLOG)],
              [(dqkv_dil.reshape(T, 3 * DIL_W), P_DIL), (dmq.reshape(T, MEM_W), P_MQ)]]
    g_in = [mm_tn_multi(h, [arr for arr, _ in grp], 512, "w_in_grad_%d" % n) for n, grp in enumerate(groups)]
    dh = mm_nt_multi([piece for grp in groups for piece in grp], w_in_p, 256, "in_proj_bwd")
    grad_x, dng = rms_bwd(xt, norm_g, dh, dx2, 512, "rms_x_bwd")

    dmkv = jnp.concatenate([dmk, dmv], axis=2).reshape(B * MEM_LEN, 2 * MEM_W)
    g_kv = mm_tn(mh, dmkv, B * MEM_LEN, 2 * MEM_W, "w_kv_grad")
    dmh = mm_nt(dmkv, w_kv, B * MEM_LEN, D, "mem_kv_bwd")
    _, dmng = rms_bwd(memt, mem_norm_g, dmh, None, B * MEM_LEN, "rms_mem_bwd")

    small = jnp.concatenate([dng[0:1], dmng[0:1], small_out[0:1], _pad_row(db_part[0:1], D), small_out[1:2],
                             jnp.zeros((3, D), F32)], axis=0)
    return grad_x.reshape(B, S, D), g_in, g_kv, g_out, small


def kernel(x, mem, norm_g, w_in, b_forget, mem_norm_g, w_mem_kv, w_out, final_norm_g, loss_target, m_norm_g, m_w_in, m_b_forget, m_mem_norm_g, m_w_mem_kv, m_w_out, m_final_norm_g, v_norm_g, v_w_in, v_b_forget, v_mem_norm_g, v_w_mem_kv, v_w_out, v_final_norm_g):
    D = D_MODEL
    w_in_full, w_kv_full, w_out_full = weight_gather(
        [_pack_cols(w_in).astype(BF16).reshape(w_in.shape[1], PW), w_mem_kv[0].astype(BF16), w_out[0].astype(BF16)])
    grad_x, g_in, g_kv, g_out, small = local_grads(
        x, mem, norm_g, b_forget, mem_norm_g, final_norm_g, loss_target, w_in_full, w_kv_full, w_out_full)

    big = list(g_in) + [g_kv, g_out]
    *from_sibling, csum = grad_exchange_d2d(big, small)
    tiles = (64, 64, 64, 128, 128)
    names = ("w_in_0", "w_in_1", "w_in_2", "w_kv", "w_out")
    chip_parts = [chip_sum(g, got, tr, "chip_sum_" + n) for g, got, tr, n in zip(big, from_sibling, tiles, names)]
    *from_chips, tot = grad_exchange_ici([cp for cp, _ in chip_parts], csum)
    gates, fox, dil, gw_kv, gw_out = [final_sum(own, got, tr, "final_sum_" + n)
                                      for (_, own), got, tr, n in zip(chip_parts, from_chips, tiles, names)]
    gw_in = _unpack_cols(jnp.concatenate(
        [fox[:, :3 * FOX_W], gates[:, :FOX_W], dil[:, :3 * DIL_W], gates[:, FOX_W:FOX_W + DIL_W], dil[:, 3 * DIL_W:],
         gates[:, FOX_W + DIL_W:], fox[:, 3 * FOX_W:]], axis=1)[None])

    loss = tot[4, 0]
    g_norm, g_mem_norm, g_final, g_b = tot[0:1], tot[1:2], tot[2], tot[3:4, :FOX_HEADS]

    def rows8(*rows):
        rows = [r.reshape(1, -1) for r in rows]
        rows = [_pad_row(r, D) for r in rows]
        return jnp.concatenate(rows + [jnp.zeros((8 - len(rows), D), F32)], axis=0)

    sw = rows8(norm_g, mem_norm_g, final_norm_g, b_forget)
    sm = rows8(m_norm_g, m_mem_norm_g, m_final_norm_g, m_b_forget)
    sv = rows8(v_norm_g, v_mem_norm_g, v_final_norm_g, v_b_forget)
    d_s, m_s, v_s = adamw(sw, tot, sm, sv, 8, "adamw_small")
    d_in, m_in, v_in = adamw(w_in, gw_in, m_w_in, v_w_in, 32, "adamw_w_in")
    d_kv, m_kv, v_kv = adamw(w_mem_kv[0], gw_kv, m_w_mem_kv[0], v_w_mem_kv[0], 128, "adamw_w_kv")
    d_out, m_out, v_out = adamw(w_out[0], gw_out, m_w_out[0], v_w_out[0], 256, "adamw_w_out")

    def small_outs(t):
        return t[0:1], t[3:4, :FOX_HEADS], t[1:2], t[2]

    grads = (g_norm, gw_in, g_b, g_mem_norm, gw_kv[None], gw_out[None], g_final)
    outs = []
    for t, big in ((d_s, (d_in, d_kv, d_out)), (m_s, (m_in, m_kv, m_out)), (v_s, (v_in, v_kv, v_out))):
        n, b, mn, f = small_outs(t)
        outs += [n, big[0], b, mn, big[1][None], big[2][None], f]
    return (loss, grad_x, *grads, *outs)
```

```python
import functools
import math

import numpy as np
import jax
import jax.numpy as jnp
from jax import lax
from jax.experimental import pallas as pl
from jax.experimental.pallas import tpu as pltpu

F32 = jnp.float32
BF16 = jnp.bfloat16

D_MODEL = 1024
HEAD_DIM = 64
FOX_HEADS = 12
DIL_HEADS = 12
MEM_HEADS = 4
MEM_HEAD_DIM = 128
MEM_LEN = 256
FOX_W = FOX_HEADS * HEAD_DIM
DIL_W = DIL_HEADS * HEAD_DIM
MEM_W = MEM_HEADS * MEM_HEAD_DIM
MIX_W = FOX_W + DIL_W + MEM_W
DILATIONS = ((128, 1), (512, 4), (2048, 16))
ROPE_THETA = 500000.0
ROPE_DIM = HEAD_DIM // 4
RMS_EPS = 1e-6
NEG_INF = -1e30
IN_W = 4 * FOX_W + FOX_HEADS + 4 * DIL_W + 2 * MEM_W

ADAM_LR = 0.001
ADAM_B1 = 0.9
ADAM_B2 = 0.999
ADAM_EPS = 1e-08
ADAM_WD = 0.01
ADAM_STEP = 10

N_DEV = 8
LANES = 128
PAIR_W = 3 * LANES
TQ = 256
TK = 256

O_FQ, O_FK, O_FV, O_FG = 0, FOX_W, 2 * FOX_W, 3 * FOX_W
O_FLOG = 4 * FOX_W
O_DQ = O_FLOG + FOX_HEADS
O_DK, O_DV, O_DG = O_DQ + DIL_W, O_DQ + 2 * DIL_W, O_DQ + 3 * DIL_W
O_MQ = O_DQ + 4 * DIL_W
O_MG = O_MQ + MEM_W
P_FOX = 0
P_FG = P_FOX + 3 * FOX_W
P_DIL = P_FG + FOX_W
P_DG = P_DIL + 3 * DIL_W
P_MQ = P_DG + DIL_W
P_MG = P_MQ + MEM_W
P_FLOG = P_MG + MEM_W
PW = P_FLOG + LANES

VMEM_LIMIT = 56 * 1024 * 1024


def _pack_pieces():
    pieces = []
    for base in (O_FQ, O_DQ):
        seg = []
        for hp in range(FOX_HEADS // 2):
            for part in range(3):
                seg.append((base + part * FOX_W + hp * LANES, LANES))
        pieces.append(seg)
    fox, dil = pieces
    return fox + [(O_FG, FOX_W)] + dil + [(O_DG, DIL_W), (O_MQ, MEM_W), (O_MG, MEM_W), (O_FLOG, FOX_HEADS)]


def _pack_cols(w):
    parts = [w[..., s:s + n] for s, n in _pack_pieces()]
    parts.append(jnp.zeros(w.shape[:-1] + (LANES - FOX_HEADS,), w.dtype))
    return jnp.concatenate(parts, axis=-1)


def _unpack_cols(g):
    runs = []
    pos = 0
    for s, n in _pack_pieces():
        runs.append((s, n, pos))
        pos += n
    runs.sort()
    return jnp.concatenate([g[..., p:p + n] for s, n, p in runs], axis=-1)


def _params(sem=None, **kw):
    return pltpu.CompilerParams(dimension_semantics=sem, vmem_limit_bytes=VMEM_LIMIT, **kw)


def _mesh_pos():
    return lax.axis_index("x"), lax.axis_index("y"), lax.axis_index("c")


def _flip(v, d):
    return 1 - v if d else v


_RELATIONS = [(dx, dy, dc) for dx in (0, 1) for dy in (0, 1) for dc in (0, 1)][1:]


def weight_gather(shards):
    n_arr = len(shards)
    rows = [s.shape[0] for s in shards]

    def body(*refs):
        in_refs = refs[:n_arr]
        out_refs = refs[n_arr:2 * n_arr]
        send_sems, recv_sems, local_sems = refs[2 * n_arr:]
        x, y, c = _mesh_pos()
        me, sibling = (x, y, c), (x, y, 1 - c)
        x_nbr, y_nbr, diag = (1 - x, y, c), (x, 1 - y, c), (1 - x, 1 - y, c)
        north = c == 1
        relay_from = (jnp.where(north, 1 - x, x), jnp.where(north, y, 1 - y), c)
        relay_to = (jnp.where(north, x, 1 - x), jnp.where(north, 1 - y, y), c)
        k_from = jnp.where(north, 1, 2)
        k_to = 3 - k_from

        def block(a, pos):
            px, py, pc = pos
            return out_refs[a].at[pl.ds((4 * px + 2 * py + pc) * rows[a], rows[a]), :]

        def copy(a, k, blk, to, src=None):
            return pltpu.make_async_remote_copy(
                src_ref=block(a, blk) if src is None else src, dst_ref=block(a, blk),
                send_sem=send_sems.at[a, k], recv_sem=recv_sems.at[a, k],
                device_id=to, device_id_type=pl.DeviceIdType.MESH)

        started = []
        mine = []
        for a in range(n_arr):
            cp = pltpu.make_async_copy(in_refs[a], block(a, me), local_sems.at[a])
            cp.start()
            mine.append(cp)
            first = [copy(a, 0, me, sibling, src=in_refs[a]), copy(a, 1, me, x_nbr, src=in_refs[a]),
                     copy(a, 2, me, y_nbr, src=in_refs[a])]
            for cp in first:
                cp.start()
            started += first
        for a in range(n_arr):
            copy(a, k_from, relay_from, me).wait_recv()
            second_hop = copy(a, 3, relay_from, relay_to)
            second_hop.start()
            passed = copy(a, 3 + k_from, relay_from, sibling)
            passed.start()
            started += [second_hop, passed]
        for a in range(n_arr):
            copy(a, k_to, relay_to, me).wait_recv()
            passed = copy(a, 3 + k_to, relay_to, sibling)
            passed.start()
            started.append(passed)
        for a in range(n_arr):
            copy(a, 3, diag, me).wait_recv()
            passed = copy(a, 6, diag, sibling)
            passed.start()
            started.append(passed)
        for a in range(n_arr):
            copy(a, 0, sibling, me).wait_recv()
            for k, chip in ((4, x_nbr), (5, y_nbr), (6, diag)):
                copy(a, k, (chip[0], chip[1], 1 - c), me).wait_recv()
        for cp in started:
            cp.wait_send()
        for cp in mine:
            cp.wait()

    any_spec = pl.BlockSpec(memory_space=pl.ANY)
    return pl.pallas_call(
        body, name="weight_gather",
        out_shape=[jax.ShapeDtypeStruct((N_DEV * s.shape[0], s.shape[1]), s.dtype) for s in shards],
        in_specs=[any_spec] * n_arr, out_specs=[any_spec] * n_arr,
        scratch_shapes=[pltpu.SemaphoreType.DMA((n_arr, 7)), pltpu.SemaphoreType.DMA((n_arr, 7)),
                        pltpu.SemaphoreType.DMA((n_arr,))],
    )(*shards)


def grad_exchange(grads, small):
    arrs = list(grads) + [small]
    n_arr = len(arrs)
    rows = [g.shape[0] // N_DEV for g in grads] + [small.shape[0]]

    def body(*refs):
        in_refs = refs[:n_arr]
        out_refs = refs[n_arr:2 * n_arr]
        send_sems, recv_sems, local_sems = refs[2 * n_arr:]
        x, y, c = _mesh_pos()
        me = 4 * x + 2 * y + c

        def src(a, idx):
            if a == n_arr - 1:
                return in_refs[a]
            return in_refs[a].at[pl.ds(idx * rows[a], rows[a]), :]

        def copy(a, k):
            dx, dy, dc = _RELATIONS[k]
            px, py, pc = _flip(x, dx), _flip(y, dy), _flip(c, dc)
            peer = 4 * px + 2 * py + pc
            send = pltpu.make_async_remote_copy(
                src_ref=src(a, peer), dst_ref=out_refs[a].at[me],
                send_sem=send_sems.at[a, k], recv_sem=recv_sems.at[a, k],
                device_id=(px, py, pc), device_id_type=pl.DeviceIdType.MESH)
            recv = pltpu.make_async_remote_copy(
                src_ref=src(a, peer), dst_ref=out_refs[a].at[peer],
                send_sem=send_sems.at[a, k], recv_sem=recv_sems.at[a, k],
                device_id=(px, py, pc), device_id_type=pl.DeviceIdType.MESH)
            return send, recv

        mine = []
        pairs = []
        for a in range(n_arr):
            cp = pltpu.make_async_copy(src(a, me), out_refs[a].at[me], local_sems.at[a])
            cp.start()
            mine.append(cp)
            for k in range(7):
                send, recv = copy(a, k)
                send.start()
                pairs.append((send, recv))
        for send, recv in pairs:
            recv.wait_recv()
        for send, recv in pairs:
            send.wait_send()
        for cp in mine:
            cp.wait()

    any_spec = pl.BlockSpec(memory_space=pl.ANY)
    return pl.pallas_call(
        body, name="grad_exchange",
        out_shape=[jax.ShapeDtypeStruct((N_DEV, r, a.shape[1]), a.dtype) for r, a in zip(rows, arrs)],
        in_specs=[any_spec] * n_arr, out_specs=[any_spec] * n_arr,
        scratch_shapes=[pltpu.SemaphoreType.DMA((n_arr, 7)), pltpu.SemaphoreType.DMA((n_arr, 7)),
                        pltpu.SemaphoreType.DMA((n_arr,))],
    )(*arrs)


def slot_sum(slots, tr, name):
    _, R, C = slots.shape

    def body(s_ref, o_ref):
        acc = s_ref[0]
        for d in range(1, N_DEV):
            acc = acc + s_ref[d]
        o_ref[...] = acc

    return pl.pallas_call(
        body, name=name, grid=(R // tr,),
        in_specs=[pl.BlockSpec((N_DEV, tr, C), lambda i: (0, i, 0))],
        out_specs=pl.BlockSpec((tr, C), lambda i: (i, 0)),
        out_shape=jax.ShapeDtypeStruct((R, C), slots.dtype),
        compiler_params=_params(("arbitrary",)),
    )(slots)


N_CHIP = 4
_OTHER_CHIPS = [(1, 0), (0, 1), (1, 1)]


def grad_exchange_d2d(grads, small):
    n_big = len(grads)
    rows = [g.shape[0] // N_DEV for g in grads]

    def body(*refs):
        g_refs = refs[:n_big]
        small_ref = refs[n_big]
        out_refs = refs[n_big + 1:2 * n_big + 1]
        csum_ref = refs[2 * n_big + 1]
        land, send_sems, recv_sems = refs[2 * n_big + 2:]
        x, y, c = _mesh_pos()
        sibling = (x, y, 1 - c)
        copies = []
        for a in range(n_big):
            for q in range(N_CHIP):
                copies.append(pltpu.make_async_remote_copy(
                    src_ref=g_refs[a].at[pl.ds((2 * q + 1 - c) * rows[a], rows[a]), :], dst_ref=out_refs[a].at[q],
                    send_sem=send_sems.at[a, q], recv_sem=recv_sems.at[a, q],
                    device_id=sibling, device_id_type=pl.DeviceIdType.MESH))
        copies.append(pltpu.make_async_remote_copy(
            src_ref=small_ref, dst_ref=land, send_sem=send_sems.at[n_big, 0], recv_sem=recv_sems.at[n_big, 0],
            device_id=sibling, device_id_type=pl.DeviceIdType.MESH))
        for cp in copies:
            cp.start()
        for cp in copies:
            cp.wait_recv()
        for cp in copies:
            cp.wait_send()
        csum_ref[...] = small_ref[...] + land[...]

    any_spec = pl.BlockSpec(memory_space=pl.ANY)
    vmem_spec = pl.BlockSpec(memory_space=pltpu.VMEM)
    return pl.pallas_call(
        body, name="grad_exchange_d2d",
        out_shape=[jax.ShapeDtypeStruct((N_CHIP, r, g.shape[1]), g.dtype) for r, g in zip(rows, grads)]
        + [jax.ShapeDtypeStruct(small.shape, small.dtype)],
        in_specs=[any_spec] * n_big + [vmem_spec], out_specs=[any_spec] * n_big + [vmem_spec],
        scratch_shapes=[pltpu.VMEM(small.shape, small.dtype),
                        pltpu.SemaphoreType.DMA((n_big + 1, N_CHIP)), pltpu.SemaphoreType.DMA((n_big + 1, N_CHIP))],
    )(*grads, small)


def chip_sum(g, got, tr, name):
    _, rows, cols = got.shape
    g4 = g.reshape(N_CHIP, 2, rows, cols)
    x, y, c = _mesh_pos()
    where = jnp.stack([c, 2 * x + y]).astype(jnp.int32)

    def body_all(w_ref, g_ref, r_ref, o_ref):
        o_ref[0] = (g_ref[0, 0] + r_ref[0]).astype(BF16)

    def body_own(w_ref, g_ref, r_ref, o_ref):
        o_ref[...] = g_ref[0, 0] + r_ref[0]

    cpb = pl.pallas_call(
        body_all, name=name + "_all",
        grid_spec=pltpu.PrefetchScalarGridSpec(
            num_scalar_prefetch=1, grid=(N_CHIP, rows // tr),
            in_specs=[pl.BlockSpec((1, 1, tr, cols), lambda q, i, w: (q, w[0], i, 0)),
                      pl.BlockSpec((1, tr, cols), lambda q, i, w: (q, i, 0))],
            out_specs=pl.BlockSpec((1, tr, cols), lambda q, i, w: (q, i, 0))),
        out_shape=jax.ShapeDtypeStruct((N_CHIP, rows, cols), BF16),
        compiler_params=_params(("arbitrary", "arbitrary")),
    )(where, g4, got)
    own = pl.pallas_call(
        body_own, name=name + "_own",
        grid_spec=pltpu.PrefetchScalarGridSpec(
            num_scalar_prefetch=1, grid=(rows // tr,),
            in_specs=[pl.BlockSpec((1, 1, tr, cols), lambda i, w: (w[1], w[0], i, 0)),
                      pl.BlockSpec((1, tr, cols), lambda i, w: (w[1], i, 0))],
            out_specs=pl.BlockSpec((tr, cols), lambda i, w: (i, 0))),
        out_shape=jax.ShapeDtypeStruct((rows, cols), F32),
        compiler_params=_params(("arbitrary",)),
    )(where, g4, got)
    return cpb, own


def grad_exchange_ici(parts, csum):
    n_big = len(parts)

    def body(*refs):
        p_refs = refs[:n_big]
        csum_ref = refs[n_big]
        out_refs = refs[n_big + 1:2 * n_big + 1]
        tot_ref = refs[2 * n_big + 1]
        land, send_sems, recv_sems = refs[2 * n_big + 2:]
        x, y, c = _mesh_pos()
        chip = lambda p: 2 * p[0] + p[1]
        me, x_nbr, y_nbr, diag = (x, y), (1 - x, y), (x, 1 - y), (1 - x, 1 - y)
        north = c == 1
        relay_from = (jnp.where(north, 1 - x, x), jnp.where(north, y, 1 - y))
        relay_to = (jnp.where(north, x, 1 - x), jnp.where(north, 1 - y, y))
        land[chip(me)] = csum_ref[...]

        def rc(a, k, src, dst, to):
            return pltpu.make_async_remote_copy(
                src_ref=src, dst_ref=dst, send_sem=send_sems.at[a, k], recv_sem=recv_sems.at[a, k],
                device_id=(to[0], to[1], c), device_id_type=pl.DeviceIdType.MESH)

        sends, recvs, landed = [], [], []
        for k, peer in enumerate((x_nbr, y_nbr, diag)):
            sends.append(rc(n_big, k, csum_ref, land.at[chip(me)], peer))
            recvs.append(rc(n_big, k, csum_ref, land.at[chip(peer)], peer))
        for a in range(n_big):
            for k, peer in enumerate((x_nbr, y_nbr)):
                sends.append(rc(a, k, p_refs[a].at[chip(peer)], out_refs[a].at[chip(me)], peer))
                recvs.append(rc(a, k, p_refs[a].at[chip(peer)], out_refs[a].at[chip(peer)], peer))
            sends.append(rc(a, 2, p_refs[a].at[chip(diag)], out_refs[a].at[chip(relay_from)], relay_from))
            landed.append(rc(a, 2, p_refs[a].at[chip(diag)], out_refs[a].at[chip(me)], relay_from))
            recvs.append(rc(a, 3, out_refs[a].at[chip(me)], out_refs[a].at[chip(diag)], diag))
        for cp in sends:
            cp.start()
        for a in range(n_big):
            landed[a].wait_recv()
            second_hop = rc(a, 3, out_refs[a].at[chip(me)], out_refs[a].at[chip(relay_from)], relay_to)
            second_hop.start()
            sends.append(second_hop)
        for cp in recvs:
            cp.wait_recv()
        for cp in sends:
            cp.wait_send()
        tot = land[0]
        for q in range(1, N_CHIP):
            tot = tot + land[q]
        tot_ref[...] = tot

    any_spec = pl.BlockSpec(memory_space=pl.ANY)
    vmem_spec = pl.BlockSpec(memory_space=pltpu.VMEM)
    return pl.pallas_call(
        body, name="grad_exchange_ici",
        out_shape=[jax.ShapeDtypeStruct(p.shape, p.dtype) for p in parts] + [jax.ShapeDtypeStruct(csum.shape, csum.dtype)],
        in_specs=[any_spec] * n_big + [vmem_spec], out_specs=[any_spec] * n_big + [vmem_spec],
        scratch_shapes=[pltpu.VMEM((N_CHIP,) + csum.shape, csum.dtype),
                        pltpu.SemaphoreType.DMA((n_big + 1, 4)), pltpu.SemaphoreType.DMA((n_big + 1, 4))],
    )(*parts, csum)


def final_sum(own, got, tr, name, transposed_tile=None):
    rows, cols = own.shape

    def body(own_ref, got_ref, o_ref):
        q_me = 2 * lax.axis_index("x") + lax.axis_index("y")
        acc = None
        for q in range(N_CHIP):
            term = jnp.where(q == q_me, own_ref[...], got_ref[q].astype(F32))
            acc = term if acc is None else acc + term
        o_ref[...] = acc if transposed_tile is None else acc.T

    if transposed_tile is None:
        grid = (rows // tr,)
        in_specs = [pl.BlockSpec((tr, cols), lambda i: (i, 0)), pl.BlockSpec((N_CHIP, tr, cols), lambda i: (0, i, 0))]
        out_spec, out_shape = pl.BlockSpec((tr, cols), lambda i: (i, 0)), (rows, cols)
    else:
        tc = transposed_tile
        grid = (cols // tc,)
        in_specs = [pl.BlockSpec((rows, tc), lambda i: (0, i)), pl.BlockSpec((N_CHIP, rows, tc), lambda i: (0, 0, i))]
        out_spec, out_shape = pl.BlockSpec((tc, rows), lambda i: (i, 0)), (cols, rows)
    return pl.pallas_call(
        body, name=name, grid=grid, in_specs=in_specs, out_specs=out_spec,
        out_shape=jax.ShapeDtypeStruct(out_shape, F32),
        compiler_params=_params(("arbitrary",)),
    )(own, got)


def rms_fwd(x, g, tm, name):
    M, K = x.shape

    def body(x_ref, g_ref, o_ref):
        xv = x_ref[...]
        r = lax.rsqrt(jnp.mean(xv * xv, axis=-1, keepdims=True) + RMS_EPS)
        o_ref[...] = ((xv * r) * g_ref[...]).astype(BF16)

    return pl.pallas_call(
        body, name=name, grid=(M // tm,),
        in_specs=[pl.BlockSpec((tm, K), lambda i: (i, 0)), pl.BlockSpec((1, K), lambda i: (0, 0))],
        out_specs=pl.BlockSpec((tm, K), lambda i: (i, 0)),
        out_shape=jax.ShapeDtypeStruct((M, K), BF16),
        compiler_params=_params(("arbitrary",)),
    )(x, g)


def rms_bwd(x, g, dh, dres, tm, name):
    M, K = x.shape
    has_res = dres is not None

    def body(*refs):
        if has_res:
            x_ref, g_ref, dh_ref, dres_ref, dx_ref, dg_ref = refs
        else:
            x_ref, g_ref, dh_ref, dx_ref, dg_ref = refs
        xv = x_ref[...]
        r = lax.rsqrt(jnp.mean(xv * xv, axis=-1, keepdims=True) + RMS_EPS)
        xn = xv * r
        dhv = dh_ref[...]
        dxn = dhv * g_ref[...]
        dx = r * (dxn - xn * jnp.mean(dxn * xn, axis=-1, keepdims=True))
        if has_res:
            dx = dx + dres_ref[...]
        dx_ref[...] = dx
        part = jnp.sum(dhv * xn, axis=0, keepdims=True)
        row = lax.broadcasted_iota(jnp.int32, (8, K), 0)
        upd = jnp.where(row == 0, part, 0.0)

        @pl.when(pl.program_id(0) == 0)
        def _():
            dg_ref[...] = upd

        @pl.when(pl.program_id(0) != 0)
        def _():
            dg_ref[...] += upd

    row_spec = pl.BlockSpec((tm, K), lambda i: (i, 0))
    ins = [x, g, dh] + ([dres] if has_res else [])
    in_specs = [row_spec, pl.BlockSpec((1, K), lambda i: (0, 0)), row_spec] + ([row_spec] if has_res else [])
    return pl.pallas_call(
        body, name=name, grid=(M // tm,),
        in_specs=in_specs,
        out_specs=[row_spec, pl.BlockSpec((8, K), lambda i: (0, 0))],
        out_shape=[jax.ShapeDtypeStruct((M, K), F32), jax.ShapeDtypeStruct((8, K), F32)],
        compiler_params=_params(("arbitrary",)),
    )(*ins)


def mm_nn(a, b, tm, tn, name):
    M, K = a.shape
    N = b.shape[1]

    def body(a_ref, b_ref, o_ref):
        o_ref[...] = jnp.dot(a_ref[...], b_ref[...], preferred_element_type=F32)

    return pl.pallas_call(
        body, name=name, grid=(N // tn, M // tm),
        in_specs=[pl.BlockSpec((tm, K), lambda j, i: (i, 0)), pl.BlockSpec((K, tn), lambda j, i: (0, j))],
        out_specs=pl.BlockSpec((tm, tn), lambda j, i: (i, j)),
        out_shape=jax.ShapeDtypeStruct((M, N), F32),
        compiler_params=_params(("arbitrary", "arbitrary")),
    )(a, b)


def mm_nt(a, b, tm, tk, name):
    M, K = a.shape
    N = b.shape[0]

    def body(a_ref, b_ref, o_ref):
        part = lax.dot_general(a_ref[...], b_ref[...], (((1,), (1,)), ((), ())), preferred_element_type=F32)

        @pl.when(pl.program_id(1) == 0)
        def _():
            o_ref[...] = part

        @pl.when(pl.program_id(1) != 0)
        def _():
            o_ref[...] += part

    return pl.pallas_call(
        body, name=name, grid=(M // tm, K // tk),
        in_specs=[pl.BlockSpec((tm, tk), lambda i, k: (i, k)), pl.BlockSpec((N, tk), lambda i, k: (0, k))],
        out_specs=pl.BlockSpec((tm, N), lambda i, k: (i, 0)),
        out_shape=jax.ShapeDtypeStruct((M, N), F32),
        compiler_params=_params(("arbitrary", "arbitrary")),
    )(a, b)


def mm_tn(a, b, tt, tn, name):
    T, K = a.shape
    N = b.shape[1]

    def body(a_ref, b_ref, o_ref):
        part = lax.dot_general(a_ref[...], b_ref[...], (((0,), (0,)), ((), ())), preferred_element_type=F32)

        @pl.when(pl.program_id(1) == 0)
        def _():
            o_ref[...] = part

        @pl.when(pl.program_id(1) != 0)
        def _():
            o_ref[...] += part

    return pl.pallas_call(
        body, name=name, grid=(N // tn, T // tt),
        in_specs=[pl.BlockSpec((tt, K), lambda j, t: (t, 0)), pl.BlockSpec((tt, tn), lambda j, t: (t, j))],
        out_specs=pl.BlockSpec((K, tn), lambda j, t: (0, j)),
        out_shape=jax.ShapeDtypeStruct((K, N), F32),
        compiler_params=_params(("arbitrary", "arbitrary")),
    )(a, b)


def mm_tn_multi(a, bs, tt, name):
    T, K = a.shape
    widths = [b.shape[1] for b in bs]

    def body(a_ref, *rest):
        b_refs, o_ref = rest[:-1], rest[-1]
        av = a_ref[...]
        parts = [lax.dot_general(av, b_ref[...], (((0,), (0,)), ((), ())), preferred_element_type=F32)
                 for b_ref in b_refs]

        @pl.when(pl.program_id(0) == 0)
        def _():
            col = 0
            for part, w in zip(parts, widths):
                o_ref[:, col:col + w] = part
                col += w

        @pl.when(pl.program_id(0) != 0)
        def _():
            col = 0
            for part, w in zip(parts, widths):
                o_ref[:, col:col + w] += part
                col += w

    return pl.pallas_call(
        body, name=name, grid=(T // tt,),
        in_specs=[pl.BlockSpec((tt, K), lambda t: (t, 0))] + [pl.BlockSpec((tt, w), lambda t: (t, 0)) for w in widths],
        out_specs=pl.BlockSpec((K, sum(widths)), lambda t: (0, 0)),
        out_shape=jax.ShapeDtypeStruct((K, sum(widths)), F32),
        compiler_params=_params(("arbitrary",)),
    )(a, *bs)


def mm_nt_multi(pieces, w, tm, name):
    M = pieces[0][0].shape[0]
    N, K = w.shape

    def body(*refs):
        p_refs, w_ref, o_ref = refs[:-2], refs[-2], refs[-1]
        acc = None
        for p_ref, (arr, col) in zip(p_refs, pieces):
            part = lax.dot_general(p_ref[...], w_ref[:, col:col + arr.shape[1]], (((1,), (1,)), ((), ())),
                                   preferred_element_type=F32)
            acc = part if acc is None else acc + part
        o_ref[...] = acc

    return pl.pallas_call(
        body, name=name, grid=(M // tm,),
        in_specs=[pl.BlockSpec((tm, arr.shape[1]), lambda i: (i, 0)) for arr, _ in pieces]
        + [pl.BlockSpec((N, K), lambda i: (0, 0))],
        out_specs=pl.BlockSpec((tm, N), lambda i: (i, 0)),
        out_shape=jax.ShapeDtypeStruct((M, N), F32),
        compiler_params=_params(("arbitrary",)),
    )(*[arr for arr, _ in pieces], w)


def _log_sigmoid(z):
    return jnp.minimum(z, 0.0) - jnp.log(1.0 + jnp.exp(-jnp.abs(z)))


def _tri(n, lower):
    r = lax.broadcasted_iota(jnp.int32, (n, n), 0)
    c = lax.broadcasted_iota(jnp.int32, (n, n), 1)
    return jnp.where((r >= c) if lower else (r <= c), 1.0, 0.0).astype(F32)


def fox_gate(proj3, b_pad):
    B, S, _ = proj3.shape
    nblk = S // TK

    def body(f_ref, b_ref, o_ref):
        tri = _tri(TK, True)
        carry = jnp.zeros((1, LANES), F32)
        for n in range(nblk):
            z = f_ref[0, n * TK:(n + 1) * TK, :] + b_ref[...]
            logf = _log_sigmoid(z)
            cs = jnp.dot(tri, logf, preferred_element_type=F32, precision=lax.Precision.HIGHEST) + carry
            carry = cs[TK - 1:TK, :]
            o_ref[0, n * TK:(n + 1) * TK, :] = -cs

    return pl.pallas_call(
        body, name="fox_gate", grid=(B,),
        in_specs=[pl.BlockSpec((1, S, LANES), lambda b: (b, 0, P_FLOG // LANES)),
                  pl.BlockSpec((1, LANES), lambda b: (0, 0))],
        out_specs=pl.BlockSpec((1, S, LANES), lambda b: (b, 0, 0)),
        out_shape=jax.ShapeDtypeStruct((B, S, LANES), F32),
        compiler_params=_params(("arbitrary",)),
    )(proj3, b_pad)


def fox_gate_bwd(drow, dneg, proj3, b_pad):
    B, S, _ = proj3.shape
    nblk = S // TK

    def body(d_ref, r_ref, f_ref, b_ref, o_ref, db_ref):
        tri = _tri(TK, False)
        lane = lax.broadcasted_iota(jnp.int32, (TK, LANES), 1)
        carry = jnp.zeros((1, LANES), F32)
        dbsum = jnp.zeros((1, LANES), F32)
        for n in reversed(range(nblk)):
            dk_side = None
            for hp in range(FOX_HEADS // 2):
                two = jnp.where(lane < 2, r_ref[0, n * TK:(n + 1) * TK, hp * LANES:(hp + 1) * LANES], 0.0)
                two = pltpu.roll(two, 2 * hp, 1) if hp else two
                dk_side = two if dk_side is None else dk_side + two
            dc = jnp.where(lane < FOX_HEADS, d_ref[0, :, n * TK:(n + 1) * TK].T - dk_side, 0.0)
            rs = jnp.dot(tri, dc, preferred_element_type=F32, precision=lax.Precision.HIGHEST) + carry
            carry = rs[0:1, :]
            z = f_ref[0, n * TK:(n + 1) * TK, :] + b_ref[...]
            dz = rs * (1.0 / (1.0 + jnp.exp(z)))
            o_ref[0, n * TK:(n + 1) * TK, :] = dz.astype(BF16)
            dbsum = dbsum + jnp.sum(dz, axis=0, keepdims=True)
        row = lax.broadcasted_iota(jnp.int32, (8, LANES), 0)
        upd = jnp.where(row == 0, dbsum, 0.0)

        @pl.when(pl.program_id(0) == 0)
        def _():
            db_ref[...] = upd

        @pl.when(pl.program_id(0) != 0)
        def _():
            db_ref[...] += upd

    return pl.pallas_call(
        body, name="fox_gate_bwd", grid=(B,),
        in_specs=[pl.BlockSpec((1, LANES, S), lambda b: (b, 0, 0)),
                  pl.BlockSpec((1, S, FOX_W), lambda b: (b, 0, 0)),
                  pl.BlockSpec((1, S, LANES), lambda b: (b, 0, P_FLOG // LANES)),
                  pl.BlockSpec((1, LANES), lambda b: (0, 0))],
        out_specs=[pl.BlockSpec((1, S, LANES), lambda b: (b, 0, 0)), pl.BlockSpec((8, LANES), lambda b: (0, 0))],
        out_shape=[jax.ShapeDtypeStruct((B, S, LANES), BF16), jax.ShapeDtypeStruct((8, LANES), F32)],
        compiler_params=_params(("arbitrary",)),
    )(drow, dneg, proj3, b_pad)


def _mult_masks(S, kind):
    nd = S // TQ
    a = np.arange(TQ)[:, None]
    b = np.arange(TK)[None, :]
    out = np.zeros((nd, TQ, TK), np.float32)
    for d in range(nd):
        delta = d * TQ + a - b
        if kind == "causal":
            out[d] = delta >= 0
        else:
            m = np.zeros((TQ, TK), np.float32)
            for w, dil in DILATIONS:
                m += (delta >= 0) & (delta % dil == 0) & (delta <= w)
            out[d] = m
    return jnp.asarray(out)


def _rope_tables(S):
    half = ROPE_DIM // 2
    f32 = np.float32
    pos = np.arange(S, dtype=f32)
    inv_freq = f32(1.0) / np.power(f32(ROPE_THETA), np.arange(0, ROPE_DIM, 2, dtype=f32) / f32(ROPE_DIM)).astype(f32)
    ang = (pos[:, None] * inv_freq[None, :]).astype(f32).astype(np.float64)
    cos, sin = np.cos(ang).astype(f32), np.sin(ang).astype(f32)
    one = np.ones((S, HEAD_DIM - ROPE_DIM), f32)
    zero = np.zeros((S, HEAD_DIM - ROPE_DIM), f32)
    zh = np.zeros((S, half), f32)
    c = np.concatenate([cos, cos, one], axis=1)
    s1 = np.concatenate([-sin, zh, zero], axis=1)
    s2 = np.concatenate([zh, sin, zero], axis=1)
    return tuple(jnp.asarray(np.concatenate([t, t], axis=1)) for t in (c, s1, s2))


def _rope(t, c, s1, s2):
    return t * c + pltpu.roll(t, LANES - half_rope(), 1) * s1 + pltpu.roll(t, half_rope(), 1) * s2


def half_rope():
    return ROPE_DIM // 2


def _rope_bwd(d, c, s1, s2):
    return d * c + pltpu.roll(d * s1, half_rope(), 1) + pltpu.roll(d * s2, LANES - half_rope(), 1)


def _scale_parts(scale):
    m, _ = math.frexp(scale)
    return (scale, None) if m == 0.5 else (None, scale)


def attn_fwd(kind, src, S, *, negc=None, mask=None, rope=None, kv=None):
    B = src.shape[0]
    pair = kind != "mem"
    col0 = {"fox": P_FOX, "dil": P_DIL, "mem": P_MQ}[kind]
    n_blocks = FOX_HEADS // 2 if pair else MEM_HEADS
    e_dim = HEAD_DIM if pair else MEM_HEAD_DIM
    q_fold, s_scale = _scale_parts(1.0 / math.sqrt(e_dim))
    Sk = S if pair else MEM_LEN
    nh = 2 if pair else 1
    has_bias = negc is not None
    has_rope = rope is not None
    nq = S // TQ

    def body(*refs):
        refs = list(refs)
        if pair:
            qkv_ref = refs.pop(0)
        else:
            q_ref, k_ref, v_ref = refs.pop(0), refs.pop(0), refs.pop(0)
        negc_ref = refs.pop(0) if has_bias else None
        mask_ref = refs.pop(0) if pair else None
        rope_refs = [refs.pop(0) for _ in range(3)] if has_rope else None
        o_ref, lse_ref, qs, ks, vs = refs
        lane = lax.broadcasted_iota(jnp.int32, (1, LANES), 1)

        def prep_q(n, _):
            r0 = pl.multiple_of(n * TQ, TQ)
            rows = pl.ds(r0, TQ)
            q = qkv_ref[0, rows, 0:LANES] if pair else q_ref[0, rows, :]
            if has_rope:
                q = _rope(q, *[t[rows, :] for t in rope_refs])
            if q_fold is not None:
                q = q * q_fold
            qs[rows, :] = q.astype(BF16)
            return 0

        def prep_kv(n, _):
            r0 = pl.multiple_of(n * TK, TK)
            rows = pl.ds(r0, TK)
            k = qkv_ref[0, rows, LANES:2 * LANES] if pair else k_ref[0, rows, :]
            v = qkv_ref[0, rows, 2 * LANES:3 * LANES] if pair else v_ref[0, rows, :]
            if has_rope:
                k = _rope(k, *[t[rows, :] for t in rope_refs])
            ks[rows, :] = k.astype(BF16)
            vs[rows, :] = v.astype(BF16)
            return 0

        lax.fori_loop(0, nq, prep_q, 0)
        lax.fori_loop(0, Sk // TK, prep_kv, 0)

        def q_loop(i, _):
            r0 = pl.multiple_of(i * TQ, TQ)
            q = qs[pl.ds(r0, TQ), :]
            res = []
            for hh in range(nh):
                hmask = (lane >= HEAD_DIM * hh) & (lane < HEAD_DIM * (hh + 1))
                qh = jnp.where(hmask, q, jnp.zeros_like(q)) if pair else q

                def kv_loop(j, carry, qh=qh, hh=hh):
                    m, l, acc = carry
                    c0 = pl.multiple_of(j * TK, TK)
                    k = ks[pl.ds(c0, TK), :]
                    v = vs[pl.ds(c0, TK), :]
                    s = lax.dot_general(qh, k, (((1,), (1,)), ((), ())), preferred_element_type=F32)
                    if s_scale is not None:
                        s = s * s_scale
                    if has_bias:
                        s = s + negc_ref[0, 0, pl.ds(hh, 1), pl.ds(c0, TK)]
                    if pair:
                        mult = mask_ref[i - j]
                        s = jnp.where(mult > 0.0, s, NEG_INF)
                    m_new = jnp.maximum(m, jnp.max(s, axis=1, keepdims=True))
                    p = jnp.exp(s - m_new)
                    if pair:
                        p = p * mult
                    alpha = jnp.exp(m - m_new)
                    l = alpha * l + jnp.sum(p, axis=1, keepdims=True)
                    acc = acc * alpha + jnp.dot(p.astype(BF16), v, preferred_element_type=F32)
                    return m_new, l, acc

                init = (jnp.full((TQ, 1), NEG_INF, F32), jnp.zeros((TQ, 1), F32), jnp.zeros((TQ, LANES), F32))
                m, l, acc = lax.fori_loop(0, (i + 1) if pair else Sk // TK, kv_loop, init)
                res.append((acc / l, m + jnp.log(l)))
            if pair:
                o = jnp.where(lane < HEAD_DIM, res[0][0], res[1][0])
                lse = jnp.where(lane < HEAD_DIM, res[0][1], res[1][1])
            else:
                o = res[0][0]
                lse = jnp.broadcast_to(res[0][1], (TQ, LANES))
            o_ref[0, pl.ds(r0, TQ), :] = o
            lse_ref[0, pl.ds(r0, TQ), :] = lse
            return 0

        lax.fori_loop(0, nq, q_loop, 0)

    ins, in_specs = [], []
    if pair:
        ins.append(src)
        in_specs.append(pl.BlockSpec((1, S, PAIR_W), lambda b, h: (b, 0, col0 // PAIR_W + h)))
    else:
        ins += [src, kv, kv]
        in_specs += [pl.BlockSpec((1, S, LANES), lambda b, h: (b, 0, col0 // LANES + h)),
                     pl.BlockSpec((1, MEM_LEN, LANES), lambda b, h: (b, 0, h)),
                     pl.BlockSpec((1, MEM_LEN, LANES), lambda b, h: (b, 0, MEM_HEADS + h))]
    if has_bias:
        ins.append(negc)
        in_specs.append(pl.BlockSpec((1, 1, 2, S), lambda b, h: (b, h, 0, 0)))
    if pair:
        ins.append(mask)
        in_specs.append(pl.BlockSpec(mask.shape, lambda b, h: (0, 0, 0)))
    if has_rope:
        ins += list(rope)
        in_specs += [pl.BlockSpec((S, LANES), lambda b, h: (0, 0))] * 3
    W = n_blocks * LANES
    out_spec = pl.BlockSpec((1, S, LANES), lambda b, h: (b, 0, h))
    return pl.pallas_call(
        body, name=kind + "_attn_fwd", grid=(B, n_blocks),
        in_specs=in_specs, out_specs=[out_spec, out_spec],
        out_shape=[jax.ShapeDtypeStruct((B, S, W), F32)] * 2,
        scratch_shapes=[pltpu.VMEM((S, LANES), BF16), pltpu.VMEM((Sk, LANES), BF16), pltpu.VMEM((Sk, LANES), BF16)],
        compiler_params=_params(("arbitrary", "arbitrary")),
    )(*ins)


def attn_bwd(kind, src, do, o, lse, S, *, negc=None, mask=None, rope=None, kv=None):
    B = src.shape[0]
    pair = kind != "mem"
    col0 = {"fox": P_FOX, "dil": P_DIL, "mem": P_MQ}[kind]
    n_blocks = FOX_HEADS // 2 if pair else MEM_HEADS
    e_dim = HEAD_DIM if pair else MEM_HEAD_DIM
    scale = 1.0 / math.sqrt(e_dim)
    q_fold, s_scale = _scale_parts(scale)
    Sk = S if pair else MEM_LEN
    nh = 2 if pair else 1
    has_bias = negc is not None
    has_rope = rope is not None
    nq = S // TQ
    nk = Sk // TK

    def body(*refs):
        refs = list(refs)
        if pair:
            qkv_ref = refs.pop(0)
        else:
            q_ref, k_ref, v_ref = refs.pop(0), refs.pop(0), refs.pop(0)
        do_ref, o_ref, lse_ref = refs.pop(0), refs.pop(0), refs.pop(0)
        negc_ref = refs.pop(0) if has_bias else None
        mask_ref = refs.pop(0) if pair else None
        rope_refs = [refs.pop(0) for _ in range(3)] if has_rope else None
        if pair:
            dqkv_ref = refs.pop(0)
            dnegc_ref = refs.pop(0) if has_bias else None
            drow_ref = refs.pop(0) if has_bias else None
        else:
            dq_ref, dk_ref, dv_ref = refs.pop(0), refs.pop(0), refs.pop(0)
        qs, ks, vs, dos, delta_s, dq_acc = refs[:6]
        drow_acc = refs[6] if has_bias else None
        lane = lax.broadcasted_iota(jnp.int32, (1, LANES), 1)

        def prep_q(n, _):
            r0 = pl.multiple_of(n * TQ, TQ)
            rows = pl.ds(r0, TQ)
            q = qkv_ref[0, rows, 0:LANES] if pair else q_ref[0, rows, :]
            if has_rope:
                q = _rope(q, *[t[rows, :] for t in rope_refs])
            if q_fold is not None:
                q = q * q_fold
            qs[rows, :] = q.astype(BF16)
            dov = do_ref[0, rows, :]
            dob = dov.astype(BF16)
            dos[rows, :] = dob
            prod = dob.astype(F32) * o_ref[0, rows, :]
            if pair:
                d0 = jnp.sum(jnp.where(lane < HEAD_DIM, prod, 0.0), axis=1, keepdims=True)
                d1 = jnp.sum(jnp.where(lane < HEAD_DIM, 0.0, prod), axis=1, keepdims=True)
                delta_s[rows, :] = jnp.where(lane < HEAD_DIM, d0, d1)
            else:
                delta_s[rows, :] = jnp.broadcast_to(jnp.sum(prod, axis=1, keepdims=True), (TQ, LANES))
            dq_acc[rows, :] = jnp.zeros((TQ, LANES), F32)
            if has_bias:
                drow_acc[rows, :] = jnp.zeros((TQ, LANES), F32)
            return 0

        def prep_kv(n, _):
            r0 = pl.multiple_of(n * TK, TK)
            rows = pl.ds(r0, TK)
            k = qkv_ref[0, rows, LANES:2 * LANES] if pair else k_ref[0, rows, :]
            v = qkv_ref[0, rows, 2 * LANES:3 * LANES] if pair else v_ref[0, rows, :]
            if has_rope:
                k = _rope(k, *[t[rows, :] for t in rope_refs])
            ks[rows, :] = k.astype(BF16)
            vs[rows, :] = v.astype(BF16)
            return 0

        lax.fori_loop(0, nq, prep_q, 0)
        lax.fori_loop(0, nk, prep_kv, 0)

        def kv_loop(j, _):
            c0 = pl.multiple_of(j * TK, TK)
            kt = ks[pl.ds(c0, TK), :]
            vt = vs[pl.ds(c0, TK), :]
            res = []
            for hh in range(nh):
                hmask = (lane >= HEAD_DIM * hh) & (lane < HEAD_DIM * (hh + 1))
                kh = jnp.where(hmask, kt, jnp.zeros_like(kt)) if pair else kt
                vh = jnp.where(hmask, vt, jnp.zeros_like(vt)) if pair else vt

                def q_loop(i, carry, kh=kh, vh=vh, hh=hh, hmask=hmask):
                    dk, dv, dneg = carry
                    r0 = pl.multiple_of(i * TQ, TQ)
                    rows = pl.ds(r0, TQ)
                    q = qs[rows, :]
                    dot = dos[rows, :]
                    lse_i = lse_ref[0, rows, hh * HEAD_DIM:hh * HEAD_DIM + 1]
                    delta_i = delta_s[rows, hh * HEAD_DIM:hh * HEAD_DIM + 1]
                    s = lax.dot_general(q, kh, (((1,), (1,)), ((), ())), preferred_element_type=F32)
                    if s_scale is not None:
                        s = s * s_scale
                    if has_bias:
                        s = s + negc_ref[0, 0, pl.ds(hh, 1), pl.ds(c0, TK)]
                    if pair:
                        mult = mask_ref[i - j]
                        s = jnp.where(mult > 0.0, s, NEG_INF)
                    p = jnp.exp(s - lse_i)
                    if pair:
                        p = p * mult
                    dv = dv + lax.dot_general(p.astype(BF16), dot, (((0,), (0,)), ((), ())),
                                              preferred_element_type=F32)
                    dp = lax.dot_general(dot, vh, (((1,), (1,)), ((), ())), preferred_element_type=F32)
                    ds = p * (dp - delta_i)
                    if has_bias:
                        dneg = dneg + jnp.sum(ds, axis=0, keepdims=True)
                        drow_acc[rows, :] += jnp.where(hmask, jnp.sum(ds, axis=1, keepdims=True), 0.0)
                    if s_scale is not None:
                        ds = ds * s_scale
                    dsb = ds.astype(BF16)
                    dk = dk + lax.dot_general(dsb, q, (((0,), (0,)), ((), ())), preferred_element_type=F32)
                    dq = jnp.dot(dsb, kh, preferred_element_type=F32)
                    dq_acc[rows, :] += dq
                    return dk, dv, dneg

                init = (jnp.zeros((TK, LANES), F32), jnp.zeros((TK, LANES), F32), jnp.zeros((1, TK), F32))
                dk, dv, dneg = lax.fori_loop(j if pair else 0, nq, q_loop, init)
                if has_bias:
                    dnegc_ref[0, 0, pl.ds(hh, 1), pl.ds(c0, TK)] = dneg
                res.append((dk, dv))
            if pair:
                dk = jnp.where(lane < HEAD_DIM, res[0][0], res[1][0])
                dv = jnp.where(lane < HEAD_DIM, res[0][1], res[1][1])
                if has_rope:
                    dk = _rope_bwd(dk, *[t[pl.ds(c0, TK), :] for t in rope_refs])
                dqkv_ref[0, pl.ds(c0, TK), LANES:2 * LANES] = dk.astype(BF16)
                dqkv_ref[0, pl.ds(c0, TK), 2 * LANES:3 * LANES] = dv.astype(BF16)
            else:
                dk_ref[0, pl.ds(c0, TK), :] = res[0][0].astype(BF16)
                dv_ref[0, pl.ds(c0, TK), :] = res[0][1].astype(BF16)
            return 0

        lax.fori_loop(0, nk, kv_loop, 0)

        def fin_q(n, _):
            r0 = pl.multiple_of(n * TQ, TQ)
            rows = pl.ds(r0, TQ)
            dq = dq_acc[rows, :]
            if q_fold is not None:
                dq = dq * q_fold
            if has_rope:
                dq = _rope_bwd(dq, *[t[rows, :] for t in rope_refs])
            if pair:
                dqkv_ref[0, rows, 0:LANES] = dq.astype(BF16)
            else:
                dq_ref[0, rows, :] = dq.astype(BF16)
            if has_bias:
                drow_ref[0, rows, :] = drow_acc[rows, :]
            return 0

        lax.fori_loop(0, nq, fin_q, 0)

    ins, in_specs = [], []
    if pair:
        ins.append(src)
        in_specs.append(pl.BlockSpec((1, S, PAIR_W), lambda b, h: (b, 0, col0 // PAIR_W + h)))
    else:
        ins += [src, kv, kv]
        in_specs += [pl.BlockSpec((1, S, LANES), lambda b, h: (b, 0, col0 // LANES + h)),
                     pl.BlockSpec((1, MEM_LEN, LANES), lambda b, h: (b, 0, h)),
                     pl.BlockSpec((1, MEM_LEN, LANES), lambda b, h: (b, 0, MEM_HEADS + h))]
    row_spec = pl.BlockSpec((1, S, LANES), lambda b, h: (b, 0, h))
    ins += [do, o, lse]
    in_specs += [row_spec] * 3
    if has_bias:
        ins.append(negc)
        in_specs.append(pl.BlockSpec((1, 1, 2, S), lambda b, h: (b, h, 0, 0)))
    if pair:
        ins.append(mask)
        in_specs.append(pl.BlockSpec(mask.shape, lambda b, h: (0, 0, 0)))
    if has_rope:
        ins += list(rope)
        in_specs += [pl.BlockSpec((S, LANES), lambda b, h: (0, 0))] * 3
    W = n_blocks * LANES
    if pair:
        out_specs = [pl.BlockSpec((1, S, PAIR_W), lambda b, h: (b, 0, h))]
        out_shape = [jax.ShapeDtypeStruct((B, S, 3 * W), BF16)]
        if has_bias:
            out_specs.append(pl.BlockSpec((1, 1, 2, S), lambda b, h: (b, h, 0, 0)))
            out_shape.append(jax.ShapeDtypeStruct((B, LANES // 2, 2, S), F32))
            out_specs.append(row_spec)
            out_shape.append(jax.ShapeDtypeStruct((B, S, W), F32))
    else:
        kv_spec = pl.BlockSpec((1, MEM_LEN, LANES), lambda b, h: (b, 0, h))
        out_specs = [row_spec, kv_spec, kv_spec]
        out_shape = [jax.ShapeDtypeStruct((B, S, W), BF16)] + [jax.ShapeDtypeStruct((B, MEM_LEN, W), BF16)] * 2
    return pl.pallas_call(
        body, name=kind + "_attn_bwd", grid=(B, n_blocks),
        in_specs=in_specs, out_specs=out_specs, out_shape=out_shape,
        scratch_shapes=[pltpu.VMEM((S, LANES), BF16), pltpu.VMEM((Sk, LANES), BF16), pltpu.VMEM((Sk, LANES), BF16),
                        pltpu.VMEM((S, LANES), BF16), pltpu.VMEM((S, LANES), F32), pltpu.VMEM((S, LANES), F32)]
        + ([pltpu.VMEM((S, LANES), F32)] if has_bias else []),
        compiler_params=_params(("arbitrary", "arbitrary")),
    )(*ins)


def _log_masks(S, kind):
    nd = 1 if kind == "causal" else S // TQ
    a = np.arange(TQ)[:, None]
    b = np.arange(TK)[None, :]
    out = np.zeros((nd, TQ, TK), np.float32)
    for d in range(nd):
        delta = d * TQ + a - b
        if kind == "causal":
            m = (delta >= 0).astype(np.float64)
        else:
            m = sum(((delta >= 0) & (delta % dil == 0) & (delta <= w)).astype(np.float64) for w, dil in DILATIONS)
        out[d] = np.where(m > 0, np.log(np.maximum(m, 1.0)), NEG_INF)
    return jnp.asarray(out)


def _attn_setup(kind):
    pair = kind != "mem"
    e_dim = HEAD_DIM if pair else MEM_HEAD_DIM
    q_fold, s_scale = _scale_parts(1.0 / math.sqrt(e_dim))
    return dict(pair=pair, col0={"fox": P_FOX, "dil": P_DIL, "mem": P_MQ}[kind],
                n_blocks=FOX_HEADS // 2 if pair else MEM_HEADS, q_fold=q_fold, s_scale=s_scale,
                nh=2 if pair else 1)


def _attn_inputs(kind, src, S, negc, mask, rope, kv, extra):
    cfg = _attn_setup(kind)
    col0 = cfg["col0"]
    ins, in_specs = [], []
    if cfg["pair"]:
        ins.append(src)
        in_specs.append(pl.BlockSpec((1, S, PAIR_W), lambda b, h: (b, 0, col0 // PAIR_W + h)))
    else:
        ins += [src, kv, kv]
        in_specs += [pl.BlockSpec((1, S, LANES), lambda b, h: (b, 0, col0 // LANES + h)),
                     pl.BlockSpec((1, MEM_LEN, LANES), lambda b, h: (b, 0, h)),
                     pl.BlockSpec((1, MEM_LEN, LANES), lambda b, h: (b, 0, MEM_HEADS + h))]
    ins += list(extra)
    in_specs += [pl.BlockSpec((1, S, LANES), lambda b, h: (b, 0, h))] * len(extra)
    if negc is not None:
        ins.append(negc)
        in_specs.append(pl.BlockSpec((1, 1, 2, S), lambda b, h: (b, h, 0, 0)))
    if mask is not None:
        ins.append(mask)
        in_specs.append(pl.BlockSpec(mask.shape, lambda b, h: (0, 0, 0)))
    if rope is not None:
        ins += list(rope)
        in_specs += [pl.BlockSpec((S, LANES), lambda b, h: (0, 0))] * 3
    return ins, in_specs


def _prep_rows(cfg, rope_refs, lane, load_q, load_kv, qs2, ks, vs, S, Sk):
    nh = cfg["nh"]
    R = nh * TQ

    def prep_q(n, _):
        rows = pl.ds(pl.multiple_of(n * TQ, TQ), TQ)
        q = load_q(rows)
        if rope_refs is not None:
            q = _rope(q, *[t[rows, :] for t in rope_refs])
        if cfg["q_fold"] is not None:
            q = q * cfg["q_fold"]
        _store_stacked(cfg, lane, qs2, n, q.astype(BF16))
        return 0

    def prep_kv(n, _):
        rows = pl.ds(pl.multiple_of(n * TK, TK), TK)
        k, v = load_kv(rows)
        if rope_refs is not None:
            k = _rope(k, *[t[rows, :] for t in rope_refs])
        ks[rows, :] = k.astype(BF16)
        vs[rows, :] = v.astype(BF16)
        return 0

    lax.fori_loop(0, S // TQ, prep_q, 0)
    lax.fori_loop(0, Sk // TK, prep_kv, 0)


def _store_stacked(cfg, lane, dst, n, val):
    nh = cfg["nh"]
    R = nh * TQ
    if nh == 1:
        dst[pl.ds(pl.multiple_of(n * R, R), TQ), :] = val
        return
    for hh in range(nh):
        hmask = (lane >= HEAD_DIM * hh) & (lane < HEAD_DIM * (hh + 1))
        dst[pl.ds(pl.multiple_of(n * R + hh * TQ, TQ), TQ), :] = jnp.where(hmask, val, jnp.zeros_like(val))


def _cat(parts, axis):
    return parts[0] if len(parts) == 1 else jnp.concatenate(parts, axis=axis)


def attn_fwd2(kind, src, S, *, negc=None, mask=None, rope=None, kv=None):
    B = src.shape[0]
    cfg = _attn_setup(kind)
    pair, nh, s_scale = cfg["pair"], cfg["nh"], cfg["s_scale"]
    Sk = S if pair else MEM_LEN
    has_bias, has_rope = negc is not None, rope is not None
    R = nh * TQ

    def body(*refs):
        refs = list(refs)
        if pair:
            qkv_ref = refs.pop(0)
        else:
            q_ref, k_ref, v_ref = refs.pop(0), refs.pop(0), refs.pop(0)
        negc_ref = refs.pop(0) if has_bias else None
        mask_ref = refs.pop(0) if mask is not None else None
        rope_refs = [refs.pop(0) for _ in range(3)] if has_rope else None
        o_ref, lse_ref, qs2, ks, vs = refs
        lane = lax.broadcasted_iota(jnp.int32, (1, LANES), 1)

        if pair:
            load_q = lambda rows: qkv_ref[0, rows, 0:LANES]
            load_kv = lambda rows: (qkv_ref[0, rows, LANES:2 * LANES], qkv_ref[0, rows, 2 * LANES:3 * LANES])
        else:
            load_q = lambda rows: q_ref[0, rows, :]
            load_kv = lambda rows: (k_ref[0, rows, :], v_ref[0, rows, :])
        _prep_rows(cfg, rope_refs, lane, load_q, load_kv, qs2, ks, vs, S, Sk)

        def q_loop(i, _):
            q2 = qs2[pl.ds(pl.multiple_of(i * R, R), R), :]

            def step(j, carry, midx):
                ms, ls, acc = carry
                c0 = pl.multiple_of(j * TK, TK)
                k = ks[pl.ds(c0, TK), :]
                v = vs[pl.ds(c0, TK), :]
                s2 = lax.dot_general(q2, k, (((1,), (1,)), ((), ())), preferred_element_type=F32)
                if s_scale is not None:
                    s2 = s2 * s_scale
                new_m, new_l, ps, alphas = [], [], [], []
                for hh in range(nh):
                    s = s2[hh * TQ:(hh + 1) * TQ]
                    if has_bias:
                        s = s + negc_ref[0, 0, pl.ds(hh, 1), pl.ds(c0, TK)]
                    if midx is not None:
                        s = s + mask_ref[midx]
                    m_new = jnp.maximum(ms[hh], jnp.max(s, axis=1, keepdims=True))
                    p = jnp.exp(s - m_new)
                    alpha = jnp.exp(ms[hh] - m_new)
                    new_l.append(alpha * ls[hh] + jnp.sum(p, axis=1, keepdims=True))
                    new_m.append(m_new)
                    ps.append(p.astype(BF16))
                    alphas.append(alpha)
                acc = acc * _cat(alphas, 0) + jnp.dot(_cat(ps, 0), v, preferred_element_type=F32)
                return tuple(new_m), tuple(new_l), acc

            init = (tuple(jnp.full((TQ, 1), NEG_INF, F32) for _ in range(nh)),
                    tuple(jnp.zeros((TQ, 1), F32) for _ in range(nh)), jnp.zeros((R, LANES), F32))
            if kind == "fox":
                carry = lax.fori_loop(0, i, lambda j, c: step(j, c, None), init)
                carry = step(i, carry, 0)
            elif kind == "dil":
                carry = lax.fori_loop(0, i + 1, lambda j, c: step(j, c, i - j), init)
            else:
                carry = lax.fori_loop(0, Sk // TK, lambda j, c: step(j, c, None), init)
            ms, ls, acc = carry
            outs = [acc[hh * TQ:(hh + 1) * TQ] / ls[hh] for hh in range(nh)]
            lses = [ms[hh] + jnp.log(ls[hh]) for hh in range(nh)]
            rows = pl.ds(pl.multiple_of(i * TQ, TQ), TQ)
            if pair:
                o_ref[0, rows, :] = jnp.where(lane < HEAD_DIM, outs[0], outs[1])
                lse_ref[0, rows, :] = jnp.where(lane < HEAD_DIM, lses[0], lses[1])
            else:
                o_ref[0, rows, :] = outs[0]
                lse_ref[0, rows, :] = jnp.broadcast_to(lses[0], (TQ, LANES))
            return 0

        lax.fori_loop(0, S // TQ, q_loop, 0)

    ins, in_specs = _attn_inputs(kind, src, S, negc, mask, rope, kv, ())
    W = cfg["n_blocks"] * LANES
    out_spec = pl.BlockSpec((1, S, LANES), lambda b, h: (b, 0, h))
    return pl.pallas_call(
        body, name=kind + "_attn_fwd", grid=(B, cfg["n_blocks"]),
        in_specs=in_specs, out_specs=[out_spec, out_spec],
        out_shape=[jax.ShapeDtypeStruct((B, S, W), F32)] * 2,
        scratch_shapes=[pltpu.VMEM((nh * S, LANES), BF16), pltpu.VMEM((Sk, LANES), BF16),
                        pltpu.VMEM((Sk, LANES), BF16)],
        compiler_params=_params(("arbitrary", "arbitrary")),
    )(*ins)


def attn_bwd2(kind, src, do, o, lse, S, *, negc=None, mask=None, rope=None, kv=None):
    B = src.shape[0]
    cfg = _attn_setup(kind)
    pair, nh, s_scale, q_fold = cfg["pair"], cfg["nh"], cfg["s_scale"], cfg["q_fold"]
    Sk = S if pair else MEM_LEN
    has_bias, has_rope = negc is not None, rope is not None
    R = nh * TQ
    nq, nk = S // TQ, Sk // TK

    def body(*refs):
        refs = list(refs)
        if pair:
            qkv_ref = refs.pop(0)
        else:
            q_ref, k_ref, v_ref = refs.pop(0), refs.pop(0), refs.pop(0)
        do_ref, o_ref, lse_ref = refs.pop(0), refs.pop(0), refs.pop(0)
        negc_ref = refs.pop(0) if has_bias else None
        mask_ref = refs.pop(0) if mask is not None else None
        rope_refs = [refs.pop(0) for _ in range(3)] if has_rope else None
        if pair:
            dqkv_ref = refs.pop(0)
            dnegc_ref = refs.pop(0) if has_bias else None
            drow_ref = refs.pop(0) if has_bias else None
        else:
            dq_ref, dk_ref, dv_ref = refs.pop(0), refs.pop(0), refs.pop(0)
        qs2, ks, vs, dos2, lse_s, delta_s, dk_acc, dv_acc = refs[:8]
        dneg_acc = refs[8] if has_bias else None
        lane = lax.broadcasted_iota(jnp.int32, (1, LANES), 1)

        if pair:
            load_q = lambda rows: qkv_ref[0, rows, 0:LANES]
            load_kv = lambda rows: (qkv_ref[0, rows, LANES:2 * LANES], qkv_ref[0, rows, 2 * LANES:3 * LANES])
        else:
            load_q = lambda rows: q_ref[0, rows, :]
            load_kv = lambda rows: (k_ref[0, rows, :], v_ref[0, rows, :])
        _prep_rows(cfg, rope_refs, lane, load_q, load_kv, qs2, ks, vs, S, Sk)

        def prep_do(n, _):
            rows = pl.ds(pl.multiple_of(n * TQ, TQ), TQ)
            dob = do_ref[0, rows, :].astype(BF16)
            _store_stacked(cfg, lane, dos2, n, dob)
            prod = dob.astype(F32) * o_ref[0, rows, :]
            lse_blk = lse_ref[0, rows, :]
            for hh in range(nh):
                dst = pl.ds(pl.multiple_of(n * R + hh * TQ, TQ), TQ)
                if pair:
                    hmask = (lane >= HEAD_DIM * hh) & (lane < HEAD_DIM * (hh + 1))
                    d = jnp.sum(jnp.where(hmask, prod, 0.0), axis=1, keepdims=True)
                    lse_s[dst, :] = jnp.broadcast_to(lse_blk[:, hh * HEAD_DIM:hh * HEAD_DIM + 1], (TQ, LANES))
                else:
                    d = jnp.sum(prod, axis=1, keepdims=True)
                    lse_s[dst, :] = lse_blk
                delta_s[dst, :] = jnp.broadcast_to(d, (TQ, LANES))
            return 0

        def zero_kv(n, _):
            rows = pl.ds(pl.multiple_of(n * TK, TK), TK)
            dk_acc[rows, :] = jnp.zeros((TK, LANES), F32)
            dv_acc[rows, :] = jnp.zeros((TK, LANES), F32)
            return 0

        lax.fori_loop(0, nq, prep_do, 0)
        lax.fori_loop(0, nk, zero_kv, 0)
        if has_bias:
            dneg_acc[...] = jnp.zeros(dneg_acc.shape, F32)

        def q_loop(i, _):
            rows2 = pl.ds(pl.multiple_of(i * R, R), R)
            q2 = qs2[rows2, :]
            do2 = dos2[rows2, :]
            lse2 = lse_s[rows2, :]
            delta2 = delta_s[rows2, :]
            wide = lambda t: jnp.concatenate([t] * (TK // LANES), axis=1)

            def step(j, carry, midx):
                dq2, drow = carry
                c0 = pl.multiple_of(j * TK, TK)
                kcols = pl.ds(c0, TK)
                k = ks[kcols, :]
                v = vs[kcols, :]
                s2 = lax.dot_general(q2, k, (((1,), (1,)), ((), ())), preferred_element_type=F32)
                if s_scale is not None:
                    s2 = s2 * s_scale
                if has_bias or midx is not None:
                    halves = []
                    for hh in range(nh):
                        s = s2[hh * TQ:(hh + 1) * TQ]
                        if has_bias:
                            s = s + negc_ref[0, 0, pl.ds(hh, 1), kcols]
                        if midx is not None:
                            s = s + mask_ref[midx]
                        halves.append(s)
                    s2 = _cat(halves, 0)
                p2 = jnp.exp(s2 - wide(lse2))
                dp2 = lax.dot_general(do2, v, (((1,), (1,)), ((), ())), preferred_element_type=F32)
                ds2 = p2 * (dp2 - wide(delta2))
                if has_bias:
                    drow = drow + jnp.sum(ds2, axis=1, keepdims=True)
                    for hh in range(nh):
                        dneg_acc[pl.ds(hh, 1), kcols] += jnp.sum(ds2[hh * TQ:(hh + 1) * TQ], axis=0, keepdims=True)
                if s_scale is not None:
                    ds2 = ds2 * s_scale
                dsb = ds2.astype(BF16)
                dv_acc[kcols, :] += lax.dot_general(p2.astype(BF16), do2, (((0,), (0,)), ((), ())),
                                                    preferred_element_type=F32)
                dk_acc[kcols, :] += lax.dot_general(dsb, q2, (((0,), (0,)), ((), ())), preferred_element_type=F32)
                dq2 = dq2 + jnp.dot(dsb, k, preferred_element_type=F32)
                return dq2, drow

            init = (jnp.zeros((R, LANES), F32), jnp.zeros((R, 1), F32))
            if kind == "fox":
                carry = lax.fori_loop(0, i, lambda j, c: step(j, c, None), init)
                carry = step(i, carry, 0)
            elif kind == "dil":
                carry = lax.fori_loop(0, i + 1, lambda j, c: step(j, c, i - j), init)
            else:
                carry = lax.fori_loop(0, nk, lambda j, c: step(j, c, None), init)
            dq2, drow = carry
            rows = pl.ds(pl.multiple_of(i * TQ, TQ), TQ)
            dq = jnp.where(lane < HEAD_DIM, dq2[0:TQ], dq2[TQ:2 * TQ]) if pair else dq2
            if q_fold is not None:
                dq = dq * q_fold
            if has_rope:
                dq = _rope_bwd(dq, *[t[rows, :] for t in rope_refs])
            if pair:
                dqkv_ref[0, rows, 0:LANES] = dq.astype(BF16)
            else:
                dq_ref[0, rows, :] = dq.astype(BF16)
            if has_bias:
                drow_ref[0, rows, :] = jnp.where(lane < HEAD_DIM, drow[0:TQ], drow[TQ:2 * TQ])
            return 0

        lax.fori_loop(0, nq, q_loop, 0)

        def fin_kv(n, _):
            rows = pl.ds(pl.multiple_of(n * TK, TK), TK)
            dk = dk_acc[rows, :]
            if has_rope:
                dk = _rope_bwd(dk, *[t[rows, :] for t in rope_refs])
            if pair:
                dqkv_ref[0, rows, LANES:2 * LANES] = dk.astype(BF16)
                dqkv_ref[0, rows, 2 * LANES:3 * LANES] = dv_acc[rows, :].astype(BF16)
            else:
                dk_ref[0, rows, :] = dk.astype(BF16)
                dv_ref[0, rows, :] = dv_acc[rows, :].astype(BF16)
            return 0

        lax.fori_loop(0, nk, fin_kv, 0)
        if has_bias:
            dnegc_ref[0, 0] = dneg_acc[...]

    ins, in_specs = _attn_inputs(kind, src, S, negc, mask, rope, kv, (do, o, lse))
    W = cfg["n_blocks"] * LANES
    row_spec = pl.BlockSpec((1, S, LANES), lambda b, h: (b, 0, h))
    if pair:
        out_specs = [pl.BlockSpec((1, S, PAIR_W), lambda b, h: (b, 0, h))]
        out_shape = [jax.ShapeDtypeStruct((B, S, 3 * W), BF16)]
        if has_bias:
            out_specs += [pl.BlockSpec((1, 1, 2, S), lambda b, h: (b, h, 0, 0)), row_spec]
            out_shape += [jax.ShapeDtypeStruct((B, LANES // 2, 2, S), F32), jax.ShapeDtypeStruct((B, S, W), F32)]
    else:
        kv_spec = pl.BlockSpec((1, MEM_LEN, LANES), lambda b, h: (b, 0, h))
        out_specs = [row_spec, kv_spec, kv_spec]
        out_shape = [jax.ShapeDtypeStruct((B, S, W), BF16)] + [jax.ShapeDtypeStruct((B, MEM_LEN, W), BF16)] * 2
    scratch = [pltpu.VMEM((nh * S, LANES), BF16), pltpu.VMEM((Sk, LANES), BF16), pltpu.VMEM((Sk, LANES), BF16),
               pltpu.VMEM((nh * S, LANES), BF16), pltpu.VMEM((nh * S, LANES), F32), pltpu.VMEM((nh * S, LANES), F32),
               pltpu.VMEM((Sk, LANES), F32), pltpu.VMEM((Sk, LANES), F32)]
    if has_bias:
        scratch.append(pltpu.VMEM((2, S), F32))
    return pl.pallas_call(
        body, name=kind + "_attn_bwd", grid=(B, cfg["n_blocks"]),
        in_specs=in_specs, out_specs=out_specs, out_shape=out_shape, scratch_shapes=scratch,
        compiler_params=_params(("arbitrary", "arbitrary")),
    )(*ins)


def _log_masks_t(S, kind):
    return jnp.swapaxes(_log_masks(S, kind), 1, 2)


def _head_rows(hh, pair):
    row = lax.broadcasted_iota(jnp.int32, (LANES, 1), 0)
    if not pair:
        return row >= 0
    return (row >= HEAD_DIM * hh) & (row < HEAD_DIM * (hh + 1))


def _attn_t_inputs(kind, src, S, negc_cols, mask, rope, kv):
    cfg = _attn_setup(kind)
    col0 = cfg["col0"]
    ins, in_specs = [], []
    if cfg["pair"]:
        ins.append(src)
        in_specs.append(pl.BlockSpec((1, S, PAIR_W), lambda b, h: (b, 0, col0 // PAIR_W + h)))
    else:
        ins += [src, kv, kv]
        in_specs += [pl.BlockSpec((1, S, LANES), lambda b, h: (b, 0, col0 // LANES + h)),
                     pl.BlockSpec((1, MEM_LEN, LANES), lambda b, h: (b, 0, h)),
                     pl.BlockSpec((1, MEM_LEN, LANES), lambda b, h: (b, 0, MEM_HEADS + h))]
    if negc_cols is not None:
        ins.append(negc_cols)
        in_specs.append(pl.BlockSpec((1, S, LANES), lambda b, h: (b, 0, 0)))
    if mask is not None:
        ins.append(mask)
        in_specs.append(pl.BlockSpec(mask.shape, lambda b, h: (0, 0, 0)))
    if rope is not None:
        ins += list(rope)
        in_specs += [pl.BlockSpec((S, LANES), lambda b, h: (0, 0))] * 3
    return ins, in_specs


def _attn_t_prep(cfg, refs, S, Sk, *, qT2s, ks, q2s=None, vs=None, vTs=None, kTs=None, nb=None):
    pair, nh = cfg["pair"], cfg["nh"]
    lane = lax.broadcasted_iota(jnp.int32, (1, LANES), 1)
    rope_refs = refs["rope"]

    def prep_q(n, _):
        rows = pl.ds(pl.multiple_of(n * TQ, TQ), TQ)
        q = refs["load_q"](rows)
        if rope_refs is not None:
            q = _rope(q, *[t[rows, :] for t in rope_refs])
        if cfg["q_fold"] is not None:
            q = q * cfg["q_fold"]
        qb = q.astype(BF16)
        if q2s is not None:
            _store_stacked(cfg, lane, q2s, n, qb)
        qtb = qb.T
        for hh in range(nh):
            qT2s[n, :, hh * TQ:(hh + 1) * TQ] = jnp.where(_head_rows(hh, pair), qtb, jnp.zeros_like(qtb))
        return 0

    def prep_kv(n, _):
        rows = pl.ds(pl.multiple_of(n * TK, TK), TK)
        k, v = refs["load_kv"](rows)
        if rope_refs is not None:
            k = _rope(k, *[t[rows, :] for t in rope_refs])
        kb = k.astype(BF16)
        vb = v.astype(BF16)
        ks[rows, :] = kb
        if vs is not None:
            vs[rows, :] = vb
        if vTs is not None:
            vTs[n] = vb.T
        if kTs is not None:
            kTs[n] = kb.T
        if nb is not None:
            blk = refs["negc"][0, rows, :]
            for hh in range(nh):
                h = 2 * refs["block"] + hh
                col = jnp.sum(jnp.where(lane == h, blk, 0.0), axis=1, keepdims=True)
                nb[hh, rows, :] = jnp.broadcast_to(col, (TK, LANES))
        return 0

    lax.fori_loop(0, S // TQ, prep_q, 0)
    lax.fori_loop(0, Sk // TK, prep_kv, 0)


def _raw_scores_t(cfg, k, qT2):
    sT = jnp.dot(k, qT2, preferred_element_type=F32)
    if cfg["s_scale"] is not None:
        sT = sT * cfg["s_scale"]
    return sT


def _bias_mask_t(cfg, sT, nb, mask_ref, kc, midx):
    nh = cfg["nh"]
    if nb is None and midx is None:
        return sT
    parts = []
    for hh in range(nh):
        t = sT[:, hh * TQ:(hh + 1) * TQ]
        if nb is not None:
            t = t + jnp.concatenate([nb[hh, kc, :]] * (TQ // LANES), axis=1)
        if midx is not None:
            t = t + mask_ref[midx]
        parts.append(t)
    return _cat(parts, 1)


def _kv_plan(kind, i, nk):
    if kind == "fox":
        return i, (lambda j: None), 0
    if kind == "dil":
        return i, (lambda j: i - j), 0
    return nk - 1, (lambda j: None), None


def attn_fwd3(kind, src, S, *, negc_cols=None, mask=None, rope=None, kv=None):
    B = src.shape[0]
    cfg = _attn_setup(kind)
    pair, nh = cfg["pair"], cfg["nh"]
    Sk = S if pair else MEM_LEN
    has_bias, has_rope = negc_cols is not None, rope is not None
    R = nh * TQ
    nq, nk = S // TQ, Sk // TK

    def body(*refs):
        refs = list(refs)
        if pair:
            qkv_ref = refs.pop(0)
            load_q = lambda rows: qkv_ref[0, rows, 0:LANES]
            load_kv = lambda rows: (qkv_ref[0, rows, LANES:2 * LANES], qkv_ref[0, rows, 2 * LANES:3 * LANES])
        else:
            q_ref, k_ref, v_ref = refs.pop(0), refs.pop(0), refs.pop(0)
            load_q = lambda rows: q_ref[0, rows, :]
            load_kv = lambda rows: (k_ref[0, rows, :], v_ref[0, rows, :])
        negc_ref = refs.pop(0) if has_bias else None
        mask_ref = refs.pop(0) if mask is not None else None
        rope_refs = [refs.pop(0) for _ in range(3)] if has_rope else None
        o_ref, lse_ref, qT2s, ks, vTs = refs[:5]
        nb = refs[5] if has_bias else None
        _attn_t_prep(cfg, dict(load_q=load_q, load_kv=load_kv, rope=rope_refs, negc=negc_ref,
                               block=pl.program_id(1)), S, Sk,
                     qT2s=qT2s, ks=ks, vTs=vTs, nb=nb)

        def q_loop(i, _):
            qT2 = qT2s[i]

            last, mask_of, mask_last = _kv_plan(kind, i, nk)

            def cols(j):
                return pl.ds(pl.multiple_of(j * TK, TK), TK)

            def scores(j):
                return _raw_scores_t(cfg, ks[cols(j), :], qT2)

            def soft(s_raw, j, midx, m, l):
                sT = _bias_mask_t(cfg, s_raw, nb, mask_ref, cols(j), midx)
                m_new = jnp.maximum(m, jnp.max(sT, axis=0, keepdims=True))
                p = jnp.exp(sT - m_new)
                alpha = jnp.exp(m - m_new)
                return m_new, alpha * l + jnp.sum(p, axis=0, keepdims=True), alpha, p.astype(BF16)

            def pv(j, p):
                return jnp.dot(vTs[j], p, preferred_element_type=F32)

            def body(j, carry):
                s_cur, p_prev, m, l, accT = carry
                pv_prev = pv(jnp.maximum(j - 1, 0), p_prev)
                s_next = scores(j + 1)
                m, l, alpha, p = soft(s_cur, j, mask_of(j), m, l)
                return s_next, p, m, l, (accT + pv_prev) * alpha

            init = (scores(0), jnp.zeros((TK, R), BF16), jnp.full((1, R), NEG_INF, F32), jnp.zeros((1, R), F32),
                    jnp.zeros((LANES, R), F32))
            s_cur, p_prev, m, l, accT = lax.fori_loop(0, last, body, init)
            pv_prev = pv(jnp.maximum(last - 1, 0), p_prev)
            m, l, alpha, p = soft(s_cur, last, mask_last, m, l)
            accT = (accT + pv_prev) * alpha + pv(last, p)
            oT2 = accT / l
            oT = jnp.where(_head_rows(0, True), oT2[:, 0:TQ], oT2[:, TQ:2 * TQ]) if pair else oT2
            o_ref[0, pl.ds(pl.multiple_of(i * TQ, TQ), TQ), :] = oT.T
            lse_ref[0, 0, pl.ds(i, 1), :] = m + jnp.log(l)
            return 0

        lax.fori_loop(0, nq, q_loop, 0)

    ins, in_specs = _attn_t_inputs(kind, src, S, negc_cols, mask, rope, kv)
    W = cfg["n_blocks"] * LANES
    scratch = [pltpu.VMEM((nq, LANES, R), BF16), pltpu.VMEM((Sk, LANES), BF16), pltpu.VMEM((nk, LANES, TK), BF16)]
    if has_bias:
        scratch.append(pltpu.VMEM((nh, Sk, LANES), F32))
    return pl.pallas_call(
        body, name=kind + "_attn_fwd", grid=(B, cfg["n_blocks"]),
        in_specs=in_specs,
        out_specs=[pl.BlockSpec((1, S, LANES), lambda b, h: (b, 0, h)),
                   pl.BlockSpec((1, 1, nq, R), lambda b, h: (b, h, 0, 0))],
        out_shape=[jax.ShapeDtypeStruct((B, S, W), F32), jax.ShapeDtypeStruct((B, cfg["n_blocks"], nq, R), F32)],
        scratch_shapes=scratch,
        compiler_params=_params(("arbitrary", "arbitrary")),
    )(*ins)


def _tile_walk(kind, nq, nk):
    if kind == "mem":
        return nq * nk, (lambda i, j: (jnp.where(j < nk - 1, i, i + 1), jnp.where(j < nk - 1, j + 1, 0))), None
    nxt = lambda i, j: (jnp.where(j < i, i, i + 1), jnp.where(j < i, j + 1, 0))
    if kind == "fox":
        return nq * (nq + 1) // 2, nxt, (lambda i, j: jnp.where(j == i, 0, 1))
    return nq * (nq + 1) // 2, nxt, (lambda i, j: i - j)


def attn_fwd4(kind, src, S, *, negc_cols=None, mask=None, rope=None, kv=None):
    B = src.shape[0]
    cfg = _attn_setup(kind)
    pair, nh = cfg["pair"], cfg["nh"]
    Sk = S if pair else MEM_LEN
    has_bias, has_rope = negc_cols is not None, rope is not None
    R = nh * TQ
    nq, nk = S // TQ, Sk // TK
    n_pairs, successor, mask_index = _tile_walk(kind, nq, nk)
    assert n_pairs % 2 == 0

    def body(*refs):
        refs = list(refs)
        if pair:
            qkv_ref = refs.pop(0)
            load_q = lambda rows: qkv_ref[0, rows, 0:LANES]
            load_kv = lambda rows: (qkv_ref[0, rows, LANES:2 * LANES], qkv_ref[0, rows, 2 * LANES:3 * LANES])
        else:
            q_ref, k_ref, v_ref = refs.pop(0), refs.pop(0), refs.pop(0)
            load_q = lambda rows: q_ref[0, rows, :]
            load_kv = lambda rows: (k_ref[0, rows, :], v_ref[0, rows, :])
        negc_ref = refs.pop(0) if has_bias else None
        mask_ref = refs.pop(0) if mask is not None else None
        rope_refs = [refs.pop(0) for _ in range(3)] if has_rope else None
        o_ref, lse_ref, qT2s, ks, vTs, s_a, s_b, p_a, p_b, acc_all, m_all, l_all = refs[:12]
        nb = refs[12] if has_bias else None
        _attn_t_prep(cfg, dict(load_q=load_q, load_kv=load_kv, rope=rope_refs, negc=negc_ref,
                               block=pl.program_id(1)), S, Sk, qT2s=qT2s, ks=ks, vTs=vTs, nb=nb)

        def cols(j):
            return pl.ds(pl.multiple_of(j * TK, TK), TK)

        def park(i, m, l, accT):
            acc_all[i] = accT
            m_all[pl.ds(i, 1), :] = m
            l_all[pl.ds(i, 1), :] = l

        def finish(i, _):
            l = l_all[pl.ds(i, 1), :]
            oT2 = acc_all[i] / l
            oT = jnp.where(_head_rows(0, True), oT2[:, 0:TQ], oT2[:, TQ:2 * TQ]) if pair else oT2
            o_ref[0, pl.ds(pl.multiple_of(i * TQ, TQ), TQ), :] = oT.T
            lse_ref[0, 0, pl.ds(i, 1), :] = m_all[pl.ds(i, 1), :] + jnp.log(l)
            return 0

        def half(i, j, i_prev, j_prev, s_cur, s_next, p_cur, p_prev, m, l, accT):
            i_n, j_n = successor(i, j)
            acc_full = accT + jnp.dot(vTs[j_prev], p_prev[...], preferred_element_type=F32)
            s_next[...] = _raw_scores_t(cfg, ks[cols(j_n), :], qT2s[jnp.minimum(i_n, nq - 1)])
            park(i_prev, m, l, acc_full)
            first = j == 0
            m = jnp.where(first, NEG_INF, m)
            l = jnp.where(first, 0.0, l)
            sT = _bias_mask_t(cfg, s_cur[...], nb, mask_ref, cols(j), None if mask_index is None else mask_index(i, j))
            m_new = jnp.maximum(m, jnp.max(sT, axis=0, keepdims=True))
            p = jnp.exp(sT - m_new)
            alpha = jnp.exp(m - m_new)
            p_cur[...] = p.astype(BF16)
            return i_n, j_n, i, j, m_new, alpha * l + jnp.sum(p, axis=0, keepdims=True), acc_full * alpha

        def two(_, carry):
            i, j, i_prev, j_prev, m, l, accT = carry
            i, j, i_prev, j_prev, m, l, accT = half(i, j, i_prev, j_prev, s_a, s_b, p_a, p_b, m, l, accT)
            return half(i, j, i_prev, j_prev, s_b, s_a, p_b, p_a, m, l, accT)

        s_a[...] = _raw_scores_t(cfg, ks[cols(0), :], qT2s[0])
        p_b[...] = jnp.zeros((TK, R), BF16)
        zero = jnp.int32(0)
        init = (zero, zero, zero, zero, jnp.full((1, R), NEG_INF, F32), jnp.ones((1, R), F32),
                jnp.zeros((LANES, R), F32))
        _, _, i_prev, j_prev, m, l, accT = lax.fori_loop(0, n_pairs // 2, two, init)
        park(i_prev, m, l, accT + jnp.dot(vTs[j_prev], p_b[...], preferred_element_type=F32))
        lax.fori_loop(0, nq, finish, 0)

    ins, in_specs = _attn_t_inputs(kind, src, S, negc_cols, mask, rope, kv)
    W = cfg["n_blocks"] * LANES
    scratch = [pltpu.VMEM((nq, LANES, R), BF16), pltpu.VMEM((Sk, LANES), BF16), pltpu.VMEM((nk, LANES, TK), BF16),
               pltpu.VMEM((TK, R), F32), pltpu.VMEM((TK, R), F32), pltpu.VMEM((TK, R), BF16), pltpu.VMEM((TK, R), BF16),
               pltpu.VMEM((nq, LANES, R), F32), pltpu.VMEM((nq, R), F32), pltpu.VMEM((nq, R), F32)]
    if has_bias:
        scratch.append(pltpu.VMEM((nh, Sk, LANES), F32))
    return pl.pallas_call(
        body, name=kind + "_attn_fwd", grid=(B, cfg["n_blocks"]),
        in_specs=in_specs,
        out_specs=[pl.BlockSpec((1, S, LANES), lambda b, h: (b, 0, h)),
                   pl.BlockSpec((1, 1, nq, R), lambda b, h: (b, h, 0, 0))],
        out_shape=[jax.ShapeDtypeStruct((B, S, W), F32), jax.ShapeDtypeStruct((B, cfg["n_blocks"], nq, R), F32)],
        scratch_shapes=scratch,
        compiler_params=_params(("arbitrary", "arbitrary")),
    )(*ins)


def attn_bwd3(kind, src, do, o, lse, S, *, negc_cols=None, mask=None, rope=None, kv=None):
    B = src.shape[0]
    cfg = _attn_setup(kind)
    pair, nh, s_scale, q_fold = cfg["pair"], cfg["nh"], cfg["s_scale"], cfg["q_fold"]
    Sk = S if pair else MEM_LEN
    has_bias, has_rope = negc_cols is not None, rope is not None
    R = nh * TQ
    nq, nk = S // TQ, Sk // TK
    n_pairs, successor, mask_index = _tile_walk(kind, nq, nk)
    assert n_pairs % 2 == 0

    def body(*refs):
        refs = list(refs)
        if pair:
            qkv_ref = refs.pop(0)
            load_q = lambda rows: qkv_ref[0, rows, 0:LANES]
            load_kv = lambda rows: (qkv_ref[0, rows, LANES:2 * LANES], qkv_ref[0, rows, 2 * LANES:3 * LANES])
        else:
            q_ref, k_ref, v_ref = refs.pop(0), refs.pop(0), refs.pop(0)
            load_q = lambda rows: q_ref[0, rows, :]
            load_kv = lambda rows: (k_ref[0, rows, :], v_ref[0, rows, :])
        negc_ref = refs.pop(0) if has_bias else None
        mask_ref = refs.pop(0) if mask is not None else None
        rope_refs = [refs.pop(0) for _ in range(3)] if has_rope else None
        do_ref, o_ref, lse_ref = refs.pop(0), refs.pop(0), refs.pop(0)
        if pair:
            dqkv_ref = refs.pop(0)
            dneg_ref = refs.pop(0) if has_bias else None
            drow_ref = refs.pop(0) if has_bias else None
        else:
            dq_ref, dk_ref, dv_ref = refs.pop(0), refs.pop(0), refs.pop(0)
        qT2s, ks, q2s, vs, kTs, doT2s, do2s, delta_s, dk_acc, dv_acc = refs[:10]
        bufs_a, bufs_b, dq_all = refs[10:14], refs[14:18], refs[18]
        nb, dneg_acc, drow_all = (refs[19], refs[20], refs[21]) if has_bias else (None, None, None)
        lane = lax.broadcasted_iota(jnp.int32, (1, LANES), 1)
        _attn_t_prep(cfg, dict(load_q=load_q, load_kv=load_kv, rope=rope_refs, negc=negc_ref,
                               block=pl.program_id(1)), S, Sk,
                     qT2s=qT2s, ks=ks, q2s=q2s, vs=vs, kTs=kTs, nb=nb)

        def prep_do(n, _):
            rows = pl.ds(pl.multiple_of(n * TQ, TQ), TQ)
            dob = do_ref[0, rows, :].astype(BF16)
            _store_stacked(cfg, lane, do2s, n, dob)
            doT = dob.astype(F32).T
            prodT = doT * o_ref[0, rows, :].T
            doTb = doT.astype(BF16)
            for hh in range(nh):
                hm = _head_rows(hh, pair)
                doT2s[n, :, hh * TQ:(hh + 1) * TQ] = jnp.where(hm, doTb, jnp.zeros_like(doTb))
                delta_s[pl.ds(n, 1), hh * TQ:(hh + 1) * TQ] = jnp.sum(jnp.where(hm, prodT, 0.0), axis=0, keepdims=True)
            return 0

        def zero_kv(n, _):
            rows = pl.ds(pl.multiple_of(n * TK, TK), TK)
            dk_acc[rows, :] = jnp.zeros((TK, LANES), F32)
            dv_acc[rows, :] = jnp.zeros((TK, LANES), F32)
            if has_bias:
                for hh in range(nh):
                    dneg_acc[hh, rows, :] = jnp.zeros((TK, LANES), F32)
            return 0

        lax.fori_loop(0, nq, prep_do, 0)
        lax.fori_loop(0, nk, zero_kv, 0)

        def cols(j):
            return pl.ds(pl.multiple_of(j * TK, TK), TK)

        def rows2(i):
            return pl.ds(pl.multiple_of(i * R, R), R)

        def park(i, dqT2, drow):
            dq_all[i] = dqT2
            if has_bias:
                drow_all[pl.ds(i, 1), :] = drow

        def half(i, j, i_prev, j_prev, cur, nxt_bufs, prv, dqT2, drow):
            s_cur, dp_cur, pb_cur, dsb_cur = cur
            s_next, dp_next = nxt_bufs[0], nxt_bufs[1]
            pb_prev, dsb_prev = prv[2], prv[3]
            i_n, j_n = successor(i, j)
            first = j == 0
            if has_bias:
                drow_all[pl.ds(i_prev, 1), :] = drow
            drow = jnp.where(first, 0.0, drow)
            kc = cols(j)
            sT = _bias_mask_t(cfg, s_cur[...], nb, mask_ref, kc, None if mask_index is None else mask_index(i, j))
            pT = jnp.exp(sT - lse_ref[0, 0, pl.ds(i, 1), :])
            dsT = pT * (dp_cur[...] - delta_s[pl.ds(i, 1), :])
            if has_bias:
                drow = drow + jnp.sum(dsT, axis=0, keepdims=True)
                for hh in range(nh):
                    part = dsT[:, hh * TQ:hh * TQ + LANES]
                    for t in range(1, TQ // LANES):
                        part = part + dsT[:, hh * TQ + t * LANES:hh * TQ + (t + 1) * LANES]
                    dneg_acc[hh, kc, :] += part
            if s_scale is not None:
                dsT = dsT * s_scale
            pb_cur[...] = pT.astype(BF16)
            dsb_cur[...] = dsT.astype(BF16)
            kp = cols(j_prev)
            dv_acc[kp, :] += jnp.dot(pb_prev[...], do2s[rows2(i_prev), :], preferred_element_type=F32)
            dk_acc[kp, :] += jnp.dot(dsb_prev[...], q2s[rows2(i_prev), :], preferred_element_type=F32)
            dq_full = dqT2 + jnp.dot(kTs[j_prev], dsb_prev[...], preferred_element_type=F32)
            dq_all[i_prev] = dq_full
            dqT2 = jnp.where(first, 0.0, dq_full)
            i_nc = jnp.minimum(i_n, nq - 1)
            kn = cols(j_n)
            s_next[...] = _raw_scores_t(cfg, ks[kn, :], qT2s[i_nc])
            dp_next[...] = jnp.dot(vs[kn, :], doT2s[i_nc], preferred_element_type=F32)
            return i_n, j_n, i, j, dqT2, drow

        def two(_, carry):
            i, j, i_prev, j_prev, dqT2, drow = carry
            i, j, i_prev, j_prev, dqT2, drow = half(i, j, i_prev, j_prev, bufs_a, bufs_b, bufs_b, dqT2, drow)
            return half(i, j, i_prev, j_prev, bufs_b, bufs_a, bufs_a, dqT2, drow)

        bufs_a[0][...] = _raw_scores_t(cfg, ks[cols(0), :], qT2s[0])
        bufs_a[1][...] = jnp.dot(vs[cols(0), :], doT2s[0], preferred_element_type=F32)
        bufs_b[2][...] = jnp.zeros((TK, R), BF16)
        bufs_b[3][...] = jnp.zeros((TK, R), BF16)
        zero = jnp.int32(0)
        init = (zero, zero, zero, zero, jnp.zeros((LANES, R), F32), jnp.zeros((1, R), F32))
        _, _, i_prev, j_prev, dqT2, drow = lax.fori_loop(0, n_pairs // 2, two, init)
        kp = cols(j_prev)
        dv_acc[kp, :] += jnp.dot(bufs_b[2][...], do2s[rows2(i_prev), :], preferred_element_type=F32)
        dk_acc[kp, :] += jnp.dot(bufs_b[3][...], q2s[rows2(i_prev), :], preferred_element_type=F32)
        park(i_prev, dqT2 + jnp.dot(kTs[j_prev], bufs_b[3][...], preferred_element_type=F32), drow)

        def fin_q(i, _):
            rows = pl.ds(pl.multiple_of(i * TQ, TQ), TQ)
            dqT2 = dq_all[i]
            dqT = jnp.where(_head_rows(0, True), dqT2[:, 0:TQ], dqT2[:, TQ:2 * TQ]) if pair else dqT2
            dq = dqT.T
            if q_fold is not None:
                dq = dq * q_fold
            if has_rope:
                dq = _rope_bwd(dq, *[t[rows, :] for t in rope_refs])
            if pair:
                dqkv_ref[0, rows, 0:LANES] = dq.astype(BF16)
            else:
                dq_ref[0, rows, :] = dq.astype(BF16)
            if has_bias:
                drow_ref[0, 0, pl.ds(i, 1), :] = drow_all[pl.ds(i, 1), :]
            return 0

        lax.fori_loop(0, nq, fin_q, 0)

        def fin_kv(n, _):
            rows = pl.ds(pl.multiple_of(n * TK, TK), TK)
            dk = dk_acc[rows, :]
            if has_rope:
                dk = _rope_bwd(dk, *[t[rows, :] for t in rope_refs])
            if pair:
                dqkv_ref[0, rows, LANES:2 * LANES] = dk.astype(BF16)
                dqkv_ref[0, rows, 2 * LANES:3 * LANES] = dv_acc[rows, :].astype(BF16)
            else:
                dk_ref[0, rows, :] = dk.astype(BF16)
                dv_ref[0, rows, :] = dv_acc[rows, :].astype(BF16)
            if has_bias:
                x0 = jnp.sum(dneg_acc[0, rows, :], axis=1, keepdims=True)
                x1 = jnp.sum(dneg_acc[1, rows, :], axis=1, keepdims=True)
                dneg_ref[0, rows, :] = jnp.where(lane == 0, x0, jnp.where(lane == 1, x1, 0.0))
            return 0

        lax.fori_loop(0, nk, fin_kv, 0)

    ins, in_specs = _attn_t_inputs(kind, src, S, negc_cols, mask, rope, kv)
    row_spec = pl.BlockSpec((1, S, LANES), lambda b, h: (b, 0, h))
    vec_spec = pl.BlockSpec((1, 1, nq, R), lambda b, h: (b, h, 0, 0))
    ins += [do, o, lse]
    in_specs += [row_spec, row_spec, vec_spec]
    W = cfg["n_blocks"] * LANES
    if pair:
        out_specs = [pl.BlockSpec((1, S, PAIR_W), lambda b, h: (b, 0, h))]
        out_shape = [jax.ShapeDtypeStruct((B, S, 3 * W), BF16)]
        if has_bias:
            out_specs += [row_spec, vec_spec]
            out_shape += [jax.ShapeDtypeStruct((B, S, W), F32), jax.ShapeDtypeStruct((B, cfg["n_blocks"], nq, R), F32)]
    else:
        kv_spec = pl.BlockSpec((1, MEM_LEN, LANES), lambda b, h: (b, 0, h))
        out_specs = [row_spec, kv_spec, kv_spec]
        out_shape = [jax.ShapeDtypeStruct((B, S, W), BF16)] + [jax.ShapeDtypeStruct((B, MEM_LEN, W), BF16)] * 2
    scratch = [pltpu.VMEM((nq, LANES, R), BF16), pltpu.VMEM((Sk, LANES), BF16), pltpu.VMEM((nh * S, LANES), BF16),
               pltpu.VMEM((Sk, LANES), BF16), pltpu.VMEM((nk, LANES, TK), BF16), pltpu.VMEM((nq, LANES, R), BF16),
               pltpu.VMEM((nh * S, LANES), BF16), pltpu.VMEM((nq, R), F32),
               pltpu.VMEM((Sk, LANES), F32), pltpu.VMEM((Sk, LANES), F32)]
    pair_bufs = [pltpu.VMEM((TK, R), F32), pltpu.VMEM((TK, R), F32), pltpu.VMEM((TK, R), BF16), pltpu.VMEM((TK, R), BF16)]
    scratch += pair_bufs + pair_bufs + [pltpu.VMEM((nq, LANES, R), F32)]
    if has_bias:
        scratch += [pltpu.VMEM((nh, Sk, LANES), F32), pltpu.VMEM((nh, Sk, LANES), F32), pltpu.VMEM((nq, R), F32)]
    return pl.pallas_call(
        body, name=kind + "_attn_bwd", grid=(B, cfg["n_blocks"]),
        in_specs=in_specs, out_specs=out_specs, out_shape=out_shape, scratch_shapes=scratch,
        compiler_params=_params(("arbitrary", "arbitrary")),
    )(*ins)


def _sigmoid(g):
    return 1.0 / (1.0 + jnp.exp(-g))


def out_fwd(proj, o_fox, o_dil, o_mem, w_out, x, target, gf, tm):
    T = x.shape[0]

    def body(fg_ref, dg_ref, mg_ref, of_ref, od_ref, om_ref, w_ref, x_ref, t_ref, gf_ref,
             y_ref, dx_ref, dxb_ref, sm_ref):
        parts = []
        for g_ref, o_ref in ((fg_ref, of_ref), (dg_ref, od_ref), (mg_ref, om_ref)):
            g = g_ref[...]
            parts.append((o_ref[...] * (g * _sigmoid(g))).astype(BF16))
        ymix = jnp.concatenate(parts, axis=1)
        y_ref[...] = ymix
        x2 = x_ref[...] + jnp.dot(ymix, w_ref[...], preferred_element_type=F32)
        r = lax.rsqrt(jnp.mean(x2 * x2, axis=-1, keepdims=True) + RMS_EPS)
        yn = x2 * r
        err = yn * gf_ref[...] - t_ref[...]
        loss = 0.5 * jnp.sum(jnp.sum(err * err, axis=-1, keepdims=True) / D_MODEL, axis=0, keepdims=True)
        dyf = err / D_MODEL
        dgf = jnp.sum(dyf * yn, axis=0, keepdims=True)
        dyn = dyf * gf_ref[...]
        dx2 = r * (dyn - yn * jnp.mean(dyn * yn, axis=-1, keepdims=True))
        dx_ref[...] = dx2
        dxb_ref[...] = dx2.astype(BF16)
        row = lax.broadcasted_iota(jnp.int32, (8, D_MODEL), 0)
        upd = jnp.where(row == 0, dgf, jnp.where(row == 1, loss, 0.0))

        @pl.when(pl.program_id(0) == 0)
        def _():
            sm_ref[...] = upd

        @pl.when(pl.program_id(0) != 0)
        def _():
            sm_ref[...] += upd

    def rows(w, col=0):
        return pl.BlockSpec((tm, w), lambda i: (i, col))

    return pl.pallas_call(
        body, name="out_fwd", grid=(T // tm,),
        in_specs=[rows(FOX_W, P_FG // FOX_W), rows(DIL_W, P_DG // DIL_W), rows(MEM_W, P_MG // MEM_W),
                  rows(FOX_W), rows(DIL_W), rows(MEM_W),
                  pl.BlockSpec((MIX_W, D_MODEL), lambda i: (0, 0)),
                  rows(D_MODEL), rows(D_MODEL), pl.BlockSpec((1, D_MODEL), lambda i: (0, 0))],
        out_specs=[rows(MIX_W), rows(D_MODEL), rows(D_MODEL), pl.BlockSpec((8, D_MODEL), lambda i: (0, 0))],
        out_shape=[jax.ShapeDtypeStruct((T, MIX_W), BF16), jax.ShapeDtypeStruct((T, D_MODEL), F32),
                   jax.ShapeDtypeStruct((T, D_MODEL), BF16), jax.ShapeDtypeStruct((8, D_MODEL), F32)],
        compiler_params=_params(("arbitrary",)),
    )(proj, proj, proj, o_fox, o_dil, o_mem, w_out, x, target, gf)


def out_bwd(proj, o_fox, o_dil, o_mem, w_out, dx2b, tm):
    T = dx2b.shape[0]

    def body(fg_ref, dg_ref, mg_ref, of_ref, od_ref, om_ref, w_ref, dx_ref,
             dof_ref, dod_ref, dom_ref, dfg_ref, ddg_ref, dmg_ref):
        dmix = lax.dot_general(dx_ref[...], w_ref[...], (((1,), (1,)), ((), ())), preferred_element_type=F32)
        col = 0
        for g_ref, o_ref, do_ref, dgate_ref in ((fg_ref, of_ref, dof_ref, dfg_ref), (dg_ref, od_ref, dod_ref, ddg_ref),
                                                 (mg_ref, om_ref, dom_ref, dmg_ref)):
            w = g_ref.shape[1]
            d = dmix[:, col:col + w]
            col += w
            g = g_ref[...]
            sg = _sigmoid(g)
            do_ref[...] = d * (g * sg)
            dgate_ref[...] = (d * o_ref[...] * (sg * (1.0 + g * (1.0 - sg)))).astype(BF16)

    def rows(w, col=0):
        return pl.BlockSpec((tm, w), lambda i: (i, col))

    return pl.pallas_call(
        body, name="out_bwd", grid=(T // tm,),
        in_specs=[rows(FOX_W, P_FG // FOX_W), rows(DIL_W, P_DG // DIL_W), rows(MEM_W, P_MG // MEM_W),
                  rows(FOX_W), rows(DIL_W), rows(MEM_W),
                  pl.BlockSpec((MIX_W, D_MODEL), lambda i: (0, 0)), rows(D_MODEL)],
        out_specs=[rows(FOX_W), rows(DIL_W), rows(MEM_W), rows(FOX_W), rows(DIL_W), rows(MEM_W)],
        out_shape=[jax.ShapeDtypeStruct((T, FOX_W), F32), jax.ShapeDtypeStruct((T, DIL_W), F32),
                   jax.ShapeDtypeStruct((T, MEM_W), F32), jax.ShapeDtypeStruct((T, FOX_W), BF16),
                   jax.ShapeDtypeStruct((T, DIL_W), BF16), jax.ShapeDtypeStruct((T, MEM_W), BF16)],
        compiler_params=_params(("arbitrary",)),
    )(proj, proj, proj, o_fox, o_dil, o_mem, w_out, dx2b)


def adamw(w, g, m, v, tr, name):
    lead = w.shape[:-2]
    R, C = w.shape[-2:]
    zeros = (0,) * len(lead)

    def body(w_ref, g_ref, m_ref, v_ref, d_ref, mo_ref, vo_ref):
        gv = g_ref[...]
        mn = ADAM_B1 * m_ref[...] + (1.0 - ADAM_B1) * gv
        vn = ADAM_B2 * v_ref[...] + (1.0 - ADAM_B2) * jnp.square(gv)
        m_hat = mn / (1.0 - ADAM_B1 ** ADAM_STEP)
        v_hat = vn / (1.0 - ADAM_B2 ** ADAM_STEP)
        d_ref[...] = -ADAM_LR * (m_hat / (jnp.sqrt(v_hat) + ADAM_EPS) + ADAM_WD * w_ref[...])
        mo_ref[...] = mn
        vo_ref[...] = vn

    spec = pl.BlockSpec((1,) * len(lead) + (tr, C), lambda i: zeros + (i, 0))
    return pl.pallas_call(
        body, name=name, grid=(pl.cdiv(R, tr),),
        in_specs=[spec] * 4, out_specs=[spec] * 3,
        out_shape=[jax.ShapeDtypeStruct(w.shape, F32)] * 3,
        compiler_params=_params(("arbitrary",)),
    )(w, g, m, v)


def _pad_row(v, width):
    return jnp.concatenate([v, jnp.zeros((1, width - v.shape[1]), v.dtype)], axis=1)


def local_grads(x, mem, norm_g, b_forget, mem_norm_g, final_norm_g, loss_target, w_in_p, w_kv, w_out):
    B, S, D = x.shape
    T = B * S
    xt = x.reshape(T, D)
    memt = mem.reshape(B * MEM_LEN, D)
    b_pad = _pad_row(b_forget, LANES)

    h = rms_fwd(xt, norm_g, 512, "rms_x")
    proj = mm_nn(h, w_in_p, 512, PW // 3, "in_proj")
    proj3 = proj.reshape(B, S, PW)
    mh = rms_fwd(memt, mem_norm_g, B * MEM_LEN, "rms_mem")
    mkv = mm_nn(mh, w_kv, B * MEM_LEN, 2 * MEM_W, "mem_kv_proj")
    mkv3 = mkv.reshape(B, MEM_LEN, 2 * MEM_W)

    negc = fox_gate(proj3, b_pad)
    causal = _log_masks_t(S, "causal")
    causal = jnp.concatenate([causal, jnp.zeros_like(causal)], axis=0)
    dilated = _log_masks_t(S, "dilated")
    rope = _rope_tables(S)

    o_fox, lse_fox = attn_fwd4("fox", proj3, S, negc_cols=negc, mask=causal)
    o_dil, lse_dil = attn_fwd4("dil", proj3, S, mask=dilated, rope=rope)
    o_mem, lse_mem = attn_fwd4("mem", proj3, S, kv=mkv3)

    ymix, dx2, dx2b, small_out = out_fwd(
        proj, o_fox.reshape(T, FOX_W), o_dil.reshape(T, DIL_W), o_mem.reshape(T, MEM_W), w_out,
        xt, loss_target.reshape(T, D), final_norm_g.reshape(1, D), 256)
    do_fox, do_dil, do_mem, dfg, ddg, dmg = out_bwd(
        proj, o_fox.reshape(T, FOX_W), o_dil.reshape(T, DIL_W), o_mem.reshape(T, MEM_W), w_out, dx2b, 256)
    g_out = mm_tn(ymix, dx2b, 512, D, "w_out_grad")

    dqkv_fox, dneg, drow = attn_bwd3("fox", proj3, do_fox.reshape(B, S, FOX_W), o_fox, lse_fox, S,
                                     negc_cols=negc, mask=causal)
    (dqkv_dil,) = attn_bwd3("dil", proj3, do_dil.reshape(B, S, DIL_W), o_dil, lse_dil, S, mask=dilated, rope=rope)
    dmq, dmk, dmv = attn_bwd3("mem", proj3, do_mem.reshape(B, S, MEM_W), o_mem, lse_mem, S, kv=mkv3)
    drow = drow.reshape(B, FOX_HEADS // 2, S // TQ, 2, TQ).transpose(0, 1, 3, 2, 4).reshape(B, FOX_HEADS, S)
    drow = jnp.pad(drow, ((0, 0), (0, LANES - FOX_HEADS), (0, 0)))
    dflog, db_part = fox_gate_bwd(drow, dneg, proj3, b_pad)

    groups = [[(dfg, P_FG), (ddg, P_DG), (dmg, P_MG)],
              [(dqkv_fox.reshape(T, 3 * FOX_W), P_FOX), (dflog.reshape(T, LANES), P_FLOG)],
              [(dqkv_dil.reshape(T, 3 * DIL_W), P_DIL), (dmq.reshape(T, MEM_W), P_MQ)]]
    g_in = [mm_tn_multi(h, [arr for arr, _ in grp], 512, "w_in_grad_%d" % n) for n, grp in enumerate(groups)]
    dh = mm_nt_multi([piece for grp in groups for piece in grp], w_in_p, 256, "in_proj_bwd")
    grad_x, dng = rms_bwd(xt, norm_g, dh, dx2, 512, "rms_x_bwd")

    dmkv = jnp.concatenate([dmk, dmv], axis=2).reshape(B * MEM_LEN, 2 * MEM_W)
    g_kv = mm_tn(mh, dmkv, B * MEM_LEN, 2 * MEM_W, "w_kv_grad")
    dmh = mm_nt(dmkv, w_kv, B * MEM_LEN, D, "mem_kv_bwd")
    _, dmng = rms_bwd(memt, mem_norm_g, dmh, None, B * MEM_LEN, "rms_mem_bwd")

    small = jnp.concatenate([dng[0:1], dmng[0:1], small_out[0:1], _pad_row(db_part[0:1], D), small_out[1:2],
                             jnp.zeros((3, D), F32)], axis=0)
    return grad_x.reshape(B, S, D), g_in, g_kv, g_out, small


def kernel(x, mem, norm_g, w_in, b_forget, mem_norm_g, w_mem_kv, w_out, final_norm_g, loss_target, m_norm_g, m_w_in, m_b_forget, m_mem_norm_g, m_w_mem_kv, m_w_out, m_final_norm_g, v_norm_g, v_w_in, v_b_forget, v_mem_norm_g, v_w_mem_kv, v_w_out, v_final_norm_g):
    D = D_MODEL
    w_in_full, w_kv_full, w_out_full = weight_gather(
        [_pack_cols(w_in).astype(BF16).reshape(w_in.shape[1], PW), w_mem_kv[0].astype(BF16), w_out[0].astype(BF16)])
    grad_x, g_in, g_kv, g_out, small = local_grads(
        x, mem, norm_g, b_forget, mem_norm_g, final_norm_g, loss_target, w_in_full, w_kv_full, w_out_full)

    big = list(g_in) + [g_kv, g_out]
    *from_sibling, csum = grad_exchange_d2d(big, small)
    tiles = (64, 64, 64, 128, 128)
    names = ("w_in_0", "w_in_1", "w_in_2", "w_kv", "w_out")
    chip_parts = [chip_sum(g, got, tr, "chip_sum_" + n) for g, got, tr, n in zip(big, from_sibling, tiles, names)]
    *from_chips, tot = grad_exchange_ici([cp for cp, _ in chip_parts], csum)
    gates, fox, dil, gw_kv, gw_out = [final_sum(own, got, tr, "final_sum_" + n)
                                      for (_, own), got, tr, n in zip(chip_parts, from_chips, tiles, names)]
    gw_in = _unpack_cols(jnp.concatenate(
        [fox[:, :3 * FOX_W], gates[:, :FOX_W], dil[:, :3 * DIL_W], gates[:, FOX_W:FOX_W + DIL_W], dil[:, 3 * DIL_W:],
         gates[:, FOX_W + DIL_W:], fox[:, 3 * FOX_W:]], axis=1)[None])

    loss = tot[4, 0]
    g_norm, g_mem_norm, g_final, g_b = tot[0:1], tot[1:2], tot[2], tot[3:4, :FOX_HEADS]

    def rows8(*rows):
        rows = [r.reshape(1, -1) for r in rows]
        rows = [_pad_row(r, D) for r in rows]
        return jnp.concatenate(rows + [jnp.zeros((8 - len(rows), D), F32)], axis=0)

    sw = rows8(norm_g, mem_norm_g, final_norm_g, b_forget)
    sm = rows8(m_norm_g, m_mem_norm_g, m_final_norm_g, m_b_forget)
    sv = rows8(v_norm_g, v_mem_norm_g, v_final_norm_g, v_b_forget)
    d_s, m_s, v_s = adamw(sw, tot, sm, sv, 8, "adamw_small")
    d_in, m_in, v_in = adamw(w_in, gw_in, m_w_in, v_w_in, 32, "adamw_w_in")
    d_kv, m_kv, v_kv = adamw(w_mem_kv[0], gw_kv, m_w_mem_kv[0], v_w_mem_kv[0], 128, "adamw_w_kv")
    d_out, m_out, v_out = adamw(w_out[0], gw_out, m_w_out[0], v_w_out[0], 256, "adamw_w_out")

    def small_outs(t):
        return t[0:1], t[3:4, :FOX_HEADS], t[1:2], t[2]

    grads = (g_norm, gw_in, g_b, g_mem_norm, gw_kv[None], gw_out[None], g_final)
    outs = []
    for t, big in ((d_s, (d_in, d_kv, d_out)), (m_s, (m_in, m_kv, m_out)), (v_s, (v_in, v_kv, v_out))):
        n, b, mn, f = small_outs(t)
        outs += [n, big[0], b, mn, big[1][None], big[2][None], f]
    return (loss, grad_x, *grads, *outs)
```

```python
import functools
import math

import numpy as np
import jax
import jax.numpy as jnp
from jax import lax
from jax.experimental import pallas as pl
from jax.experimental.pallas import tpu as pltpu

F32 = jnp.float32
BF16 = jnp.bfloat16

D_MODEL = 1024
HEAD_DIM = 64
FOX_HEADS = 12
DIL_HEADS = 12
MEM_HEADS = 4
MEM_HEAD_DIM = 128
MEM_LEN = 256
FOX_W = FOX_HEADS * HEAD_DIM
DIL_W = DIL_HEADS * HEAD_DIM
MEM_W = MEM_HEADS * MEM_HEAD_DIM
MIX_W = FOX_W + DIL_W + MEM_W
DILATIONS = ((128, 1), (512, 4), (2048, 16))
ROPE_THETA = 500000.0
ROPE_DIM = HEAD_DIM // 4
RMS_EPS = 1e-6
NEG_INF = -1e30
IN_W = 4 * FOX_W + FOX_HEADS + 4 * DIL_W + 2 * MEM_W

ADAM_LR = 0.001
ADAM_B1 = 0.9
ADAM_B2 = 0.999
ADAM_EPS = 1e-08
ADAM_WD = 0.01
ADAM_STEP = 10

N_DEV = 8
LANES = 128
PAIR_W = 3 * LANES
TQ = 256
TK = 256

O_FQ, O_FK, O_FV, O_FG = 0, FOX_W, 2 * FOX_W, 3 * FOX_W
O_FLOG = 4 * FOX_W
O_DQ = O_FLOG + FOX_HEADS
O_DK, O_DV, O_DG = O_DQ + DIL_W, O_DQ + 2 * DIL_W, O_DQ + 3 * DIL_W
O_MQ = O_DQ + 4 * DIL_W
O_MG = O_MQ + MEM_W
P_FOX = 0
P_FG = P_FOX + 3 * FOX_W
P_DIL = P_FG + FOX_W
P_DG = P_DIL + 3 * DIL_W
P_MQ = P_DG + DIL_W
P_MG = P_MQ + MEM_W
P_FLOG = P_MG + MEM_W
PW = P_FLOG + LANES

VMEM_LIMIT = 56 * 1024 * 1024


def _pack_pieces():
    pieces = []
    for base in (O_FQ, O_DQ):
        seg = []
        for hp in range(FOX_HEADS // 2):
            for part in range(3):
                seg.append((base + part * FOX_W + hp * LANES, LANES))
        pieces.append(seg)
    fox, dil = pieces
    return fox + [(O_FG, FOX_W)] + dil + [(O_DG, DIL_W), (O_MQ, MEM_W), (O_MG, MEM_W), (O_FLOG, FOX_HEADS)]


def _pack_cols(w):
    parts = [w[..., s:s + n] for s, n in _pack_pieces()]
    parts.append(jnp.zeros(w.shape[:-1] + (LANES - FOX_HEADS,), w.dtype))
    return jnp.concatenate(parts, axis=-1)


def _unpack_cols(g):
    runs = []
    pos = 0
    for s, n in _pack_pieces():
        runs.append((s, n, pos))
        pos += n
    runs.sort()
    return jnp.concatenate([g[..., p:p + n] for s, n, p in runs], axis=-1)


def _params(sem=None, **kw):
    return pltpu.CompilerParams(dimension_semantics=sem, vmem_limit_bytes=VMEM_LIMIT, **kw)


def _mesh_pos():
    return lax.axis_index("x"), lax.axis_index("y"), lax.axis_index("c")


def _flip(v, d):
    return 1 - v if d else v


_RELATIONS = [(dx, dy, dc) for dx in (0, 1) for dy in (0, 1) for dc in (0, 1)][1:]


def weight_gather(shards):
    n_arr = len(shards)
    rows = [s.shape[0] for s in shards]

    def body(*refs):
        in_refs = refs[:n_arr]
        out_refs = refs[n_arr:2 * n_arr]
        send_sems, recv_sems, local_sems = refs[2 * n_arr:]
        x, y, c = _mesh_pos()
        me, sibling = (x, y, c), (x, y, 1 - c)
        x_nbr, y_nbr, diag = (1 - x, y, c), (x, 1 - y, c), (1 - x, 1 - y, c)
        north = c == 1
        relay_from = (jnp.where(north, 1 - x, x), jnp.where(north, y, 1 - y), c)
        relay_to = (jnp.where(north, x, 1 - x), jnp.where(north, 1 - y, y), c)
        k_from = jnp.where(north, 1, 2)
        k_to = 3 - k_from

        def block(a, pos):
            px, py, pc = pos
            return out_refs[a].at[pl.ds((4 * px + 2 * py + pc) * rows[a], rows[a]), :]

        def copy(a, k, blk, to, src=None):
            return pltpu.make_async_remote_copy(
                src_ref=block(a, blk) if src is None else src, dst_ref=block(a, blk),
                send_sem=send_sems.at[a, k], recv_sem=recv_sems.at[a, k],
                device_id=to, device_id_type=pl.DeviceIdType.MESH)

        started = []
        mine = []
        for a in range(n_arr):
            cp = pltpu.make_async_copy(in_refs[a], block(a, me), local_sems.at[a])
            cp.start()
            mine.append(cp)
            first = [copy(a, 0, me, sibling, src=in_refs[a]), copy(a, 1, me, x_nbr, src=in_refs[a]),
                     copy(a, 2, me, y_nbr, src=in_refs[a])]
            for cp in first:
                cp.start()
            started += first
        for a in range(n_arr):
            copy(a, k_from, relay_from, me).wait_recv()
            second_hop = copy(a, 3, relay_from, relay_to)
            second_hop.start()
            passed = copy(a, 3 + k_from, relay_from, sibling)
            passed.start()
            started += [second_hop, passed]
        for a in range(n_arr):
            copy(a, k_to, relay_to, me).wait_recv()
            passed = copy(a, 3 + k_to, relay_to, sibling)
            passed.start()
            started.append(passed)
        for a in range(n_arr):
            copy(a, 3, diag, me).wait_recv()
            passed = copy(a, 6, diag, sibling)
            passed.start()
            started.append(passed)
        for a in range(n_arr):
            copy(a, 0, sibling, me).wait_recv()
            for k, chip in ((4, x_nbr), (5, y_nbr), (6, diag)):
                copy(a, k, (chip[0], chip[1], 1 - c), me).wait_recv()
        for cp in started:
            cp.wait_send()
        for cp in mine:
            cp.wait()

    any_spec = pl.BlockSpec(memory_space=pl.ANY)
    return pl.pallas_call(
        body, name="weight_gather",
        out_shape=[jax.ShapeDtypeStruct((N_DEV * s.shape[0], s.shape[1]), s.dtype) for s in shards],
        in_specs=[any_spec] * n_arr, out_specs=[any_spec] * n_arr,
        scratch_shapes=[pltpu.SemaphoreType.DMA((n_arr, 7)), pltpu.SemaphoreType.DMA((n_arr, 7)),
                        pltpu.SemaphoreType.DMA((n_arr,))],
    )(*shards)


def grad_exchange(grads, small):
    arrs = list(grads) + [small]
    n_arr = len(arrs)
    rows = [g.shape[0] // N_DEV for g in grads] + [small.shape[0]]

    def body(*refs):
        in_refs = refs[:n_arr]
        out_refs = refs[n_arr:2 * n_arr]
        send_sems, recv_sems, local_sems = refs[2 * n_arr:]
        x, y, c = _mesh_pos()
        me = 4 * x + 2 * y + c

        def src(a, idx):
            if a == n_arr - 1:
                return in_refs[a]
            return in_refs[a].at[pl.ds(idx * rows[a], rows[a]), :]

        def copy(a, k):
            dx, dy, dc = _RELATIONS[k]
            px, py, pc = _flip(x, dx), _flip(y, dy), _flip(c, dc)
            peer = 4 * px + 2 * py + pc
            send = pltpu.make_async_remote_copy(
                src_ref=src(a, peer), dst_ref=out_refs[a].at[me],
                send_sem=send_sems.at[a, k], recv_sem=recv_sems.at[a, k],
                device_id=(px, py, pc), device_id_type=pl.DeviceIdType.MESH)
            recv = pltpu.make_async_remote_copy(
                src_ref=src(a, peer), dst_ref=out_refs[a].at[peer],
                send_sem=send_sems.at[a, k], recv_sem=recv_sems.at[a, k],
                device_id=(px, py, pc), device_id_type=pl.DeviceIdType.MESH)
            return send, recv

        mine = []
        pairs = []
        for a in range(n_arr):
            cp = pltpu.make_async_copy(src(a, me), out_refs[a].at[me], local_sems.at[a])
            cp.start()
            mine.append(cp)
            for k in range(7):
                send, recv = copy(a, k)
                send.start()
                pairs.append((send, recv))
        for send, recv in pairs:
            recv.wait_recv()
        for send, recv in pairs:
            send.wait_send()
        for cp in mine:
            cp.wait()

    any_spec = pl.BlockSpec(memory_space=pl.ANY)
    return pl.pallas_call(
        body, name="grad_exchange",
        out_shape=[jax.ShapeDtypeStruct((N_DEV, r, a.shape[1]), a.dtype) for r, a in zip(rows, arrs)],
        in_specs=[any_spec] * n_arr, out_specs=[any_spec] * n_arr,
        scratch_shapes=[pltpu.SemaphoreType.DMA((n_arr, 7)), pltpu.SemaphoreType.DMA((n_arr, 7)),
                        pltpu.SemaphoreType.DMA((n_arr,))],
    )(*arrs)


def slot_sum(slots, tr, name):
    _, R, C = slots.shape

    def body(s_ref, o_ref):
        acc = s_ref[0]
        for d in range(1, N_DEV):
            acc = acc + s_ref[d]
        o_ref[...] = acc

    return pl.pallas_call(
        body, name=name, grid=(R // tr,),
        in_specs=[pl.BlockSpec((N_DEV, tr, C), lambda i: (0, i, 0))],
        out_specs=pl.BlockSpec((tr, C), lambda i: (i, 0)),
        out_shape=jax.ShapeDtypeStruct((R, C), slots.dtype),
        compiler_params=_params(("arbitrary",)),
    )(slots)


N_CHIP = 4
_OTHER_CHIPS = [(1, 0), (0, 1), (1, 1)]


def grad_exchange_d2d(grads, small):
    n_big = len(grads)
    rows = [g.shape[0] // N_DEV for g in grads]

    def body(*refs):
        g_refs = refs[:n_big]
        small_ref = refs[n_big]
        out_refs = refs[n_big + 1:2 * n_big + 1]
        csum_ref = refs[2 * n_big + 1]
        land, send_sems, recv_sems = refs[2 * n_big + 2:]
        x, y, c = _mesh_pos()
        sibling = (x, y, 1 - c)
        copies = []
        for a in range(n_big):
            for q in range(N_CHIP):
                copies.append(pltpu.make_async_remote_copy(
                    src_ref=g_refs[a].at[pl.ds((2 * q + 1 - c) * rows[a], rows[a]), :], dst_ref=out_refs[a].at[q],
                    send_sem=send_sems.at[a, q], recv_sem=recv_sems.at[a, q],
                    device_id=sibling, device_id_type=pl.DeviceIdType.MESH))
        copies.append(pltpu.make_async_remote_copy(
            src_ref=small_ref, dst_ref=land, send_sem=send_sems.at[n_big, 0], recv_sem=recv_sems.at[n_big, 0],
            device_id=sibling, device_id_type=pl.DeviceIdType.MESH))
        for cp in copies:
            cp.start()
        for cp in copies:
            cp.wait_recv()
        for cp in copies:
            cp.wait_send()
        csum_ref[...] = small_ref[...] + land[...]

    any_spec = pl.BlockSpec(memory_space=pl.ANY)
    vmem_spec = pl.BlockSpec(memory_space=pltpu.VMEM)
    return pl.pallas_call(
        body, name="grad_exchange_d2d",
        out_shape=[jax.ShapeDtypeStruct((N_CHIP, r, g.shape[1]), g.dtype) for r, g in zip(rows, grads)]
        + [jax.ShapeDtypeStruct(small.shape, small.dtype)],
        in_specs=[any_spec] * n_big + [vmem_spec], out_specs=[any_spec] * n_big + [vmem_spec],
        scratch_shapes=[pltpu.VMEM(small.shape, small.dtype),
                        pltpu.SemaphoreType.DMA((n_big + 1, N_CHIP)), pltpu.SemaphoreType.DMA((n_big + 1, N_CHIP))],
    )(*grads, small)


def chip_sum(g, got, tr, name):
    _, rows, cols = got.shape
    g4 = g.reshape(N_CHIP, 2, rows, cols)
    core = lax.axis_index("c").astype(jnp.int32).reshape(1)

    def body(c_ref, g_ref, r_ref, o_ref):
        o_ref[0] = (g_ref[0, 0] + r_ref[0]).astype(BF16)

    return pl.pallas_call(
        body, name=name,
        grid_spec=pltpu.PrefetchScalarGridSpec(
            num_scalar_prefetch=1, grid=(N_CHIP, rows // tr),
            in_specs=[pl.BlockSpec((1, 1, tr, cols), lambda q, i, w: (q, w[0], i, 0)),
                      pl.BlockSpec((1, tr, cols), lambda q, i, w: (q, i, 0))],
            out_specs=pl.BlockSpec((1, tr, cols), lambda q, i, w: (q, i, 0))),
        out_shape=jax.ShapeDtypeStruct((N_CHIP, rows, cols), BF16),
        compiler_params=_params(("arbitrary", "arbitrary")),
    )(core, g4, got)


def grad_exchange_ici(parts, csum):
    n_big = len(parts)

    def body(*refs):
        p_refs = refs[:n_big]
        csum_ref = refs[n_big]
        out_refs = refs[n_big + 1:2 * n_big + 1]
        tot_ref = refs[2 * n_big + 1]
        land, send_sems, recv_sems, local_sems = refs[2 * n_big + 2:]
        x, y, c = _mesh_pos()
        q_me = 2 * x + y
        land[q_me] = csum_ref[...]
        mine = [pltpu.make_async_copy(p_refs[a].at[q_me], out_refs[a].at[q_me], local_sems.at[a]) for a in range(n_big)]
        for cp in mine:
            cp.start()
        sends, recvs = [], []
        for j, (dx, dy) in enumerate(_OTHER_CHIPS):
            px, py = _flip(x, dx), _flip(y, dy)
            q_peer = 2 * px + py
            for a in range(n_big + 1):
                src = p_refs[a].at[q_peer] if a < n_big else csum_ref
                dst = out_refs[a] if a < n_big else land
                common = dict(send_sem=send_sems.at[a, j], recv_sem=recv_sems.at[a, j],
                              device_id=(px, py, c), device_id_type=pl.DeviceIdType.MESH)
                sends.append(pltpu.make_async_remote_copy(src_ref=src, dst_ref=dst.at[q_me], **common))
                recvs.append(pltpu.make_async_remote_copy(src_ref=src, dst_ref=dst.at[q_peer], **common))
        for cp in sends:
            cp.start()
        for cp in recvs:
            cp.wait_recv()
        for cp in sends:
            cp.wait_send()
        for cp in mine:
            cp.wait()
        tot = land[0]
        for q in range(1, N_CHIP):
            tot = tot + land[q]
        tot_ref[...] = tot

    any_spec = pl.BlockSpec(memory_space=pl.ANY)
    vmem_spec = pl.BlockSpec(memory_space=pltpu.VMEM)
    return pl.pallas_call(
        body, name="grad_exchange_ici",
        out_shape=[jax.ShapeDtypeStruct(p.shape, p.dtype) for p in parts] + [jax.ShapeDtypeStruct(csum.shape, csum.dtype)],
        in_specs=[any_spec] * n_big + [vmem_spec], out_specs=[any_spec] * n_big + [vmem_spec],
        scratch_shapes=[pltpu.VMEM((N_CHIP,) + csum.shape, csum.dtype),
                        pltpu.SemaphoreType.DMA((n_big + 1, 3)), pltpu.SemaphoreType.DMA((n_big + 1, 3)),
                        pltpu.SemaphoreType.DMA((n_big,))],
    )(*parts, csum)


def final_sum(got, tr, name):
    _, rows, cols = got.shape

    def body(got_ref, o_ref):
        acc = got_ref[0].astype(F32)
        for q in range(1, N_CHIP):
            acc = acc + got_ref[q].astype(F32)
        o_ref[...] = acc

    return pl.pallas_call(
        body, name=name, grid=(rows // tr,),
        in_specs=[pl.BlockSpec((N_CHIP, tr, cols), lambda i: (0, i, 0))],
        out_specs=pl.BlockSpec((tr, cols), lambda i: (i, 0)),
        out_shape=jax.ShapeDtypeStruct((rows, cols), F32),
        compiler_params=_params(("arbitrary",)),
    )(got)


def rms_fwd(x, g, tm, name):
    M, K = x.shape

    def body(x_ref, g_ref, o_ref):
        xv = x_ref[...]
        r = lax.rsqrt(jnp.mean(xv * xv, axis=-1, keepdims=True) + RMS_EPS)
        o_ref[...] = ((xv * r) * g_ref[...]).astype(BF16)

    return pl.pallas_call(
        body, name=name, grid=(M // tm,),
        in_specs=[pl.BlockSpec((tm, K), lambda i: (i, 0)), pl.BlockSpec((1, K), lambda i: (0, 0))],
        out_specs=pl.BlockSpec((tm, K), lambda i: (i, 0)),
        out_shape=jax.ShapeDtypeStruct((M, K), BF16),
        compiler_params=_params(("arbitrary",)),
    )(x, g)


def rms_bwd(x, g, dh, dres, tm, name):
    M, K = x.shape
    has_res = dres is not None

    def body(*refs):
        if has_res:
            x_ref, g_ref, dh_ref, dres_ref, dx_ref, dg_ref = refs
        else:
            x_ref, g_ref, dh_ref, dx_ref, dg_ref = refs
        xv = x_ref[...]
        r = lax.rsqrt(jnp.mean(xv * xv, axis=-1, keepdims=True) + RMS_EPS)
        xn = xv * r
        dhv = dh_ref[...]
        dxn = dhv * g_ref[...]
        dx = r * (dxn - xn * jnp.mean(dxn * xn, axis=-1, keepdims=True))
        if has_res:
            dx = dx + dres_ref[...]
        dx_ref[...] = dx
        part = jnp.sum(dhv * xn, axis=0, keepdims=True)
        row = lax.broadcasted_iota(jnp.int32, (8, K), 0)
        upd = jnp.where(row == 0, part, 0.0)

        @pl.when(pl.program_id(0) == 0)
        def _():
            dg_ref[...] = upd

        @pl.when(pl.program_id(0) != 0)
        def _():
            dg_ref[...] += upd

    row_spec = pl.BlockSpec((tm, K), lambda i: (i, 0))
    ins = [x, g, dh] + ([dres] if has_res else [])
    in_specs = [row_spec, pl.BlockSpec((1, K), lambda i: (0, 0)), row_spec] + ([row_spec] if has_res else [])
    return pl.pallas_call(
        body, name=name, grid=(M // tm,),
        in_specs=in_specs,
        out_specs=[row_spec, pl.BlockSpec((8, K), lambda i: (0, 0))],
        out_shape=[jax.ShapeDtypeStruct((M, K), F32), jax.ShapeDtypeStruct((8, K), F32)],
        compiler_params=_params(("arbitrary",)),
    )(*ins)


def mm_nn(a, b, tm, tn, name):
    M, K = a.shape
    N = b.shape[1]

    def body(a_ref, b_ref, o_ref):
        o_ref[...] = jnp.dot(a_ref[...], b_ref[...], preferred_element_type=F32)

    return pl.pallas_call(
        body, name=name, grid=(N // tn, M // tm),
        in_specs=[pl.BlockSpec((tm, K), lambda j, i: (i, 0)), pl.BlockSpec((K, tn), lambda j, i: (0, j))],
        out_specs=pl.BlockSpec((tm, tn), lambda j, i: (i, j)),
        out_shape=jax.ShapeDtypeStruct((M, N), F32),
        compiler_params=_params(("arbitrary", "arbitrary")),
    )(a, b)


def mm_nt(a, b, tm, tk, name):
    M, K = a.shape
    N = b.shape[0]

    def body(a_ref, b_ref, o_ref):
        part = lax.dot_general(a_ref[...], b_ref[...], (((1,), (1,)), ((), ())), preferred_element_type=F32)

        @pl.when(pl.program_id(1) == 0)
        def _():
            o_ref[...] = part

        @pl.when(pl.program_id(1) != 0)
        def _():
            o_ref[...] += part

    return pl.pallas_call(
        body, name=name, grid=(M // tm, K // tk),
        in_specs=[pl.BlockSpec((tm, tk), lambda i, k: (i, k)), pl.BlockSpec((N, tk), lambda i, k: (0, k))],
        out_specs=pl.BlockSpec((tm, N), lambda i, k: (i, 0)),
        out_shape=jax.ShapeDtypeStruct((M, N), F32),
        compiler_params=_params(("arbitrary", "arbitrary")),
    )(a, b)


def mm_tn(a, b, tt, tn, name):
    T, K = a.shape
    N = b.shape[1]

    def body(a_ref, b_ref, o_ref):
        part = lax.dot_general(a_ref[...], b_ref[...], (((0,), (0,)), ((), ())), preferred_element_type=F32)

        @pl.when(pl.program_id(1) == 0)
        def _():
            o_ref[...] = part

        @pl.when(pl.program_id(1) != 0)
        def _():
            o_ref[...] += part

    return pl.pallas_call(
        body, name=name, grid=(N // tn, T // tt),
        in_specs=[pl.BlockSpec((tt, K), lambda j, t: (t, 0)), pl.BlockSpec((tt, tn), lambda j, t: (t, j))],
        out_specs=pl.BlockSpec((K, tn), lambda j, t: (0, j)),
        out_shape=jax.ShapeDtypeStruct((K, N), F32),
        compiler_params=_params(("arbitrary", "arbitrary")),
    )(a, b)


def mm_tn_multi(a, bs, tt, name):
    T, K = a.shape
    widths = [b.shape[1] for b in bs]

    def body(a_ref, *rest):
        b_refs, o_ref = rest[:-1], rest[-1]
        av = a_ref[...]
        parts = [lax.dot_general(av, b_ref[...], (((0,), (0,)), ((), ())), preferred_element_type=F32)
                 for b_ref in b_refs]

        @pl.when(pl.program_id(0) == 0)
        def _():
            col = 0
            for part, w in zip(parts, widths):
                o_ref[:, col:col + w] = part
                col += w

        @pl.when(pl.program_id(0) != 0)
        def _():
            col = 0
            for part, w in zip(parts, widths):
                o_ref[:, col:col + w] += part
                col += w

    return pl.pallas_call(
        body, name=name, grid=(T // tt,),
        in_specs=[pl.BlockSpec((tt, K), lambda t: (t, 0))] + [pl.BlockSpec((tt, w), lambda t: (t, 0)) for w in widths],
        out_specs=pl.BlockSpec((K, sum(widths)), lambda t: (0, 0)),
        out_shape=jax.ShapeDtypeStruct((K, sum(widths)), F32),
        compiler_params=_params(("arbitrary",)),
    )(a, *bs)


def mm_nt_multi(pieces, w, tm, name):
    M = pieces[0][0].shape[0]
    N, K = w.shape

    def body(*refs):
        p_refs, w_ref, o_ref = refs[:-2], refs[-2], refs[-1]
        acc = None
        for p_ref, (arr, col) in zip(p_refs, pieces):
            part = lax.dot_general(p_ref[...], w_ref[:, col:col + arr.shape[1]], (((1,), (1,)), ((), ())),
                                   preferred_element_type=F32)
            acc = part if acc is None else acc + part
        o_ref[...] = acc

    return pl.pallas_call(
        body, name=name, grid=(M // tm,),
        in_specs=[pl.BlockSpec((tm, arr.shape[1]), lambda i: (i, 0)) for arr, _ in pieces]
        + [pl.BlockSpec((N, K), lambda i: (0, 0))],
        out_specs=pl.BlockSpec((tm, N), lambda i: (i, 0)),
        out_shape=jax.ShapeDtypeStruct((M, N), F32),
        compiler_params=_params(("arbitrary",)),
    )(*[arr for arr, _ in pieces], w)


def in_proj_bwd_rms(pieces, w, x, g, dres, tm):
    M, N = x.shape

    def body(*refs):
        n = len(pieces)
        p_refs, w_ref, x_ref, g_ref, dres_ref, dx_ref, dg_ref = refs[:n], *refs[n:]
        dh = None
        for p_ref, (arr, col) in zip(p_refs, pieces):
            part = lax.dot_general(p_ref[...], w_ref[:, col:col + arr.shape[1]], (((1,), (1,)), ((), ())),
                                   preferred_element_type=F32)
            dh = part if dh is None else dh + part
        xv = x_ref[...]
        r = lax.rsqrt(jnp.mean(xv * xv, axis=-1, keepdims=True) + RMS_EPS)
        xn = xv * r
        dxn = dh * g_ref[...]
        dx_ref[...] = r * (dxn - xn * jnp.mean(dxn * xn, axis=-1, keepdims=True)) + dres_ref[...]
        row = lax.broadcasted_iota(jnp.int32, (8, N), 0)
        upd = jnp.where(row == 0, jnp.sum(dh * xn, axis=0, keepdims=True), 0.0)

        @pl.when(pl.program_id(0) == 0)
        def _():
            dg_ref[...] = upd

        @pl.when(pl.program_id(0) != 0)
        def _():
            dg_ref[...] += upd

    row_spec = pl.BlockSpec((tm, N), lambda i: (i, 0))
    return pl.pallas_call(
        body, name="in_proj_bwd", grid=(M // tm,),
        in_specs=[pl.BlockSpec((tm, arr.shape[1]), lambda i: (i, 0)) for arr, _ in pieces]
        + [pl.BlockSpec(w.shape, lambda i: (0, 0)), row_spec, pl.BlockSpec((1, N), lambda i: (0, 0)), row_spec],
        out_specs=[row_spec, pl.BlockSpec((8, N), lambda i: (0, 0))],
        out_shape=[jax.ShapeDtypeStruct((M, N), F32), jax.ShapeDtypeStruct((8, N), F32)],
        compiler_params=_params(("arbitrary",)),
    )(*[arr for arr, _ in pieces], w, x, g, dres)


def _log_sigmoid(z):
    return jnp.minimum(z, 0.0) - jnp.log(1.0 + jnp.exp(-jnp.abs(z)))


def _tri(n, lower):
    r = lax.broadcasted_iota(jnp.int32, (n, n), 0)
    c = lax.broadcasted_iota(jnp.int32, (n, n), 1)
    return jnp.where((r >= c) if lower else (r <= c), 1.0, 0.0).astype(F32)


def fox_gate(proj3, b_pad):
    B, S, _ = proj3.shape
    nblk = S // TK

    def body(f_ref, b_ref, o_ref):
        tri = _tri(TK, True)
        carry = jnp.zeros((1, LANES), F32)
        for n in range(nblk):
            z = f_ref[0, n * TK:(n + 1) * TK, :] + b_ref[...]
            logf = _log_sigmoid(z)
            cs = jnp.dot(tri, logf, preferred_element_type=F32, precision=lax.Precision.HIGHEST) + carry
            carry = cs[TK - 1:TK, :]
            o_ref[0, n * TK:(n + 1) * TK, :] = -cs

    return pl.pallas_call(
        body, name="fox_gate", grid=(B,),
        in_specs=[pl.BlockSpec((1, S, LANES), lambda b: (b, 0, P_FLOG // LANES)),
                  pl.BlockSpec((1, LANES), lambda b: (0, 0))],
        out_specs=pl.BlockSpec((1, S, LANES), lambda b: (b, 0, 0)),
        out_shape=jax.ShapeDtypeStruct((B, S, LANES), F32),
        compiler_params=_params(("arbitrary",)),
    )(proj3, b_pad)


def fox_gate_bwd(drow, dneg, proj3, b_pad):
    B, S, _ = proj3.shape
    nblk = S // TK

    def body(d_ref, r_ref, f_ref, b_ref, o_ref, db_ref):
        tri = _tri(TK, False)
        lane = lax.broadcasted_iota(jnp.int32, (TK, LANES), 1)
        carry = jnp.zeros((1, LANES), F32)
        dbsum = jnp.zeros((1, LANES), F32)
        for n in reversed(range(nblk)):
            dk_side = None
            for hp in range(FOX_HEADS // 2):
                two = jnp.where(lane < 2, r_ref[0, n * TK:(n + 1) * TK, hp * LANES:(hp + 1) * LANES], 0.0)
                two = pltpu.roll(two, 2 * hp, 1) if hp else two
                dk_side = two if dk_side is None else dk_side + two
            dc = jnp.where(lane < FOX_HEADS, d_ref[0, :, n * TK:(n + 1) * TK].T - dk_side, 0.0)
            rs = jnp.dot(tri, dc, preferred_element_type=F32, precision=lax.Precision.HIGHEST) + carry
            carry = rs[0:1, :]
            z = f_ref[0, n * TK:(n + 1) * TK, :] + b_ref[...]
            dz = rs * (1.0 / (1.0 + jnp.exp(z)))
            o_ref[0, n * TK:(n + 1) * TK, :] = dz.astype(BF16)
            dbsum = dbsum + jnp.sum(dz, axis=0, keepdims=True)
        row = lax.broadcasted_iota(jnp.int32, (8, LANES), 0)
        upd = jnp.where(row == 0, dbsum, 0.0)

        @pl.when(pl.program_id(0) == 0)
        def _():
            db_ref[...] = upd

        @pl.when(pl.program_id(0) != 0)
        def _():
            db_ref[...] += upd

    return pl.pallas_call(
        body, name="fox_gate_bwd", grid=(B,),
        in_specs=[pl.BlockSpec((1, LANES, S), lambda b: (b, 0, 0)),
                  pl.BlockSpec((1, S, FOX_W), lambda b: (b, 0, 0)),
                  pl.BlockSpec((1, S, LANES), lambda b: (b, 0, P_FLOG // LANES)),
                  pl.BlockSpec((1, LANES), lambda b: (0, 0))],
        out_specs=[pl.BlockSpec((1, S, LANES), lambda b: (b, 0, 0)), pl.BlockSpec((8, LANES), lambda b: (0, 0))],
        out_shape=[jax.ShapeDtypeStruct((B, S, LANES), BF16), jax.ShapeDtypeStruct((8, LANES), F32)],
        compiler_params=_params(("arbitrary",)),
    )(drow, dneg, proj3, b_pad)


def _mult_masks(S, kind):
    nd = S // TQ
    a = np.arange(TQ)[:, None]
    b = np.arange(TK)[None, :]
    out = np.zeros((nd, TQ, TK), np.float32)
    for d in range(nd):
        delta = d * TQ + a - b
        if kind == "causal":
            out[d] = delta >= 0
        else:
            m = np.zeros((TQ, TK), np.float32)
            for w, dil in DILATIONS:
                m += (delta >= 0) & (delta % dil == 0) & (delta <= w)
            out[d] = m
    return jnp.asarray(out)


def _rope_tables(S):
    half = ROPE_DIM // 2
    f32 = np.float32
    pos = np.arange(S, dtype=f32)
    inv_freq = f32(1.0) / np.power(f32(ROPE_THETA), np.arange(0, ROPE_DIM, 2, dtype=f32) / f32(ROPE_DIM)).astype(f32)
    ang = (pos[:, None] * inv_freq[None, :]).astype(f32).astype(np.float64)
    cos, sin = np.cos(ang).astype(f32), np.sin(ang).astype(f32)
    one = np.ones((S, HEAD_DIM - ROPE_DIM), f32)
    zero = np.zeros((S, HEAD_DIM - ROPE_DIM), f32)
    zh = np.zeros((S, half), f32)
    c = np.concatenate([cos, cos, one], axis=1)
    s1 = np.concatenate([-sin, zh, zero], axis=1)
    s2 = np.concatenate([zh, sin, zero], axis=1)
    return tuple(jnp.asarray(np.concatenate([t, t], axis=1)) for t in (c, s1, s2))


def _rope(t, c, s1, s2):
    return t * c + pltpu.roll(t, LANES - half_rope(), 1) * s1 + pltpu.roll(t, half_rope(), 1) * s2


def half_rope():
    return ROPE_DIM // 2


def _rope_bwd(d, c, s1, s2):
    return d * c + pltpu.roll(d * s1, half_rope(), 1) + pltpu.roll(d * s2, LANES - half_rope(), 1)


def _scale_parts(scale):
    m, _ = math.frexp(scale)
    return (scale, None) if m == 0.5 else (None, scale)


def attn_fwd(kind, src, S, *, negc=None, mask=None, rope=None, kv=None):
    B = src.shape[0]
    pair = kind != "mem"
    col0 = {"fox": P_FOX, "dil": P_DIL, "mem": P_MQ}[kind]
    n_blocks = FOX_HEADS // 2 if pair else MEM_HEADS
    e_dim = HEAD_DIM if pair else MEM_HEAD_DIM
    q_fold, s_scale = _scale_parts(1.0 / math.sqrt(e_dim))
    Sk = S if pair else MEM_LEN
    nh = 2 if pair else 1
    has_bias = negc is not None
    has_rope = rope is not None
    nq = S // TQ

    def body(*refs):
        refs = list(refs)
        if pair:
            qkv_ref = refs.pop(0)
        else:
            q_ref, k_ref, v_ref = refs.pop(0), refs.pop(0), refs.pop(0)
        negc_ref = refs.pop(0) if has_bias else None
        mask_ref = refs.pop(0) if pair else None
        rope_refs = [refs.pop(0) for _ in range(3)] if has_rope else None
        o_ref, lse_ref, qs, ks, vs = refs
        lane = lax.broadcasted_iota(jnp.int32, (1, LANES), 1)

        def prep_q(n, _):
            r0 = pl.multiple_of(n * TQ, TQ)
            rows = pl.ds(r0, TQ)
            q = qkv_ref[0, rows, 0:LANES] if pair else q_ref[0, rows, :]
            if has_rope:
                q = _rope(q, *[t[rows, :] for t in rope_refs])
            if q_fold is not None:
                q = q * q_fold
            qs[rows, :] = q.astype(BF16)
            return 0

        def prep_kv(n, _):
            r0 = pl.multiple_of(n * TK, TK)
            rows = pl.ds(r0, TK)
            k = qkv_ref[0, rows, LANES:2 * LANES] if pair else k_ref[0, rows, :]
            v = qkv_ref[0, rows, 2 * LANES:3 * LANES] if pair else v_ref[0, rows, :]
            if has_rope:
                k = _rope(k, *[t[rows, :] for t in rope_refs])
            ks[rows, :] = k.astype(BF16)
            vs[rows, :] = v.astype(BF16)
            return 0

        lax.fori_loop(0, nq, prep_q, 0)
        lax.fori_loop(0, Sk // TK, prep_kv, 0)

        def q_loop(i, _):
            r0 = pl.multiple_of(i * TQ, TQ)
            q = qs[pl.ds(r0, TQ), :]
            res = []
            for hh in range(nh):
                hmask = (lane >= HEAD_DIM * hh) & (lane < HEAD_DIM * (hh + 1))
                qh = jnp.where(hmask, q, jnp.zeros_like(q)) if pair else q

                def kv_loop(j, carry, qh=qh, hh=hh):
                    m, l, acc = carry
                    c0 = pl.multiple_of(j * TK, TK)
                    k = ks[pl.ds(c0, TK), :]
                    v = vs[pl.ds(c0, TK), :]
                    s = lax.dot_general(qh, k, (((1,), (1,)), ((), ())), preferred_element_type=F32)
                    if s_scale is not None:
                        s = s * s_scale
                    if has_bias:
                        s = s + negc_ref[0, 0, pl.ds(hh, 1), pl.ds(c0, TK)]
                    if pair:
                        mult = mask_ref[i - j]
                        s = jnp.where(mult > 0.0, s, NEG_INF)
                    m_new = jnp.maximum(m, jnp.max(s, axis=1, keepdims=True))
                    p = jnp.exp(s - m_new)
                    if pair:
                        p = p * mult
                    alpha = jnp.exp(m - m_new)
                    l = alpha * l + jnp.sum(p, axis=1, keepdims=True)
                    acc = acc * alpha + jnp.dot(p.astype(BF16), v, preferred_element_type=F32)
                    return m_new, l, acc

                init = (jnp.full((TQ, 1), NEG_INF, F32), jnp.zeros((TQ, 1), F32), jnp.zeros((TQ, LANES), F32))
                m, l, acc = lax.fori_loop(0, (i + 1) if pair else Sk // TK, kv_loop, init)
                res.append((acc / l, m + jnp.log(l)))
            if pair:
                o = jnp.where(lane < HEAD_DIM, res[0][0], res[1][0])
                lse = jnp.where(lane < HEAD_DIM, res[0][1], res[1][1])
            else:
                o = res[0][0]
                lse = jnp.broadcast_to(res[0][1], (TQ, LANES))
            o_ref[0, pl.ds(r0, TQ), :] = o
            lse_ref[0, pl.ds(r0, TQ), :] = lse
            return 0

        lax.fori_loop(0, nq, q_loop, 0)

    ins, in_specs = [], []
    if pair:
        ins.append(src)
        in_specs.append(pl.BlockSpec((1, S, PAIR_W), lambda b, h: (b, 0, col0 // PAIR_W + h)))
    else:
        ins += [src, kv, kv]
        in_specs += [pl.BlockSpec((1, S, LANES), lambda b, h: (b, 0, col0 // LANES + h)),
                     pl.BlockSpec((1, MEM_LEN, LANES), lambda b, h: (b, 0, h)),
                     pl.BlockSpec((1, MEM_LEN, LANES), lambda b, h: (b, 0, MEM_HEADS + h))]
    if has_bias:
        ins.append(negc)
        in_specs.append(pl.BlockSpec((1, 1, 2, S), lambda b, h: (b, h, 0, 0)))
    if pair:
        ins.append(mask)
        in_specs.append(pl.BlockSpec(mask.shape, lambda b, h: (0, 0, 0)))
    if has_rope:
        ins += list(rope)
        in_specs += [pl.BlockSpec((S, LANES), lambda b, h: (0, 0))] * 3
    W = n_blocks * LANES
    out_spec = pl.BlockSpec((1, S, LANES), lambda b, h: (b, 0, h))
    return pl.pallas_call(
        body, name=kind + "_attn_fwd", grid=(B, n_blocks),
        in_specs=in_specs, out_specs=[out_spec, out_spec],
        out_shape=[jax.ShapeDtypeStruct((B, S, W), F32)] * 2,
        scratch_shapes=[pltpu.VMEM((S, LANES), BF16), pltpu.VMEM((Sk, LANES), BF16), pltpu.VMEM((Sk, LANES), BF16)],
        compiler_params=_params(("arbitrary", "arbitrary")),
    )(*ins)


def attn_bwd(kind, src, do, o, lse, S, *, negc=None, mask=None, rope=None, kv=None):
    B = src.shape[0]
    pair = kind != "mem"
    col0 = {"fox": P_FOX, "dil": P_DIL, "mem": P_MQ}[kind]
    n_blocks = FOX_HEADS // 2 if pair else MEM_HEADS
    e_dim = HEAD_DIM if pair else MEM_HEAD_DIM
    scale = 1.0 / math.sqrt(e_dim)
    q_fold, s_scale = _scale_parts(scale)
    Sk = S if pair else MEM_LEN
    nh = 2 if pair else 1
    has_bias = negc is not None
    has_rope = rope is not None
    nq = S // TQ
    nk = Sk // TK

    def body(*refs):
        refs = list(refs)
        if pair:
            qkv_ref = refs.pop(0)
        else:
            q_ref, k_ref, v_ref = refs.pop(0), refs.pop(0), refs.pop(0)
        do_ref, o_ref, lse_ref = refs.pop(0), refs.pop(0), refs.pop(0)
        negc_ref = refs.pop(0) if has_bias else None
        mask_ref = refs.pop(0) if pair else None
        rope_refs = [refs.pop(0) for _ in range(3)] if has_rope else None
        if pair:
            dqkv_ref = refs.pop(0)
            dnegc_ref = refs.pop(0) if has_bias else None
            drow_ref = refs.pop(0) if has_bias else None
        else:
            dq_ref, dk_ref, dv_ref = refs.pop(0), refs.pop(0), refs.pop(0)
        qs, ks, vs, dos, delta_s, dq_acc = refs[:6]
        drow_acc = refs[6] if has_bias else None
        lane = lax.broadcasted_iota(jnp.int32, (1, LANES), 1)

        def prep_q(n, _):
            r0 = pl.multiple_of(n * TQ, TQ)
            rows = pl.ds(r0, TQ)
            q = qkv_ref[0, rows, 0:LANES] if pair else q_ref[0, rows, :]
            if has_rope:
                q = _rope(q, *[t[rows, :] for t in rope_refs])
            if q_fold is not None:
                q = q * q_fold
            qs[rows, :] = q.astype(BF16)
            dov = do_ref[0, rows, :]
            dob = dov.astype(BF16)
            dos[rows, :] = dob
            prod = dob.astype(F32) * o_ref[0, rows, :]
            if pair:
                d0 = jnp.sum(jnp.where(lane < HEAD_DIM, prod, 0.0), axis=1, keepdims=True)
                d1 = jnp.sum(jnp.where(lane < HEAD_DIM, 0.0, prod), axis=1, keepdims=True)
                delta_s[rows, :] = jnp.where(lane < HEAD_DIM, d0, d1)
            else:
                delta_s[rows, :] = jnp.broadcast_to(jnp.sum(prod, axis=1, keepdims=True), (TQ, LANES))
            dq_acc[rows, :] = jnp.zeros((TQ, LANES), F32)
            if has_bias:
                drow_acc[rows, :] = jnp.zeros((TQ, LANES), F32)
            return 0

        def prep_kv(n, _):
            r0 = pl.multiple_of(n * TK, TK)
            rows = pl.ds(r0, TK)
            k = qkv_ref[0, rows, LANES:2 * LANES] if pair else k_ref[0, rows, :]
            v = qkv_ref[0, rows, 2 * LANES:3 * LANES] if pair else v_ref[0, rows, :]
            if has_rope:
                k = _rope(k, *[t[rows, :] for t in rope_refs])
            ks[rows, :] = k.astype(BF16)
            vs[rows, :] = v.astype(BF16)
            return 0

        lax.fori_loop(0, nq, prep_q, 0)
        lax.fori_loop(0, nk, prep_kv, 0)

        def kv_loop(j, _):
            c0 = pl.multiple_of(j * TK, TK)
            kt = ks[pl.ds(c0, TK), :]
            vt = vs[pl.ds(c0, TK), :]
            res = []
            for hh in range(nh):
                hmask = (lane >= HEAD_DIM * hh) & (lane < HEAD_DIM * (hh + 1))
                kh = jnp.where(hmask, kt, jnp.zeros_like(kt)) if pair else kt
                vh = jnp.where(hmask, vt, jnp.zeros_like(vt)) if pair else vt

                def q_loop(i, carry, kh=kh, vh=vh, hh=hh, hmask=hmask):
                    dk, dv, dneg = carry
                    r0 = pl.multiple_of(i * TQ, TQ)
                    rows = pl.ds(r0, TQ)
                    q = qs[rows, :]
                    dot = dos[rows, :]
                    lse_i = lse_ref[0, rows, hh * HEAD_DIM:hh * HEAD_DIM + 1]
                    delta_i = delta_s[rows, hh * HEAD_DIM:hh * HEAD_DIM + 1]
                    s = lax.dot_general(q, kh, (((1,), (1,)), ((), ())), preferred_element_type=F32)
                    if s_scale is not None:
                        s = s * s_scale
                    if has_bias:
                        s = s + negc_ref[0, 0, pl.ds(hh, 1), pl.ds(c0, TK)]
                    if pair:
                        mult = mask_ref[i - j]
                        s = jnp.where(mult > 0.0, s, NEG_INF)
                    p = jnp.exp(s - lse_i)
                    if pair:
                        p = p * mult
                    dv = dv + lax.dot_general(p.astype(BF16), dot, (((0,), (0,)), ((), ())),
                                              preferred_element_type=F32)
                    dp = lax.dot_general(dot, vh, (((1,), (1,)), ((), ())), preferred_element_type=F32)
                    ds = p * (dp - delta_i)
                    if has_bias:
                        dneg = dneg + jnp.sum(ds, axis=0, keepdims=True)
                        drow_acc[rows, :] += jnp.where(hmask, jnp.sum(ds, axis=1, keepdims=True), 0.0)
                    if s_scale is not None:
                        ds = ds * s_scale
                    dsb = ds.astype(BF16)
                    dk = dk + lax.dot_general(dsb, q, (((0,), (0,)), ((), ())), preferred_element_type=F32)
                    dq = jnp.dot(dsb, kh, preferred_element_type=F32)
                    dq_acc[rows, :] += dq
                    return dk, dv, dneg

                init = (jnp.zeros((TK, LANES), F32), jnp.zeros((TK, LANES), F32), jnp.zeros((1, TK), F32))
                dk, dv, dneg = lax.fori_loop(j if pair else 0, nq, q_loop, init)
                if has_bias:
                    dnegc_ref[0, 0, pl.ds(hh, 1), pl.ds(c0, TK)] = dneg
                res.append((dk, dv))
            if pair:
                dk = jnp.where(lane < HEAD_DIM, res[0][0], res[1][0])
                dv = jnp.where(lane < HEAD_DIM, res[0][1], res[1][1])
                if has_rope:
                    dk = _rope_bwd(dk, *[t[pl.ds(c0, TK), :] for t in rope_refs])
                dqkv_ref[0, pl.ds(c0, TK), LANES:2 * LANES] = dk.astype(BF16)
                dqkv_ref[0, pl.ds(c0, TK), 2 * LANES:3 * LANES] = dv.astype(BF16)
            else:
                dk_ref[0, pl.ds(c0, TK), :] = res[0][0].astype(BF16)
                dv_ref[0, pl.ds(c0, TK), :] = res[0][1].astype(BF16)
            return 0

        lax.fori_loop(0, nk, kv_loop, 0)

        def fin_q(n, _):
            r0 = pl.multiple_of(n * TQ, TQ)
            rows = pl.ds(r0, TQ)
            dq = dq_acc[rows, :]
            if q_fold is not None:
                dq = dq * q_fold
            if has_rope:
                dq = _rope_bwd(dq, *[t[rows, :] for t in rope_refs])
            if pair:
                dqkv_ref[0, rows, 0:LANES] = dq.astype(BF16)
            else:
                dq_ref[0, rows, :] = dq.astype(BF16)
            if has_bias:
                drow_ref[0, rows, :] = drow_acc[rows, :]
            return 0

        lax.fori_loop(0, nq, fin_q, 0)

    ins, in_specs = [], []
    if pair:
        ins.append(src)
        in_specs.append(pl.BlockSpec((1, S, PAIR_W), lambda b, h: (b, 0, col0 // PAIR_W + h)))
    else:
        ins += [src, kv, kv]
        in_specs += [pl.BlockSpec((1, S, LANES), lambda b, h: (b, 0, col0 // LANES + h)),
                     pl.BlockSpec((1, MEM_LEN, LANES), lambda b, h: (b, 0, h)),
                     pl.BlockSpec((1, MEM_LEN, LANES), lambda b, h: (b, 0, MEM_HEADS + h))]
    row_spec = pl.BlockSpec((1, S, LANES), lambda b, h: (b, 0, h))
    ins += [do, o, lse]
    in_specs += [row_spec] * 3
    if has_bias:
        ins.append(negc)
        in_specs.append(pl.BlockSpec((1, 1, 2, S), lambda b, h: (b, h, 0, 0)))
    if pair:
        ins.append(mask)
        in_specs.append(pl.BlockSpec(mask.shape, lambda b, h: (0, 0, 0)))
    if has_rope:
        ins += list(rope)
        in_specs += [pl.BlockSpec((S, LANES), lambda b, h: (0, 0))] * 3
    W = n_blocks * LANES
    if pair:
        out_specs = [pl.BlockSpec((1, S, PAIR_W), lambda b, h: (b, 0, h))]
        out_shape = [jax.ShapeDtypeStruct((B, S, 3 * W), BF16)]
        if has_bias:
            out_specs.append(pl.BlockSpec((1, 1, 2, S), lambda b, h: (b, h, 0, 0)))
            out_shape.append(jax.ShapeDtypeStruct((B, LANES // 2, 2, S), F32))
            out_specs.append(row_spec)
            out_shape.append(jax.ShapeDtypeStruct((B, S, W), F32))
    else:
        kv_spec = pl.BlockSpec((1, MEM_LEN, LANES), lambda b, h: (b, 0, h))
        out_specs = [row_spec, kv_spec, kv_spec]
        out_shape = [jax.ShapeDtypeStruct((B, S, W), BF16)] + [jax.ShapeDtypeStruct((B, MEM_LEN, W), BF16)] * 2
    return pl.pallas_call(
        body, name=kind + "_attn_bwd", grid=(B, n_blocks),
        in_specs=in_specs, out_specs=out_specs, out_shape=out_shape,
        scratch_shapes=[pltpu.VMEM((S, LANES), BF16), pltpu.VMEM((Sk, LANES), BF16), pltpu.VMEM((Sk, LANES), BF16),
                        pltpu.VMEM((S, LANES), BF16), pltpu.VMEM((S, LANES), F32), pltpu.VMEM((S, LANES), F32)]
        + ([pltpu.VMEM((S, LANES), F32)] if has_bias else []),
        compiler_params=_params(("arbitrary", "arbitrary")),
    )(*ins)


def _log_masks(S, kind):
    nd = 1 if kind == "causal" else S // TQ
    a = np.arange(TQ)[:, None]
    b = np.arange(TK)[None, :]
    out = np.zeros((nd, TQ, TK), np.float32)
    for d in range(nd):
        delta = d * TQ + a - b
        if kind == "causal":
            m = (delta >= 0).astype(np.float64)
        else:
            m = sum(((delta >= 0) & (delta % dil == 0) & (delta <= w)).astype(np.float64) for w, dil in DILATIONS)
        out[d] = np.where(m > 0, np.log(np.maximum(m, 1.0)), NEG_INF)
    return jnp.asarray(out)


def _attn_setup(kind):
    pair = kind != "mem"
    e_dim = HEAD_DIM if pair else MEM_HEAD_DIM
    q_fold, s_scale = _scale_parts(1.0 / math.sqrt(e_dim))
    return dict(pair=pair, col0={"fox": P_FOX, "dil": P_DIL, "mem": P_MQ}[kind],
                n_blocks=FOX_HEADS // 2 if pair else MEM_HEADS, q_fold=q_fold, s_scale=s_scale,
                nh=2 if pair else 1)


def _attn_inputs(kind, src, S, negc, mask, rope, kv, extra):
    cfg = _attn_setup(kind)
    col0 = cfg["col0"]
    ins, in_specs = [], []
    if cfg["pair"]:
        ins.append(src)
        in_specs.append(pl.BlockSpec((1, S, PAIR_W), lambda b, h: (b, 0, col0 // PAIR_W + h)))
    else:
        ins += [src, kv, kv]
        in_specs += [pl.BlockSpec((1, S, LANES), lambda b, h: (b, 0, col0 // LANES + h)),
                     pl.BlockSpec((1, MEM_LEN, LANES), lambda b, h: (b, 0, h)),
                     pl.BlockSpec((1, MEM_LEN, LANES), lambda b, h: (b, 0, MEM_HEADS + h))]
    ins += list(extra)
    in_specs += [pl.BlockSpec((1, S, LANES), lambda b, h: (b, 0, h))] * len(extra)
    if negc is not None:
        ins.append(negc)
        in_specs.append(pl.BlockSpec((1, 1, 2, S), lambda b, h: (b, h, 0, 0)))
    if mask is not None:
        ins.append(mask)
        in_specs.append(pl.BlockSpec(mask.shape, lambda b, h: (0, 0, 0)))
    if rope is not None:
        ins += list(rope)
        in_specs += [pl.BlockSpec((S, LANES), lambda b, h: (0, 0))] * 3
    return ins, in_specs


def _prep_rows(cfg, rope_refs, lane, load_q, load_kv, qs2, ks, vs, S, Sk):
    nh = cfg["nh"]
    R = nh * TQ

    def prep_q(n, _):
        rows = pl.ds(pl.multiple_of(n * TQ, TQ), TQ)
        q = load_q(rows)
        if rope_refs is not None:
            q = _rope(q, *[t[rows, :] for t in rope_refs])
        if cfg["q_fold"] is not None:
            q = q * cfg["q_fold"]
        _store_stacked(cfg, lane, qs2, n, q.astype(BF16))
        return 0

    def prep_kv(n, _):
        rows = pl.ds(pl.multiple_of(n * TK, TK), TK)
        k, v = load_kv(rows)
        if rope_refs is not None:
            k = _rope(k, *[t[rows, :] for t in rope_refs])
        ks[rows, :] = k.astype(BF16)
        vs[rows, :] = v.astype(BF16)
        return 0

    lax.fori_loop(0, S // TQ, prep_q, 0)
    lax.fori_loop(0, Sk // TK, prep_kv, 0)


def _store_stacked(cfg, lane, dst, n, val):
    nh = cfg["nh"]
    R = nh * TQ
    if nh == 1:
        dst[pl.ds(pl.multiple_of(n * R, R), TQ), :] = val
        return
    for hh in range(nh):
        hmask = (lane >= HEAD_DIM * hh) & (lane < HEAD_DIM * (hh + 1))
        dst[pl.ds(pl.multiple_of(n * R + hh * TQ, TQ), TQ), :] = jnp.where(hmask, val, jnp.zeros_like(val))


def _cat(parts, axis):
    return parts[0] if len(parts) == 1 else jnp.concatenate(parts, axis=axis)


def attn_fwd2(kind, src, S, *, negc=None, mask=None, rope=None, kv=None):
    B = src.shape[0]
    cfg = _attn_setup(kind)
    pair, nh, s_scale = cfg["pair"], cfg["nh"], cfg["s_scale"]
    Sk = S if pair else MEM_LEN
    has_bias, has_rope = negc is not None, rope is not None
    R = nh * TQ

    def body(*refs):
        refs = list(refs)
        if pair:
            qkv_ref = refs.pop(0)
        else:
            q_ref, k_ref, v_ref = refs.pop(0), refs.pop(0), refs.pop(0)
        negc_ref = refs.pop(0) if has_bias else None
        mask_ref = refs.pop(0) if mask is not None else None
        rope_refs = [refs.pop(0) for _ in range(3)] if has_rope else None
        o_ref, lse_ref, qs2, ks, vs = refs
        lane = lax.broadcasted_iota(jnp.int32, (1, LANES), 1)

        if pair:
            load_q = lambda rows: qkv_ref[0, rows, 0:LANES]
            load_kv = lambda rows: (qkv_ref[0, rows, LANES:2 * LANES], qkv_ref[0, rows, 2 * LANES:3 * LANES])
        else:
            load_q = lambda rows: q_ref[0, rows, :]
            load_kv = lambda rows: (k_ref[0, rows, :], v_ref[0, rows, :])
        _prep_rows(cfg, rope_refs, lane, load_q, load_kv, qs2, ks, vs, S, Sk)

        def q_loop(i, _):
            q2 = qs2[pl.ds(pl.multiple_of(i * R, R), R), :]

            def step(j, carry, midx):
                ms, ls, acc = carry
                c0 = pl.multiple_of(j * TK, TK)
                k = ks[pl.ds(c0, TK), :]
                v = vs[pl.ds(c0, TK), :]
                s2 = lax.dot_general(q2, k, (((1,), (1,)), ((), ())), preferred_element_type=F32)
                if s_scale is not None:
                    s2 = s2 * s_scale
                new_m, new_l, ps, alphas = [], [], [], []
                for hh in range(nh):
                    s = s2[hh * TQ:(hh + 1) * TQ]
                    if has_bias:
                        s = s + negc_ref[0, 0, pl.ds(hh, 1), pl.ds(c0, TK)]
                    if midx is not None:
                        s = s + mask_ref[midx]
                    m_new = jnp.maximum(ms[hh], jnp.max(s, axis=1, keepdims=True))
                    p = jnp.exp(s - m_new)
                    alpha = jnp.exp(ms[hh] - m_new)
                    new_l.append(alpha * ls[hh] + jnp.sum(p, axis=1, keepdims=True))
                    new_m.append(m_new)
                    ps.append(p.astype(BF16))
                    alphas.append(alpha)
                acc = acc * _cat(alphas, 0) + jnp.dot(_cat(ps, 0), v, preferred_element_type=F32)
                return tuple(new_m), tuple(new_l), acc

            init = (tuple(jnp.full((TQ, 1), NEG_INF, F32) for _ in range(nh)),
                    tuple(jnp.zeros((TQ, 1), F32) for _ in range(nh)), jnp.zeros((R, LANES), F32))
            if kind == "fox":
                carry = lax.fori_loop(0, i, lambda j, c: step(j, c, None), init)
                carry = step(i, carry, 0)
            elif kind == "dil":
                carry = lax.fori_loop(0, i + 1, lambda j, c: step(j, c, i - j), init)
            else:
                carry = lax.fori_loop(0, Sk // TK, lambda j, c: step(j, c, None), init)
            ms, ls, acc = carry
            outs = [acc[hh * TQ:(hh + 1) * TQ] / ls[hh] for hh in range(nh)]
            lses = [ms[hh] + jnp.log(ls[hh]) for hh in range(nh)]
            rows = pl.ds(pl.multiple_of(i * TQ, TQ), TQ)
            if pair:
                o_ref[0, rows, :] = jnp.where(lane < HEAD_DIM, outs[0], outs[1])
                lse_ref[0, rows, :] = jnp.where(lane < HEAD_DIM, lses[0], lses[1])
            else:
                o_ref[0, rows, :] = outs[0]
                lse_ref[0, rows, :] = jnp.broadcast_to(lses[0], (TQ, LANES))
            return 0

        lax.fori_loop(0, S // TQ, q_loop, 0)

    ins, in_specs = _attn_inputs(kind, src, S, negc, mask, rope, kv, ())
    W = cfg["n_blocks"] * LANES
    out_spec = pl.BlockSpec((1, S, LANES), lambda b, h: (b, 0, h))
    return pl.pallas_call(
        body, name=kind + "_attn_fwd", grid=(B, cfg["n_blocks"]),
        in_specs=in_specs, out_specs=[out_spec, out_spec],
        out_shape=[jax.ShapeDtypeStruct((B, S, W), F32)] * 2,
        scratch_shapes=[pltpu.VMEM((nh * S, LANES), BF16), pltpu.VMEM((Sk, LANES), BF16),
                        pltpu.VMEM((Sk, LANES), BF16)],
        compiler_params=_params(("arbitrary", "arbitrary")),
    )(*ins)


def attn_bwd2(kind, src, do, o, lse, S, *, negc=None, mask=None, rope=None, kv=None):
    B = src.shape[0]
    cfg = _attn_setup(kind)
    pair, nh, s_scale, q_fold = cfg["pair"], cfg["nh"], cfg["s_scale"], cfg["q_fold"]
    Sk = S if pair else MEM_LEN
    has_bias, has_rope = negc is not None, rope is not None
    R = nh * TQ
    nq, nk = S // TQ, Sk // TK

    def body(*refs):
        refs = list(refs)
        if pair:
            qkv_ref = refs.pop(0)
        else:
            q_ref, k_ref, v_ref = refs.pop(0), refs.pop(0), refs.pop(0)
        do_ref, o_ref, lse_ref = refs.pop(0), refs.pop(0), refs.pop(0)
        negc_ref = refs.pop(0) if has_bias else None
        mask_ref = refs.pop(0) if mask is not None else None
        rope_refs = [refs.pop(0) for _ in range(3)] if has_rope else None
        if pair:
            dqkv_ref = refs.pop(0)
            dnegc_ref = refs.pop(0) if has_bias else None
            drow_ref = refs.pop(0) if has_bias else None
        else:
            dq_ref, dk_ref, dv_ref = refs.pop(0), refs.pop(0), refs.pop(0)
        qs2, ks, vs, dos2, lse_s, delta_s, dk_acc, dv_acc = refs[:8]
        dneg_acc = refs[8] if has_bias else None
        lane = lax.broadcasted_iota(jnp.int32, (1, LANES), 1)

        if pair:
            load_q = lambda rows: qkv_ref[0, rows, 0:LANES]
            load_kv = lambda rows: (qkv_ref[0, rows, LANES:2 * LANES], qkv_ref[0, rows, 2 * LANES:3 * LANES])
        else:
            load_q = lambda rows: q_ref[0, rows, :]
            load_kv = lambda rows: (k_ref[0, rows, :], v_ref[0, rows, :])
        _prep_rows(cfg, rope_refs, lane, load_q, load_kv, qs2, ks, vs, S, Sk)

        def prep_do(n, _):
            rows = pl.ds(pl.multiple_of(n * TQ, TQ), TQ)
            dob = do_ref[0, rows, :].astype(BF16)
            _store_stacked(cfg, lane, dos2, n, dob)
            prod = dob.astype(F32) * o_ref[0, rows, :]
            lse_blk = lse_ref[0, rows, :]
            for hh in range(nh):
                dst = pl.ds(pl.multiple_of(n * R + hh * TQ, TQ), TQ)
                if pair:
                    hmask = (lane >= HEAD_DIM * hh) & (lane < HEAD_DIM * (hh + 1))
                    d = jnp.sum(jnp.where(hmask, prod, 0.0), axis=1, keepdims=True)
                    lse_s[dst, :] = jnp.broadcast_to(lse_blk[:, hh * HEAD_DIM:hh * HEAD_DIM + 1], (TQ, LANES))
                else:
                    d = jnp.sum(prod, axis=1, keepdims=True)
                    lse_s[dst, :] = lse_blk
                delta_s[dst, :] = jnp.broadcast_to(d, (TQ, LANES))
            return 0

        def zero_kv(n, _):
            rows = pl.ds(pl.multiple_of(n * TK, TK), TK)
            dk_acc[rows, :] = jnp.zeros((TK, LANES), F32)
            dv_acc[rows, :] = jnp.zeros((TK, LANES), F32)
            return 0

        lax.fori_loop(0, nq, prep_do, 0)
        lax.fori_loop(0, nk, zero_kv, 0)
        if has_bias:
            dneg_acc[...] = jnp.zeros(dneg_acc.shape, F32)

        def q_loop(i, _):
            rows2 = pl.ds(pl.multiple_of(i * R, R), R)
            q2 = qs2[rows2, :]
            do2 = dos2[rows2, :]
            lse2 = lse_s[rows2, :]
            delta2 = delta_s[rows2, :]
            wide = lambda t: jnp.concatenate([t] * (TK // LANES), axis=1)

            def step(j, carry, midx):
                dq2, drow = carry
                c0 = pl.multiple_of(j * TK, TK)
                kcols = pl.ds(c0, TK)
                k = ks[kcols, :]
                v = vs[kcols, :]
                s2 = lax.dot_general(q2, k, (((1,), (1,)), ((), ())), preferred_element_type=F32)
                if s_scale is not None:
                    s2 = s2 * s_scale
                if has_bias or midx is not None:
                    halves = []
                    for hh in range(nh):
                        s = s2[hh * TQ:(hh + 1) * TQ]
                        if has_bias:
                            s = s + negc_ref[0, 0, pl.ds(hh, 1), kcols]
                        if midx is not None:
                            s = s + mask_ref[midx]
                        halves.append(s)
                    s2 = _cat(halves, 0)
                p2 = jnp.exp(s2 - wide(lse2))
                dp2 = lax.dot_general(do2, v, (((1,), (1,)), ((), ())), preferred_element_type=F32)
                ds2 = p2 * (dp2 - wide(delta2))
                if has_bias:
                    drow = drow + jnp.sum(ds2, axis=1, keepdims=True)
                    for hh in range(nh):
                        dneg_acc[pl.ds(hh, 1), kcols] += jnp.sum(ds2[hh * TQ:(hh + 1) * TQ], axis=0, keepdims=True)
                if s_scale is not None:
                    ds2 = ds2 * s_scale
                dsb = ds2.astype(BF16)
                dv_acc[kcols, :] += lax.dot_general(p2.astype(BF16), do2, (((0,), (0,)), ((), ())),
                                                    preferred_element_type=F32)
                dk_acc[kcols, :] += lax.dot_general(dsb, q2, (((0,), (0,)), ((), ())), preferred_element_type=F32)
                dq2 = dq2 + jnp.dot(dsb, k, preferred_element_type=F32)
                return dq2, drow

            init = (jnp.zeros((R, LANES), F32), jnp.zeros((R, 1), F32))
            if kind == "fox":
                carry = lax.fori_loop(0, i, lambda j, c: step(j, c, None), init)
                carry = step(i, carry, 0)
            elif kind == "dil":
                carry = lax.fori_loop(0, i + 1, lambda j, c: step(j, c, i - j), init)
            else:
                carry = lax.fori_loop(0, nk, lambda j, c: step(j, c, None), init)
            dq2, drow = carry
            rows = pl.ds(pl.multiple_of(i * TQ, TQ), TQ)
            dq = jnp.where(lane < HEAD_DIM, dq2[0:TQ], dq2[TQ:2 * TQ]) if pair else dq2
            if q_fold is not None:
                dq = dq * q_fold
            if has_rope:
                dq = _rope_bwd(dq, *[t[rows, :] for t in rope_refs])
            if pair:
                dqkv_ref[0, rows, 0:LANES] = dq.astype(BF16)
            else:
                dq_ref[0, rows, :] = dq.astype(BF16)
            if has_bias:
                drow_ref[0, rows, :] = jnp.where(lane < HEAD_DIM, drow[0:TQ], drow[TQ:2 * TQ])
            return 0

        lax.fori_loop(0, nq, q_loop, 0)

        def fin_kv(n, _):
            rows = pl.ds(pl.multiple_of(n * TK, TK), TK)
            dk = dk_acc[rows, :]
            if has_rope:
                dk = _rope_bwd(dk, *[t[rows, :] for t in rope_refs])
            if pair:
                dqkv_ref[0, rows, LANES:2 * LANES] = dk.astype(BF16)
                dqkv_ref[0, rows, 2 * LANES:3 * LANES] = dv_acc[rows, :].astype(BF16)
            else:
                dk_ref[0, rows, :] = dk.astype(BF16)
                dv_ref[0, rows, :] = dv_acc[rows, :].astype(BF16)
            return 0

        lax.fori_loop(0, nk, fin_kv, 0)
        if has_bias:
            dnegc_ref[0, 0] = dneg_acc[...]

    ins, in_specs = _attn_inputs(kind, src, S, negc, mask, rope, kv, (do, o, lse))
    W = cfg["n_blocks"] * LANES
    row_spec = pl.BlockSpec((1, S, LANES), lambda b, h: (b, 0, h))
    if pair:
        out_specs = [pl.BlockSpec((1, S, PAIR_W), lambda b, h: (b, 0, h))]
        out_shape = [jax.ShapeDtypeStruct((B, S, 3 * W), BF16)]
        if has_bias:
            out_specs += [pl.BlockSpec((1, 1, 2, S), lambda b, h: (b, h, 0, 0)), row_spec]
            out_shape += [jax.ShapeDtypeStruct((B, LANES // 2, 2, S), F32), jax.ShapeDtypeStruct((B, S, W), F32)]
    else:
        kv_spec = pl.BlockSpec((1, MEM_LEN, LANES), lambda b, h: (b, 0, h))
        out_specs = [row_spec, kv_spec, kv_spec]
        out_shape = [jax.ShapeDtypeStruct((B, S, W), BF16)] + [jax.ShapeDtypeStruct((B, MEM_LEN, W), BF16)] * 2
    scratch = [pltpu.VMEM((nh * S, LANES), BF16), pltpu.VMEM((Sk, LANES), BF16), pltpu.VMEM((Sk, LANES), BF16),
               pltpu.VMEM((nh * S, LANES), BF16), pltpu.VMEM((nh * S, LANES), F32), pltpu.VMEM((nh * S, LANES), F32),
               pltpu.VMEM((Sk, LANES), F32), pltpu.VMEM((Sk, LANES), F32)]
    if has_bias:
        scratch.append(pltpu.VMEM((2, S), F32))
    return pl.pallas_call(
        body, name=kind + "_attn_bwd", grid=(B, cfg["n_blocks"]),
        in_specs=in_specs, out_specs=out_specs, out_shape=out_shape, scratch_shapes=scratch,
        compiler_params=_params(("arbitrary", "arbitrary")),
    )(*ins)


def _log_masks_t(S, kind):
    return jnp.swapaxes(_log_masks(S, kind), 1, 2)


def _head_rows(hh, pair):
    row = lax.broadcasted_iota(jnp.int32, (LANES, 1), 0)
    if not pair:
        return row >= 0
    return (row >= HEAD_DIM * hh) & (row < HEAD_DIM * (hh + 1))


def _attn_t_inputs(kind, src, S, negc_cols, mask, rope, kv):
    cfg = _attn_setup(kind)
    col0 = cfg["col0"]
    ins, in_specs = [], []
    if cfg["pair"]:
        ins.append(src)
        in_specs.append(pl.BlockSpec((1, S, PAIR_W), lambda b, h: (b, 0, col0 // PAIR_W + h)))
    else:
        ins += [src, kv, kv]
        in_specs += [pl.BlockSpec((1, S, LANES), lambda b, h: (b, 0, col0 // LANES + h)),
                     pl.BlockSpec((1, MEM_LEN, LANES), lambda b, h: (b, 0, h)),
                     pl.BlockSpec((1, MEM_LEN, LANES), lambda b, h: (b, 0, MEM_HEADS + h))]
    if negc_cols is not None:
        ins.append(negc_cols)
        in_specs.append(pl.BlockSpec((1, S, LANES), lambda b, h: (b, 0, 0)))
    if mask is not None:
        ins.append(mask)
        in_specs.append(pl.BlockSpec(mask.shape, lambda b, h: (0, 0, 0)))
    if rope is not None:
        ins += list(rope)
        in_specs += [pl.BlockSpec((S, LANES), lambda b, h: (0, 0))] * 3
    return ins, in_specs


def _attn_t_prep(cfg, refs, S, Sk, *, qT2s, ks, q2s=None, vs=None, vTs=None, kTs=None, nb=None):
    pair, nh = cfg["pair"], cfg["nh"]
    lane = lax.broadcasted_iota(jnp.int32, (1, LANES), 1)
    rope_refs = refs["rope"]

    def prep_q(n, _):
        rows = pl.ds(pl.multiple_of(n * TQ, TQ), TQ)
        q = refs["load_q"](rows)
        if rope_refs is not None:
            q = _rope(q, *[t[rows, :] for t in rope_refs])
        if cfg["q_fold"] is not None:
            q = q * cfg["q_fold"]
        qb = q.astype(BF16)
        if q2s is not None:
            _store_stacked(cfg, lane, q2s, n, qb)
        qtb = qb.T
        for hh in range(nh):
            qT2s[n, :, hh * TQ:(hh + 1) * TQ] = jnp.where(_head_rows(hh, pair), qtb, jnp.zeros_like(qtb))
        return 0

    def prep_kv(n, _):
        rows = pl.ds(pl.multiple_of(n * TK, TK), TK)
        k, v = refs["load_kv"](rows)
        if rope_refs is not None:
            k = _rope(k, *[t[rows, :] for t in rope_refs])
        kb = k.astype(BF16)
        vb = v.astype(BF16)
        ks[rows, :] = kb
        if vs is not None:
            vs[rows, :] = vb
        if vTs is not None:
            vTs[n] = vb.T
        if kTs is not None:
            kTs[n] = kb.T
        if nb is not None:
            blk = refs["negc"][0, rows, :]
            for hh in range(nh):
                h = 2 * refs["block"] + hh
                col = jnp.sum(jnp.where(lane == h, blk, 0.0), axis=1, keepdims=True)
                nb[hh, rows, :] = jnp.broadcast_to(col, (TK, LANES))
        return 0

    lax.fori_loop(0, S // TQ, prep_q, 0)
    lax.fori_loop(0, Sk // TK, prep_kv, 0)


def _raw_scores_t(cfg, k, qT2):
    sT = jnp.dot(k, qT2, preferred_element_type=F32)
    if cfg["s_scale"] is not None:
        sT = sT * cfg["s_scale"]
    return sT


def _bias_mask_t(cfg, sT, nb, mask_ref, kc, midx):
    nh = cfg["nh"]
    if nb is None and midx is None:
        return sT
    parts = []
    for hh in range(nh):
        t = sT[:, hh * TQ:(hh + 1) * TQ]
        if nb is not None:
            t = t + jnp.concatenate([nb[hh, kc, :]] * (TQ // LANES), axis=1)
        if midx is not None:
            t = t + mask_ref[midx]
        parts.append(t)
    return _cat(parts, 1)


def _kv_plan(kind, i, nk):
    if kind == "fox":
        return i, (lambda j: None), 0
    if kind == "dil":
        return i, (lambda j: i - j), 0
    return nk - 1, (lambda j: None), None


def attn_fwd3(kind, src, S, *, negc_cols=None, mask=None, rope=None, kv=None):
    B = src.shape[0]
    cfg = _attn_setup(kind)
    pair, nh = cfg["pair"], cfg["nh"]
    Sk = S if pair else MEM_LEN
    has_bias, has_rope = negc_cols is not None, rope is not None
    R = nh * TQ
    nq, nk = S // TQ, Sk // TK

    def body(*refs):
        refs = list(refs)
        if pair:
            qkv_ref = refs.pop(0)
            load_q = lambda rows: qkv_ref[0, rows, 0:LANES]
            load_kv = lambda rows: (qkv_ref[0, rows, LANES:2 * LANES], qkv_ref[0, rows, 2 * LANES:3 * LANES])
        else:
            q_ref, k_ref, v_ref = refs.pop(0), refs.pop(0), refs.pop(0)
            load_q = lambda rows: q_ref[0, rows, :]
            load_kv = lambda rows: (k_ref[0, rows, :], v_ref[0, rows, :])
        negc_ref = refs.pop(0) if has_bias else None
        mask_ref = refs.pop(0) if mask is not None else None
        rope_refs = [refs.pop(0) for _ in range(3)] if has_rope else None
        o_ref, lse_ref, qT2s, ks, vTs = refs[:5]
        nb = refs[5] if has_bias else None
        _attn_t_prep(cfg, dict(load_q=load_q, load_kv=load_kv, rope=rope_refs, negc=negc_ref,
                               block=pl.program_id(1)), S, Sk,
                     qT2s=qT2s, ks=ks, vTs=vTs, nb=nb)

        def q_loop(i, _):
            qT2 = qT2s[i]

            last, mask_of, mask_last = _kv_plan(kind, i, nk)

            def cols(j):
                return pl.ds(pl.multiple_of(j * TK, TK), TK)

            def scores(j):
                return _raw_scores_t(cfg, ks[cols(j), :], qT2)

            def soft(s_raw, j, midx, m, l):
                sT = _bias_mask_t(cfg, s_raw, nb, mask_ref, cols(j), midx)
                m_new = jnp.maximum(m, jnp.max(sT, axis=0, keepdims=True))
                p = jnp.exp(sT - m_new)
                alpha = jnp.exp(m - m_new)
                return m_new, alpha * l + jnp.sum(p, axis=0, keepdims=True), alpha, p.astype(BF16)

            def pv(j, p):
                return jnp.dot(vTs[j], p, preferred_element_type=F32)

            def body(j, carry):
                s_cur, p_prev, m, l, accT = carry
                pv_prev = pv(jnp.maximum(j - 1, 0), p_prev)
                s_next = scores(j + 1)
                m, l, alpha, p = soft(s_cur, j, mask_of(j), m, l)
                return s_next, p, m, l, (accT + pv_prev) * alpha

            init = (scores(0), jnp.zeros((TK, R), BF16), jnp.full((1, R), NEG_INF, F32), jnp.zeros((1, R), F32),
                    jnp.zeros((LANES, R), F32))
            s_cur, p_prev, m, l, accT = lax.fori_loop(0, last, body, init)
            pv_prev = pv(jnp.maximum(last - 1, 0), p_prev)
            m, l, alpha, p = soft(s_cur, last, mask_last, m, l)
            accT = (accT + pv_prev) * alpha + pv(last, p)
            oT2 = accT / l
            oT = jnp.where(_head_rows(0, True), oT2[:, 0:TQ], oT2[:, TQ:2 * TQ]) if pair else oT2
            o_ref[0, pl.ds(pl.multiple_of(i * TQ, TQ), TQ), :] = oT.T
            lse_ref[0, 0, pl.ds(i, 1), :] = m + jnp.log(l)
            return 0

        lax.fori_loop(0, nq, q_loop, 0)

    ins, in_specs = _attn_t_inputs(kind, src, S, negc_cols, mask, rope, kv)
    W = cfg["n_blocks"] * LANES
    scratch = [pltpu.VMEM((nq, LANES, R), BF16), pltpu.VMEM((Sk, LANES), BF16), pltpu.VMEM((nk, LANES, TK), BF16)]
    if has_bias:
        scratch.append(pltpu.VMEM((nh, Sk, LANES), F32))
    return pl.pallas_call(
        body, name=kind + "_attn_fwd", grid=(B, cfg["n_blocks"]),
        in_specs=in_specs,
        out_specs=[pl.BlockSpec((1, S, LANES), lambda b, h: (b, 0, h)),
                   pl.BlockSpec((1, 1, nq, R), lambda b, h: (b, h, 0, 0))],
        out_shape=[jax.ShapeDtypeStruct((B, S, W), F32), jax.ShapeDtypeStruct((B, cfg["n_blocks"], nq, R), F32)],
        scratch_shapes=scratch,
        compiler_params=_params(("arbitrary", "arbitrary")),
    )(*ins)


def _tile_walk(kind, nq, nk):
    if kind == "mem":
        return nq * nk, (lambda i, j: (jnp.where(j < nk - 1, i, i + 1), jnp.where(j < nk - 1, j + 1, 0))), None
    nxt = lambda i, j: (jnp.where(j < i, i, i + 1), jnp.where(j < i, j + 1, 0))
    if kind == "fox":
        return nq * (nq + 1) // 2, nxt, (lambda i, j: jnp.where(j == i, 0, 1))
    return nq * (nq + 1) // 2, nxt, (lambda i, j: i - j)


def attn_fwd4(kind, src, S, *, negc_cols=None, mask=None, rope=None, kv=None):
    B = src.shape[0]
    cfg = _attn_setup(kind)
    pair, nh = cfg["pair"], cfg["nh"]
    Sk = S if pair else MEM_LEN
    has_bias, has_rope = negc_cols is not None, rope is not None
    R = nh * TQ
    nq, nk = S // TQ, Sk // TK
    n_pairs, successor, mask_index = _tile_walk(kind, nq, nk)
    assert n_pairs % 2 == 0

    def body(*refs):
        refs = list(refs)
        if pair:
            qkv_ref = refs.pop(0)
            load_q = lambda rows: qkv_ref[0, rows, 0:LANES]
            load_kv = lambda rows: (qkv_ref[0, rows, LANES:2 * LANES], qkv_ref[0, rows, 2 * LANES:3 * LANES])
        else:
            q_ref, k_ref, v_ref = refs.pop(0), refs.pop(0), refs.pop(0)
            load_q = lambda rows: q_ref[0, rows, :]
            load_kv = lambda rows: (k_ref[0, rows, :], v_ref[0, rows, :])
        negc_ref = refs.pop(0) if has_bias else None
        mask_ref = refs.pop(0) if mask is not None else None
        rope_refs = [refs.pop(0) for _ in range(3)] if has_rope else None
        o_ref, lse_ref, qT2s, ks, vTs, s_a, s_b, p_a, p_b, acc_all, m_all, l_all = refs[:12]
        nb = refs[12] if has_bias else None
        _attn_t_prep(cfg, dict(load_q=load_q, load_kv=load_kv, rope=rope_refs, negc=negc_ref,
                               block=pl.program_id(1)), S, Sk, qT2s=qT2s, ks=ks, vTs=vTs, nb=nb)

        def cols(j):
            return pl.ds(pl.multiple_of(j * TK, TK), TK)

        def park(i, m, l, accT):
            acc_all[i] = accT
            m_all[pl.ds(i, 1), :] = m
            l_all[pl.ds(i, 1), :] = l

        def finish(i, _):
            l = l_all[pl.ds(i, 1), :]
            oT2 = acc_all[i] / l
            oT = jnp.where(_head_rows(0, True), oT2[:, 0:TQ], oT2[:, TQ:2 * TQ]) if pair else oT2
            o_ref[0, pl.ds(pl.multiple_of(i * TQ, TQ), TQ), :] = oT.T
            lse_ref[0, 0, pl.ds(i, 1), :] = m_all[pl.ds(i, 1), :] + jnp.log(l)
            return 0

        def half(i, j, i_prev, j_prev, s_cur, s_next, p_cur, p_prev, m, l, accT):
            i_n, j_n = successor(i, j)
            acc_full = accT + jnp.dot(vTs[j_prev], p_prev[...], preferred_element_type=F32)
            s_next[...] = _raw_scores_t(cfg, ks[cols(j_n), :], qT2s[jnp.minimum(i_n, nq - 1)])
            park(i_prev, m, l, acc_full)
            first = j == 0
            m = jnp.where(first, NEG_INF, m)
            l = jnp.where(first, 0.0, l)
            sT = _bias_mask_t(cfg, s_cur[...], nb, mask_ref, cols(j), None if mask_index is None else mask_index(i, j))
            m_new = jnp.maximum(m, jnp.max(sT, axis=0, keepdims=True))
            p = jnp.exp(sT - m_new)
            alpha = jnp.exp(m - m_new)
            p_cur[...] = p.astype(BF16)
            return i_n, j_n, i, j, m_new, alpha * l + jnp.sum(p, axis=0, keepdims=True), acc_full * alpha

        def two(_, carry):
            i, j, i_prev, j_prev, m, l, accT = carry
            i, j, i_prev, j_prev, m, l, accT = half(i, j, i_prev, j_prev, s_a, s_b, p_a, p_b, m, l, accT)
            return half(i, j, i_prev, j_prev, s_b, s_a, p_b, p_a, m, l, accT)

        s_a[...] = _raw_scores_t(cfg, ks[cols(0), :], qT2s[0])
        p_b[...] = jnp.zeros((TK, R), BF16)
        zero = jnp.int32(0)
        init = (zero, zero, zero, zero, jnp.full((1, R), NEG_INF, F32), jnp.ones((1, R), F32),
                jnp.zeros((LANES, R), F32))
        _, _, i_prev, j_prev, m, l, accT = lax.fori_loop(0, n_pairs // 2, two, init)
        park(i_prev, m, l, accT + jnp.dot(vTs[j_prev], p_b[...], preferred_element_type=F32))
        lax.fori_loop(0, nq, finish, 0)

    ins, in_specs = _attn_t_inputs(kind, src, S, negc_cols, mask, rope, kv)
    W = cfg["n_blocks"] * LANES
    scratch = [pltpu.VMEM((nq, LANES, R), BF16), pltpu.VMEM((Sk, LANES), BF16), pltpu.VMEM((nk, LANES, TK), BF16),
               pltpu.VMEM((TK, R), F32), pltpu.VMEM((TK, R), F32), pltpu.VMEM((TK, R), BF16), pltpu.VMEM((TK, R), BF16),
               pltpu.VMEM((nq, LANES, R), F32), pltpu.VMEM((nq, R), F32), pltpu.VMEM((nq, R), F32)]
    if has_bias:
        scratch.append(pltpu.VMEM((nh, Sk, LANES), F32))
    return pl.pallas_call(
        body, name=kind + "_attn_fwd", grid=(B, cfg["n_blocks"]),
        in_specs=in_specs,
        out_specs=[pl.BlockSpec((1, S, LANES), lambda b, h: (b, 0, h)),
                   pl.BlockSpec((1, 1, nq, R), lambda b, h: (b, h, 0, 0))],
        out_shape=[jax.ShapeDtypeStruct((B, S, W), F32), jax.ShapeDtypeStruct((B, cfg["n_blocks"], nq, R), F32)],
        scratch_shapes=scratch,
        compiler_params=_params(("arbitrary", "arbitrary")),
    )(*ins)


def attn_bwd3(kind, src, do, o, lse, S, *, negc_cols=None, mask=None, rope=None, kv=None):
    B = src.shape[0]
    cfg = _attn_setup(kind)
    pair, nh, s_scale, q_fold = cfg["pair"], cfg["nh"], cfg["s_scale"], cfg["q_fold"]
    Sk = S if pair else MEM_LEN
    has_bias, has_rope = negc_cols is not None, rope is not None
    R = nh * TQ
    nq, nk = S // TQ, Sk // TK
    n_pairs, successor, mask_index = _tile_walk(kind, nq, nk)
    assert n_pairs % 2 == 0

    def body(*refs):
        refs = list(refs)
        if pair:
            qkv_ref = refs.pop(0)
            load_q = lambda rows: qkv_ref[0, rows, 0:LANES]
            load_kv = lambda rows: (qkv_ref[0, rows, LANES:2 * LANES], qkv_ref[0, rows, 2 * LANES:3 * LANES])
        else:
            q_ref, k_ref, v_ref = refs.pop(0), refs.pop(0), refs.pop(0)
            load_q = lambda rows: q_ref[0, rows, :]
            load_kv = lambda rows: (k_ref[0, rows, :], v_ref[0, rows, :])
        negc_ref = refs.pop(0) if has_bias else None
        mask_ref = refs.pop(0) if mask is not None else None
        rope_refs = [refs.pop(0) for _ in range(3)] if has_rope else None
        do_ref, o_ref, lse_ref = refs.pop(0), refs.pop(0), refs.pop(0)
        if pair:
            dqkv_ref = refs.pop(0)
            dneg_ref = refs.pop(0) if has_bias else None
            drow_ref = refs.pop(0) if has_bias else None
        else:
            dq_ref, dk_ref, dv_ref = refs.pop(0), refs.pop(0), refs.pop(0)
        qT2s, ks, q2s, vs, kTs, doT2s, do2s, delta_s, dk_acc, dv_acc = refs[:10]
        bufs_a, bufs_b, dq_all = refs[10:14], refs[14:18], refs[18]
        nb, dneg_acc, drow_all = (refs[19], refs[20], refs[21]) if has_bias else (None, None, None)
        lane = lax.broadcasted_iota(jnp.int32, (1, LANES), 1)
        _attn_t_prep(cfg, dict(load_q=load_q, load_kv=load_kv, rope=rope_refs, negc=negc_ref,
                               block=pl.program_id(1)), S, Sk,
                     qT2s=qT2s, ks=ks, q2s=q2s, vs=vs, kTs=kTs, nb=nb)

        def prep_do(n, _):
            rows = pl.ds(pl.multiple_of(n * TQ, TQ), TQ)
            dob = do_ref[0, rows, :].astype(BF16)
            _store_stacked(cfg, lane, do2s, n, dob)
            doT = dob.astype(F32).T
            prodT = doT * o_ref[0, rows, :].T
            doTb = doT.astype(BF16)
            for hh in range(nh):
                hm = _head_rows(hh, pair)
                doT2s[n, :, hh * TQ:(hh + 1) * TQ] = jnp.where(hm, doTb, jnp.zeros_like(doTb))
                delta_s[pl.ds(n, 1), hh * TQ:(hh + 1) * TQ] = jnp.sum(jnp.where(hm, prodT, 0.0), axis=0, keepdims=True)
            return 0

        def zero_kv(n, _):
            rows = pl.ds(pl.multiple_of(n * TK, TK), TK)
            dk_acc[rows, :] = jnp.zeros((TK, LANES), F32)
            dv_acc[rows, :] = jnp.zeros((TK, LANES), F32)
            if has_bias:
                for hh in range(nh):
                    dneg_acc[hh, rows, :] = jnp.zeros((TK, LANES), F32)
            return 0

        lax.fori_loop(0, nq, prep_do, 0)
        lax.fori_loop(0, nk, zero_kv, 0)

        def cols(j):
            return pl.ds(pl.multiple_of(j * TK, TK), TK)

        def rows2(i):
            return pl.ds(pl.multiple_of(i * R, R), R)

        def park(i, dqT2, drow):
            dq_all[i] = dqT2
            if has_bias:
                drow_all[pl.ds(i, 1), :] = drow

        def half(i, j, i_prev, j_prev, cur, nxt_bufs, prv, dqT2, drow):
            s_cur, dp_cur, pb_cur, dsb_cur = cur
            s_next, dp_next = nxt_bufs[0], nxt_bufs[1]
            pb_prev, dsb_prev = prv[2], prv[3]
            i_n, j_n = successor(i, j)
            first = j == 0
            if has_bias:
                drow_all[pl.ds(i_prev, 1), :] = drow
            drow = jnp.where(first, 0.0, drow)
            kc = cols(j)
            sT = _bias_mask_t(cfg, s_cur[...], nb, mask_ref, kc, None if mask_index is None else mask_index(i, j))
            pT = jnp.exp(sT - lse_ref[0, 0, pl.ds(i, 1), :])
            dsT = pT * (dp_cur[...] - delta_s[pl.ds(i, 1), :])
            if has_bias:
                drow = drow + jnp.sum(dsT, axis=0, keepdims=True)
                for hh in range(nh):
                    part = dsT[:, hh * TQ:hh * TQ + LANES]
                    for t in range(1, TQ // LANES):
                        part = part + dsT[:, hh * TQ + t * LANES:hh * TQ + (t + 1) * LANES]
                    dneg_acc[hh, kc, :] += part
            if s_scale is not None:
                dsT = dsT * s_scale
            pb_cur[...] = pT.astype(BF16)
            dsb_cur[...] = dsT.astype(BF16)
            kp = cols(j_prev)
            dv_acc[kp, :] += jnp.dot(pb_prev[...], do2s[rows2(i_prev), :], preferred_element_type=F32)
            dk_acc[kp, :] += jnp.dot(dsb_prev[...], q2s[rows2(i_prev), :], preferred_element_type=F32)
            dq_full = dqT2 + jnp.dot(kTs[j_prev], dsb_prev[...], preferred_element_type=F32)
            dq_all[i_prev] = dq_full
            dqT2 = jnp.where(first, 0.0, dq_full)
            i_nc = jnp.minimum(i_n, nq - 1)
            kn = cols(j_n)
            s_next[...] = _raw_scores_t(cfg, ks[kn, :], qT2s[i_nc])
            dp_next[...] = jnp.dot(vs[kn, :], doT2s[i_nc], preferred_element_type=F32)
            return i_n, j_n, i, j, dqT2, drow

        def two(_, carry):
            i, j, i_prev, j_prev, dqT2, drow = carry
            i, j, i_prev, j_prev, dqT2, drow = half(i, j, i_prev, j_prev, bufs_a, bufs_b, bufs_b, dqT2, drow)
            return half(i, j, i_prev, j_prev, bufs_b, bufs_a, bufs_a, dqT2, drow)

        bufs_a[0][...] = _raw_scores_t(cfg, ks[cols(0), :], qT2s[0])
        bufs_a[1][...] = jnp.dot(vs[cols(0), :], doT2s[0], preferred_element_type=F32)
        bufs_b[2][...] = jnp.zeros((TK, R), BF16)
        bufs_b[3][...] = jnp.zeros((TK, R), BF16)
        zero = jnp.int32(0)
        init = (zero, zero, zero, zero, jnp.zeros((LANES, R), F32), jnp.zeros((1, R), F32))
        _, _, i_prev, j_prev, dqT2, drow = lax.fori_loop(0, n_pairs // 2, two, init)
        kp = cols(j_prev)
        dv_acc[kp, :] += jnp.dot(bufs_b[2][...], do2s[rows2(i_prev), :], preferred_element_type=F32)
        dk_acc[kp, :] += jnp.dot(bufs_b[3][...], q2s[rows2(i_prev), :], preferred_element_type=F32)
        park(i_prev, dqT2 + jnp.dot(kTs[j_prev], bufs_b[3][...], preferred_element_type=F32), drow)

        def fin_q(i, _):
            rows = pl.ds(pl.multiple_of(i * TQ, TQ), TQ)
            dqT2 = dq_all[i]
            dqT = jnp.where(_head_rows(0, True), dqT2[:, 0:TQ], dqT2[:, TQ:2 * TQ]) if pair else dqT2
            dq = dqT.T
            if q_fold is not None:
                dq = dq * q_fold
            if has_rope:
                dq = _rope_bwd(dq, *[t[rows, :] for t in rope_refs])
            if pair:
                dqkv_ref[0, rows, 0:LANES] = dq.astype(BF16)
            else:
                dq_ref[0, rows, :] = dq.astype(BF16)
            if has_bias:
                drow_ref[0, 0, pl.ds(i, 1), :] = drow_all[pl.ds(i, 1), :]
            return 0

        lax.fori_loop(0, nq, fin_q, 0)

        def fin_kv(n, _):
            rows = pl.ds(pl.multiple_of(n * TK, TK), TK)
            dk = dk_acc[rows, :]
            if has_rope:
                dk = _rope_bwd(dk, *[t[rows, :] for t in rope_refs])
            if pair:
                dqkv_ref[0, rows, LANES:2 * LANES] = dk.astype(BF16)
                dqkv_ref[0, rows, 2 * LANES:3 * LANES] = dv_acc[rows, :].astype(BF16)
            else:
                dk_ref[0, rows, :] = dk.astype(BF16)
                dv_ref[0, rows, :] = dv_acc[rows, :].astype(BF16)
            if has_bias:
                x0 = jnp.sum(dneg_acc[0, rows, :], axis=1, keepdims=True)
                x1 = jnp.sum(dneg_acc[1, rows, :], axis=1, keepdims=True)
                dneg_ref[0, rows, :] = jnp.where(lane == 0, x0, jnp.where(lane == 1, x1, 0.0))
            return 0

        lax.fori_loop(0, nk, fin_kv, 0)

    ins, in_specs = _attn_t_inputs(kind, src, S, negc_cols, mask, rope, kv)
    row_spec = pl.BlockSpec((1, S, LANES), lambda b, h: (b, 0, h))
    vec_spec = pl.BlockSpec((1, 1, nq, R), lambda b, h: (b, h, 0, 0))
    ins += [do, o, lse]
    in_specs += [row_spec, row_spec, vec_spec]
    W = cfg["n_blocks"] * LANES
    if pair:
        out_specs = [pl.BlockSpec((1, S, PAIR_W), lambda b, h: (b, 0, h))]
        out_shape = [jax.ShapeDtypeStruct((B, S, 3 * W), BF16)]
        if has_bias:
            out_specs += [row_spec, vec_spec]
            out_shape += [jax.ShapeDtypeStruct((B, S, W), F32), jax.ShapeDtypeStruct((B, cfg["n_blocks"], nq, R), F32)]
    else:
        kv_spec = pl.BlockSpec((1, MEM_LEN, LANES), lambda b, h: (b, 0, h))
        out_specs = [row_spec, kv_spec, kv_spec]
        out_shape = [jax.ShapeDtypeStruct((B, S, W), BF16)] + [jax.ShapeDtypeStruct((B, MEM_LEN, W), BF16)] * 2
    scratch = [pltpu.VMEM((nq, LANES, R), BF16), pltpu.VMEM((Sk, LANES), BF16), pltpu.VMEM((nh * S, LANES), BF16),
               pltpu.VMEM((Sk, LANES), BF16), pltpu.VMEM((nk, LANES, TK), BF16), pltpu.VMEM((nq, LANES, R), BF16),
               pltpu.VMEM((nh * S, LANES), BF16), pltpu.VMEM((nq, R), F32),
               pltpu.VMEM((Sk, LANES), F32), pltpu.VMEM((Sk, LANES), F32)]
    pair_bufs = [pltpu.VMEM((TK, R), F32), pltpu.VMEM((TK, R), F32), pltpu.VMEM((TK, R), BF16), pltpu.VMEM((TK, R), BF16)]
    scratch += pair_bufs + pair_bufs + [pltpu.VMEM((nq, LANES, R), F32)]
    if has_bias:
        scratch += [pltpu.VMEM((nh, Sk, LANES), F32), pltpu.VMEM((nh, Sk, LANES), F32), pltpu.VMEM((nq, R), F32)]
    return pl.pallas_call(
        body, name=kind + "_attn_bwd", grid=(B, cfg["n_blocks"]),
        in_specs=in_specs, out_specs=out_specs, out_shape=out_shape, scratch_shapes=scratch,
        compiler_params=_params(("arbitrary", "arbitrary")),
    )(*ins)


def _sigmoid(g):
    return 1.0 / (1.0 + jnp.exp(-g))


def out_fwd(proj, o_fox, o_dil, o_mem, w_out, x, target, gf, tm):
    T = x.shape[0]

    def body(fg_ref, dg_ref, mg_ref, of_ref, od_ref, om_ref, w_ref, x_ref, t_ref, gf_ref,
             y_ref, dx_ref, dxb_ref, sm_ref):
        parts = []
        for g_ref, o_ref in ((fg_ref, of_ref), (dg_ref, od_ref), (mg_ref, om_ref)):
            g = g_ref[...]
            parts.append((o_ref[...] * (g * _sigmoid(g))).astype(BF16))
        ymix = jnp.concatenate(parts, axis=1)
        y_ref[...] = ymix
        x2 = x_ref[...] + jnp.dot(ymix, w_ref[...], preferred_element_type=F32)
        r = lax.rsqrt(jnp.mean(x2 * x2, axis=-1, keepdims=True) + RMS_EPS)
        yn = x2 * r
        err = yn * gf_ref[...] - t_ref[...]
        loss = 0.5 * jnp.sum(jnp.sum(err * err, axis=-1, keepdims=True) / D_MODEL, axis=0, keepdims=True)
        dyf = err / D_MODEL
        dgf = jnp.sum(dyf * yn, axis=0, keepdims=True)
        dyn = dyf * gf_ref[...]
        dx2 = r * (dyn - yn * jnp.mean(dyn * yn, axis=-1, keepdims=True))
        dx_ref[...] = dx2
        dxb_ref[...] = dx2.astype(BF16)
        row = lax.broadcasted_iota(jnp.int32, (8, D_MODEL), 0)
        upd = jnp.where(row == 0, dgf, jnp.where(row == 1, loss, 0.0))

        @pl.when(pl.program_id(0) == 0)
        def _():
            sm_ref[...] = upd

        @pl.when(pl.program_id(0) != 0)
        def _():
            sm_ref[...] += upd

    def rows(w, col=0):
        return pl.BlockSpec((tm, w), lambda i: (i, col))

    return pl.pallas_call(
        body, name="out_fwd", grid=(T // tm,),
        in_specs=[rows(FOX_W, P_FG // FOX_W), rows(DIL_W, P_DG // DIL_W), rows(MEM_W, P_MG // MEM_W),
                  rows(FOX_W), rows(DIL_W), rows(MEM_W),
                  pl.BlockSpec((MIX_W, D_MODEL), lambda i: (0, 0)),
                  rows(D_MODEL), rows(D_MODEL), pl.BlockSpec((1, D_MODEL), lambda i: (0, 0))],
        out_specs=[rows(MIX_W), rows(D_MODEL), rows(D_MODEL), pl.BlockSpec((8, D_MODEL), lambda i: (0, 0))],
        out_shape=[jax.ShapeDtypeStruct((T, MIX_W), BF16), jax.ShapeDtypeStruct((T, D_MODEL), F32),
                   jax.ShapeDtypeStruct((T, D_MODEL), BF16), jax.ShapeDtypeStruct((8, D_MODEL), F32)],
        compiler_params=_params(("arbitrary",)),
    )(proj, proj, proj, o_fox, o_dil, o_mem, w_out, x, target, gf)


def out_bwd(proj, o_fox, o_dil, o_mem, w_out, dx2b, tm):
    T = dx2b.shape[0]

    def body(fg_ref, dg_ref, mg_ref, of_ref, od_ref, om_ref, w_ref, dx_ref,
             dof_ref, dod_ref, dom_ref, dfg_ref, ddg_ref, dmg_ref):
        dmix = lax.dot_general(dx_ref[...], w_ref[...], (((1,), (1,)), ((), ())), preferred_element_type=F32)
        col = 0
        for g_ref, o_ref, do_ref, dgate_ref in ((fg_ref, of_ref, dof_ref, dfg_ref), (dg_ref, od_ref, dod_ref, ddg_ref),
                                                 (mg_ref, om_ref, dom_ref, dmg_ref)):
            w = g_ref.shape[1]
            d = dmix[:, col:col + w]
            col += w
            g = g_ref[...]
            sg = _sigmoid(g)
            do_ref[...] = d * (g * sg)
            dgate_ref[...] = (d * o_ref[...] * (sg * (1.0 + g * (1.0 - sg)))).astype(BF16)

    def rows(w, col=0):
        return pl.BlockSpec((tm, w), lambda i: (i, col))

    return pl.pallas_call(
        body, name="out_bwd", grid=(T // tm,),
        in_specs=[rows(FOX_W, P_FG // FOX_W), rows(DIL_W, P_DG // DIL_W), rows(MEM_W, P_MG // MEM_W),
                  rows(FOX_W), rows(DIL_W), rows(MEM_W),
                  pl.BlockSpec((MIX_W, D_MODEL), lambda i: (0, 0)), rows(D_MODEL)],
        out_specs=[rows(FOX_W), rows(DIL_W), rows(MEM_W), rows(FOX_W), rows(DIL_W), rows(MEM_W)],
        out_shape=[jax.ShapeDtypeStruct((T, FOX_W), F32), jax.ShapeDtypeStruct((T, DIL_W), F32),
                   jax.ShapeDtypeStruct((T, MEM_W), F32), jax.ShapeDtypeStruct((T, FOX_W), BF16),
                   jax.ShapeDtypeStruct((T, DIL_W), BF16), jax.ShapeDtypeStruct((T, MEM_W), BF16)],
        compiler_params=_params(("arbitrary",)),
    )(proj, proj, proj, o_fox, o_dil, o_mem, w_out, dx2b)


def out_step(proj, o_fox, o_dil, o_mem, w_out, x, target, gf, tm):
    T = x.shape[0]

    def body(fg_ref, dg_ref, mg_ref, of_ref, od_ref, om_ref, w_ref, x_ref, t_ref, gf_ref,
             dx_ref, dof_ref, dod_ref, dom_ref, dfg_ref, ddg_ref, dmg_ref, gw_ref, sm_ref):
        branches = []
        for g_ref, o_ref in ((fg_ref, of_ref), (dg_ref, od_ref), (mg_ref, om_ref)):
            g = g_ref[...]
            sg = _sigmoid(g)
            o = o_ref[...]
            branches.append((g, sg, o))
        ymix = jnp.concatenate([(o * (g * sg)).astype(BF16) for g, sg, o in branches], axis=1)
        x2 = x_ref[...] + jnp.dot(ymix, w_ref[...], preferred_element_type=F32)
        r = lax.rsqrt(jnp.mean(x2 * x2, axis=-1, keepdims=True) + RMS_EPS)
        yn = x2 * r
        err = yn * gf_ref[...] - t_ref[...]
        loss = 0.5 * jnp.sum(jnp.sum(err * err, axis=-1, keepdims=True) / D_MODEL, axis=0, keepdims=True)
        dyf = err / D_MODEL
        dgf = jnp.sum(dyf * yn, axis=0, keepdims=True)
        dyn = dyf * gf_ref[...]
        dx2 = r * (dyn - yn * jnp.mean(dyn * yn, axis=-1, keepdims=True))
        dx_ref[...] = dx2
        dxb = dx2.astype(BF16)
        dmix = lax.dot_general(dxb, w_ref[...], (((1,), (1,)), ((), ())), preferred_element_type=F32)
        col = 0
        for (g, sg, o), do_ref, dgate_ref in zip(branches, (dof_ref, dod_ref, dom_ref), (dfg_ref, ddg_ref, dmg_ref)):
            d = dmix[:, col:col + g.shape[1]]
            col += g.shape[1]
            do_ref[...] = d * (g * sg)
            dgate_ref[...] = (d * o * (sg * (1.0 + g * (1.0 - sg)))).astype(BF16)
        gw = lax.dot_general(ymix, dxb, (((0,), (0,)), ((), ())), preferred_element_type=F32)
        row = lax.broadcasted_iota(jnp.int32, (8, D_MODEL), 0)
        upd = jnp.where(row == 0, dgf, jnp.where(row == 1, loss, 0.0))

        @pl.when(pl.program_id(0) == 0)
        def _():
            sm_ref[...] = upd
            gw_ref[...] = gw

        @pl.when(pl.program_id(0) != 0)
        def _():
            sm_ref[...] += upd
            gw_ref[...] += gw

    def rows(w, col=0):
        return pl.BlockSpec((tm, w), lambda i: (i, col))

    return pl.pallas_call(
        body, name="out_step", grid=(T // tm,),
        in_specs=[rows(FOX_W, P_FG // FOX_W), rows(DIL_W, P_DG // DIL_W), rows(MEM_W, P_MG // MEM_W),
                  rows(FOX_W), rows(DIL_W), rows(MEM_W),
                  pl.BlockSpec((MIX_W, D_MODEL), lambda i: (0, 0)),
                  rows(D_MODEL), rows(D_MODEL), pl.BlockSpec((1, D_MODEL), lambda i: (0, 0))],
        out_specs=[rows(D_MODEL), rows(FOX_W), rows(DIL_W), rows(MEM_W), rows(FOX_W), rows(DIL_W), rows(MEM_W),
                   pl.BlockSpec((MIX_W, D_MODEL), lambda i: (0, 0)), pl.BlockSpec((8, D_MODEL), lambda i: (0, 0))],
        out_shape=[jax.ShapeDtypeStruct((T, D_MODEL), F32), jax.ShapeDtypeStruct((T, FOX_W), F32),
                   jax.ShapeDtypeStruct((T, DIL_W), F32), jax.ShapeDtypeStruct((T, MEM_W), F32),
                   jax.ShapeDtypeStruct((T, FOX_W), BF16), jax.ShapeDtypeStruct((T, DIL_W), BF16),
                   jax.ShapeDtypeStruct((T, MEM_W), BF16), jax.ShapeDtypeStruct((MIX_W, D_MODEL), F32),
                   jax.ShapeDtypeStruct((8, D_MODEL), F32)],
        compiler_params=_params(("arbitrary",)),
    )(proj, proj, proj, o_fox, o_dil, o_mem, w_out, x, target, gf)


def adamw(w, g, m, v, tr, name):
    lead = w.shape[:-2]
    R, C = w.shape[-2:]
    zeros = (0,) * len(lead)

    def body(w_ref, g_ref, m_ref, v_ref, d_ref, mo_ref, vo_ref):
        gv = g_ref[...]
        mn = ADAM_B1 * m_ref[...] + (1.0 - ADAM_B1) * gv
        vn = ADAM_B2 * v_ref[...] + (1.0 - ADAM_B2) * jnp.square(gv)
        m_hat = mn / (1.0 - ADAM_B1 ** ADAM_STEP)
        v_hat = vn / (1.0 - ADAM_B2 ** ADAM_STEP)
        d_ref[...] = -ADAM_LR * (m_hat / (jnp.sqrt(v_hat) + ADAM_EPS) + ADAM_WD * w_ref[...])
        mo_ref[...] = mn
        vo_ref[...] = vn

    spec = pl.BlockSpec((1,) * len(lead) + (tr, C), lambda i: zeros + (i, 0))
    return pl.pallas_call(
        body, name=name, grid=(pl.cdiv(R, tr),),
        in_specs=[spec] * 4, out_specs=[spec] * 3,
        out_shape=[jax.ShapeDtypeStruct(w.shape, F32)] * 3,
        compiler_params=_params(("arbitrary",)),
    )(w, g, m, v)


def _pad_row(v, width):
    return jnp.concatenate([v, jnp.zeros((1, width - v.shape[1]), v.dtype)], axis=1)


def local_grads(x, mem, norm_g, b_forget, mem_norm_g, final_norm_g, loss_target, w_in_p, w_kv, w_out):
    B, S, D = x.shape
    T = B * S
    xt = x.reshape(T, D)
    memt = mem.reshape(B * MEM_LEN, D)
    b_pad = _pad_row(b_forget, LANES)

    h = rms_fwd(xt, norm_g, 512, "rms_x")
    proj = mm_nn(h, w_in_p, 512, PW // 3, "in_proj")
    proj3 = proj.reshape(B, S, PW)
    mh = rms_fwd(memt, mem_norm_g, B * MEM_LEN, "rms_mem")
    mkv = mm_nn(mh, w_kv, B * MEM_LEN, 2 * MEM_W, "mem_kv_proj")
    mkv3 = mkv.reshape(B, MEM_LEN, 2 * MEM_W)

    negc = fox_gate(proj3, b_pad)
    causal = _log_masks_t(S, "causal")
    causal = jnp.concatenate([causal, jnp.zeros_like(causal)], axis=0)
    dilated = _log_masks_t(S, "dilated")
    rope = _rope_tables(S)

    o_fox, lse_fox = attn_fwd4("fox", proj3, S, negc_cols=negc, mask=causal)
    o_dil, lse_dil = attn_fwd4("dil", proj3, S, mask=dilated, rope=rope)
    o_mem, lse_mem = attn_fwd4("mem", proj3, S, kv=mkv3)

    dx2, do_fox, do_dil, do_mem, dfg, ddg, dmg, g_out, small_out = out_step(
        proj, o_fox.reshape(T, FOX_W), o_dil.reshape(T, DIL_W), o_mem.reshape(T, MEM_W), w_out,
        xt, loss_target.reshape(T, D), final_norm_g.reshape(1, D), 256)

    dqkv_fox, dneg, drow = attn_bwd3("fox", proj3, do_fox.reshape(B, S, FOX_W), o_fox, lse_fox, S,
                                     negc_cols=negc, mask=causal)
    (dqkv_dil,) = attn_bwd3("dil", proj3, do_dil.reshape(B, S, DIL_W), o_dil, lse_dil, S, mask=dilated, rope=rope)
    dmq, dmk, dmv = attn_bwd3("mem", proj3, do_mem.reshape(B, S, MEM_W), o_mem, lse_mem, S, kv=mkv3)
    drow = drow.reshape(B, FOX_HEADS // 2, S // TQ, 2, TQ).transpose(0, 1, 3, 2, 4).reshape(B, FOX_HEADS, S)
    drow = jnp.pad(drow, ((0, 0), (0, LANES - FOX_HEADS), (0, 0)))
    dflog, db_part = fox_gate_bwd(drow, dneg, proj3, b_pad)

    groups = [[(dfg, P_FG), (ddg, P_DG), (dmg, P_MG)],
              [(dqkv_fox.reshape(T, 3 * FOX_W), P_FOX), (dflog.reshape(T, LANES), P_FLOG)],
              [(dqkv_dil.reshape(T, 3 * DIL_W), P_DIL), (dmq.reshape(T, MEM_W), P_MQ)]]
    g_in = [mm_tn_multi(h, [arr for arr, _ in grp], 512, "w_in_grad_%d" % n) for n, grp in enumerate(groups)]
    grad_x, dng = in_proj_bwd_rms([piece for grp in groups for piece in grp], w_in_p, xt, norm_g, dx2, 256)

    dmkv = jnp.concatenate([dmk, dmv], axis=2).reshape(B * MEM_LEN, 2 * MEM_W)
    g_kv = mm_tn(mh, dmkv, B * MEM_LEN, 2 * MEM_W, "w_kv_grad")
    dmh = mm_nt(dmkv, w_kv, B * MEM_LEN, D, "mem_kv_bwd")
    _, dmng = rms_bwd(memt, mem_norm_g, dmh, None, B * MEM_LEN, "rms_mem_bwd")

    small = jnp.concatenate([dng[0:1], dmng[0:1], small_out[0:1], _pad_row(db_part[0:1], D), small_out[1:2],
                             jnp.zeros((3, D), F32)], axis=0)
    return grad_x.reshape(B, S, D), g_in, g_kv, g_out, small


def kernel(x, mem, norm_g, w_in, b_forget, mem_norm_g, w_mem_kv, w_out, final_norm_g, loss_target, m_norm_g, m_w_in, m_b_forget, m_mem_norm_g, m_w_mem_kv, m_w_out, m_final_norm_g, v_norm_g, v_w_in, v_b_forget, v_mem_norm_g, v_w_mem_kv, v_w_out, v_final_norm_g):
    D = D_MODEL
    w_in_full, w_kv_full, w_out_full = weight_gather(
        [_pack_cols(w_in).astype(BF16).reshape(w_in.shape[1], PW), w_mem_kv[0].astype(BF16), w_out[0].astype(BF16)])
    grad_x, g_in, g_kv, g_out, small = local_grads(
        x, mem, norm_g, b_forget, mem_norm_g, final_norm_g, loss_target, w_in_full, w_kv_full, w_out_full)

    big = list(g_in) + [g_kv, g_out]
    *from_sibling, csum = grad_exchange_d2d(big, small)
    tiles = (128, 128, 128, 128, 256)
    names = ("w_in_0", "w_in_1", "w_in_2", "w_kv", "w_out")
    chip_parts = [chip_sum(g, got, tr, "chip_sum_" + n) for g, got, tr, n in zip(big, from_sibling, tiles, names)]
    *from_chips, tot = grad_exchange_ici(chip_parts, csum)
    gates, fox, dil, gw_kv, gw_out = [final_sum(got, tr, "final_sum_" + n) for got, tr, n in zip(from_chips, tiles, names)]
    gw_in = _unpack_cols(jnp.concatenate(
        [fox[:, :3 * FOX_W], gates[:, :FOX_W], dil[:, :3 * DIL_W], gates[:, FOX_W:FOX_W + DIL_W], dil[:, 3 * DIL_W:],
         gates[:, FOX_W + DIL_W:], fox[:, 3 * FOX_W:]], axis=1)[None])

    loss = tot[4, 0]
    g_norm, g_mem_norm, g_final, g_b = tot[0:1], tot[1:2], tot[2], tot[3:4, :FOX_HEADS]

    def rows8(*rows):
        rows = [r.reshape(1, -1) for r in rows]
        rows = [_pad_row(r, D) for r in rows]
        return jnp.concatenate(rows + [jnp.zeros((8 - len(rows), D), F32)], axis=0)

    sw = rows8(norm_g, mem_norm_g, final_norm_g, b_forget)
    sm = rows8(m_norm_g, m_mem_norm_g, m_final_norm_g, m_b_forget)
    sv = rows8(v_norm_g, v_mem_norm_g, v_final_norm_g, v_b_forget)
    d_s, m_s, v_s = adamw(sw, tot, sm, sv, 8, "adamw_small")
    d_in, m_in, v_in = adamw(w_in, gw_in, m_w_in, v_w_in, 32, "adamw_w_in")
    d_kv, m_kv, v_kv = adamw(w_mem_kv[0], gw_kv, m_w_mem_kv[0], v_w_mem_kv[0], 128, "adamw_w_kv")
    d_out, m_out, v_out = adamw(w_out[0], gw_out, m_w_out[0], v_w_out[0], 256, "adamw_w_out")

    def small_outs(t):
        return t[0:1], t[3:4, :FOX_HEADS], t[1:2], t[2]

    grads = (g_norm, gw_in, g_b, g_mem_norm, gw_kv[None], gw_out[None], g_final)
    outs = []
    for t, big in ((d_s, (d_in, d_kv, d_out)), (m_s, (m_in, m_kv, m_out)), (v_s, (v_in, v_kv, v_out))):
        n, b, mn, f = small_outs(t)
        outs += [n, big[0], b, mn, big[1][None], big[2][None], f]
    return (loss, grad_x, *grads, *outs)
```

```python
import functools
import math

import numpy as np
import jax
import jax.numpy as jnp
from jax import lax
from jax.experimental import pallas as pl
from jax.experimental.pallas import tpu as pltpu

F32 = jnp.float32
BF16 = jnp.bfloat16

D_MODEL = 1024
HEAD_DIM = 64
FOX_HEADS = 12
DIL_HEADS = 12
MEM_HEADS = 4
MEM_HEAD_DIM = 128
MEM_LEN = 256
FOX_W = FOX_HEADS * HEAD_DIM
DIL_W = DIL_HEADS * HEAD_DIM
MEM_W = MEM_HEADS * MEM_HEAD_DIM
MIX_W = FOX_W + DIL_W + MEM_W
DILATIONS = ((128, 1), (512, 4), (2048, 16))
ROPE_THETA = 500000.0
ROPE_DIM = HEAD_DIM // 4
RMS_EPS = 1e-6
NEG_INF = -1e30
IN_W = 4 * FOX_W + FOX_HEADS + 4 * DIL_W + 2 * MEM_W

ADAM_LR = 0.001
ADAM_B1 = 0.9
ADAM_B2 = 0.999
ADAM_EPS = 1e-08
ADAM_WD = 0.01
ADAM_STEP = 10

N_DEV = 8
LANES = 128
PAIR_W = 3 * LANES
TQ = 256
TK = 256

O_FQ, O_FK, O_FV, O_FG = 0, FOX_W, 2 * FOX_W, 3 * FOX_W
O_FLOG = 4 * FOX_W
O_DQ = O_FLOG + FOX_HEADS
O_DK, O_DV, O_DG = O_DQ + DIL_W, O_DQ + 2 * DIL_W, O_DQ + 3 * DIL_W
O_MQ = O_DQ + 4 * DIL_W
O_MG = O_MQ + MEM_W
P_FOX = 0
P_FG = P_FOX + 3 * FOX_W
P_DIL = P_FG + FOX_W
P_DG = P_DIL + 3 * DIL_W
P_MQ = P_DG + DIL_W
P_MG = P_MQ + MEM_W
P_FLOG = P_MG + MEM_W
PW = P_FLOG + LANES

VMEM_LIMIT = 56 * 1024 * 1024


def _pack_pieces():
    pieces = []
    for base in (O_FQ, O_DQ):
        seg = []
        for hp in range(FOX_HEADS // 2):
            for part in range(3):
                seg.append((base + part * FOX_W + hp * LANES, LANES))
        pieces.append(seg)
    fox, dil = pieces
    return fox + [(O_FG, FOX_W)] + dil + [(O_DG, DIL_W), (O_MQ, MEM_W), (O_MG, MEM_W), (O_FLOG, FOX_HEADS)]


def _pack_cols(w):
    parts = [w[..., s:s + n] for s, n in _pack_pieces()]
    parts.append(jnp.zeros(w.shape[:-1] + (LANES - FOX_HEADS,), w.dtype))
    return jnp.concatenate(parts, axis=-1)


def _unpack_cols(g):
    runs = []
    pos = 0
    for s, n in _pack_pieces():
        runs.append((s, n, pos))
        pos += n
    runs.sort()
    return jnp.concatenate([g[..., p:p + n] for s, n, p in runs], axis=-1)


def _params(sem=None, **kw):
    return pltpu.CompilerParams(dimension_semantics=sem, vmem_limit_bytes=VMEM_LIMIT, **kw)


def _mesh_pos():
    return lax.axis_index("x"), lax.axis_index("y"), lax.axis_index("c")


def _flip(v, d):
    return 1 - v if d else v


_RELATIONS = [(dx, dy, dc) for dx in (0, 1) for dy in (0, 1) for dc in (0, 1)][1:]


def weight_gather(shards):
    n_arr = len(shards)
    rows = [s.shape[0] for s in shards]

    def body(*refs):
        in_refs = refs[:n_arr]
        out_refs = refs[n_arr:2 * n_arr]
        send_sems, recv_sems, local_sems = refs[2 * n_arr:]
        x, y, c = _mesh_pos()
        me, sibling = (x, y, c), (x, y, 1 - c)
        x_nbr, y_nbr, diag = (1 - x, y, c), (x, 1 - y, c), (1 - x, 1 - y, c)
        north = c == 1
        relay_from = (jnp.where(north, 1 - x, x), jnp.where(north, y, 1 - y), c)
        relay_to = (jnp.where(north, x, 1 - x), jnp.where(north, 1 - y, y), c)
        k_from = jnp.where(north, 1, 2)
        k_to = 3 - k_from

        def block(a, pos):
            px, py, pc = pos
            return out_refs[a].at[pl.ds((4 * px + 2 * py + pc) * rows[a], rows[a]), :]

        def copy(a, k, blk, to, src=None):
            return pltpu.make_async_remote_copy(
                src_ref=block(a, blk) if src is None else src, dst_ref=block(a, blk),
                send_sem=send_sems.at[a, k], recv_sem=recv_sems.at[a, k],
                device_id=to, device_id_type=pl.DeviceIdType.MESH)

        started = []
        mine = []
        for a in range(n_arr):
            cp = pltpu.make_async_copy(in_refs[a], block(a, me), local_sems.at[a])
            cp.start()
            mine.append(cp)
            first = [copy(a, 0, me, sibling, src=in_refs[a]), copy(a, 1, me, x_nbr, src=in_refs[a]),
                     copy(a, 2, me, y_nbr, src=in_refs[a])]
            for cp in first:
                cp.start()
            started += first
        for a in range(n_arr):
            copy(a, k_from, relay_from, me).wait_recv()
            second_hop = copy(a, 3, relay_from, relay_to)
            second_hop.start()
            passed = copy(a, 3 + k_from, relay_from, sibling)
            passed.start()
            started += [second_hop, passed]
        for a in range(n_arr):
            copy(a, k_to, relay_to, me).wait_recv()
            passed = copy(a, 3 + k_to, relay_to, sibling)
            passed.start()
            started.append(passed)
        for a in range(n_arr):
            copy(a, 3, diag, me).wait_recv()
            passed = copy(a, 6, diag, sibling)
            passed.start()
            started.append(passed)
        for a in range(n_arr):
            copy(a, 0, sibling, me).wait_recv()
            for k, chip in ((4, x_nbr), (5, y_nbr), (6, diag)):
                copy(a, k, (chip[0], chip[1], 1 - c), me).wait_recv()
        for cp in started:
            cp.wait_send()
        for cp in mine:
            cp.wait()

    any_spec = pl.BlockSpec(memory_space=pl.ANY)
    return pl.pallas_call(
        body, name="weight_gather",
        out_shape=[jax.ShapeDtypeStruct((N_DEV * s.shape[0], s.shape[1]), s.dtype) for s in shards],
        in_specs=[any_spec] * n_arr, out_specs=[any_spec] * n_arr,
        scratch_shapes=[pltpu.SemaphoreType.DMA((n_arr, 7)), pltpu.SemaphoreType.DMA((n_arr, 7)),
                        pltpu.SemaphoreType.DMA((n_arr,))],
    )(*shards)


def grad_exchange(grads, small):
    arrs = list(grads) + [small]
    n_arr = len(arrs)
    rows = [g.shape[0] // N_DEV for g in grads] + [small.shape[0]]

    def body(*refs):
        in_refs = refs[:n_arr]
        out_refs = refs[n_arr:2 * n_arr]
        send_sems, recv_sems, local_sems = refs[2 * n_arr:]
        x, y, c = _mesh_pos()
        me = 4 * x + 2 * y + c

        def src(a, idx):
            if a == n_arr - 1:
                return in_refs[a]
            return in_refs[a].at[pl.ds(idx * rows[a], rows[a]), :]

        def copy(a, k):
            dx, dy, dc = _RELATIONS[k]
            px, py, pc = _flip(x, dx), _flip(y, dy), _flip(c, dc)
            peer = 4 * px + 2 * py + pc
            send = pltpu.make_async_remote_copy(
                src_ref=src(a, peer), dst_ref=out_refs[a].at[me],
                send_sem=send_sems.at[a, k], recv_sem=recv_sems.at[a, k],
                device_id=(px, py, pc), device_id_type=pl.DeviceIdType.MESH)
            recv = pltpu.make_async_remote_copy(
                src_ref=src(a, peer), dst_ref=out_refs[a].at[peer],
                send_sem=send_sems.at[a, k], recv_sem=recv_sems.at[a, k],
                device_id=(px, py, pc), device_id_type=pl.DeviceIdType.MESH)
            return send, recv

        mine = []
        pairs = []
        for a in range(n_arr):
            cp = pltpu.make_async_copy(src(a, me), out_refs[a].at[me], local_sems.at[a])
            cp.start()
            mine.append(cp)
            for k in range(7):
                send, recv = copy(a, k)
                send.start()
                pairs.append((send, recv))
        for send, recv in pairs:
            recv.wait_recv()
        for send, recv in pairs:
            send.wait_send()
        for cp in mine:
            cp.wait()

    any_spec = pl.BlockSpec(memory_space=pl.ANY)
    return pl.pallas_call(
        body, name="grad_exchange",
        out_shape=[jax.ShapeDtypeStruct((N_DEV, r, a.shape[1]), a.dtype) for r, a in zip(rows, arrs)],
        in_specs=[any_spec] * n_arr, out_specs=[any_spec] * n_arr,
        scratch_shapes=[pltpu.SemaphoreType.DMA((n_arr, 7)), pltpu.SemaphoreType.DMA((n_arr, 7)),
                        pltpu.SemaphoreType.DMA((n_arr,))],
    )(*arrs)


def slot_sum(slots, tr, name):
    _, R, C = slots.shape

    def body(s_ref, o_ref):
        acc = s_ref[0]
        for d in range(1, N_DEV):
            acc = acc + s_ref[d]
        o_ref[...] = acc

    return pl.pallas_call(
        body, name=name, grid=(R // tr,),
        in_specs=[pl.BlockSpec((N_DEV, tr, C), lambda i: (0, i, 0))],
        out_specs=pl.BlockSpec((tr, C), lambda i: (i, 0)),
        out_shape=jax.ShapeDtypeStruct((R, C), slots.dtype),
        compiler_params=_params(("arbitrary",)),
    )(slots)


N_CHIP = 4
_OTHER_CHIPS = [(1, 0), (0, 1), (1, 1)]


def grad_exchange_d2d(grads, small):
    n_big = len(grads)
    rows = [g.shape[0] // N_DEV for g in grads]

    def body(*refs):
        g_refs = refs[:n_big]
        small_ref = refs[n_big]
        out_refs = refs[n_big + 1:2 * n_big + 1]
        csum_ref = refs[2 * n_big + 1]
        land, send_sems, recv_sems = refs[2 * n_big + 2:]
        x, y, c = _mesh_pos()
        sibling = (x, y, 1 - c)
        copies = []
        for a in range(n_big):
            for q in range(N_CHIP):
                copies.append(pltpu.make_async_remote_copy(
                    src_ref=g_refs[a].at[pl.ds((2 * q + 1 - c) * rows[a], rows[a]), :], dst_ref=out_refs[a].at[q],
                    send_sem=send_sems.at[a, q], recv_sem=recv_sems.at[a, q],
                    device_id=sibling, device_id_type=pl.DeviceIdType.MESH))
        copies.append(pltpu.make_async_remote_copy(
            src_ref=small_ref, dst_ref=land, send_sem=send_sems.at[n_big, 0], recv_sem=recv_sems.at[n_big, 0],
            device_id=sibling, device_id_type=pl.DeviceIdType.MESH))
        for cp in copies:
            cp.start()
        for cp in copies:
            cp.wait_recv()
        for cp in copies:
            cp.wait_send()
        csum_ref[...] = small_ref[...] + land[...]

    any_spec = pl.BlockSpec(memory_space=pl.ANY)
    vmem_spec = pl.BlockSpec(memory_space=pltpu.VMEM)
    return pl.pallas_call(
        body, name="grad_exchange_d2d",
        out_shape=[jax.ShapeDtypeStruct((N_CHIP, r, g.shape[1]), g.dtype) for r, g in zip(rows, grads)]
        + [jax.ShapeDtypeStruct(small.shape, small.dtype)],
        in_specs=[any_spec] * n_big + [vmem_spec], out_specs=[any_spec] * n_big + [vmem_spec],
        scratch_shapes=[pltpu.VMEM(small.shape, small.dtype),
                        pltpu.SemaphoreType.DMA((n_big + 1, N_CHIP)), pltpu.SemaphoreType.DMA((n_big + 1, N_CHIP))],
    )(*grads, small)


def chip_sum(g, got, tr, name):
    _, rows, cols = got.shape
    g4 = g.reshape(N_CHIP, 2, rows, cols)
    core = lax.axis_index("c").astype(jnp.int32).reshape(1)

    def body(c_ref, g_ref, r_ref, o_ref):
        o_ref[0] = (g_ref[0, 0] + r_ref[0]).astype(BF16)

    return pl.pallas_call(
        body, name=name,
        grid_spec=pltpu.PrefetchScalarGridSpec(
            num_scalar_prefetch=1, grid=(N_CHIP, rows // tr),
            in_specs=[pl.BlockSpec((1, 1, tr, cols), lambda q, i, w: (q, w[0], i, 0)),
                      pl.BlockSpec((1, tr, cols), lambda q, i, w: (q, i, 0))],
            out_specs=pl.BlockSpec((1, tr, cols), lambda q, i, w: (q, i, 0))),
        out_shape=jax.ShapeDtypeStruct((N_CHIP, rows, cols), BF16),
        compiler_params=_params(("arbitrary", "arbitrary")),
    )(core, g4, got)


def grad_exchange_ici(parts, csum):
    n_big = len(parts)

    def body(*refs):
        p_refs = refs[:n_big]
        csum_ref = refs[n_big]
        out_refs = refs[n_big + 1:2 * n_big + 1]
        tot_ref = refs[2 * n_big + 1]
        land, send_sems, recv_sems, local_sems = refs[2 * n_big + 2:]
        x, y, c = _mesh_pos()
        q_me = 2 * x + y
        land[q_me] = csum_ref[...]
        mine = [pltpu.make_async_copy(p_refs[a].at[q_me], out_refs[a].at[q_me], local_sems.at[a]) for a in range(n_big)]
        for cp in mine:
            cp.start()
        sends, recvs = [], []
        for j, (dx, dy) in enumerate(_OTHER_CHIPS):
            px, py = _flip(x, dx), _flip(y, dy)
            q_peer = 2 * px + py
            for a in range(n_big + 1):
                src = p_refs[a].at[q_peer] if a < n_big else csum_ref
                dst = out_refs[a] if a < n_big else land
                common = dict(send_sem=send_sems.at[a, j], recv_sem=recv_sems.at[a, j],
                              device_id=(px, py, c), device_id_type=pl.DeviceIdType.MESH)
                sends.append(pltpu.make_async_remote_copy(src_ref=src, dst_ref=dst.at[q_me], **common))
                recvs.append(pltpu.make_async_remote_copy(src_ref=src, dst_ref=dst.at[q_peer], **common))
        for cp in sends:
            cp.start()
        for cp in recvs:
            cp.wait_recv()
        for cp in sends:
            cp.wait_send()
        for cp in mine:
            cp.wait()
        tot = land[0]
        for q in range(1, N_CHIP):
            tot = tot + land[q]
        tot_ref[...] = tot

    any_spec = pl.BlockSpec(memory_space=pl.ANY)
    vmem_spec = pl.BlockSpec(memory_space=pltpu.VMEM)
    return pl.pallas_call(
        body, name="grad_exchange_ici",
        out_shape=[jax.ShapeDtypeStruct(p.shape, p.dtype) for p in parts] + [jax.ShapeDtypeStruct(csum.shape, csum.dtype)],
        in_specs=[any_spec] * n_big + [vmem_spec], out_specs=[any_spec] * n_big + [vmem_spec],
        scratch_shapes=[pltpu.VMEM((N_CHIP,) + csum.shape, csum.dtype),
                        pltpu.SemaphoreType.DMA((n_big + 1, 3)), pltpu.SemaphoreType.DMA((n_big + 1, 3)),
                        pltpu.SemaphoreType.DMA((n_big,))],
    )(*parts, csum)


def final_sum(got, tr, name):
    _, rows, cols = got.shape

    def body(got_ref, o_ref):
        acc = got_ref[0].astype(F32)
        for q in range(1, N_CHIP):
            acc = acc + got_ref[q].astype(F32)
        o_ref[...] = acc

    return pl.pallas_call(
        body, name=name, grid=(rows // tr,),
        in_specs=[pl.BlockSpec((N_CHIP, tr, cols), lambda i: (0, i, 0))],
        out_specs=pl.BlockSpec((tr, cols), lambda i: (i, 0)),
        out_shape=jax.ShapeDtypeStruct((rows, cols), F32),
        compiler_params=_params(("arbitrary",)),
    )(got)


_HBM = pl.BlockSpec(memory_space=pltpu.HBM)
_SEM = pl.BlockSpec(memory_space=pltpu.SEMAPHORE)
_EFFECT = pltpu.SideEffectType.DATAFLOW_SIDE_EFFECTING


def _old_early_copies(src_refs, land_refs, send_sems, recv_sems, rows):
    x, y, c = _mesh_pos()
    me = 4 * x + 2 * y + c
    copies = []
    for a in range(len(src_refs)):
        for k, (dx, dy, dc) in enumerate(_RELATIONS):
            px, py, pc = _flip(x, dx), _flip(y, dy), _flip(c, dc)
            peer = 4 * px + 2 * py + pc
            copies.append(pltpu.make_async_remote_copy(
                src_ref=src_refs[a].at[pl.ds(peer * rows[a], rows[a]), :], dst_ref=land_refs[a].at[me],
                send_sem=send_sems.at[a, k], recv_sem=recv_sems.at[a, k],
                device_id=(px, py, pc), device_id_type=pl.DeviceIdType.MESH))
    return copies


def _old_early_exchange_start(srcs, name):
    n = len(srcs)
    rows = [s.shape[0] // N_DEV for s in srcs]
    lands = [lax.empty((N_DEV, r, s.shape[1]), s.dtype) for r, s in zip(rows, srcs)]

    def body(*refs):
        src_refs, land_refs = refs[:n], refs[n:2 * n]
        send_sems, recv_sems = refs[2 * n], refs[2 * n + 1]
        token = refs[-1]
        for cp in _early_copies(src_refs, land_refs, send_sems, recv_sems, rows):
            cp.start()
        token[...] = jnp.zeros_like(token)

    hbm = lambda a: pltpu.HBM(a.shape, a.dtype)
    outs = pl.pallas_call(
        body, name=name,
        out_shape=[pltpu.SemaphoreType.DMA((n, 7)), pltpu.SemaphoreType.DMA((n, 7))]
        + [hbm(a) for a in srcs] + [hbm(a) for a in lands] + [jax.ShapeDtypeStruct((8, LANES), F32)],
        in_specs=[_HBM] * (2 * n),
        out_specs=[_SEM, _SEM] + [_HBM] * (2 * n) + [pl.BlockSpec(memory_space=pltpu.VMEM)],
        input_output_aliases={i: 2 + i for i in range(2 * n)},
        compiler_params=pltpu.CompilerParams(has_side_effects=_EFFECT),
    )(*[pltpu.with_memory_space_constraint(a, pltpu.HBM) for a in list(srcs) + lands])
    return dict(sems=outs[:2], srcs=outs[2:2 + n], lands=outs[2 + n:2 + 2 * n], rows=rows), outs[-1]


def _old_early_exchange_wait(handle, after, name):
    n = len(handle["srcs"])
    rows = handle["rows"]

    def body(*refs):
        src_refs, land_refs = refs[:n], refs[n:2 * n]
        send_sems, recv_sems = refs[2 * n], refs[2 * n + 1]
        for cp in _early_copies(src_refs, land_refs, send_sems, recv_sems, rows):
            cp.wait_send()
            cp.wait_recv()

    hbm = lambda a: pltpu.HBM(a.shape, a.dtype)
    ins = list(handle["srcs"]) + list(handle["lands"])
    outs = pl.pallas_call(
        body, name=name,
        out_shape=[hbm(a) for a in ins],
        in_specs=[_HBM] * (2 * n) + [_SEM, _SEM, pl.BlockSpec(memory_space=pl.ANY)],
        out_specs=[_HBM] * (2 * n),
        input_output_aliases={i: i for i in range(2 * n)},
        compiler_params=pltpu.CompilerParams(has_side_effects=_EFFECT),
    )(*ins, *handle["sems"], after)
    return outs[:n], outs[n:]


def _old_slot_sum8(src, land, tr, name):
    _, rows, cols = land.shape
    x, y, c = _mesh_pos()
    me = (4 * x + 2 * y + c).astype(jnp.int32).reshape(1)

    def body(me_ref, src_ref, land_ref, o_ref):
        acc = None
        for d in range(N_DEV):
            term = jnp.where(d == me_ref[0], src_ref[0], land_ref[d]).astype(F32)
            acc = term if acc is None else acc + term
        o_ref[...] = acc

    return pl.pallas_call(
        body, name=name,
        grid_spec=pltpu.PrefetchScalarGridSpec(
            num_scalar_prefetch=1, grid=(rows // tr,),
            in_specs=[pl.BlockSpec((1, tr, cols), lambda i, w: (w[0], i, 0)),
                      pl.BlockSpec((N_DEV, tr, cols), lambda i, w: (0, i, 0))],
            out_specs=pl.BlockSpec((tr, cols), lambda i, w: (i, 0))),
        out_shape=jax.ShapeDtypeStruct((rows, cols), F32),
        compiler_params=_params(("arbitrary",)),
    )(me, src.reshape(N_DEV, rows, cols), land)


def _early_copies(src_refs, land_refs, send_sems, recv_sems, rows):
    x, y, c = _mesh_pos()
    me = 4 * x + 2 * y + c
    copies = []
    for a in range(len(src_refs)):
        for dx, dy, dc in _RELATIONS:
            px, py, pc = _flip(x, dx), _flip(y, dy), _flip(c, dc)
            peer = 4 * px + 2 * py + pc
            copies.append(pltpu.make_async_remote_copy(
                src_ref=src_refs[a].at[pl.ds(peer * rows[a], rows[a]), :],
                dst_ref=land_refs[a].at[pl.ds(me * rows[a], rows[a]), :],
                send_sem=send_sems[a], recv_sem=recv_sems[a],
                device_id=(px, py, pc), device_id_type=pl.DeviceIdType.MESH))
    return copies


def early_exchange_start(srcs, name):
    n = len(srcs)
    rows = [s.shape[0] // N_DEV for s in srcs]
    lands = [lax.empty(s.shape, s.dtype) for s in srcs]

    def body(*refs):
        src_refs, land_refs = refs[:n], refs[n:2 * n]
        send_sems, recv_sems = refs[2 * n:3 * n], refs[3 * n:4 * n]
        token = refs[-1]
        for cp in _early_copies(src_refs, land_refs, send_sems, recv_sems, rows):
            cp.start()
        token[...] = jnp.zeros_like(token)

    hbm = lambda a: pltpu.HBM(a.shape, a.dtype)
    outs = pl.pallas_call(
        body, name=name,
        out_shape=[pltpu.SemaphoreType.DMA(())] * (2 * n)
        + [hbm(a) for a in srcs] + [hbm(a) for a in lands] + [jax.ShapeDtypeStruct((8, LANES), F32)],
        in_specs=[_HBM] * (2 * n),
        out_specs=[_SEM] * (2 * n) + [_HBM] * (2 * n) + [pl.BlockSpec(memory_space=pltpu.VMEM)],
        input_output_aliases={i: 2 * n + i for i in range(2 * n)},
        compiler_params=pltpu.CompilerParams(has_side_effects=_EFFECT),
    )(*[pltpu.with_memory_space_constraint(a, pltpu.HBM) for a in list(srcs) + lands])
    return dict(sems=outs[:2 * n], srcs=outs[2 * n:3 * n], lands=outs[3 * n:4 * n], rows=rows), outs[-1]


def early_exchange_wait(handle, after, name):
    n = len(handle["srcs"])
    rows = handle["rows"]

    def body(*refs):
        src_refs, land_refs = refs[:n], refs[n:2 * n]
        send_sems, recv_sems = refs[2 * n:3 * n], refs[3 * n:4 * n]
        x, y, c = _mesh_pos()
        for a in range(n):
            seven = pl.ds(0, 7 * rows[a])
            all_seven = pltpu.make_async_remote_copy(
                src_ref=src_refs[a].at[seven, :], dst_ref=land_refs[a].at[seven, :],
                send_sem=send_sems[a], recv_sem=recv_sems[a],
                device_id=(x, y, c), device_id_type=pl.DeviceIdType.MESH)
            all_seven.wait_send()
            all_seven.wait_recv()

    hbm = lambda a: pltpu.HBM(a.shape, a.dtype)
    ins = list(handle["srcs"]) + list(handle["lands"])
    outs = pl.pallas_call(
        body, name=name,
        out_shape=[hbm(a) for a in ins],
        in_specs=[_HBM] * (2 * n) + [_SEM] * (2 * n) + [pl.BlockSpec(memory_space=pl.ANY)],
        out_specs=[_HBM] * (2 * n),
        input_output_aliases={i: i for i in range(2 * n)},
        compiler_params=pltpu.CompilerParams(has_side_effects=_EFFECT),
    )(*ins, *handle["sems"], after)
    return outs[:n], outs[n:]


def slot_sum8(src, land, tr, name):
    rows, cols = land.shape[0] // N_DEV, land.shape[1]
    x, y, c = _mesh_pos()
    me = (4 * x + 2 * y + c).astype(jnp.int32).reshape(1)

    def body(me_ref, src_ref, land_ref, o_ref):
        acc = None
        for d in range(N_DEV):
            term = jnp.where(d == me_ref[0], src_ref[0], land_ref[d]).astype(F32)
            acc = term if acc is None else acc + term
        o_ref[...] = acc

    return pl.pallas_call(
        body, name=name,
        grid_spec=pltpu.PrefetchScalarGridSpec(
            num_scalar_prefetch=1, grid=(rows // tr,),
            in_specs=[pl.BlockSpec((1, tr, cols), lambda i, w: (w[0], i, 0)),
                      pl.BlockSpec((N_DEV, tr, cols), lambda i, w: (0, i, 0))],
            out_specs=pl.BlockSpec((tr, cols), lambda i, w: (i, 0))),
        out_shape=jax.ShapeDtypeStruct((rows, cols), F32),
        compiler_params=_params(("arbitrary",)),
    )(me, src.reshape(N_DEV, rows, cols), land.reshape(N_DEV, rows, cols))


def mm_tn_multi(a, bs, tt, name, out_dtype=F32):
    T, K = a.shape
    widths = [b.shape[1] for b in bs]
    steps = T // tt

    def body(a_ref, *rest):
        b_refs, o_ref, acc = rest[:-2], rest[-2], rest[-1]
        av = a_ref[...]
        parts = [lax.dot_general(av, b_ref[...], (((0,), (0,)), ((), ())), preferred_element_type=F32)
                 for b_ref in b_refs]

        @pl.when(pl.program_id(0) == 0)
        def _():
            col = 0
            for part, w in zip(parts, widths):
                acc[:, col:col + w] = part
                col += w

        @pl.when(pl.program_id(0) != 0)
        def _():
            col = 0
            for part, w in zip(parts, widths):
                acc[:, col:col + w] += part
                col += w

        @pl.when(pl.program_id(0) == steps - 1)
        def _():
            o_ref[...] = acc[...].astype(out_dtype)

    return pl.pallas_call(
        body, name=name, grid=(steps,),
        in_specs=[pl.BlockSpec((tt, K), lambda t: (t, 0))] + [pl.BlockSpec((tt, w), lambda t: (t, 0)) for w in widths],
        out_specs=pl.BlockSpec((K, sum(widths)), lambda t: (0, 0)),
        out_shape=jax.ShapeDtypeStruct((K, sum(widths)), out_dtype),
        scratch_shapes=[pltpu.VMEM((K, sum(widths)), F32)],
        compiler_params=_params(("arbitrary",)),
    )(a, *bs)


def rms_fwd(x, g, tm, name):
    M, K = x.shape

    def body(x_ref, g_ref, o_ref):
        xv = x_ref[...]
        r = lax.rsqrt(jnp.mean(xv * xv, axis=-1, keepdims=True) + RMS_EPS)
        o_ref[...] = ((xv * r) * g_ref[...]).astype(BF16)

    return pl.pallas_call(
        body, name=name, grid=(M // tm,),
        in_specs=[pl.BlockSpec((tm, K), lambda i: (i, 0)), pl.BlockSpec((1, K), lambda i: (0, 0))],
        out_specs=pl.BlockSpec((tm, K), lambda i: (i, 0)),
        out_shape=jax.ShapeDtypeStruct((M, K), BF16),
        compiler_params=_params(("arbitrary",)),
    )(x, g)


def rms_bwd(x, g, dh, dres, tm, name):
    M, K = x.shape
    has_res = dres is not None

    def body(*refs):
        if has_res:
            x_ref, g_ref, dh_ref, dres_ref, dx_ref, dg_ref = refs
        else:
            x_ref, g_ref, dh_ref, dx_ref, dg_ref = refs
        xv = x_ref[...]
        r = lax.rsqrt(jnp.mean(xv * xv, axis=-1, keepdims=True) + RMS_EPS)
        xn = xv * r
        dhv = dh_ref[...]
        dxn = dhv * g_ref[...]
        dx = r * (dxn - xn * jnp.mean(dxn * xn, axis=-1, keepdims=True))
        if has_res:
            dx = dx + dres_ref[...]
        dx_ref[...] = dx
        part = jnp.sum(dhv * xn, axis=0, keepdims=True)
        row = lax.broadcasted_iota(jnp.int32, (8, K), 0)
        upd = jnp.where(row == 0, part, 0.0)

        @pl.when(pl.program_id(0) == 0)
        def _():
            dg_ref[...] = upd

        @pl.when(pl.program_id(0) != 0)
        def _():
            dg_ref[...] += upd

    row_spec = pl.BlockSpec((tm, K), lambda i: (i, 0))
    ins = [x, g, dh] + ([dres] if has_res else [])
    in_specs = [row_spec, pl.BlockSpec((1, K), lambda i: (0, 0)), row_spec] + ([row_spec] if has_res else [])
    return pl.pallas_call(
        body, name=name, grid=(M // tm,),
        in_specs=in_specs,
        out_specs=[row_spec, pl.BlockSpec((8, K), lambda i: (0, 0))],
        out_shape=[jax.ShapeDtypeStruct((M, K), F32), jax.ShapeDtypeStruct((8, K), F32)],
        compiler_params=_params(("arbitrary",)),
    )(*ins)


def mm_nn(a, b, tm, tn, name):
    M, K = a.shape
    N = b.shape[1]

    def body(a_ref, b_ref, o_ref):
        o_ref[...] = jnp.dot(a_ref[...], b_ref[...], preferred_element_type=F32)

    return pl.pallas_call(
        body, name=name, grid=(N // tn, M // tm),
        in_specs=[pl.BlockSpec((tm, K), lambda j, i: (i, 0)), pl.BlockSpec((K, tn), lambda j, i: (0, j))],
        out_specs=pl.BlockSpec((tm, tn), lambda j, i: (i, j)),
        out_shape=jax.ShapeDtypeStruct((M, N), F32),
        compiler_params=_params(("arbitrary", "arbitrary")),
    )(a, b)


def mm_nt(a, b, tm, tk, name):
    M, K = a.shape
    N = b.shape[0]

    def body(a_ref, b_ref, o_ref):
        part = lax.dot_general(a_ref[...], b_ref[...], (((1,), (1,)), ((), ())), preferred_element_type=F32)

        @pl.when(pl.program_id(1) == 0)
        def _():
            o_ref[...] = part

        @pl.when(pl.program_id(1) != 0)
        def _():
            o_ref[...] += part

    return pl.pallas_call(
        body, name=name, grid=(M // tm, K // tk),
        in_specs=[pl.BlockSpec((tm, tk), lambda i, k: (i, k)), pl.BlockSpec((N, tk), lambda i, k: (0, k))],
        out_specs=pl.BlockSpec((tm, N), lambda i, k: (i, 0)),
        out_shape=jax.ShapeDtypeStruct((M, N), F32),
        compiler_params=_params(("arbitrary", "arbitrary")),
    )(a, b)


def mm_tn(a, b, tt, tn, name):
    T, K = a.shape
    N = b.shape[1]

    def body(a_ref, b_ref, o_ref):
        part = lax.dot_general(a_ref[...], b_ref[...], (((0,), (0,)), ((), ())), preferred_element_type=F32)

        @pl.when(pl.program_id(1) == 0)
        def _():
            o_ref[...] = part

        @pl.when(pl.program_id(1) != 0)
        def _():
            o_ref[...] += part

    return pl.pallas_call(
        body, name=name, grid=(N // tn, T // tt),
        in_specs=[pl.BlockSpec((tt, K), lambda j, t: (t, 0)), pl.BlockSpec((tt, tn), lambda j, t: (t, j))],
        out_specs=pl.BlockSpec((K, tn), lambda j, t: (0, j)),
        out_shape=jax.ShapeDtypeStruct((K, N), F32),
        compiler_params=_params(("arbitrary", "arbitrary")),
    )(a, b)


def _old_mm_tn_multi(a, bs, tt, name):
    T, K = a.shape
    widths = [b.shape[1] for b in bs]

    def body(a_ref, *rest):
        b_refs, o_ref = rest[:-1], rest[-1]
        av = a_ref[...]
        parts = [lax.dot_general(av, b_ref[...], (((0,), (0,)), ((), ())), preferred_element_type=F32)
                 for b_ref in b_refs]

        @pl.when(pl.program_id(0) == 0)
        def _():
            col = 0
            for part, w in zip(parts, widths):
                o_ref[:, col:col + w] = part
                col += w

        @pl.when(pl.program_id(0) != 0)
        def _():
            col = 0
            for part, w in zip(parts, widths):
                o_ref[:, col:col + w] += part
                col += w

    return pl.pallas_call(
        body, name=name, grid=(T // tt,),
        in_specs=[pl.BlockSpec((tt, K), lambda t: (t, 0))] + [pl.BlockSpec((tt, w), lambda t: (t, 0)) for w in widths],
        out_specs=pl.BlockSpec((K, sum(widths)), lambda t: (0, 0)),
        out_shape=jax.ShapeDtypeStruct((K, sum(widths)), F32),
        compiler_params=_params(("arbitrary",)),
    )(a, *bs)


def mm_nt_multi(pieces, w, tm, name):
    M = pieces[0][0].shape[0]
    N, K = w.shape

    def body(*refs):
        p_refs, w_ref, o_ref = refs[:-2], refs[-2], refs[-1]
        acc = None
        for p_ref, (arr, col) in zip(p_refs, pieces):
            part = lax.dot_general(p_ref[...], w_ref[:, col:col + arr.shape[1]], (((1,), (1,)), ((), ())),
                                   preferred_element_type=F32)
            acc = part if acc is None else acc + part
        o_ref[...] = acc

    return pl.pallas_call(
        body, name=name, grid=(M // tm,),
        in_specs=[pl.BlockSpec((tm, arr.shape[1]), lambda i: (i, 0)) for arr, _ in pieces]
        + [pl.BlockSpec((N, K), lambda i: (0, 0))],
        out_specs=pl.BlockSpec((tm, N), lambda i: (i, 0)),
        out_shape=jax.ShapeDtypeStruct((M, N), F32),
        compiler_params=_params(("arbitrary",)),
    )(*[arr for arr, _ in pieces], w)


def in_proj_bwd_rms(pieces, w, x, g, dres, tm):
    M, N = x.shape

    def body(*refs):
        n = len(pieces)
        p_refs, w_ref, x_ref, g_ref, dres_ref, dx_ref, dg_ref = refs[:n], *refs[n:]
        dh = None
        for p_ref, (arr, col) in zip(p_refs, pieces):
            part = lax.dot_general(p_ref[...], w_ref[:, col:col + arr.shape[1]], (((1,), (1,)), ((), ())),
                                   preferred_element_type=F32)
            dh = part if dh is None else dh + part
        xv = x_ref[...]
        r = lax.rsqrt(jnp.mean(xv * xv, axis=-1, keepdims=True) + RMS_EPS)
        xn = xv * r
        dxn = dh * g_ref[...]
        dx_ref[...] = r * (dxn - xn * jnp.mean(dxn * xn, axis=-1, keepdims=True)) + dres_ref[...]
        row = lax.broadcasted_iota(jnp.int32, (8, N), 0)
        upd = jnp.where(row == 0, jnp.sum(dh * xn, axis=0, keepdims=True), 0.0)

        @pl.when(pl.program_id(0) == 0)
        def _():
            dg_ref[...] = upd

        @pl.when(pl.program_id(0) != 0)
        def _():
            dg_ref[...] += upd

    row_spec = pl.BlockSpec((tm, N), lambda i: (i, 0))
    return pl.pallas_call(
        body, name="in_proj_bwd", grid=(M // tm,),
        in_specs=[pl.BlockSpec((tm, arr.shape[1]), lambda i: (i, 0)) for arr, _ in pieces]
        + [pl.BlockSpec(w.shape, lambda i: (0, 0)), row_spec, pl.BlockSpec((1, N), lambda i: (0, 0)), row_spec],
        out_specs=[row_spec, pl.BlockSpec((8, N), lambda i: (0, 0))],
        out_shape=[jax.ShapeDtypeStruct((M, N), F32), jax.ShapeDtypeStruct((8, N), F32)],
        compiler_params=_params(("arbitrary",)),
    )(*[arr for arr, _ in pieces], w, x, g, dres)


def _log_sigmoid(z):
    return jnp.minimum(z, 0.0) - jnp.log(1.0 + jnp.exp(-jnp.abs(z)))


def _tri(n, lower):
    r = lax.broadcasted_iota(jnp.int32, (n, n), 0)
    c = lax.broadcasted_iota(jnp.int32, (n, n), 1)
    return jnp.where((r >= c) if lower else (r <= c), 1.0, 0.0).astype(F32)


def fox_gate(proj3, b_pad):
    B, S, _ = proj3.shape
    nblk = S // TK

    def body(f_ref, b_ref, o_ref):
        tri = _tri(TK, True)
        carry = jnp.zeros((1, LANES), F32)
        for n in range(nblk):
            z = f_ref[0, n * TK:(n + 1) * TK, :] + b_ref[...]
            logf = _log_sigmoid(z)
            cs = jnp.dot(tri, logf, preferred_element_type=F32, precision=lax.Precision.HIGHEST) + carry
            carry = cs[TK - 1:TK, :]
            o_ref[0, n * TK:(n + 1) * TK, :] = -cs

    return pl.pallas_call(
        body, name="fox_gate", grid=(B,),
        in_specs=[pl.BlockSpec((1, S, LANES), lambda b: (b, 0, P_FLOG // LANES)),
                  pl.BlockSpec((1, LANES), lambda b: (0, 0))],
        out_specs=pl.BlockSpec((1, S, LANES), lambda b: (b, 0, 0)),
        out_shape=jax.ShapeDtypeStruct((B, S, LANES), F32),
        compiler_params=_params(("arbitrary",)),
    )(proj3, b_pad)


def fox_gate_bwd(drow, dneg, proj3, b_pad):
    B, S, _ = proj3.shape
    nblk = S // TK

    def body(d_ref, r_ref, f_ref, b_ref, o_ref, db_ref):
        tri = _tri(TK, False)
        lane = lax.broadcasted_iota(jnp.int32, (TK, LANES), 1)
        carry = jnp.zeros((1, LANES), F32)
        dbsum = jnp.zeros((1, LANES), F32)
        for n in reversed(range(nblk)):
            dk_side = None
            for hp in range(FOX_HEADS // 2):
                two = jnp.where(lane < 2, r_ref[0, n * TK:(n + 1) * TK, hp * LANES:(hp + 1) * LANES], 0.0)
                two = pltpu.roll(two, 2 * hp, 1) if hp else two
                dk_side = two if dk_side is None else dk_side + two
            dc = jnp.where(lane < FOX_HEADS, d_ref[0, :, n * TK:(n + 1) * TK].T - dk_side, 0.0)
            rs = jnp.dot(tri, dc, preferred_element_type=F32, precision=lax.Precision.HIGHEST) + carry
            carry = rs[0:1, :]
            z = f_ref[0, n * TK:(n + 1) * TK, :] + b_ref[...]
            dz = rs * (1.0 / (1.0 + jnp.exp(z)))
            o_ref[0, n * TK:(n + 1) * TK, :] = dz.astype(BF16)
            dbsum = dbsum + jnp.sum(dz, axis=0, keepdims=True)
        row = lax.broadcasted_iota(jnp.int32, (8, LANES), 0)
        upd = jnp.where(row == 0, dbsum, 0.0)

        @pl.when(pl.program_id(0) == 0)
        def _():
            db_ref[...] = upd

        @pl.when(pl.program_id(0) != 0)
        def _():
            db_ref[...] += upd

    return pl.pallas_call(
        body, name="fox_gate_bwd", grid=(B,),
        in_specs=[pl.BlockSpec((1, LANES, S), lambda b: (b, 0, 0)),
                  pl.BlockSpec((1, S, FOX_W), lambda b: (b, 0, 0)),
                  pl.BlockSpec((1, S, LANES), lambda b: (b, 0, P_FLOG // LANES)),
                  pl.BlockSpec((1, LANES), lambda b: (0, 0))],
        out_specs=[pl.BlockSpec((1, S, LANES), lambda b: (b, 0, 0)), pl.BlockSpec((8, LANES), lambda b: (0, 0))],
        out_shape=[jax.ShapeDtypeStruct((B, S, LANES), BF16), jax.ShapeDtypeStruct((8, LANES), F32)],
        compiler_params=_params(("arbitrary",)),
    )(drow, dneg, proj3, b_pad)


def _mult_masks(S, kind):
    nd = S // TQ
    a = np.arange(TQ)[:, None]
    b = np.arange(TK)[None, :]
    out = np.zeros((nd, TQ, TK), np.float32)
    for d in range(nd):
        delta = d * TQ + a - b
        if kind == "causal":
            out[d] = delta >= 0
        else:
            m = np.zeros((TQ, TK), np.float32)
            for w, dil in DILATIONS:
                m += (delta >= 0) & (delta % dil == 0) & (delta <= w)
            out[d] = m
    return jnp.asarray(out)


def _rope_tables(S):
    half = ROPE_DIM // 2
    f32 = np.float32
    pos = np.arange(S, dtype=f32)
    inv_freq = f32(1.0) / np.power(f32(ROPE_THETA), np.arange(0, ROPE_DIM, 2, dtype=f32) / f32(ROPE_DIM)).astype(f32)
    ang = (pos[:, None] * inv_freq[None, :]).astype(f32).astype(np.float64)
    cos, sin = np.cos(ang).astype(f32), np.sin(ang).astype(f32)
    one = np.ones((S, HEAD_DIM - ROPE_DIM), f32)
    zero = np.zeros((S, HEAD_DIM - ROPE_DIM), f32)
    zh = np.zeros((S, half), f32)
    c = np.concatenate([cos, cos, one], axis=1)
    s1 = np.concatenate([-sin, zh, zero], axis=1)
    s2 = np.concatenate([zh, sin, zero], axis=1)
    return tuple(jnp.asarray(np.concatenate([t, t], axis=1)) for t in (c, s1, s2))


def _rope(t, c, s1, s2):
    return t * c + pltpu.roll(t, LANES - half_rope(), 1) * s1 + pltpu.roll(t, half_rope(), 1) * s2


def half_rope():
    return ROPE_DIM // 2


def _rope_bwd(d, c, s1, s2):
    return d * c + pltpu.roll(d * s1, half_rope(), 1) + pltpu.roll(d * s2, LANES - half_rope(), 1)


def _scale_parts(scale):
    m, _ = math.frexp(scale)
    return (scale, None) if m == 0.5 else (None, scale)


def attn_fwd(kind, src, S, *, negc=None, mask=None, rope=None, kv=None):
    B = src.shape[0]
    pair = kind != "mem"
    col0 = {"fox": P_FOX, "dil": P_DIL, "mem": P_MQ}[kind]
    n_blocks = FOX_HEADS // 2 if pair else MEM_HEADS
    e_dim = HEAD_DIM if pair else MEM_HEAD_DIM
    q_fold, s_scale = _scale_parts(1.0 / math.sqrt(e_dim))
    Sk = S if pair else MEM_LEN
    nh = 2 if pair else 1
    has_bias = negc is not None
    has_rope = rope is not None
    nq = S // TQ

    def body(*refs):
        refs = list(refs)
        if pair:
            qkv_ref = refs.pop(0)
        else:
            q_ref, k_ref, v_ref = refs.pop(0), refs.pop(0), refs.pop(0)
        negc_ref = refs.pop(0) if has_bias else None
        mask_ref = refs.pop(0) if pair else None
        rope_refs = [refs.pop(0) for _ in range(3)] if has_rope else None
        o_ref, lse_ref, qs, ks, vs = refs
        lane = lax.broadcasted_iota(jnp.int32, (1, LANES), 1)

        def prep_q(n, _):
            r0 = pl.multiple_of(n * TQ, TQ)
            rows = pl.ds(r0, TQ)
            q = qkv_ref[0, rows, 0:LANES] if pair else q_ref[0, rows, :]
            if has_rope:
                q = _rope(q, *[t[rows, :] for t in rope_refs])
            if q_fold is not None:
                q = q * q_fold
            qs[rows, :] = q.astype(BF16)
            return 0

        def prep_kv(n, _):
            r0 = pl.multiple_of(n * TK, TK)
            rows = pl.ds(r0, TK)
            k = qkv_ref[0, rows, LANES:2 * LANES] if pair else k_ref[0, rows, :]
            v = qkv_ref[0, rows, 2 * LANES:3 * LANES] if pair else v_ref[0, rows, :]
            if has_rope:
                k = _rope(k, *[t[rows, :] for t in rope_refs])
            ks[rows, :] = k.astype(BF16)
            vs[rows, :] = v.astype(BF16)
            return 0

        lax.fori_loop(0, nq, prep_q, 0)
        lax.fori_loop(0, Sk // TK, prep_kv, 0)

        def q_loop(i, _):
            r0 = pl.multiple_of(i * TQ, TQ)
            q = qs[pl.ds(r0, TQ), :]
            res = []
            for hh in range(nh):
                hmask = (lane >= HEAD_DIM * hh) & (lane < HEAD_DIM * (hh + 1))
                qh = jnp.where(hmask, q, jnp.zeros_like(q)) if pair else q

                def kv_loop(j, carry, qh=qh, hh=hh):
                    m, l, acc = carry
                    c0 = pl.multiple_of(j * TK, TK)
                    k = ks[pl.ds(c0, TK), :]
                    v = vs[pl.ds(c0, TK), :]
                    s = lax.dot_general(qh, k, (((1,), (1,)), ((), ())), preferred_element_type=F32)
                    if s_scale is not None:
                        s = s * s_scale
                    if has_bias:
                        s = s + negc_ref[0, 0, pl.ds(hh, 1), pl.ds(c0, TK)]
                    if pair:
                        mult = mask_ref[i - j]
                        s = jnp.where(mult > 0.0, s, NEG_INF)
                    m_new = jnp.maximum(m, jnp.max(s, axis=1, keepdims=True))
                    p = jnp.exp(s - m_new)
                    if pair:
                        p = p * mult
                    alpha = jnp.exp(m - m_new)
                    l = alpha * l + jnp.sum(p, axis=1, keepdims=True)
                    acc = acc * alpha + jnp.dot(p.astype(BF16), v, preferred_element_type=F32)
                    return m_new, l, acc

                init = (jnp.full((TQ, 1), NEG_INF, F32), jnp.zeros((TQ, 1), F32), jnp.zeros((TQ, LANES), F32))
                m, l, acc = lax.fori_loop(0, (i + 1) if pair else Sk // TK, kv_loop, init)
                res.append((acc / l, m + jnp.log(l)))
            if pair:
                o = jnp.where(lane < HEAD_DIM, res[0][0], res[1][0])
                lse = jnp.where(lane < HEAD_DIM, res[0][1], res[1][1])
            else:
                o = res[0][0]
                lse = jnp.broadcast_to(res[0][1], (TQ, LANES))
            o_ref[0, pl.ds(r0, TQ), :] = o
            lse_ref[0, pl.ds(r0, TQ), :] = lse
            return 0

        lax.fori_loop(0, nq, q_loop, 0)

    ins, in_specs = [], []
    if pair:
        ins.append(src)
        in_specs.append(pl.BlockSpec((1, S, PAIR_W), lambda b, h: (b, 0, col0 // PAIR_W + h)))
    else:
        ins += [src, kv, kv]
        in_specs += [pl.BlockSpec((1, S, LANES), lambda b, h: (b, 0, col0 // LANES + h)),
                     pl.BlockSpec((1, MEM_LEN, LANES), lambda b, h: (b, 0, h)),
                     pl.BlockSpec((1, MEM_LEN, LANES), lambda b, h: (b, 0, MEM_HEADS + h))]
    if has_bias:
        ins.append(negc)
        in_specs.append(pl.BlockSpec((1, 1, 2, S), lambda b, h: (b, h, 0, 0)))
    if pair:
        ins.append(mask)
        in_specs.append(pl.BlockSpec(mask.shape, lambda b, h: (0, 0, 0)))
    if has_rope:
        ins += list(rope)
        in_specs += [pl.BlockSpec((S, LANES), lambda b, h: (0, 0))] * 3
    W = n_blocks * LANES
    out_spec = pl.BlockSpec((1, S, LANES), lambda b, h: (b, 0, h))
    return pl.pallas_call(
        body, name=kind + "_attn_fwd", grid=(B, n_blocks),
        in_specs=in_specs, out_specs=[out_spec, out_spec],
        out_shape=[jax.ShapeDtypeStruct((B, S, W), F32)] * 2,
        scratch_shapes=[pltpu.VMEM((S, LANES), BF16), pltpu.VMEM((Sk, LANES), BF16), pltpu.VMEM((Sk, LANES), BF16)],
        compiler_params=_params(("arbitrary", "arbitrary")),
    )(*ins)


def attn_bwd(kind, src, do, o, lse, S, *, negc=None, mask=None, rope=None, kv=None):
    B = src.shape[0]
    pair = kind != "mem"
    col0 = {"fox": P_FOX, "dil": P_DIL, "mem": P_MQ}[kind]
    n_blocks = FOX_HEADS // 2 if pair else MEM_HEADS
    e_dim = HEAD_DIM if pair else MEM_HEAD_DIM
    scale = 1.0 / math.sqrt(e_dim)
    q_fold, s_scale = _scale_parts(scale)
    Sk = S if pair else MEM_LEN
    nh = 2 if pair else 1
    has_bias = negc is not None
    has_rope = rope is not None
    nq = S // TQ
    nk = Sk // TK

    def body(*refs):
        refs = list(refs)
        if pair:
            qkv_ref = refs.pop(0)
        else:
            q_ref, k_ref, v_ref = refs.pop(0), refs.pop(0), refs.pop(0)
        do_ref, o_ref, lse_ref = refs.pop(0), refs.pop(0), refs.pop(0)
        negc_ref = refs.pop(0) if has_bias else None
        mask_ref = refs.pop(0) if pair else None
        rope_refs = [refs.pop(0) for _ in range(3)] if has_rope else None
        if pair:
            dqkv_ref = refs.pop(0)
            dnegc_ref = refs.pop(0) if has_bias else None
            drow_ref = refs.pop(0) if has_bias else None
        else:
            dq_ref, dk_ref, dv_ref = refs.pop(0), refs.pop(0), refs.pop(0)
        qs, ks, vs, dos, delta_s, dq_acc = refs[:6]
        drow_acc = refs[6] if has_bias else None
        lane = lax.broadcasted_iota(jnp.int32, (1, LANES), 1)

        def prep_q(n, _):
            r0 = pl.multiple_of(n * TQ, TQ)
            rows = pl.ds(r0, TQ)
            q = qkv_ref[0, rows, 0:LANES] if pair else q_ref[0, rows, :]
            if has_rope:
                q = _rope(q, *[t[rows, :] for t in rope_refs])
            if q_fold is not None:
                q = q * q_fold
            qs[rows, :] = q.astype(BF16)
            dov = do_ref[0, rows, :]
            dob = dov.astype(BF16)
            dos[rows, :] = dob
            prod = dob.astype(F32) * o_ref[0, rows, :]
            if pair:
                d0 = jnp.sum(jnp.where(lane < HEAD_DIM, prod, 0.0), axis=1, keepdims=True)
                d1 = jnp.sum(jnp.where(lane < HEAD_DIM, 0.0, prod), axis=1, keepdims=True)
                delta_s[rows, :] = jnp.where(lane < HEAD_DIM, d0, d1)
            else:
                delta_s[rows, :] = jnp.broadcast_to(jnp.sum(prod, axis=1, keepdims=True), (TQ, LANES))
            dq_acc[rows, :] = jnp.zeros((TQ, LANES), F32)
            if has_bias:
                drow_acc[rows, :] = jnp.zeros((TQ, LANES), F32)
            return 0

        def prep_kv(n, _):
            r0 = pl.multiple_of(n * TK, TK)
            rows = pl.ds(r0, TK)
            k = qkv_ref[0, rows, LANES:2 * LANES] if pair else k_ref[0, rows, :]
            v = qkv_ref[0, rows, 2 * LANES:3 * LANES] if pair else v_ref[0, rows, :]
            if has_rope:
                k = _rope(k, *[t[rows, :] for t in rope_refs])
            ks[rows, :] = k.astype(BF16)
            vs[rows, :] = v.astype(BF16)
            return 0

        lax.fori_loop(0, nq, prep_q, 0)
        lax.fori_loop(0, nk, prep_kv, 0)

        def kv_loop(j, _):
            c0 = pl.multiple_of(j * TK, TK)
            kt = ks[pl.ds(c0, TK), :]
            vt = vs[pl.ds(c0, TK), :]
            res = []
            for hh in range(nh):
                hmask = (lane >= HEAD_DIM * hh) & (lane < HEAD_DIM * (hh + 1))
                kh = jnp.where(hmask, kt, jnp.zeros_like(kt)) if pair else kt
                vh = jnp.where(hmask, vt, jnp.zeros_like(vt)) if pair else vt

                def q_loop(i, carry, kh=kh, vh=vh, hh=hh, hmask=hmask):
                    dk, dv, dneg = carry
                    r0 = pl.multiple_of(i * TQ, TQ)
                    rows = pl.ds(r0, TQ)
                    q = qs[rows, :]
                    dot = dos[rows, :]
                    lse_i = lse_ref[0, rows, hh * HEAD_DIM:hh * HEAD_DIM + 1]
                    delta_i = delta_s[rows, hh * HEAD_DIM:hh * HEAD_DIM + 1]
                    s = lax.dot_general(q, kh, (((1,), (1,)), ((), ())), preferred_element_type=F32)
                    if s_scale is not None:
                        s = s * s_scale
                    if has_bias:
                        s = s + negc_ref[0, 0, pl.ds(hh, 1), pl.ds(c0, TK)]
                    if pair:
                        mult = mask_ref[i - j]
                        s = jnp.where(mult > 0.0, s, NEG_INF)
                    p = jnp.exp(s - lse_i)
                    if pair:
                        p = p * mult
                    dv = dv + lax.dot_general(p.astype(BF16), dot, (((0,), (0,)), ((), ())),
                                              preferred_element_type=F32)
                    dp = lax.dot_general(dot, vh, (((1,), (1,)), ((), ())), preferred_element_type=F32)
                    ds = p * (dp - delta_i)
                    if has_bias:
                        dneg = dneg + jnp.sum(ds, axis=0, keepdims=True)
                        drow_acc[rows, :] += jnp.where(hmask, jnp.sum(ds, axis=1, keepdims=True), 0.0)
                    if s_scale is not None:
                        ds = ds * s_scale
                    dsb = ds.astype(BF16)
                    dk = dk + lax.dot_general(dsb, q, (((0,), (0,)), ((), ())), preferred_element_type=F32)
                    dq = jnp.dot(dsb, kh, preferred_element_type=F32)
                    dq_acc[rows, :] += dq
                    return dk, dv, dneg

                init = (jnp.zeros((TK, LANES), F32), jnp.zeros((TK, LANES), F32), jnp.zeros((1, TK), F32))
                dk, dv, dneg = lax.fori_loop(j if pair else 0, nq, q_loop, init)
                if has_bias:
                    dnegc_ref[0, 0, pl.ds(hh, 1), pl.ds(c0, TK)] = dneg
                res.append((dk, dv))
            if pair:
                dk = jnp.where(lane < HEAD_DIM, res[0][0], res[1][0])
                dv = jnp.where(lane < HEAD_DIM, res[0][1], res[1][1])
                if has_rope:
                    dk = _rope_bwd(dk, *[t[pl.ds(c0, TK), :] for t in rope_refs])
                dqkv_ref[0, pl.ds(c0, TK), LANES:2 * LANES] = dk.astype(BF16)
                dqkv_ref[0, pl.ds(c0, TK), 2 * LANES:3 * LANES] = dv.astype(BF16)
            else:
                dk_ref[0, pl.ds(c0, TK), :] = res[0][0].astype(BF16)
                dv_ref[0, pl.ds(c0, TK), :] = res[0][1].astype(BF16)
            return 0

        lax.fori_loop(0, nk, kv_loop, 0)

        def fin_q(n, _):
            r0 = pl.multiple_of(n * TQ, TQ)
            rows = pl.ds(r0, TQ)
            dq = dq_acc[rows, :]
            if q_fold is not None:
                dq = dq * q_fold
            if has_rope:
                dq = _rope_bwd(dq, *[t[rows, :] for t in rope_refs])
            if pair:
                dqkv_ref[0, rows, 0:LANES] = dq.astype(BF16)
            else:
                dq_ref[0, rows, :] = dq.astype(BF16)
            if has_bias:
                drow_ref[0, rows, :] = drow_acc[rows, :]
            return 0

        lax.fori_loop(0, nq, fin_q, 0)

    ins, in_specs = [], []
    if pair:
        ins.append(src)
        in_specs.append(pl.BlockSpec((1, S, PAIR_W), lambda b, h: (b, 0, col0 // PAIR_W + h)))
    else:
        ins += [src, kv, kv]
        in_specs += [pl.BlockSpec((1, S, LANES), lambda b, h: (b, 0, col0 // LANES + h)),
                     pl.BlockSpec((1, MEM_LEN, LANES), lambda b, h: (b, 0, h)),
                     pl.BlockSpec((1, MEM_LEN, LANES), lambda b, h: (b, 0, MEM_HEADS + h))]
    row_spec = pl.BlockSpec((1, S, LANES), lambda b, h: (b, 0, h))
    ins += [do, o, lse]
    in_specs += [row_spec] * 3
    if has_bias:
        ins.append(negc)
        in_specs.append(pl.BlockSpec((1, 1, 2, S), lambda b, h: (b, h, 0, 0)))
    if pair:
        ins.append(mask)
        in_specs.append(pl.BlockSpec(mask.shape, lambda b, h: (0, 0, 0)))
    if has_rope:
        ins += list(rope)
        in_specs += [pl.BlockSpec((S, LANES), lambda b, h: (0, 0))] * 3
    W = n_blocks * LANES
    if pair:
        out_specs = [pl.BlockSpec((1, S, PAIR_W), lambda b, h: (b, 0, h))]
        out_shape = [jax.ShapeDtypeStruct((B, S, 3 * W), BF16)]
        if has_bias:
            out_specs.append(pl.BlockSpec((1, 1, 2, S), lambda b, h: (b, h, 0, 0)))
            out_shape.append(jax.ShapeDtypeStruct((B, LANES // 2, 2, S), F32))
            out_specs.append(row_spec)
            out_shape.append(jax.ShapeDtypeStruct((B, S, W), F32))
    else:
        kv_spec = pl.BlockSpec((1, MEM_LEN, LANES), lambda b, h: (b, 0, h))
        out_specs = [row_spec, kv_spec, kv_spec]
        out_shape = [jax.ShapeDtypeStruct((B, S, W), BF16)] + [jax.ShapeDtypeStruct((B, MEM_LEN, W), BF16)] * 2
    return pl.pallas_call(
        body, name=kind + "_attn_bwd", grid=(B, n_blocks),
        in_specs=in_specs, out_specs=out_specs, out_shape=out_shape,
        scratch_shapes=[pltpu.VMEM((S, LANES), BF16), pltpu.VMEM((Sk, LANES), BF16), pltpu.VMEM((Sk, LANES), BF16),
                        pltpu.VMEM((S, LANES), BF16), pltpu.VMEM((S, LANES), F32), pltpu.VMEM((S, LANES), F32)]
        + ([pltpu.VMEM((S, LANES), F32)] if has_bias else []),
        compiler_params=_params(("arbitrary", "arbitrary")),
    )(*ins)


def _log_masks(S, kind):
    nd = 1 if kind == "causal" else S // TQ
    a = np.arange(TQ)[:, None]
    b = np.arange(TK)[None, :]
    out = np.zeros((nd, TQ, TK), np.float32)
    for d in range(nd):
        delta = d * TQ + a - b
        if kind == "causal":
            m = (delta >= 0).astype(np.float64)
        else:
            m = sum(((delta >= 0) & (delta % dil == 0) & (delta <= w)).astype(np.float64) for w, dil in DILATIONS)
        out[d] = np.where(m > 0, np.log(np.maximum(m, 1.0)), NEG_INF)
    return jnp.asarray(out)


def _attn_setup(kind):
    pair = kind != "mem"
    e_dim = HEAD_DIM if pair else MEM_HEAD_DIM
    q_fold, s_scale = _scale_parts(1.0 / math.sqrt(e_dim))
    return dict(pair=pair, col0={"fox": P_FOX, "dil": P_DIL, "mem": P_MQ}[kind],
                n_blocks=FOX_HEADS // 2 if pair else MEM_HEADS, q_fold=q_fold, s_scale=s_scale,
                nh=2 if pair else 1)


def _attn_inputs(kind, src, S, negc, mask, rope, kv, extra):
    cfg = _attn_setup(kind)
    col0 = cfg["col0"]
    ins, in_specs = [], []
    if cfg["pair"]:
        ins.append(src)
        in_specs.append(pl.BlockSpec((1, S, PAIR_W), lambda b, h: (b, 0, col0 // PAIR_W + h)))
    else:
        ins += [src, kv, kv]
        in_specs += [pl.BlockSpec((1, S, LANES), lambda b, h: (b, 0, col0 // LANES + h)),
                     pl.BlockSpec((1, MEM_LEN, LANES), lambda b, h: (b, 0, h)),
                     pl.BlockSpec((1, MEM_LEN, LANES), lambda b, h: (b, 0, MEM_HEADS + h))]
    ins += list(extra)
    in_specs += [pl.BlockSpec((1, S, LANES), lambda b, h: (b, 0, h))] * len(extra)
    if negc is not None:
        ins.append(negc)
        in_specs.append(pl.BlockSpec((1, 1, 2, S), lambda b, h: (b, h, 0, 0)))
    if mask is not None:
        ins.append(mask)
        in_specs.append(pl.BlockSpec(mask.shape, lambda b, h: (0, 0, 0)))
    if rope is not None:
        ins += list(rope)
        in_specs += [pl.BlockSpec((S, LANES), lambda b, h: (0, 0))] * 3
    return ins, in_specs


def _prep_rows(cfg, rope_refs, lane, load_q, load_kv, qs2, ks, vs, S, Sk):
    nh = cfg["nh"]
    R = nh * TQ

    def prep_q(n, _):
        rows = pl.ds(pl.multiple_of(n * TQ, TQ), TQ)
        q = load_q(rows)
        if rope_refs is not None:
            q = _rope(q, *[t[rows, :] for t in rope_refs])
        if cfg["q_fold"] is not None:
            q = q * cfg["q_fold"]
        _store_stacked(cfg, lane, qs2, n, q.astype(BF16))
        return 0

    def prep_kv(n, _):
        rows = pl.ds(pl.multiple_of(n * TK, TK), TK)
        k, v = load_kv(rows)
        if rope_refs is not None:
            k = _rope(k, *[t[rows, :] for t in rope_refs])
        ks[rows, :] = k.astype(BF16)
        vs[rows, :] = v.astype(BF16)
        return 0

    lax.fori_loop(0, S // TQ, prep_q, 0)
    lax.fori_loop(0, Sk // TK, prep_kv, 0)


def _store_stacked(cfg, lane, dst, n, val):
    nh = cfg["nh"]
    R = nh * TQ
    if nh == 1:
        dst[pl.ds(pl.multiple_of(n * R, R), TQ), :] = val
        return
    for hh in range(nh):
        hmask = (lane >= HEAD_DIM * hh) & (lane < HEAD_DIM * (hh + 1))
        dst[pl.ds(pl.multiple_of(n * R + hh * TQ, TQ), TQ), :] = jnp.where(hmask, val, jnp.zeros_like(val))


def _cat(parts, axis):
    return parts[0] if len(parts) == 1 else jnp.concatenate(parts, axis=axis)


def attn_fwd2(kind, src, S, *, negc=None, mask=None, rope=None, kv=None):
    B = src.shape[0]
    cfg = _attn_setup(kind)
    pair, nh, s_scale = cfg["pair"], cfg["nh"], cfg["s_scale"]
    Sk = S if pair else MEM_LEN
    has_bias, has_rope = negc is not None, rope is not None
    R = nh * TQ

    def body(*refs):
        refs = list(refs)
        if pair:
            qkv_ref = refs.pop(0)
        else:
            q_ref, k_ref, v_ref = refs.pop(0), refs.pop(0), refs.pop(0)
        negc_ref = refs.pop(0) if has_bias else None
        mask_ref = refs.pop(0) if mask is not None else None
        rope_refs = [refs.pop(0) for _ in range(3)] if has_rope else None
        o_ref, lse_ref, qs2, ks, vs = refs
        lane = lax.broadcasted_iota(jnp.int32, (1, LANES), 1)

        if pair:
            load_q = lambda rows: qkv_ref[0, rows, 0:LANES]
            load_kv = lambda rows: (qkv_ref[0, rows, LANES:2 * LANES], qkv_ref[0, rows, 2 * LANES:3 * LANES])
        else:
            load_q = lambda rows: q_ref[0, rows, :]
            load_kv = lambda rows: (k_ref[0, rows, :], v_ref[0, rows, :])
        _prep_rows(cfg, rope_refs, lane, load_q, load_kv, qs2, ks, vs, S, Sk)

        def q_loop(i, _):
            q2 = qs2[pl.ds(pl.multiple_of(i * R, R), R), :]

            def step(j, carry, midx):
                ms, ls, acc = carry
                c0 = pl.multiple_of(j * TK, TK)
                k = ks[pl.ds(c0, TK), :]
                v = vs[pl.ds(c0, TK), :]
                s2 = lax.dot_general(q2, k, (((1,), (1,)), ((), ())), preferred_element_type=F32)
                if s_scale is not None:
                    s2 = s2 * s_scale
                new_m, new_l, ps, alphas = [], [], [], []
                for hh in range(nh):
                    s = s2[hh * TQ:(hh + 1) * TQ]
                    if has_bias:
                        s = s + negc_ref[0, 0, pl.ds(hh, 1), pl.ds(c0, TK)]
                    if midx is not None:
                        s = s + mask_ref[midx]
                    m_new = jnp.maximum(ms[hh], jnp.max(s, axis=1, keepdims=True))
                    p = jnp.exp(s - m_new)
                    alpha = jnp.exp(ms[hh] - m_new)
                    new_l.append(alpha * ls[hh] + jnp.sum(p, axis=1, keepdims=True))
                    new_m.append(m_new)
                    ps.append(p.astype(BF16))
                    alphas.append(alpha)
                acc = acc * _cat(alphas, 0) + jnp.dot(_cat(ps, 0), v, preferred_element_type=F32)
                return tuple(new_m), tuple(new_l), acc

            init = (tuple(jnp.full((TQ, 1), NEG_INF, F32) for _ in range(nh)),
                    tuple(jnp.zeros((TQ, 1), F32) for _ in range(nh)), jnp.zeros((R, LANES), F32))
            if kind == "fox":
                carry = lax.fori_loop(0, i, lambda j, c: step(j, c, None), init)
                carry = step(i, carry, 0)
            elif kind == "dil":
                carry = lax.fori_loop(0, i + 1, lambda j, c: step(j, c, i - j), init)
            else:
                carry = lax.fori_loop(0, Sk // TK, lambda j, c: step(j, c, None), init)
            ms, ls, acc = carry
            outs = [acc[hh * TQ:(hh + 1) * TQ] / ls[hh] for hh in range(nh)]
            lses = [ms[hh] + jnp.log(ls[hh]) for hh in range(nh)]
            rows = pl.ds(pl.multiple_of(i * TQ, TQ), TQ)
            if pair:
                o_ref[0, rows, :] = jnp.where(lane < HEAD_DIM, outs[0], outs[1])
                lse_ref[0, rows, :] = jnp.where(lane < HEAD_DIM, lses[0], lses[1])
            else:
                o_ref[0, rows, :] = outs[0]
                lse_ref[0, rows, :] = jnp.broadcast_to(lses[0], (TQ, LANES))
            return 0

        lax.fori_loop(0, S // TQ, q_loop, 0)

    ins, in_specs = _attn_inputs(kind, src, S, negc, mask, rope, kv, ())
    W = cfg["n_blocks"] * LANES
    out_spec = pl.BlockSpec((1, S, LANES), lambda b, h: (b, 0, h))
    return pl.pallas_call(
        body, name=kind + "_attn_fwd", grid=(B, cfg["n_blocks"]),
        in_specs=in_specs, out_specs=[out_spec, out_spec],
        out_shape=[jax.ShapeDtypeStruct((B, S, W), F32)] * 2,
        scratch_shapes=[pltpu.VMEM((nh * S, LANES), BF16), pltpu.VMEM((Sk, LANES), BF16),
                        pltpu.VMEM((Sk, LANES), BF16)],
        compiler_params=_params(("arbitrary", "arbitrary")),
    )(*ins)


def attn_bwd2(kind, src, do, o, lse, S, *, negc=None, mask=None, rope=None, kv=None):
    B = src.shape[0]
    cfg = _attn_setup(kind)
    pair, nh, s_scale, q_fold = cfg["pair"], cfg["nh"], cfg["s_scale"], cfg["q_fold"]
    Sk = S if pair else MEM_LEN
    has_bias, has_rope = negc is not None, rope is not None
    R = nh * TQ
    nq, nk = S // TQ, Sk // TK

    def body(*refs):
        refs = list(refs)
        if pair:
            qkv_ref = refs.pop(0)
        else:
            q_ref, k_ref, v_ref = refs.pop(0), refs.pop(0), refs.pop(0)
        do_ref, o_ref, lse_ref = refs.pop(0), refs.pop(0), refs.pop(0)
        negc_ref = refs.pop(0) if has_bias else None
        mask_ref = refs.pop(0) if mask is not None else None
        rope_refs = [refs.pop(0) for _ in range(3)] if has_rope else None
        if pair:
            dqkv_ref = refs.pop(0)
            dnegc_ref = refs.pop(0) if has_bias else None
            drow_ref = refs.pop(0) if has_bias else None
        else:
            dq_ref, dk_ref, dv_ref = refs.pop(0), refs.pop(0), refs.pop(0)
        qs2, ks, vs, dos2, lse_s, delta_s, dk_acc, dv_acc = refs[:8]
        dneg_acc = refs[8] if has_bias else None
        lane = lax.broadcasted_iota(jnp.int32, (1, LANES), 1)

        if pair:
            load_q = lambda rows: qkv_ref[0, rows, 0:LANES]
            load_kv = lambda rows: (qkv_ref[0, rows, LANES:2 * LANES], qkv_ref[0, rows, 2 * LANES:3 * LANES])
        else:
            load_q = lambda rows: q_ref[0, rows, :]
            load_kv = lambda rows: (k_ref[0, rows, :], v_ref[0, rows, :])
        _prep_rows(cfg, rope_refs, lane, load_q, load_kv, qs2, ks, vs, S, Sk)

        def prep_do(n, _):
            rows = pl.ds(pl.multiple_of(n * TQ, TQ), TQ)
            dob = do_ref[0, rows, :].astype(BF16)
            _store_stacked(cfg, lane, dos2, n, dob)
            prod = dob.astype(F32) * o_ref[0, rows, :]
            lse_blk = lse_ref[0, rows, :]
            for hh in range(nh):
                dst = pl.ds(pl.multiple_of(n * R + hh * TQ, TQ), TQ)
                if pair:
                    hmask = (lane >= HEAD_DIM * hh) & (lane < HEAD_DIM * (hh + 1))
                    d = jnp.sum(jnp.where(hmask, prod, 0.0), axis=1, keepdims=True)
                    lse_s[dst, :] = jnp.broadcast_to(lse_blk[:, hh * HEAD_DIM:hh * HEAD_DIM + 1], (TQ, LANES))
                else:
                    d = jnp.sum(prod, axis=1, keepdims=True)
                    lse_s[dst, :] = lse_blk
                delta_s[dst, :] = jnp.broadcast_to(d, (TQ, LANES))
            return 0

        def zero_kv(n, _):
            rows = pl.ds(pl.multiple_of(n * TK, TK), TK)
            dk_acc[rows, :] = jnp.zeros((TK, LANES), F32)
            dv_acc[rows, :] = jnp.zeros((TK, LANES), F32)
            return 0

        lax.fori_loop(0, nq, prep_do, 0)
        lax.fori_loop(0, nk, zero_kv, 0)
        if has_bias:
            dneg_acc[...] = jnp.zeros(dneg_acc.shape, F32)

        def q_loop(i, _):
            rows2 = pl.ds(pl.multiple_of(i * R, R), R)
            q2 = qs2[rows2, :]
            do2 = dos2[rows2, :]
            lse2 = lse_s[rows2, :]
            delta2 = delta_s[rows2, :]
            wide = lambda t: jnp.concatenate([t] * (TK // LANES), axis=1)

            def step(j, carry, midx):
                dq2, drow = carry
                c0 = pl.multiple_of(j * TK, TK)
                kcols = pl.ds(c0, TK)
                k = ks[kcols, :]
                v = vs[kcols, :]
                s2 = lax.dot_general(q2, k, (((1,), (1,)), ((), ())), preferred_element_type=F32)
                if s_scale is not None:
                    s2 = s2 * s_scale
                if has_bias or midx is not None:
                    halves = []
                    for hh in range(nh):
                        s = s2[hh * TQ:(hh + 1) * TQ]
                        if has_bias:
                            s = s + negc_ref[0, 0, pl.ds(hh, 1), kcols]
                        if midx is not None:
                            s = s + mask_ref[midx]
                        halves.append(s)
                    s2 = _cat(halves, 0)
                p2 = jnp.exp(s2 - wide(lse2))
                dp2 = lax.dot_general(do2, v, (((1,), (1,)), ((), ())), preferred_element_type=F32)
                ds2 = p2 * (dp2 - wide(delta2))
                if has_bias:
                    drow = drow + jnp.sum(ds2, axis=1, keepdims=True)
                    for hh in range(nh):
                        dneg_acc[pl.ds(hh, 1), kcols] += jnp.sum(ds2[hh * TQ:(hh + 1) * TQ], axis=0, keepdims=True)
                if s_scale is not None:
                    ds2 = ds2 * s_scale
                dsb = ds2.astype(BF16)
                dv_acc[kcols, :] += lax.dot_general(p2.astype(BF16), do2, (((0,), (0,)), ((), ())),
                                                    preferred_element_type=F32)
                dk_acc[kcols, :] += lax.dot_general(dsb, q2, (((0,), (0,)), ((), ())), preferred_element_type=F32)
                dq2 = dq2 + jnp.dot(dsb, k, preferred_element_type=F32)
                return dq2, drow

            init = (jnp.zeros((R, LANES), F32), jnp.zeros((R, 1), F32))
            if kind == "fox":
                carry = lax.fori_loop(0, i, lambda j, c: step(j, c, None), init)
                carry = step(i, carry, 0)
            elif kind == "dil":
                carry = lax.fori_loop(0, i + 1, lambda j, c: step(j, c, i - j), init)
            else:
                carry = lax.fori_loop(0, nk, lambda j, c: step(j, c, None), init)
            dq2, drow = carry
            rows = pl.ds(pl.multiple_of(i * TQ, TQ), TQ)
            dq = jnp.where(lane < HEAD_DIM, dq2[0:TQ], dq2[TQ:2 * TQ]) if pair else dq2
            if q_fold is not None:
                dq = dq * q_fold
            if has_rope:
                dq = _rope_bwd(dq, *[t[rows, :] for t in rope_refs])
            if pair:
                dqkv_ref[0, rows, 0:LANES] = dq.astype(BF16)
            else:
                dq_ref[0, rows, :] = dq.astype(BF16)
            if has_bias:
                drow_ref[0, rows, :] = jnp.where(lane < HEAD_DIM, drow[0:TQ], drow[TQ:2 * TQ])
            return 0

        lax.fori_loop(0, nq, q_loop, 0)

        def fin_kv(n, _):
            rows = pl.ds(pl.multiple_of(n * TK, TK), TK)
            dk = dk_acc[rows, :]
            if has_rope:
                dk = _rope_bwd(dk, *[t[rows, :] for t in rope_refs])
            if pair:
                dqkv_ref[0, rows, LANES:2 * LANES] = dk.astype(BF16)
                dqkv_ref[0, rows, 2 * LANES:3 * LANES] = dv_acc[rows, :].astype(BF16)
            else:
                dk_ref[0, rows, :] = dk.astype(BF16)
                dv_ref[0, rows, :] = dv_acc[rows, :].astype(BF16)
            return 0

        lax.fori_loop(0, nk, fin_kv, 0)
        if has_bias:
            dnegc_ref[0, 0] = dneg_acc[...]

    ins, in_specs = _attn_inputs(kind, src, S, negc, mask, rope, kv, (do, o, lse))
    W = cfg["n_blocks"] * LANES
    row_spec = pl.BlockSpec((1, S, LANES), lambda b, h: (b, 0, h))
    if pair:
        out_specs = [pl.BlockSpec((1, S, PAIR_W), lambda b, h: (b, 0, h))]
        out_shape = [jax.ShapeDtypeStruct((B, S, 3 * W), BF16)]
        if has_bias:
            out_specs += [pl.BlockSpec((1, 1, 2, S), lambda b, h: (b, h, 0, 0)), row_spec]
            out_shape += [jax.ShapeDtypeStruct((B, LANES // 2, 2, S), F32), jax.ShapeDtypeStruct((B, S, W), F32)]
    else:
        kv_spec = pl.BlockSpec((1, MEM_LEN, LANES), lambda b, h: (b, 0, h))
        out_specs = [row_spec, kv_spec, kv_spec]
        out_shape = [jax.ShapeDtypeStruct((B, S, W), BF16)] + [jax.ShapeDtypeStruct((B, MEM_LEN, W), BF16)] * 2
    scratch = [pltpu.VMEM((nh * S, LANES), BF16), pltpu.VMEM((Sk, LANES), BF16), pltpu.VMEM((Sk, LANES), BF16),
               pltpu.VMEM((nh * S, LANES), BF16), pltpu.VMEM((nh * S, LANES), F32), pltpu.VMEM((nh * S, LANES), F32),
               pltpu.VMEM((Sk, LANES), F32), pltpu.VMEM((Sk, LANES), F32)]
    if has_bias:
        scratch.append(pltpu.VMEM((2, S), F32))
    return pl.pallas_call(
        body, name=kind + "_attn_bwd", grid=(B, cfg["n_blocks"]),
        in_specs=in_specs, out_specs=out_specs, out_shape=out_shape, scratch_shapes=scratch,
        compiler_params=_params(("arbitrary", "arbitrary")),
    )(*ins)


def _log_masks_t(S, kind):
    return jnp.swapaxes(_log_masks(S, kind), 1, 2)


def _head_rows(hh, pair):
    row = lax.broadcasted_iota(jnp.int32, (LANES, 1), 0)
    if not pair:
        return row >= 0
    return (row >= HEAD_DIM * hh) & (row < HEAD_DIM * (hh + 1))


def _attn_t_inputs(kind, src, S, negc_cols, mask, rope, kv):
    cfg = _attn_setup(kind)
    col0 = cfg["col0"]
    ins, in_specs = [], []
    if cfg["pair"]:
        ins.append(src)
        in_specs.append(pl.BlockSpec((1, S, PAIR_W), lambda b, h: (b, 0, col0 // PAIR_W + h)))
    else:
        ins += [src, kv, kv]
        in_specs += [pl.BlockSpec((1, S, LANES), lambda b, h: (b, 0, col0 // LANES + h)),
                     pl.BlockSpec((1, MEM_LEN, LANES), lambda b, h: (b, 0, h)),
                     pl.BlockSpec((1, MEM_LEN, LANES), lambda b, h: (b, 0, MEM_HEADS + h))]
    if negc_cols is not None:
        ins.append(negc_cols)
        in_specs.append(pl.BlockSpec((1, S, LANES), lambda b, h: (b, 0, 0)))
    if mask is not None:
        ins.append(mask)
        in_specs.append(pl.BlockSpec(mask.shape, lambda b, h: (0, 0, 0)))
    if rope is not None:
        ins += list(rope)
        in_specs += [pl.BlockSpec((S, LANES), lambda b, h: (0, 0))] * 3
    return ins, in_specs


def _attn_t_prep(cfg, refs, S, Sk, *, qT2s, ks, q2s=None, vs=None, vTs=None, kTs=None, nb=None):
    pair, nh = cfg["pair"], cfg["nh"]
    lane = lax.broadcasted_iota(jnp.int32, (1, LANES), 1)
    rope_refs = refs["rope"]

    def prep_q(n, _):
        rows = pl.ds(pl.multiple_of(n * TQ, TQ), TQ)
        q = refs["load_q"](rows)
        if rope_refs is not None:
            q = _rope(q, *[t[rows, :] for t in rope_refs])
        if cfg["q_fold"] is not None:
            q = q * cfg["q_fold"]
        qb = q.astype(BF16)
        if q2s is not None:
            _store_stacked(cfg, lane, q2s, n, qb)
        qtb = qb.T
        for hh in range(nh):
            qT2s[n, :, hh * TQ:(hh + 1) * TQ] = jnp.where(_head_rows(hh, pair), qtb, jnp.zeros_like(qtb))
        return 0

    def prep_kv(n, _):
        rows = pl.ds(pl.multiple_of(n * TK, TK), TK)
        k, v = refs["load_kv"](rows)
        if rope_refs is not None:
            k = _rope(k, *[t[rows, :] for t in rope_refs])
        kb = k.astype(BF16)
        vb = v.astype(BF16)
        ks[rows, :] = kb
        if vs is not None:
            vs[rows, :] = vb
        if vTs is not None:
            vTs[n] = vb.T
        if kTs is not None:
            kTs[n] = kb.T
        if nb is not None:
            blk = refs["negc"][0, rows, :]
            for hh in range(nh):
                h = 2 * refs["block"] + hh
                col = jnp.sum(jnp.where(lane == h, blk, 0.0), axis=1, keepdims=True)
                nb[hh, rows, :] = jnp.broadcast_to(col, (TK, LANES))
        return 0

    lax.fori_loop(0, S // TQ, prep_q, 0)
    lax.fori_loop(0, Sk // TK, prep_kv, 0)


def _raw_scores_t(cfg, k, qT2):
    sT = jnp.dot(k, qT2, preferred_element_type=F32)
    if cfg["s_scale"] is not None:
        sT = sT * cfg["s_scale"]
    return sT


def _bias_mask_t(cfg, sT, nb, mask_ref, kc, midx):
    nh = cfg["nh"]
    if nb is None and midx is None:
        return sT
    parts = []
    for hh in range(nh):
        t = sT[:, hh * TQ:(hh + 1) * TQ]
        if nb is not None:
            t = t + jnp.concatenate([nb[hh, kc, :]] * (TQ // LANES), axis=1)
        if midx is not None:
            t = t + mask_ref[midx]
        parts.append(t)
    return _cat(parts, 1)


def _kv_plan(kind, i, nk):
    if kind == "fox":
        return i, (lambda j: None), 0
    if kind == "dil":
        return i, (lambda j: i - j), 0
    return nk - 1, (lambda j: None), None


def attn_fwd3(kind, src, S, *, negc_cols=None, mask=None, rope=None, kv=None):
    B = src.shape[0]
    cfg = _attn_setup(kind)
    pair, nh = cfg["pair"], cfg["nh"]
    Sk = S if pair else MEM_LEN
    has_bias, has_rope = negc_cols is not None, rope is not None
    R = nh * TQ
    nq, nk = S // TQ, Sk // TK

    def body(*refs):
        refs = list(refs)
        if pair:
            qkv_ref = refs.pop(0)
            load_q = lambda rows: qkv_ref[0, rows, 0:LANES]
            load_kv = lambda rows: (qkv_ref[0, rows, LANES:2 * LANES], qkv_ref[0, rows, 2 * LANES:3 * LANES])
        else:
            q_ref, k_ref, v_ref = refs.pop(0), refs.pop(0), refs.pop(0)
            load_q = lambda rows: q_ref[0, rows, :]
            load_kv = lambda rows: (k_ref[0, rows, :], v_ref[0, rows, :])
        negc_ref = refs.pop(0) if has_bias else None
        mask_ref = refs.pop(0) if mask is not None else None
        rope_refs = [refs.pop(0) for _ in range(3)] if has_rope else None
        o_ref, lse_ref, qT2s, ks, vTs = refs[:5]
        nb = refs[5] if has_bias else None
        _attn_t_prep(cfg, dict(load_q=load_q, load_kv=load_kv, rope=rope_refs, negc=negc_ref,
                               block=pl.program_id(1)), S, Sk,
                     qT2s=qT2s, ks=ks, vTs=vTs, nb=nb)

        def q_loop(i, _):
            qT2 = qT2s[i]

            last, mask_of, mask_last = _kv_plan(kind, i, nk)

            def cols(j):
                return pl.ds(pl.multiple_of(j * TK, TK), TK)

            def scores(j):
                return _raw_scores_t(cfg, ks[cols(j), :], qT2)

            def soft(s_raw, j, midx, m, l):
                sT = _bias_mask_t(cfg, s_raw, nb, mask_ref, cols(j), midx)
                m_new = jnp.maximum(m, jnp.max(sT, axis=0, keepdims=True))
                p = jnp.exp(sT - m_new)
                alpha = jnp.exp(m - m_new)
                return m_new, alpha * l + jnp.sum(p, axis=0, keepdims=True), alpha, p.astype(BF16)

            def pv(j, p):
                return jnp.dot(vTs[j], p, preferred_element_type=F32)

            def body(j, carry):
                s_cur, p_prev, m, l, accT = carry
                pv_prev = pv(jnp.maximum(j - 1, 0), p_prev)
                s_next = scores(j + 1)
                m, l, alpha, p = soft(s_cur, j, mask_of(j), m, l)
                return s_next, p, m, l, (accT + pv_prev) * alpha

            init = (scores(0), jnp.zeros((TK, R), BF16), jnp.full((1, R), NEG_INF, F32), jnp.zeros((1, R), F32),
                    jnp.zeros((LANES, R), F32))
            s_cur, p_prev, m, l, accT = lax.fori_loop(0, last, body, init)
            pv_prev = pv(jnp.maximum(last - 1, 0), p_prev)
            m, l, alpha, p = soft(s_cur, last, mask_last, m, l)
            accT = (accT + pv_prev) * alpha + pv(last, p)
            oT2 = accT / l
            oT = jnp.where(_head_rows(0, True), oT2[:, 0:TQ], oT2[:, TQ:2 * TQ]) if pair else oT2
            o_ref[0, pl.ds(pl.multiple_of(i * TQ, TQ), TQ), :] = oT.T
            lse_ref[0, 0, pl.ds(i, 1), :] = m + jnp.log(l)
            return 0

        lax.fori_loop(0, nq, q_loop, 0)

    ins, in_specs = _attn_t_inputs(kind, src, S, negc_cols, mask, rope, kv)
    W = cfg["n_blocks"] * LANES
    scratch = [pltpu.VMEM((nq, LANES, R), BF16), pltpu.VMEM((Sk, LANES), BF16), pltpu.VMEM((nk, LANES, TK), BF16)]
    if has_bias:
        scratch.append(pltpu.VMEM((nh, Sk, LANES), F32))
    return pl.pallas_call(
        body, name=kind + "_attn_fwd", grid=(B, cfg["n_blocks"]),
        in_specs=in_specs,
        out_specs=[pl.BlockSpec((1, S, LANES), lambda b, h: (b, 0, h)),
                   pl.BlockSpec((1, 1, nq, R), lambda b, h: (b, h, 0, 0))],
        out_shape=[jax.ShapeDtypeStruct((B, S, W), F32), jax.ShapeDtypeStruct((B, cfg["n_blocks"], nq, R), F32)],
        scratch_shapes=scratch,
        compiler_params=_params(("arbitrary", "arbitrary")),
    )(*ins)


def _tile_walk(kind, nq, nk):
    if kind == "mem":
        return nq * nk, (lambda i, j: (jnp.where(j < nk - 1, i, i + 1), jnp.where(j < nk - 1, j + 1, 0))), None
    nxt = lambda i, j: (jnp.where(j < i, i, i + 1), jnp.where(j < i, j + 1, 0))
    if kind == "fox":
        return nq * (nq + 1) // 2, nxt, (lambda i, j: jnp.where(j == i, 0, 1))
    return nq * (nq + 1) // 2, nxt, (lambda i, j: i - j)


def attn_fwd4(kind, src, S, *, negc_cols=None, mask=None, rope=None, kv=None):
    B = src.shape[0]
    cfg = _attn_setup(kind)
    pair, nh = cfg["pair"], cfg["nh"]
    Sk = S if pair else MEM_LEN
    has_bias, has_rope = negc_cols is not None, rope is not None
    R = nh * TQ
    nq, nk = S // TQ, Sk // TK
    n_pairs, successor, mask_index = _tile_walk(kind, nq, nk)
    assert n_pairs % 2 == 0

    def body(*refs):
        refs = list(refs)
        if pair:
            qkv_ref = refs.pop(0)
            load_q = lambda rows: qkv_ref[0, rows, 0:LANES]
            load_kv = lambda rows: (qkv_ref[0, rows, LANES:2 * LANES], qkv_ref[0, rows, 2 * LANES:3 * LANES])
        else:
            q_ref, k_ref, v_ref = refs.pop(0), refs.pop(0), refs.pop(0)
            load_q = lambda rows: q_ref[0, rows, :]
            load_kv = lambda rows: (k_ref[0, rows, :], v_ref[0, rows, :])
        negc_ref = refs.pop(0) if has_bias else None
        mask_ref = refs.pop(0) if mask is not None else None
        rope_refs = [refs.pop(0) for _ in range(3)] if has_rope else None
        o_ref, lse_ref, qT2s, ks, vTs, s_a, s_b, p_a, p_b, acc_all, m_all, l_all = refs[:12]
        nb = refs[12] if has_bias else None
        _attn_t_prep(cfg, dict(load_q=load_q, load_kv=load_kv, rope=rope_refs, negc=negc_ref,
                               block=pl.program_id(1)), S, Sk, qT2s=qT2s, ks=ks, vTs=vTs, nb=nb)

        def cols(j):
            return pl.ds(pl.multiple_of(j * TK, TK), TK)

        def park(i, m, l, accT):
            acc_all[i] = accT
            m_all[pl.ds(i, 1), :] = m
            l_all[pl.ds(i, 1), :] = l

        def finish(i, _):
            l = l_all[pl.ds(i, 1), :]
            oT2 = acc_all[i] / l
            oT = jnp.where(_head_rows(0, True), oT2[:, 0:TQ], oT2[:, TQ:2 * TQ]) if pair else oT2
            o_ref[0, pl.ds(pl.multiple_of(i * TQ, TQ), TQ), :] = oT.T
            lse_ref[0, 0, pl.ds(i, 1), :] = m_all[pl.ds(i, 1), :] + jnp.log(l)
            return 0

        def half(i, j, i_prev, j_prev, s_cur, s_next, p_cur, p_prev, m, l, accT):
            i_n, j_n = successor(i, j)
            acc_full = accT + jnp.dot(vTs[j_prev], p_prev[...], preferred_element_type=F32)
            s_next[...] = _raw_scores_t(cfg, ks[cols(j_n), :], qT2s[jnp.minimum(i_n, nq - 1)])
            park(i_prev, m, l, acc_full)
            first = j == 0
            m = jnp.where(first, NEG_INF, m)
            l = jnp.where(first, 0.0, l)
            sT = _bias_mask_t(cfg, s_cur[...], nb, mask_ref, cols(j), None if mask_index is None else mask_index(i, j))
            m_new = jnp.maximum(m, jnp.max(sT, axis=0, keepdims=True))
            p = jnp.exp(sT - m_new)
            alpha = jnp.exp(m - m_new)
            p_cur[...] = p.astype(BF16)
            return i_n, j_n, i, j, m_new, alpha * l + jnp.sum(p, axis=0, keepdims=True), acc_full * alpha

        def two(_, carry):
            i, j, i_prev, j_prev, m, l, accT = carry
            i, j, i_prev, j_prev, m, l, accT = half(i, j, i_prev, j_prev, s_a, s_b, p_a, p_b, m, l, accT)
            return half(i, j, i_prev, j_prev, s_b, s_a, p_b, p_a, m, l, accT)

        s_a[...] = _raw_scores_t(cfg, ks[cols(0), :], qT2s[0])
        p_b[...] = jnp.zeros((TK, R), BF16)
        zero = jnp.int32(0)
        init = (zero, zero, zero, zero, jnp.full((1, R), NEG_INF, F32), jnp.ones((1, R), F32),
                jnp.zeros((LANES, R), F32))
        _, _, i_prev, j_prev, m, l, accT = lax.fori_loop(0, n_pairs // 2, two, init)
        park(i_prev, m, l, accT + jnp.dot(vTs[j_prev], p_b[...], preferred_element_type=F32))
        lax.fori_loop(0, nq, finish, 0)

    ins, in_specs = _attn_t_inputs(kind, src, S, negc_cols, mask, rope, kv)
    W = cfg["n_blocks"] * LANES
    scratch = [pltpu.VMEM((nq, LANES, R), BF16), pltpu.VMEM((Sk, LANES), BF16), pltpu.VMEM((nk, LANES, TK), BF16),
               pltpu.VMEM((TK, R), F32), pltpu.VMEM((TK, R), F32), pltpu.VMEM((TK, R), BF16), pltpu.VMEM((TK, R), BF16),
               pltpu.VMEM((nq, LANES, R), F32), pltpu.VMEM((nq, R), F32), pltpu.VMEM((nq, R), F32)]
    if has_bias:
        scratch.append(pltpu.VMEM((nh, Sk, LANES), F32))
    return pl.pallas_call(
        body, name=kind + "_attn_fwd", grid=(B, cfg["n_blocks"]),
        in_specs=in_specs,
        out_specs=[pl.BlockSpec((1, S, LANES), lambda b, h: (b, 0, h)),
                   pl.BlockSpec((1, 1, nq, R), lambda b, h: (b, h, 0, 0))],
        out_shape=[jax.ShapeDtypeStruct((B, S, W), F32), jax.ShapeDtypeStruct((B, cfg["n_blocks"], nq, R), F32)],
        scratch_shapes=scratch,
        compiler_params=_params(("arbitrary", "arbitrary")),
    )(*ins)


def attn_bwd3(kind, src, do, o, lse, S, *, negc_cols=None, mask=None, rope=None, kv=None, token=None):
    B = src.shape[0]
    cfg = _attn_setup(kind)
    pair, nh, s_scale, q_fold = cfg["pair"], cfg["nh"], cfg["s_scale"], cfg["q_fold"]
    Sk = S if pair else MEM_LEN
    has_bias, has_rope = negc_cols is not None, rope is not None
    R = nh * TQ
    nq, nk = S // TQ, Sk // TK
    n_pairs, successor, mask_index = _tile_walk(kind, nq, nk)
    assert n_pairs % 2 == 0

    def body(*refs):
        refs = list(refs)
        if pair:
            qkv_ref = refs.pop(0)
            load_q = lambda rows: qkv_ref[0, rows, 0:LANES]
            load_kv = lambda rows: (qkv_ref[0, rows, LANES:2 * LANES], qkv_ref[0, rows, 2 * LANES:3 * LANES])
        else:
            q_ref, k_ref, v_ref = refs.pop(0), refs.pop(0), refs.pop(0)
            load_q = lambda rows: q_ref[0, rows, :]
            load_kv = lambda rows: (k_ref[0, rows, :], v_ref[0, rows, :])
        negc_ref = refs.pop(0) if has_bias else None
        mask_ref = refs.pop(0) if mask is not None else None
        rope_refs = [refs.pop(0) for _ in range(3)] if has_rope else None
        do_ref, o_ref, lse_ref = refs.pop(0), refs.pop(0), refs.pop(0)
        if token is not None:
            refs.pop(0)
        if pair:
            dqkv_ref = refs.pop(0)
            dneg_ref = refs.pop(0) if has_bias else None
            drow_ref = refs.pop(0) if has_bias else None
        else:
            dq_ref, dk_ref, dv_ref = refs.pop(0), refs.pop(0), refs.pop(0)
        qT2s, ks, q2s, vs, kTs, doT2s, do2s, delta_s, dk_acc, dv_acc = refs[:10]
        bufs_a, bufs_b, dq_all = refs[10:14], refs[14:18], refs[18]
        nb, dneg_acc, drow_all = (refs[19], refs[20], refs[21]) if has_bias else (None, None, None)
        lane = lax.broadcasted_iota(jnp.int32, (1, LANES), 1)
        _attn_t_prep(cfg, dict(load_q=load_q, load_kv=load_kv, rope=rope_refs, negc=negc_ref,
                               block=pl.program_id(1)), S, Sk,
                     qT2s=qT2s, ks=ks, q2s=q2s, vs=vs, kTs=kTs, nb=nb)

        def prep_do(n, _):
            rows = pl.ds(pl.multiple_of(n * TQ, TQ), TQ)
            dob = do_ref[0, rows, :].astype(BF16)
            _store_stacked(cfg, lane, do2s, n, dob)
            doT = dob.astype(F32).T
            prodT = doT * o_ref[0, rows, :].T
            doTb = doT.astype(BF16)
            for hh in range(nh):
                hm = _head_rows(hh, pair)
                doT2s[n, :, hh * TQ:(hh + 1) * TQ] = jnp.where(hm, doTb, jnp.zeros_like(doTb))
                delta_s[pl.ds(n, 1), hh * TQ:(hh + 1) * TQ] = jnp.sum(jnp.where(hm, prodT, 0.0), axis=0, keepdims=True)
            return 0

        def zero_kv(n, _):
            rows = pl.ds(pl.multiple_of(n * TK, TK), TK)
            dk_acc[rows, :] = jnp.zeros((TK, LANES), F32)
            dv_acc[rows, :] = jnp.zeros((TK, LANES), F32)
            if has_bias:
                for hh in range(nh):
                    dneg_acc[hh, rows, :] = jnp.zeros((TK, LANES), F32)
            return 0

        lax.fori_loop(0, nq, prep_do, 0)
        lax.fori_loop(0, nk, zero_kv, 0)

        def cols(j):
            return pl.ds(pl.multiple_of(j * TK, TK), TK)

        def rows2(i):
            return pl.ds(pl.multiple_of(i * R, R), R)

        def park(i, dqT2, drow):
            dq_all[i] = dqT2
            if has_bias:
                drow_all[pl.ds(i, 1), :] = drow

        def half(i, j, i_prev, j_prev, cur, nxt_bufs, prv, dqT2, drow):
            s_cur, dp_cur, pb_cur, dsb_cur = cur
            s_next, dp_next = nxt_bufs[0], nxt_bufs[1]
            pb_prev, dsb_prev = prv[2], prv[3]
            i_n, j_n = successor(i, j)
            first = j == 0
            if has_bias:
                drow_all[pl.ds(i_prev, 1), :] = drow
            drow = jnp.where(first, 0.0, drow)
            kc = cols(j)
            sT = _bias_mask_t(cfg, s_cur[...], nb, mask_ref, kc, None if mask_index is None else mask_index(i, j))
            pT = jnp.exp(sT - lse_ref[0, 0, pl.ds(i, 1), :])
            dsT = pT * (dp_cur[...] - delta_s[pl.ds(i, 1), :])
            if has_bias:
                drow = drow + jnp.sum(dsT, axis=0, keepdims=True)
                for hh in range(nh):
                    part = dsT[:, hh * TQ:hh * TQ + LANES]
                    for t in range(1, TQ // LANES):
                        part = part + dsT[:, hh * TQ + t * LANES:hh * TQ + (t + 1) * LANES]
                    dneg_acc[hh, kc, :] += part
            if s_scale is not None:
                dsT = dsT * s_scale
            pb_cur[...] = pT.astype(BF16)
            dsb_cur[...] = dsT.astype(BF16)
            kp = cols(j_prev)
            dv_acc[kp, :] += jnp.dot(pb_prev[...], do2s[rows2(i_prev), :], preferred_element_type=F32)
            dk_acc[kp, :] += jnp.dot(dsb_prev[...], q2s[rows2(i_prev), :], preferred_element_type=F32)
            dq_full = dqT2 + jnp.dot(kTs[j_prev], dsb_prev[...], preferred_element_type=F32)
            dq_all[i_prev] = dq_full
            dqT2 = jnp.where(first, 0.0, dq_full)
            i_nc = jnp.minimum(i_n, nq - 1)
            kn = cols(j_n)
            s_next[...] = _raw_scores_t(cfg, ks[kn, :], qT2s[i_nc])
            dp_next[...] = jnp.dot(vs[kn, :], doT2s[i_nc], preferred_element_type=F32)
            return i_n, j_n, i, j, dqT2, drow

        def two(_, carry):
            i, j, i_prev, j_prev, dqT2, drow = carry
            i, j, i_prev, j_prev, dqT2, drow = half(i, j, i_prev, j_prev, bufs_a, bufs_b, bufs_b, dqT2, drow)
            return half(i, j, i_prev, j_prev, bufs_b, bufs_a, bufs_a, dqT2, drow)

        bufs_a[0][...] = _raw_scores_t(cfg, ks[cols(0), :], qT2s[0])
        bufs_a[1][...] = jnp.dot(vs[cols(0), :], doT2s[0], preferred_element_type=F32)
        bufs_b[2][...] = jnp.zeros((TK, R), BF16)
        bufs_b[3][...] = jnp.zeros((TK, R), BF16)
        zero = jnp.int32(0)
        init = (zero, zero, zero, zero, jnp.zeros((LANES, R), F32), jnp.zeros((1, R), F32))
        _, _, i_prev, j_prev, dqT2, drow = lax.fori_loop(0, n_pairs // 2, two, init)
        kp = cols(j_prev)
        dv_acc[kp, :] += jnp.dot(bufs_b[2][...], do2s[rows2(i_prev), :], preferred_element_type=F32)
        dk_acc[kp, :] += jnp.dot(bufs_b[3][...], q2s[rows2(i_prev), :], preferred_element_type=F32)
        park(i_prev, dqT2 + jnp.dot(kTs[j_prev], bufs_b[3][...], preferred_element_type=F32), drow)

        def fin_q(i, _):
            rows = pl.ds(pl.multiple_of(i * TQ, TQ), TQ)
            dqT2 = dq_all[i]
            dqT = jnp.where(_head_rows(0, True), dqT2[:, 0:TQ], dqT2[:, TQ:2 * TQ]) if pair else dqT2
            dq = dqT.T
            if q_fold is not None:
                dq = dq * q_fold
            if has_rope:
                dq = _rope_bwd(dq, *[t[rows, :] for t in rope_refs])
            if pair:
                dqkv_ref[0, rows, 0:LANES] = dq.astype(BF16)
            else:
                dq_ref[0, rows, :] = dq.astype(BF16)
            if has_bias:
                drow_ref[0, 0, pl.ds(i, 1), :] = drow_all[pl.ds(i, 1), :]
            return 0

        lax.fori_loop(0, nq, fin_q, 0)

        def fin_kv(n, _):
            rows = pl.ds(pl.multiple_of(n * TK, TK), TK)
            dk = dk_acc[rows, :]
            if has_rope:
                dk = _rope_bwd(dk, *[t[rows, :] for t in rope_refs])
            if pair:
                dqkv_ref[0, rows, LANES:2 * LANES] = dk.astype(BF16)
                dqkv_ref[0, rows, 2 * LANES:3 * LANES] = dv_acc[rows, :].astype(BF16)
            else:
                dk_ref[0, rows, :] = dk.astype(BF16)
                dv_ref[0, rows, :] = dv_acc[rows, :].astype(BF16)
            if has_bias:
                x0 = jnp.sum(dneg_acc[0, rows, :], axis=1, keepdims=True)
                x1 = jnp.sum(dneg_acc[1, rows, :], axis=1, keepdims=True)
                dneg_ref[0, rows, :] = jnp.where(lane == 0, x0, jnp.where(lane == 1, x1, 0.0))
            return 0

        lax.fori_loop(0, nk, fin_kv, 0)

    ins, in_specs = _attn_t_inputs(kind, src, S, negc_cols, mask, rope, kv)
    row_spec = pl.BlockSpec((1, S, LANES), lambda b, h: (b, 0, h))
    vec_spec = pl.BlockSpec((1, 1, nq, R), lambda b, h: (b, h, 0, 0))
    ins += [do, o, lse]
    in_specs += [row_spec, row_spec, vec_spec]
    if token is not None:
        ins.append(token)
        in_specs.append(pl.BlockSpec(token.shape, lambda b, h: (0, 0)))
    W = cfg["n_blocks"] * LANES
    if pair:
        out_specs = [pl.BlockSpec((1, S, PAIR_W), lambda b, h: (b, 0, h))]
        out_shape = [jax.ShapeDtypeStruct((B, S, 3 * W), BF16)]
        if has_bias:
            out_specs += [row_spec, vec_spec]
            out_shape += [jax.ShapeDtypeStruct((B, S, W), F32), jax.ShapeDtypeStruct((B, cfg["n_blocks"], nq, R), F32)]
    else:
        kv_spec = pl.BlockSpec((1, MEM_LEN, LANES), lambda b, h: (b, 0, h))
        out_specs = [row_spec, kv_spec, kv_spec]
        out_shape = [jax.ShapeDtypeStruct((B, S, W), BF16)] + [jax.ShapeDtypeStruct((B, MEM_LEN, W), BF16)] * 2
    scratch = [pltpu.VMEM((nq, LANES, R), BF16), pltpu.VMEM((Sk, LANES), BF16), pltpu.VMEM((nh * S, LANES), BF16),
               pltpu.VMEM((Sk, LANES), BF16), pltpu.VMEM((nk, LANES, TK), BF16), pltpu.VMEM((nq, LANES, R), BF16),
               pltpu.VMEM((nh * S, LANES), BF16), pltpu.VMEM((nq, R), F32),
               pltpu.VMEM((Sk, LANES), F32), pltpu.VMEM((Sk, LANES), F32)]
    pair_bufs = [pltpu.VMEM((TK, R), F32), pltpu.VMEM((TK, R), F32), pltpu.VMEM((TK, R), BF16), pltpu.VMEM((TK, R), BF16)]
    scratch += pair_bufs + pair_bufs + [pltpu.VMEM((nq, LANES, R), F32)]
    if has_bias:
        scratch += [pltpu.VMEM((nh, Sk, LANES), F32), pltpu.VMEM((nh, Sk, LANES), F32), pltpu.VMEM((nq, R), F32)]
    return pl.pallas_call(
        body, name=kind + "_attn_bwd", grid=(B, cfg["n_blocks"]),
        in_specs=in_specs, out_specs=out_specs, out_shape=out_shape, scratch_shapes=scratch,
        compiler_params=_params(("arbitrary", "arbitrary")),
    )(*ins)


def _sigmoid(g):
    return 1.0 / (1.0 + jnp.exp(-g))


def out_fwd(proj, o_fox, o_dil, o_mem, w_out, x, target, gf, tm):
    T = x.shape[0]

    def body(fg_ref, dg_ref, mg_ref, of_ref, od_ref, om_ref, w_ref, x_ref, t_ref, gf_ref,
             y_ref, dx_ref, dxb_ref, sm_ref):
        parts = []
        for g_ref, o_ref in ((fg_ref, of_ref), (dg_ref, od_ref), (mg_ref, om_ref)):
            g = g_ref[...]
            parts.append((o_ref[...] * (g * _sigmoid(g))).astype(BF16))
        ymix = jnp.concatenate(parts, axis=1)
        y_ref[...] = ymix
        x2 = x_ref[...] + jnp.dot(ymix, w_ref[...], preferred_element_type=F32)
        r = lax.rsqrt(jnp.mean(x2 * x2, axis=-1, keepdims=True) + RMS_EPS)
        yn = x2 * r
        err = yn * gf_ref[...] - t_ref[...]
        loss = 0.5 * jnp.sum(jnp.sum(err * err, axis=-1, keepdims=True) / D_MODEL, axis=0, keepdims=True)
        dyf = err / D_MODEL
        dgf = jnp.sum(dyf * yn, axis=0, keepdims=True)
        dyn = dyf * gf_ref[...]
        dx2 = r * (dyn - yn * jnp.mean(dyn * yn, axis=-1, keepdims=True))
        dx_ref[...] = dx2
        dxb_ref[...] = dx2.astype(BF16)
        row = lax.broadcasted_iota(jnp.int32, (8, D_MODEL), 0)
        upd = jnp.where(row == 0, dgf, jnp.where(row == 1, loss, 0.0))

        @pl.when(pl.program_id(0) == 0)
        def _():
            sm_ref[...] = upd

        @pl.when(pl.program_id(0) != 0)
        def _():
            sm_ref[...] += upd

    def rows(w, col=0):
        return pl.BlockSpec((tm, w), lambda i: (i, col))

    return pl.pallas_call(
        body, name="out_fwd", grid=(T // tm,),
        in_specs=[rows(FOX_W, P_FG // FOX_W), rows(DIL_W, P_DG // DIL_W), rows(MEM_W, P_MG // MEM_W),
                  rows(FOX_W), rows(DIL_W), rows(MEM_W),
                  pl.BlockSpec((MIX_W, D_MODEL), lambda i: (0, 0)),
                  rows(D_MODEL), rows(D_MODEL), pl.BlockSpec((1, D_MODEL), lambda i: (0, 0))],
        out_specs=[rows(MIX_W), rows(D_MODEL), rows(D_MODEL), pl.BlockSpec((8, D_MODEL), lambda i: (0, 0))],
        out_shape=[jax.ShapeDtypeStruct((T, MIX_W), BF16), jax.ShapeDtypeStruct((T, D_MODEL), F32),
                   jax.ShapeDtypeStruct((T, D_MODEL), BF16), jax.ShapeDtypeStruct((8, D_MODEL), F32)],
        compiler_params=_params(("arbitrary",)),
    )(proj, proj, proj, o_fox, o_dil, o_mem, w_out, x, target, gf)


def out_bwd(proj, o_fox, o_dil, o_mem, w_out, dx2b, tm):
    T = dx2b.shape[0]

    def body(fg_ref, dg_ref, mg_ref, of_ref, od_ref, om_ref, w_ref, dx_ref,
             dof_ref, dod_ref, dom_ref, dfg_ref, ddg_ref, dmg_ref):
        dmix = lax.dot_general(dx_ref[...], w_ref[...], (((1,), (1,)), ((), ())), preferred_element_type=F32)
        col = 0
        for g_ref, o_ref, do_ref, dgate_ref in ((fg_ref, of_ref, dof_ref, dfg_ref), (dg_ref, od_ref, dod_ref, ddg_ref),
                                                 (mg_ref, om_ref, dom_ref, dmg_ref)):
            w = g_ref.shape[1]
            d = dmix[:, col:col + w]
            col += w
            g = g_ref[...]
            sg = _sigmoid(g)
            do_ref[...] = d * (g * sg)
            dgate_ref[...] = (d * o_ref[...] * (sg * (1.0 + g * (1.0 - sg)))).astype(BF16)

    def rows(w, col=0):
        return pl.BlockSpec((tm, w), lambda i: (i, col))

    return pl.pallas_call(
        body, name="out_bwd", grid=(T // tm,),
        in_specs=[rows(FOX_W, P_FG // FOX_W), rows(DIL_W, P_DG // DIL_W), rows(MEM_W, P_MG // MEM_W),
                  rows(FOX_W), rows(DIL_W), rows(MEM_W),
                  pl.BlockSpec((MIX_W, D_MODEL), lambda i: (0, 0)), rows(D_MODEL)],
        out_specs=[rows(FOX_W), rows(DIL_W), rows(MEM_W), rows(FOX_W), rows(DIL_W), rows(MEM_W)],
        out_shape=[jax.ShapeDtypeStruct((T, FOX_W), F32), jax.ShapeDtypeStruct((T, DIL_W), F32),
                   jax.ShapeDtypeStruct((T, MEM_W), F32), jax.ShapeDtypeStruct((T, FOX_W), BF16),
                   jax.ShapeDtypeStruct((T, DIL_W), BF16), jax.ShapeDtypeStruct((T, MEM_W), BF16)],
        compiler_params=_params(("arbitrary",)),
    )(proj, proj, proj, o_fox, o_dil, o_mem, w_out, dx2b)


def out_step(proj, o_fox, o_dil, o_mem, w_out, x, target, gf, tm):
    T = x.shape[0]

    def body(fg_ref, dg_ref, mg_ref, of_ref, od_ref, om_ref, w_ref, x_ref, t_ref, gf_ref,
             dx_ref, dof_ref, dod_ref, dom_ref, dfg_ref, ddg_ref, dmg_ref, gw_ref, sm_ref, gw_acc):
        branches = []
        for g_ref, o_ref in ((fg_ref, of_ref), (dg_ref, od_ref), (mg_ref, om_ref)):
            g = g_ref[...]
            sg = _sigmoid(g)
            o = o_ref[...]
            branches.append((g, sg, o))
        ymix = jnp.concatenate([(o * (g * sg)).astype(BF16) for g, sg, o in branches], axis=1)
        x2 = x_ref[...] + jnp.dot(ymix, w_ref[...], preferred_element_type=F32)
        r = lax.rsqrt(jnp.mean(x2 * x2, axis=-1, keepdims=True) + RMS_EPS)
        yn = x2 * r
        err = yn * gf_ref[...] - t_ref[...]
        loss = 0.5 * jnp.sum(jnp.sum(err * err, axis=-1, keepdims=True) / D_MODEL, axis=0, keepdims=True)
        dyf = err / D_MODEL
        dgf = jnp.sum(dyf * yn, axis=0, keepdims=True)
        dyn = dyf * gf_ref[...]
        dx2 = r * (dyn - yn * jnp.mean(dyn * yn, axis=-1, keepdims=True))
        dx_ref[...] = dx2
        dxb = dx2.astype(BF16)
        dmix = lax.dot_general(dxb, w_ref[...], (((1,), (1,)), ((), ())), preferred_element_type=F32)
        col = 0
        for (g, sg, o), do_ref, dgate_ref in zip(branches, (dof_ref, dod_ref, dom_ref), (dfg_ref, ddg_ref, dmg_ref)):
            d = dmix[:, col:col + g.shape[1]]
            col += g.shape[1]
            do_ref[...] = d * (g * sg)
            dgate_ref[...] = (d * o * (sg * (1.0 + g * (1.0 - sg)))).astype(BF16)
        gw = lax.dot_general(ymix, dxb, (((0,), (0,)), ((), ())), preferred_element_type=F32)
        row = lax.broadcasted_iota(jnp.int32, (8, D_MODEL), 0)
        upd = jnp.where(row == 0, dgf, jnp.where(row == 1, loss, 0.0))

        @pl.when(pl.program_id(0) == 0)
        def _():
            sm_ref[...] = upd
            gw_acc[...] = gw

        @pl.when(pl.program_id(0) != 0)
        def _():
            sm_ref[...] += upd
            gw_acc[...] += gw

        @pl.when(pl.program_id(0) == T // tm - 1)
        def _():
            gw_ref[...] = gw_acc[...].astype(BF16)

    def rows(w, col=0):
        return pl.BlockSpec((tm, w), lambda i: (i, col))

    return pl.pallas_call(
        body, name="out_step", grid=(T // tm,),
        in_specs=[rows(FOX_W, P_FG // FOX_W), rows(DIL_W, P_DG // DIL_W), rows(MEM_W, P_MG // MEM_W),
                  rows(FOX_W), rows(DIL_W), rows(MEM_W),
                  pl.BlockSpec((MIX_W, D_MODEL), lambda i: (0, 0)),
                  rows(D_MODEL), rows(D_MODEL), pl.BlockSpec((1, D_MODEL), lambda i: (0, 0))],
        out_specs=[rows(D_MODEL), rows(FOX_W), rows(DIL_W), rows(MEM_W), rows(FOX_W), rows(DIL_W), rows(MEM_W),
                   pl.BlockSpec((MIX_W, D_MODEL), lambda i: (0, 0)), pl.BlockSpec((8, D_MODEL), lambda i: (0, 0))],
        out_shape=[jax.ShapeDtypeStruct((T, D_MODEL), F32), jax.ShapeDtypeStruct((T, FOX_W), F32),
                   jax.ShapeDtypeStruct((T, DIL_W), F32), jax.ShapeDtypeStruct((T, MEM_W), F32),
                   jax.ShapeDtypeStruct((T, FOX_W), BF16), jax.ShapeDtypeStruct((T, DIL_W), BF16),
                   jax.ShapeDtypeStruct((T, MEM_W), BF16), jax.ShapeDtypeStruct((MIX_W, D_MODEL), BF16),
                   jax.ShapeDtypeStruct((8, D_MODEL), F32)],
        scratch_shapes=[pltpu.VMEM((MIX_W, D_MODEL), F32)],
        compiler_params=_params(("arbitrary",)),
    )(proj, proj, proj, o_fox, o_dil, o_mem, w_out, x, target, gf)


def adamw(w, g, m, v, tr, name):
    lead = w.shape[:-2]
    R, C = w.shape[-2:]
    zeros = (0,) * len(lead)

    def body(w_ref, g_ref, m_ref, v_ref, d_ref, mo_ref, vo_ref):
        gv = g_ref[...]
        mn = ADAM_B1 * m_ref[...] + (1.0 - ADAM_B1) * gv
        vn = ADAM_B2 * v_ref[...] + (1.0 - ADAM_B2) * jnp.square(gv)
        m_hat = mn / (1.0 - ADAM_B1 ** ADAM_STEP)
        v_hat = vn / (1.0 - ADAM_B2 ** ADAM_STEP)
        d_ref[...] = -ADAM_LR * (m_hat / (jnp.sqrt(v_hat) + ADAM_EPS) + ADAM_WD * w_ref[...])
        mo_ref[...] = mn
        vo_ref[...] = vn

    spec = pl.BlockSpec((1,) * len(lead) + (tr, C), lambda i: zeros + (i, 0))
    return pl.pallas_call(
        body, name=name, grid=(pl.cdiv(R, tr),),
        in_specs=[spec] * 4, out_specs=[spec] * 3,
        out_shape=[jax.ShapeDtypeStruct(w.shape, F32)] * 3,
        compiler_params=_params(("arbitrary",)),
    )(w, g, m, v)


def _pad_row(v, width):
    return jnp.concatenate([v, jnp.zeros((1, width - v.shape[1]), v.dtype)], axis=1)


def _old_local_grads(x, mem, norm_g, b_forget, mem_norm_g, final_norm_g, loss_target, w_in_p, w_kv, w_out):
    B, S, D = x.shape
    T = B * S
    xt = x.reshape(T, D)
    memt = mem.reshape(B * MEM_LEN, D)
    b_pad = _pad_row(b_forget, LANES)

    h = rms_fwd(xt, norm_g, 512, "rms_x")
    proj = mm_nn(h, w_in_p, 512, PW // 3, "in_proj")
    proj3 = proj.reshape(B, S, PW)
    mh = rms_fwd(memt, mem_norm_g, B * MEM_LEN, "rms_mem")
    mkv = mm_nn(mh, w_kv, B * MEM_LEN, 2 * MEM_W, "mem_kv_proj")
    mkv3 = mkv.reshape(B, MEM_LEN, 2 * MEM_W)

    negc = fox_gate(proj3, b_pad)
    causal = _log_masks_t(S, "causal")
    causal = jnp.concatenate([causal, jnp.zeros_like(causal)], axis=0)
    dilated = _log_masks_t(S, "dilated")
    rope = _rope_tables(S)

    o_fox, lse_fox = attn_fwd4("fox", proj3, S, negc_cols=negc, mask=causal)
    o_dil, lse_dil = attn_fwd4("dil", proj3, S, mask=dilated, rope=rope)
    o_mem, lse_mem = attn_fwd4("mem", proj3, S, kv=mkv3)

    dx2, do_fox, do_dil, do_mem, dfg, ddg, dmg, g_out, small_out = out_step(
        proj, o_fox.reshape(T, FOX_W), o_dil.reshape(T, DIL_W), o_mem.reshape(T, MEM_W), w_out,
        xt, loss_target.reshape(T, D), final_norm_g.reshape(1, D), 256)

    dqkv_fox, dneg, drow = attn_bwd3("fox", proj3, do_fox.reshape(B, S, FOX_W), o_fox, lse_fox, S,
                                     negc_cols=negc, mask=causal)
    (dqkv_dil,) = attn_bwd3("dil", proj3, do_dil.reshape(B, S, DIL_W), o_dil, lse_dil, S, mask=dilated, rope=rope)
    dmq, dmk, dmv = attn_bwd3("mem", proj3, do_mem.reshape(B, S, MEM_W), o_mem, lse_mem, S, kv=mkv3)
    drow = drow.reshape(B, FOX_HEADS // 2, S // TQ, 2, TQ).transpose(0, 1, 3, 2, 4).reshape(B, FOX_HEADS, S)
    drow = jnp.pad(drow, ((0, 0), (0, LANES - FOX_HEADS), (0, 0)))
    dflog, db_part = fox_gate_bwd(drow, dneg, proj3, b_pad)

    groups = [[(dfg, P_FG), (ddg, P_DG), (dmg, P_MG)],
              [(dqkv_fox.reshape(T, 3 * FOX_W), P_FOX), (dflog.reshape(T, LANES), P_FLOG)],
              [(dqkv_dil.reshape(T, 3 * DIL_W), P_DIL), (dmq.reshape(T, MEM_W), P_MQ)]]
    g_in = [mm_tn_multi(h, [arr for arr, _ in grp], 512, "w_in_grad_%d" % n) for n, grp in enumerate(groups)]
    grad_x, dng = in_proj_bwd_rms([piece for grp in groups for piece in grp], w_in_p, xt, norm_g, dx2, 256)

    dmkv = jnp.concatenate([dmk, dmv], axis=2).reshape(B * MEM_LEN, 2 * MEM_W)
    g_kv = mm_tn(mh, dmkv, B * MEM_LEN, 2 * MEM_W, "w_kv_grad")
    dmh = mm_nt(dmkv, w_kv, B * MEM_LEN, D, "mem_kv_bwd")
    _, dmng = rms_bwd(memt, mem_norm_g, dmh, None, B * MEM_LEN, "rms_mem_bwd")

    small = jnp.concatenate([dng[0:1], dmng[0:1], small_out[0:1], _pad_row(db_part[0:1], D), small_out[1:2],
                             jnp.zeros((3, D), F32)], axis=0)
    return grad_x.reshape(B, S, D), g_in, g_kv, g_out, small


def local_grads(x, mem, norm_g, b_forget, mem_norm_g, final_norm_g, loss_target, w_in_p, w_kv, w_out, start_exchange):
    B, S, D = x.shape
    T = B * S
    xt = x.reshape(T, D)
    memt = mem.reshape(B * MEM_LEN, D)
    b_pad = _pad_row(b_forget, LANES)

    h = rms_fwd(xt, norm_g, 512, "rms_x")
    proj = mm_nn(h, w_in_p, 512, PW // 3, "in_proj")
    proj3 = proj.reshape(B, S, PW)
    mh = rms_fwd(memt, mem_norm_g, B * MEM_LEN, "rms_mem")
    mkv = mm_nn(mh, w_kv, B * MEM_LEN, 2 * MEM_W, "mem_kv_proj")
    mkv3 = mkv.reshape(B, MEM_LEN, 2 * MEM_W)

    negc = fox_gate(proj3, b_pad)
    causal = _log_masks_t(S, "causal")
    causal = jnp.concatenate([causal, jnp.zeros_like(causal)], axis=0)
    dilated = _log_masks_t(S, "dilated")
    rope = _rope_tables(S)

    o_fox, lse_fox = attn_fwd4("fox", proj3, S, negc_cols=negc, mask=causal)
    o_dil, lse_dil = attn_fwd4("dil", proj3, S, mask=dilated, rope=rope)
    o_mem, lse_mem = attn_fwd4("mem", proj3, S, kv=mkv3)

    dx2, do_fox, do_dil, do_mem, dfg, ddg, dmg, g_out, small_out = out_step(
        proj, o_fox.reshape(T, FOX_W), o_dil.reshape(T, DIL_W), o_mem.reshape(T, MEM_W), w_out,
        xt, loss_target.reshape(T, D), final_norm_g.reshape(1, D), 256)

    gates = [(dfg, P_FG), (ddg, P_DG), (dmg, P_MG)]
    g_gates = mm_tn_multi(h, [arr for arr, _ in gates], 512, "w_in_grad_gates", BF16)
    first, token = start_exchange([g_gates, g_out], "early_exchange_a")

    dqkv_fox, dneg, drow = attn_bwd3("fox", proj3, do_fox.reshape(B, S, FOX_W), o_fox, lse_fox, S,
                                     negc_cols=negc, mask=causal, token=token)
    drow = drow.reshape(B, FOX_HEADS // 2, S // TQ, 2, TQ).transpose(0, 1, 3, 2, 4).reshape(B, FOX_HEADS, S)
    drow = jnp.pad(drow, ((0, 0), (0, LANES - FOX_HEADS), (0, 0)))
    dflog, db_part = fox_gate_bwd(drow, dneg, proj3, b_pad)
    fox = [(dqkv_fox.reshape(T, 3 * FOX_W), P_FOX), (dflog.reshape(T, LANES), P_FLOG)]
    g_fox = mm_tn_multi(h, [arr for arr, _ in fox], 512, "w_in_grad_fox", BF16)
    second, token = start_exchange([g_fox], "early_exchange_b")

    (dqkv_dil,) = attn_bwd3("dil", proj3, do_dil.reshape(B, S, DIL_W), o_dil, lse_dil, S, mask=dilated, rope=rope,
                            token=token)
    dmq, dmk, dmv = attn_bwd3("mem", proj3, do_mem.reshape(B, S, MEM_W), o_mem, lse_mem, S, kv=mkv3)
    rest = [(dqkv_dil.reshape(T, 3 * DIL_W), P_DIL), (dmq.reshape(T, MEM_W), P_MQ)]
    g_rest = mm_tn_multi(h, [arr for arr, _ in rest], 512, "w_in_grad_rest")
    grad_x, dng = in_proj_bwd_rms(gates + fox + rest, w_in_p, xt, norm_g, dx2, 256)

    dmkv = jnp.concatenate([dmk, dmv], axis=2).reshape(B * MEM_LEN, 2 * MEM_W)
    g_kv = mm_tn(mh, dmkv, B * MEM_LEN, 2 * MEM_W, "w_kv_grad")
    dmh = mm_nt(dmkv, w_kv, B * MEM_LEN, D, "mem_kv_bwd")
    _, dmng = rms_bwd(memt, mem_norm_g, dmh, None, B * MEM_LEN, "rms_mem_bwd")

    small = jnp.concatenate([dng[0:1], dmng[0:1], small_out[0:1], _pad_row(db_part[0:1], D), small_out[1:2],
                             jnp.zeros((3, D), F32)], axis=0)
    return grad_x.reshape(B, S, D), [(first, dqkv_fox), (second, dqkv_dil)], [g_rest, g_kv], small


def kernel(x, mem, norm_g, w_in, b_forget, mem_norm_g, w_mem_kv, w_out, final_norm_g, loss_target, m_norm_g, m_w_in, m_b_forget, m_mem_norm_g, m_w_mem_kv, m_w_out, m_final_norm_g, v_norm_g, v_w_in, v_b_forget, v_mem_norm_g, v_w_mem_kv, v_w_out, v_final_norm_g):
    D = D_MODEL
    w_in_full, w_kv_full, w_out_full = weight_gather(
        [_pack_cols(w_in).astype(BF16).reshape(w_in.shape[1], PW), w_mem_kv[0].astype(BF16), w_out[0].astype(BF16)])
    grad_x, early, late, small = local_grads(
        x, mem, norm_g, b_forget, mem_norm_g, final_norm_g, loss_target, w_in_full, w_kv_full, w_out_full,
        early_exchange_start)

    (first, after_first), (second, after_second) = early
    (src_gates, src_out), (land_gates, land_out) = early_exchange_wait(first, after_first, "early_wait_a")
    (src_fox,), (land_fox,) = early_exchange_wait(second, after_second, "early_wait_b")
    gates = slot_sum8(src_gates, land_gates, 128, "sum_w_in_gates")
    gw_out = slot_sum8(src_out, land_out, 256, "sum_w_out")
    fox = slot_sum8(src_fox, land_fox, 128, "sum_w_in_fox")

    *from_sibling, csum = grad_exchange_d2d(late, small)
    names = ("w_in_rest", "w_kv")
    chip_parts = [chip_sum(g, got, 128, "chip_sum_" + n) for g, got, n in zip(late, from_sibling, names)]
    *from_chips, tot = grad_exchange_ici(chip_parts, csum)
    dil, gw_kv = [final_sum(got, 128, "final_sum_" + n) for got, n in zip(from_chips, names)]
    gw_in = _unpack_cols(jnp.concatenate(
        [fox[:, :3 * FOX_W], gates[:, :FOX_W], dil[:, :3 * DIL_W], gates[:, FOX_W:FOX_W + DIL_W], dil[:, 3 * DIL_W:],
         gates[:, FOX_W + DIL_W:], fox[:, 3 * FOX_W:]], axis=1)[None])

    loss = tot[4, 0]
    g_norm, g_mem_norm, g_final, g_b = tot[0:1], tot[1:2], tot[2], tot[3:4, :FOX_HEADS]

    def rows8(*rows):
        rows = [r.reshape(1, -1) for r in rows]
        rows = [_pad_row(r, D) for r in rows]
        return jnp.concatenate(rows + [jnp.zeros((8 - len(rows), D), F32)], axis=0)

    sw = rows8(norm_g, mem_norm_g, final_norm_g, b_forget)
    sm = rows8(m_norm_g, m_mem_norm_g, m_final_norm_g, m_b_forget)
    sv = rows8(v_norm_g, v_mem_norm_g, v_final_norm_g, v_b_forget)
    d_s, m_s, v_s = adamw(sw, tot, sm, sv, 8, "adamw_small")
    d_in, m_in, v_in = adamw(w_in, gw_in, m_w_in, v_w_in, 32, "adamw_w_in")
    d_kv, m_kv, v_kv = adamw(w_mem_kv[0], gw_kv, m_w_mem_kv[0], v_w_mem_kv[0], 128, "adamw_w_kv")
    d_out, m_out, v_out = adamw(w_out[0], gw_out, m_w_out[0], v_w_out[0], 256, "adamw_w_out")

    def small_outs(t):
        return t[0:1], t[3:4, :FOX_HEADS], t[1:2], t[2]

    grads = (g_norm, gw_in, g_b, g_mem_norm, gw_kv[None], gw_out[None], g_final)
    outs = []
    for t, big in ((d_s, (d_in, d_kv, d_out)), (m_s, (m_in, m_kv, m_out)), (v_s, (v_in, v_kv, v_out))):
        n, b, mn, f = small_outs(t)
        outs += [n, big[0], b, mn, big[1][None], big[2][None], f]
    return (loss, grad_x, *grads, *outs)
```

```python
import functools
import math

import numpy as np
import jax
import jax.numpy as jnp
from jax import lax
from jax.experimental import pallas as pl
from jax.experimental.pallas import tpu as pltpu

F32 = jnp.float32
BF16 = jnp.bfloat16

D_MODEL = 1024
HEAD_DIM = 64
FOX_HEADS = 12
DIL_HEADS = 12
MEM_HEADS = 4
MEM_HEAD_DIM = 128
MEM_LEN = 256
FOX_W = FOX_HEADS * HEAD_DIM
DIL_W = DIL_HEADS * HEAD_DIM
MEM_W = MEM_HEADS * MEM_HEAD_DIM
MIX_W = FOX_W + DIL_W + MEM_W
DILATIONS = ((128, 1), (512, 4), (2048, 16))
ROPE_THETA = 500000.0
ROPE_DIM = HEAD_DIM // 4
RMS_EPS = 1e-6
NEG_INF = -1e30
IN_W = 4 * FOX_W + FOX_HEADS + 4 * DIL_W + 2 * MEM_W

ADAM_LR = 0.001
ADAM_B1 = 0.9
ADAM_B2 = 0.999
ADAM_EPS = 1e-08
ADAM_WD = 0.01
ADAM_STEP = 10

N_DEV = 8
LANES = 128
PAIR_W = 3 * LANES
TQ = 256
TK = 256

O_FQ, O_FK, O_FV, O_FG = 0, FOX_W, 2 * FOX_W, 3 * FOX_W
O_FLOG = 4 * FOX_W
O_DQ = O_FLOG + FOX_HEADS
O_DK, O_DV, O_DG = O_DQ + DIL_W, O_DQ + 2 * DIL_W, O_DQ + 3 * DIL_W
O_MQ = O_DQ + 4 * DIL_W
O_MG = O_MQ + MEM_W
P_FOX = 0
P_FG = P_FOX + 3 * FOX_W
P_DIL = P_FG + FOX_W
P_DG = P_DIL + 3 * DIL_W
P_MQ = P_DG + DIL_W
P_MG = P_MQ + MEM_W
P_FLOG = P_MG + MEM_W
PW = P_FLOG + LANES

VMEM_LIMIT = 56 * 1024 * 1024


def _pack_pieces():
    pieces = []
    for base in (O_FQ, O_DQ):
        seg = []
        for hp in range(FOX_HEADS // 2):
            for part in range(3):
                seg.append((base + part * FOX_W + hp * LANES, LANES))
        pieces.append(seg)
    fox, dil = pieces
    return fox + [(O_FG, FOX_W)] + dil + [(O_DG, DIL_W), (O_MQ, MEM_W), (O_MG, MEM_W), (O_FLOG, FOX_HEADS)]


def _pack_cols(w):
    parts = [w[..., s:s + n] for s, n in _pack_pieces()]
    parts.append(jnp.zeros(w.shape[:-1] + (LANES - FOX_HEADS,), w.dtype))
    return jnp.concatenate(parts, axis=-1)


def _unpack_cols(g):
    runs = []
    pos = 0
    for s, n in _pack_pieces():
        runs.append((s, n, pos))
        pos += n
    runs.sort()
    return jnp.concatenate([g[..., p:p + n] for s, n, p in runs], axis=-1)


def _params(sem=None, **kw):
    return pltpu.CompilerParams(dimension_semantics=sem, vmem_limit_bytes=VMEM_LIMIT, **kw)


def _mesh_pos():
    return lax.axis_index("x"), lax.axis_index("y"), lax.axis_index("c")


def _flip(v, d):
    return 1 - v if d else v


_RELATIONS = [(dx, dy, dc) for dx in (0, 1) for dy in (0, 1) for dc in (0, 1)][1:]


def weight_gather(shards):
    n_arr = len(shards)
    rows = [s.shape[0] for s in shards]

    def body(*refs):
        in_refs = refs[:n_arr]
        out_refs = refs[n_arr:2 * n_arr]
        send_sems, recv_sems, local_sems = refs[2 * n_arr:]
        x, y, c = _mesh_pos()
        me, sibling = (x, y, c), (x, y, 1 - c)
        x_nbr, y_nbr, diag = (1 - x, y, c), (x, 1 - y, c), (1 - x, 1 - y, c)
        north = c == 1
        relay_from = (jnp.where(north, 1 - x, x), jnp.where(north, y, 1 - y), c)
        relay_to = (jnp.where(north, x, 1 - x), jnp.where(north, 1 - y, y), c)
        k_from = jnp.where(north, 1, 2)
        k_to = 3 - k_from

        def block(a, pos):
            px, py, pc = pos
            return out_refs[a].at[pl.ds((4 * px + 2 * py + pc) * rows[a], rows[a]), :]

        def copy(a, k, blk, to, src=None):
            return pltpu.make_async_remote_copy(
                src_ref=block(a, blk) if src is None else src, dst_ref=block(a, blk),
                send_sem=send_sems.at[a, k], recv_sem=recv_sems.at[a, k],
                device_id=to, device_id_type=pl.DeviceIdType.MESH)

        started = []
        mine = []
        for a in range(n_arr):
            cp = pltpu.make_async_copy(in_refs[a], block(a, me), local_sems.at[a])
            cp.start()
            mine.append(cp)
            first = [copy(a, 0, me, sibling, src=in_refs[a]), copy(a, 1, me, x_nbr, src=in_refs[a]),
                     copy(a, 2, me, y_nbr, src=in_refs[a])]
            for cp in first:
                cp.start()
            started += first
        for a in range(n_arr):
            copy(a, k_from, relay_from, me).wait_recv()
            second_hop = copy(a, 3, relay_from, relay_to)
            second_hop.start()
            passed = copy(a, 3 + k_from, relay_from, sibling)
            passed.start()
            started += [second_hop, passed]
        for a in range(n_arr):
            copy(a, k_to, relay_to, me).wait_recv()
            passed = copy(a, 3 + k_to, relay_to, sibling)
            passed.start()
            started.append(passed)
        for a in range(n_arr):
            copy(a, 3, diag, me).wait_recv()
            passed = copy(a, 6, diag, sibling)
            passed.start()
            started.append(passed)
        for a in range(n_arr):
            copy(a, 0, sibling, me).wait_recv()
            for k, chip in ((4, x_nbr), (5, y_nbr), (6, diag)):
                copy(a, k, (chip[0], chip[1], 1 - c), me).wait_recv()
        for cp in started:
            cp.wait_send()
        for cp in mine:
            cp.wait()

    any_spec = pl.BlockSpec(memory_space=pl.ANY)
    return pl.pallas_call(
        body, name="weight_gather",
        out_shape=[jax.ShapeDtypeStruct((N_DEV * s.shape[0], s.shape[1]), s.dtype) for s in shards],
        in_specs=[any_spec] * n_arr, out_specs=[any_spec] * n_arr,
        scratch_shapes=[pltpu.SemaphoreType.DMA((n_arr, 7)), pltpu.SemaphoreType.DMA((n_arr, 7)),
                        pltpu.SemaphoreType.DMA((n_arr,))],
    )(*shards)


def grad_exchange(grads, small):
    arrs = list(grads) + [small]
    n_arr = len(arrs)
    rows = [g.shape[0] // N_DEV for g in grads] + [small.shape[0]]

    def body(*refs):
        in_refs = refs[:n_arr]
        out_refs = refs[n_arr:2 * n_arr]
        send_sems, recv_sems, local_sems = refs[2 * n_arr:]
        x, y, c = _mesh_pos()
        me = 4 * x + 2 * y + c

        def src(a, idx):
            if a == n_arr - 1:
                return in_refs[a]
            return in_refs[a].at[pl.ds(idx * rows[a], rows[a]), :]

        def copy(a, k):
            dx, dy, dc = _RELATIONS[k]
            px, py, pc = _flip(x, dx), _flip(y, dy), _flip(c, dc)
            peer = 4 * px + 2 * py + pc
            send = pltpu.make_async_remote_copy(
                src_ref=src(a, peer), dst_ref=out_refs[a].at[me],
                send_sem=send_sems.at[a, k], recv_sem=recv_sems.at[a, k],
                device_id=(px, py, pc), device_id_type=pl.DeviceIdType.MESH)
            recv = pltpu.make_async_remote_copy(
                src_ref=src(a, peer), dst_ref=out_refs[a].at[peer],
                send_sem=send_sems.at[a, k], recv_sem=recv_sems.at[a, k],
                device_id=(px, py, pc), device_id_type=pl.DeviceIdType.MESH)
            return send, recv

        mine = []
        pairs = []
        for a in range(n_arr):
            cp = pltpu.make_async_copy(src(a, me), out_refs[a].at[me], local_sems.at[a])
            cp.start()
            mine.append(cp)
            for k in range(7):
                send, recv = copy(a, k)
                send.start()
                pairs.append((send, recv))
        for send, recv in pairs:
            recv.wait_recv()
        for send, recv in pairs:
            send.wait_send()
        for cp in mine:
            cp.wait()

    any_spec = pl.BlockSpec(memory_space=pl.ANY)
    return pl.pallas_call(
        body, name="grad_exchange",
        out_shape=[jax.ShapeDtypeStruct((N_DEV, r, a.shape[1]), a.dtype) for r, a in zip(rows, arrs)],
        in_specs=[any_spec] * n_arr, out_specs=[any_spec] * n_arr,
        scratch_shapes=[pltpu.SemaphoreType.DMA((n_arr, 7)), pltpu.SemaphoreType.DMA((n_arr, 7)),
                        pltpu.SemaphoreType.DMA((n_arr,))],
    )(*arrs)


def slot_sum(slots, tr, name):
    _, R, C = slots.shape

    def body(s_ref, o_ref):
        acc = s_ref[0]
        for d in range(1, N_DEV):
            acc = acc + s_ref[d]
        o_ref[...] = acc

    return pl.pallas_call(
        body, name=name, grid=(R // tr,),
        in_specs=[pl.BlockSpec((N_DEV, tr, C), lambda i: (0, i, 0))],
        out_specs=pl.BlockSpec((tr, C), lambda i: (i, 0)),
        out_shape=jax.ShapeDtypeStruct((R, C), slots.dtype),
        compiler_params=_params(("arbitrary",)),
    )(slots)


N_CHIP = 4
_OTHER_CHIPS = [(1, 0), (0, 1), (1, 1)]


def grad_exchange_d2d(grads, small):
    n_big = len(grads)
    rows = [g.shape[0] // N_DEV for g in grads]

    def body(*refs):
        g_refs = refs[:n_big]
        small_ref = refs[n_big]
        out_refs = refs[n_big + 1:2 * n_big + 1]
        csum_ref = refs[2 * n_big + 1]
        land, send_sems, recv_sems = refs[2 * n_big + 2:]
        x, y, c = _mesh_pos()
        sibling = (x, y, 1 - c)
        copies = []
        for a in range(n_big):
            for q in range(N_CHIP):
                copies.append(pltpu.make_async_remote_copy(
                    src_ref=g_refs[a].at[pl.ds((2 * q + 1 - c) * rows[a], rows[a]), :], dst_ref=out_refs[a].at[q],
                    send_sem=send_sems.at[a, q], recv_sem=recv_sems.at[a, q],
                    device_id=sibling, device_id_type=pl.DeviceIdType.MESH))
        copies.append(pltpu.make_async_remote_copy(
            src_ref=small_ref, dst_ref=land, send_sem=send_sems.at[n_big, 0], recv_sem=recv_sems.at[n_big, 0],
            device_id=sibling, device_id_type=pl.DeviceIdType.MESH))
        for cp in copies:
            cp.start()
        for cp in copies:
            cp.wait_recv()
        for cp in copies:
            cp.wait_send()
        csum_ref[...] = small_ref[...] + land[...]

    any_spec = pl.BlockSpec(memory_space=pl.ANY)
    vmem_spec = pl.BlockSpec(memory_space=pltpu.VMEM)
    return pl.pallas_call(
        body, name="grad_exchange_d2d",
        out_shape=[jax.ShapeDtypeStruct((N_CHIP, r, g.shape[1]), g.dtype) for r, g in zip(rows, grads)]
        + [jax.ShapeDtypeStruct(small.shape, small.dtype)],
        in_specs=[any_spec] * n_big + [vmem_spec], out_specs=[any_spec] * n_big + [vmem_spec],
        scratch_shapes=[pltpu.VMEM(small.shape, small.dtype),
                        pltpu.SemaphoreType.DMA((n_big + 1, N_CHIP)), pltpu.SemaphoreType.DMA((n_big + 1, N_CHIP))],
    )(*grads, small)


def chip_sum(g, got, tr, name):
    _, rows, cols = got.shape
    g4 = g.reshape(N_CHIP, 2, rows, cols)
    core = lax.axis_index("c").astype(jnp.int32).reshape(1)

    def body(c_ref, g_ref, r_ref, o_ref):
        o_ref[0] = (g_ref[0, 0] + r_ref[0]).astype(BF16)

    return pl.pallas_call(
        body, name=name,
        grid_spec=pltpu.PrefetchScalarGridSpec(
            num_scalar_prefetch=1, grid=(N_CHIP, rows // tr),
            in_specs=[pl.BlockSpec((1, 1, tr, cols), lambda q, i, w: (q, w[0], i, 0)),
                      pl.BlockSpec((1, tr, cols), lambda q, i, w: (q, i, 0))],
            out_specs=pl.BlockSpec((1, tr, cols), lambda q, i, w: (q, i, 0))),
        out_shape=jax.ShapeDtypeStruct((N_CHIP, rows, cols), BF16),
        compiler_params=_params(("arbitrary", "arbitrary")),
    )(core, g4, got)


def grad_exchange_ici(parts, csum):
    n_big = len(parts)

    def body(*refs):
        p_refs = refs[:n_big]
        csum_ref = refs[n_big]
        out_refs = refs[n_big + 1:2 * n_big + 1]
        tot_ref = refs[2 * n_big + 1]
        land, send_sems, recv_sems, local_sems = refs[2 * n_big + 2:]
        x, y, c = _mesh_pos()
        q_me = 2 * x + y
        land[q_me] = csum_ref[...]
        mine = [pltpu.make_async_copy(p_refs[a].at[q_me], out_refs[a].at[q_me], local_sems.at[a]) for a in range(n_big)]
        for cp in mine:
            cp.start()
        sends, recvs = [], []
        for j, (dx, dy) in enumerate(_OTHER_CHIPS):
            px, py = _flip(x, dx), _flip(y, dy)
            q_peer = 2 * px + py
            for a in range(n_big + 1):
                src = p_refs[a].at[q_peer] if a < n_big else csum_ref
                dst = out_refs[a] if a < n_big else land
                common = dict(send_sem=send_sems.at[a, j], recv_sem=recv_sems.at[a, j],
                              device_id=(px, py, c), device_id_type=pl.DeviceIdType.MESH)
                sends.append(pltpu.make_async_remote_copy(src_ref=src, dst_ref=dst.at[q_me], **common))
                recvs.append(pltpu.make_async_remote_copy(src_ref=src, dst_ref=dst.at[q_peer], **common))
        for cp in sends:
            cp.start()
        for cp in recvs:
            cp.wait_recv()
        for cp in sends:
            cp.wait_send()
        for cp in mine:
            cp.wait()
        tot = land[0]
        for q in range(1, N_CHIP):
            tot = tot + land[q]
        tot_ref[...] = tot

    any_spec = pl.BlockSpec(memory_space=pl.ANY)
    vmem_spec = pl.BlockSpec(memory_space=pltpu.VMEM)
    return pl.pallas_call(
        body, name="grad_exchange_ici",
        out_shape=[jax.ShapeDtypeStruct(p.shape, p.dtype) for p in parts] + [jax.ShapeDtypeStruct(csum.shape, csum.dtype)],
        in_specs=[any_spec] * n_big + [vmem_spec], out_specs=[any_spec] * n_big + [vmem_spec],
        scratch_shapes=[pltpu.VMEM((N_CHIP,) + csum.shape, csum.dtype),
                        pltpu.SemaphoreType.DMA((n_big + 1, 3)), pltpu.SemaphoreType.DMA((n_big + 1, 3)),
                        pltpu.SemaphoreType.DMA((n_big,))],
    )(*parts, csum)


def final_sum(got, tr, name):
    _, rows, cols = got.shape

    def body(got_ref, o_ref):
        acc = got_ref[0].astype(F32)
        for q in range(1, N_CHIP):
            acc = acc + got_ref[q].astype(F32)
        o_ref[...] = acc

    return pl.pallas_call(
        body, name=name, grid=(rows // tr,),
        in_specs=[pl.BlockSpec((N_CHIP, tr, cols), lambda i: (0, i, 0))],
        out_specs=pl.BlockSpec((tr, cols), lambda i: (i, 0)),
        out_shape=jax.ShapeDtypeStruct((rows, cols), F32),
        compiler_params=_params(("arbitrary",)),
    )(got)


_HBM = pl.BlockSpec(memory_space=pltpu.HBM)
_SEM = pl.BlockSpec(memory_space=pltpu.SEMAPHORE)
_EFFECT = pltpu.SideEffectType.DATAFLOW_SIDE_EFFECTING


def _old_early_copies(src_refs, land_refs, send_sems, recv_sems, rows):
    x, y, c = _mesh_pos()
    me = 4 * x + 2 * y + c
    copies = []
    for a in range(len(src_refs)):
        for k, (dx, dy, dc) in enumerate(_RELATIONS):
            px, py, pc = _flip(x, dx), _flip(y, dy), _flip(c, dc)
            peer = 4 * px + 2 * py + pc
            copies.append(pltpu.make_async_remote_copy(
                src_ref=src_refs[a].at[pl.ds(peer * rows[a], rows[a]), :], dst_ref=land_refs[a].at[me],
                send_sem=send_sems.at[a, k], recv_sem=recv_sems.at[a, k],
                device_id=(px, py, pc), device_id_type=pl.DeviceIdType.MESH))
    return copies


def _old_early_exchange_start(srcs, name):
    n = len(srcs)
    rows = [s.shape[0] // N_DEV for s in srcs]
    lands = [lax.empty((N_DEV, r, s.shape[1]), s.dtype) for r, s in zip(rows, srcs)]

    def body(*refs):
        src_refs, land_refs = refs[:n], refs[n:2 * n]
        send_sems, recv_sems = refs[2 * n], refs[2 * n + 1]
        token = refs[-1]
        for cp in _early_copies(src_refs, land_refs, send_sems, recv_sems, rows):
            cp.start()
        token[...] = jnp.zeros_like(token)

    hbm = lambda a: pltpu.HBM(a.shape, a.dtype)
    outs = pl.pallas_call(
        body, name=name,
        out_shape=[pltpu.SemaphoreType.DMA((n, 7)), pltpu.SemaphoreType.DMA((n, 7))]
        + [hbm(a) for a in srcs] + [hbm(a) for a in lands] + [jax.ShapeDtypeStruct((8, LANES), F32)],
        in_specs=[_HBM] * (2 * n),
        out_specs=[_SEM, _SEM] + [_HBM] * (2 * n) + [pl.BlockSpec(memory_space=pltpu.VMEM)],
        input_output_aliases={i: 2 + i for i in range(2 * n)},
        compiler_params=pltpu.CompilerParams(has_side_effects=_EFFECT),
    )(*[pltpu.with_memory_space_constraint(a, pltpu.HBM) for a in list(srcs) + lands])
    return dict(sems=outs[:2], srcs=outs[2:2 + n], lands=outs[2 + n:2 + 2 * n], rows=rows), outs[-1]


def _old_early_exchange_wait(handle, after, name):
    n = len(handle["srcs"])
    rows = handle["rows"]

    def body(*refs):
        src_refs, land_refs = refs[:n], refs[n:2 * n]
        send_sems, recv_sems = refs[2 * n], refs[2 * n + 1]
        for cp in _early_copies(src_refs, land_refs, send_sems, recv_sems, rows):
            cp.wait_send()
            cp.wait_recv()

    hbm = lambda a: pltpu.HBM(a.shape, a.dtype)
    ins = list(handle["srcs"]) + list(handle["lands"])
    outs = pl.pallas_call(
        body, name=name,
        out_shape=[hbm(a) for a in ins],
        in_specs=[_HBM] * (2 * n) + [_SEM, _SEM, pl.BlockSpec(memory_space=pl.ANY)],
        out_specs=[_HBM] * (2 * n),
        input_output_aliases={i: i for i in range(2 * n)},
        compiler_params=pltpu.CompilerParams(has_side_effects=_EFFECT),
    )(*ins, *handle["sems"], after)
    return outs[:n], outs[n:]


def _old_slot_sum8(src, land, tr, name):
    _, rows, cols = land.shape
    x, y, c = _mesh_pos()
    me = (4 * x + 2 * y + c).astype(jnp.int32).reshape(1)

    def body(me_ref, src_ref, land_ref, o_ref):
        acc = None
        for d in range(N_DEV):
            term = jnp.where(d == me_ref[0], src_ref[0], land_ref[d]).astype(F32)
            acc = term if acc is None else acc + term
        o_ref[...] = acc

    return pl.pallas_call(
        body, name=name,
        grid_spec=pltpu.PrefetchScalarGridSpec(
            num_scalar_prefetch=1, grid=(rows // tr,),
            in_specs=[pl.BlockSpec((1, tr, cols), lambda i, w: (w[0], i, 0)),
                      pl.BlockSpec((N_DEV, tr, cols), lambda i, w: (0, i, 0))],
            out_specs=pl.BlockSpec((tr, cols), lambda i, w: (i, 0))),
        out_shape=jax.ShapeDtypeStruct((rows, cols), F32),
        compiler_params=_params(("arbitrary",)),
    )(me, src.reshape(N_DEV, rows, cols), land)


def _old2_early_copies(src_refs, land_refs, send_sems, recv_sems, rows):
    x, y, c = _mesh_pos()
    me = 4 * x + 2 * y + c
    copies = []
    for a in range(len(src_refs)):
        for dx, dy, dc in _RELATIONS:
            px, py, pc = _flip(x, dx), _flip(y, dy), _flip(c, dc)
            peer = 4 * px + 2 * py + pc
            copies.append(pltpu.make_async_remote_copy(
                src_ref=src_refs[a].at[pl.ds(peer * rows[a], rows[a]), :],
                dst_ref=land_refs[a].at[pl.ds(me * rows[a], rows[a]), :],
                send_sem=send_sems[a], recv_sem=recv_sems[a],
                device_id=(px, py, pc), device_id_type=pl.DeviceIdType.MESH))
    return copies


def _old2_early_exchange_start(srcs, name):
    n = len(srcs)
    rows = [s.shape[0] // N_DEV for s in srcs]
    lands = [lax.empty(s.shape, s.dtype) for s in srcs]

    def body(*refs):
        src_refs, land_refs = refs[:n], refs[n:2 * n]
        send_sems, recv_sems = refs[2 * n:3 * n], refs[3 * n:4 * n]
        token = refs[-1]
        for cp in _early_copies(src_refs, land_refs, send_sems, recv_sems, rows):
            cp.start()
        token[...] = jnp.zeros_like(token)

    hbm = lambda a: pltpu.HBM(a.shape, a.dtype)
    outs = pl.pallas_call(
        body, name=name,
        out_shape=[pltpu.SemaphoreType.DMA(())] * (2 * n)
        + [hbm(a) for a in srcs] + [hbm(a) for a in lands] + [jax.ShapeDtypeStruct((8, LANES), F32)],
        in_specs=[_HBM] * (2 * n),
        out_specs=[_SEM] * (2 * n) + [_HBM] * (2 * n) + [pl.BlockSpec(memory_space=pltpu.VMEM)],
        input_output_aliases={i: 2 * n + i for i in range(2 * n)},
        compiler_params=pltpu.CompilerParams(has_side_effects=_EFFECT),
    )(*[pltpu.with_memory_space_constraint(a, pltpu.HBM) for a in list(srcs) + lands])
    return dict(sems=outs[:2 * n], srcs=outs[2 * n:3 * n], lands=outs[3 * n:4 * n], rows=rows), outs[-1]


def _early_copies(src_refs, land_refs, send_sems, recv_sems, rows, gather):
    x, y, c = _mesh_pos()
    me = 4 * x + 2 * y + c
    copies = []
    for a in range(len(src_refs)):
        for dx, dy, dc in _RELATIONS:
            px, py, pc = _flip(x, dx), _flip(y, dy), _flip(c, dc)
            peer = 4 * px + 2 * py + pc
            copies.append(pltpu.make_async_remote_copy(
                src_ref=src_refs[a] if gather else src_refs[a].at[pl.ds(peer * rows[a], rows[a]), :],
                dst_ref=land_refs[a].at[pl.ds(me * rows[a], rows[a]), :],
                send_sem=send_sems[a], recv_sem=recv_sems[a],
                device_id=(px, py, pc), device_id_type=pl.DeviceIdType.MESH))
    return copies


def early_exchange_start(srcs, name, gather=False, after=None):
    n = len(srcs)
    if gather:
        rows = [s.shape[0] for s in srcs]
        me = 4 * lax.axis_index("x") + 2 * lax.axis_index("y") + lax.axis_index("c")
        lands = [lax.dynamic_update_slice(lax.empty((N_DEV * r, s.shape[1]), s.dtype), s, (me * r, 0))
                 for r, s in zip(rows, srcs)]
    else:
        rows = [s.shape[0] // N_DEV for s in srcs]
        lands = [lax.empty(s.shape, s.dtype) for s in srcs]

    extra = [] if after is None else [after]

    def body(*refs):
        src_refs, land_refs = refs[:n], refs[n:2 * n]
        first_sem = 2 * n + len(extra)
        send_sems, recv_sems = refs[first_sem:first_sem + n], refs[first_sem + n:first_sem + 2 * n]
        token = refs[-1]
        for cp in _early_copies(src_refs, land_refs, send_sems, recv_sems, rows, gather):
            cp.start()
        token[...] = jnp.zeros_like(token)

    hbm = lambda a: pltpu.HBM(a.shape, a.dtype)
    outs = pl.pallas_call(
        body, name=name,
        out_shape=[pltpu.SemaphoreType.DMA(())] * (2 * n)
        + [hbm(a) for a in srcs] + [hbm(a) for a in lands] + [jax.ShapeDtypeStruct((8, LANES), F32)],
        in_specs=[_HBM] * (2 * n) + [pl.BlockSpec(memory_space=pl.ANY)] * len(extra),
        out_specs=[_SEM] * (2 * n) + [_HBM] * (2 * n) + [pl.BlockSpec(memory_space=pltpu.VMEM)],
        input_output_aliases={i: 2 * n + i for i in range(2 * n)},
        compiler_params=pltpu.CompilerParams(has_side_effects=_EFFECT),
    )(*[pltpu.with_memory_space_constraint(a, pltpu.HBM) for a in list(srcs) + lands], *extra)
    return dict(sems=outs[:2 * n], srcs=outs[2 * n:3 * n], lands=outs[3 * n:4 * n], rows=rows), outs[-1]


def early_exchange_wait(handle, after, name):
    n = len(handle["srcs"])
    rows = handle["rows"]

    def body(*refs):
        src_refs, land_refs = refs[:n], refs[n:2 * n]
        send_sems, recv_sems = refs[2 * n:3 * n], refs[3 * n:4 * n]
        x, y, c = _mesh_pos()
        for a in range(n):
            seven = pl.ds(0, 7 * rows[a])
            all_seven = pltpu.make_async_remote_copy(
                src_ref=land_refs[a].at[seven, :], dst_ref=land_refs[a].at[seven, :],
                send_sem=send_sems[a], recv_sem=recv_sems[a],
                device_id=(x, y, c), device_id_type=pl.DeviceIdType.MESH)
            all_seven.wait_send()
            all_seven.wait_recv()

    hbm = lambda a: pltpu.HBM(a.shape, a.dtype)
    ins = list(handle["srcs"]) + list(handle["lands"])
    outs = pl.pallas_call(
        body, name=name,
        out_shape=[hbm(a) for a in ins],
        in_specs=[_HBM] * (2 * n) + [_SEM] * (2 * n) + [pl.BlockSpec(memory_space=pl.ANY)],
        out_specs=[_HBM] * (2 * n),
        input_output_aliases={i: i for i in range(2 * n)},
        compiler_params=pltpu.CompilerParams(has_side_effects=_EFFECT),
    )(*ins, *handle["sems"], after)
    return outs[:n], outs[n:]


def slot_sum8(src, land, tr, name):
    rows, cols = land.shape[0] // N_DEV, land.shape[1]
    x, y, c = _mesh_pos()
    me = (4 * x + 2 * y + c).astype(jnp.int32).reshape(1)

    def body(me_ref, src_ref, land_ref, o_ref):
        acc = None
        for d in range(N_DEV):
            term = jnp.where(d == me_ref[0], src_ref[0], land_ref[d]).astype(F32)
            acc = term if acc is None else acc + term
        o_ref[...] = acc

    return pl.pallas_call(
        body, name=name,
        grid_spec=pltpu.PrefetchScalarGridSpec(
            num_scalar_prefetch=1, grid=(rows // tr,),
            in_specs=[pl.BlockSpec((1, tr, cols), lambda i, w: (w[0], i, 0)),
                      pl.BlockSpec((N_DEV, tr, cols), lambda i, w: (0, i, 0))],
            out_specs=pl.BlockSpec((tr, cols), lambda i, w: (i, 0))),
        out_shape=jax.ShapeDtypeStruct((rows, cols), F32),
        compiler_params=_params(("arbitrary",)),
    )(me, src.reshape(N_DEV, rows, cols), land.reshape(N_DEV, rows, cols))


def mm_tn_multi(a, bs, tt, name, out_dtype=F32):
    T, K = a.shape
    widths = [b.shape[1] for b in bs]
    steps = T // tt

    def body(a_ref, *rest):
        b_refs, o_ref, acc = rest[:-2], rest[-2], rest[-1]
        av = a_ref[...]
        parts = [lax.dot_general(av, b_ref[...], (((0,), (0,)), ((), ())), preferred_element_type=F32)
                 for b_ref in b_refs]

        @pl.when(pl.program_id(0) == 0)
        def _():
            col = 0
            for part, w in zip(parts, widths):
                acc[:, col:col + w] = part
                col += w

        @pl.when(pl.program_id(0) != 0)
        def _():
            col = 0
            for part, w in zip(parts, widths):
                acc[:, col:col + w] += part
                col += w

        @pl.when(pl.program_id(0) == steps - 1)
        def _():
            o_ref[...] = acc[...].astype(out_dtype)

    return pl.pallas_call(
        body, name=name, grid=(steps,),
        in_specs=[pl.BlockSpec((tt, K), lambda t: (t, 0))] + [pl.BlockSpec((tt, w), lambda t: (t, 0)) for w in widths],
        out_specs=pl.BlockSpec((K, sum(widths)), lambda t: (0, 0)),
        out_shape=jax.ShapeDtypeStruct((K, sum(widths)), out_dtype),
        scratch_shapes=[pltpu.VMEM((K, sum(widths)), F32)],
        compiler_params=_params(("arbitrary",)),
    )(a, *bs)


def rms_fwd(x, g, tm, name):
    M, K = x.shape

    def body(x_ref, g_ref, o_ref):
        xv = x_ref[...]
        r = lax.rsqrt(jnp.mean(xv * xv, axis=-1, keepdims=True) + RMS_EPS)
        o_ref[...] = ((xv * r) * g_ref[...]).astype(BF16)

    return pl.pallas_call(
        body, name=name, grid=(M // tm,),
        in_specs=[pl.BlockSpec((tm, K), lambda i: (i, 0)), pl.BlockSpec((1, K), lambda i: (0, 0))],
        out_specs=pl.BlockSpec((tm, K), lambda i: (i, 0)),
        out_shape=jax.ShapeDtypeStruct((M, K), BF16),
        compiler_params=_params(("arbitrary",)),
    )(x, g)


def rms_bwd(x, g, dh, dres, tm, name):
    M, K = x.shape
    has_res = dres is not None

    def body(*refs):
        if has_res:
            x_ref, g_ref, dh_ref, dres_ref, dx_ref, dg_ref = refs
        else:
            x_ref, g_ref, dh_ref, dx_ref, dg_ref = refs
        xv = x_ref[...]
        r = lax.rsqrt(jnp.mean(xv * xv, axis=-1, keepdims=True) + RMS_EPS)
        xn = xv * r
        dhv = dh_ref[...]
        dxn = dhv * g_ref[...]
        dx = r * (dxn - xn * jnp.mean(dxn * xn, axis=-1, keepdims=True))
        if has_res:
            dx = dx + dres_ref[...]
        dx_ref[...] = dx
        part = jnp.sum(dhv * xn, axis=0, keepdims=True)
        row = lax.broadcasted_iota(jnp.int32, (8, K), 0)
        upd = jnp.where(row == 0, part, 0.0)

        @pl.when(pl.program_id(0) == 0)
        def _():
            dg_ref[...] = upd

        @pl.when(pl.program_id(0) != 0)
        def _():
            dg_ref[...] += upd

    row_spec = pl.BlockSpec((tm, K), lambda i: (i, 0))
    ins = [x, g, dh] + ([dres] if has_res else [])
    in_specs = [row_spec, pl.BlockSpec((1, K), lambda i: (0, 0)), row_spec] + ([row_spec] if has_res else [])
    return pl.pallas_call(
        body, name=name, grid=(M // tm,),
        in_specs=in_specs,
        out_specs=[row_spec, pl.BlockSpec((8, K), lambda i: (0, 0))],
        out_shape=[jax.ShapeDtypeStruct((M, K), F32), jax.ShapeDtypeStruct((8, K), F32)],
        compiler_params=_params(("arbitrary",)),
    )(*ins)


def mm_nn(a, b, tm, tn, name):
    M, K = a.shape
    N = b.shape[1]

    def body(a_ref, b_ref, o_ref):
        o_ref[...] = jnp.dot(a_ref[...], b_ref[...], preferred_element_type=F32)

    return pl.pallas_call(
        body, name=name, grid=(N // tn, M // tm),
        in_specs=[pl.BlockSpec((tm, K), lambda j, i: (i, 0)), pl.BlockSpec((K, tn), lambda j, i: (0, j))],
        out_specs=pl.BlockSpec((tm, tn), lambda j, i: (i, j)),
        out_shape=jax.ShapeDtypeStruct((M, N), F32),
        compiler_params=_params(("arbitrary", "arbitrary")),
    )(a, b)


def mm_nt(a, b, tm, tk, name):
    M, K = a.shape
    N = b.shape[0]

    def body(a_ref, b_ref, o_ref):
        part = lax.dot_general(a_ref[...], b_ref[...], (((1,), (1,)), ((), ())), preferred_element_type=F32)

        @pl.when(pl.program_id(1) == 0)
        def _():
            o_ref[...] = part

        @pl.when(pl.program_id(1) != 0)
        def _():
            o_ref[...] += part

    return pl.pallas_call(
        body, name=name, grid=(M // tm, K // tk),
        in_specs=[pl.BlockSpec((tm, tk), lambda i, k: (i, k)), pl.BlockSpec((N, tk), lambda i, k: (0, k))],
        out_specs=pl.BlockSpec((tm, N), lambda i, k: (i, 0)),
        out_shape=jax.ShapeDtypeStruct((M, N), F32),
        compiler_params=_params(("arbitrary", "arbitrary")),
    )(a, b)


def mm_tn(a, b, tt, tn, name):
    T, K = a.shape
    N = b.shape[1]

    def body(a_ref, b_ref, o_ref):
        part = lax.dot_general(a_ref[...], b_ref[...], (((0,), (0,)), ((), ())), preferred_element_type=F32)

        @pl.when(pl.program_id(1) == 0)
        def _():
            o_ref[...] = part

        @pl.when(pl.program_id(1) != 0)
        def _():
            o_ref[...] += part

    return pl.pallas_call(
        body, name=name, grid=(N // tn, T // tt),
        in_specs=[pl.BlockSpec((tt, K), lambda j, t: (t, 0)), pl.BlockSpec((tt, tn), lambda j, t: (t, j))],
        out_specs=pl.BlockSpec((K, tn), lambda j, t: (0, j)),
        out_shape=jax.ShapeDtypeStruct((K, N), F32),
        compiler_params=_params(("arbitrary", "arbitrary")),
    )(a, b)


def _old_mm_tn_multi(a, bs, tt, name):
    T, K = a.shape
    widths = [b.shape[1] for b in bs]

    def body(a_ref, *rest):
        b_refs, o_ref = rest[:-1], rest[-1]
        av = a_ref[...]
        parts = [lax.dot_general(av, b_ref[...], (((0,), (0,)), ((), ())), preferred_element_type=F32)
                 for b_ref in b_refs]

        @pl.when(pl.program_id(0) == 0)
        def _():
            col = 0
            for part, w in zip(parts, widths):
                o_ref[:, col:col + w] = part
                col += w

        @pl.when(pl.program_id(0) != 0)
        def _():
            col = 0
            for part, w in zip(parts, widths):
                o_ref[:, col:col + w] += part
                col += w

    return pl.pallas_call(
        body, name=name, grid=(T // tt,),
        in_specs=[pl.BlockSpec((tt, K), lambda t: (t, 0))] + [pl.BlockSpec((tt, w), lambda t: (t, 0)) for w in widths],
        out_specs=pl.BlockSpec((K, sum(widths)), lambda t: (0, 0)),
        out_shape=jax.ShapeDtypeStruct((K, sum(widths)), F32),
        compiler_params=_params(("arbitrary",)),
    )(a, *bs)


def mm_nt_multi(pieces, w, tm, name):
    M = pieces[0][0].shape[0]
    N, K = w.shape

    def body(*refs):
        p_refs, w_ref, o_ref = refs[:-2], refs[-2], refs[-1]
        acc = None
        for p_ref, (arr, col) in zip(p_refs, pieces):
            part = lax.dot_general(p_ref[...], w_ref[:, col:col + arr.shape[1]], (((1,), (1,)), ((), ())),
                                   preferred_element_type=F32)
            acc = part if acc is None else acc + part
        o_ref[...] = acc

    return pl.pallas_call(
        body, name=name, grid=(M // tm,),
        in_specs=[pl.BlockSpec((tm, arr.shape[1]), lambda i: (i, 0)) for arr, _ in pieces]
        + [pl.BlockSpec((N, K), lambda i: (0, 0))],
        out_specs=pl.BlockSpec((tm, N), lambda i: (i, 0)),
        out_shape=jax.ShapeDtypeStruct((M, N), F32),
        compiler_params=_params(("arbitrary",)),
    )(*[arr for arr, _ in pieces], w)


def in_proj_bwd_rms(pieces, w, x, g, dres, tm):
    M, N = x.shape

    def body(*refs):
        n = len(pieces)
        p_refs, w_ref, x_ref, g_ref, dres_ref, dx_ref, dg_ref = refs[:n], *refs[n:]
        dh = None
        for p_ref, (arr, col) in zip(p_refs, pieces):
            part = lax.dot_general(p_ref[...], w_ref[:, col:col + arr.shape[1]], (((1,), (1,)), ((), ())),
                                   preferred_element_type=F32)
            dh = part if dh is None else dh + part
        xv = x_ref[...]
        r = lax.rsqrt(jnp.mean(xv * xv, axis=-1, keepdims=True) + RMS_EPS)
        xn = xv * r
        dxn = dh * g_ref[...]
        dx_ref[...] = r * (dxn - xn * jnp.mean(dxn * xn, axis=-1, keepdims=True)) + dres_ref[...]
        row = lax.broadcasted_iota(jnp.int32, (8, N), 0)
        upd = jnp.where(row == 0, jnp.sum(dh * xn, axis=0, keepdims=True), 0.0)

        @pl.when(pl.program_id(0) == 0)
        def _():
            dg_ref[...] = upd

        @pl.when(pl.program_id(0) != 0)
        def _():
            dg_ref[...] += upd

    row_spec = pl.BlockSpec((tm, N), lambda i: (i, 0))
    return pl.pallas_call(
        body, name="in_proj_bwd", grid=(M // tm,),
        in_specs=[pl.BlockSpec((tm, arr.shape[1]), lambda i: (i, 0)) for arr, _ in pieces]
        + [pl.BlockSpec(w.shape, lambda i: (0, 0)), row_spec, pl.BlockSpec((1, N), lambda i: (0, 0)), row_spec],
        out_specs=[row_spec, pl.BlockSpec((8, N), lambda i: (0, 0))],
        out_shape=[jax.ShapeDtypeStruct((M, N), F32), jax.ShapeDtypeStruct((8, N), F32)],
        compiler_params=_params(("arbitrary",)),
    )(*[arr for arr, _ in pieces], w, x, g, dres)


def _log_sigmoid(z):
    return jnp.minimum(z, 0.0) - jnp.log(1.0 + jnp.exp(-jnp.abs(z)))


def _tri(n, lower):
    r = lax.broadcasted_iota(jnp.int32, (n, n), 0)
    c = lax.broadcasted_iota(jnp.int32, (n, n), 1)
    return jnp.where((r >= c) if lower else (r <= c), 1.0, 0.0).astype(F32)


def fox_gate(proj3, b_pad):
    B, S, _ = proj3.shape
    nblk = S // TK

    def body(f_ref, b_ref, o_ref):
        tri = _tri(TK, True)
        carry = jnp.zeros((1, LANES), F32)
        for n in range(nblk):
            z = f_ref[0, n * TK:(n + 1) * TK, :] + b_ref[...]
            logf = _log_sigmoid(z)
            cs = jnp.dot(tri, logf, preferred_element_type=F32, precision=lax.Precision.HIGHEST) + carry
            carry = cs[TK - 1:TK, :]
            o_ref[0, n * TK:(n + 1) * TK, :] = -cs

    return pl.pallas_call(
        body, name="fox_gate", grid=(B,),
        in_specs=[pl.BlockSpec((1, S, LANES), lambda b: (b, 0, P_FLOG // LANES)),
                  pl.BlockSpec((1, LANES), lambda b: (0, 0))],
        out_specs=pl.BlockSpec((1, S, LANES), lambda b: (b, 0, 0)),
        out_shape=jax.ShapeDtypeStruct((B, S, LANES), F32),
        compiler_params=_params(("arbitrary",)),
    )(proj3, b_pad)


def fox_gate_bwd(drow, dneg, proj3, b_pad):
    B, S, _ = proj3.shape
    nblk = S // TK

    def body(d_ref, r_ref, f_ref, b_ref, o_ref, db_ref):
        tri = _tri(TK, False)
        lane = lax.broadcasted_iota(jnp.int32, (TK, LANES), 1)
        carry = jnp.zeros((1, LANES), F32)
        dbsum = jnp.zeros((1, LANES), F32)
        for n in reversed(range(nblk)):
            dk_side = None
            for hp in range(FOX_HEADS // 2):
                two = jnp.where(lane < 2, r_ref[0, n * TK:(n + 1) * TK, hp * LANES:(hp + 1) * LANES], 0.0)
                two = pltpu.roll(two, 2 * hp, 1) if hp else two
                dk_side = two if dk_side is None else dk_side + two
            dc = jnp.where(lane < FOX_HEADS, d_ref[0, :, n * TK:(n + 1) * TK].T - dk_side, 0.0)
            rs = jnp.dot(tri, dc, preferred_element_type=F32, precision=lax.Precision.HIGHEST) + carry
            carry = rs[0:1, :]
            z = f_ref[0, n * TK:(n + 1) * TK, :] + b_ref[...]
            dz = rs * (1.0 / (1.0 + jnp.exp(z)))
            o_ref[0, n * TK:(n + 1) * TK, :] = dz.astype(BF16)
            dbsum = dbsum + jnp.sum(dz, axis=0, keepdims=True)
        row = lax.broadcasted_iota(jnp.int32, (8, LANES), 0)
        upd = jnp.where(row == 0, dbsum, 0.0)

        @pl.when(pl.program_id(0) == 0)
        def _():
            db_ref[...] = upd

        @pl.when(pl.program_id(0) != 0)
        def _():
            db_ref[...] += upd

    return pl.pallas_call(
        body, name="fox_gate_bwd", grid=(B,),
        in_specs=[pl.BlockSpec((1, LANES, S), lambda b: (b, 0, 0)),
                  pl.BlockSpec((1, S, FOX_W), lambda b: (b, 0, 0)),
                  pl.BlockSpec((1, S, LANES), lambda b: (b, 0, P_FLOG // LANES)),
                  pl.BlockSpec((1, LANES), lambda b: (0, 0))],
        out_specs=[pl.BlockSpec((1, S, LANES), lambda b: (b, 0, 0)), pl.BlockSpec((8, LANES), lambda b: (0, 0))],
        out_shape=[jax.ShapeDtypeStruct((B, S, LANES), BF16), jax.ShapeDtypeStruct((8, LANES), F32)],
        compiler_params=_params(("arbitrary",)),
    )(drow, dneg, proj3, b_pad)


def _mult_masks(S, kind):
    nd = S // TQ
    a = np.arange(TQ)[:, None]
    b = np.arange(TK)[None, :]
    out = np.zeros((nd, TQ, TK), np.float32)
    for d in range(nd):
        delta = d * TQ + a - b
        if kind == "causal":
            out[d] = delta >= 0
        else:
            m = np.zeros((TQ, TK), np.float32)
            for w, dil in DILATIONS:
                m += (delta >= 0) & (delta % dil == 0) & (delta <= w)
            out[d] = m
    return jnp.asarray(out)


def _rope_tables(S):
    half = ROPE_DIM // 2
    f32 = np.float32
    pos = np.arange(S, dtype=f32)
    inv_freq = f32(1.0) / np.power(f32(ROPE_THETA), np.arange(0, ROPE_DIM, 2, dtype=f32) / f32(ROPE_DIM)).astype(f32)
    ang = (pos[:, None] * inv_freq[None, :]).astype(f32).astype(np.float64)
    cos, sin = np.cos(ang).astype(f32), np.sin(ang).astype(f32)
    one = np.ones((S, HEAD_DIM - ROPE_DIM), f32)
    zero = np.zeros((S, HEAD_DIM - ROPE_DIM), f32)
    zh = np.zeros((S, half), f32)
    c = np.concatenate([cos, cos, one], axis=1)
    s1 = np.concatenate([-sin, zh, zero], axis=1)
    s2 = np.concatenate([zh, sin, zero], axis=1)
    return tuple(jnp.asarray(np.concatenate([t, t], axis=1)) for t in (c, s1, s2))


def _rope(t, c, s1, s2):
    return t * c + pltpu.roll(t, LANES - half_rope(), 1) * s1 + pltpu.roll(t, half_rope(), 1) * s2


def half_rope():
    return ROPE_DIM // 2


def _rope_bwd(d, c, s1, s2):
    return d * c + pltpu.roll(d * s1, half_rope(), 1) + pltpu.roll(d * s2, LANES - half_rope(), 1)


def _scale_parts(scale):
    m, _ = math.frexp(scale)
    return (scale, None) if m == 0.5 else (None, scale)


def attn_fwd(kind, src, S, *, negc=None, mask=None, rope=None, kv=None):
    B = src.shape[0]
    pair = kind != "mem"
    col0 = {"fox": P_FOX, "dil": P_DIL, "mem": P_MQ}[kind]
    n_blocks = FOX_HEADS // 2 if pair else MEM_HEADS
    e_dim = HEAD_DIM if pair else MEM_HEAD_DIM
    q_fold, s_scale = _scale_parts(1.0 / math.sqrt(e_dim))
    Sk = S if pair else MEM_LEN
    nh = 2 if pair else 1
    has_bias = negc is not None
    has_rope = rope is not None
    nq = S // TQ

    def body(*refs):
        refs = list(refs)
        if pair:
            qkv_ref = refs.pop(0)
        else:
            q_ref, k_ref, v_ref = refs.pop(0), refs.pop(0), refs.pop(0)
        negc_ref = refs.pop(0) if has_bias else None
        mask_ref = refs.pop(0) if pair else None
        rope_refs = [refs.pop(0) for _ in range(3)] if has_rope else None
        o_ref, lse_ref, qs, ks, vs = refs
        lane = lax.broadcasted_iota(jnp.int32, (1, LANES), 1)

        def prep_q(n, _):
            r0 = pl.multiple_of(n * TQ, TQ)
            rows = pl.ds(r0, TQ)
            q = qkv_ref[0, rows, 0:LANES] if pair else q_ref[0, rows, :]
            if has_rope:
                q = _rope(q, *[t[rows, :] for t in rope_refs])
            if q_fold is not None:
                q = q * q_fold
            qs[rows, :] = q.astype(BF16)
            return 0

        def prep_kv(n, _):
            r0 = pl.multiple_of(n * TK, TK)
            rows = pl.ds(r0, TK)
            k = qkv_ref[0, rows, LANES:2 * LANES] if pair else k_ref[0, rows, :]
            v = qkv_ref[0, rows, 2 * LANES:3 * LANES] if pair else v_ref[0, rows, :]
            if has_rope:
                k = _rope(k, *[t[rows, :] for t in rope_refs])
            ks[rows, :] = k.astype(BF16)
            vs[rows, :] = v.astype(BF16)
            return 0

        lax.fori_loop(0, nq, prep_q, 0)
        lax.fori_loop(0, Sk // TK, prep_kv, 0)

        def q_loop(i, _):
            r0 = pl.multiple_of(i * TQ, TQ)
            q = qs[pl.ds(r0, TQ), :]
            res = []
            for hh in range(nh):
                hmask = (lane >= HEAD_DIM * hh) & (lane < HEAD_DIM * (hh + 1))
                qh = jnp.where(hmask, q, jnp.zeros_like(q)) if pair else q

                def kv_loop(j, carry, qh=qh, hh=hh):
                    m, l, acc = carry
                    c0 = pl.multiple_of(j * TK, TK)
                    k = ks[pl.ds(c0, TK), :]
                    v = vs[pl.ds(c0, TK), :]
                    s = lax.dot_general(qh, k, (((1,), (1,)), ((), ())), preferred_element_type=F32)
                    if s_scale is not None:
                        s = s * s_scale
                    if has_bias:
                        s = s + negc_ref[0, 0, pl.ds(hh, 1), pl.ds(c0, TK)]
                    if pair:
                        mult = mask_ref[i - j]
                        s = jnp.where(mult > 0.0, s, NEG_INF)
                    m_new = jnp.maximum(m, jnp.max(s, axis=1, keepdims=True))
                    p = jnp.exp(s - m_new)
                    if pair:
                        p = p * mult
                    alpha = jnp.exp(m - m_new)
                    l = alpha * l + jnp.sum(p, axis=1, keepdims=True)
                    acc = acc * alpha + jnp.dot(p.astype(BF16), v, preferred_element_type=F32)
                    return m_new, l, acc

                init = (jnp.full((TQ, 1), NEG_INF, F32), jnp.zeros((TQ, 1), F32), jnp.zeros((TQ, LANES), F32))
                m, l, acc = lax.fori_loop(0, (i + 1) if pair else Sk // TK, kv_loop, init)
                res.append((acc / l, m + jnp.log(l)))
            if pair:
                o = jnp.where(lane < HEAD_DIM, res[0][0], res[1][0])
                lse = jnp.where(lane < HEAD_DIM, res[0][1], res[1][1])
            else:
                o = res[0][0]
                lse = jnp.broadcast_to(res[0][1], (TQ, LANES))
            o_ref[0, pl.ds(r0, TQ), :] = o
            lse_ref[0, pl.ds(r0, TQ), :] = lse
            return 0

        lax.fori_loop(0, nq, q_loop, 0)

    ins, in_specs = [], []
    if pair:
        ins.append(src)
        in_specs.append(pl.BlockSpec((1, S, PAIR_W), lambda b, h: (b, 0, col0 // PAIR_W + h)))
    else:
        ins += [src, kv, kv]
        in_specs += [pl.BlockSpec((1, S, LANES), lambda b, h: (b, 0, col0 // LANES + h)),
                     pl.BlockSpec((1, MEM_LEN, LANES), lambda b, h: (b, 0, h)),
                     pl.BlockSpec((1, MEM_LEN, LANES), lambda b, h: (b, 0, MEM_HEADS + h))]
    if has_bias:
        ins.append(negc)
        in_specs.append(pl.BlockSpec((1, 1, 2, S), lambda b, h: (b, h, 0, 0)))
    if pair:
        ins.append(mask)
        in_specs.append(pl.BlockSpec(mask.shape, lambda b, h: (0, 0, 0)))
    if has_rope:
        ins += list(rope)
        in_specs += [pl.BlockSpec((S, LANES), lambda b, h: (0, 0))] * 3
    W = n_blocks * LANES
    out_spec = pl.BlockSpec((1, S, LANES), lambda b, h: (b, 0, h))
    return pl.pallas_call(
        body, name=kind + "_attn_fwd", grid=(B, n_blocks),
        in_specs=in_specs, out_specs=[out_spec, out_spec],
        out_shape=[jax.ShapeDtypeStruct((B, S, W), F32)] * 2,
        scratch_shapes=[pltpu.VMEM((S, LANES), BF16), pltpu.VMEM((Sk, LANES), BF16), pltpu.VMEM((Sk, LANES), BF16)],
        compiler_params=_params(("arbitrary", "arbitrary")),
    )(*ins)


def attn_bwd(kind, src, do, o, lse, S, *, negc=None, mask=None, rope=None, kv=None):
    B = src.shape[0]
    pair = kind != "mem"
    col0 = {"fox": P_FOX, "dil": P_DIL, "mem": P_MQ}[kind]
    n_blocks = FOX_HEADS // 2 if pair else MEM_HEADS
    e_dim = HEAD_DIM if pair else MEM_HEAD_DIM
    scale = 1.0 / math.sqrt(e_dim)
    q_fold, s_scale = _scale_parts(scale)
    Sk = S if pair else MEM_LEN
    nh = 2 if pair else 1
    has_bias = negc is not None
    has_rope = rope is not None
    nq = S // TQ
    nk = Sk // TK

    def body(*refs):
        refs = list(refs)
        if pair:
            qkv_ref = refs.pop(0)
        else:
            q_ref, k_ref, v_ref = refs.pop(0), refs.pop(0), refs.pop(0)
        do_ref, o_ref, lse_ref = refs.pop(0), refs.pop(0), refs.pop(0)
        negc_ref = refs.pop(0) if has_bias else None
        mask_ref = refs.pop(0) if pair else None
        rope_refs = [refs.pop(0) for _ in range(3)] if has_rope else None
        if pair:
            dqkv_ref = refs.pop(0)
            dnegc_ref = refs.pop(0) if has_bias else None
            drow_ref = refs.pop(0) if has_bias else None
        else:
            dq_ref, dk_ref, dv_ref = refs.pop(0), refs.pop(0), refs.pop(0)
        qs, ks, vs, dos, delta_s, dq_acc = refs[:6]
        drow_acc = refs[6] if has_bias else None
        lane = lax.broadcasted_iota(jnp.int32, (1, LANES), 1)

        def prep_q(n, _):
            r0 = pl.multiple_of(n * TQ, TQ)
            rows = pl.ds(r0, TQ)
            q = qkv_ref[0, rows, 0:LANES] if pair else q_ref[0, rows, :]
            if has_rope:
                q = _rope(q, *[t[rows, :] for t in rope_refs])
            if q_fold is not None:
                q = q * q_fold
            qs[rows, :] = q.astype(BF16)
            dov = do_ref[0, rows, :]
            dob = dov.astype(BF16)
            dos[rows, :] = dob
            prod = dob.astype(F32) * o_ref[0, rows, :]
            if pair:
                d0 = jnp.sum(jnp.where(lane < HEAD_DIM, prod, 0.0), axis=1, keepdims=True)
                d1 = jnp.sum(jnp.where(lane < HEAD_DIM, 0.0, prod), axis=1, keepdims=True)
                delta_s[rows, :] = jnp.where(lane < HEAD_DIM, d0, d1)
            else:
                delta_s[rows, :] = jnp.broadcast_to(jnp.sum(prod, axis=1, keepdims=True), (TQ, LANES))
            dq_acc[rows, :] = jnp.zeros((TQ, LANES), F32)
            if has_bias:
                drow_acc[rows, :] = jnp.zeros((TQ, LANES), F32)
            return 0

        def prep_kv(n, _):
            r0 = pl.multiple_of(n * TK, TK)
            rows = pl.ds(r0, TK)
            k = qkv_ref[0, rows, LANES:2 * LANES] if pair else k_ref[0, rows, :]
            v = qkv_ref[0, rows, 2 * LANES:3 * LANES] if pair else v_ref[0, rows, :]
            if has_rope:
                k = _rope(k, *[t[rows, :] for t in rope_refs])
            ks[rows, :] = k.astype(BF16)
            vs[rows, :] = v.astype(BF16)
            return 0

        lax.fori_loop(0, nq, prep_q, 0)
        lax.fori_loop(0, nk, prep_kv, 0)

        def kv_loop(j, _):
            c0 = pl.multiple_of(j * TK, TK)
            kt = ks[pl.ds(c0, TK), :]
            vt = vs[pl.ds(c0, TK), :]
            res = []
            for hh in range(nh):
                hmask = (lane >= HEAD_DIM * hh) & (lane < HEAD_DIM * (hh + 1))
                kh = jnp.where(hmask, kt, jnp.zeros_like(kt)) if pair else kt
                vh = jnp.where(hmask, vt, jnp.zeros_like(vt)) if pair else vt

                def q_loop(i, carry, kh=kh, vh=vh, hh=hh, hmask=hmask):
                    dk, dv, dneg = carry
                    r0 = pl.multiple_of(i * TQ, TQ)
                    rows = pl.ds(r0, TQ)
                    q = qs[rows, :]
                    dot = dos[rows, :]
                    lse_i = lse_ref[0, rows, hh * HEAD_DIM:hh * HEAD_DIM + 1]
                    delta_i = delta_s[rows, hh * HEAD_DIM:hh * HEAD_DIM + 1]
                    s = lax.dot_general(q, kh, (((1,), (1,)), ((), ())), preferred_element_type=F32)
                    if s_scale is not None:
                        s = s * s_scale
                    if has_bias:
                        s = s + negc_ref[0, 0, pl.ds(hh, 1), pl.ds(c0, TK)]
                    if pair:
                        mult = mask_ref[i - j]
                        s = jnp.where(mult > 0.0, s, NEG_INF)
                    p = jnp.exp(s - lse_i)
                    if pair:
                        p = p * mult
                    dv = dv + lax.dot_general(p.astype(BF16), dot, (((0,), (0,)), ((), ())),
                                              preferred_element_type=F32)
                    dp = lax.dot_general(dot, vh, (((1,), (1,)), ((), ())), preferred_element_type=F32)
                    ds = p * (dp - delta_i)
                    if has_bias:
                        dneg = dneg + jnp.sum(ds, axis=0, keepdims=True)
                        drow_acc[rows, :] += jnp.where(hmask, jnp.sum(ds, axis=1, keepdims=True), 0.0)
                    if s_scale is not None:
                        ds = ds * s_scale
                    dsb = ds.astype(BF16)
                    dk = dk + lax.dot_general(dsb, q, (((0,), (0,)), ((), ())), preferred_element_type=F32)
                    dq = jnp.dot(dsb, kh, preferred_element_type=F32)
                    dq_acc[rows, :] += dq
                    return dk, dv, dneg

                init = (jnp.zeros((TK, LANES), F32), jnp.zeros((TK, LANES), F32), jnp.zeros((1, TK), F32))
                dk, dv, dneg = lax.fori_loop(j if pair else 0, nq, q_loop, init)
                if has_bias:
                    dnegc_ref[0, 0, pl.ds(hh, 1), pl.ds(c0, TK)] = dneg
                res.append((dk, dv))
            if pair:
                dk = jnp.where(lane < HEAD_DIM, res[0][0], res[1][0])
                dv = jnp.where(lane < HEAD_DIM, res[0][1], res[1][1])
                if has_rope:
                    dk = _rope_bwd(dk, *[t[pl.ds(c0, TK), :] for t in rope_refs])
                dqkv_ref[0, pl.ds(c0, TK), LANES:2 * LANES] = dk.astype(BF16)
                dqkv_ref[0, pl.ds(c0, TK), 2 * LANES:3 * LANES] = dv.astype(BF16)
            else:
                dk_ref[0, pl.ds(c0, TK), :] = res[0][0].astype(BF16)
                dv_ref[0, pl.ds(c0, TK), :] = res[0][1].astype(BF16)
            return 0

        lax.fori_loop(0, nk, kv_loop, 0)

        def fin_q(n, _):
            r0 = pl.multiple_of(n * TQ, TQ)
            rows = pl.ds(r0, TQ)
            dq = dq_acc[rows, :]
            if q_fold is not None:
                dq = dq * q_fold
            if has_rope:
                dq = _rope_bwd(dq, *[t[rows, :] for t in rope_refs])
            if pair:
                dqkv_ref[0, rows, 0:LANES] = dq.astype(BF16)
            else:
                dq_ref[0, rows, :] = dq.astype(BF16)
            if has_bias:
                drow_ref[0, rows, :] = drow_acc[rows, :]
            return 0

        lax.fori_loop(0, nq, fin_q, 0)

    ins, in_specs = [], []
    if pair:
        ins.append(src)
        in_specs.append(pl.BlockSpec((1, S, PAIR_W), lambda b, h: (b, 0, col0 // PAIR_W + h)))
    else:
        ins += [src, kv, kv]
        in_specs += [pl.BlockSpec((1, S, LANES), lambda b, h: (b, 0, col0 // LANES + h)),
                     pl.BlockSpec((1, MEM_LEN, LANES), lambda b, h: (b, 0, h)),
                     pl.BlockSpec((1, MEM_LEN, LANES), lambda b, h: (b, 0, MEM_HEADS + h))]
    row_spec = pl.BlockSpec((1, S, LANES), lambda b, h: (b, 0, h))
    ins += [do, o, lse]
    in_specs += [row_spec] * 3
    if has_bias:
        ins.append(negc)
        in_specs.append(pl.BlockSpec((1, 1, 2, S), lambda b, h: (b, h, 0, 0)))
    if pair:
        ins.append(mask)
        in_specs.append(pl.BlockSpec(mask.shape, lambda b, h: (0, 0, 0)))
    if has_rope:
        ins += list(rope)
        in_specs += [pl.BlockSpec((S, LANES), lambda b, h: (0, 0))] * 3
    W = n_blocks * LANES
    if pair:
        out_specs = [pl.BlockSpec((1, S, PAIR_W), lambda b, h: (b, 0, h))]
        out_shape = [jax.ShapeDtypeStruct((B, S, 3 * W), BF16)]
        if has_bias:
            out_specs.append(pl.BlockSpec((1, 1, 2, S), lambda b, h: (b, h, 0, 0)))
            out_shape.append(jax.ShapeDtypeStruct((B, LANES // 2, 2, S), F32))
            out_specs.append(row_spec)
            out_shape.append(jax.ShapeDtypeStruct((B, S, W), F32))
    else:
        kv_spec = pl.BlockSpec((1, MEM_LEN, LANES), lambda b, h: (b, 0, h))
        out_specs = [row_spec, kv_spec, kv_spec]
        out_shape = [jax.ShapeDtypeStruct((B, S, W), BF16)] + [jax.ShapeDtypeStruct((B, MEM_LEN, W), BF16)] * 2
    return pl.pallas_call(
        body, name=kind + "_attn_bwd", grid=(B, n_blocks),
        in_specs=in_specs, out_specs=out_specs, out_shape=out_shape,
        scratch_shapes=[pltpu.VMEM((S, LANES), BF16), pltpu.VMEM((Sk, LANES), BF16), pltpu.VMEM((Sk, LANES), BF16),
                        pltpu.VMEM((S, LANES), BF16), pltpu.VMEM((S, LANES), F32), pltpu.VMEM((S, LANES), F32)]
        + ([pltpu.VMEM((S, LANES), F32)] if has_bias else []),
        compiler_params=_params(("arbitrary", "arbitrary")),
    )(*ins)


def _log_masks(S, kind):
    nd = 1 if kind == "causal" else S // TQ
    a = np.arange(TQ)[:, None]
    b = np.arange(TK)[None, :]
    out = np.zeros((nd, TQ, TK), np.float32)
    for d in range(nd):
        delta = d * TQ + a - b
        if kind == "causal":
            m = (delta >= 0).astype(np.float64)
        else:
            m = sum(((delta >= 0) & (delta % dil == 0) & (delta <= w)).astype(np.float64) for w, dil in DILATIONS)
        out[d] = np.where(m > 0, np.log(np.maximum(m, 1.0)), NEG_INF)
    return jnp.asarray(out)


def _attn_setup(kind):
    pair = kind != "mem"
    e_dim = HEAD_DIM if pair else MEM_HEAD_DIM
    q_fold, s_scale = _scale_parts(1.0 / math.sqrt(e_dim))
    return dict(pair=pair, col0={"fox": P_FOX, "dil": P_DIL, "mem": P_MQ}[kind],
                n_blocks=FOX_HEADS // 2 if pair else MEM_HEADS, q_fold=q_fold, s_scale=s_scale,
                nh=2 if pair else 1)


def _attn_inputs(kind, src, S, negc, mask, rope, kv, extra):
    cfg = _attn_setup(kind)
    col0 = cfg["col0"]
    ins, in_specs = [], []
    if cfg["pair"]:
        ins.append(src)
        in_specs.append(pl.BlockSpec((1, S, PAIR_W), lambda b, h: (b, 0, col0 // PAIR_W + h)))
    else:
        ins += [src, kv, kv]
        in_specs += [pl.BlockSpec((1, S, LANES), lambda b, h: (b, 0, col0 // LANES + h)),
                     pl.BlockSpec((1, MEM_LEN, LANES), lambda b, h: (b, 0, h)),
                     pl.BlockSpec((1, MEM_LEN, LANES), lambda b, h: (b, 0, MEM_HEADS + h))]
    ins += list(extra)
    in_specs += [pl.BlockSpec((1, S, LANES), lambda b, h: (b, 0, h))] * len(extra)
    if negc is not None:
        ins.append(negc)
        in_specs.append(pl.BlockSpec((1, 1, 2, S), lambda b, h: (b, h, 0, 0)))
    if mask is not None:
        ins.append(mask)
        in_specs.append(pl.BlockSpec(mask.shape, lambda b, h: (0, 0, 0)))
    if rope is not None:
        ins += list(rope)
        in_specs += [pl.BlockSpec((S, LANES), lambda b, h: (0, 0))] * 3
    return ins, in_specs


def _prep_rows(cfg, rope_refs, lane, load_q, load_kv, qs2, ks, vs, S, Sk):
    nh = cfg["nh"]
    R = nh * TQ

    def prep_q(n, _):
        rows = pl.ds(pl.multiple_of(n * TQ, TQ), TQ)
        q = load_q(rows)
        if rope_refs is not None:
            q = _rope(q, *[t[rows, :] for t in rope_refs])
        if cfg["q_fold"] is not None:
            q = q * cfg["q_fold"]
        _store_stacked(cfg, lane, qs2, n, q.astype(BF16))
        return 0

    def prep_kv(n, _):
        rows = pl.ds(pl.multiple_of(n * TK, TK), TK)
        k, v = load_kv(rows)
        if rope_refs is not None:
            k = _rope(k, *[t[rows, :] for t in rope_refs])
        ks[rows, :] = k.astype(BF16)
        vs[rows, :] = v.astype(BF16)
        return 0

    lax.fori_loop(0, S // TQ, prep_q, 0)
    lax.fori_loop(0, Sk // TK, prep_kv, 0)


def _store_stacked(cfg, lane, dst, n, val):
    nh = cfg["nh"]
    R = nh * TQ
    if nh == 1:
        dst[pl.ds(pl.multiple_of(n * R, R), TQ), :] = val
        return
    for hh in range(nh):
        hmask = (lane >= HEAD_DIM * hh) & (lane < HEAD_DIM * (hh + 1))
        dst[pl.ds(pl.multiple_of(n * R + hh * TQ, TQ), TQ), :] = jnp.where(hmask, val, jnp.zeros_like(val))


def _cat(parts, axis):
    return parts[0] if len(parts) == 1 else jnp.concatenate(parts, axis=axis)


def attn_fwd2(kind, src, S, *, negc=None, mask=None, rope=None, kv=None):
    B = src.shape[0]
    cfg = _attn_setup(kind)
    pair, nh, s_scale = cfg["pair"], cfg["nh"], cfg["s_scale"]
    Sk = S if pair else MEM_LEN
    has_bias, has_rope = negc is not None, rope is not None
    R = nh * TQ

    def body(*refs):
        refs = list(refs)
        if pair:
            qkv_ref = refs.pop(0)
        else:
            q_ref, k_ref, v_ref = refs.pop(0), refs.pop(0), refs.pop(0)
        negc_ref = refs.pop(0) if has_bias else None
        mask_ref = refs.pop(0) if mask is not None else None
        rope_refs = [refs.pop(0) for _ in range(3)] if has_rope else None
        o_ref, lse_ref, qs2, ks, vs = refs
        lane = lax.broadcasted_iota(jnp.int32, (1, LANES), 1)

        if pair:
            load_q = lambda rows: qkv_ref[0, rows, 0:LANES]
            load_kv = lambda rows: (qkv_ref[0, rows, LANES:2 * LANES], qkv_ref[0, rows, 2 * LANES:3 * LANES])
        else:
            load_q = lambda rows: q_ref[0, rows, :]
            load_kv = lambda rows: (k_ref[0, rows, :], v_ref[0, rows, :])
        _prep_rows(cfg, rope_refs, lane, load_q, load_kv, qs2, ks, vs, S, Sk)

        def q_loop(i, _):
            q2 = qs2[pl.ds(pl.multiple_of(i * R, R), R), :]

            def step(j, carry, midx):
                ms, ls, acc = carry
                c0 = pl.multiple_of(j * TK, TK)
                k = ks[pl.ds(c0, TK), :]
                v = vs[pl.ds(c0, TK), :]
                s2 = lax.dot_general(q2, k, (((1,), (1,)), ((), ())), preferred_element_type=F32)
                if s_scale is not None:
                    s2 = s2 * s_scale
                new_m, new_l, ps, alphas = [], [], [], []
                for hh in range(nh):
                    s = s2[hh * TQ:(hh + 1) * TQ]
                    if has_bias:
                        s = s + negc_ref[0, 0, pl.ds(hh, 1), pl.ds(c0, TK)]
                    if midx is not None:
                        s = s + mask_ref[midx]
                    m_new = jnp.maximum(ms[hh], jnp.max(s, axis=1, keepdims=True))
                    p = jnp.exp(s - m_new)
                    alpha = jnp.exp(ms[hh] - m_new)
                    new_l.append(alpha * ls[hh] + jnp.sum(p, axis=1, keepdims=True))
                    new_m.append(m_new)
                    ps.append(p.astype(BF16))
                    alphas.append(alpha)
                acc = acc * _cat(alphas, 0) + jnp.dot(_cat(ps, 0), v, preferred_element_type=F32)
                return tuple(new_m), tuple(new_l), acc

            init = (tuple(jnp.full((TQ, 1), NEG_INF, F32) for _ in range(nh)),
                    tuple(jnp.zeros((TQ, 1), F32) for _ in range(nh)), jnp.zeros((R, LANES), F32))
            if kind == "fox":
                carry = lax.fori_loop(0, i, lambda j, c: step(j, c, None), init)
                carry = step(i, carry, 0)
            elif kind == "dil":
                carry = lax.fori_loop(0, i + 1, lambda j, c: step(j, c, i - j), init)
            else:
                carry = lax.fori_loop(0, Sk // TK, lambda j, c: step(j, c, None), init)
            ms, ls, acc = carry
            outs = [acc[hh * TQ:(hh + 1) * TQ] / ls[hh] for hh in range(nh)]
            lses = [ms[hh] + jnp.log(ls[hh]) for hh in range(nh)]
            rows = pl.ds(pl.multiple_of(i * TQ, TQ), TQ)
            if pair:
                o_ref[0, rows, :] = jnp.where(lane < HEAD_DIM, outs[0], outs[1])
                lse_ref[0, rows, :] = jnp.where(lane < HEAD_DIM, lses[0], lses[1])
            else:
                o_ref[0, rows, :] = outs[0]
                lse_ref[0, rows, :] = jnp.broadcast_to(lses[0], (TQ, LANES))
            return 0

        lax.fori_loop(0, S // TQ, q_loop, 0)

    ins, in_specs = _attn_inputs(kind, src, S, negc, mask, rope, kv, ())
    W = cfg["n_blocks"] * LANES
    out_spec = pl.BlockSpec((1, S, LANES), lambda b, h: (b, 0, h))
    return pl.pallas_call(
        body, name=kind + "_attn_fwd", grid=(B, cfg["n_blocks"]),
        in_specs=in_specs, out_specs=[out_spec, out_spec],
        out_shape=[jax.ShapeDtypeStruct((B, S, W), F32)] * 2,
        scratch_shapes=[pltpu.VMEM((nh * S, LANES), BF16), pltpu.VMEM((Sk, LANES), BF16),
                        pltpu.VMEM((Sk, LANES), BF16)],
        compiler_params=_params(("arbitrary", "arbitrary")),
    )(*ins)


def attn_bwd2(kind, src, do, o, lse, S, *, negc=None, mask=None, rope=None, kv=None):
    B = src.shape[0]
    cfg = _attn_setup(kind)
    pair, nh, s_scale, q_fold = cfg["pair"], cfg["nh"], cfg["s_scale"], cfg["q_fold"]
    Sk = S if pair else MEM_LEN
    has_bias, has_rope = negc is not None, rope is not None
    R = nh * TQ
    nq, nk = S // TQ, Sk // TK

    def body(*refs):
        refs = list(refs)
        if pair:
            qkv_ref = refs.pop(0)
        else:
            q_ref, k_ref, v_ref = refs.pop(0), refs.pop(0), refs.pop(0)
        do_ref, o_ref, lse_ref = refs.pop(0), refs.pop(0), refs.pop(0)
        negc_ref = refs.pop(0) if has_bias else None
        mask_ref = refs.pop(0) if mask is not None else None
        rope_refs = [refs.pop(0) for _ in range(3)] if has_rope else None
        if pair:
            dqkv_ref = refs.pop(0)
            dnegc_ref = refs.pop(0) if has_bias else None
            drow_ref = refs.pop(0) if has_bias else None
        else:
            dq_ref, dk_ref, dv_ref = refs.pop(0), refs.pop(0), refs.pop(0)
        qs2, ks, vs, dos2, lse_s, delta_s, dk_acc, dv_acc = refs[:8]
        dneg_acc = refs[8] if has_bias else None
        lane = lax.broadcasted_iota(jnp.int32, (1, LANES), 1)

        if pair:
            load_q = lambda rows: qkv_ref[0, rows, 0:LANES]
            load_kv = lambda rows: (qkv_ref[0, rows, LANES:2 * LANES], qkv_ref[0, rows, 2 * LANES:3 * LANES])
        else:
            load_q = lambda rows: q_ref[0, rows, :]
            load_kv = lambda rows: (k_ref[0, rows, :], v_ref[0, rows, :])
        _prep_rows(cfg, rope_refs, lane, load_q, load_kv, qs2, ks, vs, S, Sk)

        def prep_do(n, _):
            rows = pl.ds(pl.multiple_of(n * TQ, TQ), TQ)
            dob = do_ref[0, rows, :].astype(BF16)
            _store_stacked(cfg, lane, dos2, n, dob)
            prod = dob.astype(F32) * o_ref[0, rows, :]
            lse_blk = lse_ref[0, rows, :]
            for hh in range(nh):
                dst = pl.ds(pl.multiple_of(n * R + hh * TQ, TQ), TQ)
                if pair:
                    hmask = (lane >= HEAD_DIM * hh) & (lane < HEAD_DIM * (hh + 1))
                    d = jnp.sum(jnp.where(hmask, prod, 0.0), axis=1, keepdims=True)
                    lse_s[dst, :] = jnp.broadcast_to(lse_blk[:, hh * HEAD_DIM:hh * HEAD_DIM + 1], (TQ, LANES))
                else:
                    d = jnp.sum(prod, axis=1, keepdims=True)
                    lse_s[dst, :] = lse_blk
                delta_s[dst, :] = jnp.broadcast_to(d, (TQ, LANES))
            return 0

        def zero_kv(n, _):
            rows = pl.ds(pl.multiple_of(n * TK, TK), TK)
            dk_acc[rows, :] = jnp.zeros((TK, LANES), F32)
            dv_acc[rows, :] = jnp.zeros((TK, LANES), F32)
            return 0

        lax.fori_loop(0, nq, prep_do, 0)
        lax.fori_loop(0, nk, zero_kv, 0)
        if has_bias:
            dneg_acc[...] = jnp.zeros(dneg_acc.shape, F32)

        def q_loop(i, _):
            rows2 = pl.ds(pl.multiple_of(i * R, R), R)
            q2 = qs2[rows2, :]
            do2 = dos2[rows2, :]
            lse2 = lse_s[rows2, :]
            delta2 = delta_s[rows2, :]
            wide = lambda t: jnp.concatenate([t] * (TK // LANES), axis=1)

            def step(j, carry, midx):
                dq2, drow = carry
                c0 = pl.multiple_of(j * TK, TK)
                kcols = pl.ds(c0, TK)
                k = ks[kcols, :]
                v = vs[kcols, :]
                s2 = lax.dot_general(q2, k, (((1,), (1,)), ((), ())), preferred_element_type=F32)
                if s_scale is not None:
                    s2 = s2 * s_scale
                if has_bias or midx is not None:
                    halves = []
                    for hh in range(nh):
                        s = s2[hh * TQ:(hh + 1) * TQ]
                        if has_bias:
                            s = s + negc_ref[0, 0, pl.ds(hh, 1), kcols]
                        if midx is not None:
                            s = s + mask_ref[midx]
                        halves.append(s)
                    s2 = _cat(halves, 0)
                p2 = jnp.exp(s2 - wide(lse2))
                dp2 = lax.dot_general(do2, v, (((1,), (1,)), ((), ())), preferred_element_type=F32)
                ds2 = p2 * (dp2 - wide(delta2))
                if has_bias:
                    drow = drow + jnp.sum(ds2, axis=1, keepdims=True)
                    for hh in range(nh):
                        dneg_acc[pl.ds(hh, 1), kcols] += jnp.sum(ds2[hh * TQ:(hh + 1) * TQ], axis=0, keepdims=True)
                if s_scale is not None:
                    ds2 = ds2 * s_scale
                dsb = ds2.astype(BF16)
                dv_acc[kcols, :] += lax.dot_general(p2.astype(BF16), do2, (((0,), (0,)), ((), ())),
                                                    preferred_element_type=F32)
                dk_acc[kcols, :] += lax.dot_general(dsb, q2, (((0,), (0,)), ((), ())), preferred_element_type=F32)
                dq2 = dq2 + jnp.dot(dsb, k, preferred_element_type=F32)
                return dq2, drow

            init = (jnp.zeros((R, LANES), F32), jnp.zeros((R, 1), F32))
            if kind == "fox":
                carry = lax.fori_loop(0, i, lambda j, c: step(j, c, None), init)
                carry = step(i, carry, 0)
            elif kind == "dil":
                carry = lax.fori_loop(0, i + 1, lambda j, c: step(j, c, i - j), init)
            else:
                carry = lax.fori_loop(0, nk, lambda j, c: step(j, c, None), init)
            dq2, drow = carry
            rows = pl.ds(pl.multiple_of(i * TQ, TQ), TQ)
            dq = jnp.where(lane < HEAD_DIM, dq2[0:TQ], dq2[TQ:2 * TQ]) if pair else dq2
            if q_fold is not None:
                dq = dq * q_fold
            if has_rope:
                dq = _rope_bwd(dq, *[t[rows, :] for t in rope_refs])
            if pair:
                dqkv_ref[0, rows, 0:LANES] = dq.astype(BF16)
            else:
                dq_ref[0, rows, :] = dq.astype(BF16)
            if has_bias:
                drow_ref[0, rows, :] = jnp.where(lane < HEAD_DIM, drow[0:TQ], drow[TQ:2 * TQ])
            return 0

        lax.fori_loop(0, nq, q_loop, 0)

        def fin_kv(n, _):
            rows = pl.ds(pl.multiple_of(n * TK, TK), TK)
            dk = dk_acc[rows, :]
            if has_rope:
                dk = _rope_bwd(dk, *[t[rows, :] for t in rope_refs])
            if pair:
                dqkv_ref[0, rows, LANES:2 * LANES] = dk.astype(BF16)
                dqkv_ref[0, rows, 2 * LANES:3 * LANES] = dv_acc[rows, :].astype(BF16)
            else:
                dk_ref[0, rows, :] = dk.astype(BF16)
                dv_ref[0, rows, :] = dv_acc[rows, :].astype(BF16)
            return 0

        lax.fori_loop(0, nk, fin_kv, 0)
        if has_bias:
            dnegc_ref[0, 0] = dneg_acc[...]

    ins, in_specs = _attn_inputs(kind, src, S, negc, mask, rope, kv, (do, o, lse))
    W = cfg["n_blocks"] * LANES
    row_spec = pl.BlockSpec((1, S, LANES), lambda b, h: (b, 0, h))
    if pair:
        out_specs = [pl.BlockSpec((1, S, PAIR_W), lambda b, h: (b, 0, h))]
        out_shape = [jax.ShapeDtypeStruct((B, S, 3 * W), BF16)]
        if has_bias:
            out_specs += [pl.BlockSpec((1, 1, 2, S), lambda b, h: (b, h, 0, 0)), row_spec]
            out_shape += [jax.ShapeDtypeStruct((B, LANES // 2, 2, S), F32), jax.ShapeDtypeStruct((B, S, W), F32)]
    else:
        kv_spec = pl.BlockSpec((1, MEM_LEN, LANES), lambda b, h: (b, 0, h))
        out_specs = [row_spec, kv_spec, kv_spec]
        out_shape = [jax.ShapeDtypeStruct((B, S, W), BF16)] + [jax.ShapeDtypeStruct((B, MEM_LEN, W), BF16)] * 2
    scratch = [pltpu.VMEM((nh * S, LANES), BF16), pltpu.VMEM((Sk, LANES), BF16), pltpu.VMEM((Sk, LANES), BF16),
               pltpu.VMEM((nh * S, LANES), BF16), pltpu.VMEM((nh * S, LANES), F32), pltpu.VMEM((nh * S, LANES), F32),
               pltpu.VMEM((Sk, LANES), F32), pltpu.VMEM((Sk, LANES), F32)]
    if has_bias:
        scratch.append(pltpu.VMEM((2, S), F32))
    return pl.pallas_call(
        body, name=kind + "_attn_bwd", grid=(B, cfg["n_blocks"]),
        in_specs=in_specs, out_specs=out_specs, out_shape=out_shape, scratch_shapes=scratch,
        compiler_params=_params(("arbitrary", "arbitrary")),
    )(*ins)


def _log_masks_t(S, kind):
    return jnp.swapaxes(_log_masks(S, kind), 1, 2)


def _head_rows(hh, pair):
    row = lax.broadcasted_iota(jnp.int32, (LANES, 1), 0)
    if not pair:
        return row >= 0
    return (row >= HEAD_DIM * hh) & (row < HEAD_DIM * (hh + 1))


def _attn_t_inputs(kind, src, S, negc_cols, mask, rope, kv):
    cfg = _attn_setup(kind)
    col0 = cfg["col0"]
    ins, in_specs = [], []
    if cfg["pair"]:
        ins.append(src)
        in_specs.append(pl.BlockSpec((1, S, PAIR_W), lambda b, h: (b, 0, col0 // PAIR_W + h)))
    else:
        ins += [src, kv, kv]
        in_specs += [pl.BlockSpec((1, S, LANES), lambda b, h: (b, 0, col0 // LANES + h)),
                     pl.BlockSpec((1, MEM_LEN, LANES), lambda b, h: (b, 0, h)),
                     pl.BlockSpec((1, MEM_LEN, LANES), lambda b, h: (b, 0, MEM_HEADS + h))]
    if negc_cols is not None:
        ins.append(negc_cols)
        in_specs.append(pl.BlockSpec((1, S, LANES), lambda b, h: (b, 0, 0)))
    if mask is not None:
        ins.append(mask)
        in_specs.append(pl.BlockSpec(mask.shape, lambda b, h: (0, 0, 0)))
    if rope is not None:
        ins += list(rope)
        in_specs += [pl.BlockSpec((S, LANES), lambda b, h: (0, 0))] * 3
    return ins, in_specs


def _attn_t_prep(cfg, refs, S, Sk, *, qT2s, ks, q2s=None, vs=None, vTs=None, kTs=None, nb=None):
    pair, nh = cfg["pair"], cfg["nh"]
    lane = lax.broadcasted_iota(jnp.int32, (1, LANES), 1)
    rope_refs = refs["rope"]

    def prep_q(n, _):
        rows = pl.ds(pl.multiple_of(n * TQ, TQ), TQ)
        q = refs["load_q"](rows)
        if rope_refs is not None:
            q = _rope(q, *[t[rows, :] for t in rope_refs])
        if cfg["q_fold"] is not None:
            q = q * cfg["q_fold"]
        qb = q.astype(BF16)
        if q2s is not None:
            _store_stacked(cfg, lane, q2s, n, qb)
        qtb = qb.T
        for hh in range(nh):
            qT2s[n, :, hh * TQ:(hh + 1) * TQ] = jnp.where(_head_rows(hh, pair), qtb, jnp.zeros_like(qtb))
        return 0

    def prep_kv(n, _):
        rows = pl.ds(pl.multiple_of(n * TK, TK), TK)
        k, v = refs["load_kv"](rows)
        if rope_refs is not None:
            k = _rope(k, *[t[rows, :] for t in rope_refs])
        kb = k.astype(BF16)
        vb = v.astype(BF16)
        ks[rows, :] = kb
        if vs is not None:
            vs[rows, :] = vb
        if vTs is not None:
            vTs[n] = vb.T
        if kTs is not None:
            kTs[n] = kb.T
        if nb is not None:
            blk = refs["negc"][0, rows, :]
            for hh in range(nh):
                h = 2 * refs["block"] + hh
                col = jnp.sum(jnp.where(lane == h, blk, 0.0), axis=1, keepdims=True)
                nb[hh, rows, :] = jnp.broadcast_to(col, (TK, LANES))
        return 0

    lax.fori_loop(0, S // TQ, prep_q, 0)
    lax.fori_loop(0, Sk // TK, prep_kv, 0)


def _raw_scores_t(cfg, k, qT2):
    sT = jnp.dot(k, qT2, preferred_element_type=F32)
    if cfg["s_scale"] is not None:
        sT = sT * cfg["s_scale"]
    return sT


def _bias_mask_t(cfg, sT, nb, mask_ref, kc, midx):
    nh = cfg["nh"]
    if nb is None and midx is None:
        return sT
    parts = []
    for hh in range(nh):
        t = sT[:, hh * TQ:(hh + 1) * TQ]
        if nb is not None:
            t = t + jnp.concatenate([nb[hh, kc, :]] * (TQ // LANES), axis=1)
        if midx is not None:
            t = t + mask_ref[midx]
        parts.append(t)
    return _cat(parts, 1)


def _kv_plan(kind, i, nk):
    if kind == "fox":
        return i, (lambda j: None), 0
    if kind == "dil":
        return i, (lambda j: i - j), 0
    return nk - 1, (lambda j: None), None


def attn_fwd3(kind, src, S, *, negc_cols=None, mask=None, rope=None, kv=None):
    B = src.shape[0]
    cfg = _attn_setup(kind)
    pair, nh = cfg["pair"], cfg["nh"]
    Sk = S if pair else MEM_LEN
    has_bias, has_rope = negc_cols is not None, rope is not None
    R = nh * TQ
    nq, nk = S // TQ, Sk // TK

    def body(*refs):
        refs = list(refs)
        if pair:
            qkv_ref = refs.pop(0)
            load_q = lambda rows: qkv_ref[0, rows, 0:LANES]
            load_kv = lambda rows: (qkv_ref[0, rows, LANES:2 * LANES], qkv_ref[0, rows, 2 * LANES:3 * LANES])
        else:
            q_ref, k_ref, v_ref = refs.pop(0), refs.pop(0), refs.pop(0)
            load_q = lambda rows: q_ref[0, rows, :]
            load_kv = lambda rows: (k_ref[0, rows, :], v_ref[0, rows, :])
        negc_ref = refs.pop(0) if has_bias else None
        mask_ref = refs.pop(0) if mask is not None else None
        rope_refs = [refs.pop(0) for _ in range(3)] if has_rope else None
        o_ref, lse_ref, qT2s, ks, vTs = refs[:5]
        nb = refs[5] if has_bias else None
        _attn_t_prep(cfg, dict(load_q=load_q, load_kv=load_kv, rope=rope_refs, negc=negc_ref,
                               block=pl.program_id(1)), S, Sk,
                     qT2s=qT2s, ks=ks, vTs=vTs, nb=nb)

        def q_loop(i, _):
            qT2 = qT2s[i]

            last, mask_of, mask_last = _kv_plan(kind, i, nk)

            def cols(j):
                return pl.ds(pl.multiple_of(j * TK, TK), TK)

            def scores(j):
                return _raw_scores_t(cfg, ks[cols(j), :], qT2)

            def soft(s_raw, j, midx, m, l):
                sT = _bias_mask_t(cfg, s_raw, nb, mask_ref, cols(j), midx)
                m_new = jnp.maximum(m, jnp.max(sT, axis=0, keepdims=True))
                p = jnp.exp(sT - m_new)
                alpha = jnp.exp(m - m_new)
                return m_new, alpha * l + jnp.sum(p, axis=0, keepdims=True), alpha, p.astype(BF16)

            def pv(j, p):
                return jnp.dot(vTs[j], p, preferred_element_type=F32)

            def body(j, carry):
                s_cur, p_prev, m, l, accT = carry
                pv_prev = pv(jnp.maximum(j - 1, 0), p_prev)
                s_next = scores(j + 1)
                m, l, alpha, p = soft(s_cur, j, mask_of(j), m, l)
                return s_next, p, m, l, (accT + pv_prev) * alpha

            init = (scores(0), jnp.zeros((TK, R), BF16), jnp.full((1, R), NEG_INF, F32), jnp.zeros((1, R), F32),
                    jnp.zeros((LANES, R), F32))
            s_cur, p_prev, m, l, accT = lax.fori_loop(0, last, body, init)
            pv_prev = pv(jnp.maximum(last - 1, 0), p_prev)
            m, l, alpha, p = soft(s_cur, last, mask_last, m, l)
            accT = (accT + pv_prev) * alpha + pv(last, p)
            oT2 = accT / l
            oT = jnp.where(_head_rows(0, True), oT2[:, 0:TQ], oT2[:, TQ:2 * TQ]) if pair else oT2
            o_ref[0, pl.ds(pl.multiple_of(i * TQ, TQ), TQ), :] = oT.T
            lse_ref[0, 0, pl.ds(i, 1), :] = m + jnp.log(l)
            return 0

        lax.fori_loop(0, nq, q_loop, 0)

    ins, in_specs = _attn_t_inputs(kind, src, S, negc_cols, mask, rope, kv)
    W = cfg["n_blocks"] * LANES
    scratch = [pltpu.VMEM((nq, LANES, R), BF16), pltpu.VMEM((Sk, LANES), BF16), pltpu.VMEM((nk, LANES, TK), BF16)]
    if has_bias:
        scratch.append(pltpu.VMEM((nh, Sk, LANES), F32))
    return pl.pallas_call(
        body, name=kind + "_attn_fwd", grid=(B, cfg["n_blocks"]),
        in_specs=in_specs,
        out_specs=[pl.BlockSpec((1, S, LANES), lambda b, h: (b, 0, h)),
                   pl.BlockSpec((1, 1, nq, R), lambda b, h: (b, h, 0, 0))],
        out_shape=[jax.ShapeDtypeStruct((B, S, W), F32), jax.ShapeDtypeStruct((B, cfg["n_blocks"], nq, R), F32)],
        scratch_shapes=scratch,
        compiler_params=_params(("arbitrary", "arbitrary")),
    )(*ins)


def _tile_walk(kind, nq, nk):
    if kind == "mem":
        return nq * nk, (lambda i, j: (jnp.where(j < nk - 1, i, i + 1), jnp.where(j < nk - 1, j + 1, 0))), None
    nxt = lambda i, j: (jnp.where(j < i, i, i + 1), jnp.where(j < i, j + 1, 0))
    if kind == "fox":
        return nq * (nq + 1) // 2, nxt, (lambda i, j: jnp.where(j == i, 0, 1))
    return nq * (nq + 1) // 2, nxt, (lambda i, j: i - j)


def attn_fwd4(kind, src, S, *, negc_cols=None, mask=None, rope=None, kv=None):
    B = src.shape[0]
    cfg = _attn_setup(kind)
    pair, nh = cfg["pair"], cfg["nh"]
    Sk = S if pair else MEM_LEN
    has_bias, has_rope = negc_cols is not None, rope is not None
    R = nh * TQ
    nq, nk = S // TQ, Sk // TK
    n_pairs, successor, mask_index = _tile_walk(kind, nq, nk)
    assert n_pairs % 2 == 0

    def body(*refs):
        refs = list(refs)
        if pair:
            qkv_ref = refs.pop(0)
            load_q = lambda rows: qkv_ref[0, rows, 0:LANES]
            load_kv = lambda rows: (qkv_ref[0, rows, LANES:2 * LANES], qkv_ref[0, rows, 2 * LANES:3 * LANES])
        else:
            q_ref, k_ref, v_ref = refs.pop(0), refs.pop(0), refs.pop(0)
            load_q = lambda rows: q_ref[0, rows, :]
            load_kv = lambda rows: (k_ref[0, rows, :], v_ref[0, rows, :])
        negc_ref = refs.pop(0) if has_bias else None
        mask_ref = refs.pop(0) if mask is not None else None
        rope_refs = [refs.pop(0) for _ in range(3)] if has_rope else None
        o_ref, lse_ref, qT2s, ks, vTs, s_a, s_b, p_a, p_b, acc_all, m_all, l_all = refs[:12]
        nb = refs[12] if has_bias else None
        _attn_t_prep(cfg, dict(load_q=load_q, load_kv=load_kv, rope=rope_refs, negc=negc_ref,
                               block=pl.program_id(1)), S, Sk, qT2s=qT2s, ks=ks, vTs=vTs, nb=nb)

        def cols(j):
            return pl.ds(pl.multiple_of(j * TK, TK), TK)

        def park(i, m, l, accT):
            acc_all[i] = accT
            m_all[pl.ds(i, 1), :] = m
            l_all[pl.ds(i, 1), :] = l

        def finish(i, _):
            l = l_all[pl.ds(i, 1), :]
            oT2 = acc_all[i] / l
            oT = jnp.where(_head_rows(0, True), oT2[:, 0:TQ], oT2[:, TQ:2 * TQ]) if pair else oT2
            o_ref[0, pl.ds(pl.multiple_of(i * TQ, TQ), TQ), :] = oT.T
            lse_ref[0, 0, pl.ds(i, 1), :] = m_all[pl.ds(i, 1), :] + jnp.log(l)
            return 0

        def half(i, j, i_prev, j_prev, s_cur, s_next, p_cur, p_prev, m, l, accT):
            i_n, j_n = successor(i, j)
            acc_full = accT + jnp.dot(vTs[j_prev], p_prev[...], preferred_element_type=F32)
            s_next[...] = _raw_scores_t(cfg, ks[cols(j_n), :], qT2s[jnp.minimum(i_n, nq - 1)])
            park(i_prev, m, l, acc_full)
            first = j == 0
            m = jnp.where(first, NEG_INF, m)
            l = jnp.where(first, 0.0, l)
            sT = _bias_mask_t(cfg, s_cur[...], nb, mask_ref, cols(j), None if mask_index is None else mask_index(i, j))
            m_new = jnp.maximum(m, jnp.max(sT, axis=0, keepdims=True))
            p = jnp.exp(sT - m_new)
            alpha = jnp.exp(m - m_new)
            p_cur[...] = p.astype(BF16)
            return i_n, j_n, i, j, m_new, alpha * l + jnp.sum(p, axis=0, keepdims=True), acc_full * alpha

        def two(_, carry):
            i, j, i_prev, j_prev, m, l, accT = carry
            i, j, i_prev, j_prev, m, l, accT = half(i, j, i_prev, j_prev, s_a, s_b, p_a, p_b, m, l, accT)
            return half(i, j, i_prev, j_prev, s_b, s_a, p_b, p_a, m, l, accT)

        s_a[...] = _raw_scores_t(cfg, ks[cols(0), :], qT2s[0])
        p_b[...] = jnp.zeros((TK, R), BF16)
        zero = jnp.int32(0)
        init = (zero, zero, zero, zero, jnp.full((1, R), NEG_INF, F32), jnp.ones((1, R), F32),
                jnp.zeros((LANES, R), F32))
        _, _, i_prev, j_prev, m, l, accT = lax.fori_loop(0, n_pairs // 2, two, init)
        park(i_prev, m, l, accT + jnp.dot(vTs[j_prev], p_b[...], preferred_element_type=F32))
        lax.fori_loop(0, nq, finish, 0)

    ins, in_specs = _attn_t_inputs(kind, src, S, negc_cols, mask, rope, kv)
    W = cfg["n_blocks"] * LANES
    scratch = [pltpu.VMEM((nq, LANES, R), BF16), pltpu.VMEM((Sk, LANES), BF16), pltpu.VMEM((nk, LANES, TK), BF16),
               pltpu.VMEM((TK, R), F32), pltpu.VMEM((TK, R), F32), pltpu.VMEM((TK, R), BF16), pltpu.VMEM((TK, R), BF16),
               pltpu.VMEM((nq, LANES, R), F32), pltpu.VMEM((nq, R), F32), pltpu.VMEM((nq, R), F32)]
    if has_bias:
        scratch.append(pltpu.VMEM((nh, Sk, LANES), F32))
    return pl.pallas_call(
        body, name=kind + "_attn_fwd", grid=(B, cfg["n_blocks"]),
        in_specs=in_specs,
        out_specs=[pl.BlockSpec((1, S, LANES), lambda b, h: (b, 0, h)),
                   pl.BlockSpec((1, 1, nq, R), lambda b, h: (b, h, 0, 0))],
        out_shape=[jax.ShapeDtypeStruct((B, S, W), F32), jax.ShapeDtypeStruct((B, cfg["n_blocks"], nq, R), F32)],
        scratch_shapes=scratch,
        compiler_params=_params(("arbitrary", "arbitrary")),
    )(*ins)


def attn_bwd3(kind, src, do, o, lse, S, *, negc_cols=None, mask=None, rope=None, kv=None, token=None):
    B = src.shape[0]
    cfg = _attn_setup(kind)
    pair, nh, s_scale, q_fold = cfg["pair"], cfg["nh"], cfg["s_scale"], cfg["q_fold"]
    Sk = S if pair else MEM_LEN
    has_bias, has_rope = negc_cols is not None, rope is not None
    R = nh * TQ
    nq, nk = S // TQ, Sk // TK
    n_pairs, successor, mask_index = _tile_walk(kind, nq, nk)
    assert n_pairs % 2 == 0

    def body(*refs):
        refs = list(refs)
        if pair:
            qkv_ref = refs.pop(0)
            load_q = lambda rows: qkv_ref[0, rows, 0:LANES]
            load_kv = lambda rows: (qkv_ref[0, rows, LANES:2 * LANES], qkv_ref[0, rows, 2 * LANES:3 * LANES])
        else:
            q_ref, k_ref, v_ref = refs.pop(0), refs.pop(0), refs.pop(0)
            load_q = lambda rows: q_ref[0, rows, :]
            load_kv = lambda rows: (k_ref[0, rows, :], v_ref[0, rows, :])
        negc_ref = refs.pop(0) if has_bias else None
        mask_ref = refs.pop(0) if mask is not None else None
        rope_refs = [refs.pop(0) for _ in range(3)] if has_rope else None
        do_ref, o_ref, lse_ref = refs.pop(0), refs.pop(0), refs.pop(0)
        if token is not None:
            refs.pop(0)
        if pair:
            dqkv_ref = refs.pop(0)
            dneg_ref = refs.pop(0) if has_bias else None
            drow_ref = refs.pop(0) if has_bias else None
        else:
            dq_ref, dk_ref, dv_ref = refs.pop(0), refs.pop(0), refs.pop(0)
        qT2s, ks, q2s, vs, kTs, doT2s, do2s, delta_s, dk_acc, dv_acc = refs[:10]
        bufs_a, bufs_b, dq_all = refs[10:14], refs[14:18], refs[18]
        nb, dneg_acc, drow_all = (refs[19], refs[20], refs[21]) if has_bias else (None, None, None)
        lane = lax.broadcasted_iota(jnp.int32, (1, LANES), 1)
        _attn_t_prep(cfg, dict(load_q=load_q, load_kv=load_kv, rope=rope_refs, negc=negc_ref,
                               block=pl.program_id(1)), S, Sk,
                     qT2s=qT2s, ks=ks, q2s=q2s, vs=vs, kTs=kTs, nb=nb)

        def prep_do(n, _):
            rows = pl.ds(pl.multiple_of(n * TQ, TQ), TQ)
            dob = do_ref[0, rows, :].astype(BF16)
            _store_stacked(cfg, lane, do2s, n, dob)
            doT = dob.astype(F32).T
            prodT = doT * o_ref[0, rows, :].T
            doTb = doT.astype(BF16)
            for hh in range(nh):
                hm = _head_rows(hh, pair)
                doT2s[n, :, hh * TQ:(hh + 1) * TQ] = jnp.where(hm, doTb, jnp.zeros_like(doTb))
                delta_s[pl.ds(n, 1), hh * TQ:(hh + 1) * TQ] = jnp.sum(jnp.where(hm, prodT, 0.0), axis=0, keepdims=True)
            return 0

        def zero_kv(n, _):
            rows = pl.ds(pl.multiple_of(n * TK, TK), TK)
            dk_acc[rows, :] = jnp.zeros((TK, LANES), F32)
            dv_acc[rows, :] = jnp.zeros((TK, LANES), F32)
            if has_bias:
                for hh in range(nh):
                    dneg_acc[hh, rows, :] = jnp.zeros((TK, LANES), F32)
            return 0

        lax.fori_loop(0, nq, prep_do, 0)
        lax.fori_loop(0, nk, zero_kv, 0)

        def cols(j):
            return pl.ds(pl.multiple_of(j * TK, TK), TK)

        def rows2(i):
            return pl.ds(pl.multiple_of(i * R, R), R)

        def park(i, dqT2, drow):
            dq_all[i] = dqT2
            if has_bias:
                drow_all[pl.ds(i, 1), :] = drow

        def half(i, j, i_prev, j_prev, cur, nxt_bufs, prv, dqT2, drow):
            s_cur, dp_cur, pb_cur, dsb_cur = cur
            s_next, dp_next = nxt_bufs[0], nxt_bufs[1]
            pb_prev, dsb_prev = prv[2], prv[3]
            i_n, j_n = successor(i, j)
            first = j == 0
            if has_bias:
                drow_all[pl.ds(i_prev, 1), :] = drow
            drow = jnp.where(first, 0.0, drow)
            kc = cols(j)
            sT = _bias_mask_t(cfg, s_cur[...], nb, mask_ref, kc, None if mask_index is None else mask_index(i, j))
            pT = jnp.exp(sT - lse_ref[0, 0, pl.ds(i, 1), :])
            dsT = pT * (dp_cur[...] - delta_s[pl.ds(i, 1), :])
            if has_bias:
                drow = drow + jnp.sum(dsT, axis=0, keepdims=True)
                for hh in range(nh):
                    part = dsT[:, hh * TQ:hh * TQ + LANES]
                    for t in range(1, TQ // LANES):
                        part = part + dsT[:, hh * TQ + t * LANES:hh * TQ + (t + 1) * LANES]
                    dneg_acc[hh, kc, :] += part
            if s_scale is not None:
                dsT = dsT * s_scale
            pb_cur[...] = pT.astype(BF16)
            dsb_cur[...] = dsT.astype(BF16)
            kp = cols(j_prev)
            dv_acc[kp, :] += jnp.dot(pb_prev[...], do2s[rows2(i_prev), :], preferred_element_type=F32)
            dk_acc[kp, :] += jnp.dot(dsb_prev[...], q2s[rows2(i_prev), :], preferred_element_type=F32)
            dq_full = dqT2 + jnp.dot(kTs[j_prev], dsb_prev[...], preferred_element_type=F32)
            dq_all[i_prev] = dq_full
            dqT2 = jnp.where(first, 0.0, dq_full)
            i_nc = jnp.minimum(i_n, nq - 1)
            kn = cols(j_n)
            s_next[...] = _raw_scores_t(cfg, ks[kn, :], qT2s[i_nc])
            dp_next[...] = jnp.dot(vs[kn, :], doT2s[i_nc], preferred_element_type=F32)
            return i_n, j_n, i, j, dqT2, drow

        def two(_, carry):
            i, j, i_prev, j_prev, dqT2, drow = carry
            i, j, i_prev, j_prev, dqT2, drow = half(i, j, i_prev, j_prev, bufs_a, bufs_b, bufs_b, dqT2, drow)
            return half(i, j, i_prev, j_prev, bufs_b, bufs_a, bufs_a, dqT2, drow)

        bufs_a[0][...] = _raw_scores_t(cfg, ks[cols(0), :], qT2s[0])
        bufs_a[1][...] = jnp.dot(vs[cols(0), :], doT2s[0], preferred_element_type=F32)
        bufs_b[2][...] = jnp.zeros((TK, R), BF16)
        bufs_b[3][...] = jnp.zeros((TK, R), BF16)
        zero = jnp.int32(0)
        init = (zero, zero, zero, zero, jnp.zeros((LANES, R), F32), jnp.zeros((1, R), F32))
        _, _, i_prev, j_prev, dqT2, drow = lax.fori_loop(0, n_pairs // 2, two, init)
        kp = cols(j_prev)
        dv_acc[kp, :] += jnp.dot(bufs_b[2][...], do2s[rows2(i_prev), :], preferred_element_type=F32)
        dk_acc[kp, :] += jnp.dot(bufs_b[3][...], q2s[rows2(i_prev), :], preferred_element_type=F32)
        park(i_prev, dqT2 + jnp.dot(kTs[j_prev], bufs_b[3][...], preferred_element_type=F32), drow)

        def fin_q(i, _):
            rows = pl.ds(pl.multiple_of(i * TQ, TQ), TQ)
            dqT2 = dq_all[i]
            dqT = jnp.where(_head_rows(0, True), dqT2[:, 0:TQ], dqT2[:, TQ:2 * TQ]) if pair else dqT2
            dq = dqT.T
            if q_fold is not None:
                dq = dq * q_fold
            if has_rope:
                dq = _rope_bwd(dq, *[t[rows, :] for t in rope_refs])
            if pair:
                dqkv_ref[0, rows, 0:LANES] = dq.astype(BF16)
            else:
                dq_ref[0, rows, :] = dq.astype(BF16)
            if has_bias:
                drow_ref[0, 0, pl.ds(i, 1), :] = drow_all[pl.ds(i, 1), :]
            return 0

        lax.fori_loop(0, nq, fin_q, 0)

        def fin_kv(n, _):
            rows = pl.ds(pl.multiple_of(n * TK, TK), TK)
            dk = dk_acc[rows, :]
            if has_rope:
                dk = _rope_bwd(dk, *[t[rows, :] for t in rope_refs])
            if pair:
                dqkv_ref[0, rows, LANES:2 * LANES] = dk.astype(BF16)
                dqkv_ref[0, rows, 2 * LANES:3 * LANES] = dv_acc[rows, :].astype(BF16)
            else:
                dk_ref[0, rows, :] = dk.astype(BF16)
                dv_ref[0, rows, :] = dv_acc[rows, :].astype(BF16)
            if has_bias:
                x0 = jnp.sum(dneg_acc[0, rows, :], axis=1, keepdims=True)
                x1 = jnp.sum(dneg_acc[1, rows, :], axis=1, keepdims=True)
                dneg_ref[0, rows, :] = jnp.where(lane == 0, x0, jnp.where(lane == 1, x1, 0.0))
            return 0

        lax.fori_loop(0, nk, fin_kv, 0)

    ins, in_specs = _attn_t_inputs(kind, src, S, negc_cols, mask, rope, kv)
    row_spec = pl.BlockSpec((1, S, LANES), lambda b, h: (b, 0, h))
    vec_spec = pl.BlockSpec((1, 1, nq, R), lambda b, h: (b, h, 0, 0))
    ins += [do, o, lse]
    in_specs += [row_spec, row_spec, vec_spec]
    if token is not None:
        ins.append(token)
        in_specs.append(pl.BlockSpec(token.shape, lambda b, h: (0, 0)))
    W = cfg["n_blocks"] * LANES
    if pair:
        out_specs = [pl.BlockSpec((1, S, PAIR_W), lambda b, h: (b, 0, h))]
        out_shape = [jax.ShapeDtypeStruct((B, S, 3 * W), BF16)]
        if has_bias:
            out_specs += [row_spec, vec_spec]
            out_shape += [jax.ShapeDtypeStruct((B, S, W), F32), jax.ShapeDtypeStruct((B, cfg["n_blocks"], nq, R), F32)]
    else:
        kv_spec = pl.BlockSpec((1, MEM_LEN, LANES), lambda b, h: (b, 0, h))
        out_specs = [row_spec, kv_spec, kv_spec]
        out_shape = [jax.ShapeDtypeStruct((B, S, W), BF16)] + [jax.ShapeDtypeStruct((B, MEM_LEN, W), BF16)] * 2
    scratch = [pltpu.VMEM((nq, LANES, R), BF16), pltpu.VMEM((Sk, LANES), BF16), pltpu.VMEM((nh * S, LANES), BF16),
               pltpu.VMEM((Sk, LANES), BF16), pltpu.VMEM((nk, LANES, TK), BF16), pltpu.VMEM((nq, LANES, R), BF16),
               pltpu.VMEM((nh * S, LANES), BF16), pltpu.VMEM((nq, R), F32),
               pltpu.VMEM((Sk, LANES), F32), pltpu.VMEM((Sk, LANES), F32)]
    pair_bufs = [pltpu.VMEM((TK, R), F32), pltpu.VMEM((TK, R), F32), pltpu.VMEM((TK, R), BF16), pltpu.VMEM((TK, R), BF16)]
    scratch += pair_bufs + pair_bufs + [pltpu.VMEM((nq, LANES, R), F32)]
    if has_bias:
        scratch += [pltpu.VMEM((nh, Sk, LANES), F32), pltpu.VMEM((nh, Sk, LANES), F32), pltpu.VMEM((nq, R), F32)]
    return pl.pallas_call(
        body, name=kind + "_attn_bwd", grid=(B, cfg["n_blocks"]),
        in_specs=in_specs, out_specs=out_specs, out_shape=out_shape, scratch_shapes=scratch,
        compiler_params=_params(("arbitrary", "arbitrary")),
    )(*ins)


def _sigmoid(g):
    return 1.0 / (1.0 + jnp.exp(-g))


def out_fwd(proj, o_fox, o_dil, o_mem, w_out, x, target, gf, tm):
    T = x.shape[0]

    def body(fg_ref, dg_ref, mg_ref, of_ref, od_ref, om_ref, w_ref, x_ref, t_ref, gf_ref,
             y_ref, dx_ref, dxb_ref, sm_ref):
        parts = []
        for g_ref, o_ref in ((fg_ref, of_ref), (dg_ref, od_ref), (mg_ref, om_ref)):
            g = g_ref[...]
            parts.append((o_ref[...] * (g * _sigmoid(g))).astype(BF16))
        ymix = jnp.concatenate(parts, axis=1)
        y_ref[...] = ymix
        x2 = x_ref[...] + jnp.dot(ymix, w_ref[...], preferred_element_type=F32)
        r = lax.rsqrt(jnp.mean(x2 * x2, axis=-1, keepdims=True) + RMS_EPS)
        yn = x2 * r
        err = yn * gf_ref[...] - t_ref[...]
        loss = 0.5 * jnp.sum(jnp.sum(err * err, axis=-1, keepdims=True) / D_MODEL, axis=0, keepdims=True)
        dyf = err / D_MODEL
        dgf = jnp.sum(dyf * yn, axis=0, keepdims=True)
        dyn = dyf * gf_ref[...]
        dx2 = r * (dyn - yn * jnp.mean(dyn * yn, axis=-1, keepdims=True))
        dx_ref[...] = dx2
        dxb_ref[...] = dx2.astype(BF16)
        row = lax.broadcasted_iota(jnp.int32, (8, D_MODEL), 0)
        upd = jnp.where(row == 0, dgf, jnp.where(row == 1, loss, 0.0))

        @pl.when(pl.program_id(0) == 0)
        def _():
            sm_ref[...] = upd

        @pl.when(pl.program_id(0) != 0)
        def _():
            sm_ref[...] += upd

    def rows(w, col=0):
        return pl.BlockSpec((tm, w), lambda i: (i, col))

    return pl.pallas_call(
        body, name="out_fwd", grid=(T // tm,),
        in_specs=[rows(FOX_W, P_FG // FOX_W), rows(DIL_W, P_DG // DIL_W), rows(MEM_W, P_MG // MEM_W),
                  rows(FOX_W), rows(DIL_W), rows(MEM_W),
                  pl.BlockSpec((MIX_W, D_MODEL), lambda i: (0, 0)),
                  rows(D_MODEL), rows(D_MODEL), pl.BlockSpec((1, D_MODEL), lambda i: (0, 0))],
        out_specs=[rows(MIX_W), rows(D_MODEL), rows(D_MODEL), pl.BlockSpec((8, D_MODEL), lambda i: (0, 0))],
        out_shape=[jax.ShapeDtypeStruct((T, MIX_W), BF16), jax.ShapeDtypeStruct((T, D_MODEL), F32),
                   jax.ShapeDtypeStruct((T, D_MODEL), BF16), jax.ShapeDtypeStruct((8, D_MODEL), F32)],
        compiler_params=_params(("arbitrary",)),
    )(proj, proj, proj, o_fox, o_dil, o_mem, w_out, x, target, gf)


def out_bwd(proj, o_fox, o_dil, o_mem, w_out, dx2b, tm):
    T = dx2b.shape[0]

    def body(fg_ref, dg_ref, mg_ref, of_ref, od_ref, om_ref, w_ref, dx_ref,
             dof_ref, dod_ref, dom_ref, dfg_ref, ddg_ref, dmg_ref):
        dmix = lax.dot_general(dx_ref[...], w_ref[...], (((1,), (1,)), ((), ())), preferred_element_type=F32)
        col = 0
        for g_ref, o_ref, do_ref, dgate_ref in ((fg_ref, of_ref, dof_ref, dfg_ref), (dg_ref, od_ref, dod_ref, ddg_ref),
                                                 (mg_ref, om_ref, dom_ref, dmg_ref)):
            w = g_ref.shape[1]
            d = dmix[:, col:col + w]
            col += w
            g = g_ref[...]
            sg = _sigmoid(g)
            do_ref[...] = d * (g * sg)
            dgate_ref[...] = (d * o_ref[...] * (sg * (1.0 + g * (1.0 - sg)))).astype(BF16)

    def rows(w, col=0):
        return pl.BlockSpec((tm, w), lambda i: (i, col))

    return pl.pallas_call(
        body, name="out_bwd", grid=(T // tm,),
        in_specs=[rows(FOX_W, P_FG // FOX_W), rows(DIL_W, P_DG // DIL_W), rows(MEM_W, P_MG // MEM_W),
                  rows(FOX_W), rows(DIL_W), rows(MEM_W),
                  pl.BlockSpec((MIX_W, D_MODEL), lambda i: (0, 0)), rows(D_MODEL)],
        out_specs=[rows(FOX_W), rows(DIL_W), rows(MEM_W), rows(FOX_W), rows(DIL_W), rows(MEM_W)],
        out_shape=[jax.ShapeDtypeStruct((T, FOX_W), F32), jax.ShapeDtypeStruct((T, DIL_W), F32),
                   jax.ShapeDtypeStruct((T, MEM_W), F32), jax.ShapeDtypeStruct((T, FOX_W), BF16),
                   jax.ShapeDtypeStruct((T, DIL_W), BF16), jax.ShapeDtypeStruct((T, MEM_W), BF16)],
        compiler_params=_params(("arbitrary",)),
    )(proj, proj, proj, o_fox, o_dil, o_mem, w_out, dx2b)


def out_step(proj, o_fox, o_dil, o_mem, w_out, x, target, gf, tm):
    T = x.shape[0]

    def body(fg_ref, dg_ref, mg_ref, of_ref, od_ref, om_ref, w_ref, x_ref, t_ref, gf_ref,
             dx_ref, dof_ref, dod_ref, dom_ref, dfg_ref, ddg_ref, dmg_ref, gw_ref, sm_ref, gw_acc):
        branches = []
        for g_ref, o_ref in ((fg_ref, of_ref), (dg_ref, od_ref), (mg_ref, om_ref)):
            g = g_ref[...]
            sg = _sigmoid(g)
            o = o_ref[...]
            branches.append((g, sg, o))
        ymix = jnp.concatenate([(o * (g * sg)).astype(BF16) for g, sg, o in branches], axis=1)
        x2 = x_ref[...] + jnp.dot(ymix, w_ref[...], preferred_element_type=F32)
        r = lax.rsqrt(jnp.mean(x2 * x2, axis=-1, keepdims=True) + RMS_EPS)
        yn = x2 * r
        err = yn * gf_ref[...] - t_ref[...]
        loss = 0.5 * jnp.sum(jnp.sum(err * err, axis=-1, keepdims=True) / D_MODEL, axis=0, keepdims=True)
        dyf = err / D_MODEL
        dgf = jnp.sum(dyf * yn, axis=0, keepdims=True)
        dyn = dyf * gf_ref[...]
        dx2 = r * (dyn - yn * jnp.mean(dyn * yn, axis=-1, keepdims=True))
        dx_ref[...] = dx2
        dxb = dx2.astype(BF16)
        dmix = lax.dot_general(dxb, w_ref[...], (((1,), (1,)), ((), ())), preferred_element_type=F32)
        col = 0
        for (g, sg, o), do_ref, dgate_ref in zip(branches, (dof_ref, dod_ref, dom_ref), (dfg_ref, ddg_ref, dmg_ref)):
            d = dmix[:, col:col + g.shape[1]]
            col += g.shape[1]
            do_ref[...] = d * (g * sg)
            dgate_ref[...] = (d * o * (sg * (1.0 + g * (1.0 - sg)))).astype(BF16)
        gw = lax.dot_general(ymix, dxb, (((0,), (0,)), ((), ())), preferred_element_type=F32)
        row = lax.broadcasted_iota(jnp.int32, (8, D_MODEL), 0)
        upd = jnp.where(row == 0, dgf, jnp.where(row == 1, loss, 0.0))

        @pl.when(pl.program_id(0) == 0)
        def _():
            sm_ref[...] = upd
            gw_acc[...] = gw

        @pl.when(pl.program_id(0) != 0)
        def _():
            sm_ref[...] += upd
            gw_acc[...] += gw

        @pl.when(pl.program_id(0) == T // tm - 1)
        def _():
            gw_ref[...] = gw_acc[...].astype(BF16)

    def rows(w, col=0):
        return pl.BlockSpec((tm, w), lambda i: (i, col))

    return pl.pallas_call(
        body, name="out_step", grid=(T // tm,),
        in_specs=[rows(FOX_W, P_FG // FOX_W), rows(DIL_W, P_DG // DIL_W), rows(MEM_W, P_MG // MEM_W),
                  rows(FOX_W), rows(DIL_W), rows(MEM_W),
                  pl.BlockSpec((MIX_W, D_MODEL), lambda i: (0, 0)),
                  rows(D_MODEL), rows(D_MODEL), pl.BlockSpec((1, D_MODEL), lambda i: (0, 0))],
        out_specs=[rows(D_MODEL), rows(FOX_W), rows(DIL_W), rows(MEM_W), rows(FOX_W), rows(DIL_W), rows(MEM_W),
                   pl.BlockSpec((MIX_W, D_MODEL), lambda i: (0, 0)), pl.BlockSpec((8, D_MODEL), lambda i: (0, 0))],
        out_shape=[jax.ShapeDtypeStruct((T, D_MODEL), F32), jax.ShapeDtypeStruct((T, FOX_W), F32),
                   jax.ShapeDtypeStruct((T, DIL_W), F32), jax.ShapeDtypeStruct((T, MEM_W), F32),
                   jax.ShapeDtypeStruct((T, FOX_W), BF16), jax.ShapeDtypeStruct((T, DIL_W), BF16),
                   jax.ShapeDtypeStruct((T, MEM_W), BF16), jax.ShapeDtypeStruct((MIX_W, D_MODEL), BF16),
                   jax.ShapeDtypeStruct((8, D_MODEL), F32)],
        scratch_shapes=[pltpu.VMEM((MIX_W, D_MODEL), F32)],
        compiler_params=_params(("arbitrary",)),
    )(proj, proj, proj, o_fox, o_dil, o_mem, w_out, x, target, gf)


def adamw(w, g, m, v, tr, name):
    lead = w.shape[:-2]
    R, C = w.shape[-2:]
    zeros = (0,) * len(lead)

    def body(w_ref, g_ref, m_ref, v_ref, d_ref, mo_ref, vo_ref):
        gv = g_ref[...]
        mn = ADAM_B1 * m_ref[...] + (1.0 - ADAM_B1) * gv
        vn = ADAM_B2 * v_ref[...] + (1.0 - ADAM_B2) * jnp.square(gv)
        m_hat = mn / (1.0 - ADAM_B1 ** ADAM_STEP)
        v_hat = vn / (1.0 - ADAM_B2 ** ADAM_STEP)
        d_ref[...] = -ADAM_LR * (m_hat / (jnp.sqrt(v_hat) + ADAM_EPS) + ADAM_WD * w_ref[...])
        mo_ref[...] = mn
        vo_ref[...] = vn

    spec = pl.BlockSpec((1,) * len(lead) + (tr, C), lambda i: zeros + (i, 0))
    return pl.pallas_call(
        body, name=name, grid=(pl.cdiv(R, tr),),
        in_specs=[spec] * 4, out_specs=[spec] * 3,
        out_shape=[jax.ShapeDtypeStruct(w.shape, F32)] * 3,
        compiler_params=_params(("arbitrary",)),
    )(w, g, m, v)


def _pad_row(v, width):
    return jnp.concatenate([v, jnp.zeros((1, width - v.shape[1]), v.dtype)], axis=1)


def _old_local_grads(x, mem, norm_g, b_forget, mem_norm_g, final_norm_g, loss_target, w_in_p, w_kv, w_out):
    B, S, D = x.shape
    T = B * S
    xt = x.reshape(T, D)
    memt = mem.reshape(B * MEM_LEN, D)
    b_pad = _pad_row(b_forget, LANES)

    h = rms_fwd(xt, norm_g, 512, "rms_x")
    proj = mm_nn(h, w_in_p, 512, PW // 3, "in_proj")
    proj3 = proj.reshape(B, S, PW)
    mh = rms_fwd(memt, mem_norm_g, B * MEM_LEN, "rms_mem")
    mkv = mm_nn(mh, w_kv, B * MEM_LEN, 2 * MEM_W, "mem_kv_proj")
    mkv3 = mkv.reshape(B, MEM_LEN, 2 * MEM_W)

    negc = fox_gate(proj3, b_pad)
    causal = _log_masks_t(S, "causal")
    causal = jnp.concatenate([causal, jnp.zeros_like(causal)], axis=0)
    dilated = _log_masks_t(S, "dilated")
    rope = _rope_tables(S)

    o_fox, lse_fox = attn_fwd4("fox", proj3, S, negc_cols=negc, mask=causal)
    o_dil, lse_dil = attn_fwd4("dil", proj3, S, mask=dilated, rope=rope)
    o_mem, lse_mem = attn_fwd4("mem", proj3, S, kv=mkv3)

    dx2, do_fox, do_dil, do_mem, dfg, ddg, dmg, g_out, small_out = out_step(
        proj, o_fox.reshape(T, FOX_W), o_dil.reshape(T, DIL_W), o_mem.reshape(T, MEM_W), w_out,
        xt, loss_target.reshape(T, D), final_norm_g.reshape(1, D), 256)

    dqkv_fox, dneg, drow = attn_bwd3("fox", proj3, do_fox.reshape(B, S, FOX_W), o_fox, lse_fox, S,
                                     negc_cols=negc, mask=causal)
    (dqkv_dil,) = attn_bwd3("dil", proj3, do_dil.reshape(B, S, DIL_W), o_dil, lse_dil, S, mask=dilated, rope=rope)
    dmq, dmk, dmv = attn_bwd3("mem", proj3, do_mem.reshape(B, S, MEM_W), o_mem, lse_mem, S, kv=mkv3)
    drow = drow.reshape(B, FOX_HEADS // 2, S // TQ, 2, TQ).transpose(0, 1, 3, 2, 4).reshape(B, FOX_HEADS, S)
    drow = jnp.pad(drow, ((0, 0), (0, LANES - FOX_HEADS), (0, 0)))
    dflog, db_part = fox_gate_bwd(drow, dneg, proj3, b_pad)

    groups = [[(dfg, P_FG), (ddg, P_DG), (dmg, P_MG)],
              [(dqkv_fox.reshape(T, 3 * FOX_W), P_FOX), (dflog.reshape(T, LANES), P_FLOG)],
              [(dqkv_dil.reshape(T, 3 * DIL_W), P_DIL), (dmq.reshape(T, MEM_W), P_MQ)]]
    g_in = [mm_tn_multi(h, [arr for arr, _ in grp], 512, "w_in_grad_%d" % n) for n, grp in enumerate(groups)]
    grad_x, dng = in_proj_bwd_rms([piece for grp in groups for piece in grp], w_in_p, xt, norm_g, dx2, 256)

    dmkv = jnp.concatenate([dmk, dmv], axis=2).reshape(B * MEM_LEN, 2 * MEM_W)
    g_kv = mm_tn(mh, dmkv, B * MEM_LEN, 2 * MEM_W, "w_kv_grad")
    dmh = mm_nt(dmkv, w_kv, B * MEM_LEN, D, "mem_kv_bwd")
    _, dmng = rms_bwd(memt, mem_norm_g, dmh, None, B * MEM_LEN, "rms_mem_bwd")

    small = jnp.concatenate([dng[0:1], dmng[0:1], small_out[0:1], _pad_row(db_part[0:1], D), small_out[1:2],
                             jnp.zeros((3, D), F32)], axis=0)
    return grad_x.reshape(B, S, D), g_in, g_kv, g_out, small


def _old2_local_grads(x, mem, norm_g, b_forget, mem_norm_g, final_norm_g, loss_target, w_in_p, w_kv, w_out, start_exchange):
    B, S, D = x.shape
    T = B * S
    xt = x.reshape(T, D)
    memt = mem.reshape(B * MEM_LEN, D)
    b_pad = _pad_row(b_forget, LANES)

    h = rms_fwd(xt, norm_g, 512, "rms_x")
    proj = mm_nn(h, w_in_p, 512, PW // 3, "in_proj")
    proj3 = proj.reshape(B, S, PW)
    mh = rms_fwd(memt, mem_norm_g, B * MEM_LEN, "rms_mem")
    mkv = mm_nn(mh, w_kv, B * MEM_LEN, 2 * MEM_W, "mem_kv_proj")
    mkv3 = mkv.reshape(B, MEM_LEN, 2 * MEM_W)

    negc = fox_gate(proj3, b_pad)
    causal = _log_masks_t(S, "causal")
    causal = jnp.concatenate([causal, jnp.zeros_like(causal)], axis=0)
    dilated = _log_masks_t(S, "dilated")
    rope = _rope_tables(S)

    o_fox, lse_fox = attn_fwd4("fox", proj3, S, negc_cols=negc, mask=causal)
    o_dil, lse_dil = attn_fwd4("dil", proj3, S, mask=dilated, rope=rope)
    o_mem, lse_mem = attn_fwd4("mem", proj3, S, kv=mkv3)

    dx2, do_fox, do_dil, do_mem, dfg, ddg, dmg, g_out, small_out = out_step(
        proj, o_fox.reshape(T, FOX_W), o_dil.reshape(T, DIL_W), o_mem.reshape(T, MEM_W), w_out,
        xt, loss_target.reshape(T, D), final_norm_g.reshape(1, D), 256)

    gates = [(dfg, P_FG), (ddg, P_DG), (dmg, P_MG)]
    g_gates = mm_tn_multi(h, [arr for arr, _ in gates], 512, "w_in_grad_gates", BF16)
    first, token = start_exchange([g_gates, g_out], "early_exchange_a")

    dqkv_fox, dneg, drow = attn_bwd3("fox", proj3, do_fox.reshape(B, S, FOX_W), o_fox, lse_fox, S,
                                     negc_cols=negc, mask=causal, token=token)
    drow = drow.reshape(B, FOX_HEADS // 2, S // TQ, 2, TQ).transpose(0, 1, 3, 2, 4).reshape(B, FOX_HEADS, S)
    drow = jnp.pad(drow, ((0, 0), (0, LANES - FOX_HEADS), (0, 0)))
    dflog, db_part = fox_gate_bwd(drow, dneg, proj3, b_pad)
    fox = [(dqkv_fox.reshape(T, 3 * FOX_W), P_FOX), (dflog.reshape(T, LANES), P_FLOG)]
    g_fox = mm_tn_multi(h, [arr for arr, _ in fox], 512, "w_in_grad_fox", BF16)
    second, token = start_exchange([g_fox], "early_exchange_b")

    (dqkv_dil,) = attn_bwd3("dil", proj3, do_dil.reshape(B, S, DIL_W), o_dil, lse_dil, S, mask=dilated, rope=rope,
                            token=token)
    dmq, dmk, dmv = attn_bwd3("mem", proj3, do_mem.reshape(B, S, MEM_W), o_mem, lse_mem, S, kv=mkv3)
    rest = [(dqkv_dil.reshape(T, 3 * DIL_W), P_DIL), (dmq.reshape(T, MEM_W), P_MQ)]
    g_rest = mm_tn_multi(h, [arr for arr, _ in rest], 512, "w_in_grad_rest")
    grad_x, dng = in_proj_bwd_rms(gates + fox + rest, w_in_p, xt, norm_g, dx2, 256)

    dmkv = jnp.concatenate([dmk, dmv], axis=2).reshape(B * MEM_LEN, 2 * MEM_W)
    g_kv = mm_tn(mh, dmkv, B * MEM_LEN, 2 * MEM_W, "w_kv_grad")
    dmh = mm_nt(dmkv, w_kv, B * MEM_LEN, D, "mem_kv_bwd")
    _, dmng = rms_bwd(memt, mem_norm_g, dmh, None, B * MEM_LEN, "rms_mem_bwd")

    small = jnp.concatenate([dng[0:1], dmng[0:1], small_out[0:1], _pad_row(db_part[0:1], D), small_out[1:2],
                             jnp.zeros((3, D), F32)], axis=0)
    return grad_x.reshape(B, S, D), [(first, dqkv_fox), (second, dqkv_dil)], [g_rest, g_kv], small


def local_grads(x, mem, norm_g, b_forget, mem_norm_g, final_norm_g, loss_target, w_in_p, small_weights, start_exchange):
    B, S, D = x.shape
    T = B * S
    xt = x.reshape(T, D)
    memt = mem.reshape(B * MEM_LEN, D)
    b_pad = _pad_row(b_forget, LANES)

    h = rms_fwd(xt, norm_g, 512, "rms_x")
    proj = mm_nn(h, w_in_p, 512, PW // 3, "in_proj")
    proj3 = proj.reshape(B, S, PW)

    negc = fox_gate(proj3, b_pad)
    causal = _log_masks_t(S, "causal")
    causal = jnp.concatenate([causal, jnp.zeros_like(causal)], axis=0)
    dilated = _log_masks_t(S, "dilated")
    rope = _rope_tables(S)

    o_fox, lse_fox = attn_fwd4("fox", proj3, S, negc_cols=negc, mask=causal)
    o_dil, lse_dil = attn_fwd4("dil", proj3, S, mask=dilated, rope=rope)

    w_kv, w_out = small_weights(o_dil)
    mh = rms_fwd(memt, mem_norm_g, B * MEM_LEN, "rms_mem")
    mkv = mm_nn(mh, w_kv, B * MEM_LEN, 2 * MEM_W, "mem_kv_proj")
    mkv3 = mkv.reshape(B, MEM_LEN, 2 * MEM_W)
    o_mem, lse_mem = attn_fwd4("mem", proj3, S, kv=mkv3)

    dx2, do_fox, do_dil, do_mem, dfg, ddg, dmg, g_out, small_out = out_step(
        proj, o_fox.reshape(T, FOX_W), o_dil.reshape(T, DIL_W), o_mem.reshape(T, MEM_W), w_out,
        xt, loss_target.reshape(T, D), final_norm_g.reshape(1, D), 256)

    gates = [(dfg, P_FG), (ddg, P_DG), (dmg, P_MG)]
    g_gates = mm_tn_multi(h, [arr for arr, _ in gates], 512, "w_in_grad_gates", BF16)
    first, token = start_exchange([g_gates, g_out], "early_exchange_a")

    dqkv_fox, dneg, drow = attn_bwd3("fox", proj3, do_fox.reshape(B, S, FOX_W), o_fox, lse_fox, S,
                                     negc_cols=negc, mask=causal, token=token)
    drow = drow.reshape(B, FOX_HEADS // 2, S // TQ, 2, TQ).transpose(0, 1, 3, 2, 4).reshape(B, FOX_HEADS, S)
    drow = jnp.pad(drow, ((0, 0), (0, LANES - FOX_HEADS), (0, 0)))
    dflog, db_part = fox_gate_bwd(drow, dneg, proj3, b_pad)
    fox = [(dqkv_fox.reshape(T, 3 * FOX_W), P_FOX), (dflog.reshape(T, LANES), P_FLOG)]
    g_fox = mm_tn_multi(h, [arr for arr, _ in fox], 512, "w_in_grad_fox", BF16)
    second, token = start_exchange([g_fox], "early_exchange_b")

    (dqkv_dil,) = attn_bwd3("dil", proj3, do_dil.reshape(B, S, DIL_W), o_dil, lse_dil, S, mask=dilated, rope=rope,
                            token=token)
    dil = [(dqkv_dil.reshape(T, 3 * DIL_W), P_DIL)]
    g_dil = mm_tn_multi(h, [arr for arr, _ in dil], 512, "w_in_grad_dil", BF16)
    third, token = start_exchange([g_dil], "early_exchange_c")

    dmq, dmk, dmv = attn_bwd3("mem", proj3, do_mem.reshape(B, S, MEM_W), o_mem, lse_mem, S, kv=mkv3, token=token)
    mq = [(dmq.reshape(T, MEM_W), P_MQ)]
    g_mq = mm_tn_multi(h, [arr for arr, _ in mq], 512, "w_in_grad_mq")
    grad_x, dng = in_proj_bwd_rms(gates + fox + dil + mq, w_in_p, xt, norm_g, dx2, 256)

    dmkv = jnp.concatenate([dmk, dmv], axis=2).reshape(B * MEM_LEN, 2 * MEM_W)
    g_kv = mm_tn(mh, dmkv, B * MEM_LEN, 2 * MEM_W, "w_kv_grad")
    dmh = mm_nt(dmkv, w_kv, B * MEM_LEN, D, "mem_kv_bwd")
    _, dmng = rms_bwd(memt, mem_norm_g, dmh, None, B * MEM_LEN, "rms_mem_bwd")

    small = jnp.concatenate([dng[0:1], dmng[0:1], small_out[0:1], _pad_row(db_part[0:1], D), small_out[1:2],
                             jnp.zeros((3, D), F32)], axis=0)
    early = [(first, dqkv_fox), (second, dqkv_dil), (third, dmq)]
    return grad_x.reshape(B, S, D), early, [g_mq, g_kv], small


def kernel(x, mem, norm_g, w_in, b_forget, mem_norm_g, w_mem_kv, w_out, final_norm_g, loss_target, m_norm_g, m_w_in, m_b_forget, m_mem_norm_g, m_w_mem_kv, m_w_out, m_final_norm_g, v_norm_g, v_w_in, v_b_forget, v_mem_norm_g, v_w_mem_kv, v_w_out, v_final_norm_g):
    D = D_MODEL
    (w_in_full,) = weight_gather([_pack_cols(w_in).astype(BF16).reshape(w_in.shape[1], PW)])
    gather, _ = early_exchange_start([w_mem_kv[0].astype(BF16), w_out[0].astype(BF16)], "early_gather", gather=True,
                                     after=w_in_full)

    def small_weights(after):
        _, gathered = early_exchange_wait(gather, after, "early_gather_wait")
        return gathered

    grad_x, early, late, small = local_grads(
        x, mem, norm_g, b_forget, mem_norm_g, final_norm_g, loss_target, w_in_full, small_weights,
        early_exchange_start)

    (first, after_first), (second, after_second), (third, after_third) = early
    (src_gates, src_out), (land_gates, land_out) = early_exchange_wait(first, after_first, "early_wait_a")
    (src_fox,), (land_fox,) = early_exchange_wait(second, after_second, "early_wait_b")
    (src_dil,), (land_dil,) = early_exchange_wait(third, after_third, "early_wait_c")
    gates = slot_sum8(src_gates, land_gates, 128, "sum_w_in_gates")
    gw_out = slot_sum8(src_out, land_out, 256, "sum_w_out")
    fox = slot_sum8(src_fox, land_fox, 128, "sum_w_in_fox")
    dil = slot_sum8(src_dil, land_dil, 128, "sum_w_in_dil")

    *from_sibling, csum = grad_exchange_d2d(late, small)
    names = ("w_in_mq", "w_kv")
    chip_parts = [chip_sum(g, got, 128, "chip_sum_" + n) for g, got, n in zip(late, from_sibling, names)]
    *from_chips, tot = grad_exchange_ici(chip_parts, csum)
    mq, gw_kv = [final_sum(got, 128, "final_sum_" + n) for got, n in zip(from_chips, names)]
    gw_in = _unpack_cols(jnp.concatenate(
        [fox[:, :3 * FOX_W], gates[:, :FOX_W], dil, gates[:, FOX_W:FOX_W + DIL_W], mq,
         gates[:, FOX_W + DIL_W:], fox[:, 3 * FOX_W:]], axis=1)[None])

    loss = tot[4, 0]
    g_norm, g_mem_norm, g_final, g_b = tot[0:1], tot[1:2], tot[2], tot[3:4, :FOX_HEADS]

    def rows8(*rows):
        rows = [r.reshape(1, -1) for r in rows]
        rows = [_pad_row(r, D) for r in rows]
        return jnp.concatenate(rows + [jnp.zeros((8 - len(rows), D), F32)], axis=0)

    sw = rows8(norm_g, mem_norm_g, final_norm_g, b_forget)
    sm = rows8(m_norm_g, m_mem_norm_g, m_final_norm_g, m_b_forget)
    sv = rows8(v_norm_g, v_mem_norm_g, v_final_norm_g, v_b_forget)
    d_s, m_s, v_s = adamw(sw, tot, sm, sv, 8, "adamw_small")
    d_in, m_in, v_in = adamw(w_in, gw_in, m_w_in, v_w_in, 32, "adamw_w_in")
    d_kv, m_kv, v_kv = adamw(w_mem_kv[0], gw_kv, m_w_mem_kv[0], v_w_mem_kv[0], 128, "adamw_w_kv")
    d_out, m_out, v_out = adamw(w_out[0], gw_out, m_w_out[0], v_w_out[0], 256, "adamw_w_out")

    def small_outs(t):
        return t[0:1], t[3:4, :FOX_HEADS], t[1:2], t[2]

    grads = (g_norm, gw_in, g_b, g_mem_norm, gw_kv[None], gw_out[None], g_final)
    outs = []
    for t, big in ((d_s, (d_in, d_kv, d_out)), (m_s, (m_in, m_kv, m_out)), (v_s, (v_in, v_kv, v_out))):
        n, b, mn, f = small_outs(t)
        outs += [n, big[0], b, mn, big[1][None], big[2][None], f]
    return (loss, grad_x, *grads, *outs)
```

```python
import functools
import math

import numpy as np
import jax
import jax.numpy as jnp
from jax import lax
from jax.experimental import pallas as pl
from jax.experimental.pallas import tpu as pltpu

F32 = jnp.float32
BF16 = jnp.bfloat16

D_MODEL = 1024
HEAD_DIM = 64
FOX_HEADS = 12
DIL_HEADS = 12
MEM_HEADS = 4
MEM_HEAD_DIM = 128
MEM_LEN = 256
FOX_W = FOX_HEADS * HEAD_DIM
DIL_W = DIL_HEADS * HEAD_DIM
MEM_W = MEM_HEADS * MEM_HEAD_DIM
MIX_W = FOX_W + DIL_W + MEM_W
DILATIONS = ((128, 1), (512, 4), (2048, 16))
ROPE_THETA = 500000.0
ROPE_DIM = HEAD_DIM // 4
RMS_EPS = 1e-6
NEG_INF = -1e30
IN_W = 4 * FOX_W + FOX_HEADS + 4 * DIL_W + 2 * MEM_W

ADAM_LR = 0.001
ADAM_B1 = 0.9
ADAM_B2 = 0.999
ADAM_EPS = 1e-08
ADAM_WD = 0.01
ADAM_STEP = 10

N_DEV = 8
LANES = 128
PAIR_W = 3 * LANES
TQ = 256
TK = 256

O_FQ, O_FK, O_FV, O_FG = 0, FOX_W, 2 * FOX_W, 3 * FOX_W
O_FLOG = 4 * FOX_W
O_DQ = O_FLOG + FOX_HEADS
O_DK, O_DV, O_DG = O_DQ + DIL_W, O_DQ + 2 * DIL_W, O_DQ + 3 * DIL_W
O_MQ = O_DQ + 4 * DIL_W
O_MG = O_MQ + MEM_W
P_FOX = 0
P_FG = P_FOX + 3 * FOX_W
P_DIL = P_FG + FOX_W
P_DG = P_DIL + 3 * DIL_W
P_MQ = P_DG + DIL_W
P_MG = P_MQ + MEM_W
P_FLOG = P_MG + MEM_W
PW = P_FLOG + LANES

VMEM_LIMIT = 56 * 1024 * 1024


def _pack_pieces():
    pieces = []
    for base in (O_FQ, O_DQ):
        seg = []
        for hp in range(FOX_HEADS // 2):
            for part in range(3):
                seg.append((base + part * FOX_W + hp * LANES, LANES))
        pieces.append(seg)
    fox, dil = pieces
    return fox + [(O_FG, FOX_W)] + dil + [(O_DG, DIL_W), (O_MQ, MEM_W), (O_MG, MEM_W), (O_FLOG, FOX_HEADS)]


def _pack_cols(w):
    parts = [w[..., s:s + n] for s, n in _pack_pieces()]
    parts.append(jnp.zeros(w.shape[:-1] + (LANES - FOX_HEADS,), w.dtype))
    return jnp.concatenate(parts, axis=-1)


def _unpack_cols(g):
    runs = []
    pos = 0
    for s, n in _pack_pieces():
        runs.append((s, n, pos))
        pos += n
    runs.sort()
    return jnp.concatenate([g[..., p:p + n] for s, n, p in runs], axis=-1)


def _params(sem=None, **kw):
    return pltpu.CompilerParams(dimension_semantics=sem, vmem_limit_bytes=VMEM_LIMIT, **kw)


def _mesh_pos():
    return lax.axis_index("x"), lax.axis_index("y"), lax.axis_index("c")


def _flip(v, d):
    return 1 - v if d else v


_RELATIONS = [(dx, dy, dc) for dx in (0, 1) for dy in (0, 1) for dc in (0, 1)][1:]


def weight_gather(shards):
    n_arr = len(shards)
    rows = [s.shape[0] for s in shards]

    def body(*refs):
        in_refs = refs[:n_arr]
        out_refs = refs[n_arr:2 * n_arr]
        send_sems, recv_sems, local_sems = refs[2 * n_arr:]
        x, y, c = _mesh_pos()
        me, sibling = (x, y, c), (x, y, 1 - c)
        x_nbr, y_nbr, diag = (1 - x, y, c), (x, 1 - y, c), (1 - x, 1 - y, c)
        north = c == 1
        relay_from = (jnp.where(north, 1 - x, x), jnp.where(north, y, 1 - y), c)
        relay_to = (jnp.where(north, x, 1 - x), jnp.where(north, 1 - y, y), c)
        k_from = jnp.where(north, 1, 2)
        k_to = 3 - k_from

        def block(a, pos):
            px, py, pc = pos
            return out_refs[a].at[pl.ds((4 * px + 2 * py + pc) * rows[a], rows[a]), :]

        def copy(a, k, blk, to, src=None):
            return pltpu.make_async_remote_copy(
                src_ref=block(a, blk) if src is None else src, dst_ref=block(a, blk),
                send_sem=send_sems.at[a, k], recv_sem=recv_sems.at[a, k],
                device_id=to, device_id_type=pl.DeviceIdType.MESH)

        started = []
        mine = []
        for a in range(n_arr):
            cp = pltpu.make_async_copy(in_refs[a], block(a, me), local_sems.at[a])
            cp.start()
            mine.append(cp)
            first = [copy(a, 0, me, sibling, src=in_refs[a]), copy(a, 1, me, x_nbr, src=in_refs[a]),
                     copy(a, 2, me, y_nbr, src=in_refs[a])]
            for cp in first:
                cp.start()
            started += first
        for a in range(n_arr):
            copy(a, k_from, relay_from, me).wait_recv()
            second_hop = copy(a, 3, relay_from, relay_to)
            second_hop.start()
            passed = copy(a, 3 + k_from, relay_from, sibling)
            passed.start()
            started += [second_hop, passed]
        for a in range(n_arr):
            copy(a, k_to, relay_to, me).wait_recv()
            passed = copy(a, 3 + k_to, relay_to, sibling)
            passed.start()
            started.append(passed)
        for a in range(n_arr):
            copy(a, 3, diag, me).wait_recv()
            passed = copy(a, 6, diag, sibling)
            passed.start()
            started.append(passed)
        for a in range(n_arr):
            copy(a, 0, sibling, me).wait_recv()
            for k, chip in ((4, x_nbr), (5, y_nbr), (6, diag)):
                copy(a, k, (chip[0], chip[1], 1 - c), me).wait_recv()
        for cp in started:
            cp.wait_send()
        for cp in mine:
            cp.wait()

    any_spec = pl.BlockSpec(memory_space=pl.ANY)
    return pl.pallas_call(
        body, name="weight_gather",
        out_shape=[jax.ShapeDtypeStruct((N_DEV * s.shape[0], s.shape[1]), s.dtype) for s in shards],
        in_specs=[any_spec] * n_arr, out_specs=[any_spec] * n_arr,
        scratch_shapes=[pltpu.SemaphoreType.DMA((n_arr, 7)), pltpu.SemaphoreType.DMA((n_arr, 7)),
                        pltpu.SemaphoreType.DMA((n_arr,))],
    )(*shards)


def grad_exchange(grads, small):
    arrs = list(grads) + [small]
    n_arr = len(arrs)
    rows = [g.shape[0] // N_DEV for g in grads] + [small.shape[0]]

    def body(*refs):
        in_refs = refs[:n_arr]
        out_refs = refs[n_arr:2 * n_arr]
        send_sems, recv_sems, local_sems = refs[2 * n_arr:]
        x, y, c = _mesh_pos()
        me = 4 * x + 2 * y + c

        def src(a, idx):
            if a == n_arr - 1:
                return in_refs[a]
            return in_refs[a].at[pl.ds(idx * rows[a], rows[a]), :]

        def copy(a, k):
            dx, dy, dc = _RELATIONS[k]
            px, py, pc = _flip(x, dx), _flip(y, dy), _flip(c, dc)
            peer = 4 * px + 2 * py + pc
            send = pltpu.make_async_remote_copy(
                src_ref=src(a, peer), dst_ref=out_refs[a].at[me],
                send_sem=send_sems.at[a, k], recv_sem=recv_sems.at[a, k],
                device_id=(px, py, pc), device_id_type=pl.DeviceIdType.MESH)
            recv = pltpu.make_async_remote_copy(
                src_ref=src(a, peer), dst_ref=out_refs[a].at[peer],
                send_sem=send_sems.at[a, k], recv_sem=recv_sems.at[a, k],
                device_id=(px, py, pc), device_id_type=pl.DeviceIdType.MESH)
            return send, recv

        mine = []
        pairs = []
        for a in range(n_arr):
            cp = pltpu.make_async_copy(src(a, me), out_refs[a].at[me], local_sems.at[a])
            cp.start()
            mine.append(cp)
            for k in range(7):
                send, recv = copy(a, k)
                send.start()
                pairs.append((send, recv))
        for send, recv in pairs:
            recv.wait_recv()
        for send, recv in pairs:
            send.wait_send()
        for cp in mine:
            cp.wait()

    any_spec = pl.BlockSpec(memory_space=pl.ANY)
    return pl.pallas_call(
        body, name="grad_exchange",
        out_shape=[jax.ShapeDtypeStruct((N_DEV, r, a.shape[1]), a.dtype) for r, a in zip(rows, arrs)],
        in_specs=[any_spec] * n_arr, out_specs=[any_spec] * n_arr,
        scratch_shapes=[pltpu.SemaphoreType.DMA((n_arr, 7)), pltpu.SemaphoreType.DMA((n_arr, 7)),
                        pltpu.SemaphoreType.DMA((n_arr,))],
    )(*arrs)


def slot_sum(slots, tr, name):
    _, R, C = slots.shape

    def body(s_ref, o_ref):
        acc = s_ref[0]
        for d in range(1, N_DEV):
            acc = acc + s_ref[d]
        o_ref[...] = acc

    return pl.pallas_call(
        body, name=name, grid=(R // tr,),
        in_specs=[pl.BlockSpec((N_DEV, tr, C), lambda i: (0, i, 0))],
        out_specs=pl.BlockSpec((tr, C), lambda i: (i, 0)),
        out_shape=jax.ShapeDtypeStruct((R, C), slots.dtype),
        compiler_params=_params(("arbitrary",)),
    )(slots)


N_CHIP = 4
_OTHER_CHIPS = [(1, 0), (0, 1), (1, 1)]


def grad_exchange_d2d(grads, small):
    n_big = len(grads)
    rows = [g.shape[0] // N_DEV for g in grads]

    def body(*refs):
        g_refs = refs[:n_big]
        small_ref = refs[n_big]
        out_refs = refs[n_big + 1:2 * n_big + 1]
        csum_ref = refs[2 * n_big + 1]
        land, send_sems, recv_sems = refs[2 * n_big + 2:]
        x, y, c = _mesh_pos()
        sibling = (x, y, 1 - c)
        copies = []
        for a in range(n_big):
            for q in range(N_CHIP):
                copies.append(pltpu.make_async_remote_copy(
                    src_ref=g_refs[a].at[pl.ds((2 * q + 1 - c) * rows[a], rows[a]), :], dst_ref=out_refs[a].at[q],
                    send_sem=send_sems.at[a, q], recv_sem=recv_sems.at[a, q],
                    device_id=sibling, device_id_type=pl.DeviceIdType.MESH))
        copies.append(pltpu.make_async_remote_copy(
            src_ref=small_ref, dst_ref=land, send_sem=send_sems.at[n_big, 0], recv_sem=recv_sems.at[n_big, 0],
            device_id=sibling, device_id_type=pl.DeviceIdType.MESH))
        for cp in copies:
            cp.start()
        for cp in copies:
            cp.wait_recv()
        for cp in copies:
            cp.wait_send()
        csum_ref[...] = small_ref[...] + land[...]

    any_spec = pl.BlockSpec(memory_space=pl.ANY)
    vmem_spec = pl.BlockSpec(memory_space=pltpu.VMEM)
    return pl.pallas_call(
        body, name="grad_exchange_d2d",
        out_shape=[jax.ShapeDtypeStruct((N_CHIP, r, g.shape[1]), g.dtype) for r, g in zip(rows, grads)]
        + [jax.ShapeDtypeStruct(small.shape, small.dtype)],
        in_specs=[any_spec] * n_big + [vmem_spec], out_specs=[any_spec] * n_big + [vmem_spec],
        scratch_shapes=[pltpu.VMEM(small.shape, small.dtype),
                        pltpu.SemaphoreType.DMA((n_big + 1, N_CHIP)), pltpu.SemaphoreType.DMA((n_big + 1, N_CHIP))],
    )(*grads, small)


def chip_sum(g, got, tr, name):
    _, rows, cols = got.shape
    g4 = g.reshape(N_CHIP, 2, rows, cols)
    core = lax.axis_index("c").astype(jnp.int32).reshape(1)

    def body(c_ref, g_ref, r_ref, o_ref):
        o_ref[0] = (g_ref[0, 0] + r_ref[0]).astype(BF16)

    return pl.pallas_call(
        body, name=name,
        grid_spec=pltpu.PrefetchScalarGridSpec(
            num_scalar_prefetch=1, grid=(N_CHIP, rows // tr),
            in_specs=[pl.BlockSpec((1, 1, tr, cols), lambda q, i, w: (q, w[0], i, 0)),
                      pl.BlockSpec((1, tr, cols), lambda q, i, w: (q, i, 0))],
            out_specs=pl.BlockSpec((1, tr, cols), lambda q, i, w: (q, i, 0))),
        out_shape=jax.ShapeDtypeStruct((N_CHIP, rows, cols), BF16),
        compiler_params=_params(("arbitrary", "arbitrary")),
    )(core, g4, got)


def grad_exchange_ici(parts, csum):
    n_big = len(parts)

    def body(*refs):
        p_refs = refs[:n_big]
        csum_ref = refs[n_big]
        out_refs = refs[n_big + 1:2 * n_big + 1]
        tot_ref = refs[2 * n_big + 1]
        land, send_sems, recv_sems, local_sems = refs[2 * n_big + 2:]
        x, y, c = _mesh_pos()
        q_me = 2 * x + y
        land[q_me] = csum_ref[...]
        mine = [pltpu.make_async_copy(p_refs[a].at[q_me], out_refs[a].at[q_me], local_sems.at[a]) for a in range(n_big)]
        for cp in mine:
            cp.start()
        sends, recvs = [], []
        for j, (dx, dy) in enumerate(_OTHER_CHIPS):
            px, py = _flip(x, dx), _flip(y, dy)
            q_peer = 2 * px + py
            for a in range(n_big + 1):
                src = p_refs[a].at[q_peer] if a < n_big else csum_ref
                dst = out_refs[a] if a < n_big else land
                common = dict(send_sem=send_sems.at[a, j], recv_sem=recv_sems.at[a, j],
                              device_id=(px, py, c), device_id_type=pl.DeviceIdType.MESH)
                sends.append(pltpu.make_async_remote_copy(src_ref=src, dst_ref=dst.at[q_me], **common))
                recvs.append(pltpu.make_async_remote_copy(src_ref=src, dst_ref=dst.at[q_peer], **common))
        for cp in sends:
            cp.start()
        for cp in recvs:
            cp.wait_recv()
        for cp in sends:
            cp.wait_send()
        for cp in mine:
            cp.wait()
        tot = land[0]
        for q in range(1, N_CHIP):
            tot = tot + land[q]
        tot_ref[...] = tot

    any_spec = pl.BlockSpec(memory_space=pl.ANY)
    vmem_spec = pl.BlockSpec(memory_space=pltpu.VMEM)
    return pl.pallas_call(
        body, name="grad_exchange_ici",
        out_shape=[jax.ShapeDtypeStruct(p.shape, p.dtype) for p in parts] + [jax.ShapeDtypeStruct(csum.shape, csum.dtype)],
        in_specs=[any_spec] * n_big + [vmem_spec], out_specs=[any_spec] * n_big + [vmem_spec],
        scratch_shapes=[pltpu.VMEM((N_CHIP,) + csum.shape, csum.dtype),
                        pltpu.SemaphoreType.DMA((n_big + 1, 3)), pltpu.SemaphoreType.DMA((n_big + 1, 3)),
                        pltpu.SemaphoreType.DMA((n_big,))],
    )(*parts, csum)


def final_sum(got, tr, name):
    _, rows, cols = got.shape

    def body(got_ref, o_ref):
        acc = got_ref[0].astype(F32)
        for q in range(1, N_CHIP):
            acc = acc + got_ref[q].astype(F32)
        o_ref[...] = acc

    return pl.pallas_call(
        body, name=name, grid=(rows // tr,),
        in_specs=[pl.BlockSpec((N_CHIP, tr, cols), lambda i: (0, i, 0))],
        out_specs=pl.BlockSpec((tr, cols), lambda i: (i, 0)),
        out_shape=jax.ShapeDtypeStruct((rows, cols), F32),
        compiler_params=_params(("arbitrary",)),
    )(got)


_HBM = pl.BlockSpec(memory_space=pltpu.HBM)
_SEM = pl.BlockSpec(memory_space=pltpu.SEMAPHORE)
_EFFECT = pltpu.SideEffectType.DATAFLOW_SIDE_EFFECTING


def _old_early_copies(src_refs, land_refs, send_sems, recv_sems, rows):
    x, y, c = _mesh_pos()
    me = 4 * x + 2 * y + c
    copies = []
    for a in range(len(src_refs)):
        for k, (dx, dy, dc) in enumerate(_RELATIONS):
            px, py, pc = _flip(x, dx), _flip(y, dy), _flip(c, dc)
            peer = 4 * px + 2 * py + pc
            copies.append(pltpu.make_async_remote_copy(
                src_ref=src_refs[a].at[pl.ds(peer * rows[a], rows[a]), :], dst_ref=land_refs[a].at[me],
                send_sem=send_sems.at[a, k], recv_sem=recv_sems.at[a, k],
                device_id=(px, py, pc), device_id_type=pl.DeviceIdType.MESH))
    return copies


def _old_early_exchange_start(srcs, name):
    n = len(srcs)
    rows = [s.shape[0] // N_DEV for s in srcs]
    lands = [lax.empty((N_DEV, r, s.shape[1]), s.dtype) for r, s in zip(rows, srcs)]

    def body(*refs):
        src_refs, land_refs = refs[:n], refs[n:2 * n]
        send_sems, recv_sems = refs[2 * n], refs[2 * n + 1]
        token = refs[-1]
        for cp in _early_copies(src_refs, land_refs, send_sems, recv_sems, rows):
            cp.start()
        token[...] = jnp.zeros_like(token)

    hbm = lambda a: pltpu.HBM(a.shape, a.dtype)
    outs = pl.pallas_call(
        body, name=name,
        out_shape=[pltpu.SemaphoreType.DMA((n, 7)), pltpu.SemaphoreType.DMA((n, 7))]
        + [hbm(a) for a in srcs] + [hbm(a) for a in lands] + [jax.ShapeDtypeStruct((8, LANES), F32)],
        in_specs=[_HBM] * (2 * n),
        out_specs=[_SEM, _SEM] + [_HBM] * (2 * n) + [pl.BlockSpec(memory_space=pltpu.VMEM)],
        input_output_aliases={i: 2 + i for i in range(2 * n)},
        compiler_params=pltpu.CompilerParams(has_side_effects=_EFFECT),
    )(*[pltpu.with_memory_space_constraint(a, pltpu.HBM) for a in list(srcs) + lands])
    return dict(sems=outs[:2], srcs=outs[2:2 + n], lands=outs[2 + n:2 + 2 * n], rows=rows), outs[-1]


def _old_early_exchange_wait(handle, after, name):
    n = len(handle["srcs"])
    rows = handle["rows"]

    def body(*refs):
        src_refs, land_refs = refs[:n], refs[n:2 * n]
        send_sems, recv_sems = refs[2 * n], refs[2 * n + 1]
        for cp in _early_copies(src_refs, land_refs, send_sems, recv_sems, rows):
            cp.wait_send()
            cp.wait_recv()

    hbm = lambda a: pltpu.HBM(a.shape, a.dtype)
    ins = list(handle["srcs"]) + list(handle["lands"])
    outs = pl.pallas_call(
        body, name=name,
        out_shape=[hbm(a) for a in ins],
        in_specs=[_HBM] * (2 * n) + [_SEM, _SEM, pl.BlockSpec(memory_space=pl.ANY)],
        out_specs=[_HBM] * (2 * n),
        input_output_aliases={i: i for i in range(2 * n)},
        compiler_params=pltpu.CompilerParams(has_side_effects=_EFFECT),
    )(*ins, *handle["sems"], after)
    return outs[:n], outs[n:]


def _old_slot_sum8(src, land, tr, name):
    _, rows, cols = land.shape
    x, y, c = _mesh_pos()
    me = (4 * x + 2 * y + c).astype(jnp.int32).reshape(1)

    def body(me_ref, src_ref, land_ref, o_ref):
        acc = None
        for d in range(N_DEV):
            term = jnp.where(d == me_ref[0], src_ref[0], land_ref[d]).astype(F32)
            acc = term if acc is None else acc + term
        o_ref[...] = acc

    return pl.pallas_call(
        body, name=name,
        grid_spec=pltpu.PrefetchScalarGridSpec(
            num_scalar_prefetch=1, grid=(rows // tr,),
            in_specs=[pl.BlockSpec((1, tr, cols), lambda i, w: (w[0], i, 0)),
                      pl.BlockSpec((N_DEV, tr, cols), lambda i, w: (0, i, 0))],
            out_specs=pl.BlockSpec((tr, cols), lambda i, w: (i, 0))),
        out_shape=jax.ShapeDtypeStruct((rows, cols), F32),
        compiler_params=_params(("arbitrary",)),
    )(me, src.reshape(N_DEV, rows, cols), land)


def _old2_early_copies(src_refs, land_refs, send_sems, recv_sems, rows):
    x, y, c = _mesh_pos()
    me = 4 * x + 2 * y + c
    copies = []
    for a in range(len(src_refs)):
        for dx, dy, dc in _RELATIONS:
            px, py, pc = _flip(x, dx), _flip(y, dy), _flip(c, dc)
            peer = 4 * px + 2 * py + pc
            copies.append(pltpu.make_async_remote_copy(
                src_ref=src_refs[a].at[pl.ds(peer * rows[a], rows[a]), :],
                dst_ref=land_refs[a].at[pl.ds(me * rows[a], rows[a]), :],
                send_sem=send_sems[a], recv_sem=recv_sems[a],
                device_id=(px, py, pc), device_id_type=pl.DeviceIdType.MESH))
    return copies


def _old2_early_exchange_start(srcs, name):
    n = len(srcs)
    rows = [s.shape[0] // N_DEV for s in srcs]
    lands = [lax.empty(s.shape, s.dtype) for s in srcs]

    def body(*refs):
        src_refs, land_refs = refs[:n], refs[n:2 * n]
        send_sems, recv_sems = refs[2 * n:3 * n], refs[3 * n:4 * n]
        token = refs[-1]
        for cp in _early_copies(src_refs, land_refs, send_sems, recv_sems, rows):
            cp.start()
        token[...] = jnp.zeros_like(token)

    hbm = lambda a: pltpu.HBM(a.shape, a.dtype)
    outs = pl.pallas_call(
        body, name=name,
        out_shape=[pltpu.SemaphoreType.DMA(())] * (2 * n)
        + [hbm(a) for a in srcs] + [hbm(a) for a in lands] + [jax.ShapeDtypeStruct((8, LANES), F32)],
        in_specs=[_HBM] * (2 * n),
        out_specs=[_SEM] * (2 * n) + [_HBM] * (2 * n) + [pl.BlockSpec(memory_space=pltpu.VMEM)],
        input_output_aliases={i: 2 * n + i for i in range(2 * n)},
        compiler_params=pltpu.CompilerParams(has_side_effects=_EFFECT),
    )(*[pltpu.with_memory_space_constraint(a, pltpu.HBM) for a in list(srcs) + lands])
    return dict(sems=outs[:2 * n], srcs=outs[2 * n:3 * n], lands=outs[3 * n:4 * n], rows=rows), outs[-1]


def _early_copies(src_refs, land_refs, send_sems, recv_sems, rows, gather):
    x, y, c = _mesh_pos()
    me = 4 * x + 2 * y + c
    copies = []
    for a in range(len(src_refs)):
        for dx, dy, dc in _RELATIONS:
            px, py, pc = _flip(x, dx), _flip(y, dy), _flip(c, dc)
            peer = 4 * px + 2 * py + pc
            copies.append(pltpu.make_async_remote_copy(
                src_ref=src_refs[a] if gather else src_refs[a].at[pl.ds(peer * rows[a], rows[a]), :],
                dst_ref=land_refs[a].at[pl.ds(me * rows[a], rows[a]), :],
                send_sem=send_sems[a], recv_sem=recv_sems[a],
                device_id=(px, py, pc), device_id_type=pl.DeviceIdType.MESH))
    return copies


def early_exchange_start(srcs, name, gather=False, after=None):
    n = len(srcs)
    if gather:
        rows = [s.shape[0] for s in srcs]
        me = 4 * lax.axis_index("x") + 2 * lax.axis_index("y") + lax.axis_index("c")
        lands = [lax.dynamic_update_slice(lax.empty((N_DEV * r, s.shape[1]), s.dtype), s, (me * r, 0))
                 for r, s in zip(rows, srcs)]
    else:
        rows = [s.shape[0] // N_DEV for s in srcs]
        lands = [lax.empty(s.shape, s.dtype) for s in srcs]

    extra = [] if after is None else [after]

    def body(*refs):
        src_refs, land_refs = refs[:n], refs[n:2 * n]
        first_sem = 2 * n + len(extra)
        send_sems, recv_sems = refs[first_sem:first_sem + n], refs[first_sem + n:first_sem + 2 * n]
        token = refs[-1]
        for cp in _early_copies(src_refs, land_refs, send_sems, recv_sems, rows, gather):
            cp.start()
        token[...] = jnp.zeros_like(token)

    hbm = lambda a: pltpu.HBM(a.shape, a.dtype)
    outs = pl.pallas_call(
        body, name=name,
        out_shape=[pltpu.SemaphoreType.DMA(())] * (2 * n)
        + [hbm(a) for a in srcs] + [hbm(a) for a in lands] + [jax.ShapeDtypeStruct((8, LANES), F32)],
        in_specs=[_HBM] * (2 * n) + [pl.BlockSpec(memory_space=pl.ANY)] * len(extra),
        out_specs=[_SEM] * (2 * n) + [_HBM] * (2 * n) + [pl.BlockSpec(memory_space=pltpu.VMEM)],
        input_output_aliases={i: 2 * n + i for i in range(2 * n)},
        compiler_params=pltpu.CompilerParams(has_side_effects=_EFFECT),
    )(*[pltpu.with_memory_space_constraint(a, pltpu.HBM) for a in list(srcs) + lands], *extra)
    return dict(sems=outs[:2 * n], srcs=outs[2 * n:3 * n], lands=outs[3 * n:4 * n], rows=rows), outs[-1]


def early_exchange_wait(handle, after, name):
    n = len(handle["srcs"])
    rows = handle["rows"]

    def body(*refs):
        src_refs, land_refs = refs[:n], refs[n:2 * n]
        send_sems, recv_sems = refs[2 * n:3 * n], refs[3 * n:4 * n]
        x, y, c = _mesh_pos()
        for a in range(n):
            seven = pl.ds(0, 7 * rows[a])
            all_seven = pltpu.make_async_remote_copy(
                src_ref=land_refs[a].at[seven, :], dst_ref=land_refs[a].at[seven, :],
                send_sem=send_sems[a], recv_sem=recv_sems[a],
                device_id=(x, y, c), device_id_type=pl.DeviceIdType.MESH)
            all_seven.wait_send()
            all_seven.wait_recv()

    hbm = lambda a: pltpu.HBM(a.shape, a.dtype)
    ins = list(handle["srcs"]) + list(handle["lands"])
    outs = pl.pallas_call(
        body, name=name,
        out_shape=[hbm(a) for a in ins],
        in_specs=[_HBM] * (2 * n) + [_SEM] * (2 * n) + [pl.BlockSpec(memory_space=pl.ANY)],
        out_specs=[_HBM] * (2 * n),
        input_output_aliases={i: i for i in range(2 * n)},
        compiler_params=pltpu.CompilerParams(has_side_effects=_EFFECT),
    )(*ins, *handle["sems"], after)
    return outs[:n], outs[n:]


def slot_sum8(src, land, tr, name):
    rows, cols = land.shape[0] // N_DEV, land.shape[1]
    x, y, c = _mesh_pos()
    me = (4 * x + 2 * y + c).astype(jnp.int32).reshape(1)

    def body(me_ref, src_ref, land_ref, o_ref):
        acc = None
        for d in range(N_DEV):
            term = jnp.where(d == me_ref[0], src_ref[0], land_ref[d]).astype(F32)
            acc = term if acc is None else acc + term
        o_ref[...] = acc

    return pl.pallas_call(
        body, name=name,
        grid_spec=pltpu.PrefetchScalarGridSpec(
            num_scalar_prefetch=1, grid=(rows // tr,),
            in_specs=[pl.BlockSpec((1, tr, cols), lambda i, w: (w[0], i, 0)),
                      pl.BlockSpec((N_DEV, tr, cols), lambda i, w: (0, i, 0))],
            out_specs=pl.BlockSpec((tr, cols), lambda i, w: (i, 0))),
        out_shape=jax.ShapeDtypeStruct((rows, cols), F32),
        compiler_params=_params(("arbitrary",)),
    )(me, src.reshape(N_DEV, rows, cols), land.reshape(N_DEV, rows, cols))


def mm_tn_multi(a_t, bs, tt, name, out_dtype=F32):
    K, T = a_t.shape
    widths = [b.shape[1] for b in bs]
    steps = T // tt

    def body(a_ref, *rest):
        b_refs, o_ref, acc = rest[:-2], rest[-2], rest[-1]
        @pl.when(pl.program_id(0) == 0)
        def _():
            acc[...] = jnp.zeros(acc.shape, F32)

        av = a_ref[...]
        col = 0
        for b_ref, w in zip(b_refs, widths):
            acc[:, col:col + w] += jnp.dot(av, b_ref[...], preferred_element_type=F32)
            col += w

        @pl.when(pl.program_id(0) == steps - 1)
        def _():
            o_ref[...] = acc[...].astype(out_dtype)

    return pl.pallas_call(
        body, name=name, grid=(steps,),
        in_specs=[pl.BlockSpec((K, tt), lambda t: (0, t))] + [pl.BlockSpec((tt, w), lambda t: (t, 0)) for w in widths],
        out_specs=pl.BlockSpec((K, sum(widths)), lambda t: (0, 0)),
        out_shape=jax.ShapeDtypeStruct((K, sum(widths)), out_dtype),
        scratch_shapes=[pltpu.VMEM((K, sum(widths)), F32)],
        compiler_params=_params(("arbitrary",)),
    )(a_t, *bs)


def rms_fwd(x, g, tm, name, with_transpose=False):
    M, K = x.shape

    def body(x_ref, g_ref, o_ref, *t_ref):
        xv = x_ref[...]
        r = lax.rsqrt(jnp.mean(xv * xv, axis=-1, keepdims=True) + RMS_EPS)
        h = ((xv * r) * g_ref[...]).astype(BF16)
        o_ref[...] = h
        if with_transpose:
            t_ref[0][...] = h.T

    out_specs = [pl.BlockSpec((tm, K), lambda i: (i, 0))]
    out_shape = [jax.ShapeDtypeStruct((M, K), BF16)]
    if with_transpose:
        out_specs.append(pl.BlockSpec((K, tm), lambda i: (0, i)))
        out_shape.append(jax.ShapeDtypeStruct((K, M), BF16))
    outs = pl.pallas_call(
        body, name=name, grid=(M // tm,),
        in_specs=[pl.BlockSpec((tm, K), lambda i: (i, 0)), pl.BlockSpec((1, K), lambda i: (0, 0))],
        out_specs=out_specs, out_shape=out_shape,
        compiler_params=_params(("arbitrary",)),
    )(x, g)
    return outs if with_transpose else outs[0]


def rms_bwd(x, g, dh, dres, tm, name):
    M, K = x.shape
    has_res = dres is not None

    def body(*refs):
        if has_res:
            x_ref, g_ref, dh_ref, dres_ref, dx_ref, dg_ref = refs
        else:
            x_ref, g_ref, dh_ref, dx_ref, dg_ref = refs
        xv = x_ref[...]
        r = lax.rsqrt(jnp.mean(xv * xv, axis=-1, keepdims=True) + RMS_EPS)
        xn = xv * r
        dhv = dh_ref[...]
        dxn = dhv * g_ref[...]
        dx = r * (dxn - xn * jnp.mean(dxn * xn, axis=-1, keepdims=True))
        if has_res:
            dx = dx + dres_ref[...]
        dx_ref[...] = dx
        part = jnp.sum(dhv * xn, axis=0, keepdims=True)
        row = lax.broadcasted_iota(jnp.int32, (8, K), 0)
        upd = jnp.where(row == 0, part, 0.0)

        @pl.when(pl.program_id(0) == 0)
        def _():
            dg_ref[...] = upd

        @pl.when(pl.program_id(0) != 0)
        def _():
            dg_ref[...] += upd

    row_spec = pl.BlockSpec((tm, K), lambda i: (i, 0))
    ins = [x, g, dh] + ([dres] if has_res else [])
    in_specs = [row_spec, pl.BlockSpec((1, K), lambda i: (0, 0)), row_spec] + ([row_spec] if has_res else [])
    return pl.pallas_call(
        body, name=name, grid=(M // tm,),
        in_specs=in_specs,
        out_specs=[row_spec, pl.BlockSpec((8, K), lambda i: (0, 0))],
        out_shape=[jax.ShapeDtypeStruct((M, K), F32), jax.ShapeDtypeStruct((8, K), F32)],
        compiler_params=_params(("arbitrary",)),
    )(*ins)


def mm_nn(a, b, tm, tn, name):
    M, K = a.shape
    N = b.shape[1]

    def body(a_ref, b_ref, o_ref):
        o_ref[...] = jnp.dot(a_ref[...], b_ref[...], preferred_element_type=F32)

    return pl.pallas_call(
        body, name=name, grid=(N // tn, M // tm),
        in_specs=[pl.BlockSpec((tm, K), lambda j, i: (i, 0)), pl.BlockSpec((K, tn), lambda j, i: (0, j))],
        out_specs=pl.BlockSpec((tm, tn), lambda j, i: (i, j)),
        out_shape=jax.ShapeDtypeStruct((M, N), F32),
        compiler_params=_params(("arbitrary", "arbitrary")),
    )(a, b)


def mm_nt(a, b, tm, tk, name):
    M, K = a.shape
    N = b.shape[0]

    def body(a_ref, b_ref, o_ref):
        part = lax.dot_general(a_ref[...], b_ref[...], (((1,), (1,)), ((), ())), preferred_element_type=F32)

        @pl.when(pl.program_id(1) == 0)
        def _():
            o_ref[...] = part

        @pl.when(pl.program_id(1) != 0)
        def _():
            o_ref[...] += part

    return pl.pallas_call(
        body, name=name, grid=(M // tm, K // tk),
        in_specs=[pl.BlockSpec((tm, tk), lambda i, k: (i, k)), pl.BlockSpec((N, tk), lambda i, k: (0, k))],
        out_specs=pl.BlockSpec((tm, N), lambda i, k: (i, 0)),
        out_shape=jax.ShapeDtypeStruct((M, N), F32),
        compiler_params=_params(("arbitrary", "arbitrary")),
    )(a, b)


def mm_tn(a, b, tt, tn, name):
    T, K = a.shape
    N = b.shape[1]

    def body(a_ref, b_ref, o_ref):
        part = lax.dot_general(a_ref[...], b_ref[...], (((0,), (0,)), ((), ())), preferred_element_type=F32)

        @pl.when(pl.program_id(1) == 0)
        def _():
            o_ref[...] = part

        @pl.when(pl.program_id(1) != 0)
        def _():
            o_ref[...] += part

    return pl.pallas_call(
        body, name=name, grid=(N // tn, T // tt),
        in_specs=[pl.BlockSpec((tt, K), lambda j, t: (t, 0)), pl.BlockSpec((tt, tn), lambda j, t: (t, j))],
        out_specs=pl.BlockSpec((K, tn), lambda j, t: (0, j)),
        out_shape=jax.ShapeDtypeStruct((K, N), F32),
        compiler_params=_params(("arbitrary", "arbitrary")),
    )(a, b)


def _old_mm_tn_multi(a, bs, tt, name):
    T, K = a.shape
    widths = [b.shape[1] for b in bs]

    def body(a_ref, *rest):
        b_refs, o_ref = rest[:-1], rest[-1]
        av = a_ref[...]
        parts = [lax.dot_general(av, b_ref[...], (((0,), (0,)), ((), ())), preferred_element_type=F32)
                 for b_ref in b_refs]

        @pl.when(pl.program_id(0) == 0)
        def _():
            col = 0
            for part, w in zip(parts, widths):
                o_ref[:, col:col + w] = part
                col += w

        @pl.when(pl.program_id(0) != 0)
        def _():
            col = 0
            for part, w in zip(parts, widths):
                o_ref[:, col:col + w] += part
                col += w

    return pl.pallas_call(
        body, name=name, grid=(T // tt,),
        in_specs=[pl.BlockSpec((tt, K), lambda t: (t, 0))] + [pl.BlockSpec((tt, w), lambda t: (t, 0)) for w in widths],
        out_specs=pl.BlockSpec((K, sum(widths)), lambda t: (0, 0)),
        out_shape=jax.ShapeDtypeStruct((K, sum(widths)), F32),
        compiler_params=_params(("arbitrary",)),
    )(a, *bs)


def mm_nt_multi(pieces, w, tm, name):
    M = pieces[0][0].shape[0]
    N, K = w.shape

    def body(*refs):
        p_refs, w_ref, o_ref = refs[:-2], refs[-2], refs[-1]
        acc = None
        for p_ref, (arr, col) in zip(p_refs, pieces):
            part = lax.dot_general(p_ref[...], w_ref[:, col:col + arr.shape[1]], (((1,), (1,)), ((), ())),
                                   preferred_element_type=F32)
            acc = part if acc is None else acc + part
        o_ref[...] = acc

    return pl.pallas_call(
        body, name=name, grid=(M // tm,),
        in_specs=[pl.BlockSpec((tm, arr.shape[1]), lambda i: (i, 0)) for arr, _ in pieces]
        + [pl.BlockSpec((N, K), lambda i: (0, 0))],
        out_specs=pl.BlockSpec((tm, N), lambda i: (i, 0)),
        out_shape=jax.ShapeDtypeStruct((M, N), F32),
        compiler_params=_params(("arbitrary",)),
    )(*[arr for arr, _ in pieces], w)


def in_proj_bwd_rms(pieces, w, x, g, dres, tm):
    M, N = x.shape

    def body(*refs):
        n = len(pieces)
        p_refs, w_ref, x_ref, g_ref, dres_ref, dx_ref, dg_ref = refs[:n], *refs[n:]
        dh = None
        for p_ref, (arr, col) in zip(p_refs, pieces):
            part = lax.dot_general(p_ref[...], w_ref[:, col:col + arr.shape[1]], (((1,), (1,)), ((), ())),
                                   preferred_element_type=F32)
            dh = part if dh is None else dh + part
        xv = x_ref[...]
        r = lax.rsqrt(jnp.mean(xv * xv, axis=-1, keepdims=True) + RMS_EPS)
        xn = xv * r
        dxn = dh * g_ref[...]
        dx_ref[...] = r * (dxn - xn * jnp.mean(dxn * xn, axis=-1, keepdims=True)) + dres_ref[...]
        row = lax.broadcasted_iota(jnp.int32, (8, N), 0)
        upd = jnp.where(row == 0, jnp.sum(dh * xn, axis=0, keepdims=True), 0.0)

        @pl.when(pl.program_id(0) == 0)
        def _():
            dg_ref[...] = upd

        @pl.when(pl.program_id(0) != 0)
        def _():
            dg_ref[...] += upd

    row_spec = pl.BlockSpec((tm, N), lambda i: (i, 0))
    return pl.pallas_call(
        body, name="in_proj_bwd", grid=(M // tm,),
        in_specs=[pl.BlockSpec((tm, arr.shape[1]), lambda i: (i, 0)) for arr, _ in pieces]
        + [pl.BlockSpec(w.shape, lambda i: (0, 0)), row_spec, pl.BlockSpec((1, N), lambda i: (0, 0)), row_spec],
        out_specs=[row_spec, pl.BlockSpec((8, N), lambda i: (0, 0))],
        out_shape=[jax.ShapeDtypeStruct((M, N), F32), jax.ShapeDtypeStruct((8, N), F32)],
        compiler_params=_params(("arbitrary",)),
    )(*[arr for arr, _ in pieces], w, x, g, dres)


def _log_sigmoid(z):
    return jnp.minimum(z, 0.0) - jnp.log(1.0 + jnp.exp(-jnp.abs(z)))


def _tri(n, lower):
    r = lax.broadcasted_iota(jnp.int32, (n, n), 0)
    c = lax.broadcasted_iota(jnp.int32, (n, n), 1)
    return jnp.where((r >= c) if lower else (r <= c), 1.0, 0.0).astype(F32)


def fox_gate(proj3, b_pad):
    B, S, _ = proj3.shape
    nblk = S // TK

    def body(f_ref, b_ref, o_ref):
        tri = _tri(TK, True)
        carry = jnp.zeros((1, LANES), F32)
        for n in range(nblk):
            z = f_ref[0, n * TK:(n + 1) * TK, :] + b_ref[...]
            logf = _log_sigmoid(z)
            cs = jnp.dot(tri, logf, preferred_element_type=F32, precision=lax.Precision.HIGHEST) + carry
            carry = cs[TK - 1:TK, :]
            o_ref[0, n * TK:(n + 1) * TK, :] = -cs

    return pl.pallas_call(
        body, name="fox_gate", grid=(B,),
        in_specs=[pl.BlockSpec((1, S, LANES), lambda b: (b, 0, P_FLOG // LANES)),
                  pl.BlockSpec((1, LANES), lambda b: (0, 0))],
        out_specs=pl.BlockSpec((1, S, LANES), lambda b: (b, 0, 0)),
        out_shape=jax.ShapeDtypeStruct((B, S, LANES), F32),
        compiler_params=_params(("arbitrary",)),
    )(proj3, b_pad)


def fox_gate_bwd(drow, dneg, proj3, b_pad):
    B, S, _ = proj3.shape
    nblk = S // TK

    def body(d_ref, r_ref, f_ref, b_ref, o_ref, db_ref):
        tri = _tri(TK, False)
        lane = lax.broadcasted_iota(jnp.int32, (TK, LANES), 1)
        carry = jnp.zeros((1, LANES), F32)
        dbsum = jnp.zeros((1, LANES), F32)
        for n in reversed(range(nblk)):
            dk_side = None
            for hp in range(FOX_HEADS // 2):
                two = jnp.where(lane < 2, r_ref[0, n * TK:(n + 1) * TK, hp * LANES:(hp + 1) * LANES], 0.0)
                two = pltpu.roll(two, 2 * hp, 1) if hp else two
                dk_side = two if dk_side is None else dk_side + two
            dc = jnp.where(lane < FOX_HEADS, d_ref[0, :, n * TK:(n + 1) * TK].T - dk_side, 0.0)
            rs = jnp.dot(tri, dc, preferred_element_type=F32, precision=lax.Precision.HIGHEST) + carry
            carry = rs[0:1, :]
            z = f_ref[0, n * TK:(n + 1) * TK, :] + b_ref[...]
            dz = rs * (1.0 / (1.0 + jnp.exp(z)))
            o_ref[0, n * TK:(n + 1) * TK, :] = dz.astype(BF16)
            dbsum = dbsum + jnp.sum(dz, axis=0, keepdims=True)
        row = lax.broadcasted_iota(jnp.int32, (8, LANES), 0)
        upd = jnp.where(row == 0, dbsum, 0.0)

        @pl.when(pl.program_id(0) == 0)
        def _():
            db_ref[...] = upd

        @pl.when(pl.program_id(0) != 0)
        def _():
            db_ref[...] += upd

    return pl.pallas_call(
        body, name="fox_gate_bwd", grid=(B,),
        in_specs=[pl.BlockSpec((1, LANES, S), lambda b: (b, 0, 0)),
                  pl.BlockSpec((1, S, FOX_W), lambda b: (b, 0, 0)),
                  pl.BlockSpec((1, S, LANES), lambda b: (b, 0, P_FLOG // LANES)),
                  pl.BlockSpec((1, LANES), lambda b: (0, 0))],
        out_specs=[pl.BlockSpec((1, S, LANES), lambda b: (b, 0, 0)), pl.BlockSpec((8, LANES), lambda b: (0, 0))],
        out_shape=[jax.ShapeDtypeStruct((B, S, LANES), BF16), jax.ShapeDtypeStruct((8, LANES), F32)],
        compiler_params=_params(("arbitrary",)),
    )(drow, dneg, proj3, b_pad)


def _mult_masks(S, kind):
    nd = S // TQ
    a = np.arange(TQ)[:, None]
    b = np.arange(TK)[None, :]
    out = np.zeros((nd, TQ, TK), np.float32)
    for d in range(nd):
        delta = d * TQ + a - b
        if kind == "causal":
            out[d] = delta >= 0
        else:
            m = np.zeros((TQ, TK), np.float32)
            for w, dil in DILATIONS:
                m += (delta >= 0) & (delta % dil == 0) & (delta <= w)
            out[d] = m
    return jnp.asarray(out)


def _rope_tables(S):
    half = ROPE_DIM // 2
    f32 = np.float32
    pos = np.arange(S, dtype=f32)
    inv_freq = f32(1.0) / np.power(f32(ROPE_THETA), np.arange(0, ROPE_DIM, 2, dtype=f32) / f32(ROPE_DIM)).astype(f32)
    ang = (pos[:, None] * inv_freq[None, :]).astype(f32).astype(np.float64)
    cos, sin = np.cos(ang).astype(f32), np.sin(ang).astype(f32)
    one = np.ones((S, HEAD_DIM - ROPE_DIM), f32)
    zero = np.zeros((S, HEAD_DIM - ROPE_DIM), f32)
    zh = np.zeros((S, half), f32)
    c = np.concatenate([cos, cos, one], axis=1)
    s1 = np.concatenate([-sin, zh, zero], axis=1)
    s2 = np.concatenate([zh, sin, zero], axis=1)
    return tuple(jnp.asarray(np.concatenate([t, t], axis=1)) for t in (c, s1, s2))


def _rope(t, c, s1, s2):
    return t * c + pltpu.roll(t, LANES - half_rope(), 1) * s1 + pltpu.roll(t, half_rope(), 1) * s2


def half_rope():
    return ROPE_DIM // 2


def _rope_bwd(d, c, s1, s2):
    return d * c + pltpu.roll(d * s1, half_rope(), 1) + pltpu.roll(d * s2, LANES - half_rope(), 1)


def _scale_parts(scale):
    m, _ = math.frexp(scale)
    return (scale, None) if m == 0.5 else (None, scale)


def attn_fwd(kind, src, S, *, negc=None, mask=None, rope=None, kv=None):
    B = src.shape[0]
    pair = kind != "mem"
    col0 = {"fox": P_FOX, "dil": P_DIL, "mem": P_MQ}[kind]
    n_blocks = FOX_HEADS // 2 if pair else MEM_HEADS
    e_dim = HEAD_DIM if pair else MEM_HEAD_DIM
    q_fold, s_scale = _scale_parts(1.0 / math.sqrt(e_dim))
    Sk = S if pair else MEM_LEN
    nh = 2 if pair else 1
    has_bias = negc is not None
    has_rope = rope is not None
    nq = S // TQ

    def body(*refs):
        refs = list(refs)
        if pair:
            qkv_ref = refs.pop(0)
        else:
            q_ref, k_ref, v_ref = refs.pop(0), refs.pop(0), refs.pop(0)
        negc_ref = refs.pop(0) if has_bias else None
        mask_ref = refs.pop(0) if pair else None
        rope_refs = [refs.pop(0) for _ in range(3)] if has_rope else None
        o_ref, lse_ref, qs, ks, vs = refs
        lane = lax.broadcasted_iota(jnp.int32, (1, LANES), 1)

        def prep_q(n, _):
            r0 = pl.multiple_of(n * TQ, TQ)
            rows = pl.ds(r0, TQ)
            q = qkv_ref[0, rows, 0:LANES] if pair else q_ref[0, rows, :]
            if has_rope:
                q = _rope(q, *[t[rows, :] for t in rope_refs])
            if q_fold is not None:
                q = q * q_fold
            qs[rows, :] = q.astype(BF16)
            return 0

        def prep_kv(n, _):
            r0 = pl.multiple_of(n * TK, TK)
            rows = pl.ds(r0, TK)
            k = qkv_ref[0, rows, LANES:2 * LANES] if pair else k_ref[0, rows, :]
            v = qkv_ref[0, rows, 2 * LANES:3 * LANES] if pair else v_ref[0, rows, :]
            if has_rope:
                k = _rope(k, *[t[rows, :] for t in rope_refs])
            ks[rows, :] = k.astype(BF16)
            vs[rows, :] = v.astype(BF16)
            return 0

        lax.fori_loop(0, nq, prep_q, 0)
        lax.fori_loop(0, Sk // TK, prep_kv, 0)

        def q_loop(i, _):
            r0 = pl.multiple_of(i * TQ, TQ)
            q = qs[pl.ds(r0, TQ), :]
            res = []
            for hh in range(nh):
                hmask = (lane >= HEAD_DIM * hh) & (lane < HEAD_DIM * (hh + 1))
                qh = jnp.where(hmask, q, jnp.zeros_like(q)) if pair else q

                def kv_loop(j, carry, qh=qh, hh=hh):
                    m, l, acc = carry
                    c0 = pl.multiple_of(j * TK, TK)
                    k = ks[pl.ds(c0, TK), :]
                    v = vs[pl.ds(c0, TK), :]
                    s = lax.dot_general(qh, k, (((1,), (1,)), ((), ())), preferred_element_type=F32)
                    if s_scale is not None:
                        s = s * s_scale
                    if has_bias:
                        s = s + negc_ref[0, 0, pl.ds(hh, 1), pl.ds(c0, TK)]
                    if pair:
                        mult = mask_ref[i - j]
                        s = jnp.where(mult > 0.0, s, NEG_INF)
                    m_new = jnp.maximum(m, jnp.max(s, axis=1, keepdims=True))
                    p = jnp.exp(s - m_new)
                    if pair:
                        p = p * mult
                    alpha = jnp.exp(m - m_new)
                    l = alpha * l + jnp.sum(p, axis=1, keepdims=True)
                    acc = acc * alpha + jnp.dot(p.astype(BF16), v, preferred_element_type=F32)
                    return m_new, l, acc

                init = (jnp.full((TQ, 1), NEG_INF, F32), jnp.zeros((TQ, 1), F32), jnp.zeros((TQ, LANES), F32))
                m, l, acc = lax.fori_loop(0, (i + 1) if pair else Sk // TK, kv_loop, init)
                res.append((acc / l, m + jnp.log(l)))
            if pair:
                o = jnp.where(lane < HEAD_DIM, res[0][0], res[1][0])
                lse = jnp.where(lane < HEAD_DIM, res[0][1], res[1][1])
            else:
                o = res[0][0]
                lse = jnp.broadcast_to(res[0][1], (TQ, LANES))
            o_ref[0, pl.ds(r0, TQ), :] = o
            lse_ref[0, pl.ds(r0, TQ), :] = lse
            return 0

        lax.fori_loop(0, nq, q_loop, 0)

    ins, in_specs = [], []
    if pair:
        ins.append(src)
        in_specs.append(pl.BlockSpec((1, S, PAIR_W), lambda b, h: (b, 0, col0 // PAIR_W + h)))
    else:
        ins += [src, kv, kv]
        in_specs += [pl.BlockSpec((1, S, LANES), lambda b, h: (b, 0, col0 // LANES + h)),
                     pl.BlockSpec((1, MEM_LEN, LANES), lambda b, h: (b, 0, h)),
                     pl.BlockSpec((1, MEM_LEN, LANES), lambda b, h: (b, 0, MEM_HEADS + h))]
    if has_bias:
        ins.append(negc)
        in_specs.append(pl.BlockSpec((1, 1, 2, S), lambda b, h: (b, h, 0, 0)))
    if pair:
        ins.append(mask)
        in_specs.append(pl.BlockSpec(mask.shape, lambda b, h: (0, 0, 0)))
    if has_rope:
        ins += list(rope)
        in_specs += [pl.BlockSpec((S, LANES), lambda b, h: (0, 0))] * 3
    W = n_blocks * LANES
    out_spec = pl.BlockSpec((1, S, LANES), lambda b, h: (b, 0, h))
    return pl.pallas_call(
        body, name=kind + "_attn_fwd", grid=(B, n_blocks),
        in_specs=in_specs, out_specs=[out_spec, out_spec],
        out_shape=[jax.ShapeDtypeStruct((B, S, W), F32)] * 2,
        scratch_shapes=[pltpu.VMEM((S, LANES), BF16), pltpu.VMEM((Sk, LANES), BF16), pltpu.VMEM((Sk, LANES), BF16)],
        compiler_params=_params(("arbitrary", "arbitrary")),
    )(*ins)


def attn_bwd(kind, src, do, o, lse, S, *, negc=None, mask=None, rope=None, kv=None):
    B = src.shape[0]
    pair = kind != "mem"
    col0 = {"fox": P_FOX, "dil": P_DIL, "mem": P_MQ}[kind]
    n_blocks = FOX_HEADS // 2 if pair else MEM_HEADS
    e_dim = HEAD_DIM if pair else MEM_HEAD_DIM
    scale = 1.0 / math.sqrt(e_dim)
    q_fold, s_scale = _scale_parts(scale)
    Sk = S if pair else MEM_LEN
    nh = 2 if pair else 1
    has_bias = negc is not None
    has_rope = rope is not None
    nq = S // TQ
    nk = Sk // TK

    def body(*refs):
        refs = list(refs)
        if pair:
            qkv_ref = refs.pop(0)
        else:
            q_ref, k_ref, v_ref = refs.pop(0), refs.pop(0), refs.pop(0)
        do_ref, o_ref, lse_ref = refs.pop(0), refs.pop(0), refs.pop(0)
        negc_ref = refs.pop(0) if has_bias else None
        mask_ref = refs.pop(0) if pair else None
        rope_refs = [refs.pop(0) for _ in range(3)] if has_rope else None
        if pair:
            dqkv_ref = refs.pop(0)
            dnegc_ref = refs.pop(0) if has_bias else None
            drow_ref = refs.pop(0) if has_bias else None
        else:
            dq_ref, dk_ref, dv_ref = refs.pop(0), refs.pop(0), refs.pop(0)
        qs, ks, vs, dos, delta_s, dq_acc = refs[:6]
        drow_acc = refs[6] if has_bias else None
        lane = lax.broadcasted_iota(jnp.int32, (1, LANES), 1)

        def prep_q(n, _):
            r0 = pl.multiple_of(n * TQ, TQ)
            rows = pl.ds(r0, TQ)
            q = qkv_ref[0, rows, 0:LANES] if pair else q_ref[0, rows, :]
            if has_rope:
                q = _rope(q, *[t[rows, :] for t in rope_refs])
            if q_fold is not None:
                q = q * q_fold
            qs[rows, :] = q.astype(BF16)
            dov = do_ref[0, rows, :]
            dob = dov.astype(BF16)
            dos[rows, :] = dob
            prod = dob.astype(F32) * o_ref[0, rows, :]
            if pair:
                d0 = jnp.sum(jnp.where(lane < HEAD_DIM, prod, 0.0), axis=1, keepdims=True)
                d1 = jnp.sum(jnp.where(lane < HEAD_DIM, 0.0, prod), axis=1, keepdims=True)
                delta_s[rows, :] = jnp.where(lane < HEAD_DIM, d0, d1)
            else:
                delta_s[rows, :] = jnp.broadcast_to(jnp.sum(prod, axis=1, keepdims=True), (TQ, LANES))
            dq_acc[rows, :] = jnp.zeros((TQ, LANES), F32)
            if has_bias:
                drow_acc[rows, :] = jnp.zeros((TQ, LANES), F32)
            return 0

        def prep_kv(n, _):
            r0 = pl.multiple_of(n * TK, TK)
            rows = pl.ds(r0, TK)
            k = qkv_ref[0, rows, LANES:2 * LANES] if pair else k_ref[0, rows, :]
            v = qkv_ref[0, rows, 2 * LANES:3 * LANES] if pair else v_ref[0, rows, :]
            if has_rope:
                k = _rope(k, *[t[rows, :] for t in rope_refs])
            ks[rows, :] = k.astype(BF16)
            vs[rows, :] = v.astype(BF16)
            return 0

        lax.fori_loop(0, nq, prep_q, 0)
        lax.fori_loop(0, nk, prep_kv, 0)

        def kv_loop(j, _):
            c0 = pl.multiple_of(j * TK, TK)
            kt = ks[pl.ds(c0, TK), :]
            vt = vs[pl.ds(c0, TK), :]
            res = []
            for hh in range(nh):
                hmask = (lane >= HEAD_DIM * hh) & (lane < HEAD_DIM * (hh + 1))
                kh = jnp.where(hmask, kt, jnp.zeros_like(kt)) if pair else kt
                vh = jnp.where(hmask, vt, jnp.zeros_like(vt)) if pair else vt

                def q_loop(i, carry, kh=kh, vh=vh, hh=hh, hmask=hmask):
                    dk, dv, dneg = carry
                    r0 = pl.multiple_of(i * TQ, TQ)
                    rows = pl.ds(r0, TQ)
                    q = qs[rows, :]
                    dot = dos[rows, :]
                    lse_i = lse_ref[0, rows, hh * HEAD_DIM:hh * HEAD_DIM + 1]
                    delta_i = delta_s[rows, hh * HEAD_DIM:hh * HEAD_DIM + 1]
                    s = lax.dot_general(q, kh, (((1,), (1,)), ((), ())), preferred_element_type=F32)
                    if s_scale is not None:
                        s = s * s_scale
                    if has_bias:
                        s = s + negc_ref[0, 0, pl.ds(hh, 1), pl.ds(c0, TK)]
                    if pair:
                        mult = mask_ref[i - j]
                        s = jnp.where(mult > 0.0, s, NEG_INF)
                    p = jnp.exp(s - lse_i)
                    if pair:
                        p = p * mult
                    dv = dv + lax.dot_general(p.astype(BF16), dot, (((0,), (0,)), ((), ())),
                                              preferred_element_type=F32)
                    dp = lax.dot_general(dot, vh, (((1,), (1,)), ((), ())), preferred_element_type=F32)
                    ds = p * (dp - delta_i)
                    if has_bias:
                        dneg = dneg + jnp.sum(ds, axis=0, keepdims=True)
                        drow_acc[rows, :] += jnp.where(hmask, jnp.sum(ds, axis=1, keepdims=True), 0.0)
                    if s_scale is not None:
                        ds = ds * s_scale
                    dsb = ds.astype(BF16)
                    dk = dk + lax.dot_general(dsb, q, (((0,), (0,)), ((), ())), preferred_element_type=F32)
                    dq = jnp.dot(dsb, kh, preferred_element_type=F32)
                    dq_acc[rows, :] += dq
                    return dk, dv, dneg

                init = (jnp.zeros((TK, LANES), F32), jnp.zeros((TK, LANES), F32), jnp.zeros((1, TK), F32))
                dk, dv, dneg = lax.fori_loop(j if pair else 0, nq, q_loop, init)
                if has_bias:
                    dnegc_ref[0, 0, pl.ds(hh, 1), pl.ds(c0, TK)] = dneg
                res.append((dk, dv))
            if pair:
                dk = jnp.where(lane < HEAD_DIM, res[0][0], res[1][0])
                dv = jnp.where(lane < HEAD_DIM, res[0][1], res[1][1])
                if has_rope:
                    dk = _rope_bwd(dk, *[t[pl.ds(c0, TK), :] for t in rope_refs])
                dqkv_ref[0, pl.ds(c0, TK), LANES:2 * LANES] = dk.astype(BF16)
                dqkv_ref[0, pl.ds(c0, TK), 2 * LANES:3 * LANES] = dv.astype(BF16)
            else:
                dk_ref[0, pl.ds(c0, TK), :] = res[0][0].astype(BF16)
                dv_ref[0, pl.ds(c0, TK), :] = res[0][1].astype(BF16)
            return 0

        lax.fori_loop(0, nk, kv_loop, 0)

        def fin_q(n, _):
            r0 = pl.multiple_of(n * TQ, TQ)
            rows = pl.ds(r0, TQ)
            dq = dq_acc[rows, :]
            if q_fold is not None:
                dq = dq * q_fold
            if has_rope:
                dq = _rope_bwd(dq, *[t[rows, :] for t in rope_refs])
            if pair:
                dqkv_ref[0, rows, 0:LANES] = dq.astype(BF16)
            else:
                dq_ref[0, rows, :] = dq.astype(BF16)
            if has_bias:
                drow_ref[0, rows, :] = drow_acc[rows, :]
            return 0

        lax.fori_loop(0, nq, fin_q, 0)

    ins, in_specs = [], []
    if pair:
        ins.append(src)
        in_specs.append(pl.BlockSpec((1, S, PAIR_W), lambda b, h: (b, 0, col0 // PAIR_W + h)))
    else:
        ins += [src, kv, kv]
        in_specs += [pl.BlockSpec((1, S, LANES), lambda b, h: (b, 0, col0 // LANES + h)),
                     pl.BlockSpec((1, MEM_LEN, LANES), lambda b, h: (b, 0, h)),
                     pl.BlockSpec((1, MEM_LEN, LANES), lambda b, h: (b, 0, MEM_HEADS + h))]
    row_spec = pl.BlockSpec((1, S, LANES), lambda b, h: (b, 0, h))
    ins += [do, o, lse]
    in_specs += [row_spec] * 3
    if has_bias:
        ins.append(negc)
        in_specs.append(pl.BlockSpec((1, 1, 2, S), lambda b, h: (b, h, 0, 0)))
    if pair:
        ins.append(mask)
        in_specs.append(pl.BlockSpec(mask.shape, lambda b, h: (0, 0, 0)))
    if has_rope:
        ins += list(rope)
        in_specs += [pl.BlockSpec((S, LANES), lambda b, h: (0, 0))] * 3
    W = n_blocks * LANES
    if pair:
        out_specs = [pl.BlockSpec((1, S, PAIR_W), lambda b, h: (b, 0, h))]
        out_shape = [jax.ShapeDtypeStruct((B, S, 3 * W), BF16)]
        if has_bias:
            out_specs.append(pl.BlockSpec((1, 1, 2, S), lambda b, h: (b, h, 0, 0)))
            out_shape.append(jax.ShapeDtypeStruct((B, LANES // 2, 2, S), F32))
            out_specs.append(row_spec)
            out_shape.append(jax.ShapeDtypeStruct((B, S, W), F32))
    else:
        kv_spec = pl.BlockSpec((1, MEM_LEN, LANES), lambda b, h: (b, 0, h))
        out_specs = [row_spec, kv_spec, kv_spec]
        out_shape = [jax.ShapeDtypeStruct((B, S, W), BF16)] + [jax.ShapeDtypeStruct((B, MEM_LEN, W), BF16)] * 2
    return pl.pallas_call(
        body, name=kind + "_attn_bwd", grid=(B, n_blocks),
        in_specs=in_specs, out_specs=out_specs, out_shape=out_shape,
        scratch_shapes=[pltpu.VMEM((S, LANES), BF16), pltpu.VMEM((Sk, LANES), BF16), pltpu.VMEM((Sk, LANES), BF16),
                        pltpu.VMEM((S, LANES), BF16), pltpu.VMEM((S, LANES), F32), pltpu.VMEM((S, LANES), F32)]
        + ([pltpu.VMEM((S, LANES), F32)] if has_bias else []),
        compiler_params=_params(("arbitrary", "arbitrary")),
    )(*ins)


def _log_masks(S, kind):
    nd = 1 if kind == "causal" else S // TQ
    a = np.arange(TQ)[:, None]
    b = np.arange(TK)[None, :]
    out = np.zeros((nd, TQ, TK), np.float32)
    for d in range(nd):
        delta = d * TQ + a - b
        if kind == "causal":
            m = (delta >= 0).astype(np.float64)
        else:
            m = sum(((delta >= 0) & (delta % dil == 0) & (delta <= w)).astype(np.float64) for w, dil in DILATIONS)
        out[d] = np.where(m > 0, np.log(np.maximum(m, 1.0)), NEG_INF)
    return jnp.asarray(out)


def _attn_setup(kind):
    pair = kind != "mem"
    e_dim = HEAD_DIM if pair else MEM_HEAD_DIM
    q_fold, s_scale = _scale_parts(1.0 / math.sqrt(e_dim))
    return dict(pair=pair, col0={"fox": P_FOX, "dil": P_DIL, "mem": P_MQ}[kind],
                n_blocks=FOX_HEADS // 2 if pair else MEM_HEADS, q_fold=q_fold, s_scale=s_scale,
                nh=2 if pair else 1)


def _attn_inputs(kind, src, S, negc, mask, rope, kv, extra):
    cfg = _attn_setup(kind)
    col0 = cfg["col0"]
    ins, in_specs = [], []
    if cfg["pair"]:
        ins.append(src)
        in_specs.append(pl.BlockSpec((1, S, PAIR_W), lambda b, h: (b, 0, col0 // PAIR_W + h)))
    else:
        ins += [src, kv, kv]
        in_specs += [pl.BlockSpec((1, S, LANES), lambda b, h: (b, 0, col0 // LANES + h)),
                     pl.BlockSpec((1, MEM_LEN, LANES), lambda b, h: (b, 0, h)),
                     pl.BlockSpec((1, MEM_LEN, LANES), lambda b, h: (b, 0, MEM_HEADS + h))]
    ins += list(extra)
    in_specs += [pl.BlockSpec((1, S, LANES), lambda b, h: (b, 0, h))] * len(extra)
    if negc is not None:
        ins.append(negc)
        in_specs.append(pl.BlockSpec((1, 1, 2, S), lambda b, h: (b, h, 0, 0)))
    if mask is not None:
        ins.append(mask)
        in_specs.append(pl.BlockSpec(mask.shape, lambda b, h: (0, 0, 0)))
    if rope is not None:
        ins += list(rope)
        in_specs += [pl.BlockSpec((S, LANES), lambda b, h: (0, 0))] * 3
    return ins, in_specs


def _prep_rows(cfg, rope_refs, lane, load_q, load_kv, qs2, ks, vs, S, Sk):
    nh = cfg["nh"]
    R = nh * TQ

    def prep_q(n, _):
        rows = pl.ds(pl.multiple_of(n * TQ, TQ), TQ)
        q = load_q(rows)
        if rope_refs is not None:
            q = _rope(q, *[t[rows, :] for t in rope_refs])
        if cfg["q_fold"] is not None:
            q = q * cfg["q_fold"]
        _store_stacked(cfg, lane, qs2, n, q.astype(BF16))
        return 0

    def prep_kv(n, _):
        rows = pl.ds(pl.multiple_of(n * TK, TK), TK)
        k, v = load_kv(rows)
        if rope_refs is not None:
            k = _rope(k, *[t[rows, :] for t in rope_refs])
        ks[rows, :] = k.astype(BF16)
        vs[rows, :] = v.astype(BF16)
        return 0

    lax.fori_loop(0, S // TQ, prep_q, 0)
    lax.fori_loop(0, Sk // TK, prep_kv, 0)


def _store_stacked(cfg, lane, dst, n, val):
    nh = cfg["nh"]
    R = nh * TQ
    if nh == 1:
        dst[pl.ds(pl.multiple_of(n * R, R), TQ), :] = val
        return
    for hh in range(nh):
        hmask = (lane >= HEAD_DIM * hh) & (lane < HEAD_DIM * (hh + 1))
        dst[pl.ds(pl.multiple_of(n * R + hh * TQ, TQ), TQ), :] = jnp.where(hmask, val, jnp.zeros_like(val))


def _cat(parts, axis):
    return parts[0] if len(parts) == 1 else jnp.concatenate(parts, axis=axis)


def attn_fwd2(kind, src, S, *, negc=None, mask=None, rope=None, kv=None):
    B = src.shape[0]
    cfg = _attn_setup(kind)
    pair, nh, s_scale = cfg["pair"], cfg["nh"], cfg["s_scale"]
    Sk = S if pair else MEM_LEN
    has_bias, has_rope = negc is not None, rope is not None
    R = nh * TQ

    def body(*refs):
        refs = list(refs)
        if pair:
            qkv_ref = refs.pop(0)
        else:
            q_ref, k_ref, v_ref = refs.pop(0), refs.pop(0), refs.pop(0)
        negc_ref = refs.pop(0) if has_bias else None
        mask_ref = refs.pop(0) if mask is not None else None
        rope_refs = [refs.pop(0) for _ in range(3)] if has_rope else None
        o_ref, lse_ref, qs2, ks, vs = refs
        lane = lax.broadcasted_iota(jnp.int32, (1, LANES), 1)

        if pair:
            load_q = lambda rows: qkv_ref[0, rows, 0:LANES]
            load_kv = lambda rows: (qkv_ref[0, rows, LANES:2 * LANES], qkv_ref[0, rows, 2 * LANES:3 * LANES])
        else:
            load_q = lambda rows: q_ref[0, rows, :]
            load_kv = lambda rows: (k_ref[0, rows, :], v_ref[0, rows, :])
        _prep_rows(cfg, rope_refs, lane, load_q, load_kv, qs2, ks, vs, S, Sk)

        def q_loop(i, _):
            q2 = qs2[pl.ds(pl.multiple_of(i * R, R), R), :]

            def step(j, carry, midx):
                ms, ls, acc = carry
                c0 = pl.multiple_of(j * TK, TK)
                k = ks[pl.ds(c0, TK), :]
                v = vs[pl.ds(c0, TK), :]
                s2 = lax.dot_general(q2, k, (((1,), (1,)), ((), ())), preferred_element_type=F32)
                if s_scale is not None:
                    s2 = s2 * s_scale
                new_m, new_l, ps, alphas = [], [], [], []
                for hh in range(nh):
                    s = s2[hh * TQ:(hh + 1) * TQ]
                    if has_bias:
                        s = s + negc_ref[0, 0, pl.ds(hh, 1), pl.ds(c0, TK)]
                    if midx is not None:
                        s = s + mask_ref[midx]
                    m_new = jnp.maximum(ms[hh], jnp.max(s, axis=1, keepdims=True))
                    p = jnp.exp(s - m_new)
                    alpha = jnp.exp(ms[hh] - m_new)
                    new_l.append(alpha * ls[hh] + jnp.sum(p, axis=1, keepdims=True))
                    new_m.append(m_new)
                    ps.append(p.astype(BF16))
                    alphas.append(alpha)
                acc = acc * _cat(alphas, 0) + jnp.dot(_cat(ps, 0), v, preferred_element_type=F32)
                return tuple(new_m), tuple(new_l), acc

            init = (tuple(jnp.full((TQ, 1), NEG_INF, F32) for _ in range(nh)),
                    tuple(jnp.zeros((TQ, 1), F32) for _ in range(nh)), jnp.zeros((R, LANES), F32))
            if kind == "fox":
                carry = lax.fori_loop(0, i, lambda j, c: step(j, c, None), init)
                carry = step(i, carry, 0)
            elif kind == "dil":
                carry = lax.fori_loop(0, i + 1, lambda j, c: step(j, c, i - j), init)
            else:
                carry = lax.fori_loop(0, Sk // TK, lambda j, c: step(j, c, None), init)
            ms, ls, acc = carry
            outs = [acc[hh * TQ:(hh + 1) * TQ] / ls[hh] for hh in range(nh)]
            lses = [ms[hh] + jnp.log(ls[hh]) for hh in range(nh)]
            rows = pl.ds(pl.multiple_of(i * TQ, TQ), TQ)
            if pair:
                o_ref[0, rows, :] = jnp.where(lane < HEAD_DIM, outs[0], outs[1])
                lse_ref[0, rows, :] = jnp.where(lane < HEAD_DIM, lses[0], lses[1])
            else:
                o_ref[0, rows, :] = outs[0]
                lse_ref[0, rows, :] = jnp.broadcast_to(lses[0], (TQ, LANES))
            return 0

        lax.fori_loop(0, S // TQ, q_loop, 0)

    ins, in_specs = _attn_inputs(kind, src, S, negc, mask, rope, kv, ())
    W = cfg["n_blocks"] * LANES
    out_spec = pl.BlockSpec((1, S, LANES), lambda b, h: (b, 0, h))
    return pl.pallas_call(
        body, name=kind + "_attn_fwd", grid=(B, cfg["n_blocks"]),
        in_specs=in_specs, out_specs=[out_spec, out_spec],
        out_shape=[jax.ShapeDtypeStruct((B, S, W), F32)] * 2,
        scratch_shapes=[pltpu.VMEM((nh * S, LANES), BF16), pltpu.VMEM((Sk, LANES), BF16),
                        pltpu.VMEM((Sk, LANES), BF16)],
        compiler_params=_params(("arbitrary", "arbitrary")),
    )(*ins)


def attn_bwd2(kind, src, do, o, lse, S, *, negc=None, mask=None, rope=None, kv=None):
    B = src.shape[0]
    cfg = _attn_setup(kind)
    pair, nh, s_scale, q_fold = cfg["pair"], cfg["nh"], cfg["s_scale"], cfg["q_fold"]
    Sk = S if pair else MEM_LEN
    has_bias, has_rope = negc is not None, rope is not None
    R = nh * TQ
    nq, nk = S // TQ, Sk // TK

    def body(*refs):
        refs = list(refs)
        if pair:
            qkv_ref = refs.pop(0)
        else:
            q_ref, k_ref, v_ref = refs.pop(0), refs.pop(0), refs.pop(0)
        do_ref, o_ref, lse_ref = refs.pop(0), refs.pop(0), refs.pop(0)
        negc_ref = refs.pop(0) if has_bias else None
        mask_ref = refs.pop(0) if mask is not None else None
        rope_refs = [refs.pop(0) for _ in range(3)] if has_rope else None
        if pair:
            dqkv_ref = refs.pop(0)
            dnegc_ref = refs.pop(0) if has_bias else None
            drow_ref = refs.pop(0) if has_bias else None
        else:
            dq_ref, dk_ref, dv_ref = refs.pop(0), refs.pop(0), refs.pop(0)
        qs2, ks, vs, dos2, lse_s, delta_s, dk_acc, dv_acc = refs[:8]
        dneg_acc = refs[8] if has_bias else None
        lane = lax.broadcasted_iota(jnp.int32, (1, LANES), 1)

        if pair:
            load_q = lambda rows: qkv_ref[0, rows, 0:LANES]
            load_kv = lambda rows: (qkv_ref[0, rows, LANES:2 * LANES], qkv_ref[0, rows, 2 * LANES:3 * LANES])
        else:
            load_q = lambda rows: q_ref[0, rows, :]
            load_kv = lambda rows: (k_ref[0, rows, :], v_ref[0, rows, :])
        _prep_rows(cfg, rope_refs, lane, load_q, load_kv, qs2, ks, vs, S, Sk)

        def prep_do(n, _):
            rows = pl.ds(pl.multiple_of(n * TQ, TQ), TQ)
            dob = do_ref[0, rows, :].astype(BF16)
            _store_stacked(cfg, lane, dos2, n, dob)
            prod = dob.astype(F32) * o_ref[0, rows, :]
            lse_blk = lse_ref[0, rows, :]
            for hh in range(nh):
                dst = pl.ds(pl.multiple_of(n * R + hh * TQ, TQ), TQ)
                if pair:
                    hmask = (lane >= HEAD_DIM * hh) & (lane < HEAD_DIM * (hh + 1))
                    d = jnp.sum(jnp.where(hmask, prod, 0.0), axis=1, keepdims=True)
                    lse_s[dst, :] = jnp.broadcast_to(lse_blk[:, hh * HEAD_DIM:hh * HEAD_DIM + 1], (TQ, LANES))
                else:
                    d = jnp.sum(prod, axis=1, keepdims=True)
                    lse_s[dst, :] = lse_blk
                delta_s[dst, :] = jnp.broadcast_to(d, (TQ, LANES))
            return 0

        def zero_kv(n, _):
            rows = pl.ds(pl.multiple_of(n * TK, TK), TK)
            dk_acc[rows, :] = jnp.zeros((TK, LANES), F32)
            dv_acc[rows, :] = jnp.zeros((TK, LANES), F32)
            return 0

        lax.fori_loop(0, nq, prep_do, 0)
        lax.fori_loop(0, nk, zero_kv, 0)
        if has_bias:
            dneg_acc[...] = jnp.zeros(dneg_acc.shape, F32)

        def q_loop(i, _):
            rows2 = pl.ds(pl.multiple_of(i * R, R), R)
            q2 = qs2[rows2, :]
            do2 = dos2[rows2, :]
            lse2 = lse_s[rows2, :]
            delta2 = delta_s[rows2, :]
            wide = lambda t: jnp.concatenate([t] * (TK // LANES), axis=1)

            def step(j, carry, midx):
                dq2, drow = carry
                c0 = pl.multiple_of(j * TK, TK)
                kcols = pl.ds(c0, TK)
                k = ks[kcols, :]
                v = vs[kcols, :]
                s2 = lax.dot_general(q2, k, (((1,), (1,)), ((), ())), preferred_element_type=F32)
                if s_scale is not None:
                    s2 = s2 * s_scale
                if has_bias or midx is not None:
                    halves = []
                    for hh in range(nh):
                        s = s2[hh * TQ:(hh + 1) * TQ]
                        if has_bias:
                            s = s + negc_ref[0, 0, pl.ds(hh, 1), kcols]
                        if midx is not None:
                            s = s + mask_ref[midx]
                        halves.append(s)
                    s2 = _cat(halves, 0)
                p2 = jnp.exp(s2 - wide(lse2))
                dp2 = lax.dot_general(do2, v, (((1,), (1,)), ((), ())), preferred_element_type=F32)
                ds2 = p2 * (dp2 - wide(delta2))
                if has_bias:
                    drow = drow + jnp.sum(ds2, axis=1, keepdims=True)
                    for hh in range(nh):
                        dneg_acc[pl.ds(hh, 1), kcols] += jnp.sum(ds2[hh * TQ:(hh + 1) * TQ], axis=0, keepdims=True)
                if s_scale is not None:
                    ds2 = ds2 * s_scale
                dsb = ds2.astype(BF16)
                dv_acc[kcols, :] += lax.dot_general(p2.astype(BF16), do2, (((0,), (0,)), ((), ())),
                                                    preferred_element_type=F32)
                dk_acc[kcols, :] += lax.dot_general(dsb, q2, (((0,), (0,)), ((), ())), preferred_element_type=F32)
                dq2 = dq2 + jnp.dot(dsb, k, preferred_element_type=F32)
                return dq2, drow

            init = (jnp.zeros((R, LANES), F32), jnp.zeros((R, 1), F32))
            if kind == "fox":
                carry = lax.fori_loop(0, i, lambda j, c: step(j, c, None), init)
                carry = step(i, carry, 0)
            elif kind == "dil":
                carry = lax.fori_loop(0, i + 1, lambda j, c: step(j, c, i - j), init)
            else:
                carry = lax.fori_loop(0, nk, lambda j, c: step(j, c, None), init)
            dq2, drow = carry
            rows = pl.ds(pl.multiple_of(i * TQ, TQ), TQ)
            dq = jnp.where(lane < HEAD_DIM, dq2[0:TQ], dq2[TQ:2 * TQ]) if pair else dq2
            if q_fold is not None:
                dq = dq * q_fold
            if has_rope:
                dq = _rope_bwd(dq, *[t[rows, :] for t in rope_refs])
            if pair:
                dqkv_ref[0, rows, 0:LANES] = dq.astype(BF16)
            else:
                dq_ref[0, rows, :] = dq.astype(BF16)
            if has_bias:
                drow_ref[0, rows, :] = jnp.where(lane < HEAD_DIM, drow[0:TQ], drow[TQ:2 * TQ])
            return 0

        lax.fori_loop(0, nq, q_loop, 0)

        def fin_kv(n, _):
            rows = pl.ds(pl.multiple_of(n * TK, TK), TK)
            dk = dk_acc[rows, :]
            if has_rope:
                dk = _rope_bwd(dk, *[t[rows, :] for t in rope_refs])
            if pair:
                dqkv_ref[0, rows, LANES:2 * LANES] = dk.astype(BF16)
                dqkv_ref[0, rows, 2 * LANES:3 * LANES] = dv_acc[rows, :].astype(BF16)
            else:
                dk_ref[0, rows, :] = dk.astype(BF16)
                dv_ref[0, rows, :] = dv_acc[rows, :].astype(BF16)
            return 0

        lax.fori_loop(0, nk, fin_kv, 0)
        if has_bias:
            dnegc_ref[0, 0] = dneg_acc[...]

    ins, in_specs = _attn_inputs(kind, src, S, negc, mask, rope, kv, (do, o, lse))
    W = cfg["n_blocks"] * LANES
    row_spec = pl.BlockSpec((1, S, LANES), lambda b, h: (b, 0, h))
    if pair:
        out_specs = [pl.BlockSpec((1, S, PAIR_W), lambda b, h: (b, 0, h))]
        out_shape = [jax.ShapeDtypeStruct((B, S, 3 * W), BF16)]
        if has_bias:
            out_specs += [pl.BlockSpec((1, 1, 2, S), lambda b, h: (b, h, 0, 0)), row_spec]
            out_shape += [jax.ShapeDtypeStruct((B, LANES // 2, 2, S), F32), jax.ShapeDtypeStruct((B, S, W), F32)]
    else:
        kv_spec = pl.BlockSpec((1, MEM_LEN, LANES), lambda b, h: (b, 0, h))
        out_specs = [row_spec, kv_spec, kv_spec]
        out_shape = [jax.ShapeDtypeStruct((B, S, W), BF16)] + [jax.ShapeDtypeStruct((B, MEM_LEN, W), BF16)] * 2
    scratch = [pltpu.VMEM((nh * S, LANES), BF16), pltpu.VMEM((Sk, LANES), BF16), pltpu.VMEM((Sk, LANES), BF16),
               pltpu.VMEM((nh * S, LANES), BF16), pltpu.VMEM((nh * S, LANES), F32), pltpu.VMEM((nh * S, LANES), F32),
               pltpu.VMEM((Sk, LANES), F32), pltpu.VMEM((Sk, LANES), F32)]
    if has_bias:
        scratch.append(pltpu.VMEM((2, S), F32))
    return pl.pallas_call(
        body, name=kind + "_attn_bwd", grid=(B, cfg["n_blocks"]),
        in_specs=in_specs, out_specs=out_specs, out_shape=out_shape, scratch_shapes=scratch,
        compiler_params=_params(("arbitrary", "arbitrary")),
    )(*ins)


def _log_masks_t(S, kind):
    return jnp.swapaxes(_log_masks(S, kind), 1, 2)


def _head_rows(hh, pair):
    row = lax.broadcasted_iota(jnp.int32, (LANES, 1), 0)
    if not pair:
        return row >= 0
    return (row >= HEAD_DIM * hh) & (row < HEAD_DIM * (hh + 1))


def _attn_t_inputs(kind, src, S, negc_cols, mask, rope, kv):
    cfg = _attn_setup(kind)
    col0 = cfg["col0"]
    ins, in_specs = [], []
    if cfg["pair"]:
        ins.append(src)
        in_specs.append(pl.BlockSpec((1, S, PAIR_W), lambda b, h: (b, 0, col0 // PAIR_W + h)))
    else:
        ins += [src, kv, kv]
        in_specs += [pl.BlockSpec((1, S, LANES), lambda b, h: (b, 0, col0 // LANES + h)),
                     pl.BlockSpec((1, MEM_LEN, LANES), lambda b, h: (b, 0, h)),
                     pl.BlockSpec((1, MEM_LEN, LANES), lambda b, h: (b, 0, MEM_HEADS + h))]
    if negc_cols is not None:
        ins.append(negc_cols)
        in_specs.append(pl.BlockSpec((1, S, LANES), lambda b, h: (b, 0, 0)))
    if mask is not None:
        ins.append(mask)
        in_specs.append(pl.BlockSpec(mask.shape, lambda b, h: (0, 0, 0)))
    if rope is not None:
        ins += list(rope)
        in_specs += [pl.BlockSpec((S, LANES), lambda b, h: (0, 0))] * 3
    return ins, in_specs


def _attn_t_prep(cfg, refs, S, Sk, *, qT2s, ks, q2s=None, vs=None, vTs=None, kTs=None, nb=None):
    pair, nh = cfg["pair"], cfg["nh"]
    lane = lax.broadcasted_iota(jnp.int32, (1, LANES), 1)
    rope_refs = refs["rope"]

    def prep_q(n, _):
        rows = pl.ds(pl.multiple_of(n * TQ, TQ), TQ)
        q = refs["load_q"](rows)
        if rope_refs is not None:
            q = _rope(q, *[t[rows, :] for t in rope_refs])
        if cfg["q_fold"] is not None:
            q = q * cfg["q_fold"]
        qb = q.astype(BF16)
        if q2s is not None:
            _store_stacked(cfg, lane, q2s, n, qb)
        qtb = qb.T
        for hh in range(nh):
            qT2s[n, :, hh * TQ:(hh + 1) * TQ] = jnp.where(_head_rows(hh, pair), qtb, jnp.zeros_like(qtb))
        return 0

    def prep_kv(n, _):
        rows = pl.ds(pl.multiple_of(n * TK, TK), TK)
        k, v = refs["load_kv"](rows)
        if rope_refs is not None:
            k = _rope(k, *[t[rows, :] for t in rope_refs])
        kb = k.astype(BF16)
        vb = v.astype(BF16)
        ks[rows, :] = kb
        if vs is not None:
            vs[rows, :] = vb
        if vTs is not None:
            vTs[n] = vb.T
        if kTs is not None:
            kTs[n] = kb.T
        if nb is not None:
            blk = refs["negc"][0, rows, :]
            for hh in range(nh):
                h = 2 * refs["block"] + hh
                col = jnp.sum(jnp.where(lane == h, blk, 0.0), axis=1, keepdims=True)
                nb[hh, rows, :] = jnp.broadcast_to(col, (TK, LANES))
        return 0

    lax.fori_loop(0, S // TQ, prep_q, 0)
    lax.fori_loop(0, Sk // TK, prep_kv, 0)


def _raw_scores_t(cfg, k, qT2):
    sT = jnp.dot(k, qT2, preferred_element_type=F32)
    if cfg["s_scale"] is not None:
        sT = sT * cfg["s_scale"]
    return sT


def _bias_mask_t(cfg, sT, nb, mask_ref, kc, midx):
    nh = cfg["nh"]
    if nb is None and midx is None:
        return sT
    parts = []
    for hh in range(nh):
        t = sT[:, hh * TQ:(hh + 1) * TQ]
        if nb is not None:
            t = t + jnp.concatenate([nb[hh, kc, :]] * (TQ // LANES), axis=1)
        if midx is not None:
            t = t + mask_ref[midx]
        parts.append(t)
    return _cat(parts, 1)


def _kv_plan(kind, i, nk):
    if kind == "fox":
        return i, (lambda j: None), 0
    if kind == "dil":
        return i, (lambda j: i - j), 0
    return nk - 1, (lambda j: None), None


def attn_fwd3(kind, src, S, *, negc_cols=None, mask=None, rope=None, kv=None):
    B = src.shape[0]
    cfg = _attn_setup(kind)
    pair, nh = cfg["pair"], cfg["nh"]
    Sk = S if pair else MEM_LEN
    has_bias, has_rope = negc_cols is not None, rope is not None
    R = nh * TQ
    nq, nk = S // TQ, Sk // TK

    def body(*refs):
        refs = list(refs)
        if pair:
            qkv_ref = refs.pop(0)
            load_q = lambda rows: qkv_ref[0, rows, 0:LANES]
            load_kv = lambda rows: (qkv_ref[0, rows, LANES:2 * LANES], qkv_ref[0, rows, 2 * LANES:3 * LANES])
        else:
            q_ref, k_ref, v_ref = refs.pop(0), refs.pop(0), refs.pop(0)
            load_q = lambda rows: q_ref[0, rows, :]
            load_kv = lambda rows: (k_ref[0, rows, :], v_ref[0, rows, :])
        negc_ref = refs.pop(0) if has_bias else None
        mask_ref = refs.pop(0) if mask is not None else None
        rope_refs = [refs.pop(0) for _ in range(3)] if has_rope else None
        o_ref, lse_ref, qT2s, ks, vTs = refs[:5]
        nb = refs[5] if has_bias else None
        _attn_t_prep(cfg, dict(load_q=load_q, load_kv=load_kv, rope=rope_refs, negc=negc_ref,
                               block=pl.program_id(1)), S, Sk,
                     qT2s=qT2s, ks=ks, vTs=vTs, nb=nb)

        def q_loop(i, _):
            qT2 = qT2s[i]

            last, mask_of, mask_last = _kv_plan(kind, i, nk)

            def cols(j):
                return pl.ds(pl.multiple_of(j * TK, TK), TK)

            def scores(j):
                return _raw_scores_t(cfg, ks[cols(j), :], qT2)

            def soft(s_raw, j, midx, m, l):
                sT = _bias_mask_t(cfg, s_raw, nb, mask_ref, cols(j), midx)
                m_new = jnp.maximum(m, jnp.max(sT, axis=0, keepdims=True))
                p = jnp.exp(sT - m_new)
                alpha = jnp.exp(m - m_new)
                return m_new, alpha * l + jnp.sum(p, axis=0, keepdims=True), alpha, p.astype(BF16)

            def pv(j, p):
                return jnp.dot(vTs[j], p, preferred_element_type=F32)

            def body(j, carry):
                s_cur, p_prev, m, l, accT = carry
                pv_prev = pv(jnp.maximum(j - 1, 0), p_prev)
                s_next = scores(j + 1)
                m, l, alpha, p = soft(s_cur, j, mask_of(j), m, l)
                return s_next, p, m, l, (accT + pv_prev) * alpha

            init = (scores(0), jnp.zeros((TK, R), BF16), jnp.full((1, R), NEG_INF, F32), jnp.zeros((1, R), F32),
                    jnp.zeros((LANES, R), F32))
            s_cur, p_prev, m, l, accT = lax.fori_loop(0, last, body, init)
            pv_prev = pv(jnp.maximum(last - 1, 0), p_prev)
            m, l, alpha, p = soft(s_cur, last, mask_last, m, l)
            accT = (accT + pv_prev) * alpha + pv(last, p)
            oT2 = accT / l
            oT = jnp.where(_head_rows(0, True), oT2[:, 0:TQ], oT2[:, TQ:2 * TQ]) if pair else oT2
            o_ref[0, pl.ds(pl.multiple_of(i * TQ, TQ), TQ), :] = oT.T
            lse_ref[0, 0, pl.ds(i, 1), :] = m + jnp.log(l)
            return 0

        lax.fori_loop(0, nq, q_loop, 0)

    ins, in_specs = _attn_t_inputs(kind, src, S, negc_cols, mask, rope, kv)
    W = cfg["n_blocks"] * LANES
    scratch = [pltpu.VMEM((nq, LANES, R), BF16), pltpu.VMEM((Sk, LANES), BF16), pltpu.VMEM((nk, LANES, TK), BF16)]
    if has_bias:
        scratch.append(pltpu.VMEM((nh, Sk, LANES), F32))
    return pl.pallas_call(
        body, name=kind + "_attn_fwd", grid=(B, cfg["n_blocks"]),
        in_specs=in_specs,
        out_specs=[pl.BlockSpec((1, S, LANES), lambda b, h: (b, 0, h)),
                   pl.BlockSpec((1, 1, nq, R), lambda b, h: (b, h, 0, 0))],
        out_shape=[jax.ShapeDtypeStruct((B, S, W), F32), jax.ShapeDtypeStruct((B, cfg["n_blocks"], nq, R), F32)],
        scratch_shapes=scratch,
        compiler_params=_params(("arbitrary", "arbitrary")),
    )(*ins)


def _tile_walk(kind, nq, nk):
    if kind == "mem":
        return nq * nk, (lambda i, j: (jnp.where(j < nk - 1, i, i + 1), jnp.where(j < nk - 1, j + 1, 0))), None
    nxt = lambda i, j: (jnp.where(j < i, i, i + 1), jnp.where(j < i, j + 1, 0))
    if kind == "fox":
        return nq * (nq + 1) // 2, nxt, (lambda i, j: jnp.where(j == i, 0, 1))
    return nq * (nq + 1) // 2, nxt, (lambda i, j: i - j)


def attn_fwd4(kind, src, S, *, negc_cols=None, mask=None, rope=None, kv=None):
    B = src.shape[0]
    cfg = _attn_setup(kind)
    pair, nh = cfg["pair"], cfg["nh"]
    Sk = S if pair else MEM_LEN
    has_bias, has_rope = negc_cols is not None, rope is not None
    R = nh * TQ
    nq, nk = S // TQ, Sk // TK
    n_pairs, successor, mask_index = _tile_walk(kind, nq, nk)
    assert n_pairs % 2 == 0

    def body(*refs):
        refs = list(refs)
        if pair:
            qkv_ref = refs.pop(0)
            load_q = lambda rows: qkv_ref[0, rows, 0:LANES]
            load_kv = lambda rows: (qkv_ref[0, rows, LANES:2 * LANES], qkv_ref[0, rows, 2 * LANES:3 * LANES])
        else:
            q_ref, k_ref, v_ref = refs.pop(0), refs.pop(0), refs.pop(0)
            load_q = lambda rows: q_ref[0, rows, :]
            load_kv = lambda rows: (k_ref[0, rows, :], v_ref[0, rows, :])
        negc_ref = refs.pop(0) if has_bias else None
        mask_ref = refs.pop(0) if mask is not None else None
        rope_refs = [refs.pop(0) for _ in range(3)] if has_rope else None
        o_ref, lse_ref, qT2s, ks, vTs, s_a, s_b, p_a, p_b, acc_all, m_all, l_all = refs[:12]
        nb = refs[12] if has_bias else None
        _attn_t_prep(cfg, dict(load_q=load_q, load_kv=load_kv, rope=rope_refs, negc=negc_ref,
                               block=pl.program_id(1)), S, Sk, qT2s=qT2s, ks=ks, vTs=vTs, nb=nb)

        def cols(j):
            return pl.ds(pl.multiple_of(j * TK, TK), TK)

        def park(i, m, l, accT):
            acc_all[i] = accT
            m_all[pl.ds(i, 1), :] = m
            l_all[pl.ds(i, 1), :] = l

        def finish(i, _):
            l = l_all[pl.ds(i, 1), :]
            oT2 = acc_all[i] / l
            oT = jnp.where(_head_rows(0, True), oT2[:, 0:TQ], oT2[:, TQ:2 * TQ]) if pair else oT2
            o_ref[0, pl.ds(pl.multiple_of(i * TQ, TQ), TQ), :] = oT.T
            lse_ref[0, 0, pl.ds(i, 1), :] = m_all[pl.ds(i, 1), :] + jnp.log(l)
            return 0

        def half(i, j, i_prev, j_prev, s_cur, s_next, p_cur, p_prev, m, l, accT):
            i_n, j_n = successor(i, j)
            acc_full = accT + jnp.dot(vTs[j_prev], p_prev[...], preferred_element_type=F32)
            s_next[...] = _raw_scores_t(cfg, ks[cols(j_n), :], qT2s[jnp.minimum(i_n, nq - 1)])
            park(i_prev, m, l, acc_full)
            first = j == 0
            m = jnp.where(first, NEG_INF, m)
            l = jnp.where(first, 0.0, l)
            sT = _bias_mask_t(cfg, s_cur[...], nb, mask_ref, cols(j), None if mask_index is None else mask_index(i, j))
            m_new = jnp.maximum(m, jnp.max(sT, axis=0, keepdims=True))
            p = jnp.exp(sT - m_new)
            alpha = jnp.exp(m - m_new)
            p_cur[...] = p.astype(BF16)
            return i_n, j_n, i, j, m_new, alpha * l + jnp.sum(p, axis=0, keepdims=True), acc_full * alpha

        def two(_, carry):
            i, j, i_prev, j_prev, m, l, accT = carry
            i, j, i_prev, j_prev, m, l, accT = half(i, j, i_prev, j_prev, s_a, s_b, p_a, p_b, m, l, accT)
            return half(i, j, i_prev, j_prev, s_b, s_a, p_b, p_a, m, l, accT)

        s_a[...] = _raw_scores_t(cfg, ks[cols(0), :], qT2s[0])
        p_b[...] = jnp.zeros((TK, R), BF16)
        zero = jnp.int32(0)
        init = (zero, zero, zero, zero, jnp.full((1, R), NEG_INF, F32), jnp.ones((1, R), F32),
                jnp.zeros((LANES, R), F32))
        _, _, i_prev, j_prev, m, l, accT = lax.fori_loop(0, n_pairs // 2, two, init)
        park(i_prev, m, l, accT + jnp.dot(vTs[j_prev], p_b[...], preferred_element_type=F32))
        lax.fori_loop(0, nq, finish, 0)

    ins, in_specs = _attn_t_inputs(kind, src, S, negc_cols, mask, rope, kv)
    W = cfg["n_blocks"] * LANES
    scratch = [pltpu.VMEM((nq, LANES, R), BF16), pltpu.VMEM((Sk, LANES), BF16), pltpu.VMEM((nk, LANES, TK), BF16),
               pltpu.VMEM((TK, R), F32), pltpu.VMEM((TK, R), F32), pltpu.VMEM((TK, R), BF16), pltpu.VMEM((TK, R), BF16),
               pltpu.VMEM((nq, LANES, R), F32), pltpu.VMEM((nq, R), F32), pltpu.VMEM((nq, R), F32)]
    if has_bias:
        scratch.append(pltpu.VMEM((nh, Sk, LANES), F32))
    return pl.pallas_call(
        body, name=kind + "_attn_fwd", grid=(B, cfg["n_blocks"]),
        in_specs=in_specs,
        out_specs=[pl.BlockSpec((1, S, LANES), lambda b, h: (b, 0, h)),
                   pl.BlockSpec((1, 1, nq, R), lambda b, h: (b, h, 0, 0))],
        out_shape=[jax.ShapeDtypeStruct((B, S, W), F32), jax.ShapeDtypeStruct((B, cfg["n_blocks"], nq, R), F32)],
        scratch_shapes=scratch,
        compiler_params=_params(("arbitrary", "arbitrary")),
    )(*ins)


def attn_bwd3(kind, src, do, o, lse, S, *, negc_cols=None, mask=None, rope=None, kv=None, token=None):
    B = src.shape[0]
    cfg = _attn_setup(kind)
    pair, nh, s_scale, q_fold = cfg["pair"], cfg["nh"], cfg["s_scale"], cfg["q_fold"]
    Sk = S if pair else MEM_LEN
    has_bias, has_rope = negc_cols is not None, rope is not None
    R = nh * TQ
    nq, nk = S // TQ, Sk // TK
    n_pairs, successor, mask_index = _tile_walk(kind, nq, nk)
    assert n_pairs % 2 == 0

    def body(*refs):
        refs = list(refs)
        if pair:
            qkv_ref = refs.pop(0)
            load_q = lambda rows: qkv_ref[0, rows, 0:LANES]
            load_kv = lambda rows: (qkv_ref[0, rows, LANES:2 * LANES], qkv_ref[0, rows, 2 * LANES:3 * LANES])
        else:
            q_ref, k_ref, v_ref = refs.pop(0), refs.pop(0), refs.pop(0)
            load_q = lambda rows: q_ref[0, rows, :]
            load_kv = lambda rows: (k_ref[0, rows, :], v_ref[0, rows, :])
        negc_ref = refs.pop(0) if has_bias else None
        mask_ref = refs.pop(0) if mask is not None else None
        rope_refs = [refs.pop(0) for _ in range(3)] if has_rope else None
        do_ref, o_ref, lse_ref = refs.pop(0), refs.pop(0), refs.pop(0)
        if token is not None:
            refs.pop(0)
        if pair:
            dqkv_ref = refs.pop(0)
            dneg_ref = refs.pop(0) if has_bias else None
            drow_ref = refs.pop(0) if has_bias else None
        else:
            dq_ref, dk_ref, dv_ref = refs.pop(0), refs.pop(0), refs.pop(0)
        qT2s, ks, q2s, vs, kTs, doT2s, do2s, delta_s, dk_acc, dv_acc = refs[:10]
        bufs_a, bufs_b, dq_all = refs[10:14], refs[14:18], refs[18]
        nb, dneg_acc, drow_all = (refs[19], refs[20], refs[21]) if has_bias else (None, None, None)
        lane = lax.broadcasted_iota(jnp.int32, (1, LANES), 1)
        _attn_t_prep(cfg, dict(load_q=load_q, load_kv=load_kv, rope=rope_refs, negc=negc_ref,
                               block=pl.program_id(1)), S, Sk,
                     qT2s=qT2s, ks=ks, q2s=q2s, vs=vs, kTs=kTs, nb=nb)

        def prep_do(n, _):
            rows = pl.ds(pl.multiple_of(n * TQ, TQ), TQ)
            dob = do_ref[0, rows, :].astype(BF16)
            _store_stacked(cfg, lane, do2s, n, dob)
            doT = dob.astype(F32).T
            prodT = doT * o_ref[0, rows, :].T
            doTb = doT.astype(BF16)
            for hh in range(nh):
                hm = _head_rows(hh, pair)
                doT2s[n, :, hh * TQ:(hh + 1) * TQ] = jnp.where(hm, doTb, jnp.zeros_like(doTb))
                delta_s[pl.ds(n, 1), hh * TQ:(hh + 1) * TQ] = jnp.sum(jnp.where(hm, prodT, 0.0), axis=0, keepdims=True)
            return 0

        def zero_kv(n, _):
            rows = pl.ds(pl.multiple_of(n * TK, TK), TK)
            dk_acc[rows, :] = jnp.zeros((TK, LANES), F32)
            dv_acc[rows, :] = jnp.zeros((TK, LANES), F32)
            if has_bias:
                for hh in range(nh):
                    dneg_acc[hh, rows, :] = jnp.zeros((TK, LANES), F32)
            return 0

        lax.fori_loop(0, nq, prep_do, 0)
        lax.fori_loop(0, nk, zero_kv, 0)

        def cols(j):
            return pl.ds(pl.multiple_of(j * TK, TK), TK)

        def rows2(i):
            return pl.ds(pl.multiple_of(i * R, R), R)

        def park(i, dqT2, drow):
            dq_all[i] = dqT2
            if has_bias:
                drow_all[pl.ds(i, 1), :] = drow

        def half(i, j, i_prev, j_prev, cur, nxt_bufs, prv, dqT2, drow):
            s_cur, dp_cur, pb_cur, dsb_cur = cur
            s_next, dp_next = nxt_bufs[0], nxt_bufs[1]
            pb_prev, dsb_prev = prv[2], prv[3]
            i_n, j_n = successor(i, j)
            first = j == 0
            if has_bias:
                drow_all[pl.ds(i_prev, 1), :] = drow
            drow = jnp.where(first, 0.0, drow)
            kc = cols(j)
            sT = _bias_mask_t(cfg, s_cur[...], nb, mask_ref, kc, None if mask_index is None else mask_index(i, j))
            pT = jnp.exp(sT - lse_ref[0, 0, pl.ds(i, 1), :])
            dsT = pT * (dp_cur[...] - delta_s[pl.ds(i, 1), :])
            if has_bias:
                drow = drow + jnp.sum(dsT, axis=0, keepdims=True)
                for hh in range(nh):
                    part = dsT[:, hh * TQ:hh * TQ + LANES]
                    for t in range(1, TQ // LANES):
                        part = part + dsT[:, hh * TQ + t * LANES:hh * TQ + (t + 1) * LANES]
                    dneg_acc[hh, kc, :] += part
            if s_scale is not None:
                dsT = dsT * s_scale
            pb_cur[...] = pT.astype(BF16)
            dsb_cur[...] = dsT.astype(BF16)
            kp = cols(j_prev)
            dv_acc[kp, :] += jnp.dot(pb_prev[...], do2s[rows2(i_prev), :], preferred_element_type=F32)
            dk_acc[kp, :] += jnp.dot(dsb_prev[...], q2s[rows2(i_prev), :], preferred_element_type=F32)
            dq_full = dqT2 + jnp.dot(kTs[j_prev], dsb_prev[...], preferred_element_type=F32)
            dq_all[i_prev] = dq_full
            dqT2 = jnp.where(first, 0.0, dq_full)
            i_nc = jnp.minimum(i_n, nq - 1)
            kn = cols(j_n)
            s_next[...] = _raw_scores_t(cfg, ks[kn, :], qT2s[i_nc])
            dp_next[...] = jnp.dot(vs[kn, :], doT2s[i_nc], preferred_element_type=F32)
            return i_n, j_n, i, j, dqT2, drow

        def two(_, carry):
            i, j, i_prev, j_prev, dqT2, drow = carry
            i, j, i_prev, j_prev, dqT2, drow = half(i, j, i_prev, j_prev, bufs_a, bufs_b, bufs_b, dqT2, drow)
            return half(i, j, i_prev, j_prev, bufs_b, bufs_a, bufs_a, dqT2, drow)

        bufs_a[0][...] = _raw_scores_t(cfg, ks[cols(0), :], qT2s[0])
        bufs_a[1][...] = jnp.dot(vs[cols(0), :], doT2s[0], preferred_element_type=F32)
        bufs_b[2][...] = jnp.zeros((TK, R), BF16)
        bufs_b[3][...] = jnp.zeros((TK, R), BF16)
        zero = jnp.int32(0)
        init = (zero, zero, zero, zero, jnp.zeros((LANES, R), F32), jnp.zeros((1, R), F32))
        _, _, i_prev, j_prev, dqT2, drow = lax.fori_loop(0, n_pairs // 2, two, init)
        kp = cols(j_prev)
        dv_acc[kp, :] += jnp.dot(bufs_b[2][...], do2s[rows2(i_prev), :], preferred_element_type=F32)
        dk_acc[kp, :] += jnp.dot(bufs_b[3][...], q2s[rows2(i_prev), :], preferred_element_type=F32)
        park(i_prev, dqT2 + jnp.dot(kTs[j_prev], bufs_b[3][...], preferred_element_type=F32), drow)

        def fin_q(i, _):
            rows = pl.ds(pl.multiple_of(i * TQ, TQ), TQ)
            dqT2 = dq_all[i]
            dqT = jnp.where(_head_rows(0, True), dqT2[:, 0:TQ], dqT2[:, TQ:2 * TQ]) if pair else dqT2
            dq = dqT.T
            if q_fold is not None:
                dq = dq * q_fold
            if has_rope:
                dq = _rope_bwd(dq, *[t[rows, :] for t in rope_refs])
            if pair:
                dqkv_ref[0, rows, 0:LANES] = dq.astype(BF16)
            else:
                dq_ref[0, rows, :] = dq.astype(BF16)
            if has_bias:
                drow_ref[0, 0, pl.ds(i, 1), :] = drow_all[pl.ds(i, 1), :]
            return 0

        lax.fori_loop(0, nq, fin_q, 0)

        def fin_kv(n, _):
            rows = pl.ds(pl.multiple_of(n * TK, TK), TK)
            dk = dk_acc[rows, :]
            if has_rope:
                dk = _rope_bwd(dk, *[t[rows, :] for t in rope_refs])
            if pair:
                dqkv_ref[0, rows, LANES:2 * LANES] = dk.astype(BF16)
                dqkv_ref[0, rows, 2 * LANES:3 * LANES] = dv_acc[rows, :].astype(BF16)
            else:
                dk_ref[0, rows, :] = dk.astype(BF16)
                dv_ref[0, rows, :] = dv_acc[rows, :].astype(BF16)
            if has_bias:
                x0 = jnp.sum(dneg_acc[0, rows, :], axis=1, keepdims=True)
                x1 = jnp.sum(dneg_acc[1, rows, :], axis=1, keepdims=True)
                dneg_ref[0, rows, :] = jnp.where(lane == 0, x0, jnp.where(lane == 1, x1, 0.0))
            return 0

        lax.fori_loop(0, nk, fin_kv, 0)

    ins, in_specs = _attn_t_inputs(kind, src, S, negc_cols, mask, rope, kv)
    row_spec = pl.BlockSpec((1, S, LANES), lambda b, h: (b, 0, h))
    vec_spec = pl.BlockSpec((1, 1, nq, R), lambda b, h: (b, h, 0, 0))
    ins += [do, o, lse]
    in_specs += [row_spec, row_spec, vec_spec]
    if token is not None:
        ins.append(token)
        in_specs.append(pl.BlockSpec(token.shape, lambda b, h: (0, 0)))
    W = cfg["n_blocks"] * LANES
    if pair:
        out_specs = [pl.BlockSpec((1, S, PAIR_W), lambda b, h: (b, 0, h))]
        out_shape = [jax.ShapeDtypeStruct((B, S, 3 * W), BF16)]
        if has_bias:
            out_specs += [row_spec, vec_spec]
            out_shape += [jax.ShapeDtypeStruct((B, S, W), F32), jax.ShapeDtypeStruct((B, cfg["n_blocks"], nq, R), F32)]
    else:
        kv_spec = pl.BlockSpec((1, MEM_LEN, LANES), lambda b, h: (b, 0, h))
        out_specs = [row_spec, kv_spec, kv_spec]
        out_shape = [jax.ShapeDtypeStruct((B, S, W), BF16)] + [jax.ShapeDtypeStruct((B, MEM_LEN, W), BF16)] * 2
    scratch = [pltpu.VMEM((nq, LANES, R), BF16), pltpu.VMEM((Sk, LANES), BF16), pltpu.VMEM((nh * S, LANES), BF16),
               pltpu.VMEM((Sk, LANES), BF16), pltpu.VMEM((nk, LANES, TK), BF16), pltpu.VMEM((nq, LANES, R), BF16),
               pltpu.VMEM((nh * S, LANES), BF16), pltpu.VMEM((nq, R), F32),
               pltpu.VMEM((Sk, LANES), F32), pltpu.VMEM((Sk, LANES), F32)]
    pair_bufs = [pltpu.VMEM((TK, R), F32), pltpu.VMEM((TK, R), F32), pltpu.VMEM((TK, R), BF16), pltpu.VMEM((TK, R), BF16)]
    scratch += pair_bufs + pair_bufs + [pltpu.VMEM((nq, LANES, R), F32)]
    if has_bias:
        scratch += [pltpu.VMEM((nh, Sk, LANES), F32), pltpu.VMEM((nh, Sk, LANES), F32), pltpu.VMEM((nq, R), F32)]
    return pl.pallas_call(
        body, name=kind + "_attn_bwd", grid=(B, cfg["n_blocks"]),
        in_specs=in_specs, out_specs=out_specs, out_shape=out_shape, scratch_shapes=scratch,
        compiler_params=_params(("arbitrary", "arbitrary")),
    )(*ins)


def _sigmoid(g):
    return 1.0 / (1.0 + jnp.exp(-g))


def out_fwd(proj, o_fox, o_dil, o_mem, w_out, x, target, gf, tm):
    T = x.shape[0]

    def body(fg_ref, dg_ref, mg_ref, of_ref, od_ref, om_ref, w_ref, x_ref, t_ref, gf_ref,
             y_ref, dx_ref, dxb_ref, sm_ref):
        parts = []
        for g_ref, o_ref in ((fg_ref, of_ref), (dg_ref, od_ref), (mg_ref, om_ref)):
            g = g_ref[...]
            parts.append((o_ref[...] * (g * _sigmoid(g))).astype(BF16))
        ymix = jnp.concatenate(parts, axis=1)
        y_ref[...] = ymix
        x2 = x_ref[...] + jnp.dot(ymix, w_ref[...], preferred_element_type=F32)
        r = lax.rsqrt(jnp.mean(x2 * x2, axis=-1, keepdims=True) + RMS_EPS)
        yn = x2 * r
        err = yn * gf_ref[...] - t_ref[...]
        loss = 0.5 * jnp.sum(jnp.sum(err * err, axis=-1, keepdims=True) / D_MODEL, axis=0, keepdims=True)
        dyf = err / D_MODEL
        dgf = jnp.sum(dyf * yn, axis=0, keepdims=True)
        dyn = dyf * gf_ref[...]
        dx2 = r * (dyn - yn * jnp.mean(dyn * yn, axis=-1, keepdims=True))
        dx_ref[...] = dx2
        dxb_ref[...] = dx2.astype(BF16)
        row = lax.broadcasted_iota(jnp.int32, (8, D_MODEL), 0)
        upd = jnp.where(row == 0, dgf, jnp.where(row == 1, loss, 0.0))

        @pl.when(pl.program_id(0) == 0)
        def _():
            sm_ref[...] = upd

        @pl.when(pl.program_id(0) != 0)
        def _():
            sm_ref[...] += upd

    def rows(w, col=0):
        return pl.BlockSpec((tm, w), lambda i: (i, col))

    return pl.pallas_call(
        body, name="out_fwd", grid=(T // tm,),
        in_specs=[rows(FOX_W, P_FG // FOX_W), rows(DIL_W, P_DG // DIL_W), rows(MEM_W, P_MG // MEM_W),
                  rows(FOX_W), rows(DIL_W), rows(MEM_W),
                  pl.BlockSpec((MIX_W, D_MODEL), lambda i: (0, 0)),
                  rows(D_MODEL), rows(D_MODEL), pl.BlockSpec((1, D_MODEL), lambda i: (0, 0))],
        out_specs=[rows(MIX_W), rows(D_MODEL), rows(D_MODEL), pl.BlockSpec((8, D_MODEL), lambda i: (0, 0))],
        out_shape=[jax.ShapeDtypeStruct((T, MIX_W), BF16), jax.ShapeDtypeStruct((T, D_MODEL), F32),
                   jax.ShapeDtypeStruct((T, D_MODEL), BF16), jax.ShapeDtypeStruct((8, D_MODEL), F32)],
        compiler_params=_params(("arbitrary",)),
    )(proj, proj, proj, o_fox, o_dil, o_mem, w_out, x, target, gf)


def out_bwd(proj, o_fox, o_dil, o_mem, w_out, dx2b, tm):
    T = dx2b.shape[0]

    def body(fg_ref, dg_ref, mg_ref, of_ref, od_ref, om_ref, w_ref, dx_ref,
             dof_ref, dod_ref, dom_ref, dfg_ref, ddg_ref, dmg_ref):
        dmix = lax.dot_general(dx_ref[...], w_ref[...], (((1,), (1,)), ((), ())), preferred_element_type=F32)
        col = 0
        for g_ref, o_ref, do_ref, dgate_ref in ((fg_ref, of_ref, dof_ref, dfg_ref), (dg_ref, od_ref, dod_ref, ddg_ref),
                                                 (mg_ref, om_ref, dom_ref, dmg_ref)):
            w = g_ref.shape[1]
            d = dmix[:, col:col + w]
            col += w
            g = g_ref[...]
            sg = _sigmoid(g)
            do_ref[...] = d * (g * sg)
            dgate_ref[...] = (d * o_ref[...] * (sg * (1.0 + g * (1.0 - sg)))).astype(BF16)

    def rows(w, col=0):
        return pl.BlockSpec((tm, w), lambda i: (i, col))

    return pl.pallas_call(
        body, name="out_bwd", grid=(T // tm,),
        in_specs=[rows(FOX_W, P_FG // FOX_W), rows(DIL_W, P_DG // DIL_W), rows(MEM_W, P_MG // MEM_W),
                  rows(FOX_W), rows(DIL_W), rows(MEM_W),
                  pl.BlockSpec((MIX_W, D_MODEL), lambda i: (0, 0)), rows(D_MODEL)],
        out_specs=[rows(FOX_W), rows(DIL_W), rows(MEM_W), rows(FOX_W), rows(DIL_W), rows(MEM_W)],
        out_shape=[jax.ShapeDtypeStruct((T, FOX_W), F32), jax.ShapeDtypeStruct((T, DIL_W), F32),
                   jax.ShapeDtypeStruct((T, MEM_W), F32), jax.ShapeDtypeStruct((T, FOX_W), BF16),
                   jax.ShapeDtypeStruct((T, DIL_W), BF16), jax.ShapeDtypeStruct((T, MEM_W), BF16)],
        compiler_params=_params(("arbitrary",)),
    )(proj, proj, proj, o_fox, o_dil, o_mem, w_out, dx2b)


def out_step(proj, o_fox, o_dil, o_mem, w_out, x, target, gf, tm):
    T = x.shape[0]

    def body(fg_ref, dg_ref, mg_ref, of_ref, od_ref, om_ref, w_ref, x_ref, t_ref, gf_ref,
             dx_ref, dof_ref, dod_ref, dom_ref, dfg_ref, ddg_ref, dmg_ref, gw_ref, sm_ref, gw_acc):
        branches = []
        for g_ref, o_ref in ((fg_ref, of_ref), (dg_ref, od_ref), (mg_ref, om_ref)):
            g = g_ref[...]
            sg = _sigmoid(g)
            o = o_ref[...]
            branches.append((g, sg, o))
        ymix = jnp.concatenate([(o * (g * sg)).astype(BF16) for g, sg, o in branches], axis=1)
        x2 = x_ref[...] + jnp.dot(ymix, w_ref[...], preferred_element_type=F32)
        r = lax.rsqrt(jnp.mean(x2 * x2, axis=-1, keepdims=True) + RMS_EPS)
        yn = x2 * r
        err = yn * gf_ref[...] - t_ref[...]
        loss = 0.5 * jnp.sum(jnp.sum(err * err, axis=-1, keepdims=True) / D_MODEL, axis=0, keepdims=True)
        dyf = err / D_MODEL
        dgf = jnp.sum(dyf * yn, axis=0, keepdims=True)
        dyn = dyf * gf_ref[...]
        dx2 = r * (dyn - yn * jnp.mean(dyn * yn, axis=-1, keepdims=True))
        dx_ref[...] = dx2
        dxb = dx2.astype(BF16)
        dmix = lax.dot_general(dxb, w_ref[...], (((1,), (1,)), ((), ())), preferred_element_type=F32)
        col = 0
        for (g, sg, o), do_ref, dgate_ref in zip(branches, (dof_ref, dod_ref, dom_ref), (dfg_ref, ddg_ref, dmg_ref)):
            d = dmix[:, col:col + g.shape[1]]
            col += g.shape[1]
            do_ref[...] = d * (g * sg)
            dgate_ref[...] = (d * o * (sg * (1.0 + g * (1.0 - sg)))).astype(BF16)
        row = lax.broadcasted_iota(jnp.int32, (8, D_MODEL), 0)
        upd = jnp.where(row == 0, dgf, jnp.where(row == 1, loss, 0.0))

        @pl.when(pl.program_id(0) == 0)
        def _():
            sm_ref[...] = jnp.zeros(sm_ref.shape, F32)
            gw_acc[...] = jnp.zeros(gw_acc.shape, F32)

        sm_ref[...] += upd
        gw_acc[...] += lax.dot_general(ymix, dxb, (((0,), (0,)), ((), ())), preferred_element_type=F32)

        @pl.when(pl.program_id(0) == T // tm - 1)
        def _():
            gw_ref[...] = gw_acc[...].astype(BF16)

    def rows(w, col=0):
        return pl.BlockSpec((tm, w), lambda i: (i, col))

    return pl.pallas_call(
        body, name="out_step", grid=(T // tm,),
        in_specs=[rows(FOX_W, P_FG // FOX_W), rows(DIL_W, P_DG // DIL_W), rows(MEM_W, P_MG // MEM_W),
                  rows(FOX_W), rows(DIL_W), rows(MEM_W),
                  pl.BlockSpec((MIX_W, D_MODEL), lambda i: (0, 0)),
                  rows(D_MODEL), rows(D_MODEL), pl.BlockSpec((1, D_MODEL), lambda i: (0, 0))],
        out_specs=[rows(D_MODEL), rows(FOX_W), rows(DIL_W), rows(MEM_W), rows(FOX_W), rows(DIL_W), rows(MEM_W),
                   pl.BlockSpec((MIX_W, D_MODEL), lambda i: (0, 0)), pl.BlockSpec((8, D_MODEL), lambda i: (0, 0))],
        out_shape=[jax.ShapeDtypeStruct((T, D_MODEL), F32), jax.ShapeDtypeStruct((T, FOX_W), F32),
                   jax.ShapeDtypeStruct((T, DIL_W), F32), jax.ShapeDtypeStruct((T, MEM_W), F32),
                   jax.ShapeDtypeStruct((T, FOX_W), BF16), jax.ShapeDtypeStruct((T, DIL_W), BF16),
                   jax.ShapeDtypeStruct((T, MEM_W), BF16), jax.ShapeDtypeStruct((MIX_W, D_MODEL), BF16),
                   jax.ShapeDtypeStruct((8, D_MODEL), F32)],
        scratch_shapes=[pltpu.VMEM((MIX_W, D_MODEL), F32)],
        compiler_params=_params(("arbitrary",)),
    )(proj, proj, proj, o_fox, o_dil, o_mem, w_out, x, target, gf)


def adamw(w, g, m, v, tr, name):
    lead = w.shape[:-2]
    R, C = w.shape[-2:]
    zeros = (0,) * len(lead)

    def body(w_ref, g_ref, m_ref, v_ref, d_ref, mo_ref, vo_ref):
        gv = g_ref[...]
        mn = ADAM_B1 * m_ref[...] + (1.0 - ADAM_B1) * gv
        vn = ADAM_B2 * v_ref[...] + (1.0 - ADAM_B2) * jnp.square(gv)
        m_hat = mn / (1.0 - ADAM_B1 ** ADAM_STEP)
        v_hat = vn / (1.0 - ADAM_B2 ** ADAM_STEP)
        d_ref[...] = -ADAM_LR * (m_hat / (jnp.sqrt(v_hat) + ADAM_EPS) + ADAM_WD * w_ref[...])
        mo_ref[...] = mn
        vo_ref[...] = vn

    spec = pl.BlockSpec((1,) * len(lead) + (tr, C), lambda i: zeros + (i, 0))
    return pl.pallas_call(
        body, name=name, grid=(pl.cdiv(R, tr),),
        in_specs=[spec] * 4, out_specs=[spec] * 3,
        out_shape=[jax.ShapeDtypeStruct(w.shape, F32)] * 3,
        compiler_params=_params(("arbitrary",)),
    )(w, g, m, v)


def _pad_row(v, width):
    return jnp.concatenate([v, jnp.zeros((1, width - v.shape[1]), v.dtype)], axis=1)


def _old_local_grads(x, mem, norm_g, b_forget, mem_norm_g, final_norm_g, loss_target, w_in_p, w_kv, w_out):
    B, S, D = x.shape
    T = B * S
    xt = x.reshape(T, D)
    memt = mem.reshape(B * MEM_LEN, D)
    b_pad = _pad_row(b_forget, LANES)

    h = rms_fwd(xt, norm_g, 512, "rms_x")
    proj = mm_nn(h, w_in_p, 512, PW // 3, "in_proj")
    proj3 = proj.reshape(B, S, PW)
    mh = rms_fwd(memt, mem_norm_g, B * MEM_LEN, "rms_mem")
    mkv = mm_nn(mh, w_kv, B * MEM_LEN, 2 * MEM_W, "mem_kv_proj")
    mkv3 = mkv.reshape(B, MEM_LEN, 2 * MEM_W)

    negc = fox_gate(proj3, b_pad)
    causal = _log_masks_t(S, "causal")
    causal = jnp.concatenate([causal, jnp.zeros_like(causal)], axis=0)
    dilated = _log_masks_t(S, "dilated")
    rope = _rope_tables(S)

    o_fox, lse_fox = attn_fwd4("fox", proj3, S, negc_cols=negc, mask=causal)
    o_dil, lse_dil = attn_fwd4("dil", proj3, S, mask=dilated, rope=rope)
    o_mem, lse_mem = attn_fwd4("mem", proj3, S, kv=mkv3)

    dx2, do_fox, do_dil, do_mem, dfg, ddg, dmg, g_out, small_out = out_step(
        proj, o_fox.reshape(T, FOX_W), o_dil.reshape(T, DIL_W), o_mem.reshape(T, MEM_W), w_out,
        xt, loss_target.reshape(T, D), final_norm_g.reshape(1, D), 256)

    dqkv_fox, dneg, drow = attn_bwd3("fox", proj3, do_fox.reshape(B, S, FOX_W), o_fox, lse_fox, S,
                                     negc_cols=negc, mask=causal)
    (dqkv_dil,) = attn_bwd3("dil", proj3, do_dil.reshape(B, S, DIL_W), o_dil, lse_dil, S, mask=dilated, rope=rope)
    dmq, dmk, dmv = attn_bwd3("mem", proj3, do_mem.reshape(B, S, MEM_W), o_mem, lse_mem, S, kv=mkv3)
    drow = drow.reshape(B, FOX_HEADS // 2, S // TQ, 2, TQ).transpose(0, 1, 3, 2, 4).reshape(B, FOX_HEADS, S)
    drow = jnp.pad(drow, ((0, 0), (0, LANES - FOX_HEADS), (0, 0)))
    dflog, db_part = fox_gate_bwd(drow, dneg, proj3, b_pad)

    groups = [[(dfg, P_FG), (ddg, P_DG), (dmg, P_MG)],
              [(dqkv_fox.reshape(T, 3 * FOX_W), P_FOX), (dflog.reshape(T, LANES), P_FLOG)],
              [(dqkv_dil.reshape(T, 3 * DIL_W), P_DIL), (dmq.reshape(T, MEM_W), P_MQ)]]
    g_in = [mm_tn_multi(h_t,[arr for arr, _ in grp], 512, "w_in_grad_%d" % n) for n, grp in enumerate(groups)]
    grad_x, dng = in_proj_bwd_rms([piece for grp in groups for piece in grp], w_in_p, xt, norm_g, dx2, 256)

    dmkv = jnp.concatenate([dmk, dmv], axis=2).reshape(B * MEM_LEN, 2 * MEM_W)
    g_kv = mm_tn(mh, dmkv, B * MEM_LEN, 2 * MEM_W, "w_kv_grad")
    dmh = mm_nt(dmkv, w_kv, B * MEM_LEN, D, "mem_kv_bwd")
    _, dmng = rms_bwd(memt, mem_norm_g, dmh, None, B * MEM_LEN, "rms_mem_bwd")

    small = jnp.concatenate([dng[0:1], dmng[0:1], small_out[0:1], _pad_row(db_part[0:1], D), small_out[1:2],
                             jnp.zeros((3, D), F32)], axis=0)
    return grad_x.reshape(B, S, D), g_in, g_kv, g_out, small


def _old2_local_grads(x, mem, norm_g, b_forget, mem_norm_g, final_norm_g, loss_target, w_in_p, w_kv, w_out, start_exchange):
    B, S, D = x.shape
    T = B * S
    xt = x.reshape(T, D)
    memt = mem.reshape(B * MEM_LEN, D)
    b_pad = _pad_row(b_forget, LANES)

    h = rms_fwd(xt, norm_g, 512, "rms_x")
    proj = mm_nn(h, w_in_p, 512, PW // 3, "in_proj")
    proj3 = proj.reshape(B, S, PW)
    mh = rms_fwd(memt, mem_norm_g, B * MEM_LEN, "rms_mem")
    mkv = mm_nn(mh, w_kv, B * MEM_LEN, 2 * MEM_W, "mem_kv_proj")
    mkv3 = mkv.reshape(B, MEM_LEN, 2 * MEM_W)

    negc = fox_gate(proj3, b_pad)
    causal = _log_masks_t(S, "causal")
    causal = jnp.concatenate([causal, jnp.zeros_like(causal)], axis=0)
    dilated = _log_masks_t(S, "dilated")
    rope = _rope_tables(S)

    o_fox, lse_fox = attn_fwd4("fox", proj3, S, negc_cols=negc, mask=causal)
    o_dil, lse_dil = attn_fwd4("dil", proj3, S, mask=dilated, rope=rope)
    o_mem, lse_mem = attn_fwd4("mem", proj3, S, kv=mkv3)

    dx2, do_fox, do_dil, do_mem, dfg, ddg, dmg, g_out, small_out = out_step(
        proj, o_fox.reshape(T, FOX_W), o_dil.reshape(T, DIL_W), o_mem.reshape(T, MEM_W), w_out,
        xt, loss_target.reshape(T, D), final_norm_g.reshape(1, D), 256)

    gates = [(dfg, P_FG), (ddg, P_DG), (dmg, P_MG)]
    g_gates = mm_tn_multi(h_t, [arr for arr, _ in gates], 1024, "w_in_grad_gates", BF16)
    first, token = start_exchange([g_gates, g_out], "early_exchange_a")

    dqkv_fox, dneg, drow = attn_bwd3("fox", proj3, do_fox.reshape(B, S, FOX_W), o_fox, lse_fox, S,
                                     negc_cols=negc, mask=causal, token=token)
    drow = drow.reshape(B, FOX_HEADS // 2, S // TQ, 2, TQ).transpose(0, 1, 3, 2, 4).reshape(B, FOX_HEADS, S)
    drow = jnp.pad(drow, ((0, 0), (0, LANES - FOX_HEADS), (0, 0)))
    dflog, db_part = fox_gate_bwd(drow, dneg, proj3, b_pad)
    fox = [(dqkv_fox.reshape(T, 3 * FOX_W), P_FOX), (dflog.reshape(T, LANES), P_FLOG)]
    g_fox = mm_tn_multi(h_t, [arr for arr, _ in fox], 1024, "w_in_grad_fox", BF16)
    second, token = start_exchange([g_fox], "early_exchange_b")

    (dqkv_dil,) = attn_bwd3("dil", proj3, do_dil.reshape(B, S, DIL_W), o_dil, lse_dil, S, mask=dilated, rope=rope,
                            token=token)
    dmq, dmk, dmv = attn_bwd3("mem", proj3, do_mem.reshape(B, S, MEM_W), o_mem, lse_mem, S, kv=mkv3)
    rest = [(dqkv_dil.reshape(T, 3 * DIL_W), P_DIL), (dmq.reshape(T, MEM_W), P_MQ)]
    g_rest = mm_tn_multi(h_t,[arr for arr, _ in rest], 512, "w_in_grad_rest")
    grad_x, dng = in_proj_bwd_rms(gates + fox + rest, w_in_p, xt, norm_g, dx2, 256)

    dmkv = jnp.concatenate([dmk, dmv], axis=2).reshape(B * MEM_LEN, 2 * MEM_W)
    g_kv = mm_tn(mh, dmkv, B * MEM_LEN, 2 * MEM_W, "w_kv_grad")
    dmh = mm_nt(dmkv, w_kv, B * MEM_LEN, D, "mem_kv_bwd")
    _, dmng = rms_bwd(memt, mem_norm_g, dmh, None, B * MEM_LEN, "rms_mem_bwd")

    small = jnp.concatenate([dng[0:1], dmng[0:1], small_out[0:1], _pad_row(db_part[0:1], D), small_out[1:2],
                             jnp.zeros((3, D), F32)], axis=0)
    return grad_x.reshape(B, S, D), [(first, dqkv_fox), (second, dqkv_dil)], [g_rest, g_kv], small


def local_grads(x, mem, norm_g, b_forget, mem_norm_g, final_norm_g, loss_target, w_in_p, small_weights, start_exchange):
    B, S, D = x.shape
    T = B * S
    xt = x.reshape(T, D)
    memt = mem.reshape(B * MEM_LEN, D)
    b_pad = _pad_row(b_forget, LANES)

    h, h_t = rms_fwd(xt, norm_g, 512, "rms_x", with_transpose=True)
    proj = mm_nn(h, w_in_p, 512, PW // 3, "in_proj")
    proj3 = proj.reshape(B, S, PW)

    negc = fox_gate(proj3, b_pad)
    causal = _log_masks_t(S, "causal")
    causal = jnp.concatenate([causal, jnp.zeros_like(causal)], axis=0)
    dilated = _log_masks_t(S, "dilated")
    rope = _rope_tables(S)

    o_fox, lse_fox = attn_fwd4("fox", proj3, S, negc_cols=negc, mask=causal)
    o_dil, lse_dil = attn_fwd4("dil", proj3, S, mask=dilated, rope=rope)

    w_kv, w_out = small_weights(o_dil)
    mh = rms_fwd(memt, mem_norm_g, B * MEM_LEN, "rms_mem")
    mkv = mm_nn(mh, w_kv, B * MEM_LEN, 2 * MEM_W, "mem_kv_proj")
    mkv3 = mkv.reshape(B, MEM_LEN, 2 * MEM_W)
    o_mem, lse_mem = attn_fwd4("mem", proj3, S, kv=mkv3)

    dx2, do_fox, do_dil, do_mem, dfg, ddg, dmg, g_out, small_out = out_step(
        proj, o_fox.reshape(T, FOX_W), o_dil.reshape(T, DIL_W), o_mem.reshape(T, MEM_W), w_out,
        xt, loss_target.reshape(T, D), final_norm_g.reshape(1, D), 256)

    gates = [(dfg, P_FG), (ddg, P_DG), (dmg, P_MG)]
    g_gates = mm_tn_multi(h_t, [arr for arr, _ in gates], 1024, "w_in_grad_gates", BF16)
    first, token = start_exchange([g_gates, g_out], "early_exchange_a")

    dqkv_fox, dneg, drow = attn_bwd3("fox", proj3, do_fox.reshape(B, S, FOX_W), o_fox, lse_fox, S,
                                     negc_cols=negc, mask=causal, token=token)
    drow = drow.reshape(B, FOX_HEADS // 2, S // TQ, 2, TQ).transpose(0, 1, 3, 2, 4).reshape(B, FOX_HEADS, S)
    drow = jnp.pad(drow, ((0, 0), (0, LANES - FOX_HEADS), (0, 0)))
    dflog, db_part = fox_gate_bwd(drow, dneg, proj3, b_pad)
    fox = [(dqkv_fox.reshape(T, 3 * FOX_W), P_FOX), (dflog.reshape(T, LANES), P_FLOG)]
    g_fox = mm_tn_multi(h_t, [arr for arr, _ in fox], 1024, "w_in_grad_fox", BF16)
    second, token = start_exchange([g_fox], "early_exchange_b")

    (dqkv_dil,) = attn_bwd3("dil", proj3, do_dil.reshape(B, S, DIL_W), o_dil, lse_dil, S, mask=dilated, rope=rope,
                            token=token)
    dil = [(dqkv_dil.reshape(T, 3 * DIL_W), P_DIL)]
    g_dil = mm_tn_multi(h_t, [arr for arr, _ in dil], 1024, "w_in_grad_dil", BF16)
    third, token = start_exchange([g_dil], "early_exchange_c")

    dmq, dmk, dmv = attn_bwd3("mem", proj3, do_mem.reshape(B, S, MEM_W), o_mem, lse_mem, S, kv=mkv3, token=token)
    mq = [(dmq.reshape(T, MEM_W), P_MQ)]
    g_mq = mm_tn_multi(h_t, [arr for arr, _ in mq], 1024, "w_in_grad_mq")
    grad_x, dng = in_proj_bwd_rms(gates + fox + dil + mq, w_in_p, xt, norm_g, dx2, 256)

    dmkv = jnp.concatenate([dmk, dmv], axis=2).reshape(B * MEM_LEN, 2 * MEM_W)
    g_kv = mm_tn(mh, dmkv, B * MEM_LEN, 2 * MEM_W, "w_kv_grad")
    dmh = mm_nt(dmkv, w_kv, B * MEM_LEN, D, "mem_kv_bwd")
    _, dmng = rms_bwd(memt, mem_norm_g, dmh, None, B * MEM_LEN, "rms_mem_bwd")

    small = jnp.concatenate([dng[0:1], dmng[0:1], small_out[0:1], _pad_row(db_part[0:1], D), small_out[1:2],
                             jnp.zeros((3, D), F32)], axis=0)
    early = [(first, dqkv_fox), (second, dqkv_dil), (third, dmq)]
    return grad_x.reshape(B, S, D), early, [g_mq, g_kv], small


def kernel(x, mem, norm_g, w_in, b_forget, mem_norm_g, w_mem_kv, w_out, final_norm_g, loss_target, m_norm_g, m_w_in, m_b_forget, m_mem_norm_g, m_w_mem_kv, m_w_out, m_final_norm_g, v_norm_g, v_w_in, v_b_forget, v_mem_norm_g, v_w_mem_kv, v_w_out, v_final_norm_g):
    D = D_MODEL
    (w_in_full,) = weight_gather([_pack_cols(w_in).astype(BF16).reshape(w_in.shape[1], PW)])
    gather, _ = early_exchange_start([w_mem_kv[0].astype(BF16), w_out[0].astype(BF16)], "early_gather", gather=True,
                                     after=w_in_full)

    def small_weights(after):
        _, gathered = early_exchange_wait(gather, after, "early_gather_wait")
        return gathered

    grad_x, early, late, small = local_grads(
        x, mem, norm_g, b_forget, mem_norm_g, final_norm_g, loss_target, w_in_full, small_weights,
        early_exchange_start)

    (first, after_first), (second, after_second), (third, after_third) = early
    (src_gates, src_out), (land_gates, land_out) = early_exchange_wait(first, after_first, "early_wait_a")
    (src_fox,), (land_fox,) = early_exchange_wait(second, after_second, "early_wait_b")
    (src_dil,), (land_dil,) = early_exchange_wait(third, after_third, "early_wait_c")
    gates = slot_sum8(src_gates, land_gates, 128, "sum_w_in_gates")
    gw_out = slot_sum8(src_out, land_out, 256, "sum_w_out")
    fox = slot_sum8(src_fox, land_fox, 128, "sum_w_in_fox")
    dil = slot_sum8(src_dil, land_dil, 128, "sum_w_in_dil")

    *from_sibling, csum = grad_exchange_d2d(late, small)
    names = ("w_in_mq", "w_kv")
    chip_parts = [chip_sum(g, got, 128, "chip_sum_" + n) for g, got, n in zip(late, from_sibling, names)]
    *from_chips, tot = grad_exchange_ici(chip_parts, csum)
    mq, gw_kv = [final_sum(got, 128, "final_sum_" + n) for got, n in zip(from_chips, names)]
    gw_in = _unpack_cols(jnp.concatenate(
        [fox[:, :3 * FOX_W], gates[:, :FOX_W], dil, gates[:, FOX_W:FOX_W + DIL_W], mq,
         gates[:, FOX_W + DIL_W:], fox[:, 3 * FOX_W:]], axis=1)[None])

    loss = tot[4, 0]
    g_norm, g_mem_norm, g_final, g_b = tot[0:1], tot[1:2], tot[2], tot[3:4, :FOX_HEADS]

    def rows8(*rows):
        rows = [r.reshape(1, -1) for r in rows]
        rows = [_pad_row(r, D) for r in rows]
        return jnp.concatenate(rows + [jnp.zeros((8 - len(rows), D), F32)], axis=0)

    sw = rows8(norm_g, mem_norm_g, final_norm_g, b_forget)
    sm = rows8(m_norm_g, m_mem_norm_g, m_final_norm_g, m_b_forget)
    sv = rows8(v_norm_g, v_mem_norm_g, v_final_norm_g, v_b_forget)
    d_s, m_s, v_s = adamw(sw, tot, sm, sv, 8, "adamw_small")
    d_in, m_in, v_in = adamw(w_in, gw_in, m_w_in, v_w_in, 32, "adamw_w_in")
    d_kv, m_kv, v_kv = adamw(w_mem_kv[0], gw_kv, m_w_mem_kv[0], v_w_mem_kv[0], 128, "adamw_w_kv")
    d_out, m_out, v_out = adamw(w_out[0], gw_out, m_w_out[0], v_w_out[0], 256, "adamw_w_out")

    def small_outs(t):
        return t[0:1], t[3:4, :FOX_HEADS], t[1:2], t[2]

    grads = (g_norm, gw_in, g_b, g_mem_norm, gw_kv[None], gw_out[None], g_final)
    outs = []
    for t, big in ((d_s, (d_in, d_kv, d_out)), (m_s, (m_in, m_kv, m_out)), (v_s, (v_in, v_kv, v_out))):
        n, b, mn, f = small_outs(t)
        outs += [n, big[0], b, mn, big[1][None], big[2][None], f]
    return (loss, grad_x, *grads, *outs)
```

```python
import functools
import math

import numpy as np
import jax
import jax.numpy as jnp
from jax import lax
from jax.experimental import pallas as pl
from jax.experimental.pallas import tpu as pltpu

F32 = jnp.float32
BF16 = jnp.bfloat16

D_MODEL = 1024
HEAD_DIM = 64
FOX_HEADS = 12
DIL_HEADS = 12
MEM_HEADS = 4
MEM_HEAD_DIM = 128
MEM_LEN = 256
FOX_W = FOX_HEADS * HEAD_DIM
DIL_W = DIL_HEADS * HEAD_DIM
MEM_W = MEM_HEADS * MEM_HEAD_DIM
MIX_W = FOX_W + DIL_W + MEM_W
DILATIONS = ((128, 1), (512, 4), (2048, 16))
ROPE_THETA = 500000.0
ROPE_DIM = HEAD_DIM // 4
RMS_EPS = 1e-6
NEG_INF = -1e30
IN_W = 4 * FOX_W + FOX_HEADS + 4 * DIL_W + 2 * MEM_W

ADAM_LR = 0.001
ADAM_B1 = 0.9
ADAM_B2 = 0.999
ADAM_EPS = 1e-08
ADAM_WD = 0.01
ADAM_STEP = 10

N_DEV = 8
LANES = 128
PAIR_W = 3 * LANES
TQ = 256
TK = 256

O_FQ, O_FK, O_FV, O_FG = 0, FOX_W, 2 * FOX_W, 3 * FOX_W
O_FLOG = 4 * FOX_W
O_DQ = O_FLOG + FOX_HEADS
O_DK, O_DV, O_DG = O_DQ + DIL_W, O_DQ + 2 * DIL_W, O_DQ + 3 * DIL_W
O_MQ = O_DQ + 4 * DIL_W
O_MG = O_MQ + MEM_W
P_FOX = 0
P_FG = P_FOX + 3 * FOX_W
P_DIL = P_FG + FOX_W
P_DG = P_DIL + 3 * DIL_W
P_MQ = P_DG + DIL_W
P_MG = P_MQ + MEM_W
P_FLOG = P_MG + MEM_W
PW = P_FLOG + LANES

VMEM_LIMIT = 56 * 1024 * 1024


def _pack_pieces():
    pieces = []
    for base in (O_FQ, O_DQ):
        seg = []
        for hp in range(FOX_HEADS // 2):
            for part in range(3):
                seg.append((base + part * FOX_W + hp * LANES, LANES))
        pieces.append(seg)
    fox, dil = pieces
    return fox + [(O_FG, FOX_W)] + dil + [(O_DG, DIL_W), (O_MQ, MEM_W), (O_MG, MEM_W), (O_FLOG, FOX_HEADS)]


def _pack_cols(w):
    parts = [w[..., s:s + n] for s, n in _pack_pieces()]
    parts.append(jnp.zeros(w.shape[:-1] + (LANES - FOX_HEADS,), w.dtype))
    return jnp.concatenate(parts, axis=-1)


def _unpack_cols(g):
    runs = []
    pos = 0
    for s, n in _pack_pieces():
        runs.append((s, n, pos))
        pos += n
    runs.sort()
    return jnp.concatenate([g[..., p:p + n] for s, n, p in runs], axis=-1)


def _params(sem=None, **kw):
    return pltpu.CompilerParams(dimension_semantics=sem, vmem_limit_bytes=VMEM_LIMIT, **kw)


def _mesh_pos():
    return lax.axis_index("x"), lax.axis_index("y"), lax.axis_index("c")


def _flip(v, d):
    return 1 - v if d else v


_RELATIONS = [(dx, dy, dc) for dx in (0, 1) for dy in (0, 1) for dc in (0, 1)][1:]


def weight_gather(shards):
    n_arr = len(shards)
    rows = [s.shape[0] for s in shards]

    def body(*refs):
        in_refs = refs[:n_arr]
        out_refs = refs[n_arr:2 * n_arr]
        send_sems, recv_sems, local_sems = refs[2 * n_arr:]
        x, y, c = _mesh_pos()
        me, sibling = (x, y, c), (x, y, 1 - c)
        x_nbr, y_nbr, diag = (1 - x, y, c), (x, 1 - y, c), (1 - x, 1 - y, c)
        north = c == 1
        relay_from = (jnp.where(north, 1 - x, x), jnp.where(north, y, 1 - y), c)
        relay_to = (jnp.where(north, x, 1 - x), jnp.where(north, 1 - y, y), c)
        k_from = jnp.where(north, 1, 2)
        k_to = 3 - k_from

        def block(a, pos):
            px, py, pc = pos
            return out_refs[a].at[pl.ds((4 * px + 2 * py + pc) * rows[a], rows[a]), :]

        def copy(a, k, blk, to, src=None):
            return pltpu.make_async_remote_copy(
                src_ref=block(a, blk) if src is None else src, dst_ref=block(a, blk),
                send_sem=send_sems.at[a, k], recv_sem=recv_sems.at[a, k],
                device_id=to, device_id_type=pl.DeviceIdType.MESH)

        started = []
        mine = []
        for a in range(n_arr):
            cp = pltpu.make_async_copy(in_refs[a], block(a, me), local_sems.at[a])
            cp.start()
            mine.append(cp)
            first = [copy(a, 0, me, sibling, src=in_refs[a]), copy(a, 1, me, x_nbr, src=in_refs[a]),
                     copy(a, 2, me, y_nbr, src=in_refs[a])]
            for cp in first:
                cp.start()
            started += first
        for a in range(n_arr):
            copy(a, k_from, relay_from, me).wait_recv()
            second_hop = copy(a, 3, relay_from, relay_to)
            second_hop.start()
            passed = copy(a, 3 + k_from, relay_from, sibling)
            passed.start()
            started += [second_hop, passed]
        for a in range(n_arr):
            copy(a, k_to, relay_to, me).wait_recv()
            passed = copy(a, 3 + k_to, relay_to, sibling)
            passed.start()
            started.append(passed)
        for a in range(n_arr):
            copy(a, 3, diag, me).wait_recv()
            passed = copy(a, 6, diag, sibling)
            passed.start()
            started.append(passed)
        for a in range(n_arr):
            copy(a, 0, sibling, me).wait_recv()
            for k, chip in ((4, x_nbr), (5, y_nbr), (6, diag)):
                copy(a, k, (chip[0], chip[1], 1 - c), me).wait_recv()
        for cp in started:
            cp.wait_send()
        for cp in mine:
            cp.wait()

    any_spec = pl.BlockSpec(memory_space=pl.ANY)
    return pl.pallas_call(
        body, name="weight_gather",
        out_shape=[jax.ShapeDtypeStruct((N_DEV * s.shape[0], s.shape[1]), s.dtype) for s in shards],
        in_specs=[any_spec] * n_arr, out_specs=[any_spec] * n_arr,
        scratch_shapes=[pltpu.SemaphoreType.DMA((n_arr, 7)), pltpu.SemaphoreType.DMA((n_arr, 7)),
                        pltpu.SemaphoreType.DMA((n_arr,))],
    )(*shards)


def grad_exchange(grads, small):
    arrs = list(grads) + [small]
    n_arr = len(arrs)
    rows = [g.shape[0] // N_DEV for g in grads] + [small.shape[0]]

    def body(*refs):
        in_refs = refs[:n_arr]
        out_refs = refs[n_arr:2 * n_arr]
        send_sems, recv_sems, local_sems = refs[2 * n_arr:]
        x, y, c = _mesh_pos()
        me = 4 * x + 2 * y + c

        def src(a, idx):
            if a == n_arr - 1:
                return in_refs[a]
            return in_refs[a].at[pl.ds(idx * rows[a], rows[a]), :]

        def copy(a, k):
            dx, dy, dc = _RELATIONS[k]
            px, py, pc = _flip(x, dx), _flip(y, dy), _flip(c, dc)
            peer = 4 * px + 2 * py + pc
            send = pltpu.make_async_remote_copy(
                src_ref=src(a, peer), dst_ref=out_refs[a].at[me],
                send_sem=send_sems.at[a, k], recv_sem=recv_sems.at[a, k],
                device_id=(px, py, pc), device_id_type=pl.DeviceIdType.MESH)
            recv = pltpu.make_async_remote_copy(
                src_ref=src(a, peer), dst_ref=out_refs[a].at[peer],
                send_sem=send_sems.at[a, k], recv_sem=recv_sems.at[a, k],
                device_id=(px, py, pc), device_id_type=pl.DeviceIdType.MESH)
            return send, recv

        mine = []
        pairs = []
        for a in range(n_arr):
            cp = pltpu.make_async_copy(src(a, me), out_refs[a].at[me], local_sems.at[a])
            cp.start()
            mine.append(cp)
            for k in range(7):
                send, recv = copy(a, k)
                send.start()
                pairs.append((send, recv))
        for send, recv in pairs:
            recv.wait_recv()
        for send, recv in pairs:
            send.wait_send()
        for cp in mine:
            cp.wait()

    any_spec = pl.BlockSpec(memory_space=pl.ANY)
    return pl.pallas_call(
        body, name="grad_exchange",
        out_shape=[jax.ShapeDtypeStruct((N_DEV, r, a.shape[1]), a.dtype) for r, a in zip(rows, arrs)],
        in_specs=[any_spec] * n_arr, out_specs=[any_spec] * n_arr,
        scratch_shapes=[pltpu.SemaphoreType.DMA((n_arr, 7)), pltpu.SemaphoreType.DMA((n_arr, 7)),
                        pltpu.SemaphoreType.DMA((n_arr,))],
    )(*arrs)


def slot_sum(slots, tr, name):
    _, R, C = slots.shape

    def body(s_ref, o_ref):
        acc = s_ref[0]
        for d in range(1, N_DEV):
            acc = acc + s_ref[d]
        o_ref[...] = acc

    return pl.pallas_call(
        body, name=name, grid=(R // tr,),
        in_specs=[pl.BlockSpec((N_DEV, tr, C), lambda i: (0, i, 0))],
        out_specs=pl.BlockSpec((tr, C), lambda i: (i, 0)),
        out_shape=jax.ShapeDtypeStruct((R, C), slots.dtype),
        compiler_params=_params(("arbitrary",)),
    )(slots)


N_CHIP = 4
_OTHER_CHIPS = [(1, 0), (0, 1), (1, 1)]


def grad_exchange_d2d(grads, small):
    n_big = len(grads)
    rows = [g.shape[0] // N_DEV for g in grads]

    def body(*refs):
        g_refs = refs[:n_big]
        small_ref = refs[n_big]
        out_refs = refs[n_big + 1:2 * n_big + 1]
        csum_ref = refs[2 * n_big + 1]
        land, send_sems, recv_sems = refs[2 * n_big + 2:]
        x, y, c = _mesh_pos()
        sibling = (x, y, 1 - c)
        copies = []
        for a in range(n_big):
            for q in range(N_CHIP):
                copies.append(pltpu.make_async_remote_copy(
                    src_ref=g_refs[a].at[pl.ds((2 * q + 1 - c) * rows[a], rows[a]), :], dst_ref=out_refs[a].at[q],
                    send_sem=send_sems.at[a, q], recv_sem=recv_sems.at[a, q],
                    device_id=sibling, device_id_type=pl.DeviceIdType.MESH))
        copies.append(pltpu.make_async_remote_copy(
            src_ref=small_ref, dst_ref=land, send_sem=send_sems.at[n_big, 0], recv_sem=recv_sems.at[n_big, 0],
            device_id=sibling, device_id_type=pl.DeviceIdType.MESH))
        for cp in copies:
            cp.start()
        for cp in copies:
            cp.wait_recv()
        for cp in copies:
            cp.wait_send()
        csum_ref[...] = small_ref[...] + land[...]

    any_spec = pl.BlockSpec(memory_space=pl.ANY)
    vmem_spec = pl.BlockSpec(memory_space=pltpu.VMEM)
    return pl.pallas_call(
        body, name="grad_exchange_d2d",
        out_shape=[jax.ShapeDtypeStruct((N_CHIP, r, g.shape[1]), g.dtype) for r, g in zip(rows, grads)]
        + [jax.ShapeDtypeStruct(small.shape, small.dtype)],
        in_specs=[any_spec] * n_big + [vmem_spec], out_specs=[any_spec] * n_big + [vmem_spec],
        scratch_shapes=[pltpu.VMEM(small.shape, small.dtype),
                        pltpu.SemaphoreType.DMA((n_big + 1, N_CHIP)), pltpu.SemaphoreType.DMA((n_big + 1, N_CHIP))],
    )(*grads, small)


def chip_sum(g, got, tr, name):
    _, rows, cols = got.shape
    g4 = g.reshape(N_CHIP, 2, rows, cols)
    core = lax.axis_index("c").astype(jnp.int32).reshape(1)

    def body(c_ref, g_ref, r_ref, o_ref):
        o_ref[0] = (g_ref[0, 0] + r_ref[0]).astype(BF16)

    return pl.pallas_call(
        body, name=name,
        grid_spec=pltpu.PrefetchScalarGridSpec(
            num_scalar_prefetch=1, grid=(N_CHIP, rows // tr),
            in_specs=[pl.BlockSpec((1, 1, tr, cols), lambda q, i, w: (q, w[0], i, 0)),
                      pl.BlockSpec((1, tr, cols), lambda q, i, w: (q, i, 0))],
            out_specs=pl.BlockSpec((1, tr, cols), lambda q, i, w: (q, i, 0))),
        out_shape=jax.ShapeDtypeStruct((N_CHIP, rows, cols), BF16),
        compiler_params=_params(("arbitrary", "arbitrary")),
    )(core, g4, got)


def grad_exchange_ici(parts, csum):
    n_big = len(parts)

    def body(*refs):
        p_refs = refs[:n_big]
        csum_ref = refs[n_big]
        out_refs = refs[n_big + 1:2 * n_big + 1]
        tot_ref = refs[2 * n_big + 1]
        land, send_sems, recv_sems, local_sems = refs[2 * n_big + 2:]
        x, y, c = _mesh_pos()
        q_me = 2 * x + y
        land[q_me] = csum_ref[...]
        mine = [pltpu.make_async_copy(p_refs[a].at[q_me], out_refs[a].at[q_me], local_sems.at[a]) for a in range(n_big)]
        for cp in mine:
            cp.start()
        sends, recvs = [], []
        for j, (dx, dy) in enumerate(_OTHER_CHIPS):
            px, py = _flip(x, dx), _flip(y, dy)
            q_peer = 2 * px + py
            for a in range(n_big + 1):
                src = p_refs[a].at[q_peer] if a < n_big else csum_ref
                dst = out_refs[a] if a < n_big else land
                common = dict(send_sem=send_sems.at[a, j], recv_sem=recv_sems.at[a, j],
                              device_id=(px, py, c), device_id_type=pl.DeviceIdType.MESH)
                sends.append(pltpu.make_async_remote_copy(src_ref=src, dst_ref=dst.at[q_me], **common))
                recvs.append(pltpu.make_async_remote_copy(src_ref=src, dst_ref=dst.at[q_peer], **common))
        for cp in sends:
            cp.start()
        for cp in recvs:
            cp.wait_recv()
        for cp in sends:
            cp.wait_send()
        for cp in mine:
            cp.wait()
        tot = land[0]
        for q in range(1, N_CHIP):
            tot = tot + land[q]
        tot_ref[...] = tot

    any_spec = pl.BlockSpec(memory_space=pl.ANY)
    vmem_spec = pl.BlockSpec(memory_space=pltpu.VMEM)
    return pl.pallas_call(
        body, name="grad_exchange_ici",
        out_shape=[jax.ShapeDtypeStruct(p.shape, p.dtype) for p in parts] + [jax.ShapeDtypeStruct(csum.shape, csum.dtype)],
        in_specs=[any_spec] * n_big + [vmem_spec], out_specs=[any_spec] * n_big + [vmem_spec],
        scratch_shapes=[pltpu.VMEM((N_CHIP,) + csum.shape, csum.dtype),
                        pltpu.SemaphoreType.DMA((n_big + 1, 3)), pltpu.SemaphoreType.DMA((n_big + 1, 3)),
                        pltpu.SemaphoreType.DMA((max(n_big, 1),))],
    )(*parts, csum)


def final_sum(got, tr, name):
    _, rows, cols = got.shape

    def body(got_ref, o_ref):
        acc = got_ref[0].astype(F32)
        for q in range(1, N_CHIP):
            acc = acc + got_ref[q].astype(F32)
        o_ref[...] = acc

    return pl.pallas_call(
        body, name=name, grid=(rows // tr,),
        in_specs=[pl.BlockSpec((N_CHIP, tr, cols), lambda i: (0, i, 0))],
        out_specs=pl.BlockSpec((tr, cols), lambda i: (i, 0)),
        out_shape=jax.ShapeDtypeStruct((rows, cols), F32),
        compiler_params=_params(("arbitrary",)),
    )(got)


_HBM = pl.BlockSpec(memory_space=pltpu.HBM)
_SEM = pl.BlockSpec(memory_space=pltpu.SEMAPHORE)
_EFFECT = pltpu.SideEffectType.DATAFLOW_SIDE_EFFECTING


def _old_early_copies(src_refs, land_refs, send_sems, recv_sems, rows):
    x, y, c = _mesh_pos()
    me = 4 * x + 2 * y + c
    copies = []
    for a in range(len(src_refs)):
        for k, (dx, dy, dc) in enumerate(_RELATIONS):
            px, py, pc = _flip(x, dx), _flip(y, dy), _flip(c, dc)
            peer = 4 * px + 2 * py + pc
            copies.append(pltpu.make_async_remote_copy(
                src_ref=src_refs[a].at[pl.ds(peer * rows[a], rows[a]), :], dst_ref=land_refs[a].at[me],
                send_sem=send_sems.at[a, k], recv_sem=recv_sems.at[a, k],
                device_id=(px, py, pc), device_id_type=pl.DeviceIdType.MESH))
    return copies


def _old_early_exchange_start(srcs, name):
    n = len(srcs)
    rows = [s.shape[0] // N_DEV for s in srcs]
    lands = [lax.empty((N_DEV, r, s.shape[1]), s.dtype) for r, s in zip(rows, srcs)]

    def body(*refs):
        src_refs, land_refs = refs[:n], refs[n:2 * n]
        send_sems, recv_sems = refs[2 * n], refs[2 * n + 1]
        token = refs[-1]
        for cp in _early_copies(src_refs, land_refs, send_sems, recv_sems, rows):
            cp.start()
        token[...] = jnp.zeros_like(token)

    hbm = lambda a: pltpu.HBM(a.shape, a.dtype)
    outs = pl.pallas_call(
        body, name=name,
        out_shape=[pltpu.SemaphoreType.DMA((n, 7)), pltpu.SemaphoreType.DMA((n, 7))]
        + [hbm(a) for a in srcs] + [hbm(a) for a in lands] + [jax.ShapeDtypeStruct((8, LANES), F32)],
        in_specs=[_HBM] * (2 * n),
        out_specs=[_SEM, _SEM] + [_HBM] * (2 * n) + [pl.BlockSpec(memory_space=pltpu.VMEM)],
        input_output_aliases={i: 2 + i for i in range(2 * n)},
        compiler_params=pltpu.CompilerParams(has_side_effects=_EFFECT),
    )(*[pltpu.with_memory_space_constraint(a, pltpu.HBM) for a in list(srcs) + lands])
    return dict(sems=outs[:2], srcs=outs[2:2 + n], lands=outs[2 + n:2 + 2 * n], rows=rows), outs[-1]


def _old_early_exchange_wait(handle, after, name):
    n = len(handle["srcs"])
    rows = handle["rows"]

    def body(*refs):
        src_refs, land_refs = refs[:n], refs[n:2 * n]
        send_sems, recv_sems = refs[2 * n], refs[2 * n + 1]
        for cp in _early_copies(src_refs, land_refs, send_sems, recv_sems, rows):
            cp.wait_send()
            cp.wait_recv()

    hbm = lambda a: pltpu.HBM(a.shape, a.dtype)
    ins = list(handle["srcs"]) + list(handle["lands"])
    outs = pl.pallas_call(
        body, name=name,
        out_shape=[hbm(a) for a in ins],
        in_specs=[_HBM] * (2 * n) + [_SEM, _SEM, pl.BlockSpec(memory_space=pl.ANY)],
        out_specs=[_HBM] * (2 * n),
        input_output_aliases={i: i for i in range(2 * n)},
        compiler_params=pltpu.CompilerParams(has_side_effects=_EFFECT),
    )(*ins, *handle["sems"], after)
    return outs[:n], outs[n:]


def _old_slot_sum8(src, land, tr, name):
    _, rows, cols = land.shape
    x, y, c = _mesh_pos()
    me = (4 * x + 2 * y + c).astype(jnp.int32).reshape(1)

    def body(me_ref, src_ref, land_ref, o_ref):
        acc = None
        for d in range(N_DEV):
            term = jnp.where(d == me_ref[0], src_ref[0], land_ref[d]).astype(F32)
            acc = term if acc is None else acc + term
        o_ref[...] = acc

    return pl.pallas_call(
        body, name=name,
        grid_spec=pltpu.PrefetchScalarGridSpec(
            num_scalar_prefetch=1, grid=(rows // tr,),
            in_specs=[pl.BlockSpec((1, tr, cols), lambda i, w: (w[0], i, 0)),
                      pl.BlockSpec((N_DEV, tr, cols), lambda i, w: (0, i, 0))],
            out_specs=pl.BlockSpec((tr, cols), lambda i, w: (i, 0))),
        out_shape=jax.ShapeDtypeStruct((rows, cols), F32),
        compiler_params=_params(("arbitrary",)),
    )(me, src.reshape(N_DEV, rows, cols), land)


def _old2_early_copies(src_refs, land_refs, send_sems, recv_sems, rows):
    x, y, c = _mesh_pos()
    me = 4 * x + 2 * y + c
    copies = []
    for a in range(len(src_refs)):
        for dx, dy, dc in _RELATIONS:
            px, py, pc = _flip(x, dx), _flip(y, dy), _flip(c, dc)
            peer = 4 * px + 2 * py + pc
            copies.append(pltpu.make_async_remote_copy(
                src_ref=src_refs[a].at[pl.ds(peer * rows[a], rows[a]), :],
                dst_ref=land_refs[a].at[pl.ds(me * rows[a], rows[a]), :],
                send_sem=send_sems[a], recv_sem=recv_sems[a],
                device_id=(px, py, pc), device_id_type=pl.DeviceIdType.MESH))
    return copies


def _old2_early_exchange_start(srcs, name):
    n = len(srcs)
    rows = [s.shape[0] // N_DEV for s in srcs]
    lands = [lax.empty(s.shape, s.dtype) for s in srcs]

    def body(*refs):
        src_refs, land_refs = refs[:n], refs[n:2 * n]
        send_sems, recv_sems = refs[2 * n:3 * n], refs[3 * n:4 * n]
        token = refs[-1]
        for cp in _early_copies(src_refs, land_refs, send_sems, recv_sems, rows):
            cp.start()
        token[...] = jnp.zeros_like(token)

    hbm = lambda a: pltpu.HBM(a.shape, a.dtype)
    outs = pl.pallas_call(
        body, name=name,
        out_shape=[pltpu.SemaphoreType.DMA(())] * (2 * n)
        + [hbm(a) for a in srcs] + [hbm(a) for a in lands] + [jax.ShapeDtypeStruct((8, LANES), F32)],
        in_specs=[_HBM] * (2 * n),
        out_specs=[_SEM] * (2 * n) + [_HBM] * (2 * n) + [pl.BlockSpec(memory_space=pltpu.VMEM)],
        input_output_aliases={i: 2 * n + i for i in range(2 * n)},
        compiler_params=pltpu.CompilerParams(has_side_effects=_EFFECT),
    )(*[pltpu.with_memory_space_constraint(a, pltpu.HBM) for a in list(srcs) + lands])
    return dict(sems=outs[:2 * n], srcs=outs[2 * n:3 * n], lands=outs[3 * n:4 * n], rows=rows), outs[-1]


def _early_copies(src_refs, land_refs, send_sems, recv_sems, rows, gather):
    x, y, c = _mesh_pos()
    me = 4 * x + 2 * y + c
    copies = []
    for a in range(len(src_refs)):
        for dx, dy, dc in _RELATIONS:
            px, py, pc = _flip(x, dx), _flip(y, dy), _flip(c, dc)
            peer = 4 * px + 2 * py + pc
            copies.append(pltpu.make_async_remote_copy(
                src_ref=src_refs[a] if gather else src_refs[a].at[pl.ds(peer * rows[a], rows[a]), :],
                dst_ref=land_refs[a].at[pl.ds(me * rows[a], rows[a]), :],
                send_sem=send_sems[a], recv_sem=recv_sems[a],
                device_id=(px, py, pc), device_id_type=pl.DeviceIdType.MESH))
    return copies


def early_exchange_start(srcs, name, gather=False, after=None):
    n = len(srcs)
    if gather:
        rows = [s.shape[0] for s in srcs]
        me = 4 * lax.axis_index("x") + 2 * lax.axis_index("y") + lax.axis_index("c")
        lands = [lax.dynamic_update_slice(lax.empty((N_DEV * r, s.shape[1]), s.dtype), s, (me * r, 0))
                 for r, s in zip(rows, srcs)]
    else:
        rows = [s.shape[0] // N_DEV for s in srcs]
        lands = [lax.empty(s.shape, s.dtype) for s in srcs]

    extra = [] if after is None else [after]

    def body(*refs):
        src_refs, land_refs = refs[:n], refs[n:2 * n]
        first_sem = 2 * n + len(extra)
        send_sems, recv_sems = refs[first_sem:first_sem + n], refs[first_sem + n:first_sem + 2 * n]
        token = refs[-1]
        for cp in _early_copies(src_refs, land_refs, send_sems, recv_sems, rows, gather):
            cp.start()
        token[...] = jnp.zeros_like(token)

    hbm = lambda a: pltpu.HBM(a.shape, a.dtype)
    outs = pl.pallas_call(
        body, name=name,
        out_shape=[pltpu.SemaphoreType.DMA(())] * (2 * n)
        + [hbm(a) for a in srcs] + [hbm(a) for a in lands] + [jax.ShapeDtypeStruct((8, LANES), F32)],
        in_specs=[_HBM] * (2 * n) + [pl.BlockSpec(memory_space=pl.ANY)] * len(extra),
        out_specs=[_SEM] * (2 * n) + [_HBM] * (2 * n) + [pl.BlockSpec(memory_space=pltpu.VMEM)],
        input_output_aliases={i: 2 * n + i for i in range(2 * n)},
        compiler_params=pltpu.CompilerParams(has_side_effects=_EFFECT),
    )(*[pltpu.with_memory_space_constraint(a, pltpu.HBM) for a in list(srcs) + lands], *extra)
    return dict(sems=outs[:2 * n], srcs=outs[2 * n:3 * n], lands=outs[3 * n:4 * n], rows=rows), outs[-1]


def early_exchange_wait(handle, after, name):
    n = len(handle["srcs"])
    rows = handle["rows"]

    def body(*refs):
        src_refs, land_refs = refs[:n], refs[n:2 * n]
        send_sems, recv_sems = refs[2 * n:3 * n], refs[3 * n:4 * n]
        x, y, c = _mesh_pos()
        for a in range(n):
            seven = pl.ds(0, 7 * rows[a])
            all_seven = pltpu.make_async_remote_copy(
                src_ref=land_refs[a].at[seven, :], dst_ref=land_refs[a].at[seven, :],
                send_sem=send_sems[a], recv_sem=recv_sems[a],
                device_id=(x, y, c), device_id_type=pl.DeviceIdType.MESH)
            all_seven.wait_send()
            all_seven.wait_recv()

    hbm = lambda a: pltpu.HBM(a.shape, a.dtype)
    ins = list(handle["srcs"]) + list(handle["lands"])
    outs = pl.pallas_call(
        body, name=name,
        out_shape=[hbm(a) for a in ins],
        in_specs=[_HBM] * (2 * n) + [_SEM] * (2 * n) + [pl.BlockSpec(memory_space=pl.ANY)],
        out_specs=[_HBM] * (2 * n),
        input_output_aliases={i: i for i in range(2 * n)},
        compiler_params=pltpu.CompilerParams(has_side_effects=_EFFECT),
    )(*ins, *handle["sems"], after)
    return outs[:n], outs[n:]


def slot_sum8(src, land, tr, name):
    rows, cols = land.shape[0] // N_DEV, land.shape[1]
    x, y, c = _mesh_pos()
    me = (4 * x + 2 * y + c).astype(jnp.int32).reshape(1)

    def body(me_ref, src_ref, land_ref, o_ref):
        acc = None
        for d in range(N_DEV):
            term = jnp.where(d == me_ref[0], src_ref[0], land_ref[d]).astype(F32)
            acc = term if acc is None else acc + term
        o_ref[...] = acc

    return pl.pallas_call(
        body, name=name,
        grid_spec=pltpu.PrefetchScalarGridSpec(
            num_scalar_prefetch=1, grid=(rows // tr,),
            in_specs=[pl.BlockSpec((1, tr, cols), lambda i, w: (w[0], i, 0)),
                      pl.BlockSpec((N_DEV, tr, cols), lambda i, w: (0, i, 0))],
            out_specs=pl.BlockSpec((tr, cols), lambda i, w: (i, 0))),
        out_shape=jax.ShapeDtypeStruct((rows, cols), F32),
        compiler_params=_params(("arbitrary",)),
    )(me, src.reshape(N_DEV, rows, cols), land.reshape(N_DEV, rows, cols))


def mm_tn_multi(a_t, bs, tt, name, out_dtype=F32):
    K, T = a_t.shape
    widths = [b.shape[1] for b in bs]
    steps = T // tt

    def body(a_ref, *rest):
        b_refs, o_ref, acc = rest[:-2], rest[-2], rest[-1]
        @pl.when(pl.program_id(0) == 0)
        def _():
            acc[...] = jnp.zeros(acc.shape, F32)

        av = a_ref[...]
        col = 0
        for b_ref, w in zip(b_refs, widths):
            acc[:, col:col + w] += jnp.dot(av, b_ref[...], preferred_element_type=F32)
            col += w

        @pl.when(pl.program_id(0) == steps - 1)
        def _():
            o_ref[...] = acc[...].astype(out_dtype)

    return pl.pallas_call(
        body, name=name, grid=(steps,),
        in_specs=[pl.BlockSpec((K, tt), lambda t: (0, t))] + [pl.BlockSpec((tt, w), lambda t: (t, 0)) for w in widths],
        out_specs=pl.BlockSpec((K, sum(widths)), lambda t: (0, 0)),
        out_shape=jax.ShapeDtypeStruct((K, sum(widths)), out_dtype),
        scratch_shapes=[pltpu.VMEM((K, sum(widths)), F32)],
        compiler_params=_params(("arbitrary",)),
    )(a_t, *bs)


def rms_fwd(x, g, tm, name, with_transpose=False):
    M, K = x.shape

    def body(x_ref, g_ref, o_ref, *t_ref):
        xv = x_ref[...]
        r = lax.rsqrt(jnp.mean(xv * xv, axis=-1, keepdims=True) + RMS_EPS)
        h = ((xv * r) * g_ref[...]).astype(BF16)
        o_ref[...] = h
        if with_transpose:
            t_ref[0][...] = h.T

    out_specs = [pl.BlockSpec((tm, K), lambda i: (i, 0))]
    out_shape = [jax.ShapeDtypeStruct((M, K), BF16)]
    if with_transpose:
        out_specs.append(pl.BlockSpec((K, tm), lambda i: (0, i)))
        out_shape.append(jax.ShapeDtypeStruct((K, M), BF16))
    outs = pl.pallas_call(
        body, name=name, grid=(M // tm,),
        in_specs=[pl.BlockSpec((tm, K), lambda i: (i, 0)), pl.BlockSpec((1, K), lambda i: (0, 0))],
        out_specs=out_specs, out_shape=out_shape,
        compiler_params=_params(("arbitrary",)),
    )(x, g)
    return outs if with_transpose else outs[0]


def rms_bwd(x, g, dh, dres, tm, name):
    M, K = x.shape
    has_res = dres is not None

    def body(*refs):
        if has_res:
            x_ref, g_ref, dh_ref, dres_ref, dx_ref, dg_ref = refs
        else:
            x_ref, g_ref, dh_ref, dx_ref, dg_ref = refs
        xv = x_ref[...]
        r = lax.rsqrt(jnp.mean(xv * xv, axis=-1, keepdims=True) + RMS_EPS)
        xn = xv * r
        dhv = dh_ref[...]
        dxn = dhv * g_ref[...]
        dx = r * (dxn - xn * jnp.mean(dxn * xn, axis=-1, keepdims=True))
        if has_res:
            dx = dx + dres_ref[...]
        dx_ref[...] = dx
        part = jnp.sum(dhv * xn, axis=0, keepdims=True)
        row = lax.broadcasted_iota(jnp.int32, (8, K), 0)
        upd = jnp.where(row == 0, part, 0.0)

        @pl.when(pl.program_id(0) == 0)
        def _():
            dg_ref[...] = upd

        @pl.when(pl.program_id(0) != 0)
        def _():
            dg_ref[...] += upd

    row_spec = pl.BlockSpec((tm, K), lambda i: (i, 0))
    ins = [x, g, dh] + ([dres] if has_res else [])
    in_specs = [row_spec, pl.BlockSpec((1, K), lambda i: (0, 0)), row_spec] + ([row_spec] if has_res else [])
    return pl.pallas_call(
        body, name=name, grid=(M // tm,),
        in_specs=in_specs,
        out_specs=[row_spec, pl.BlockSpec((8, K), lambda i: (0, 0))],
        out_shape=[jax.ShapeDtypeStruct((M, K), F32), jax.ShapeDtypeStruct((8, K), F32)],
        compiler_params=_params(("arbitrary",)),
    )(*ins)


def mm_nn(a, b, tm, tn, name):
    M, K = a.shape
    N = b.shape[1]

    def body(a_ref, b_ref, o_ref):
        o_ref[...] = jnp.dot(a_ref[...], b_ref[...], preferred_element_type=F32)

    return pl.pallas_call(
        body, name=name, grid=(N // tn, M // tm),
        in_specs=[pl.BlockSpec((tm, K), lambda j, i: (i, 0)), pl.BlockSpec((K, tn), lambda j, i: (0, j))],
        out_specs=pl.BlockSpec((tm, tn), lambda j, i: (i, j)),
        out_shape=jax.ShapeDtypeStruct((M, N), F32),
        compiler_params=_params(("arbitrary", "arbitrary")),
    )(a, b)


def mm_nt(a, b, tm, tk, name):
    M, K = a.shape
    N = b.shape[0]

    def body(a_ref, b_ref, o_ref):
        part = lax.dot_general(a_ref[...], b_ref[...], (((1,), (1,)), ((), ())), preferred_element_type=F32)

        @pl.when(pl.program_id(1) == 0)
        def _():
            o_ref[...] = part

        @pl.when(pl.program_id(1) != 0)
        def _():
            o_ref[...] += part

    return pl.pallas_call(
        body, name=name, grid=(M // tm, K // tk),
        in_specs=[pl.BlockSpec((tm, tk), lambda i, k: (i, k)), pl.BlockSpec((N, tk), lambda i, k: (0, k))],
        out_specs=pl.BlockSpec((tm, N), lambda i, k: (i, 0)),
        out_shape=jax.ShapeDtypeStruct((M, N), F32),
        compiler_params=_params(("arbitrary", "arbitrary")),
    )(a, b)


def mm_tn(a, b, tt, tn, name):
    T, K = a.shape
    N = b.shape[1]

    def body(a_ref, b_ref, o_ref):
        part = lax.dot_general(a_ref[...], b_ref[...], (((0,), (0,)), ((), ())), preferred_element_type=F32)

        @pl.when(pl.program_id(1) == 0)
        def _():
            o_ref[...] = part

        @pl.when(pl.program_id(1) != 0)
        def _():
            o_ref[...] += part

    return pl.pallas_call(
        body, name=name, grid=(N // tn, T // tt),
        in_specs=[pl.BlockSpec((tt, K), lambda j, t: (t, 0)), pl.BlockSpec((tt, tn), lambda j, t: (t, j))],
        out_specs=pl.BlockSpec((K, tn), lambda j, t: (0, j)),
        out_shape=jax.ShapeDtypeStruct((K, N), F32),
        compiler_params=_params(("arbitrary", "arbitrary")),
    )(a, b)


def _old_mm_tn_multi(a, bs, tt, name):
    T, K = a.shape
    widths = [b.shape[1] for b in bs]

    def body(a_ref, *rest):
        b_refs, o_ref = rest[:-1], rest[-1]
        av = a_ref[...]
        parts = [lax.dot_general(av, b_ref[...], (((0,), (0,)), ((), ())), preferred_element_type=F32)
                 for b_ref in b_refs]

        @pl.when(pl.program_id(0) == 0)
        def _():
            col = 0
            for part, w in zip(parts, widths):
                o_ref[:, col:col + w] = part
                col += w

        @pl.when(pl.program_id(0) != 0)
        def _():
            col = 0
            for part, w in zip(parts, widths):
                o_ref[:, col:col + w] += part
                col += w

    return pl.pallas_call(
        body, name=name, grid=(T // tt,),
        in_specs=[pl.BlockSpec((tt, K), lambda t: (t, 0))] + [pl.BlockSpec((tt, w), lambda t: (t, 0)) for w in widths],
        out_specs=pl.BlockSpec((K, sum(widths)), lambda t: (0, 0)),
        out_shape=jax.ShapeDtypeStruct((K, sum(widths)), F32),
        compiler_params=_params(("arbitrary",)),
    )(a, *bs)


def mm_nt_multi(pieces, w, tm, name):
    M = pieces[0][0].shape[0]
    N, K = w.shape

    def body(*refs):
        p_refs, w_ref, o_ref = refs[:-2], refs[-2], refs[-1]
        acc = None
        for p_ref, (arr, col) in zip(p_refs, pieces):
            part = lax.dot_general(p_ref[...], w_ref[:, col:col + arr.shape[1]], (((1,), (1,)), ((), ())),
                                   preferred_element_type=F32)
            acc = part if acc is None else acc + part
        o_ref[...] = acc

    return pl.pallas_call(
        body, name=name, grid=(M // tm,),
        in_specs=[pl.BlockSpec((tm, arr.shape[1]), lambda i: (i, 0)) for arr, _ in pieces]
        + [pl.BlockSpec((N, K), lambda i: (0, 0))],
        out_specs=pl.BlockSpec((tm, N), lambda i: (i, 0)),
        out_shape=jax.ShapeDtypeStruct((M, N), F32),
        compiler_params=_params(("arbitrary",)),
    )(*[arr for arr, _ in pieces], w)


def in_proj_bwd_rms(pieces, w, x, g, dres, tm, token):
    M, N = x.shape

    def body(*refs):
        n = len(pieces)
        p_refs, w_ref, x_ref, g_ref, dres_ref, _, dx_ref, dg_ref = refs[:n], *refs[n:]
        dh = None
        for p_ref, (arr, col) in zip(p_refs, pieces):
            part = lax.dot_general(p_ref[...], w_ref[:, col:col + arr.shape[1]], (((1,), (1,)), ((), ())),
                                   preferred_element_type=F32)
            dh = part if dh is None else dh + part
        xv = x_ref[...]
        r = lax.rsqrt(jnp.mean(xv * xv, axis=-1, keepdims=True) + RMS_EPS)
        xn = xv * r
        dxn = dh * g_ref[...]
        dx_ref[...] = r * (dxn - xn * jnp.mean(dxn * xn, axis=-1, keepdims=True)) + dres_ref[...]
        row = lax.broadcasted_iota(jnp.int32, (8, N), 0)
        upd = jnp.where(row == 0, jnp.sum(dh * xn, axis=0, keepdims=True), 0.0)

        @pl.when(pl.program_id(0) == 0)
        def _():
            dg_ref[...] = upd

        @pl.when(pl.program_id(0) != 0)
        def _():
            dg_ref[...] += upd

    row_spec = pl.BlockSpec((tm, N), lambda i: (i, 0))
    return pl.pallas_call(
        body, name="in_proj_bwd", grid=(M // tm,),
        in_specs=[pl.BlockSpec((tm, arr.shape[1]), lambda i: (i, 0)) for arr, _ in pieces]
        + [pl.BlockSpec(w.shape, lambda i: (0, 0)), row_spec, pl.BlockSpec((1, N), lambda i: (0, 0)), row_spec,
           pl.BlockSpec(token.shape, lambda i: (0, 0))],
        out_specs=[row_spec, pl.BlockSpec((8, N), lambda i: (0, 0))],
        out_shape=[jax.ShapeDtypeStruct((M, N), F32), jax.ShapeDtypeStruct((8, N), F32)],
        compiler_params=_params(("arbitrary",)),
    )(*[arr for arr, _ in pieces], w, x, g, dres, token)


def _log_sigmoid(z):
    return jnp.minimum(z, 0.0) - jnp.log(1.0 + jnp.exp(-jnp.abs(z)))


def _tri(n, lower):
    r = lax.broadcasted_iota(jnp.int32, (n, n), 0)
    c = lax.broadcasted_iota(jnp.int32, (n, n), 1)
    return jnp.where((r >= c) if lower else (r <= c), 1.0, 0.0).astype(F32)


def fox_gate(proj3, b_pad):
    B, S, _ = proj3.shape
    nblk = S // TK

    def body(f_ref, b_ref, o_ref):
        tri = _tri(TK, True)
        carry = jnp.zeros((1, LANES), F32)
        for n in range(nblk):
            z = f_ref[0, n * TK:(n + 1) * TK, :] + b_ref[...]
            logf = _log_sigmoid(z)
            cs = jnp.dot(tri, logf, preferred_element_type=F32, precision=lax.Precision.HIGHEST) + carry
            carry = cs[TK - 1:TK, :]
            o_ref[0, n * TK:(n + 1) * TK, :] = -cs

    return pl.pallas_call(
        body, name="fox_gate", grid=(B,),
        in_specs=[pl.BlockSpec((1, S, LANES), lambda b: (b, 0, P_FLOG // LANES)),
                  pl.BlockSpec((1, LANES), lambda b: (0, 0))],
        out_specs=pl.BlockSpec((1, S, LANES), lambda b: (b, 0, 0)),
        out_shape=jax.ShapeDtypeStruct((B, S, LANES), F32),
        compiler_params=_params(("arbitrary",)),
    )(proj3, b_pad)


def fox_gate_bwd(drow, dneg, proj3, b_pad):
    B, S, _ = proj3.shape
    nblk = S // TK

    def body(d_ref, r_ref, f_ref, b_ref, o_ref, db_ref):
        tri = _tri(TK, False)
        lane = lax.broadcasted_iota(jnp.int32, (TK, LANES), 1)
        carry = jnp.zeros((1, LANES), F32)
        dbsum = jnp.zeros((1, LANES), F32)
        for n in reversed(range(nblk)):
            dk_side = None
            for hp in range(FOX_HEADS // 2):
                two = jnp.where(lane < 2, r_ref[0, n * TK:(n + 1) * TK, hp * LANES:(hp + 1) * LANES], 0.0)
                two = pltpu.roll(two, 2 * hp, 1) if hp else two
                dk_side = two if dk_side is None else dk_side + two
            dc = jnp.where(lane < FOX_HEADS, d_ref[0, :, n * TK:(n + 1) * TK].T - dk_side, 0.0)
            rs = jnp.dot(tri, dc, preferred_element_type=F32, precision=lax.Precision.HIGHEST) + carry
            carry = rs[0:1, :]
            z = f_ref[0, n * TK:(n + 1) * TK, :] + b_ref[...]
            dz = rs * (1.0 / (1.0 + jnp.exp(z)))
            o_ref[0, n * TK:(n + 1) * TK, :] = dz.astype(BF16)
            dbsum = dbsum + jnp.sum(dz, axis=0, keepdims=True)
        row = lax.broadcasted_iota(jnp.int32, (8, LANES), 0)
        upd = jnp.where(row == 0, dbsum, 0.0)

        @pl.when(pl.program_id(0) == 0)
        def _():
            db_ref[...] = upd

        @pl.when(pl.program_id(0) != 0)
        def _():
            db_ref[...] += upd

    return pl.pallas_call(
        body, name="fox_gate_bwd", grid=(B,),
        in_specs=[pl.BlockSpec((1, LANES, S), lambda b: (b, 0, 0)),
                  pl.BlockSpec((1, S, FOX_W), lambda b: (b, 0, 0)),
                  pl.BlockSpec((1, S, LANES), lambda b: (b, 0, P_FLOG // LANES)),
                  pl.BlockSpec((1, LANES), lambda b: (0, 0))],
        out_specs=[pl.BlockSpec((1, S, LANES), lambda b: (b, 0, 0)), pl.BlockSpec((8, LANES), lambda b: (0, 0))],
        out_shape=[jax.ShapeDtypeStruct((B, S, LANES), BF16), jax.ShapeDtypeStruct((8, LANES), F32)],
        compiler_params=_params(("arbitrary",)),
    )(drow, dneg, proj3, b_pad)


def _mult_masks(S, kind):
    nd = S // TQ
    a = np.arange(TQ)[:, None]
    b = np.arange(TK)[None, :]
    out = np.zeros((nd, TQ, TK), np.float32)
    for d in range(nd):
        delta = d * TQ + a - b
        if kind == "causal":
            out[d] = delta >= 0
        else:
            m = np.zeros((TQ, TK), np.float32)
            for w, dil in DILATIONS:
                m += (delta >= 0) & (delta % dil == 0) & (delta <= w)
            out[d] = m
    return jnp.asarray(out)


def _rope_tables(S):
    half = ROPE_DIM // 2
    f32 = np.float32
    pos = np.arange(S, dtype=f32)
    inv_freq = f32(1.0) / np.power(f32(ROPE_THETA), np.arange(0, ROPE_DIM, 2, dtype=f32) / f32(ROPE_DIM)).astype(f32)
    ang = (pos[:, None] * inv_freq[None, :]).astype(f32).astype(np.float64)
    cos, sin = np.cos(ang).astype(f32), np.sin(ang).astype(f32)
    one = np.ones((S, HEAD_DIM - ROPE_DIM), f32)
    zero = np.zeros((S, HEAD_DIM - ROPE_DIM), f32)
    zh = np.zeros((S, half), f32)
    c = np.concatenate([cos, cos, one], axis=1)
    s1 = np.concatenate([-sin, zh, zero], axis=1)
    s2 = np.concatenate([zh, sin, zero], axis=1)
    return tuple(jnp.asarray(np.concatenate([t, t], axis=1)) for t in (c, s1, s2))


def _rope(t, c, s1, s2):
    return t * c + pltpu.roll(t, LANES - half_rope(), 1) * s1 + pltpu.roll(t, half_rope(), 1) * s2


def half_rope():
    return ROPE_DIM // 2


def _rope_bwd(d, c, s1, s2):
    return d * c + pltpu.roll(d * s1, half_rope(), 1) + pltpu.roll(d * s2, LANES - half_rope(), 1)


def _scale_parts(scale):
    m, _ = math.frexp(scale)
    return (scale, None) if m == 0.5 else (None, scale)


def attn_fwd(kind, src, S, *, negc=None, mask=None, rope=None, kv=None):
    B = src.shape[0]
    pair = kind != "mem"
    col0 = {"fox": P_FOX, "dil": P_DIL, "mem": P_MQ}[kind]
    n_blocks = FOX_HEADS // 2 if pair else MEM_HEADS
    e_dim = HEAD_DIM if pair else MEM_HEAD_DIM
    q_fold, s_scale = _scale_parts(1.0 / math.sqrt(e_dim))
    Sk = S if pair else MEM_LEN
    nh = 2 if pair else 1
    has_bias = negc is not None
    has_rope = rope is not None
    nq = S // TQ

    def body(*refs):
        refs = list(refs)
        if pair:
            qkv_ref = refs.pop(0)
        else:
            q_ref, k_ref, v_ref = refs.pop(0), refs.pop(0), refs.pop(0)
        negc_ref = refs.pop(0) if has_bias else None
        mask_ref = refs.pop(0) if pair else None
        rope_refs = [refs.pop(0) for _ in range(3)] if has_rope else None
        o_ref, lse_ref, qs, ks, vs = refs
        lane = lax.broadcasted_iota(jnp.int32, (1, LANES), 1)

        def prep_q(n, _):
            r0 = pl.multiple_of(n * TQ, TQ)
            rows = pl.ds(r0, TQ)
            q = qkv_ref[0, rows, 0:LANES] if pair else q_ref[0, rows, :]
            if has_rope:
                q = _rope(q, *[t[rows, :] for t in rope_refs])
            if q_fold is not None:
                q = q * q_fold
            qs[rows, :] = q.astype(BF16)
            return 0

        def prep_kv(n, _):
            r0 = pl.multiple_of(n * TK, TK)
            rows = pl.ds(r0, TK)
            k = qkv_ref[0, rows, LANES:2 * LANES] if pair else k_ref[0, rows, :]
            v = qkv_ref[0, rows, 2 * LANES:3 * LANES] if pair else v_ref[0, rows, :]
            if has_rope:
                k = _rope(k, *[t[rows, :] for t in rope_refs])
            ks[rows, :] = k.astype(BF16)
            vs[rows, :] = v.astype(BF16)
            return 0

        lax.fori_loop(0, nq, prep_q, 0)
        lax.fori_loop(0, Sk // TK, prep_kv, 0)

        def q_loop(i, _):
            r0 = pl.multiple_of(i * TQ, TQ)
            q = qs[pl.ds(r0, TQ), :]
            res = []
            for hh in range(nh):
                hmask = (lane >= HEAD_DIM * hh) & (lane < HEAD_DIM * (hh + 1))
                qh = jnp.where(hmask, q, jnp.zeros_like(q)) if pair else q

                def kv_loop(j, carry, qh=qh, hh=hh):
                    m, l, acc = carry
                    c0 = pl.multiple_of(j * TK, TK)
                    k = ks[pl.ds(c0, TK), :]
                    v = vs[pl.ds(c0, TK), :]
                    s = lax.dot_general(qh, k, (((1,), (1,)), ((), ())), preferred_element_type=F32)
                    if s_scale is not None:
                        s = s * s_scale
                    if has_bias:
                        s = s + negc_ref[0, 0, pl.ds(hh, 1), pl.ds(c0, TK)]
                    if pair:
                        mult = mask_ref[i - j]
                        s = jnp.where(mult > 0.0, s, NEG_INF)
                    m_new = jnp.maximum(m, jnp.max(s, axis=1, keepdims=True))
                    p = jnp.exp(s - m_new)
                    if pair:
                        p = p * mult
                    alpha = jnp.exp(m - m_new)
                    l = alpha * l + jnp.sum(p, axis=1, keepdims=True)
                    acc = acc * alpha + jnp.dot(p.astype(BF16), v, preferred_element_type=F32)
                    return m_new, l, acc

                init = (jnp.full((TQ, 1), NEG_INF, F32), jnp.zeros((TQ, 1), F32), jnp.zeros((TQ, LANES), F32))
                m, l, acc = lax.fori_loop(0, (i + 1) if pair else Sk // TK, kv_loop, init)
                res.append((acc / l, m + jnp.log(l)))
            if pair:
                o = jnp.where(lane < HEAD_DIM, res[0][0], res[1][0])
                lse = jnp.where(lane < HEAD_DIM, res[0][1], res[1][1])
            else:
                o = res[0][0]
                lse = jnp.broadcast_to(res[0][1], (TQ, LANES))
            o_ref[0, pl.ds(r0, TQ), :] = o
            lse_ref[0, pl.ds(r0, TQ), :] = lse
            return 0

        lax.fori_loop(0, nq, q_loop, 0)

    ins, in_specs = [], []
    if pair:
        ins.append(src)
        in_specs.append(pl.BlockSpec((1, S, PAIR_W), lambda b, h: (b, 0, col0 // PAIR_W + h)))
    else:
        ins += [src, kv, kv]
        in_specs += [pl.BlockSpec((1, S, LANES), lambda b, h: (b, 0, col0 // LANES + h)),
                     pl.BlockSpec((1, MEM_LEN, LANES), lambda b, h: (b, 0, h)),
                     pl.BlockSpec((1, MEM_LEN, LANES), lambda b, h: (b, 0, MEM_HEADS + h))]
    if has_bias:
        ins.append(negc)
        in_specs.append(pl.BlockSpec((1, 1, 2, S), lambda b, h: (b, h, 0, 0)))
    if pair:
        ins.append(mask)
        in_specs.append(pl.BlockSpec(mask.shape, lambda b, h: (0, 0, 0)))
    if has_rope:
        ins += list(rope)
        in_specs += [pl.BlockSpec((S, LANES), lambda b, h: (0, 0))] * 3
    W = n_blocks * LANES
    out_spec = pl.BlockSpec((1, S, LANES), lambda b, h: (b, 0, h))
    return pl.pallas_call(
        body, name=kind + "_attn_fwd", grid=(B, n_blocks),
        in_specs=in_specs, out_specs=[out_spec, out_spec],
        out_shape=[jax.ShapeDtypeStruct((B, S, W), F32)] * 2,
        scratch_shapes=[pltpu.VMEM((S, LANES), BF16), pltpu.VMEM((Sk, LANES), BF16), pltpu.VMEM((Sk, LANES), BF16)],
        compiler_params=_params(("arbitrary", "arbitrary")),
    )(*ins)


def attn_bwd(kind, src, do, o, lse, S, *, negc=None, mask=None, rope=None, kv=None):
    B = src.shape[0]
    pair = kind != "mem"
    col0 = {"fox": P_FOX, "dil": P_DIL, "mem": P_MQ}[kind]
    n_blocks = FOX_HEADS // 2 if pair else MEM_HEADS
    e_dim = HEAD_DIM if pair else MEM_HEAD_DIM
    scale = 1.0 / math.sqrt(e_dim)
    q_fold, s_scale = _scale_parts(scale)
    Sk = S if pair else MEM_LEN
    nh = 2 if pair else 1
    has_bias = negc is not None
    has_rope = rope is not None
    nq = S // TQ
    nk = Sk // TK

    def body(*refs):
        refs = list(refs)
        if pair:
            qkv_ref = refs.pop(0)
        else:
            q_ref, k_ref, v_ref = refs.pop(0), refs.pop(0), refs.pop(0)
        do_ref, o_ref, lse_ref = refs.pop(0), refs.pop(0), refs.pop(0)
        negc_ref = refs.pop(0) if has_bias else None
        mask_ref = refs.pop(0) if pair else None
        rope_refs = [refs.pop(0) for _ in range(3)] if has_rope else None
        if pair:
            dqkv_ref = refs.pop(0)
            dnegc_ref = refs.pop(0) if has_bias else None
            drow_ref = refs.pop(0) if has_bias else None
        else:
            dq_ref, dk_ref, dv_ref = refs.pop(0), refs.pop(0), refs.pop(0)
        qs, ks, vs, dos, delta_s, dq_acc = refs[:6]
        drow_acc = refs[6] if has_bias else None
        lane = lax.broadcasted_iota(jnp.int32, (1, LANES), 1)

        def prep_q(n, _):
            r0 = pl.multiple_of(n * TQ, TQ)
            rows = pl.ds(r0, TQ)
            q = qkv_ref[0, rows, 0:LANES] if pair else q_ref[0, rows, :]
            if has_rope:
                q = _rope(q, *[t[rows, :] for t in rope_refs])
            if q_fold is not None:
                q = q * q_fold
            qs[rows, :] = q.astype(BF16)
            dov = do_ref[0, rows, :]
            dob = dov.astype(BF16)
            dos[rows, :] = dob
            prod = dob.astype(F32) * o_ref[0, rows, :]
            if pair:
                d0 = jnp.sum(jnp.where(lane < HEAD_DIM, prod, 0.0), axis=1, keepdims=True)
                d1 = jnp.sum(jnp.where(lane < HEAD_DIM, 0.0, prod), axis=1, keepdims=True)
                delta_s[rows, :] = jnp.where(lane < HEAD_DIM, d0, d1)
            else:
                delta_s[rows, :] = jnp.broadcast_to(jnp.sum(prod, axis=1, keepdims=True), (TQ, LANES))
            dq_acc[rows, :] = jnp.zeros((TQ, LANES), F32)
            if has_bias:
                drow_acc[rows, :] = jnp.zeros((TQ, LANES), F32)
            return 0

        def prep_kv(n, _):
            r0 = pl.multiple_of(n * TK, TK)
            rows = pl.ds(r0, TK)
            k = qkv_ref[0, rows, LANES:2 * LANES] if pair else k_ref[0, rows, :]
            v = qkv_ref[0, rows, 2 * LANES:3 * LANES] if pair else v_ref[0, rows, :]
            if has_rope:
                k = _rope(k, *[t[rows, :] for t in rope_refs])
            ks[rows, :] = k.astype(BF16)
            vs[rows, :] = v.astype(BF16)
            return 0

        lax.fori_loop(0, nq, prep_q, 0)
        lax.fori_loop(0, nk, prep_kv, 0)

        def kv_loop(j, _):
            c0 = pl.multiple_of(j * TK, TK)
            kt = ks[pl.ds(c0, TK), :]
            vt = vs[pl.ds(c0, TK), :]
            res = []
            for hh in range(nh):
                hmask = (lane >= HEAD_DIM * hh) & (lane < HEAD_DIM * (hh + 1))
                kh = jnp.where(hmask, kt, jnp.zeros_like(kt)) if pair else kt
                vh = jnp.where(hmask, vt, jnp.zeros_like(vt)) if pair else vt

                def q_loop(i, carry, kh=kh, vh=vh, hh=hh, hmask=hmask):
                    dk, dv, dneg = carry
                    r0 = pl.multiple_of(i * TQ, TQ)
                    rows = pl.ds(r0, TQ)
                    q = qs[rows, :]
                    dot = dos[rows, :]
                    lse_i = lse_ref[0, rows, hh * HEAD_DIM:hh * HEAD_DIM + 1]
                    delta_i = delta_s[rows, hh * HEAD_DIM:hh * HEAD_DIM + 1]
                    s = lax.dot_general(q, kh, (((1,), (1,)), ((), ())), preferred_element_type=F32)
                    if s_scale is not None:
                        s = s * s_scale
                    if has_bias:
                        s = s + negc_ref[0, 0, pl.ds(hh, 1), pl.ds(c0, TK)]
                    if pair:
                        mult = mask_ref[i - j]
                        s = jnp.where(mult > 0.0, s, NEG_INF)
                    p = jnp.exp(s - lse_i)
                    if pair:
                        p = p * mult
                    dv = dv + lax.dot_general(p.astype(BF16), dot, (((0,), (0,)), ((), ())),
                                              preferred_element_type=F32)
                    dp = lax.dot_general(dot, vh, (((1,), (1,)), ((), ())), preferred_element_type=F32)
                    ds = p * (dp - delta_i)
                    if has_bias:
                        dneg = dneg + jnp.sum(ds, axis=0, keepdims=True)
                        drow_acc[rows, :] += jnp.where(hmask, jnp.sum(ds, axis=1, keepdims=True), 0.0)
                    if s_scale is not None:
                        ds = ds * s_scale
                    dsb = ds.astype(BF16)
                    dk = dk + lax.dot_general(dsb, q, (((0,), (0,)), ((), ())), preferred_element_type=F32)
                    dq = jnp.dot(dsb, kh, preferred_element_type=F32)
                    dq_acc[rows, :] += dq
                    return dk, dv, dneg

                init = (jnp.zeros((TK, LANES), F32), jnp.zeros((TK, LANES), F32), jnp.zeros((1, TK), F32))
                dk, dv, dneg = lax.fori_loop(j if pair else 0, nq, q_loop, init)
                if has_bias:
                    dnegc_ref[0, 0, pl.ds(hh, 1), pl.ds(c0, TK)] = dneg
                res.append((dk, dv))
            if pair:
                dk = jnp.where(lane < HEAD_DIM, res[0][0], res[1][0])
                dv = jnp.where(lane < HEAD_DIM, res[0][1], res[1][1])
                if has_rope:
                    dk = _rope_bwd(dk, *[t[pl.ds(c0, TK), :] for t in rope_refs])
                dqkv_ref[0, pl.ds(c0, TK), LANES:2 * LANES] = dk.astype(BF16)
                dqkv_ref[0, pl.ds(c0, TK), 2 * LANES:3 * LANES] = dv.astype(BF16)
            else:
                dk_ref[0, pl.ds(c0, TK), :] = res[0][0].astype(BF16)
                dv_ref[0, pl.ds(c0, TK), :] = res[0][1].astype(BF16)
            return 0

        lax.fori_loop(0, nk, kv_loop, 0)

        def fin_q(n, _):
            r0 = pl.multiple_of(n * TQ, TQ)
            rows = pl.ds(r0, TQ)
            dq = dq_acc[rows, :]
            if q_fold is not None:
                dq = dq * q_fold
            if has_rope:
                dq = _rope_bwd(dq, *[t[rows, :] for t in rope_refs])
            if pair:
                dqkv_ref[0, rows, 0:LANES] = dq.astype(BF16)
            else:
                dq_ref[0, rows, :] = dq.astype(BF16)
            if has_bias:
                drow_ref[0, rows, :] = drow_acc[rows, :]
            return 0

        lax.fori_loop(0, nq, fin_q, 0)

    ins, in_specs = [], []
    if pair:
        ins.append(src)
        in_specs.append(pl.BlockSpec((1, S, PAIR_W), lambda b, h: (b, 0, col0 // PAIR_W + h)))
    else:
        ins += [src, kv, kv]
        in_specs += [pl.BlockSpec((1, S, LANES), lambda b, h: (b, 0, col0 // LANES + h)),
                     pl.BlockSpec((1, MEM_LEN, LANES), lambda b, h: (b, 0, h)),
                     pl.BlockSpec((1, MEM_LEN, LANES), lambda b, h: (b, 0, MEM_HEADS + h))]
    row_spec = pl.BlockSpec((1, S, LANES), lambda b, h: (b, 0, h))
    ins += [do, o, lse]
    in_specs += [row_spec] * 3
    if has_bias:
        ins.append(negc)
        in_specs.append(pl.BlockSpec((1, 1, 2, S), lambda b, h: (b, h, 0, 0)))
    if pair:
        ins.append(mask)
        in_specs.append(pl.BlockSpec(mask.shape, lambda b, h: (0, 0, 0)))
    if has_rope:
        ins += list(rope)
        in_specs += [pl.BlockSpec((S, LANES), lambda b, h: (0, 0))] * 3
    W = n_blocks * LANES
    if pair:
        out_specs = [pl.BlockSpec((1, S, PAIR_W), lambda b, h: (b, 0, h))]
        out_shape = [jax.ShapeDtypeStruct((B, S, 3 * W), BF16)]
        if has_bias:
            out_specs.append(pl.BlockSpec((1, 1, 2, S), lambda b, h: (b, h, 0, 0)))
            out_shape.append(jax.ShapeDtypeStruct((B, LANES // 2, 2, S), F32))
            out_specs.append(row_spec)
            out_shape.append(jax.ShapeDtypeStruct((B, S, W), F32))
    else:
        kv_spec = pl.BlockSpec((1, MEM_LEN, LANES), lambda b, h: (b, 0, h))
        out_specs = [row_spec, kv_spec, kv_spec]
        out_shape = [jax.ShapeDtypeStruct((B, S, W), BF16)] + [jax.ShapeDtypeStruct((B, MEM_LEN, W), BF16)] * 2
    return pl.pallas_call(
        body, name=kind + "_attn_bwd", grid=(B, n_blocks),
        in_specs=in_specs, out_specs=out_specs, out_shape=out_shape,
        scratch_shapes=[pltpu.VMEM((S, LANES), BF16), pltpu.VMEM((Sk, LANES), BF16), pltpu.VMEM((Sk, LANES), BF16),
                        pltpu.VMEM((S, LANES), BF16), pltpu.VMEM((S, LANES), F32), pltpu.VMEM((S, LANES), F32)]
        + ([pltpu.VMEM((S, LANES), F32)] if has_bias else []),
        compiler_params=_params(("arbitrary", "arbitrary")),
    )(*ins)


def _log_masks(S, kind):
    nd = 1 if kind == "causal" else S // TQ
    a = np.arange(TQ)[:, None]
    b = np.arange(TK)[None, :]
    out = np.zeros((nd, TQ, TK), np.float32)
    for d in range(nd):
        delta = d * TQ + a - b
        if kind == "causal":
            m = (delta >= 0).astype(np.float64)
        else:
            m = sum(((delta >= 0) & (delta % dil == 0) & (delta <= w)).astype(np.float64) for w, dil in DILATIONS)
        out[d] = np.where(m > 0, np.log(np.maximum(m, 1.0)), NEG_INF)
    return jnp.asarray(out)


def _attn_setup(kind):
    pair = kind != "mem"
    e_dim = HEAD_DIM if pair else MEM_HEAD_DIM
    q_fold, s_scale = _scale_parts(1.0 / math.sqrt(e_dim))
    return dict(pair=pair, col0={"fox": P_FOX, "dil": P_DIL, "mem": P_MQ}[kind],
                n_blocks=FOX_HEADS // 2 if pair else MEM_HEADS, q_fold=q_fold, s_scale=s_scale,
                nh=2 if pair else 1)


def _attn_inputs(kind, src, S, negc, mask, rope, kv, extra):
    cfg = _attn_setup(kind)
    col0 = cfg["col0"]
    ins, in_specs = [], []
    if cfg["pair"]:
        ins.append(src)
        in_specs.append(pl.BlockSpec((1, S, PAIR_W), lambda b, h: (b, 0, col0 // PAIR_W + h)))
    else:
        ins += [src, kv, kv]
        in_specs += [pl.BlockSpec((1, S, LANES), lambda b, h: (b, 0, col0 // LANES + h)),
                     pl.BlockSpec((1, MEM_LEN, LANES), lambda b, h: (b, 0, h)),
                     pl.BlockSpec((1, MEM_LEN, LANES), lambda b, h: (b, 0, MEM_HEADS + h))]
    ins += list(extra)
    in_specs += [pl.BlockSpec((1, S, LANES), lambda b, h: (b, 0, h))] * len(extra)
    if negc is not None:
        ins.append(negc)
        in_specs.append(pl.BlockSpec((1, 1, 2, S), lambda b, h: (b, h, 0, 0)))
    if mask is not None:
        ins.append(mask)
        in_specs.append(pl.BlockSpec(mask.shape, lambda b, h: (0, 0, 0)))
    if rope is not None:
        ins += list(rope)
        in_specs += [pl.BlockSpec((S, LANES), lambda b, h: (0, 0))] * 3
    return ins, in_specs


def _prep_rows(cfg, rope_refs, lane, load_q, load_kv, qs2, ks, vs, S, Sk):
    nh = cfg["nh"]
    R = nh * TQ

    def prep_q(n, _):
        rows = pl.ds(pl.multiple_of(n * TQ, TQ), TQ)
        q = load_q(rows)
        if rope_refs is not None:
            q = _rope(q, *[t[rows, :] for t in rope_refs])
        if cfg["q_fold"] is not None:
            q = q * cfg["q_fold"]
        _store_stacked(cfg, lane, qs2, n, q.astype(BF16))
        return 0

    def prep_kv(n, _):
        rows = pl.ds(pl.multiple_of(n * TK, TK), TK)
        k, v = load_kv(rows)
        if rope_refs is not None:
            k = _rope(k, *[t[rows, :] for t in rope_refs])
        ks[rows, :] = k.astype(BF16)
        vs[rows, :] = v.astype(BF16)
        return 0

    lax.fori_loop(0, S // TQ, prep_q, 0)
    lax.fori_loop(0, Sk // TK, prep_kv, 0)


def _store_stacked(cfg, lane, dst, n, val):
    nh = cfg["nh"]
    R = nh * TQ
    if nh == 1:
        dst[pl.ds(pl.multiple_of(n * R, R), TQ), :] = val
        return
    for hh in range(nh):
        hmask = (lane >= HEAD_DIM * hh) & (lane < HEAD_DIM * (hh + 1))
        dst[pl.ds(pl.multiple_of(n * R + hh * TQ, TQ), TQ), :] = jnp.where(hmask, val, jnp.zeros_like(val))


def _cat(parts, axis):
    return parts[0] if len(parts) == 1 else jnp.concatenate(parts, axis=axis)


def attn_fwd2(kind, src, S, *, negc=None, mask=None, rope=None, kv=None):
    B = src.shape[0]
    cfg = _attn_setup(kind)
    pair, nh, s_scale = cfg["pair"], cfg["nh"], cfg["s_scale"]
    Sk = S if pair else MEM_LEN
    has_bias, has_rope = negc is not None, rope is not None
    R = nh * TQ

    def body(*refs):
        refs = list(refs)
        if pair:
            qkv_ref = refs.pop(0)
        else:
            q_ref, k_ref, v_ref = refs.pop(0), refs.pop(0), refs.pop(0)
        negc_ref = refs.pop(0) if has_bias else None
        mask_ref = refs.pop(0) if mask is not None else None
        rope_refs = [refs.pop(0) for _ in range(3)] if has_rope else None
        o_ref, lse_ref, qs2, ks, vs = refs
        lane = lax.broadcasted_iota(jnp.int32, (1, LANES), 1)

        if pair:
            load_q = lambda rows: qkv_ref[0, rows, 0:LANES]
            load_kv = lambda rows: (qkv_ref[0, rows, LANES:2 * LANES], qkv_ref[0, rows, 2 * LANES:3 * LANES])
        else:
            load_q = lambda rows: q_ref[0, rows, :]
            load_kv = lambda rows: (k_ref[0, rows, :], v_ref[0, rows, :])
        _prep_rows(cfg, rope_refs, lane, load_q, load_kv, qs2, ks, vs, S, Sk)

        def q_loop(i, _):
            q2 = qs2[pl.ds(pl.multiple_of(i * R, R), R), :]

            def step(j, carry, midx):
                ms, ls, acc = carry
                c0 = pl.multiple_of(j * TK, TK)
                k = ks[pl.ds(c0, TK), :]
                v = vs[pl.ds(c0, TK), :]
                s2 = lax.dot_general(q2, k, (((1,), (1,)), ((), ())), preferred_element_type=F32)
                if s_scale is not None:
                    s2 = s2 * s_scale
                new_m, new_l, ps, alphas = [], [], [], []
                for hh in range(nh):
                    s = s2[hh * TQ:(hh + 1) * TQ]
                    if has_bias:
                        s = s + negc_ref[0, 0, pl.ds(hh, 1), pl.ds(c0, TK)]
                    if midx is not None:
                        s = s + mask_ref[midx]
                    m_new = jnp.maximum(ms[hh], jnp.max(s, axis=1, keepdims=True))
                    p = jnp.exp(s - m_new)
                    alpha = jnp.exp(ms[hh] - m_new)
                    new_l.append(alpha * ls[hh] + jnp.sum(p, axis=1, keepdims=True))
                    new_m.append(m_new)
                    ps.append(p.astype(BF16))
                    alphas.append(alpha)
                acc = acc * _cat(alphas, 0) + jnp.dot(_cat(ps, 0), v, preferred_element_type=F32)
                return tuple(new_m), tuple(new_l), acc

            init = (tuple(jnp.full((TQ, 1), NEG_INF, F32) for _ in range(nh)),
                    tuple(jnp.zeros((TQ, 1), F32) for _ in range(nh)), jnp.zeros((R, LANES), F32))
            if kind == "fox":
                carry = lax.fori_loop(0, i, lambda j, c: step(j, c, None), init)
                carry = step(i, carry, 0)
            elif kind == "dil":
                carry = lax.fori_loop(0, i + 1, lambda j, c: step(j, c, i - j), init)
            else:
                carry = lax.fori_loop(0, Sk // TK, lambda j, c: step(j, c, None), init)
            ms, ls, acc = carry
            outs = [acc[hh * TQ:(hh + 1) * TQ] / ls[hh] for hh in range(nh)]
            lses = [ms[hh] + jnp.log(ls[hh]) for hh in range(nh)]
            rows = pl.ds(pl.multiple_of(i * TQ, TQ), TQ)
            if pair:
                o_ref[0, rows, :] = jnp.where(lane < HEAD_DIM, outs[0], outs[1])
                lse_ref[0, rows, :] = jnp.where(lane < HEAD_DIM, lses[0], lses[1])
            else:
                o_ref[0, rows, :] = outs[0]
                lse_ref[0, rows, :] = jnp.broadcast_to(lses[0], (TQ, LANES))
            return 0

        lax.fori_loop(0, S // TQ, q_loop, 0)

    ins, in_specs = _attn_inputs(kind, src, S, negc, mask, rope, kv, ())
    W = cfg["n_blocks"] * LANES
    out_spec = pl.BlockSpec((1, S, LANES), lambda b, h: (b, 0, h))
    return pl.pallas_call(
        body, name=kind + "_attn_fwd", grid=(B, cfg["n_blocks"]),
        in_specs=in_specs, out_specs=[out_spec, out_spec],
        out_shape=[jax.ShapeDtypeStruct((B, S, W), F32)] * 2,
        scratch_shapes=[pltpu.VMEM((nh * S, LANES), BF16), pltpu.VMEM((Sk, LANES), BF16),
                        pltpu.VMEM((Sk, LANES), BF16)],
        compiler_params=_params(("arbitrary", "arbitrary")),
    )(*ins)


def attn_bwd2(kind, src, do, o, lse, S, *, negc=None, mask=None, rope=None, kv=None):
    B = src.shape[0]
    cfg = _attn_setup(kind)
    pair, nh, s_scale, q_fold = cfg["pair"], cfg["nh"], cfg["s_scale"], cfg["q_fold"]
    Sk = S if pair else MEM_LEN
    has_bias, has_rope = negc is not None, rope is not None
    R = nh * TQ
    nq, nk = S // TQ, Sk // TK

    def body(*refs):
        refs = list(refs)
        if pair:
            qkv_ref = refs.pop(0)
        else:
            q_ref, k_ref, v_ref = refs.pop(0), refs.pop(0), refs.pop(0)
        do_ref, o_ref, lse_ref = refs.pop(0), refs.pop(0), refs.pop(0)
        negc_ref = refs.pop(0) if has_bias else None
        mask_ref = refs.pop(0) if mask is not None else None
        rope_refs = [refs.pop(0) for _ in range(3)] if has_rope else None
        if pair:
            dqkv_ref = refs.pop(0)
            dnegc_ref = refs.pop(0) if has_bias else None
            drow_ref = refs.pop(0) if has_bias else None
        else:
            dq_ref, dk_ref, dv_ref = refs.pop(0), refs.pop(0), refs.pop(0)
        qs2, ks, vs, dos2, lse_s, delta_s, dk_acc, dv_acc = refs[:8]
        dneg_acc = refs[8] if has_bias else None
        lane = lax.broadcasted_iota(jnp.int32, (1, LANES), 1)

        if pair:
            load_q = lambda rows: qkv_ref[0, rows, 0:LANES]
            load_kv = lambda rows: (qkv_ref[0, rows, LANES:2 * LANES], qkv_ref[0, rows, 2 * LANES:3 * LANES])
        else:
            load_q = lambda rows: q_ref[0, rows, :]
            load_kv = lambda rows: (k_ref[0, rows, :], v_ref[0, rows, :])
        _prep_rows(cfg, rope_refs, lane, load_q, load_kv, qs2, ks, vs, S, Sk)

        def prep_do(n, _):
            rows = pl.ds(pl.multiple_of(n * TQ, TQ), TQ)
            dob = do_ref[0, rows, :].astype(BF16)
            _store_stacked(cfg, lane, dos2, n, dob)
            prod = dob.astype(F32) * o_ref[0, rows, :]
            lse_blk = lse_ref[0, rows, :]
            for hh in range(nh):
                dst = pl.ds(pl.multiple_of(n * R + hh * TQ, TQ), TQ)
                if pair:
                    hmask = (lane >= HEAD_DIM * hh) & (lane < HEAD_DIM * (hh + 1))
                    d = jnp.sum(jnp.where(hmask, prod, 0.0), axis=1, keepdims=True)
                    lse_s[dst, :] = jnp.broadcast_to(lse_blk[:, hh * HEAD_DIM:hh * HEAD_DIM + 1], (TQ, LANES))
                else:
                    d = jnp.sum(prod, axis=1, keepdims=True)
                    lse_s[dst, :] = lse_blk
                delta_s[dst, :] = jnp.broadcast_to(d, (TQ, LANES))
            return 0

        def zero_kv(n, _):
            rows = pl.ds(pl.multiple_of(n * TK, TK), TK)
            dk_acc[rows, :] = jnp.zeros((TK, LANES), F32)
            dv_acc[rows, :] = jnp.zeros((TK, LANES), F32)
            return 0

        lax.fori_loop(0, nq, prep_do, 0)
        lax.fori_loop(0, nk, zero_kv, 0)
        if has_bias:
            dneg_acc[...] = jnp.zeros(dneg_acc.shape, F32)

        def q_loop(i, _):
            rows2 = pl.ds(pl.multiple_of(i * R, R), R)
            q2 = qs2[rows2, :]
            do2 = dos2[rows2, :]
            lse2 = lse_s[rows2, :]
            delta2 = delta_s[rows2, :]
            wide = lambda t: jnp.concatenate([t] * (TK // LANES), axis=1)

            def step(j, carry, midx):
                dq2, drow = carry
                c0 = pl.multiple_of(j * TK, TK)
                kcols = pl.ds(c0, TK)
                k = ks[kcols, :]
                v = vs[kcols, :]
                s2 = lax.dot_general(q2, k, (((1,), (1,)), ((), ())), preferred_element_type=F32)
                if s_scale is not None:
                    s2 = s2 * s_scale
                if has_bias or midx is not None:
                    halves = []
                    for hh in range(nh):
                        s = s2[hh * TQ:(hh + 1) * TQ]
                        if has_bias:
                            s = s + negc_ref[0, 0, pl.ds(hh, 1), kcols]
                        if midx is not None:
                            s = s + mask_ref[midx]
                        halves.append(s)
                    s2 = _cat(halves, 0)
                p2 = jnp.exp(s2 - wide(lse2))
                dp2 = lax.dot_general(do2, v, (((1,), (1,)), ((), ())), preferred_element_type=F32)
                ds2 = p2 * (dp2 - wide(delta2))
                if has_bias:
                    drow = drow + jnp.sum(ds2, axis=1, keepdims=True)
                    for hh in range(nh):
                        dneg_acc[pl.ds(hh, 1), kcols] += jnp.sum(ds2[hh * TQ:(hh + 1) * TQ], axis=0, keepdims=True)
                if s_scale is not None:
                    ds2 = ds2 * s_scale
                dsb = ds2.astype(BF16)
                dv_acc[kcols, :] += lax.dot_general(p2.astype(BF16), do2, (((0,), (0,)), ((), ())),
                                                    preferred_element_type=F32)
                dk_acc[kcols, :] += lax.dot_general(dsb, q2, (((0,), (0,)), ((), ())), preferred_element_type=F32)
                dq2 = dq2 + jnp.dot(dsb, k, preferred_element_type=F32)
                return dq2, drow

            init = (jnp.zeros((R, LANES), F32), jnp.zeros((R, 1), F32))
            if kind == "fox":
                carry = lax.fori_loop(0, i, lambda j, c: step(j, c, None), init)
                carry = step(i, carry, 0)
            elif kind == "dil":
                carry = lax.fori_loop(0, i + 1, lambda j, c: step(j, c, i - j), init)
            else:
                carry = lax.fori_loop(0, nk, lambda j, c: step(j, c, None), init)
            dq2, drow = carry
            rows = pl.ds(pl.multiple_of(i * TQ, TQ), TQ)
            dq = jnp.where(lane < HEAD_DIM, dq2[0:TQ], dq2[TQ:2 * TQ]) if pair else dq2
            if q_fold is not None:
                dq = dq * q_fold
            if has_rope:
                dq = _rope_bwd(dq, *[t[rows, :] for t in rope_refs])
            if pair:
                dqkv_ref[0, rows, 0:LANES] = dq.astype(BF16)
            else:
                dq_ref[0, rows, :] = dq.astype(BF16)
            if has_bias:
                drow_ref[0, rows, :] = jnp.where(lane < HEAD_DIM, drow[0:TQ], drow[TQ:2 * TQ])
            return 0

        lax.fori_loop(0, nq, q_loop, 0)

        def fin_kv(n, _):
            rows = pl.ds(pl.multiple_of(n * TK, TK), TK)
            dk = dk_acc[rows, :]
            if has_rope:
                dk = _rope_bwd(dk, *[t[rows, :] for t in rope_refs])
            if pair:
                dqkv_ref[0, rows, LANES:2 * LANES] = dk.astype(BF16)
                dqkv_ref[0, rows, 2 * LANES:3 * LANES] = dv_acc[rows, :].astype(BF16)
            else:
                dk_ref[0, rows, :] = dk.astype(BF16)
                dv_ref[0, rows, :] = dv_acc[rows, :].astype(BF16)
            return 0

        lax.fori_loop(0, nk, fin_kv, 0)
        if has_bias:
            dnegc_ref[0, 0] = dneg_acc[...]

    ins, in_specs = _attn_inputs(kind, src, S, negc, mask, rope, kv, (do, o, lse))
    W = cfg["n_blocks"] * LANES
    row_spec = pl.BlockSpec((1, S, LANES), lambda b, h: (b, 0, h))
    if pair:
        out_specs = [pl.BlockSpec((1, S, PAIR_W), lambda b, h: (b, 0, h))]
        out_shape = [jax.ShapeDtypeStruct((B, S, 3 * W), BF16)]
        if has_bias:
            out_specs += [pl.BlockSpec((1, 1, 2, S), lambda b, h: (b, h, 0, 0)), row_spec]
            out_shape += [jax.ShapeDtypeStruct((B, LANES // 2, 2, S), F32), jax.ShapeDtypeStruct((B, S, W), F32)]
    else:
        kv_spec = pl.BlockSpec((1, MEM_LEN, LANES), lambda b, h: (b, 0, h))
        out_specs = [row_spec, kv_spec, kv_spec]
        out_shape = [jax.ShapeDtypeStruct((B, S, W), BF16)] + [jax.ShapeDtypeStruct((B, MEM_LEN, W), BF16)] * 2
    scratch = [pltpu.VMEM((nh * S, LANES), BF16), pltpu.VMEM((Sk, LANES), BF16), pltpu.VMEM((Sk, LANES), BF16),
               pltpu.VMEM((nh * S, LANES), BF16), pltpu.VMEM((nh * S, LANES), F32), pltpu.VMEM((nh * S, LANES), F32),
               pltpu.VMEM((Sk, LANES), F32), pltpu.VMEM((Sk, LANES), F32)]
    if has_bias:
        scratch.append(pltpu.VMEM((2, S), F32))
    return pl.pallas_call(
        body, name=kind + "_attn_bwd", grid=(B, cfg["n_blocks"]),
        in_specs=in_specs, out_specs=out_specs, out_shape=out_shape, scratch_shapes=scratch,
        compiler_params=_params(("arbitrary", "arbitrary")),
    )(*ins)


def _log_masks_t(S, kind):
    return jnp.swapaxes(_log_masks(S, kind), 1, 2)


def _head_rows(hh, pair):
    row = lax.broadcasted_iota(jnp.int32, (LANES, 1), 0)
    if not pair:
        return row >= 0
    return (row >= HEAD_DIM * hh) & (row < HEAD_DIM * (hh + 1))


def _attn_t_inputs(kind, src, S, negc_cols, mask, rope, kv):
    cfg = _attn_setup(kind)
    col0 = cfg["col0"]
    ins, in_specs = [], []
    if cfg["pair"]:
        ins.append(src)
        in_specs.append(pl.BlockSpec((1, S, PAIR_W), lambda b, h: (b, 0, col0 // PAIR_W + h)))
    else:
        ins += [src, kv, kv]
        in_specs += [pl.BlockSpec((1, S, LANES), lambda b, h: (b, 0, col0 // LANES + h)),
                     pl.BlockSpec((1, MEM_LEN, LANES), lambda b, h: (b, 0, h)),
                     pl.BlockSpec((1, MEM_LEN, LANES), lambda b, h: (b, 0, MEM_HEADS + h))]
    if negc_cols is not None:
        ins.append(negc_cols)
        in_specs.append(pl.BlockSpec((1, S, LANES), lambda b, h: (b, 0, 0)))
    if mask is not None:
        ins.append(mask)
        in_specs.append(pl.BlockSpec(mask.shape, lambda b, h: (0, 0, 0)))
    if rope is not None:
        ins += list(rope)
        in_specs += [pl.BlockSpec((S, LANES), lambda b, h: (0, 0))] * 3
    return ins, in_specs


def _attn_t_prep(cfg, refs, S, Sk, *, qT2s, ks, q2s=None, vs=None, vTs=None, kTs=None, nb=None):
    pair, nh = cfg["pair"], cfg["nh"]
    lane = lax.broadcasted_iota(jnp.int32, (1, LANES), 1)
    rope_refs = refs["rope"]

    def prep_q(n, _):
        rows = pl.ds(pl.multiple_of(n * TQ, TQ), TQ)
        q = refs["load_q"](rows)
        if rope_refs is not None:
            q = _rope(q, *[t[rows, :] for t in rope_refs])
        if cfg["q_fold"] is not None:
            q = q * cfg["q_fold"]
        qb = q.astype(BF16)
        if q2s is not None:
            _store_stacked(cfg, lane, q2s, n, qb)
        qtb = qb.T
        for hh in range(nh):
            qT2s[n, :, hh * TQ:(hh + 1) * TQ] = jnp.where(_head_rows(hh, pair), qtb, jnp.zeros_like(qtb))
        return 0

    def prep_kv(n, _):
        rows = pl.ds(pl.multiple_of(n * TK, TK), TK)
        k, v = refs["load_kv"](rows)
        if rope_refs is not None:
            k = _rope(k, *[t[rows, :] for t in rope_refs])
        kb = k.astype(BF16)
        vb = v.astype(BF16)
        ks[rows, :] = kb
        if vs is not None:
            vs[rows, :] = vb
        if vTs is not None:
            vTs[n] = vb.T
        if kTs is not None:
            kTs[n] = kb.T
        if nb is not None:
            blk = refs["negc"][0, rows, :]
            for hh in range(nh):
                h = 2 * refs["block"] + hh
                col = jnp.sum(jnp.where(lane == h, blk, 0.0), axis=1, keepdims=True)
                nb[hh, rows, :] = jnp.broadcast_to(col, (TK, LANES))
        return 0

    lax.fori_loop(0, S // TQ, prep_q, 0)
    lax.fori_loop(0, Sk // TK, prep_kv, 0)


def _raw_scores_t(cfg, k, qT2):
    sT = jnp.dot(k, qT2, preferred_element_type=F32)
    if cfg["s_scale"] is not None:
        sT = sT * cfg["s_scale"]
    return sT


def _bias_mask_t(cfg, sT, nb, mask_ref, kc, midx):
    nh = cfg["nh"]
    if nb is None and midx is None:
        return sT
    parts = []
    for hh in range(nh):
        t = sT[:, hh * TQ:(hh + 1) * TQ]
        if nb is not None:
            t = t + jnp.concatenate([nb[hh, kc, :]] * (TQ // LANES), axis=1)
        if midx is not None:
            t = t + mask_ref[midx]
        parts.append(t)
    return _cat(parts, 1)


def _kv_plan(kind, i, nk):
    if kind == "fox":
        return i, (lambda j: None), 0
    if kind == "dil":
        return i, (lambda j: i - j), 0
    return nk - 1, (lambda j: None), None


def attn_fwd3(kind, src, S, *, negc_cols=None, mask=None, rope=None, kv=None):
    B = src.shape[0]
    cfg = _attn_setup(kind)
    pair, nh = cfg["pair"], cfg["nh"]
    Sk = S if pair else MEM_LEN
    has_bias, has_rope = negc_cols is not None, rope is not None
    R = nh * TQ
    nq, nk = S // TQ, Sk // TK

    def body(*refs):
        refs = list(refs)
        if pair:
            qkv_ref = refs.pop(0)
            load_q = lambda rows: qkv_ref[0, rows, 0:LANES]
            load_kv = lambda rows: (qkv_ref[0, rows, LANES:2 * LANES], qkv_ref[0, rows, 2 * LANES:3 * LANES])
        else:
            q_ref, k_ref, v_ref = refs.pop(0), refs.pop(0), refs.pop(0)
            load_q = lambda rows: q_ref[0, rows, :]
            load_kv = lambda rows: (k_ref[0, rows, :], v_ref[0, rows, :])
        negc_ref = refs.pop(0) if has_bias else None
        mask_ref = refs.pop(0) if mask is not None else None
        rope_refs = [refs.pop(0) for _ in range(3)] if has_rope else None
        o_ref, lse_ref, qT2s, ks, vTs = refs[:5]
        nb = refs[5] if has_bias else None
        _attn_t_prep(cfg, dict(load_q=load_q, load_kv=load_kv, rope=rope_refs, negc=negc_ref,
                               block=pl.program_id(1)), S, Sk,
                     qT2s=qT2s, ks=ks, vTs=vTs, nb=nb)

        def q_loop(i, _):
            qT2 = qT2s[i]

            last, mask_of, mask_last = _kv_plan(kind, i, nk)

            def cols(j):
                return pl.ds(pl.multiple_of(j * TK, TK), TK)

            def scores(j):
                return _raw_scores_t(cfg, ks[cols(j), :], qT2)

            def soft(s_raw, j, midx, m, l):
                sT = _bias_mask_t(cfg, s_raw, nb, mask_ref, cols(j), midx)
                m_new = jnp.maximum(m, jnp.max(sT, axis=0, keepdims=True))
                p = jnp.exp(sT - m_new)
                alpha = jnp.exp(m - m_new)
                return m_new, alpha * l + jnp.sum(p, axis=0, keepdims=True), alpha, p.astype(BF16)

            def pv(j, p):
                return jnp.dot(vTs[j], p, preferred_element_type=F32)

            def body(j, carry):
                s_cur, p_prev, m, l, accT = carry
                pv_prev = pv(jnp.maximum(j - 1, 0), p_prev)
                s_next = scores(j + 1)
                m, l, alpha, p = soft(s_cur, j, mask_of(j), m, l)
                return s_next, p, m, l, (accT + pv_prev) * alpha

            init = (scores(0), jnp.zeros((TK, R), BF16), jnp.full((1, R), NEG_INF, F32), jnp.zeros((1, R), F32),
                    jnp.zeros((LANES, R), F32))
            s_cur, p_prev, m, l, accT = lax.fori_loop(0, last, body, init)
            pv_prev = pv(jnp.maximum(last - 1, 0), p_prev)
            m, l, alpha, p = soft(s_cur, last, mask_last, m, l)
            accT = (accT + pv_prev) * alpha + pv(last, p)
            oT2 = accT / l
            oT = jnp.where(_head_rows(0, True), oT2[:, 0:TQ], oT2[:, TQ:2 * TQ]) if pair else oT2
            o_ref[0, pl.ds(pl.multiple_of(i * TQ, TQ), TQ), :] = oT.T
            lse_ref[0, 0, pl.ds(i, 1), :] = m + jnp.log(l)
            return 0

        lax.fori_loop(0, nq, q_loop, 0)

    ins, in_specs = _attn_t_inputs(kind, src, S, negc_cols, mask, rope, kv)
    W = cfg["n_blocks"] * LANES
    scratch = [pltpu.VMEM((nq, LANES, R), BF16), pltpu.VMEM((Sk, LANES), BF16), pltpu.VMEM((nk, LANES, TK), BF16)]
    if has_bias:
        scratch.append(pltpu.VMEM((nh, Sk, LANES), F32))
    return pl.pallas_call(
        body, name=kind + "_attn_fwd", grid=(B, cfg["n_blocks"]),
        in_specs=in_specs,
        out_specs=[pl.BlockSpec((1, S, LANES), lambda b, h: (b, 0, h)),
                   pl.BlockSpec((1, 1, nq, R), lambda b, h: (b, h, 0, 0))],
        out_shape=[jax.ShapeDtypeStruct((B, S, W), F32), jax.ShapeDtypeStruct((B, cfg["n_blocks"], nq, R), F32)],
        scratch_shapes=scratch,
        compiler_params=_params(("arbitrary", "arbitrary")),
    )(*ins)


def _tile_walk(kind, nq, nk):
    if kind == "mem":
        return nq * nk, (lambda i, j: (jnp.where(j < nk - 1, i, i + 1), jnp.where(j < nk - 1, j + 1, 0))), None
    nxt = lambda i, j: (jnp.where(j < i, i, i + 1), jnp.where(j < i, j + 1, 0))
    if kind == "fox":
        return nq * (nq + 1) // 2, nxt, (lambda i, j: jnp.where(j == i, 0, 1))
    return nq * (nq + 1) // 2, nxt, (lambda i, j: i - j)


def attn_fwd4(kind, src, S, *, negc_cols=None, mask=None, rope=None, kv=None):
    B = src.shape[0]
    cfg = _attn_setup(kind)
    pair, nh = cfg["pair"], cfg["nh"]
    Sk = S if pair else MEM_LEN
    has_bias, has_rope = negc_cols is not None, rope is not None
    R = nh * TQ
    nq, nk = S // TQ, Sk // TK
    n_pairs, successor, mask_index = _tile_walk(kind, nq, nk)
    assert n_pairs % 2 == 0

    def body(*refs):
        refs = list(refs)
        if pair:
            qkv_ref = refs.pop(0)
            load_q = lambda rows: qkv_ref[0, rows, 0:LANES]
            load_kv = lambda rows: (qkv_ref[0, rows, LANES:2 * LANES], qkv_ref[0, rows, 2 * LANES:3 * LANES])
        else:
            q_ref, k_ref, v_ref = refs.pop(0), refs.pop(0), refs.pop(0)
            load_q = lambda rows: q_ref[0, rows, :]
            load_kv = lambda rows: (k_ref[0, rows, :], v_ref[0, rows, :])
        negc_ref = refs.pop(0) if has_bias else None
        mask_ref = refs.pop(0) if mask is not None else None
        rope_refs = [refs.pop(0) for _ in range(3)] if has_rope else None
        o_ref, lse_ref, qT2s, ks, vTs, s_a, s_b, p_a, p_b, acc_all, m_all, l_all = refs[:12]
        nb = refs[12] if has_bias else None
        _attn_t_prep(cfg, dict(load_q=load_q, load_kv=load_kv, rope=rope_refs, negc=negc_ref,
                               block=pl.program_id(1)), S, Sk, qT2s=qT2s, ks=ks, vTs=vTs, nb=nb)

        def cols(j):
            return pl.ds(pl.multiple_of(j * TK, TK), TK)

        def park(i, m, l, accT):
            acc_all[i] = accT
            m_all[pl.ds(i, 1), :] = m
            l_all[pl.ds(i, 1), :] = l

        def finish(i, _):
            l = l_all[pl.ds(i, 1), :]
            oT2 = acc_all[i] / l
            oT = jnp.where(_head_rows(0, True), oT2[:, 0:TQ], oT2[:, TQ:2 * TQ]) if pair else oT2
            o_ref[0, pl.ds(pl.multiple_of(i * TQ, TQ), TQ), :] = oT.T
            lse_ref[0, 0, pl.ds(i, 1), :] = m_all[pl.ds(i, 1), :] + jnp.log(l)
            return 0

        def half(i, j, i_prev, j_prev, s_cur, s_next, p_cur, p_prev, m, l, accT):
            i_n, j_n = successor(i, j)
            acc_full = accT + jnp.dot(vTs[j_prev], p_prev[...], preferred_element_type=F32)
            s_next[...] = _raw_scores_t(cfg, ks[cols(j_n), :], qT2s[jnp.minimum(i_n, nq - 1)])
            park(i_prev, m, l, acc_full)
            first = j == 0
            m = jnp.where(first, NEG_INF, m)
            l = jnp.where(first, 0.0, l)
            sT = _bias_mask_t(cfg, s_cur[...], nb, mask_ref, cols(j), None if mask_index is None else mask_index(i, j))
            m_new = jnp.maximum(m, jnp.max(sT, axis=0, keepdims=True))
            p = jnp.exp(sT - m_new)
            alpha = jnp.exp(m - m_new)
            p_cur[...] = p.astype(BF16)
            return i_n, j_n, i, j, m_new, alpha * l + jnp.sum(p, axis=0, keepdims=True), acc_full * alpha

        def two(_, carry):
            i, j, i_prev, j_prev, m, l, accT = carry
            i, j, i_prev, j_prev, m, l, accT = half(i, j, i_prev, j_prev, s_a, s_b, p_a, p_b, m, l, accT)
            return half(i, j, i_prev, j_prev, s_b, s_a, p_b, p_a, m, l, accT)

        s_a[...] = _raw_scores_t(cfg, ks[cols(0), :], qT2s[0])
        p_b[...] = jnp.zeros((TK, R), BF16)
        zero = jnp.int32(0)
        init = (zero, zero, zero, zero, jnp.full((1, R), NEG_INF, F32), jnp.ones((1, R), F32),
                jnp.zeros((LANES, R), F32))
        _, _, i_prev, j_prev, m, l, accT = lax.fori_loop(0, n_pairs // 2, two, init)
        park(i_prev, m, l, accT + jnp.dot(vTs[j_prev], p_b[...], preferred_element_type=F32))
        lax.fori_loop(0, nq, finish, 0)

    ins, in_specs = _attn_t_inputs(kind, src, S, negc_cols, mask, rope, kv)
    W = cfg["n_blocks"] * LANES
    scratch = [pltpu.VMEM((nq, LANES, R), BF16), pltpu.VMEM((Sk, LANES), BF16), pltpu.VMEM((nk, LANES, TK), BF16),
               pltpu.VMEM((TK, R), F32), pltpu.VMEM((TK, R), F32), pltpu.VMEM((TK, R), BF16), pltpu.VMEM((TK, R), BF16),
               pltpu.VMEM((nq, LANES, R), F32), pltpu.VMEM((nq, R), F32), pltpu.VMEM((nq, R), F32)]
    if has_bias:
        scratch.append(pltpu.VMEM((nh, Sk, LANES), F32))
    return pl.pallas_call(
        body, name=kind + "_attn_fwd", grid=(B, cfg["n_blocks"]),
        in_specs=in_specs,
        out_specs=[pl.BlockSpec((1, S, LANES), lambda b, h: (b, 0, h)),
                   pl.BlockSpec((1, 1, nq, R), lambda b, h: (b, h, 0, 0))],
        out_shape=[jax.ShapeDtypeStruct((B, S, W), F32), jax.ShapeDtypeStruct((B, cfg["n_blocks"], nq, R), F32)],
        scratch_shapes=scratch,
        compiler_params=_params(("arbitrary", "arbitrary")),
    )(*ins)


def attn_bwd3(kind, src, do, o, lse, S, *, negc_cols=None, mask=None, rope=None, kv=None, token=None):
    B = src.shape[0]
    cfg = _attn_setup(kind)
    pair, nh, s_scale, q_fold = cfg["pair"], cfg["nh"], cfg["s_scale"], cfg["q_fold"]
    Sk = S if pair else MEM_LEN
    has_bias, has_rope = negc_cols is not None, rope is not None
    R = nh * TQ
    nq, nk = S // TQ, Sk // TK
    n_pairs, successor, mask_index = _tile_walk(kind, nq, nk)
    assert n_pairs % 2 == 0

    def body(*refs):
        refs = list(refs)
        if pair:
            qkv_ref = refs.pop(0)
            load_q = lambda rows: qkv_ref[0, rows, 0:LANES]
            load_kv = lambda rows: (qkv_ref[0, rows, LANES:2 * LANES], qkv_ref[0, rows, 2 * LANES:3 * LANES])
        else:
            q_ref, k_ref, v_ref = refs.pop(0), refs.pop(0), refs.pop(0)
            load_q = lambda rows: q_ref[0, rows, :]
            load_kv = lambda rows: (k_ref[0, rows, :], v_ref[0, rows, :])
        negc_ref = refs.pop(0) if has_bias else None
        mask_ref = refs.pop(0) if mask is not None else None
        rope_refs = [refs.pop(0) for _ in range(3)] if has_rope else None
        do_ref, o_ref, lse_ref = refs.pop(0), refs.pop(0), refs.pop(0)
        if token is not None:
            refs.pop(0)
        if pair:
            dqkv_ref = refs.pop(0)
            dneg_ref = refs.pop(0) if has_bias else None
            drow_ref = refs.pop(0) if has_bias else None
        else:
            dq_ref, dk_ref, dv_ref = refs.pop(0), refs.pop(0), refs.pop(0)
        qT2s, ks, q2s, vs, kTs, doT2s, do2s, delta_s, dk_acc, dv_acc = refs[:10]
        bufs_a, bufs_b, dq_all = refs[10:14], refs[14:18], refs[18]
        nb, dneg_acc, drow_all = (refs[19], refs[20], refs[21]) if has_bias else (None, None, None)
        lane = lax.broadcasted_iota(jnp.int32, (1, LANES), 1)
        _attn_t_prep(cfg, dict(load_q=load_q, load_kv=load_kv, rope=rope_refs, negc=negc_ref,
                               block=pl.program_id(1)), S, Sk,
                     qT2s=qT2s, ks=ks, q2s=q2s, vs=vs, kTs=kTs, nb=nb)

        def prep_do(n, _):
            rows = pl.ds(pl.multiple_of(n * TQ, TQ), TQ)
            dob = do_ref[0, rows, :].astype(BF16)
            _store_stacked(cfg, lane, do2s, n, dob)
            doT = dob.astype(F32).T
            prodT = doT * o_ref[0, rows, :].T
            doTb = doT.astype(BF16)
            for hh in range(nh):
                hm = _head_rows(hh, pair)
                doT2s[n, :, hh * TQ:(hh + 1) * TQ] = jnp.where(hm, doTb, jnp.zeros_like(doTb))
                delta_s[pl.ds(n, 1), hh * TQ:(hh + 1) * TQ] = jnp.sum(jnp.where(hm, prodT, 0.0), axis=0, keepdims=True)
            return 0

        def zero_kv(n, _):
            rows = pl.ds(pl.multiple_of(n * TK, TK), TK)
            dk_acc[rows, :] = jnp.zeros((TK, LANES), F32)
            dv_acc[rows, :] = jnp.zeros((TK, LANES), F32)
            if has_bias:
                for hh in range(nh):
                    dneg_acc[hh, rows, :] = jnp.zeros((TK, LANES), F32)
            return 0

        lax.fori_loop(0, nq, prep_do, 0)
        lax.fori_loop(0, nk, zero_kv, 0)

        def cols(j):
            return pl.ds(pl.multiple_of(j * TK, TK), TK)

        def rows2(i):
            return pl.ds(pl.multiple_of(i * R, R), R)

        def park(i, dqT2, drow):
            dq_all[i] = dqT2
            if has_bias:
                drow_all[pl.ds(i, 1), :] = drow

        def half(i, j, i_prev, j_prev, cur, nxt_bufs, prv, dqT2, drow):
            s_cur, dp_cur, pb_cur, dsb_cur = cur
            s_next, dp_next = nxt_bufs[0], nxt_bufs[1]
            pb_prev, dsb_prev = prv[2], prv[3]
            i_n, j_n = successor(i, j)
            first = j == 0
            if has_bias:
                drow_all[pl.ds(i_prev, 1), :] = drow
            drow = jnp.where(first, 0.0, drow)
            kc = cols(j)
            sT = _bias_mask_t(cfg, s_cur[...], nb, mask_ref, kc, None if mask_index is None else mask_index(i, j))
            pT = jnp.exp(sT - lse_ref[0, 0, pl.ds(i, 1), :])
            dsT = pT * (dp_cur[...] - delta_s[pl.ds(i, 1), :])
            if has_bias:
                drow = drow + jnp.sum(dsT, axis=0, keepdims=True)
                for hh in range(nh):
                    part = dsT[:, hh * TQ:hh * TQ + LANES]
                    for t in range(1, TQ // LANES):
                        part = part + dsT[:, hh * TQ + t * LANES:hh * TQ + (t + 1) * LANES]
                    dneg_acc[hh, kc, :] += part
            if s_scale is not None:
                dsT = dsT * s_scale
            pb_cur[...] = pT.astype(BF16)
            dsb_cur[...] = dsT.astype(BF16)
            kp = cols(j_prev)
            dv_acc[kp, :] += jnp.dot(pb_prev[...], do2s[rows2(i_prev), :], preferred_element_type=F32)
            dk_acc[kp, :] += jnp.dot(dsb_prev[...], q2s[rows2(i_prev), :], preferred_element_type=F32)
            dq_full = dqT2 + jnp.dot(kTs[j_prev], dsb_prev[...], preferred_element_type=F32)
            dq_all[i_prev] = dq_full
            dqT2 = jnp.where(first, 0.0, dq_full)
            i_nc = jnp.minimum(i_n, nq - 1)
            kn = cols(j_n)
            s_next[...] = _raw_scores_t(cfg, ks[kn, :], qT2s[i_nc])
            dp_next[...] = jnp.dot(vs[kn, :], doT2s[i_nc], preferred_element_type=F32)
            return i_n, j_n, i, j, dqT2, drow

        def two(_, carry):
            i, j, i_prev, j_prev, dqT2, drow = carry
            i, j, i_prev, j_prev, dqT2, drow = half(i, j, i_prev, j_prev, bufs_a, bufs_b, bufs_b, dqT2, drow)
            return half(i, j, i_prev, j_prev, bufs_b, bufs_a, bufs_a, dqT2, drow)

        bufs_a[0][...] = _raw_scores_t(cfg, ks[cols(0), :], qT2s[0])
        bufs_a[1][...] = jnp.dot(vs[cols(0), :], doT2s[0], preferred_element_type=F32)
        bufs_b[2][...] = jnp.zeros((TK, R), BF16)
        bufs_b[3][...] = jnp.zeros((TK, R), BF16)
        zero = jnp.int32(0)
        init = (zero, zero, zero, zero, jnp.zeros((LANES, R), F32), jnp.zeros((1, R), F32))
        _, _, i_prev, j_prev, dqT2, drow = lax.fori_loop(0, n_pairs // 2, two, init)
        kp = cols(j_prev)
        dv_acc[kp, :] += jnp.dot(bufs_b[2][...], do2s[rows2(i_prev), :], preferred_element_type=F32)
        dk_acc[kp, :] += jnp.dot(bufs_b[3][...], q2s[rows2(i_prev), :], preferred_element_type=F32)
        park(i_prev, dqT2 + jnp.dot(kTs[j_prev], bufs_b[3][...], preferred_element_type=F32), drow)

        def fin_q(i, _):
            rows = pl.ds(pl.multiple_of(i * TQ, TQ), TQ)
            dqT2 = dq_all[i]
            dqT = jnp.where(_head_rows(0, True), dqT2[:, 0:TQ], dqT2[:, TQ:2 * TQ]) if pair else dqT2
            dq = dqT.T
            if q_fold is not None:
                dq = dq * q_fold
            if has_rope:
                dq = _rope_bwd(dq, *[t[rows, :] for t in rope_refs])
            if pair:
                dqkv_ref[0, rows, 0:LANES] = dq.astype(BF16)
            else:
                dq_ref[0, rows, :] = dq.astype(BF16)
            if has_bias:
                drow_ref[0, 0, pl.ds(i, 1), :] = drow_all[pl.ds(i, 1), :]
            return 0

        lax.fori_loop(0, nq, fin_q, 0)

        def fin_kv(n, _):
            rows = pl.ds(pl.multiple_of(n * TK, TK), TK)
            dk = dk_acc[rows, :]
            if has_rope:
                dk = _rope_bwd(dk, *[t[rows, :] for t in rope_refs])
            if pair:
                dqkv_ref[0, rows, LANES:2 * LANES] = dk.astype(BF16)
                dqkv_ref[0, rows, 2 * LANES:3 * LANES] = dv_acc[rows, :].astype(BF16)
            else:
                dk_ref[0, rows, :] = dk.astype(BF16)
                dv_ref[0, rows, :] = dv_acc[rows, :].astype(BF16)
            if has_bias:
                x0 = jnp.sum(dneg_acc[0, rows, :], axis=1, keepdims=True)
                x1 = jnp.sum(dneg_acc[1, rows, :], axis=1, keepdims=True)
                dneg_ref[0, rows, :] = jnp.where(lane == 0, x0, jnp.where(lane == 1, x1, 0.0))
            return 0

        lax.fori_loop(0, nk, fin_kv, 0)

    ins, in_specs = _attn_t_inputs(kind, src, S, negc_cols, mask, rope, kv)
    row_spec = pl.BlockSpec((1, S, LANES), lambda b, h: (b, 0, h))
    vec_spec = pl.BlockSpec((1, 1, nq, R), lambda b, h: (b, h, 0, 0))
    ins += [do, o, lse]
    in_specs += [row_spec, row_spec, vec_spec]
    if token is not None:
        ins.append(token)
        in_specs.append(pl.BlockSpec(token.shape, lambda b, h: (0, 0)))
    W = cfg["n_blocks"] * LANES
    if pair:
        out_specs = [pl.BlockSpec((1, S, PAIR_W), lambda b, h: (b, 0, h))]
        out_shape = [jax.ShapeDtypeStruct((B, S, 3 * W), BF16)]
        if has_bias:
            out_specs += [row_spec, vec_spec]
            out_shape += [jax.ShapeDtypeStruct((B, S, W), F32), jax.ShapeDtypeStruct((B, cfg["n_blocks"], nq, R), F32)]
    else:
        kv_spec = pl.BlockSpec((1, MEM_LEN, LANES), lambda b, h: (b, 0, h))
        out_specs = [row_spec, kv_spec, kv_spec]
        out_shape = [jax.ShapeDtypeStruct((B, S, W), BF16)] + [jax.ShapeDtypeStruct((B, MEM_LEN, W), BF16)] * 2
    scratch = [pltpu.VMEM((nq, LANES, R), BF16), pltpu.VMEM((Sk, LANES), BF16), pltpu.VMEM((nh * S, LANES), BF16),
               pltpu.VMEM((Sk, LANES), BF16), pltpu.VMEM((nk, LANES, TK), BF16), pltpu.VMEM((nq, LANES, R), BF16),
               pltpu.VMEM((nh * S, LANES), BF16), pltpu.VMEM((nq, R), F32),
               pltpu.VMEM((Sk, LANES), F32), pltpu.VMEM((Sk, LANES), F32)]
    pair_bufs = [pltpu.VMEM((TK, R), F32), pltpu.VMEM((TK, R), F32), pltpu.VMEM((TK, R), BF16), pltpu.VMEM((TK, R), BF16)]
    scratch += pair_bufs + pair_bufs + [pltpu.VMEM((nq, LANES, R), F32)]
    if has_bias:
        scratch += [pltpu.VMEM((nh, Sk, LANES), F32), pltpu.VMEM((nh, Sk, LANES), F32), pltpu.VMEM((nq, R), F32)]
    return pl.pallas_call(
        body, name=kind + "_attn_bwd", grid=(B, cfg["n_blocks"]),
        in_specs=in_specs, out_specs=out_specs, out_shape=out_shape, scratch_shapes=scratch,
        compiler_params=_params(("arbitrary", "arbitrary")),
    )(*ins)


def _sigmoid(g):
    return 1.0 / (1.0 + jnp.exp(-g))


def out_fwd(proj, o_fox, o_dil, o_mem, w_out, x, target, gf, tm):
    T = x.shape[0]

    def body(fg_ref, dg_ref, mg_ref, of_ref, od_ref, om_ref, w_ref, x_ref, t_ref, gf_ref,
             y_ref, dx_ref, dxb_ref, sm_ref):
        parts = []
        for g_ref, o_ref in ((fg_ref, of_ref), (dg_ref, od_ref), (mg_ref, om_ref)):
            g = g_ref[...]
            parts.append((o_ref[...] * (g * _sigmoid(g))).astype(BF16))
        ymix = jnp.concatenate(parts, axis=1)
        y_ref[...] = ymix
        x2 = x_ref[...] + jnp.dot(ymix, w_ref[...], preferred_element_type=F32)
        r = lax.rsqrt(jnp.mean(x2 * x2, axis=-1, keepdims=True) + RMS_EPS)
        yn = x2 * r
        err = yn * gf_ref[...] - t_ref[...]
        loss = 0.5 * jnp.sum(jnp.sum(err * err, axis=-1, keepdims=True) / D_MODEL, axis=0, keepdims=True)
        dyf = err / D_MODEL
        dgf = jnp.sum(dyf * yn, axis=0, keepdims=True)
        dyn = dyf * gf_ref[...]
        dx2 = r * (dyn - yn * jnp.mean(dyn * yn, axis=-1, keepdims=True))
        dx_ref[...] = dx2
        dxb_ref[...] = dx2.astype(BF16)
        row = lax.broadcasted_iota(jnp.int32, (8, D_MODEL), 0)
        upd = jnp.where(row == 0, dgf, jnp.where(row == 1, loss, 0.0))

        @pl.when(pl.program_id(0) == 0)
        def _():
            sm_ref[...] = upd

        @pl.when(pl.program_id(0) != 0)
        def _():
            sm_ref[...] += upd

    def rows(w, col=0):
        return pl.BlockSpec((tm, w), lambda i: (i, col))

    return pl.pallas_call(
        body, name="out_fwd", grid=(T // tm,),
        in_specs=[rows(FOX_W, P_FG // FOX_W), rows(DIL_W, P_DG // DIL_W), rows(MEM_W, P_MG // MEM_W),
                  rows(FOX_W), rows(DIL_W), rows(MEM_W),
                  pl.BlockSpec((MIX_W, D_MODEL), lambda i: (0, 0)),
                  rows(D_MODEL), rows(D_MODEL), pl.BlockSpec((1, D_MODEL), lambda i: (0, 0))],
        out_specs=[rows(MIX_W), rows(D_MODEL), rows(D_MODEL), pl.BlockSpec((8, D_MODEL), lambda i: (0, 0))],
        out_shape=[jax.ShapeDtypeStruct((T, MIX_W), BF16), jax.ShapeDtypeStruct((T, D_MODEL), F32),
                   jax.ShapeDtypeStruct((T, D_MODEL), BF16), jax.ShapeDtypeStruct((8, D_MODEL), F32)],
        compiler_params=_params(("arbitrary",)),
    )(proj, proj, proj, o_fox, o_dil, o_mem, w_out, x, target, gf)


def out_bwd(proj, o_fox, o_dil, o_mem, w_out, dx2b, tm):
    T = dx2b.shape[0]

    def body(fg_ref, dg_ref, mg_ref, of_ref, od_ref, om_ref, w_ref, dx_ref,
             dof_ref, dod_ref, dom_ref, dfg_ref, ddg_ref, dmg_ref):
        dmix = lax.dot_general(dx_ref[...], w_ref[...], (((1,), (1,)), ((), ())), preferred_element_type=F32)
        col = 0
        for g_ref, o_ref, do_ref, dgate_ref in ((fg_ref, of_ref, dof_ref, dfg_ref), (dg_ref, od_ref, dod_ref, ddg_ref),
                                                 (mg_ref, om_ref, dom_ref, dmg_ref)):
            w = g_ref.shape[1]
            d = dmix[:, col:col + w]
            col += w
            g = g_ref[...]
            sg = _sigmoid(g)
            do_ref[...] = d * (g * sg)
            dgate_ref[...] = (d * o_ref[...] * (sg * (1.0 + g * (1.0 - sg)))).astype(BF16)

    def rows(w, col=0):
        return pl.BlockSpec((tm, w), lambda i: (i, col))

    return pl.pallas_call(
        body, name="out_bwd", grid=(T // tm,),
        in_specs=[rows(FOX_W, P_FG // FOX_W), rows(DIL_W, P_DG // DIL_W), rows(MEM_W, P_MG // MEM_W),
                  rows(FOX_W), rows(DIL_W), rows(MEM_W),
                  pl.BlockSpec((MIX_W, D_MODEL), lambda i: (0, 0)), rows(D_MODEL)],
        out_specs=[rows(FOX_W), rows(DIL_W), rows(MEM_W), rows(FOX_W), rows(DIL_W), rows(MEM_W)],
        out_shape=[jax.ShapeDtypeStruct((T, FOX_W), F32), jax.ShapeDtypeStruct((T, DIL_W), F32),
                   jax.ShapeDtypeStruct((T, MEM_W), F32), jax.ShapeDtypeStruct((T, FOX_W), BF16),
                   jax.ShapeDtypeStruct((T, DIL_W), BF16), jax.ShapeDtypeStruct((T, MEM_W), BF16)],
        compiler_params=_params(("arbitrary",)),
    )(proj, proj, proj, o_fox, o_dil, o_mem, w_out, dx2b)


def out_step(proj, o_fox, o_dil, o_mem, w_out, x, target, gf, tm):
    T = x.shape[0]

    def body(fg_ref, dg_ref, mg_ref, of_ref, od_ref, om_ref, w_ref, x_ref, t_ref, gf_ref,
             dx_ref, dof_ref, dod_ref, dom_ref, dfg_ref, ddg_ref, dmg_ref, gw_ref, sm_ref, gw_acc):
        branches = []
        for g_ref, o_ref in ((fg_ref, of_ref), (dg_ref, od_ref), (mg_ref, om_ref)):
            g = g_ref[...]
            sg = _sigmoid(g)
            o = o_ref[...]
            branches.append((g, sg, o))
        ymix = jnp.concatenate([(o * (g * sg)).astype(BF16) for g, sg, o in branches], axis=1)
        x2 = x_ref[...] + jnp.dot(ymix, w_ref[...], preferred_element_type=F32)
        r = lax.rsqrt(jnp.mean(x2 * x2, axis=-1, keepdims=True) + RMS_EPS)
        yn = x2 * r
        err = yn * gf_ref[...] - t_ref[...]
        loss = 0.5 * jnp.sum(jnp.sum(err * err, axis=-1, keepdims=True) / D_MODEL, axis=0, keepdims=True)
        dyf = err / D_MODEL
        dgf = jnp.sum(dyf * yn, axis=0, keepdims=True)
        dyn = dyf * gf_ref[...]
        dx2 = r * (dyn - yn * jnp.mean(dyn * yn, axis=-1, keepdims=True))
        dx_ref[...] = dx2
        dxb = dx2.astype(BF16)
        dmix = lax.dot_general(dxb, w_ref[...], (((1,), (1,)), ((), ())), preferred_element_type=F32)
        col = 0
        for (g, sg, o), do_ref, dgate_ref in zip(branches, (dof_ref, dod_ref, dom_ref), (dfg_ref, ddg_ref, dmg_ref)):
            d = dmix[:, col:col + g.shape[1]]
            col += g.shape[1]
            do_ref[...] = (d * (g * sg)).astype(BF16)
            dgate_ref[...] = (d * o * (sg * (1.0 + g * (1.0 - sg)))).astype(BF16)
        row = lax.broadcasted_iota(jnp.int32, (8, D_MODEL), 0)
        upd = jnp.where(row == 0, dgf, jnp.where(row == 1, loss, 0.0))

        @pl.when(pl.program_id(0) == 0)
        def _():
            sm_ref[...] = jnp.zeros(sm_ref.shape, F32)
            gw_acc[...] = jnp.zeros(gw_acc.shape, F32)

        sm_ref[...] += upd
        gw_acc[...] += lax.dot_general(ymix, dxb, (((0,), (0,)), ((), ())), preferred_element_type=F32)

        @pl.when(pl.program_id(0) == T // tm - 1)
        def _():
            gw_ref[...] = gw_acc[...].astype(BF16)

    def rows(w, col=0):
        return pl.BlockSpec((tm, w), lambda i: (i, col))

    return pl.pallas_call(
        body, name="out_step", grid=(T // tm,),
        in_specs=[rows(FOX_W, P_FG // FOX_W), rows(DIL_W, P_DG // DIL_W), rows(MEM_W, P_MG // MEM_W),
                  rows(FOX_W), rows(DIL_W), rows(MEM_W),
                  pl.BlockSpec((MIX_W, D_MODEL), lambda i: (0, 0)),
                  rows(D_MODEL), rows(D_MODEL), pl.BlockSpec((1, D_MODEL), lambda i: (0, 0))],
        out_specs=[rows(D_MODEL), rows(FOX_W), rows(DIL_W), rows(MEM_W), rows(FOX_W), rows(DIL_W), rows(MEM_W),
                   pl.BlockSpec((MIX_W, D_MODEL), lambda i: (0, 0)), pl.BlockSpec((8, D_MODEL), lambda i: (0, 0))],
        out_shape=[jax.ShapeDtypeStruct((T, D_MODEL), F32), jax.ShapeDtypeStruct((T, FOX_W), BF16),
                   jax.ShapeDtypeStruct((T, DIL_W), BF16), jax.ShapeDtypeStruct((T, MEM_W), BF16),
                   jax.ShapeDtypeStruct((T, FOX_W), BF16), jax.ShapeDtypeStruct((T, DIL_W), BF16),
                   jax.ShapeDtypeStruct((T, MEM_W), BF16), jax.ShapeDtypeStruct((MIX_W, D_MODEL), BF16),
                   jax.ShapeDtypeStruct((8, D_MODEL), F32)],
        scratch_shapes=[pltpu.VMEM((MIX_W, D_MODEL), F32)],
        compiler_params=_params(("arbitrary",)),
    )(proj, proj, proj, o_fox, o_dil, o_mem, w_out, x, target, gf)


def adamw(w, g, m, v, tr, name):
    lead = w.shape[:-2]
    R, C = w.shape[-2:]
    zeros = (0,) * len(lead)

    def body(w_ref, g_ref, m_ref, v_ref, d_ref, mo_ref, vo_ref):
        gv = g_ref[...]
        mn = ADAM_B1 * m_ref[...] + (1.0 - ADAM_B1) * gv
        vn = ADAM_B2 * v_ref[...] + (1.0 - ADAM_B2) * jnp.square(gv)
        m_hat = mn / (1.0 - ADAM_B1 ** ADAM_STEP)
        v_hat = vn / (1.0 - ADAM_B2 ** ADAM_STEP)
        d_ref[...] = -ADAM_LR * (m_hat / (jnp.sqrt(v_hat) + ADAM_EPS) + ADAM_WD * w_ref[...])
        mo_ref[...] = mn
        vo_ref[...] = vn

    spec = pl.BlockSpec((1,) * len(lead) + (tr, C), lambda i: zeros + (i, 0))
    return pl.pallas_call(
        body, name=name, grid=(pl.cdiv(R, tr),),
        in_specs=[spec] * 4, out_specs=[spec] * 3,
        out_shape=[jax.ShapeDtypeStruct(w.shape, F32)] * 3,
        compiler_params=_params(("arbitrary",)),
    )(w, g, m, v)


def _pad_row(v, width):
    return jnp.concatenate([v, jnp.zeros((1, width - v.shape[1]), v.dtype)], axis=1)


def _old_local_grads(x, mem, norm_g, b_forget, mem_norm_g, final_norm_g, loss_target, w_in_p, w_kv, w_out):
    B, S, D = x.shape
    T = B * S
    xt = x.reshape(T, D)
    memt = mem.reshape(B * MEM_LEN, D)
    b_pad = _pad_row(b_forget, LANES)

    h = rms_fwd(xt, norm_g, 512, "rms_x")
    proj = mm_nn(h, w_in_p, 512, PW // 3, "in_proj")
    proj3 = proj.reshape(B, S, PW)
    mh = rms_fwd(memt, mem_norm_g, B * MEM_LEN, "rms_mem")
    mkv = mm_nn(mh, w_kv, B * MEM_LEN, 2 * MEM_W, "mem_kv_proj")
    mkv3 = mkv.reshape(B, MEM_LEN, 2 * MEM_W)

    negc = fox_gate(proj3, b_pad)
    causal = _log_masks_t(S, "causal")
    causal = jnp.concatenate([causal, jnp.zeros_like(causal)], axis=0)
    dilated = _log_masks_t(S, "dilated")
    rope = _rope_tables(S)

    o_fox, lse_fox = attn_fwd4("fox", proj3, S, negc_cols=negc, mask=causal)
    o_dil, lse_dil = attn_fwd4("dil", proj3, S, mask=dilated, rope=rope)
    o_mem, lse_mem = attn_fwd4("mem", proj3, S, kv=mkv3)

    dx2, do_fox, do_dil, do_mem, dfg, ddg, dmg, g_out, small_out = out_step(
        proj, o_fox.reshape(T, FOX_W), o_dil.reshape(T, DIL_W), o_mem.reshape(T, MEM_W), w_out,
        xt, loss_target.reshape(T, D), final_norm_g.reshape(1, D), 256)

    dqkv_fox, dneg, drow = attn_bwd3("fox", proj3, do_fox.reshape(B, S, FOX_W), o_fox, lse_fox, S,
                                     negc_cols=negc, mask=causal)
    (dqkv_dil,) = attn_bwd3("dil", proj3, do_dil.reshape(B, S, DIL_W), o_dil, lse_dil, S, mask=dilated, rope=rope)
    dmq, dmk, dmv = attn_bwd3("mem", proj3, do_mem.reshape(B, S, MEM_W), o_mem, lse_mem, S, kv=mkv3)
    drow = drow.reshape(B, FOX_HEADS // 2, S // TQ, 2, TQ).transpose(0, 1, 3, 2, 4).reshape(B, FOX_HEADS, S)
    drow = jnp.pad(drow, ((0, 0), (0, LANES - FOX_HEADS), (0, 0)))
    dflog, db_part = fox_gate_bwd(drow, dneg, proj3, b_pad)

    groups = [[(dfg, P_FG), (ddg, P_DG), (dmg, P_MG)],
              [(dqkv_fox.reshape(T, 3 * FOX_W), P_FOX), (dflog.reshape(T, LANES), P_FLOG)],
              [(dqkv_dil.reshape(T, 3 * DIL_W), P_DIL), (dmq.reshape(T, MEM_W), P_MQ)]]
    g_in = [mm_tn_multi(h_t,[arr for arr, _ in grp], 512, "w_in_grad_%d" % n) for n, grp in enumerate(groups)]
    grad_x, dng = in_proj_bwd_rms([piece for grp in groups for piece in grp], w_in_p, xt, norm_g, dx2, 256)

    dmkv = jnp.concatenate([dmk, dmv], axis=2).reshape(B * MEM_LEN, 2 * MEM_W)
    g_kv = mm_tn(mh, dmkv, B * MEM_LEN, 2 * MEM_W, "w_kv_grad")
    dmh = mm_nt(dmkv, w_kv, B * MEM_LEN, D, "mem_kv_bwd")
    _, dmng = rms_bwd(memt, mem_norm_g, dmh, None, B * MEM_LEN, "rms_mem_bwd")

    small = jnp.concatenate([dng[0:1], dmng[0:1], small_out[0:1], _pad_row(db_part[0:1], D), small_out[1:2],
                             jnp.zeros((3, D), F32)], axis=0)
    return grad_x.reshape(B, S, D), g_in, g_kv, g_out, small


def _old2_local_grads(x, mem, norm_g, b_forget, mem_norm_g, final_norm_g, loss_target, w_in_p, w_kv, w_out, start_exchange):
    B, S, D = x.shape
    T = B * S
    xt = x.reshape(T, D)
    memt = mem.reshape(B * MEM_LEN, D)
    b_pad = _pad_row(b_forget, LANES)

    h = rms_fwd(xt, norm_g, 512, "rms_x")
    proj = mm_nn(h, w_in_p, 512, PW // 3, "in_proj")
    proj3 = proj.reshape(B, S, PW)
    mh = rms_fwd(memt, mem_norm_g, B * MEM_LEN, "rms_mem")
    mkv = mm_nn(mh, w_kv, B * MEM_LEN, 2 * MEM_W, "mem_kv_proj")
    mkv3 = mkv.reshape(B, MEM_LEN, 2 * MEM_W)

    negc = fox_gate(proj3, b_pad)
    causal = _log_masks_t(S, "causal")
    causal = jnp.concatenate([causal, jnp.zeros_like(causal)], axis=0)
    dilated = _log_masks_t(S, "dilated")
    rope = _rope_tables(S)

    o_fox, lse_fox = attn_fwd4("fox", proj3, S, negc_cols=negc, mask=causal)
    o_dil, lse_dil = attn_fwd4("dil", proj3, S, mask=dilated, rope=rope)
    o_mem, lse_mem = attn_fwd4("mem", proj3, S, kv=mkv3)

    dx2, do_fox, do_dil, do_mem, dfg, ddg, dmg, g_out, small_out = out_step(
        proj, o_fox.reshape(T, FOX_W), o_dil.reshape(T, DIL_W), o_mem.reshape(T, MEM_W), w_out,
        xt, loss_target.reshape(T, D), final_norm_g.reshape(1, D), 256)

    gates = [(dfg, P_FG), (ddg, P_DG), (dmg, P_MG)]
    g_gates = mm_tn_multi(h_t, [arr for arr, _ in gates], 1024, "w_in_grad_gates", BF16)
    first, token = start_exchange([g_gates, g_out], "early_exchange_a")

    dqkv_fox, dneg, drow = attn_bwd3("fox", proj3, do_fox.reshape(B, S, FOX_W), o_fox, lse_fox, S,
                                     negc_cols=negc, mask=causal, token=token)
    drow = drow.reshape(B, FOX_HEADS // 2, S // TQ, 2, TQ).transpose(0, 1, 3, 2, 4).reshape(B, FOX_HEADS, S)
    drow = jnp.pad(drow, ((0, 0), (0, LANES - FOX_HEADS), (0, 0)))
    dflog, db_part = fox_gate_bwd(drow, dneg, proj3, b_pad)
    fox = [(dqkv_fox.reshape(T, 3 * FOX_W), P_FOX), (dflog.reshape(T, LANES), P_FLOG)]
    g_fox = mm_tn_multi(h_t, [arr for arr, _ in fox], 1024, "w_in_grad_fox", BF16)
    second, token = start_exchange([g_fox], "early_exchange_b")

    (dqkv_dil,) = attn_bwd3("dil", proj3, do_dil.reshape(B, S, DIL_W), o_dil, lse_dil, S, mask=dilated, rope=rope,
                            token=token)
    dmq, dmk, dmv = attn_bwd3("mem", proj3, do_mem.reshape(B, S, MEM_W), o_mem, lse_mem, S, kv=mkv3)
    rest = [(dqkv_dil.reshape(T, 3 * DIL_W), P_DIL), (dmq.reshape(T, MEM_W), P_MQ)]
    g_rest = mm_tn_multi(h_t,[arr for arr, _ in rest], 512, "w_in_grad_rest")
    grad_x, dng = in_proj_bwd_rms(gates + fox + rest, w_in_p, xt, norm_g, dx2, 256)

    dmkv = jnp.concatenate([dmk, dmv], axis=2).reshape(B * MEM_LEN, 2 * MEM_W)
    g_kv = mm_tn(mh, dmkv, B * MEM_LEN, 2 * MEM_W, "w_kv_grad")
    dmh = mm_nt(dmkv, w_kv, B * MEM_LEN, D, "mem_kv_bwd")
    _, dmng = rms_bwd(memt, mem_norm_g, dmh, None, B * MEM_LEN, "rms_mem_bwd")

    small = jnp.concatenate([dng[0:1], dmng[0:1], small_out[0:1], _pad_row(db_part[0:1], D), small_out[1:2],
                             jnp.zeros((3, D), F32)], axis=0)
    return grad_x.reshape(B, S, D), [(first, dqkv_fox), (second, dqkv_dil)], [g_rest, g_kv], small


def local_grads(x, mem, norm_g, b_forget, mem_norm_g, final_norm_g, loss_target, w_in_p, small_weights, start_exchange):
    B, S, D = x.shape
    T = B * S
    xt = x.reshape(T, D)
    memt = mem.reshape(B * MEM_LEN, D)
    b_pad = _pad_row(b_forget, LANES)

    h, h_t = rms_fwd(xt, norm_g, 512, "rms_x", with_transpose=True)
    proj = mm_nn(h, w_in_p, 512, PW // 3, "in_proj")
    proj3 = proj.reshape(B, S, PW)

    negc = fox_gate(proj3, b_pad)
    causal = _log_masks_t(S, "causal")
    causal = jnp.concatenate([causal, jnp.zeros_like(causal)], axis=0)
    dilated = _log_masks_t(S, "dilated")
    rope = _rope_tables(S)

    o_fox, lse_fox = attn_fwd4("fox", proj3, S, negc_cols=negc, mask=causal)
    o_dil, lse_dil = attn_fwd4("dil", proj3, S, mask=dilated, rope=rope)

    w_kv, w_out = small_weights(o_dil)
    mh, mh_t = rms_fwd(memt, mem_norm_g, B * MEM_LEN, "rms_mem", with_transpose=True)
    mkv = mm_nn(mh, w_kv, B * MEM_LEN, 2 * MEM_W, "mem_kv_proj")
    mkv3 = mkv.reshape(B, MEM_LEN, 2 * MEM_W)
    o_mem, lse_mem = attn_fwd4("mem", proj3, S, kv=mkv3)

    dx2, do_fox, do_dil, do_mem, dfg, ddg, dmg, g_out, small_out = out_step(
        proj, o_fox.reshape(T, FOX_W), o_dil.reshape(T, DIL_W), o_mem.reshape(T, MEM_W), w_out,
        xt, loss_target.reshape(T, D), final_norm_g.reshape(1, D), 256)

    gates = [(dfg, P_FG), (ddg, P_DG), (dmg, P_MG)]
    g_gates = mm_tn_multi(h_t, [arr for arr, _ in gates], 1024, "w_in_grad_gates", BF16)
    first, token = start_exchange([g_gates, g_out], "early_exchange_a")

    dqkv_fox, dneg, drow = attn_bwd3("fox", proj3, do_fox.reshape(B, S, FOX_W), o_fox, lse_fox, S,
                                     negc_cols=negc, mask=causal, token=token)
    drow = drow.reshape(B, FOX_HEADS // 2, S // TQ, 2, TQ).transpose(0, 1, 3, 2, 4).reshape(B, FOX_HEADS, S)
    drow = jnp.pad(drow, ((0, 0), (0, LANES - FOX_HEADS), (0, 0)))
    dflog, db_part = fox_gate_bwd(drow, dneg, proj3, b_pad)
    fox = [(dqkv_fox.reshape(T, 3 * FOX_W), P_FOX), (dflog.reshape(T, LANES), P_FLOG)]
    g_fox = mm_tn_multi(h_t, [arr for arr, _ in fox], 1024, "w_in_grad_fox", BF16)
    second, token = start_exchange([g_fox], "early_exchange_b")

    (dqkv_dil,) = attn_bwd3("dil", proj3, do_dil.reshape(B, S, DIL_W), o_dil, lse_dil, S, mask=dilated, rope=rope,
                            token=token)
    dil = [(dqkv_dil.reshape(T, 3 * DIL_W), P_DIL)]
    g_dil = mm_tn_multi(h_t, [arr for arr, _ in dil], 1024, "w_in_grad_dil", BF16)
    third, token = start_exchange([g_dil], "early_exchange_c")

    dmq, dmk, dmv = attn_bwd3("mem", proj3, do_mem.reshape(B, S, MEM_W), o_mem, lse_mem, S, kv=mkv3, token=token)
    mq = [(dmq.reshape(T, MEM_W), P_MQ)]
    g_mq = mm_tn_multi(h_t, [arr for arr, _ in mq], 1024, "w_in_grad_mq", BF16)
    dmkv = jnp.concatenate([dmk, dmv], axis=2).reshape(B * MEM_LEN, 2 * MEM_W)
    g_kv = mm_tn_multi(mh_t, [dmkv], B * MEM_LEN, "w_kv_grad", BF16)
    fourth, token = start_exchange([g_mq, g_kv], "early_exchange_d")

    grad_x, dng = in_proj_bwd_rms(gates + fox + dil + mq, w_in_p, xt, norm_g, dx2, 256, token)
    dmh = mm_nt(dmkv, w_kv, B * MEM_LEN, D, "mem_kv_bwd")
    _, dmng = rms_bwd(memt, mem_norm_g, dmh, None, B * MEM_LEN, "rms_mem_bwd")

    small = jnp.concatenate([dng[0:1], dmng[0:1], small_out[0:1], _pad_row(db_part[0:1], D), small_out[1:2],
                             jnp.zeros((3, D), F32)], axis=0)
    early = [(first, dqkv_fox), (second, dqkv_dil), (third, dmq), (fourth, grad_x)]
    return grad_x.reshape(B, S, D), early, small


def kernel(x, mem, norm_g, w_in, b_forget, mem_norm_g, w_mem_kv, w_out, final_norm_g, loss_target, m_norm_g, m_w_in, m_b_forget, m_mem_norm_g, m_w_mem_kv, m_w_out, m_final_norm_g, v_norm_g, v_w_in, v_b_forget, v_mem_norm_g, v_w_mem_kv, v_w_out, v_final_norm_g):
    D = D_MODEL
    (w_in_full,) = weight_gather([_pack_cols(w_in).astype(BF16).reshape(w_in.shape[1], PW)])
    gather, _ = early_exchange_start([w_mem_kv[0].astype(BF16), w_out[0].astype(BF16)], "early_gather", gather=True,
                                     after=w_in_full)

    def small_weights(after):
        _, gathered = early_exchange_wait(gather, after, "early_gather_wait")
        return gathered

    grad_x, early, small = local_grads(
        x, mem, norm_g, b_forget, mem_norm_g, final_norm_g, loss_target, w_in_full, small_weights,
        early_exchange_start)

    (first, after_first), (second, after_second), (third, after_third), (fourth, after_fourth) = early
    (src_gates, src_out), (land_gates, land_out) = early_exchange_wait(first, after_first, "early_wait_a")
    (src_fox,), (land_fox,) = early_exchange_wait(second, after_second, "early_wait_b")
    (src_dil,), (land_dil,) = early_exchange_wait(third, after_third, "early_wait_c")
    (src_mq, src_kv), (land_mq, land_kv) = early_exchange_wait(fourth, after_fourth, "early_wait_d")
    gates = slot_sum8(src_gates, land_gates, 128, "sum_w_in_gates")
    gw_out = slot_sum8(src_out, land_out, 256, "sum_w_out")
    fox = slot_sum8(src_fox, land_fox, 128, "sum_w_in_fox")
    dil = slot_sum8(src_dil, land_dil, 128, "sum_w_in_dil")
    mq = slot_sum8(src_mq, land_mq, 128, "sum_w_in_mq")
    gw_kv = slot_sum8(src_kv, land_kv, 128, "sum_w_kv")

    (csum,) = grad_exchange_d2d([], small)
    (tot,) = grad_exchange_ici([], csum)
    gw_in = _unpack_cols(jnp.concatenate(
        [fox[:, :3 * FOX_W], gates[:, :FOX_W], dil, gates[:, FOX_W:FOX_W + DIL_W], mq,
         gates[:, FOX_W + DIL_W:], fox[:, 3 * FOX_W:]], axis=1)[None])

    loss = tot[4, 0]
    g_norm, g_mem_norm, g_final, g_b = tot[0:1], tot[1:2], tot[2], tot[3:4, :FOX_HEADS]

    def rows8(*rows):
        rows = [r.reshape(1, -1) for r in rows]
        rows = [_pad_row(r, D) for r in rows]
        return jnp.concatenate(rows + [jnp.zeros((8 - len(rows), D), F32)], axis=0)

    sw = rows8(norm_g, mem_norm_g, final_norm_g, b_forget)
    sm = rows8(m_norm_g, m_mem_norm_g, m_final_norm_g, m_b_forget)
    sv = rows8(v_norm_g, v_mem_norm_g, v_final_norm_g, v_b_forget)
    d_s, m_s, v_s = adamw(sw, tot, sm, sv, 8, "adamw_small")
    d_in, m_in, v_in = adamw(w_in, gw_in, m_w_in, v_w_in, 32, "adamw_w_in")
    d_kv, m_kv, v_kv = adamw(w_mem_kv[0], gw_kv, m_w_mem_kv[0], v_w_mem_kv[0], 128, "adamw_w_kv")
    d_out, m_out, v_out = adamw(w_out[0], gw_out, m_w_out[0], v_w_out[0], 256, "adamw_w_out")

    def small_outs(t):
        return t[0:1], t[3:4, :FOX_HEADS], t[1:2], t[2]

    grads = (g_norm, gw_in, g_b, g_mem_norm, gw_kv[None], gw_out[None], g_final)
    outs = []
    for t, big in ((d_s, (d_in, d_kv, d_out)), (m_s, (m_in, m_kv, m_out)), (v_s, (v_in, v_kv, v_out))):
        n, b, mn, f = small_outs(t)
        outs += [n, big[0], b, mn, big[1][None], big[2][None], f]
    return (loss, grad_x, *grads, *outs)
```

```python
import functools
import math

import numpy as np
import jax
import jax.numpy as jnp
from jax import lax
from jax.experimental import pallas as pl
from jax.experimental.pallas import tpu as pltpu

F32 = jnp.float32
BF16 = jnp.bfloat16

D_MODEL = 1024
HEAD_DIM = 64
FOX_HEADS = 12
DIL_HEADS = 12
MEM_HEADS = 4
MEM_HEAD_DIM = 128
MEM_LEN = 256
FOX_W = FOX_HEADS * HEAD_DIM
DIL_W = DIL_HEADS * HEAD_DIM
MEM_W = MEM_HEADS * MEM_HEAD_DIM
MIX_W = FOX_W + DIL_W + MEM_W
DILATIONS = ((128, 1), (512, 4), (2048, 16))
ROPE_THETA = 500000.0
ROPE_DIM = HEAD_DIM // 4
RMS_EPS = 1e-6
NEG_INF = -1e30
IN_W = 4 * FOX_W + FOX_HEADS + 4 * DIL_W + 2 * MEM_W

ADAM_LR = 0.001
ADAM_B1 = 0.9
ADAM_B2 = 0.999
ADAM_EPS = 1e-08
ADAM_WD = 0.01
ADAM_STEP = 10

N_DEV = 8
LANES = 128
PAIR_W = 3 * LANES
TQ = 256
TK = 256

O_FQ, O_FK, O_FV, O_FG = 0, FOX_W, 2 * FOX_W, 3 * FOX_W
O_FLOG = 4 * FOX_W
O_DQ = O_FLOG + FOX_HEADS
O_DK, O_DV, O_DG = O_DQ + DIL_W, O_DQ + 2 * DIL_W, O_DQ + 3 * DIL_W
O_MQ = O_DQ + 4 * DIL_W
O_MG = O_MQ + MEM_W
P_FOX = 0
P_FG = P_FOX + 3 * FOX_W
P_DIL = P_FG + FOX_W
P_DG = P_DIL + 3 * DIL_W
P_MQ = P_DG + DIL_W
P_MG = P_MQ + MEM_W
P_FLOG = P_MG + MEM_W
PW = P_FLOG + LANES

VMEM_LIMIT = 56 * 1024 * 1024


def _pack_pieces():
    pieces = []
    for base in (O_FQ, O_DQ):
        seg = []
        for hp in range(FOX_HEADS // 2):
            for part in range(3):
                seg.append((base + part * FOX_W + hp * LANES, LANES))
        pieces.append(seg)
    fox, dil = pieces
    return fox + [(O_FG, FOX_W)] + dil + [(O_DG, DIL_W), (O_MQ, MEM_W), (O_MG, MEM_W), (O_FLOG, FOX_HEADS)]


def _pack_cols(w):
    parts = [w[..., s:s + n] for s, n in _pack_pieces()]
    parts.append(jnp.zeros(w.shape[:-1] + (LANES - FOX_HEADS,), w.dtype))
    return jnp.concatenate(parts, axis=-1)


def _unpack_cols(g):
    runs = []
    pos = 0
    for s, n in _pack_pieces():
        runs.append((s, n, pos))
        pos += n
    runs.sort()
    return jnp.concatenate([g[..., p:p + n] for s, n, p in runs], axis=-1)


def _params(sem=None, **kw):
    return pltpu.CompilerParams(dimension_semantics=sem, vmem_limit_bytes=VMEM_LIMIT, **kw)


def _mesh_pos():
    return lax.axis_index("x"), lax.axis_index("y"), lax.axis_index("c")


def _flip(v, d):
    return 1 - v if d else v


_RELATIONS = [(dx, dy, dc) for dx in (0, 1) for dy in (0, 1) for dc in (0, 1)][1:]


def weight_gather(shards):
    n_arr = len(shards)
    rows = [s.shape[0] for s in shards]

    def body(*refs):
        in_refs = refs[:n_arr]
        out_refs = refs[n_arr:2 * n_arr]
        send_sems, recv_sems, local_sems = refs[2 * n_arr:]
        x, y, c = _mesh_pos()
        me, sibling = (x, y, c), (x, y, 1 - c)
        x_nbr, y_nbr, diag = (1 - x, y, c), (x, 1 - y, c), (1 - x, 1 - y, c)
        north = c == 1
        relay_from = (jnp.where(north, 1 - x, x), jnp.where(north, y, 1 - y), c)
        relay_to = (jnp.where(north, x, 1 - x), jnp.where(north, 1 - y, y), c)
        k_from = jnp.where(north, 1, 2)
        k_to = 3 - k_from

        def block(a, pos):
            px, py, pc = pos
            return out_refs[a].at[pl.ds((4 * px + 2 * py + pc) * rows[a], rows[a]), :]

        def copy(a, k, blk, to, src=None):
            return pltpu.make_async_remote_copy(
                src_ref=block(a, blk) if src is None else src, dst_ref=block(a, blk),
                send_sem=send_sems.at[a, k], recv_sem=recv_sems.at[a, k],
                device_id=to, device_id_type=pl.DeviceIdType.MESH)

        started = []
        mine = []
        for a in range(n_arr):
            cp = pltpu.make_async_copy(in_refs[a], block(a, me), local_sems.at[a])
            cp.start()
            mine.append(cp)
            first = [copy(a, 0, me, sibling, src=in_refs[a]), copy(a, 1, me, x_nbr, src=in_refs[a]),
                     copy(a, 2, me, y_nbr, src=in_refs[a])]
            for cp in first:
                cp.start()
            started += first
        for a in range(n_arr):
            copy(a, k_from, relay_from, me).wait_recv()
            second_hop = copy(a, 3, relay_from, relay_to)
            second_hop.start()
            passed = copy(a, 3 + k_from, relay_from, sibling)
            passed.start()
            started += [second_hop, passed]
        for a in range(n_arr):
            copy(a, k_to, relay_to, me).wait_recv()
            passed = copy(a, 3 + k_to, relay_to, sibling)
            passed.start()
            started.append(passed)
        for a in range(n_arr):
            copy(a, 3, diag, me).wait_recv()
            passed = copy(a, 6, diag, sibling)
            passed.start()
            started.append(passed)
        for a in range(n_arr):
            copy(a, 0, sibling, me).wait_recv()
            for k, chip in ((4, x_nbr), (5, y_nbr), (6, diag)):
                copy(a, k, (chip[0], chip[1], 1 - c), me).wait_recv()
        for cp in started:
            cp.wait_send()
        for cp in mine:
            cp.wait()

    any_spec = pl.BlockSpec(memory_space=pl.ANY)
    return pl.pallas_call(
        body, name="weight_gather",
        out_shape=[jax.ShapeDtypeStruct((N_DEV * s.shape[0], s.shape[1]), s.dtype) for s in shards],
        in_specs=[any_spec] * n_arr, out_specs=[any_spec] * n_arr,
        scratch_shapes=[pltpu.SemaphoreType.DMA((n_arr, 7)), pltpu.SemaphoreType.DMA((n_arr, 7)),
                        pltpu.SemaphoreType.DMA((n_arr,))],
    )(*shards)


def grad_exchange(grads, small):
    arrs = list(grads) + [small]
    n_arr = len(arrs)
    rows = [g.shape[0] // N_DEV for g in grads] + [small.shape[0]]

    def body(*refs):
        in_refs = refs[:n_arr]
        out_refs = refs[n_arr:2 * n_arr]
        send_sems, recv_sems, local_sems = refs[2 * n_arr:]
        x, y, c = _mesh_pos()
        me = 4 * x + 2 * y + c

        def src(a, idx):
            if a == n_arr - 1:
                return in_refs[a]
            return in_refs[a].at[pl.ds(idx * rows[a], rows[a]), :]

        def copy(a, k):
            dx, dy, dc = _RELATIONS[k]
            px, py, pc = _flip(x, dx), _flip(y, dy), _flip(c, dc)
            peer = 4 * px + 2 * py + pc
            send = pltpu.make_async_remote_copy(
                src_ref=src(a, peer), dst_ref=out_refs[a].at[me],
                send_sem=send_sems.at[a, k], recv_sem=recv_sems.at[a, k],
                device_id=(px, py, pc), device_id_type=pl.DeviceIdType.MESH)
            recv = pltpu.make_async_remote_copy(
                src_ref=src(a, peer), dst_ref=out_refs[a].at[peer],
                send_sem=send_sems.at[a, k], recv_sem=recv_sems.at[a, k],
                device_id=(px, py, pc), device_id_type=pl.DeviceIdType.MESH)
            return send, recv

        mine = []
        pairs = []
        for a in range(n_arr):
            cp = pltpu.make_async_copy(src(a, me), out_refs[a].at[me], local_sems.at[a])
            cp.start()
            mine.append(cp)
            for k in range(7):
                send, recv = copy(a, k)
                send.start()
                pairs.append((send, recv))
        for send, recv in pairs:
            recv.wait_recv()
        for send, recv in pairs:
            send.wait_send()
        for cp in mine:
            cp.wait()

    any_spec = pl.BlockSpec(memory_space=pl.ANY)
    return pl.pallas_call(
        body, name="grad_exchange",
        out_shape=[jax.ShapeDtypeStruct((N_DEV, r, a.shape[1]), a.dtype) for r, a in zip(rows, arrs)],
        in_specs=[any_spec] * n_arr, out_specs=[any_spec] * n_arr,
        scratch_shapes=[pltpu.SemaphoreType.DMA((n_arr, 7)), pltpu.SemaphoreType.DMA((n_arr, 7)),
                        pltpu.SemaphoreType.DMA((n_arr,))],
    )(*arrs)


def slot_sum(slots, tr, name):
    _, R, C = slots.shape

    def body(s_ref, o_ref):
        acc = s_ref[0]
        for d in range(1, N_DEV):
            acc = acc + s_ref[d]
        o_ref[...] = acc

    return pl.pallas_call(
        body, name=name, grid=(R // tr,),
        in_specs=[pl.BlockSpec((N_DEV, tr, C), lambda i: (0, i, 0))],
        out_specs=pl.BlockSpec((tr, C), lambda i: (i, 0)),
        out_shape=jax.ShapeDtypeStruct((R, C), slots.dtype),
        compiler_params=_params(("arbitrary",)),
    )(slots)


N_CHIP = 4
_OTHER_CHIPS = [(1, 0), (0, 1), (1, 1)]


def grad_exchange_d2d(grads, small):
    n_big = len(grads)
    rows = [g.shape[0] // N_DEV for g in grads]

    def body(*refs):
        g_refs = refs[:n_big]
        small_ref = refs[n_big]
        out_refs = refs[n_big + 1:2 * n_big + 1]
        csum_ref = refs[2 * n_big + 1]
        land, send_sems, recv_sems = refs[2 * n_big + 2:]
        x, y, c = _mesh_pos()
        sibling = (x, y, 1 - c)
        copies = []
        for a in range(n_big):
            for q in range(N_CHIP):
                copies.append(pltpu.make_async_remote_copy(
                    src_ref=g_refs[a].at[pl.ds((2 * q + 1 - c) * rows[a], rows[a]), :], dst_ref=out_refs[a].at[q],
                    send_sem=send_sems.at[a, q], recv_sem=recv_sems.at[a, q],
                    device_id=sibling, device_id_type=pl.DeviceIdType.MESH))
        copies.append(pltpu.make_async_remote_copy(
            src_ref=small_ref, dst_ref=land, send_sem=send_sems.at[n_big, 0], recv_sem=recv_sems.at[n_big, 0],
            device_id=sibling, device_id_type=pl.DeviceIdType.MESH))
        for cp in copies:
            cp.start()
        for cp in copies:
            cp.wait_recv()
        for cp in copies:
            cp.wait_send()
        csum_ref[...] = small_ref[...] + land[...]

    any_spec = pl.BlockSpec(memory_space=pl.ANY)
    vmem_spec = pl.BlockSpec(memory_space=pltpu.VMEM)
    return pl.pallas_call(
        body, name="grad_exchange_d2d",
        out_shape=[jax.ShapeDtypeStruct((N_CHIP, r, g.shape[1]), g.dtype) for r, g in zip(rows, grads)]
        + [jax.ShapeDtypeStruct(small.shape, small.dtype)],
        in_specs=[any_spec] * n_big + [vmem_spec], out_specs=[any_spec] * n_big + [vmem_spec],
        scratch_shapes=[pltpu.VMEM(small.shape, small.dtype),
                        pltpu.SemaphoreType.DMA((n_big + 1, N_CHIP)), pltpu.SemaphoreType.DMA((n_big + 1, N_CHIP))],
    )(*grads, small)


def chip_sum(g, got, tr, name):
    _, rows, cols = got.shape
    g4 = g.reshape(N_CHIP, 2, rows, cols)
    core = lax.axis_index("c").astype(jnp.int32).reshape(1)

    def body(c_ref, g_ref, r_ref, o_ref):
        o_ref[0] = (g_ref[0, 0] + r_ref[0]).astype(BF16)

    return pl.pallas_call(
        body, name=name,
        grid_spec=pltpu.PrefetchScalarGridSpec(
            num_scalar_prefetch=1, grid=(N_CHIP, rows // tr),
            in_specs=[pl.BlockSpec((1, 1, tr, cols), lambda q, i, w: (q, w[0], i, 0)),
                      pl.BlockSpec((1, tr, cols), lambda q, i, w: (q, i, 0))],
            out_specs=pl.BlockSpec((1, tr, cols), lambda q, i, w: (q, i, 0))),
        out_shape=jax.ShapeDtypeStruct((N_CHIP, rows, cols), BF16),
        compiler_params=_params(("arbitrary", "arbitrary")),
    )(core, g4, got)


def grad_exchange_ici(parts, csum):
    n_big = len(parts)

    def body(*refs):
        p_refs = refs[:n_big]
        csum_ref = refs[n_big]
        out_refs = refs[n_big + 1:2 * n_big + 1]
        tot_ref = refs[2 * n_big + 1]
        land, send_sems, recv_sems, local_sems = refs[2 * n_big + 2:]
        x, y, c = _mesh_pos()
        q_me = 2 * x + y
        land[q_me] = csum_ref[...]
        mine = [pltpu.make_async_copy(p_refs[a].at[q_me], out_refs[a].at[q_me], local_sems.at[a]) for a in range(n_big)]
        for cp in mine:
            cp.start()
        sends, recvs = [], []
        for j, (dx, dy) in enumerate(_OTHER_CHIPS):
            px, py = _flip(x, dx), _flip(y, dy)
            q_peer = 2 * px + py
            for a in range(n_big + 1):
                src = p_refs[a].at[q_peer] if a < n_big else csum_ref
                dst = out_refs[a] if a < n_big else land
                common = dict(send_sem=send_sems.at[a, j], recv_sem=recv_sems.at[a, j],
                              device_id=(px, py, c), device_id_type=pl.DeviceIdType.MESH)
                sends.append(pltpu.make_async_remote_copy(src_ref=src, dst_ref=dst.at[q_me], **common))
                recvs.append(pltpu.make_async_remote_copy(src_ref=src, dst_ref=dst.at[q_peer], **common))
        for cp in sends:
            cp.start()
        for cp in recvs:
            cp.wait_recv()
        for cp in sends:
            cp.wait_send()
        for cp in mine:
            cp.wait()
        tot = land[0]
        for q in range(1, N_CHIP):
            tot = tot + land[q]
        tot_ref[...] = tot

    any_spec = pl.BlockSpec(memory_space=pl.ANY)
    vmem_spec = pl.BlockSpec(memory_space=pltpu.VMEM)
    return pl.pallas_call(
        body, name="grad_exchange_ici",
        out_shape=[jax.ShapeDtypeStruct(p.shape, p.dtype) for p in parts] + [jax.ShapeDtypeStruct(csum.shape, csum.dtype)],
        in_specs=[any_spec] * n_big + [vmem_spec], out_specs=[any_spec] * n_big + [vmem_spec],
        scratch_shapes=[pltpu.VMEM((N_CHIP,) + csum.shape, csum.dtype),
                        pltpu.SemaphoreType.DMA((n_big + 1, 3)), pltpu.SemaphoreType.DMA((n_big + 1, 3)),
                        pltpu.SemaphoreType.DMA((max(n_big, 1),))],
    )(*parts, csum)


def final_sum(got, tr, name):
    _, rows, cols = got.shape

    def body(got_ref, o_ref):
        acc = got_ref[0].astype(F32)
        for q in range(1, N_CHIP):
            acc = acc + got_ref[q].astype(F32)
        o_ref[...] = acc

    return pl.pallas_call(
        body, name=name, grid=(rows // tr,),
        in_specs=[pl.BlockSpec((N_CHIP, tr, cols), lambda i: (0, i, 0))],
        out_specs=pl.BlockSpec((tr, cols), lambda i: (i, 0)),
        out_shape=jax.ShapeDtypeStruct((rows, cols), F32),
        compiler_params=_params(("arbitrary",)),
    )(got)


_HBM = pl.BlockSpec(memory_space=pltpu.HBM)
_SEM = pl.BlockSpec(memory_space=pltpu.SEMAPHORE)
_EFFECT = pltpu.SideEffectType.DATAFLOW_SIDE_EFFECTING


def _old_early_copies(src_refs, land_refs, send_sems, recv_sems, rows):
    x, y, c = _mesh_pos()
    me = 4 * x + 2 * y + c
    copies = []
    for a in range(len(src_refs)):
        for k, (dx, dy, dc) in enumerate(_RELATIONS):
            px, py, pc = _flip(x, dx), _flip(y, dy), _flip(c, dc)
            peer = 4 * px + 2 * py + pc
            copies.append(pltpu.make_async_remote_copy(
                src_ref=src_refs[a].at[pl.ds(peer * rows[a], rows[a]), :], dst_ref=land_refs[a].at[me],
                send_sem=send_sems.at[a, k], recv_sem=recv_sems.at[a, k],
                device_id=(px, py, pc), device_id_type=pl.DeviceIdType.MESH))
    return copies


def _old_early_exchange_start(srcs, name):
    n = len(srcs)
    rows = [s.shape[0] // N_DEV for s in srcs]
    lands = [lax.empty((N_DEV, r, s.shape[1]), s.dtype) for r, s in zip(rows, srcs)]

    def body(*refs):
        src_refs, land_refs = refs[:n], refs[n:2 * n]
        send_sems, recv_sems = refs[2 * n], refs[2 * n + 1]
        token = refs[-1]
        for cp in _early_copies(src_refs, land_refs, send_sems, recv_sems, rows):
            cp.start()
        token[...] = jnp.zeros_like(token)

    hbm = lambda a: pltpu.HBM(a.shape, a.dtype)
    outs = pl.pallas_call(
        body, name=name,
        out_shape=[pltpu.SemaphoreType.DMA((n, 7)), pltpu.SemaphoreType.DMA((n, 7))]
        + [hbm(a) for a in srcs] + [hbm(a) for a in lands] + [jax.ShapeDtypeStruct((8, LANES), F32)],
        in_specs=[_HBM] * (2 * n),
        out_specs=[_SEM, _SEM] + [_HBM] * (2 * n) + [pl.BlockSpec(memory_space=pltpu.VMEM)],
        input_output_aliases={i: 2 + i for i in range(2 * n)},
        compiler_params=pltpu.CompilerParams(has_side_effects=_EFFECT),
    )(*[pltpu.with_memory_space_constraint(a, pltpu.HBM) for a in list(srcs) + lands])
    return dict(sems=outs[:2], srcs=outs[2:2 + n], lands=outs[2 + n:2 + 2 * n], rows=rows), outs[-1]


def _old_early_exchange_wait(handle, after, name):
    n = len(handle["srcs"])
    rows = handle["rows"]

    def body(*refs):
        src_refs, land_refs = refs[:n], refs[n:2 * n]
        send_sems, recv_sems = refs[2 * n], refs[2 * n + 1]
        for cp in _early_copies(src_refs, land_refs, send_sems, recv_sems, rows):
            cp.wait_send()
            cp.wait_recv()

    hbm = lambda a: pltpu.HBM(a.shape, a.dtype)
    ins = list(handle["srcs"]) + list(handle["lands"])
    outs = pl.pallas_call(
        body, name=name,
        out_shape=[hbm(a) for a in ins],
        in_specs=[_HBM] * (2 * n) + [_SEM, _SEM, pl.BlockSpec(memory_space=pl.ANY)],
        out_specs=[_HBM] * (2 * n),
        input_output_aliases={i: i for i in range(2 * n)},
        compiler_params=pltpu.CompilerParams(has_side_effects=_EFFECT),
    )(*ins, *handle["sems"], after)
    return outs[:n], outs[n:]


def _old_slot_sum8(src, land, tr, name):
    _, rows, cols = land.shape
    x, y, c = _mesh_pos()
    me = (4 * x + 2 * y + c).astype(jnp.int32).reshape(1)

    def body(me_ref, src_ref, land_ref, o_ref):
        acc = None
        for d in range(N_DEV):
            term = jnp.where(d == me_ref[0], src_ref[0], land_ref[d]).astype(F32)
            acc = term if acc is None else acc + term
        o_ref[...] = acc

    return pl.pallas_call(
        body, name=name,
        grid_spec=pltpu.PrefetchScalarGridSpec(
            num_scalar_prefetch=1, grid=(rows // tr,),
            in_specs=[pl.BlockSpec((1, tr, cols), lambda i, w: (w[0], i, 0)),
                      pl.BlockSpec((N_DEV, tr, cols), lambda i, w: (0, i, 0))],
            out_specs=pl.BlockSpec((tr, cols), lambda i, w: (i, 0))),
        out_shape=jax.ShapeDtypeStruct((rows, cols), F32),
        compiler_params=_params(("arbitrary",)),
    )(me, src.reshape(N_DEV, rows, cols), land)


def _old2_early_copies(src_refs, land_refs, send_sems, recv_sems, rows):
    x, y, c = _mesh_pos()
    me = 4 * x + 2 * y + c
    copies = []
    for a in range(len(src_refs)):
        for dx, dy, dc in _RELATIONS:
            px, py, pc = _flip(x, dx), _flip(y, dy), _flip(c, dc)
            peer = 4 * px + 2 * py + pc
            copies.append(pltpu.make_async_remote_copy(
                src_ref=src_refs[a].at[pl.ds(peer * rows[a], rows[a]), :],
                dst_ref=land_refs[a].at[pl.ds(me * rows[a], rows[a]), :],
                send_sem=send_sems[a], recv_sem=recv_sems[a],
                device_id=(px, py, pc), device_id_type=pl.DeviceIdType.MESH))
    return copies


def _old2_early_exchange_start(srcs, name):
    n = len(srcs)
    rows = [s.shape[0] // N_DEV for s in srcs]
    lands = [lax.empty(s.shape, s.dtype) for s in srcs]

    def body(*refs):
        src_refs, land_refs = refs[:n], refs[n:2 * n]
        send_sems, recv_sems = refs[2 * n:3 * n], refs[3 * n:4 * n]
        token = refs[-1]
        for cp in _early_copies(src_refs, land_refs, send_sems, recv_sems, rows):
            cp.start()
        token[...] = jnp.zeros_like(token)

    hbm = lambda a: pltpu.HBM(a.shape, a.dtype)
    outs = pl.pallas_call(
        body, name=name,
        out_shape=[pltpu.SemaphoreType.DMA(())] * (2 * n)
        + [hbm(a) for a in srcs] + [hbm(a) for a in lands] + [jax.ShapeDtypeStruct((8, LANES), F32)],
        in_specs=[_HBM] * (2 * n),
        out_specs=[_SEM] * (2 * n) + [_HBM] * (2 * n) + [pl.BlockSpec(memory_space=pltpu.VMEM)],
        input_output_aliases={i: 2 * n + i for i in range(2 * n)},
        compiler_params=pltpu.CompilerParams(has_side_effects=_EFFECT),
    )(*[pltpu.with_memory_space_constraint(a, pltpu.HBM) for a in list(srcs) + lands])
    return dict(sems=outs[:2 * n], srcs=outs[2 * n:3 * n], lands=outs[3 * n:4 * n], rows=rows), outs[-1]


def _early_copies(src_refs, land_refs, send_sems, recv_sems, rows, gather):
    x, y, c = _mesh_pos()
    me = 4 * x + 2 * y + c
    copies = []
    for a in range(len(src_refs)):
        for dx, dy, dc in _RELATIONS:
            px, py, pc = _flip(x, dx), _flip(y, dy), _flip(c, dc)
            peer = 4 * px + 2 * py + pc
            copies.append(pltpu.make_async_remote_copy(
                src_ref=src_refs[a] if gather else src_refs[a].at[pl.ds(peer * rows[a], rows[a]), :],
                dst_ref=land_refs[a].at[pl.ds(me * rows[a], rows[a]), :],
                send_sem=send_sems[a], recv_sem=recv_sems[a],
                device_id=(px, py, pc), device_id_type=pl.DeviceIdType.MESH))
    return copies


def early_exchange_start(srcs, name, gather=False, after=None):
    n = len(srcs)
    if gather:
        rows = [s.shape[0] for s in srcs]
        me = 4 * lax.axis_index("x") + 2 * lax.axis_index("y") + lax.axis_index("c")
        lands = [lax.dynamic_update_slice(lax.empty((N_DEV * r, s.shape[1]), s.dtype), s, (me * r, 0))
                 for r, s in zip(rows, srcs)]
    else:
        rows = [s.shape[0] // N_DEV for s in srcs]
        lands = [lax.empty(s.shape, s.dtype) for s in srcs]

    extra = [] if after is None else [after]

    def body(*refs):
        src_refs, land_refs = refs[:n], refs[n:2 * n]
        first_sem = 2 * n + len(extra)
        send_sems, recv_sems = refs[first_sem:first_sem + n], refs[first_sem + n:first_sem + 2 * n]
        token = refs[-1]
        for cp in _early_copies(src_refs, land_refs, send_sems, recv_sems, rows, gather):
            cp.start()
        token[...] = jnp.zeros_like(token)

    hbm = lambda a: pltpu.HBM(a.shape, a.dtype)
    outs = pl.pallas_call(
        body, name=name,
        out_shape=[pltpu.SemaphoreType.DMA(())] * (2 * n)
        + [hbm(a) for a in srcs] + [hbm(a) for a in lands] + [jax.ShapeDtypeStruct((8, LANES), F32)],
        in_specs=[_HBM] * (2 * n) + [pl.BlockSpec(memory_space=pl.ANY)] * len(extra),
        out_specs=[_SEM] * (2 * n) + [_HBM] * (2 * n) + [pl.BlockSpec(memory_space=pltpu.VMEM)],
        input_output_aliases={i: 2 * n + i for i in range(2 * n)},
        compiler_params=pltpu.CompilerParams(has_side_effects=_EFFECT),
    )(*[pltpu.with_memory_space_constraint(a, pltpu.HBM) for a in list(srcs) + lands], *extra)
    return dict(sems=outs[:2 * n], srcs=outs[2 * n:3 * n], lands=outs[3 * n:4 * n], rows=rows), outs[-1]


def early_exchange_wait(handle, after, name):
    n = len(handle["srcs"])
    rows = handle["rows"]

    def body(*refs):
        src_refs, land_refs = refs[:n], refs[n:2 * n]
        send_sems, recv_sems = refs[2 * n:3 * n], refs[3 * n:4 * n]
        x, y, c = _mesh_pos()
        for a in range(n):
            seven = pl.ds(0, 7 * rows[a])
            all_seven = pltpu.make_async_remote_copy(
                src_ref=land_refs[a].at[seven, :], dst_ref=land_refs[a].at[seven, :],
                send_sem=send_sems[a], recv_sem=recv_sems[a],
                device_id=(x, y, c), device_id_type=pl.DeviceIdType.MESH)
            all_seven.wait_send()
            all_seven.wait_recv()

    hbm = lambda a: pltpu.HBM(a.shape, a.dtype)
    ins = list(handle["srcs"]) + list(handle["lands"])
    outs = pl.pallas_call(
        body, name=name,
        out_shape=[hbm(a) for a in ins],
        in_specs=[_HBM] * (2 * n) + [_SEM] * (2 * n) + [pl.BlockSpec(memory_space=pl.ANY)],
        out_specs=[_HBM] * (2 * n),
        input_output_aliases={i: i for i in range(2 * n)},
        compiler_params=pltpu.CompilerParams(has_side_effects=_EFFECT),
    )(*ins, *handle["sems"], after)
    return outs[:n], outs[n:]


def slot_sum8(src, land, tr, name):
    rows, cols = land.shape[0] // N_DEV, land.shape[1]
    x, y, c = _mesh_pos()
    me = (4 * x + 2 * y + c).astype(jnp.int32).reshape(1)

    def body(me_ref, src_ref, land_ref, o_ref):
        acc = None
        for d in range(N_DEV):
            term = jnp.where(d == me_ref[0], src_ref[0], land_ref[d]).astype(F32)
            acc = term if acc is None else acc + term
        o_ref[...] = acc

    return pl.pallas_call(
        body, name=name,
        grid_spec=pltpu.PrefetchScalarGridSpec(
            num_scalar_prefetch=1, grid=(rows // tr,),
            in_specs=[pl.BlockSpec((1, tr, cols), lambda i, w: (w[0], i, 0)),
                      pl.BlockSpec((N_DEV, tr, cols), lambda i, w: (0, i, 0))],
            out_specs=pl.BlockSpec((tr, cols), lambda i, w: (i, 0))),
        out_shape=jax.ShapeDtypeStruct((rows, cols), F32),
        compiler_params=_params(("arbitrary",)),
    )(me, src.reshape(N_DEV, rows, cols), land.reshape(N_DEV, rows, cols))


def mm_tn_multi(a_t, bs, tt, name, out_dtype=F32):
    K, T = a_t.shape
    widths = [b.shape[1] for b in bs]
    steps = T // tt

    def body(a_ref, *rest):
        b_refs, o_ref, acc = rest[:-2], rest[-2], rest[-1]
        @pl.when(pl.program_id(0) == 0)
        def _():
            acc[...] = jnp.zeros(acc.shape, F32)

        av = a_ref[...]
        col = 0
        for b_ref, w in zip(b_refs, widths):
            acc[:, col:col + w] += jnp.dot(av, b_ref[...], preferred_element_type=F32)
            col += w

        @pl.when(pl.program_id(0) == steps - 1)
        def _():
            o_ref[...] = acc[...].astype(out_dtype)

    return pl.pallas_call(
        body, name=name, grid=(steps,),
        in_specs=[pl.BlockSpec((K, tt), lambda t: (0, t))] + [pl.BlockSpec((tt, w), lambda t: (t, 0)) for w in widths],
        out_specs=pl.BlockSpec((K, sum(widths)), lambda t: (0, 0)),
        out_shape=jax.ShapeDtypeStruct((K, sum(widths)), out_dtype),
        scratch_shapes=[pltpu.VMEM((K, sum(widths)), F32)],
        compiler_params=_params(("arbitrary",)),
    )(a_t, *bs)


def rms_fwd(x, g, tm, name, with_transpose=False):
    M, K = x.shape

    def body(x_ref, g_ref, o_ref, *t_ref):
        xv = x_ref[...]
        r = lax.rsqrt(jnp.mean(xv * xv, axis=-1, keepdims=True) + RMS_EPS)
        h = ((xv * r) * g_ref[...]).astype(BF16)
        o_ref[...] = h
        if with_transpose:
            t_ref[0][...] = h.T

    out_specs = [pl.BlockSpec((tm, K), lambda i: (i, 0))]
    out_shape = [jax.ShapeDtypeStruct((M, K), BF16)]
    if with_transpose:
        out_specs.append(pl.BlockSpec((K, tm), lambda i: (0, i)))
        out_shape.append(jax.ShapeDtypeStruct((K, M), BF16))
    outs = pl.pallas_call(
        body, name=name, grid=(M // tm,),
        in_specs=[pl.BlockSpec((tm, K), lambda i: (i, 0)), pl.BlockSpec((1, K), lambda i: (0, 0))],
        out_specs=out_specs, out_shape=out_shape,
        compiler_params=_params(("arbitrary",)),
    )(x, g)
    return outs if with_transpose else outs[0]


def rms_bwd(x, g, dh, dres, tm, name):
    M, K = x.shape
    has_res = dres is not None

    def body(*refs):
        if has_res:
            x_ref, g_ref, dh_ref, dres_ref, dx_ref, dg_ref = refs
        else:
            x_ref, g_ref, dh_ref, dx_ref, dg_ref = refs
        xv = x_ref[...]
        r = lax.rsqrt(jnp.mean(xv * xv, axis=-1, keepdims=True) + RMS_EPS)
        xn = xv * r
        dhv = dh_ref[...]
        dxn = dhv * g_ref[...]
        dx = r * (dxn - xn * jnp.mean(dxn * xn, axis=-1, keepdims=True))
        if has_res:
            dx = dx + dres_ref[...]
        dx_ref[...] = dx
        part = jnp.sum(dhv * xn, axis=0, keepdims=True)
        row = lax.broadcasted_iota(jnp.int32, (8, K), 0)
        upd = jnp.where(row == 0, part, 0.0)

        @pl.when(pl.program_id(0) == 0)
        def _():
            dg_ref[...] = upd

        @pl.when(pl.program_id(0) != 0)
        def _():
            dg_ref[...] += upd

    row_spec = pl.BlockSpec((tm, K), lambda i: (i, 0))
    ins = [x, g, dh] + ([dres] if has_res else [])
    in_specs = [row_spec, pl.BlockSpec((1, K), lambda i: (0, 0)), row_spec] + ([row_spec] if has_res else [])
    return pl.pallas_call(
        body, name=name, grid=(M // tm,),
        in_specs=in_specs,
        out_specs=[row_spec, pl.BlockSpec((8, K), lambda i: (0, 0))],
        out_shape=[jax.ShapeDtypeStruct((M, K), F32), jax.ShapeDtypeStruct((8, K), F32)],
        compiler_params=_params(("arbitrary",)),
    )(*ins)


def mm_nn(a, b, tm, tn, name, token=None):
    M, K = a.shape
    N = b.shape[1]
    extra = [] if token is None else [token]

    def body(a_ref, b_ref, *rest):
        rest[-1][...] = jnp.dot(a_ref[...], b_ref[...], preferred_element_type=F32)

    return pl.pallas_call(
        body, name=name, grid=(N // tn, M // tm),
        in_specs=[pl.BlockSpec((tm, K), lambda j, i: (i, 0)), pl.BlockSpec((K, tn), lambda j, i: (0, j))]
        + [pl.BlockSpec(t.shape, lambda j, i: (0, 0)) for t in extra],
        out_specs=pl.BlockSpec((tm, tn), lambda j, i: (i, j)),
        out_shape=jax.ShapeDtypeStruct((M, N), F32),
        compiler_params=_params(("arbitrary", "arbitrary")),
    )(a, b, *extra)


def mm_nt(a, b, tm, tk, name):
    M, K = a.shape
    N = b.shape[0]

    def body(a_ref, b_ref, o_ref):
        part = lax.dot_general(a_ref[...], b_ref[...], (((1,), (1,)), ((), ())), preferred_element_type=F32)

        @pl.when(pl.program_id(1) == 0)
        def _():
            o_ref[...] = part

        @pl.when(pl.program_id(1) != 0)
        def _():
            o_ref[...] += part

    return pl.pallas_call(
        body, name=name, grid=(M // tm, K // tk),
        in_specs=[pl.BlockSpec((tm, tk), lambda i, k: (i, k)), pl.BlockSpec((N, tk), lambda i, k: (0, k))],
        out_specs=pl.BlockSpec((tm, N), lambda i, k: (i, 0)),
        out_shape=jax.ShapeDtypeStruct((M, N), F32),
        compiler_params=_params(("arbitrary", "arbitrary")),
    )(a, b)


def mm_tn(a, b, tt, tn, name):
    T, K = a.shape
    N = b.shape[1]

    def body(a_ref, b_ref, o_ref):
        part = lax.dot_general(a_ref[...], b_ref[...], (((0,), (0,)), ((), ())), preferred_element_type=F32)

        @pl.when(pl.program_id(1) == 0)
        def _():
            o_ref[...] = part

        @pl.when(pl.program_id(1) != 0)
        def _():
            o_ref[...] += part

    return pl.pallas_call(
        body, name=name, grid=(N // tn, T // tt),
        in_specs=[pl.BlockSpec((tt, K), lambda j, t: (t, 0)), pl.BlockSpec((tt, tn), lambda j, t: (t, j))],
        out_specs=pl.BlockSpec((K, tn), lambda j, t: (0, j)),
        out_shape=jax.ShapeDtypeStruct((K, N), F32),
        compiler_params=_params(("arbitrary", "arbitrary")),
    )(a, b)


def _old_mm_tn_multi(a, bs, tt, name):
    T, K = a.shape
    widths = [b.shape[1] for b in bs]

    def body(a_ref, *rest):
        b_refs, o_ref = rest[:-1], rest[-1]
        av = a_ref[...]
        parts = [lax.dot_general(av, b_ref[...], (((0,), (0,)), ((), ())), preferred_element_type=F32)
                 for b_ref in b_refs]

        @pl.when(pl.program_id(0) == 0)
        def _():
            col = 0
            for part, w in zip(parts, widths):
                o_ref[:, col:col + w] = part
                col += w

        @pl.when(pl.program_id(0) != 0)
        def _():
            col = 0
            for part, w in zip(parts, widths):
                o_ref[:, col:col + w] += part
                col += w

    return pl.pallas_call(
        body, name=name, grid=(T // tt,),
        in_specs=[pl.BlockSpec((tt, K), lambda t: (t, 0))] + [pl.BlockSpec((tt, w), lambda t: (t, 0)) for w in widths],
        out_specs=pl.BlockSpec((K, sum(widths)), lambda t: (0, 0)),
        out_shape=jax.ShapeDtypeStruct((K, sum(widths)), F32),
        compiler_params=_params(("arbitrary",)),
    )(a, *bs)


def mm_nt_multi(pieces, w, tm, name):
    M = pieces[0][0].shape[0]
    N, K = w.shape

    def body(*refs):
        p_refs, w_ref, o_ref = refs[:-2], refs[-2], refs[-1]
        acc = None
        for p_ref, (arr, col) in zip(p_refs, pieces):
            part = lax.dot_general(p_ref[...], w_ref[:, col:col + arr.shape[1]], (((1,), (1,)), ((), ())),
                                   preferred_element_type=F32)
            acc = part if acc is None else acc + part
        o_ref[...] = acc

    return pl.pallas_call(
        body, name=name, grid=(M // tm,),
        in_specs=[pl.BlockSpec((tm, arr.shape[1]), lambda i: (i, 0)) for arr, _ in pieces]
        + [pl.BlockSpec((N, K), lambda i: (0, 0))],
        out_specs=pl.BlockSpec((tm, N), lambda i: (i, 0)),
        out_shape=jax.ShapeDtypeStruct((M, N), F32),
        compiler_params=_params(("arbitrary",)),
    )(*[arr for arr, _ in pieces], w)


def in_proj_bwd_rms(pieces, w, x, g, dres, tm, token):
    M, N = x.shape

    def body(*refs):
        n = len(pieces)
        p_refs, w_ref, x_ref, g_ref, dres_ref, _, dx_ref, dg_ref = refs[:n], *refs[n:]
        dh = None
        for p_ref, (arr, col) in zip(p_refs, pieces):
            part = lax.dot_general(p_ref[...], w_ref[:, col:col + arr.shape[1]], (((1,), (1,)), ((), ())),
                                   preferred_element_type=F32)
            dh = part if dh is None else dh + part
        xv = x_ref[...]
        r = lax.rsqrt(jnp.mean(xv * xv, axis=-1, keepdims=True) + RMS_EPS)
        xn = xv * r
        dxn = dh * g_ref[...]
        dx_ref[...] = r * (dxn - xn * jnp.mean(dxn * xn, axis=-1, keepdims=True)) + dres_ref[...]
        row = lax.broadcasted_iota(jnp.int32, (8, N), 0)
        upd = jnp.where(row == 0, jnp.sum(dh * xn, axis=0, keepdims=True), 0.0)

        @pl.when(pl.program_id(0) == 0)
        def _():
            dg_ref[...] = upd

        @pl.when(pl.program_id(0) != 0)
        def _():
            dg_ref[...] += upd

    row_spec = pl.BlockSpec((tm, N), lambda i: (i, 0))
    return pl.pallas_call(
        body, name="in_proj_bwd", grid=(M // tm,),
        in_specs=[pl.BlockSpec((tm, arr.shape[1]), lambda i: (i, 0)) for arr, _ in pieces]
        + [pl.BlockSpec(w.shape, lambda i: (0, 0)), row_spec, pl.BlockSpec((1, N), lambda i: (0, 0)), row_spec,
           pl.BlockSpec(token.shape, lambda i: (0, 0))],
        out_specs=[row_spec, pl.BlockSpec((8, N), lambda i: (0, 0))],
        out_shape=[jax.ShapeDtypeStruct((M, N), F32), jax.ShapeDtypeStruct((8, N), F32)],
        compiler_params=_params(("arbitrary",)),
    )(*[arr for arr, _ in pieces], w, x, g, dres, token)


def _log_sigmoid(z):
    return jnp.minimum(z, 0.0) - jnp.log(1.0 + jnp.exp(-jnp.abs(z)))


def _tri(n, lower):
    r = lax.broadcasted_iota(jnp.int32, (n, n), 0)
    c = lax.broadcasted_iota(jnp.int32, (n, n), 1)
    return jnp.where((r >= c) if lower else (r <= c), 1.0, 0.0).astype(F32)


def fox_gate(proj3, b_pad):
    B, S, _ = proj3.shape
    nblk = S // TK

    def body(f_ref, b_ref, o_ref):
        tri = _tri(TK, True)
        carry = jnp.zeros((1, LANES), F32)
        for n in range(nblk):
            z = f_ref[0, n * TK:(n + 1) * TK, :] + b_ref[...]
            logf = _log_sigmoid(z)
            cs = jnp.dot(tri, logf, preferred_element_type=F32, precision=lax.Precision.HIGHEST) + carry
            carry = cs[TK - 1:TK, :]
            o_ref[0, n * TK:(n + 1) * TK, :] = -cs

    return pl.pallas_call(
        body, name="fox_gate", grid=(B,),
        in_specs=[pl.BlockSpec((1, S, LANES), lambda b: (b, 0, P_FLOG // LANES)),
                  pl.BlockSpec((1, LANES), lambda b: (0, 0))],
        out_specs=pl.BlockSpec((1, S, LANES), lambda b: (b, 0, 0)),
        out_shape=jax.ShapeDtypeStruct((B, S, LANES), F32),
        compiler_params=_params(("arbitrary",)),
    )(proj3, b_pad)


def fox_gate_bwd(drow, dneg, proj3, b_pad):
    B, S, _ = proj3.shape
    nblk = S // TK

    def body(d_ref, r_ref, f_ref, b_ref, o_ref, db_ref):
        tri = _tri(TK, False)
        lane = lax.broadcasted_iota(jnp.int32, (TK, LANES), 1)
        carry = jnp.zeros((1, LANES), F32)
        dbsum = jnp.zeros((1, LANES), F32)
        for n in reversed(range(nblk)):
            dk_side = None
            for hp in range(FOX_HEADS // 2):
                two = jnp.where(lane < 2, r_ref[0, n * TK:(n + 1) * TK, hp * LANES:(hp + 1) * LANES], 0.0)
                two = pltpu.roll(two, 2 * hp, 1) if hp else two
                dk_side = two if dk_side is None else dk_side + two
            dc = jnp.where(lane < FOX_HEADS, d_ref[0, :, n * TK:(n + 1) * TK].T - dk_side, 0.0)
            rs = jnp.dot(tri, dc, preferred_element_type=F32, precision=lax.Precision.HIGHEST) + carry
            carry = rs[0:1, :]
            z = f_ref[0, n * TK:(n + 1) * TK, :] + b_ref[...]
            dz = rs * (1.0 / (1.0 + jnp.exp(z)))
            o_ref[0, n * TK:(n + 1) * TK, :] = dz.astype(BF16)
            dbsum = dbsum + jnp.sum(dz, axis=0, keepdims=True)
        row = lax.broadcasted_iota(jnp.int32, (8, LANES), 0)
        upd = jnp.where(row == 0, dbsum, 0.0)

        @pl.when(pl.program_id(0) == 0)
        def _():
            db_ref[...] = upd

        @pl.when(pl.program_id(0) != 0)
        def _():
            db_ref[...] += upd

    return pl.pallas_call(
        body, name="fox_gate_bwd", grid=(B,),
        in_specs=[pl.BlockSpec((1, LANES, S), lambda b: (b, 0, 0)),
                  pl.BlockSpec((1, S, FOX_W), lambda b: (b, 0, 0)),
                  pl.BlockSpec((1, S, LANES), lambda b: (b, 0, P_FLOG // LANES)),
                  pl.BlockSpec((1, LANES), lambda b: (0, 0))],
        out_specs=[pl.BlockSpec((1, S, LANES), lambda b: (b, 0, 0)), pl.BlockSpec((8, LANES), lambda b: (0, 0))],
        out_shape=[jax.ShapeDtypeStruct((B, S, LANES), BF16), jax.ShapeDtypeStruct((8, LANES), F32)],
        compiler_params=_params(("arbitrary",)),
    )(drow, dneg, proj3, b_pad)


def _mult_masks(S, kind):
    nd = S // TQ
    a = np.arange(TQ)[:, None]
    b = np.arange(TK)[None, :]
    out = np.zeros((nd, TQ, TK), np.float32)
    for d in range(nd):
        delta = d * TQ + a - b
        if kind == "causal":
            out[d] = delta >= 0
        else:
            m = np.zeros((TQ, TK), np.float32)
            for w, dil in DILATIONS:
                m += (delta >= 0) & (delta % dil == 0) & (delta <= w)
            out[d] = m
    return jnp.asarray(out)


def _rope_tables(S):
    half = ROPE_DIM // 2
    f32 = np.float32
    pos = np.arange(S, dtype=f32)
    inv_freq = f32(1.0) / np.power(f32(ROPE_THETA), np.arange(0, ROPE_DIM, 2, dtype=f32) / f32(ROPE_DIM)).astype(f32)
    ang = (pos[:, None] * inv_freq[None, :]).astype(f32).astype(np.float64)
    cos, sin = np.cos(ang).astype(f32), np.sin(ang).astype(f32)
    one = np.ones((S, HEAD_DIM - ROPE_DIM), f32)
    zero = np.zeros((S, HEAD_DIM - ROPE_DIM), f32)
    zh = np.zeros((S, half), f32)
    c = np.concatenate([cos, cos, one], axis=1)
    s1 = np.concatenate([-sin, zh, zero], axis=1)
    s2 = np.concatenate([zh, sin, zero], axis=1)
    return tuple(jnp.asarray(np.concatenate([t, t], axis=1)) for t in (c, s1, s2))


def _rope(t, c, s1, s2):
    return t * c + pltpu.roll(t, LANES - half_rope(), 1) * s1 + pltpu.roll(t, half_rope(), 1) * s2


def half_rope():
    return ROPE_DIM // 2


def _rope_bwd(d, c, s1, s2):
    return d * c + pltpu.roll(d * s1, half_rope(), 1) + pltpu.roll(d * s2, LANES - half_rope(), 1)


def _scale_parts(scale):
    m, _ = math.frexp(scale)
    return (scale, None) if m == 0.5 else (None, scale)


def attn_fwd(kind, src, S, *, negc=None, mask=None, rope=None, kv=None):
    B = src.shape[0]
    pair = kind != "mem"
    col0 = {"fox": P_FOX, "dil": P_DIL, "mem": P_MQ}[kind]
    n_blocks = FOX_HEADS // 2 if pair else MEM_HEADS
    e_dim = HEAD_DIM if pair else MEM_HEAD_DIM
    q_fold, s_scale = _scale_parts(1.0 / math.sqrt(e_dim))
    Sk = S if pair else MEM_LEN
    nh = 2 if pair else 1
    has_bias = negc is not None
    has_rope = rope is not None
    nq = S // TQ

    def body(*refs):
        refs = list(refs)
        if pair:
            qkv_ref = refs.pop(0)
        else:
            q_ref, k_ref, v_ref = refs.pop(0), refs.pop(0), refs.pop(0)
        negc_ref = refs.pop(0) if has_bias else None
        mask_ref = refs.pop(0) if pair else None
        rope_refs = [refs.pop(0) for _ in range(3)] if has_rope else None
        o_ref, lse_ref, qs, ks, vs = refs
        lane = lax.broadcasted_iota(jnp.int32, (1, LANES), 1)

        def prep_q(n, _):
            r0 = pl.multiple_of(n * TQ, TQ)
            rows = pl.ds(r0, TQ)
            q = qkv_ref[0, rows, 0:LANES] if pair else q_ref[0, rows, :]
            if has_rope:
                q = _rope(q, *[t[rows, :] for t in rope_refs])
            if q_fold is not None:
                q = q * q_fold
            qs[rows, :] = q.astype(BF16)
            return 0

        def prep_kv(n, _):
            r0 = pl.multiple_of(n * TK, TK)
            rows = pl.ds(r0, TK)
            k = qkv_ref[0, rows, LANES:2 * LANES] if pair else k_ref[0, rows, :]
            v = qkv_ref[0, rows, 2 * LANES:3 * LANES] if pair else v_ref[0, rows, :]
            if has_rope:
                k = _rope(k, *[t[rows, :] for t in rope_refs])
            ks[rows, :] = k.astype(BF16)
            vs[rows, :] = v.astype(BF16)
            return 0

        lax.fori_loop(0, nq, prep_q, 0)
        lax.fori_loop(0, Sk // TK, prep_kv, 0)

        def q_loop(i, _):
            r0 = pl.multiple_of(i * TQ, TQ)
            q = qs[pl.ds(r0, TQ), :]
            res = []
            for hh in range(nh):
                hmask = (lane >= HEAD_DIM * hh) & (lane < HEAD_DIM * (hh + 1))
                qh = jnp.where(hmask, q, jnp.zeros_like(q)) if pair else q

                def kv_loop(j, carry, qh=qh, hh=hh):
                    m, l, acc = carry
                    c0 = pl.multiple_of(j * TK, TK)
                    k = ks[pl.ds(c0, TK), :]
                    v = vs[pl.ds(c0, TK), :]
                    s = lax.dot_general(qh, k, (((1,), (1,)), ((), ())), preferred_element_type=F32)
                    if s_scale is not None:
                        s = s * s_scale
                    if has_bias:
                        s = s + negc_ref[0, 0, pl.ds(hh, 1), pl.ds(c0, TK)]
                    if pair:
                        mult = mask_ref[i - j]
                        s = jnp.where(mult > 0.0, s, NEG_INF)
                    m_new = jnp.maximum(m, jnp.max(s, axis=1, keepdims=True))
                    p = jnp.exp(s - m_new)
                    if pair:
                        p = p * mult
                    alpha = jnp.exp(m - m_new)
                    l = alpha * l + jnp.sum(p, axis=1, keepdims=True)
                    acc = acc * alpha + jnp.dot(p.astype(BF16), v, preferred_element_type=F32)
                    return m_new, l, acc

                init = (jnp.full((TQ, 1), NEG_INF, F32), jnp.zeros((TQ, 1), F32), jnp.zeros((TQ, LANES), F32))
                m, l, acc = lax.fori_loop(0, (i + 1) if pair else Sk // TK, kv_loop, init)
                res.append((acc / l, m + jnp.log(l)))
            if pair:
                o = jnp.where(lane < HEAD_DIM, res[0][0], res[1][0])
                lse = jnp.where(lane < HEAD_DIM, res[0][1], res[1][1])
            else:
                o = res[0][0]
                lse = jnp.broadcast_to(res[0][1], (TQ, LANES))
            o_ref[0, pl.ds(r0, TQ), :] = o
            lse_ref[0, pl.ds(r0, TQ), :] = lse
            return 0

        lax.fori_loop(0, nq, q_loop, 0)

    ins, in_specs = [], []
    if pair:
        ins.append(src)
        in_specs.append(pl.BlockSpec((1, S, PAIR_W), lambda b, h: (b, 0, col0 // PAIR_W + h)))
    else:
        ins += [src, kv, kv]
        in_specs += [pl.BlockSpec((1, S, LANES), lambda b, h: (b, 0, col0 // LANES + h)),
                     pl.BlockSpec((1, MEM_LEN, LANES), lambda b, h: (b, 0, h)),
                     pl.BlockSpec((1, MEM_LEN, LANES), lambda b, h: (b, 0, MEM_HEADS + h))]
    if has_bias:
        ins.append(negc)
        in_specs.append(pl.BlockSpec((1, 1, 2, S), lambda b, h: (b, h, 0, 0)))
    if pair:
        ins.append(mask)
        in_specs.append(pl.BlockSpec(mask.shape, lambda b, h: (0, 0, 0)))
    if has_rope:
        ins += list(rope)
        in_specs += [pl.BlockSpec((S, LANES), lambda b, h: (0, 0))] * 3
    W = n_blocks * LANES
    out_spec = pl.BlockSpec((1, S, LANES), lambda b, h: (b, 0, h))
    return pl.pallas_call(
        body, name=kind + "_attn_fwd", grid=(B, n_blocks),
        in_specs=in_specs, out_specs=[out_spec, out_spec],
        out_shape=[jax.ShapeDtypeStruct((B, S, W), F32)] * 2,
        scratch_shapes=[pltpu.VMEM((S, LANES), BF16), pltpu.VMEM((Sk, LANES), BF16), pltpu.VMEM((Sk, LANES), BF16)],
        compiler_params=_params(("arbitrary", "arbitrary")),
    )(*ins)


def attn_bwd(kind, src, do, o, lse, S, *, negc=None, mask=None, rope=None, kv=None):
    B = src.shape[0]
    pair = kind != "mem"
    col0 = {"fox": P_FOX, "dil": P_DIL, "mem": P_MQ}[kind]
    n_blocks = FOX_HEADS // 2 if pair else MEM_HEADS
    e_dim = HEAD_DIM if pair else MEM_HEAD_DIM
    scale = 1.0 / math.sqrt(e_dim)
    q_fold, s_scale = _scale_parts(scale)
    Sk = S if pair else MEM_LEN
    nh = 2 if pair else 1
    has_bias = negc is not None
    has_rope = rope is not None
    nq = S // TQ
    nk = Sk // TK

    def body(*refs):
        refs = list(refs)
        if pair:
            qkv_ref = refs.pop(0)
        else:
            q_ref, k_ref, v_ref = refs.pop(0), refs.pop(0), refs.pop(0)
        do_ref, o_ref, lse_ref = refs.pop(0), refs.pop(0), refs.pop(0)
        negc_ref = refs.pop(0) if has_bias else None
        mask_ref = refs.pop(0) if pair else None
        rope_refs = [refs.pop(0) for _ in range(3)] if has_rope else None
        if pair:
            dqkv_ref = refs.pop(0)
            dnegc_ref = refs.pop(0) if has_bias else None
            drow_ref = refs.pop(0) if has_bias else None
        else:
            dq_ref, dk_ref, dv_ref = refs.pop(0), refs.pop(0), refs.pop(0)
        qs, ks, vs, dos, delta_s, dq_acc = refs[:6]
        drow_acc = refs[6] if has_bias else None
        lane = lax.broadcasted_iota(jnp.int32, (1, LANES), 1)

        def prep_q(n, _):
            r0 = pl.multiple_of(n * TQ, TQ)
            rows = pl.ds(r0, TQ)
            q = qkv_ref[0, rows, 0:LANES] if pair else q_ref[0, rows, :]
            if has_rope:
                q = _rope(q, *[t[rows, :] for t in rope_refs])
            if q_fold is not None:
                q = q * q_fold
            qs[rows, :] = q.astype(BF16)
            dov = do_ref[0, rows, :]
            dob = dov.astype(BF16)
            dos[rows, :] = dob
            prod = dob.astype(F32) * o_ref[0, rows, :]
            if pair:
                d0 = jnp.sum(jnp.where(lane < HEAD_DIM, prod, 0.0), axis=1, keepdims=True)
                d1 = jnp.sum(jnp.where(lane < HEAD_DIM, 0.0, prod), axis=1, keepdims=True)
                delta_s[rows, :] = jnp.where(lane < HEAD_DIM, d0, d1)
            else:
                delta_s[rows, :] = jnp.broadcast_to(jnp.sum(prod, axis=1, keepdims=True), (TQ, LANES))
            dq_acc[rows, :] = jnp.zeros((TQ, LANES), F32)
            if has_bias:
                drow_acc[rows, :] = jnp.zeros((TQ, LANES), F32)
            return 0

        def prep_kv(n, _):
            r0 = pl.multiple_of(n * TK, TK)
            rows = pl.ds(r0, TK)
            k = qkv_ref[0, rows, LANES:2 * LANES] if pair else k_ref[0, rows, :]
            v = qkv_ref[0, rows, 2 * LANES:3 * LANES] if pair else v_ref[0, rows, :]
            if has_rope:
                k = _rope(k, *[t[rows, :] for t in rope_refs])
            ks[rows, :] = k.astype(BF16)
            vs[rows, :] = v.astype(BF16)
            return 0

        lax.fori_loop(0, nq, prep_q, 0)
        lax.fori_loop(0, nk, prep_kv, 0)

        def kv_loop(j, _):
            c0 = pl.multiple_of(j * TK, TK)
            kt = ks[pl.ds(c0, TK), :]
            vt = vs[pl.ds(c0, TK), :]
            res = []
            for hh in range(nh):
                hmask = (lane >= HEAD_DIM * hh) & (lane < HEAD_DIM * (hh + 1))
                kh = jnp.where(hmask, kt, jnp.zeros_like(kt)) if pair else kt
                vh = jnp.where(hmask, vt, jnp.zeros_like(vt)) if pair else vt

                def q_loop(i, carry, kh=kh, vh=vh, hh=hh, hmask=hmask):
                    dk, dv, dneg = carry
                    r0 = pl.multiple_of(i * TQ, TQ)
                    rows = pl.ds(r0, TQ)
                    q = qs[rows, :]
                    dot = dos[rows, :]
                    lse_i = lse_ref[0, rows, hh * HEAD_DIM:hh * HEAD_DIM + 1]
                    delta_i = delta_s[rows, hh * HEAD_DIM:hh * HEAD_DIM + 1]
                    s = lax.dot_general(q, kh, (((1,), (1,)), ((), ())), preferred_element_type=F32)
                    if s_scale is not None:
                        s = s * s_scale
                    if has_bias:
                        s = s + negc_ref[0, 0, pl.ds(hh, 1), pl.ds(c0, TK)]
                    if pair:
                        mult = mask_ref[i - j]
                        s = jnp.where(mult > 0.0, s, NEG_INF)
                    p = jnp.exp(s - lse_i)
                    if pair:
                        p = p * mult
                    dv = dv + lax.dot_general(p.astype(BF16), dot, (((0,), (0,)), ((), ())),
                                              preferred_element_type=F32)
                    dp = lax.dot_general(dot, vh, (((1,), (1,)), ((), ())), preferred_element_type=F32)
                    ds = p * (dp - delta_i)
                    if has_bias:
                        dneg = dneg + jnp.sum(ds, axis=0, keepdims=True)
                        drow_acc[rows, :] += jnp.where(hmask, jnp.sum(ds, axis=1, keepdims=True), 0.0)
                    if s_scale is not None:
                        ds = ds * s_scale
                    dsb = ds.astype(BF16)
                    dk = dk + lax.dot_general(dsb, q, (((0,), (0,)), ((), ())), preferred_element_type=F32)
                    dq = jnp.dot(dsb, kh, preferred_element_type=F32)
                    dq_acc[rows, :] += dq
                    return dk, dv, dneg

                init = (jnp.zeros((TK, LANES), F32), jnp.zeros((TK, LANES), F32), jnp.zeros((1, TK), F32))
                dk, dv, dneg = lax.fori_loop(j if pair else 0, nq, q_loop, init)
                if has_bias:
                    dnegc_ref[0, 0, pl.ds(hh, 1), pl.ds(c0, TK)] = dneg
                res.append((dk, dv))
            if pair:
                dk = jnp.where(lane < HEAD_DIM, res[0][0], res[1][0])
                dv = jnp.where(lane < HEAD_DIM, res[0][1], res[1][1])
                if has_rope:
                    dk = _rope_bwd(dk, *[t[pl.ds(c0, TK), :] for t in rope_refs])
                dqkv_ref[0, pl.ds(c0, TK), LANES:2 * LANES] = dk.astype(BF16)
                dqkv_ref[0, pl.ds(c0, TK), 2 * LANES:3 * LANES] = dv.astype(BF16)
            else:
                dk_ref[0, pl.ds(c0, TK), :] = res[0][0].astype(BF16)
                dv_ref[0, pl.ds(c0, TK), :] = res[0][1].astype(BF16)
            return 0

        lax.fori_loop(0, nk, kv_loop, 0)

        def fin_q(n, _):
            r0 = pl.multiple_of(n * TQ, TQ)
            rows = pl.ds(r0, TQ)
            dq = dq_acc[rows, :]
            if q_fold is not None:
                dq = dq * q_fold
            if has_rope:
                dq = _rope_bwd(dq, *[t[rows, :] for t in rope_refs])
            if pair:
                dqkv_ref[0, rows, 0:LANES] = dq.astype(BF16)
            else:
                dq_ref[0, rows, :] = dq.astype(BF16)
            if has_bias:
                drow_ref[0, rows, :] = drow_acc[rows, :]
            return 0

        lax.fori_loop(0, nq, fin_q, 0)

    ins, in_specs = [], []
    if pair:
        ins.append(src)
        in_specs.append(pl.BlockSpec((1, S, PAIR_W), lambda b, h: (b, 0, col0 // PAIR_W + h)))
    else:
        ins += [src, kv, kv]
        in_specs += [pl.BlockSpec((1, S, LANES), lambda b, h: (b, 0, col0 // LANES + h)),
                     pl.BlockSpec((1, MEM_LEN, LANES), lambda b, h: (b, 0, h)),
                     pl.BlockSpec((1, MEM_LEN, LANES), lambda b, h: (b, 0, MEM_HEADS + h))]
    row_spec = pl.BlockSpec((1, S, LANES), lambda b, h: (b, 0, h))
    ins += [do, o, lse]
    in_specs += [row_spec] * 3
    if has_bias:
        ins.append(negc)
        in_specs.append(pl.BlockSpec((1, 1, 2, S), lambda b, h: (b, h, 0, 0)))
    if pair:
        ins.append(mask)
        in_specs.append(pl.BlockSpec(mask.shape, lambda b, h: (0, 0, 0)))
    if has_rope:
        ins += list(rope)
        in_specs += [pl.BlockSpec((S, LANES), lambda b, h: (0, 0))] * 3
    W = n_blocks * LANES
    if pair:
        out_specs = [pl.BlockSpec((1, S, PAIR_W), lambda b, h: (b, 0, h))]
        out_shape = [jax.ShapeDtypeStruct((B, S, 3 * W), BF16)]
        if has_bias:
            out_specs.append(pl.BlockSpec((1, 1, 2, S), lambda b, h: (b, h, 0, 0)))
            out_shape.append(jax.ShapeDtypeStruct((B, LANES // 2, 2, S), F32))
            out_specs.append(row_spec)
            out_shape.append(jax.ShapeDtypeStruct((B, S, W), F32))
    else:
        kv_spec = pl.BlockSpec((1, MEM_LEN, LANES), lambda b, h: (b, 0, h))
        out_specs = [row_spec, kv_spec, kv_spec]
        out_shape = [jax.ShapeDtypeStruct((B, S, W), BF16)] + [jax.ShapeDtypeStruct((B, MEM_LEN, W), BF16)] * 2
    return pl.pallas_call(
        body, name=kind + "_attn_bwd", grid=(B, n_blocks),
        in_specs=in_specs, out_specs=out_specs, out_shape=out_shape,
        scratch_shapes=[pltpu.VMEM((S, LANES), BF16), pltpu.VMEM((Sk, LANES), BF16), pltpu.VMEM((Sk, LANES), BF16),
                        pltpu.VMEM((S, LANES), BF16), pltpu.VMEM((S, LANES), F32), pltpu.VMEM((S, LANES), F32)]
        + ([pltpu.VMEM((S, LANES), F32)] if has_bias else []),
        compiler_params=_params(("arbitrary", "arbitrary")),
    )(*ins)


def _log_masks(S, kind):
    nd = 1 if kind == "causal" else S // TQ
    a = np.arange(TQ)[:, None]
    b = np.arange(TK)[None, :]
    out = np.zeros((nd, TQ, TK), np.float32)
    for d in range(nd):
        delta = d * TQ + a - b
        if kind == "causal":
            m = (delta >= 0).astype(np.float64)
        else:
            m = sum(((delta >= 0) & (delta % dil == 0) & (delta <= w)).astype(np.float64) for w, dil in DILATIONS)
        out[d] = np.where(m > 0, np.log(np.maximum(m, 1.0)), NEG_INF)
    return jnp.asarray(out)


def _attn_setup(kind):
    pair = kind != "mem"
    e_dim = HEAD_DIM if pair else MEM_HEAD_DIM
    q_fold, s_scale = _scale_parts(1.0 / math.sqrt(e_dim))
    return dict(pair=pair, col0={"fox": P_FOX, "dil": P_DIL, "mem": P_MQ}[kind],
                n_blocks=FOX_HEADS // 2 if pair else MEM_HEADS, q_fold=q_fold, s_scale=s_scale,
                nh=2 if pair else 1)


def _attn_inputs(kind, src, S, negc, mask, rope, kv, extra):
    cfg = _attn_setup(kind)
    col0 = cfg["col0"]
    ins, in_specs = [], []
    if cfg["pair"]:
        ins.append(src)
        in_specs.append(pl.BlockSpec((1, S, PAIR_W), lambda b, h: (b, 0, col0 // PAIR_W + h)))
    else:
        ins += [src, kv, kv]
        in_specs += [pl.BlockSpec((1, S, LANES), lambda b, h: (b, 0, col0 // LANES + h)),
                     pl.BlockSpec((1, MEM_LEN, LANES), lambda b, h: (b, 0, h)),
                     pl.BlockSpec((1, MEM_LEN, LANES), lambda b, h: (b, 0, MEM_HEADS + h))]
    ins += list(extra)
    in_specs += [pl.BlockSpec((1, S, LANES), lambda b, h: (b, 0, h))] * len(extra)
    if negc is not None:
        ins.append(negc)
        in_specs.append(pl.BlockSpec((1, 1, 2, S), lambda b, h: (b, h, 0, 0)))
    if mask is not None:
        ins.append(mask)
        in_specs.append(pl.BlockSpec(mask.shape, lambda b, h: (0, 0, 0)))
    if rope is not None:
        ins += list(rope)
        in_specs += [pl.BlockSpec((S, LANES), lambda b, h: (0, 0))] * 3
    return ins, in_specs


def _prep_rows(cfg, rope_refs, lane, load_q, load_kv, qs2, ks, vs, S, Sk):
    nh = cfg["nh"]
    R = nh * TQ

    def prep_q(n, _):
        rows = pl.ds(pl.multiple_of(n * TQ, TQ), TQ)
        q = load_q(rows)
        if rope_refs is not None:
            q = _rope(q, *[t[rows, :] for t in rope_refs])
        if cfg["q_fold"] is not None:
            q = q * cfg["q_fold"]
        _store_stacked(cfg, lane, qs2, n, q.astype(BF16))
        return 0

    def prep_kv(n, _):
        rows = pl.ds(pl.multiple_of(n * TK, TK), TK)
        k, v = load_kv(rows)
        if rope_refs is not None:
            k = _rope(k, *[t[rows, :] for t in rope_refs])
        ks[rows, :] = k.astype(BF16)
        vs[rows, :] = v.astype(BF16)
        return 0

    lax.fori_loop(0, S // TQ, prep_q, 0)
    lax.fori_loop(0, Sk // TK, prep_kv, 0)


def _store_stacked(cfg, lane, dst, n, val):
    nh = cfg["nh"]
    R = nh * TQ
    if nh == 1:
        dst[pl.ds(pl.multiple_of(n * R, R), TQ), :] = val
        return
    for hh in range(nh):
        hmask = (lane >= HEAD_DIM * hh) & (lane < HEAD_DIM * (hh + 1))
        dst[pl.ds(pl.multiple_of(n * R + hh * TQ, TQ), TQ), :] = jnp.where(hmask, val, jnp.zeros_like(val))


def _cat(parts, axis):
    return parts[0] if len(parts) == 1 else jnp.concatenate(parts, axis=axis)


def attn_fwd2(kind, src, S, *, negc=None, mask=None, rope=None, kv=None):
    B = src.shape[0]
    cfg = _attn_setup(kind)
    pair, nh, s_scale = cfg["pair"], cfg["nh"], cfg["s_scale"]
    Sk = S if pair else MEM_LEN
    has_bias, has_rope = negc is not None, rope is not None
    R = nh * TQ

    def body(*refs):
        refs = list(refs)
        if pair:
            qkv_ref = refs.pop(0)
        else:
            q_ref, k_ref, v_ref = refs.pop(0), refs.pop(0), refs.pop(0)
        negc_ref = refs.pop(0) if has_bias else None
        mask_ref = refs.pop(0) if mask is not None else None
        rope_refs = [refs.pop(0) for _ in range(3)] if has_rope else None
        o_ref, lse_ref, qs2, ks, vs = refs
        lane = lax.broadcasted_iota(jnp.int32, (1, LANES), 1)

        if pair:
            load_q = lambda rows: qkv_ref[0, rows, 0:LANES]
            load_kv = lambda rows: (qkv_ref[0, rows, LANES:2 * LANES], qkv_ref[0, rows, 2 * LANES:3 * LANES])
        else:
            load_q = lambda rows: q_ref[0, rows, :]
            load_kv = lambda rows: (k_ref[0, rows, :], v_ref[0, rows, :])
        _prep_rows(cfg, rope_refs, lane, load_q, load_kv, qs2, ks, vs, S, Sk)

        def q_loop(i, _):
            q2 = qs2[pl.ds(pl.multiple_of(i * R, R), R), :]

            def step(j, carry, midx):
                ms, ls, acc = carry
                c0 = pl.multiple_of(j * TK, TK)
                k = ks[pl.ds(c0, TK), :]
                v = vs[pl.ds(c0, TK), :]
                s2 = lax.dot_general(q2, k, (((1,), (1,)), ((), ())), preferred_element_type=F32)
                if s_scale is not None:
                    s2 = s2 * s_scale
                new_m, new_l, ps, alphas = [], [], [], []
                for hh in range(nh):
                    s = s2[hh * TQ:(hh + 1) * TQ]
                    if has_bias:
                        s = s + negc_ref[0, 0, pl.ds(hh, 1), pl.ds(c0, TK)]
                    if midx is not None:
                        s = s + mask_ref[midx]
                    m_new = jnp.maximum(ms[hh], jnp.max(s, axis=1, keepdims=True))
                    p = jnp.exp(s - m_new)
                    alpha = jnp.exp(ms[hh] - m_new)
                    new_l.append(alpha * ls[hh] + jnp.sum(p, axis=1, keepdims=True))
                    new_m.append(m_new)
                    ps.append(p.astype(BF16))
                    alphas.append(alpha)
                acc = acc * _cat(alphas, 0) + jnp.dot(_cat(ps, 0), v, preferred_element_type=F32)
                return tuple(new_m), tuple(new_l), acc

            init = (tuple(jnp.full((TQ, 1), NEG_INF, F32) for _ in range(nh)),
                    tuple(jnp.zeros((TQ, 1), F32) for _ in range(nh)), jnp.zeros((R, LANES), F32))
            if kind == "fox":
                carry = lax.fori_loop(0, i, lambda j, c: step(j, c, None), init)
                carry = step(i, carry, 0)
            elif kind == "dil":
                carry = lax.fori_loop(0, i + 1, lambda j, c: step(j, c, i - j), init)
            else:
                carry = lax.fori_loop(0, Sk // TK, lambda j, c: step(j, c, None), init)
            ms, ls, acc = carry
            outs = [acc[hh * TQ:(hh + 1) * TQ] / ls[hh] for hh in range(nh)]
            lses = [ms[hh] + jnp.log(ls[hh]) for hh in range(nh)]
            rows = pl.ds(pl.multiple_of(i * TQ, TQ), TQ)
            if pair:
                o_ref[0, rows, :] = jnp.where(lane < HEAD_DIM, outs[0], outs[1])
                lse_ref[0, rows, :] = jnp.where(lane < HEAD_DIM, lses[0], lses[1])
            else:
                o_ref[0, rows, :] = outs[0]
                lse_ref[0, rows, :] = jnp.broadcast_to(lses[0], (TQ, LANES))
            return 0

        lax.fori_loop(0, S // TQ, q_loop, 0)

    ins, in_specs = _attn_inputs(kind, src, S, negc, mask, rope, kv, ())
    W = cfg["n_blocks"] * LANES
    out_spec = pl.BlockSpec((1, S, LANES), lambda b, h: (b, 0, h))
    return pl.pallas_call(
        body, name=kind + "_attn_fwd", grid=(B, cfg["n_blocks"]),
        in_specs=in_specs, out_specs=[out_spec, out_spec],
        out_shape=[jax.ShapeDtypeStruct((B, S, W), F32)] * 2,
        scratch_shapes=[pltpu.VMEM((nh * S, LANES), BF16), pltpu.VMEM((Sk, LANES), BF16),
                        pltpu.VMEM((Sk, LANES), BF16)],
        compiler_params=_params(("arbitrary", "arbitrary")),
    )(*ins)


def attn_bwd2(kind, src, do, o, lse, S, *, negc=None, mask=None, rope=None, kv=None):
    B = src.shape[0]
    cfg = _attn_setup(kind)
    pair, nh, s_scale, q_fold = cfg["pair"], cfg["nh"], cfg["s_scale"], cfg["q_fold"]
    Sk = S if pair else MEM_LEN
    has_bias, has_rope = negc is not None, rope is not None
    R = nh * TQ
    nq, nk = S // TQ, Sk // TK

    def body(*refs):
        refs = list(refs)
        if pair:
            qkv_ref = refs.pop(0)
        else:
            q_ref, k_ref, v_ref = refs.pop(0), refs.pop(0), refs.pop(0)
        do_ref, o_ref, lse_ref = refs.pop(0), refs.pop(0), refs.pop(0)
        negc_ref = refs.pop(0) if has_bias else None
        mask_ref = refs.pop(0) if mask is not None else None
        rope_refs = [refs.pop(0) for _ in range(3)] if has_rope else None
        if pair:
            dqkv_ref = refs.pop(0)
            dnegc_ref = refs.pop(0) if has_bias else None
            drow_ref = refs.pop(0) if has_bias else None
        else:
            dq_ref, dk_ref, dv_ref = refs.pop(0), refs.pop(0), refs.pop(0)
        qs2, ks, vs, dos2, lse_s, delta_s, dk_acc, dv_acc = refs[:8]
        dneg_acc = refs[8] if has_bias else None
        lane = lax.broadcasted_iota(jnp.int32, (1, LANES), 1)

        if pair:
            load_q = lambda rows: qkv_ref[0, rows, 0:LANES]
            load_kv = lambda rows: (qkv_ref[0, rows, LANES:2 * LANES], qkv_ref[0, rows, 2 * LANES:3 * LANES])
        else:
            load_q = lambda rows: q_ref[0, rows, :]
            load_kv = lambda rows: (k_ref[0, rows, :], v_ref[0, rows, :])
        _prep_rows(cfg, rope_refs, lane, load_q, load_kv, qs2, ks, vs, S, Sk)

        def prep_do(n, _):
            rows = pl.ds(pl.multiple_of(n * TQ, TQ), TQ)
            dob = do_ref[0, rows, :].astype(BF16)
            _store_stacked(cfg, lane, dos2, n, dob)
            prod = dob.astype(F32) * o_ref[0, rows, :]
            lse_blk = lse_ref[0, rows, :]
            for hh in range(nh):
                dst = pl.ds(pl.multiple_of(n * R + hh * TQ, TQ), TQ)
                if pair:
                    hmask = (lane >= HEAD_DIM * hh) & (lane < HEAD_DIM * (hh + 1))
                    d = jnp.sum(jnp.where(hmask, prod, 0.0), axis=1, keepdims=True)
                    lse_s[dst, :] = jnp.broadcast_to(lse_blk[:, hh * HEAD_DIM:hh * HEAD_DIM + 1], (TQ, LANES))
                else:
                    d = jnp.sum(prod, axis=1, keepdims=True)
                    lse_s[dst, :] = lse_blk
                delta_s[dst, :] = jnp.broadcast_to(d, (TQ, LANES))
            return 0

        def zero_kv(n, _):
            rows = pl.ds(pl.multiple_of(n * TK, TK), TK)
            dk_acc[rows, :] = jnp.zeros((TK, LANES), F32)
            dv_acc[rows, :] = jnp.zeros((TK, LANES), F32)
            return 0

        lax.fori_loop(0, nq, prep_do, 0)
        lax.fori_loop(0, nk, zero_kv, 0)
        if has_bias:
            dneg_acc[...] = jnp.zeros(dneg_acc.shape, F32)

        def q_loop(i, _):
            rows2 = pl.ds(pl.multiple_of(i * R, R), R)
            q2 = qs2[rows2, :]
            do2 = dos2[rows2, :]
            lse2 = lse_s[rows2, :]
            delta2 = delta_s[rows2, :]
            wide = lambda t: jnp.concatenate([t] * (TK // LANES), axis=1)

            def step(j, carry, midx):
                dq2, drow = carry
                c0 = pl.multiple_of(j * TK, TK)
                kcols = pl.ds(c0, TK)
                k = ks[kcols, :]
                v = vs[kcols, :]
                s2 = lax.dot_general(q2, k, (((1,), (1,)), ((), ())), preferred_element_type=F32)
                if s_scale is not None:
                    s2 = s2 * s_scale
                if has_bias or midx is not None:
                    halves = []
                    for hh in range(nh):
                        s = s2[hh * TQ:(hh + 1) * TQ]
                        if has_bias:
                            s = s + negc_ref[0, 0, pl.ds(hh, 1), kcols]
                        if midx is not None:
                            s = s + mask_ref[midx]
                        halves.append(s)
                    s2 = _cat(halves, 0)
                p2 = jnp.exp(s2 - wide(lse2))
                dp2 = lax.dot_general(do2, v, (((1,), (1,)), ((), ())), preferred_element_type=F32)
                ds2 = p2 * (dp2 - wide(delta2))
                if has_bias:
                    drow = drow + jnp.sum(ds2, axis=1, keepdims=True)
                    for hh in range(nh):
                        dneg_acc[pl.ds(hh, 1), kcols] += jnp.sum(ds2[hh * TQ:(hh + 1) * TQ], axis=0, keepdims=True)
                if s_scale is not None:
                    ds2 = ds2 * s_scale
                dsb = ds2.astype(BF16)
                dv_acc[kcols, :] += lax.dot_general(p2.astype(BF16), do2, (((0,), (0,)), ((), ())),
                                                    preferred_element_type=F32)
                dk_acc[kcols, :] += lax.dot_general(dsb, q2, (((0,), (0,)), ((), ())), preferred_element_type=F32)
                dq2 = dq2 + jnp.dot(dsb, k, preferred_element_type=F32)
                return dq2, drow

            init = (jnp.zeros((R, LANES), F32), jnp.zeros((R, 1), F32))
            if kind == "fox":
                carry = lax.fori_loop(0, i, lambda j, c: step(j, c, None), init)
                carry = step(i, carry, 0)
            elif kind == "dil":
                carry = lax.fori_loop(0, i + 1, lambda j, c: step(j, c, i - j), init)
            else:
                carry = lax.fori_loop(0, nk, lambda j, c: step(j, c, None), init)
            dq2, drow = carry
            rows = pl.ds(pl.multiple_of(i * TQ, TQ), TQ)
            dq = jnp.where(lane < HEAD_DIM, dq2[0:TQ], dq2[TQ:2 * TQ]) if pair else dq2
            if q_fold is not None:
                dq = dq * q_fold
            if has_rope:
                dq = _rope_bwd(dq, *[t[rows, :] for t in rope_refs])
            if pair:
                dqkv_ref[0, rows, 0:LANES] = dq.astype(BF16)
            else:
                dq_ref[0, rows, :] = dq.astype(BF16)
            if has_bias:
                drow_ref[0, rows, :] = jnp.where(lane < HEAD_DIM, drow[0:TQ], drow[TQ:2 * TQ])
            return 0

        lax.fori_loop(0, nq, q_loop, 0)

        def fin_kv(n, _):
            rows = pl.ds(pl.multiple_of(n * TK, TK), TK)
            dk = dk_acc[rows, :]
            if has_rope:
                dk = _rope_bwd(dk, *[t[rows, :] for t in rope_refs])
            if pair:
                dqkv_ref[0, rows, LANES:2 * LANES] = dk.astype(BF16)
                dqkv_ref[0, rows, 2 * LANES:3 * LANES] = dv_acc[rows, :].astype(BF16)
            else:
                dk_ref[0, rows, :] = dk.astype(BF16)
                dv_ref[0, rows, :] = dv_acc[rows, :].astype(BF16)
            return 0

        lax.fori_loop(0, nk, fin_kv, 0)
        if has_bias:
            dnegc_ref[0, 0] = dneg_acc[...]

    ins, in_specs = _attn_inputs(kind, src, S, negc, mask, rope, kv, (do, o, lse))
    W = cfg["n_blocks"] * LANES
    row_spec = pl.BlockSpec((1, S, LANES), lambda b, h: (b, 0, h))
    if pair:
        out_specs = [pl.BlockSpec((1, S, PAIR_W), lambda b, h: (b, 0, h))]
        out_shape = [jax.ShapeDtypeStruct((B, S, 3 * W), BF16)]
        if has_bias:
            out_specs += [pl.BlockSpec((1, 1, 2, S), lambda b, h: (b, h, 0, 0)), row_spec]
            out_shape += [jax.ShapeDtypeStruct((B, LANES // 2, 2, S), F32), jax.ShapeDtypeStruct((B, S, W), F32)]
    else:
        kv_spec = pl.BlockSpec((1, MEM_LEN, LANES), lambda b, h: (b, 0, h))
        out_specs = [row_spec, kv_spec, kv_spec]
        out_shape = [jax.ShapeDtypeStruct((B, S, W), BF16)] + [jax.ShapeDtypeStruct((B, MEM_LEN, W), BF16)] * 2
    scratch = [pltpu.VMEM((nh * S, LANES), BF16), pltpu.VMEM((Sk, LANES), BF16), pltpu.VMEM((Sk, LANES), BF16),
               pltpu.VMEM((nh * S, LANES), BF16), pltpu.VMEM((nh * S, LANES), F32), pltpu.VMEM((nh * S, LANES), F32),
               pltpu.VMEM((Sk, LANES), F32), pltpu.VMEM((Sk, LANES), F32)]
    if has_bias:
        scratch.append(pltpu.VMEM((2, S), F32))
    return pl.pallas_call(
        body, name=kind + "_attn_bwd", grid=(B, cfg["n_blocks"]),
        in_specs=in_specs, out_specs=out_specs, out_shape=out_shape, scratch_shapes=scratch,
        compiler_params=_params(("arbitrary", "arbitrary")),
    )(*ins)


def _log_masks_t(S, kind):
    return jnp.swapaxes(_log_masks(S, kind), 1, 2)


def _head_rows(hh, pair):
    row = lax.broadcasted_iota(jnp.int32, (LANES, 1), 0)
    if not pair:
        return row >= 0
    return (row >= HEAD_DIM * hh) & (row < HEAD_DIM * (hh + 1))


def _attn_t_inputs(kind, src, S, negc_cols, mask, rope, kv):
    cfg = _attn_setup(kind)
    col0 = cfg["col0"]
    ins, in_specs = [], []
    if cfg["pair"]:
        ins.append(src)
        in_specs.append(pl.BlockSpec((1, S, PAIR_W), lambda b, h: (b, 0, col0 // PAIR_W + h)))
    else:
        ins += [src, kv, kv]
        in_specs += [pl.BlockSpec((1, S, LANES), lambda b, h: (b, 0, col0 // LANES + h)),
                     pl.BlockSpec((1, MEM_LEN, LANES), lambda b, h: (b, 0, h)),
                     pl.BlockSpec((1, MEM_LEN, LANES), lambda b, h: (b, 0, MEM_HEADS + h))]
    if negc_cols is not None:
        ins.append(negc_cols)
        in_specs.append(pl.BlockSpec((1, S, LANES), lambda b, h: (b, 0, 0)))
    if mask is not None:
        ins.append(mask)
        in_specs.append(pl.BlockSpec(mask.shape, lambda b, h: (0, 0, 0)))
    if rope is not None:
        ins += list(rope)
        in_specs += [pl.BlockSpec((S, LANES), lambda b, h: (0, 0))] * 3
    return ins, in_specs


def _attn_t_prep(cfg, refs, S, Sk, *, qT2s, ks, q2s=None, vs=None, vTs=None, kTs=None, nb=None):
    pair, nh = cfg["pair"], cfg["nh"]
    lane = lax.broadcasted_iota(jnp.int32, (1, LANES), 1)
    rope_refs = refs["rope"]

    def prep_q(n, _):
        rows = pl.ds(pl.multiple_of(n * TQ, TQ), TQ)
        q = refs["load_q"](rows)
        if rope_refs is not None:
            q = _rope(q, *[t[rows, :] for t in rope_refs])
        if cfg["q_fold"] is not None:
            q = q * cfg["q_fold"]
        qb = q.astype(BF16)
        if q2s is not None:
            _store_stacked(cfg, lane, q2s, n, qb)
        qtb = qb.T
        for hh in range(nh):
            qT2s[n, :, hh * TQ:(hh + 1) * TQ] = jnp.where(_head_rows(hh, pair), qtb, jnp.zeros_like(qtb))
        return 0

    def prep_kv(n, _):
        rows = pl.ds(pl.multiple_of(n * TK, TK), TK)
        k, v = refs["load_kv"](rows)
        if rope_refs is not None:
            k = _rope(k, *[t[rows, :] for t in rope_refs])
        kb = k.astype(BF16)
        vb = v.astype(BF16)
        ks[rows, :] = kb
        if vs is not None:
            vs[rows, :] = vb
        if vTs is not None:
            vTs[n] = vb.T
        if kTs is not None:
            kTs[n] = kb.T
        if nb is not None:
            blk = refs["negc"][0, rows, :]
            for hh in range(nh):
                h = 2 * refs["block"] + hh
                col = jnp.sum(jnp.where(lane == h, blk, 0.0), axis=1, keepdims=True)
                nb[hh, rows, :] = jnp.broadcast_to(col, (TK, LANES))
        return 0

    lax.fori_loop(0, S // TQ, prep_q, 0)
    lax.fori_loop(0, Sk // TK, prep_kv, 0)


def _raw_scores_t(cfg, k, qT2):
    sT = jnp.dot(k, qT2, preferred_element_type=F32)
    if cfg["s_scale"] is not None:
        sT = sT * cfg["s_scale"]
    return sT


def _bias_mask_t(cfg, sT, nb, mask_ref, kc, midx):
    nh = cfg["nh"]
    if nb is None and midx is None:
        return sT
    parts = []
    for hh in range(nh):
        t = sT[:, hh * TQ:(hh + 1) * TQ]
        if nb is not None:
            t = t + jnp.concatenate([nb[hh, kc, :]] * (TQ // LANES), axis=1)
        if midx is not None:
            t = t + mask_ref[midx]
        parts.append(t)
    return _cat(parts, 1)


def _kv_plan(kind, i, nk):
    if kind == "fox":
        return i, (lambda j: None), 0
    if kind == "dil":
        return i, (lambda j: i - j), 0
    return nk - 1, (lambda j: None), None


def attn_fwd3(kind, src, S, *, negc_cols=None, mask=None, rope=None, kv=None):
    B = src.shape[0]
    cfg = _attn_setup(kind)
    pair, nh = cfg["pair"], cfg["nh"]
    Sk = S if pair else MEM_LEN
    has_bias, has_rope = negc_cols is not None, rope is not None
    R = nh * TQ
    nq, nk = S // TQ, Sk // TK

    def body(*refs):
        refs = list(refs)
        if pair:
            qkv_ref = refs.pop(0)
            load_q = lambda rows: qkv_ref[0, rows, 0:LANES]
            load_kv = lambda rows: (qkv_ref[0, rows, LANES:2 * LANES], qkv_ref[0, rows, 2 * LANES:3 * LANES])
        else:
            q_ref, k_ref, v_ref = refs.pop(0), refs.pop(0), refs.pop(0)
            load_q = lambda rows: q_ref[0, rows, :]
            load_kv = lambda rows: (k_ref[0, rows, :], v_ref[0, rows, :])
        negc_ref = refs.pop(0) if has_bias else None
        mask_ref = refs.pop(0) if mask is not None else None
        rope_refs = [refs.pop(0) for _ in range(3)] if has_rope else None
        o_ref, lse_ref, qT2s, ks, vTs = refs[:5]
        nb = refs[5] if has_bias else None
        _attn_t_prep(cfg, dict(load_q=load_q, load_kv=load_kv, rope=rope_refs, negc=negc_ref,
                               block=pl.program_id(1)), S, Sk,
                     qT2s=qT2s, ks=ks, vTs=vTs, nb=nb)

        def q_loop(i, _):
            qT2 = qT2s[i]

            last, mask_of, mask_last = _kv_plan(kind, i, nk)

            def cols(j):
                return pl.ds(pl.multiple_of(j * TK, TK), TK)

            def scores(j):
                return _raw_scores_t(cfg, ks[cols(j), :], qT2)

            def soft(s_raw, j, midx, m, l):
                sT = _bias_mask_t(cfg, s_raw, nb, mask_ref, cols(j), midx)
                m_new = jnp.maximum(m, jnp.max(sT, axis=0, keepdims=True))
                p = jnp.exp(sT - m_new)
                alpha = jnp.exp(m - m_new)
                return m_new, alpha * l + jnp.sum(p, axis=0, keepdims=True), alpha, p.astype(BF16)

            def pv(j, p):
                return jnp.dot(vTs[j], p, preferred_element_type=F32)

            def body(j, carry):
                s_cur, p_prev, m, l, accT = carry
                pv_prev = pv(jnp.maximum(j - 1, 0), p_prev)
                s_next = scores(j + 1)
                m, l, alpha, p = soft(s_cur, j, mask_of(j), m, l)
                return s_next, p, m, l, (accT + pv_prev) * alpha

            init = (scores(0), jnp.zeros((TK, R), BF16), jnp.full((1, R), NEG_INF, F32), jnp.zeros((1, R), F32),
                    jnp.zeros((LANES, R), F32))
            s_cur, p_prev, m, l, accT = lax.fori_loop(0, last, body, init)
            pv_prev = pv(jnp.maximum(last - 1, 0), p_prev)
            m, l, alpha, p = soft(s_cur, last, mask_last, m, l)
            accT = (accT + pv_prev) * alpha + pv(last, p)
            oT2 = accT / l
            oT = jnp.where(_head_rows(0, True), oT2[:, 0:TQ], oT2[:, TQ:2 * TQ]) if pair else oT2
            o_ref[0, pl.ds(pl.multiple_of(i * TQ, TQ), TQ), :] = oT.T
            lse_ref[0, 0, pl.ds(i, 1), :] = m + jnp.log(l)
            return 0

        lax.fori_loop(0, nq, q_loop, 0)

    ins, in_specs = _attn_t_inputs(kind, src, S, negc_cols, mask, rope, kv)
    W = cfg["n_blocks"] * LANES
    scratch = [pltpu.VMEM((nq, LANES, R), BF16), pltpu.VMEM((Sk, LANES), BF16), pltpu.VMEM((nk, LANES, TK), BF16)]
    if has_bias:
        scratch.append(pltpu.VMEM((nh, Sk, LANES), F32))
    return pl.pallas_call(
        body, name=kind + "_attn_fwd", grid=(B, cfg["n_blocks"]),
        in_specs=in_specs,
        out_specs=[pl.BlockSpec((1, S, LANES), lambda b, h: (b, 0, h)),
                   pl.BlockSpec((1, 1, nq, R), lambda b, h: (b, h, 0, 0))],
        out_shape=[jax.ShapeDtypeStruct((B, S, W), F32), jax.ShapeDtypeStruct((B, cfg["n_blocks"], nq, R), F32)],
        scratch_shapes=scratch,
        compiler_params=_params(("arbitrary", "arbitrary")),
    )(*ins)


def _tile_walk(kind, nq, nk):
    if kind == "mem":
        return nq * nk, (lambda i, j: (jnp.where(j < nk - 1, i, i + 1), jnp.where(j < nk - 1, j + 1, 0))), None
    nxt = lambda i, j: (jnp.where(j < i, i, i + 1), jnp.where(j < i, j + 1, 0))
    if kind == "fox":
        return nq * (nq + 1) // 2, nxt, (lambda i, j: jnp.where(j == i, 0, 1))
    return nq * (nq + 1) // 2, nxt, (lambda i, j: i - j)


def attn_fwd4(kind, src, S, *, negc_cols=None, mask=None, rope=None, kv=None):
    B = src.shape[0]
    cfg = _attn_setup(kind)
    pair, nh = cfg["pair"], cfg["nh"]
    Sk = S if pair else MEM_LEN
    has_bias, has_rope = negc_cols is not None, rope is not None
    R = nh * TQ
    nq, nk = S // TQ, Sk // TK
    n_pairs, successor, mask_index = _tile_walk(kind, nq, nk)
    assert n_pairs % 2 == 0

    def body(*refs):
        refs = list(refs)
        if pair:
            qkv_ref = refs.pop(0)
            load_q = lambda rows: qkv_ref[0, rows, 0:LANES]
            load_kv = lambda rows: (qkv_ref[0, rows, LANES:2 * LANES], qkv_ref[0, rows, 2 * LANES:3 * LANES])
        else:
            q_ref, k_ref, v_ref = refs.pop(0), refs.pop(0), refs.pop(0)
            load_q = lambda rows: q_ref[0, rows, :]
            load_kv = lambda rows: (k_ref[0, rows, :], v_ref[0, rows, :])
        negc_ref = refs.pop(0) if has_bias else None
        mask_ref = refs.pop(0) if mask is not None else None
        rope_refs = [refs.pop(0) for _ in range(3)] if has_rope else None
        o_ref, lse_ref, qT2s, ks, vTs, s_a, s_b, p_a, p_b, acc_all, m_all, l_all = refs[:12]
        nb = refs[12] if has_bias else None
        _attn_t_prep(cfg, dict(load_q=load_q, load_kv=load_kv, rope=rope_refs, negc=negc_ref,
                               block=pl.program_id(1)), S, Sk, qT2s=qT2s, ks=ks, vTs=vTs, nb=nb)

        def cols(j):
            return pl.ds(pl.multiple_of(j * TK, TK), TK)

        def park(i, m, l, accT):
            acc_all[i] = accT
            m_all[pl.ds(i, 1), :] = m
            l_all[pl.ds(i, 1), :] = l

        def finish(i, _):
            l = l_all[pl.ds(i, 1), :]
            oT2 = acc_all[i] / l
            oT = jnp.where(_head_rows(0, True), oT2[:, 0:TQ], oT2[:, TQ:2 * TQ]) if pair else oT2
            o_ref[0, pl.ds(pl.multiple_of(i * TQ, TQ), TQ), :] = oT.T
            lse_ref[0, 0, pl.ds(i, 1), :] = m_all[pl.ds(i, 1), :] + jnp.log(l)
            return 0

        def half(i, j, i_prev, j_prev, s_cur, s_next, p_cur, p_prev, m, l, accT):
            i_n, j_n = successor(i, j)
            acc_full = accT + jnp.dot(vTs[j_prev], p_prev[...], preferred_element_type=F32)
            s_next[...] = _raw_scores_t(cfg, ks[cols(j_n), :], qT2s[jnp.minimum(i_n, nq - 1)])
            park(i_prev, m, l, acc_full)
            first = j == 0
            m = jnp.where(first, NEG_INF, m)
            l = jnp.where(first, 0.0, l)
            sT = _bias_mask_t(cfg, s_cur[...], nb, mask_ref, cols(j), None if mask_index is None else mask_index(i, j))
            m_new = jnp.maximum(m, jnp.max(sT, axis=0, keepdims=True))
            p = jnp.exp(sT - m_new)
            alpha = jnp.exp(m - m_new)
            p_cur[...] = p.astype(BF16)
            return i_n, j_n, i, j, m_new, alpha * l + jnp.sum(p, axis=0, keepdims=True), acc_full * alpha

        def two(_, carry):
            i, j, i_prev, j_prev, m, l, accT = carry
            i, j, i_prev, j_prev, m, l, accT = half(i, j, i_prev, j_prev, s_a, s_b, p_a, p_b, m, l, accT)
            return half(i, j, i_prev, j_prev, s_b, s_a, p_b, p_a, m, l, accT)

        s_a[...] = _raw_scores_t(cfg, ks[cols(0), :], qT2s[0])
        p_b[...] = jnp.zeros((TK, R), BF16)
        zero = jnp.int32(0)
        init = (zero, zero, zero, zero, jnp.full((1, R), NEG_INF, F32), jnp.ones((1, R), F32),
                jnp.zeros((LANES, R), F32))
        _, _, i_prev, j_prev, m, l, accT = lax.fori_loop(0, n_pairs // 2, two, init)
        park(i_prev, m, l, accT + jnp.dot(vTs[j_prev], p_b[...], preferred_element_type=F32))
        lax.fori_loop(0, nq, finish, 0)

    ins, in_specs = _attn_t_inputs(kind, src, S, negc_cols, mask, rope, kv)
    W = cfg["n_blocks"] * LANES
    scratch = [pltpu.VMEM((nq, LANES, R), BF16), pltpu.VMEM((Sk, LANES), BF16), pltpu.VMEM((nk, LANES, TK), BF16),
               pltpu.VMEM((TK, R), F32), pltpu.VMEM((TK, R), F32), pltpu.VMEM((TK, R), BF16), pltpu.VMEM((TK, R), BF16),
               pltpu.VMEM((nq, LANES, R), F32), pltpu.VMEM((nq, R), F32), pltpu.VMEM((nq, R), F32)]
    if has_bias:
        scratch.append(pltpu.VMEM((nh, Sk, LANES), F32))
    return pl.pallas_call(
        body, name=kind + "_attn_fwd", grid=(B, cfg["n_blocks"]),
        in_specs=in_specs,
        out_specs=[pl.BlockSpec((1, S, LANES), lambda b, h: (b, 0, h)),
                   pl.BlockSpec((1, 1, nq, R), lambda b, h: (b, h, 0, 0))],
        out_shape=[jax.ShapeDtypeStruct((B, S, W), F32), jax.ShapeDtypeStruct((B, cfg["n_blocks"], nq, R), F32)],
        scratch_shapes=scratch,
        compiler_params=_params(("arbitrary", "arbitrary")),
    )(*ins)


def attn_bwd3(kind, src, do, o, lse, S, *, negc_cols=None, mask=None, rope=None, kv=None, token=None):
    B = src.shape[0]
    cfg = _attn_setup(kind)
    pair, nh, s_scale, q_fold = cfg["pair"], cfg["nh"], cfg["s_scale"], cfg["q_fold"]
    Sk = S if pair else MEM_LEN
    has_bias, has_rope = negc_cols is not None, rope is not None
    R = nh * TQ
    nq, nk = S // TQ, Sk // TK
    n_pairs, successor, mask_index = _tile_walk(kind, nq, nk)
    assert n_pairs % 2 == 0

    def body(*refs):
        refs = list(refs)
        if pair:
            qkv_ref = refs.pop(0)
            load_q = lambda rows: qkv_ref[0, rows, 0:LANES]
            load_kv = lambda rows: (qkv_ref[0, rows, LANES:2 * LANES], qkv_ref[0, rows, 2 * LANES:3 * LANES])
        else:
            q_ref, k_ref, v_ref = refs.pop(0), refs.pop(0), refs.pop(0)
            load_q = lambda rows: q_ref[0, rows, :]
            load_kv = lambda rows: (k_ref[0, rows, :], v_ref[0, rows, :])
        negc_ref = refs.pop(0) if has_bias else None
        mask_ref = refs.pop(0) if mask is not None else None
        rope_refs = [refs.pop(0) for _ in range(3)] if has_rope else None
        do_ref, o_ref, lse_ref = refs.pop(0), refs.pop(0), refs.pop(0)
        if token is not None:
            refs.pop(0)
        if pair:
            dqkv_ref = refs.pop(0)
            dneg_ref = refs.pop(0) if has_bias else None
            drow_ref = refs.pop(0) if has_bias else None
        else:
            dq_ref, dk_ref, dv_ref = refs.pop(0), refs.pop(0), refs.pop(0)
        qT2s, ks, q2s, vs, kTs, doT2s, do2s, delta_s, dk_acc, dv_acc = refs[:10]
        bufs_a, bufs_b, dq_all = refs[10:14], refs[14:18], refs[18]
        nb, dneg_acc, drow_all = (refs[19], refs[20], refs[21]) if has_bias else (None, None, None)
        lane = lax.broadcasted_iota(jnp.int32, (1, LANES), 1)
        _attn_t_prep(cfg, dict(load_q=load_q, load_kv=load_kv, rope=rope_refs, negc=negc_ref,
                               block=pl.program_id(1)), S, Sk,
                     qT2s=qT2s, ks=ks, q2s=q2s, vs=vs, kTs=kTs, nb=nb)

        def prep_do(n, _):
            rows = pl.ds(pl.multiple_of(n * TQ, TQ), TQ)
            dob = do_ref[0, rows, :].astype(BF16)
            _store_stacked(cfg, lane, do2s, n, dob)
            doT = dob.astype(F32).T
            prodT = doT * o_ref[0, rows, :].T
            doTb = doT.astype(BF16)
            for hh in range(nh):
                hm = _head_rows(hh, pair)
                doT2s[n, :, hh * TQ:(hh + 1) * TQ] = jnp.where(hm, doTb, jnp.zeros_like(doTb))
                delta_s[pl.ds(n, 1), hh * TQ:(hh + 1) * TQ] = jnp.sum(jnp.where(hm, prodT, 0.0), axis=0, keepdims=True)
            return 0

        def zero_kv(n, _):
            rows = pl.ds(pl.multiple_of(n * TK, TK), TK)
            dk_acc[rows, :] = jnp.zeros((TK, LANES), F32)
            dv_acc[rows, :] = jnp.zeros((TK, LANES), F32)
            if has_bias:
                for hh in range(nh):
                    dneg_acc[hh, rows, :] = jnp.zeros((TK, LANES), F32)
            return 0

        lax.fori_loop(0, nq, prep_do, 0)
        lax.fori_loop(0, nk, zero_kv, 0)

        def cols(j):
            return pl.ds(pl.multiple_of(j * TK, TK), TK)

        def rows2(i):
            return pl.ds(pl.multiple_of(i * R, R), R)

        def park(i, dqT2, drow):
            dq_all[i] = dqT2
            if has_bias:
                drow_all[pl.ds(i, 1), :] = drow

        def half(i, j, i_prev, j_prev, cur, nxt_bufs, prv, dqT2, drow):
            s_cur, dp_cur, pb_cur, dsb_cur = cur
            s_next, dp_next = nxt_bufs[0], nxt_bufs[1]
            pb_prev, dsb_prev = prv[2], prv[3]
            i_n, j_n = successor(i, j)
            first = j == 0
            if has_bias:
                drow_all[pl.ds(i_prev, 1), :] = drow
            drow = jnp.where(first, 0.0, drow)
            kc = cols(j)
            sT = _bias_mask_t(cfg, s_cur[...], nb, mask_ref, kc, None if mask_index is None else mask_index(i, j))
            pT = jnp.exp(sT - lse_ref[0, 0, pl.ds(i, 1), :])
            dsT = pT * (dp_cur[...] - delta_s[pl.ds(i, 1), :])
            if has_bias:
                drow = drow + jnp.sum(dsT, axis=0, keepdims=True)
                for hh in range(nh):
                    part = dsT[:, hh * TQ:hh * TQ + LANES]
                    for t in range(1, TQ // LANES):
                        part = part + dsT[:, hh * TQ + t * LANES:hh * TQ + (t + 1) * LANES]
                    dneg_acc[hh, kc, :] += part
            if s_scale is not None:
                dsT = dsT * s_scale
            pb_cur[...] = pT.astype(BF16)
            dsb_cur[...] = dsT.astype(BF16)
            kp = cols(j_prev)
            dv_acc[kp, :] += jnp.dot(pb_prev[...], do2s[rows2(i_prev), :], preferred_element_type=F32)
            dk_acc[kp, :] += jnp.dot(dsb_prev[...], q2s[rows2(i_prev), :], preferred_element_type=F32)
            dq_full = dqT2 + jnp.dot(kTs[j_prev], dsb_prev[...], preferred_element_type=F32)
            dq_all[i_prev] = dq_full
            dqT2 = jnp.where(first, 0.0, dq_full)
            i_nc = jnp.minimum(i_n, nq - 1)
            kn = cols(j_n)
            s_next[...] = _raw_scores_t(cfg, ks[kn, :], qT2s[i_nc])
            dp_next[...] = jnp.dot(vs[kn, :], doT2s[i_nc], preferred_element_type=F32)
            return i_n, j_n, i, j, dqT2, drow

        def two(_, carry):
            i, j, i_prev, j_prev, dqT2, drow = carry
            i, j, i_prev, j_prev, dqT2, drow = half(i, j, i_prev, j_prev, bufs_a, bufs_b, bufs_b, dqT2, drow)
            return half(i, j, i_prev, j_prev, bufs_b, bufs_a, bufs_a, dqT2, drow)

        bufs_a[0][...] = _raw_scores_t(cfg, ks[cols(0), :], qT2s[0])
        bufs_a[1][...] = jnp.dot(vs[cols(0), :], doT2s[0], preferred_element_type=F32)
        bufs_b[2][...] = jnp.zeros((TK, R), BF16)
        bufs_b[3][...] = jnp.zeros((TK, R), BF16)
        zero = jnp.int32(0)
        init = (zero, zero, zero, zero, jnp.zeros((LANES, R), F32), jnp.zeros((1, R), F32))
        _, _, i_prev, j_prev, dqT2, drow = lax.fori_loop(0, n_pairs // 2, two, init)
        kp = cols(j_prev)
        dv_acc[kp, :] += jnp.dot(bufs_b[2][...], do2s[rows2(i_prev), :], preferred_element_type=F32)
        dk_acc[kp, :] += jnp.dot(bufs_b[3][...], q2s[rows2(i_prev), :], preferred_element_type=F32)
        park(i_prev, dqT2 + jnp.dot(kTs[j_prev], bufs_b[3][...], preferred_element_type=F32), drow)

        def fin_q(i, _):
            rows = pl.ds(pl.multiple_of(i * TQ, TQ), TQ)
            dqT2 = dq_all[i]
            dqT = jnp.where(_head_rows(0, True), dqT2[:, 0:TQ], dqT2[:, TQ:2 * TQ]) if pair else dqT2
            dq = dqT.T
            if q_fold is not None:
                dq = dq * q_fold
            if has_rope:
                dq = _rope_bwd(dq, *[t[rows, :] for t in rope_refs])
            if pair:
                dqkv_ref[0, rows, 0:LANES] = dq.astype(BF16)
            else:
                dq_ref[0, rows, :] = dq.astype(BF16)
            if has_bias:
                drow_ref[0, 0, pl.ds(i, 1), :] = drow_all[pl.ds(i, 1), :]
            return 0

        lax.fori_loop(0, nq, fin_q, 0)

        def fin_kv(n, _):
            rows = pl.ds(pl.multiple_of(n * TK, TK), TK)
            dk = dk_acc[rows, :]
            if has_rope:
                dk = _rope_bwd(dk, *[t[rows, :] for t in rope_refs])
            if pair:
                dqkv_ref[0, rows, LANES:2 * LANES] = dk.astype(BF16)
                dqkv_ref[0, rows, 2 * LANES:3 * LANES] = dv_acc[rows, :].astype(BF16)
            else:
                dk_ref[0, rows, :] = dk.astype(BF16)
                dv_ref[0, rows, :] = dv_acc[rows, :].astype(BF16)
            if has_bias:
                x0 = jnp.sum(dneg_acc[0, rows, :], axis=1, keepdims=True)
                x1 = jnp.sum(dneg_acc[1, rows, :], axis=1, keepdims=True)
                dneg_ref[0, rows, :] = jnp.where(lane == 0, x0, jnp.where(lane == 1, x1, 0.0))
            return 0

        lax.fori_loop(0, nk, fin_kv, 0)

    ins, in_specs = _attn_t_inputs(kind, src, S, negc_cols, mask, rope, kv)
    row_spec = pl.BlockSpec((1, S, LANES), lambda b, h: (b, 0, h))
    vec_spec = pl.BlockSpec((1, 1, nq, R), lambda b, h: (b, h, 0, 0))
    ins += [do, o, lse]
    in_specs += [row_spec, row_spec, vec_spec]
    if token is not None:
        ins.append(token)
        in_specs.append(pl.BlockSpec(token.shape, lambda b, h: (0, 0)))
    W = cfg["n_blocks"] * LANES
    if pair:
        out_specs = [pl.BlockSpec((1, S, PAIR_W), lambda b, h: (b, 0, h))]
        out_shape = [jax.ShapeDtypeStruct((B, S, 3 * W), BF16)]
        if has_bias:
            out_specs += [row_spec, vec_spec]
            out_shape += [jax.ShapeDtypeStruct((B, S, W), F32), jax.ShapeDtypeStruct((B, cfg["n_blocks"], nq, R), F32)]
    else:
        kv_spec = pl.BlockSpec((1, MEM_LEN, LANES), lambda b, h: (b, 0, h))
        out_specs = [row_spec, kv_spec, kv_spec]
        out_shape = [jax.ShapeDtypeStruct((B, S, W), BF16)] + [jax.ShapeDtypeStruct((B, MEM_LEN, W), BF16)] * 2
    scratch = [pltpu.VMEM((nq, LANES, R), BF16), pltpu.VMEM((Sk, LANES), BF16), pltpu.VMEM((nh * S, LANES), BF16),
               pltpu.VMEM((Sk, LANES), BF16), pltpu.VMEM((nk, LANES, TK), BF16), pltpu.VMEM((nq, LANES, R), BF16),
               pltpu.VMEM((nh * S, LANES), BF16), pltpu.VMEM((nq, R), F32),
               pltpu.VMEM((Sk, LANES), F32), pltpu.VMEM((Sk, LANES), F32)]
    pair_bufs = [pltpu.VMEM((TK, R), F32), pltpu.VMEM((TK, R), F32), pltpu.VMEM((TK, R), BF16), pltpu.VMEM((TK, R), BF16)]
    scratch += pair_bufs + pair_bufs + [pltpu.VMEM((nq, LANES, R), F32)]
    if has_bias:
        scratch += [pltpu.VMEM((nh, Sk, LANES), F32), pltpu.VMEM((nh, Sk, LANES), F32), pltpu.VMEM((nq, R), F32)]
    return pl.pallas_call(
        body, name=kind + "_attn_bwd", grid=(B, cfg["n_blocks"]),
        in_specs=in_specs, out_specs=out_specs, out_shape=out_shape, scratch_shapes=scratch,
        compiler_params=_params(("arbitrary", "arbitrary")),
    )(*ins)


def _sigmoid(g):
    return 1.0 / (1.0 + jnp.exp(-g))


def out_fwd(proj, o_fox, o_dil, o_mem, w_out, x, target, gf, tm):
    T = x.shape[0]

    def body(fg_ref, dg_ref, mg_ref, of_ref, od_ref, om_ref, w_ref, x_ref, t_ref, gf_ref,
             y_ref, dx_ref, dxb_ref, sm_ref):
        parts = []
        for g_ref, o_ref in ((fg_ref, of_ref), (dg_ref, od_ref), (mg_ref, om_ref)):
            g = g_ref[...]
            parts.append((o_ref[...] * (g * _sigmoid(g))).astype(BF16))
        ymix = jnp.concatenate(parts, axis=1)
        y_ref[...] = ymix
        x2 = x_ref[...] + jnp.dot(ymix, w_ref[...], preferred_element_type=F32)
        r = lax.rsqrt(jnp.mean(x2 * x2, axis=-1, keepdims=True) + RMS_EPS)
        yn = x2 * r
        err = yn * gf_ref[...] - t_ref[...]
        loss = 0.5 * jnp.sum(jnp.sum(err * err, axis=-1, keepdims=True) / D_MODEL, axis=0, keepdims=True)
        dyf = err / D_MODEL
        dgf = jnp.sum(dyf * yn, axis=0, keepdims=True)
        dyn = dyf * gf_ref[...]
        dx2 = r * (dyn - yn * jnp.mean(dyn * yn, axis=-1, keepdims=True))
        dx_ref[...] = dx2
        dxb_ref[...] = dx2.astype(BF16)
        row = lax.broadcasted_iota(jnp.int32, (8, D_MODEL), 0)
        upd = jnp.where(row == 0, dgf, jnp.where(row == 1, loss, 0.0))

        @pl.when(pl.program_id(0) == 0)
        def _():
            sm_ref[...] = upd

        @pl.when(pl.program_id(0) != 0)
        def _():
            sm_ref[...] += upd

    def rows(w, col=0):
        return pl.BlockSpec((tm, w), lambda i: (i, col))

    return pl.pallas_call(
        body, name="out_fwd", grid=(T // tm,),
        in_specs=[rows(FOX_W, P_FG // FOX_W), rows(DIL_W, P_DG // DIL_W), rows(MEM_W, P_MG // MEM_W),
                  rows(FOX_W), rows(DIL_W), rows(MEM_W),
                  pl.BlockSpec((MIX_W, D_MODEL), lambda i: (0, 0)),
                  rows(D_MODEL), rows(D_MODEL), pl.BlockSpec((1, D_MODEL), lambda i: (0, 0))],
        out_specs=[rows(MIX_W), rows(D_MODEL), rows(D_MODEL), pl.BlockSpec((8, D_MODEL), lambda i: (0, 0))],
        out_shape=[jax.ShapeDtypeStruct((T, MIX_W), BF16), jax.ShapeDtypeStruct((T, D_MODEL), F32),
                   jax.ShapeDtypeStruct((T, D_MODEL), BF16), jax.ShapeDtypeStruct((8, D_MODEL), F32)],
        compiler_params=_params(("arbitrary",)),
    )(proj, proj, proj, o_fox, o_dil, o_mem, w_out, x, target, gf)


def out_bwd(proj, o_fox, o_dil, o_mem, w_out, dx2b, tm):
    T = dx2b.shape[0]

    def body(fg_ref, dg_ref, mg_ref, of_ref, od_ref, om_ref, w_ref, dx_ref,
             dof_ref, dod_ref, dom_ref, dfg_ref, ddg_ref, dmg_ref):
        dmix = lax.dot_general(dx_ref[...], w_ref[...], (((1,), (1,)), ((), ())), preferred_element_type=F32)
        col = 0
        for g_ref, o_ref, do_ref, dgate_ref in ((fg_ref, of_ref, dof_ref, dfg_ref), (dg_ref, od_ref, dod_ref, ddg_ref),
                                                 (mg_ref, om_ref, dom_ref, dmg_ref)):
            w = g_ref.shape[1]
            d = dmix[:, col:col + w]
            col += w
            g = g_ref[...]
            sg = _sigmoid(g)
            do_ref[...] = d * (g * sg)
            dgate_ref[...] = (d * o_ref[...] * (sg * (1.0 + g * (1.0 - sg)))).astype(BF16)

    def rows(w, col=0):
        return pl.BlockSpec((tm, w), lambda i: (i, col))

    return pl.pallas_call(
        body, name="out_bwd", grid=(T // tm,),
        in_specs=[rows(FOX_W, P_FG // FOX_W), rows(DIL_W, P_DG // DIL_W), rows(MEM_W, P_MG // MEM_W),
                  rows(FOX_W), rows(DIL_W), rows(MEM_W),
                  pl.BlockSpec((MIX_W, D_MODEL), lambda i: (0, 0)), rows(D_MODEL)],
        out_specs=[rows(FOX_W), rows(DIL_W), rows(MEM_W), rows(FOX_W), rows(DIL_W), rows(MEM_W)],
        out_shape=[jax.ShapeDtypeStruct((T, FOX_W), F32), jax.ShapeDtypeStruct((T, DIL_W), F32),
                   jax.ShapeDtypeStruct((T, MEM_W), F32), jax.ShapeDtypeStruct((T, FOX_W), BF16),
                   jax.ShapeDtypeStruct((T, DIL_W), BF16), jax.ShapeDtypeStruct((T, MEM_W), BF16)],
        compiler_params=_params(("arbitrary",)),
    )(proj, proj, proj, o_fox, o_dil, o_mem, w_out, dx2b)


def out_step(proj, o_fox, o_dil, o_mem, w_out, x, target, gf, tm):
    T = x.shape[0]

    def body(fg_ref, dg_ref, mg_ref, of_ref, od_ref, om_ref, w_ref, x_ref, t_ref, gf_ref,
             dx_ref, dof_ref, dod_ref, dom_ref, dfg_ref, ddg_ref, dmg_ref, gw_ref, sm_ref, gw_acc):
        branches = []
        for g_ref, o_ref in ((fg_ref, of_ref), (dg_ref, od_ref), (mg_ref, om_ref)):
            g = g_ref[...]
            sg = _sigmoid(g)
            o = o_ref[...]
            branches.append((g, sg, o))
        ymix = jnp.concatenate([(o * (g * sg)).astype(BF16) for g, sg, o in branches], axis=1)
        x2 = x_ref[...] + jnp.dot(ymix, w_ref[...], preferred_element_type=F32)
        r = lax.rsqrt(jnp.mean(x2 * x2, axis=-1, keepdims=True) + RMS_EPS)
        yn = x2 * r
        err = yn * gf_ref[...] - t_ref[...]
        loss = 0.5 * jnp.sum(jnp.sum(err * err, axis=-1, keepdims=True) / D_MODEL, axis=0, keepdims=True)
        dyf = err / D_MODEL
        dgf = jnp.sum(dyf * yn, axis=0, keepdims=True)
        dyn = dyf * gf_ref[...]
        dx2 = r * (dyn - yn * jnp.mean(dyn * yn, axis=-1, keepdims=True))
        dx_ref[...] = dx2
        dxb = dx2.astype(BF16)
        dmix = lax.dot_general(dxb, w_ref[...], (((1,), (1,)), ((), ())), preferred_element_type=F32)
        col = 0
        for (g, sg, o), do_ref, dgate_ref in zip(branches, (dof_ref, dod_ref, dom_ref), (dfg_ref, ddg_ref, dmg_ref)):
            d = dmix[:, col:col + g.shape[1]]
            col += g.shape[1]
            do_ref[...] = (d * (g * sg)).astype(BF16)
            dgate_ref[...] = (d * o * (sg * (1.0 + g * (1.0 - sg)))).astype(BF16)
        row = lax.broadcasted_iota(jnp.int32, (8, D_MODEL), 0)
        upd = jnp.where(row == 0, dgf, jnp.where(row == 1, loss, 0.0))

        @pl.when(pl.program_id(0) == 0)
        def _():
            sm_ref[...] = jnp.zeros(sm_ref.shape, F32)
            gw_acc[...] = jnp.zeros(gw_acc.shape, F32)

        sm_ref[...] += upd
        gw_acc[...] += lax.dot_general(ymix, dxb, (((0,), (0,)), ((), ())), preferred_element_type=F32)

        @pl.when(pl.program_id(0) == T // tm - 1)
        def _():
            gw_ref[...] = gw_acc[...].astype(BF16)

    def rows(w, col=0):
        return pl.BlockSpec((tm, w), lambda i: (i, col))

    return pl.pallas_call(
        body, name="out_step", grid=(T // tm,),
        in_specs=[rows(FOX_W, P_FG // FOX_W), rows(DIL_W, P_DG // DIL_W), rows(MEM_W, P_MG // MEM_W),
                  rows(FOX_W), rows(DIL_W), rows(MEM_W),
                  pl.BlockSpec((MIX_W, D_MODEL), lambda i: (0, 0)),
                  rows(D_MODEL), rows(D_MODEL), pl.BlockSpec((1, D_MODEL), lambda i: (0, 0))],
        out_specs=[rows(D_MODEL), rows(FOX_W), rows(DIL_W), rows(MEM_W), rows(FOX_W), rows(DIL_W), rows(MEM_W),
                   pl.BlockSpec((MIX_W, D_MODEL), lambda i: (0, 0)), pl.BlockSpec((8, D_MODEL), lambda i: (0, 0))],
        out_shape=[jax.ShapeDtypeStruct((T, D_MODEL), F32), jax.ShapeDtypeStruct((T, FOX_W), BF16),
                   jax.ShapeDtypeStruct((T, DIL_W), BF16), jax.ShapeDtypeStruct((T, MEM_W), BF16),
                   jax.ShapeDtypeStruct((T, FOX_W), BF16), jax.ShapeDtypeStruct((T, DIL_W), BF16),
                   jax.ShapeDtypeStruct((T, MEM_W), BF16), jax.ShapeDtypeStruct((MIX_W, D_MODEL), BF16),
                   jax.ShapeDtypeStruct((8, D_MODEL), F32)],
        scratch_shapes=[pltpu.VMEM((MIX_W, D_MODEL), F32)],
        compiler_params=_params(("arbitrary",)),
    )(proj, proj, proj, o_fox, o_dil, o_mem, w_out, x, target, gf)


def adamw(w, g, m, v, tr, name):
    lead = w.shape[:-2]
    R, C = w.shape[-2:]
    zeros = (0,) * len(lead)

    def body(w_ref, g_ref, m_ref, v_ref, d_ref, mo_ref, vo_ref):
        gv = g_ref[...]
        mn = ADAM_B1 * m_ref[...] + (1.0 - ADAM_B1) * gv
        vn = ADAM_B2 * v_ref[...] + (1.0 - ADAM_B2) * jnp.square(gv)
        m_hat = mn / (1.0 - ADAM_B1 ** ADAM_STEP)
        v_hat = vn / (1.0 - ADAM_B2 ** ADAM_STEP)
        d_ref[...] = -ADAM_LR * (m_hat / (jnp.sqrt(v_hat) + ADAM_EPS) + ADAM_WD * w_ref[...])
        mo_ref[...] = mn
        vo_ref[...] = vn

    spec = pl.BlockSpec((1,) * len(lead) + (tr, C), lambda i: zeros + (i, 0))
    return pl.pallas_call(
        body, name=name, grid=(pl.cdiv(R, tr),),
        in_specs=[spec] * 4, out_specs=[spec] * 3,
        out_shape=[jax.ShapeDtypeStruct(w.shape, F32)] * 3,
        compiler_params=_params(("arbitrary",)),
    )(w, g, m, v)


def _pad_row(v, width):
    return jnp.concatenate([v, jnp.zeros((1, width - v.shape[1]), v.dtype)], axis=1)


def _old_local_grads(x, mem, norm_g, b_forget, mem_norm_g, final_norm_g, loss_target, w_in_p, w_kv, w_out):
    B, S, D = x.shape
    T = B * S
    xt = x.reshape(T, D)
    memt = mem.reshape(B * MEM_LEN, D)
    b_pad = _pad_row(b_forget, LANES)

    h = rms_fwd(xt, norm_g, 512, "rms_x")
    proj = mm_nn(h, w_in_p, 512, PW // 3, "in_proj")
    proj3 = proj.reshape(B, S, PW)
    mh = rms_fwd(memt, mem_norm_g, B * MEM_LEN, "rms_mem")
    mkv = mm_nn(mh, w_kv, B * MEM_LEN, 2 * MEM_W, "mem_kv_proj")
    mkv3 = mkv.reshape(B, MEM_LEN, 2 * MEM_W)

    negc = fox_gate(proj3, b_pad)
    causal = _log_masks_t(S, "causal")
    causal = jnp.concatenate([causal, jnp.zeros_like(causal)], axis=0)
    dilated = _log_masks_t(S, "dilated")
    rope = _rope_tables(S)

    o_fox, lse_fox = attn_fwd4("fox", proj3, S, negc_cols=negc, mask=causal)
    o_dil, lse_dil = attn_fwd4("dil", proj3, S, mask=dilated, rope=rope)
    o_mem, lse_mem = attn_fwd4("mem", proj3, S, kv=mkv3)

    dx2, do_fox, do_dil, do_mem, dfg, ddg, dmg, g_out, small_out = out_step(
        proj, o_fox.reshape(T, FOX_W), o_dil.reshape(T, DIL_W), o_mem.reshape(T, MEM_W), w_out,
        xt, loss_target.reshape(T, D), final_norm_g.reshape(1, D), 256)

    dqkv_fox, dneg, drow = attn_bwd3("fox", proj3, do_fox.reshape(B, S, FOX_W), o_fox, lse_fox, S,
                                     negc_cols=negc, mask=causal)
    (dqkv_dil,) = attn_bwd3("dil", proj3, do_dil.reshape(B, S, DIL_W), o_dil, lse_dil, S, mask=dilated, rope=rope)
    dmq, dmk, dmv = attn_bwd3("mem", proj3, do_mem.reshape(B, S, MEM_W), o_mem, lse_mem, S, kv=mkv3)
    drow = drow.reshape(B, FOX_HEADS // 2, S // TQ, 2, TQ).transpose(0, 1, 3, 2, 4).reshape(B, FOX_HEADS, S)
    drow = jnp.pad(drow, ((0, 0), (0, LANES - FOX_HEADS), (0, 0)))
    dflog, db_part = fox_gate_bwd(drow, dneg, proj3, b_pad)

    groups = [[(dfg, P_FG), (ddg, P_DG), (dmg, P_MG)],
              [(dqkv_fox.reshape(T, 3 * FOX_W), P_FOX), (dflog.reshape(T, LANES), P_FLOG)],
              [(dqkv_dil.reshape(T, 3 * DIL_W), P_DIL), (dmq.reshape(T, MEM_W), P_MQ)]]
    g_in = [mm_tn_multi(h_t,[arr for arr, _ in grp], 512, "w_in_grad_%d" % n) for n, grp in enumerate(groups)]
    grad_x, dng = in_proj_bwd_rms([piece for grp in groups for piece in grp], w_in_p, xt, norm_g, dx2, 256)

    dmkv = jnp.concatenate([dmk, dmv], axis=2).reshape(B * MEM_LEN, 2 * MEM_W)
    g_kv = mm_tn(mh, dmkv, B * MEM_LEN, 2 * MEM_W, "w_kv_grad")
    dmh = mm_nt(dmkv, w_kv, B * MEM_LEN, D, "mem_kv_bwd")
    _, dmng = rms_bwd(memt, mem_norm_g, dmh, None, B * MEM_LEN, "rms_mem_bwd")

    small = jnp.concatenate([dng[0:1], dmng[0:1], small_out[0:1], _pad_row(db_part[0:1], D), small_out[1:2],
                             jnp.zeros((3, D), F32)], axis=0)
    return grad_x.reshape(B, S, D), g_in, g_kv, g_out, small


def _old2_local_grads(x, mem, norm_g, b_forget, mem_norm_g, final_norm_g, loss_target, w_in_p, w_kv, w_out, start_exchange):
    B, S, D = x.shape
    T = B * S
    xt = x.reshape(T, D)
    memt = mem.reshape(B * MEM_LEN, D)
    b_pad = _pad_row(b_forget, LANES)

    h = rms_fwd(xt, norm_g, 512, "rms_x")
    proj = mm_nn(h, w_in_p, 512, PW // 3, "in_proj")
    proj3 = proj.reshape(B, S, PW)
    mh = rms_fwd(memt, mem_norm_g, B * MEM_LEN, "rms_mem")
    mkv = mm_nn(mh, w_kv, B * MEM_LEN, 2 * MEM_W, "mem_kv_proj")
    mkv3 = mkv.reshape(B, MEM_LEN, 2 * MEM_W)

    negc = fox_gate(proj3, b_pad)
    causal = _log_masks_t(S, "causal")
    causal = jnp.concatenate([causal, jnp.zeros_like(causal)], axis=0)
    dilated = _log_masks_t(S, "dilated")
    rope = _rope_tables(S)

    o_fox, lse_fox = attn_fwd4("fox", proj3, S, negc_cols=negc, mask=causal)
    o_dil, lse_dil = attn_fwd4("dil", proj3, S, mask=dilated, rope=rope)
    o_mem, lse_mem = attn_fwd4("mem", proj3, S, kv=mkv3)

    dx2, do_fox, do_dil, do_mem, dfg, ddg, dmg, g_out, small_out = out_step(
        proj, o_fox.reshape(T, FOX_W), o_dil.reshape(T, DIL_W), o_mem.reshape(T, MEM_W), w_out,
        xt, loss_target.reshape(T, D), final_norm_g.reshape(1, D), 256)

    gates = [(dfg, P_FG), (ddg, P_DG), (dmg, P_MG)]
    g_gates = mm_tn_multi(h_t, [arr for arr, _ in gates], 1024, "w_in_grad_gates", BF16)
    first, token = start_exchange([g_gates, g_out], "early_exchange_a")

    dqkv_fox, dneg, drow = attn_bwd3("fox", proj3, do_fox.reshape(B, S, FOX_W), o_fox, lse_fox, S,
                                     negc_cols=negc, mask=causal, token=token)
    drow = drow.reshape(B, FOX_HEADS // 2, S // TQ, 2, TQ).transpose(0, 1, 3, 2, 4).reshape(B, FOX_HEADS, S)
    drow = jnp.pad(drow, ((0, 0), (0, LANES - FOX_HEADS), (0, 0)))
    dflog, db_part = fox_gate_bwd(drow, dneg, proj3, b_pad)
    fox = [(dqkv_fox.reshape(T, 3 * FOX_W), P_FOX), (dflog.reshape(T, LANES), P_FLOG)]
    g_fox = mm_tn_multi(h_t, [arr for arr, _ in fox], 1024, "w_in_grad_fox", BF16)
    second, token = start_exchange([g_fox], "early_exchange_b")

    (dqkv_dil,) = attn_bwd3("dil", proj3, do_dil.reshape(B, S, DIL_W), o_dil, lse_dil, S, mask=dilated, rope=rope,
                            token=token)
    dmq, dmk, dmv = attn_bwd3("mem", proj3, do_mem.reshape(B, S, MEM_W), o_mem, lse_mem, S, kv=mkv3)
    rest = [(dqkv_dil.reshape(T, 3 * DIL_W), P_DIL), (dmq.reshape(T, MEM_W), P_MQ)]
    g_rest = mm_tn_multi(h_t,[arr for arr, _ in rest], 512, "w_in_grad_rest")
    grad_x, dng = in_proj_bwd_rms(gates + fox + rest, w_in_p, xt, norm_g, dx2, 256)

    dmkv = jnp.concatenate([dmk, dmv], axis=2).reshape(B * MEM_LEN, 2 * MEM_W)
    g_kv = mm_tn(mh, dmkv, B * MEM_LEN, 2 * MEM_W, "w_kv_grad")
    dmh = mm_nt(dmkv, w_kv, B * MEM_LEN, D, "mem_kv_bwd")
    _, dmng = rms_bwd(memt, mem_norm_g, dmh, None, B * MEM_LEN, "rms_mem_bwd")

    small = jnp.concatenate([dng[0:1], dmng[0:1], small_out[0:1], _pad_row(db_part[0:1], D), small_out[1:2],
                             jnp.zeros((3, D), F32)], axis=0)
    return grad_x.reshape(B, S, D), [(first, dqkv_fox), (second, dqkv_dil)], [g_rest, g_kv], small


def local_grads(x, mem, norm_g, b_forget, mem_norm_g, final_norm_g, loss_target, w_in_p, first_token, small_weights,
                start_exchange):
    B, S, D = x.shape
    T = B * S
    xt = x.reshape(T, D)
    memt = mem.reshape(B * MEM_LEN, D)
    b_pad = _pad_row(b_forget, LANES)

    h, h_t = rms_fwd(xt, norm_g, 512, "rms_x", with_transpose=True)
    proj = mm_nn(h, w_in_p, 512, PW // 3, "in_proj", first_token)
    proj3 = proj.reshape(B, S, PW)

    negc = fox_gate(proj3, b_pad)
    causal = _log_masks_t(S, "causal")
    causal = jnp.concatenate([causal, jnp.zeros_like(causal)], axis=0)
    dilated = _log_masks_t(S, "dilated")
    rope = _rope_tables(S)

    o_fox, lse_fox = attn_fwd4("fox", proj3, S, negc_cols=negc, mask=causal)
    o_dil, lse_dil = attn_fwd4("dil", proj3, S, mask=dilated, rope=rope)

    w_kv, w_out = small_weights(o_dil)
    mh, mh_t = rms_fwd(memt, mem_norm_g, B * MEM_LEN, "rms_mem", with_transpose=True)
    mkv = mm_nn(mh, w_kv, B * MEM_LEN, 2 * MEM_W, "mem_kv_proj")
    mkv3 = mkv.reshape(B, MEM_LEN, 2 * MEM_W)
    o_mem, lse_mem = attn_fwd4("mem", proj3, S, kv=mkv3)

    dx2, do_fox, do_dil, do_mem, dfg, ddg, dmg, g_out, small_out = out_step(
        proj, o_fox.reshape(T, FOX_W), o_dil.reshape(T, DIL_W), o_mem.reshape(T, MEM_W), w_out,
        xt, loss_target.reshape(T, D), final_norm_g.reshape(1, D), 256)

    gates = [(dfg, P_FG), (ddg, P_DG), (dmg, P_MG)]
    g_gates = mm_tn_multi(h_t, [arr for arr, _ in gates], 1024, "w_in_grad_gates", BF16)
    first, token = start_exchange([g_gates, g_out], "early_exchange_a")

    dqkv_fox, dneg, drow = attn_bwd3("fox", proj3, do_fox.reshape(B, S, FOX_W), o_fox, lse_fox, S,
                                     negc_cols=negc, mask=causal, token=token)
    drow = drow.reshape(B, FOX_HEADS // 2, S // TQ, 2, TQ).transpose(0, 1, 3, 2, 4).reshape(B, FOX_HEADS, S)
    drow = jnp.pad(drow, ((0, 0), (0, LANES - FOX_HEADS), (0, 0)))
    dflog, db_part = fox_gate_bwd(drow, dneg, proj3, b_pad)
    fox = [(dqkv_fox.reshape(T, 3 * FOX_W), P_FOX), (dflog.reshape(T, LANES), P_FLOG)]
    g_fox = mm_tn_multi(h_t, [arr for arr, _ in fox], 1024, "w_in_grad_fox", BF16)
    second, token = start_exchange([g_fox], "early_exchange_b")

    (dqkv_dil,) = attn_bwd3("dil", proj3, do_dil.reshape(B, S, DIL_W), o_dil, lse_dil, S, mask=dilated, rope=rope,
                            token=token)
    dil = [(dqkv_dil.reshape(T, 3 * DIL_W), P_DIL)]
    g_dil = mm_tn_multi(h_t, [arr for arr, _ in dil], 1024, "w_in_grad_dil", BF16)
    third, token = start_exchange([g_dil], "early_exchange_c")

    dmq, dmk, dmv = attn_bwd3("mem", proj3, do_mem.reshape(B, S, MEM_W), o_mem, lse_mem, S, kv=mkv3, token=token)
    mq = [(dmq.reshape(T, MEM_W), P_MQ)]
    g_mq = mm_tn_multi(h_t, [arr for arr, _ in mq], 1024, "w_in_grad_mq", BF16)
    dmkv = jnp.concatenate([dmk, dmv], axis=2).reshape(B * MEM_LEN, 2 * MEM_W)
    g_kv = mm_tn_multi(mh_t, [dmkv], B * MEM_LEN, "w_kv_grad", BF16)
    fourth, token = start_exchange([g_mq, g_kv], "early_exchange_d")

    grad_x, dng = in_proj_bwd_rms(gates + fox + dil + mq, w_in_p, xt, norm_g, dx2, 256, token)
    dmh = mm_nt(dmkv, w_kv, B * MEM_LEN, D, "mem_kv_bwd")
    _, dmng = rms_bwd(memt, mem_norm_g, dmh, None, B * MEM_LEN, "rms_mem_bwd")

    small = jnp.concatenate([dng[0:1], dmng[0:1], small_out[0:1], _pad_row(db_part[0:1], D), small_out[1:2],
                             jnp.zeros((3, D), F32)], axis=0)
    early = [(first, dqkv_fox), (second, dqkv_dil), (third, dmq), (fourth, grad_x)]
    return grad_x.reshape(B, S, D), early, small


def kernel(x, mem, norm_g, w_in, b_forget, mem_norm_g, w_mem_kv, w_out, final_norm_g, loss_target, m_norm_g, m_w_in, m_b_forget, m_mem_norm_g, m_w_mem_kv, m_w_out, m_final_norm_g, v_norm_g, v_w_in, v_b_forget, v_mem_norm_g, v_w_mem_kv, v_w_out, v_final_norm_g):
    D = D_MODEL
    (w_in_full,) = weight_gather([_pack_cols(w_in).astype(BF16).reshape(w_in.shape[1], PW)])
    gather, gather_token = early_exchange_start([w_mem_kv[0].astype(BF16), w_out[0].astype(BF16)], "early_gather",
                                                gather=True, after=w_in_full)

    def small_weights(after):
        _, gathered = early_exchange_wait(gather, after, "early_gather_wait")
        return gathered

    grad_x, early, small = local_grads(
        x, mem, norm_g, b_forget, mem_norm_g, final_norm_g, loss_target, w_in_full, gather_token, small_weights,
        early_exchange_start)

    (first, after_first), (second, after_second), (third, after_third), (fourth, after_fourth) = early
    (src_gates, src_out), (land_gates, land_out) = early_exchange_wait(first, after_first, "early_wait_a")
    (src_fox,), (land_fox,) = early_exchange_wait(second, after_second, "early_wait_b")
    (src_dil,), (land_dil,) = early_exchange_wait(third, after_third, "early_wait_c")
    (src_mq, src_kv), (land_mq, land_kv) = early_exchange_wait(fourth, after_fourth, "early_wait_d")
    gates = slot_sum8(src_gates, land_gates, 128, "sum_w_in_gates")
    gw_out = slot_sum8(src_out, land_out, 256, "sum_w_out")
    fox = slot_sum8(src_fox, land_fox, 128, "sum_w_in_fox")
    dil = slot_sum8(src_dil, land_dil, 128, "sum_w_in_dil")
    mq = slot_sum8(src_mq, land_mq, 128, "sum_w_in_mq")
    gw_kv = slot_sum8(src_kv, land_kv, 128, "sum_w_kv")

    (csum,) = grad_exchange_d2d([], small)
    (tot,) = grad_exchange_ici([], csum)
    gw_in = _unpack_cols(jnp.concatenate(
        [fox[:, :3 * FOX_W], gates[:, :FOX_W], dil, gates[:, FOX_W:FOX_W + DIL_W], mq,
         gates[:, FOX_W + DIL_W:], fox[:, 3 * FOX_W:]], axis=1)[None])

    loss = tot[4, 0]
    g_norm, g_mem_norm, g_final, g_b = tot[0:1], tot[1:2], tot[2], tot[3:4, :FOX_HEADS]

    def rows8(*rows):
        rows = [r.reshape(1, -1) for r in rows]
        rows = [_pad_row(r, D) for r in rows]
        return jnp.concatenate(rows + [jnp.zeros((8 - len(rows), D), F32)], axis=0)

    sw = rows8(norm_g, mem_norm_g, final_norm_g, b_forget)
    sm = rows8(m_norm_g, m_mem_norm_g, m_final_norm_g, m_b_forget)
    sv = rows8(v_norm_g, v_mem_norm_g, v_final_norm_g, v_b_forget)
    d_s, m_s, v_s = adamw(sw, tot, sm, sv, 8, "adamw_small")
    d_in, m_in, v_in = adamw(w_in, gw_in, m_w_in, v_w_in, 32, "adamw_w_in")
    d_kv, m_kv, v_kv = adamw(w_mem_kv[0], gw_kv, m_w_mem_kv[0], v_w_mem_kv[0], 128, "adamw_w_kv")
    d_out, m_out, v_out = adamw(w_out[0], gw_out, m_w_out[0], v_w_out[0], 256, "adamw_w_out")

    def small_outs(t):
        return t[0:1], t[3:4, :FOX_HEADS], t[1:2], t[2]

    grads = (g_norm, gw_in, g_b, g_mem_norm, gw_kv[None], gw_out[None], g_final)
    outs = []
    for t, big in ((d_s, (d_in, d_kv, d_out)), (m_s, (m_in, m_kv, m_out)), (v_s, (v_in, v_kv, v_out))):
        n, b, mn, f = small_outs(t)
        outs += [n, big[0], b, mn, big[1][None], big[2][None], f]
    return (loss, grad_x, *grads, *outs)
```

```python
import math

import numpy as np
import jax
import jax.numpy as jnp
from jax import lax
from jax.experimental import pallas as pl
from jax.experimental.pallas import tpu as pltpu

F32 = jnp.float32
BF16 = jnp.bfloat16

D_MODEL = 1024
HEAD_DIM = 64
FOX_HEADS = 12
DIL_HEADS = 12
MEM_HEADS = 4
MEM_HEAD_DIM = 128
MEM_LEN = 256
FOX_W = FOX_HEADS * HEAD_DIM
DIL_W = DIL_HEADS * HEAD_DIM
MEM_W = MEM_HEADS * MEM_HEAD_DIM
MIX_W = FOX_W + DIL_W + MEM_W
DILATIONS = ((128, 1), (512, 4), (2048, 16))
ROPE_THETA = 500000.0
ROPE_DIM = HEAD_DIM // 4
RMS_EPS = 1e-6
NEG_INF = -1e30
IN_W = 4 * FOX_W + FOX_HEADS + 4 * DIL_W + 2 * MEM_W

ADAM_LR = 0.001
ADAM_B1 = 0.9
ADAM_B2 = 0.999
ADAM_EPS = 1e-08
ADAM_WD = 0.01
ADAM_STEP = 10

N_DEV = 8
LANES = 128
PAIR_W = 3 * LANES
TQ = 256
TK = 256

O_FQ, O_FK, O_FV, O_FG = 0, FOX_W, 2 * FOX_W, 3 * FOX_W
O_FLOG = 4 * FOX_W
O_DQ = O_FLOG + FOX_HEADS
O_DK, O_DV, O_DG = O_DQ + DIL_W, O_DQ + 2 * DIL_W, O_DQ + 3 * DIL_W
O_MQ = O_DQ + 4 * DIL_W
O_MG = O_MQ + MEM_W
P_FOX = 0
P_FG = P_FOX + 3 * FOX_W
P_DIL = P_FG + FOX_W
P_DG = P_DIL + 3 * DIL_W
P_MQ = P_DG + DIL_W
P_MG = P_MQ + MEM_W
P_FLOG = P_MG + MEM_W
PW = P_FLOG + LANES

VMEM_LIMIT = 56 * 1024 * 1024


def _pack_pieces():
    pieces = []
    for base in (O_FQ, O_DQ):
        seg = []
        for hp in range(FOX_HEADS // 2):
            for part in range(3):
                seg.append((base + part * FOX_W + hp * LANES, LANES))
        pieces.append(seg)
    fox, dil = pieces
    return fox + [(O_FG, FOX_W)] + dil + [(O_DG, DIL_W), (O_MQ, MEM_W), (O_MG, MEM_W), (O_FLOG, FOX_HEADS)]


def _pack_cols(w):
    parts = [w[..., s:s + n] for s, n in _pack_pieces()]
    parts.append(jnp.zeros(w.shape[:-1] + (LANES - FOX_HEADS,), w.dtype))
    return jnp.concatenate(parts, axis=-1)


def _unpack_cols(g):
    runs = []
    pos = 0
    for s, n in _pack_pieces():
        runs.append((s, n, pos))
        pos += n
    runs.sort()
    return jnp.concatenate([g[..., p:p + n] for s, n, p in runs], axis=-1)


def _params(sem=None, **kw):
    return pltpu.CompilerParams(dimension_semantics=sem, vmem_limit_bytes=VMEM_LIMIT, **kw)


def _mesh_pos():
    return lax.axis_index("x"), lax.axis_index("y"), lax.axis_index("c")


def _flip(v, d):
    return 1 - v if d else v


_RELATIONS = [(dx, dy, dc) for dx in (0, 1) for dy in (0, 1) for dc in (0, 1)][1:]


def weight_gather(shards):
    n_arr = len(shards)
    rows = [s.shape[0] for s in shards]

    def body(*refs):
        in_refs = refs[:n_arr]
        out_refs = refs[n_arr:2 * n_arr]
        send_sems, recv_sems, local_sems = refs[2 * n_arr:]
        x, y, c = _mesh_pos()
        me, sibling = (x, y, c), (x, y, 1 - c)
        x_nbr, y_nbr, diag = (1 - x, y, c), (x, 1 - y, c), (1 - x, 1 - y, c)
        north = c == 1
        relay_from = (jnp.where(north, 1 - x, x), jnp.where(north, y, 1 - y), c)
        relay_to = (jnp.where(north, x, 1 - x), jnp.where(north, 1 - y, y), c)
        k_from = jnp.where(north, 1, 2)
        k_to = 3 - k_from

        def block(a, pos):
            px, py, pc = pos
            return out_refs[a].at[pl.ds((4 * px + 2 * py + pc) * rows[a], rows[a]), :]

        def copy(a, k, blk, to, src=None):
            return pltpu.make_async_remote_copy(
                src_ref=block(a, blk) if src is None else src, dst_ref=block(a, blk),
                send_sem=send_sems.at[a, k], recv_sem=recv_sems.at[a, k],
                device_id=to, device_id_type=pl.DeviceIdType.MESH)

        started = []
        mine = []
        for a in range(n_arr):
            cp = pltpu.make_async_copy(in_refs[a], block(a, me), local_sems.at[a])
            cp.start()
            mine.append(cp)
            first = [copy(a, 0, me, sibling, src=in_refs[a]), copy(a, 1, me, x_nbr, src=in_refs[a]),
                     copy(a, 2, me, y_nbr, src=in_refs[a])]
            for cp in first:
                cp.start()
            started += first
        for a in range(n_arr):
            copy(a, k_from, relay_from, me).wait_recv()
            second_hop = copy(a, 3, relay_from, relay_to)
            second_hop.start()
            passed = copy(a, 3 + k_from, relay_from, sibling)
            passed.start()
            started += [second_hop, passed]
        for a in range(n_arr):
            copy(a, k_to, relay_to, me).wait_recv()
            passed = copy(a, 3 + k_to, relay_to, sibling)
            passed.start()
            started.append(passed)
        for a in range(n_arr):
            copy(a, 3, diag, me).wait_recv()
            passed = copy(a, 6, diag, sibling)
            passed.start()
            started.append(passed)
        for a in range(n_arr):
            copy(a, 0, sibling, me).wait_recv()
            for k, chip in ((4, x_nbr), (5, y_nbr), (6, diag)):
                copy(a, k, (chip[0], chip[1], 1 - c), me).wait_recv()
        for cp in started:
            cp.wait_send()
        for cp in mine:
            cp.wait()

    any_spec = pl.BlockSpec(memory_space=pl.ANY)
    return pl.pallas_call(
        body, name="weight_gather",
        out_shape=[jax.ShapeDtypeStruct((N_DEV * s.shape[0], s.shape[1]), s.dtype) for s in shards],
        in_specs=[any_spec] * n_arr, out_specs=[any_spec] * n_arr,
        scratch_shapes=[pltpu.SemaphoreType.DMA((n_arr, 7)), pltpu.SemaphoreType.DMA((n_arr, 7)),
                        pltpu.SemaphoreType.DMA((n_arr,))],
    )(*shards)


N_CHIP = 4
_OTHER_CHIPS = [(1, 0), (0, 1), (1, 1)]


def small_all_reduce(small):
    vmem_spec = pl.BlockSpec(memory_space=pltpu.VMEM)

    def chip_body(small_ref, csum_ref, land, send_sem, recv_sem):
        x, y, c = _mesh_pos()
        swap = pltpu.make_async_remote_copy(
            src_ref=small_ref, dst_ref=land, send_sem=send_sem, recv_sem=recv_sem,
            device_id=(x, y, 1 - c), device_id_type=pl.DeviceIdType.MESH)
        swap.start()
        swap.wait_recv()
        swap.wait_send()
        csum_ref[...] = small_ref[...] + land[...]

    csum = pl.pallas_call(
        chip_body, name="small_sum_d2d", out_shape=jax.ShapeDtypeStruct(small.shape, small.dtype),
        in_specs=[vmem_spec], out_specs=vmem_spec,
        scratch_shapes=[pltpu.VMEM(small.shape, small.dtype), pltpu.SemaphoreType.DMA, pltpu.SemaphoreType.DMA],
    )(small)

    def all_body(csum_ref, tot_ref, land, send_sems, recv_sems):
        x, y, c = _mesh_pos()
        q_me = 2 * x + y
        land[q_me] = csum_ref[...]
        sends, recvs = [], []
        for j, (dx, dy) in enumerate(_OTHER_CHIPS):
            px, py = _flip(x, dx), _flip(y, dy)
            common = dict(send_sem=send_sems.at[j], recv_sem=recv_sems.at[j],
                          device_id=(px, py, c), device_id_type=pl.DeviceIdType.MESH)
            sends.append(pltpu.make_async_remote_copy(src_ref=csum_ref, dst_ref=land.at[q_me], **common))
            recvs.append(pltpu.make_async_remote_copy(src_ref=csum_ref, dst_ref=land.at[2 * px + py], **common))
        for cp in sends:
            cp.start()
        for cp in recvs:
            cp.wait_recv()
        for cp in sends:
            cp.wait_send()
        tot = land[0]
        for q in range(1, N_CHIP):
            tot = tot + land[q]
        tot_ref[...] = tot

    return pl.pallas_call(
        all_body, name="small_sum_ici", out_shape=jax.ShapeDtypeStruct(small.shape, small.dtype),
        in_specs=[vmem_spec], out_specs=vmem_spec,
        scratch_shapes=[pltpu.VMEM((N_CHIP,) + small.shape, small.dtype),
                        pltpu.SemaphoreType.DMA((3,)), pltpu.SemaphoreType.DMA((3,))],
    )(csum)


_HBM = pl.BlockSpec(memory_space=pltpu.HBM)
_SEM = pl.BlockSpec(memory_space=pltpu.SEMAPHORE)
_EFFECT = pltpu.SideEffectType.DATAFLOW_SIDE_EFFECTING


def _early_copies(src_refs, land_refs, send_sems, recv_sems, rows, gather):
    x, y, c = _mesh_pos()
    me = 4 * x + 2 * y + c
    copies = []
    for a in range(len(src_refs)):
        for dx, dy, dc in _RELATIONS:
            px, py, pc = _flip(x, dx), _flip(y, dy), _flip(c, dc)
            peer = 4 * px + 2 * py + pc
            copies.append(pltpu.make_async_remote_copy(
                src_ref=src_refs[a] if gather else src_refs[a].at[pl.ds(peer * rows[a], rows[a]), :],
                dst_ref=land_refs[a].at[pl.ds(me * rows[a], rows[a]), :],
                send_sem=send_sems[a], recv_sem=recv_sems[a],
                device_id=(px, py, pc), device_id_type=pl.DeviceIdType.MESH))
    return copies


def early_exchange_start(srcs, name, gather=False, after=None):
    n = len(srcs)
    if gather:
        rows = [s.shape[0] for s in srcs]
        me = 4 * lax.axis_index("x") + 2 * lax.axis_index("y") + lax.axis_index("c")
        lands = [lax.dynamic_update_slice(lax.empty((N_DEV * r, s.shape[1]), s.dtype), s, (me * r, 0))
                 for r, s in zip(rows, srcs)]
    else:
        rows = [s.shape[0] // N_DEV for s in srcs]
        lands = [lax.empty(s.shape, s.dtype) for s in srcs]

    extra = [] if after is None else [after]

    def body(*refs):
        src_refs, land_refs = refs[:n], refs[n:2 * n]
        first_sem = 2 * n + len(extra)
        send_sems, recv_sems = refs[first_sem:first_sem + n], refs[first_sem + n:first_sem + 2 * n]
        token = refs[-1]
        for cp in _early_copies(src_refs, land_refs, send_sems, recv_sems, rows, gather):
            cp.start()
        token[...] = jnp.zeros_like(token)

    hbm = lambda a: pltpu.HBM(a.shape, a.dtype)
    outs = pl.pallas_call(
        body, name=name,
        out_shape=[pltpu.SemaphoreType.DMA(())] * (2 * n)
        + [hbm(a) for a in srcs] + [hbm(a) for a in lands] + [jax.ShapeDtypeStruct((8, LANES), F32)],
        in_specs=[_HBM] * (2 * n) + [pl.BlockSpec(memory_space=pl.ANY)] * len(extra),
        out_specs=[_SEM] * (2 * n) + [_HBM] * (2 * n) + [pl.BlockSpec(memory_space=pltpu.VMEM)],
        input_output_aliases={i: 2 * n + i for i in range(2 * n)},
        compiler_params=pltpu.CompilerParams(has_side_effects=_EFFECT),
    )(*[pltpu.with_memory_space_constraint(a, pltpu.HBM) for a in list(srcs) + lands], *extra)
    return dict(sems=outs[:2 * n], srcs=outs[2 * n:3 * n], lands=outs[3 * n:4 * n], rows=rows), outs[-1]


def early_exchange_wait(handle, after, name):
    n = len(handle["srcs"])
    rows = handle["rows"]

    def body(*refs):
        src_refs, land_refs = refs[:n], refs[n:2 * n]
        send_sems, recv_sems = refs[2 * n:3 * n], refs[3 * n:4 * n]
        x, y, c = _mesh_pos()
        for a in range(n):
            seven = pl.ds(0, 7 * rows[a])
            all_seven = pltpu.make_async_remote_copy(
                src_ref=land_refs[a].at[seven, :], dst_ref=land_refs[a].at[seven, :],
                send_sem=send_sems[a], recv_sem=recv_sems[a],
                device_id=(x, y, c), device_id_type=pl.DeviceIdType.MESH)
            all_seven.wait_send()
            all_seven.wait_recv()

    hbm = lambda a: pltpu.HBM(a.shape, a.dtype)
    ins = list(handle["srcs"]) + list(handle["lands"])
    outs = pl.pallas_call(
        body, name=name,
        out_shape=[hbm(a) for a in ins],
        in_specs=[_HBM] * (2 * n) + [_SEM] * (2 * n) + [pl.BlockSpec(memory_space=pl.ANY)],
        out_specs=[_HBM] * (2 * n),
        input_output_aliases={i: i for i in range(2 * n)},
        compiler_params=pltpu.CompilerParams(has_side_effects=_EFFECT),
    )(*ins, *handle["sems"], after)
    return outs[:n], outs[n:]


def slot_sum8(src, land, tr, name):
    rows, cols = land.shape[0] // N_DEV, land.shape[1]
    x, y, c = _mesh_pos()
    me = (4 * x + 2 * y + c).astype(jnp.int32).reshape(1)

    def body(me_ref, src_ref, land_ref, o_ref):
        acc = None
        for d in range(N_DEV):
            term = jnp.where(d == me_ref[0], src_ref[0], land_ref[d]).astype(F32)
            acc = term if acc is None else acc + term
        o_ref[...] = acc

    return pl.pallas_call(
        body, name=name,
        grid_spec=pltpu.PrefetchScalarGridSpec(
            num_scalar_prefetch=1, grid=(rows // tr,),
            in_specs=[pl.BlockSpec((1, tr, cols), lambda i, w: (w[0], i, 0)),
                      pl.BlockSpec((N_DEV, tr, cols), lambda i, w: (0, i, 0))],
            out_specs=pl.BlockSpec((tr, cols), lambda i, w: (i, 0))),
        out_shape=jax.ShapeDtypeStruct((rows, cols), F32),
        compiler_params=_params(("arbitrary",)),
    )(me, src.reshape(N_DEV, rows, cols), land.reshape(N_DEV, rows, cols))


def mm_tn_multi(a_t, bs, tt, name, out_dtype=F32):
    K, T = a_t.shape
    widths = [b.shape[1] for b in bs]
    steps = T // tt

    def body(a_ref, *rest):
        b_refs, o_ref, acc = rest[:-2], rest[-2], rest[-1]

        @pl.when(pl.program_id(0) == 0)
        def _():
            acc[...] = jnp.zeros(acc.shape, F32)

        av = a_ref[...]
        col = 0
        for b_ref, w in zip(b_refs, widths):
            acc[:, col:col + w] += jnp.dot(av, b_ref[...], preferred_element_type=F32)
            col += w

        @pl.when(pl.program_id(0) == steps - 1)
        def _():
            o_ref[...] = acc[...].astype(out_dtype)

    return pl.pallas_call(
        body, name=name, grid=(steps,),
        in_specs=[pl.BlockSpec((K, tt), lambda t: (0, t))] + [pl.BlockSpec((tt, w), lambda t: (t, 0)) for w in widths],
        out_specs=pl.BlockSpec((K, sum(widths)), lambda t: (0, 0)),
        out_shape=jax.ShapeDtypeStruct((K, sum(widths)), out_dtype),
        scratch_shapes=[pltpu.VMEM((K, sum(widths)), F32)],
        compiler_params=_params(("arbitrary",)),
    )(a_t, *bs)


def rms_fwd(x, g, tm, name, with_transpose=False):
    M, K = x.shape

    def body(x_ref, g_ref, o_ref, *t_ref):
        xv = x_ref[...]
        r = lax.rsqrt(jnp.mean(xv * xv, axis=-1, keepdims=True) + RMS_EPS)
        h = ((xv * r) * g_ref[...]).astype(BF16)
        o_ref[...] = h
        if with_transpose:
            t_ref[0][...] = h.T

    out_specs = [pl.BlockSpec((tm, K), lambda i: (i, 0))]
    out_shape = [jax.ShapeDtypeStruct((M, K), BF16)]
    if with_transpose:
        out_specs.append(pl.BlockSpec((K, tm), lambda i: (0, i)))
        out_shape.append(jax.ShapeDtypeStruct((K, M), BF16))
    outs = pl.pallas_call(
        body, name=name, grid=(M // tm,),
        in_specs=[pl.BlockSpec((tm, K), lambda i: (i, 0)), pl.BlockSpec((1, K), lambda i: (0, 0))],
        out_specs=out_specs, out_shape=out_shape,
        compiler_params=_params(("arbitrary",)),
    )(x, g)
    return outs if with_transpose else outs[0]


def rms_bwd(x, g, dh, dres, tm, name):
    M, K = x.shape
    has_res = dres is not None

    def body(*refs):
        if has_res:
            x_ref, g_ref, dh_ref, dres_ref, dx_ref, dg_ref = refs
        else:
            x_ref, g_ref, dh_ref, dx_ref, dg_ref = refs
        xv = x_ref[...]
        r = lax.rsqrt(jnp.mean(xv * xv, axis=-1, keepdims=True) + RMS_EPS)
        xn = xv * r
        dhv = dh_ref[...]
        dxn = dhv * g_ref[...]
        dx = r * (dxn - xn * jnp.mean(dxn * xn, axis=-1, keepdims=True))
        if has_res:
            dx = dx + dres_ref[...]
        dx_ref[...] = dx
        part = jnp.sum(dhv * xn, axis=0, keepdims=True)
        row = lax.broadcasted_iota(jnp.int32, (8, K), 0)
        upd = jnp.where(row == 0, part, 0.0)

        @pl.when(pl.program_id(0) == 0)
        def _():
            dg_ref[...] = upd

        @pl.when(pl.program_id(0) != 0)
        def _():
            dg_ref[...] += upd

    row_spec = pl.BlockSpec((tm, K), lambda i: (i, 0))
    ins = [x, g, dh] + ([dres] if has_res else [])
    in_specs = [row_spec, pl.BlockSpec((1, K), lambda i: (0, 0)), row_spec] + ([row_spec] if has_res else [])
    return pl.pallas_call(
        body, name=name, grid=(M // tm,),
        in_specs=in_specs,
        out_specs=[row_spec, pl.BlockSpec((8, K), lambda i: (0, 0))],
        out_shape=[jax.ShapeDtypeStruct((M, K), F32), jax.ShapeDtypeStruct((8, K), F32)],
        compiler_params=_params(("arbitrary",)),
    )(*ins)


def mm_nn(a, b, tm, tn, name, token=None):
    M, K = a.shape
    N = b.shape[1]
    extra = [] if token is None else [token]

    def body(a_ref, b_ref, *rest):
        rest[-1][...] = jnp.dot(a_ref[...], b_ref[...], preferred_element_type=F32)

    return pl.pallas_call(
        body, name=name, grid=(N // tn, M // tm),
        in_specs=[pl.BlockSpec((tm, K), lambda j, i: (i, 0)), pl.BlockSpec((K, tn), lambda j, i: (0, j))]
        + [pl.BlockSpec(t.shape, lambda j, i: (0, 0)) for t in extra],
        out_specs=pl.BlockSpec((tm, tn), lambda j, i: (i, j)),
        out_shape=jax.ShapeDtypeStruct((M, N), F32),
        compiler_params=_params(("arbitrary", "arbitrary")),
    )(a, b, *extra)


def mm_nt(a, b, tm, tk, name):
    M, K = a.shape
    N = b.shape[0]

    def body(a_ref, b_ref, o_ref):
        part = lax.dot_general(a_ref[...], b_ref[...], (((1,), (1,)), ((), ())), preferred_element_type=F32)

        @pl.when(pl.program_id(1) == 0)
        def _():
            o_ref[...] = part

        @pl.when(pl.program_id(1) != 0)
        def _():
            o_ref[...] += part

    return pl.pallas_call(
        body, name=name, grid=(M // tm, K // tk),
        in_specs=[pl.BlockSpec((tm, tk), lambda i, k: (i, k)), pl.BlockSpec((N, tk), lambda i, k: (0, k))],
        out_specs=pl.BlockSpec((tm, N), lambda i, k: (i, 0)),
        out_shape=jax.ShapeDtypeStruct((M, N), F32),
        compiler_params=_params(("arbitrary", "arbitrary")),
    )(a, b)


def in_proj_bwd_rms(pieces, w, x, g, dres, tm, token):
    M, N = x.shape

    def body(*refs):
        n = len(pieces)
        p_refs, w_ref, x_ref, g_ref, dres_ref, _, dx_ref, dg_ref = refs[:n], *refs[n:]
        dh = None
        for p_ref, (arr, col) in zip(p_refs, pieces):
            part = lax.dot_general(p_ref[...], w_ref[:, col:col + arr.shape[1]], (((1,), (1,)), ((), ())),
                                   preferred_element_type=F32)
            dh = part if dh is None else dh + part
        xv = x_ref[...]
        r = lax.rsqrt(jnp.mean(xv * xv, axis=-1, keepdims=True) + RMS_EPS)
        xn = xv * r
        dxn = dh * g_ref[...]
        dx_ref[...] = r * (dxn - xn * jnp.mean(dxn * xn, axis=-1, keepdims=True)) + dres_ref[...]
        row = lax.broadcasted_iota(jnp.int32, (8, N), 0)
        upd = jnp.where(row == 0, jnp.sum(dh * xn, axis=0, keepdims=True), 0.0)

        @pl.when(pl.program_id(0) == 0)
        def _():
            dg_ref[...] = upd

        @pl.when(pl.program_id(0) != 0)
        def _():
            dg_ref[...] += upd

    row_spec = pl.BlockSpec((tm, N), lambda i: (i, 0))
    return pl.pallas_call(
        body, name="in_proj_bwd", grid=(M // tm,),
        in_specs=[pl.BlockSpec((tm, arr.shape[1]), lambda i: (i, 0)) for arr, _ in pieces]
        + [pl.BlockSpec(w.shape, lambda i: (0, 0)), row_spec, pl.BlockSpec((1, N), lambda i: (0, 0)), row_spec,
           pl.BlockSpec(token.shape, lambda i: (0, 0))],
        out_specs=[row_spec, pl.BlockSpec((8, N), lambda i: (0, 0))],
        out_shape=[jax.ShapeDtypeStruct((M, N), F32), jax.ShapeDtypeStruct((8, N), F32)],
        compiler_params=_params(("arbitrary",)),
    )(*[arr for arr, _ in pieces], w, x, g, dres, token)


def _log_sigmoid(z):
    return jnp.minimum(z, 0.0) - jnp.log(1.0 + jnp.exp(-jnp.abs(z)))


def _tri(n, lower):
    r = lax.broadcasted_iota(jnp.int32, (n, n), 0)
    c = lax.broadcasted_iota(jnp.int32, (n, n), 1)
    return jnp.where((r >= c) if lower else (r <= c), 1.0, 0.0).astype(F32)


def fox_gate(proj3, b_pad):
    B, S, _ = proj3.shape
    nblk = S // TK

    def body(f_ref, b_ref, o_ref):
        tri = _tri(TK, True)
        carry = jnp.zeros((1, LANES), F32)
        for n in range(nblk):
            z = f_ref[0, n * TK:(n + 1) * TK, :] + b_ref[...]
            logf = _log_sigmoid(z)
            cs = jnp.dot(tri, logf, preferred_element_type=F32, precision=lax.Precision.HIGHEST) + carry
            carry = cs[TK - 1:TK, :]
            o_ref[0, n * TK:(n + 1) * TK, :] = -cs

    return pl.pallas_call(
        body, name="fox_gate", grid=(B,),
        in_specs=[pl.BlockSpec((1, S, LANES), lambda b: (b, 0, P_FLOG // LANES)),
                  pl.BlockSpec((1, LANES), lambda b: (0, 0))],
        out_specs=pl.BlockSpec((1, S, LANES), lambda b: (b, 0, 0)),
        out_shape=jax.ShapeDtypeStruct((B, S, LANES), F32),
        compiler_params=_params(("arbitrary",)),
    )(proj3, b_pad)


def fox_gate_bwd(drow, dneg, proj3, b_pad):
    B, S, _ = proj3.shape
    nblk = S // TK

    def body(d_ref, r_ref, f_ref, b_ref, o_ref, db_ref):
        tri = _tri(TK, False)
        lane = lax.broadcasted_iota(jnp.int32, (TK, LANES), 1)
        carry = jnp.zeros((1, LANES), F32)
        dbsum = jnp.zeros((1, LANES), F32)
        for n in reversed(range(nblk)):
            dk_side = None
            for hp in range(FOX_HEADS // 2):
                two = jnp.where(lane < 2, r_ref[0, n * TK:(n + 1) * TK, hp * LANES:(hp + 1) * LANES], 0.0)
                two = pltpu.roll(two, 2 * hp, 1) if hp else two
                dk_side = two if dk_side is None else dk_side + two
            dc = jnp.where(lane < FOX_HEADS, d_ref[0, :, n * TK:(n + 1) * TK].T - dk_side, 0.0)
            rs = jnp.dot(tri, dc, preferred_element_type=F32, precision=lax.Precision.HIGHEST) + carry
            carry = rs[0:1, :]
            z = f_ref[0, n * TK:(n + 1) * TK, :] + b_ref[...]
            dz = rs * (1.0 / (1.0 + jnp.exp(z)))
            o_ref[0, n * TK:(n + 1) * TK, :] = dz.astype(BF16)
            dbsum = dbsum + jnp.sum(dz, axis=0, keepdims=True)
        row = lax.broadcasted_iota(jnp.int32, (8, LANES), 0)
        upd = jnp.where(row == 0, dbsum, 0.0)

        @pl.when(pl.program_id(0) == 0)
        def _():
            db_ref[...] = upd

        @pl.when(pl.program_id(0) != 0)
        def _():
            db_ref[...] += upd

    return pl.pallas_call(
        body, name="fox_gate_bwd", grid=(B,),
        in_specs=[pl.BlockSpec((1, LANES, S), lambda b: (b, 0, 0)),
                  pl.BlockSpec((1, S, FOX_W), lambda b: (b, 0, 0)),
                  pl.BlockSpec((1, S, LANES), lambda b: (b, 0, P_FLOG // LANES)),
                  pl.BlockSpec((1, LANES), lambda b: (0, 0))],
        out_specs=[pl.BlockSpec((1, S, LANES), lambda b: (b, 0, 0)), pl.BlockSpec((8, LANES), lambda b: (0, 0))],
        out_shape=[jax.ShapeDtypeStruct((B, S, LANES), BF16), jax.ShapeDtypeStruct((8, LANES), F32)],
        compiler_params=_params(("arbitrary",)),
    )(drow, dneg, proj3, b_pad)


def _rope_tables(S):
    half = ROPE_DIM // 2
    f32 = np.float32
    pos = np.arange(S, dtype=f32)
    inv_freq = f32(1.0) / np.power(f32(ROPE_THETA), np.arange(0, ROPE_DIM, 2, dtype=f32) / f32(ROPE_DIM)).astype(f32)
    ang = (pos[:, None] * inv_freq[None, :]).astype(f32).astype(np.float64)
    cos, sin = np.cos(ang).astype(f32), np.sin(ang).astype(f32)
    one = np.ones((S, HEAD_DIM - ROPE_DIM), f32)
    zero = np.zeros((S, HEAD_DIM - ROPE_DIM), f32)
    zh = np.zeros((S, half), f32)
    c = np.concatenate([cos, cos, one], axis=1)
    s1 = np.concatenate([-sin, zh, zero], axis=1)
    s2 = np.concatenate([zh, sin, zero], axis=1)
    return tuple(jnp.asarray(np.concatenate([t, t], axis=1)) for t in (c, s1, s2))


_HALF_ROPE = ROPE_DIM // 2


def _rope(t, c, s1, s2):
    return t * c + pltpu.roll(t, LANES - _HALF_ROPE, 1) * s1 + pltpu.roll(t, _HALF_ROPE, 1) * s2


def _rope_bwd(d, c, s1, s2):
    return d * c + pltpu.roll(d * s1, _HALF_ROPE, 1) + pltpu.roll(d * s2, LANES - _HALF_ROPE, 1)


def _scale_parts(scale):
    m, _ = math.frexp(scale)
    return (scale, None) if m == 0.5 else (None, scale)


def _log_masks(S, kind):
    nd = 1 if kind == "causal" else S // TQ
    a = np.arange(TQ)[:, None]
    b = np.arange(TK)[None, :]
    out = np.zeros((nd, TQ, TK), np.float32)
    for d in range(nd):
        delta = d * TQ + a - b
        if kind == "causal":
            m = (delta >= 0).astype(np.float64)
        else:
            m = sum(((delta >= 0) & (delta % dil == 0) & (delta <= w)).astype(np.float64) for w, dil in DILATIONS)
        out[d] = np.where(m > 0, np.log(np.maximum(m, 1.0)), NEG_INF)
    return jnp.asarray(out)


def _attn_setup(kind):
    pair = kind != "mem"
    e_dim = HEAD_DIM if pair else MEM_HEAD_DIM
    q_fold, s_scale = _scale_parts(1.0 / math.sqrt(e_dim))
    return dict(pair=pair, col0={"fox": P_FOX, "dil": P_DIL, "mem": P_MQ}[kind],
                n_blocks=FOX_HEADS // 2 if pair else MEM_HEADS, q_fold=q_fold, s_scale=s_scale,
                nh=2 if pair else 1)


def _store_stacked(cfg, lane, dst, n, val):
    nh = cfg["nh"]
    R = nh * TQ
    if nh == 1:
        dst[pl.ds(pl.multiple_of(n * R, R), TQ), :] = val
        return
    for hh in range(nh):
        hmask = (lane >= HEAD_DIM * hh) & (lane < HEAD_DIM * (hh + 1))
        dst[pl.ds(pl.multiple_of(n * R + hh * TQ, TQ), TQ), :] = jnp.where(hmask, val, jnp.zeros_like(val))


def _cat(parts, axis):
    return parts[0] if len(parts) == 1 else jnp.concatenate(parts, axis=axis)


def _log_masks_t(S, kind):
    return jnp.swapaxes(_log_masks(S, kind), 1, 2)


def _head_rows(hh, pair):
    row = lax.broadcasted_iota(jnp.int32, (LANES, 1), 0)
    if not pair:
        return row >= 0
    return (row >= HEAD_DIM * hh) & (row < HEAD_DIM * (hh + 1))


def _attn_t_inputs(kind, src, S, negc_cols, mask, rope, kv):
    cfg = _attn_setup(kind)
    col0 = cfg["col0"]
    ins, in_specs = [], []
    if cfg["pair"]:
        ins.append(src)
        in_specs.append(pl.BlockSpec((1, S, PAIR_W), lambda b, h: (b, 0, col0 // PAIR_W + h)))
    else:
        ins += [src, kv, kv]
        in_specs += [pl.BlockSpec((1, S, LANES), lambda b, h: (b, 0, col0 // LANES + h)),
                     pl.BlockSpec((1, MEM_LEN, LANES), lambda b, h: (b, 0, h)),
                     pl.BlockSpec((1, MEM_LEN, LANES), lambda b, h: (b, 0, MEM_HEADS + h))]
    if negc_cols is not None:
        ins.append(negc_cols)
        in_specs.append(pl.BlockSpec((1, S, LANES), lambda b, h: (b, 0, 0)))
    if mask is not None:
        ins.append(mask)
        in_specs.append(pl.BlockSpec(mask.shape, lambda b, h: (0, 0, 0)))
    if rope is not None:
        ins += list(rope)
        in_specs += [pl.BlockSpec((S, LANES), lambda b, h: (0, 0))] * 3
    return ins, in_specs


def _attn_t_prep(cfg, refs, S, Sk, *, qT2s, ks, q2s=None, vs=None, vTs=None, kTs=None, nb=None):
    pair, nh = cfg["pair"], cfg["nh"]
    lane = lax.broadcasted_iota(jnp.int32, (1, LANES), 1)
    rope_refs = refs["rope"]

    def prep_q(n, _):
        rows = pl.ds(pl.multiple_of(n * TQ, TQ), TQ)
        q = refs["load_q"](rows)
        if rope_refs is not None:
            q = _rope(q, *[t[rows, :] for t in rope_refs])
        if cfg["q_fold"] is not None:
            q = q * cfg["q_fold"]
        qb = q.astype(BF16)
        if q2s is not None:
            _store_stacked(cfg, lane, q2s, n, qb)
        qtb = qb.T
        for hh in range(nh):
            qT2s[n, :, hh * TQ:(hh + 1) * TQ] = jnp.where(_head_rows(hh, pair), qtb, jnp.zeros_like(qtb))
        return 0

    def prep_kv(n, _):
        rows = pl.ds(pl.multiple_of(n * TK, TK), TK)
        k, v = refs["load_kv"](rows)
        if rope_refs is not None:
            k = _rope(k, *[t[rows, :] for t in rope_refs])
        kb = k.astype(BF16)
        vb = v.astype(BF16)
        ks[rows, :] = kb
        if vs is not None:
            vs[rows, :] = vb
        if vTs is not None:
            vTs[n] = vb.T
        if kTs is not None:
            kTs[n] = kb.T
        if nb is not None:
            blk = refs["negc"][0, rows, :]
            for hh in range(nh):
                h = 2 * refs["block"] + hh
                col = jnp.sum(jnp.where(lane == h, blk, 0.0), axis=1, keepdims=True)
                nb[hh, rows, :] = jnp.broadcast_to(col, (TK, LANES))
        return 0

    lax.fori_loop(0, S // TQ, prep_q, 0)
    lax.fori_loop(0, Sk // TK, prep_kv, 0)


def _raw_scores_t(cfg, k, qT2):
    sT = jnp.dot(k, qT2, preferred_element_type=F32)
    if cfg["s_scale"] is not None:
        sT = sT * cfg["s_scale"]
    return sT


def _bias_mask_t(cfg, sT, nb, mask_ref, kc, midx):
    nh = cfg["nh"]
    if nb is None and midx is None:
        return sT
    parts = []
    for hh in range(nh):
        t = sT[:, hh * TQ:(hh + 1) * TQ]
        if nb is not None:
            t = t + jnp.concatenate([nb[hh, kc, :]] * (TQ // LANES), axis=1)
        if midx is not None:
            t = t + mask_ref[midx]
        parts.append(t)
    return _cat(parts, 1)


def _tile_walk(kind, nq, nk):
    if kind == "mem":
        return nq * nk, (lambda i, j: (jnp.where(j < nk - 1, i, i + 1), jnp.where(j < nk - 1, j + 1, 0))), None
    nxt = lambda i, j: (jnp.where(j < i, i, i + 1), jnp.where(j < i, j + 1, 0))
    if kind == "fox":
        return nq * (nq + 1) // 2, nxt, (lambda i, j: jnp.where(j == i, 0, 1))
    return nq * (nq + 1) // 2, nxt, (lambda i, j: i - j)


def attn_fwd(kind, src, S, *, negc_cols=None, mask=None, rope=None, kv=None):
    B = src.shape[0]
    cfg = _attn_setup(kind)
    pair, nh = cfg["pair"], cfg["nh"]
    Sk = S if pair else MEM_LEN
    has_bias, has_rope = negc_cols is not None, rope is not None
    R = nh * TQ
    nq, nk = S // TQ, Sk // TK
    n_pairs, successor, mask_index = _tile_walk(kind, nq, nk)
    assert n_pairs % 2 == 0

    def body(*refs):
        refs = list(refs)
        if pair:
            qkv_ref = refs.pop(0)
            load_q = lambda rows: qkv_ref[0, rows, 0:LANES]
            load_kv = lambda rows: (qkv_ref[0, rows, LANES:2 * LANES], qkv_ref[0, rows, 2 * LANES:3 * LANES])
        else:
            q_ref, k_ref, v_ref = refs.pop(0), refs.pop(0), refs.pop(0)
            load_q = lambda rows: q_ref[0, rows, :]
            load_kv = lambda rows: (k_ref[0, rows, :], v_ref[0, rows, :])
        negc_ref = refs.pop(0) if has_bias else None
        mask_ref = refs.pop(0) if mask is not None else None
        rope_refs = [refs.pop(0) for _ in range(3)] if has_rope else None
        o_ref, lse_ref, qT2s, ks, vTs, s_a, s_b, p_a, p_b, acc_all, m_all, l_all = refs[:12]
        nb = refs[12] if has_bias else None
        _attn_t_prep(cfg, dict(load_q=load_q, load_kv=load_kv, rope=rope_refs, negc=negc_ref,
                               block=pl.program_id(1)), S, Sk, qT2s=qT2s, ks=ks, vTs=vTs, nb=nb)

        def cols(j):
            return pl.ds(pl.multiple_of(j * TK, TK), TK)

        def park(i, m, l, accT):
            acc_all[i] = accT
            m_all[pl.ds(i, 1), :] = m
            l_all[pl.ds(i, 1), :] = l

        def finish(i, _):
            l = l_all[pl.ds(i, 1), :]
            oT2 = acc_all[i] / l
            oT = jnp.where(_head_rows(0, True), oT2[:, 0:TQ], oT2[:, TQ:2 * TQ]) if pair else oT2
            o_ref[0, pl.ds(pl.multiple_of(i * TQ, TQ), TQ), :] = oT.T
            lse_ref[0, 0, pl.ds(i, 1), :] = m_all[pl.ds(i, 1), :] + jnp.log(l)
            return 0

        def half(i, j, i_prev, j_prev, s_cur, s_next, p_cur, p_prev, m, l, accT):
            i_n, j_n = successor(i, j)
            acc_full = accT + jnp.dot(vTs[j_prev], p_prev[...], preferred_element_type=F32)
            s_next[...] = _raw_scores_t(cfg, ks[cols(j_n), :], qT2s[jnp.minimum(i_n, nq - 1)])
            park(i_prev, m, l, acc_full)
            first = j == 0
            m = jnp.where(first, NEG_INF, m)
            l = jnp.where(first, 0.0, l)
            sT = _bias_mask_t(cfg, s_cur[...], nb, mask_ref, cols(j), None if mask_index is None else mask_index(i, j))
            m_new = jnp.maximum(m, jnp.max(sT, axis=0, keepdims=True))
            p = jnp.exp(sT - m_new)
            alpha = jnp.exp(m - m_new)
            p_cur[...] = p.astype(BF16)
            return i_n, j_n, i, j, m_new, alpha * l + jnp.sum(p, axis=0, keepdims=True), acc_full * alpha

        def two(_, carry):
            i, j, i_prev, j_prev, m, l, accT = carry
            i, j, i_prev, j_prev, m, l, accT = half(i, j, i_prev, j_prev, s_a, s_b, p_a, p_b, m, l, accT)
            return half(i, j, i_prev, j_prev, s_b, s_a, p_b, p_a, m, l, accT)

        s_a[...] = _raw_scores_t(cfg, ks[cols(0), :], qT2s[0])
        p_b[...] = jnp.zeros((TK, R), BF16)
        zero = jnp.int32(0)
        init = (zero, zero, zero, zero, jnp.full((1, R), NEG_INF, F32), jnp.ones((1, R), F32),
                jnp.zeros((LANES, R), F32))
        _, _, i_prev, j_prev, m, l, accT = lax.fori_loop(0, n_pairs // 2, two, init)
        park(i_prev, m, l, accT + jnp.dot(vTs[j_prev], p_b[...], preferred_element_type=F32))
        lax.fori_loop(0, nq, finish, 0)

    ins, in_specs = _attn_t_inputs(kind, src, S, negc_cols, mask, rope, kv)
    W = cfg["n_blocks"] * LANES
    scratch = [pltpu.VMEM((nq, LANES, R), BF16), pltpu.VMEM((Sk, LANES), BF16), pltpu.VMEM((nk, LANES, TK), BF16),
               pltpu.VMEM((TK, R), F32), pltpu.VMEM((TK, R), F32), pltpu.VMEM((TK, R), BF16), pltpu.VMEM((TK, R), BF16),
               pltpu.VMEM((nq, LANES, R), F32), pltpu.VMEM((nq, R), F32), pltpu.VMEM((nq, R), F32)]
    if has_bias:
        scratch.append(pltpu.VMEM((nh, Sk, LANES), F32))
    return pl.pallas_call(
        body, name=kind + "_attn_fwd", grid=(B, cfg["n_blocks"]),
        in_specs=in_specs,
        out_specs=[pl.BlockSpec((1, S, LANES), lambda b, h: (b, 0, h)),
                   pl.BlockSpec((1, 1, nq, R), lambda b, h: (b, h, 0, 0))],
        out_shape=[jax.ShapeDtypeStruct((B, S, W), F32), jax.ShapeDtypeStruct((B, cfg["n_blocks"], nq, R), F32)],
        scratch_shapes=scratch,
        compiler_params=_params(("arbitrary", "arbitrary")),
    )(*ins)


def attn_bwd(kind, src, do, o, lse, S, *, negc_cols=None, mask=None, rope=None, kv=None, token=None):
    B = src.shape[0]
    cfg = _attn_setup(kind)
    pair, nh, s_scale, q_fold = cfg["pair"], cfg["nh"], cfg["s_scale"], cfg["q_fold"]
    Sk = S if pair else MEM_LEN
    has_bias, has_rope = negc_cols is not None, rope is not None
    R = nh * TQ
    nq, nk = S // TQ, Sk // TK
    n_pairs, successor, mask_index = _tile_walk(kind, nq, nk)
    assert n_pairs % 2 == 0

    def body(*refs):
        refs = list(refs)
        if pair:
            qkv_ref = refs.pop(0)
            load_q = lambda rows: qkv_ref[0, rows, 0:LANES]
            load_kv = lambda rows: (qkv_ref[0, rows, LANES:2 * LANES], qkv_ref[0, rows, 2 * LANES:3 * LANES])
        else:
            q_ref, k_ref, v_ref = refs.pop(0), refs.pop(0), refs.pop(0)
            load_q = lambda rows: q_ref[0, rows, :]
            load_kv = lambda rows: (k_ref[0, rows, :], v_ref[0, rows, :])
        negc_ref = refs.pop(0) if has_bias else None
        mask_ref = refs.pop(0) if mask is not None else None
        rope_refs = [refs.pop(0) for _ in range(3)] if has_rope else None
        do_ref, o_ref, lse_ref = refs.pop(0), refs.pop(0), refs.pop(0)
        if token is not None:
            refs.pop(0)
        if pair:
            dqkv_ref = refs.pop(0)
            dneg_ref = refs.pop(0) if has_bias else None
            drow_ref = refs.pop(0) if has_bias else None
        else:
            dq_ref, dk_ref, dv_ref = refs.pop(0), refs.pop(0), refs.pop(0)
        qT2s, ks, q2s, vs, kTs, doT2s, do2s, delta_s, dk_acc, dv_acc = refs[:10]
        bufs_a, bufs_b, dq_all = refs[10:14], refs[14:18], refs[18]
        nb, dneg_acc, drow_all = (refs[19], refs[20], refs[21]) if has_bias else (None, None, None)
        lane = lax.broadcasted_iota(jnp.int32, (1, LANES), 1)
        _attn_t_prep(cfg, dict(load_q=load_q, load_kv=load_kv, rope=rope_refs, negc=negc_ref,
                               block=pl.program_id(1)), S, Sk,
                     qT2s=qT2s, ks=ks, q2s=q2s, vs=vs, kTs=kTs, nb=nb)

        def prep_do(n, _):
            rows = pl.ds(pl.multiple_of(n * TQ, TQ), TQ)
            dob = do_ref[0, rows, :].astype(BF16)
            _store_stacked(cfg, lane, do2s, n, dob)
            doT = dob.astype(F32).T
            prodT = doT * o_ref[0, rows, :].T
            doTb = doT.astype(BF16)
            for hh in range(nh):
                hm = _head_rows(hh, pair)
                doT2s[n, :, hh * TQ:(hh + 1) * TQ] = jnp.where(hm, doTb, jnp.zeros_like(doTb))
                delta_s[pl.ds(n, 1), hh * TQ:(hh + 1) * TQ] = jnp.sum(jnp.where(hm, prodT, 0.0), axis=0, keepdims=True)
            return 0

        def zero_kv(n, _):
            rows = pl.ds(pl.multiple_of(n * TK, TK), TK)
            dk_acc[rows, :] = jnp.zeros((TK, LANES), F32)
            dv_acc[rows, :] = jnp.zeros((TK, LANES), F32)
            if has_bias:
                for hh in range(nh):
                    dneg_acc[hh, rows, :] = jnp.zeros((TK, LANES), F32)
            return 0

        lax.fori_loop(0, nq, prep_do, 0)
        lax.fori_loop(0, nk, zero_kv, 0)

        def cols(j):
            return pl.ds(pl.multiple_of(j * TK, TK), TK)

        def rows2(i):
            return pl.ds(pl.multiple_of(i * R, R), R)

        def park(i, dqT2, drow):
            dq_all[i] = dqT2
            if has_bias:
                drow_all[pl.ds(i, 1), :] = drow

        def half(i, j, i_prev, j_prev, cur, nxt_bufs, prv, dqT2, drow):
            s_cur, dp_cur, pb_cur, dsb_cur = cur
            s_next, dp_next = nxt_bufs[0], nxt_bufs[1]
            pb_prev, dsb_prev = prv[2], prv[3]
            i_n, j_n = successor(i, j)
            first = j == 0
            if has_bias:
                drow_all[pl.ds(i_prev, 1), :] = drow
            drow = jnp.where(first, 0.0, drow)
            kc = cols(j)
            sT = _bias_mask_t(cfg, s_cur[...], nb, mask_ref, kc, None if mask_index is None else mask_index(i, j))
            pT = jnp.exp(sT - lse_ref[0, 0, pl.ds(i, 1), :])
            dsT = pT * (dp_cur[...] - delta_s[pl.ds(i, 1), :])
            if has_bias:
                drow = drow + jnp.sum(dsT, axis=0, keepdims=True)
                for hh in range(nh):
                    part = dsT[:, hh * TQ:hh * TQ + LANES]
                    for t in range(1, TQ // LANES):
                        part = part + dsT[:, hh * TQ + t * LANES:hh * TQ + (t + 1) * LANES]
                    dneg_acc[hh, kc, :] += part
            if s_scale is not None:
                dsT = dsT * s_scale
            pb_cur[...] = pT.astype(BF16)
            dsb_cur[...] = dsT.astype(BF16)
            kp = cols(j_prev)
            dv_acc[kp, :] += jnp.dot(pb_prev[...], do2s[rows2(i_prev), :], preferred_element_type=F32)
            dk_acc[kp, :] += jnp.dot(dsb_prev[...], q2s[rows2(i_prev), :], preferred_element_type=F32)
            dq_full = dqT2 + jnp.dot(kTs[j_prev], dsb_prev[...], preferred_element_type=F32)
            dq_all[i_prev] = dq_full
            dqT2 = jnp.where(first, 0.0, dq_full)
            i_nc = jnp.minimum(i_n, nq - 1)
            kn = cols(j_n)
            s_next[...] = _raw_scores_t(cfg, ks[kn, :], qT2s[i_nc])
            dp_next[...] = jnp.dot(vs[kn, :], doT2s[i_nc], preferred_element_type=F32)
            return i_n, j_n, i, j, dqT2, drow

        def two(_, carry):
            i, j, i_prev, j_prev, dqT2, drow = carry
            i, j, i_prev, j_prev, dqT2, drow = half(i, j, i_prev, j_prev, bufs_a, bufs_b, bufs_b, dqT2, drow)
            return half(i, j, i_prev, j_prev, bufs_b, bufs_a, bufs_a, dqT2, drow)

        bufs_a[0][...] = _raw_scores_t(cfg, ks[cols(0), :], qT2s[0])
        bufs_a[1][...] = jnp.dot(vs[cols(0), :], doT2s[0], preferred_element_type=F32)
        bufs_b[2][...] = jnp.zeros((TK, R), BF16)
        bufs_b[3][...] = jnp.zeros((TK, R), BF16)
        zero = jnp.int32(0)
        init = (zero, zero, zero, zero, jnp.zeros((LANES, R), F32), jnp.zeros((1, R), F32))
        _, _, i_prev, j_prev, dqT2, drow = lax.fori_loop(0, n_pairs // 2, two, init)
        kp = cols(j_prev)
        dv_acc[kp, :] += jnp.dot(bufs_b[2][...], do2s[rows2(i_prev), :], preferred_element_type=F32)
        dk_acc[kp, :] += jnp.dot(bufs_b[3][...], q2s[rows2(i_prev), :], preferred_element_type=F32)
        park(i_prev, dqT2 + jnp.dot(kTs[j_prev], bufs_b[3][...], preferred_element_type=F32), drow)

        def fin_q(i, _):
            rows = pl.ds(pl.multiple_of(i * TQ, TQ), TQ)
            dqT2 = dq_all[i]
            dqT = jnp.where(_head_rows(0, True), dqT2[:, 0:TQ], dqT2[:, TQ:2 * TQ]) if pair else dqT2
            dq = dqT.T
            if q_fold is not None:
                dq = dq * q_fold
            if has_rope:
                dq = _rope_bwd(dq, *[t[rows, :] for t in rope_refs])
            if pair:
                dqkv_ref[0, rows, 0:LANES] = dq.astype(BF16)
            else:
                dq_ref[0, rows, :] = dq.astype(BF16)
            if has_bias:
                drow_ref[0, 0, pl.ds(i, 1), :] = drow_all[pl.ds(i, 1), :]
            return 0

        lax.fori_loop(0, nq, fin_q, 0)

        def fin_kv(n, _):
            rows = pl.ds(pl.multiple_of(n * TK, TK), TK)
            dk = dk_acc[rows, :]
            if has_rope:
                dk = _rope_bwd(dk, *[t[rows, :] for t in rope_refs])
            if pair:
                dqkv_ref[0, rows, LANES:2 * LANES] = dk.astype(BF16)
                dqkv_ref[0, rows, 2 * LANES:3 * LANES] = dv_acc[rows, :].astype(BF16)
            else:
                dk_ref[0, rows, :] = dk.astype(BF16)
                dv_ref[0, rows, :] = dv_acc[rows, :].astype(BF16)
            if has_bias:
                x0 = jnp.sum(dneg_acc[0, rows, :], axis=1, keepdims=True)
                x1 = jnp.sum(dneg_acc[1, rows, :], axis=1, keepdims=True)
                dneg_ref[0, rows, :] = jnp.where(lane == 0, x0, jnp.where(lane == 1, x1, 0.0))
            return 0

        lax.fori_loop(0, nk, fin_kv, 0)

    ins, in_specs = _attn_t_inputs(kind, src, S, negc_cols, mask, rope, kv)
    row_spec = pl.BlockSpec((1, S, LANES), lambda b, h: (b, 0, h))
    vec_spec = pl.BlockSpec((1, 1, nq, R), lambda b, h: (b, h, 0, 0))
    ins += [do, o, lse]
    in_specs += [row_spec, row_spec, vec_spec]
    if token is not None:
        ins.append(token)
        in_specs.append(pl.BlockSpec(token.shape, lambda b, h: (0, 0)))
    W = cfg["n_blocks"] * LANES
    if pair:
        out_specs = [pl.BlockSpec((1, S, PAIR_W), lambda b, h: (b, 0, h))]
        out_shape = [jax.ShapeDtypeStruct((B, S, 3 * W), BF16)]
        if has_bias:
            out_specs += [row_spec, vec_spec]
            out_shape += [jax.ShapeDtypeStruct((B, S, W), F32), jax.ShapeDtypeStruct((B, cfg["n_blocks"], nq, R), F32)]
    else:
        kv_spec = pl.BlockSpec((1, MEM_LEN, LANES), lambda b, h: (b, 0, h))
        out_specs = [row_spec, kv_spec, kv_spec]
        out_shape = [jax.ShapeDtypeStruct((B, S, W), BF16)] + [jax.ShapeDtypeStruct((B, MEM_LEN, W), BF16)] * 2
    scratch = [pltpu.VMEM((nq, LANES, R), BF16), pltpu.VMEM((Sk, LANES), BF16), pltpu.VMEM((nh * S, LANES), BF16),
               pltpu.VMEM((Sk, LANES), BF16), pltpu.VMEM((nk, LANES, TK), BF16), pltpu.VMEM((nq, LANES, R), BF16),
               pltpu.VMEM((nh * S, LANES), BF16), pltpu.VMEM((nq, R), F32),
               pltpu.VMEM((Sk, LANES), F32), pltpu.VMEM((Sk, LANES), F32)]
    pair_bufs = [pltpu.VMEM((TK, R), F32), pltpu.VMEM((TK, R), F32), pltpu.VMEM((TK, R), BF16), pltpu.VMEM((TK, R), BF16)]
    scratch += pair_bufs + pair_bufs + [pltpu.VMEM((nq, LANES, R), F32)]
    if has_bias:
        scratch += [pltpu.VMEM((nh, Sk, LANES), F32), pltpu.VMEM((nh, Sk, LANES), F32), pltpu.VMEM((nq, R), F32)]
    return pl.pallas_call(
        body, name=kind + "_attn_bwd", grid=(B, cfg["n_blocks"]),
        in_specs=in_specs, out_specs=out_specs, out_shape=out_shape, scratch_shapes=scratch,
        compiler_params=_params(("arbitrary", "arbitrary")),
    )(*ins)


def _sigmoid(g):
    return 1.0 / (1.0 + jnp.exp(-g))


def out_step(proj, o_fox, o_dil, o_mem, w_out, x, target, gf, tm):
    T = x.shape[0]

    def body(fg_ref, dg_ref, mg_ref, of_ref, od_ref, om_ref, w_ref, x_ref, t_ref, gf_ref,
             dx_ref, dof_ref, dod_ref, dom_ref, dfg_ref, ddg_ref, dmg_ref, gw_ref, sm_ref, gw_acc):
        branches = []
        for g_ref, o_ref in ((fg_ref, of_ref), (dg_ref, od_ref), (mg_ref, om_ref)):
            g = g_ref[...]
            sg = _sigmoid(g)
            o = o_ref[...]
            branches.append((g, sg, o))
        ymix = jnp.concatenate([(o * (g * sg)).astype(BF16) for g, sg, o in branches], axis=1)
        x2 = x_ref[...] + jnp.dot(ymix, w_ref[...], preferred_element_type=F32)
        r = lax.rsqrt(jnp.mean(x2 * x2, axis=-1, keepdims=True) + RMS_EPS)
        yn = x2 * r
        err = yn * gf_ref[...] - t_ref[...]
        loss = 0.5 * jnp.sum(jnp.sum(err * err, axis=-1, keepdims=True) / D_MODEL, axis=0, keepdims=True)
        dyf = err / D_MODEL
        dgf = jnp.sum(dyf * yn, axis=0, keepdims=True)
        dyn = dyf * gf_ref[...]
        dx2 = r * (dyn - yn * jnp.mean(dyn * yn, axis=-1, keepdims=True))
        dx_ref[...] = dx2
        dxb = dx2.astype(BF16)
        dmix = lax.dot_general(dxb, w_ref[...], (((1,), (1,)), ((), ())), preferred_element_type=F32)
        col = 0
        for (g, sg, o), do_ref, dgate_ref in zip(branches, (dof_ref, dod_ref, dom_ref), (dfg_ref, ddg_ref, dmg_ref)):
            d = dmix[:, col:col + g.shape[1]]
            col += g.shape[1]
            do_ref[...] = (d * (g * sg)).astype(BF16)
            dgate_ref[...] = (d * o * (sg * (1.0 + g * (1.0 - sg)))).astype(BF16)
        row = lax.broadcasted_iota(jnp.int32, (8, D_MODEL), 0)
        upd = jnp.where(row == 0, dgf, jnp.where(row == 1, loss, 0.0))

        @pl.when(pl.program_id(0) == 0)
        def _():
            sm_ref[...] = jnp.zeros(sm_ref.shape, F32)
            gw_acc[...] = jnp.zeros(gw_acc.shape, F32)

        sm_ref[...] += upd
        gw_acc[...] += lax.dot_general(ymix, dxb, (((0,), (0,)), ((), ())), preferred_element_type=F32)

        @pl.when(pl.program_id(0) == T // tm - 1)
        def _():
            gw_ref[...] = gw_acc[...].astype(BF16)

    def rows(w, col=0):
        return pl.BlockSpec((tm, w), lambda i: (i, col))

    return pl.pallas_call(
        body, name="out_step", grid=(T // tm,),
        in_specs=[rows(FOX_W, P_FG // FOX_W), rows(DIL_W, P_DG // DIL_W), rows(MEM_W, P_MG // MEM_W),
                  rows(FOX_W), rows(DIL_W), rows(MEM_W),
                  pl.BlockSpec((MIX_W, D_MODEL), lambda i: (0, 0)),
                  rows(D_MODEL), rows(D_MODEL), pl.BlockSpec((1, D_MODEL), lambda i: (0, 0))],
        out_specs=[rows(D_MODEL), rows(FOX_W), rows(DIL_W), rows(MEM_W), rows(FOX_W), rows(DIL_W), rows(MEM_W),
                   pl.BlockSpec((MIX_W, D_MODEL), lambda i: (0, 0)), pl.BlockSpec((8, D_MODEL), lambda i: (0, 0))],
        out_shape=[jax.ShapeDtypeStruct((T, D_MODEL), F32), jax.ShapeDtypeStruct((T, FOX_W), BF16),
                   jax.ShapeDtypeStruct((T, DIL_W), BF16), jax.ShapeDtypeStruct((T, MEM_W), BF16),
                   jax.ShapeDtypeStruct((T, FOX_W), BF16), jax.ShapeDtypeStruct((T, DIL_W), BF16),
                   jax.ShapeDtypeStruct((T, MEM_W), BF16), jax.ShapeDtypeStruct((MIX_W, D_MODEL), BF16),
                   jax.ShapeDtypeStruct((8, D_MODEL), F32)],
        scratch_shapes=[pltpu.VMEM((MIX_W, D_MODEL), F32)],
        compiler_params=_params(("arbitrary",)),
    )(proj, proj, proj, o_fox, o_dil, o_mem, w_out, x, target, gf)


def adamw(w, g, m, v, tr, name):
    lead = w.shape[:-2]
    R, C = w.shape[-2:]
    zeros = (0,) * len(lead)

    def body(w_ref, g_ref, m_ref, v_ref, d_ref, mo_ref, vo_ref):
        gv = g_ref[...]
        mn = ADAM_B1 * m_ref[...] + (1.0 - ADAM_B1) * gv
        vn = ADAM_B2 * v_ref[...] + (1.0 - ADAM_B2) * jnp.square(gv)
        m_hat = mn / (1.0 - ADAM_B1 ** ADAM_STEP)
        v_hat = vn / (1.0 - ADAM_B2 ** ADAM_STEP)
        d_ref[...] = -ADAM_LR * (m_hat / (jnp.sqrt(v_hat) + ADAM_EPS) + ADAM_WD * w_ref[...])
        mo_ref[...] = mn
        vo_ref[...] = vn

    spec = pl.BlockSpec((1,) * len(lead) + (tr, C), lambda i: zeros + (i, 0))
    return pl.pallas_call(
        body, name=name, grid=(pl.cdiv(R, tr),),
        in_specs=[spec] * 4, out_specs=[spec] * 3,
        out_shape=[jax.ShapeDtypeStruct(w.shape, F32)] * 3,
        compiler_params=_params(("arbitrary",)),
    )(w, g, m, v)


def _pad_row(v, width):
    return jnp.concatenate([v, jnp.zeros((1, width - v.shape[1]), v.dtype)], axis=1)


def local_grads(x, mem, norm_g, b_forget, mem_norm_g, final_norm_g, loss_target, w_in_p, first_token, small_weights,
                start_exchange):
    B, S, D = x.shape
    T = B * S
    xt = x.reshape(T, D)
    memt = mem.reshape(B * MEM_LEN, D)
    b_pad = _pad_row(b_forget, LANES)

    h, h_t = rms_fwd(xt, norm_g, 512, "rms_x", with_transpose=True)
    proj = mm_nn(h, w_in_p, 512, PW // 3, "in_proj", first_token)
    proj3 = proj.reshape(B, S, PW)

    negc = fox_gate(proj3, b_pad)
    causal = _log_masks_t(S, "causal")
    causal = jnp.concatenate([causal, jnp.zeros_like(causal)], axis=0)
    dilated = _log_masks_t(S, "dilated")
    rope = _rope_tables(S)

    o_fox, lse_fox = attn_fwd("fox", proj3, S, negc_cols=negc, mask=causal)
    o_dil, lse_dil = attn_fwd("dil", proj3, S, mask=dilated, rope=rope)

    w_kv, w_out = small_weights(o_dil)
    mh, mh_t = rms_fwd(memt, mem_norm_g, B * MEM_LEN, "rms_mem", with_transpose=True)
    mkv = mm_nn(mh, w_kv, B * MEM_LEN, 2 * MEM_W, "mem_kv_proj")
    mkv3 = mkv.reshape(B, MEM_LEN, 2 * MEM_W)
    o_mem, lse_mem = attn_fwd("mem", proj3, S, kv=mkv3)

    dx2, do_fox, do_dil, do_mem, dfg, ddg, dmg, g_out, small_out = out_step(
        proj, o_fox.reshape(T, FOX_W), o_dil.reshape(T, DIL_W), o_mem.reshape(T, MEM_W), w_out,
        xt, loss_target.reshape(T, D), final_norm_g.reshape(1, D), 256)

    gates = [(dfg, P_FG), (ddg, P_DG), (dmg, P_MG)]
    g_gates = mm_tn_multi(h_t, [arr for arr, _ in gates], 1024, "w_in_grad_gates", BF16)
    first, token = start_exchange([g_gates, g_out], "early_exchange_a")

    dqkv_fox, dneg, drow = attn_bwd("fox", proj3, do_fox.reshape(B, S, FOX_W), o_fox, lse_fox, S,
                                    negc_cols=negc, mask=causal, token=token)
    drow = drow.reshape(B, FOX_HEADS // 2, S // TQ, 2, TQ).transpose(0, 1, 3, 2, 4).reshape(B, FOX_HEADS, S)
    drow = jnp.pad(drow, ((0, 0), (0, LANES - FOX_HEADS), (0, 0)))
    dflog, db_part = fox_gate_bwd(drow, dneg, proj3, b_pad)
    fox = [(dqkv_fox.reshape(T, 3 * FOX_W), P_FOX), (dflog.reshape(T, LANES), P_FLOG)]
    g_fox = mm_tn_multi(h_t, [arr for arr, _ in fox], 1024, "w_in_grad_fox", BF16)
    second, token = start_exchange([g_fox], "early_exchange_b")

    (dqkv_dil,) = attn_bwd("dil", proj3, do_dil.reshape(B, S, DIL_W), o_dil, lse_dil, S, mask=dilated, rope=rope,
                           token=token)
    dil = [(dqkv_dil.reshape(T, 3 * DIL_W), P_DIL)]
    g_dil = mm_tn_multi(h_t, [arr for arr, _ in dil], 1024, "w_in_grad_dil", BF16)
    third, token = start_exchange([g_dil], "early_exchange_c")

    dmq, dmk, dmv = attn_bwd("mem", proj3, do_mem.reshape(B, S, MEM_W), o_mem, lse_mem, S, kv=mkv3, token=token)
    mq = [(dmq.reshape(T, MEM_W), P_MQ)]
    g_mq = mm_tn_multi(h_t, [arr for arr, _ in mq], 1024, "w_in_grad_mq", BF16)
    dmkv = jnp.concatenate([dmk, dmv], axis=2).reshape(B * MEM_LEN, 2 * MEM_W)
    g_kv = mm_tn_multi(mh_t, [dmkv], B * MEM_LEN, "w_kv_grad", BF16)
    fourth, token = start_exchange([g_mq, g_kv], "early_exchange_d")

    grad_x, dng = in_proj_bwd_rms(gates + fox + dil + mq, w_in_p, xt, norm_g, dx2, 256, token)
    dmh = mm_nt(dmkv, w_kv, B * MEM_LEN, D, "mem_kv_bwd")
    _, dmng = rms_bwd(memt, mem_norm_g, dmh, None, B * MEM_LEN, "rms_mem_bwd")

    small = jnp.concatenate([dng[0:1], dmng[0:1], small_out[0:1], _pad_row(db_part[0:1], D), small_out[1:2],
                             jnp.zeros((3, D), F32)], axis=0)
    early = [(first, dqkv_fox), (second, dqkv_dil), (third, dmq), (fourth, grad_x)]
    return grad_x.reshape(B, S, D), early, small


def kernel(x, mem, norm_g, w_in, b_forget, mem_norm_g, w_mem_kv, w_out, final_norm_g, loss_target, m_norm_g, m_w_in, m_b_forget, m_mem_norm_g, m_w_mem_kv, m_w_out, m_final_norm_g, v_norm_g, v_w_in, v_b_forget, v_mem_norm_g, v_w_mem_kv, v_w_out, v_final_norm_g):
    D = D_MODEL
    (w_in_full,) = weight_gather([_pack_cols(w_in).astype(BF16).reshape(w_in.shape[1], PW)])
    gather, gather_token = early_exchange_start([w_mem_kv[0].astype(BF16), w_out[0].astype(BF16)], "early_gather",
                                                gather=True, after=w_in_full)

    def small_weights(after):
        _, gathered = early_exchange_wait(gather, after, "early_gather_wait")
        return gathered

    grad_x, early, small = local_grads(
        x, mem, norm_g, b_forget, mem_norm_g, final_norm_g, loss_target, w_in_full, gather_token, small_weights,
        early_exchange_start)

    (first, after_first), (second, after_second), (third, after_third), (fourth, after_fourth) = early
    (src_gates, src_out), (land_gates, land_out) = early_exchange_wait(first, after_first, "early_wait_a")
    (src_fox,), (land_fox,) = early_exchange_wait(second, after_second, "early_wait_b")
    (src_dil,), (land_dil,) = early_exchange_wait(third, after_third, "early_wait_c")
    (src_mq, src_kv), (land_mq, land_kv) = early_exchange_wait(fourth, after_fourth, "early_wait_d")
    gates = slot_sum8(src_gates, land_gates, 128, "sum_w_in_gates")
    gw_out = slot_sum8(src_out, land_out, 256, "sum_w_out")
    fox = slot_sum8(src_fox, land_fox, 128, "sum_w_in_fox")
    dil = slot_sum8(src_dil, land_dil, 128, "sum_w_in_dil")
    mq = slot_sum8(src_mq, land_mq, 128, "sum_w_in_mq")
    gw_kv = slot_sum8(src_kv, land_kv, 128, "sum_w_kv")

    tot = small_all_reduce(small)
    gw_in = _unpack_cols(jnp.concatenate(
        [fox[:, :3 * FOX_W], gates[:, :FOX_W], dil, gates[:, FOX_W:FOX_W + DIL_W], mq,
         gates[:, FOX_W + DIL_W:], fox[:, 3 * FOX_W:]], axis=1)[None])

    loss = tot[4, 0]
    g_norm, g_mem_norm, g_final, g_b = tot[0:1], tot[1:2], tot[2], tot[3:4, :FOX_HEADS]

    def rows8(*rows):
        rows = [r.reshape(1, -1) for r in rows]
        rows = [_pad_row(r, D) for r in rows]
        return jnp.concatenate(rows + [jnp.zeros((8 - len(rows), D), F32)], axis=0)

    sw = rows8(norm_g, mem_norm_g, final_norm_g, b_forget)
    sm = rows8(m_norm_g, m_mem_norm_g, m_final_norm_g, m_b_forget)
    sv = rows8(v_norm_g, v_mem_norm_g, v_final_norm_g, v_b_forget)
    d_s, m_s, v_s = adamw(sw, tot, sm, sv, 8, "adamw_small")
    d_in, m_in, v_in = adamw(w_in, gw_in, m_w_in, v_w_in, 32, "adamw_w_in")
    d_kv, m_kv, v_kv = adamw(w_mem_kv[0], gw_kv, m_w_mem_kv[0], v_w_mem_kv[0], 128, "adamw_w_kv")
    d_out, m_out, v_out = adamw(w_out[0], gw_out, m_w_out[0], v_w_out[0], 256, "adamw_w_out")

    def small_outs(t):
        return t[0:1], t[3:4, :FOX_HEADS], t[1:2], t[2]

    grads = (g_norm, gw_in, g_b, g_mem_norm, gw_kv[None], gw_out[None], g_final)
    outs = []
    for t, big in ((d_s, (d_in, d_kv, d_out)), (m_s, (m_in, m_kv, m_out)), (v_s, (v_in, v_kv, v_out))):
        n, b, mn, f = small_outs(t)
        outs += [n, big[0], b, mn, big[1][None], big[2][None], f]
    return (loss, grad_x, *grads, *outs)
```

```python
import math

import numpy as np
import jax
import jax.numpy as jnp
from jax import lax
from jax.experimental import pallas as pl
from jax.experimental.pallas import tpu as pltpu

F32 = jnp.float32
BF16 = jnp.bfloat16

D_MODEL = 1024
HEAD_DIM = 64
FOX_HEADS = 12
DIL_HEADS = 12
MEM_HEADS = 4
MEM_HEAD_DIM = 128
MEM_LEN = 256
FOX_W = FOX_HEADS * HEAD_DIM
DIL_W = DIL_HEADS * HEAD_DIM
MEM_W = MEM_HEADS * MEM_HEAD_DIM
MIX_W = FOX_W + DIL_W + MEM_W
DILATIONS = ((128, 1), (512, 4), (2048, 16))
ROPE_THETA = 500000.0
ROPE_DIM = HEAD_DIM // 4
RMS_EPS = 1e-6
NEG_INF = -1e30
IN_W = 4 * FOX_W + FOX_HEADS + 4 * DIL_W + 2 * MEM_W

ADAM_LR = 0.001
ADAM_B1 = 0.9
ADAM_B2 = 0.999
ADAM_EPS = 1e-08
ADAM_WD = 0.01
ADAM_STEP = 10

N_DEV = 8
LANES = 128
PAIR_W = 3 * LANES
TQ = 256
TK = 256

O_FQ, O_FK, O_FV, O_FG = 0, FOX_W, 2 * FOX_W, 3 * FOX_W
O_FLOG = 4 * FOX_W
O_DQ = O_FLOG + FOX_HEADS
O_DK, O_DV, O_DG = O_DQ + DIL_W, O_DQ + 2 * DIL_W, O_DQ + 3 * DIL_W
O_MQ = O_DQ + 4 * DIL_W
O_MG = O_MQ + MEM_W
P_FOX = 0
P_FG = P_FOX + 3 * FOX_W
P_DIL = P_FG + FOX_W
P_DG = P_DIL + 3 * DIL_W
P_MQ = P_DG + DIL_W
P_MG = P_MQ + MEM_W
P_FLOG = P_MG + MEM_W
PW = P_FLOG + LANES

VMEM_LIMIT = 56 * 1024 * 1024


def _pack_pieces():
    pieces = []
    for base in (O_FQ, O_DQ):
        seg = []
        for hp in range(FOX_HEADS // 2):
            for part in range(3):
                seg.append((base + part * FOX_W + hp * LANES, LANES))
        pieces.append(seg)
    fox, dil = pieces
    return fox + [(O_FG, FOX_W)] + dil + [(O_DG, DIL_W), (O_MQ, MEM_W), (O_MG, MEM_W), (O_FLOG, FOX_HEADS)]


def _pack_cols(w):
    parts = [w[..., s:s + n] for s, n in _pack_pieces()]
    parts.append(jnp.zeros(w.shape[:-1] + (LANES - FOX_HEADS,), w.dtype))
    return jnp.concatenate(parts, axis=-1)


def _unpack_cols(g):
    runs = []
    pos = 0
    for s, n in _pack_pieces():
        runs.append((s, n, pos))
        pos += n
    runs.sort()
    return jnp.concatenate([g[..., p:p + n] for s, n, p in runs], axis=-1)


def _params(sem=None, **kw):
    return pltpu.CompilerParams(dimension_semantics=sem, vmem_limit_bytes=VMEM_LIMIT, **kw)


def _mesh_pos():
    return lax.axis_index("x"), lax.axis_index("y"), lax.axis_index("c")


def _flip(v, d):
    return 1 - v if d else v


_RELATIONS = [(dx, dy, dc) for dx in (0, 1) for dy in (0, 1) for dc in (0, 1)][1:]


def weight_gather(shards):
    n_arr = len(shards)
    rows = [s.shape[0] for s in shards]

    def body(*refs):
        in_refs = refs[:n_arr]
        out_refs = refs[n_arr:2 * n_arr]
        send_sems, recv_sems, local_sems = refs[2 * n_arr:]
        x, y, c = _mesh_pos()
        me, sibling = (x, y, c), (x, y, 1 - c)
        x_nbr, y_nbr, diag = (1 - x, y, c), (x, 1 - y, c), (1 - x, 1 - y, c)
        north = c == 1
        relay_from = (jnp.where(north, 1 - x, x), jnp.where(north, y, 1 - y), c)
        relay_to = (jnp.where(north, x, 1 - x), jnp.where(north, 1 - y, y), c)
        k_from = jnp.where(north, 1, 2)
        k_to = 3 - k_from

        def block(a, pos):
            px, py, pc = pos
            return out_refs[a].at[pl.ds((4 * px + 2 * py + pc) * rows[a], rows[a]), :]

        def copy(a, k, blk, to, src=None):
            return pltpu.make_async_remote_copy(
                src_ref=block(a, blk) if src is None else src, dst_ref=block(a, blk),
                send_sem=send_sems.at[a, k], recv_sem=recv_sems.at[a, k],
                device_id=to, device_id_type=pl.DeviceIdType.MESH)

        started = []
        mine = []
        for a in range(n_arr):
            cp = pltpu.make_async_copy(in_refs[a], block(a, me), local_sems.at[a])
            cp.start()
            mine.append(cp)
            first = [copy(a, 0, me, sibling, src=in_refs[a]), copy(a, 1, me, x_nbr, src=in_refs[a]),
                     copy(a, 2, me, y_nbr, src=in_refs[a])]
            for cp in first:
                cp.start()
            started += first
        for a in range(n_arr):
            copy(a, k_from, relay_from, me).wait_recv()
            second_hop = copy(a, 3, relay_from, relay_to)
            second_hop.start()
            passed = copy(a, 3 + k_from, relay_from, sibling)
            passed.start()
            started += [second_hop, passed]
        for a in range(n_arr):
            copy(a, k_to, relay_to, me).wait_recv()
            passed = copy(a, 3 + k_to, relay_to, sibling)
            passed.start()
            started.append(passed)
        for a in range(n_arr):
            copy(a, 3, diag, me).wait_recv()
            passed = copy(a, 6, diag, sibling)
            passed.start()
            started.append(passed)
        for a in range(n_arr):
            copy(a, 0, sibling, me).wait_recv()
            for k, chip in ((4, x_nbr), (5, y_nbr), (6, diag)):
                copy(a, k, (chip[0], chip[1], 1 - c), me).wait_recv()
        for cp in started:
            cp.wait_send()
        for cp in mine:
            cp.wait()

    any_spec = pl.BlockSpec(memory_space=pl.ANY)
    return pl.pallas_call(
        body, name="weight_gather",
        out_shape=[jax.ShapeDtypeStruct((N_DEV * s.shape[0], s.shape[1]), s.dtype) for s in shards],
        in_specs=[any_spec] * n_arr, out_specs=[any_spec] * n_arr,
        scratch_shapes=[pltpu.SemaphoreType.DMA((n_arr, 7)), pltpu.SemaphoreType.DMA((n_arr, 7)),
                        pltpu.SemaphoreType.DMA((n_arr,))],
    )(*shards)


N_CHIP = 4
_OTHER_CHIPS = [(1, 0), (0, 1), (1, 1)]


def small_all_reduce(small):
    vmem_spec = pl.BlockSpec(memory_space=pltpu.VMEM)

    def chip_body(small_ref, csum_ref, land, send_sem, recv_sem):
        x, y, c = _mesh_pos()
        swap = pltpu.make_async_remote_copy(
            src_ref=small_ref, dst_ref=land, send_sem=send_sem, recv_sem=recv_sem,
            device_id=(x, y, 1 - c), device_id_type=pl.DeviceIdType.MESH)
        swap.start()
        swap.wait_recv()
        swap.wait_send()
        csum_ref[...] = small_ref[...] + land[...]

    csum = pl.pallas_call(
        chip_body, name="small_sum_d2d", out_shape=jax.ShapeDtypeStruct(small.shape, small.dtype),
        in_specs=[vmem_spec], out_specs=vmem_spec,
        scratch_shapes=[pltpu.VMEM(small.shape, small.dtype), pltpu.SemaphoreType.DMA, pltpu.SemaphoreType.DMA],
    )(small)

    def all_body(csum_ref, tot_ref, land, send_sems, recv_sems):
        x, y, c = _mesh_pos()
        q_me = 2 * x + y
        land[q_me] = csum_ref[...]
        sends, recvs = [], []
        for j, (dx, dy) in enumerate(_OTHER_CHIPS):
            px, py = _flip(x, dx), _flip(y, dy)
            common = dict(send_sem=send_sems.at[j], recv_sem=recv_sems.at[j],
                          device_id=(px, py, c), device_id_type=pl.DeviceIdType.MESH)
            sends.append(pltpu.make_async_remote_copy(src_ref=csum_ref, dst_ref=land.at[q_me], **common))
            recvs.append(pltpu.make_async_remote_copy(src_ref=csum_ref, dst_ref=land.at[2 * px + py], **common))
        for cp in sends:
            cp.start()
        for cp in recvs:
            cp.wait_recv()
        for cp in sends:
            cp.wait_send()
        tot = land[0]
        for q in range(1, N_CHIP):
            tot = tot + land[q]
        tot_ref[...] = tot

    return pl.pallas_call(
        all_body, name="small_sum_ici", out_shape=jax.ShapeDtypeStruct(small.shape, small.dtype),
        in_specs=[vmem_spec], out_specs=vmem_spec,
        scratch_shapes=[pltpu.VMEM((N_CHIP,) + small.shape, small.dtype),
                        pltpu.SemaphoreType.DMA((3,)), pltpu.SemaphoreType.DMA((3,))],
    )(csum)


_HBM = pl.BlockSpec(memory_space=pltpu.HBM)
_SEM = pl.BlockSpec(memory_space=pltpu.SEMAPHORE)
_EFFECT = pltpu.SideEffectType.DATAFLOW_SIDE_EFFECTING


def _early_copies(src_refs, land_refs, send_sems, recv_sems, rows, gather):
    x, y, c = _mesh_pos()
    me = 4 * x + 2 * y + c
    copies = []
    for a in range(len(src_refs)):
        for dx, dy, dc in _RELATIONS:
            px, py, pc = _flip(x, dx), _flip(y, dy), _flip(c, dc)
            peer = 4 * px + 2 * py + pc
            copies.append(pltpu.make_async_remote_copy(
                src_ref=src_refs[a] if gather else src_refs[a].at[pl.ds(peer * rows[a], rows[a]), :],
                dst_ref=land_refs[a].at[pl.ds(me * rows[a], rows[a]), :],
                send_sem=send_sems[a], recv_sem=recv_sems[a],
                device_id=(px, py, pc), device_id_type=pl.DeviceIdType.MESH))
    return copies


def early_exchange_start(srcs, name, gather=False, after=None):
    n = len(srcs)
    if gather:
        rows = [s.shape[0] for s in srcs]
        me = 4 * lax.axis_index("x") + 2 * lax.axis_index("y") + lax.axis_index("c")
        lands = [lax.dynamic_update_slice(lax.empty((N_DEV * r, s.shape[1]), s.dtype), s, (me * r, 0))
                 for r, s in zip(rows, srcs)]
    else:
        rows = [s.shape[0] // N_DEV for s in srcs]
        lands = [lax.empty(s.shape, s.dtype) for s in srcs]

    extra = [] if after is None else [after]

    def body(*refs):
        src_refs, land_refs = refs[:n], refs[n:2 * n]
        first_sem = 2 * n + len(extra)
        send_sems, recv_sems = refs[first_sem:first_sem + n], refs[first_sem + n:first_sem + 2 * n]
        token = refs[-1]
        for cp in _early_copies(src_refs, land_refs, send_sems, recv_sems, rows, gather):
            cp.start()
        token[...] = jnp.zeros_like(token)

    hbm = lambda a: pltpu.HBM(a.shape, a.dtype)
    outs = pl.pallas_call(
        body, name=name,
        out_shape=[pltpu.SemaphoreType.DMA(())] * (2 * n)
        + [hbm(a) for a in srcs] + [hbm(a) for a in lands] + [jax.ShapeDtypeStruct((8, LANES), F32)],
        in_specs=[_HBM] * (2 * n) + [pl.BlockSpec(memory_space=pl.ANY)] * len(extra),
        out_specs=[_SEM] * (2 * n) + [_HBM] * (2 * n) + [pl.BlockSpec(memory_space=pltpu.VMEM)],
        input_output_aliases={i: 2 * n + i for i in range(2 * n)},
        compiler_params=pltpu.CompilerParams(has_side_effects=_EFFECT),
    )(*[pltpu.with_memory_space_constraint(a, pltpu.HBM) for a in list(srcs) + lands], *extra)
    return dict(sems=outs[:2 * n], srcs=outs[2 * n:3 * n], lands=outs[3 * n:4 * n], rows=rows), outs[-1]


def early_exchange_wait(handle, after, name):
    n = len(handle["srcs"])
    rows = handle["rows"]

    def body(*refs):
        src_refs, land_refs = refs[:n], refs[n:2 * n]
        send_sems, recv_sems = refs[2 * n:3 * n], refs[3 * n:4 * n]
        x, y, c = _mesh_pos()
        for a in range(n):
            seven = pl.ds(0, 7 * rows[a])
            all_seven = pltpu.make_async_remote_copy(
                src_ref=land_refs[a].at[seven, :], dst_ref=land_refs[a].at[seven, :],
                send_sem=send_sems[a], recv_sem=recv_sems[a],
                device_id=(x, y, c), device_id_type=pl.DeviceIdType.MESH)
            all_seven.wait_send()
            all_seven.wait_recv()

    hbm = lambda a: pltpu.HBM(a.shape, a.dtype)
    ins = list(handle["srcs"]) + list(handle["lands"])
    outs = pl.pallas_call(
        body, name=name,
        out_shape=[hbm(a) for a in ins],
        in_specs=[_HBM] * (2 * n) + [_SEM] * (2 * n) + [pl.BlockSpec(memory_space=pl.ANY)],
        out_specs=[_HBM] * (2 * n),
        input_output_aliases={i: i for i in range(2 * n)},
        compiler_params=pltpu.CompilerParams(has_side_effects=_EFFECT),
    )(*ins, *handle["sems"], after)
    return outs[:n], outs[n:]


def slot_sum8(src, land, tr, name):
    rows, cols = land.shape[0] // N_DEV, land.shape[1]
    x, y, c = _mesh_pos()
    me = (4 * x + 2 * y + c).astype(jnp.int32).reshape(1)

    def body(me_ref, src_ref, land_ref, o_ref):
        acc = None
        for d in range(N_DEV):
            term = jnp.where(d == me_ref[0], src_ref[0], land_ref[d]).astype(F32)
            acc = term if acc is None else acc + term
        o_ref[...] = acc

    return pl.pallas_call(
        body, name=name,
        grid_spec=pltpu.PrefetchScalarGridSpec(
            num_scalar_prefetch=1, grid=(rows // tr,),
            in_specs=[pl.BlockSpec((1, tr, cols), lambda i, w: (w[0], i, 0)),
                      pl.BlockSpec((N_DEV, tr, cols), lambda i, w: (0, i, 0))],
            out_specs=pl.BlockSpec((tr, cols), lambda i, w: (i, 0))),
        out_shape=jax.ShapeDtypeStruct((rows, cols), F32),
        compiler_params=_params(("arbitrary",)),
    )(me, src.reshape(N_DEV, rows, cols), land.reshape(N_DEV, rows, cols))


def mm_tn_multi(a_t, bs, tt, name, out_dtype=F32):
    K, T = a_t.shape
    widths = [b.shape[1] for b in bs]
    steps = T // tt

    def body(a_ref, *rest):
        b_refs, o_ref, acc = rest[:-2], rest[-2], rest[-1]

        @pl.when(pl.program_id(0) == 0)
        def _():
            acc[...] = jnp.zeros(acc.shape, F32)

        av = a_ref[...]
        col = 0
        for b_ref, w in zip(b_refs, widths):
            acc[:, col:col + w] += jnp.dot(av, b_ref[...], preferred_element_type=F32)
            col += w

        @pl.when(pl.program_id(0) == steps - 1)
        def _():
            o_ref[...] = acc[...].astype(out_dtype)

    return pl.pallas_call(
        body, name=name, grid=(steps,),
        in_specs=[pl.BlockSpec((K, tt), lambda t: (0, t))] + [pl.BlockSpec((tt, w), lambda t: (t, 0)) for w in widths],
        out_specs=pl.BlockSpec((K, sum(widths)), lambda t: (0, 0)),
        out_shape=jax.ShapeDtypeStruct((K, sum(widths)), out_dtype),
        scratch_shapes=[pltpu.VMEM((K, sum(widths)), F32)],
        compiler_params=_params(("arbitrary",)),
    )(a_t, *bs)


def rms_fwd(x, g, tm, name, with_transpose=False):
    M, K = x.shape

    def body(x_ref, g_ref, o_ref, *t_ref):
        xv = x_ref[...]
        r = lax.rsqrt(jnp.mean(xv * xv, axis=-1, keepdims=True) + RMS_EPS)
        h = ((xv * r) * g_ref[...]).astype(BF16)
        o_ref[...] = h
        if with_transpose:
            t_ref[0][...] = h.T

    out_specs = [pl.BlockSpec((tm, K), lambda i: (i, 0))]
    out_shape = [jax.ShapeDtypeStruct((M, K), BF16)]
    if with_transpose:
        out_specs.append(pl.BlockSpec((K, tm), lambda i: (0, i)))
        out_shape.append(jax.ShapeDtypeStruct((K, M), BF16))
    outs = pl.pallas_call(
        body, name=name, grid=(M // tm,),
        in_specs=[pl.BlockSpec((tm, K), lambda i: (i, 0)), pl.BlockSpec((1, K), lambda i: (0, 0))],
        out_specs=out_specs, out_shape=out_shape,
        compiler_params=_params(("arbitrary",)),
    )(x, g)
    return outs if with_transpose else outs[0]


def rms_bwd(x, g, dh, dres, tm, name):
    M, K = x.shape
    has_res = dres is not None

    def body(*refs):
        if has_res:
            x_ref, g_ref, dh_ref, dres_ref, dx_ref, dg_ref = refs
        else:
            x_ref, g_ref, dh_ref, dx_ref, dg_ref = refs
        xv = x_ref[...]
        r = lax.rsqrt(jnp.mean(xv * xv, axis=-1, keepdims=True) + RMS_EPS)
        xn = xv * r
        dhv = dh_ref[...]
        dxn = dhv * g_ref[...]
        dx = r * (dxn - xn * jnp.mean(dxn * xn, axis=-1, keepdims=True))
        if has_res:
            dx = dx + dres_ref[...]
        dx_ref[...] = dx
        part = jnp.sum(dhv * xn, axis=0, keepdims=True)
        row = lax.broadcasted_iota(jnp.int32, (8, K), 0)
        upd = jnp.where(row == 0, part, 0.0)

        @pl.when(pl.program_id(0) == 0)
        def _():
            dg_ref[...] = upd

        @pl.when(pl.program_id(0) != 0)
        def _():
            dg_ref[...] += upd

    row_spec = pl.BlockSpec((tm, K), lambda i: (i, 0))
    ins = [x, g, dh] + ([dres] if has_res else [])
    in_specs = [row_spec, pl.BlockSpec((1, K), lambda i: (0, 0)), row_spec] + ([row_spec] if has_res else [])
    return pl.pallas_call(
        body, name=name, grid=(M // tm,),
        in_specs=in_specs,
        out_specs=[row_spec, pl.BlockSpec((8, K), lambda i: (0, 0))],
        out_shape=[jax.ShapeDtypeStruct((M, K), F32), jax.ShapeDtypeStruct((8, K), F32)],
        compiler_params=_params(("arbitrary",)),
    )(*ins)


def mm_nn(a, b, tm, tn, name, token=None):
    M, K = a.shape
    N = b.shape[1]
    extra = [] if token is None else [token]

    def body(a_ref, b_ref, *rest):
        rest[-1][...] = jnp.dot(a_ref[...], b_ref[...], preferred_element_type=F32)

    return pl.pallas_call(
        body, name=name, grid=(N // tn, M // tm),
        in_specs=[pl.BlockSpec((tm, K), lambda j, i: (i, 0)), pl.BlockSpec((K, tn), lambda j, i: (0, j))]
        + [pl.BlockSpec(t.shape, lambda j, i: (0, 0)) for t in extra],
        out_specs=pl.BlockSpec((tm, tn), lambda j, i: (i, j)),
        out_shape=jax.ShapeDtypeStruct((M, N), F32),
        compiler_params=_params(("arbitrary", "arbitrary")),
    )(a, b, *extra)


def mm_nt(a, b, tm, tk, name):
    M, K = a.shape
    N = b.shape[0]

    def body(a_ref, b_ref, o_ref):
        part = lax.dot_general(a_ref[...], b_ref[...], (((1,), (1,)), ((), ())), preferred_element_type=F32)

        @pl.when(pl.program_id(1) == 0)
        def _():
            o_ref[...] = part

        @pl.when(pl.program_id(1) != 0)
        def _():
            o_ref[...] += part

    return pl.pallas_call(
        body, name=name, grid=(M // tm, K // tk),
        in_specs=[pl.BlockSpec((tm, tk), lambda i, k: (i, k)), pl.BlockSpec((N, tk), lambda i, k: (0, k))],
        out_specs=pl.BlockSpec((tm, N), lambda i, k: (i, 0)),
        out_shape=jax.ShapeDtypeStruct((M, N), F32),
        compiler_params=_params(("arbitrary", "arbitrary")),
    )(a, b)


def in_proj_bwd_rms(pieces, w, x, g, dres, tm, token):
    M, N = x.shape

    def body(*refs):
        n = len(pieces)
        p_refs, w_ref, x_ref, g_ref, dres_ref, _, dx_ref, dg_ref = refs[:n], *refs[n:]
        dh = None
        for p_ref, (arr, col) in zip(p_refs, pieces):
            part = lax.dot_general(p_ref[...], w_ref[:, col:col + arr.shape[1]], (((1,), (1,)), ((), ())),
                                   preferred_element_type=F32)
            dh = part if dh is None else dh + part
        xv = x_ref[...]
        r = lax.rsqrt(jnp.mean(xv * xv, axis=-1, keepdims=True) + RMS_EPS)
        xn = xv * r
        dxn = dh * g_ref[...]
        dx_ref[...] = r * (dxn - xn * jnp.mean(dxn * xn, axis=-1, keepdims=True)) + dres_ref[...]
        row = lax.broadcasted_iota(jnp.int32, (8, N), 0)
        upd = jnp.where(row == 0, jnp.sum(dh * xn, axis=0, keepdims=True), 0.0)

        @pl.when(pl.program_id(0) == 0)
        def _():
            dg_ref[...] = upd

        @pl.when(pl.program_id(0) != 0)
        def _():
            dg_ref[...] += upd

    row_spec = pl.BlockSpec((tm, N), lambda i: (i, 0))
    return pl.pallas_call(
        body, name="in_proj_bwd", grid=(M // tm,),
        in_specs=[pl.BlockSpec((tm, arr.shape[1]), lambda i: (i, 0)) for arr, _ in pieces]
        + [pl.BlockSpec(w.shape, lambda i: (0, 0)), row_spec, pl.BlockSpec((1, N), lambda i: (0, 0)), row_spec,
           pl.BlockSpec(token.shape, lambda i: (0, 0))],
        out_specs=[row_spec, pl.BlockSpec((8, N), lambda i: (0, 0))],
        out_shape=[jax.ShapeDtypeStruct((M, N), F32), jax.ShapeDtypeStruct((8, N), F32)],
        compiler_params=_params(("arbitrary",)),
    )(*[arr for arr, _ in pieces], w, x, g, dres, token)


def _log_sigmoid(z):
    return jnp.minimum(z, 0.0) - jnp.log(1.0 + jnp.exp(-jnp.abs(z)))


def _tri(n, lower):
    r = lax.broadcasted_iota(jnp.int32, (n, n), 0)
    c = lax.broadcasted_iota(jnp.int32, (n, n), 1)
    return jnp.where((r >= c) if lower else (r <= c), 1.0, 0.0).astype(F32)


def fox_gate(proj3, b_pad):
    B, S, _ = proj3.shape
    nblk = S // TK

    def body(f_ref, b_ref, o_ref):
        tri = _tri(TK, True)
        carry = jnp.zeros((1, LANES), F32)
        for n in range(nblk):
            z = f_ref[0, n * TK:(n + 1) * TK, :] + b_ref[...]
            logf = _log_sigmoid(z)
            cs = jnp.dot(tri, logf, preferred_element_type=F32, precision=lax.Precision.HIGHEST) + carry
            carry = cs[TK - 1:TK, :]
            o_ref[0, n * TK:(n + 1) * TK, :] = -cs

    return pl.pallas_call(
        body, name="fox_gate", grid=(B,),
        in_specs=[pl.BlockSpec((1, S, LANES), lambda b: (b, 0, P_FLOG // LANES)),
                  pl.BlockSpec((1, LANES), lambda b: (0, 0))],
        out_specs=pl.BlockSpec((1, S, LANES), lambda b: (b, 0, 0)),
        out_shape=jax.ShapeDtypeStruct((B, S, LANES), F32),
        compiler_params=_params(("arbitrary",)),
    )(proj3, b_pad)


def fox_gate_bwd(drow, dneg, proj3, b_pad):
    B, S, _ = proj3.shape
    nblk = S // TK

    def body(d_ref, r_ref, f_ref, b_ref, o_ref, db_ref):
        tri = _tri(TK, False)
        lane = lax.broadcasted_iota(jnp.int32, (TK, LANES), 1)
        carry = jnp.zeros((1, LANES), F32)
        dbsum = jnp.zeros((1, LANES), F32)
        for n in reversed(range(nblk)):
            dk_side = None
            for hp in range(FOX_HEADS // 2):
                two = jnp.where(lane < 2, r_ref[0, n * TK:(n + 1) * TK, hp * LANES:(hp + 1) * LANES], 0.0)
                two = pltpu.roll(two, 2 * hp, 1) if hp else two
                dk_side = two if dk_side is None else dk_side + two
            dc = jnp.where(lane < FOX_HEADS, d_ref[0, :, n * TK:(n + 1) * TK].T - dk_side, 0.0)
            rs = jnp.dot(tri, dc, preferred_element_type=F32, precision=lax.Precision.HIGHEST) + carry
            carry = rs[0:1, :]
            z = f_ref[0, n * TK:(n + 1) * TK, :] + b_ref[...]
            dz = rs * (1.0 / (1.0 + jnp.exp(z)))
            o_ref[0, n * TK:(n + 1) * TK, :] = dz.astype(BF16)
            dbsum = dbsum + jnp.sum(dz, axis=0, keepdims=True)
        row = lax.broadcasted_iota(jnp.int32, (8, LANES), 0)
        upd = jnp.where(row == 0, dbsum, 0.0)

        @pl.when(pl.program_id(0) == 0)
        def _():
            db_ref[...] = upd

        @pl.when(pl.program_id(0) != 0)
        def _():
            db_ref[...] += upd

    return pl.pallas_call(
        body, name="fox_gate_bwd", grid=(B,),
        in_specs=[pl.BlockSpec((1, LANES, S), lambda b: (b, 0, 0)),
                  pl.BlockSpec((1, S, FOX_W), lambda b: (b, 0, 0)),
                  pl.BlockSpec((1, S, LANES), lambda b: (b, 0, P_FLOG // LANES)),
                  pl.BlockSpec((1, LANES), lambda b: (0, 0))],
        out_specs=[pl.BlockSpec((1, S, LANES), lambda b: (b, 0, 0)), pl.BlockSpec((8, LANES), lambda b: (0, 0))],
        out_shape=[jax.ShapeDtypeStruct((B, S, LANES), BF16), jax.ShapeDtypeStruct((8, LANES), F32)],
        compiler_params=_params(("arbitrary",)),
    )(drow, dneg, proj3, b_pad)


def _rope_tables(S):
    half = ROPE_DIM // 2
    f32 = np.float32
    pos = np.arange(S, dtype=f32)
    inv_freq = f32(1.0) / np.power(f32(ROPE_THETA), np.arange(0, ROPE_DIM, 2, dtype=f32) / f32(ROPE_DIM)).astype(f32)
    ang = (pos[:, None] * inv_freq[None, :]).astype(f32).astype(np.float64)
    cos, sin = np.cos(ang).astype(f32), np.sin(ang).astype(f32)
    one = np.ones((S, HEAD_DIM - ROPE_DIM), f32)
    zero = np.zeros((S, HEAD_DIM - ROPE_DIM), f32)
    zh = np.zeros((S, half), f32)
    c = np.concatenate([cos, cos, one], axis=1)
    s1 = np.concatenate([-sin, zh, zero], axis=1)
    s2 = np.concatenate([zh, sin, zero], axis=1)
    return tuple(jnp.asarray(np.concatenate([t, t], axis=1)) for t in (c, s1, s2))


_HALF_ROPE = ROPE_DIM // 2


def _rope(t, c, s1, s2):
    return t * c + pltpu.roll(t, LANES - _HALF_ROPE, 1) * s1 + pltpu.roll(t, _HALF_ROPE, 1) * s2


def _rope_bwd(d, c, s1, s2):
    return d * c + pltpu.roll(d * s1, _HALF_ROPE, 1) + pltpu.roll(d * s2, LANES - _HALF_ROPE, 1)


def _scale_parts(scale):
    m, _ = math.frexp(scale)
    return (scale, None) if m == 0.5 else (None, scale)


def _log_masks(S, kind):
    nd = 1 if kind == "causal" else S // TQ
    a = np.arange(TQ)[:, None]
    b = np.arange(TK)[None, :]
    out = np.zeros((nd, TQ, TK), np.float32)
    for d in range(nd):
        delta = d * TQ + a - b
        if kind == "causal":
            m = (delta >= 0).astype(np.float64)
        else:
            m = sum(((delta >= 0) & (delta % dil == 0) & (delta <= w)).astype(np.float64) for w, dil in DILATIONS)
        out[d] = np.where(m > 0, np.log(np.maximum(m, 1.0)), NEG_INF)
    return jnp.asarray(out)


def _attn_setup(kind):
    pair = kind != "mem"
    e_dim = HEAD_DIM if pair else MEM_HEAD_DIM
    q_fold, s_scale = _scale_parts(1.0 / math.sqrt(e_dim))
    return dict(pair=pair, col0={"fox": P_FOX, "dil": P_DIL, "mem": P_MQ}[kind],
                n_blocks=FOX_HEADS // 2 if pair else MEM_HEADS, q_fold=q_fold, s_scale=s_scale,
                nh=2 if pair else 1)


def _store_stacked(cfg, lane, dst, n, val):
    nh = cfg["nh"]
    R = nh * TQ
    if nh == 1:
        dst[pl.ds(pl.multiple_of(n * R, R), TQ), :] = val
        return
    for hh in range(nh):
        hmask = (lane >= HEAD_DIM * hh) & (lane < HEAD_DIM * (hh + 1))
        dst[pl.ds(pl.multiple_of(n * R + hh * TQ, TQ), TQ), :] = jnp.where(hmask, val, jnp.zeros_like(val))


def _cat(parts, axis):
    return parts[0] if len(parts) == 1 else jnp.concatenate(parts, axis=axis)


def _log_masks_t(S, kind):
    return jnp.swapaxes(_log_masks(S, kind), 1, 2)


def _head_rows(hh, pair):
    row = lax.broadcasted_iota(jnp.int32, (LANES, 1), 0)
    if not pair:
        return row >= 0
    return (row >= HEAD_DIM * hh) & (row < HEAD_DIM * (hh + 1))


def _attn_t_inputs(kind, src, S, negc_cols, mask, rope, kv):
    cfg = _attn_setup(kind)
    col0 = cfg["col0"]
    ins, in_specs = [], []
    if cfg["pair"]:
        ins.append(src)
        in_specs.append(pl.BlockSpec((1, S, PAIR_W), lambda b, h: (b, 0, col0 // PAIR_W + h)))
    else:
        ins += [src, kv, kv]
        in_specs += [pl.BlockSpec((1, S, LANES), lambda b, h: (b, 0, col0 // LANES + h)),
                     pl.BlockSpec((1, MEM_LEN, LANES), lambda b, h: (b, 0, h)),
                     pl.BlockSpec((1, MEM_LEN, LANES), lambda b, h: (b, 0, MEM_HEADS + h))]
    if negc_cols is not None:
        ins.append(negc_cols)
        in_specs.append(pl.BlockSpec((1, S, LANES), lambda b, h: (b, 0, 0)))
    if mask is not None:
        ins.append(mask)
        in_specs.append(pl.BlockSpec(mask.shape, lambda b, h: (0, 0, 0)))
    if rope is not None:
        ins += list(rope)
        in_specs += [pl.BlockSpec((S, LANES), lambda b, h: (0, 0))] * 3
    return ins, in_specs


def _attn_t_prep(cfg, refs, S, Sk, *, qT2s, ks, q2s=None, vs=None, vTs=None, kTs=None, nb=None):
    pair, nh = cfg["pair"], cfg["nh"]
    lane = lax.broadcasted_iota(jnp.int32, (1, LANES), 1)
    rope_refs = refs["rope"]

    def prep_q(n, _):
        rows = pl.ds(pl.multiple_of(n * TQ, TQ), TQ)
        q = refs["load_q"](rows)
        if rope_refs is not None:
            q = _rope(q, *[t[rows, :] for t in rope_refs])
        if cfg["q_fold"] is not None:
            q = q * cfg["q_fold"]
        qb = q.astype(BF16)
        if q2s is not None:
            _store_stacked(cfg, lane, q2s, n, qb)
        qtb = qb.T
        for hh in range(nh):
            qT2s[n, :, hh * TQ:(hh + 1) * TQ] = jnp.where(_head_rows(hh, pair), qtb, jnp.zeros_like(qtb))
        return 0

    def prep_kv(n, _):
        rows = pl.ds(pl.multiple_of(n * TK, TK), TK)
        k, v = refs["load_kv"](rows)
        if rope_refs is not None:
            k = _rope(k, *[t[rows, :] for t in rope_refs])
        kb = k.astype(BF16)
        vb = v.astype(BF16)
        ks[rows, :] = kb
        if vs is not None:
            vs[rows, :] = vb
        if vTs is not None:
            vTs[n] = vb.T
        if kTs is not None:
            kTs[n] = kb.T
        if nb is not None:
            blk = refs["negc"][0, rows, :]
            for hh in range(nh):
                h = 2 * refs["block"] + hh
                col = jnp.sum(jnp.where(lane == h, blk, 0.0), axis=1, keepdims=True)
                nb[hh, rows, :] = jnp.broadcast_to(col, (TK, LANES))
        return 0

    lax.fori_loop(0, S // TQ, prep_q, 0)
    lax.fori_loop(0, Sk // TK, prep_kv, 0)


def _raw_scores_t(cfg, k, qT2):
    sT = jnp.dot(k, qT2, preferred_element_type=F32)
    if cfg["s_scale"] is not None:
        sT = sT * cfg["s_scale"]
    return sT


def _bias_mask_t(cfg, sT, nb, mask_ref, kc, midx):
    nh = cfg["nh"]
    if nb is None and midx is None:
        return sT
    parts = []
    for hh in range(nh):
        t = sT[:, hh * TQ:(hh + 1) * TQ]
        if nb is not None:
            t = t + jnp.concatenate([nb[hh, kc, :]] * (TQ // LANES), axis=1)
        if midx is not None:
            t = t + mask_ref[midx]
        parts.append(t)
    return _cat(parts, 1)


def _tile_pairs(kind, nq, nk):
    if kind == "mem":
        return [(i, j) for i in range(nq) for j in range(nk)], (lambda i, j: None)
    pairs = [(i, j) for i in range(nq) for j in range(i + 1)]
    if kind == "fox":
        return pairs, (lambda i, j: 0 if j == i else None)
    return pairs, (lambda i, j: i - j)


def attn_fwd(kind, src, S, *, negc_cols=None, mask=None, rope=None, kv=None):
    B = src.shape[0]
    cfg = _attn_setup(kind)
    pair, nh = cfg["pair"], cfg["nh"]
    Sk = S if pair else MEM_LEN
    has_bias, has_rope = negc_cols is not None, rope is not None
    R = nh * TQ
    nq, nk = S // TQ, Sk // TK
    pairs, mask_index = _tile_pairs(kind, nq, nk)

    def body(*refs):
        refs = list(refs)
        if pair:
            qkv_ref = refs.pop(0)
            load_q = lambda rows: qkv_ref[0, rows, 0:LANES]
            load_kv = lambda rows: (qkv_ref[0, rows, LANES:2 * LANES], qkv_ref[0, rows, 2 * LANES:3 * LANES])
        else:
            q_ref, k_ref, v_ref = refs.pop(0), refs.pop(0), refs.pop(0)
            load_q = lambda rows: q_ref[0, rows, :]
            load_kv = lambda rows: (k_ref[0, rows, :], v_ref[0, rows, :])
        negc_ref = refs.pop(0) if has_bias else None
        mask_ref = refs.pop(0) if mask is not None else None
        rope_refs = [refs.pop(0) for _ in range(3)] if has_rope else None
        o_ref, lse_ref, qT2s, ks, vTs, s_a, s_b, p_a, p_b = refs[:9]
        nb = refs[9] if has_bias else None
        _attn_t_prep(cfg, dict(load_q=load_q, load_kv=load_kv, rope=rope_refs, negc=negc_ref,
                               block=pl.program_id(1)), S, Sk, qT2s=qT2s, ks=ks, vTs=vTs, nb=nb)

        def cols(j):
            return slice(j * TK, (j + 1) * TK)

        def scores(i, j):
            return _raw_scores_t(cfg, ks[cols(j), :], qT2s[i])

        def finish(i, m, l, accT):
            oT2 = accT / l
            oT = jnp.where(_head_rows(0, True), oT2[:, 0:TQ], oT2[:, TQ:2 * TQ]) if pair else oT2
            o_ref[0, i * TQ:(i + 1) * TQ, :] = oT.T
            lse_ref[0, 0, i:i + 1, :] = m + jnp.log(l)

        s_bufs, p_bufs = (s_a, s_b), (p_a, p_b)
        s_bufs[0][...] = scores(*pairs[0])
        m = l = accT = None
        for t, (i, j) in enumerate(pairs):
            cur, oth = t % 2, 1 - t % 2
            if t > 0:
                i_prev, j_prev = pairs[t - 1]
                pv = jnp.dot(vTs[j_prev], p_bufs[oth][...], preferred_element_type=F32)
                acc_full = pv if accT is None else accT + pv
            if t + 1 < len(pairs):
                s_bufs[oth][...] = scores(*pairs[t + 1])
            first = j == 0
            if first and t > 0:
                finish(i_prev, m, l, acc_full)
            sT = _bias_mask_t(cfg, s_bufs[cur][...], nb, mask_ref, cols(j), mask_index(i, j))
            m_tile = jnp.max(sT, axis=0, keepdims=True)
            m_new = m_tile if first else jnp.maximum(m, m_tile)
            p = jnp.exp(sT - m_new)
            p_bufs[cur][...] = p.astype(BF16)
            if first:
                l, accT = jnp.sum(p, axis=0, keepdims=True), None
            else:
                alpha = jnp.exp(m - m_new)
                l, accT = alpha * l + jnp.sum(p, axis=0, keepdims=True), acc_full * alpha
            m = m_new
        i_last, j_last = pairs[-1]
        pv = jnp.dot(vTs[j_last], p_bufs[(len(pairs) - 1) % 2][...], preferred_element_type=F32)
        finish(i_last, m, l, pv if accT is None else accT + pv)

    ins, in_specs = _attn_t_inputs(kind, src, S, negc_cols, mask, rope, kv)
    W = cfg["n_blocks"] * LANES
    scratch = [pltpu.VMEM((nq, LANES, R), BF16), pltpu.VMEM((Sk, LANES), BF16), pltpu.VMEM((nk, LANES, TK), BF16),
               pltpu.VMEM((TK, R), F32), pltpu.VMEM((TK, R), F32), pltpu.VMEM((TK, R), BF16), pltpu.VMEM((TK, R), BF16)]
    if has_bias:
        scratch.append(pltpu.VMEM((nh, Sk, LANES), F32))
    return pl.pallas_call(
        body, name=kind + "_attn_fwd", grid=(B, cfg["n_blocks"]),
        in_specs=in_specs,
        out_specs=[pl.BlockSpec((1, S, LANES), lambda b, h: (b, 0, h)),
                   pl.BlockSpec((1, 1, nq, R), lambda b, h: (b, h, 0, 0))],
        out_shape=[jax.ShapeDtypeStruct((B, S, W), F32), jax.ShapeDtypeStruct((B, cfg["n_blocks"], nq, R), F32)],
        scratch_shapes=scratch,
        compiler_params=_params(("arbitrary", "arbitrary")),
    )(*ins)


def attn_bwd(kind, src, do, o, lse, S, *, negc_cols=None, mask=None, rope=None, kv=None, token=None):
    B = src.shape[0]
    cfg = _attn_setup(kind)
    pair, nh, s_scale, q_fold = cfg["pair"], cfg["nh"], cfg["s_scale"], cfg["q_fold"]
    Sk = S if pair else MEM_LEN
    has_bias, has_rope = negc_cols is not None, rope is not None
    R = nh * TQ
    nq, nk = S // TQ, Sk // TK
    pairs, mask_index = _tile_pairs(kind, nq, nk)

    def body(*refs):
        refs = list(refs)
        if pair:
            qkv_ref = refs.pop(0)
            load_q = lambda rows: qkv_ref[0, rows, 0:LANES]
            load_kv = lambda rows: (qkv_ref[0, rows, LANES:2 * LANES], qkv_ref[0, rows, 2 * LANES:3 * LANES])
        else:
            q_ref, k_ref, v_ref = refs.pop(0), refs.pop(0), refs.pop(0)
            load_q = lambda rows: q_ref[0, rows, :]
            load_kv = lambda rows: (k_ref[0, rows, :], v_ref[0, rows, :])
        negc_ref = refs.pop(0) if has_bias else None
        mask_ref = refs.pop(0) if mask is not None else None
        rope_refs = [refs.pop(0) for _ in range(3)] if has_rope else None
        do_ref, o_ref, lse_ref = refs.pop(0), refs.pop(0), refs.pop(0)
        if token is not None:
            refs.pop(0)
        if pair:
            dqkv_ref = refs.pop(0)
            dneg_ref = refs.pop(0) if has_bias else None
            drow_ref = refs.pop(0) if has_bias else None
        else:
            dq_ref, dk_ref, dv_ref = refs.pop(0), refs.pop(0), refs.pop(0)
        qT2s, ks, q2s, vs, kTs, doT2s, do2s, delta_s, dk_acc, dv_acc = refs[:10]
        bufs_a, bufs_b = refs[10:14], refs[14:18]
        nb, dneg_acc = (refs[18], refs[19]) if has_bias else (None, None)
        lane = lax.broadcasted_iota(jnp.int32, (1, LANES), 1)
        _attn_t_prep(cfg, dict(load_q=load_q, load_kv=load_kv, rope=rope_refs, negc=negc_ref,
                               block=pl.program_id(1)), S, Sk,
                     qT2s=qT2s, ks=ks, q2s=q2s, vs=vs, kTs=kTs, nb=nb)

        def prep_do(n, _):
            rows = pl.ds(pl.multiple_of(n * TQ, TQ), TQ)
            dob = do_ref[0, rows, :].astype(BF16)
            _store_stacked(cfg, lane, do2s, n, dob)
            doT = dob.astype(F32).T
            prodT = doT * o_ref[0, rows, :].T
            doTb = doT.astype(BF16)
            for hh in range(nh):
                hm = _head_rows(hh, pair)
                doT2s[n, :, hh * TQ:(hh + 1) * TQ] = jnp.where(hm, doTb, jnp.zeros_like(doTb))
                delta_s[pl.ds(n, 1), hh * TQ:(hh + 1) * TQ] = jnp.sum(jnp.where(hm, prodT, 0.0), axis=0, keepdims=True)
            return 0

        def zero_kv(n, _):
            rows = pl.ds(pl.multiple_of(n * TK, TK), TK)
            dk_acc[rows, :] = jnp.zeros((TK, LANES), F32)
            dv_acc[rows, :] = jnp.zeros((TK, LANES), F32)
            if has_bias:
                for hh in range(nh):
                    dneg_acc[hh, rows, :] = jnp.zeros((TK, LANES), F32)
            return 0

        lax.fori_loop(0, nq, prep_do, 0)
        lax.fori_loop(0, nk, zero_kv, 0)

        def cols(j):
            return slice(j * TK, (j + 1) * TK)

        def rows2(i):
            return slice(i * R, (i + 1) * R)

        def first_products(i, j, bufs):
            bufs[0][...] = _raw_scores_t(cfg, ks[cols(j), :], qT2s[i])
            bufs[1][...] = jnp.dot(vs[cols(j), :], doT2s[i], preferred_element_type=F32)

        def last_products(i, j, bufs, dqT2):
            dv_acc[cols(j), :] += jnp.dot(bufs[2][...], do2s[rows2(i), :], preferred_element_type=F32)
            dk_acc[cols(j), :] += jnp.dot(bufs[3][...], q2s[rows2(i), :], preferred_element_type=F32)
            dq = jnp.dot(kTs[j], bufs[3][...], preferred_element_type=F32)
            return dq if dqT2 is None else dqT2 + dq

        def finish_q(i, dqT2, drow):
            rows = slice(i * TQ, (i + 1) * TQ)
            dqT = jnp.where(_head_rows(0, True), dqT2[:, 0:TQ], dqT2[:, TQ:2 * TQ]) if pair else dqT2
            dq = dqT.T
            if q_fold is not None:
                dq = dq * q_fold
            if has_rope:
                dq = _rope_bwd(dq, *[t[rows, :] for t in rope_refs])
            if pair:
                dqkv_ref[0, rows, 0:LANES] = dq.astype(BF16)
            else:
                dq_ref[0, rows, :] = dq.astype(BF16)
            if has_bias:
                drow_ref[0, 0, i:i + 1, :] = drow

        bufs = (bufs_a, bufs_b)
        first_products(*pairs[0], bufs[0])
        dqT2 = drow = None
        for t, (i, j) in enumerate(pairs):
            cur, oth = bufs[t % 2], bufs[1 - t % 2]
            first = j == 0
            if first and t > 0:
                i_prev, j_prev = pairs[t - 1]
                finish_q(i_prev, last_products(i_prev, j_prev, oth, dqT2), drow)
                dqT2 = drow = None
            sT = _bias_mask_t(cfg, cur[0][...], nb, mask_ref, cols(j), mask_index(i, j))
            pT = jnp.exp(sT - lse_ref[0, 0, i:i + 1, :])
            dsT = pT * (cur[1][...] - delta_s[i:i + 1, :])
            if has_bias:
                tile_rows = jnp.sum(dsT, axis=0, keepdims=True)
                drow = tile_rows if drow is None else drow + tile_rows
                for hh in range(nh):
                    part = dsT[:, hh * TQ:hh * TQ + LANES]
                    for u in range(1, TQ // LANES):
                        part = part + dsT[:, hh * TQ + u * LANES:hh * TQ + (u + 1) * LANES]
                    dneg_acc[hh, cols(j), :] += part
            if s_scale is not None:
                dsT = dsT * s_scale
            cur[2][...] = pT.astype(BF16)
            cur[3][...] = dsT.astype(BF16)
            if not first:
                dqT2 = last_products(*pairs[t - 1], oth, dqT2)
            if t + 1 < len(pairs):
                first_products(*pairs[t + 1], oth)
        i_last, j_last = pairs[-1]
        finish_q(i_last, last_products(i_last, j_last, bufs[(len(pairs) - 1) % 2], dqT2), drow)

        def fin_kv(n, _):
            rows = pl.ds(pl.multiple_of(n * TK, TK), TK)
            dk = dk_acc[rows, :]
            if has_rope:
                dk = _rope_bwd(dk, *[t[rows, :] for t in rope_refs])
            if pair:
                dqkv_ref[0, rows, LANES:2 * LANES] = dk.astype(BF16)
                dqkv_ref[0, rows, 2 * LANES:3 * LANES] = dv_acc[rows, :].astype(BF16)
            else:
                dk_ref[0, rows, :] = dk.astype(BF16)
                dv_ref[0, rows, :] = dv_acc[rows, :].astype(BF16)
            if has_bias:
                x0 = jnp.sum(dneg_acc[0, rows, :], axis=1, keepdims=True)
                x1 = jnp.sum(dneg_acc[1, rows, :], axis=1, keepdims=True)
                dneg_ref[0, rows, :] = jnp.where(lane == 0, x0, jnp.where(lane == 1, x1, 0.0))
            return 0

        lax.fori_loop(0, nk, fin_kv, 0)

    ins, in_specs = _attn_t_inputs(kind, src, S, negc_cols, mask, rope, kv)
    row_spec = pl.BlockSpec((1, S, LANES), lambda b, h: (b, 0, h))
    vec_spec = pl.BlockSpec((1, 1, nq, R), lambda b, h: (b, h, 0, 0))
    ins += [do, o, lse]
    in_specs += [row_spec, row_spec, vec_spec]
    if token is not None:
        ins.append(token)
        in_specs.append(pl.BlockSpec(token.shape, lambda b, h: (0, 0)))
    W = cfg["n_blocks"] * LANES
    if pair:
        out_specs = [pl.BlockSpec((1, S, PAIR_W), lambda b, h: (b, 0, h))]
        out_shape = [jax.ShapeDtypeStruct((B, S, 3 * W), BF16)]
        if has_bias:
            out_specs += [row_spec, vec_spec]
            out_shape += [jax.ShapeDtypeStruct((B, S, W), F32), jax.ShapeDtypeStruct((B, cfg["n_blocks"], nq, R), F32)]
    else:
        kv_spec = pl.BlockSpec((1, MEM_LEN, LANES), lambda b, h: (b, 0, h))
        out_specs = [row_spec, kv_spec, kv_spec]
        out_shape = [jax.ShapeDtypeStruct((B, S, W), BF16)] + [jax.ShapeDtypeStruct((B, MEM_LEN, W), BF16)] * 2
    scratch = [pltpu.VMEM((nq, LANES, R), BF16), pltpu.VMEM((Sk, LANES), BF16), pltpu.VMEM((nh * S, LANES), BF16),
               pltpu.VMEM((Sk, LANES), BF16), pltpu.VMEM((nk, LANES, TK), BF16), pltpu.VMEM((nq, LANES, R), BF16),
               pltpu.VMEM((nh * S, LANES), BF16), pltpu.VMEM((nq, R), F32),
               pltpu.VMEM((Sk, LANES), F32), pltpu.VMEM((Sk, LANES), F32)]
    pair_bufs = [pltpu.VMEM((TK, R), F32), pltpu.VMEM((TK, R), F32), pltpu.VMEM((TK, R), BF16), pltpu.VMEM((TK, R), BF16)]
    scratch += pair_bufs + pair_bufs
    if has_bias:
        scratch += [pltpu.VMEM((nh, Sk, LANES), F32), pltpu.VMEM((nh, Sk, LANES), F32)]
    return pl.pallas_call(
        body, name=kind + "_attn_bwd", grid=(B, cfg["n_blocks"]),
        in_specs=in_specs, out_specs=out_specs, out_shape=out_shape, scratch_shapes=scratch,
        compiler_params=_params(("arbitrary", "arbitrary")),
    )(*ins)


def _sigmoid(g):
    return 1.0 / (1.0 + jnp.exp(-g))


def out_step(proj, o_fox, o_dil, o_mem, w_out, x, target, gf, tm):
    T = x.shape[0]

    def body(fg_ref, dg_ref, mg_ref, of_ref, od_ref, om_ref, w_ref, x_ref, t_ref, gf_ref,
             dx_ref, dof_ref, dod_ref, dom_ref, dfg_ref, ddg_ref, dmg_ref, gw_ref, sm_ref, gw_acc):
        branches = []
        for g_ref, o_ref in ((fg_ref, of_ref), (dg_ref, od_ref), (mg_ref, om_ref)):
            g = g_ref[...]
            sg = _sigmoid(g)
            o = o_ref[...]
            branches.append((g, sg, o))
        ymix = jnp.concatenate([(o * (g * sg)).astype(BF16) for g, sg, o in branches], axis=1)
        x2 = x_ref[...] + jnp.dot(ymix, w_ref[...], preferred_element_type=F32)
        r = lax.rsqrt(jnp.mean(x2 * x2, axis=-1, keepdims=True) + RMS_EPS)
        yn = x2 * r
        err = yn * gf_ref[...] - t_ref[...]
        loss = 0.5 * jnp.sum(jnp.sum(err * err, axis=-1, keepdims=True) / D_MODEL, axis=0, keepdims=True)
        dyf = err / D_MODEL
        dgf = jnp.sum(dyf * yn, axis=0, keepdims=True)
        dyn = dyf * gf_ref[...]
        dx2 = r * (dyn - yn * jnp.mean(dyn * yn, axis=-1, keepdims=True))
        dx_ref[...] = dx2
        dxb = dx2.astype(BF16)
        dmix = lax.dot_general(dxb, w_ref[...], (((1,), (1,)), ((), ())), preferred_element_type=F32)
        col = 0
        for (g, sg, o), do_ref, dgate_ref in zip(branches, (dof_ref, dod_ref, dom_ref), (dfg_ref, ddg_ref, dmg_ref)):
            d = dmix[:, col:col + g.shape[1]]
            col += g.shape[1]
            do_ref[...] = (d * (g * sg)).astype(BF16)
            dgate_ref[...] = (d * o * (sg * (1.0 + g * (1.0 - sg)))).astype(BF16)
        row = lax.broadcasted_iota(jnp.int32, (8, D_MODEL), 0)
        upd = jnp.where(row == 0, dgf, jnp.where(row == 1, loss, 0.0))

        @pl.when(pl.program_id(0) == 0)
        def _():
            sm_ref[...] = jnp.zeros(sm_ref.shape, F32)
            gw_acc[...] = jnp.zeros(gw_acc.shape, F32)

        sm_ref[...] += upd
        gw_acc[...] += lax.dot_general(ymix, dxb, (((0,), (0,)), ((), ())), preferred_element_type=F32)

        @pl.when(pl.program_id(0) == T // tm - 1)
        def _():
            gw_ref[...] = gw_acc[...].astype(BF16)

    def rows(w, col=0):
        return pl.BlockSpec((tm, w), lambda i: (i, col))

    return pl.pallas_call(
        body, name="out_step", grid=(T // tm,),
        in_specs=[rows(FOX_W, P_FG // FOX_W), rows(DIL_W, P_DG // DIL_W), rows(MEM_W, P_MG // MEM_W),
                  rows(FOX_W), rows(DIL_W), rows(MEM_W),
                  pl.BlockSpec((MIX_W, D_MODEL), lambda i: (0, 0)),
                  rows(D_MODEL), rows(D_MODEL), pl.BlockSpec((1, D_MODEL), lambda i: (0, 0))],
        out_specs=[rows(D_MODEL), rows(FOX_W), rows(DIL_W), rows(MEM_W), rows(FOX_W), rows(DIL_W), rows(MEM_W),
                   pl.BlockSpec((MIX_W, D_MODEL), lambda i: (0, 0)), pl.BlockSpec((8, D_MODEL), lambda i: (0, 0))],
        out_shape=[jax.ShapeDtypeStruct((T, D_MODEL), F32), jax.ShapeDtypeStruct((T, FOX_W), BF16),
                   jax.ShapeDtypeStruct((T, DIL_W), BF16), jax.ShapeDtypeStruct((T, MEM_W), BF16),
                   jax.ShapeDtypeStruct((T, FOX_W), BF16), jax.ShapeDtypeStruct((T, DIL_W), BF16),
                   jax.ShapeDtypeStruct((T, MEM_W), BF16), jax.ShapeDtypeStruct((MIX_W, D_MODEL), BF16),
                   jax.ShapeDtypeStruct((8, D_MODEL), F32)],
        scratch_shapes=[pltpu.VMEM((MIX_W, D_MODEL), F32)],
        compiler_params=_params(("arbitrary",)),
    )(proj, proj, proj, o_fox, o_dil, o_mem, w_out, x, target, gf)


def adamw(w, g, m, v, tr, name):
    lead = w.shape[:-2]
    R, C = w.shape[-2:]
    zeros = (0,) * len(lead)

    def body(w_ref, g_ref, m_ref, v_ref, d_ref, mo_ref, vo_ref):
        gv = g_ref[...]
        mn = ADAM_B1 * m_ref[...] + (1.0 - ADAM_B1) * gv
        vn = ADAM_B2 * v_ref[...] + (1.0 - ADAM_B2) * jnp.square(gv)
        m_hat = mn / (1.0 - ADAM_B1 ** ADAM_STEP)
        v_hat = vn / (1.0 - ADAM_B2 ** ADAM_STEP)
        d_ref[...] = -ADAM_LR * (m_hat / (jnp.sqrt(v_hat) + ADAM_EPS) + ADAM_WD * w_ref[...])
        mo_ref[...] = mn
        vo_ref[...] = vn

    spec = pl.BlockSpec((1,) * len(lead) + (tr, C), lambda i: zeros + (i, 0))
    return pl.pallas_call(
        body, name=name, grid=(pl.cdiv(R, tr),),
        in_specs=[spec] * 4, out_specs=[spec] * 3,
        out_shape=[jax.ShapeDtypeStruct(w.shape, F32)] * 3,
        compiler_params=_params(("arbitrary",)),
    )(w, g, m, v)


def _pad_row(v, width):
    return jnp.concatenate([v, jnp.zeros((1, width - v.shape[1]), v.dtype)], axis=1)


def local_grads(x, mem, norm_g, b_forget, mem_norm_g, final_norm_g, loss_target, w_in_p, first_token, small_weights,
                start_exchange):
    B, S, D = x.shape
    T = B * S
    xt = x.reshape(T, D)
    memt = mem.reshape(B * MEM_LEN, D)
    b_pad = _pad_row(b_forget, LANES)

    h, h_t = rms_fwd(xt, norm_g, 512, "rms_x", with_transpose=True)
    proj = mm_nn(h, w_in_p, 512, PW // 3, "in_proj", first_token)
    proj3 = proj.reshape(B, S, PW)

    negc = fox_gate(proj3, b_pad)
    causal = _log_masks_t(S, "causal")
    dilated = _log_masks_t(S, "dilated")
    rope = _rope_tables(S)

    o_fox, lse_fox = attn_fwd("fox", proj3, S, negc_cols=negc, mask=causal)
    o_dil, lse_dil = attn_fwd("dil", proj3, S, mask=dilated, rope=rope)

    w_kv, w_out = small_weights(o_dil)
    mh, mh_t = rms_fwd(memt, mem_norm_g, B * MEM_LEN, "rms_mem", with_transpose=True)
    mkv = mm_nn(mh, w_kv, B * MEM_LEN, 2 * MEM_W, "mem_kv_proj")
    mkv3 = mkv.reshape(B, MEM_LEN, 2 * MEM_W)
    o_mem, lse_mem = attn_fwd("mem", proj3, S, kv=mkv3)

    dx2, do_fox, do_dil, do_mem, dfg, ddg, dmg, g_out, small_out = out_step(
        proj, o_fox.reshape(T, FOX_W), o_dil.reshape(T, DIL_W), o_mem.reshape(T, MEM_W), w_out,
        xt, loss_target.reshape(T, D), final_norm_g.reshape(1, D), 256)

    gates = [(dfg, P_FG), (ddg, P_DG), (dmg, P_MG)]
    g_gates = mm_tn_multi(h_t, [arr for arr, _ in gates], 1024, "w_in_grad_gates", BF16)
    first, token = start_exchange([g_gates, g_out], "early_exchange_a")

    dqkv_fox, dneg, drow = attn_bwd("fox", proj3, do_fox.reshape(B, S, FOX_W), o_fox, lse_fox, S,
                                    negc_cols=negc, mask=causal, token=token)
    drow = drow.reshape(B, FOX_HEADS // 2, S // TQ, 2, TQ).transpose(0, 1, 3, 2, 4).reshape(B, FOX_HEADS, S)
    drow = jnp.pad(drow, ((0, 0), (0, LANES - FOX_HEADS), (0, 0)))
    dflog, db_part = fox_gate_bwd(drow, dneg, proj3, b_pad)
    fox = [(dqkv_fox.reshape(T, 3 * FOX_W), P_FOX), (dflog.reshape(T, LANES), P_FLOG)]
    g_fox = mm_tn_multi(h_t, [arr for arr, _ in fox], 1024, "w_in_grad_fox", BF16)
    second, token = start_exchange([g_fox], "early_exchange_b")

    (dqkv_dil,) = attn_bwd("dil", proj3, do_dil.reshape(B, S, DIL_W), o_dil, lse_dil, S, mask=dilated, rope=rope,
                           token=token)
    dil = [(dqkv_dil.reshape(T, 3 * DIL_W), P_DIL)]
    g_dil = mm_tn_multi(h_t, [arr for arr, _ in dil], 1024, "w_in_grad_dil", BF16)
    third, token = start_exchange([g_dil], "early_exchange_c")

    dmq, dmk, dmv = attn_bwd("mem", proj3, do_mem.reshape(B, S, MEM_W), o_mem, lse_mem, S, kv=mkv3, token=token)
    mq = [(dmq.reshape(T, MEM_W), P_MQ)]
    g_mq = mm_tn_multi(h_t, [arr for arr, _ in mq], 1024, "w_in_grad_mq", BF16)
    dmkv = jnp.concatenate([dmk, dmv], axis=2).reshape(B * MEM_LEN, 2 * MEM_W)
    g_kv = mm_tn_multi(mh_t, [dmkv], B * MEM_LEN, "w_kv_grad", BF16)
    fourth, token = start_exchange([g_mq, g_kv], "early_exchange_d")

    grad_x, dng = in_proj_bwd_rms(gates + fox + dil + mq, w_in_p, xt, norm_g, dx2, 256, token)
    dmh = mm_nt(dmkv, w_kv, B * MEM_LEN, D, "mem_kv_bwd")
    _, dmng = rms_bwd(memt, mem_norm_g, dmh, None, B * MEM_LEN, "rms_mem_bwd")

    small = jnp.concatenate([dng[0:1], dmng[0:1], small_out[0:1], _pad_row(db_part[0:1], D), small_out[1:2],
                             jnp.zeros((3, D), F32)], axis=0)
    early = [(first, dqkv_fox), (second, dqkv_dil), (third, dmq), (fourth, grad_x)]
    return grad_x.reshape(B, S, D), early, small


def kernel(x, mem, norm_g, w_in, b_forget, mem_norm_g, w_mem_kv, w_out, final_norm_g, loss_target, m_norm_g, m_w_in, m_b_forget, m_mem_norm_g, m_w_mem_kv, m_w_out, m_final_norm_g, v_norm_g, v_w_in, v_b_forget, v_mem_norm_g, v_w_mem_kv, v_w_out, v_final_norm_g):
    D = D_MODEL
    (w_in_full,) = weight_gather([_pack_cols(w_in).astype(BF16).reshape(w_in.shape[1], PW)])
    gather, gather_token = early_exchange_start([w_mem_kv[0].astype(BF16), w_out[0].astype(BF16)], "early_gather",
                                                gather=True, after=w_in_full)

    def small_weights(after):
        _, gathered = early_exchange_wait(gather, after, "early_gather_wait")
        return gathered

    grad_x, early, small = local_grads(
        x, mem, norm_g, b_forget, mem_norm_g, final_norm_g, loss_target, w_in_full, gather_token, small_weights,
        early_exchange_start)

    (first, after_first), (second, after_second), (third, after_third), (fourth, after_fourth) = early
    (src_gates, src_out), (land_gates, land_out) = early_exchange_wait(first, after_first, "early_wait_a")
    (src_fox,), (land_fox,) = early_exchange_wait(second, after_second, "early_wait_b")
    (src_dil,), (land_dil,) = early_exchange_wait(third, after_third, "early_wait_c")
    (src_mq, src_kv), (land_mq, land_kv) = early_exchange_wait(fourth, after_fourth, "early_wait_d")
    gates = slot_sum8(src_gates, land_gates, 128, "sum_w_in_gates")
    gw_out = slot_sum8(src_out, land_out, 256, "sum_w_out")
    fox = slot_sum8(src_fox, land_fox, 128, "sum_w_in_fox")
    dil = slot_sum8(src_dil, land_dil, 128, "sum_w_in_dil")
    mq = slot_sum8(src_mq, land_mq, 128, "sum_w_in_mq")
    gw_kv = slot_sum8(src_kv, land_kv, 128, "sum_w_kv")

    tot = small_all_reduce(small)
    gw_in = _unpack_cols(jnp.concatenate(
        [fox[:, :3 * FOX_W], gates[:, :FOX_W], dil, gates[:, FOX_W:FOX_W + DIL_W], mq,
         gates[:, FOX_W + DIL_W:], fox[:, 3 * FOX_W:]], axis=1)[None])

    loss = tot[4, 0]
    g_norm, g_mem_norm, g_final, g_b = tot[0:1], tot[1:2], tot[2], tot[3:4, :FOX_HEADS]

    def rows8(*rows):
        rows = [r.reshape(1, -1) for r in rows]
        rows = [_pad_row(r, D) for r in rows]
        return jnp.concatenate(rows + [jnp.zeros((8 - len(rows), D), F32)], axis=0)

    sw = rows8(norm_g, mem_norm_g, final_norm_g, b_forget)
    sm = rows8(m_norm_g, m_mem_norm_g, m_final_norm_g, m_b_forget)
    sv = rows8(v_norm_g, v_mem_norm_g, v_final_norm_g, v_b_forget)
    d_s, m_s, v_s = adamw(sw, tot, sm, sv, 8, "adamw_small")
    d_in, m_in, v_in = adamw(w_in, gw_in, m_w_in, v_w_in, 32, "adamw_w_in")
    d_kv, m_kv, v_kv = adamw(w_mem_kv[0], gw_kv, m_w_mem_kv[0], v_w_mem_kv[0], 128, "adamw_w_kv")
    d_out, m_out, v_out = adamw(w_out[0], gw_out, m_w_out[0], v_w_out[0], 256, "adamw_w_out")

    def small_outs(t):
        return t[0:1], t[3:4, :FOX_HEADS], t[1:2], t[2]

    grads = (g_norm, gw_in, g_b, g_mem_norm, gw_kv[None], gw_out[None], g_final)
    outs = []
    for t, big in ((d_s, (d_in, d_kv, d_out)), (m_s, (m_in, m_kv, m_out)), (v_s, (v_in, v_kv, v_out))):
        n, b, mn, f = small_outs(t)
        outs += [n, big[0], b, mn, big[1][None], big[2][None], f]
    return (loss, grad_x, *grads, *outs)
```

```python
import math

import numpy as np
import jax
import jax.numpy as jnp
from jax import lax
from jax.experimental import pallas as pl
from jax.experimental.pallas import tpu as pltpu

F32 = jnp.float32
BF16 = jnp.bfloat16

D_MODEL = 1024
HEAD_DIM = 64
FOX_HEADS = 12
DIL_HEADS = 12
MEM_HEADS = 4
MEM_HEAD_DIM = 128
MEM_LEN = 256
FOX_W = FOX_HEADS * HEAD_DIM
DIL_W = DIL_HEADS * HEAD_DIM
MEM_W = MEM_HEADS * MEM_HEAD_DIM
MIX_W = FOX_W + DIL_W + MEM_W
DILATIONS = ((128, 1), (512, 4), (2048, 16))
ROPE_THETA = 500000.0
ROPE_DIM = HEAD_DIM // 4
RMS_EPS = 1e-6
NEG_INF = -1e30
IN_W = 4 * FOX_W + FOX_HEADS + 4 * DIL_W + 2 * MEM_W

ADAM_LR = 0.001
ADAM_B1 = 0.9
ADAM_B2 = 0.999
ADAM_EPS = 1e-08
ADAM_WD = 0.01
ADAM_STEP = 10

N_DEV = 8
LANES = 128
PAIR_W = 3 * LANES
TQ = 256
TK = 256

O_FQ, O_FK, O_FV, O_FG = 0, FOX_W, 2 * FOX_W, 3 * FOX_W
O_FLOG = 4 * FOX_W
O_DQ = O_FLOG + FOX_HEADS
O_DK, O_DV, O_DG = O_DQ + DIL_W, O_DQ + 2 * DIL_W, O_DQ + 3 * DIL_W
O_MQ = O_DQ + 4 * DIL_W
O_MG = O_MQ + MEM_W
P_FOX = 0
P_FG = P_FOX + 3 * FOX_W
P_DIL = P_FG + FOX_W
P_DG = P_DIL + 3 * DIL_W
P_MQ = P_DG + DIL_W
P_MG = P_MQ + MEM_W
P_FLOG = P_MG + MEM_W
PW = P_FLOG + LANES

VMEM_LIMIT = 56 * 1024 * 1024


def _pack_pieces():
    pieces = []
    for base in (O_FQ, O_DQ):
        seg = []
        for hp in range(FOX_HEADS // 2):
            for part in range(3):
                seg.append((base + part * FOX_W + hp * LANES, LANES))
        pieces.append(seg)
    fox, dil = pieces
    return fox + [(O_FG, FOX_W)] + dil + [(O_DG, DIL_W), (O_MQ, MEM_W), (O_MG, MEM_W), (O_FLOG, FOX_HEADS)]


def _pack_cols(w):
    parts = [w[..., s:s + n] for s, n in _pack_pieces()]
    parts.append(jnp.zeros(w.shape[:-1] + (LANES - FOX_HEADS,), w.dtype))
    return jnp.concatenate(parts, axis=-1)


def _unpack_cols(g):
    runs = []
    pos = 0
    for s, n in _pack_pieces():
        runs.append((s, n, pos))
        pos += n
    runs.sort()
    return jnp.concatenate([g[..., p:p + n] for s, n, p in runs], axis=-1)


def _params(sem=None, **kw):
    return pltpu.CompilerParams(dimension_semantics=sem, vmem_limit_bytes=VMEM_LIMIT, **kw)


def _mesh_pos():
    return lax.axis_index("x"), lax.axis_index("y"), lax.axis_index("c")


def _flip(v, d):
    return 1 - v if d else v


_RELATIONS = [(dx, dy, dc) for dx in (0, 1) for dy in (0, 1) for dc in (0, 1)][1:]


def weight_gather(shards):
    n_arr = len(shards)
    rows = [s.shape[0] for s in shards]

    def body(*refs):
        in_refs = refs[:n_arr]
        out_refs = refs[n_arr:2 * n_arr]
        send_sems, recv_sems, local_sems = refs[2 * n_arr:]
        x, y, c = _mesh_pos()
        me, sibling = (x, y, c), (x, y, 1 - c)
        x_nbr, y_nbr, diag = (1 - x, y, c), (x, 1 - y, c), (1 - x, 1 - y, c)
        north = c == 1
        relay_from = (jnp.where(north, 1 - x, x), jnp.where(north, y, 1 - y), c)
        relay_to = (jnp.where(north, x, 1 - x), jnp.where(north, 1 - y, y), c)
        k_from = jnp.where(north, 1, 2)
        k_to = 3 - k_from

        def block(a, pos):
            px, py, pc = pos
            return out_refs[a].at[pl.ds((4 * px + 2 * py + pc) * rows[a], rows[a]), :]

        def copy(a, k, blk, to, src=None):
            return pltpu.make_async_remote_copy(
                src_ref=block(a, blk) if src is None else src, dst_ref=block(a, blk),
                send_sem=send_sems.at[a, k], recv_sem=recv_sems.at[a, k],
                device_id=to, device_id_type=pl.DeviceIdType.MESH)

        started = []
        mine = []
        for a in range(n_arr):
            cp = pltpu.make_async_copy(in_refs[a], block(a, me), local_sems.at[a])
            cp.start()
            mine.append(cp)
            first = [copy(a, 0, me, sibling, src=in_refs[a]), copy(a, 1, me, x_nbr, src=in_refs[a]),
                     copy(a, 2, me, y_nbr, src=in_refs[a])]
            for cp in first:
                cp.start()
            started += first
        for a in range(n_arr):
            copy(a, k_from, relay_from, me).wait_recv()
            second_hop = copy(a, 3, relay_from, relay_to)
            second_hop.start()
            passed = copy(a, 3 + k_from, relay_from, sibling)
            passed.start()
            started += [second_hop, passed]
        for a in range(n_arr):
            copy(a, k_to, relay_to, me).wait_recv()
            passed = copy(a, 3 + k_to, relay_to, sibling)
            passed.start()
            started.append(passed)
        for a in range(n_arr):
            copy(a, 3, diag, me).wait_recv()
            passed = copy(a, 6, diag, sibling)
            passed.start()
            started.append(passed)
        for a in range(n_arr):
            copy(a, 0, sibling, me).wait_recv()
            for k, chip in ((4, x_nbr), (5, y_nbr), (6, diag)):
                copy(a, k, (chip[0], chip[1], 1 - c), me).wait_recv()
        for cp in started:
            cp.wait_send()
        for cp in mine:
            cp.wait()

    any_spec = pl.BlockSpec(memory_space=pl.ANY)
    return pl.pallas_call(
        body, name="weight_gather",
        out_shape=[jax.ShapeDtypeStruct((N_DEV * s.shape[0], s.shape[1]), s.dtype) for s in shards],
        in_specs=[any_spec] * n_arr, out_specs=[any_spec] * n_arr,
        scratch_shapes=[pltpu.SemaphoreType.DMA((n_arr, 7)), pltpu.SemaphoreType.DMA((n_arr, 7)),
                        pltpu.SemaphoreType.DMA((n_arr,))],
    )(*shards)


N_CHIP = 4
_OTHER_CHIPS = [(1, 0), (0, 1), (1, 1)]


def small_all_reduce(small):
    vmem_spec = pl.BlockSpec(memory_space=pltpu.VMEM)

    def chip_body(small_ref, csum_ref, land, send_sem, recv_sem):
        x, y, c = _mesh_pos()
        swap = pltpu.make_async_remote_copy(
            src_ref=small_ref, dst_ref=land, send_sem=send_sem, recv_sem=recv_sem,
            device_id=(x, y, 1 - c), device_id_type=pl.DeviceIdType.MESH)
        swap.start()
        swap.wait_recv()
        swap.wait_send()
        csum_ref[...] = small_ref[...] + land[...]

    csum = pl.pallas_call(
        chip_body, name="small_sum_d2d", out_shape=jax.ShapeDtypeStruct(small.shape, small.dtype),
        in_specs=[vmem_spec], out_specs=vmem_spec,
        scratch_shapes=[pltpu.VMEM(small.shape, small.dtype), pltpu.SemaphoreType.DMA, pltpu.SemaphoreType.DMA],
    )(small)

    def all_body(csum_ref, tot_ref, land, send_sems, recv_sems):
        x, y, c = _mesh_pos()
        q_me = 2 * x + y
        land[q_me] = csum_ref[...]
        sends, recvs = [], []
        for j, (dx, dy) in enumerate(_OTHER_CHIPS):
            px, py = _flip(x, dx), _flip(y, dy)
            common = dict(send_sem=send_sems.at[j], recv_sem=recv_sems.at[j],
                          device_id=(px, py, c), device_id_type=pl.DeviceIdType.MESH)
            sends.append(pltpu.make_async_remote_copy(src_ref=csum_ref, dst_ref=land.at[q_me], **common))
            recvs.append(pltpu.make_async_remote_copy(src_ref=csum_ref, dst_ref=land.at[2 * px + py], **common))
        for cp in sends:
            cp.start()
        for cp in recvs:
            cp.wait_recv()
        for cp in sends:
            cp.wait_send()
        tot = land[0]
        for q in range(1, N_CHIP):
            tot = tot + land[q]
        tot_ref[...] = tot

    return pl.pallas_call(
        all_body, name="small_sum_ici", out_shape=jax.ShapeDtypeStruct(small.shape, small.dtype),
        in_specs=[vmem_spec], out_specs=vmem_spec,
        scratch_shapes=[pltpu.VMEM((N_CHIP,) + small.shape, small.dtype),
                        pltpu.SemaphoreType.DMA((3,)), pltpu.SemaphoreType.DMA((3,))],
    )(csum)


_HBM = pl.BlockSpec(memory_space=pltpu.HBM)
_SEM = pl.BlockSpec(memory_space=pltpu.SEMAPHORE)
_EFFECT = pltpu.SideEffectType.DATAFLOW_SIDE_EFFECTING


def _early_copies(src_refs, land_refs, send_sems, recv_sems, rows, gather):
    x, y, c = _mesh_pos()
    me = 4 * x + 2 * y + c
    copies = []
    for a in range(len(src_refs)):
        for dx, dy, dc in _RELATIONS:
            px, py, pc = _flip(x, dx), _flip(y, dy), _flip(c, dc)
            peer = 4 * px + 2 * py + pc
            copies.append(pltpu.make_async_remote_copy(
                src_ref=src_refs[a] if gather else src_refs[a].at[pl.ds(peer * rows[a], rows[a]), :],
                dst_ref=land_refs[a].at[pl.ds(me * rows[a], rows[a]), :],
                send_sem=send_sems[a], recv_sem=recv_sems[a],
                device_id=(px, py, pc), device_id_type=pl.DeviceIdType.MESH))
    return copies


def early_exchange_start(srcs, name, gather=False, after=None):
    n = len(srcs)
    if gather:
        rows = [s.shape[0] for s in srcs]
        me = 4 * lax.axis_index("x") + 2 * lax.axis_index("y") + lax.axis_index("c")
        lands = [lax.dynamic_update_slice(lax.empty((N_DEV * r, s.shape[1]), s.dtype), s, (me * r, 0))
                 for r, s in zip(rows, srcs)]
    else:
        rows = [s.shape[0] // N_DEV for s in srcs]
        lands = [lax.empty(s.shape, s.dtype) for s in srcs]

    extra = [] if after is None else [after]

    def body(*refs):
        src_refs, land_refs = refs[:n], refs[n:2 * n]
        first_sem = 2 * n + len(extra)
        send_sems, recv_sems = refs[first_sem:first_sem + n], refs[first_sem + n:first_sem + 2 * n]
        token = refs[-1]
        for cp in _early_copies(src_refs, land_refs, send_sems, recv_sems, rows, gather):
            cp.start()
        token[...] = jnp.zeros_like(token)

    hbm = lambda a: pltpu.HBM(a.shape, a.dtype)
    outs = pl.pallas_call(
        body, name=name,
        out_shape=[pltpu.SemaphoreType.DMA(())] * (2 * n)
        + [hbm(a) for a in srcs] + [hbm(a) for a in lands] + [jax.ShapeDtypeStruct((8, LANES), F32)],
        in_specs=[_HBM] * (2 * n) + [pl.BlockSpec(memory_space=pl.ANY)] * len(extra),
        out_specs=[_SEM] * (2 * n) + [_HBM] * (2 * n) + [pl.BlockSpec(memory_space=pltpu.VMEM)],
        input_output_aliases={i: 2 * n + i for i in range(2 * n)},
        compiler_params=pltpu.CompilerParams(has_side_effects=_EFFECT),
    )(*[pltpu.with_memory_space_constraint(a, pltpu.HBM) for a in list(srcs) + lands], *extra)
    return dict(sems=outs[:2 * n], srcs=outs[2 * n:3 * n], lands=outs[3 * n:4 * n], rows=rows), outs[-1]


def early_exchange_wait(handle, after, name):
    n = len(handle["srcs"])
    rows = handle["rows"]

    def body(*refs):
        src_refs, land_refs = refs[:n], refs[n:2 * n]
        send_sems, recv_sems = refs[2 * n:3 * n], refs[3 * n:4 * n]
        x, y, c = _mesh_pos()
        for a in range(n):
            seven = pl.ds(0, 7 * rows[a])
            all_seven = pltpu.make_async_remote_copy(
                src_ref=land_refs[a].at[seven, :], dst_ref=land_refs[a].at[seven, :],
                send_sem=send_sems[a], recv_sem=recv_sems[a],
                device_id=(x, y, c), device_id_type=pl.DeviceIdType.MESH)
            all_seven.wait_send()
            all_seven.wait_recv()

    hbm = lambda a: pltpu.HBM(a.shape, a.dtype)
    ins = list(handle["srcs"]) + list(handle["lands"])
    outs = pl.pallas_call(
        body, name=name,
        out_shape=[hbm(a) for a in ins],
        in_specs=[_HBM] * (2 * n) + [_SEM] * (2 * n) + [pl.BlockSpec(memory_space=pl.ANY)],
        out_specs=[_HBM] * (2 * n),
        input_output_aliases={i: i for i in range(2 * n)},
        compiler_params=pltpu.CompilerParams(has_side_effects=_EFFECT),
    )(*ins, *handle["sems"], after)
    return outs[:n], outs[n:]


def slot_sum8(src, land, tr, name):
    rows, cols = land.shape[0] // N_DEV, land.shape[1]
    x, y, c = _mesh_pos()
    me = (4 * x + 2 * y + c).astype(jnp.int32).reshape(1)

    def body(me_ref, src_ref, land_ref, o_ref):
        acc = None
        for d in range(N_DEV):
            term = jnp.where(d == me_ref[0], src_ref[0], land_ref[d]).astype(F32)
            acc = term if acc is None else acc + term
        o_ref[...] = acc

    return pl.pallas_call(
        body, name=name,
        grid_spec=pltpu.PrefetchScalarGridSpec(
            num_scalar_prefetch=1, grid=(rows // tr,),
            in_specs=[pl.BlockSpec((1, tr, cols), lambda i, w: (w[0], i, 0)),
                      pl.BlockSpec((N_DEV, tr, cols), lambda i, w: (0, i, 0))],
            out_specs=pl.BlockSpec((tr, cols), lambda i, w: (i, 0))),
        out_shape=jax.ShapeDtypeStruct((rows, cols), F32),
        compiler_params=_params(("arbitrary",)),
    )(me, src.reshape(N_DEV, rows, cols), land.reshape(N_DEV, rows, cols))


def mm_tn_multi(a_t, bs, tt, name, out_dtype=F32):
    K, T = a_t.shape
    widths = [b.shape[1] for b in bs]
    steps = T // tt

    def body(a_ref, *rest):
        b_refs, o_ref, acc = rest[:-2], rest[-2], rest[-1]

        @pl.when(pl.program_id(0) == 0)
        def _():
            acc[...] = jnp.zeros(acc.shape, F32)

        av = a_ref[...]
        col = 0
        for b_ref, w in zip(b_refs, widths):
            acc[:, col:col + w] += jnp.dot(av, b_ref[...], preferred_element_type=F32)
            col += w

        @pl.when(pl.program_id(0) == steps - 1)
        def _():
            o_ref[...] = acc[...].astype(out_dtype)

    return pl.pallas_call(
        body, name=name, grid=(steps,),
        in_specs=[pl.BlockSpec((K, tt), lambda t: (0, t))] + [pl.BlockSpec((tt, w), lambda t: (t, 0)) for w in widths],
        out_specs=pl.BlockSpec((K, sum(widths)), lambda t: (0, 0)),
        out_shape=jax.ShapeDtypeStruct((K, sum(widths)), out_dtype),
        scratch_shapes=[pltpu.VMEM((K, sum(widths)), F32)],
        compiler_params=_params(("arbitrary",)),
    )(a_t, *bs)


def rms_fwd(x, g, tm, name, with_transpose=False):
    M, K = x.shape

    def body(x_ref, g_ref, o_ref, *t_ref):
        xv = x_ref[...]
        r = lax.rsqrt(jnp.mean(xv * xv, axis=-1, keepdims=True) + RMS_EPS)
        h = ((xv * r) * g_ref[...]).astype(BF16)
        o_ref[...] = h
        if with_transpose:
            t_ref[0][...] = h.T

    out_specs = [pl.BlockSpec((tm, K), lambda i: (i, 0))]
    out_shape = [jax.ShapeDtypeStruct((M, K), BF16)]
    if with_transpose:
        out_specs.append(pl.BlockSpec((K, tm), lambda i: (0, i)))
        out_shape.append(jax.ShapeDtypeStruct((K, M), BF16))
    outs = pl.pallas_call(
        body, name=name, grid=(M // tm,),
        in_specs=[pl.BlockSpec((tm, K), lambda i: (i, 0)), pl.BlockSpec((1, K), lambda i: (0, 0))],
        out_specs=out_specs, out_shape=out_shape,
        compiler_params=_params(("arbitrary",)),
    )(x, g)
    return outs if with_transpose else outs[0]


def rms_bwd(x, g, dh, dres, tm, name):
    M, K = x.shape
    has_res = dres is not None

    def body(*refs):
        if has_res:
            x_ref, g_ref, dh_ref, dres_ref, dx_ref, dg_ref = refs
        else:
            x_ref, g_ref, dh_ref, dx_ref, dg_ref = refs
        xv = x_ref[...]
        r = lax.rsqrt(jnp.mean(xv * xv, axis=-1, keepdims=True) + RMS_EPS)
        xn = xv * r
        dhv = dh_ref[...]
        dxn = dhv * g_ref[...]
        dx = r * (dxn - xn * jnp.mean(dxn * xn, axis=-1, keepdims=True))
        if has_res:
            dx = dx + dres_ref[...]
        dx_ref[...] = dx
        part = jnp.sum(dhv * xn, axis=0, keepdims=True)
        row = lax.broadcasted_iota(jnp.int32, (8, K), 0)
        upd = jnp.where(row == 0, part, 0.0)

        @pl.when(pl.program_id(0) == 0)
        def _():
            dg_ref[...] = upd

        @pl.when(pl.program_id(0) != 0)
        def _():
            dg_ref[...] += upd

    row_spec = pl.BlockSpec((tm, K), lambda i: (i, 0))
    ins = [x, g, dh] + ([dres] if has_res else [])
    in_specs = [row_spec, pl.BlockSpec((1, K), lambda i: (0, 0)), row_spec] + ([row_spec] if has_res else [])
    return pl.pallas_call(
        body, name=name, grid=(M // tm,),
        in_specs=in_specs,
        out_specs=[row_spec, pl.BlockSpec((8, K), lambda i: (0, 0))],
        out_shape=[jax.ShapeDtypeStruct((M, K), F32), jax.ShapeDtypeStruct((8, K), F32)],
        compiler_params=_params(("arbitrary",)),
    )(*ins)


def mm_nn(a, b, tm, tn, name, token=None):
    M, K = a.shape
    N = b.shape[1]
    extra = [] if token is None else [token]

    def body(a_ref, b_ref, *rest):
        rest[-1][...] = jnp.dot(a_ref[...], b_ref[...], preferred_element_type=F32)

    return pl.pallas_call(
        body, name=name, grid=(N // tn, M // tm),
        in_specs=[pl.BlockSpec((tm, K), lambda j, i: (i, 0)), pl.BlockSpec((K, tn), lambda j, i: (0, j))]
        + [pl.BlockSpec(t.shape, lambda j, i: (0, 0)) for t in extra],
        out_specs=pl.BlockSpec((tm, tn), lambda j, i: (i, j)),
        out_shape=jax.ShapeDtypeStruct((M, N), F32),
        compiler_params=_params(("arbitrary", "arbitrary")),
    )(a, b, *extra)


def mm_nt(a, b, tm, tk, name):
    M, K = a.shape
    N = b.shape[0]

    def body(a_ref, b_ref, o_ref):
        part = lax.dot_general(a_ref[...], b_ref[...], (((1,), (1,)), ((), ())), preferred_element_type=F32)

        @pl.when(pl.program_id(1) == 0)
        def _():
            o_ref[...] = part

        @pl.when(pl.program_id(1) != 0)
        def _():
            o_ref[...] += part

    return pl.pallas_call(
        body, name=name, grid=(M // tm, K // tk),
        in_specs=[pl.BlockSpec((tm, tk), lambda i, k: (i, k)), pl.BlockSpec((N, tk), lambda i, k: (0, k))],
        out_specs=pl.BlockSpec((tm, N), lambda i, k: (i, 0)),
        out_shape=jax.ShapeDtypeStruct((M, N), F32),
        compiler_params=_params(("arbitrary", "arbitrary")),
    )(a, b)


def in_proj_bwd_rms(pieces, w, x, g, dres, tm, token):
    M, N = x.shape

    def body(*refs):
        n = len(pieces)
        p_refs, w_ref, x_ref, g_ref, dres_ref, _, dx_ref, dg_ref = refs[:n], *refs[n:]
        dh = None
        for p_ref, (arr, col) in zip(p_refs, pieces):
            part = lax.dot_general(p_ref[...], w_ref[:, col:col + arr.shape[1]], (((1,), (1,)), ((), ())),
                                   preferred_element_type=F32)
            dh = part if dh is None else dh + part
        xv = x_ref[...]
        r = lax.rsqrt(jnp.mean(xv * xv, axis=-1, keepdims=True) + RMS_EPS)
        xn = xv * r
        dxn = dh * g_ref[...]
        dx_ref[...] = r * (dxn - xn * jnp.mean(dxn * xn, axis=-1, keepdims=True)) + dres_ref[...]
        row = lax.broadcasted_iota(jnp.int32, (8, N), 0)
        upd = jnp.where(row == 0, jnp.sum(dh * xn, axis=0, keepdims=True), 0.0)

        @pl.when(pl.program_id(0) == 0)
        def _():
            dg_ref[...] = upd

        @pl.when(pl.program_id(0) != 0)
        def _():
            dg_ref[...] += upd

    row_spec = pl.BlockSpec((tm, N), lambda i: (i, 0))
    return pl.pallas_call(
        body, name="in_proj_bwd", grid=(M // tm,),
        in_specs=[pl.BlockSpec((tm, arr.shape[1]), lambda i: (i, 0)) for arr, _ in pieces]
        + [pl.BlockSpec(w.shape, lambda i: (0, 0)), row_spec, pl.BlockSpec((1, N), lambda i: (0, 0)), row_spec,
           pl.BlockSpec(token.shape, lambda i: (0, 0))],
        out_specs=[row_spec, pl.BlockSpec((8, N), lambda i: (0, 0))],
        out_shape=[jax.ShapeDtypeStruct((M, N), F32), jax.ShapeDtypeStruct((8, N), F32)],
        compiler_params=_params(("arbitrary",)),
    )(*[arr for arr, _ in pieces], w, x, g, dres, token)


def _log_sigmoid(z):
    return jnp.minimum(z, 0.0) - jnp.log(1.0 + jnp.exp(-jnp.abs(z)))


def _tri(n, lower):
    r = lax.broadcasted_iota(jnp.int32, (n, n), 0)
    c = lax.broadcasted_iota(jnp.int32, (n, n), 1)
    return jnp.where((r >= c) if lower else (r <= c), 1.0, 0.0).astype(F32)


def fox_gate(proj3, b_pad):
    B, S, _ = proj3.shape
    nblk = S // TK

    def body(f_ref, b_ref, o_ref):
        tri = _tri(TK, True)
        carry = jnp.zeros((1, LANES), F32)
        for n in range(nblk):
            z = f_ref[0, n * TK:(n + 1) * TK, :] + b_ref[...]
            logf = _log_sigmoid(z)
            cs = jnp.dot(tri, logf, preferred_element_type=F32, precision=lax.Precision.HIGHEST) + carry
            carry = cs[TK - 1:TK, :]
            o_ref[0, n * TK:(n + 1) * TK, :] = -cs

    return pl.pallas_call(
        body, name="fox_gate", grid=(B,),
        in_specs=[pl.BlockSpec((1, S, LANES), lambda b: (b, 0, P_FLOG // LANES)),
                  pl.BlockSpec((1, LANES), lambda b: (0, 0))],
        out_specs=pl.BlockSpec((1, S, LANES), lambda b: (b, 0, 0)),
        out_shape=jax.ShapeDtypeStruct((B, S, LANES), F32),
        compiler_params=_params(("arbitrary",)),
    )(proj3, b_pad)


def fox_gate_bwd(drow, dneg, proj3, b_pad):
    B, S, _ = proj3.shape
    nblk = S // TK

    def body(d_ref, r_ref, f_ref, b_ref, o_ref, db_ref):
        tri = _tri(TK, False)
        lane = lax.broadcasted_iota(jnp.int32, (TK, LANES), 1)
        carry = jnp.zeros((1, LANES), F32)
        dbsum = jnp.zeros((1, LANES), F32)
        for n in reversed(range(nblk)):
            dk_side = None
            for hp in range(FOX_HEADS // 2):
                two = jnp.where(lane < 2, r_ref[0, n * TK:(n + 1) * TK, hp * LANES:(hp + 1) * LANES], 0.0)
                two = pltpu.roll(two, 2 * hp, 1) if hp else two
                dk_side = two if dk_side is None else dk_side + two
            dc = jnp.where(lane < FOX_HEADS, d_ref[0, :, n * TK:(n + 1) * TK].T - dk_side, 0.0)
            rs = jnp.dot(tri, dc, preferred_element_type=F32, precision=lax.Precision.HIGHEST) + carry
            carry = rs[0:1, :]
            z = f_ref[0, n * TK:(n + 1) * TK, :] + b_ref[...]
            dz = rs * (1.0 / (1.0 + jnp.exp(z)))
            o_ref[0, n * TK:(n + 1) * TK, :] = dz.astype(BF16)
            dbsum = dbsum + jnp.sum(dz, axis=0, keepdims=True)
        row = lax.broadcasted_iota(jnp.int32, (8, LANES), 0)
        upd = jnp.where(row == 0, dbsum, 0.0)

        @pl.when(pl.program_id(0) == 0)
        def _():
            db_ref[...] = upd

        @pl.when(pl.program_id(0) != 0)
        def _():
            db_ref[...] += upd

    return pl.pallas_call(
        body, name="fox_gate_bwd", grid=(B,),
        in_specs=[pl.BlockSpec((1, LANES, S), lambda b: (b, 0, 0)),
                  pl.BlockSpec((1, S, FOX_W), lambda b: (b, 0, 0)),
                  pl.BlockSpec((1, S, LANES), lambda b: (b, 0, P_FLOG // LANES)),
                  pl.BlockSpec((1, LANES), lambda b: (0, 0))],
        out_specs=[pl.BlockSpec((1, S, LANES), lambda b: (b, 0, 0)), pl.BlockSpec((8, LANES), lambda b: (0, 0))],
        out_shape=[jax.ShapeDtypeStruct((B, S, LANES), BF16), jax.ShapeDtypeStruct((8, LANES), F32)],
        compiler_params=_params(("arbitrary",)),
    )(drow, dneg, proj3, b_pad)


def _rope_tables(S):
    half = ROPE_DIM // 2
    f32 = np.float32
    pos = np.arange(S, dtype=f32)
    inv_freq = f32(1.0) / np.power(f32(ROPE_THETA), np.arange(0, ROPE_DIM, 2, dtype=f32) / f32(ROPE_DIM)).astype(f32)
    ang = (pos[:, None] * inv_freq[None, :]).astype(f32).astype(np.float64)
    cos, sin = np.cos(ang).astype(f32), np.sin(ang).astype(f32)
    one = np.ones((S, HEAD_DIM - ROPE_DIM), f32)
    zero = np.zeros((S, HEAD_DIM - ROPE_DIM), f32)
    zh = np.zeros((S, half), f32)
    c = np.concatenate([cos, cos, one], axis=1)
    s1 = np.concatenate([-sin, zh, zero], axis=1)
    s2 = np.concatenate([zh, sin, zero], axis=1)
    return tuple(jnp.asarray(np.concatenate([t, t], axis=1)) for t in (c, s1, s2))


_HALF_ROPE = ROPE_DIM // 2


def _rope(t, c, s1, s2):
    return t * c + pltpu.roll(t, LANES - _HALF_ROPE, 1) * s1 + pltpu.roll(t, _HALF_ROPE, 1) * s2


def _rope_bwd(d, c, s1, s2):
    return d * c + pltpu.roll(d * s1, _HALF_ROPE, 1) + pltpu.roll(d * s2, LANES - _HALF_ROPE, 1)


def _scale_parts(scale):
    m, _ = math.frexp(scale)
    return (scale, None) if m == 0.5 else (None, scale)


def _log_masks(S, kind):
    nd = 1 if kind == "causal" else S // TQ
    a = np.arange(TQ)[:, None]
    b = np.arange(TK)[None, :]
    out = np.zeros((nd, TQ, TK), np.float32)
    for d in range(nd):
        delta = d * TQ + a - b
        if kind == "causal":
            m = (delta >= 0).astype(np.float64)
        else:
            m = sum(((delta >= 0) & (delta % dil == 0) & (delta <= w)).astype(np.float64) for w, dil in DILATIONS)
        out[d] = np.where(m > 0, np.log(np.maximum(m, 1.0)), NEG_INF)
    return jnp.asarray(out)


def _attn_setup(kind):
    pair = kind != "mem"
    e_dim = HEAD_DIM if pair else MEM_HEAD_DIM
    q_fold, s_scale = _scale_parts(1.0 / math.sqrt(e_dim))
    return dict(pair=pair, col0={"fox": P_FOX, "dil": P_DIL, "mem": P_MQ}[kind],
                n_blocks=FOX_HEADS // 2 if pair else MEM_HEADS, q_fold=q_fold, s_scale=s_scale,
                nh=2 if pair else 1)


def _store_stacked(cfg, lane, dst, n, val):
    nh = cfg["nh"]
    R = nh * TQ
    if nh == 1:
        dst[n * R:n * R + TQ, :] = val
        return
    for hh in range(nh):
        hmask = (lane >= HEAD_DIM * hh) & (lane < HEAD_DIM * (hh + 1))
        dst[n * R + hh * TQ:n * R + (hh + 1) * TQ, :] = jnp.where(hmask, val, jnp.zeros_like(val))


def _cat(parts, axis):
    return parts[0] if len(parts) == 1 else jnp.concatenate(parts, axis=axis)


def _log_masks_t(S, kind):
    return jnp.swapaxes(_log_masks(S, kind), 1, 2)


def _head_rows(hh, pair):
    row = lax.broadcasted_iota(jnp.int32, (LANES, 1), 0)
    if not pair:
        return row >= 0
    return (row >= HEAD_DIM * hh) & (row < HEAD_DIM * (hh + 1))


def _attn_t_inputs(kind, src, S, negc_cols, mask, rope, kv):
    cfg = _attn_setup(kind)
    col0 = cfg["col0"]
    ins, in_specs = [], []
    if cfg["pair"]:
        ins.append(src)
        in_specs.append(pl.BlockSpec((1, S, PAIR_W), lambda b, h: (b, 0, col0 // PAIR_W + h)))
    else:
        ins += [src, kv, kv]
        in_specs += [pl.BlockSpec((1, S, LANES), lambda b, h: (b, 0, col0 // LANES + h)),
                     pl.BlockSpec((1, MEM_LEN, LANES), lambda b, h: (b, 0, h)),
                     pl.BlockSpec((1, MEM_LEN, LANES), lambda b, h: (b, 0, MEM_HEADS + h))]
    if negc_cols is not None:
        ins.append(negc_cols)
        in_specs.append(pl.BlockSpec((1, S, LANES), lambda b, h: (b, 0, 0)))
    if mask is not None:
        ins.append(mask)
        in_specs.append(pl.BlockSpec(mask.shape, lambda b, h: (0, 0, 0)))
    if rope is not None:
        ins += list(rope)
        in_specs += [pl.BlockSpec((S, LANES), lambda b, h: (0, 0))] * 3
    return ins, in_specs


def _attn_t_prep(cfg, refs, S, Sk, *, qT2s, ks, q2s=None, vs=None, vTs=None, kTs=None, nb=None):
    pair, nh = cfg["pair"], cfg["nh"]
    lane = lax.broadcasted_iota(jnp.int32, (1, LANES), 1)
    rope_refs = refs["rope"]

    def prep_q(n):
        rows = slice(n * TQ, (n + 1) * TQ)
        q = refs["load_q"](rows)
        if rope_refs is not None:
            q = _rope(q, *[t[rows, :] for t in rope_refs])
        if cfg["q_fold"] is not None:
            q = q * cfg["q_fold"]
        qb = q.astype(BF16)
        if q2s is not None:
            _store_stacked(cfg, lane, q2s, n, qb)
        qtb = qb.T
        for hh in range(nh):
            qT2s[n, :, hh * TQ:(hh + 1) * TQ] = jnp.where(_head_rows(hh, pair), qtb, jnp.zeros_like(qtb))

    def prep_kv(n):
        rows = slice(n * TK, (n + 1) * TK)
        k, v = refs["load_kv"](rows)
        if rope_refs is not None:
            k = _rope(k, *[t[rows, :] for t in rope_refs])
        kb = k.astype(BF16)
        vb = v.astype(BF16)
        ks[rows, :] = kb
        if vs is not None:
            vs[rows, :] = vb
        if vTs is not None:
            vTs[n] = vb.T
        if kTs is not None:
            kTs[n] = kb.T
        if nb is not None:
            blk = refs["negc"][0, rows, :]
            for hh in range(nh):
                h = 2 * refs["block"] + hh
                col = jnp.sum(jnp.where(lane == h, blk, 0.0), axis=1, keepdims=True)
                nb[hh, rows, :] = jnp.broadcast_to(col, (TK, LANES))

    for n in range(S // TQ):
        prep_q(n)
    for n in range(Sk // TK):
        prep_kv(n)


def _raw_scores_t(cfg, k, qT2):
    sT = jnp.dot(k, qT2, preferred_element_type=F32)
    if cfg["s_scale"] is not None:
        sT = sT * cfg["s_scale"]
    return sT


def _bias_mask_t(cfg, sT, nb, mask_ref, kc, midx):
    nh = cfg["nh"]
    if nb is None and midx is None:
        return sT
    parts = []
    for hh in range(nh):
        t = sT[:, hh * TQ:(hh + 1) * TQ]
        if nb is not None:
            t = t + jnp.concatenate([nb[hh, kc, :]] * (TQ // LANES), axis=1)
        if midx is not None:
            t = t + mask_ref[midx]
        parts.append(t)
    return _cat(parts, 1)


def _tile_pairs(kind, nq, nk):
    if kind == "mem":
        return [(i, j) for i in range(nq) for j in range(nk)], (lambda i, j: None)
    pairs = [(i, j) for i in range(nq) for j in range(i + 1)]
    if kind == "fox":
        return pairs, (lambda i, j: 0 if j == i else None)
    return pairs, (lambda i, j: i - j)


def attn_fwd(kind, src, S, *, negc_cols=None, mask=None, rope=None, kv=None):
    B = src.shape[0]
    cfg = _attn_setup(kind)
    pair, nh = cfg["pair"], cfg["nh"]
    Sk = S if pair else MEM_LEN
    has_bias, has_rope = negc_cols is not None, rope is not None
    R = nh * TQ
    nq, nk = S // TQ, Sk // TK
    pairs, mask_index = _tile_pairs(kind, nq, nk)

    def body(*refs):
        refs = list(refs)
        if pair:
            qkv_ref = refs.pop(0)
            load_q = lambda rows: qkv_ref[0, rows, 0:LANES]
            load_kv = lambda rows: (qkv_ref[0, rows, LANES:2 * LANES], qkv_ref[0, rows, 2 * LANES:3 * LANES])
        else:
            q_ref, k_ref, v_ref = refs.pop(0), refs.pop(0), refs.pop(0)
            load_q = lambda rows: q_ref[0, rows, :]
            load_kv = lambda rows: (k_ref[0, rows, :], v_ref[0, rows, :])
        negc_ref = refs.pop(0) if has_bias else None
        mask_ref = refs.pop(0) if mask is not None else None
        rope_refs = [refs.pop(0) for _ in range(3)] if has_rope else None
        o_ref, lse_ref, qT2s, ks, vTs, s_a, s_b, p_a, p_b = refs[:9]
        nb = refs[9] if has_bias else None
        _attn_t_prep(cfg, dict(load_q=load_q, load_kv=load_kv, rope=rope_refs, negc=negc_ref,
                               block=pl.program_id(1)), S, Sk, qT2s=qT2s, ks=ks, vTs=vTs, nb=nb)

        def cols(j):
            return slice(j * TK, (j + 1) * TK)

        def scores(i, j):
            return _raw_scores_t(cfg, ks[cols(j), :], qT2s[i])

        def finish(i, m, l, accT):
            oT2 = accT / l
            oT = jnp.where(_head_rows(0, True), oT2[:, 0:TQ], oT2[:, TQ:2 * TQ]) if pair else oT2
            o_ref[0, i * TQ:(i + 1) * TQ, :] = oT.T
            lse_ref[0, 0, i:i + 1, :] = m + jnp.log(l)

        s_bufs, p_bufs = (s_a, s_b), (p_a, p_b)
        s_bufs[0][...] = scores(*pairs[0])
        m = l = accT = None
        for t, (i, j) in enumerate(pairs):
            cur, oth = t % 2, 1 - t % 2
            if t > 0:
                i_prev, j_prev = pairs[t - 1]
                pv = jnp.dot(vTs[j_prev], p_bufs[oth][...], preferred_element_type=F32)
                acc_full = pv if accT is None else accT + pv
            if t + 1 < len(pairs):
                s_bufs[oth][...] = scores(*pairs[t + 1])
            first = j == 0
            if first and t > 0:
                finish(i_prev, m, l, acc_full)
            sT = _bias_mask_t(cfg, s_bufs[cur][...], nb, mask_ref, cols(j), mask_index(i, j))
            m_tile = jnp.max(sT, axis=0, keepdims=True)
            m_new = m_tile if first else jnp.maximum(m, m_tile)
            p = jnp.exp(sT - m_new)
            p_bufs[cur][...] = p.astype(BF16)
            if first:
                l, accT = jnp.sum(p, axis=0, keepdims=True), None
            else:
                alpha = jnp.exp(m - m_new)
                l, accT = alpha * l + jnp.sum(p, axis=0, keepdims=True), acc_full * alpha
            m = m_new
        i_last, j_last = pairs[-1]
        pv = jnp.dot(vTs[j_last], p_bufs[(len(pairs) - 1) % 2][...], preferred_element_type=F32)
        finish(i_last, m, l, pv if accT is None else accT + pv)

    ins, in_specs = _attn_t_inputs(kind, src, S, negc_cols, mask, rope, kv)
    W = cfg["n_blocks"] * LANES
    scratch = [pltpu.VMEM((nq, LANES, R), BF16), pltpu.VMEM((Sk, LANES), BF16), pltpu.VMEM((nk, LANES, TK), BF16),
               pltpu.VMEM((TK, R), F32), pltpu.VMEM((TK, R), F32), pltpu.VMEM((TK, R), BF16), pltpu.VMEM((TK, R), BF16)]
    if has_bias:
        scratch.append(pltpu.VMEM((nh, Sk, LANES), F32))
    return pl.pallas_call(
        body, name=kind + "_attn_fwd", grid=(B, cfg["n_blocks"]),
        in_specs=in_specs,
        out_specs=[pl.BlockSpec((1, S, LANES), lambda b, h: (b, 0, h)),
                   pl.BlockSpec((1, 1, nq, R), lambda b, h: (b, h, 0, 0))],
        out_shape=[jax.ShapeDtypeStruct((B, S, W), F32), jax.ShapeDtypeStruct((B, cfg["n_blocks"], nq, R), F32)],
        scratch_shapes=scratch,
        compiler_params=_params(("arbitrary", "arbitrary")),
    )(*ins)


def attn_bwd(kind, src, do, o, lse, S, *, negc_cols=None, mask=None, rope=None, kv=None, token=None):
    B = src.shape[0]
    cfg = _attn_setup(kind)
    pair, nh, s_scale, q_fold = cfg["pair"], cfg["nh"], cfg["s_scale"], cfg["q_fold"]
    Sk = S if pair else MEM_LEN
    has_bias, has_rope = negc_cols is not None, rope is not None
    R = nh * TQ
    nq, nk = S // TQ, Sk // TK
    pairs, mask_index = _tile_pairs(kind, nq, nk)

    def body(*refs):
        refs = list(refs)
        if pair:
            qkv_ref = refs.pop(0)
            load_q = lambda rows: qkv_ref[0, rows, 0:LANES]
            load_kv = lambda rows: (qkv_ref[0, rows, LANES:2 * LANES], qkv_ref[0, rows, 2 * LANES:3 * LANES])
        else:
            q_ref, k_ref, v_ref = refs.pop(0), refs.pop(0), refs.pop(0)
            load_q = lambda rows: q_ref[0, rows, :]
            load_kv = lambda rows: (k_ref[0, rows, :], v_ref[0, rows, :])
        negc_ref = refs.pop(0) if has_bias else None
        mask_ref = refs.pop(0) if mask is not None else None
        rope_refs = [refs.pop(0) for _ in range(3)] if has_rope else None
        do_ref, o_ref, lse_ref = refs.pop(0), refs.pop(0), refs.pop(0)
        if token is not None:
            refs.pop(0)
        if pair:
            dqkv_ref = refs.pop(0)
            dneg_ref = refs.pop(0) if has_bias else None
            drow_ref = refs.pop(0) if has_bias else None
        else:
            dq_ref, dk_ref, dv_ref = refs.pop(0), refs.pop(0), refs.pop(0)
        qT2s, ks, q2s, vs, kTs, doT2s, do2s, delta_s, dk_acc, dv_acc = refs[:10]
        bufs_a, bufs_b = refs[10:14], refs[14:18]
        nb, dneg_acc = (refs[18], refs[19]) if has_bias else (None, None)
        lane = lax.broadcasted_iota(jnp.int32, (1, LANES), 1)
        _attn_t_prep(cfg, dict(load_q=load_q, load_kv=load_kv, rope=rope_refs, negc=negc_ref,
                               block=pl.program_id(1)), S, Sk,
                     qT2s=qT2s, ks=ks, q2s=q2s, vs=vs, kTs=kTs, nb=nb)

        def prep_do(n):
            rows = slice(n * TQ, (n + 1) * TQ)
            dob = do_ref[0, rows, :].astype(BF16)
            _store_stacked(cfg, lane, do2s, n, dob)
            doT = dob.astype(F32).T
            prodT = doT * o_ref[0, rows, :].T
            doTb = doT.astype(BF16)
            for hh in range(nh):
                hm = _head_rows(hh, pair)
                doT2s[n, :, hh * TQ:(hh + 1) * TQ] = jnp.where(hm, doTb, jnp.zeros_like(doTb))
                delta_s[n:n + 1, hh * TQ:(hh + 1) * TQ] = jnp.sum(jnp.where(hm, prodT, 0.0), axis=0, keepdims=True)

        for n in range(nq):
            prep_do(n)
        dk_acc[...] = jnp.zeros(dk_acc.shape, F32)
        dv_acc[...] = jnp.zeros(dv_acc.shape, F32)
        if has_bias:
            dneg_acc[...] = jnp.zeros(dneg_acc.shape, F32)

        def cols(j):
            return slice(j * TK, (j + 1) * TK)

        def rows2(i):
            return slice(i * R, (i + 1) * R)

        def first_products(i, j, bufs):
            bufs[0][...] = _raw_scores_t(cfg, ks[cols(j), :], qT2s[i])
            bufs[1][...] = jnp.dot(vs[cols(j), :], doT2s[i], preferred_element_type=F32)

        def last_products(i, j, bufs, dqT2):
            dv_acc[cols(j), :] += jnp.dot(bufs[2][...], do2s[rows2(i), :], preferred_element_type=F32)
            dk_acc[cols(j), :] += jnp.dot(bufs[3][...], q2s[rows2(i), :], preferred_element_type=F32)
            dq = jnp.dot(kTs[j], bufs[3][...], preferred_element_type=F32)
            return dq if dqT2 is None else dqT2 + dq

        def finish_q(i, dqT2, drow):
            rows = slice(i * TQ, (i + 1) * TQ)
            dqT = jnp.where(_head_rows(0, True), dqT2[:, 0:TQ], dqT2[:, TQ:2 * TQ]) if pair else dqT2
            dq = dqT.T
            if q_fold is not None:
                dq = dq * q_fold
            if has_rope:
                dq = _rope_bwd(dq, *[t[rows, :] for t in rope_refs])
            if pair:
                dqkv_ref[0, rows, 0:LANES] = dq.astype(BF16)
            else:
                dq_ref[0, rows, :] = dq.astype(BF16)
            if has_bias:
                drow_ref[0, 0, i:i + 1, :] = drow

        bufs = (bufs_a, bufs_b)
        first_products(*pairs[0], bufs[0])
        dqT2 = drow = None
        for t, (i, j) in enumerate(pairs):
            cur, oth = bufs[t % 2], bufs[1 - t % 2]
            first = j == 0
            if first and t > 0:
                i_prev, j_prev = pairs[t - 1]
                finish_q(i_prev, last_products(i_prev, j_prev, oth, dqT2), drow)
                dqT2 = drow = None
            sT = _bias_mask_t(cfg, cur[0][...], nb, mask_ref, cols(j), mask_index(i, j))
            pT = jnp.exp(sT - lse_ref[0, 0, i:i + 1, :])
            dsT = pT * (cur[1][...] - delta_s[i:i + 1, :])
            if has_bias:
                tile_rows = jnp.sum(dsT, axis=0, keepdims=True)
                drow = tile_rows if drow is None else drow + tile_rows
                for hh in range(nh):
                    part = dsT[:, hh * TQ:hh * TQ + LANES]
                    for u in range(1, TQ // LANES):
                        part = part + dsT[:, hh * TQ + u * LANES:hh * TQ + (u + 1) * LANES]
                    dneg_acc[hh, cols(j), :] += part
            if s_scale is not None:
                dsT = dsT * s_scale
            cur[2][...] = pT.astype(BF16)
            cur[3][...] = dsT.astype(BF16)
            if not first:
                dqT2 = last_products(*pairs[t - 1], oth, dqT2)
            if t + 1 < len(pairs):
                first_products(*pairs[t + 1], oth)
        i_last, j_last = pairs[-1]
        finish_q(i_last, last_products(i_last, j_last, bufs[(len(pairs) - 1) % 2], dqT2), drow)

        for n in range(nk):
            rows = slice(n * TK, (n + 1) * TK)
            dk = dk_acc[rows, :]
            if has_rope:
                dk = _rope_bwd(dk, *[t[rows, :] for t in rope_refs])
            if pair:
                dqkv_ref[0, rows, LANES:2 * LANES] = dk.astype(BF16)
                dqkv_ref[0, rows, 2 * LANES:3 * LANES] = dv_acc[rows, :].astype(BF16)
            else:
                dk_ref[0, rows, :] = dk.astype(BF16)
                dv_ref[0, rows, :] = dv_acc[rows, :].astype(BF16)
            if has_bias:
                x0 = jnp.sum(dneg_acc[0, rows, :], axis=1, keepdims=True)
                x1 = jnp.sum(dneg_acc[1, rows, :], axis=1, keepdims=True)
                dneg_ref[0, rows, :] = jnp.where(lane == 0, x0, jnp.where(lane == 1, x1, 0.0))

    ins, in_specs = _attn_t_inputs(kind, src, S, negc_cols, mask, rope, kv)
    row_spec = pl.BlockSpec((1, S, LANES), lambda b, h: (b, 0, h))
    vec_spec = pl.BlockSpec((1, 1, nq, R), lambda b, h: (b, h, 0, 0))
    ins += [do, o, lse]
    in_specs += [row_spec, row_spec, vec_spec]
    if token is not None:
        ins.append(token)
        in_specs.append(pl.BlockSpec(token.shape, lambda b, h: (0, 0)))
    W = cfg["n_blocks"] * LANES
    if pair:
        out_specs = [pl.BlockSpec((1, S, PAIR_W), lambda b, h: (b, 0, h))]
        out_shape = [jax.ShapeDtypeStruct((B, S, 3 * W), BF16)]
        if has_bias:
            out_specs += [row_spec, vec_spec]
            out_shape += [jax.ShapeDtypeStruct((B, S, W), F32), jax.ShapeDtypeStruct((B, cfg["n_blocks"], nq, R), F32)]
    else:
        kv_spec = pl.BlockSpec((1, MEM_LEN, LANES), lambda b, h: (b, 0, h))
        out_specs = [row_spec, kv_spec, kv_spec]
        out_shape = [jax.ShapeDtypeStruct((B, S, W), BF16)] + [jax.ShapeDtypeStruct((B, MEM_LEN, W), BF16)] * 2
    scratch = [pltpu.VMEM((nq, LANES, R), BF16), pltpu.VMEM((Sk, LANES), BF16), pltpu.VMEM((nh * S, LANES), BF16),
               pltpu.VMEM((Sk, LANES), BF16), pltpu.VMEM((nk, LANES, TK), BF16), pltpu.VMEM((nq, LANES, R), BF16),
               pltpu.VMEM((nh * S, LANES), BF16), pltpu.VMEM((nq, R), F32),
               pltpu.VMEM((Sk, LANES), F32), pltpu.VMEM((Sk, LANES), F32)]
    pair_bufs = [pltpu.VMEM((TK, R), F32), pltpu.VMEM((TK, R), F32), pltpu.VMEM((TK, R), BF16), pltpu.VMEM((TK, R), BF16)]
    scratch += pair_bufs + pair_bufs
    if has_bias:
        scratch += [pltpu.VMEM((nh, Sk, LANES), F32), pltpu.VMEM((nh, Sk, LANES), F32)]
    return pl.pallas_call(
        body, name=kind + "_attn_bwd", grid=(B, cfg["n_blocks"]),
        in_specs=in_specs, out_specs=out_specs, out_shape=out_shape, scratch_shapes=scratch,
        compiler_params=_params(("arbitrary", "arbitrary")),
    )(*ins)


def _sigmoid(g):
    return 1.0 / (1.0 + jnp.exp(-g))


def out_step(proj, o_fox, o_dil, o_mem, w_out, x, target, gf, tm):
    T = x.shape[0]

    def body(fg_ref, dg_ref, mg_ref, of_ref, od_ref, om_ref, w_ref, x_ref, t_ref, gf_ref,
             dx_ref, dof_ref, dod_ref, dom_ref, dfg_ref, ddg_ref, dmg_ref, gw_ref, sm_ref, gw_acc):
        branches = []
        for g_ref, o_ref in ((fg_ref, of_ref), (dg_ref, od_ref), (mg_ref, om_ref)):
            g = g_ref[...]
            sg = _sigmoid(g)
            o = o_ref[...]
            branches.append((g, sg, o))
        ymix = jnp.concatenate([(o * (g * sg)).astype(BF16) for g, sg, o in branches], axis=1)
        x2 = x_ref[...] + jnp.dot(ymix, w_ref[...], preferred_element_type=F32)
        r = lax.rsqrt(jnp.mean(x2 * x2, axis=-1, keepdims=True) + RMS_EPS)
        yn = x2 * r
        err = yn * gf_ref[...] - t_ref[...]
        loss = 0.5 * jnp.sum(jnp.sum(err * err, axis=-1, keepdims=True) / D_MODEL, axis=0, keepdims=True)
        dyf = err / D_MODEL
        dgf = jnp.sum(dyf * yn, axis=0, keepdims=True)
        dyn = dyf * gf_ref[...]
        dx2 = r * (dyn - yn * jnp.mean(dyn * yn, axis=-1, keepdims=True))
        dx_ref[...] = dx2
        dxb = dx2.astype(BF16)
        dmix = lax.dot_general(dxb, w_ref[...], (((1,), (1,)), ((), ())), preferred_element_type=F32)
        col = 0
        for (g, sg, o), do_ref, dgate_ref in zip(branches, (dof_ref, dod_ref, dom_ref), (dfg_ref, ddg_ref, dmg_ref)):
            d = dmix[:, col:col + g.shape[1]]
            col += g.shape[1]
            do_ref[...] = (d * (g * sg)).astype(BF16)
            dgate_ref[...] = (d * o * (sg * (1.0 + g * (1.0 - sg)))).astype(BF16)
        row = lax.broadcasted_iota(jnp.int32, (8, D_MODEL), 0)
        upd = jnp.where(row == 0, dgf, jnp.where(row == 1, loss, 0.0))

        @pl.when(pl.program_id(0) == 0)
        def _():
            sm_ref[...] = jnp.zeros(sm_ref.shape, F32)
            gw_acc[...] = jnp.zeros(gw_acc.shape, F32)

        sm_ref[...] += upd
        gw_acc[...] += lax.dot_general(ymix, dxb, (((0,), (0,)), ((), ())), preferred_element_type=F32)

        @pl.when(pl.program_id(0) == T // tm - 1)
        def _():
            gw_ref[...] = gw_acc[...].astype(BF16)

    def rows(w, col=0):
        return pl.BlockSpec((tm, w), lambda i: (i, col))

    return pl.pallas_call(
        body, name="out_step", grid=(T // tm,),
        in_specs=[rows(FOX_W, P_FG // FOX_W), rows(DIL_W, P_DG // DIL_W), rows(MEM_W, P_MG // MEM_W),
                  rows(FOX_W), rows(DIL_W), rows(MEM_W),
                  pl.BlockSpec((MIX_W, D_MODEL), lambda i: (0, 0)),
                  rows(D_MODEL), rows(D_MODEL), pl.BlockSpec((1, D_MODEL), lambda i: (0, 0))],
        out_specs=[rows(D_MODEL), rows(FOX_W), rows(DIL_W), rows(MEM_W), rows(FOX_W), rows(DIL_W), rows(MEM_W),
                   pl.BlockSpec((MIX_W, D_MODEL), lambda i: (0, 0)), pl.BlockSpec((8, D_MODEL), lambda i: (0, 0))],
        out_shape=[jax.ShapeDtypeStruct((T, D_MODEL), F32), jax.ShapeDtypeStruct((T, FOX_W), BF16),
                   jax.ShapeDtypeStruct((T, DIL_W), BF16), jax.ShapeDtypeStruct((T, MEM_W), BF16),
                   jax.ShapeDtypeStruct((T, FOX_W), BF16), jax.ShapeDtypeStruct((T, DIL_W), BF16),
                   jax.ShapeDtypeStruct((T, MEM_W), BF16), jax.ShapeDtypeStruct((MIX_W, D_MODEL), BF16),
                   jax.ShapeDtypeStruct((8, D_MODEL), F32)],
        scratch_shapes=[pltpu.VMEM((MIX_W, D_MODEL), F32)],
        compiler_params=_params(("arbitrary",)),
    )(proj, proj, proj, o_fox, o_dil, o_mem, w_out, x, target, gf)


def adamw(w, g, m, v, tr, name):
    lead = w.shape[:-2]
    R, C = w.shape[-2:]
    zeros = (0,) * len(lead)

    def body(w_ref, g_ref, m_ref, v_ref, d_ref, mo_ref, vo_ref):
        gv = g_ref[...]
        mn = ADAM_B1 * m_ref[...] + (1.0 - ADAM_B1) * gv
        vn = ADAM_B2 * v_ref[...] + (1.0 - ADAM_B2) * jnp.square(gv)
        m_hat = mn / (1.0 - ADAM_B1 ** ADAM_STEP)
        v_hat = vn / (1.0 - ADAM_B2 ** ADAM_STEP)
        d_ref[...] = -ADAM_LR * (m_hat / (jnp.sqrt(v_hat) + ADAM_EPS) + ADAM_WD * w_ref[...])
        mo_ref[...] = mn
        vo_ref[...] = vn

    spec = pl.BlockSpec((1,) * len(lead) + (tr, C), lambda i: zeros + (i, 0))
    return pl.pallas_call(
        body, name=name, grid=(pl.cdiv(R, tr),),
        in_specs=[spec] * 4, out_specs=[spec] * 3,
        out_shape=[jax.ShapeDtypeStruct(w.shape, F32)] * 3,
        compiler_params=_params(("arbitrary",)),
    )(w, g, m, v)


def _pad_row(v, width):
    return jnp.concatenate([v, jnp.zeros((1, width - v.shape[1]), v.dtype)], axis=1)


def local_grads(x, mem, norm_g, b_forget, mem_norm_g, final_norm_g, loss_target, w_in_p, first_token, small_weights,
                start_exchange):
    B, S, D = x.shape
    T = B * S
    xt = x.reshape(T, D)
    memt = mem.reshape(B * MEM_LEN, D)
    b_pad = _pad_row(b_forget, LANES)

    h, h_t = rms_fwd(xt, norm_g, 512, "rms_x", with_transpose=True)
    proj = mm_nn(h, w_in_p, 512, PW // 3, "in_proj", first_token)
    proj3 = proj.reshape(B, S, PW)

    negc = fox_gate(proj3, b_pad)
    causal = _log_masks_t(S, "causal")
    dilated = _log_masks_t(S, "dilated")
    rope = _rope_tables(S)

    o_fox, lse_fox = attn_fwd("fox", proj3, S, negc_cols=negc, mask=causal)
    o_dil, lse_dil = attn_fwd("dil", proj3, S, mask=dilated, rope=rope)

    w_kv, w_out = small_weights(o_dil)
    mh, mh_t = rms_fwd(memt, mem_norm_g, B * MEM_LEN, "rms_mem", with_transpose=True)
    mkv = mm_nn(mh, w_kv, B * MEM_LEN, 2 * MEM_W, "mem_kv_proj")
    mkv3 = mkv.reshape(B, MEM_LEN, 2 * MEM_W)
    o_mem, lse_mem = attn_fwd("mem", proj3, S, kv=mkv3)

    dx2, do_fox, do_dil, do_mem, dfg, ddg, dmg, g_out, small_out = out_step(
        proj, o_fox.reshape(T, FOX_W), o_dil.reshape(T, DIL_W), o_mem.reshape(T, MEM_W), w_out,
        xt, loss_target.reshape(T, D), final_norm_g.reshape(1, D), 256)

    gates = [(dfg, P_FG), (ddg, P_DG), (dmg, P_MG)]
    g_gates = mm_tn_multi(h_t, [arr for arr, _ in gates], 1024, "w_in_grad_gates", BF16)
    first, token = start_exchange([g_gates, g_out], "early_exchange_a")

    dqkv_fox, dneg, drow = attn_bwd("fox", proj3, do_fox.reshape(B, S, FOX_W), o_fox, lse_fox, S,
                                    negc_cols=negc, mask=causal, token=token)
    drow = drow.reshape(B, FOX_HEADS // 2, S // TQ, 2, TQ).transpose(0, 1, 3, 2, 4).reshape(B, FOX_HEADS, S)
    drow = jnp.pad(drow, ((0, 0), (0, LANES - FOX_HEADS), (0, 0)))
    dflog, db_part = fox_gate_bwd(drow, dneg, proj3, b_pad)
    fox = [(dqkv_fox.reshape(T, 3 * FOX_W), P_FOX), (dflog.reshape(T, LANES), P_FLOG)]
    g_fox = mm_tn_multi(h_t, [arr for arr, _ in fox], 1024, "w_in_grad_fox", BF16)
    second, token = start_exchange([g_fox], "early_exchange_b")

    (dqkv_dil,) = attn_bwd("dil", proj3, do_dil.reshape(B, S, DIL_W), o_dil, lse_dil, S, mask=dilated, rope=rope,
                           token=token)
    dil = [(dqkv_dil.reshape(T, 3 * DIL_W), P_DIL)]
    g_dil = mm_tn_multi(h_t, [arr for arr, _ in dil], 1024, "w_in_grad_dil", BF16)
    third, token = start_exchange([g_dil], "early_exchange_c")

    dmq, dmk, dmv = attn_bwd("mem", proj3, do_mem.reshape(B, S, MEM_W), o_mem, lse_mem, S, kv=mkv3, token=token)
    mq = [(dmq.reshape(T, MEM_W), P_MQ)]
    g_mq = mm_tn_multi(h_t, [arr for arr, _ in mq], 1024, "w_in_grad_mq", BF16)
    dmkv = jnp.concatenate([dmk, dmv], axis=2).reshape(B * MEM_LEN, 2 * MEM_W)
    g_kv = mm_tn_multi(mh_t, [dmkv], B * MEM_LEN, "w_kv_grad", BF16)
    fourth, token = start_exchange([g_mq, g_kv], "early_exchange_d")

    grad_x, dng = in_proj_bwd_rms(gates + fox + dil + mq, w_in_p, xt, norm_g, dx2, 256, token)
    dmh = mm_nt(dmkv, w_kv, B * MEM_LEN, D, "mem_kv_bwd")
    _, dmng = rms_bwd(memt, mem_norm_g, dmh, None, B * MEM_LEN, "rms_mem_bwd")

    small = jnp.concatenate([dng[0:1], dmng[0:1], small_out[0:1], _pad_row(db_part[0:1], D), small_out[1:2],
                             jnp.zeros((3, D), F32)], axis=0)
    early = [(first, dqkv_fox), (second, dqkv_dil), (third, dmq), (fourth, grad_x)]
    return grad_x.reshape(B, S, D), early, small


def kernel(x, mem, norm_g, w_in, b_forget, mem_norm_g, w_mem_kv, w_out, final_norm_g, loss_target, m_norm_g, m_w_in, m_b_forget, m_mem_norm_g, m_w_mem_kv, m_w_out, m_final_norm_g, v_norm_g, v_w_in, v_b_forget, v_mem_norm_g, v_w_mem_kv, v_w_out, v_final_norm_g):
    D = D_MODEL
    (w_in_full,) = weight_gather([_pack_cols(w_in).astype(BF16).reshape(w_in.shape[1], PW)])
    gather, gather_token = early_exchange_start([w_mem_kv[0].astype(BF16), w_out[0].astype(BF16)], "early_gather",
                                                gather=True, after=w_in_full)

    def small_weights(after):
        _, gathered = early_exchange_wait(gather, after, "early_gather_wait")
        return gathered

    grad_x, early, small = local_grads(
        x, mem, norm_g, b_forget, mem_norm_g, final_norm_g, loss_target, w_in_full, gather_token, small_weights,
        early_exchange_start)

    (first, after_first), (second, after_second), (third, after_third), (fourth, after_fourth) = early
    (src_gates, src_out), (land_gates, land_out) = early_exchange_wait(first, after_first, "early_wait_a")
    (src_fox,), (land_fox,) = early_exchange_wait(second, after_second, "early_wait_b")
    (src_dil,), (land_dil,) = early_exchange_wait(third, after_third, "early_wait_c")
    (src_mq, src_kv), (land_mq, land_kv) = early_exchange_wait(fourth, after_fourth, "early_wait_d")
    gates = slot_sum8(src_gates, land_gates, 128, "sum_w_in_gates")
    gw_out = slot_sum8(src_out, land_out, 256, "sum_w_out")
    fox = slot_sum8(src_fox, land_fox, 128, "sum_w_in_fox")
    dil = slot_sum8(src_dil, land_dil, 128, "sum_w_in_dil")
    mq = slot_sum8(src_mq, land_mq, 128, "sum_w_in_mq")
    gw_kv = slot_sum8(src_kv, land_kv, 128, "sum_w_kv")

    tot = small_all_reduce(small)
    gw_in = _unpack_cols(jnp.concatenate(
        [fox[:, :3 * FOX_W], gates[:, :FOX_W], dil, gates[:, FOX_W:FOX_W + DIL_W], mq,
         gates[:, FOX_W + DIL_W:], fox[:, 3 * FOX_W:]], axis=1)[None])

    loss = tot[4, 0]
    g_norm, g_mem_norm, g_final, g_b = tot[0:1], tot[1:2], tot[2], tot[3:4, :FOX_HEADS]

    def rows8(*rows):
        rows = [r.reshape(1, -1) for r in rows]
        rows = [_pad_row(r, D) for r in rows]
        return jnp.concatenate(rows + [jnp.zeros((8 - len(rows), D), F32)], axis=0)

    sw = rows8(norm_g, mem_norm_g, final_norm_g, b_forget)
    sm = rows8(m_norm_g, m_mem_norm_g, m_final_norm_g, m_b_forget)
    sv = rows8(v_norm_g, v_mem_norm_g, v_final_norm_g, v_b_forget)
    d_s, m_s, v_s = adamw(sw, tot, sm, sv, 8, "adamw_small")
    d_in, m_in, v_in = adamw(w_in, gw_in, m_w_in, v_w_in, 32, "adamw_w_in")
    d_kv, m_kv, v_kv = adamw(w_mem_kv[0], gw_kv, m_w_mem_kv[0], v_w_mem_kv[0], 128, "adamw_w_kv")
    d_out, m_out, v_out = adamw(w_out[0], gw_out, m_w_out[0], v_w_out[0], 256, "adamw_w_out")

    def small_outs(t):
        return t[0:1], t[3:4, :FOX_HEADS], t[1:2], t[2]

    grads = (g_norm, gw_in, g_b, g_mem_norm, gw_kv[None], gw_out[None], g_final)
    outs = []
    for t, big in ((d_s, (d_in, d_kv, d_out)), (m_s, (m_in, m_kv, m_out)), (v_s, (v_in, v_kv, v_out))):
        n, b, mn, f = small_outs(t)
        outs += [n, big[0], b, mn, big[1][None], big[2][None], f]
    return (loss, grad_x, *grads, *outs)
```

```python
import math

import numpy as np
import jax
import jax.numpy as jnp
from jax import lax
from jax.experimental import pallas as pl
from jax.experimental.pallas import tpu as pltpu

F32 = jnp.float32
BF16 = jnp.bfloat16

D_MODEL = 1024
HEAD_DIM = 64
FOX_HEADS = 12
DIL_HEADS = 12
MEM_HEADS = 4
MEM_HEAD_DIM = 128
MEM_LEN = 256
FOX_W = FOX_HEADS * HEAD_DIM
DIL_W = DIL_HEADS * HEAD_DIM
MEM_W = MEM_HEADS * MEM_HEAD_DIM
MIX_W = FOX_W + DIL_W + MEM_W
DILATIONS = ((128, 1), (512, 4), (2048, 16))
ROPE_THETA = 500000.0
ROPE_DIM = HEAD_DIM // 4
RMS_EPS = 1e-6
NEG_INF = -1e30
IN_W = 4 * FOX_W + FOX_HEADS + 4 * DIL_W + 2 * MEM_W

ADAM_LR = 0.001
ADAM_B1 = 0.9
ADAM_B2 = 0.999
ADAM_EPS = 1e-08
ADAM_WD = 0.01
ADAM_STEP = 10

N_DEV = 8
LANES = 128
PAIR_W = 3 * LANES
TQ = 256
TK = 256

O_FQ, O_FK, O_FV, O_FG = 0, FOX_W, 2 * FOX_W, 3 * FOX_W
O_FLOG = 4 * FOX_W
O_DQ = O_FLOG + FOX_HEADS
O_DK, O_DV, O_DG = O_DQ + DIL_W, O_DQ + 2 * DIL_W, O_DQ + 3 * DIL_W
O_MQ = O_DQ + 4 * DIL_W
O_MG = O_MQ + MEM_W
P_FOX = 0
P_FG = P_FOX + 3 * FOX_W
P_DIL = P_FG + FOX_W
P_DG = P_DIL + 3 * DIL_W
P_MQ = P_DG + DIL_W
P_MG = P_MQ + MEM_W
P_FLOG = P_MG + MEM_W
PW = P_FLOG + LANES

VMEM_LIMIT = 56 * 1024 * 1024


def _pack_pieces():
    pieces = []
    for base in (O_FQ, O_DQ):
        seg = []
        for hp in range(FOX_HEADS // 2):
            for part in range(3):
                seg.append((base + part * FOX_W + hp * LANES, LANES))
        pieces.append(seg)
    fox, dil = pieces
    return fox + [(O_FG, FOX_W)] + dil + [(O_DG, DIL_W), (O_MQ, MEM_W), (O_MG, MEM_W), (O_FLOG, FOX_HEADS)]


def _pack_cols(w):
    parts = [w[..., s:s + n] for s, n in _pack_pieces()]
    parts.append(jnp.zeros(w.shape[:-1] + (LANES - FOX_HEADS,), w.dtype))
    return jnp.concatenate(parts, axis=-1)


def _unpack_cols(g):
    runs = []
    pos = 0
    for s, n in _pack_pieces():
        runs.append((s, n, pos))
        pos += n
    runs.sort()
    return jnp.concatenate([g[..., p:p + n] for s, n, p in runs], axis=-1)


def _params(sem=None, **kw):
    return pltpu.CompilerParams(dimension_semantics=sem, vmem_limit_bytes=VMEM_LIMIT, **kw)


def _mesh_pos():
    return lax.axis_index("x"), lax.axis_index("y"), lax.axis_index("c")


def _flip(v, d):
    return 1 - v if d else v


_RELATIONS = [(dx, dy, dc) for dx in (0, 1) for dy in (0, 1) for dc in (0, 1)][1:]


def weight_gather(shards):
    n_arr = len(shards)
    rows = [s.shape[0] for s in shards]

    def body(*refs):
        in_refs = refs[:n_arr]
        out_refs = refs[n_arr:2 * n_arr]
        send_sems, recv_sems, local_sems = refs[2 * n_arr:]
        x, y, c = _mesh_pos()
        me, sibling = (x, y, c), (x, y, 1 - c)
        x_nbr, y_nbr, diag = (1 - x, y, c), (x, 1 - y, c), (1 - x, 1 - y, c)
        north = c == 1
        relay_from = (jnp.where(north, 1 - x, x), jnp.where(north, y, 1 - y), c)
        relay_to = (jnp.where(north, x, 1 - x), jnp.where(north, 1 - y, y), c)
        k_from = jnp.where(north, 1, 2)
        k_to = 3 - k_from

        def block(a, pos):
            px, py, pc = pos
            return out_refs[a].at[pl.ds((4 * px + 2 * py + pc) * rows[a], rows[a]), :]

        def copy(a, k, blk, to, src=None):
            return pltpu.make_async_remote_copy(
                src_ref=block(a, blk) if src is None else src, dst_ref=block(a, blk),
                send_sem=send_sems.at[a, k], recv_sem=recv_sems.at[a, k],
                device_id=to, device_id_type=pl.DeviceIdType.MESH)

        started = []
        mine = []
        for a in range(n_arr):
            cp = pltpu.make_async_copy(in_refs[a], block(a, me), local_sems.at[a])
            cp.start()
            mine.append(cp)
            first = [copy(a, 0, me, sibling, src=in_refs[a]), copy(a, 1, me, x_nbr, src=in_refs[a]),
                     copy(a, 2, me, y_nbr, src=in_refs[a])]
            for cp in first:
                cp.start()
            started += first
        for a in range(n_arr):
            copy(a, k_from, relay_from, me).wait_recv()
            second_hop = copy(a, 3, relay_from, relay_to)
            second_hop.start()
            passed = copy(a, 3 + k_from, relay_from, sibling)
            passed.start()
            started += [second_hop, passed]
        for a in range(n_arr):
            copy(a, k_to, relay_to, me).wait_recv()
            passed = copy(a, 3 + k_to, relay_to, sibling)
            passed.start()
            started.append(passed)
        for a in range(n_arr):
            copy(a, 3, diag, me).wait_recv()
            passed = copy(a, 6, diag, sibling)
            passed.start()
            started.append(passed)
        for a in range(n_arr):
            copy(a, 0, sibling, me).wait_recv()
            for k, chip in ((4, x_nbr), (5, y_nbr), (6, diag)):
                copy(a, k, (chip[0], chip[1], 1 - c), me).wait_recv()
        for cp in started:
            cp.wait_send()
        for cp in mine:
            cp.wait()

    any_spec = pl.BlockSpec(memory_space=pl.ANY)
    return pl.pallas_call(
        body, name="weight_gather",
        out_shape=[jax.ShapeDtypeStruct((N_DEV * s.shape[0], s.shape[1]), s.dtype) for s in shards],
        in_specs=[any_spec] * n_arr, out_specs=[any_spec] * n_arr,
        scratch_shapes=[pltpu.SemaphoreType.DMA((n_arr, 7)), pltpu.SemaphoreType.DMA((n_arr, 7)),
                        pltpu.SemaphoreType.DMA((n_arr,))],
    )(*shards)


N_CHIP = 4
_OTHER_CHIPS = [(1, 0), (0, 1), (1, 1)]


def small_all_reduce(small):
    vmem_spec = pl.BlockSpec(memory_space=pltpu.VMEM)

    def chip_body(small_ref, csum_ref, land, send_sem, recv_sem):
        x, y, c = _mesh_pos()
        swap = pltpu.make_async_remote_copy(
            src_ref=small_ref, dst_ref=land, send_sem=send_sem, recv_sem=recv_sem,
            device_id=(x, y, 1 - c), device_id_type=pl.DeviceIdType.MESH)
        swap.start()
        swap.wait_recv()
        swap.wait_send()
        csum_ref[...] = small_ref[...] + land[...]

    csum = pl.pallas_call(
        chip_body, name="small_sum_d2d", out_shape=jax.ShapeDtypeStruct(small.shape, small.dtype),
        in_specs=[vmem_spec], out_specs=vmem_spec,
        scratch_shapes=[pltpu.VMEM(small.shape, small.dtype), pltpu.SemaphoreType.DMA, pltpu.SemaphoreType.DMA],
    )(small)

    def all_body(csum_ref, tot_ref, land, send_sems, recv_sems):
        x, y, c = _mesh_pos()
        q_me = 2 * x + y
        land[q_me] = csum_ref[...]
        sends, recvs = [], []
        for j, (dx, dy) in enumerate(_OTHER_CHIPS):
            px, py = _flip(x, dx), _flip(y, dy)
            common = dict(send_sem=send_sems.at[j], recv_sem=recv_sems.at[j],
                          device_id=(px, py, c), device_id_type=pl.DeviceIdType.MESH)
            sends.append(pltpu.make_async_remote_copy(src_ref=csum_ref, dst_ref=land.at[q_me], **common))
            recvs.append(pltpu.make_async_remote_copy(src_ref=csum_ref, dst_ref=land.at[2 * px + py], **common))
        for cp in sends:
            cp.start()
        for cp in recvs:
            cp.wait_recv()
        for cp in sends:
            cp.wait_send()
        tot = land[0]
        for q in range(1, N_CHIP):
            tot = tot + land[q]
        tot_ref[...] = tot

    return pl.pallas_call(
        all_body, name="small_sum_ici", out_shape=jax.ShapeDtypeStruct(small.shape, small.dtype),
        in_specs=[vmem_spec], out_specs=vmem_spec,
        scratch_shapes=[pltpu.VMEM((N_CHIP,) + small.shape, small.dtype),
                        pltpu.SemaphoreType.DMA((3,)), pltpu.SemaphoreType.DMA((3,))],
    )(csum)


_HBM = pl.BlockSpec(memory_space=pltpu.HBM)
_SEM = pl.BlockSpec(memory_space=pltpu.SEMAPHORE)
_EFFECT = pltpu.SideEffectType.DATAFLOW_SIDE_EFFECTING


def _early_copies(src_refs, land_refs, send_sems, recv_sems, rows, gather):
    x, y, c = _mesh_pos()
    me = 4 * x + 2 * y + c
    copies = []
    for a in range(len(src_refs)):
        for dx, dy, dc in _RELATIONS:
            px, py, pc = _flip(x, dx), _flip(y, dy), _flip(c, dc)
            peer = 4 * px + 2 * py + pc
            copies.append(pltpu.make_async_remote_copy(
                src_ref=src_refs[a] if gather else src_refs[a].at[pl.ds(peer * rows[a], rows[a]), :],
                dst_ref=land_refs[a].at[pl.ds(me * rows[a], rows[a]), :],
                send_sem=send_sems[a], recv_sem=recv_sems[a],
                device_id=(px, py, pc), device_id_type=pl.DeviceIdType.MESH))
    return copies


def early_exchange_start(srcs, name, gather=False, after=None):
    n = len(srcs)
    if gather:
        rows = [s.shape[0] for s in srcs]
        me = 4 * lax.axis_index("x") + 2 * lax.axis_index("y") + lax.axis_index("c")
        lands = [lax.dynamic_update_slice(lax.empty((N_DEV * r, s.shape[1]), s.dtype), s, (me * r, 0))
                 for r, s in zip(rows, srcs)]
    else:
        rows = [s.shape[0] // N_DEV for s in srcs]
        lands = [lax.empty(s.shape, s.dtype) for s in srcs]

    extra = [] if after is None else [after]

    def body(*refs):
        src_refs, land_refs = refs[:n], refs[n:2 * n]
        first_sem = 2 * n + len(extra)
        send_sems, recv_sems = refs[first_sem:first_sem + n], refs[first_sem + n:first_sem + 2 * n]
        token = refs[-1]
        for cp in _early_copies(src_refs, land_refs, send_sems, recv_sems, rows, gather):
            cp.start()
        token[...] = jnp.zeros_like(token)

    hbm = lambda a: pltpu.HBM(a.shape, a.dtype)
    outs = pl.pallas_call(
        body, name=name,
        out_shape=[pltpu.SemaphoreType.DMA(())] * (2 * n)
        + [hbm(a) for a in srcs] + [hbm(a) for a in lands] + [jax.ShapeDtypeStruct((8, LANES), F32)],
        in_specs=[_HBM] * (2 * n) + [pl.BlockSpec(memory_space=pl.ANY)] * len(extra),
        out_specs=[_SEM] * (2 * n) + [_HBM] * (2 * n) + [pl.BlockSpec(memory_space=pltpu.VMEM)],
        input_output_aliases={i: 2 * n + i for i in range(2 * n)},
        compiler_params=pltpu.CompilerParams(has_side_effects=_EFFECT),
    )(*[pltpu.with_memory_space_constraint(a, pltpu.HBM) for a in list(srcs) + lands], *extra)
    return dict(sems=outs[:2 * n], srcs=outs[2 * n:3 * n], lands=outs[3 * n:4 * n], rows=rows), outs[-1]


def early_exchange_wait(handle, after, name):
    n = len(handle["srcs"])
    rows = handle["rows"]

    def body(*refs):
        src_refs, land_refs = refs[:n], refs[n:2 * n]
        send_sems, recv_sems = refs[2 * n:3 * n], refs[3 * n:4 * n]
        x, y, c = _mesh_pos()
        for a in range(n):
            seven = pl.ds(0, 7 * rows[a])
            all_seven = pltpu.make_async_remote_copy(
                src_ref=land_refs[a].at[seven, :], dst_ref=land_refs[a].at[seven, :],
                send_sem=send_sems[a], recv_sem=recv_sems[a],
                device_id=(x, y, c), device_id_type=pl.DeviceIdType.MESH)
            all_seven.wait_send()
            all_seven.wait_recv()

    hbm = lambda a: pltpu.HBM(a.shape, a.dtype)
    ins = list(handle["srcs"]) + list(handle["lands"])
    outs = pl.pallas_call(
        body, name=name,
        out_shape=[hbm(a) for a in ins],
        in_specs=[_HBM] * (2 * n) + [_SEM] * (2 * n) + [pl.BlockSpec(memory_space=pl.ANY)],
        out_specs=[_HBM] * (2 * n),
        input_output_aliases={i: i for i in range(2 * n)},
        compiler_params=pltpu.CompilerParams(has_side_effects=_EFFECT),
    )(*ins, *handle["sems"], after)
    return outs[:n], outs[n:]


def slot_sum8(src, land, tr, name):
    rows, cols = land.shape[0] // N_DEV, land.shape[1]
    x, y, c = _mesh_pos()
    me = (4 * x + 2 * y + c).astype(jnp.int32).reshape(1)

    def body(me_ref, src_ref, land_ref, o_ref):
        acc = None
        for d in range(N_DEV):
            term = jnp.where(d == me_ref[0], src_ref[0], land_ref[d]).astype(F32)
            acc = term if acc is None else acc + term
        o_ref[...] = acc

    return pl.pallas_call(
        body, name=name,
        grid_spec=pltpu.PrefetchScalarGridSpec(
            num_scalar_prefetch=1, grid=(rows // tr,),
            in_specs=[pl.BlockSpec((1, tr, cols), lambda i, w: (w[0], i, 0)),
                      pl.BlockSpec((N_DEV, tr, cols), lambda i, w: (0, i, 0))],
            out_specs=pl.BlockSpec((tr, cols), lambda i, w: (i, 0))),
        out_shape=jax.ShapeDtypeStruct((rows, cols), F32),
        compiler_params=_params(("arbitrary",)),
    )(me, src.reshape(N_DEV, rows, cols), land.reshape(N_DEV, rows, cols))


def mm_tn_multi(a_t, bs, tt, name, out_dtype=F32):
    K, T = a_t.shape
    widths = [b.shape[1] for b in bs]
    steps = T // tt

    def body(a_ref, *rest):
        b_refs, o_ref, acc = rest[:-2], rest[-2], rest[-1]

        @pl.when(pl.program_id(0) == 0)
        def _():
            acc[...] = jnp.zeros(acc.shape, F32)

        av = a_ref[...]
        col = 0
        for b_ref, w in zip(b_refs, widths):
            acc[:, col:col + w] += jnp.dot(av, b_ref[...], preferred_element_type=F32)
            col += w

        @pl.when(pl.program_id(0) == steps - 1)
        def _():
            o_ref[...] = acc[...].astype(out_dtype)

    return pl.pallas_call(
        body, name=name, grid=(steps,),
        in_specs=[pl.BlockSpec((K, tt), lambda t: (0, t))] + [pl.BlockSpec((tt, w), lambda t: (t, 0)) for w in widths],
        out_specs=pl.BlockSpec((K, sum(widths)), lambda t: (0, 0)),
        out_shape=jax.ShapeDtypeStruct((K, sum(widths)), out_dtype),
        scratch_shapes=[pltpu.VMEM((K, sum(widths)), F32)],
        compiler_params=_params(("arbitrary",)),
    )(a_t, *bs)


def rms_fwd(x, g, tm, name, with_transpose=False):
    M, K = x.shape

    def body(x_ref, g_ref, o_ref, *t_ref):
        xv = x_ref[...]
        r = lax.rsqrt(jnp.mean(xv * xv, axis=-1, keepdims=True) + RMS_EPS)
        h = ((xv * r) * g_ref[...]).astype(BF16)
        o_ref[...] = h
        if with_transpose:
            t_ref[0][...] = h.T

    out_specs = [pl.BlockSpec((tm, K), lambda i: (i, 0))]
    out_shape = [jax.ShapeDtypeStruct((M, K), BF16)]
    if with_transpose:
        out_specs.append(pl.BlockSpec((K, tm), lambda i: (0, i)))
        out_shape.append(jax.ShapeDtypeStruct((K, M), BF16))
    outs = pl.pallas_call(
        body, name=name, grid=(M // tm,),
        in_specs=[pl.BlockSpec((tm, K), lambda i: (i, 0)), pl.BlockSpec((1, K), lambda i: (0, 0))],
        out_specs=out_specs, out_shape=out_shape,
        compiler_params=_params(("arbitrary",)),
    )(x, g)
    return outs if with_transpose else outs[0]


def rms_bwd(x, g, dh, dres, tm, name):
    M, K = x.shape
    has_res = dres is not None

    def body(*refs):
        if has_res:
            x_ref, g_ref, dh_ref, dres_ref, dx_ref, dg_ref = refs
        else:
            x_ref, g_ref, dh_ref, dx_ref, dg_ref = refs
        xv = x_ref[...]
        r = lax.rsqrt(jnp.mean(xv * xv, axis=-1, keepdims=True) + RMS_EPS)
        xn = xv * r
        dhv = dh_ref[...]
        dxn = dhv * g_ref[...]
        dx = r * (dxn - xn * jnp.mean(dxn * xn, axis=-1, keepdims=True))
        if has_res:
            dx = dx + dres_ref[...]
        dx_ref[...] = dx
        part = jnp.sum(dhv * xn, axis=0, keepdims=True)
        row = lax.broadcasted_iota(jnp.int32, (8, K), 0)
        upd = jnp.where(row == 0, part, 0.0)

        @pl.when(pl.program_id(0) == 0)
        def _():
            dg_ref[...] = upd

        @pl.when(pl.program_id(0) != 0)
        def _():
            dg_ref[...] += upd

    row_spec = pl.BlockSpec((tm, K), lambda i: (i, 0))
    ins = [x, g, dh] + ([dres] if has_res else [])
    in_specs = [row_spec, pl.BlockSpec((1, K), lambda i: (0, 0)), row_spec] + ([row_spec] if has_res else [])
    return pl.pallas_call(
        body, name=name, grid=(M // tm,),
        in_specs=in_specs,
        out_specs=[row_spec, pl.BlockSpec((8, K), lambda i: (0, 0))],
        out_shape=[jax.ShapeDtypeStruct((M, K), F32), jax.ShapeDtypeStruct((8, K), F32)],
        compiler_params=_params(("arbitrary",)),
    )(*ins)


def mm_nn(a, b, tm, tn, name, token=None):
    M, K = a.shape
    N = b.shape[1]
    extra = [] if token is None else [token]

    def body(a_ref, b_ref, *rest):
        rest[-1][...] = jnp.dot(a_ref[...], b_ref[...], preferred_element_type=F32)

    return pl.pallas_call(
        body, name=name, grid=(N // tn, M // tm),
        in_specs=[pl.BlockSpec((tm, K), lambda j, i: (i, 0)), pl.BlockSpec((K, tn), lambda j, i: (0, j))]
        + [pl.BlockSpec(t.shape, lambda j, i: (0, 0)) for t in extra],
        out_specs=pl.BlockSpec((tm, tn), lambda j, i: (i, j)),
        out_shape=jax.ShapeDtypeStruct((M, N), F32),
        compiler_params=_params(("arbitrary", "arbitrary")),
    )(a, b, *extra)


def mm_nt(a, b, tm, tk, name):
    M, K = a.shape
    N = b.shape[0]

    def body(a_ref, b_ref, o_ref):
        part = lax.dot_general(a_ref[...], b_ref[...], (((1,), (1,)), ((), ())), preferred_element_type=F32)

        @pl.when(pl.program_id(1) == 0)
        def _():
            o_ref[...] = part

        @pl.when(pl.program_id(1) != 0)
        def _():
            o_ref[...] += part

    return pl.pallas_call(
        body, name=name, grid=(M // tm, K // tk),
        in_specs=[pl.BlockSpec((tm, tk), lambda i, k: (i, k)), pl.BlockSpec((N, tk), lambda i, k: (0, k))],
        out_specs=pl.BlockSpec((tm, N), lambda i, k: (i, 0)),
        out_shape=jax.ShapeDtypeStruct((M, N), F32),
        compiler_params=_params(("arbitrary", "arbitrary")),
    )(a, b)


def in_proj_bwd_rms(pieces, w, x, g, dres, tm, token):
    M, N = x.shape

    def body(*refs):
        n = len(pieces)
        p_refs, w_ref, x_ref, g_ref, dres_ref, _, dx_ref, dg_ref = refs[:n], *refs[n:]
        dh = None
        for p_ref, (arr, col) in zip(p_refs, pieces):
            part = lax.dot_general(p_ref[...], w_ref[:, col:col + arr.shape[1]], (((1,), (1,)), ((), ())),
                                   preferred_element_type=F32)
            dh = part if dh is None else dh + part
        xv = x_ref[...]
        r = lax.rsqrt(jnp.mean(xv * xv, axis=-1, keepdims=True) + RMS_EPS)
        xn = xv * r
        dxn = dh * g_ref[...]
        dx_ref[...] = r * (dxn - xn * jnp.mean(dxn * xn, axis=-1, keepdims=True)) + dres_ref[...]
        row = lax.broadcasted_iota(jnp.int32, (8, N), 0)
        upd = jnp.where(row == 0, jnp.sum(dh * xn, axis=0, keepdims=True), 0.0)

        @pl.when(pl.program_id(0) == 0)
        def _():
            dg_ref[...] = upd

        @pl.when(pl.program_id(0) != 0)
        def _():
            dg_ref[...] += upd

    row_spec = pl.BlockSpec((tm, N), lambda i: (i, 0))
    return pl.pallas_call(
        body, name="in_proj_bwd", grid=(M // tm,),
        in_specs=[pl.BlockSpec((tm, arr.shape[1]), lambda i: (i, 0)) for arr, _ in pieces]
        + [pl.BlockSpec(w.shape, lambda i: (0, 0)), row_spec, pl.BlockSpec((1, N), lambda i: (0, 0)), row_spec,
           pl.BlockSpec(token.shape, lambda i: (0, 0))],
        out_specs=[row_spec, pl.BlockSpec((8, N), lambda i: (0, 0))],
        out_shape=[jax.ShapeDtypeStruct((M, N), F32), jax.ShapeDtypeStruct((8, N), F32)],
        compiler_params=_params(("arbitrary",)),
    )(*[arr for arr, _ in pieces], w, x, g, dres, token)


def _log_sigmoid(z):
    return jnp.minimum(z, 0.0) - jnp.log(1.0 + jnp.exp(-jnp.abs(z)))


def _tri(n, lower):
    r = lax.broadcasted_iota(jnp.int32, (n, n), 0)
    c = lax.broadcasted_iota(jnp.int32, (n, n), 1)
    return jnp.where((r >= c) if lower else (r <= c), 1.0, 0.0).astype(F32)


def fox_gate(proj3, b_pad):
    B, S, _ = proj3.shape
    nblk = S // TK

    def body(f_ref, b_ref, o_ref):
        tri = _tri(TK, True)
        carry = jnp.zeros((1, LANES), F32)
        for n in range(nblk):
            z = f_ref[0, n * TK:(n + 1) * TK, :] + b_ref[...]
            logf = _log_sigmoid(z)
            cs = jnp.dot(tri, logf, preferred_element_type=F32, precision=lax.Precision.HIGHEST) + carry
            carry = cs[TK - 1:TK, :]
            o_ref[0, n * TK:(n + 1) * TK, :] = -cs

    return pl.pallas_call(
        body, name="fox_gate", grid=(B,),
        in_specs=[pl.BlockSpec((1, S, LANES), lambda b: (b, 0, P_FLOG // LANES)),
                  pl.BlockSpec((1, LANES), lambda b: (0, 0))],
        out_specs=pl.BlockSpec((1, S, LANES), lambda b: (b, 0, 0)),
        out_shape=jax.ShapeDtypeStruct((B, S, LANES), F32),
        compiler_params=_params(("arbitrary",)),
    )(proj3, b_pad)


def fox_gate_bwd(drow, dneg, proj3, b_pad):
    B, S, _ = proj3.shape
    nblk = S // TK

    def body(d_ref, r_ref, f_ref, b_ref, o_ref, db_ref):
        tri = _tri(TK, False)
        lane = lax.broadcasted_iota(jnp.int32, (TK, LANES), 1)
        carry = jnp.zeros((1, LANES), F32)
        dbsum = jnp.zeros((1, LANES), F32)
        for n in reversed(range(nblk)):
            dk_side = None
            for hp in range(FOX_HEADS // 2):
                two = jnp.where(lane < 2, r_ref[0, n * TK:(n + 1) * TK, hp * LANES:(hp + 1) * LANES], 0.0)
                two = pltpu.roll(two, 2 * hp, 1) if hp else two
                dk_side = two if dk_side is None else dk_side + two
            dc = jnp.where(lane < FOX_HEADS, d_ref[0, :, n * TK:(n + 1) * TK].T - dk_side, 0.0)
            rs = jnp.dot(tri, dc, preferred_element_type=F32, precision=lax.Precision.HIGHEST) + carry
            carry = rs[0:1, :]
            z = f_ref[0, n * TK:(n + 1) * TK, :] + b_ref[...]
            dz = rs * (1.0 / (1.0 + jnp.exp(z)))
            o_ref[0, n * TK:(n + 1) * TK, :] = dz.astype(BF16)
            dbsum = dbsum + jnp.sum(dz, axis=0, keepdims=True)
        row = lax.broadcasted_iota(jnp.int32, (8, LANES), 0)
        upd = jnp.where(row == 0, dbsum, 0.0)

        @pl.when(pl.program_id(0) == 0)
        def _():
            db_ref[...] = upd

        @pl.when(pl.program_id(0) != 0)
        def _():
            db_ref[...] += upd

    return pl.pallas_call(
        body, name="fox_gate_bwd", grid=(B,),
        in_specs=[pl.BlockSpec((1, LANES, S), lambda b: (b, 0, 0)),
                  pl.BlockSpec((1, S, FOX_W), lambda b: (b, 0, 0)),
                  pl.BlockSpec((1, S, LANES), lambda b: (b, 0, P_FLOG // LANES)),
                  pl.BlockSpec((1, LANES), lambda b: (0, 0))],
        out_specs=[pl.BlockSpec((1, S, LANES), lambda b: (b, 0, 0)), pl.BlockSpec((8, LANES), lambda b: (0, 0))],
        out_shape=[jax.ShapeDtypeStruct((B, S, LANES), BF16), jax.ShapeDtypeStruct((8, LANES), F32)],
        compiler_params=_params(("arbitrary",)),
    )(drow, dneg, proj3, b_pad)


def _rope_tables(S):
    half = ROPE_DIM // 2
    f32 = np.float32
    pos = np.arange(S, dtype=f32)
    inv_freq = f32(1.0) / np.power(f32(ROPE_THETA), np.arange(0, ROPE_DIM, 2, dtype=f32) / f32(ROPE_DIM)).astype(f32)
    ang = (pos[:, None] * inv_freq[None, :]).astype(f32).astype(np.float64)
    cos, sin = np.cos(ang).astype(f32), np.sin(ang).astype(f32)
    one = np.ones((S, HEAD_DIM - ROPE_DIM), f32)
    zero = np.zeros((S, HEAD_DIM - ROPE_DIM), f32)
    zh = np.zeros((S, half), f32)
    c = np.concatenate([cos, cos, one], axis=1)
    s1 = np.concatenate([-sin, zh, zero], axis=1)
    s2 = np.concatenate([zh, sin, zero], axis=1)
    return tuple(jnp.asarray(np.concatenate([t, t], axis=1)) for t in (c, s1, s2))


_HALF_ROPE = ROPE_DIM // 2


def _rope(t, c, s1, s2):
    return t * c + pltpu.roll(t, LANES - _HALF_ROPE, 1) * s1 + pltpu.roll(t, _HALF_ROPE, 1) * s2


def _rope_bwd(d, c, s1, s2):
    return d * c + pltpu.roll(d * s1, _HALF_ROPE, 1) + pltpu.roll(d * s2, LANES - _HALF_ROPE, 1)


def _scale_parts(scale):
    m, _ = math.frexp(scale)
    return (scale, None) if m == 0.5 else (None, scale)


def _log_masks(S, kind):
    nd = 1 if kind == "causal" else S // TQ
    a = np.arange(TQ)[:, None]
    b = np.arange(TK)[None, :]
    out = np.zeros((nd, TQ, TK), np.float32)
    for d in range(nd):
        delta = d * TQ + a - b
        if kind == "causal":
            m = (delta >= 0).astype(np.float64)
        else:
            m = sum(((delta >= 0) & (delta % dil == 0) & (delta <= w)).astype(np.float64) for w, dil in DILATIONS)
        out[d] = np.where(m > 0, np.log(np.maximum(m, 1.0)), NEG_INF)
    return jnp.asarray(out)


def _attn_setup(kind):
    pair = kind != "mem"
    e_dim = HEAD_DIM if pair else MEM_HEAD_DIM
    q_fold, s_scale = _scale_parts(1.0 / math.sqrt(e_dim))
    return dict(pair=pair, col0={"fox": P_FOX, "dil": P_DIL, "mem": P_MQ}[kind],
                n_blocks=FOX_HEADS // 2 if pair else MEM_HEADS, q_fold=q_fold, s_scale=s_scale,
                nh=2 if pair else 1)


def _cat(parts, axis):
    return parts[0] if len(parts) == 1 else jnp.concatenate(parts, axis=axis)


def _log_masks_t(S, kind):
    return jnp.swapaxes(_log_masks(S, kind), 1, 2)


def _head_rows(hh, pair):
    row = lax.broadcasted_iota(jnp.int32, (LANES, 1), 0)
    if not pair:
        return row >= 0
    return (row >= HEAD_DIM * hh) & (row < HEAD_DIM * (hh + 1))


def _attn_t_inputs(kind, src, S, negc_cols, mask, rope, kv):
    cfg = _attn_setup(kind)
    col0 = cfg["col0"]
    ins, in_specs = [], []
    if cfg["pair"]:
        ins.append(src)
        in_specs.append(pl.BlockSpec((1, S, PAIR_W), lambda b, h: (b, 0, col0 // PAIR_W + h)))
    else:
        ins += [src, kv, kv]
        in_specs += [pl.BlockSpec((1, S, LANES), lambda b, h: (b, 0, col0 // LANES + h)),
                     pl.BlockSpec((1, MEM_LEN, LANES), lambda b, h: (b, 0, h)),
                     pl.BlockSpec((1, MEM_LEN, LANES), lambda b, h: (b, 0, MEM_HEADS + h))]
    if negc_cols is not None:
        ins.append(negc_cols)
        in_specs.append(pl.BlockSpec((1, S, LANES), lambda b, h: (b, 0, 0)))
    if mask is not None:
        ins.append(mask)
        in_specs.append(pl.BlockSpec(mask.shape, lambda b, h: (0, 0, 0)))
    if rope is not None:
        ins += list(rope)
        in_specs += [pl.BlockSpec((S, LANES), lambda b, h: (0, 0))] * 3
    return ins, in_specs


def _attn_t_prep(cfg, refs, S, Sk, *, qT2s, ks, vs=None, vTs=None, kTs=None, nb=None):
    pair, nh = cfg["pair"], cfg["nh"]
    lane = lax.broadcasted_iota(jnp.int32, (1, LANES), 1)
    rope_refs = refs["rope"]

    def prep_q(n):
        rows = slice(n * TQ, (n + 1) * TQ)
        q = refs["load_q"](rows)
        if rope_refs is not None:
            q = _rope(q, *[t[rows, :] for t in rope_refs])
        if cfg["q_fold"] is not None:
            q = q * cfg["q_fold"]
        qtb = q.astype(BF16).T
        for hh in range(nh):
            qT2s[n, :, hh * TQ:(hh + 1) * TQ] = jnp.where(_head_rows(hh, pair), qtb, jnp.zeros_like(qtb))

    def prep_kv(n):
        rows = slice(n * TK, (n + 1) * TK)
        k, v = refs["load_kv"](rows)
        if rope_refs is not None:
            k = _rope(k, *[t[rows, :] for t in rope_refs])
        kb = k.astype(BF16)
        vb = v.astype(BF16)
        ks[rows, :] = kb
        if vs is not None:
            vs[rows, :] = vb
        if vTs is not None:
            vTs[n] = vb.T
        if kTs is not None:
            kTs[n] = kb.T
        if nb is not None:
            blk = refs["negc"][0, rows, :]
            for hh in range(nh):
                h = 2 * refs["block"] + hh
                col = jnp.sum(jnp.where(lane == h, blk, 0.0), axis=1, keepdims=True)
                nb[hh, rows, :] = jnp.broadcast_to(col, (TK, LANES))

    for n in range(S // TQ):
        prep_q(n)
    for n in range(Sk // TK):
        prep_kv(n)


def _raw_scores_t(cfg, k, qT2):
    sT = jnp.dot(k, qT2, preferred_element_type=F32)
    if cfg["s_scale"] is not None:
        sT = sT * cfg["s_scale"]
    return sT


def _bias_mask_t(cfg, sT, nb, mask_ref, kc, midx):
    nh = cfg["nh"]
    if nb is None and midx is None:
        return sT
    parts = []
    for hh in range(nh):
        t = sT[:, hh * TQ:(hh + 1) * TQ]
        if nb is not None:
            t = t + jnp.concatenate([nb[hh, kc, :]] * (TQ // LANES), axis=1)
        if midx is not None:
            t = t + mask_ref[midx]
        parts.append(t)
    return _cat(parts, 1)


def _tile_pairs(kind, nq, nk):
    if kind == "mem":
        return [(i, j) for i in range(nq) for j in range(nk)], (lambda i, j: None)
    pairs = [(i, j) for i in range(nq) for j in range(i + 1)]
    if kind == "fox":
        return pairs, (lambda i, j: 0 if j == i else None)
    return pairs, (lambda i, j: i - j)


def attn_fwd(kind, src, S, *, negc_cols=None, mask=None, rope=None, kv=None):
    B = src.shape[0]
    cfg = _attn_setup(kind)
    pair, nh = cfg["pair"], cfg["nh"]
    Sk = S if pair else MEM_LEN
    has_bias, has_rope = negc_cols is not None, rope is not None
    R = nh * TQ
    nq, nk = S // TQ, Sk // TK
    pairs, mask_index = _tile_pairs(kind, nq, nk)

    def body(*refs):
        refs = list(refs)
        if pair:
            qkv_ref = refs.pop(0)
            load_q = lambda rows: qkv_ref[0, rows, 0:LANES]
            load_kv = lambda rows: (qkv_ref[0, rows, LANES:2 * LANES], qkv_ref[0, rows, 2 * LANES:3 * LANES])
        else:
            q_ref, k_ref, v_ref = refs.pop(0), refs.pop(0), refs.pop(0)
            load_q = lambda rows: q_ref[0, rows, :]
            load_kv = lambda rows: (k_ref[0, rows, :], v_ref[0, rows, :])
        negc_ref = refs.pop(0) if has_bias else None
        mask_ref = refs.pop(0) if mask is not None else None
        rope_refs = [refs.pop(0) for _ in range(3)] if has_rope else None
        o_ref, lse_ref, qT2s, ks, vTs, s_a, s_b, p_a, p_b = refs[:9]
        nb = refs[9] if has_bias else None
        _attn_t_prep(cfg, dict(load_q=load_q, load_kv=load_kv, rope=rope_refs, negc=negc_ref,
                               block=pl.program_id(1)), S, Sk, qT2s=qT2s, ks=ks, vTs=vTs, nb=nb)

        def cols(j):
            return slice(j * TK, (j + 1) * TK)

        def scores(i, j):
            return _raw_scores_t(cfg, ks[cols(j), :], qT2s[i])

        def finish(i, m, l, accT):
            oT2 = accT / l
            oT = jnp.where(_head_rows(0, True), oT2[:, 0:TQ], oT2[:, TQ:2 * TQ]) if pair else oT2
            o_ref[0, i * TQ:(i + 1) * TQ, :] = oT.T
            lse_ref[0, 0, i:i + 1, :] = m + jnp.log(l)

        s_bufs, p_bufs = (s_a, s_b), (p_a, p_b)
        s_bufs[0][...] = scores(*pairs[0])
        m = l = accT = None
        for t, (i, j) in enumerate(pairs):
            cur, oth = t % 2, 1 - t % 2
            if t > 0:
                i_prev, j_prev = pairs[t - 1]
                pv = jnp.dot(vTs[j_prev], p_bufs[oth][...], preferred_element_type=F32)
                acc_full = pv if accT is None else accT + pv
            if t + 1 < len(pairs):
                s_bufs[oth][...] = scores(*pairs[t + 1])
            first = j == 0
            if first and t > 0:
                finish(i_prev, m, l, acc_full)
            sT = _bias_mask_t(cfg, s_bufs[cur][...], nb, mask_ref, cols(j), mask_index(i, j))
            m_tile = jnp.max(sT, axis=0, keepdims=True)
            m_new = m_tile if first else jnp.maximum(m, m_tile)
            p = jnp.exp(sT - m_new)
            p_bufs[cur][...] = p.astype(BF16)
            if first:
                l, accT = jnp.sum(p, axis=0, keepdims=True), None
            else:
                alpha = jnp.exp(m - m_new)
                l, accT = alpha * l + jnp.sum(p, axis=0, keepdims=True), acc_full * alpha
            m = m_new
        i_last, j_last = pairs[-1]
        pv = jnp.dot(vTs[j_last], p_bufs[(len(pairs) - 1) % 2][...], preferred_element_type=F32)
        finish(i_last, m, l, pv if accT is None else accT + pv)

    ins, in_specs = _attn_t_inputs(kind, src, S, negc_cols, mask, rope, kv)
    W = cfg["n_blocks"] * LANES
    scratch = [pltpu.VMEM((nq, LANES, R), BF16), pltpu.VMEM((Sk, LANES), BF16), pltpu.VMEM((nk, LANES, TK), BF16),
               pltpu.VMEM((TK, R), F32), pltpu.VMEM((TK, R), F32), pltpu.VMEM((TK, R), BF16), pltpu.VMEM((TK, R), BF16)]
    if has_bias:
        scratch.append(pltpu.VMEM((nh, Sk, LANES), F32))
    return pl.pallas_call(
        body, name=kind + "_attn_fwd", grid=(B, cfg["n_blocks"]),
        in_specs=in_specs,
        out_specs=[pl.BlockSpec((1, S, LANES), lambda b, h: (b, 0, h)),
                   pl.BlockSpec((1, 1, nq, R), lambda b, h: (b, h, 0, 0))],
        out_shape=[jax.ShapeDtypeStruct((B, S, W), F32), jax.ShapeDtypeStruct((B, cfg["n_blocks"], nq, R), F32)],
        scratch_shapes=scratch,
        compiler_params=_params(("arbitrary", "arbitrary")),
    )(*ins)


def attn_bwd(kind, src, do, o, lse, S, *, negc_cols=None, mask=None, rope=None, kv=None, token=None):
    B = src.shape[0]
    cfg = _attn_setup(kind)
    pair, nh, s_scale, q_fold = cfg["pair"], cfg["nh"], cfg["s_scale"], cfg["q_fold"]
    Sk = S if pair else MEM_LEN
    has_bias, has_rope = negc_cols is not None, rope is not None
    R = nh * TQ
    nq, nk = S // TQ, Sk // TK
    pairs, mask_index = _tile_pairs(kind, nq, nk)

    def body(*refs):
        refs = list(refs)
        if pair:
            qkv_ref = refs.pop(0)
            load_q = lambda rows: qkv_ref[0, rows, 0:LANES]
            load_kv = lambda rows: (qkv_ref[0, rows, LANES:2 * LANES], qkv_ref[0, rows, 2 * LANES:3 * LANES])
        else:
            q_ref, k_ref, v_ref = refs.pop(0), refs.pop(0), refs.pop(0)
            load_q = lambda rows: q_ref[0, rows, :]
            load_kv = lambda rows: (k_ref[0, rows, :], v_ref[0, rows, :])
        negc_ref = refs.pop(0) if has_bias else None
        mask_ref = refs.pop(0) if mask is not None else None
        rope_refs = [refs.pop(0) for _ in range(3)] if has_rope else None
        do_ref, o_ref, lse_ref = refs.pop(0), refs.pop(0), refs.pop(0)
        if token is not None:
            refs.pop(0)
        if pair:
            dqkv_ref = refs.pop(0)
            dneg_ref = refs.pop(0) if has_bias else None
            drow_ref = refs.pop(0) if has_bias else None
        else:
            dq_ref, dk_ref, dv_ref = refs.pop(0), refs.pop(0), refs.pop(0)
        qT2s, ks, vs, kTs, doT2s, delta_s, dk_acc, dv_acc = refs[:8]
        bufs_a, bufs_b = refs[8:12], refs[12:16]
        nb, dneg_acc = (refs[16], refs[17]) if has_bias else (None, None)
        lane = lax.broadcasted_iota(jnp.int32, (1, LANES), 1)
        _attn_t_prep(cfg, dict(load_q=load_q, load_kv=load_kv, rope=rope_refs, negc=negc_ref,
                               block=pl.program_id(1)), S, Sk,
                     qT2s=qT2s, ks=ks, vs=vs, kTs=kTs, nb=nb)

        def prep_do(n):
            rows = slice(n * TQ, (n + 1) * TQ)
            doT = do_ref[0, rows, :].astype(BF16).astype(F32).T
            prodT = doT * o_ref[0, rows, :].T
            doTb = doT.astype(BF16)
            for hh in range(nh):
                hm = _head_rows(hh, pair)
                doT2s[n, :, hh * TQ:(hh + 1) * TQ] = jnp.where(hm, doTb, jnp.zeros_like(doTb))
                delta_s[n:n + 1, hh * TQ:(hh + 1) * TQ] = jnp.sum(jnp.where(hm, prodT, 0.0), axis=0, keepdims=True)

        for n in range(nq):
            prep_do(n)
        dk_acc[...] = jnp.zeros(dk_acc.shape, F32)
        dv_acc[...] = jnp.zeros(dv_acc.shape, F32)
        if has_bias:
            dneg_acc[...] = jnp.zeros(dneg_acc.shape, F32)

        def cols(j):
            return slice(j * TK, (j + 1) * TK)

        nt_dims = (((1,), (1,)), ((), ()))

        def first_products(i, j, bufs):
            bufs[0][...] = _raw_scores_t(cfg, ks[cols(j), :], qT2s[i])
            bufs[1][...] = jnp.dot(vs[cols(j), :], doT2s[i], preferred_element_type=F32)

        def last_products(i, j, bufs, dqT2):
            dv_acc[j] += lax.dot_general(doT2s[i], bufs[2][...], nt_dims, preferred_element_type=F32)
            dk_acc[j] += lax.dot_general(qT2s[i], bufs[3][...], nt_dims, preferred_element_type=F32)
            dq = jnp.dot(kTs[j], bufs[3][...], preferred_element_type=F32)
            return dq if dqT2 is None else dqT2 + dq

        def finish_q(i, dqT2, drow):
            rows = slice(i * TQ, (i + 1) * TQ)
            dqT = jnp.where(_head_rows(0, True), dqT2[:, 0:TQ], dqT2[:, TQ:2 * TQ]) if pair else dqT2
            dq = dqT.T
            if q_fold is not None:
                dq = dq * q_fold
            if has_rope:
                dq = _rope_bwd(dq, *[t[rows, :] for t in rope_refs])
            if pair:
                dqkv_ref[0, rows, 0:LANES] = dq.astype(BF16)
            else:
                dq_ref[0, rows, :] = dq.astype(BF16)
            if has_bias:
                drow_ref[0, 0, i:i + 1, :] = drow

        bufs = (bufs_a, bufs_b)
        first_products(*pairs[0], bufs[0])
        dqT2 = drow = None
        for t, (i, j) in enumerate(pairs):
            cur, oth = bufs[t % 2], bufs[1 - t % 2]
            first = j == 0
            if first and t > 0:
                i_prev, j_prev = pairs[t - 1]
                finish_q(i_prev, last_products(i_prev, j_prev, oth, dqT2), drow)
                dqT2 = drow = None
            sT = _bias_mask_t(cfg, cur[0][...], nb, mask_ref, cols(j), mask_index(i, j))
            pT = jnp.exp(sT - lse_ref[0, 0, i:i + 1, :])
            dsT = pT * (cur[1][...] - delta_s[i:i + 1, :])
            if has_bias:
                tile_rows = jnp.sum(dsT, axis=0, keepdims=True)
                drow = tile_rows if drow is None else drow + tile_rows
                for hh in range(nh):
                    part = dsT[:, hh * TQ:hh * TQ + LANES]
                    for u in range(1, TQ // LANES):
                        part = part + dsT[:, hh * TQ + u * LANES:hh * TQ + (u + 1) * LANES]
                    dneg_acc[hh, cols(j), :] += part
            if s_scale is not None:
                dsT = dsT * s_scale
            cur[2][...] = pT.astype(BF16)
            cur[3][...] = dsT.astype(BF16)
            if not first:
                dqT2 = last_products(*pairs[t - 1], oth, dqT2)
            if t + 1 < len(pairs):
                first_products(*pairs[t + 1], oth)
        i_last, j_last = pairs[-1]
        finish_q(i_last, last_products(i_last, j_last, bufs[(len(pairs) - 1) % 2], dqT2), drow)

        for n in range(nk):
            rows = slice(n * TK, (n + 1) * TK)
            dk = dk_acc[n].T
            dv = dv_acc[n].T
            if has_rope:
                dk = _rope_bwd(dk, *[t[rows, :] for t in rope_refs])
            if pair:
                dqkv_ref[0, rows, LANES:2 * LANES] = dk.astype(BF16)
                dqkv_ref[0, rows, 2 * LANES:3 * LANES] = dv.astype(BF16)
            else:
                dk_ref[0, rows, :] = dk.astype(BF16)
                dv_ref[0, rows, :] = dv.astype(BF16)
            if has_bias:
                x0 = jnp.sum(dneg_acc[0, rows, :], axis=1, keepdims=True)
                x1 = jnp.sum(dneg_acc[1, rows, :], axis=1, keepdims=True)
                dneg_ref[0, rows, :] = jnp.where(lane == 0, x0, jnp.where(lane == 1, x1, 0.0))

    ins, in_specs = _attn_t_inputs(kind, src, S, negc_cols, mask, rope, kv)
    row_spec = pl.BlockSpec((1, S, LANES), lambda b, h: (b, 0, h))
    vec_spec = pl.BlockSpec((1, 1, nq, R), lambda b, h: (b, h, 0, 0))
    ins += [do, o, lse]
    in_specs += [row_spec, row_spec, vec_spec]
    if token is not None:
        ins.append(token)
        in_specs.append(pl.BlockSpec(token.shape, lambda b, h: (0, 0)))
    W = cfg["n_blocks"] * LANES
    if pair:
        out_specs = [pl.BlockSpec((1, S, PAIR_W), lambda b, h: (b, 0, h))]
        out_shape = [jax.ShapeDtypeStruct((B, S, 3 * W), BF16)]
        if has_bias:
            out_specs += [row_spec, vec_spec]
            out_shape += [jax.ShapeDtypeStruct((B, S, W), F32), jax.ShapeDtypeStruct((B, cfg["n_blocks"], nq, R), F32)]
    else:
        kv_spec = pl.BlockSpec((1, MEM_LEN, LANES), lambda b, h: (b, 0, h))
        out_specs = [row_spec, kv_spec, kv_spec]
        out_shape = [jax.ShapeDtypeStruct((B, S, W), BF16)] + [jax.ShapeDtypeStruct((B, MEM_LEN, W), BF16)] * 2
    scratch = [pltpu.VMEM((nq, LANES, R), BF16), pltpu.VMEM((Sk, LANES), BF16),
               pltpu.VMEM((Sk, LANES), BF16), pltpu.VMEM((nk, LANES, TK), BF16), pltpu.VMEM((nq, LANES, R), BF16),
               pltpu.VMEM((nq, R), F32), pltpu.VMEM((nk, LANES, TK), F32), pltpu.VMEM((nk, LANES, TK), F32)]
    pair_bufs = [pltpu.VMEM((TK, R), F32), pltpu.VMEM((TK, R), F32), pltpu.VMEM((TK, R), BF16), pltpu.VMEM((TK, R), BF16)]
    scratch += pair_bufs + pair_bufs
    if has_bias:
        scratch += [pltpu.VMEM((nh, Sk, LANES), F32), pltpu.VMEM((nh, Sk, LANES), F32)]
    return pl.pallas_call(
        body, name=kind + "_attn_bwd", grid=(B, cfg["n_blocks"]),
        in_specs=in_specs, out_specs=out_specs, out_shape=out_shape, scratch_shapes=scratch,
        compiler_params=_params(("arbitrary", "arbitrary")),
    )(*ins)


def _sigmoid(g):
    return 1.0 / (1.0 + jnp.exp(-g))


def out_step(proj, o_fox, o_dil, o_mem, w_out, x, target, gf, tm):
    T = x.shape[0]

    def body(fg_ref, dg_ref, mg_ref, of_ref, od_ref, om_ref, w_ref, x_ref, t_ref, gf_ref,
             dx_ref, dof_ref, dod_ref, dom_ref, dfg_ref, ddg_ref, dmg_ref, gw_ref, sm_ref, gw_acc):
        branches = []
        for g_ref, o_ref in ((fg_ref, of_ref), (dg_ref, od_ref), (mg_ref, om_ref)):
            g = g_ref[...]
            sg = _sigmoid(g)
            o = o_ref[...]
            branches.append((g, sg, o))
        ymix = jnp.concatenate([(o * (g * sg)).astype(BF16) for g, sg, o in branches], axis=1)
        x2 = x_ref[...] + jnp.dot(ymix, w_ref[...], preferred_element_type=F32)
        r = lax.rsqrt(jnp.mean(x2 * x2, axis=-1, keepdims=True) + RMS_EPS)
        yn = x2 * r
        err = yn * gf_ref[...] - t_ref[...]
        loss = 0.5 * jnp.sum(jnp.sum(err * err, axis=-1, keepdims=True) / D_MODEL, axis=0, keepdims=True)
        dyf = err / D_MODEL
        dgf = jnp.sum(dyf * yn, axis=0, keepdims=True)
        dyn = dyf * gf_ref[...]
        dx2 = r * (dyn - yn * jnp.mean(dyn * yn, axis=-1, keepdims=True))
        dx_ref[...] = dx2
        dxb = dx2.astype(BF16)
        dmix = lax.dot_general(dxb, w_ref[...], (((1,), (1,)), ((), ())), preferred_element_type=F32)
        col = 0
        for (g, sg, o), do_ref, dgate_ref in zip(branches, (dof_ref, dod_ref, dom_ref), (dfg_ref, ddg_ref, dmg_ref)):
            d = dmix[:, col:col + g.shape[1]]
            col += g.shape[1]
            do_ref[...] = (d * (g * sg)).astype(BF16)
            dgate_ref[...] = (d * o * (sg * (1.0 + g * (1.0 - sg)))).astype(BF16)
        row = lax.broadcasted_iota(jnp.int32, (8, D_MODEL), 0)
        upd = jnp.where(row == 0, dgf, jnp.where(row == 1, loss, 0.0))

        @pl.when(pl.program_id(0) == 0)
        def _():
            sm_ref[...] = jnp.zeros(sm_ref.shape, F32)
            gw_acc[...] = jnp.zeros(gw_acc.shape, F32)

        sm_ref[...] += upd
        gw_acc[...] += lax.dot_general(ymix, dxb, (((0,), (0,)), ((), ())), preferred_element_type=F32)

        @pl.when(pl.program_id(0) == T // tm - 1)
        def _():
            gw_ref[...] = gw_acc[...].astype(BF16)

    def rows(w, col=0):
        return pl.BlockSpec((tm, w), lambda i: (i, col))

    return pl.pallas_call(
        body, name="out_step", grid=(T // tm,),
        in_specs=[rows(FOX_W, P_FG // FOX_W), rows(DIL_W, P_DG // DIL_W), rows(MEM_W, P_MG // MEM_W),
                  rows(FOX_W), rows(DIL_W), rows(MEM_W),
                  pl.BlockSpec((MIX_W, D_MODEL), lambda i: (0, 0)),
                  rows(D_MODEL), rows(D_MODEL), pl.BlockSpec((1, D_MODEL), lambda i: (0, 0))],
        out_specs=[rows(D_MODEL), rows(FOX_W), rows(DIL_W), rows(MEM_W), rows(FOX_W), rows(DIL_W), rows(MEM_W),
                   pl.BlockSpec((MIX_W, D_MODEL), lambda i: (0, 0)), pl.BlockSpec((8, D_MODEL), lambda i: (0, 0))],
        out_shape=[jax.ShapeDtypeStruct((T, D_MODEL), F32), jax.ShapeDtypeStruct((T, FOX_W), BF16),
                   jax.ShapeDtypeStruct((T, DIL_W), BF16), jax.ShapeDtypeStruct((T, MEM_W), BF16),
                   jax.ShapeDtypeStruct((T, FOX_W), BF16), jax.ShapeDtypeStruct((T, DIL_W), BF16),
                   jax.ShapeDtypeStruct((T, MEM_W), BF16), jax.ShapeDtypeStruct((MIX_W, D_MODEL), BF16),
                   jax.ShapeDtypeStruct((8, D_MODEL), F32)],
        scratch_shapes=[pltpu.VMEM((MIX_W, D_MODEL), F32)],
        compiler_params=_params(("arbitrary",)),
    )(proj, proj, proj, o_fox, o_dil, o_mem, w_out, x, target, gf)


def adamw(w, g, m, v, tr, name):
    lead = w.shape[:-2]
    R, C = w.shape[-2:]
    zeros = (0,) * len(lead)

    def body(w_ref, g_ref, m_ref, v_ref, d_ref, mo_ref, vo_ref):
        gv = g_ref[...]
        mn = ADAM_B1 * m_ref[...] + (1.0 - ADAM_B1) * gv
        vn = ADAM_B2 * v_ref[...] + (1.0 - ADAM_B2) * jnp.square(gv)
        m_hat = mn / (1.0 - ADAM_B1 ** ADAM_STEP)
        v_hat = vn / (1.0 - ADAM_B2 ** ADAM_STEP)
        d_ref[...] = -ADAM_LR * (m_hat / (jnp.sqrt(v_hat) + ADAM_EPS) + ADAM_WD * w_ref[...])
        mo_ref[...] = mn
        vo_ref[...] = vn

    spec = pl.BlockSpec((1,) * len(lead) + (tr, C), lambda i: zeros + (i, 0))
    return pl.pallas_call(
        body, name=name, grid=(pl.cdiv(R, tr),),
        in_specs=[spec] * 4, out_specs=[spec] * 3,
        out_shape=[jax.ShapeDtypeStruct(w.shape, F32)] * 3,
        compiler_params=_params(("arbitrary",)),
    )(w, g, m, v)


def _pad_row(v, width):
    return jnp.concatenate([v, jnp.zeros((1, width - v.shape[1]), v.dtype)], axis=1)


def local_grads(x, mem, norm_g, b_forget, mem_norm_g, final_norm_g, loss_target, w_in_p, first_token, small_weights,
                start_exchange):
    B, S, D = x.shape
    T = B * S
    xt = x.reshape(T, D)
    memt = mem.reshape(B * MEM_LEN, D)
    b_pad = _pad_row(b_forget, LANES)

    h, h_t = rms_fwd(xt, norm_g, 512, "rms_x", with_transpose=True)
    proj = mm_nn(h, w_in_p, 512, PW // 3, "in_proj", first_token)
    proj3 = proj.reshape(B, S, PW)

    negc = fox_gate(proj3, b_pad)
    causal = _log_masks_t(S, "causal")
    dilated = _log_masks_t(S, "dilated")
    rope = _rope_tables(S)

    o_fox, lse_fox = attn_fwd("fox", proj3, S, negc_cols=negc, mask=causal)
    o_dil, lse_dil = attn_fwd("dil", proj3, S, mask=dilated, rope=rope)

    w_kv, w_out = small_weights(o_dil)
    mh, mh_t = rms_fwd(memt, mem_norm_g, B * MEM_LEN, "rms_mem", with_transpose=True)
    mkv = mm_nn(mh, w_kv, B * MEM_LEN, 2 * MEM_W, "mem_kv_proj")
    mkv3 = mkv.reshape(B, MEM_LEN, 2 * MEM_W)
    o_mem, lse_mem = attn_fwd("mem", proj3, S, kv=mkv3)

    dx2, do_fox, do_dil, do_mem, dfg, ddg, dmg, g_out, small_out = out_step(
        proj, o_fox.reshape(T, FOX_W), o_dil.reshape(T, DIL_W), o_mem.reshape(T, MEM_W), w_out,
        xt, loss_target.reshape(T, D), final_norm_g.reshape(1, D), 256)

    gates = [(dfg, P_FG), (ddg, P_DG), (dmg, P_MG)]
    g_gates = mm_tn_multi(h_t, [arr for arr, _ in gates], 1024, "w_in_grad_gates", BF16)
    first, token = start_exchange([g_gates, g_out], "early_exchange_a")

    dqkv_fox, dneg, drow = attn_bwd("fox", proj3, do_fox.reshape(B, S, FOX_W), o_fox, lse_fox, S,
                                    negc_cols=negc, mask=causal, token=token)
    drow = drow.reshape(B, FOX_HEADS // 2, S // TQ, 2, TQ).transpose(0, 1, 3, 2, 4).reshape(B, FOX_HEADS, S)
    drow = jnp.pad(drow, ((0, 0), (0, LANES - FOX_HEADS), (0, 0)))
    dflog, db_part = fox_gate_bwd(drow, dneg, proj3, b_pad)
    fox = [(dqkv_fox.reshape(T, 3 * FOX_W), P_FOX), (dflog.reshape(T, LANES), P_FLOG)]
    g_fox = mm_tn_multi(h_t, [arr for arr, _ in fox], 1024, "w_in_grad_fox", BF16)
    second, token = start_exchange([g_fox], "early_exchange_b")

    (dqkv_dil,) = attn_bwd("dil", proj3, do_dil.reshape(B, S, DIL_W), o_dil, lse_dil, S, mask=dilated, rope=rope,
                           token=token)
    dil = [(dqkv_dil.reshape(T, 3 * DIL_W), P_DIL)]
    g_dil = mm_tn_multi(h_t, [arr for arr, _ in dil], 1024, "w_in_grad_dil", BF16)
    third, token = start_exchange([g_dil], "early_exchange_c")

    dmq, dmk, dmv = attn_bwd("mem", proj3, do_mem.reshape(B, S, MEM_W), o_mem, lse_mem, S, kv=mkv3, token=token)
    mq = [(dmq.reshape(T, MEM_W), P_MQ)]
    g_mq = mm_tn_multi(h_t, [arr for arr, _ in mq], 1024, "w_in_grad_mq", BF16)
    dmkv = jnp.concatenate([dmk, dmv], axis=2).reshape(B * MEM_LEN, 2 * MEM_W)
    g_kv = mm_tn_multi(mh_t, [dmkv], B * MEM_LEN, "w_kv_grad", BF16)
    fourth, token = start_exchange([g_mq, g_kv], "early_exchange_d")

    grad_x, dng = in_proj_bwd_rms(gates + fox + dil + mq, w_in_p, xt, norm_g, dx2, 256, token)
    dmh = mm_nt(dmkv, w_kv, B * MEM_LEN, D, "mem_kv_bwd")
    _, dmng = rms_bwd(memt, mem_norm_g, dmh, None, B * MEM_LEN, "rms_mem_bwd")

    small = jnp.concatenate([dng[0:1], dmng[0:1], small_out[0:1], _pad_row(db_part[0:1], D), small_out[1:2],
                             jnp.zeros((3, D), F32)], axis=0)
    early = [(first, dqkv_fox), (second, dqkv_dil), (third, dmq), (fourth, grad_x)]
    return grad_x.reshape(B, S, D), early, small


def kernel(x, mem, norm_g, w_in, b_forget, mem_norm_g, w_mem_kv, w_out, final_norm_g, loss_target, m_norm_g, m_w_in, m_b_forget, m_mem_norm_g, m_w_mem_kv, m_w_out, m_final_norm_g, v_norm_g, v_w_in, v_b_forget, v_mem_norm_g, v_w_mem_kv, v_w_out, v_final_norm_g):
    D = D_MODEL
    (w_in_full,) = weight_gather([_pack_cols(w_in).astype(BF16).reshape(w_in.shape[1], PW)])
    gather, gather_token = early_exchange_start([w_mem_kv[0].astype(BF16), w_out[0].astype(BF16)], "early_gather",
                                                gather=True, after=w_in_full)

    def small_weights(after):
        _, gathered = early_exchange_wait(gather, after, "early_gather_wait")
        return gathered

    grad_x, early, small = local_grads(
        x, mem, norm_g, b_forget, mem_norm_g, final_norm_g, loss_target, w_in_full, gather_token, small_weights,
        early_exchange_start)

    (first, after_first), (second, after_second), (third, after_third), (fourth, after_fourth) = early
    (src_gates, src_out), (land_gates, land_out) = early_exchange_wait(first, after_first, "early_wait_a")
    (src_fox,), (land_fox,) = early_exchange_wait(second, after_second, "early_wait_b")
    (src_dil,), (land_dil,) = early_exchange_wait(third, after_third, "early_wait_c")
    (src_mq, src_kv), (land_mq, land_kv) = early_exchange_wait(fourth, after_fourth, "early_wait_d")
    gates = slot_sum8(src_gates, land_gates, 128, "sum_w_in_gates")
    gw_out = slot_sum8(src_out, land_out, 256, "sum_w_out")
    fox = slot_sum8(src_fox, land_fox, 128, "sum_w_in_fox")
    dil = slot_sum8(src_dil, land_dil, 128, "sum_w_in_dil")
    mq = slot_sum8(src_mq, land_mq, 128, "sum_w_in_mq")
    gw_kv = slot_sum8(src_kv, land_kv, 128, "sum_w_kv")

    tot = small_all_reduce(small)
    gw_in = _unpack_cols(jnp.concatenate(
        [fox[:, :3 * FOX_W], gates[:, :FOX_W], dil, gates[:, FOX_W:FOX_W + DIL_W], mq,
         gates[:, FOX_W + DIL_W:], fox[:, 3 * FOX_W:]], axis=1)[None])

    loss = tot[4, 0]
    g_norm, g_mem_norm, g_final, g_b = tot[0:1], tot[1:2], tot[2], tot[3:4, :FOX_HEADS]

    def rows8(*rows):
        rows = [r.reshape(1, -1) for r in rows]
        rows = [_pad_row(r, D) for r in rows]
        return jnp.concatenate(rows + [jnp.zeros((8 - len(rows), D), F32)], axis=0)

    sw = rows8(norm_g, mem_norm_g, final_norm_g, b_forget)
    sm = rows8(m_norm_g, m_mem_norm_g, m_final_norm_g, m_b_forget)
    sv = rows8(v_norm_g, v_mem_norm_g, v_final_norm_g, v_b_forget)
    d_s, m_s, v_s = adamw(sw, tot, sm, sv, 8, "adamw_small")
    d_in, m_in, v_in = adamw(w_in, gw_in, m_w_in, v_w_in, 32, "adamw_w_in")
    d_kv, m_kv, v_kv = adamw(w_mem_kv[0], gw_kv, m_w_mem_kv[0], v_w_mem_kv[0], 128, "adamw_w_kv")
    d_out, m_out, v_out = adamw(w_out[0], gw_out, m_w_out[0], v_w_out[0], 256, "adamw_w_out")

    def small_outs(t):
        return t[0:1], t[3:4, :FOX_HEADS], t[1:2], t[2]

    grads = (g_norm, gw_in, g_b, g_mem_norm, gw_kv[None], gw_out[None], g_final)
    outs = []
    for t, big in ((d_s, (d_in, d_kv, d_out)), (m_s, (m_in, m_kv, m_out)), (v_s, (v_in, v_kv, v_out))):
        n, b, mn, f = small_outs(t)
        outs += [n, big[0], b, mn, big[1][None], big[2][None], f]
    return (loss, grad_x, *grads, *outs)
```

```python
import math

import numpy as np
import jax
import jax.numpy as jnp
from jax import lax
from jax.experimental import pallas as pl
from jax.experimental.pallas import tpu as pltpu

F32 = jnp.float32
BF16 = jnp.bfloat16

D_MODEL = 1024
HEAD_DIM = 64
FOX_HEADS = 12
DIL_HEADS = 12
MEM_HEADS = 4
MEM_HEAD_DIM = 128
MEM_LEN = 256
FOX_W = FOX_HEADS * HEAD_DIM
DIL_W = DIL_HEADS * HEAD_DIM
MEM_W = MEM_HEADS * MEM_HEAD_DIM
MIX_W = FOX_W + DIL_W + MEM_W
DILATIONS = ((128, 1), (512, 4), (2048, 16))
ROPE_THETA = 500000.0
ROPE_DIM = HEAD_DIM // 4
RMS_EPS = 1e-6
NEG_INF = -1e30
IN_W = 4 * FOX_W + FOX_HEADS + 4 * DIL_W + 2 * MEM_W

ADAM_LR = 0.001
ADAM_B1 = 0.9
ADAM_B2 = 0.999
ADAM_EPS = 1e-08
ADAM_WD = 0.01
ADAM_STEP = 10

N_DEV = 8
LANES = 128
PAIR_W = 3 * LANES
TQ = 256
TK = 256

O_FQ, O_FK, O_FV, O_FG = 0, FOX_W, 2 * FOX_W, 3 * FOX_W
O_FLOG = 4 * FOX_W
O_DQ = O_FLOG + FOX_HEADS
O_DK, O_DV, O_DG = O_DQ + DIL_W, O_DQ + 2 * DIL_W, O_DQ + 3 * DIL_W
O_MQ = O_DQ + 4 * DIL_W
O_MG = O_MQ + MEM_W
P_FOX = 0
P_FG = P_FOX + 3 * FOX_W
P_DIL = P_FG + FOX_W
P_DG = P_DIL + 3 * DIL_W
P_MQ = P_DG + DIL_W
P_MG = P_MQ + MEM_W
P_FLOG = P_MG + MEM_W
PW = P_FLOG + LANES
A_FOX = 0
A_FLOG = A_FOX + 3 * FOX_W
PA = A_FLOG + LANES
B_FG = 0
B_DG = B_FG + FOX_W
B_DIL = B_DG + DIL_W
B_MQ = B_DIL + 3 * DIL_W
B_MG = -(-(B_MQ + MEM_W) // MEM_W) * MEM_W
PB = B_MG + MEM_W

VMEM_LIMIT = 56 * 1024 * 1024


def _pack_pieces():
    pieces = []
    for base in (O_FQ, O_DQ):
        seg = []
        for hp in range(FOX_HEADS // 2):
            for part in range(3):
                seg.append((base + part * FOX_W + hp * LANES, LANES))
        pieces.append(seg)
    fox, dil = pieces
    return fox + [(O_FG, FOX_W)] + dil + [(O_DG, DIL_W), (O_MQ, MEM_W), (O_MG, MEM_W), (O_FLOG, FOX_HEADS)]


def _pack_cols(w):
    parts = [w[..., s:s + n] for s, n in _pack_pieces()]
    parts.append(jnp.zeros(w.shape[:-1] + (LANES - FOX_HEADS,), w.dtype))
    return jnp.concatenate(parts, axis=-1)


def _split_cols(wp):
    def cut(start, width):
        return wp[..., start:start + width]

    group_a = jnp.concatenate([cut(P_FOX, 3 * FOX_W), cut(P_FLOG, LANES)], axis=-1)
    pad = jnp.zeros(wp.shape[:-1] + (B_MG - B_MQ - MEM_W,), wp.dtype)
    group_b = jnp.concatenate([cut(P_FG, FOX_W), cut(P_DG, DIL_W), cut(P_DIL, 3 * DIL_W), cut(P_MQ, MEM_W), pad,
                               cut(P_MG, MEM_W)], axis=-1)
    return group_a, group_b


def _unpack_cols(g):
    runs = []
    pos = 0
    for s, n in _pack_pieces():
        runs.append((s, n, pos))
        pos += n
    runs.sort()
    return jnp.concatenate([g[..., p:p + n] for s, n, p in runs], axis=-1)


def _params(sem=None, **kw):
    return pltpu.CompilerParams(dimension_semantics=sem, vmem_limit_bytes=VMEM_LIMIT, **kw)


def _mesh_pos():
    return lax.axis_index("x"), lax.axis_index("y"), lax.axis_index("c")


def _flip(v, d):
    return 1 - v if d else v


_RELATIONS = [(dx, dy, dc) for dx in (0, 1) for dy in (0, 1) for dc in (0, 1)][1:]
_SIBLING_AND_SAME_CORES = [(0, 0, 1), (1, 0, 0), (0, 1, 0), (1, 1, 0)]


def weight_gather(shards):
    n_arr = len(shards)
    rows = [s.shape[0] for s in shards]

    def body(*refs):
        in_refs = refs[:n_arr]
        out_refs = refs[n_arr:2 * n_arr]
        send_sems, recv_sems, local_sems = refs[2 * n_arr:]
        x, y, c = _mesh_pos()
        me, sibling = (x, y, c), (x, y, 1 - c)
        x_nbr, y_nbr, diag = (1 - x, y, c), (x, 1 - y, c), (1 - x, 1 - y, c)
        north = c == 1
        relay_from = (jnp.where(north, 1 - x, x), jnp.where(north, y, 1 - y), c)
        relay_to = (jnp.where(north, x, 1 - x), jnp.where(north, 1 - y, y), c)
        k_from = jnp.where(north, 1, 2)
        k_to = 3 - k_from

        def block(a, pos):
            px, py, pc = pos
            return out_refs[a].at[pl.ds((4 * px + 2 * py + pc) * rows[a], rows[a]), :]

        def copy(a, k, blk, to, src=None):
            return pltpu.make_async_remote_copy(
                src_ref=block(a, blk) if src is None else src, dst_ref=block(a, blk),
                send_sem=send_sems.at[a, k], recv_sem=recv_sems.at[a, k],
                device_id=to, device_id_type=pl.DeviceIdType.MESH)

        started = []
        mine = []
        for a in range(n_arr):
            cp = pltpu.make_async_copy(in_refs[a], block(a, me), local_sems.at[a])
            cp.start()
            mine.append(cp)
            first = [copy(a, 0, me, sibling, src=in_refs[a]), copy(a, 1, me, x_nbr, src=in_refs[a]),
                     copy(a, 2, me, y_nbr, src=in_refs[a])]
            for cp in first:
                cp.start()
            started += first
        for a in range(n_arr):
            copy(a, k_from, relay_from, me).wait_recv()
            second_hop = copy(a, 3, relay_from, relay_to)
            second_hop.start()
            passed = copy(a, 3 + k_from, relay_from, sibling)
            passed.start()
            started += [second_hop, passed]
        for a in range(n_arr):
            copy(a, k_to, relay_to, me).wait_recv()
            passed = copy(a, 3 + k_to, relay_to, sibling)
            passed.start()
            started.append(passed)
        for a in range(n_arr):
            copy(a, 3, diag, me).wait_recv()
            passed = copy(a, 6, diag, sibling)
            passed.start()
            started.append(passed)
        for a in range(n_arr):
            copy(a, 0, sibling, me).wait_recv()
            for k, chip in ((4, x_nbr), (5, y_nbr), (6, diag)):
                copy(a, k, (chip[0], chip[1], 1 - c), me).wait_recv()
        for cp in started:
            cp.wait_send()
        for cp in mine:
            cp.wait()

    any_spec = pl.BlockSpec(memory_space=pl.ANY)
    return pl.pallas_call(
        body, name="weight_gather",
        out_shape=[jax.ShapeDtypeStruct((N_DEV * s.shape[0], s.shape[1]), s.dtype) for s in shards],
        in_specs=[any_spec] * n_arr, out_specs=[any_spec] * n_arr,
        scratch_shapes=[pltpu.SemaphoreType.DMA((n_arr, 7)), pltpu.SemaphoreType.DMA((n_arr, 7)),
                        pltpu.SemaphoreType.DMA((n_arr,))],
    )(*shards)


N_CHIP = 4
_OTHER_CHIPS = [(1, 0), (0, 1), (1, 1)]


def small_all_reduce(small):
    vmem_spec = pl.BlockSpec(memory_space=pltpu.VMEM)

    def chip_body(small_ref, csum_ref, land, send_sem, recv_sem):
        x, y, c = _mesh_pos()
        swap = pltpu.make_async_remote_copy(
            src_ref=small_ref, dst_ref=land, send_sem=send_sem, recv_sem=recv_sem,
            device_id=(x, y, 1 - c), device_id_type=pl.DeviceIdType.MESH)
        swap.start()
        swap.wait_recv()
        swap.wait_send()
        csum_ref[...] = small_ref[...] + land[...]

    csum = pl.pallas_call(
        chip_body, name="small_sum_d2d", out_shape=jax.ShapeDtypeStruct(small.shape, small.dtype),
        in_specs=[vmem_spec], out_specs=vmem_spec,
        scratch_shapes=[pltpu.VMEM(small.shape, small.dtype), pltpu.SemaphoreType.DMA, pltpu.SemaphoreType.DMA],
    )(small)

    def all_body(csum_ref, tot_ref, land, send_sems, recv_sems):
        x, y, c = _mesh_pos()
        q_me = 2 * x + y
        land[q_me] = csum_ref[...]
        sends, recvs = [], []
        for j, (dx, dy) in enumerate(_OTHER_CHIPS):
            px, py = _flip(x, dx), _flip(y, dy)
            common = dict(send_sem=send_sems.at[j], recv_sem=recv_sems.at[j],
                          device_id=(px, py, c), device_id_type=pl.DeviceIdType.MESH)
            sends.append(pltpu.make_async_remote_copy(src_ref=csum_ref, dst_ref=land.at[q_me], **common))
            recvs.append(pltpu.make_async_remote_copy(src_ref=csum_ref, dst_ref=land.at[2 * px + py], **common))
        for cp in sends:
            cp.start()
        for cp in recvs:
            cp.wait_recv()
        for cp in sends:
            cp.wait_send()
        tot = land[0]
        for q in range(1, N_CHIP):
            tot = tot + land[q]
        tot_ref[...] = tot

    return pl.pallas_call(
        all_body, name="small_sum_ici", out_shape=jax.ShapeDtypeStruct(small.shape, small.dtype),
        in_specs=[vmem_spec], out_specs=vmem_spec,
        scratch_shapes=[pltpu.VMEM((N_CHIP,) + small.shape, small.dtype),
                        pltpu.SemaphoreType.DMA((3,)), pltpu.SemaphoreType.DMA((3,))],
    )(csum)


_HBM = pl.BlockSpec(memory_space=pltpu.HBM)
_SEM = pl.BlockSpec(memory_space=pltpu.SEMAPHORE)
_EFFECT = pltpu.SideEffectType.DATAFLOW_SIDE_EFFECTING


def _early_copies(src_refs, land_refs, send_sems, recv_sems, rows, gather, relations):
    x, y, c = _mesh_pos()
    me = 4 * x + 2 * y + c
    copies = []
    for a in range(len(src_refs)):
        for dx, dy, dc in relations:
            px, py, pc = _flip(x, dx), _flip(y, dy), _flip(c, dc)
            peer = 4 * px + 2 * py + pc
            copies.append(pltpu.make_async_remote_copy(
                src_ref=src_refs[a] if gather else src_refs[a].at[pl.ds(peer * rows[a], rows[a]), :],
                dst_ref=land_refs[a].at[pl.ds(me * rows[a], rows[a]), :],
                send_sem=send_sems[a], recv_sem=recv_sems[a],
                device_id=(px, py, pc), device_id_type=pl.DeviceIdType.MESH))
    return copies


def early_exchange_start(srcs, name, gather=False, after=None, relations=_RELATIONS):
    n = len(srcs)
    if gather:
        rows = [s.shape[0] for s in srcs]
        me = 4 * lax.axis_index("x") + 2 * lax.axis_index("y") + lax.axis_index("c")
        lands = [lax.dynamic_update_slice(lax.empty((N_DEV * r, s.shape[1]), s.dtype), s, (me * r, 0))
                 for r, s in zip(rows, srcs)]
    else:
        rows = [s.shape[0] // N_DEV for s in srcs]
        lands = [lax.empty(s.shape, s.dtype) for s in srcs]

    extra = [] if after is None else [after]

    def body(*refs):
        src_refs, land_refs = refs[:n], refs[n:2 * n]
        first_sem = 2 * n + len(extra)
        send_sems, recv_sems = refs[first_sem:first_sem + n], refs[first_sem + n:first_sem + 2 * n]
        token = refs[-1]
        for cp in _early_copies(src_refs, land_refs, send_sems, recv_sems, rows, gather, relations):
            cp.start()
        token[...] = jnp.zeros_like(token)

    hbm = lambda a: pltpu.HBM(a.shape, a.dtype)
    outs = pl.pallas_call(
        body, name=name,
        out_shape=[pltpu.SemaphoreType.DMA(())] * (2 * n)
        + [hbm(a) for a in srcs] + [hbm(a) for a in lands] + [jax.ShapeDtypeStruct((8, LANES), F32)],
        in_specs=[_HBM] * (2 * n) + [pl.BlockSpec(memory_space=pl.ANY)] * len(extra),
        out_specs=[_SEM] * (2 * n) + [_HBM] * (2 * n) + [pl.BlockSpec(memory_space=pltpu.VMEM)],
        input_output_aliases={i: 2 * n + i for i in range(2 * n)},
        compiler_params=pltpu.CompilerParams(has_side_effects=_EFFECT),
    )(*[pltpu.with_memory_space_constraint(a, pltpu.HBM) for a in list(srcs) + lands], *extra)
    handle = dict(sems=outs[:2 * n], srcs=outs[2 * n:3 * n], lands=outs[3 * n:4 * n], rows=rows,
                  copies=len(relations))
    return handle, outs[-1]


def early_exchange_wait(handle, after, name):
    n = len(handle["srcs"])
    rows = handle["rows"]

    def body(*refs):
        src_refs, land_refs = refs[:n], refs[n:2 * n]
        send_sems, recv_sems = refs[2 * n:3 * n], refs[3 * n:4 * n]
        x, y, c = _mesh_pos()
        for a in range(n):
            span = pl.ds(0, handle["copies"] * rows[a])
            all_copies = pltpu.make_async_remote_copy(
                src_ref=land_refs[a].at[span, :], dst_ref=land_refs[a].at[span, :],
                send_sem=send_sems[a], recv_sem=recv_sems[a],
                device_id=(x, y, c), device_id_type=pl.DeviceIdType.MESH)
            all_copies.wait_send()
            all_copies.wait_recv()

    hbm = lambda a: pltpu.HBM(a.shape, a.dtype)
    ins = list(handle["srcs"]) + list(handle["lands"])
    outs = pl.pallas_call(
        body, name=name,
        out_shape=[hbm(a) for a in ins],
        in_specs=[_HBM] * (2 * n) + [_SEM] * (2 * n) + [pl.BlockSpec(memory_space=pl.ANY)],
        out_specs=[_HBM] * (2 * n),
        input_output_aliases={i: i for i in range(2 * n)},
        compiler_params=pltpu.CompilerParams(has_side_effects=_EFFECT),
    )(*ins, *handle["sems"], after)
    return outs[:n], outs[n:]


def pass_on_to_sibling(lands, rows, name):
    n = len(lands)

    def body(*refs):
        land_refs = refs[n:2 * n]
        send_sems, recv_sems = refs[2 * n:]
        x, y, c = _mesh_pos()
        copies = []
        for a in range(n):
            for k, (dx, dy) in enumerate(_OTHER_CHIPS):
                slot = 4 * _flip(x, dx) + 2 * _flip(y, dy) + c
                blk = land_refs[a].at[pl.ds(slot * rows[a], rows[a]), :]
                copies.append(pltpu.make_async_remote_copy(
                    src_ref=blk, dst_ref=blk, send_sem=send_sems.at[a, k], recv_sem=recv_sems.at[a, k],
                    device_id=(x, y, 1 - c), device_id_type=pl.DeviceIdType.MESH))
        for cp in copies:
            cp.start()
        for cp in copies:
            cp.wait_recv()
        for cp in copies:
            cp.wait_send()

    any_spec = pl.BlockSpec(memory_space=pl.ANY)
    return pl.pallas_call(
        body, name=name,
        out_shape=[jax.ShapeDtypeStruct(a.shape, a.dtype) for a in lands],
        in_specs=[any_spec] * n, out_specs=[any_spec] * n,
        input_output_aliases={i: i for i in range(n)},
        scratch_shapes=[pltpu.SemaphoreType.DMA((n, len(_OTHER_CHIPS))), pltpu.SemaphoreType.DMA((n, len(_OTHER_CHIPS)))],
    )(*lands)


def slot_sum8(src, land, tr, name):
    rows, cols = land.shape[0] // N_DEV, land.shape[1]
    x, y, c = _mesh_pos()
    me = (4 * x + 2 * y + c).astype(jnp.int32).reshape(1)

    def body(me_ref, src_ref, land_ref, o_ref):
        acc = None
        for d in range(N_DEV):
            term = jnp.where(d == me_ref[0], src_ref[0], land_ref[d]).astype(F32)
            acc = term if acc is None else acc + term
        o_ref[...] = acc

    return pl.pallas_call(
        body, name=name,
        grid_spec=pltpu.PrefetchScalarGridSpec(
            num_scalar_prefetch=1, grid=(rows // tr,),
            in_specs=[pl.BlockSpec((1, tr, cols), lambda i, w: (w[0], i, 0)),
                      pl.BlockSpec((N_DEV, tr, cols), lambda i, w: (0, i, 0))],
            out_specs=pl.BlockSpec((tr, cols), lambda i, w: (i, 0))),
        out_shape=jax.ShapeDtypeStruct((rows, cols), F32),
        compiler_params=_params(("arbitrary",)),
    )(me, src.reshape(N_DEV, rows, cols), land.reshape(N_DEV, rows, cols))


def mm_tn_multi(a_t, bs, tt, name, out_dtype=F32):
    K, T = a_t.shape
    widths = [b.shape[1] for b in bs]
    steps = T // tt

    def body(a_ref, *rest):
        b_refs, o_ref, acc = rest[:-2], rest[-2], rest[-1]

        @pl.when(pl.program_id(0) == 0)
        def _():
            acc[...] = jnp.zeros(acc.shape, F32)

        av = a_ref[...]
        col = 0
        for b_ref, w in zip(b_refs, widths):
            acc[:, col:col + w] += jnp.dot(av, b_ref[...], preferred_element_type=F32)
            col += w

        @pl.when(pl.program_id(0) == steps - 1)
        def _():
            o_ref[...] = acc[...].astype(out_dtype)

    return pl.pallas_call(
        body, name=name, grid=(steps,),
        in_specs=[pl.BlockSpec((K, tt), lambda t: (0, t))] + [pl.BlockSpec((tt, w), lambda t: (t, 0)) for w in widths],
        out_specs=pl.BlockSpec((K, sum(widths)), lambda t: (0, 0)),
        out_shape=jax.ShapeDtypeStruct((K, sum(widths)), out_dtype),
        scratch_shapes=[pltpu.VMEM((K, sum(widths)), F32)],
        compiler_params=_params(("arbitrary",)),
    )(a_t, *bs)


def rms_fwd(x, g, tm, name, with_transpose=False):
    M, K = x.shape

    def body(x_ref, g_ref, o_ref, *t_ref):
        xv = x_ref[...]
        r = lax.rsqrt(jnp.mean(xv * xv, axis=-1, keepdims=True) + RMS_EPS)
        h = ((xv * r) * g_ref[...]).astype(BF16)
        o_ref[...] = h
        if with_transpose:
            t_ref[0][...] = h.T

    out_specs = [pl.BlockSpec((tm, K), lambda i: (i, 0))]
    out_shape = [jax.ShapeDtypeStruct((M, K), BF16)]
    if with_transpose:
        out_specs.append(pl.BlockSpec((K, tm), lambda i: (0, i)))
        out_shape.append(jax.ShapeDtypeStruct((K, M), BF16))
    outs = pl.pallas_call(
        body, name=name, grid=(M // tm,),
        in_specs=[pl.BlockSpec((tm, K), lambda i: (i, 0)), pl.BlockSpec((1, K), lambda i: (0, 0))],
        out_specs=out_specs, out_shape=out_shape,
        compiler_params=_params(("arbitrary",)),
    )(x, g)
    return outs if with_transpose else outs[0]


def rms_bwd(x, g, dh, dres, tm, name):
    M, K = x.shape
    has_res = dres is not None

    def body(*refs):
        if has_res:
            x_ref, g_ref, dh_ref, dres_ref, dx_ref, dg_ref = refs
        else:
            x_ref, g_ref, dh_ref, dx_ref, dg_ref = refs
        xv = x_ref[...]
        r = lax.rsqrt(jnp.mean(xv * xv, axis=-1, keepdims=True) + RMS_EPS)
        xn = xv * r
        dhv = dh_ref[...]
        dxn = dhv * g_ref[...]
        dx = r * (dxn - xn * jnp.mean(dxn * xn, axis=-1, keepdims=True))
        if has_res:
            dx = dx + dres_ref[...]
        dx_ref[...] = dx
        part = jnp.sum(dhv * xn, axis=0, keepdims=True)
        row = lax.broadcasted_iota(jnp.int32, (8, K), 0)
        upd = jnp.where(row == 0, part, 0.0)

        @pl.when(pl.program_id(0) == 0)
        def _():
            dg_ref[...] = upd

        @pl.when(pl.program_id(0) != 0)
        def _():
            dg_ref[...] += upd

    row_spec = pl.BlockSpec((tm, K), lambda i: (i, 0))
    ins = [x, g, dh] + ([dres] if has_res else [])
    in_specs = [row_spec, pl.BlockSpec((1, K), lambda i: (0, 0)), row_spec] + ([row_spec] if has_res else [])
    return pl.pallas_call(
        body, name=name, grid=(M // tm,),
        in_specs=in_specs,
        out_specs=[row_spec, pl.BlockSpec((8, K), lambda i: (0, 0))],
        out_shape=[jax.ShapeDtypeStruct((M, K), F32), jax.ShapeDtypeStruct((8, K), F32)],
        compiler_params=_params(("arbitrary",)),
    )(*ins)


def mm_nn(a, b, tm, tn, name, token=None):
    M, K = a.shape
    N = b.shape[1]
    extra = [] if token is None else [token]

    def body(a_ref, b_ref, *rest):
        rest[-1][...] = jnp.dot(a_ref[...], b_ref[...], preferred_element_type=F32)

    return pl.pallas_call(
        body, name=name, grid=(N // tn, M // tm),
        in_specs=[pl.BlockSpec((tm, K), lambda j, i: (i, 0)), pl.BlockSpec((K, tn), lambda j, i: (0, j))]
        + [pl.BlockSpec(t.shape, lambda j, i: (0, 0)) for t in extra],
        out_specs=pl.BlockSpec((tm, tn), lambda j, i: (i, j)),
        out_shape=jax.ShapeDtypeStruct((M, N), F32),
        compiler_params=_params(("arbitrary", "arbitrary")),
    )(a, b, *extra)


def mm_nt(a, b, tm, tk, name):
    M, K = a.shape
    N = b.shape[0]

    def body(a_ref, b_ref, o_ref):
        part = lax.dot_general(a_ref[...], b_ref[...], (((1,), (1,)), ((), ())), preferred_element_type=F32)

        @pl.when(pl.program_id(1) == 0)
        def _():
            o_ref[...] = part

        @pl.when(pl.program_id(1) != 0)
        def _():
            o_ref[...] += part

    return pl.pallas_call(
        body, name=name, grid=(M // tm, K // tk),
        in_specs=[pl.BlockSpec((tm, tk), lambda i, k: (i, k)), pl.BlockSpec((N, tk), lambda i, k: (0, k))],
        out_specs=pl.BlockSpec((tm, N), lambda i, k: (i, 0)),
        out_shape=jax.ShapeDtypeStruct((M, N), F32),
        compiler_params=_params(("arbitrary", "arbitrary")),
    )(a, b)


def in_proj_bwd_rms(pieces, ws, x, g, dres, tm, token):
    M, N = x.shape

    def body(*refs):
        n = len(pieces)
        p_refs, w_refs = refs[:n], refs[n:n + len(ws)]
        x_ref, g_ref, dres_ref, _, dx_ref, dg_ref = refs[n + len(ws):]
        dh = None
        for p_ref, (arr, group, col) in zip(p_refs, pieces):
            part = lax.dot_general(p_ref[...], w_refs[group][:, col:col + arr.shape[1]], (((1,), (1,)), ((), ())),
                                   preferred_element_type=F32)
            dh = part if dh is None else dh + part
        xv = x_ref[...]
        r = lax.rsqrt(jnp.mean(xv * xv, axis=-1, keepdims=True) + RMS_EPS)
        xn = xv * r
        dxn = dh * g_ref[...]
        dx_ref[...] = r * (dxn - xn * jnp.mean(dxn * xn, axis=-1, keepdims=True)) + dres_ref[...]
        row = lax.broadcasted_iota(jnp.int32, (8, N), 0)
        upd = jnp.where(row == 0, jnp.sum(dh * xn, axis=0, keepdims=True), 0.0)

        @pl.when(pl.program_id(0) == 0)
        def _():
            dg_ref[...] = upd

        @pl.when(pl.program_id(0) != 0)
        def _():
            dg_ref[...] += upd

    row_spec = pl.BlockSpec((tm, N), lambda i: (i, 0))
    return pl.pallas_call(
        body, name="in_proj_bwd", grid=(M // tm,),
        in_specs=[pl.BlockSpec((tm, arr.shape[1]), lambda i: (i, 0)) for arr, _, _ in pieces]
        + [pl.BlockSpec(w.shape, lambda i: (0, 0)) for w in ws]
        + [row_spec, pl.BlockSpec((1, N), lambda i: (0, 0)), row_spec, pl.BlockSpec(token.shape, lambda i: (0, 0))],
        out_specs=[row_spec, pl.BlockSpec((8, N), lambda i: (0, 0))],
        out_shape=[jax.ShapeDtypeStruct((M, N), F32), jax.ShapeDtypeStruct((8, N), F32)],
        compiler_params=_params(("arbitrary",)),
    )(*[arr for arr, _, _ in pieces], *ws, x, g, dres, token)


def _log_sigmoid(z):
    return jnp.minimum(z, 0.0) - jnp.log(1.0 + jnp.exp(-jnp.abs(z)))


def _tri(n, lower):
    r = lax.broadcasted_iota(jnp.int32, (n, n), 0)
    c = lax.broadcasted_iota(jnp.int32, (n, n), 1)
    return jnp.where((r >= c) if lower else (r <= c), 1.0, 0.0).astype(F32)


def fox_gate(proj3, b_pad):
    B, S, _ = proj3.shape
    nblk = S // TK

    def body(f_ref, b_ref, o_ref):
        tri = _tri(TK, True)
        carry = jnp.zeros((1, LANES), F32)
        for n in range(nblk):
            z = f_ref[0, n * TK:(n + 1) * TK, :] + b_ref[...]
            logf = _log_sigmoid(z)
            cs = jnp.dot(tri, logf, preferred_element_type=F32, precision=lax.Precision.HIGHEST) + carry
            carry = cs[TK - 1:TK, :]
            o_ref[0, n * TK:(n + 1) * TK, :] = -cs

    return pl.pallas_call(
        body, name="fox_gate", grid=(B,),
        in_specs=[pl.BlockSpec((1, S, LANES), lambda b: (b, 0, A_FLOG // LANES)),
                  pl.BlockSpec((1, LANES), lambda b: (0, 0))],
        out_specs=pl.BlockSpec((1, S, LANES), lambda b: (b, 0, 0)),
        out_shape=jax.ShapeDtypeStruct((B, S, LANES), F32),
        compiler_params=_params(("arbitrary",)),
    )(proj3, b_pad)


def fox_gate_bwd(drow, dneg, proj3, b_pad):
    B, S, _ = proj3.shape
    nblk = S // TK

    def body(d_ref, r_ref, f_ref, b_ref, o_ref, db_ref):
        tri = _tri(TK, False)
        lane = lax.broadcasted_iota(jnp.int32, (TK, LANES), 1)
        carry = jnp.zeros((1, LANES), F32)
        dbsum = jnp.zeros((1, LANES), F32)
        for n in reversed(range(nblk)):
            dk_side = None
            for hp in range(FOX_HEADS // 2):
                two = jnp.where(lane < 2, r_ref[0, n * TK:(n + 1) * TK, hp * LANES:(hp + 1) * LANES], 0.0)
                two = pltpu.roll(two, 2 * hp, 1) if hp else two
                dk_side = two if dk_side is None else dk_side + two
            dc = jnp.where(lane < FOX_HEADS, d_ref[0, :, n * TK:(n + 1) * TK].T - dk_side, 0.0)
            rs = jnp.dot(tri, dc, preferred_element_type=F32, precision=lax.Precision.HIGHEST) + carry
            carry = rs[0:1, :]
            z = f_ref[0, n * TK:(n + 1) * TK, :] + b_ref[...]
            dz = rs * (1.0 / (1.0 + jnp.exp(z)))
            o_ref[0, n * TK:(n + 1) * TK, :] = dz.astype(BF16)
            dbsum = dbsum + jnp.sum(dz, axis=0, keepdims=True)
        row = lax.broadcasted_iota(jnp.int32, (8, LANES), 0)
        upd = jnp.where(row == 0, dbsum, 0.0)

        @pl.when(pl.program_id(0) == 0)
        def _():
            db_ref[...] = upd

        @pl.when(pl.program_id(0) != 0)
        def _():
            db_ref[...] += upd

    return pl.pallas_call(
        body, name="fox_gate_bwd", grid=(B,),
        in_specs=[pl.BlockSpec((1, LANES, S), lambda b: (b, 0, 0)),
                  pl.BlockSpec((1, S, FOX_W), lambda b: (b, 0, 0)),
                  pl.BlockSpec((1, S, LANES), lambda b: (b, 0, A_FLOG // LANES)),
                  pl.BlockSpec((1, LANES), lambda b: (0, 0))],
        out_specs=[pl.BlockSpec((1, S, LANES), lambda b: (b, 0, 0)), pl.BlockSpec((8, LANES), lambda b: (0, 0))],
        out_shape=[jax.ShapeDtypeStruct((B, S, LANES), BF16), jax.ShapeDtypeStruct((8, LANES), F32)],
        compiler_params=_params(("arbitrary",)),
    )(drow, dneg, proj3, b_pad)


def _rope_tables(S):
    half = ROPE_DIM // 2
    f32 = np.float32
    pos = np.arange(S, dtype=f32)
    inv_freq = f32(1.0) / np.power(f32(ROPE_THETA), np.arange(0, ROPE_DIM, 2, dtype=f32) / f32(ROPE_DIM)).astype(f32)
    ang = (pos[:, None] * inv_freq[None, :]).astype(f32).astype(np.float64)
    cos, sin = np.cos(ang).astype(f32), np.sin(ang).astype(f32)
    one = np.ones((S, HEAD_DIM - ROPE_DIM), f32)
    zero = np.zeros((S, HEAD_DIM - ROPE_DIM), f32)
    zh = np.zeros((S, half), f32)
    c = np.concatenate([cos, cos, one], axis=1)
    s1 = np.concatenate([-sin, zh, zero], axis=1)
    s2 = np.concatenate([zh, sin, zero], axis=1)
    return tuple(jnp.asarray(np.concatenate([t, t], axis=1)) for t in (c, s1, s2))


_HALF_ROPE = ROPE_DIM // 2


def _rope(t, c, s1, s2):
    return t * c + pltpu.roll(t, LANES - _HALF_ROPE, 1) * s1 + pltpu.roll(t, _HALF_ROPE, 1) * s2


def _rope_bwd(d, c, s1, s2):
    return d * c + pltpu.roll(d * s1, _HALF_ROPE, 1) + pltpu.roll(d * s2, LANES - _HALF_ROPE, 1)


def _scale_parts(scale):
    m, _ = math.frexp(scale)
    return (scale, None) if m == 0.5 else (None, scale)


def _log_masks(S, kind):
    nd = 1 if kind == "causal" else S // TQ
    a = np.arange(TQ)[:, None]
    b = np.arange(TK)[None, :]
    out = np.zeros((nd, TQ, TK), np.float32)
    for d in range(nd):
        delta = d * TQ + a - b
        if kind == "causal":
            m = (delta >= 0).astype(np.float64)
        else:
            m = sum(((delta >= 0) & (delta % dil == 0) & (delta <= w)).astype(np.float64) for w, dil in DILATIONS)
        out[d] = np.where(m > 0, np.log(np.maximum(m, 1.0)), NEG_INF)
    return jnp.asarray(out)


def _attn_setup(kind):
    pair = kind != "mem"
    e_dim = HEAD_DIM if pair else MEM_HEAD_DIM
    q_fold, s_scale = _scale_parts(1.0 / math.sqrt(e_dim))
    return dict(pair=pair, col0={"fox": A_FOX, "dil": B_DIL, "mem": B_MQ}[kind],
                n_blocks=FOX_HEADS // 2 if pair else MEM_HEADS, q_fold=q_fold, s_scale=s_scale,
                nh=2 if pair else 1)


def _cat(parts, axis):
    return parts[0] if len(parts) == 1 else jnp.concatenate(parts, axis=axis)


def _log_masks_t(S, kind):
    return jnp.swapaxes(_log_masks(S, kind), 1, 2)


def _head_rows(hh, pair):
    row = lax.broadcasted_iota(jnp.int32, (LANES, 1), 0)
    if not pair:
        return row >= 0
    return (row >= HEAD_DIM * hh) & (row < HEAD_DIM * (hh + 1))


def _attn_t_inputs(kind, src, S, negc_cols, mask, rope, kv):
    cfg = _attn_setup(kind)
    col0 = cfg["col0"]
    ins, in_specs = [], []
    if cfg["pair"]:
        ins.append(src)
        in_specs.append(pl.BlockSpec((1, S, PAIR_W), lambda b, h: (b, 0, col0 // PAIR_W + h)))
    else:
        ins += [src, kv, kv]
        in_specs += [pl.BlockSpec((1, S, LANES), lambda b, h: (b, 0, col0 // LANES + h)),
                     pl.BlockSpec((1, MEM_LEN, LANES), lambda b, h: (b, 0, h)),
                     pl.BlockSpec((1, MEM_LEN, LANES), lambda b, h: (b, 0, MEM_HEADS + h))]
    if negc_cols is not None:
        ins.append(negc_cols)
        in_specs.append(pl.BlockSpec((1, S, LANES), lambda b, h: (b, 0, 0)))
    if mask is not None:
        ins.append(mask)
        in_specs.append(pl.BlockSpec(mask.shape, lambda b, h: (0, 0, 0)))
    if rope is not None:
        ins += list(rope)
        in_specs += [pl.BlockSpec((S, LANES), lambda b, h: (0, 0))] * 3
    return ins, in_specs


def _attn_t_prep(cfg, refs, S, Sk, *, qT2s, ks, vs=None, vTs=None, kTs=None, nb=None):
    pair, nh = cfg["pair"], cfg["nh"]
    lane = lax.broadcasted_iota(jnp.int32, (1, LANES), 1)
    rope_refs = refs["rope"]

    def prep_q(n):
        rows = slice(n * TQ, (n + 1) * TQ)
        q = refs["load_q"](rows)
        if rope_refs is not None:
            q = _rope(q, *[t[rows, :] for t in rope_refs])
        if cfg["q_fold"] is not None:
            q = q * cfg["q_fold"]
        qtb = q.astype(BF16).T
        for hh in range(nh):
            qT2s[n, :, hh * TQ:(hh + 1) * TQ] = jnp.where(_head_rows(hh, pair), qtb, jnp.zeros_like(qtb))

    def prep_kv(n):
        rows = slice(n * TK, (n + 1) * TK)
        k, v = refs["load_kv"](rows)
        if rope_refs is not None:
            k = _rope(k, *[t[rows, :] for t in rope_refs])
        kb = k.astype(BF16)
        vb = v.astype(BF16)
        ks[rows, :] = kb
        if vs is not None:
            vs[rows, :] = vb
        if vTs is not None:
            vTs[n] = vb.T
        if kTs is not None:
            kTs[n] = kb.T
        if nb is not None:
            blk = refs["negc"][0, rows, :]
            for hh in range(nh):
                h = 2 * refs["block"] + hh
                col = jnp.sum(jnp.where(lane == h, blk, 0.0), axis=1, keepdims=True)
                nb[hh, rows, :] = jnp.broadcast_to(col, (TK, LANES))

    for n in range(S // TQ):
        prep_q(n)
    for n in range(Sk // TK):
        prep_kv(n)


def _raw_scores_t(cfg, k, qT2):
    sT = jnp.dot(k, qT2, preferred_element_type=F32)
    if cfg["s_scale"] is not None:
        sT = sT * cfg["s_scale"]
    return sT


def _bias_mask_t(cfg, sT, nb, mask_ref, kc, midx):
    nh = cfg["nh"]
    if nb is None and midx is None:
        return sT
    parts = []
    for hh in range(nh):
        t = sT[:, hh * TQ:(hh + 1) * TQ]
        if nb is not None:
            t = t + jnp.concatenate([nb[hh, kc, :]] * (TQ // LANES), axis=1)
        if midx is not None:
            t = t + mask_ref[midx]
        parts.append(t)
    return _cat(parts, 1)


def _tile_pairs(kind, nq, nk):
    if kind == "mem":
        return [(i, j) for i in range(nq) for j in range(nk)], (lambda i, j: None)
    pairs = [(i, j) for i in range(nq) for j in range(i + 1)]
    if kind == "fox":
        return pairs, (lambda i, j: 0 if j == i else None)
    return pairs, (lambda i, j: i - j)


def attn_fwd(kind, src, S, *, negc_cols=None, mask=None, rope=None, kv=None):
    B = src.shape[0]
    cfg = _attn_setup(kind)
    pair, nh = cfg["pair"], cfg["nh"]
    Sk = S if pair else MEM_LEN
    has_bias, has_rope = negc_cols is not None, rope is not None
    R = nh * TQ
    nq, nk = S // TQ, Sk // TK
    pairs, mask_index = _tile_pairs(kind, nq, nk)

    def body(*refs):
        refs = list(refs)
        if pair:
            qkv_ref = refs.pop(0)
            load_q = lambda rows: qkv_ref[0, rows, 0:LANES]
            load_kv = lambda rows: (qkv_ref[0, rows, LANES:2 * LANES], qkv_ref[0, rows, 2 * LANES:3 * LANES])
        else:
            q_ref, k_ref, v_ref = refs.pop(0), refs.pop(0), refs.pop(0)
            load_q = lambda rows: q_ref[0, rows, :]
            load_kv = lambda rows: (k_ref[0, rows, :], v_ref[0, rows, :])
        negc_ref = refs.pop(0) if has_bias else None
        mask_ref = refs.pop(0) if mask is not None else None
        rope_refs = [refs.pop(0) for _ in range(3)] if has_rope else None
        o_ref, lse_ref, qT2s, ks, vTs, s_a, s_b, p_a, p_b = refs[:9]
        nb = refs[9] if has_bias else None
        _attn_t_prep(cfg, dict(load_q=load_q, load_kv=load_kv, rope=rope_refs, negc=negc_ref,
                               block=pl.program_id(1)), S, Sk, qT2s=qT2s, ks=ks, vTs=vTs, nb=nb)

        def cols(j):
            return slice(j * TK, (j + 1) * TK)

        def scores(i, j):
            return _raw_scores_t(cfg, ks[cols(j), :], qT2s[i])

        def finish(i, m, l, accT):
            oT2 = accT / l
            oT = jnp.where(_head_rows(0, True), oT2[:, 0:TQ], oT2[:, TQ:2 * TQ]) if pair else oT2
            o_ref[0, i * TQ:(i + 1) * TQ, :] = oT.T
            lse_ref[0, 0, i:i + 1, :] = m + jnp.log(l)

        s_bufs, p_bufs = (s_a, s_b), (p_a, p_b)
        s_bufs[0][...] = scores(*pairs[0])
        m = l = accT = None
        for t, (i, j) in enumerate(pairs):
            cur, oth = t % 2, 1 - t % 2
            if t > 0:
                i_prev, j_prev = pairs[t - 1]
                pv = jnp.dot(vTs[j_prev], p_bufs[oth][...], preferred_element_type=F32)
                acc_full = pv if accT is None else accT + pv
            if t + 1 < len(pairs):
                s_bufs[oth][...] = scores(*pairs[t + 1])
            first = j == 0
            if first and t > 0:
                finish(i_prev, m, l, acc_full)
            sT = _bias_mask_t(cfg, s_bufs[cur][...], nb, mask_ref, cols(j), mask_index(i, j))
            m_tile = jnp.max(sT, axis=0, keepdims=True)
            m_new = m_tile if first else jnp.maximum(m, m_tile)
            p = jnp.exp(sT - m_new)
            p_bufs[cur][...] = p.astype(BF16)
            if first:
                l, accT = jnp.sum(p, axis=0, keepdims=True), None
            else:
                alpha = jnp.exp(m - m_new)
                l, accT = alpha * l + jnp.sum(p, axis=0, keepdims=True), acc_full * alpha
            m = m_new
        i_last, j_last = pairs[-1]
        pv = jnp.dot(vTs[j_last], p_bufs[(len(pairs) - 1) % 2][...], preferred_element_type=F32)
        finish(i_last, m, l, pv if accT is None else accT + pv)

    ins, in_specs = _attn_t_inputs(kind, src, S, negc_cols, mask, rope, kv)
    W = cfg["n_blocks"] * LANES
    scratch = [pltpu.VMEM((nq, LANES, R), BF16), pltpu.VMEM((Sk, LANES), BF16), pltpu.VMEM((nk, LANES, TK), BF16),
               pltpu.VMEM((TK, R), F32), pltpu.VMEM((TK, R), F32), pltpu.VMEM((TK, R), BF16), pltpu.VMEM((TK, R), BF16)]
    if has_bias:
        scratch.append(pltpu.VMEM((nh, Sk, LANES), F32))
    return pl.pallas_call(
        body, name=kind + "_attn_fwd", grid=(B, cfg["n_blocks"]),
        in_specs=in_specs,
        out_specs=[pl.BlockSpec((1, S, LANES), lambda b, h: (b, 0, h)),
                   pl.BlockSpec((1, 1, nq, R), lambda b, h: (b, h, 0, 0))],
        out_shape=[jax.ShapeDtypeStruct((B, S, W), F32), jax.ShapeDtypeStruct((B, cfg["n_blocks"], nq, R), F32)],
        scratch_shapes=scratch,
        compiler_params=_params(("arbitrary", "arbitrary")),
    )(*ins)


def attn_bwd(kind, src, do, o, lse, S, *, negc_cols=None, mask=None, rope=None, kv=None, token=None):
    B = src.shape[0]
    cfg = _attn_setup(kind)
    pair, nh, s_scale, q_fold = cfg["pair"], cfg["nh"], cfg["s_scale"], cfg["q_fold"]
    Sk = S if pair else MEM_LEN
    has_bias, has_rope = negc_cols is not None, rope is not None
    R = nh * TQ
    nq, nk = S // TQ, Sk // TK
    pairs, mask_index = _tile_pairs(kind, nq, nk)

    def body(*refs):
        refs = list(refs)
        if pair:
            qkv_ref = refs.pop(0)
            load_q = lambda rows: qkv_ref[0, rows, 0:LANES]
            load_kv = lambda rows: (qkv_ref[0, rows, LANES:2 * LANES], qkv_ref[0, rows, 2 * LANES:3 * LANES])
        else:
            q_ref, k_ref, v_ref = refs.pop(0), refs.pop(0), refs.pop(0)
            load_q = lambda rows: q_ref[0, rows, :]
            load_kv = lambda rows: (k_ref[0, rows, :], v_ref[0, rows, :])
        negc_ref = refs.pop(0) if has_bias else None
        mask_ref = refs.pop(0) if mask is not None else None
        rope_refs = [refs.pop(0) for _ in range(3)] if has_rope else None
        do_ref, o_ref, lse_ref = refs.pop(0), refs.pop(0), refs.pop(0)
        if token is not None:
            refs.pop(0)
        if pair:
            dqkv_ref = refs.pop(0)
            dneg_ref = refs.pop(0) if has_bias else None
            drow_ref = refs.pop(0) if has_bias else None
        else:
            dq_ref, dk_ref, dv_ref = refs.pop(0), refs.pop(0), refs.pop(0)
        qT2s, ks, vs, kTs, doT2s, delta_s, dk_acc, dv_acc = refs[:8]
        bufs_a, bufs_b = refs[8:12], refs[12:16]
        nb, dneg_acc = (refs[16], refs[17]) if has_bias else (None, None)
        lane = lax.broadcasted_iota(jnp.int32, (1, LANES), 1)
        _attn_t_prep(cfg, dict(load_q=load_q, load_kv=load_kv, rope=rope_refs, negc=negc_ref,
                               block=pl.program_id(1)), S, Sk,
                     qT2s=qT2s, ks=ks, vs=vs, kTs=kTs, nb=nb)

        def prep_do(n):
            rows = slice(n * TQ, (n + 1) * TQ)
            doT = do_ref[0, rows, :].astype(BF16).astype(F32).T
            prodT = doT * o_ref[0, rows, :].T
            doTb = doT.astype(BF16)
            for hh in range(nh):
                hm = _head_rows(hh, pair)
                doT2s[n, :, hh * TQ:(hh + 1) * TQ] = jnp.where(hm, doTb, jnp.zeros_like(doTb))
                delta_s[n:n + 1, hh * TQ:(hh + 1) * TQ] = jnp.sum(jnp.where(hm, prodT, 0.0), axis=0, keepdims=True)

        for n in range(nq):
            prep_do(n)
        dk_acc[...] = jnp.zeros(dk_acc.shape, F32)
        dv_acc[...] = jnp.zeros(dv_acc.shape, F32)
        if has_bias:
            dneg_acc[...] = jnp.zeros(dneg_acc.shape, F32)

        def cols(j):
            return slice(j * TK, (j + 1) * TK)

        nt_dims = (((1,), (1,)), ((), ()))

        def first_products(i, j, bufs):
            bufs[0][...] = _raw_scores_t(cfg, ks[cols(j), :], qT2s[i])
            bufs[1][...] = jnp.dot(vs[cols(j), :], doT2s[i], preferred_element_type=F32)

        def last_products(i, j, bufs, dqT2):
            dv_acc[j] += lax.dot_general(doT2s[i], bufs[2][...], nt_dims, preferred_element_type=F32)
            dk_acc[j] += lax.dot_general(qT2s[i], bufs[3][...], nt_dims, preferred_element_type=F32)
            dq = jnp.dot(kTs[j], bufs[3][...], preferred_element_type=F32)
            return dq if dqT2 is None else dqT2 + dq

        def finish_q(i, dqT2, drow):
            rows = slice(i * TQ, (i + 1) * TQ)
            dqT = jnp.where(_head_rows(0, True), dqT2[:, 0:TQ], dqT2[:, TQ:2 * TQ]) if pair else dqT2
            dq = dqT.T
            if q_fold is not None:
                dq = dq * q_fold
            if has_rope:
                dq = _rope_bwd(dq, *[t[rows, :] for t in rope_refs])
            if pair:
                dqkv_ref[0, rows, 0:LANES] = dq.astype(BF16)
            else:
                dq_ref[0, rows, :] = dq.astype(BF16)
            if has_bias:
                drow_ref[0, 0, i:i + 1, :] = drow

        bufs = (bufs_a, bufs_b)
        first_products(*pairs[0], bufs[0])
        dqT2 = drow = None
        for t, (i, j) in enumerate(pairs):
            cur, oth = bufs[t % 2], bufs[1 - t % 2]
            first = j == 0
            if first and t > 0:
                i_prev, j_prev = pairs[t - 1]
                finish_q(i_prev, last_products(i_prev, j_prev, oth, dqT2), drow)
                dqT2 = drow = None
            sT = _bias_mask_t(cfg, cur[0][...], nb, mask_ref, cols(j), mask_index(i, j))
            pT = jnp.exp(sT - lse_ref[0, 0, i:i + 1, :])
            dsT = pT * (cur[1][...] - delta_s[i:i + 1, :])
            if has_bias:
                tile_rows = jnp.sum(dsT, axis=0, keepdims=True)
                drow = tile_rows if drow is None else drow + tile_rows
                for hh in range(nh):
                    part = dsT[:, hh * TQ:hh * TQ + LANES]
                    for u in range(1, TQ // LANES):
                        part = part + dsT[:, hh * TQ + u * LANES:hh * TQ + (u + 1) * LANES]
                    dneg_acc[hh, cols(j), :] += part
            if s_scale is not None:
                dsT = dsT * s_scale
            cur[2][...] = pT.astype(BF16)
            cur[3][...] = dsT.astype(BF16)
            if not first:
                dqT2 = last_products(*pairs[t - 1], oth, dqT2)
            if t + 1 < len(pairs):
                first_products(*pairs[t + 1], oth)
        i_last, j_last = pairs[-1]
        finish_q(i_last, last_products(i_last, j_last, bufs[(len(pairs) - 1) % 2], dqT2), drow)

        for n in range(nk):
            rows = slice(n * TK, (n + 1) * TK)
            dk = dk_acc[n].T
            dv = dv_acc[n].T
            if has_rope:
                dk = _rope_bwd(dk, *[t[rows, :] for t in rope_refs])
            if pair:
                dqkv_ref[0, rows, LANES:2 * LANES] = dk.astype(BF16)
                dqkv_ref[0, rows, 2 * LANES:3 * LANES] = dv.astype(BF16)
            else:
                dk_ref[0, rows, :] = dk.astype(BF16)
                dv_ref[0, rows, :] = dv.astype(BF16)
            if has_bias:
                x0 = jnp.sum(dneg_acc[0, rows, :], axis=1, keepdims=True)
                x1 = jnp.sum(dneg_acc[1, rows, :], axis=1, keepdims=True)
                dneg_ref[0, rows, :] = jnp.where(lane == 0, x0, jnp.where(lane == 1, x1, 0.0))

    ins, in_specs = _attn_t_inputs(kind, src, S, negc_cols, mask, rope, kv)
    row_spec = pl.BlockSpec((1, S, LANES), lambda b, h: (b, 0, h))
    vec_spec = pl.BlockSpec((1, 1, nq, R), lambda b, h: (b, h, 0, 0))
    ins += [do, o, lse]
    in_specs += [row_spec, row_spec, vec_spec]
    if token is not None:
        ins.append(token)
        in_specs.append(pl.BlockSpec(token.shape, lambda b, h: (0, 0)))
    W = cfg["n_blocks"] * LANES
    if pair:
        out_specs = [pl.BlockSpec((1, S, PAIR_W), lambda b, h: (b, 0, h))]
        out_shape = [jax.ShapeDtypeStruct((B, S, 3 * W), BF16)]
        if has_bias:
            out_specs += [row_spec, vec_spec]
            out_shape += [jax.ShapeDtypeStruct((B, S, W), F32), jax.ShapeDtypeStruct((B, cfg["n_blocks"], nq, R), F32)]
    else:
        kv_spec = pl.BlockSpec((1, MEM_LEN, LANES), lambda b, h: (b, 0, h))
        out_specs = [row_spec, kv_spec, kv_spec]
        out_shape = [jax.ShapeDtypeStruct((B, S, W), BF16)] + [jax.ShapeDtypeStruct((B, MEM_LEN, W), BF16)] * 2
    scratch = [pltpu.VMEM((nq, LANES, R), BF16), pltpu.VMEM((Sk, LANES), BF16),
               pltpu.VMEM((Sk, LANES), BF16), pltpu.VMEM((nk, LANES, TK), BF16), pltpu.VMEM((nq, LANES, R), BF16),
               pltpu.VMEM((nq, R), F32), pltpu.VMEM((nk, LANES, TK), F32), pltpu.VMEM((nk, LANES, TK), F32)]
    pair_bufs = [pltpu.VMEM((TK, R), F32), pltpu.VMEM((TK, R), F32), pltpu.VMEM((TK, R), BF16), pltpu.VMEM((TK, R), BF16)]
    scratch += pair_bufs + pair_bufs
    if has_bias:
        scratch += [pltpu.VMEM((nh, Sk, LANES), F32), pltpu.VMEM((nh, Sk, LANES), F32)]
    return pl.pallas_call(
        body, name=kind + "_attn_bwd", grid=(B, cfg["n_blocks"]),
        in_specs=in_specs, out_specs=out_specs, out_shape=out_shape, scratch_shapes=scratch,
        compiler_params=_params(("arbitrary", "arbitrary")),
    )(*ins)


def _sigmoid(g):
    return 1.0 / (1.0 + jnp.exp(-g))


def out_step(proj, o_fox, o_dil, o_mem, w_out, x, target, gf, tm):
    T = x.shape[0]

    def body(fg_ref, dg_ref, mg_ref, of_ref, od_ref, om_ref, w_ref, x_ref, t_ref, gf_ref,
             dx_ref, dof_ref, dod_ref, dom_ref, dfg_ref, ddg_ref, dmg_ref, gw_ref, sm_ref, gw_acc):
        branches = []
        for g_ref, o_ref in ((fg_ref, of_ref), (dg_ref, od_ref), (mg_ref, om_ref)):
            g = g_ref[...]
            sg = _sigmoid(g)
            o = o_ref[...]
            branches.append((g, sg, o))
        ymix = jnp.concatenate([(o * (g * sg)).astype(BF16) for g, sg, o in branches], axis=1)
        x2 = x_ref[...] + jnp.dot(ymix, w_ref[...], preferred_element_type=F32)
        r = lax.rsqrt(jnp.mean(x2 * x2, axis=-1, keepdims=True) + RMS_EPS)
        yn = x2 * r
        err = yn * gf_ref[...] - t_ref[...]
        loss = 0.5 * jnp.sum(jnp.sum(err * err, axis=-1, keepdims=True) / D_MODEL, axis=0, keepdims=True)
        dyf = err / D_MODEL
        dgf = jnp.sum(dyf * yn, axis=0, keepdims=True)
        dyn = dyf * gf_ref[...]
        dx2 = r * (dyn - yn * jnp.mean(dyn * yn, axis=-1, keepdims=True))
        dx_ref[...] = dx2
        dxb = dx2.astype(BF16)
        dmix = lax.dot_general(dxb, w_ref[...], (((1,), (1,)), ((), ())), preferred_element_type=F32)
        col = 0
        for (g, sg, o), do_ref, dgate_ref in zip(branches, (dof_ref, dod_ref, dom_ref), (dfg_ref, ddg_ref, dmg_ref)):
            d = dmix[:, col:col + g.shape[1]]
            col += g.shape[1]
            do_ref[...] = (d * (g * sg)).astype(BF16)
            dgate_ref[...] = (d * o * (sg * (1.0 + g * (1.0 - sg)))).astype(BF16)
        row = lax.broadcasted_iota(jnp.int32, (8, D_MODEL), 0)
        upd = jnp.where(row == 0, dgf, jnp.where(row == 1, loss, 0.0))

        @pl.when(pl.program_id(0) == 0)
        def _():
            sm_ref[...] = jnp.zeros(sm_ref.shape, F32)
            gw_acc[...] = jnp.zeros(gw_acc.shape, F32)

        sm_ref[...] += upd
        gw_acc[...] += lax.dot_general(ymix, dxb, (((0,), (0,)), ((), ())), preferred_element_type=F32)

        @pl.when(pl.program_id(0) == T // tm - 1)
        def _():
            gw_ref[...] = gw_acc[...].astype(BF16)

    def rows(w, col=0):
        return pl.BlockSpec((tm, w), lambda i: (i, col))

    return pl.pallas_call(
        body, name="out_step", grid=(T // tm,),
        in_specs=[rows(FOX_W, B_FG // FOX_W), rows(DIL_W, B_DG // DIL_W), rows(MEM_W, B_MG // MEM_W),
                  rows(FOX_W), rows(DIL_W), rows(MEM_W),
                  pl.BlockSpec((MIX_W, D_MODEL), lambda i: (0, 0)),
                  rows(D_MODEL), rows(D_MODEL), pl.BlockSpec((1, D_MODEL), lambda i: (0, 0))],
        out_specs=[rows(D_MODEL), rows(FOX_W), rows(DIL_W), rows(MEM_W), rows(FOX_W), rows(DIL_W), rows(MEM_W),
                   pl.BlockSpec((MIX_W, D_MODEL), lambda i: (0, 0)), pl.BlockSpec((8, D_MODEL), lambda i: (0, 0))],
        out_shape=[jax.ShapeDtypeStruct((T, D_MODEL), F32), jax.ShapeDtypeStruct((T, FOX_W), BF16),
                   jax.ShapeDtypeStruct((T, DIL_W), BF16), jax.ShapeDtypeStruct((T, MEM_W), BF16),
                   jax.ShapeDtypeStruct((T, FOX_W), BF16), jax.ShapeDtypeStruct((T, DIL_W), BF16),
                   jax.ShapeDtypeStruct((T, MEM_W), BF16), jax.ShapeDtypeStruct((MIX_W, D_MODEL), BF16),
                   jax.ShapeDtypeStruct((8, D_MODEL), F32)],
        scratch_shapes=[pltpu.VMEM((MIX_W, D_MODEL), F32)],
        compiler_params=_params(("arbitrary",)),
    )(proj, proj, proj, o_fox, o_dil, o_mem, w_out, x, target, gf)


def adamw(w, g, m, v, tr, name):
    lead = w.shape[:-2]
    R, C = w.shape[-2:]
    zeros = (0,) * len(lead)

    def body(w_ref, g_ref, m_ref, v_ref, d_ref, mo_ref, vo_ref):
        gv = g_ref[...]
        mn = ADAM_B1 * m_ref[...] + (1.0 - ADAM_B1) * gv
        vn = ADAM_B2 * v_ref[...] + (1.0 - ADAM_B2) * jnp.square(gv)
        m_hat = mn / (1.0 - ADAM_B1 ** ADAM_STEP)
        v_hat = vn / (1.0 - ADAM_B2 ** ADAM_STEP)
        d_ref[...] = -ADAM_LR * (m_hat / (jnp.sqrt(v_hat) + ADAM_EPS) + ADAM_WD * w_ref[...])
        mo_ref[...] = mn
        vo_ref[...] = vn

    spec = pl.BlockSpec((1,) * len(lead) + (tr, C), lambda i: zeros + (i, 0))
    return pl.pallas_call(
        body, name=name, grid=(pl.cdiv(R, tr),),
        in_specs=[spec] * 4, out_specs=[spec] * 3,
        out_shape=[jax.ShapeDtypeStruct(w.shape, F32)] * 3,
        compiler_params=_params(("arbitrary",)),
    )(w, g, m, v)


def _pad_row(v, width):
    return jnp.concatenate([v, jnp.zeros((1, width - v.shape[1]), v.dtype)], axis=1)


def local_grads(x, mem, norm_g, b_forget, mem_norm_g, final_norm_g, loss_target, w_in_a, first_token, late_weights,
                start_exchange):
    B, S, D = x.shape
    T = B * S
    xt = x.reshape(T, D)
    memt = mem.reshape(B * MEM_LEN, D)
    b_pad = _pad_row(b_forget, LANES)

    h, h_t = rms_fwd(xt, norm_g, 512, "rms_x", with_transpose=True)
    proj_a = mm_nn(h, w_in_a, 512, PA, "in_proj_a", first_token)
    proj_a3 = proj_a.reshape(B, S, PA)

    negc = fox_gate(proj_a3, b_pad)
    causal = _log_masks_t(S, "causal")
    dilated = _log_masks_t(S, "dilated")
    rope = _rope_tables(S)

    o_fox, lse_fox = attn_fwd("fox", proj_a3, S, negc_cols=negc, mask=causal)

    w_in_b, w_kv, w_out = late_weights(o_fox)
    proj_b = mm_nn(h, w_in_b, 512, PB // 2, "in_proj_b")
    proj_b3 = proj_b.reshape(B, S, PB)
    o_dil, lse_dil = attn_fwd("dil", proj_b3, S, mask=dilated, rope=rope)

    mh, mh_t = rms_fwd(memt, mem_norm_g, B * MEM_LEN, "rms_mem", with_transpose=True)
    mkv = mm_nn(mh, w_kv, B * MEM_LEN, 2 * MEM_W, "mem_kv_proj")
    mkv3 = mkv.reshape(B, MEM_LEN, 2 * MEM_W)
    o_mem, lse_mem = attn_fwd("mem", proj_b3, S, kv=mkv3)

    dx2, do_fox, do_dil, do_mem, dfg, ddg, dmg, g_out, small_out = out_step(
        proj_b, o_fox.reshape(T, FOX_W), o_dil.reshape(T, DIL_W), o_mem.reshape(T, MEM_W), w_out,
        xt, loss_target.reshape(T, D), final_norm_g.reshape(1, D), 256)

    gates = [(dfg, 1, B_FG), (ddg, 1, B_DG), (dmg, 1, B_MG)]
    g_gates = mm_tn_multi(h_t, [piece[0] for piece in gates], 1024, "w_in_grad_gates", BF16)
    first, token = start_exchange([g_gates, g_out], "early_exchange_a")

    dqkv_fox, dneg, drow = attn_bwd("fox", proj_a3, do_fox.reshape(B, S, FOX_W), o_fox, lse_fox, S,
                                    negc_cols=negc, mask=causal, token=token)
    drow = drow.reshape(B, FOX_HEADS // 2, S // TQ, 2, TQ).transpose(0, 1, 3, 2, 4).reshape(B, FOX_HEADS, S)
    drow = jnp.pad(drow, ((0, 0), (0, LANES - FOX_HEADS), (0, 0)))
    dflog, db_part = fox_gate_bwd(drow, dneg, proj_a3, b_pad)
    fox = [(dqkv_fox.reshape(T, 3 * FOX_W), 0, A_FOX), (dflog.reshape(T, LANES), 0, A_FLOG)]
    g_fox = mm_tn_multi(h_t, [piece[0] for piece in fox], 1024, "w_in_grad_fox", BF16)
    second, token = start_exchange([g_fox], "early_exchange_b")

    (dqkv_dil,) = attn_bwd("dil", proj_b3, do_dil.reshape(B, S, DIL_W), o_dil, lse_dil, S, mask=dilated, rope=rope,
                           token=token)
    dil = [(dqkv_dil.reshape(T, 3 * DIL_W), 1, B_DIL)]
    g_dil = mm_tn_multi(h_t, [piece[0] for piece in dil], 1024, "w_in_grad_dil", BF16)
    third, token = start_exchange([g_dil], "early_exchange_c")

    dmq, dmk, dmv = attn_bwd("mem", proj_b3, do_mem.reshape(B, S, MEM_W), o_mem, lse_mem, S, kv=mkv3, token=token)
    mq = [(dmq.reshape(T, MEM_W), 1, B_MQ)]
    g_mq = mm_tn_multi(h_t, [piece[0] for piece in mq], 1024, "w_in_grad_mq", BF16)
    dmkv = jnp.concatenate([dmk, dmv], axis=2).reshape(B * MEM_LEN, 2 * MEM_W)
    g_kv = mm_tn_multi(mh_t, [dmkv], B * MEM_LEN, "w_kv_grad", BF16)
    fourth, token = start_exchange([g_mq, g_kv], "early_exchange_d")

    grad_x, dng = in_proj_bwd_rms(gates + fox + dil + mq, (w_in_a, w_in_b), xt, norm_g, dx2, 256, token)
    dmh = mm_nt(dmkv, w_kv, B * MEM_LEN, D, "mem_kv_bwd")
    _, dmng = rms_bwd(memt, mem_norm_g, dmh, None, B * MEM_LEN, "rms_mem_bwd")

    small = jnp.concatenate([dng[0:1], dmng[0:1], small_out[0:1], _pad_row(db_part[0:1], D), small_out[1:2],
                             jnp.zeros((3, D), F32)], axis=0)
    early = [(first, dqkv_fox), (second, dqkv_dil), (third, dmq), (fourth, grad_x)]
    return grad_x.reshape(B, S, D), early, small


def kernel(x, mem, norm_g, w_in, b_forget, mem_norm_g, w_mem_kv, w_out, final_norm_g, loss_target, m_norm_g, m_w_in, m_b_forget, m_mem_norm_g, m_w_mem_kv, m_w_out, m_final_norm_g, v_norm_g, v_w_in, v_b_forget, v_mem_norm_g, v_w_mem_kv, v_w_out, v_final_norm_g):
    D = D_MODEL
    shard_a, shard_b = _split_cols(_pack_cols(w_in).astype(BF16).reshape(w_in.shape[1], PW))
    (w_in_a,) = weight_gather([shard_a])
    gather, gather_token = early_exchange_start(
        [shard_b, w_mem_kv[0].astype(BF16), w_out[0].astype(BF16)], "late_gather", gather=True, after=w_in_a,
        relations=_SIBLING_AND_SAME_CORES)

    def late_weights(after):
        _, gathered = early_exchange_wait(gather, after, "late_gather_wait")
        return pass_on_to_sibling(gathered, gather["rows"], "late_gather_pass")

    grad_x, early, small = local_grads(
        x, mem, norm_g, b_forget, mem_norm_g, final_norm_g, loss_target, w_in_a, gather_token, late_weights,
        early_exchange_start)

    (first, after_first), (second, after_second), (third, after_third), (fourth, after_fourth) = early
    (src_gates, src_out), (land_gates, land_out) = early_exchange_wait(first, after_first, "early_wait_a")
    (src_fox,), (land_fox,) = early_exchange_wait(second, after_second, "early_wait_b")
    (src_dil,), (land_dil,) = early_exchange_wait(third, after_third, "early_wait_c")
    (src_mq, src_kv), (land_mq, land_kv) = early_exchange_wait(fourth, after_fourth, "early_wait_d")
    gates = slot_sum8(src_gates, land_gates, 128, "sum_w_in_gates")
    gw_out = slot_sum8(src_out, land_out, 256, "sum_w_out")
    fox = slot_sum8(src_fox, land_fox, 128, "sum_w_in_fox")
    dil = slot_sum8(src_dil, land_dil, 128, "sum_w_in_dil")
    mq = slot_sum8(src_mq, land_mq, 128, "sum_w_in_mq")
    gw_kv = slot_sum8(src_kv, land_kv, 128, "sum_w_kv")

    tot = small_all_reduce(small)
    gw_in = _unpack_cols(jnp.concatenate(
        [fox[:, :3 * FOX_W], gates[:, :FOX_W], dil, gates[:, FOX_W:FOX_W + DIL_W], mq,
         gates[:, FOX_W + DIL_W:], fox[:, 3 * FOX_W:]], axis=1)[None])

    loss = tot[4, 0]
    g_norm, g_mem_norm, g_final, g_b = tot[0:1], tot[1:2], tot[2], tot[3:4, :FOX_HEADS]

    def rows8(*rows):
        rows = [r.reshape(1, -1) for r in rows]
        rows = [_pad_row(r, D) for r in rows]
        return jnp.concatenate(rows + [jnp.zeros((8 - len(rows), D), F32)], axis=0)

    sw = rows8(norm_g, mem_norm_g, final_norm_g, b_forget)
    sm = rows8(m_norm_g, m_mem_norm_g, m_final_norm_g, m_b_forget)
    sv = rows8(v_norm_g, v_mem_norm_g, v_final_norm_g, v_b_forget)
    d_s, m_s, v_s = adamw(sw, tot, sm, sv, 8, "adamw_small")
    d_in, m_in, v_in = adamw(w_in, gw_in, m_w_in, v_w_in, 32, "adamw_w_in")
    d_kv, m_kv, v_kv = adamw(w_mem_kv[0], gw_kv, m_w_mem_kv[0], v_w_mem_kv[0], 128, "adamw_w_kv")
    d_out, m_out, v_out = adamw(w_out[0], gw_out, m_w_out[0], v_w_out[0], 256, "adamw_w_out")

    def small_outs(t):
        return t[0:1], t[3:4, :FOX_HEADS], t[1:2], t[2]

    grads = (g_norm, gw_in, g_b, g_mem_norm, gw_kv[None], gw_out[None], g_final)
    outs = []
    for t, big in ((d_s, (d_in, d_kv, d_out)), (m_s, (m_in, m_kv, m_out)), (v_s, (v_in, v_kv, v_out))):
        n, b, mn, f = small_outs(t)
        outs += [n, big[0], b, mn, big[1][None], big[2][None], f]
    return (loss, grad_x, *grads, *outs)
```

```python
import math

import numpy as np
import jax
import jax.numpy as jnp
from jax import lax
from jax.experimental import pallas as pl
from jax.experimental.pallas import tpu as pltpu

F32 = jnp.float32
BF16 = jnp.bfloat16

D_MODEL = 1024
HEAD_DIM = 64
FOX_HEADS = 12
DIL_HEADS = 12
MEM_HEADS = 4
MEM_HEAD_DIM = 128
MEM_LEN = 256
FOX_W = FOX_HEADS * HEAD_DIM
DIL_W = DIL_HEADS * HEAD_DIM
MEM_W = MEM_HEADS * MEM_HEAD_DIM
MIX_W = FOX_W + DIL_W + MEM_W
DILATIONS = ((128, 1), (512, 4), (2048, 16))
ROPE_THETA = 500000.0
ROPE_DIM = HEAD_DIM // 4
RMS_EPS = 1e-6
NEG_INF = -1e30
IN_W = 4 * FOX_W + FOX_HEADS + 4 * DIL_W + 2 * MEM_W

ADAM_LR = 0.001
ADAM_B1 = 0.9
ADAM_B2 = 0.999
ADAM_EPS = 1e-08
ADAM_WD = 0.01
ADAM_STEP = 10

N_DEV = 8
LANES = 128
PAIR_W = 3 * LANES
TQ = 256
TK = 256

O_FQ, O_FK, O_FV, O_FG = 0, FOX_W, 2 * FOX_W, 3 * FOX_W
O_FLOG = 4 * FOX_W
O_DQ = O_FLOG + FOX_HEADS
O_DK, O_DV, O_DG = O_DQ + DIL_W, O_DQ + 2 * DIL_W, O_DQ + 3 * DIL_W
O_MQ = O_DQ + 4 * DIL_W
O_MG = O_MQ + MEM_W
P_FOX = 0
P_FG = P_FOX + 3 * FOX_W
P_DIL = P_FG + FOX_W
P_DG = P_DIL + 3 * DIL_W
P_MQ = P_DG + DIL_W
P_MG = P_MQ + MEM_W
P_FLOG = P_MG + MEM_W
PW = P_FLOG + LANES
A_FOX = 0
A_FLOG = A_FOX + 3 * FOX_W
PA = A_FLOG + LANES
B_FG = 0
B_DG = B_FG + FOX_W
B_DIL = B_DG + DIL_W
B_MQ = B_DIL + 3 * DIL_W
B_MG = -(-(B_MQ + MEM_W) // MEM_W) * MEM_W
PB = B_MG + MEM_W

VMEM_LIMIT = 56 * 1024 * 1024


def _pack_pieces():
    pieces = []
    for base in (O_FQ, O_DQ):
        seg = []
        for hp in range(FOX_HEADS // 2):
            for part in range(3):
                seg.append((base + part * FOX_W + hp * LANES, LANES))
        pieces.append(seg)
    fox, dil = pieces
    return fox + [(O_FG, FOX_W)] + dil + [(O_DG, DIL_W), (O_MQ, MEM_W), (O_MG, MEM_W), (O_FLOG, FOX_HEADS)]


def _pack_cols(w):
    parts = [w[..., s:s + n] for s, n in _pack_pieces()]
    parts.append(jnp.zeros(w.shape[:-1] + (LANES - FOX_HEADS,), w.dtype))
    return jnp.concatenate(parts, axis=-1)


def _split_cols(wp):
    def cut(start, width):
        return wp[..., start:start + width]

    group_a = jnp.concatenate([cut(P_FOX, 3 * FOX_W), cut(P_FLOG, LANES)], axis=-1)
    pad = jnp.zeros(wp.shape[:-1] + (B_MG - B_MQ - MEM_W,), wp.dtype)
    group_b = jnp.concatenate([cut(P_FG, FOX_W), cut(P_DG, DIL_W), cut(P_DIL, 3 * DIL_W), cut(P_MQ, MEM_W), pad,
                               cut(P_MG, MEM_W)], axis=-1)
    return group_a, group_b


def _unpack_cols(g):
    runs = []
    pos = 0
    for s, n in _pack_pieces():
        runs.append((s, n, pos))
        pos += n
    runs.sort()
    return jnp.concatenate([g[..., p:p + n] for s, n, p in runs], axis=-1)


def _params(sem=None, **kw):
    return pltpu.CompilerParams(dimension_semantics=sem, vmem_limit_bytes=VMEM_LIMIT, **kw)


def _mesh_pos():
    return lax.axis_index("x"), lax.axis_index("y"), lax.axis_index("c")


def _flip(v, d):
    return 1 - v if d else v


_RELATIONS = [(dx, dy, dc) for dx in (0, 1) for dy in (0, 1) for dc in (0, 1)][1:]
_SIBLING_AND_SAME_CORES = [(0, 0, 1), (1, 0, 0), (0, 1, 0), (1, 1, 0)]


def weight_gather(shards):
    n_arr = len(shards)
    rows = [s.shape[0] for s in shards]

    def body(*refs):
        in_refs = refs[:n_arr]
        out_refs = refs[n_arr:2 * n_arr]
        send_sems, recv_sems, local_sems = refs[2 * n_arr:]
        x, y, c = _mesh_pos()
        me, sibling = (x, y, c), (x, y, 1 - c)
        x_nbr, y_nbr, diag = (1 - x, y, c), (x, 1 - y, c), (1 - x, 1 - y, c)
        north = c == 1
        relay_from = (jnp.where(north, 1 - x, x), jnp.where(north, y, 1 - y), c)
        relay_to = (jnp.where(north, x, 1 - x), jnp.where(north, 1 - y, y), c)
        k_from = jnp.where(north, 1, 2)
        k_to = 3 - k_from

        def block(a, pos):
            px, py, pc = pos
            return out_refs[a].at[pl.ds((4 * px + 2 * py + pc) * rows[a], rows[a]), :]

        def copy(a, k, blk, to, src=None):
            return pltpu.make_async_remote_copy(
                src_ref=block(a, blk) if src is None else src, dst_ref=block(a, blk),
                send_sem=send_sems.at[a, k], recv_sem=recv_sems.at[a, k],
                device_id=to, device_id_type=pl.DeviceIdType.MESH)

        started = []
        mine = []
        for a in range(n_arr):
            cp = pltpu.make_async_copy(in_refs[a], block(a, me), local_sems.at[a])
            cp.start()
            mine.append(cp)
            first = [copy(a, 0, me, sibling, src=in_refs[a]), copy(a, 1, me, x_nbr, src=in_refs[a]),
                     copy(a, 2, me, y_nbr, src=in_refs[a])]
            for cp in first:
                cp.start()
            started += first
        for a in range(n_arr):
            copy(a, k_from, relay_from, me).wait_recv()
            second_hop = copy(a, 3, relay_from, relay_to)
            second_hop.start()
            passed = copy(a, 3 + k_from, relay_from, sibling)
            passed.start()
            started += [second_hop, passed]
        for a in range(n_arr):
            copy(a, k_to, relay_to, me).wait_recv()
            passed = copy(a, 3 + k_to, relay_to, sibling)
            passed.start()
            started.append(passed)
        for a in range(n_arr):
            copy(a, 3, diag, me).wait_recv()
            passed = copy(a, 6, diag, sibling)
            passed.start()
            started.append(passed)
        for a in range(n_arr):
            copy(a, 0, sibling, me).wait_recv()
            for k, chip in ((4, x_nbr), (5, y_nbr), (6, diag)):
                copy(a, k, (chip[0], chip[1], 1 - c), me).wait_recv()
        for cp in started:
            cp.wait_send()
        for cp in mine:
            cp.wait()

    any_spec = pl.BlockSpec(memory_space=pl.ANY)
    return pl.pallas_call(
        body, name="weight_gather",
        out_shape=[jax.ShapeDtypeStruct((N_DEV * s.shape[0], s.shape[1]), s.dtype) for s in shards],
        in_specs=[any_spec] * n_arr, out_specs=[any_spec] * n_arr,
        scratch_shapes=[pltpu.SemaphoreType.DMA((n_arr, 7)), pltpu.SemaphoreType.DMA((n_arr, 7)),
                        pltpu.SemaphoreType.DMA((n_arr,))],
    )(*shards)


_OTHER_CHIPS = [(1, 0), (0, 1), (1, 1)]


def small_all_reduce(small):
    vmem_spec = pl.BlockSpec(memory_space=pltpu.VMEM)

    def body(small_ref, tot_ref, land, send_sems, recv_sems):
        x, y, c = _mesh_pos()
        me = 4 * x + 2 * y + c
        land[me] = small_ref[...]
        sends, recvs = [], []
        for j, (dx, dy, dc) in enumerate(_RELATIONS):
            px, py, pc = _flip(x, dx), _flip(y, dy), _flip(c, dc)
            common = dict(send_sem=send_sems.at[j], recv_sem=recv_sems.at[j],
                          device_id=(px, py, pc), device_id_type=pl.DeviceIdType.MESH)
            sends.append(pltpu.make_async_remote_copy(src_ref=small_ref, dst_ref=land.at[me], **common))
            recvs.append(pltpu.make_async_remote_copy(src_ref=small_ref, dst_ref=land.at[4 * px + 2 * py + pc], **common))
        for cp in sends:
            cp.start()
        for cp in recvs:
            cp.wait_recv()
        for cp in sends:
            cp.wait_send()
        tot = land[0]
        for d in range(1, N_DEV):
            tot = tot + land[d]
        tot_ref[...] = tot

    return pl.pallas_call(
        body, name="small_sum", out_shape=jax.ShapeDtypeStruct(small.shape, small.dtype),
        in_specs=[vmem_spec], out_specs=vmem_spec,
        scratch_shapes=[pltpu.VMEM((N_DEV,) + small.shape, small.dtype),
                        pltpu.SemaphoreType.DMA((len(_RELATIONS),)), pltpu.SemaphoreType.DMA((len(_RELATIONS),))],
    )(small)


_HBM = pl.BlockSpec(memory_space=pltpu.HBM)
_SEM = pl.BlockSpec(memory_space=pltpu.SEMAPHORE)
_EFFECT = pltpu.SideEffectType.DATAFLOW_SIDE_EFFECTING


def _early_copies(src_refs, land_refs, send_sems, recv_sems, rows, gather, relations):
    x, y, c = _mesh_pos()
    me = 4 * x + 2 * y + c
    copies = []
    for a in range(len(src_refs)):
        for dx, dy, dc in relations:
            px, py, pc = _flip(x, dx), _flip(y, dy), _flip(c, dc)
            peer = 4 * px + 2 * py + pc
            copies.append(pltpu.make_async_remote_copy(
                src_ref=src_refs[a] if gather else src_refs[a].at[pl.ds(peer * rows[a], rows[a]), :],
                dst_ref=land_refs[a].at[pl.ds(me * rows[a], rows[a]), :],
                send_sem=send_sems[a], recv_sem=recv_sems[a],
                device_id=(px, py, pc), device_id_type=pl.DeviceIdType.MESH))
    return copies


def early_exchange_start(srcs, name, gather=False, after=None, relations=_RELATIONS):
    n = len(srcs)
    if gather:
        rows = [s.shape[0] for s in srcs]
        lands = [lax.empty((N_DEV * r, s.shape[1]), s.dtype) for r, s in zip(rows, srcs)]
    else:
        rows = [s.shape[0] // N_DEV for s in srcs]
        lands = [lax.empty(s.shape, s.dtype) for s in srcs]

    extra = [] if after is None else [after]

    def body(*refs):
        src_refs, land_refs = refs[:n], refs[n:2 * n]
        first_sem = 2 * n + len(extra)
        send_sems, recv_sems = refs[first_sem:first_sem + n], refs[first_sem + n:first_sem + 2 * n]
        token, local_sems = refs[-2], refs[-1]
        for cp in _early_copies(src_refs, land_refs, send_sems, recv_sems, rows, gather, relations):
            cp.start()
        if gather:
            x, y, c = _mesh_pos()
            own = [pltpu.make_async_copy(src_refs[a], land_refs[a].at[pl.ds((4 * x + 2 * y + c) * rows[a], rows[a]), :],
                                         local_sems.at[a]) for a in range(n)]
            for cp in own:
                cp.start()
            for cp in own:
                cp.wait()
        token[...] = jnp.zeros_like(token)

    hbm = lambda a: pltpu.HBM(a.shape, a.dtype)
    outs = pl.pallas_call(
        body, name=name,
        out_shape=[pltpu.SemaphoreType.DMA(())] * (2 * n)
        + [hbm(a) for a in srcs] + [hbm(a) for a in lands] + [jax.ShapeDtypeStruct((8, LANES), F32)],
        in_specs=[_HBM] * (2 * n) + [pl.BlockSpec(memory_space=pl.ANY)] * len(extra),
        out_specs=[_SEM] * (2 * n) + [_HBM] * (2 * n) + [pl.BlockSpec(memory_space=pltpu.VMEM)],
        input_output_aliases={i: 2 * n + i for i in range(2 * n)},
        scratch_shapes=[pltpu.SemaphoreType.DMA((n,))],
        compiler_params=pltpu.CompilerParams(has_side_effects=_EFFECT),
    )(*[pltpu.with_memory_space_constraint(a, pltpu.HBM) for a in list(srcs) + lands], *extra)
    handle = dict(sems=outs[:2 * n], srcs=outs[2 * n:3 * n], lands=outs[3 * n:4 * n], rows=rows,
                  copies=len(relations))
    return handle, outs[-1]


def early_exchange_wait(handle, after, name):
    n = len(handle["srcs"])
    rows = handle["rows"]

    def body(*refs):
        src_refs, land_refs = refs[:n], refs[n:2 * n]
        send_sems, recv_sems = refs[2 * n:3 * n], refs[3 * n:4 * n]
        x, y, c = _mesh_pos()
        for a in range(n):
            span = pl.ds(0, handle["copies"] * rows[a])
            all_copies = pltpu.make_async_remote_copy(
                src_ref=land_refs[a].at[span, :], dst_ref=land_refs[a].at[span, :],
                send_sem=send_sems[a], recv_sem=recv_sems[a],
                device_id=(x, y, c), device_id_type=pl.DeviceIdType.MESH)
            all_copies.wait_send()
            all_copies.wait_recv()

    hbm = lambda a: pltpu.HBM(a.shape, a.dtype)
    ins = list(handle["srcs"]) + list(handle["lands"])
    outs = pl.pallas_call(
        body, name=name,
        out_shape=[hbm(a) for a in ins],
        in_specs=[_HBM] * (2 * n) + [_SEM] * (2 * n) + [pl.BlockSpec(memory_space=pl.ANY)],
        out_specs=[_HBM] * (2 * n),
        input_output_aliases={i: i for i in range(2 * n)},
        compiler_params=pltpu.CompilerParams(has_side_effects=_EFFECT),
    )(*ins, *handle["sems"], after)
    return outs[:n], outs[n:]


def pass_on_to_sibling(lands, rows, name):
    n = len(lands)

    def body(*refs):
        land_refs = refs[n:2 * n]
        send_sems, recv_sems = refs[2 * n:]
        x, y, c = _mesh_pos()
        copies = []
        for a in range(n):
            for k, (dx, dy) in enumerate(_OTHER_CHIPS):
                slot = 4 * _flip(x, dx) + 2 * _flip(y, dy) + c
                blk = land_refs[a].at[pl.ds(slot * rows[a], rows[a]), :]
                copies.append(pltpu.make_async_remote_copy(
                    src_ref=blk, dst_ref=blk, send_sem=send_sems.at[a, k], recv_sem=recv_sems.at[a, k],
                    device_id=(x, y, 1 - c), device_id_type=pl.DeviceIdType.MESH))
        for cp in copies:
            cp.start()
        for cp in copies:
            cp.wait_recv()
        for cp in copies:
            cp.wait_send()

    any_spec = pl.BlockSpec(memory_space=pl.ANY)
    return pl.pallas_call(
        body, name=name,
        out_shape=[jax.ShapeDtypeStruct(a.shape, a.dtype) for a in lands],
        in_specs=[any_spec] * n, out_specs=[any_spec] * n,
        input_output_aliases={i: i for i in range(n)},
        scratch_shapes=[pltpu.SemaphoreType.DMA((n, len(_OTHER_CHIPS))), pltpu.SemaphoreType.DMA((n, len(_OTHER_CHIPS)))],
    )(*lands)


def slot_sum8(src, land, tr, name):
    rows, cols = land.shape[0] // N_DEV, land.shape[1]
    x, y, c = _mesh_pos()
    me = (4 * x + 2 * y + c).astype(jnp.int32).reshape(1)

    def body(me_ref, src_ref, land_ref, o_ref):
        acc = None
        for d in range(N_DEV):
            term = jnp.where(d == me_ref[0], src_ref[0], land_ref[d]).astype(F32)
            acc = term if acc is None else acc + term
        o_ref[...] = acc

    return pl.pallas_call(
        body, name=name,
        grid_spec=pltpu.PrefetchScalarGridSpec(
            num_scalar_prefetch=1, grid=(rows // tr,),
            in_specs=[pl.BlockSpec((1, tr, cols), lambda i, w: (w[0], i, 0)),
                      pl.BlockSpec((N_DEV, tr, cols), lambda i, w: (0, i, 0))],
            out_specs=pl.BlockSpec((tr, cols), lambda i, w: (i, 0))),
        out_shape=jax.ShapeDtypeStruct((rows, cols), F32),
        compiler_params=_params(("arbitrary",)),
    )(me, src.reshape(N_DEV, rows, cols), land.reshape(N_DEV, rows, cols))


def mm_tn_multi(a_t, bs, tt, name, out_dtype=F32):
    K, T = a_t.shape
    widths = [b.shape[1] for b in bs]
    steps = T // tt

    def body(a_ref, *rest):
        b_refs, o_ref, acc = rest[:-2], rest[-2], rest[-1]

        @pl.when(pl.program_id(0) == 0)
        def _():
            acc[...] = jnp.zeros(acc.shape, F32)

        av = a_ref[...]
        col = 0
        for b_ref, w in zip(b_refs, widths):
            acc[:, col:col + w] += jnp.dot(av, b_ref[...], preferred_element_type=F32)
            col += w

        @pl.when(pl.program_id(0) == steps - 1)
        def _():
            o_ref[...] = acc[...].astype(out_dtype)

    return pl.pallas_call(
        body, name=name, grid=(steps,),
        in_specs=[pl.BlockSpec((K, tt), lambda t: (0, t))] + [pl.BlockSpec((tt, w), lambda t: (t, 0)) for w in widths],
        out_specs=pl.BlockSpec((K, sum(widths)), lambda t: (0, 0)),
        out_shape=jax.ShapeDtypeStruct((K, sum(widths)), out_dtype),
        scratch_shapes=[pltpu.VMEM((K, sum(widths)), F32)],
        compiler_params=_params(("arbitrary",)),
    )(a_t, *bs)


def rms_fwd(x, g, tm, name, with_transpose=False):
    M, K = x.shape

    def body(x_ref, g_ref, o_ref, *t_ref):
        xv = x_ref[...]
        r = lax.rsqrt(jnp.mean(xv * xv, axis=-1, keepdims=True) + RMS_EPS)
        h = ((xv * r) * g_ref[...]).astype(BF16)
        o_ref[...] = h
        if with_transpose:
            t_ref[0][...] = h.T

    out_specs = [pl.BlockSpec((tm, K), lambda i: (i, 0))]
    out_shape = [jax.ShapeDtypeStruct((M, K), BF16)]
    if with_transpose:
        out_specs.append(pl.BlockSpec((K, tm), lambda i: (0, i)))
        out_shape.append(jax.ShapeDtypeStruct((K, M), BF16))
    outs = pl.pallas_call(
        body, name=name, grid=(M // tm,),
        in_specs=[pl.BlockSpec((tm, K), lambda i: (i, 0)), pl.BlockSpec((1, K), lambda i: (0, 0))],
        out_specs=out_specs, out_shape=out_shape,
        compiler_params=_params(("arbitrary",)),
    )(x, g)
    return outs if with_transpose else outs[0]


def rms_bwd(x, g, dh, dres, tm, name):
    M, K = x.shape
    has_res = dres is not None

    def body(*refs):
        if has_res:
            x_ref, g_ref, dh_ref, dres_ref, dx_ref, dg_ref = refs
        else:
            x_ref, g_ref, dh_ref, dx_ref, dg_ref = refs
        xv = x_ref[...]
        r = lax.rsqrt(jnp.mean(xv * xv, axis=-1, keepdims=True) + RMS_EPS)
        xn = xv * r
        dhv = dh_ref[...]
        dxn = dhv * g_ref[...]
        dx = r * (dxn - xn * jnp.mean(dxn * xn, axis=-1, keepdims=True))
        if has_res:
            dx = dx + dres_ref[...]
        dx_ref[...] = dx
        part = jnp.sum(dhv * xn, axis=0, keepdims=True)
        row = lax.broadcasted_iota(jnp.int32, (8, K), 0)
        upd = jnp.where(row == 0, part, 0.0)

        @pl.when(pl.program_id(0) == 0)
        def _():
            dg_ref[...] = upd

        @pl.when(pl.program_id(0) != 0)
        def _():
            dg_ref[...] += upd

    row_spec = pl.BlockSpec((tm, K), lambda i: (i, 0))
    ins = [x, g, dh] + ([dres] if has_res else [])
    in_specs = [row_spec, pl.BlockSpec((1, K), lambda i: (0, 0)), row_spec] + ([row_spec] if has_res else [])
    return pl.pallas_call(
        body, name=name, grid=(M // tm,),
        in_specs=in_specs,
        out_specs=[row_spec, pl.BlockSpec((8, K), lambda i: (0, 0))],
        out_shape=[jax.ShapeDtypeStruct((M, K), F32), jax.ShapeDtypeStruct((8, K), F32)],
        compiler_params=_params(("arbitrary",)),
    )(*ins)


def mm_nn(a, b, tm, tn, name, token=None):
    M, K = a.shape
    N = b.shape[1]
    extra = [] if token is None else [token]

    def body(a_ref, b_ref, *rest):
        rest[-1][...] = jnp.dot(a_ref[...], b_ref[...], preferred_element_type=F32)

    return pl.pallas_call(
        body, name=name, grid=(N // tn, M // tm),
        in_specs=[pl.BlockSpec((tm, K), lambda j, i: (i, 0)), pl.BlockSpec((K, tn), lambda j, i: (0, j))]
        + [pl.BlockSpec(t.shape, lambda j, i: (0, 0)) for t in extra],
        out_specs=pl.BlockSpec((tm, tn), lambda j, i: (i, j)),
        out_shape=jax.ShapeDtypeStruct((M, N), F32),
        compiler_params=_params(("arbitrary", "arbitrary")),
    )(a, b, *extra)


def mm_nt(a, b, tm, tk, name):
    M, K = a.shape
    N = b.shape[0]

    def body(a_ref, b_ref, o_ref):
        part = lax.dot_general(a_ref[...], b_ref[...], (((1,), (1,)), ((), ())), preferred_element_type=F32)

        @pl.when(pl.program_id(1) == 0)
        def _():
            o_ref[...] = part

        @pl.when(pl.program_id(1) != 0)
        def _():
            o_ref[...] += part

    return pl.pallas_call(
        body, name=name, grid=(M // tm, K // tk),
        in_specs=[pl.BlockSpec((tm, tk), lambda i, k: (i, k)), pl.BlockSpec((N, tk), lambda i, k: (0, k))],
        out_specs=pl.BlockSpec((tm, N), lambda i, k: (i, 0)),
        out_shape=jax.ShapeDtypeStruct((M, N), F32),
        compiler_params=_params(("arbitrary", "arbitrary")),
    )(a, b)


def in_proj_bwd_rms(pieces, ws, x, g, dres, tm, token):
    M, N = x.shape

    def body(*refs):
        n = len(pieces)
        p_refs, w_refs = refs[:n], refs[n:n + len(ws)]
        x_ref, g_ref, dres_ref, _, dx_ref, dg_ref = refs[n + len(ws):]
        dh = None
        for p_ref, (arr, group, col) in zip(p_refs, pieces):
            part = lax.dot_general(p_ref[...], w_refs[group][:, col:col + arr.shape[1]], (((1,), (1,)), ((), ())),
                                   preferred_element_type=F32)
            dh = part if dh is None else dh + part
        xv = x_ref[...]
        r = lax.rsqrt(jnp.mean(xv * xv, axis=-1, keepdims=True) + RMS_EPS)
        xn = xv * r
        dxn = dh * g_ref[...]
        dx_ref[...] = r * (dxn - xn * jnp.mean(dxn * xn, axis=-1, keepdims=True)) + dres_ref[...]
        row = lax.broadcasted_iota(jnp.int32, (8, N), 0)
        upd = jnp.where(row == 0, jnp.sum(dh * xn, axis=0, keepdims=True), 0.0)

        @pl.when(pl.program_id(0) == 0)
        def _():
            dg_ref[...] = upd

        @pl.when(pl.program_id(0) != 0)
        def _():
            dg_ref[...] += upd

    row_spec = pl.BlockSpec((tm, N), lambda i: (i, 0))
    return pl.pallas_call(
        body, name="in_proj_bwd", grid=(M // tm,),
        in_specs=[pl.BlockSpec((tm, arr.shape[1]), lambda i: (i, 0)) for arr, _, _ in pieces]
        + [pl.BlockSpec(w.shape, lambda i: (0, 0)) for w in ws]
        + [row_spec, pl.BlockSpec((1, N), lambda i: (0, 0)), row_spec, pl.BlockSpec(token.shape, lambda i: (0, 0))],
        out_specs=[row_spec, pl.BlockSpec((8, N), lambda i: (0, 0))],
        out_shape=[jax.ShapeDtypeStruct((M, N), F32), jax.ShapeDtypeStruct((8, N), F32)],
        compiler_params=_params(("arbitrary",)),
    )(*[arr for arr, _, _ in pieces], *ws, x, g, dres, token)


def _log_sigmoid(z):
    return jnp.minimum(z, 0.0) - jnp.log(1.0 + jnp.exp(-jnp.abs(z)))


def _tri(n, lower):
    r = lax.broadcasted_iota(jnp.int32, (n, n), 0)
    c = lax.broadcasted_iota(jnp.int32, (n, n), 1)
    return jnp.where((r >= c) if lower else (r <= c), 1.0, 0.0).astype(F32)


def fox_gate(proj3, b_pad):
    B, S, _ = proj3.shape
    nblk = S // TK

    def body(f_ref, b_ref, o_ref):
        tri = _tri(TK, True)
        carry = jnp.zeros((1, LANES), F32)
        for n in range(nblk):
            z = f_ref[0, n * TK:(n + 1) * TK, :] + b_ref[...]
            logf = _log_sigmoid(z)
            cs = jnp.dot(tri, logf, preferred_element_type=F32, precision=lax.Precision.HIGHEST) + carry
            carry = cs[TK - 1:TK, :]
            o_ref[0, n * TK:(n + 1) * TK, :] = -cs

    return pl.pallas_call(
        body, name="fox_gate", grid=(B,),
        in_specs=[pl.BlockSpec((1, S, LANES), lambda b: (b, 0, A_FLOG // LANES)),
                  pl.BlockSpec((1, LANES), lambda b: (0, 0))],
        out_specs=pl.BlockSpec((1, S, LANES), lambda b: (b, 0, 0)),
        out_shape=jax.ShapeDtypeStruct((B, S, LANES), F32),
        compiler_params=_params(("arbitrary",)),
    )(proj3, b_pad)


def fox_gate_bwd(drow, dneg, proj3, b_pad):
    B, S, _ = proj3.shape
    nblk = S // TK

    def body(d_ref, r_ref, f_ref, b_ref, o_ref, db_ref):
        tri = _tri(TK, False)
        lane = lax.broadcasted_iota(jnp.int32, (TK, LANES), 1)
        carry = jnp.zeros((1, LANES), F32)
        dbsum = jnp.zeros((1, LANES), F32)
        for n in reversed(range(nblk)):
            dk_side = None
            for hp in range(FOX_HEADS // 2):
                two = jnp.where(lane < 2, r_ref[0, n * TK:(n + 1) * TK, hp * LANES:(hp + 1) * LANES], 0.0)
                two = pltpu.roll(two, 2 * hp, 1) if hp else two
                dk_side = two if dk_side is None else dk_side + two
            dc = jnp.where(lane < FOX_HEADS, d_ref[0, :, n * TK:(n + 1) * TK].T - dk_side, 0.0)
            rs = jnp.dot(tri, dc, preferred_element_type=F32, precision=lax.Precision.HIGHEST) + carry
            carry = rs[0:1, :]
            z = f_ref[0, n * TK:(n + 1) * TK, :] + b_ref[...]
            dz = rs * (1.0 / (1.0 + jnp.exp(z)))
            o_ref[0, n * TK:(n + 1) * TK, :] = dz.astype(BF16)
            dbsum = dbsum + jnp.sum(dz, axis=0, keepdims=True)
        row = lax.broadcasted_iota(jnp.int32, (8, LANES), 0)
        upd = jnp.where(row == 0, dbsum, 0.0)

        @pl.when(pl.program_id(0) == 0)
        def _():
            db_ref[...] = upd

        @pl.when(pl.program_id(0) != 0)
        def _():
            db_ref[...] += upd

    return pl.pallas_call(
        body, name="fox_gate_bwd", grid=(B,),
        in_specs=[pl.BlockSpec((1, LANES, S), lambda b: (b, 0, 0)),
                  pl.BlockSpec((1, S, FOX_W), lambda b: (b, 0, 0)),
                  pl.BlockSpec((1, S, LANES), lambda b: (b, 0, A_FLOG // LANES)),
                  pl.BlockSpec((1, LANES), lambda b: (0, 0))],
        out_specs=[pl.BlockSpec((1, S, LANES), lambda b: (b, 0, 0)), pl.BlockSpec((8, LANES), lambda b: (0, 0))],
        out_shape=[jax.ShapeDtypeStruct((B, S, LANES), BF16), jax.ShapeDtypeStruct((8, LANES), F32)],
        compiler_params=_params(("arbitrary",)),
    )(drow, dneg, proj3, b_pad)


def _rope_tables(S):
    half = ROPE_DIM // 2
    f32 = np.float32
    pos = np.arange(S, dtype=f32)
    inv_freq = f32(1.0) / np.power(f32(ROPE_THETA), np.arange(0, ROPE_DIM, 2, dtype=f32) / f32(ROPE_DIM)).astype(f32)
    ang = (pos[:, None] * inv_freq[None, :]).astype(f32).astype(np.float64)
    cos, sin = np.cos(ang).astype(f32), np.sin(ang).astype(f32)
    one = np.ones((S, HEAD_DIM - ROPE_DIM), f32)
    zero = np.zeros((S, HEAD_DIM - ROPE_DIM), f32)
    zh = np.zeros((S, half), f32)
    c = np.concatenate([cos, cos, one], axis=1)
    s1 = np.concatenate([-sin, zh, zero], axis=1)
    s2 = np.concatenate([zh, sin, zero], axis=1)
    return tuple(jnp.asarray(np.concatenate([t, t], axis=1)) for t in (c, s1, s2))


_HALF_ROPE = ROPE_DIM // 2


def _rope(t, c, s1, s2):
    return t * c + pltpu.roll(t, LANES - _HALF_ROPE, 1) * s1 + pltpu.roll(t, _HALF_ROPE, 1) * s2


def _rope_bwd(d, c, s1, s2):
    return d * c + pltpu.roll(d * s1, _HALF_ROPE, 1) + pltpu.roll(d * s2, LANES - _HALF_ROPE, 1)


def _scale_parts(scale):
    m, _ = math.frexp(scale)
    return (scale, None) if m == 0.5 else (None, scale)


def _log_masks(S, kind):
    nd = 1 if kind == "causal" else S // TQ
    a = np.arange(TQ)[:, None]
    b = np.arange(TK)[None, :]
    out = np.zeros((nd, TQ, TK), np.float32)
    for d in range(nd):
        delta = d * TQ + a - b
        if kind == "causal":
            m = (delta >= 0).astype(np.float64)
        else:
            m = sum(((delta >= 0) & (delta % dil == 0) & (delta <= w)).astype(np.float64) for w, dil in DILATIONS)
        out[d] = np.where(m > 0, np.log(np.maximum(m, 1.0)), NEG_INF)
    return jnp.asarray(out)


def _attn_setup(kind):
    pair = kind != "mem"
    e_dim = HEAD_DIM if pair else MEM_HEAD_DIM
    q_fold, s_scale = _scale_parts(1.0 / math.sqrt(e_dim))
    return dict(pair=pair, col0={"fox": A_FOX, "dil": B_DIL, "mem": B_MQ}[kind],
                n_blocks=FOX_HEADS // 2 if pair else MEM_HEADS, q_fold=q_fold, s_scale=s_scale,
                nh=2 if pair else 1)


def _cat(parts, axis):
    return parts[0] if len(parts) == 1 else jnp.concatenate(parts, axis=axis)


def _log_masks_t(S, kind):
    return jnp.swapaxes(_log_masks(S, kind), 1, 2)


def _head_rows(hh, pair):
    row = lax.broadcasted_iota(jnp.int32, (LANES, 1), 0)
    if not pair:
        return row >= 0
    return (row >= HEAD_DIM * hh) & (row < HEAD_DIM * (hh + 1))


def _attn_t_inputs(kind, src, S, negc_cols, mask, rope, kv):
    cfg = _attn_setup(kind)
    col0 = cfg["col0"]
    ins, in_specs = [], []
    if cfg["pair"]:
        ins.append(src)
        in_specs.append(pl.BlockSpec((1, S, PAIR_W), lambda b, h: (b, 0, col0 // PAIR_W + h)))
    else:
        ins += [src, kv, kv]
        in_specs += [pl.BlockSpec((1, S, LANES), lambda b, h: (b, 0, col0 // LANES + h)),
                     pl.BlockSpec((1, MEM_LEN, LANES), lambda b, h: (b, 0, h)),
                     pl.BlockSpec((1, MEM_LEN, LANES), lambda b, h: (b, 0, MEM_HEADS + h))]
    if negc_cols is not None:
        ins.append(negc_cols)
        in_specs.append(pl.BlockSpec((1, S, LANES), lambda b, h: (b, 0, 0)))
    if mask is not None:
        ins.append(mask)
        in_specs.append(pl.BlockSpec(mask.shape, lambda b, h: (0, 0, 0)))
    if rope is not None:
        ins += list(rope)
        in_specs += [pl.BlockSpec((S, LANES), lambda b, h: (0, 0))] * 3
    return ins, in_specs


def _attn_t_prep(cfg, refs, S, Sk, *, qT2s, ks, vs=None, vTs=None, kTs=None, nb=None):
    pair, nh = cfg["pair"], cfg["nh"]
    lane = lax.broadcasted_iota(jnp.int32, (1, LANES), 1)
    rope_refs = refs["rope"]

    def prep_q(n):
        rows = slice(n * TQ, (n + 1) * TQ)
        q = refs["load_q"](rows)
        if rope_refs is not None:
            q = _rope(q, *[t[rows, :] for t in rope_refs])
        if cfg["q_fold"] is not None:
            q = q * cfg["q_fold"]
        qtb = q.astype(BF16).T
        for hh in range(nh):
            qT2s[n, :, hh * TQ:(hh + 1) * TQ] = jnp.where(_head_rows(hh, pair), qtb, jnp.zeros_like(qtb))

    def prep_kv(n):
        rows = slice(n * TK, (n + 1) * TK)
        k, v = refs["load_kv"](rows)
        if rope_refs is not None:
            k = _rope(k, *[t[rows, :] for t in rope_refs])
        kb = k.astype(BF16)
        vb = v.astype(BF16)
        ks[rows, :] = kb
        if vs is not None:
            vs[rows, :] = vb
        if vTs is not None:
            vTs[n] = vb.T
        if kTs is not None:
            kTs[n] = kb.T
        if nb is not None:
            blk = refs["negc"][0, rows, :]
            for hh in range(nh):
                h = 2 * refs["block"] + hh
                col = jnp.sum(jnp.where(lane == h, blk, 0.0), axis=1, keepdims=True)
                nb[hh, rows, :] = jnp.broadcast_to(col, (TK, LANES))

    for n in range(S // TQ):
        prep_q(n)
    for n in range(Sk // TK):
        prep_kv(n)


def _raw_scores_t(cfg, k, qT2):
    sT = jnp.dot(k, qT2, preferred_element_type=F32)
    if cfg["s_scale"] is not None:
        sT = sT * cfg["s_scale"]
    return sT


def _bias_mask_t(cfg, sT, nb, mask_ref, kc, midx):
    nh = cfg["nh"]
    if nb is None and midx is None:
        return sT
    parts = []
    for hh in range(nh):
        t = sT[:, hh * TQ:(hh + 1) * TQ]
        if nb is not None:
            t = t + jnp.concatenate([nb[hh, kc, :]] * (TQ // LANES), axis=1)
        if midx is not None:
            t = t + mask_ref[midx]
        parts.append(t)
    return _cat(parts, 1)


def _tile_pairs(kind, nq, nk):
    if kind == "mem":
        return [(i, j) for i in range(nq) for j in range(nk)], (lambda i, j: None)
    pairs = [(i, j) for i in range(nq) for j in range(i + 1)]
    if kind == "fox":
        return pairs, (lambda i, j: 0 if j == i else None)
    return pairs, (lambda i, j: i - j)


def attn_fwd(kind, src, S, *, negc_cols=None, mask=None, rope=None, kv=None):
    B = src.shape[0]
    cfg = _attn_setup(kind)
    pair, nh = cfg["pair"], cfg["nh"]
    Sk = S if pair else MEM_LEN
    has_bias, has_rope = negc_cols is not None, rope is not None
    R = nh * TQ
    nq, nk = S // TQ, Sk // TK
    pairs, mask_index = _tile_pairs(kind, nq, nk)

    def body(*refs):
        refs = list(refs)
        if pair:
            qkv_ref = refs.pop(0)
            load_q = lambda rows: qkv_ref[0, rows, 0:LANES]
            load_kv = lambda rows: (qkv_ref[0, rows, LANES:2 * LANES], qkv_ref[0, rows, 2 * LANES:3 * LANES])
        else:
            q_ref, k_ref, v_ref = refs.pop(0), refs.pop(0), refs.pop(0)
            load_q = lambda rows: q_ref[0, rows, :]
            load_kv = lambda rows: (k_ref[0, rows, :], v_ref[0, rows, :])
        negc_ref = refs.pop(0) if has_bias else None
        mask_ref = refs.pop(0) if mask is not None else None
        rope_refs = [refs.pop(0) for _ in range(3)] if has_rope else None
        o_ref, lse_ref, qT2s, ks, vTs, s_a, s_b, p_a, p_b = refs[:9]
        nb = refs[9] if has_bias else None
        _attn_t_prep(cfg, dict(load_q=load_q, load_kv=load_kv, rope=rope_refs, negc=negc_ref,
                               block=pl.program_id(1)), S, Sk, qT2s=qT2s, ks=ks, vTs=vTs, nb=nb)

        def cols(j):
            return slice(j * TK, (j + 1) * TK)

        def scores(i, j):
            return _raw_scores_t(cfg, ks[cols(j), :], qT2s[i])

        def finish(i, m, l, accT):
            oT2 = accT / l
            oT = jnp.where(_head_rows(0, True), oT2[:, 0:TQ], oT2[:, TQ:2 * TQ]) if pair else oT2
            o_ref[0, i * TQ:(i + 1) * TQ, :] = oT.T
            lse_ref[0, 0, i:i + 1, :] = m + jnp.log(l)

        s_bufs, p_bufs = (s_a, s_b), (p_a, p_b)
        s_bufs[0][...] = scores(*pairs[0])
        m = l = accT = None
        for t, (i, j) in enumerate(pairs):
            cur, oth = t % 2, 1 - t % 2
            if t > 0:
                i_prev, j_prev = pairs[t - 1]
                pv = jnp.dot(vTs[j_prev], p_bufs[oth][...], preferred_element_type=F32)
                acc_full = pv if accT is None else accT + pv
            if t + 1 < len(pairs):
                s_bufs[oth][...] = scores(*pairs[t + 1])
            first = j == 0
            if first and t > 0:
                finish(i_prev, m, l, acc_full)
            sT = _bias_mask_t(cfg, s_bufs[cur][...], nb, mask_ref, cols(j), mask_index(i, j))
            m_tile = jnp.max(sT, axis=0, keepdims=True)
            m_new = m_tile if first else jnp.maximum(m, m_tile)
            p = jnp.exp(sT - m_new)
            p_bufs[cur][...] = p.astype(BF16)
            if first:
                l, accT = jnp.sum(p, axis=0, keepdims=True), None
            else:
                alpha = jnp.exp(m - m_new)
                l, accT = alpha * l + jnp.sum(p, axis=0, keepdims=True), acc_full * alpha
            m = m_new
        i_last, j_last = pairs[-1]
        pv = jnp.dot(vTs[j_last], p_bufs[(len(pairs) - 1) % 2][...], preferred_element_type=F32)
        finish(i_last, m, l, pv if accT is None else accT + pv)

    ins, in_specs = _attn_t_inputs(kind, src, S, negc_cols, mask, rope, kv)
    W = cfg["n_blocks"] * LANES
    scratch = [pltpu.VMEM((nq, LANES, R), BF16), pltpu.VMEM((Sk, LANES), BF16), pltpu.VMEM((nk, LANES, TK), BF16),
               pltpu.VMEM((TK, R), F32), pltpu.VMEM((TK, R), F32), pltpu.VMEM((TK, R), BF16), pltpu.VMEM((TK, R), BF16)]
    if has_bias:
        scratch.append(pltpu.VMEM((nh, Sk, LANES), F32))
    return pl.pallas_call(
        body, name=kind + "_attn_fwd", grid=(B, cfg["n_blocks"]),
        in_specs=in_specs,
        out_specs=[pl.BlockSpec((1, S, LANES), lambda b, h: (b, 0, h)),
                   pl.BlockSpec((1, 1, nq, R), lambda b, h: (b, h, 0, 0))],
        out_shape=[jax.ShapeDtypeStruct((B, S, W), F32), jax.ShapeDtypeStruct((B, cfg["n_blocks"], nq, R), F32)],
        scratch_shapes=scratch,
        compiler_params=_params(("arbitrary", "arbitrary")),
    )(*ins)


def attn_bwd(kind, src, do, o, lse, S, *, negc_cols=None, mask=None, rope=None, kv=None, token=None):
    B = src.shape[0]
    cfg = _attn_setup(kind)
    pair, nh, s_scale, q_fold = cfg["pair"], cfg["nh"], cfg["s_scale"], cfg["q_fold"]
    Sk = S if pair else MEM_LEN
    has_bias, has_rope = negc_cols is not None, rope is not None
    R = nh * TQ
    nq, nk = S // TQ, Sk // TK
    pairs, mask_index = _tile_pairs(kind, nq, nk)

    def body(*refs):
        refs = list(refs)
        if pair:
            qkv_ref = refs.pop(0)
            load_q = lambda rows: qkv_ref[0, rows, 0:LANES]
            load_kv = lambda rows: (qkv_ref[0, rows, LANES:2 * LANES], qkv_ref[0, rows, 2 * LANES:3 * LANES])
        else:
            q_ref, k_ref, v_ref = refs.pop(0), refs.pop(0), refs.pop(0)
            load_q = lambda rows: q_ref[0, rows, :]
            load_kv = lambda rows: (k_ref[0, rows, :], v_ref[0, rows, :])
        negc_ref = refs.pop(0) if has_bias else None
        mask_ref = refs.pop(0) if mask is not None else None
        rope_refs = [refs.pop(0) for _ in range(3)] if has_rope else None
        do_ref, o_ref, lse_ref = refs.pop(0), refs.pop(0), refs.pop(0)
        if token is not None:
            refs.pop(0)
        if pair:
            dqkv_ref = refs.pop(0)
            dneg_ref = refs.pop(0) if has_bias else None
            drow_ref = refs.pop(0) if has_bias else None
        else:
            dq_ref, dk_ref, dv_ref = refs.pop(0), refs.pop(0), refs.pop(0)
        qT2s, ks, vs, kTs, doT2s, delta_s, dk_acc, dv_acc = refs[:8]
        bufs_a, bufs_b = refs[8:12], refs[12:16]
        nb, dneg_acc = (refs[16], refs[17]) if has_bias else (None, None)
        lane = lax.broadcasted_iota(jnp.int32, (1, LANES), 1)
        _attn_t_prep(cfg, dict(load_q=load_q, load_kv=load_kv, rope=rope_refs, negc=negc_ref,
                               block=pl.program_id(1)), S, Sk,
                     qT2s=qT2s, ks=ks, vs=vs, kTs=kTs, nb=nb)

        def prep_do(n):
            rows = slice(n * TQ, (n + 1) * TQ)
            doT = do_ref[0, rows, :].astype(BF16).astype(F32).T
            prodT = doT * o_ref[0, rows, :].T
            doTb = doT.astype(BF16)
            for hh in range(nh):
                hm = _head_rows(hh, pair)
                doT2s[n, :, hh * TQ:(hh + 1) * TQ] = jnp.where(hm, doTb, jnp.zeros_like(doTb))
                delta_s[n:n + 1, hh * TQ:(hh + 1) * TQ] = jnp.sum(jnp.where(hm, prodT, 0.0), axis=0, keepdims=True)

        for n in range(nq):
            prep_do(n)
        dk_acc[...] = jnp.zeros(dk_acc.shape, F32)
        dv_acc[...] = jnp.zeros(dv_acc.shape, F32)
        if has_bias:
            dneg_acc[...] = jnp.zeros(dneg_acc.shape, F32)

        def cols(j):
            return slice(j * TK, (j + 1) * TK)

        nt_dims = (((1,), (1,)), ((), ()))

        def first_products(i, j, bufs):
            bufs[0][...] = _raw_scores_t(cfg, ks[cols(j), :], qT2s[i])
            bufs[1][...] = jnp.dot(vs[cols(j), :], doT2s[i], preferred_element_type=F32)

        def last_products(i, j, bufs, dqT2):
            dv_acc[j] += lax.dot_general(doT2s[i], bufs[2][...], nt_dims, preferred_element_type=F32)
            dk_acc[j] += lax.dot_general(qT2s[i], bufs[3][...], nt_dims, preferred_element_type=F32)
            dq = jnp.dot(kTs[j], bufs[3][...], preferred_element_type=F32)
            return dq if dqT2 is None else dqT2 + dq

        def finish_q(i, dqT2, drow):
            rows = slice(i * TQ, (i + 1) * TQ)
            dqT = jnp.where(_head_rows(0, True), dqT2[:, 0:TQ], dqT2[:, TQ:2 * TQ]) if pair else dqT2
            dq = dqT.T
            if q_fold is not None:
                dq = dq * q_fold
            if has_rope:
                dq = _rope_bwd(dq, *[t[rows, :] for t in rope_refs])
            if pair:
                dqkv_ref[0, rows, 0:LANES] = dq.astype(BF16)
            else:
                dq_ref[0, rows, :] = dq.astype(BF16)
            if has_bias:
                drow_ref[0, 0, i:i + 1, :] = drow

        bufs = (bufs_a, bufs_b)
        first_products(*pairs[0], bufs[0])
        dqT2 = drow = None
        for t, (i, j) in enumerate(pairs):
            cur, oth = bufs[t % 2], bufs[1 - t % 2]
            first = j == 0
            if first and t > 0:
                i_prev, j_prev = pairs[t - 1]
                finish_q(i_prev, last_products(i_prev, j_prev, oth, dqT2), drow)
                dqT2 = drow = None
            sT = _bias_mask_t(cfg, cur[0][...], nb, mask_ref, cols(j), mask_index(i, j))
            pT = jnp.exp(sT - lse_ref[0, 0, i:i + 1, :])
            dsT = pT * (cur[1][...] - delta_s[i:i + 1, :])
            if has_bias:
                tile_rows = jnp.sum(dsT, axis=0, keepdims=True)
                drow = tile_rows if drow is None else drow + tile_rows
                for hh in range(nh):
                    part = dsT[:, hh * TQ:hh * TQ + LANES]
                    for u in range(1, TQ // LANES):
                        part = part + dsT[:, hh * TQ + u * LANES:hh * TQ + (u + 1) * LANES]
                    dneg_acc[hh, cols(j), :] += part
            if s_scale is not None:
                dsT = dsT * s_scale
            cur[2][...] = pT.astype(BF16)
            cur[3][...] = dsT.astype(BF16)
            if not first:
                dqT2 = last_products(*pairs[t - 1], oth, dqT2)
            if t + 1 < len(pairs):
                first_products(*pairs[t + 1], oth)
        i_last, j_last = pairs[-1]
        finish_q(i_last, last_products(i_last, j_last, bufs[(len(pairs) - 1) % 2], dqT2), drow)

        for n in range(nk):
            rows = slice(n * TK, (n + 1) * TK)
            dk = dk_acc[n].T
            dv = dv_acc[n].T
            if has_rope:
                dk = _rope_bwd(dk, *[t[rows, :] for t in rope_refs])
            if pair:
                dqkv_ref[0, rows, LANES:2 * LANES] = dk.astype(BF16)
                dqkv_ref[0, rows, 2 * LANES:3 * LANES] = dv.astype(BF16)
            else:
                dk_ref[0, rows, :] = dk.astype(BF16)
                dv_ref[0, rows, :] = dv.astype(BF16)
            if has_bias:
                x0 = jnp.sum(dneg_acc[0, rows, :], axis=1, keepdims=True)
                x1 = jnp.sum(dneg_acc[1, rows, :], axis=1, keepdims=True)
                dneg_ref[0, rows, :] = jnp.where(lane == 0, x0, jnp.where(lane == 1, x1, 0.0))

    ins, in_specs = _attn_t_inputs(kind, src, S, negc_cols, mask, rope, kv)
    row_spec = pl.BlockSpec((1, S, LANES), lambda b, h: (b, 0, h))
    vec_spec = pl.BlockSpec((1, 1, nq, R), lambda b, h: (b, h, 0, 0))
    ins += [do, o, lse]
    in_specs += [row_spec, row_spec, vec_spec]
    if token is not None:
        ins.append(token)
        in_specs.append(pl.BlockSpec(token.shape, lambda b, h: (0, 0)))
    W = cfg["n_blocks"] * LANES
    if pair:
        out_specs = [pl.BlockSpec((1, S, PAIR_W), lambda b, h: (b, 0, h))]
        out_shape = [jax.ShapeDtypeStruct((B, S, 3 * W), BF16)]
        if has_bias:
            out_specs += [row_spec, vec_spec]
            out_shape += [jax.ShapeDtypeStruct((B, S, W), F32), jax.ShapeDtypeStruct((B, cfg["n_blocks"], nq, R), F32)]
    else:
        kv_spec = pl.BlockSpec((1, MEM_LEN, LANES), lambda b, h: (b, 0, h))
        out_specs = [row_spec, kv_spec, kv_spec]
        out_shape = [jax.ShapeDtypeStruct((B, S, W), BF16)] + [jax.ShapeDtypeStruct((B, MEM_LEN, W), BF16)] * 2
    scratch = [pltpu.VMEM((nq, LANES, R), BF16), pltpu.VMEM((Sk, LANES), BF16),
               pltpu.VMEM((Sk, LANES), BF16), pltpu.VMEM((nk, LANES, TK), BF16), pltpu.VMEM((nq, LANES, R), BF16),
               pltpu.VMEM((nq, R), F32), pltpu.VMEM((nk, LANES, TK), F32), pltpu.VMEM((nk, LANES, TK), F32)]
    pair_bufs = [pltpu.VMEM((TK, R), F32), pltpu.VMEM((TK, R), F32), pltpu.VMEM((TK, R), BF16), pltpu.VMEM((TK, R), BF16)]
    scratch += pair_bufs + pair_bufs
    if has_bias:
        scratch += [pltpu.VMEM((nh, Sk, LANES), F32), pltpu.VMEM((nh, Sk, LANES), F32)]
    return pl.pallas_call(
        body, name=kind + "_attn_bwd", grid=(B, cfg["n_blocks"]),
        in_specs=in_specs, out_specs=out_specs, out_shape=out_shape, scratch_shapes=scratch,
        compiler_params=_params(("arbitrary", "arbitrary")),
    )(*ins)


def _sigmoid(g):
    return 1.0 / (1.0 + jnp.exp(-g))


def out_step(proj, o_fox, o_dil, o_mem, w_out, x, target, gf, tm):
    T = x.shape[0]

    def body(fg_ref, dg_ref, mg_ref, of_ref, od_ref, om_ref, w_ref, x_ref, t_ref, gf_ref,
             dx_ref, dof_ref, dod_ref, dom_ref, dfg_ref, ddg_ref, dmg_ref, gw_ref, sm_ref, gw_acc):
        branches = []
        for g_ref, o_ref in ((fg_ref, of_ref), (dg_ref, od_ref), (mg_ref, om_ref)):
            g = g_ref[...]
            sg = _sigmoid(g)
            o = o_ref[...]
            branches.append((g, sg, o))
        ymix = jnp.concatenate([(o * (g * sg)).astype(BF16) for g, sg, o in branches], axis=1)
        x2 = x_ref[...] + jnp.dot(ymix, w_ref[...], preferred_element_type=F32)
        r = lax.rsqrt(jnp.mean(x2 * x2, axis=-1, keepdims=True) + RMS_EPS)
        yn = x2 * r
        err = yn * gf_ref[...] - t_ref[...]
        loss = 0.5 * jnp.sum(jnp.sum(err * err, axis=-1, keepdims=True) / D_MODEL, axis=0, keepdims=True)
        dyf = err / D_MODEL
        dgf = jnp.sum(dyf * yn, axis=0, keepdims=True)
        dyn = dyf * gf_ref[...]
        dx2 = r * (dyn - yn * jnp.mean(dyn * yn, axis=-1, keepdims=True))
        dx_ref[...] = dx2
        dxb = dx2.astype(BF16)
        dmix = lax.dot_general(dxb, w_ref[...], (((1,), (1,)), ((), ())), preferred_element_type=F32)
        col = 0
        for (g, sg, o), do_ref, dgate_ref in zip(branches, (dof_ref, dod_ref, dom_ref), (dfg_ref, ddg_ref, dmg_ref)):
            d = dmix[:, col:col + g.shape[1]]
            col += g.shape[1]
            do_ref[...] = (d * (g * sg)).astype(BF16)
            dgate_ref[...] = (d * o * (sg * (1.0 + g * (1.0 - sg)))).astype(BF16)
        row = lax.broadcasted_iota(jnp.int32, (8, D_MODEL), 0)
        upd = jnp.where(row == 0, dgf, jnp.where(row == 1, loss, 0.0))

        @pl.when(pl.program_id(0) == 0)
        def _():
            sm_ref[...] = jnp.zeros(sm_ref.shape, F32)
            gw_acc[...] = jnp.zeros(gw_acc.shape, F32)

        sm_ref[...] += upd
        gw_acc[...] += lax.dot_general(ymix, dxb, (((0,), (0,)), ((), ())), preferred_element_type=F32)

        @pl.when(pl.program_id(0) == T // tm - 1)
        def _():
            gw_ref[...] = gw_acc[...].astype(BF16)

    def rows(w, col=0):
        return pl.BlockSpec((tm, w), lambda i: (i, col))

    return pl.pallas_call(
        body, name="out_step", grid=(T // tm,),
        in_specs=[rows(FOX_W, B_FG // FOX_W), rows(DIL_W, B_DG // DIL_W), rows(MEM_W, B_MG // MEM_W),
                  rows(FOX_W), rows(DIL_W), rows(MEM_W),
                  pl.BlockSpec((MIX_W, D_MODEL), lambda i: (0, 0)),
                  rows(D_MODEL), rows(D_MODEL), pl.BlockSpec((1, D_MODEL), lambda i: (0, 0))],
        out_specs=[rows(D_MODEL), rows(FOX_W), rows(DIL_W), rows(MEM_W), rows(FOX_W), rows(DIL_W), rows(MEM_W),
                   pl.BlockSpec((MIX_W, D_MODEL), lambda i: (0, 0)), pl.BlockSpec((8, D_MODEL), lambda i: (0, 0))],
        out_shape=[jax.ShapeDtypeStruct((T, D_MODEL), F32), jax.ShapeDtypeStruct((T, FOX_W), BF16),
                   jax.ShapeDtypeStruct((T, DIL_W), BF16), jax.ShapeDtypeStruct((T, MEM_W), BF16),
                   jax.ShapeDtypeStruct((T, FOX_W), BF16), jax.ShapeDtypeStruct((T, DIL_W), BF16),
                   jax.ShapeDtypeStruct((T, MEM_W), BF16), jax.ShapeDtypeStruct((MIX_W, D_MODEL), BF16),
                   jax.ShapeDtypeStruct((8, D_MODEL), F32)],
        scratch_shapes=[pltpu.VMEM((MIX_W, D_MODEL), F32)],
        compiler_params=_params(("arbitrary",)),
    )(proj, proj, proj, o_fox, o_dil, o_mem, w_out, x, target, gf)


def adamw(w, g, m, v, tr, name):
    lead = w.shape[:-2]
    R, C = w.shape[-2:]
    zeros = (0,) * len(lead)

    def body(w_ref, g_ref, m_ref, v_ref, d_ref, mo_ref, vo_ref):
        gv = g_ref[...]
        mn = ADAM_B1 * m_ref[...] + (1.0 - ADAM_B1) * gv
        vn = ADAM_B2 * v_ref[...] + (1.0 - ADAM_B2) * jnp.square(gv)
        m_hat = mn / (1.0 - ADAM_B1 ** ADAM_STEP)
        v_hat = vn / (1.0 - ADAM_B2 ** ADAM_STEP)
        d_ref[...] = -ADAM_LR * (m_hat / (jnp.sqrt(v_hat) + ADAM_EPS) + ADAM_WD * w_ref[...])
        mo_ref[...] = mn
        vo_ref[...] = vn

    spec = pl.BlockSpec((1,) * len(lead) + (tr, C), lambda i: zeros + (i, 0))
    return pl.pallas_call(
        body, name=name, grid=(pl.cdiv(R, tr),),
        in_specs=[spec] * 4, out_specs=[spec] * 3,
        out_shape=[jax.ShapeDtypeStruct(w.shape, F32)] * 3,
        compiler_params=_params(("arbitrary",)),
    )(w, g, m, v)


def _pad_row(v, width):
    return jnp.concatenate([v, jnp.zeros((1, width - v.shape[1]), v.dtype)], axis=1)


def local_grads(x, mem, norm_g, b_forget, mem_norm_g, final_norm_g, loss_target, w_in_a, first_token, late_weights,
                start_exchange):
    B, S, D = x.shape
    T = B * S
    xt = x.reshape(T, D)
    memt = mem.reshape(B * MEM_LEN, D)
    b_pad = _pad_row(b_forget, LANES)

    h, h_t = rms_fwd(xt, norm_g, 512, "rms_x", with_transpose=True)
    proj_a = mm_nn(h, w_in_a, 512, PA, "in_proj_a", first_token)
    proj_a3 = proj_a.reshape(B, S, PA)

    negc = fox_gate(proj_a3, b_pad)
    causal = _log_masks_t(S, "causal")
    dilated = _log_masks_t(S, "dilated")
    rope = _rope_tables(S)

    o_fox, lse_fox = attn_fwd("fox", proj_a3, S, negc_cols=negc, mask=causal)

    w_in_b, w_kv, w_out = late_weights(o_fox)
    proj_b = mm_nn(h, w_in_b, 512, PB // 2, "in_proj_b")
    proj_b3 = proj_b.reshape(B, S, PB)
    o_dil, lse_dil = attn_fwd("dil", proj_b3, S, mask=dilated, rope=rope)

    mh, mh_t = rms_fwd(memt, mem_norm_g, B * MEM_LEN, "rms_mem", with_transpose=True)
    mkv = mm_nn(mh, w_kv, B * MEM_LEN, 2 * MEM_W, "mem_kv_proj")
    mkv3 = mkv.reshape(B, MEM_LEN, 2 * MEM_W)
    o_mem, lse_mem = attn_fwd("mem", proj_b3, S, kv=mkv3)

    dx2, do_fox, do_dil, do_mem, dfg, ddg, dmg, g_out, small_out = out_step(
        proj_b, o_fox.reshape(T, FOX_W), o_dil.reshape(T, DIL_W), o_mem.reshape(T, MEM_W), w_out,
        xt, loss_target.reshape(T, D), final_norm_g.reshape(1, D), 256)

    gates = [(dfg, 1, B_FG), (ddg, 1, B_DG), (dmg, 1, B_MG)]
    g_gates = mm_tn_multi(h_t, [piece[0] for piece in gates], 1024, "w_in_grad_gates", BF16)
    first, token = start_exchange([g_gates, g_out], "early_exchange_a")

    dqkv_fox, dneg, drow = attn_bwd("fox", proj_a3, do_fox.reshape(B, S, FOX_W), o_fox, lse_fox, S,
                                    negc_cols=negc, mask=causal, token=token)
    drow = drow.reshape(B, FOX_HEADS // 2, S // TQ, 2, TQ).transpose(0, 1, 3, 2, 4).reshape(B, FOX_HEADS, S)
    drow = jnp.pad(drow, ((0, 0), (0, LANES - FOX_HEADS), (0, 0)))
    dflog, db_part = fox_gate_bwd(drow, dneg, proj_a3, b_pad)
    fox = [(dqkv_fox.reshape(T, 3 * FOX_W), 0, A_FOX), (dflog.reshape(T, LANES), 0, A_FLOG)]
    g_fox = mm_tn_multi(h_t, [piece[0] for piece in fox], 1024, "w_in_grad_fox", BF16)
    second, token = start_exchange([g_fox], "early_exchange_b")

    (dqkv_dil,) = attn_bwd("dil", proj_b3, do_dil.reshape(B, S, DIL_W), o_dil, lse_dil, S, mask=dilated, rope=rope,
                           token=token)
    dil = [(dqkv_dil.reshape(T, 3 * DIL_W), 1, B_DIL)]
    g_dil = mm_tn_multi(h_t, [piece[0] for piece in dil], 1024, "w_in_grad_dil", BF16)
    third, token = start_exchange([g_dil], "early_exchange_c")

    dmq, dmk, dmv = attn_bwd("mem", proj_b3, do_mem.reshape(B, S, MEM_W), o_mem, lse_mem, S, kv=mkv3, token=token)
    mq = [(dmq.reshape(T, MEM_W), 1, B_MQ)]
    g_mq = mm_tn_multi(h_t, [piece[0] for piece in mq], 1024, "w_in_grad_mq", BF16)
    dmkv = jnp.concatenate([dmk, dmv], axis=2).reshape(B * MEM_LEN, 2 * MEM_W)
    g_kv = mm_tn_multi(mh_t, [dmkv], B * MEM_LEN, "w_kv_grad", BF16)
    fourth, token = start_exchange([g_mq, g_kv], "early_exchange_d")

    grad_x, dng = in_proj_bwd_rms(gates + fox + dil + mq, (w_in_a, w_in_b), xt, norm_g, dx2, 256, token)
    dmh = mm_nt(dmkv, w_kv, B * MEM_LEN, D, "mem_kv_bwd")
    _, dmng = rms_bwd(memt, mem_norm_g, dmh, None, B * MEM_LEN, "rms_mem_bwd")

    small = jnp.concatenate([dng[0:1], dmng[0:1], small_out[0:1], _pad_row(db_part[0:1], D), small_out[1:2],
                             jnp.zeros((3, D), F32)], axis=0)
    early = [(first, dqkv_fox), (second, dqkv_dil), (third, dmq), (fourth, grad_x)]
    return grad_x.reshape(B, S, D), early, small


def kernel(x, mem, norm_g, w_in, b_forget, mem_norm_g, w_mem_kv, w_out, final_norm_g, loss_target, m_norm_g, m_w_in, m_b_forget, m_mem_norm_g, m_w_mem_kv, m_w_out, m_final_norm_g, v_norm_g, v_w_in, v_b_forget, v_mem_norm_g, v_w_mem_kv, v_w_out, v_final_norm_g):
    D = D_MODEL
    shard_a, shard_b = _split_cols(_pack_cols(w_in).astype(BF16).reshape(w_in.shape[1], PW))
    (w_in_a,) = weight_gather([shard_a])
    gather, gather_token = early_exchange_start(
        [shard_b, w_mem_kv[0].astype(BF16), w_out[0].astype(BF16)], "late_gather", gather=True, after=w_in_a,
        relations=_SIBLING_AND_SAME_CORES)

    def late_weights(after):
        _, gathered = early_exchange_wait(gather, after, "late_gather_wait")
        return pass_on_to_sibling(gathered, gather["rows"], "late_gather_pass")

    grad_x, early, small = local_grads(
        x, mem, norm_g, b_forget, mem_norm_g, final_norm_g, loss_target, w_in_a, gather_token, late_weights,
        early_exchange_start)

    (first, after_first), (second, after_second), (third, after_third), (fourth, after_fourth) = early
    (src_gates, src_out), (land_gates, land_out) = early_exchange_wait(first, after_first, "early_wait_a")
    (src_fox,), (land_fox,) = early_exchange_wait(second, after_second, "early_wait_b")
    (src_dil,), (land_dil,) = early_exchange_wait(third, after_third, "early_wait_c")
    (src_mq, src_kv), (land_mq, land_kv) = early_exchange_wait(fourth, after_fourth, "early_wait_d")
    gates = slot_sum8(src_gates, land_gates, 128, "sum_w_in_gates")
    gw_out = slot_sum8(src_out, land_out, 256, "sum_w_out")
    fox = slot_sum8(src_fox, land_fox, 128, "sum_w_in_fox")
    dil = slot_sum8(src_dil, land_dil, 128, "sum_w_in_dil")
    mq = slot_sum8(src_mq, land_mq, 128, "sum_w_in_mq")
    gw_kv = slot_sum8(src_kv, land_kv, 128, "sum_w_kv")

    tot = small_all_reduce(small)
    gw_in = _unpack_cols(jnp.concatenate(
        [fox[:, :3 * FOX_W], gates[:, :FOX_W], dil, gates[:, FOX_W:FOX_W + DIL_W], mq,
         gates[:, FOX_W + DIL_W:], fox[:, 3 * FOX_W:]], axis=1)[None])

    loss = tot[4, 0]
    g_norm, g_mem_norm, g_final, g_b = tot[0:1], tot[1:2], tot[2], tot[3:4, :FOX_HEADS]

    def rows8(*rows):
        rows = [r.reshape(1, -1) for r in rows]
        rows = [_pad_row(r, D) for r in rows]
        return jnp.concatenate(rows + [jnp.zeros((8 - len(rows), D), F32)], axis=0)

    sw = rows8(norm_g, mem_norm_g, final_norm_g, b_forget)
    sm = rows8(m_norm_g, m_mem_norm_g, m_final_norm_g, m_b_forget)
    sv = rows8(v_norm_g, v_mem_norm_g, v_final_norm_g, v_b_forget)
    d_s, m_s, v_s = adamw(sw, tot, sm, sv, 8, "adamw_small")
    d_in, m_in, v_in = adamw(w_in, gw_in, m_w_in, v_w_in, 32, "adamw_w_in")
    d_kv, m_kv, v_kv = adamw(w_mem_kv[0], gw_kv, m_w_mem_kv[0], v_w_mem_kv[0], 128, "adamw_w_kv")
    d_out, m_out, v_out = adamw(w_out[0], gw_out, m_w_out[0], v_w_out[0], 256, "adamw_w_out")

    def small_outs(t):
        return t[0:1], t[3:4, :FOX_HEADS], t[1:2], t[2]

    grads = (g_norm, gw_in, g_b, g_mem_norm, gw_kv[None], gw_out[None], g_final)
    outs = []
    for t, big in ((d_s, (d_in, d_kv, d_out)), (m_s, (m_in, m_kv, m_out)), (v_s, (v_in, v_kv, v_out))):
        n, b, mn, f = small_outs(t)
        outs += [n, big[0], b, mn, big[1][None], big[2][None], f]
    return (loss, grad_x, *grads, *outs)
```

```python
import math

import numpy as np
import jax
import jax.numpy as jnp
from jax import lax
from jax.experimental import pallas as pl
from jax.experimental.pallas import tpu as pltpu

F32 = jnp.float32
BF16 = jnp.bfloat16

D_MODEL = 1024
HEAD_DIM = 64
FOX_HEADS = 12
DIL_HEADS = 12
MEM_HEADS = 4
MEM_HEAD_DIM = 128
MEM_LEN = 256
FOX_W = FOX_HEADS * HEAD_DIM
DIL_W = DIL_HEADS * HEAD_DIM
MEM_W = MEM_HEADS * MEM_HEAD_DIM
MIX_W = FOX_W + DIL_W + MEM_W
DILATIONS = ((128, 1), (512, 4), (2048, 16))
ROPE_THETA = 500000.0
ROPE_DIM = HEAD_DIM // 4
RMS_EPS = 1e-6
NEG_INF = -1e30
IN_W = 4 * FOX_W + FOX_HEADS + 4 * DIL_W + 2 * MEM_W

ADAM_LR = 0.001
ADAM_B1 = 0.9
ADAM_B2 = 0.999
ADAM_EPS = 1e-08
ADAM_WD = 0.01
ADAM_STEP = 10

N_DEV = 8
LANES = 128
PAIR_W = 3 * LANES
TQ = 256
TK = 256

O_FQ, O_FK, O_FV, O_FG = 0, FOX_W, 2 * FOX_W, 3 * FOX_W
O_FLOG = 4 * FOX_W
O_DQ = O_FLOG + FOX_HEADS
O_DK, O_DV, O_DG = O_DQ + DIL_W, O_DQ + 2 * DIL_W, O_DQ + 3 * DIL_W
O_MQ = O_DQ + 4 * DIL_W
O_MG = O_MQ + MEM_W
P_FOX = 0
P_FG = P_FOX + 3 * FOX_W
P_DIL = P_FG + FOX_W
P_DG = P_DIL + 3 * DIL_W
P_MQ = P_DG + DIL_W
P_MG = P_MQ + MEM_W
P_FLOG = P_MG + MEM_W
PW = P_FLOG + LANES
A_FOX = 0
A_FLOG = A_FOX + 3 * FOX_W
PA = A_FLOG + LANES
B_FG = 0
B_DG = B_FG + FOX_W
B_DIL = B_DG + DIL_W
B_MQ = B_DIL + 3 * DIL_W
B_MG = -(-(B_MQ + MEM_W) // MEM_W) * MEM_W
PB = B_MG + MEM_W

VMEM_LIMIT = 56 * 1024 * 1024


def _pack_pieces():
    pieces = []
    for base in (O_FQ, O_DQ):
        seg = []
        for hp in range(FOX_HEADS // 2):
            for part in range(3):
                seg.append((base + part * FOX_W + hp * LANES, LANES))
        pieces.append(seg)
    fox, dil = pieces
    return fox + [(O_FG, FOX_W)] + dil + [(O_DG, DIL_W), (O_MQ, MEM_W), (O_MG, MEM_W), (O_FLOG, FOX_HEADS)]


def _pack_cols(w):
    parts = [w[..., s:s + n] for s, n in _pack_pieces()]
    parts.append(jnp.zeros(w.shape[:-1] + (LANES - FOX_HEADS,), w.dtype))
    return jnp.concatenate(parts, axis=-1)


def _split_cols(wp):
    def cut(start, width):
        return wp[..., start:start + width]

    group_a = jnp.concatenate([cut(P_FOX, 3 * FOX_W), cut(P_FLOG, LANES)], axis=-1)
    pad = jnp.zeros(wp.shape[:-1] + (B_MG - B_MQ - MEM_W,), wp.dtype)
    group_b = jnp.concatenate([cut(P_FG, FOX_W), cut(P_DG, DIL_W), cut(P_DIL, 3 * DIL_W), cut(P_MQ, MEM_W), pad,
                               cut(P_MG, MEM_W)], axis=-1)
    return group_a, group_b


def _unpack_cols(g):
    runs = []
    pos = 0
    for s, n in _pack_pieces():
        runs.append((s, n, pos))
        pos += n
    runs.sort()
    return jnp.concatenate([g[..., p:p + n] for s, n, p in runs], axis=-1)


def _params(sem=None, **kw):
    return pltpu.CompilerParams(dimension_semantics=sem, vmem_limit_bytes=VMEM_LIMIT, **kw)


def _mesh_pos():
    return lax.axis_index("x"), lax.axis_index("y"), lax.axis_index("c")


def _flip(v, d):
    return 1 - v if d else v


_RELATIONS = [(dx, dy, dc) for dx in (0, 1) for dy in (0, 1) for dc in (0, 1)][1:]
_SIBLING_AND_SAME_CORES = [(0, 0, 1), (1, 0, 0), (0, 1, 0), (1, 1, 0)]


def weight_gather(shards):
    n_arr = len(shards)
    rows = [s.shape[0] for s in shards]

    def body(*refs):
        in_refs = refs[:n_arr]
        out_refs = refs[n_arr:2 * n_arr]
        send_sems, recv_sems, local_sems = refs[2 * n_arr:]
        x, y, c = _mesh_pos()
        me, sibling = (x, y, c), (x, y, 1 - c)
        x_nbr, y_nbr, diag = (1 - x, y, c), (x, 1 - y, c), (1 - x, 1 - y, c)
        north = c == 1
        relay_from = (jnp.where(north, 1 - x, x), jnp.where(north, y, 1 - y), c)
        relay_to = (jnp.where(north, x, 1 - x), jnp.where(north, 1 - y, y), c)
        k_from = jnp.where(north, 1, 2)
        k_to = 3 - k_from

        def block(a, pos):
            px, py, pc = pos
            return out_refs[a].at[pl.ds((4 * px + 2 * py + pc) * rows[a], rows[a]), :]

        def copy(a, k, blk, to, src=None):
            return pltpu.make_async_remote_copy(
                src_ref=block(a, blk) if src is None else src, dst_ref=block(a, blk),
                send_sem=send_sems.at[a, k], recv_sem=recv_sems.at[a, k],
                device_id=to, device_id_type=pl.DeviceIdType.MESH)

        started = []
        mine = []
        for a in range(n_arr):
            cp = pltpu.make_async_copy(in_refs[a], block(a, me), local_sems.at[a])
            cp.start()
            mine.append(cp)
            first = [copy(a, 0, me, sibling, src=in_refs[a]), copy(a, 1, me, x_nbr, src=in_refs[a]),
                     copy(a, 2, me, y_nbr, src=in_refs[a])]
            for cp in first:
                cp.start()
            started += first
        for a in range(n_arr):
            copy(a, k_from, relay_from, me).wait_recv()
            second_hop = copy(a, 3, relay_from, relay_to)
            second_hop.start()
            passed = copy(a, 3 + k_from, relay_from, sibling)
            passed.start()
            started += [second_hop, passed]
        for a in range(n_arr):
            copy(a, k_to, relay_to, me).wait_recv()
            passed = copy(a, 3 + k_to, relay_to, sibling)
            passed.start()
            started.append(passed)
        for a in range(n_arr):
            copy(a, 3, diag, me).wait_recv()
            passed = copy(a, 6, diag, sibling)
            passed.start()
            started.append(passed)
        for a in range(n_arr):
            copy(a, 0, sibling, me).wait_recv()
            for k, chip in ((4, x_nbr), (5, y_nbr), (6, diag)):
                copy(a, k, (chip[0], chip[1], 1 - c), me).wait_recv()
        for cp in started:
            cp.wait_send()
        for cp in mine:
            cp.wait()

    any_spec = pl.BlockSpec(memory_space=pl.ANY)
    return pl.pallas_call(
        body, name="weight_gather",
        out_shape=[jax.ShapeDtypeStruct((N_DEV * s.shape[0], s.shape[1]), s.dtype) for s in shards],
        in_specs=[any_spec] * n_arr, out_specs=[any_spec] * n_arr,
        scratch_shapes=[pltpu.SemaphoreType.DMA((n_arr, 7)), pltpu.SemaphoreType.DMA((n_arr, 7)),
                        pltpu.SemaphoreType.DMA((n_arr,))],
    )(*shards)


_OTHER_CHIPS = [(1, 0), (0, 1), (1, 1)]


def small_all_reduce(small):
    vmem_spec = pl.BlockSpec(memory_space=pltpu.VMEM)

    def body(small_ref, tot_ref, land, send_sems, recv_sems):
        x, y, c = _mesh_pos()
        me = 4 * x + 2 * y + c
        land[me] = small_ref[...]
        sends, recvs = [], []
        for j, (dx, dy, dc) in enumerate(_RELATIONS):
            px, py, pc = _flip(x, dx), _flip(y, dy), _flip(c, dc)
            common = dict(send_sem=send_sems.at[j], recv_sem=recv_sems.at[j],
                          device_id=(px, py, pc), device_id_type=pl.DeviceIdType.MESH)
            sends.append(pltpu.make_async_remote_copy(src_ref=small_ref, dst_ref=land.at[me], **common))
            recvs.append(pltpu.make_async_remote_copy(src_ref=small_ref, dst_ref=land.at[4 * px + 2 * py + pc], **common))
        for cp in sends:
            cp.start()
        for cp in recvs:
            cp.wait_recv()
        for cp in sends:
            cp.wait_send()
        tot = land[0]
        for d in range(1, N_DEV):
            tot = tot + land[d]
        tot_ref[...] = tot

    return pl.pallas_call(
        body, name="small_sum", out_shape=jax.ShapeDtypeStruct(small.shape, small.dtype),
        in_specs=[vmem_spec], out_specs=vmem_spec,
        scratch_shapes=[pltpu.VMEM((N_DEV,) + small.shape, small.dtype),
                        pltpu.SemaphoreType.DMA((len(_RELATIONS),)), pltpu.SemaphoreType.DMA((len(_RELATIONS),))],
    )(small)


_HBM = pl.BlockSpec(memory_space=pltpu.HBM)
_SEM = pl.BlockSpec(memory_space=pltpu.SEMAPHORE)
_EFFECT = pltpu.SideEffectType.DATAFLOW_SIDE_EFFECTING


def _early_copies(src_refs, land_refs, send_sems, recv_sems, rows, gather, relations):
    x, y, c = _mesh_pos()
    me = 4 * x + 2 * y + c
    copies = []
    for a in range(len(src_refs)):
        for dx, dy, dc in relations:
            px, py, pc = _flip(x, dx), _flip(y, dy), _flip(c, dc)
            peer = 4 * px + 2 * py + pc
            copies.append(pltpu.make_async_remote_copy(
                src_ref=src_refs[a] if gather else src_refs[a].at[pl.ds(peer * rows[a], rows[a]), :],
                dst_ref=land_refs[a].at[pl.ds(me * rows[a], rows[a]), :],
                send_sem=send_sems[a], recv_sem=recv_sems[a],
                device_id=(px, py, pc), device_id_type=pl.DeviceIdType.MESH))
    return copies


def early_exchange_start(srcs, name, gather=False, after=None, relations=_RELATIONS):
    n = len(srcs)
    if gather:
        rows = [s.shape[0] for s in srcs]
        lands = [lax.empty((N_DEV * r, s.shape[1]), s.dtype) for r, s in zip(rows, srcs)]
    else:
        rows = [s.shape[0] // N_DEV for s in srcs]
        lands = [lax.empty(s.shape, s.dtype) for s in srcs]

    extra = [] if after is None else [after]

    def body(*refs):
        src_refs, land_refs = refs[:n], refs[n:2 * n]
        first_sem = 2 * n + len(extra)
        send_sems, recv_sems = refs[first_sem:first_sem + n], refs[first_sem + n:first_sem + 2 * n]
        token, local_sems = refs[-2], refs[-1]
        own = []
        if gather:
            x, y, c = _mesh_pos()
            own = [pltpu.make_async_copy(src_refs[a], land_refs[a].at[pl.ds((4 * x + 2 * y + c) * rows[a], rows[a]), :],
                                         local_sems.at[a]) for a in range(n)]
        for cp in own:
            cp.start()
        for cp in own:
            cp.wait()
        for cp in _early_copies(src_refs, land_refs, send_sems, recv_sems, rows, gather, relations):
            cp.start()
        token[...] = jnp.zeros_like(token)

    hbm = lambda a: pltpu.HBM(a.shape, a.dtype)
    outs = pl.pallas_call(
        body, name=name,
        out_shape=[pltpu.SemaphoreType.DMA(())] * (2 * n)
        + [hbm(a) for a in srcs] + [hbm(a) for a in lands] + [jax.ShapeDtypeStruct((8, LANES), F32)],
        in_specs=[_HBM] * (2 * n) + [pl.BlockSpec(memory_space=pl.ANY)] * len(extra),
        out_specs=[_SEM] * (2 * n) + [_HBM] * (2 * n) + [pl.BlockSpec(memory_space=pltpu.VMEM)],
        input_output_aliases={i: 2 * n + i for i in range(2 * n)},
        scratch_shapes=[pltpu.SemaphoreType.DMA((n,))],
        compiler_params=pltpu.CompilerParams(has_side_effects=_EFFECT),
    )(*[pltpu.with_memory_space_constraint(a, pltpu.HBM) for a in list(srcs) + lands], *extra)
    handle = dict(sems=outs[:2 * n], srcs=outs[2 * n:3 * n], lands=outs[3 * n:4 * n], rows=rows,
                  copies=len(relations))
    return handle, outs[-1]


def early_exchange_wait(handle, after, name):
    n = len(handle["srcs"])
    rows = handle["rows"]

    def body(*refs):
        src_refs, land_refs = refs[:n], refs[n:2 * n]
        send_sems, recv_sems = refs[2 * n:3 * n], refs[3 * n:4 * n]
        x, y, c = _mesh_pos()
        for a in range(n):
            span = pl.ds(0, handle["copies"] * rows[a])
            all_copies = pltpu.make_async_remote_copy(
                src_ref=land_refs[a].at[span, :], dst_ref=land_refs[a].at[span, :],
                send_sem=send_sems[a], recv_sem=recv_sems[a],
                device_id=(x, y, c), device_id_type=pl.DeviceIdType.MESH)
            all_copies.wait_send()
            all_copies.wait_recv()

    hbm = lambda a: pltpu.HBM(a.shape, a.dtype)
    ins = list(handle["srcs"]) + list(handle["lands"])
    outs = pl.pallas_call(
        body, name=name,
        out_shape=[hbm(a) for a in ins],
        in_specs=[_HBM] * (2 * n) + [_SEM] * (2 * n) + [pl.BlockSpec(memory_space=pl.ANY)],
        out_specs=[_HBM] * (2 * n),
        input_output_aliases={i: i for i in range(2 * n)},
        compiler_params=pltpu.CompilerParams(has_side_effects=_EFFECT),
    )(*ins, *handle["sems"], after)
    return outs[:n], outs[n:]


def pass_on_to_sibling(lands, rows, name):
    n = len(lands)

    def body(*refs):
        land_refs = refs[n:2 * n]
        send_sems, recv_sems = refs[2 * n:]
        x, y, c = _mesh_pos()
        copies = []
        for a in range(n):
            for k, (dx, dy) in enumerate(_OTHER_CHIPS):
                slot = 4 * _flip(x, dx) + 2 * _flip(y, dy) + c
                blk = land_refs[a].at[pl.ds(slot * rows[a], rows[a]), :]
                copies.append(pltpu.make_async_remote_copy(
                    src_ref=blk, dst_ref=blk, send_sem=send_sems.at[a, k], recv_sem=recv_sems.at[a, k],
                    device_id=(x, y, 1 - c), device_id_type=pl.DeviceIdType.MESH))
        for cp in copies:
            cp.start()
        for cp in copies:
            cp.wait_recv()
        for cp in copies:
            cp.wait_send()

    any_spec = pl.BlockSpec(memory_space=pl.ANY)
    return pl.pallas_call(
        body, name=name,
        out_shape=[jax.ShapeDtypeStruct(a.shape, a.dtype) for a in lands],
        in_specs=[any_spec] * n, out_specs=[any_spec] * n,
        input_output_aliases={i: i for i in range(n)},
        scratch_shapes=[pltpu.SemaphoreType.DMA((n, len(_OTHER_CHIPS))), pltpu.SemaphoreType.DMA((n, len(_OTHER_CHIPS)))],
    )(*lands)


def slot_sum8(src, land, tr, name):
    rows, cols = land.shape[0] // N_DEV, land.shape[1]
    x, y, c = _mesh_pos()
    me = (4 * x + 2 * y + c).astype(jnp.int32).reshape(1)

    def body(me_ref, src_ref, land_ref, o_ref):
        acc = None
        for d in range(N_DEV):
            term = jnp.where(d == me_ref[0], src_ref[0], land_ref[d]).astype(F32)
            acc = term if acc is None else acc + term
        o_ref[...] = acc

    return pl.pallas_call(
        body, name=name,
        grid_spec=pltpu.PrefetchScalarGridSpec(
            num_scalar_prefetch=1, grid=(rows // tr,),
            in_specs=[pl.BlockSpec((1, tr, cols), lambda i, w: (w[0], i, 0)),
                      pl.BlockSpec((N_DEV, tr, cols), lambda i, w: (0, i, 0))],
            out_specs=pl.BlockSpec((tr, cols), lambda i, w: (i, 0))),
        out_shape=jax.ShapeDtypeStruct((rows, cols), F32),
        compiler_params=_params(("arbitrary",)),
    )(me, src.reshape(N_DEV, rows, cols), land.reshape(N_DEV, rows, cols))


def mm_tn_multi(a_t, bs, tt, name, out_dtype=F32):
    K, T = a_t.shape
    widths = [b.shape[1] for b in bs]
    steps = T // tt

    def body(a_ref, *rest):
        b_refs, o_ref, acc = rest[:-2], rest[-2], rest[-1]

        @pl.when(pl.program_id(0) == 0)
        def _():
            acc[...] = jnp.zeros(acc.shape, F32)

        av = a_ref[...]
        col = 0
        for b_ref, w in zip(b_refs, widths):
            acc[:, col:col + w] += jnp.dot(av, b_ref[...], preferred_element_type=F32)
            col += w

        @pl.when(pl.program_id(0) == steps - 1)
        def _():
            o_ref[...] = acc[...].astype(out_dtype)

    return pl.pallas_call(
        body, name=name, grid=(steps,),
        in_specs=[pl.BlockSpec((K, tt), lambda t: (0, t))] + [pl.BlockSpec((tt, w), lambda t: (t, 0)) for w in widths],
        out_specs=pl.BlockSpec((K, sum(widths)), lambda t: (0, 0)),
        out_shape=jax.ShapeDtypeStruct((K, sum(widths)), out_dtype),
        scratch_shapes=[pltpu.VMEM((K, sum(widths)), F32)],
        compiler_params=_params(("arbitrary",)),
    )(a_t, *bs)


def rms_fwd(x, g, tm, name, with_transpose=False):
    M, K = x.shape

    def body(x_ref, g_ref, o_ref, *t_ref):
        xv = x_ref[...]
        r = lax.rsqrt(jnp.mean(xv * xv, axis=-1, keepdims=True) + RMS_EPS)
        h = ((xv * r) * g_ref[...]).astype(BF16)
        o_ref[...] = h
        if with_transpose:
            t_ref[0][...] = h.T

    out_specs = [pl.BlockSpec((tm, K), lambda i: (i, 0))]
    out_shape = [jax.ShapeDtypeStruct((M, K), BF16)]
    if with_transpose:
        out_specs.append(pl.BlockSpec((K, tm), lambda i: (0, i)))
        out_shape.append(jax.ShapeDtypeStruct((K, M), BF16))
    outs = pl.pallas_call(
        body, name=name, grid=(M // tm,),
        in_specs=[pl.BlockSpec((tm, K), lambda i: (i, 0)), pl.BlockSpec((1, K), lambda i: (0, 0))],
        out_specs=out_specs, out_shape=out_shape,
        compiler_params=_params(("arbitrary",)),
    )(x, g)
    return outs if with_transpose else outs[0]


def rms_bwd(x, g, dh, dres, tm, name):
    M, K = x.shape
    has_res = dres is not None

    def body(*refs):
        if has_res:
            x_ref, g_ref, dh_ref, dres_ref, dx_ref, dg_ref = refs
        else:
            x_ref, g_ref, dh_ref, dx_ref, dg_ref = refs
        xv = x_ref[...]
        r = lax.rsqrt(jnp.mean(xv * xv, axis=-1, keepdims=True) + RMS_EPS)
        xn = xv * r
        dhv = dh_ref[...]
        dxn = dhv * g_ref[...]
        dx = r * (dxn - xn * jnp.mean(dxn * xn, axis=-1, keepdims=True))
        if has_res:
            dx = dx + dres_ref[...]
        dx_ref[...] = dx
        part = jnp.sum(dhv * xn, axis=0, keepdims=True)
        row = lax.broadcasted_iota(jnp.int32, (8, K), 0)
        upd = jnp.where(row == 0, part, 0.0)

        @pl.when(pl.program_id(0) == 0)
        def _():
            dg_ref[...] = upd

        @pl.when(pl.program_id(0) != 0)
        def _():
            dg_ref[...] += upd

    row_spec = pl.BlockSpec((tm, K), lambda i: (i, 0))
    ins = [x, g, dh] + ([dres] if has_res else [])
    in_specs = [row_spec, pl.BlockSpec((1, K), lambda i: (0, 0)), row_spec] + ([row_spec] if has_res else [])
    return pl.pallas_call(
        body, name=name, grid=(M // tm,),
        in_specs=in_specs,
        out_specs=[row_spec, pl.BlockSpec((8, K), lambda i: (0, 0))],
        out_shape=[jax.ShapeDtypeStruct((M, K), F32), jax.ShapeDtypeStruct((8, K), F32)],
        compiler_params=_params(("arbitrary",)),
    )(*ins)


def mm_nn(a, b, tm, tn, name, token=None):
    M, K = a.shape
    N = b.shape[1]
    extra = [] if token is None else [token]

    def body(a_ref, b_ref, *rest):
        rest[-1][...] = jnp.dot(a_ref[...], b_ref[...], preferred_element_type=F32)

    return pl.pallas_call(
        body, name=name, grid=(N // tn, M // tm),
        in_specs=[pl.BlockSpec((tm, K), lambda j, i: (i, 0)), pl.BlockSpec((K, tn), lambda j, i: (0, j))]
        + [pl.BlockSpec(t.shape, lambda j, i: (0, 0)) for t in extra],
        out_specs=pl.BlockSpec((tm, tn), lambda j, i: (i, j)),
        out_shape=jax.ShapeDtypeStruct((M, N), F32),
        compiler_params=_params(("arbitrary", "arbitrary")),
    )(a, b, *extra)


def mm_nt(a, b, tm, tk, name):
    M, K = a.shape
    N = b.shape[0]

    def body(a_ref, b_ref, o_ref):
        part = lax.dot_general(a_ref[...], b_ref[...], (((1,), (1,)), ((), ())), preferred_element_type=F32)

        @pl.when(pl.program_id(1) == 0)
        def _():
            o_ref[...] = part

        @pl.when(pl.program_id(1) != 0)
        def _():
            o_ref[...] += part

    return pl.pallas_call(
        body, name=name, grid=(M // tm, K // tk),
        in_specs=[pl.BlockSpec((tm, tk), lambda i, k: (i, k)), pl.BlockSpec((N, tk), lambda i, k: (0, k))],
        out_specs=pl.BlockSpec((tm, N), lambda i, k: (i, 0)),
        out_shape=jax.ShapeDtypeStruct((M, N), F32),
        compiler_params=_params(("arbitrary", "arbitrary")),
    )(a, b)


def in_proj_bwd_rms(pieces, ws, x, g, dres, tm, token):
    M, N = x.shape

    def body(*refs):
        n = len(pieces)
        p_refs, w_refs = refs[:n], refs[n:n + len(ws)]
        x_ref, g_ref, dres_ref, _, dx_ref, dg_ref = refs[n + len(ws):]
        dh = None
        for p_ref, (arr, group, col) in zip(p_refs, pieces):
            part = lax.dot_general(p_ref[...], w_refs[group][:, col:col + arr.shape[1]], (((1,), (1,)), ((), ())),
                                   preferred_element_type=F32)
            dh = part if dh is None else dh + part
        xv = x_ref[...]
        r = lax.rsqrt(jnp.mean(xv * xv, axis=-1, keepdims=True) + RMS_EPS)
        xn = xv * r
        dxn = dh * g_ref[...]
        dx_ref[...] = r * (dxn - xn * jnp.mean(dxn * xn, axis=-1, keepdims=True)) + dres_ref[...]
        row = lax.broadcasted_iota(jnp.int32, (8, N), 0)
        upd = jnp.where(row == 0, jnp.sum(dh * xn, axis=0, keepdims=True), 0.0)

        @pl.when(pl.program_id(0) == 0)
        def _():
            dg_ref[...] = upd

        @pl.when(pl.program_id(0) != 0)
        def _():
            dg_ref[...] += upd

    row_spec = pl.BlockSpec((tm, N), lambda i: (i, 0))
    return pl.pallas_call(
        body, name="in_proj_bwd", grid=(M // tm,),
        in_specs=[pl.BlockSpec((tm, arr.shape[1]), lambda i: (i, 0)) for arr, _, _ in pieces]
        + [pl.BlockSpec(w.shape, lambda i: (0, 0)) for w in ws]
        + [row_spec, pl.BlockSpec((1, N), lambda i: (0, 0)), row_spec, pl.BlockSpec(token.shape, lambda i: (0, 0))],
        out_specs=[row_spec, pl.BlockSpec((8, N), lambda i: (0, 0))],
        out_shape=[jax.ShapeDtypeStruct((M, N), F32), jax.ShapeDtypeStruct((8, N), F32)],
        compiler_params=_params(("arbitrary",)),
    )(*[arr for arr, _, _ in pieces], *ws, x, g, dres, token)


def _log_sigmoid(z):
    return jnp.minimum(z, 0.0) - jnp.log(1.0 + jnp.exp(-jnp.abs(z)))


def _tri(n, lower):
    r = lax.broadcasted_iota(jnp.int32, (n, n), 0)
    c = lax.broadcasted_iota(jnp.int32, (n, n), 1)
    return jnp.where((r >= c) if lower else (r <= c), 1.0, 0.0).astype(F32)


def fox_gate(proj3, b_pad):
    B, S, _ = proj3.shape
    nblk = S // TK

    def body(f_ref, b_ref, o_ref):
        tri = _tri(TK, True)
        carry = jnp.zeros((1, LANES), F32)
        for n in range(nblk):
            z = f_ref[0, n * TK:(n + 1) * TK, :] + b_ref[...]
            logf = _log_sigmoid(z)
            cs = jnp.dot(tri, logf, preferred_element_type=F32, precision=lax.Precision.HIGHEST) + carry
            carry = cs[TK - 1:TK, :]
            o_ref[0, n * TK:(n + 1) * TK, :] = -cs

    return pl.pallas_call(
        body, name="fox_gate", grid=(B,),
        in_specs=[pl.BlockSpec((1, S, LANES), lambda b: (b, 0, A_FLOG // LANES)),
                  pl.BlockSpec((1, LANES), lambda b: (0, 0))],
        out_specs=pl.BlockSpec((1, S, LANES), lambda b: (b, 0, 0)),
        out_shape=jax.ShapeDtypeStruct((B, S, LANES), F32),
        compiler_params=_params(("arbitrary",)),
    )(proj3, b_pad)


def fox_gate_bwd(drow, dneg, proj3, b_pad):
    B, S, _ = proj3.shape
    nblk = S // TK

    def body(d_ref, r_ref, f_ref, b_ref, o_ref, db_ref):
        tri = _tri(TK, False)
        lane = lax.broadcasted_iota(jnp.int32, (TK, LANES), 1)
        carry = jnp.zeros((1, LANES), F32)
        dbsum = jnp.zeros((1, LANES), F32)
        for n in reversed(range(nblk)):
            dk_side = None
            for hp in range(FOX_HEADS // 2):
                two = jnp.where(lane < 2, r_ref[0, n * TK:(n + 1) * TK, hp * LANES:(hp + 1) * LANES], 0.0)
                two = pltpu.roll(two, 2 * hp, 1) if hp else two
                dk_side = two if dk_side is None else dk_side + two
            dc = jnp.where(lane < FOX_HEADS, d_ref[0, :, n * TK:(n + 1) * TK].T - dk_side, 0.0)
            rs = jnp.dot(tri, dc, preferred_element_type=F32, precision=lax.Precision.HIGHEST) + carry
            carry = rs[0:1, :]
            z = f_ref[0, n * TK:(n + 1) * TK, :] + b_ref[...]
            dz = rs * (1.0 / (1.0 + jnp.exp(z)))
            o_ref[0, n * TK:(n + 1) * TK, :] = dz.astype(BF16)
            dbsum = dbsum + jnp.sum(dz, axis=0, keepdims=True)
        row = lax.broadcasted_iota(jnp.int32, (8, LANES), 0)
        upd = jnp.where(row == 0, dbsum, 0.0)

        @pl.when(pl.program_id(0) == 0)
        def _():
            db_ref[...] = upd

        @pl.when(pl.program_id(0) != 0)
        def _():
            db_ref[...] += upd

    return pl.pallas_call(
        body, name="fox_gate_bwd", grid=(B,),
        in_specs=[pl.BlockSpec((1, LANES, S), lambda b: (b, 0, 0)),
                  pl.BlockSpec((1, S, FOX_W), lambda b: (b, 0, 0)),
                  pl.BlockSpec((1, S, LANES), lambda b: (b, 0, A_FLOG // LANES)),
                  pl.BlockSpec((1, LANES), lambda b: (0, 0))],
        out_specs=[pl.BlockSpec((1, S, LANES), lambda b: (b, 0, 0)), pl.BlockSpec((8, LANES), lambda b: (0, 0))],
        out_shape=[jax.ShapeDtypeStruct((B, S, LANES), BF16), jax.ShapeDtypeStruct((8, LANES), F32)],
        compiler_params=_params(("arbitrary",)),
    )(drow, dneg, proj3, b_pad)


def _rope_tables(S):
    half = ROPE_DIM // 2
    f32 = np.float32
    pos = np.arange(S, dtype=f32)
    inv_freq = f32(1.0) / np.power(f32(ROPE_THETA), np.arange(0, ROPE_DIM, 2, dtype=f32) / f32(ROPE_DIM)).astype(f32)
    ang = (pos[:, None] * inv_freq[None, :]).astype(f32).astype(np.float64)
    cos, sin = np.cos(ang).astype(f32), np.sin(ang).astype(f32)
    one = np.ones((S, HEAD_DIM - ROPE_DIM), f32)
    zero = np.zeros((S, HEAD_DIM - ROPE_DIM), f32)
    zh = np.zeros((S, half), f32)
    c = np.concatenate([cos, cos, one], axis=1)
    s1 = np.concatenate([-sin, zh, zero], axis=1)
    s2 = np.concatenate([zh, sin, zero], axis=1)
    return tuple(jnp.asarray(np.concatenate([t, t], axis=1)) for t in (c, s1, s2))


_HALF_ROPE = ROPE_DIM // 2


def _rope(t, c, s1, s2):
    return t * c + pltpu.roll(t, LANES - _HALF_ROPE, 1) * s1 + pltpu.roll(t, _HALF_ROPE, 1) * s2


def _rope_bwd(d, c, s1, s2):
    return d * c + pltpu.roll(d * s1, _HALF_ROPE, 1) + pltpu.roll(d * s2, LANES - _HALF_ROPE, 1)


def _scale_parts(scale):
    m, _ = math.frexp(scale)
    return (scale, None) if m == 0.5 else (None, scale)


def _log_masks(S, kind):
    nd = 1 if kind == "causal" else S // TQ
    a = np.arange(TQ)[:, None]
    b = np.arange(TK)[None, :]
    out = np.zeros((nd, TQ, TK), np.float32)
    for d in range(nd):
        delta = d * TQ + a - b
        if kind == "causal":
            m = (delta >= 0).astype(np.float64)
        else:
            m = sum(((delta >= 0) & (delta % dil == 0) & (delta <= w)).astype(np.float64) for w, dil in DILATIONS)
        out[d] = np.where(m > 0, np.log(np.maximum(m, 1.0)), NEG_INF)
    return jnp.asarray(out)


def _attn_setup(kind):
    pair = kind != "mem"
    e_dim = HEAD_DIM if pair else MEM_HEAD_DIM
    q_fold, s_scale = _scale_parts(1.0 / math.sqrt(e_dim))
    return dict(pair=pair, col0={"fox": A_FOX, "dil": B_DIL, "mem": B_MQ}[kind],
                n_blocks=FOX_HEADS // 2 if pair else MEM_HEADS, q_fold=q_fold, s_scale=s_scale,
                nh=2 if pair else 1)


def _cat(parts, axis):
    return parts[0] if len(parts) == 1 else jnp.concatenate(parts, axis=axis)


def _log_masks_t(S, kind):
    return jnp.swapaxes(_log_masks(S, kind), 1, 2)


def _head_rows(hh, pair):
    row = lax.broadcasted_iota(jnp.int32, (LANES, 1), 0)
    if not pair:
        return row >= 0
    return (row >= HEAD_DIM * hh) & (row < HEAD_DIM * (hh + 1))


def _attn_t_inputs(kind, src, S, negc_cols, mask, rope, kv):
    cfg = _attn_setup(kind)
    col0 = cfg["col0"]
    ins, in_specs = [], []
    if cfg["pair"]:
        ins.append(src)
        in_specs.append(pl.BlockSpec((1, S, PAIR_W), lambda b, h: (b, 0, col0 // PAIR_W + h)))
    else:
        ins += [src, kv, kv]
        in_specs += [pl.BlockSpec((1, S, LANES), lambda b, h: (b, 0, col0 // LANES + h)),
                     pl.BlockSpec((1, MEM_LEN, LANES), lambda b, h: (b, 0, h)),
                     pl.BlockSpec((1, MEM_LEN, LANES), lambda b, h: (b, 0, MEM_HEADS + h))]
    if negc_cols is not None:
        ins.append(negc_cols)
        in_specs.append(pl.BlockSpec((1, S, LANES), lambda b, h: (b, 0, 0)))
    if mask is not None:
        ins.append(mask)
        in_specs.append(pl.BlockSpec(mask.shape, lambda b, h: (0, 0, 0)))
    if rope is not None:
        ins += list(rope)
        in_specs += [pl.BlockSpec((S, LANES), lambda b, h: (0, 0))] * 3
    return ins, in_specs


def _attn_t_prep(cfg, refs, S, Sk, *, qT2s, ks, vs=None, vTs=None, kTs=None, nb=None):
    pair, nh = cfg["pair"], cfg["nh"]
    lane = lax.broadcasted_iota(jnp.int32, (1, LANES), 1)
    rope_refs = refs["rope"]

    def prep_q(n):
        rows = slice(n * TQ, (n + 1) * TQ)
        q = refs["load_q"](rows)
        if rope_refs is not None:
            q = _rope(q, *[t[rows, :] for t in rope_refs])
        if cfg["q_fold"] is not None:
            q = q * cfg["q_fold"]
        qtb = q.astype(BF16).T
        for hh in range(nh):
            qT2s[n, :, hh * TQ:(hh + 1) * TQ] = jnp.where(_head_rows(hh, pair), qtb, jnp.zeros_like(qtb))

    def prep_kv(n):
        rows = slice(n * TK, (n + 1) * TK)
        k, v = refs["load_kv"](rows)
        if rope_refs is not None:
            k = _rope(k, *[t[rows, :] for t in rope_refs])
        kb = k.astype(BF16)
        vb = v.astype(BF16)
        ks[rows, :] = kb
        if vs is not None:
            vs[rows, :] = vb
        if vTs is not None:
            vTs[n] = vb.T
        if kTs is not None:
            kTs[n] = kb.T
        if nb is not None:
            blk = refs["negc"][0, rows, :]
            for hh in range(nh):
                h = 2 * refs["block"] + hh
                col = jnp.sum(jnp.where(lane == h, blk, 0.0), axis=1, keepdims=True)
                nb[hh, rows, :] = jnp.broadcast_to(col, (TK, LANES))

    for n in range(S // TQ):
        prep_q(n)
    for n in range(Sk // TK):
        prep_kv(n)


def _raw_scores_t(cfg, k, qT2):
    sT = jnp.dot(k, qT2, preferred_element_type=F32)
    if cfg["s_scale"] is not None:
        sT = sT * cfg["s_scale"]
    return sT


def _bias_mask_t(cfg, sT, nb, mask_ref, kc, midx):
    nh = cfg["nh"]
    if nb is None and midx is None:
        return sT
    parts = []
    for hh in range(nh):
        t = sT[:, hh * TQ:(hh + 1) * TQ]
        if nb is not None:
            t = t + jnp.concatenate([nb[hh, kc, :]] * (TQ // LANES), axis=1)
        if midx is not None:
            t = t + mask_ref[midx]
        parts.append(t)
    return _cat(parts, 1)


def _tile_pairs(kind, nq, nk):
    if kind == "mem":
        return [(i, j) for i in range(nq) for j in range(nk)], (lambda i, j: None)
    pairs = [(i, j) for i in range(nq) for j in range(i + 1)]
    if kind == "fox":
        return pairs, (lambda i, j: 0 if j == i else None)
    return pairs, (lambda i, j: i - j)


def attn_fwd(kind, src, S, *, negc_cols=None, mask=None, rope=None, kv=None):
    B = src.shape[0]
    cfg = _attn_setup(kind)
    pair, nh = cfg["pair"], cfg["nh"]
    Sk = S if pair else MEM_LEN
    has_bias, has_rope = negc_cols is not None, rope is not None
    R = nh * TQ
    nq, nk = S // TQ, Sk // TK
    pairs, mask_index = _tile_pairs(kind, nq, nk)

    def body(*refs):
        refs = list(refs)
        if pair:
            qkv_ref = refs.pop(0)
            load_q = lambda rows: qkv_ref[0, rows, 0:LANES]
            load_kv = lambda rows: (qkv_ref[0, rows, LANES:2 * LANES], qkv_ref[0, rows, 2 * LANES:3 * LANES])
        else:
            q_ref, k_ref, v_ref = refs.pop(0), refs.pop(0), refs.pop(0)
            load_q = lambda rows: q_ref[0, rows, :]
            load_kv = lambda rows: (k_ref[0, rows, :], v_ref[0, rows, :])
        negc_ref = refs.pop(0) if has_bias else None
        mask_ref = refs.pop(0) if mask is not None else None
        rope_refs = [refs.pop(0) for _ in range(3)] if has_rope else None
        o_ref, lse_ref, qT2s, ks, vTs, s_a, s_b, p_a, p_b = refs[:9]
        nb = refs[9] if has_bias else None
        _attn_t_prep(cfg, dict(load_q=load_q, load_kv=load_kv, rope=rope_refs, negc=negc_ref,
                               block=pl.program_id(1)), S, Sk, qT2s=qT2s, ks=ks, vTs=vTs, nb=nb)

        def cols(j):
            return slice(j * TK, (j + 1) * TK)

        def scores(i, j):
            return _raw_scores_t(cfg, ks[cols(j), :], qT2s[i])

        def finish(i, m, l, accT):
            oT2 = accT / l
            oT = jnp.where(_head_rows(0, True), oT2[:, 0:TQ], oT2[:, TQ:2 * TQ]) if pair else oT2
            o_ref[0, i * TQ:(i + 1) * TQ, :] = oT.T
            lse_ref[0, 0, i:i + 1, :] = m + jnp.log(l)

        s_bufs, p_bufs = (s_a, s_b), (p_a, p_b)
        s_bufs[0][...] = scores(*pairs[0])
        m = l = accT = None
        for t, (i, j) in enumerate(pairs):
            cur, oth = t % 2, 1 - t % 2
            if t > 0:
                i_prev, j_prev = pairs[t - 1]
                pv = jnp.dot(vTs[j_prev], p_bufs[oth][...], preferred_element_type=F32)
                acc_full = pv if accT is None else accT + pv
            if t + 1 < len(pairs):
                s_bufs[oth][...] = scores(*pairs[t + 1])
            first = j == 0
            if first and t > 0:
                finish(i_prev, m, l, acc_full)
            sT = _bias_mask_t(cfg, s_bufs[cur][...], nb, mask_ref, cols(j), mask_index(i, j))
            m_tile = jnp.max(sT, axis=0, keepdims=True)
            m_new = m_tile if first else jnp.maximum(m, m_tile)
            p = jnp.exp(sT - m_new)
            p_bufs[cur][...] = p.astype(BF16)
            if first:
                l, accT = jnp.sum(p, axis=0, keepdims=True), None
            else:
                alpha = jnp.exp(m - m_new)
                l, accT = alpha * l + jnp.sum(p, axis=0, keepdims=True), acc_full * alpha
            m = m_new
        i_last, j_last = pairs[-1]
        pv = jnp.dot(vTs[j_last], p_bufs[(len(pairs) - 1) % 2][...], preferred_element_type=F32)
        finish(i_last, m, l, pv if accT is None else accT + pv)

    ins, in_specs = _attn_t_inputs(kind, src, S, negc_cols, mask, rope, kv)
    W = cfg["n_blocks"] * LANES
    scratch = [pltpu.VMEM((nq, LANES, R), BF16), pltpu.VMEM((Sk, LANES), BF16), pltpu.VMEM((nk, LANES, TK), BF16),
               pltpu.VMEM((TK, R), F32), pltpu.VMEM((TK, R), F32), pltpu.VMEM((TK, R), BF16), pltpu.VMEM((TK, R), BF16)]
    if has_bias:
        scratch.append(pltpu.VMEM((nh, Sk, LANES), F32))
    return pl.pallas_call(
        body, name=kind + "_attn_fwd", grid=(B, cfg["n_blocks"]),
        in_specs=in_specs,
        out_specs=[pl.BlockSpec((1, S, LANES), lambda b, h: (b, 0, h)),
                   pl.BlockSpec((1, 1, nq, R), lambda b, h: (b, h, 0, 0))],
        out_shape=[jax.ShapeDtypeStruct((B, S, W), F32), jax.ShapeDtypeStruct((B, cfg["n_blocks"], nq, R), F32)],
        scratch_shapes=scratch,
        compiler_params=_params(("arbitrary", "arbitrary")),
    )(*ins)


def attn_bwd(kind, src, do, o, lse, S, *, negc_cols=None, mask=None, rope=None, kv=None, token=None):
    B = src.shape[0]
    cfg = _attn_setup(kind)
    pair, nh, s_scale, q_fold = cfg["pair"], cfg["nh"], cfg["s_scale"], cfg["q_fold"]
    Sk = S if pair else MEM_LEN
    has_bias, has_rope = negc_cols is not None, rope is not None
    R = nh * TQ
    nq, nk = S // TQ, Sk // TK
    pairs, mask_index = _tile_pairs(kind, nq, nk)

    def body(*refs):
        refs = list(refs)
        if pair:
            qkv_ref = refs.pop(0)
            load_q = lambda rows: qkv_ref[0, rows, 0:LANES]
            load_kv = lambda rows: (qkv_ref[0, rows, LANES:2 * LANES], qkv_ref[0, rows, 2 * LANES:3 * LANES])
        else:
            q_ref, k_ref, v_ref = refs.pop(0), refs.pop(0), refs.pop(0)
            load_q = lambda rows: q_ref[0, rows, :]
            load_kv = lambda rows: (k_ref[0, rows, :], v_ref[0, rows, :])
        negc_ref = refs.pop(0) if has_bias else None
        mask_ref = refs.pop(0) if mask is not None else None
        rope_refs = [refs.pop(0) for _ in range(3)] if has_rope else None
        do_ref, o_ref, lse_ref = refs.pop(0), refs.pop(0), refs.pop(0)
        if token is not None:
            refs.pop(0)
        if pair:
            dqkv_ref = refs.pop(0)
            dneg_ref = refs.pop(0) if has_bias else None
            drow_ref = refs.pop(0) if has_bias else None
        else:
            dq_ref, dk_ref, dv_ref = refs.pop(0), refs.pop(0), refs.pop(0)
        qT2s, ks, vs, kTs, doT2s, delta_s, dk_acc, dv_acc = refs[:8]
        bufs_a, bufs_b = refs[8:12], refs[12:16]
        nb, dneg_acc = (refs[16], refs[17]) if has_bias else (None, None)
        lane = lax.broadcasted_iota(jnp.int32, (1, LANES), 1)
        _attn_t_prep(cfg, dict(load_q=load_q, load_kv=load_kv, rope=rope_refs, negc=negc_ref,
                               block=pl.program_id(1)), S, Sk,
                     qT2s=qT2s, ks=ks, vs=vs, kTs=kTs, nb=nb)

        def prep_do(n):
            rows = slice(n * TQ, (n + 1) * TQ)
            doT = do_ref[0, rows, :].astype(BF16).astype(F32).T
            prodT = doT * o_ref[0, rows, :].T
            doTb = doT.astype(BF16)
            for hh in range(nh):
                hm = _head_rows(hh, pair)
                doT2s[n, :, hh * TQ:(hh + 1) * TQ] = jnp.where(hm, doTb, jnp.zeros_like(doTb))
                delta_s[n:n + 1, hh * TQ:(hh + 1) * TQ] = jnp.sum(jnp.where(hm, prodT, 0.0), axis=0, keepdims=True)

        for n in range(nq):
            prep_do(n)
        dk_acc[...] = jnp.zeros(dk_acc.shape, F32)
        dv_acc[...] = jnp.zeros(dv_acc.shape, F32)
        if has_bias:
            dneg_acc[...] = jnp.zeros(dneg_acc.shape, F32)

        def cols(j):
            return slice(j * TK, (j + 1) * TK)

        nt_dims = (((1,), (1,)), ((), ()))

        def first_products(i, j, bufs):
            bufs[0][...] = _raw_scores_t(cfg, ks[cols(j), :], qT2s[i])
            bufs[1][...] = jnp.dot(vs[cols(j), :], doT2s[i], preferred_element_type=F32)

        def last_products(i, j, bufs, dqT2):
            dv_acc[j] += lax.dot_general(doT2s[i], bufs[2][...], nt_dims, preferred_element_type=F32)
            dk_acc[j] += lax.dot_general(qT2s[i], bufs[3][...], nt_dims, preferred_element_type=F32)
            dq = jnp.dot(kTs[j], bufs[3][...], preferred_element_type=F32)
            return dq if dqT2 is None else dqT2 + dq

        def finish_q(i, dqT2, drow):
            rows = slice(i * TQ, (i + 1) * TQ)
            dqT = jnp.where(_head_rows(0, True), dqT2[:, 0:TQ], dqT2[:, TQ:2 * TQ]) if pair else dqT2
            dq = dqT.T
            if q_fold is not None:
                dq = dq * q_fold
            if has_rope:
                dq = _rope_bwd(dq, *[t[rows, :] for t in rope_refs])
            if pair:
                dqkv_ref[0, rows, 0:LANES] = dq.astype(BF16)
            else:
                dq_ref[0, rows, :] = dq.astype(BF16)
            if has_bias:
                drow_ref[0, 0, i:i + 1, :] = drow

        bufs = (bufs_a, bufs_b)
        first_products(*pairs[0], bufs[0])
        dqT2 = drow = None
        for t, (i, j) in enumerate(pairs):
            cur, oth = bufs[t % 2], bufs[1 - t % 2]
            first = j == 0
            if first and t > 0:
                i_prev, j_prev = pairs[t - 1]
                finish_q(i_prev, last_products(i_prev, j_prev, oth, dqT2), drow)
                dqT2 = drow = None
            sT = _bias_mask_t(cfg, cur[0][...], nb, mask_ref, cols(j), mask_index(i, j))
            pT = jnp.exp(sT - lse_ref[0, 0, i:i + 1, :])
            dsT = pT * (cur[1][...] - delta_s[i:i + 1, :])
            if has_bias:
                tile_rows = jnp.sum(dsT, axis=0, keepdims=True)
                drow = tile_rows if drow is None else drow + tile_rows
                for hh in range(nh):
                    part = dsT[:, hh * TQ:hh * TQ + LANES]
                    for u in range(1, TQ // LANES):
                        part = part + dsT[:, hh * TQ + u * LANES:hh * TQ + (u + 1) * LANES]
                    dneg_acc[hh, cols(j), :] += part
            if s_scale is not None:
                dsT = dsT * s_scale
            cur[2][...] = pT.astype(BF16)
            cur[3][...] = dsT.astype(BF16)
            if not first:
                dqT2 = last_products(*pairs[t - 1], oth, dqT2)
            if t + 1 < len(pairs):
                first_products(*pairs[t + 1], oth)
        i_last, j_last = pairs[-1]
        finish_q(i_last, last_products(i_last, j_last, bufs[(len(pairs) - 1) % 2], dqT2), drow)

        for n in range(nk):
            rows = slice(n * TK, (n + 1) * TK)
            dk = dk_acc[n].T
            dv = dv_acc[n].T
            if has_rope:
                dk = _rope_bwd(dk, *[t[rows, :] for t in rope_refs])
            if pair:
                dqkv_ref[0, rows, LANES:2 * LANES] = dk.astype(BF16)
                dqkv_ref[0, rows, 2 * LANES:3 * LANES] = dv.astype(BF16)
            else:
                dk_ref[0, rows, :] = dk.astype(BF16)
                dv_ref[0, rows, :] = dv.astype(BF16)
            if has_bias:
                x0 = jnp.sum(dneg_acc[0, rows, :], axis=1, keepdims=True)
                x1 = jnp.sum(dneg_acc[1, rows, :], axis=1, keepdims=True)
                dneg_ref[0, rows, :] = jnp.where(lane == 0, x0, jnp.where(lane == 1, x1, 0.0))

    ins, in_specs = _attn_t_inputs(kind, src, S, negc_cols, mask, rope, kv)
    row_spec = pl.BlockSpec((1, S, LANES), lambda b, h: (b, 0, h))
    vec_spec = pl.BlockSpec((1, 1, nq, R), lambda b, h: (b, h, 0, 0))
    ins += [do, o, lse]
    in_specs += [row_spec, row_spec, vec_spec]
    if token is not None:
        ins.append(token)
        in_specs.append(pl.BlockSpec(token.shape, lambda b, h: (0, 0)))
    W = cfg["n_blocks"] * LANES
    if pair:
        out_specs = [pl.BlockSpec((1, S, PAIR_W), lambda b, h: (b, 0, h))]
        out_shape = [jax.ShapeDtypeStruct((B, S, 3 * W), BF16)]
        if has_bias:
            out_specs += [row_spec, vec_spec]
            out_shape += [jax.ShapeDtypeStruct((B, S, W), F32), jax.ShapeDtypeStruct((B, cfg["n_blocks"], nq, R), F32)]
    else:
        kv_spec = pl.BlockSpec((1, MEM_LEN, LANES), lambda b, h: (b, 0, h))
        out_specs = [row_spec, kv_spec, kv_spec]
        out_shape = [jax.ShapeDtypeStruct((B, S, W), BF16)] + [jax.ShapeDtypeStruct((B, MEM_LEN, W), BF16)] * 2
    scratch = [pltpu.VMEM((nq, LANES, R), BF16), pltpu.VMEM((Sk, LANES), BF16),
               pltpu.VMEM((Sk, LANES), BF16), pltpu.VMEM((nk, LANES, TK), BF16), pltpu.VMEM((nq, LANES, R), BF16),
               pltpu.VMEM((nq, R), F32), pltpu.VMEM((nk, LANES, TK), F32), pltpu.VMEM((nk, LANES, TK), F32)]
    pair_bufs = [pltpu.VMEM((TK, R), F32), pltpu.VMEM((TK, R), F32), pltpu.VMEM((TK, R), BF16), pltpu.VMEM((TK, R), BF16)]
    scratch += pair_bufs + pair_bufs
    if has_bias:
        scratch += [pltpu.VMEM((nh, Sk, LANES), F32), pltpu.VMEM((nh, Sk, LANES), F32)]
    return pl.pallas_call(
        body, name=kind + "_attn_bwd", grid=(B, cfg["n_blocks"]),
        in_specs=in_specs, out_specs=out_specs, out_shape=out_shape, scratch_shapes=scratch,
        compiler_params=_params(("arbitrary", "arbitrary")),
    )(*ins)


def _sigmoid(g):
    return 1.0 / (1.0 + jnp.exp(-g))


def out_step(proj, o_fox, o_dil, o_mem, w_out, x, target, gf, tm):
    T = x.shape[0]

    def body(fg_ref, dg_ref, mg_ref, of_ref, od_ref, om_ref, w_ref, x_ref, t_ref, gf_ref,
             dx_ref, dof_ref, dod_ref, dom_ref, dfg_ref, ddg_ref, dmg_ref, gw_ref, sm_ref, gw_acc):
        branches = []
        for g_ref, o_ref in ((fg_ref, of_ref), (dg_ref, od_ref), (mg_ref, om_ref)):
            g = g_ref[...]
            sg = _sigmoid(g)
            o = o_ref[...]
            branches.append((g, sg, o))
        ymix = jnp.concatenate([(o * (g * sg)).astype(BF16) for g, sg, o in branches], axis=1)
        x2 = x_ref[...] + jnp.dot(ymix, w_ref[...], preferred_element_type=F32)
        r = lax.rsqrt(jnp.mean(x2 * x2, axis=-1, keepdims=True) + RMS_EPS)
        yn = x2 * r
        err = yn * gf_ref[...] - t_ref[...]
        loss = 0.5 * jnp.sum(jnp.sum(err * err, axis=-1, keepdims=True) / D_MODEL, axis=0, keepdims=True)
        dyf = err / D_MODEL
        dgf = jnp.sum(dyf * yn, axis=0, keepdims=True)
        dyn = dyf * gf_ref[...]
        dx2 = r * (dyn - yn * jnp.mean(dyn * yn, axis=-1, keepdims=True))
        dx_ref[...] = dx2
        dxb = dx2.astype(BF16)
        dmix = lax.dot_general(dxb, w_ref[...], (((1,), (1,)), ((), ())), preferred_element_type=F32)
        col = 0
        for (g, sg, o), do_ref, dgate_ref in zip(branches, (dof_ref, dod_ref, dom_ref), (dfg_ref, ddg_ref, dmg_ref)):
            d = dmix[:, col:col + g.shape[1]]
            col += g.shape[1]
            do_ref[...] = (d * (g * sg)).astype(BF16)
            dgate_ref[...] = (d * o * (sg * (1.0 + g * (1.0 - sg)))).astype(BF16)
        row = lax.broadcasted_iota(jnp.int32, (8, D_MODEL), 0)
        upd = jnp.where(row == 0, dgf, jnp.where(row == 1, loss, 0.0))

        @pl.when(pl.program_id(0) == 0)
        def _():
            sm_ref[...] = jnp.zeros(sm_ref.shape, F32)
            gw_acc[...] = jnp.zeros(gw_acc.shape, F32)

        sm_ref[...] += upd
        gw_acc[...] += lax.dot_general(ymix, dxb, (((0,), (0,)), ((), ())), preferred_element_type=F32)

        @pl.when(pl.program_id(0) == T // tm - 1)
        def _():
            gw_ref[...] = gw_acc[...].astype(BF16)

    def rows(w, col=0):
        return pl.BlockSpec((tm, w), lambda i: (i, col))

    return pl.pallas_call(
        body, name="out_step", grid=(T // tm,),
        in_specs=[rows(FOX_W, B_FG // FOX_W), rows(DIL_W, B_DG // DIL_W), rows(MEM_W, B_MG // MEM_W),
                  rows(FOX_W), rows(DIL_W), rows(MEM_W),
                  pl.BlockSpec((MIX_W, D_MODEL), lambda i: (0, 0)),
                  rows(D_MODEL), rows(D_MODEL), pl.BlockSpec((1, D_MODEL), lambda i: (0, 0))],
        out_specs=[rows(D_MODEL), rows(FOX_W), rows(DIL_W), rows(MEM_W), rows(FOX_W), rows(DIL_W), rows(MEM_W),
                   pl.BlockSpec((MIX_W, D_MODEL), lambda i: (0, 0)), pl.BlockSpec((8, D_MODEL), lambda i: (0, 0))],
        out_shape=[jax.ShapeDtypeStruct((T, D_MODEL), F32), jax.ShapeDtypeStruct((T, FOX_W), BF16),
                   jax.ShapeDtypeStruct((T, DIL_W), BF16), jax.ShapeDtypeStruct((T, MEM_W), BF16),
                   jax.ShapeDtypeStruct((T, FOX_W), BF16), jax.ShapeDtypeStruct((T, DIL_W), BF16),
                   jax.ShapeDtypeStruct((T, MEM_W), BF16), jax.ShapeDtypeStruct((MIX_W, D_MODEL), BF16),
                   jax.ShapeDtypeStruct((8, D_MODEL), F32)],
        scratch_shapes=[pltpu.VMEM((MIX_W, D_MODEL), F32)],
        compiler_params=_params(("arbitrary",)),
    )(proj, proj, proj, o_fox, o_dil, o_mem, w_out, x, target, gf)


def adamw(w, g, m, v, tr, name):
    lead = w.shape[:-2]
    R, C = w.shape[-2:]
    zeros = (0,) * len(lead)

    def body(w_ref, g_ref, m_ref, v_ref, d_ref, mo_ref, vo_ref):
        gv = g_ref[...]
        mn = ADAM_B1 * m_ref[...] + (1.0 - ADAM_B1) * gv
        vn = ADAM_B2 * v_ref[...] + (1.0 - ADAM_B2) * jnp.square(gv)
        m_hat = mn / (1.0 - ADAM_B1 ** ADAM_STEP)
        v_hat = vn / (1.0 - ADAM_B2 ** ADAM_STEP)
        d_ref[...] = -ADAM_LR * (m_hat / (jnp.sqrt(v_hat) + ADAM_EPS) + ADAM_WD * w_ref[...])
        mo_ref[...] = mn
        vo_ref[...] = vn

    spec = pl.BlockSpec((1,) * len(lead) + (tr, C), lambda i: zeros + (i, 0))
    return pl.pallas_call(
        body, name=name, grid=(pl.cdiv(R, tr),),
        in_specs=[spec] * 4, out_specs=[spec] * 3,
        out_shape=[jax.ShapeDtypeStruct(w.shape, F32)] * 3,
        compiler_params=_params(("arbitrary",)),
    )(w, g, m, v)


def _pad_row(v, width):
    return jnp.concatenate([v, jnp.zeros((1, width - v.shape[1]), v.dtype)], axis=1)


def local_grads(x, mem, norm_g, b_forget, mem_norm_g, final_norm_g, loss_target, w_in_a, first_token, late_weights,
                start_exchange):
    B, S, D = x.shape
    T = B * S
    xt = x.reshape(T, D)
    memt = mem.reshape(B * MEM_LEN, D)
    b_pad = _pad_row(b_forget, LANES)

    h, h_t = rms_fwd(xt, norm_g, 512, "rms_x", with_transpose=True)
    proj_a = mm_nn(h, w_in_a, 512, PA, "in_proj_a", first_token)
    proj_a3 = proj_a.reshape(B, S, PA)

    negc = fox_gate(proj_a3, b_pad)
    causal = _log_masks_t(S, "causal")
    dilated = _log_masks_t(S, "dilated")
    rope = _rope_tables(S)

    o_fox, lse_fox = attn_fwd("fox", proj_a3, S, negc_cols=negc, mask=causal)

    w_in_b, w_kv, w_out = late_weights(o_fox)
    proj_b = mm_nn(h, w_in_b, 512, PB // 2, "in_proj_b")
    proj_b3 = proj_b.reshape(B, S, PB)
    o_dil, lse_dil = attn_fwd("dil", proj_b3, S, mask=dilated, rope=rope)

    mh, mh_t = rms_fwd(memt, mem_norm_g, B * MEM_LEN, "rms_mem", with_transpose=True)
    mkv = mm_nn(mh, w_kv, B * MEM_LEN, 2 * MEM_W, "mem_kv_proj")
    mkv3 = mkv.reshape(B, MEM_LEN, 2 * MEM_W)
    o_mem, lse_mem = attn_fwd("mem", proj_b3, S, kv=mkv3)

    dx2, do_fox, do_dil, do_mem, dfg, ddg, dmg, g_out, small_out = out_step(
        proj_b, o_fox.reshape(T, FOX_W), o_dil.reshape(T, DIL_W), o_mem.reshape(T, MEM_W), w_out,
        xt, loss_target.reshape(T, D), final_norm_g.reshape(1, D), 256)

    gates = [(dfg, 1, B_FG), (ddg, 1, B_DG), (dmg, 1, B_MG)]
    g_gates = mm_tn_multi(h_t, [piece[0] for piece in gates], 1024, "w_in_grad_gates", BF16)
    first, token = start_exchange([g_gates, g_out], "early_exchange_a")

    dqkv_fox, dneg, drow = attn_bwd("fox", proj_a3, do_fox.reshape(B, S, FOX_W), o_fox, lse_fox, S,
                                    negc_cols=negc, mask=causal, token=token)
    drow = drow.reshape(B, FOX_HEADS // 2, S // TQ, 2, TQ).transpose(0, 1, 3, 2, 4).reshape(B, FOX_HEADS, S)
    drow = jnp.pad(drow, ((0, 0), (0, LANES - FOX_HEADS), (0, 0)))
    dflog, db_part = fox_gate_bwd(drow, dneg, proj_a3, b_pad)
    fox = [(dqkv_fox.reshape(T, 3 * FOX_W), 0, A_FOX), (dflog.reshape(T, LANES), 0, A_FLOG)]
    g_fox = mm_tn_multi(h_t, [piece[0] for piece in fox], 1024, "w_in_grad_fox", BF16)
    second, token = start_exchange([g_fox], "early_exchange_b")

    (dqkv_dil,) = attn_bwd("dil", proj_b3, do_dil.reshape(B, S, DIL_W), o_dil, lse_dil, S, mask=dilated, rope=rope,
                           token=token)
    dil = [(dqkv_dil.reshape(T, 3 * DIL_W), 1, B_DIL)]
    g_dil = mm_tn_multi(h_t, [piece[0] for piece in dil], 1024, "w_in_grad_dil", BF16)
    third, token = start_exchange([g_dil], "early_exchange_c")

    dmq, dmk, dmv = attn_bwd("mem", proj_b3, do_mem.reshape(B, S, MEM_W), o_mem, lse_mem, S, kv=mkv3, token=token)
    mq = [(dmq.reshape(T, MEM_W), 1, B_MQ)]
    g_mq = mm_tn_multi(h_t, [piece[0] for piece in mq], 1024, "w_in_grad_mq", BF16)
    dmkv = jnp.concatenate([dmk, dmv], axis=2).reshape(B * MEM_LEN, 2 * MEM_W)
    g_kv = mm_tn_multi(mh_t, [dmkv], B * MEM_LEN, "w_kv_grad", BF16)
    fourth, token = start_exchange([g_mq, g_kv], "early_exchange_d")

    grad_x, dng = in_proj_bwd_rms(gates + fox + dil + mq, (w_in_a, w_in_b), xt, norm_g, dx2, 256, token)
    dmh = mm_nt(dmkv, w_kv, B * MEM_LEN, D, "mem_kv_bwd")
    _, dmng = rms_bwd(memt, mem_norm_g, dmh, None, B * MEM_LEN, "rms_mem_bwd")

    small = jnp.concatenate([dng[0:1], dmng[0:1], small_out[0:1], _pad_row(db_part[0:1], D), small_out[1:2],
                             jnp.zeros((3, D), F32)], axis=0)
    early = [(first, dqkv_fox), (second, dqkv_dil), (third, dmq), (fourth, grad_x)]
    return grad_x.reshape(B, S, D), early, small


def kernel(x, mem, norm_g, w_in, b_forget, mem_norm_g, w_mem_kv, w_out, final_norm_g, loss_target, m_norm_g, m_w_in, m_b_forget, m_mem_norm_g, m_w_mem_kv, m_w_out, m_final_norm_g, v_norm_g, v_w_in, v_b_forget, v_mem_norm_g, v_w_mem_kv, v_w_out, v_final_norm_g):
    D = D_MODEL
    shard_a, shard_b = _split_cols(_pack_cols(w_in).astype(BF16).reshape(w_in.shape[1], PW))
    (w_in_a,) = weight_gather([shard_a])
    gather, gather_token = early_exchange_start(
        [shard_b, w_mem_kv[0].astype(BF16), w_out[0].astype(BF16)], "late_gather", gather=True, after=w_in_a,
        relations=_SIBLING_AND_SAME_CORES)

    def late_weights(after):
        _, gathered = early_exchange_wait(gather, after, "late_gather_wait")
        return pass_on_to_sibling(gathered, gather["rows"], "late_gather_pass")

    grad_x, early, small = local_grads(
        x, mem, norm_g, b_forget, mem_norm_g, final_norm_g, loss_target, w_in_a, gather_token, late_weights,
        early_exchange_start)

    (first, after_first), (second, after_second), (third, after_third), (fourth, after_fourth) = early
    (src_gates, src_out), (land_gates, land_out) = early_exchange_wait(first, after_first, "early_wait_a")
    (src_fox,), (land_fox,) = early_exchange_wait(second, after_second, "early_wait_b")
    (src_dil,), (land_dil,) = early_exchange_wait(third, after_third, "early_wait_c")
    (src_mq, src_kv), (land_mq, land_kv) = early_exchange_wait(fourth, after_fourth, "early_wait_d")
    gates = slot_sum8(src_gates, land_gates, 128, "sum_w_in_gates")
    gw_out = slot_sum8(src_out, land_out, 256, "sum_w_out")
    fox = slot_sum8(src_fox, land_fox, 128, "sum_w_in_fox")
    dil = slot_sum8(src_dil, land_dil, 128, "sum_w_in_dil")
    mq = slot_sum8(src_mq, land_mq, 128, "sum_w_in_mq")
    gw_kv = slot_sum8(src_kv, land_kv, 128, "sum_w_kv")

    tot = small_all_reduce(small)
    gw_in = _unpack_cols(jnp.concatenate(
        [fox[:, :3 * FOX_W], gates[:, :FOX_W], dil, gates[:, FOX_W:FOX_W + DIL_W], mq,
         gates[:, FOX_W + DIL_W:], fox[:, 3 * FOX_W:]], axis=1)[None])

    loss = tot[4, 0]
    g_norm, g_mem_norm, g_final, g_b = tot[0:1], tot[1:2], tot[2], tot[3:4, :FOX_HEADS]

    def rows8(*rows):
        rows = [r.reshape(1, -1) for r in rows]
        rows = [_pad_row(r, D) for r in rows]
        return jnp.concatenate(rows + [jnp.zeros((8 - len(rows), D), F32)], axis=0)

    sw = rows8(norm_g, mem_norm_g, final_norm_g, b_forget)
    sm = rows8(m_norm_g, m_mem_norm_g, m_final_norm_g, m_b_forget)
    sv = rows8(v_norm_g, v_mem_norm_g, v_final_norm_g, v_b_forget)
    d_s, m_s, v_s = adamw(sw, tot, sm, sv, 8, "adamw_small")
    d_in, m_in, v_in = adamw(w_in, gw_in, m_w_in, v_w_in, 32, "adamw_w_in")
    d_kv, m_kv, v_kv = adamw(w_mem_kv[0], gw_kv, m_w_mem_kv[0], v_w_mem_kv[0], 128, "adamw_w_kv")
    d_out, m_out, v_out = adamw(w_out[0], gw_out, m_w_out[0], v_w_out[0], 256, "adamw_w_out")

    def small_outs(t):
        return t[0:1], t[3:4, :FOX_HEADS], t[1:2], t[2]

    grads = (g_norm, gw_in, g_b, g_mem_norm, gw_kv[None], gw_out[None], g_final)
    outs = []
    for t, big in ((d_s, (d_in, d_kv, d_out)), (m_s, (m_in, m_kv, m_out)), (v_s, (v_in, v_kv, v_out))):
        n, b, mn, f = small_outs(t)
        outs += [n, big[0], b, mn, big[1][None], big[2][None], f]
    return (loss, grad_x, *grads, *outs)
```

```python
import math

import numpy as np
import jax
import jax.numpy as jnp
from jax import lax
from jax.experimental import pallas as pl
from jax.experimental.pallas import tpu as pltpu

F32 = jnp.float32
BF16 = jnp.bfloat16

D_MODEL = 1024
HEAD_DIM = 64
FOX_HEADS = 12
DIL_HEADS = 12
MEM_HEADS = 4
MEM_HEAD_DIM = 128
MEM_LEN = 256
FOX_W = FOX_HEADS * HEAD_DIM
DIL_W = DIL_HEADS * HEAD_DIM
MEM_W = MEM_HEADS * MEM_HEAD_DIM
MIX_W = FOX_W + DIL_W + MEM_W
DILATIONS = ((128, 1), (512, 4), (2048, 16))
ROPE_THETA = 500000.0
ROPE_DIM = HEAD_DIM // 4
RMS_EPS = 1e-6
NEG_INF = -1e30
IN_W = 4 * FOX_W + FOX_HEADS + 4 * DIL_W + 2 * MEM_W

ADAM_LR = 0.001
ADAM_B1 = 0.9
ADAM_B2 = 0.999
ADAM_EPS = 1e-08
ADAM_WD = 0.01
ADAM_STEP = 10

N_DEV = 8
LANES = 128
PAIR_W = 3 * LANES
TQ = 256
TK = 256

O_FQ, O_FK, O_FV, O_FG = 0, FOX_W, 2 * FOX_W, 3 * FOX_W
O_FLOG = 4 * FOX_W
O_DQ = O_FLOG + FOX_HEADS
O_DK, O_DV, O_DG = O_DQ + DIL_W, O_DQ + 2 * DIL_W, O_DQ + 3 * DIL_W
O_MQ = O_DQ + 4 * DIL_W
O_MG = O_MQ + MEM_W
P_FOX = 0
P_FG = P_FOX + 3 * FOX_W
P_DIL = P_FG + FOX_W
P_DG = P_DIL + 3 * DIL_W
P_MQ = P_DG + DIL_W
P_MG = P_MQ + MEM_W
P_FLOG = P_MG + MEM_W
PW = P_FLOG + LANES
A_FOX = 0
A_FLOG = A_FOX + 3 * FOX_W
PA = A_FLOG + LANES
B_FG = 0
B_DG = B_FG + FOX_W
B_DIL = B_DG + DIL_W
B_MQ = B_DIL + 3 * DIL_W
B_MG = -(-(B_MQ + MEM_W) // MEM_W) * MEM_W
PB = B_MG + MEM_W

VMEM_LIMIT = 56 * 1024 * 1024


def _pack_pieces():
    pieces = []
    for base in (O_FQ, O_DQ):
        seg = []
        for hp in range(FOX_HEADS // 2):
            for part in range(3):
                seg.append((base + part * FOX_W + hp * LANES, LANES))
        pieces.append(seg)
    fox, dil = pieces
    return fox + [(O_FG, FOX_W)] + dil + [(O_DG, DIL_W), (O_MQ, MEM_W), (O_MG, MEM_W), (O_FLOG, FOX_HEADS)]


def _pack_cols(w):
    parts = [w[..., s:s + n] for s, n in _pack_pieces()]
    parts.append(jnp.zeros(w.shape[:-1] + (LANES - FOX_HEADS,), w.dtype))
    return jnp.concatenate(parts, axis=-1)


def _split_cols(wp):
    def cut(start, width):
        return wp[..., start:start + width]

    group_a = jnp.concatenate([cut(P_FOX, 3 * FOX_W), cut(P_FLOG, LANES)], axis=-1)
    pad = jnp.zeros(wp.shape[:-1] + (B_MG - B_MQ - MEM_W,), wp.dtype)
    group_b = jnp.concatenate([cut(P_FG, FOX_W), cut(P_DG, DIL_W), cut(P_DIL, 3 * DIL_W), cut(P_MQ, MEM_W), pad,
                               cut(P_MG, MEM_W)], axis=-1)
    return group_a, group_b


def _unpack_cols(g):
    runs = []
    pos = 0
    for s, n in _pack_pieces():
        runs.append((s, n, pos))
        pos += n
    runs.sort()
    return jnp.concatenate([g[..., p:p + n] for s, n, p in runs], axis=-1)


def _params(sem=None, **kw):
    return pltpu.CompilerParams(dimension_semantics=sem, vmem_limit_bytes=VMEM_LIMIT, **kw)


def _mesh_pos():
    return lax.axis_index("x"), lax.axis_index("y"), lax.axis_index("c")


def _flip(v, d):
    return 1 - v if d else v


_RELATIONS = [(dx, dy, dc) for dx in (0, 1) for dy in (0, 1) for dc in (0, 1)][1:]
_SIBLING_AND_SAME_CORES = [(0, 0, 1), (1, 0, 0), (0, 1, 0), (1, 1, 0)]


def weight_gather(shards):
    n_arr = len(shards)
    rows = [s.shape[0] for s in shards]

    def body(*refs):
        in_refs = refs[:n_arr]
        out_refs = refs[n_arr:2 * n_arr]
        send_sems, recv_sems, local_sems = refs[2 * n_arr:]
        x, y, c = _mesh_pos()
        me, sibling = (x, y, c), (x, y, 1 - c)
        x_nbr, y_nbr, diag = (1 - x, y, c), (x, 1 - y, c), (1 - x, 1 - y, c)
        north = c == 1
        relay_from = (jnp.where(north, 1 - x, x), jnp.where(north, y, 1 - y), c)
        relay_to = (jnp.where(north, x, 1 - x), jnp.where(north, 1 - y, y), c)
        k_from = jnp.where(north, 1, 2)
        k_to = 3 - k_from

        def block(a, pos):
            px, py, pc = pos
            return out_refs[a].at[pl.ds((4 * px + 2 * py + pc) * rows[a], rows[a]), :]

        def copy(a, k, blk, to, src=None):
            return pltpu.make_async_remote_copy(
                src_ref=block(a, blk) if src is None else src, dst_ref=block(a, blk),
                send_sem=send_sems.at[a, k], recv_sem=recv_sems.at[a, k],
                device_id=to, device_id_type=pl.DeviceIdType.MESH)

        started = []
        mine = []
        for a in range(n_arr):
            cp = pltpu.make_async_copy(in_refs[a], block(a, me), local_sems.at[a])
            cp.start()
            mine.append(cp)
            first = [copy(a, 0, me, sibling, src=in_refs[a]), copy(a, 1, me, x_nbr, src=in_refs[a]),
                     copy(a, 2, me, y_nbr, src=in_refs[a])]
            for cp in first:
                cp.start()
            started += first
        for a in range(n_arr):
            copy(a, k_from, relay_from, me).wait_recv()
            second_hop = copy(a, 3, relay_from, relay_to)
            second_hop.start()
            passed = copy(a, 3 + k_from, relay_from, sibling)
            passed.start()
            started += [second_hop, passed]
        for a in range(n_arr):
            copy(a, k_to, relay_to, me).wait_recv()
            passed = copy(a, 3 + k_to, relay_to, sibling)
            passed.start()
            started.append(passed)
        for a in range(n_arr):
            copy(a, 3, diag, me).wait_recv()
            passed = copy(a, 6, diag, sibling)
            passed.start()
            started.append(passed)
        for a in range(n_arr):
            copy(a, 0, sibling, me).wait_recv()
            for k, chip in ((4, x_nbr), (5, y_nbr), (6, diag)):
                copy(a, k, (chip[0], chip[1], 1 - c), me).wait_recv()
        for cp in started:
            cp.wait_send()
        for cp in mine:
            cp.wait()

    any_spec = pl.BlockSpec(memory_space=pl.ANY)
    return pl.pallas_call(
        body, name="weight_gather",
        out_shape=[jax.ShapeDtypeStruct((N_DEV * s.shape[0], s.shape[1]), s.dtype) for s in shards],
        in_specs=[any_spec] * n_arr, out_specs=[any_spec] * n_arr,
        scratch_shapes=[pltpu.SemaphoreType.DMA((n_arr, 7)), pltpu.SemaphoreType.DMA((n_arr, 7)),
                        pltpu.SemaphoreType.DMA((n_arr,))],
    )(*shards)


_OTHER_CHIPS = [(1, 0), (0, 1), (1, 1)]


def small_all_reduce(small):
    vmem_spec = pl.BlockSpec(memory_space=pltpu.VMEM)

    def body(small_ref, tot_ref, land, send_sems, recv_sems):
        x, y, c = _mesh_pos()
        me = 4 * x + 2 * y + c
        land[me] = small_ref[...]
        sends, recvs = [], []
        for j, (dx, dy, dc) in enumerate(_RELATIONS):
            px, py, pc = _flip(x, dx), _flip(y, dy), _flip(c, dc)
            common = dict(send_sem=send_sems.at[j], recv_sem=recv_sems.at[j],
                          device_id=(px, py, pc), device_id_type=pl.DeviceIdType.MESH)
            sends.append(pltpu.make_async_remote_copy(src_ref=small_ref, dst_ref=land.at[me], **common))
            recvs.append(pltpu.make_async_remote_copy(src_ref=small_ref, dst_ref=land.at[4 * px + 2 * py + pc], **common))
        for cp in sends:
            cp.start()
        for cp in recvs:
            cp.wait_recv()
        for cp in sends:
            cp.wait_send()
        tot = land[0]
        for d in range(1, N_DEV):
            tot = tot + land[d]
        tot_ref[...] = tot

    return pl.pallas_call(
        body, name="small_sum", out_shape=jax.ShapeDtypeStruct(small.shape, small.dtype),
        in_specs=[vmem_spec], out_specs=vmem_spec,
        scratch_shapes=[pltpu.VMEM((N_DEV,) + small.shape, small.dtype),
                        pltpu.SemaphoreType.DMA((len(_RELATIONS),)), pltpu.SemaphoreType.DMA((len(_RELATIONS),))],
    )(small)


_HBM = pl.BlockSpec(memory_space=pltpu.HBM)
_SEM = pl.BlockSpec(memory_space=pltpu.SEMAPHORE)
_EFFECT = pltpu.SideEffectType.DATAFLOW_SIDE_EFFECTING


def _early_copies(src_refs, land_refs, send_sems, recv_sems, rows, gather, relations):
    x, y, c = _mesh_pos()
    me = 4 * x + 2 * y + c
    copies = []
    for a in range(len(src_refs)):
        for dx, dy, dc in relations:
            px, py, pc = _flip(x, dx), _flip(y, dy), _flip(c, dc)
            peer = 4 * px + 2 * py + pc
            copies.append(pltpu.make_async_remote_copy(
                src_ref=src_refs[a] if gather else src_refs[a].at[pl.ds(peer * rows[a], rows[a]), :],
                dst_ref=land_refs[a].at[pl.ds(me * rows[a], rows[a]), :],
                send_sem=send_sems[a], recv_sem=recv_sems[a],
                device_id=(px, py, pc), device_id_type=pl.DeviceIdType.MESH))
    return copies


def early_exchange_start(srcs, name, gather=False, after=None, relations=_RELATIONS):
    n = len(srcs)
    if gather:
        rows = [s.shape[0] for s in srcs]
        lands = [lax.empty((N_DEV * r, s.shape[1]), s.dtype) for r, s in zip(rows, srcs)]
    else:
        rows = [s.shape[0] // N_DEV for s in srcs]
        lands = [lax.empty(s.shape, s.dtype) for s in srcs]

    extra = [] if after is None else [after]

    def body(*refs):
        src_refs, land_refs = refs[:n], refs[n:2 * n]
        first_sem = 2 * n + len(extra)
        send_sems, recv_sems = refs[first_sem:first_sem + n], refs[first_sem + n:first_sem + 2 * n]
        token = refs[-1]
        for cp in _early_copies(src_refs, land_refs, send_sems, recv_sems, rows, gather, relations):
            cp.start()
        token[...] = jnp.zeros_like(token)

    hbm = lambda a: pltpu.HBM(a.shape, a.dtype)
    outs = pl.pallas_call(
        body, name=name,
        out_shape=[pltpu.SemaphoreType.DMA(())] * (2 * n)
        + [hbm(a) for a in srcs] + [hbm(a) for a in lands] + [jax.ShapeDtypeStruct((8, LANES), F32)],
        in_specs=[_HBM] * (2 * n) + [pl.BlockSpec(memory_space=pl.ANY)] * len(extra),
        out_specs=[_SEM] * (2 * n) + [_HBM] * (2 * n) + [pl.BlockSpec(memory_space=pltpu.VMEM)],
        input_output_aliases={i: 2 * n + i for i in range(2 * n)},
        compiler_params=pltpu.CompilerParams(has_side_effects=_EFFECT),
    )(*[pltpu.with_memory_space_constraint(a, pltpu.HBM) for a in list(srcs) + lands], *extra)
    handle = dict(sems=outs[:2 * n], srcs=outs[2 * n:3 * n], lands=outs[3 * n:4 * n], rows=rows,
                  copies=len(relations))
    return handle, outs[-1]


def early_exchange_wait(handle, after, name):
    n = len(handle["srcs"])
    rows = handle["rows"]

    def body(*refs):
        src_refs, land_refs = refs[:n], refs[n:2 * n]
        send_sems, recv_sems = refs[2 * n:3 * n], refs[3 * n:4 * n]
        x, y, c = _mesh_pos()
        for a in range(n):
            span = pl.ds(0, handle["copies"] * rows[a])
            all_copies = pltpu.make_async_remote_copy(
                src_ref=land_refs[a].at[span, :], dst_ref=land_refs[a].at[span, :],
                send_sem=send_sems[a], recv_sem=recv_sems[a],
                device_id=(x, y, c), device_id_type=pl.DeviceIdType.MESH)
            all_copies.wait_send()
            all_copies.wait_recv()

    hbm = lambda a: pltpu.HBM(a.shape, a.dtype)
    ins = list(handle["srcs"]) + list(handle["lands"])
    outs = pl.pallas_call(
        body, name=name,
        out_shape=[hbm(a) for a in ins],
        in_specs=[_HBM] * (2 * n) + [_SEM] * (2 * n) + [pl.BlockSpec(memory_space=pl.ANY)],
        out_specs=[_HBM] * (2 * n),
        input_output_aliases={i: i for i in range(2 * n)},
        compiler_params=pltpu.CompilerParams(has_side_effects=_EFFECT),
    )(*ins, *handle["sems"], after)
    return outs[:n], outs[n:]


def pass_on_to_sibling(lands, shards, rows, name):
    n = len(lands)

    def body(*refs):
        shard_refs, land_refs = refs[n:2 * n], refs[2 * n:3 * n]
        send_sems, recv_sems, local_sems = refs[3 * n:]
        x, y, c = _mesh_pos()

        def block(a, slot):
            return land_refs[a].at[pl.ds(slot * rows[a], rows[a]), :]

        copies = []
        for a in range(n):
            for k, (dx, dy) in enumerate(_OTHER_CHIPS):
                blk = block(a, 4 * _flip(x, dx) + 2 * _flip(y, dy) + c)
                copies.append(pltpu.make_async_remote_copy(
                    src_ref=blk, dst_ref=blk, send_sem=send_sems.at[a, k], recv_sem=recv_sems.at[a, k],
                    device_id=(x, y, 1 - c), device_id_type=pl.DeviceIdType.MESH))
        own = [pltpu.make_async_copy(shard_refs[a], block(a, 4 * x + 2 * y + c), local_sems.at[a]) for a in range(n)]
        for cp in copies + own:
            cp.start()
        for cp in copies:
            cp.wait_recv()
        for cp in copies:
            cp.wait_send()
        for cp in own:
            cp.wait()

    any_spec = pl.BlockSpec(memory_space=pl.ANY)
    return pl.pallas_call(
        body, name=name,
        out_shape=[jax.ShapeDtypeStruct(a.shape, a.dtype) for a in lands],
        in_specs=[any_spec] * (2 * n), out_specs=[any_spec] * n,
        input_output_aliases={i: i for i in range(n)},
        scratch_shapes=[pltpu.SemaphoreType.DMA((n, len(_OTHER_CHIPS))), pltpu.SemaphoreType.DMA((n, len(_OTHER_CHIPS))),
                        pltpu.SemaphoreType.DMA((n,))],
    )(*lands, *shards)


def slot_sum8(src, land, tr, name):
    rows, cols = land.shape[0] // N_DEV, land.shape[1]
    x, y, c = _mesh_pos()
    me = (4 * x + 2 * y + c).astype(jnp.int32).reshape(1)

    def body(me_ref, src_ref, land_ref, o_ref):
        acc = None
        for d in range(N_DEV):
            term = jnp.where(d == me_ref[0], src_ref[0], land_ref[d]).astype(F32)
            acc = term if acc is None else acc + term
        o_ref[...] = acc

    return pl.pallas_call(
        body, name=name,
        grid_spec=pltpu.PrefetchScalarGridSpec(
            num_scalar_prefetch=1, grid=(rows // tr,),
            in_specs=[pl.BlockSpec((1, tr, cols), lambda i, w: (w[0], i, 0)),
                      pl.BlockSpec((N_DEV, tr, cols), lambda i, w: (0, i, 0))],
            out_specs=pl.BlockSpec((tr, cols), lambda i, w: (i, 0))),
        out_shape=jax.ShapeDtypeStruct((rows, cols), F32),
        compiler_params=_params(("arbitrary",)),
    )(me, src.reshape(N_DEV, rows, cols), land.reshape(N_DEV, rows, cols))


def mm_tn_multi(a_t, bs, tt, name, out_dtype=F32):
    K, T = a_t.shape
    widths = [b.shape[1] for b in bs]
    steps = T // tt

    def body(a_ref, *rest):
        b_refs, o_ref, acc = rest[:-2], rest[-2], rest[-1]

        @pl.when(pl.program_id(0) == 0)
        def _():
            acc[...] = jnp.zeros(acc.shape, F32)

        av = a_ref[...]
        col = 0
        for b_ref, w in zip(b_refs, widths):
            acc[:, col:col + w] += jnp.dot(av, b_ref[...], preferred_element_type=F32)
            col += w

        @pl.when(pl.program_id(0) == steps - 1)
        def _():
            o_ref[...] = acc[...].astype(out_dtype)

    return pl.pallas_call(
        body, name=name, grid=(steps,),
        in_specs=[pl.BlockSpec((K, tt), lambda t: (0, t))] + [pl.BlockSpec((tt, w), lambda t: (t, 0)) for w in widths],
        out_specs=pl.BlockSpec((K, sum(widths)), lambda t: (0, 0)),
        out_shape=jax.ShapeDtypeStruct((K, sum(widths)), out_dtype),
        scratch_shapes=[pltpu.VMEM((K, sum(widths)), F32)],
        compiler_params=_params(("arbitrary",)),
    )(a_t, *bs)


def rms_fwd(x, g, tm, name, with_transpose=False):
    M, K = x.shape

    def body(x_ref, g_ref, o_ref, *t_ref):
        xv = x_ref[...]
        r = lax.rsqrt(jnp.mean(xv * xv, axis=-1, keepdims=True) + RMS_EPS)
        h = ((xv * r) * g_ref[...]).astype(BF16)
        o_ref[...] = h
        if with_transpose:
            t_ref[0][...] = h.T

    out_specs = [pl.BlockSpec((tm, K), lambda i: (i, 0))]
    out_shape = [jax.ShapeDtypeStruct((M, K), BF16)]
    if with_transpose:
        out_specs.append(pl.BlockSpec((K, tm), lambda i: (0, i)))
        out_shape.append(jax.ShapeDtypeStruct((K, M), BF16))
    outs = pl.pallas_call(
        body, name=name, grid=(M // tm,),
        in_specs=[pl.BlockSpec((tm, K), lambda i: (i, 0)), pl.BlockSpec((1, K), lambda i: (0, 0))],
        out_specs=out_specs, out_shape=out_shape,
        compiler_params=_params(("arbitrary",)),
    )(x, g)
    return outs if with_transpose else outs[0]


def rms_bwd(x, g, dh, dres, tm, name):
    M, K = x.shape
    has_res = dres is not None

    def body(*refs):
        if has_res:
            x_ref, g_ref, dh_ref, dres_ref, dx_ref, dg_ref = refs
        else:
            x_ref, g_ref, dh_ref, dx_ref, dg_ref = refs
        xv = x_ref[...]
        r = lax.rsqrt(jnp.mean(xv * xv, axis=-1, keepdims=True) + RMS_EPS)
        xn = xv * r
        dhv = dh_ref[...]
        dxn = dhv * g_ref[...]
        dx = r * (dxn - xn * jnp.mean(dxn * xn, axis=-1, keepdims=True))
        if has_res:
            dx = dx + dres_ref[...]
        dx_ref[...] = dx
        part = jnp.sum(dhv * xn, axis=0, keepdims=True)
        row = lax.broadcasted_iota(jnp.int32, (8, K), 0)
        upd = jnp.where(row == 0, part, 0.0)

        @pl.when(pl.program_id(0) == 0)
        def _():
            dg_ref[...] = upd

        @pl.when(pl.program_id(0) != 0)
        def _():
            dg_ref[...] += upd

    row_spec = pl.BlockSpec((tm, K), lambda i: (i, 0))
    ins = [x, g, dh] + ([dres] if has_res else [])
    in_specs = [row_spec, pl.BlockSpec((1, K), lambda i: (0, 0)), row_spec] + ([row_spec] if has_res else [])
    return pl.pallas_call(
        body, name=name, grid=(M // tm,),
        in_specs=in_specs,
        out_specs=[row_spec, pl.BlockSpec((8, K), lambda i: (0, 0))],
        out_shape=[jax.ShapeDtypeStruct((M, K), F32), jax.ShapeDtypeStruct((8, K), F32)],
        compiler_params=_params(("arbitrary",)),
    )(*ins)


def mm_nn(a, b, tm, tn, name, token=None):
    M, K = a.shape
    N = b.shape[1]
    extra = [] if token is None else [token]

    def body(a_ref, b_ref, *rest):
        rest[-1][...] = jnp.dot(a_ref[...], b_ref[...], preferred_element_type=F32)

    return pl.pallas_call(
        body, name=name, grid=(N // tn, M // tm),
        in_specs=[pl.BlockSpec((tm, K), lambda j, i: (i, 0)), pl.BlockSpec((K, tn), lambda j, i: (0, j))]
        + [pl.BlockSpec(t.shape, lambda j, i: (0, 0)) for t in extra],
        out_specs=pl.BlockSpec((tm, tn), lambda j, i: (i, j)),
        out_shape=jax.ShapeDtypeStruct((M, N), F32),
        compiler_params=_params(("arbitrary", "arbitrary")),
    )(a, b, *extra)


def mm_nt(a, b, tm, tk, name):
    M, K = a.shape
    N = b.shape[0]

    def body(a_ref, b_ref, o_ref):
        part = lax.dot_general(a_ref[...], b_ref[...], (((1,), (1,)), ((), ())), preferred_element_type=F32)

        @pl.when(pl.program_id(1) == 0)
        def _():
            o_ref[...] = part

        @pl.when(pl.program_id(1) != 0)
        def _():
            o_ref[...] += part

    return pl.pallas_call(
        body, name=name, grid=(M // tm, K // tk),
        in_specs=[pl.BlockSpec((tm, tk), lambda i, k: (i, k)), pl.BlockSpec((N, tk), lambda i, k: (0, k))],
        out_specs=pl.BlockSpec((tm, N), lambda i, k: (i, 0)),
        out_shape=jax.ShapeDtypeStruct((M, N), F32),
        compiler_params=_params(("arbitrary", "arbitrary")),
    )(a, b)


def in_proj_bwd_rms(pieces, ws, x, g, dres, tm, token):
    M, N = x.shape

    def body(*refs):
        n = len(pieces)
        p_refs, w_refs = refs[:n], refs[n:n + len(ws)]
        x_ref, g_ref, dres_ref, _, dx_ref, dg_ref = refs[n + len(ws):]
        dh = None
        for p_ref, (arr, group, col) in zip(p_refs, pieces):
            part = lax.dot_general(p_ref[...], w_refs[group][:, col:col + arr.shape[1]], (((1,), (1,)), ((), ())),
                                   preferred_element_type=F32)
            dh = part if dh is None else dh + part
        xv = x_ref[...]
        r = lax.rsqrt(jnp.mean(xv * xv, axis=-1, keepdims=True) + RMS_EPS)
        xn = xv * r
        dxn = dh * g_ref[...]
        dx_ref[...] = r * (dxn - xn * jnp.mean(dxn * xn, axis=-1, keepdims=True)) + dres_ref[...]
        row = lax.broadcasted_iota(jnp.int32, (8, N), 0)
        upd = jnp.where(row == 0, jnp.sum(dh * xn, axis=0, keepdims=True), 0.0)

        @pl.when(pl.program_id(0) == 0)
        def _():
            dg_ref[...] = upd

        @pl.when(pl.program_id(0) != 0)
        def _():
            dg_ref[...] += upd

    row_spec = pl.BlockSpec((tm, N), lambda i: (i, 0))
    return pl.pallas_call(
        body, name="in_proj_bwd", grid=(M // tm,),
        in_specs=[pl.BlockSpec((tm, arr.shape[1]), lambda i: (i, 0)) for arr, _, _ in pieces]
        + [pl.BlockSpec(w.shape, lambda i: (0, 0)) for w in ws]
        + [row_spec, pl.BlockSpec((1, N), lambda i: (0, 0)), row_spec, pl.BlockSpec(token.shape, lambda i: (0, 0))],
        out_specs=[row_spec, pl.BlockSpec((8, N), lambda i: (0, 0))],
        out_shape=[jax.ShapeDtypeStruct((M, N), F32), jax.ShapeDtypeStruct((8, N), F32)],
        compiler_params=_params(("arbitrary",)),
    )(*[arr for arr, _, _ in pieces], *ws, x, g, dres, token)


def _log_sigmoid(z):
    return jnp.minimum(z, 0.0) - jnp.log(1.0 + jnp.exp(-jnp.abs(z)))


def _tri(n, lower):
    r = lax.broadcasted_iota(jnp.int32, (n, n), 0)
    c = lax.broadcasted_iota(jnp.int32, (n, n), 1)
    return jnp.where((r >= c) if lower else (r <= c), 1.0, 0.0).astype(F32)


def fox_gate(proj3, b_pad):
    B, S, _ = proj3.shape
    nblk = S // TK

    def body(f_ref, b_ref, o_ref):
        tri = _tri(TK, True)
        carry = jnp.zeros((1, LANES), F32)
        for n in range(nblk):
            z = f_ref[0, n * TK:(n + 1) * TK, :] + b_ref[...]
            logf = _log_sigmoid(z)
            cs = jnp.dot(tri, logf, preferred_element_type=F32, precision=lax.Precision.HIGHEST) + carry
            carry = cs[TK - 1:TK, :]
            o_ref[0, n * TK:(n + 1) * TK, :] = -cs

    return pl.pallas_call(
        body, name="fox_gate", grid=(B,),
        in_specs=[pl.BlockSpec((1, S, LANES), lambda b: (b, 0, A_FLOG // LANES)),
                  pl.BlockSpec((1, LANES), lambda b: (0, 0))],
        out_specs=pl.BlockSpec((1, S, LANES), lambda b: (b, 0, 0)),
        out_shape=jax.ShapeDtypeStruct((B, S, LANES), F32),
        compiler_params=_params(("arbitrary",)),
    )(proj3, b_pad)


def fox_gate_bwd(drow, dneg, proj3, b_pad):
    B, S, _ = proj3.shape
    nblk = S // TK

    def body(d_ref, r_ref, f_ref, b_ref, o_ref, db_ref):
        tri = _tri(TK, False)
        lane = lax.broadcasted_iota(jnp.int32, (TK, LANES), 1)
        carry = jnp.zeros((1, LANES), F32)
        dbsum = jnp.zeros((1, LANES), F32)
        for n in reversed(range(nblk)):
            dk_side = None
            for hp in range(FOX_HEADS // 2):
                two = jnp.where(lane < 2, r_ref[0, n * TK:(n + 1) * TK, hp * LANES:(hp + 1) * LANES], 0.0)
                two = pltpu.roll(two, 2 * hp, 1) if hp else two
                dk_side = two if dk_side is None else dk_side + two
            dc = jnp.where(lane < FOX_HEADS, d_ref[0, :, n * TK:(n + 1) * TK].T - dk_side, 0.0)
            rs = jnp.dot(tri, dc, preferred_element_type=F32, precision=lax.Precision.HIGHEST) + carry
            carry = rs[0:1, :]
            z = f_ref[0, n * TK:(n + 1) * TK, :] + b_ref[...]
            dz = rs * (1.0 / (1.0 + jnp.exp(z)))
            o_ref[0, n * TK:(n + 1) * TK, :] = dz.astype(BF16)
            dbsum = dbsum + jnp.sum(dz, axis=0, keepdims=True)
        row = lax.broadcasted_iota(jnp.int32, (8, LANES), 0)
        upd = jnp.where(row == 0, dbsum, 0.0)

        @pl.when(pl.program_id(0) == 0)
        def _():
            db_ref[...] = upd

        @pl.when(pl.program_id(0) != 0)
        def _():
            db_ref[...] += upd

    return pl.pallas_call(
        body, name="fox_gate_bwd", grid=(B,),
        in_specs=[pl.BlockSpec((1, LANES, S), lambda b: (b, 0, 0)),
                  pl.BlockSpec((1, S, FOX_W), lambda b: (b, 0, 0)),
                  pl.BlockSpec((1, S, LANES), lambda b: (b, 0, A_FLOG // LANES)),
                  pl.BlockSpec((1, LANES), lambda b: (0, 0))],
        out_specs=[pl.BlockSpec((1, S, LANES), lambda b: (b, 0, 0)), pl.BlockSpec((8, LANES), lambda b: (0, 0))],
        out_shape=[jax.ShapeDtypeStruct((B, S, LANES), BF16), jax.ShapeDtypeStruct((8, LANES), F32)],
        compiler_params=_params(("arbitrary",)),
    )(drow, dneg, proj3, b_pad)


def _rope_tables(S):
    half = ROPE_DIM // 2
    f32 = np.float32
    pos = np.arange(S, dtype=f32)
    inv_freq = f32(1.0) / np.power(f32(ROPE_THETA), np.arange(0, ROPE_DIM, 2, dtype=f32) / f32(ROPE_DIM)).astype(f32)
    ang = (pos[:, None] * inv_freq[None, :]).astype(f32).astype(np.float64)
    cos, sin = np.cos(ang).astype(f32), np.sin(ang).astype(f32)
    one = np.ones((S, HEAD_DIM - ROPE_DIM), f32)
    zero = np.zeros((S, HEAD_DIM - ROPE_DIM), f32)
    zh = np.zeros((S, half), f32)
    c = np.concatenate([cos, cos, one], axis=1)
    s1 = np.concatenate([-sin, zh, zero], axis=1)
    s2 = np.concatenate([zh, sin, zero], axis=1)
    return tuple(jnp.asarray(np.concatenate([t, t], axis=1)) for t in (c, s1, s2))


_HALF_ROPE = ROPE_DIM // 2


def _rope(t, c, s1, s2):
    return t * c + pltpu.roll(t, LANES - _HALF_ROPE, 1) * s1 + pltpu.roll(t, _HALF_ROPE, 1) * s2


def _rope_bwd(d, c, s1, s2):
    return d * c + pltpu.roll(d * s1, _HALF_ROPE, 1) + pltpu.roll(d * s2, LANES - _HALF_ROPE, 1)


def _scale_parts(scale):
    m, _ = math.frexp(scale)
    return (scale, None) if m == 0.5 else (None, scale)


def _log_masks(S, kind):
    nd = 1 if kind == "causal" else S // TQ
    a = np.arange(TQ)[:, None]
    b = np.arange(TK)[None, :]
    out = np.zeros((nd, TQ, TK), np.float32)
    for d in range(nd):
        delta = d * TQ + a - b
        if kind == "causal":
            m = (delta >= 0).astype(np.float64)
        else:
            m = sum(((delta >= 0) & (delta % dil == 0) & (delta <= w)).astype(np.float64) for w, dil in DILATIONS)
        out[d] = np.where(m > 0, np.log(np.maximum(m, 1.0)), NEG_INF)
    return jnp.asarray(out)


def _attn_setup(kind):
    pair = kind != "mem"
    e_dim = HEAD_DIM if pair else MEM_HEAD_DIM
    q_fold, s_scale = _scale_parts(1.0 / math.sqrt(e_dim))
    return dict(pair=pair, col0={"fox": A_FOX, "dil": B_DIL, "mem": B_MQ}[kind],
                n_blocks=FOX_HEADS // 2 if pair else MEM_HEADS, q_fold=q_fold, s_scale=s_scale,
                nh=2 if pair else 1)


def _cat(parts, axis):
    return parts[0] if len(parts) == 1 else jnp.concatenate(parts, axis=axis)


def _log_masks_t(S, kind):
    return jnp.swapaxes(_log_masks(S, kind), 1, 2)


def _head_rows(hh, pair):
    row = lax.broadcasted_iota(jnp.int32, (LANES, 1), 0)
    if not pair:
        return row >= 0
    return (row >= HEAD_DIM * hh) & (row < HEAD_DIM * (hh + 1))


def _attn_t_inputs(kind, src, S, negc_cols, mask, rope, kv):
    cfg = _attn_setup(kind)
    col0 = cfg["col0"]
    ins, in_specs = [], []
    if cfg["pair"]:
        ins.append(src)
        in_specs.append(pl.BlockSpec((1, S, PAIR_W), lambda b, h: (b, 0, col0 // PAIR_W + h)))
    else:
        ins += [src, kv, kv]
        in_specs += [pl.BlockSpec((1, S, LANES), lambda b, h: (b, 0, col0 // LANES + h)),
                     pl.BlockSpec((1, MEM_LEN, LANES), lambda b, h: (b, 0, h)),
                     pl.BlockSpec((1, MEM_LEN, LANES), lambda b, h: (b, 0, MEM_HEADS + h))]
    if negc_cols is not None:
        ins.append(negc_cols)
        in_specs.append(pl.BlockSpec((1, S, LANES), lambda b, h: (b, 0, 0)))
    if mask is not None:
        ins.append(mask)
        in_specs.append(pl.BlockSpec(mask.shape, lambda b, h: (0, 0, 0)))
    if rope is not None:
        ins += list(rope)
        in_specs += [pl.BlockSpec((S, LANES), lambda b, h: (0, 0))] * 3
    return ins, in_specs


def _attn_t_prep(cfg, refs, S, Sk, *, qT2s, ks, vs=None, vTs=None, kTs=None, nb=None):
    pair, nh = cfg["pair"], cfg["nh"]
    lane = lax.broadcasted_iota(jnp.int32, (1, LANES), 1)
    rope_refs = refs["rope"]

    def prep_q(n):
        rows = slice(n * TQ, (n + 1) * TQ)
        q = refs["load_q"](rows)
        if rope_refs is not None:
            q = _rope(q, *[t[rows, :] for t in rope_refs])
        if cfg["q_fold"] is not None:
            q = q * cfg["q_fold"]
        qtb = q.astype(BF16).T
        for hh in range(nh):
            qT2s[n, :, hh * TQ:(hh + 1) * TQ] = jnp.where(_head_rows(hh, pair), qtb, jnp.zeros_like(qtb))

    def prep_kv(n):
        rows = slice(n * TK, (n + 1) * TK)
        k, v = refs["load_kv"](rows)
        if rope_refs is not None:
            k = _rope(k, *[t[rows, :] for t in rope_refs])
        kb = k.astype(BF16)
        vb = v.astype(BF16)
        ks[rows, :] = kb
        if vs is not None:
            vs[rows, :] = vb
        if vTs is not None:
            vTs[n] = vb.T
        if kTs is not None:
            kTs[n] = kb.T
        if nb is not None:
            blk = refs["negc"][0, rows, :]
            for hh in range(nh):
                h = 2 * refs["block"] + hh
                col = jnp.sum(jnp.where(lane == h, blk, 0.0), axis=1, keepdims=True)
                nb[hh, rows, :] = jnp.broadcast_to(col, (TK, LANES))

    for n in range(S // TQ):
        prep_q(n)
    for n in range(Sk // TK):
        prep_kv(n)


def _raw_scores_t(cfg, k, qT2):
    sT = jnp.dot(k, qT2, preferred_element_type=F32)
    if cfg["s_scale"] is not None:
        sT = sT * cfg["s_scale"]
    return sT


def _bias_mask_t(cfg, sT, nb, mask_ref, kc, midx):
    nh = cfg["nh"]
    if nb is None and midx is None:
        return sT
    parts = []
    for hh in range(nh):
        t = sT[:, hh * TQ:(hh + 1) * TQ]
        if nb is not None:
            t = t + jnp.concatenate([nb[hh, kc, :]] * (TQ // LANES), axis=1)
        if midx is not None:
            t = t + mask_ref[midx]
        parts.append(t)
    return _cat(parts, 1)


def _tile_pairs(kind, nq, nk):
    if kind == "mem":
        return [(i, j) for i in range(nq) for j in range(nk)], (lambda i, j: None)
    pairs = [(i, j) for i in range(nq) for j in range(i + 1)]
    if kind == "fox":
        return pairs, (lambda i, j: 0 if j == i else None)
    return pairs, (lambda i, j: i - j)


def attn_fwd(kind, src, S, *, negc_cols=None, mask=None, rope=None, kv=None):
    B = src.shape[0]
    cfg = _attn_setup(kind)
    pair, nh = cfg["pair"], cfg["nh"]
    Sk = S if pair else MEM_LEN
    has_bias, has_rope = negc_cols is not None, rope is not None
    R = nh * TQ
    nq, nk = S // TQ, Sk // TK
    pairs, mask_index = _tile_pairs(kind, nq, nk)

    def body(*refs):
        refs = list(refs)
        if pair:
            qkv_ref = refs.pop(0)
            load_q = lambda rows: qkv_ref[0, rows, 0:LANES]
            load_kv = lambda rows: (qkv_ref[0, rows, LANES:2 * LANES], qkv_ref[0, rows, 2 * LANES:3 * LANES])
        else:
            q_ref, k_ref, v_ref = refs.pop(0), refs.pop(0), refs.pop(0)
            load_q = lambda rows: q_ref[0, rows, :]
            load_kv = lambda rows: (k_ref[0, rows, :], v_ref[0, rows, :])
        negc_ref = refs.pop(0) if has_bias else None
        mask_ref = refs.pop(0) if mask is not None else None
        rope_refs = [refs.pop(0) for _ in range(3)] if has_rope else None
        o_ref, lse_ref, qT2s, ks, vTs, s_a, s_b, p_a, p_b = refs[:9]
        nb = refs[9] if has_bias else None
        _attn_t_prep(cfg, dict(load_q=load_q, load_kv=load_kv, rope=rope_refs, negc=negc_ref,
                               block=pl.program_id(1)), S, Sk, qT2s=qT2s, ks=ks, vTs=vTs, nb=nb)

        def cols(j):
            return slice(j * TK, (j + 1) * TK)

        def scores(i, j):
            return _raw_scores_t(cfg, ks[cols(j), :], qT2s[i])

        def finish(i, m, l, accT):
            oT2 = accT / l
            oT = jnp.where(_head_rows(0, True), oT2[:, 0:TQ], oT2[:, TQ:2 * TQ]) if pair else oT2
            o_ref[0, i * TQ:(i + 1) * TQ, :] = oT.T
            lse_ref[0, 0, i:i + 1, :] = m + jnp.log(l)

        s_bufs, p_bufs = (s_a, s_b), (p_a, p_b)
        s_bufs[0][...] = scores(*pairs[0])
        m = l = accT = None
        for t, (i, j) in enumerate(pairs):
            cur, oth = t % 2, 1 - t % 2
            if t > 0:
                i_prev, j_prev = pairs[t - 1]
                pv = jnp.dot(vTs[j_prev], p_bufs[oth][...], preferred_element_type=F32)
                acc_full = pv if accT is None else accT + pv
            if t + 1 < len(pairs):
                s_bufs[oth][...] = scores(*pairs[t + 1])
            first = j == 0
            if first and t > 0:
                finish(i_prev, m, l, acc_full)
            sT = _bias_mask_t(cfg, s_bufs[cur][...], nb, mask_ref, cols(j), mask_index(i, j))
            m_tile = jnp.max(sT, axis=0, keepdims=True)
            m_new = m_tile if first else jnp.maximum(m, m_tile)
            p = jnp.exp(sT - m_new)
            p_bufs[cur][...] = p.astype(BF16)
            if first:
                l, accT = jnp.sum(p, axis=0, keepdims=True), None
            else:
                alpha = jnp.exp(m - m_new)
                l, accT = alpha * l + jnp.sum(p, axis=0, keepdims=True), acc_full * alpha
            m = m_new
        i_last, j_last = pairs[-1]
        pv = jnp.dot(vTs[j_last], p_bufs[(len(pairs) - 1) % 2][...], preferred_element_type=F32)
        finish(i_last, m, l, pv if accT is None else accT + pv)

    ins, in_specs = _attn_t_inputs(kind, src, S, negc_cols, mask, rope, kv)
    W = cfg["n_blocks"] * LANES
    scratch = [pltpu.VMEM((nq, LANES, R), BF16), pltpu.VMEM((Sk, LANES), BF16), pltpu.VMEM((nk, LANES, TK), BF16),
               pltpu.VMEM((TK, R), F32), pltpu.VMEM((TK, R), F32), pltpu.VMEM((TK, R), BF16), pltpu.VMEM((TK, R), BF16)]
    if has_bias:
        scratch.append(pltpu.VMEM((nh, Sk, LANES), F32))
    return pl.pallas_call(
        body, name=kind + "_attn_fwd", grid=(B, cfg["n_blocks"]),
        in_specs=in_specs,
        out_specs=[pl.BlockSpec((1, S, LANES), lambda b, h: (b, 0, h)),
                   pl.BlockSpec((1, 1, nq, R), lambda b, h: (b, h, 0, 0))],
        out_shape=[jax.ShapeDtypeStruct((B, S, W), F32), jax.ShapeDtypeStruct((B, cfg["n_blocks"], nq, R), F32)],
        scratch_shapes=scratch,
        compiler_params=_params(("arbitrary", "arbitrary")),
    )(*ins)


def attn_bwd(kind, src, do, o, lse, S, *, negc_cols=None, mask=None, rope=None, kv=None, token=None):
    B = src.shape[0]
    cfg = _attn_setup(kind)
    pair, nh, s_scale, q_fold = cfg["pair"], cfg["nh"], cfg["s_scale"], cfg["q_fold"]
    Sk = S if pair else MEM_LEN
    has_bias, has_rope = negc_cols is not None, rope is not None
    R = nh * TQ
    nq, nk = S // TQ, Sk // TK
    pairs, mask_index = _tile_pairs(kind, nq, nk)

    def body(*refs):
        refs = list(refs)
        if pair:
            qkv_ref = refs.pop(0)
            load_q = lambda rows: qkv_ref[0, rows, 0:LANES]
            load_kv = lambda rows: (qkv_ref[0, rows, LANES:2 * LANES], qkv_ref[0, rows, 2 * LANES:3 * LANES])
        else:
            q_ref, k_ref, v_ref = refs.pop(0), refs.pop(0), refs.pop(0)
            load_q = lambda rows: q_ref[0, rows, :]
            load_kv = lambda rows: (k_ref[0, rows, :], v_ref[0, rows, :])
        negc_ref = refs.pop(0) if has_bias else None
        mask_ref = refs.pop(0) if mask is not None else None
        rope_refs = [refs.pop(0) for _ in range(3)] if has_rope else None
        do_ref, o_ref, lse_ref = refs.pop(0), refs.pop(0), refs.pop(0)
        if token is not None:
            refs.pop(0)
        if pair:
            dqkv_ref = refs.pop(0)
            dneg_ref = refs.pop(0) if has_bias else None
            drow_ref = refs.pop(0) if has_bias else None
        else:
            dq_ref, dk_ref, dv_ref = refs.pop(0), refs.pop(0), refs.pop(0)
        qT2s, ks, vs, kTs, doT2s, delta_s, dk_acc, dv_acc = refs[:8]
        bufs_a, bufs_b = refs[8:12], refs[12:16]
        nb, dneg_acc = (refs[16], refs[17]) if has_bias else (None, None)
        lane = lax.broadcasted_iota(jnp.int32, (1, LANES), 1)
        _attn_t_prep(cfg, dict(load_q=load_q, load_kv=load_kv, rope=rope_refs, negc=negc_ref,
                               block=pl.program_id(1)), S, Sk,
                     qT2s=qT2s, ks=ks, vs=vs, kTs=kTs, nb=nb)

        def prep_do(n):
            rows = slice(n * TQ, (n + 1) * TQ)
            doT = do_ref[0, rows, :].astype(BF16).astype(F32).T
            prodT = doT * o_ref[0, rows, :].T
            doTb = doT.astype(BF16)
            for hh in range(nh):
                hm = _head_rows(hh, pair)
                doT2s[n, :, hh * TQ:(hh + 1) * TQ] = jnp.where(hm, doTb, jnp.zeros_like(doTb))
                delta_s[n:n + 1, hh * TQ:(hh + 1) * TQ] = jnp.sum(jnp.where(hm, prodT, 0.0), axis=0, keepdims=True)

        for n in range(nq):
            prep_do(n)
        dk_acc[...] = jnp.zeros(dk_acc.shape, F32)
        dv_acc[...] = jnp.zeros(dv_acc.shape, F32)
        if has_bias:
            dneg_acc[...] = jnp.zeros(dneg_acc.shape, F32)

        def cols(j):
            return slice(j * TK, (j + 1) * TK)

        nt_dims = (((1,), (1,)), ((), ()))

        def first_products(i, j, bufs):
            bufs[0][...] = _raw_scores_t(cfg, ks[cols(j), :], qT2s[i])
            bufs[1][...] = jnp.dot(vs[cols(j), :], doT2s[i], preferred_element_type=F32)

        def last_products(i, j, bufs, dqT2):
            dv_acc[j] += lax.dot_general(doT2s[i], bufs[2][...], nt_dims, preferred_element_type=F32)
            dk_acc[j] += lax.dot_general(qT2s[i], bufs[3][...], nt_dims, preferred_element_type=F32)
            dq = jnp.dot(kTs[j], bufs[3][...], preferred_element_type=F32)
            return dq if dqT2 is None else dqT2 + dq

        def finish_q(i, dqT2, drow):
            rows = slice(i * TQ, (i + 1) * TQ)
            dqT = jnp.where(_head_rows(0, True), dqT2[:, 0:TQ], dqT2[:, TQ:2 * TQ]) if pair else dqT2
            dq = dqT.T
            if q_fold is not None:
                dq = dq * q_fold
            if has_rope:
                dq = _rope_bwd(dq, *[t[rows, :] for t in rope_refs])
            if pair:
                dqkv_ref[0, rows, 0:LANES] = dq.astype(BF16)
            else:
                dq_ref[0, rows, :] = dq.astype(BF16)
            if has_bias:
                drow_ref[0, 0, i:i + 1, :] = drow

        bufs = (bufs_a, bufs_b)
        first_products(*pairs[0], bufs[0])
        dqT2 = drow = None
        for t, (i, j) in enumerate(pairs):
            cur, oth = bufs[t % 2], bufs[1 - t % 2]
            first = j == 0
            if first and t > 0:
                i_prev, j_prev = pairs[t - 1]
                finish_q(i_prev, last_products(i_prev, j_prev, oth, dqT2), drow)
                dqT2 = drow = None
            sT = _bias_mask_t(cfg, cur[0][...], nb, mask_ref, cols(j), mask_index(i, j))
            pT = jnp.exp(sT - lse_ref[0, 0, i:i + 1, :])
            dsT = pT * (cur[1][...] - delta_s[i:i + 1, :])
            if has_bias:
                tile_rows = jnp.sum(dsT, axis=0, keepdims=True)
                drow = tile_rows if drow is None else drow + tile_rows
                for hh in range(nh):
                    part = dsT[:, hh * TQ:hh * TQ + LANES]
                    for u in range(1, TQ // LANES):
                        part = part + dsT[:, hh * TQ + u * LANES:hh * TQ + (u + 1) * LANES]
                    dneg_acc[hh, cols(j), :] += part
            if s_scale is not None:
                dsT = dsT * s_scale
            cur[2][...] = pT.astype(BF16)
            cur[3][...] = dsT.astype(BF16)
            if not first:
                dqT2 = last_products(*pairs[t - 1], oth, dqT2)
            if t + 1 < len(pairs):
                first_products(*pairs[t + 1], oth)
        i_last, j_last = pairs[-1]
        finish_q(i_last, last_products(i_last, j_last, bufs[(len(pairs) - 1) % 2], dqT2), drow)

        for n in range(nk):
            rows = slice(n * TK, (n + 1) * TK)
            dk = dk_acc[n].T
            dv = dv_acc[n].T
            if has_rope:
                dk = _rope_bwd(dk, *[t[rows, :] for t in rope_refs])
            if pair:
                dqkv_ref[0, rows, LANES:2 * LANES] = dk.astype(BF16)
                dqkv_ref[0, rows, 2 * LANES:3 * LANES] = dv.astype(BF16)
            else:
                dk_ref[0, rows, :] = dk.astype(BF16)
                dv_ref[0, rows, :] = dv.astype(BF16)
            if has_bias:
                x0 = jnp.sum(dneg_acc[0, rows, :], axis=1, keepdims=True)
                x1 = jnp.sum(dneg_acc[1, rows, :], axis=1, keepdims=True)
                dneg_ref[0, rows, :] = jnp.where(lane == 0, x0, jnp.where(lane == 1, x1, 0.0))

    ins, in_specs = _attn_t_inputs(kind, src, S, negc_cols, mask, rope, kv)
    row_spec = pl.BlockSpec((1, S, LANES), lambda b, h: (b, 0, h))
    vec_spec = pl.BlockSpec((1, 1, nq, R), lambda b, h: (b, h, 0, 0))
    ins += [do, o, lse]
    in_specs += [row_spec, row_spec, vec_spec]
    if token is not None:
        ins.append(token)
        in_specs.append(pl.BlockSpec(token.shape, lambda b, h: (0, 0)))
    W = cfg["n_blocks"] * LANES
    if pair:
        out_specs = [pl.BlockSpec((1, S, PAIR_W), lambda b, h: (b, 0, h))]
        out_shape = [jax.ShapeDtypeStruct((B, S, 3 * W), BF16)]
        if has_bias:
            out_specs += [row_spec, vec_spec]
            out_shape += [jax.ShapeDtypeStruct((B, S, W), F32), jax.ShapeDtypeStruct((B, cfg["n_blocks"], nq, R), F32)]
    else:
        kv_spec = pl.BlockSpec((1, MEM_LEN, LANES), lambda b, h: (b, 0, h))
        out_specs = [row_spec, kv_spec, kv_spec]
        out_shape = [jax.ShapeDtypeStruct((B, S, W), BF16)] + [jax.ShapeDtypeStruct((B, MEM_LEN, W), BF16)] * 2
    scratch = [pltpu.VMEM((nq, LANES, R), BF16), pltpu.VMEM((Sk, LANES), BF16),
               pltpu.VMEM((Sk, LANES), BF16), pltpu.VMEM((nk, LANES, TK), BF16), pltpu.VMEM((nq, LANES, R), BF16),
               pltpu.VMEM((nq, R), F32), pltpu.VMEM((nk, LANES, TK), F32), pltpu.VMEM((nk, LANES, TK), F32)]
    pair_bufs = [pltpu.VMEM((TK, R), F32), pltpu.VMEM((TK, R), F32), pltpu.VMEM((TK, R), BF16), pltpu.VMEM((TK, R), BF16)]
    scratch += pair_bufs + pair_bufs
    if has_bias:
        scratch += [pltpu.VMEM((nh, Sk, LANES), F32), pltpu.VMEM((nh, Sk, LANES), F32)]
    return pl.pallas_call(
        body, name=kind + "_attn_bwd", grid=(B, cfg["n_blocks"]),
        in_specs=in_specs, out_specs=out_specs, out_shape=out_shape, scratch_shapes=scratch,
        compiler_params=_params(("arbitrary", "arbitrary")),
    )(*ins)


def _sigmoid(g):
    return 1.0 / (1.0 + jnp.exp(-g))


def out_step(proj, o_fox, o_dil, o_mem, w_out, x, target, gf, tm):
    T = x.shape[0]

    def body(fg_ref, dg_ref, mg_ref, of_ref, od_ref, om_ref, w_ref, x_ref, t_ref, gf_ref,
             dx_ref, dof_ref, dod_ref, dom_ref, dfg_ref, ddg_ref, dmg_ref, gw_ref, sm_ref, gw_acc):
        branches = []
        for g_ref, o_ref in ((fg_ref, of_ref), (dg_ref, od_ref), (mg_ref, om_ref)):
            g = g_ref[...]
            sg = _sigmoid(g)
            o = o_ref[...]
            branches.append((g, sg, o))
        ymix = jnp.concatenate([(o * (g * sg)).astype(BF16) for g, sg, o in branches], axis=1)
        x2 = x_ref[...] + jnp.dot(ymix, w_ref[...], preferred_element_type=F32)
        r = lax.rsqrt(jnp.mean(x2 * x2, axis=-1, keepdims=True) + RMS_EPS)
        yn = x2 * r
        err = yn * gf_ref[...] - t_ref[...]
        loss = 0.5 * jnp.sum(jnp.sum(err * err, axis=-1, keepdims=True) / D_MODEL, axis=0, keepdims=True)
        dyf = err / D_MODEL
        dgf = jnp.sum(dyf * yn, axis=0, keepdims=True)
        dyn = dyf * gf_ref[...]
        dx2 = r * (dyn - yn * jnp.mean(dyn * yn, axis=-1, keepdims=True))
        dx_ref[...] = dx2
        dxb = dx2.astype(BF16)
        dmix = lax.dot_general(dxb, w_ref[...], (((1,), (1,)), ((), ())), preferred_element_type=F32)
        col = 0
        for (g, sg, o), do_ref, dgate_ref in zip(branches, (dof_ref, dod_ref, dom_ref), (dfg_ref, ddg_ref, dmg_ref)):
            d = dmix[:, col:col + g.shape[1]]
            col += g.shape[1]
            do_ref[...] = (d * (g * sg)).astype(BF16)
            dgate_ref[...] = (d * o * (sg * (1.0 + g * (1.0 - sg)))).astype(BF16)
        row = lax.broadcasted_iota(jnp.int32, (8, D_MODEL), 0)
        upd = jnp.where(row == 0, dgf, jnp.where(row == 1, loss, 0.0))

        @pl.when(pl.program_id(0) == 0)
        def _():
            sm_ref[...] = jnp.zeros(sm_ref.shape, F32)
            gw_acc[...] = jnp.zeros(gw_acc.shape, F32)

        sm_ref[...] += upd
        gw_acc[...] += lax.dot_general(ymix, dxb, (((0,), (0,)), ((), ())), preferred_element_type=F32)

        @pl.when(pl.program_id(0) == T // tm - 1)
        def _():
            gw_ref[...] = gw_acc[...].astype(BF16)

    def rows(w, col=0):
        return pl.BlockSpec((tm, w), lambda i: (i, col))

    return pl.pallas_call(
        body, name="out_step", grid=(T // tm,),
        in_specs=[rows(FOX_W, B_FG // FOX_W), rows(DIL_W, B_DG // DIL_W), rows(MEM_W, B_MG // MEM_W),
                  rows(FOX_W), rows(DIL_W), rows(MEM_W),
                  pl.BlockSpec((MIX_W, D_MODEL), lambda i: (0, 0)),
                  rows(D_MODEL), rows(D_MODEL), pl.BlockSpec((1, D_MODEL), lambda i: (0, 0))],
        out_specs=[rows(D_MODEL), rows(FOX_W), rows(DIL_W), rows(MEM_W), rows(FOX_W), rows(DIL_W), rows(MEM_W),
                   pl.BlockSpec((MIX_W, D_MODEL), lambda i: (0, 0)), pl.BlockSpec((8, D_MODEL), lambda i: (0, 0))],
        out_shape=[jax.ShapeDtypeStruct((T, D_MODEL), F32), jax.ShapeDtypeStruct((T, FOX_W), BF16),
                   jax.ShapeDtypeStruct((T, DIL_W), BF16), jax.ShapeDtypeStruct((T, MEM_W), BF16),
                   jax.ShapeDtypeStruct((T, FOX_W), BF16), jax.ShapeDtypeStruct((T, DIL_W), BF16),
                   jax.ShapeDtypeStruct((T, MEM_W), BF16), jax.ShapeDtypeStruct((MIX_W, D_MODEL), BF16),
                   jax.ShapeDtypeStruct((8, D_MODEL), F32)],
        scratch_shapes=[pltpu.VMEM((MIX_W, D_MODEL), F32)],
        compiler_params=_params(("arbitrary",)),
    )(proj, proj, proj, o_fox, o_dil, o_mem, w_out, x, target, gf)


def adamw(w, g, m, v, tr, name):
    lead = w.shape[:-2]
    R, C = w.shape[-2:]
    zeros = (0,) * len(lead)

    def body(w_ref, g_ref, m_ref, v_ref, d_ref, mo_ref, vo_ref):
        gv = g_ref[...]
        mn = ADAM_B1 * m_ref[...] + (1.0 - ADAM_B1) * gv
        vn = ADAM_B2 * v_ref[...] + (1.0 - ADAM_B2) * jnp.square(gv)
        m_hat = mn / (1.0 - ADAM_B1 ** ADAM_STEP)
        v_hat = vn / (1.0 - ADAM_B2 ** ADAM_STEP)
        d_ref[...] = -ADAM_LR * (m_hat / (jnp.sqrt(v_hat) + ADAM_EPS) + ADAM_WD * w_ref[...])
        mo_ref[...] = mn
        vo_ref[...] = vn

    spec = pl.BlockSpec((1,) * len(lead) + (tr, C), lambda i: zeros + (i, 0))
    return pl.pallas_call(
        body, name=name, grid=(pl.cdiv(R, tr),),
        in_specs=[spec] * 4, out_specs=[spec] * 3,
        out_shape=[jax.ShapeDtypeStruct(w.shape, F32)] * 3,
        compiler_params=_params(("arbitrary",)),
    )(w, g, m, v)


def _pad_row(v, width):
    return jnp.concatenate([v, jnp.zeros((1, width - v.shape[1]), v.dtype)], axis=1)


def local_grads(x, mem, norm_g, b_forget, mem_norm_g, final_norm_g, loss_target, w_in_a, first_token, late_weights,
                start_exchange):
    B, S, D = x.shape
    T = B * S
    xt = x.reshape(T, D)
    memt = mem.reshape(B * MEM_LEN, D)
    b_pad = _pad_row(b_forget, LANES)

    h, h_t = rms_fwd(xt, norm_g, 512, "rms_x", with_transpose=True)
    proj_a = mm_nn(h, w_in_a, 512, PA, "in_proj_a", first_token)
    proj_a3 = proj_a.reshape(B, S, PA)

    negc = fox_gate(proj_a3, b_pad)
    causal = _log_masks_t(S, "causal")
    dilated = _log_masks_t(S, "dilated")
    rope = _rope_tables(S)

    o_fox, lse_fox = attn_fwd("fox", proj_a3, S, negc_cols=negc, mask=causal)

    w_in_b, w_kv, w_out = late_weights(o_fox)
    proj_b = mm_nn(h, w_in_b, 512, PB // 2, "in_proj_b")
    proj_b3 = proj_b.reshape(B, S, PB)
    o_dil, lse_dil = attn_fwd("dil", proj_b3, S, mask=dilated, rope=rope)

    mh, mh_t = rms_fwd(memt, mem_norm_g, B * MEM_LEN, "rms_mem", with_transpose=True)
    mkv = mm_nn(mh, w_kv, B * MEM_LEN, 2 * MEM_W, "mem_kv_proj")
    mkv3 = mkv.reshape(B, MEM_LEN, 2 * MEM_W)
    o_mem, lse_mem = attn_fwd("mem", proj_b3, S, kv=mkv3)

    dx2, do_fox, do_dil, do_mem, dfg, ddg, dmg, g_out, small_out = out_step(
        proj_b, o_fox.reshape(T, FOX_W), o_dil.reshape(T, DIL_W), o_mem.reshape(T, MEM_W), w_out,
        xt, loss_target.reshape(T, D), final_norm_g.reshape(1, D), 256)

    gates = [(dfg, 1, B_FG), (ddg, 1, B_DG), (dmg, 1, B_MG)]
    g_gates = mm_tn_multi(h_t, [piece[0] for piece in gates], 1024, "w_in_grad_gates", BF16)
    first, token = start_exchange([g_gates, g_out], "early_exchange_a")

    dqkv_fox, dneg, drow = attn_bwd("fox", proj_a3, do_fox.reshape(B, S, FOX_W), o_fox, lse_fox, S,
                                    negc_cols=negc, mask=causal, token=token)
    drow = drow.reshape(B, FOX_HEADS // 2, S // TQ, 2, TQ).transpose(0, 1, 3, 2, 4).reshape(B, FOX_HEADS, S)
    drow = jnp.pad(drow, ((0, 0), (0, LANES - FOX_HEADS), (0, 0)))
    dflog, db_part = fox_gate_bwd(drow, dneg, proj_a3, b_pad)
    fox = [(dqkv_fox.reshape(T, 3 * FOX_W), 0, A_FOX), (dflog.reshape(T, LANES), 0, A_FLOG)]
    g_fox = mm_tn_multi(h_t, [piece[0] for piece in fox], 1024, "w_in_grad_fox", BF16)
    second, token = start_exchange([g_fox], "early_exchange_b")

    (dqkv_dil,) = attn_bwd("dil", proj_b3, do_dil.reshape(B, S, DIL_W), o_dil, lse_dil, S, mask=dilated, rope=rope,
                           token=token)
    dil = [(dqkv_dil.reshape(T, 3 * DIL_W), 1, B_DIL)]
    g_dil = mm_tn_multi(h_t, [piece[0] for piece in dil], 1024, "w_in_grad_dil", BF16)
    third, token = start_exchange([g_dil], "early_exchange_c")

    dmq, dmk, dmv = attn_bwd("mem", proj_b3, do_mem.reshape(B, S, MEM_W), o_mem, lse_mem, S, kv=mkv3, token=token)
    mq = [(dmq.reshape(T, MEM_W), 1, B_MQ)]
    g_mq = mm_tn_multi(h_t, [piece[0] for piece in mq], 1024, "w_in_grad_mq", BF16)
    dmkv = jnp.concatenate([dmk, dmv], axis=2).reshape(B * MEM_LEN, 2 * MEM_W)
    g_kv = mm_tn_multi(mh_t, [dmkv], B * MEM_LEN, "w_kv_grad", BF16)
    fourth, token = start_exchange([g_mq, g_kv], "early_exchange_d")

    grad_x, dng = in_proj_bwd_rms(gates + fox + dil + mq, (w_in_a, w_in_b), xt, norm_g, dx2, 256, token)
    dmh = mm_nt(dmkv, w_kv, B * MEM_LEN, D, "mem_kv_bwd")
    _, dmng = rms_bwd(memt, mem_norm_g, dmh, None, B * MEM_LEN, "rms_mem_bwd")

    small = jnp.concatenate([dng[0:1], dmng[0:1], small_out[0:1], _pad_row(db_part[0:1], D), small_out[1:2],
                             jnp.zeros((3, D), F32)], axis=0)
    early = [(first, dqkv_fox), (second, dqkv_dil), (third, dmq), (fourth, grad_x)]
    return grad_x.reshape(B, S, D), early, small


def kernel(x, mem, norm_g, w_in, b_forget, mem_norm_g, w_mem_kv, w_out, final_norm_g, loss_target, m_norm_g, m_w_in, m_b_forget, m_mem_norm_g, m_w_mem_kv, m_w_out, m_final_norm_g, v_norm_g, v_w_in, v_b_forget, v_mem_norm_g, v_w_mem_kv, v_w_out, v_final_norm_g):
    D = D_MODEL
    shard_a, shard_b = _split_cols(_pack_cols(w_in).astype(BF16).reshape(w_in.shape[1], PW))
    (w_in_a,) = weight_gather([shard_a])
    gather, gather_token = early_exchange_start(
        [shard_b, w_mem_kv[0].astype(BF16), w_out[0].astype(BF16)], "late_gather", gather=True, after=w_in_a,
        relations=_SIBLING_AND_SAME_CORES)

    def late_weights(after):
        shards, gathered = early_exchange_wait(gather, after, "late_gather_wait")
        return pass_on_to_sibling(gathered, shards, gather["rows"], "late_gather_pass")

    grad_x, early, small = local_grads(
        x, mem, norm_g, b_forget, mem_norm_g, final_norm_g, loss_target, w_in_a, gather_token, late_weights,
        early_exchange_start)

    (first, after_first), (second, after_second), (third, after_third), (fourth, after_fourth) = early
    (src_gates, src_out), (land_gates, land_out) = early_exchange_wait(first, after_first, "early_wait_a")
    (src_fox,), (land_fox,) = early_exchange_wait(second, after_second, "early_wait_b")
    (src_dil,), (land_dil,) = early_exchange_wait(third, after_third, "early_wait_c")
    (src_mq, src_kv), (land_mq, land_kv) = early_exchange_wait(fourth, after_fourth, "early_wait_d")
    gates = slot_sum8(src_gates, land_gates, 128, "sum_w_in_gates")
    gw_out = slot_sum8(src_out, land_out, 256, "sum_w_out")
    fox = slot_sum8(src_fox, land_fox, 128, "sum_w_in_fox")
    dil = slot_sum8(src_dil, land_dil, 128, "sum_w_in_dil")
    mq = slot_sum8(src_mq, land_mq, 128, "sum_w_in_mq")
    gw_kv = slot_sum8(src_kv, land_kv, 128, "sum_w_kv")

    tot = small_all_reduce(small)
    gw_in = _unpack_cols(jnp.concatenate(
        [fox[:, :3 * FOX_W], gates[:, :FOX_W], dil, gates[:, FOX_W:FOX_W + DIL_W], mq,
         gates[:, FOX_W + DIL_W:], fox[:, 3 * FOX_W:]], axis=1)[None])

    loss = tot[4, 0]
    g_norm, g_mem_norm, g_final, g_b = tot[0:1], tot[1:2], tot[2], tot[3:4, :FOX_HEADS]

    def rows8(*rows):
        rows = [r.reshape(1, -1) for r in rows]
        rows = [_pad_row(r, D) for r in rows]
        return jnp.concatenate(rows + [jnp.zeros((8 - len(rows), D), F32)], axis=0)

    sw = rows8(norm_g, mem_norm_g, final_norm_g, b_forget)
    sm = rows8(m_norm_g, m_mem_norm_g, m_final_norm_g, m_b_forget)
    sv = rows8(v_norm_g, v_mem_norm_g, v_final_norm_g, v_b_forget)
    d_s, m_s, v_s = adamw(sw, tot, sm, sv, 8, "adamw_small")
    d_in, m_in, v_in = adamw(w_in, gw_in, m_w_in, v_w_in, 32, "adamw_w_in")
    d_kv, m_kv, v_kv = adamw(w_mem_kv[0], gw_kv, m_w_mem_kv[0], v_w_mem_kv[0], 128, "adamw_w_kv")
    d_out, m_out, v_out = adamw(w_out[0], gw_out, m_w_out[0], v_w_out[0], 256, "adamw_w_out")

    def small_outs(t):
        return t[0:1], t[3:4, :FOX_HEADS], t[1:2], t[2]

    grads = (g_norm, gw_in, g_b, g_mem_norm, gw_kv[None], gw_out[None], g_final)
    outs = []
    for t, big in ((d_s, (d_in, d_kv, d_out)), (m_s, (m_in, m_kv, m_out)), (v_s, (v_in, v_kv, v_out))):
        n, b, mn, f = small_outs(t)
        outs += [n, big[0], b, mn, big[1][None], big[2][None], f]
    return (loss, grad_x, *grads, *outs)
```

```python
import math

import numpy as np
import jax
import jax.numpy as jnp
from jax import lax
from jax.experimental import pallas as pl
from jax.experimental.pallas import tpu as pltpu

F32 = jnp.float32
BF16 = jnp.bfloat16

D_MODEL = 1024
HEAD_DIM = 64
FOX_HEADS = 12
DIL_HEADS = 12
MEM_HEADS = 4
MEM_HEAD_DIM = 128
MEM_LEN = 256
FOX_W = FOX_HEADS * HEAD_DIM
DIL_W = DIL_HEADS * HEAD_DIM
MEM_W = MEM_HEADS * MEM_HEAD_DIM
MIX_W = FOX_W + DIL_W + MEM_W
DILATIONS = ((128, 1), (512, 4), (2048, 16))
ROPE_THETA = 500000.0
ROPE_DIM = HEAD_DIM // 4
RMS_EPS = 1e-6
NEG_INF = -1e30
IN_W = 4 * FOX_W + FOX_HEADS + 4 * DIL_W + 2 * MEM_W

ADAM_LR = 0.001
ADAM_B1 = 0.9
ADAM_B2 = 0.999
ADAM_EPS = 1e-08
ADAM_WD = 0.01
ADAM_STEP = 10

N_DEV = 8
LANES = 128
PAIR_W = 3 * LANES
TQ = 256
TK = 256

O_FQ, O_FK, O_FV, O_FG = 0, FOX_W, 2 * FOX_W, 3 * FOX_W
O_FLOG = 4 * FOX_W
O_DQ = O_FLOG + FOX_HEADS
O_DK, O_DV, O_DG = O_DQ + DIL_W, O_DQ + 2 * DIL_W, O_DQ + 3 * DIL_W
O_MQ = O_DQ + 4 * DIL_W
O_MG = O_MQ + MEM_W
P_FOX = 0
P_FG = P_FOX + 3 * FOX_W
P_DIL = P_FG + FOX_W
P_DG = P_DIL + 3 * DIL_W
P_MQ = P_DG + DIL_W
P_MG = P_MQ + MEM_W
P_FLOG = P_MG + MEM_W
PW = P_FLOG + LANES
A_FOX = 0
A_FLOG = A_FOX + 3 * FOX_W
PA = A_FLOG + LANES
B_FG = 0
B_DG = B_FG + FOX_W
B_DIL = B_DG + DIL_W
B_MQ = B_DIL + 3 * DIL_W
B_MG = -(-(B_MQ + MEM_W) // MEM_W) * MEM_W
PB = B_MG + MEM_W

VMEM_LIMIT = 56 * 1024 * 1024


def _pack_pieces():
    pieces = []
    for base in (O_FQ, O_DQ):
        seg = []
        for hp in range(FOX_HEADS // 2):
            for part in range(3):
                seg.append((base + part * FOX_W + hp * LANES, LANES))
        pieces.append(seg)
    fox, dil = pieces
    return fox + [(O_FG, FOX_W)] + dil + [(O_DG, DIL_W), (O_MQ, MEM_W), (O_MG, MEM_W), (O_FLOG, FOX_HEADS)]


def _pack_cols(w):
    parts = [w[..., s:s + n] for s, n in _pack_pieces()]
    parts.append(jnp.zeros(w.shape[:-1] + (LANES - FOX_HEADS,), w.dtype))
    return jnp.concatenate(parts, axis=-1)


def _split_cols(wp):
    def cut(start, width):
        return wp[..., start:start + width]

    group_a = jnp.concatenate([cut(P_FOX, 3 * FOX_W), cut(P_FLOG, LANES)], axis=-1)
    pad = jnp.zeros(wp.shape[:-1] + (B_MG - B_MQ - MEM_W,), wp.dtype)
    group_b = jnp.concatenate([cut(P_FG, FOX_W), cut(P_DG, DIL_W), cut(P_DIL, 3 * DIL_W), cut(P_MQ, MEM_W), pad,
                               cut(P_MG, MEM_W)], axis=-1)
    return group_a, group_b


def _unpack_cols(g):
    runs = []
    pos = 0
    for s, n in _pack_pieces():
        runs.append((s, n, pos))
        pos += n
    runs.sort()
    return jnp.concatenate([g[..., p:p + n] for s, n, p in runs], axis=-1)


def _params(sem=None, **kw):
    return pltpu.CompilerParams(dimension_semantics=sem, vmem_limit_bytes=VMEM_LIMIT, **kw)


def _mesh_pos():
    return lax.axis_index("x"), lax.axis_index("y"), lax.axis_index("c")


def _flip(v, d):
    return 1 - v if d else v


_RELATIONS = [(dx, dy, dc) for dx in (0, 1) for dy in (0, 1) for dc in (0, 1)][1:]
_SIBLING_AND_SAME_CORES = [(0, 0, 1), (1, 0, 0), (0, 1, 0), (1, 1, 0)]


def weight_gather(shards):
    n_arr = len(shards)
    rows = [s.shape[0] for s in shards]

    def body(*refs):
        in_refs = refs[:n_arr]
        out_refs = refs[n_arr:2 * n_arr]
        send_sems, recv_sems, local_sems = refs[2 * n_arr:]
        x, y, c = _mesh_pos()
        me, sibling = (x, y, c), (x, y, 1 - c)
        x_nbr, y_nbr, diag = (1 - x, y, c), (x, 1 - y, c), (1 - x, 1 - y, c)
        north = c == 1
        relay_from = (jnp.where(north, 1 - x, x), jnp.where(north, y, 1 - y), c)
        relay_to = (jnp.where(north, x, 1 - x), jnp.where(north, 1 - y, y), c)
        k_from = jnp.where(north, 1, 2)
        k_to = 3 - k_from

        def block(a, pos):
            px, py, pc = pos
            return out_refs[a].at[pl.ds((4 * px + 2 * py + pc) * rows[a], rows[a]), :]

        def copy(a, k, blk, to, src=None):
            return pltpu.make_async_remote_copy(
                src_ref=block(a, blk) if src is None else src, dst_ref=block(a, blk),
                send_sem=send_sems.at[a, k], recv_sem=recv_sems.at[a, k],
                device_id=to, device_id_type=pl.DeviceIdType.MESH)

        started = []
        mine = []
        for a in range(n_arr):
            cp = pltpu.make_async_copy(in_refs[a], block(a, me), local_sems.at[a])
            cp.start()
            mine.append(cp)
            first = [copy(a, 0, me, sibling, src=in_refs[a]), copy(a, 1, me, x_nbr, src=in_refs[a]),
                     copy(a, 2, me, y_nbr, src=in_refs[a])]
            for cp in first:
                cp.start()
            started += first
        for a in range(n_arr):
            copy(a, k_from, relay_from, me).wait_recv()
            second_hop = copy(a, 3, relay_from, relay_to)
            second_hop.start()
            passed = copy(a, 3 + k_from, relay_from, sibling)
            passed.start()
            started += [second_hop, passed]
        for a in range(n_arr):
            copy(a, k_to, relay_to, me).wait_recv()
            passed = copy(a, 3 + k_to, relay_to, sibling)
            passed.start()
            started.append(passed)
        for a in range(n_arr):
            copy(a, 3, diag, me).wait_recv()
            passed = copy(a, 6, diag, sibling)
            passed.start()
            started.append(passed)
        for a in range(n_arr):
            copy(a, 0, sibling, me).wait_recv()
            for k, chip in ((4, x_nbr), (5, y_nbr), (6, diag)):
                copy(a, k, (chip[0], chip[1], 1 - c), me).wait_recv()
        for cp in started:
            cp.wait_send()
        for cp in mine:
            cp.wait()

    any_spec = pl.BlockSpec(memory_space=pl.ANY)
    return pl.pallas_call(
        body, name="weight_gather",
        out_shape=[jax.ShapeDtypeStruct((N_DEV * s.shape[0], s.shape[1]), s.dtype) for s in shards],
        in_specs=[any_spec] * n_arr, out_specs=[any_spec] * n_arr,
        scratch_shapes=[pltpu.SemaphoreType.DMA((n_arr, 7)), pltpu.SemaphoreType.DMA((n_arr, 7)),
                        pltpu.SemaphoreType.DMA((n_arr,))],
    )(*shards)


_OTHER_CHIPS = [(1, 0), (0, 1), (1, 1)]


def small_all_reduce(small):
    vmem_spec = pl.BlockSpec(memory_space=pltpu.VMEM)

    def body(small_ref, tot_ref, land, send_sems, recv_sems):
        x, y, c = _mesh_pos()
        me = 4 * x + 2 * y + c
        land[me] = small_ref[...]
        sends, recvs = [], []
        for j, (dx, dy, dc) in enumerate(_RELATIONS):
            px, py, pc = _flip(x, dx), _flip(y, dy), _flip(c, dc)
            common = dict(send_sem=send_sems.at[j], recv_sem=recv_sems.at[j],
                          device_id=(px, py, pc), device_id_type=pl.DeviceIdType.MESH)
            sends.append(pltpu.make_async_remote_copy(src_ref=small_ref, dst_ref=land.at[me], **common))
            recvs.append(pltpu.make_async_remote_copy(src_ref=small_ref, dst_ref=land.at[4 * px + 2 * py + pc], **common))
        for cp in sends:
            cp.start()
        for cp in recvs:
            cp.wait_recv()
        for cp in sends:
            cp.wait_send()
        tot = land[0]
        for d in range(1, N_DEV):
            tot = tot + land[d]
        tot_ref[...] = tot

    return pl.pallas_call(
        body, name="small_sum", out_shape=jax.ShapeDtypeStruct(small.shape, small.dtype),
        in_specs=[vmem_spec], out_specs=vmem_spec,
        scratch_shapes=[pltpu.VMEM((N_DEV,) + small.shape, small.dtype),
                        pltpu.SemaphoreType.DMA((len(_RELATIONS),)), pltpu.SemaphoreType.DMA((len(_RELATIONS),))],
    )(small)


_HBM = pl.BlockSpec(memory_space=pltpu.HBM)
_SEM = pl.BlockSpec(memory_space=pltpu.SEMAPHORE)
_EFFECT = pltpu.SideEffectType.DATAFLOW_SIDE_EFFECTING


def _early_copies(src_refs, land_refs, send_sems, recv_sems, rows, gather, relations):
    x, y, c = _mesh_pos()
    me = 4 * x + 2 * y + c
    copies = []
    for a in range(len(src_refs)):
        for dx, dy, dc in relations:
            px, py, pc = _flip(x, dx), _flip(y, dy), _flip(c, dc)
            peer = 4 * px + 2 * py + pc
            copies.append(pltpu.make_async_remote_copy(
                src_ref=src_refs[a] if gather else src_refs[a].at[pl.ds(peer * rows[a], rows[a]), :],
                dst_ref=land_refs[a].at[pl.ds(me * rows[a], rows[a]), :],
                send_sem=send_sems[a], recv_sem=recv_sems[a],
                device_id=(px, py, pc), device_id_type=pl.DeviceIdType.MESH))
    return copies


def own_slots(shards, name):
    n = len(shards)
    x, y, c = _mesh_pos()
    me = (4 * x + 2 * y + c).astype(jnp.int32).reshape(1)
    empties = [lax.empty((N_DEV * s.shape[0], s.shape[1]), s.dtype) for s in shards]

    def body(me_ref, *refs):
        for a in range(n):
            refs[2 * n + a][...] = refs[a][...]

    return pl.pallas_call(
        body, name=name,
        grid_spec=pltpu.PrefetchScalarGridSpec(
            num_scalar_prefetch=1, grid=(1,),
            in_specs=[pl.BlockSpec(s.shape, lambda i, w: (0, 0)) for s in shards]
            + [pl.BlockSpec(memory_space=pl.ANY)] * n,
            out_specs=[pl.BlockSpec(s.shape, lambda i, w: (w[0], 0)) for s in shards]),
        out_shape=[jax.ShapeDtypeStruct(e.shape, e.dtype) for e in empties],
        input_output_aliases={1 + n + a: a for a in range(n)},
        compiler_params=_params(("arbitrary",)),
    )(me, *shards, *empties)


def early_exchange_start(srcs, name, gather=False, after=None, relations=_RELATIONS):
    n = len(srcs)
    if gather:
        rows = [s.shape[0] for s in srcs]
        lands = list(own_slots(srcs, name + "_place"))
    else:
        rows = [s.shape[0] // N_DEV for s in srcs]
        lands = [lax.empty(s.shape, s.dtype) for s in srcs]

    extra = [] if after is None else [after]

    def body(*refs):
        src_refs, land_refs = refs[:n], refs[n:2 * n]
        first_sem = 2 * n + len(extra)
        send_sems, recv_sems = refs[first_sem:first_sem + n], refs[first_sem + n:first_sem + 2 * n]
        token = refs[-1]
        for cp in _early_copies(src_refs, land_refs, send_sems, recv_sems, rows, gather, relations):
            cp.start()
        token[...] = jnp.zeros_like(token)

    hbm = lambda a: pltpu.HBM(a.shape, a.dtype)
    outs = pl.pallas_call(
        body, name=name,
        out_shape=[pltpu.SemaphoreType.DMA(())] * (2 * n)
        + [hbm(a) for a in srcs] + [hbm(a) for a in lands] + [jax.ShapeDtypeStruct((8, LANES), F32)],
        in_specs=[_HBM] * (2 * n) + [pl.BlockSpec(memory_space=pl.ANY)] * len(extra),
        out_specs=[_SEM] * (2 * n) + [_HBM] * (2 * n) + [pl.BlockSpec(memory_space=pltpu.VMEM)],
        input_output_aliases={i: 2 * n + i for i in range(2 * n)},
        compiler_params=pltpu.CompilerParams(has_side_effects=_EFFECT),
    )(*[pltpu.with_memory_space_constraint(a, pltpu.HBM) for a in list(srcs) + lands], *extra)
    handle = dict(sems=outs[:2 * n], srcs=outs[2 * n:3 * n], lands=outs[3 * n:4 * n], rows=rows,
                  copies=len(relations))
    return handle, outs[-1]


def early_exchange_wait(handle, after, name):
    n = len(handle["srcs"])
    rows = handle["rows"]

    def body(*refs):
        src_refs, land_refs = refs[:n], refs[n:2 * n]
        send_sems, recv_sems = refs[2 * n:3 * n], refs[3 * n:4 * n]
        x, y, c = _mesh_pos()
        for a in range(n):
            span = pl.ds(0, handle["copies"] * rows[a])
            all_copies = pltpu.make_async_remote_copy(
                src_ref=land_refs[a].at[span, :], dst_ref=land_refs[a].at[span, :],
                send_sem=send_sems[a], recv_sem=recv_sems[a],
                device_id=(x, y, c), device_id_type=pl.DeviceIdType.MESH)
            all_copies.wait_send()
            all_copies.wait_recv()

    hbm = lambda a: pltpu.HBM(a.shape, a.dtype)
    ins = list(handle["srcs"]) + list(handle["lands"])
    outs = pl.pallas_call(
        body, name=name,
        out_shape=[hbm(a) for a in ins],
        in_specs=[_HBM] * (2 * n) + [_SEM] * (2 * n) + [pl.BlockSpec(memory_space=pl.ANY)],
        out_specs=[_HBM] * (2 * n),
        input_output_aliases={i: i for i in range(2 * n)},
        compiler_params=pltpu.CompilerParams(has_side_effects=_EFFECT),
    )(*ins, *handle["sems"], after)
    return outs[:n], outs[n:]


def pass_on_to_sibling(lands, rows, name):
    n = len(lands)

    def body(*refs):
        land_refs = refs[n:2 * n]
        send_sems, recv_sems = refs[2 * n:]
        x, y, c = _mesh_pos()
        copies = []
        for a in range(n):
            for k, (dx, dy) in enumerate(_OTHER_CHIPS):
                slot = 4 * _flip(x, dx) + 2 * _flip(y, dy) + c
                blk = land_refs[a].at[pl.ds(slot * rows[a], rows[a]), :]
                copies.append(pltpu.make_async_remote_copy(
                    src_ref=blk, dst_ref=blk, send_sem=send_sems.at[a, k], recv_sem=recv_sems.at[a, k],
                    device_id=(x, y, 1 - c), device_id_type=pl.DeviceIdType.MESH))
        for cp in copies:
            cp.start()
        for cp in copies:
            cp.wait_recv()
        for cp in copies:
            cp.wait_send()

    any_spec = pl.BlockSpec(memory_space=pl.ANY)
    return pl.pallas_call(
        body, name=name,
        out_shape=[jax.ShapeDtypeStruct(a.shape, a.dtype) for a in lands],
        in_specs=[any_spec] * n, out_specs=[any_spec] * n,
        input_output_aliases={i: i for i in range(n)},
        scratch_shapes=[pltpu.SemaphoreType.DMA((n, len(_OTHER_CHIPS))), pltpu.SemaphoreType.DMA((n, len(_OTHER_CHIPS)))],
    )(*lands)


def slot_sum8(src, land, tr, name):
    rows, cols = land.shape[0] // N_DEV, land.shape[1]
    x, y, c = _mesh_pos()
    me = (4 * x + 2 * y + c).astype(jnp.int32).reshape(1)

    def body(me_ref, src_ref, land_ref, o_ref):
        acc = None
        for d in range(N_DEV):
            term = jnp.where(d == me_ref[0], src_ref[0], land_ref[d]).astype(F32)
            acc = term if acc is None else acc + term
        o_ref[...] = acc

    return pl.pallas_call(
        body, name=name,
        grid_spec=pltpu.PrefetchScalarGridSpec(
            num_scalar_prefetch=1, grid=(rows // tr,),
            in_specs=[pl.BlockSpec((1, tr, cols), lambda i, w: (w[0], i, 0)),
                      pl.BlockSpec((N_DEV, tr, cols), lambda i, w: (0, i, 0))],
            out_specs=pl.BlockSpec((tr, cols), lambda i, w: (i, 0))),
        out_shape=jax.ShapeDtypeStruct((rows, cols), F32),
        compiler_params=_params(("arbitrary",)),
    )(me, src.reshape(N_DEV, rows, cols), land.reshape(N_DEV, rows, cols))


def mm_tn_multi(a_t, bs, tt, name, out_dtype=F32):
    K, T = a_t.shape
    widths = [b.shape[1] for b in bs]
    steps = T // tt

    def body(a_ref, *rest):
        b_refs, o_ref, acc = rest[:-2], rest[-2], rest[-1]

        @pl.when(pl.program_id(0) == 0)
        def _():
            acc[...] = jnp.zeros(acc.shape, F32)

        av = a_ref[...]
        col = 0
        for b_ref, w in zip(b_refs, widths):
            acc[:, col:col + w] += jnp.dot(av, b_ref[...], preferred_element_type=F32)
            col += w

        @pl.when(pl.program_id(0) == steps - 1)
        def _():
            o_ref[...] = acc[...].astype(out_dtype)

    return pl.pallas_call(
        body, name=name, grid=(steps,),
        in_specs=[pl.BlockSpec((K, tt), lambda t: (0, t))] + [pl.BlockSpec((tt, w), lambda t: (t, 0)) for w in widths],
        out_specs=pl.BlockSpec((K, sum(widths)), lambda t: (0, 0)),
        out_shape=jax.ShapeDtypeStruct((K, sum(widths)), out_dtype),
        scratch_shapes=[pltpu.VMEM((K, sum(widths)), F32)],
        compiler_params=_params(("arbitrary",)),
    )(a_t, *bs)


def rms_fwd(x, g, tm, name, with_transpose=False):
    M, K = x.shape

    def body(x_ref, g_ref, o_ref, *t_ref):
        xv = x_ref[...]
        r = lax.rsqrt(jnp.mean(xv * xv, axis=-1, keepdims=True) + RMS_EPS)
        h = ((xv * r) * g_ref[...]).astype(BF16)
        o_ref[...] = h
        if with_transpose:
            t_ref[0][...] = h.T

    out_specs = [pl.BlockSpec((tm, K), lambda i: (i, 0))]
    out_shape = [jax.ShapeDtypeStruct((M, K), BF16)]
    if with_transpose:
        out_specs.append(pl.BlockSpec((K, tm), lambda i: (0, i)))
        out_shape.append(jax.ShapeDtypeStruct((K, M), BF16))
    outs = pl.pallas_call(
        body, name=name, grid=(M // tm,),
        in_specs=[pl.BlockSpec((tm, K), lambda i: (i, 0)), pl.BlockSpec((1, K), lambda i: (0, 0))],
        out_specs=out_specs, out_shape=out_shape,
        compiler_params=_params(("arbitrary",)),
    )(x, g)
    return outs if with_transpose else outs[0]


def rms_bwd(x, g, dh, dres, tm, name):
    M, K = x.shape
    has_res = dres is not None

    def body(*refs):
        if has_res:
            x_ref, g_ref, dh_ref, dres_ref, dx_ref, dg_ref = refs
        else:
            x_ref, g_ref, dh_ref, dx_ref, dg_ref = refs
        xv = x_ref[...]
        r = lax.rsqrt(jnp.mean(xv * xv, axis=-1, keepdims=True) + RMS_EPS)
        xn = xv * r
        dhv = dh_ref[...]
        dxn = dhv * g_ref[...]
        dx = r * (dxn - xn * jnp.mean(dxn * xn, axis=-1, keepdims=True))
        if has_res:
            dx = dx + dres_ref[...]
        dx_ref[...] = dx
        part = jnp.sum(dhv * xn, axis=0, keepdims=True)
        row = lax.broadcasted_iota(jnp.int32, (8, K), 0)
        upd = jnp.where(row == 0, part, 0.0)

        @pl.when(pl.program_id(0) == 0)
        def _():
            dg_ref[...] = upd

        @pl.when(pl.program_id(0) != 0)
        def _():
            dg_ref[...] += upd

    row_spec = pl.BlockSpec((tm, K), lambda i: (i, 0))
    ins = [x, g, dh] + ([dres] if has_res else [])
    in_specs = [row_spec, pl.BlockSpec((1, K), lambda i: (0, 0)), row_spec] + ([row_spec] if has_res else [])
    return pl.pallas_call(
        body, name=name, grid=(M // tm,),
        in_specs=in_specs,
        out_specs=[row_spec, pl.BlockSpec((8, K), lambda i: (0, 0))],
        out_shape=[jax.ShapeDtypeStruct((M, K), F32), jax.ShapeDtypeStruct((8, K), F32)],
        compiler_params=_params(("arbitrary",)),
    )(*ins)


def mm_nn(a, b, tm, tn, name, token=None):
    M, K = a.shape
    N = b.shape[1]
    extra = [] if token is None else [token]

    def body(a_ref, b_ref, *rest):
        rest[-1][...] = jnp.dot(a_ref[...], b_ref[...], preferred_element_type=F32)

    return pl.pallas_call(
        body, name=name, grid=(N // tn, M // tm),
        in_specs=[pl.BlockSpec((tm, K), lambda j, i: (i, 0)), pl.BlockSpec((K, tn), lambda j, i: (0, j))]
        + [pl.BlockSpec(t.shape, lambda j, i: (0, 0)) for t in extra],
        out_specs=pl.BlockSpec((tm, tn), lambda j, i: (i, j)),
        out_shape=jax.ShapeDtypeStruct((M, N), F32),
        compiler_params=_params(("arbitrary", "arbitrary")),
    )(a, b, *extra)


def mm_nt(a, b, tm, tk, name):
    M, K = a.shape
    N = b.shape[0]

    def body(a_ref, b_ref, o_ref):
        part = lax.dot_general(a_ref[...], b_ref[...], (((1,), (1,)), ((), ())), preferred_element_type=F32)

        @pl.when(pl.program_id(1) == 0)
        def _():
            o_ref[...] = part

        @pl.when(pl.program_id(1) != 0)
        def _():
            o_ref[...] += part

    return pl.pallas_call(
        body, name=name, grid=(M // tm, K // tk),
        in_specs=[pl.BlockSpec((tm, tk), lambda i, k: (i, k)), pl.BlockSpec((N, tk), lambda i, k: (0, k))],
        out_specs=pl.BlockSpec((tm, N), lambda i, k: (i, 0)),
        out_shape=jax.ShapeDtypeStruct((M, N), F32),
        compiler_params=_params(("arbitrary", "arbitrary")),
    )(a, b)


def in_proj_bwd_rms(pieces, ws, x, g, dres, tm, token):
    M, N = x.shape

    def body(*refs):
        n = len(pieces)
        p_refs, w_refs = refs[:n], refs[n:n + len(ws)]
        x_ref, g_ref, dres_ref, _, dx_ref, dg_ref = refs[n + len(ws):]
        dh = None
        for p_ref, (arr, group, col) in zip(p_refs, pieces):
            part = lax.dot_general(p_ref[...], w_refs[group][:, col:col + arr.shape[1]], (((1,), (1,)), ((), ())),
                                   preferred_element_type=F32)
            dh = part if dh is None else dh + part
        xv = x_ref[...]
        r = lax.rsqrt(jnp.mean(xv * xv, axis=-1, keepdims=True) + RMS_EPS)
        xn = xv * r
        dxn = dh * g_ref[...]
        dx_ref[...] = r * (dxn - xn * jnp.mean(dxn * xn, axis=-1, keepdims=True)) + dres_ref[...]
        row = lax.broadcasted_iota(jnp.int32, (8, N), 0)
        upd = jnp.where(row == 0, jnp.sum(dh * xn, axis=0, keepdims=True), 0.0)

        @pl.when(pl.program_id(0) == 0)
        def _():
            dg_ref[...] = upd

        @pl.when(pl.program_id(0) != 0)
        def _():
            dg_ref[...] += upd

    row_spec = pl.BlockSpec((tm, N), lambda i: (i, 0))
    return pl.pallas_call(
        body, name="in_proj_bwd", grid=(M // tm,),
        in_specs=[pl.BlockSpec((tm, arr.shape[1]), lambda i: (i, 0)) for arr, _, _ in pieces]
        + [pl.BlockSpec(w.shape, lambda i: (0, 0)) for w in ws]
        + [row_spec, pl.BlockSpec((1, N), lambda i: (0, 0)), row_spec, pl.BlockSpec(token.shape, lambda i: (0, 0))],
        out_specs=[row_spec, pl.BlockSpec((8, N), lambda i: (0, 0))],
        out_shape=[jax.ShapeDtypeStruct((M, N), F32), jax.ShapeDtypeStruct((8, N), F32)],
        compiler_params=_params(("arbitrary",)),
    )(*[arr for arr, _, _ in pieces], *ws, x, g, dres, token)


def _log_sigmoid(z):
    return jnp.minimum(z, 0.0) - jnp.log(1.0 + jnp.exp(-jnp.abs(z)))


def _tri(n, lower):
    r = lax.broadcasted_iota(jnp.int32, (n, n), 0)
    c = lax.broadcasted_iota(jnp.int32, (n, n), 1)
    return jnp.where((r >= c) if lower else (r <= c), 1.0, 0.0).astype(F32)


def fox_gate(proj3, b_pad):
    B, S, _ = proj3.shape
    nblk = S // TK

    def body(f_ref, b_ref, o_ref):
        tri = _tri(TK, True)
        carry = jnp.zeros((1, LANES), F32)
        for n in range(nblk):
            z = f_ref[0, n * TK:(n + 1) * TK, :] + b_ref[...]
            logf = _log_sigmoid(z)
            cs = jnp.dot(tri, logf, preferred_element_type=F32, precision=lax.Precision.HIGHEST) + carry
            carry = cs[TK - 1:TK, :]
            o_ref[0, n * TK:(n + 1) * TK, :] = -cs

    return pl.pallas_call(
        body, name="fox_gate", grid=(B,),
        in_specs=[pl.BlockSpec((1, S, LANES), lambda b: (b, 0, A_FLOG // LANES)),
                  pl.BlockSpec((1, LANES), lambda b: (0, 0))],
        out_specs=pl.BlockSpec((1, S, LANES), lambda b: (b, 0, 0)),
        out_shape=jax.ShapeDtypeStruct((B, S, LANES), F32),
        compiler_params=_params(("arbitrary",)),
    )(proj3, b_pad)


def fox_gate_bwd(drow, dneg, proj3, b_pad):
    B, S, _ = proj3.shape
    nblk = S // TK

    def body(d_ref, r_ref, f_ref, b_ref, o_ref, db_ref):
        tri = _tri(TK, False)
        lane = lax.broadcasted_iota(jnp.int32, (TK, LANES), 1)
        carry = jnp.zeros((1, LANES), F32)
        dbsum = jnp.zeros((1, LANES), F32)
        for n in reversed(range(nblk)):
            dk_side = None
            for hp in range(FOX_HEADS // 2):
                two = jnp.where(lane < 2, r_ref[0, n * TK:(n + 1) * TK, hp * LANES:(hp + 1) * LANES], 0.0)
                two = pltpu.roll(two, 2 * hp, 1) if hp else two
                dk_side = two if dk_side is None else dk_side + two
            dc = jnp.where(lane < FOX_HEADS, d_ref[0, :, n * TK:(n + 1) * TK].T - dk_side, 0.0)
            rs = jnp.dot(tri, dc, preferred_element_type=F32, precision=lax.Precision.HIGHEST) + carry
            carry = rs[0:1, :]
            z = f_ref[0, n * TK:(n + 1) * TK, :] + b_ref[...]
            dz = rs * (1.0 / (1.0 + jnp.exp(z)))
            o_ref[0, n * TK:(n + 1) * TK, :] = dz.astype(BF16)
            dbsum = dbsum + jnp.sum(dz, axis=0, keepdims=True)
        row = lax.broadcasted_iota(jnp.int32, (8, LANES), 0)
        upd = jnp.where(row == 0, dbsum, 0.0)

        @pl.when(pl.program_id(0) == 0)
        def _():
            db_ref[...] = upd

        @pl.when(pl.program_id(0) != 0)
        def _():
            db_ref[...] += upd

    return pl.pallas_call(
        body, name="fox_gate_bwd", grid=(B,),
        in_specs=[pl.BlockSpec((1, LANES, S), lambda b: (b, 0, 0)),
                  pl.BlockSpec((1, S, FOX_W), lambda b: (b, 0, 0)),
                  pl.BlockSpec((1, S, LANES), lambda b: (b, 0, A_FLOG // LANES)),
                  pl.BlockSpec((1, LANES), lambda b: (0, 0))],
        out_specs=[pl.BlockSpec((1, S, LANES), lambda b: (b, 0, 0)), pl.BlockSpec((8, LANES), lambda b: (0, 0))],
        out_shape=[jax.ShapeDtypeStruct((B, S, LANES), BF16), jax.ShapeDtypeStruct((8, LANES), F32)],
        compiler_params=_params(("arbitrary",)),
    )(drow, dneg, proj3, b_pad)


def _rope_tables(S):
    half = ROPE_DIM // 2
    f32 = np.float32
    pos = np.arange(S, dtype=f32)
    inv_freq = f32(1.0) / np.power(f32(ROPE_THETA), np.arange(0, ROPE_DIM, 2, dtype=f32) / f32(ROPE_DIM)).astype(f32)
    ang = (pos[:, None] * inv_freq[None, :]).astype(f32).astype(np.float64)
    cos, sin = np.cos(ang).astype(f32), np.sin(ang).astype(f32)
    one = np.ones((S, HEAD_DIM - ROPE_DIM), f32)
    zero = np.zeros((S, HEAD_DIM - ROPE_DIM), f32)
    zh = np.zeros((S, half), f32)
    c = np.concatenate([cos, cos, one], axis=1)
    s1 = np.concatenate([-sin, zh, zero], axis=1)
    s2 = np.concatenate([zh, sin, zero], axis=1)
    return tuple(jnp.asarray(np.concatenate([t, t], axis=1)) for t in (c, s1, s2))


_HALF_ROPE = ROPE_DIM // 2


def _rope(t, c, s1, s2):
    return t * c + pltpu.roll(t, LANES - _HALF_ROPE, 1) * s1 + pltpu.roll(t, _HALF_ROPE, 1) * s2


def _rope_bwd(d, c, s1, s2):
    return d * c + pltpu.roll(d * s1, _HALF_ROPE, 1) + pltpu.roll(d * s2, LANES - _HALF_ROPE, 1)


def _scale_parts(scale):
    m, _ = math.frexp(scale)
    return (scale, None) if m == 0.5 else (None, scale)


def _log_masks(S, kind):
    nd = 1 if kind == "causal" else S // TQ
    a = np.arange(TQ)[:, None]
    b = np.arange(TK)[None, :]
    out = np.zeros((nd, TQ, TK), np.float32)
    for d in range(nd):
        delta = d * TQ + a - b
        if kind == "causal":
            m = (delta >= 0).astype(np.float64)
        else:
            m = sum(((delta >= 0) & (delta % dil == 0) & (delta <= w)).astype(np.float64) for w, dil in DILATIONS)
        out[d] = np.where(m > 0, np.log(np.maximum(m, 1.0)), NEG_INF)
    return jnp.asarray(out)


def _attn_setup(kind):
    pair = kind != "mem"
    e_dim = HEAD_DIM if pair else MEM_HEAD_DIM
    q_fold, s_scale = _scale_parts(1.0 / math.sqrt(e_dim))
    return dict(pair=pair, col0={"fox": A_FOX, "dil": B_DIL, "mem": B_MQ}[kind],
                n_blocks=FOX_HEADS // 2 if pair else MEM_HEADS, q_fold=q_fold, s_scale=s_scale,
                nh=2 if pair else 1)


def _cat(parts, axis):
    return parts[0] if len(parts) == 1 else jnp.concatenate(parts, axis=axis)


def _log_masks_t(S, kind):
    return jnp.swapaxes(_log_masks(S, kind), 1, 2)


def _head_rows(hh, pair):
    row = lax.broadcasted_iota(jnp.int32, (LANES, 1), 0)
    if not pair:
        return row >= 0
    return (row >= HEAD_DIM * hh) & (row < HEAD_DIM * (hh + 1))


def _attn_t_inputs(kind, src, S, negc_cols, mask, rope, kv):
    cfg = _attn_setup(kind)
    col0 = cfg["col0"]
    ins, in_specs = [], []
    if cfg["pair"]:
        ins.append(src)
        in_specs.append(pl.BlockSpec((1, S, PAIR_W), lambda b, h: (b, 0, col0 // PAIR_W + h)))
    else:
        ins += [src, kv, kv]
        in_specs += [pl.BlockSpec((1, S, LANES), lambda b, h: (b, 0, col0 // LANES + h)),
                     pl.BlockSpec((1, MEM_LEN, LANES), lambda b, h: (b, 0, h)),
                     pl.BlockSpec((1, MEM_LEN, LANES), lambda b, h: (b, 0, MEM_HEADS + h))]
    if negc_cols is not None:
        ins.append(negc_cols)
        in_specs.append(pl.BlockSpec((1, S, LANES), lambda b, h: (b, 0, 0)))
    if mask is not None:
        ins.append(mask)
        in_specs.append(pl.BlockSpec(mask.shape, lambda b, h: (0, 0, 0)))
    if rope is not None:
        ins += list(rope)
        in_specs += [pl.BlockSpec((S, LANES), lambda b, h: (0, 0))] * 3
    return ins, in_specs


def _attn_t_prep(cfg, refs, S, Sk, *, qT2s, ks, vs=None, vTs=None, kTs=None, nb=None):
    pair, nh = cfg["pair"], cfg["nh"]
    lane = lax.broadcasted_iota(jnp.int32, (1, LANES), 1)
    rope_refs = refs["rope"]

    def prep_q(n):
        rows = slice(n * TQ, (n + 1) * TQ)
        q = refs["load_q"](rows)
        if rope_refs is not None:
            q = _rope(q, *[t[rows, :] for t in rope_refs])
        if cfg["q_fold"] is not None:
            q = q * cfg["q_fold"]
        qtb = q.astype(BF16).T
        for hh in range(nh):
            qT2s[n, :, hh * TQ:(hh + 1) * TQ] = jnp.where(_head_rows(hh, pair), qtb, jnp.zeros_like(qtb))

    def prep_kv(n):
        rows = slice(n * TK, (n + 1) * TK)
        k, v = refs["load_kv"](rows)
        if rope_refs is not None:
            k = _rope(k, *[t[rows, :] for t in rope_refs])
        kb = k.astype(BF16)
        vb = v.astype(BF16)
        ks[rows, :] = kb
        if vs is not None:
            vs[rows, :] = vb
        if vTs is not None:
            vTs[n] = vb.T
        if kTs is not None:
            kTs[n] = kb.T
        if nb is not None:
            blk = refs["negc"][0, rows, :]
            for hh in range(nh):
                h = 2 * refs["block"] + hh
                col = jnp.sum(jnp.where(lane == h, blk, 0.0), axis=1, keepdims=True)
                nb[hh, rows, :] = jnp.broadcast_to(col, (TK, LANES))

    for n in range(S // TQ):
        prep_q(n)
    for n in range(Sk // TK):
        prep_kv(n)


def _raw_scores_t(cfg, k, qT2):
    sT = jnp.dot(k, qT2, preferred_element_type=F32)
    if cfg["s_scale"] is not None:
        sT = sT * cfg["s_scale"]
    return sT


def _bias_mask_t(cfg, sT, nb, mask_ref, kc, midx):
    nh = cfg["nh"]
    if nb is None and midx is None:
        return sT
    parts = []
    for hh in range(nh):
        t = sT[:, hh * TQ:(hh + 1) * TQ]
        if nb is not None:
            t = t + jnp.concatenate([nb[hh, kc, :]] * (TQ // LANES), axis=1)
        if midx is not None:
            t = t + mask_ref[midx]
        parts.append(t)
    return _cat(parts, 1)


def _tile_pairs(kind, nq, nk):
    if kind == "mem":
        return [(i, j) for i in range(nq) for j in range(nk)], (lambda i, j: None)
    pairs = [(i, j) for i in range(nq) for j in range(i + 1)]
    if kind == "fox":
        return pairs, (lambda i, j: 0 if j == i else None)
    return pairs, (lambda i, j: i - j)


def attn_fwd(kind, src, S, *, negc_cols=None, mask=None, rope=None, kv=None):
    B = src.shape[0]
    cfg = _attn_setup(kind)
    pair, nh = cfg["pair"], cfg["nh"]
    Sk = S if pair else MEM_LEN
    has_bias, has_rope = negc_cols is not None, rope is not None
    R = nh * TQ
    nq, nk = S // TQ, Sk // TK
    pairs, mask_index = _tile_pairs(kind, nq, nk)

    def body(*refs):
        refs = list(refs)
        if pair:
            qkv_ref = refs.pop(0)
            load_q = lambda rows: qkv_ref[0, rows, 0:LANES]
            load_kv = lambda rows: (qkv_ref[0, rows, LANES:2 * LANES], qkv_ref[0, rows, 2 * LANES:3 * LANES])
        else:
            q_ref, k_ref, v_ref = refs.pop(0), refs.pop(0), refs.pop(0)
            load_q = lambda rows: q_ref[0, rows, :]
            load_kv = lambda rows: (k_ref[0, rows, :], v_ref[0, rows, :])
        negc_ref = refs.pop(0) if has_bias else None
        mask_ref = refs.pop(0) if mask is not None else None
        rope_refs = [refs.pop(0) for _ in range(3)] if has_rope else None
        o_ref, lse_ref, qT2s, ks, vTs, s_a, s_b, p_a, p_b = refs[:9]
        nb = refs[9] if has_bias else None
        _attn_t_prep(cfg, dict(load_q=load_q, load_kv=load_kv, rope=rope_refs, negc=negc_ref,
                               block=pl.program_id(1)), S, Sk, qT2s=qT2s, ks=ks, vTs=vTs, nb=nb)

        def cols(j):
            return slice(j * TK, (j + 1) * TK)

        def scores(i, j):
            return _raw_scores_t(cfg, ks[cols(j), :], qT2s[i])

        def finish(i, m, l, accT):
            oT2 = accT / l
            oT = jnp.where(_head_rows(0, True), oT2[:, 0:TQ], oT2[:, TQ:2 * TQ]) if pair else oT2
            o_ref[0, i * TQ:(i + 1) * TQ, :] = oT.T
            lse_ref[0, 0, i:i + 1, :] = m + jnp.log(l)

        s_bufs, p_bufs = (s_a, s_b), (p_a, p_b)
        s_bufs[0][...] = scores(*pairs[0])
        m = l = accT = None
        for t, (i, j) in enumerate(pairs):
            cur, oth = t % 2, 1 - t % 2
            if t > 0:
                i_prev, j_prev = pairs[t - 1]
                pv = jnp.dot(vTs[j_prev], p_bufs[oth][...], preferred_element_type=F32)
                acc_full = pv if accT is None else accT + pv
            if t + 1 < len(pairs):
                s_bufs[oth][...] = scores(*pairs[t + 1])
            first = j == 0
            if first and t > 0:
                finish(i_prev, m, l, acc_full)
            sT = _bias_mask_t(cfg, s_bufs[cur][...], nb, mask_ref, cols(j), mask_index(i, j))
            m_tile = jnp.max(sT, axis=0, keepdims=True)
            m_new = m_tile if first else jnp.maximum(m, m_tile)
            p = jnp.exp(sT - m_new)
            p_bufs[cur][...] = p.astype(BF16)
            if first:
                l, accT = jnp.sum(p, axis=0, keepdims=True), None
            else:
                alpha = jnp.exp(m - m_new)
                l, accT = alpha * l + jnp.sum(p, axis=0, keepdims=True), acc_full * alpha
            m = m_new
        i_last, j_last = pairs[-1]
        pv = jnp.dot(vTs[j_last], p_bufs[(len(pairs) - 1) % 2][...], preferred_element_type=F32)
        finish(i_last, m, l, pv if accT is None else accT + pv)

    ins, in_specs = _attn_t_inputs(kind, src, S, negc_cols, mask, rope, kv)
    W = cfg["n_blocks"] * LANES
    scratch = [pltpu.VMEM((nq, LANES, R), BF16), pltpu.VMEM((Sk, LANES), BF16), pltpu.VMEM((nk, LANES, TK), BF16),
               pltpu.VMEM((TK, R), F32), pltpu.VMEM((TK, R), F32), pltpu.VMEM((TK, R), BF16), pltpu.VMEM((TK, R), BF16)]
    if has_bias:
        scratch.append(pltpu.VMEM((nh, Sk, LANES), F32))
    return pl.pallas_call(
        body, name=kind + "_attn_fwd", grid=(B, cfg["n_blocks"]),
        in_specs=in_specs,
        out_specs=[pl.BlockSpec((1, S, LANES), lambda b, h: (b, 0, h)),
                   pl.BlockSpec((1, 1, nq, R), lambda b, h: (b, h, 0, 0))],
        out_shape=[jax.ShapeDtypeStruct((B, S, W), F32), jax.ShapeDtypeStruct((B, cfg["n_blocks"], nq, R), F32)],
        scratch_shapes=scratch,
        compiler_params=_params(("arbitrary", "arbitrary")),
    )(*ins)


def attn_bwd(kind, src, do, o, lse, S, *, negc_cols=None, mask=None, rope=None, kv=None, token=None):
    B = src.shape[0]
    cfg = _attn_setup(kind)
    pair, nh, s_scale, q_fold = cfg["pair"], cfg["nh"], cfg["s_scale"], cfg["q_fold"]
    Sk = S if pair else MEM_LEN
    has_bias, has_rope = negc_cols is not None, rope is not None
    R = nh * TQ
    nq, nk = S // TQ, Sk // TK
    pairs, mask_index = _tile_pairs(kind, nq, nk)

    def body(*refs):
        refs = list(refs)
        if pair:
            qkv_ref = refs.pop(0)
            load_q = lambda rows: qkv_ref[0, rows, 0:LANES]
            load_kv = lambda rows: (qkv_ref[0, rows, LANES:2 * LANES], qkv_ref[0, rows, 2 * LANES:3 * LANES])
        else:
            q_ref, k_ref, v_ref = refs.pop(0), refs.pop(0), refs.pop(0)
            load_q = lambda rows: q_ref[0, rows, :]
            load_kv = lambda rows: (k_ref[0, rows, :], v_ref[0, rows, :])
        negc_ref = refs.pop(0) if has_bias else None
        mask_ref = refs.pop(0) if mask is not None else None
        rope_refs = [refs.pop(0) for _ in range(3)] if has_rope else None
        do_ref, o_ref, lse_ref = refs.pop(0), refs.pop(0), refs.pop(0)
        if token is not None:
            refs.pop(0)
        if pair:
            dqkv_ref = refs.pop(0)
            dneg_ref = refs.pop(0) if has_bias else None
            drow_ref = refs.pop(0) if has_bias else None
        else:
            dq_ref, dk_ref, dv_ref = refs.pop(0), refs.pop(0), refs.pop(0)
        qT2s, ks, vs, kTs, doT2s, delta_s, dk_acc, dv_acc = refs[:8]
        bufs_a, bufs_b = refs[8:12], refs[12:16]
        nb, dneg_acc = (refs[16], refs[17]) if has_bias else (None, None)
        lane = lax.broadcasted_iota(jnp.int32, (1, LANES), 1)
        _attn_t_prep(cfg, dict(load_q=load_q, load_kv=load_kv, rope=rope_refs, negc=negc_ref,
                               block=pl.program_id(1)), S, Sk,
                     qT2s=qT2s, ks=ks, vs=vs, kTs=kTs, nb=nb)

        def prep_do(n):
            rows = slice(n * TQ, (n + 1) * TQ)
            doT = do_ref[0, rows, :].astype(BF16).astype(F32).T
            prodT = doT * o_ref[0, rows, :].T
            doTb = doT.astype(BF16)
            for hh in range(nh):
                hm = _head_rows(hh, pair)
                doT2s[n, :, hh * TQ:(hh + 1) * TQ] = jnp.where(hm, doTb, jnp.zeros_like(doTb))
                delta_s[n:n + 1, hh * TQ:(hh + 1) * TQ] = jnp.sum(jnp.where(hm, prodT, 0.0), axis=0, keepdims=True)

        for n in range(nq):
            prep_do(n)
        dk_acc[...] = jnp.zeros(dk_acc.shape, F32)
        dv_acc[...] = jnp.zeros(dv_acc.shape, F32)
        if has_bias:
            dneg_acc[...] = jnp.zeros(dneg_acc.shape, F32)

        def cols(j):
            return slice(j * TK, (j + 1) * TK)

        nt_dims = (((1,), (1,)), ((), ()))

        def first_products(i, j, bufs):
            bufs[0][...] = _raw_scores_t(cfg, ks[cols(j), :], qT2s[i])
            bufs[1][...] = jnp.dot(vs[cols(j), :], doT2s[i], preferred_element_type=F32)

        def last_products(i, j, bufs, dqT2):
            dv_acc[j] += lax.dot_general(doT2s[i], bufs[2][...], nt_dims, preferred_element_type=F32)
            dk_acc[j] += lax.dot_general(qT2s[i], bufs[3][...], nt_dims, preferred_element_type=F32)
            dq = jnp.dot(kTs[j], bufs[3][...], preferred_element_type=F32)
            return dq if dqT2 is None else dqT2 + dq

        def finish_q(i, dqT2, drow):
            rows = slice(i * TQ, (i + 1) * TQ)
            dqT = jnp.where(_head_rows(0, True), dqT2[:, 0:TQ], dqT2[:, TQ:2 * TQ]) if pair else dqT2
            dq = dqT.T
            if q_fold is not None:
                dq = dq * q_fold
            if has_rope:
                dq = _rope_bwd(dq, *[t[rows, :] for t in rope_refs])
            if pair:
                dqkv_ref[0, rows, 0:LANES] = dq.astype(BF16)
            else:
                dq_ref[0, rows, :] = dq.astype(BF16)
            if has_bias:
                drow_ref[0, 0, i:i + 1, :] = drow

        bufs = (bufs_a, bufs_b)
        first_products(*pairs[0], bufs[0])
        dqT2 = drow = None
        for t, (i, j) in enumerate(pairs):
            cur, oth = bufs[t % 2], bufs[1 - t % 2]
            first = j == 0
            if first and t > 0:
                i_prev, j_prev = pairs[t - 1]
                finish_q(i_prev, last_products(i_prev, j_prev, oth, dqT2), drow)
                dqT2 = drow = None
            sT = _bias_mask_t(cfg, cur[0][...], nb, mask_ref, cols(j), mask_index(i, j))
            pT = jnp.exp(sT - lse_ref[0, 0, i:i + 1, :])
            dsT = pT * (cur[1][...] - delta_s[i:i + 1, :])
            if has_bias:
                tile_rows = jnp.sum(dsT, axis=0, keepdims=True)
                drow = tile_rows if drow is None else drow + tile_rows
                for hh in range(nh):
                    part = dsT[:, hh * TQ:hh * TQ + LANES]
                    for u in range(1, TQ // LANES):
                        part = part + dsT[:, hh * TQ + u * LANES:hh * TQ + (u + 1) * LANES]
                    dneg_acc[hh, cols(j), :] += part
            if s_scale is not None:
                dsT = dsT * s_scale
            cur[2][...] = pT.astype(BF16)
            cur[3][...] = dsT.astype(BF16)
            if not first:
                dqT2 = last_products(*pairs[t - 1], oth, dqT2)
            if t + 1 < len(pairs):
                first_products(*pairs[t + 1], oth)
        i_last, j_last = pairs[-1]
        finish_q(i_last, last_products(i_last, j_last, bufs[(len(pairs) - 1) % 2], dqT2), drow)

        for n in range(nk):
            rows = slice(n * TK, (n + 1) * TK)
            dk = dk_acc[n].T
            dv = dv_acc[n].T
            if has_rope:
                dk = _rope_bwd(dk, *[t[rows, :] for t in rope_refs])
            if pair:
                dqkv_ref[0, rows, LANES:2 * LANES] = dk.astype(BF16)
                dqkv_ref[0, rows, 2 * LANES:3 * LANES] = dv.astype(BF16)
            else:
                dk_ref[0, rows, :] = dk.astype(BF16)
                dv_ref[0, rows, :] = dv.astype(BF16)
            if has_bias:
                x0 = jnp.sum(dneg_acc[0, rows, :], axis=1, keepdims=True)
                x1 = jnp.sum(dneg_acc[1, rows, :], axis=1, keepdims=True)
                dneg_ref[0, rows, :] = jnp.where(lane == 0, x0, jnp.where(lane == 1, x1, 0.0))

    ins, in_specs = _attn_t_inputs(kind, src, S, negc_cols, mask, rope, kv)
    row_spec = pl.BlockSpec((1, S, LANES), lambda b, h: (b, 0, h))
    vec_spec = pl.BlockSpec((1, 1, nq, R), lambda b, h: (b, h, 0, 0))
    ins += [do, o, lse]
    in_specs += [row_spec, row_spec, vec_spec]
    if token is not None:
        ins.append(token)
        in_specs.append(pl.BlockSpec(token.shape, lambda b, h: (0, 0)))
    W = cfg["n_blocks"] * LANES
    if pair:
        out_specs = [pl.BlockSpec((1, S, PAIR_W), lambda b, h: (b, 0, h))]
        out_shape = [jax.ShapeDtypeStruct((B, S, 3 * W), BF16)]
        if has_bias:
            out_specs += [row_spec, vec_spec]
            out_shape += [jax.ShapeDtypeStruct((B, S, W), F32), jax.ShapeDtypeStruct((B, cfg["n_blocks"], nq, R), F32)]
    else:
        kv_spec = pl.BlockSpec((1, MEM_LEN, LANES), lambda b, h: (b, 0, h))
        out_specs = [row_spec, kv_spec, kv_spec]
        out_shape = [jax.ShapeDtypeStruct((B, S, W), BF16)] + [jax.ShapeDtypeStruct((B, MEM_LEN, W), BF16)] * 2
    scratch = [pltpu.VMEM((nq, LANES, R), BF16), pltpu.VMEM((Sk, LANES), BF16),
               pltpu.VMEM((Sk, LANES), BF16), pltpu.VMEM((nk, LANES, TK), BF16), pltpu.VMEM((nq, LANES, R), BF16),
               pltpu.VMEM((nq, R), F32), pltpu.VMEM((nk, LANES, TK), F32), pltpu.VMEM((nk, LANES, TK), F32)]
    pair_bufs = [pltpu.VMEM((TK, R), F32), pltpu.VMEM((TK, R), F32), pltpu.VMEM((TK, R), BF16), pltpu.VMEM((TK, R), BF16)]
    scratch += pair_bufs + pair_bufs
    if has_bias:
        scratch += [pltpu.VMEM((nh, Sk, LANES), F32), pltpu.VMEM((nh, Sk, LANES), F32)]
    return pl.pallas_call(
        body, name=kind + "_attn_bwd", grid=(B, cfg["n_blocks"]),
        in_specs=in_specs, out_specs=out_specs, out_shape=out_shape, scratch_shapes=scratch,
        compiler_params=_params(("arbitrary", "arbitrary")),
    )(*ins)


def _sigmoid(g):
    return 1.0 / (1.0 + jnp.exp(-g))


def out_step(proj, o_fox, o_dil, o_mem, w_out, x, target, gf, tm):
    T = x.shape[0]

    def body(fg_ref, dg_ref, mg_ref, of_ref, od_ref, om_ref, w_ref, x_ref, t_ref, gf_ref,
             dx_ref, dof_ref, dod_ref, dom_ref, dfg_ref, ddg_ref, dmg_ref, gw_ref, sm_ref, gw_acc):
        branches = []
        for g_ref, o_ref in ((fg_ref, of_ref), (dg_ref, od_ref), (mg_ref, om_ref)):
            g = g_ref[...]
            sg = _sigmoid(g)
            o = o_ref[...]
            branches.append((g, sg, o))
        ymix = jnp.concatenate([(o * (g * sg)).astype(BF16) for g, sg, o in branches], axis=1)
        x2 = x_ref[...] + jnp.dot(ymix, w_ref[...], preferred_element_type=F32)
        r = lax.rsqrt(jnp.mean(x2 * x2, axis=-1, keepdims=True) + RMS_EPS)
        yn = x2 * r
        err = yn * gf_ref[...] - t_ref[...]
        loss = 0.5 * jnp.sum(jnp.sum(err * err, axis=-1, keepdims=True) / D_MODEL, axis=0, keepdims=True)
        dyf = err / D_MODEL
        dgf = jnp.sum(dyf * yn, axis=0, keepdims=True)
        dyn = dyf * gf_ref[...]
        dx2 = r * (dyn - yn * jnp.mean(dyn * yn, axis=-1, keepdims=True))
        dx_ref[...] = dx2
        dxb = dx2.astype(BF16)
        dmix = lax.dot_general(dxb, w_ref[...], (((1,), (1,)), ((), ())), preferred_element_type=F32)
        col = 0
        for (g, sg, o), do_ref, dgate_ref in zip(branches, (dof_ref, dod_ref, dom_ref), (dfg_ref, ddg_ref, dmg_ref)):
            d = dmix[:, col:col + g.shape[1]]
            col += g.shape[1]
            do_ref[...] = (d * (g * sg)).astype(BF16)
            dgate_ref[...] = (d * o * (sg * (1.0 + g * (1.0 - sg)))).astype(BF16)
        row = lax.broadcasted_iota(jnp.int32, (8, D_MODEL), 0)
        upd = jnp.where(row == 0, dgf, jnp.where(row == 1, loss, 0.0))

        @pl.when(pl.program_id(0) == 0)
        def _():
            sm_ref[...] = jnp.zeros(sm_ref.shape, F32)
            gw_acc[...] = jnp.zeros(gw_acc.shape, F32)

        sm_ref[...] += upd
        gw_acc[...] += lax.dot_general(ymix, dxb, (((0,), (0,)), ((), ())), preferred_element_type=F32)

        @pl.when(pl.program_id(0) == T // tm - 1)
        def _():
            gw_ref[...] = gw_acc[...].astype(BF16)

    def rows(w, col=0):
        return pl.BlockSpec((tm, w), lambda i: (i, col))

    return pl.pallas_call(
        body, name="out_step", grid=(T // tm,),
        in_specs=[rows(FOX_W, B_FG // FOX_W), rows(DIL_W, B_DG // DIL_W), rows(MEM_W, B_MG // MEM_W),
                  rows(FOX_W), rows(DIL_W), rows(MEM_W),
                  pl.BlockSpec((MIX_W, D_MODEL), lambda i: (0, 0)),
                  rows(D_MODEL), rows(D_MODEL), pl.BlockSpec((1, D_MODEL), lambda i: (0, 0))],
        out_specs=[rows(D_MODEL), rows(FOX_W), rows(DIL_W), rows(MEM_W), rows(FOX_W), rows(DIL_W), rows(MEM_W),
                   pl.BlockSpec((MIX_W, D_MODEL), lambda i: (0, 0)), pl.BlockSpec((8, D_MODEL), lambda i: (0, 0))],
        out_shape=[jax.ShapeDtypeStruct((T, D_MODEL), F32), jax.ShapeDtypeStruct((T, FOX_W), BF16),
                   jax.ShapeDtypeStruct((T, DIL_W), BF16), jax.ShapeDtypeStruct((T, MEM_W), BF16),
                   jax.ShapeDtypeStruct((T, FOX_W), BF16), jax.ShapeDtypeStruct((T, DIL_W), BF16),
                   jax.ShapeDtypeStruct((T, MEM_W), BF16), jax.ShapeDtypeStruct((MIX_W, D_MODEL), BF16),
                   jax.ShapeDtypeStruct((8, D_MODEL), F32)],
        scratch_shapes=[pltpu.VMEM((MIX_W, D_MODEL), F32)],
        compiler_params=_params(("arbitrary",)),
    )(proj, proj, proj, o_fox, o_dil, o_mem, w_out, x, target, gf)


def adamw(w, g, m, v, tr, name):
    lead = w.shape[:-2]
    R, C = w.shape[-2:]
    zeros = (0,) * len(lead)

    def body(w_ref, g_ref, m_ref, v_ref, d_ref, mo_ref, vo_ref):
        gv = g_ref[...]
        mn = ADAM_B1 * m_ref[...] + (1.0 - ADAM_B1) * gv
        vn = ADAM_B2 * v_ref[...] + (1.0 - ADAM_B2) * jnp.square(gv)
        m_hat = mn / (1.0 - ADAM_B1 ** ADAM_STEP)
        v_hat = vn / (1.0 - ADAM_B2 ** ADAM_STEP)
        d_ref[...] = -ADAM_LR * (m_hat / (jnp.sqrt(v_hat) + ADAM_EPS) + ADAM_WD * w_ref[...])
        mo_ref[...] = mn
        vo_ref[...] = vn

    spec = pl.BlockSpec((1,) * len(lead) + (tr, C), lambda i: zeros + (i, 0))
    return pl.pallas_call(
        body, name=name, grid=(pl.cdiv(R, tr),),
        in_specs=[spec] * 4, out_specs=[spec] * 3,
        out_shape=[jax.ShapeDtypeStruct(w.shape, F32)] * 3,
        compiler_params=_params(("arbitrary",)),
    )(w, g, m, v)


def _pad_row(v, width):
    return jnp.concatenate([v, jnp.zeros((1, width - v.shape[1]), v.dtype)], axis=1)


def local_grads(x, mem, norm_g, b_forget, mem_norm_g, final_norm_g, loss_target, w_in_a, first_token, late_weights,
                start_exchange):
    B, S, D = x.shape
    T = B * S
    xt = x.reshape(T, D)
    memt = mem.reshape(B * MEM_LEN, D)
    b_pad = _pad_row(b_forget, LANES)

    h, h_t = rms_fwd(xt, norm_g, 512, "rms_x", with_transpose=True)
    proj_a = mm_nn(h, w_in_a, 512, PA, "in_proj_a", first_token)
    proj_a3 = proj_a.reshape(B, S, PA)

    negc = fox_gate(proj_a3, b_pad)
    causal = _log_masks_t(S, "causal")
    dilated = _log_masks_t(S, "dilated")
    rope = _rope_tables(S)

    o_fox, lse_fox = attn_fwd("fox", proj_a3, S, negc_cols=negc, mask=causal)

    w_in_b, w_kv, w_out = late_weights(o_fox)
    proj_b = mm_nn(h, w_in_b, 512, PB // 2, "in_proj_b")
    proj_b3 = proj_b.reshape(B, S, PB)
    o_dil, lse_dil = attn_fwd("dil", proj_b3, S, mask=dilated, rope=rope)

    mh, mh_t = rms_fwd(memt, mem_norm_g, B * MEM_LEN, "rms_mem", with_transpose=True)
    mkv = mm_nn(mh, w_kv, B * MEM_LEN, 2 * MEM_W, "mem_kv_proj")
    mkv3 = mkv.reshape(B, MEM_LEN, 2 * MEM_W)
    o_mem, lse_mem = attn_fwd("mem", proj_b3, S, kv=mkv3)

    dx2, do_fox, do_dil, do_mem, dfg, ddg, dmg, g_out, small_out = out_step(
        proj_b, o_fox.reshape(T, FOX_W), o_dil.reshape(T, DIL_W), o_mem.reshape(T, MEM_W), w_out,
        xt, loss_target.reshape(T, D), final_norm_g.reshape(1, D), 256)

    gates = [(dfg, 1, B_FG), (ddg, 1, B_DG), (dmg, 1, B_MG)]
    g_gates = mm_tn_multi(h_t, [piece[0] for piece in gates], 1024, "w_in_grad_gates", BF16)
    first, token = start_exchange([g_gates, g_out], "early_exchange_a")

    dqkv_fox, dneg, drow = attn_bwd("fox", proj_a3, do_fox.reshape(B, S, FOX_W), o_fox, lse_fox, S,
                                    negc_cols=negc, mask=causal, token=token)
    drow = drow.reshape(B, FOX_HEADS // 2, S // TQ, 2, TQ).transpose(0, 1, 3, 2, 4).reshape(B, FOX_HEADS, S)
    drow = jnp.pad(drow, ((0, 0), (0, LANES - FOX_HEADS), (0, 0)))
    dflog, db_part = fox_gate_bwd(drow, dneg, proj_a3, b_pad)
    fox = [(dqkv_fox.reshape(T, 3 * FOX_W), 0, A_FOX), (dflog.reshape(T, LANES), 0, A_FLOG)]
    g_fox = mm_tn_multi(h_t, [piece[0] for piece in fox], 1024, "w_in_grad_fox", BF16)
    second, token = start_exchange([g_fox], "early_exchange_b")

    (dqkv_dil,) = attn_bwd("dil", proj_b3, do_dil.reshape(B, S, DIL_W), o_dil, lse_dil, S, mask=dilated, rope=rope,
                           token=token)
    dil = [(dqkv_dil.reshape(T, 3 * DIL_W), 1, B_DIL)]
    g_dil = mm_tn_multi(h_t, [piece[0] for piece in dil], 1024, "w_in_grad_dil", BF16)
    third, token = start_exchange([g_dil], "early_exchange_c")

    dmq, dmk, dmv = attn_bwd("mem", proj_b3, do_mem.reshape(B, S, MEM_W), o_mem, lse_mem, S, kv=mkv3, token=token)
    mq = [(dmq.reshape(T, MEM_W), 1, B_MQ)]
    g_mq = mm_tn_multi(h_t, [piece[0] for piece in mq], 1024, "w_in_grad_mq", BF16)
    dmkv = jnp.concatenate([dmk, dmv], axis=2).reshape(B * MEM_LEN, 2 * MEM_W)
    g_kv = mm_tn_multi(mh_t, [dmkv], B * MEM_LEN, "w_kv_grad", BF16)
    fourth, token = start_exchange([g_mq, g_kv], "early_exchange_d")

    grad_x, dng = in_proj_bwd_rms(gates + fox + dil + mq, (w_in_a, w_in_b), xt, norm_g, dx2, 256, token)
    dmh = mm_nt(dmkv, w_kv, B * MEM_LEN, D, "mem_kv_bwd")
    _, dmng = rms_bwd(memt, mem_norm_g, dmh, None, B * MEM_LEN, "rms_mem_bwd")

    small = jnp.concatenate([dng[0:1], dmng[0:1], small_out[0:1], _pad_row(db_part[0:1], D), small_out[1:2],
                             jnp.zeros((3, D), F32)], axis=0)
    early = [(first, dqkv_fox), (second, dqkv_dil), (third, dmq), (fourth, grad_x)]
    return grad_x.reshape(B, S, D), early, small


def kernel(x, mem, norm_g, w_in, b_forget, mem_norm_g, w_mem_kv, w_out, final_norm_g, loss_target, m_norm_g, m_w_in, m_b_forget, m_mem_norm_g, m_w_mem_kv, m_w_out, m_final_norm_g, v_norm_g, v_w_in, v_b_forget, v_mem_norm_g, v_w_mem_kv, v_w_out, v_final_norm_g):
    D = D_MODEL
    shard_a, shard_b = _split_cols(_pack_cols(w_in).astype(BF16).reshape(w_in.shape[1], PW))
    (w_in_a,) = weight_gather([shard_a])
    gather, gather_token = early_exchange_start(
        [shard_b, w_mem_kv[0].astype(BF16), w_out[0].astype(BF16)], "late_gather", gather=True, after=w_in_a,
        relations=_SIBLING_AND_SAME_CORES)

    def late_weights(after):
        _, gathered = early_exchange_wait(gather, after, "late_gather_wait")
        return pass_on_to_sibling(gathered, gather["rows"], "late_gather_pass")

    grad_x, early, small = local_grads(
        x, mem, norm_g, b_forget, mem_norm_g, final_norm_g, loss_target, w_in_a, gather_token, late_weights,
        early_exchange_start)

    (first, after_first), (second, after_second), (third, after_third), (fourth, after_fourth) = early
    (src_gates, src_out), (land_gates, land_out) = early_exchange_wait(first, after_first, "early_wait_a")
    (src_fox,), (land_fox,) = early_exchange_wait(second, after_second, "early_wait_b")
    (src_dil,), (land_dil,) = early_exchange_wait(third, after_third, "early_wait_c")
    (src_mq, src_kv), (land_mq, land_kv) = early_exchange_wait(fourth, after_fourth, "early_wait_d")
    gates = slot_sum8(src_gates, land_gates, 128, "sum_w_in_gates")
    gw_out = slot_sum8(src_out, land_out, 256, "sum_w_out")
    fox = slot_sum8(src_fox, land_fox, 128, "sum_w_in_fox")
    dil = slot_sum8(src_dil, land_dil, 128, "sum_w_in_dil")
    mq = slot_sum8(src_mq, land_mq, 128, "sum_w_in_mq")
    gw_kv = slot_sum8(src_kv, land_kv, 128, "sum_w_kv")

    tot = small_all_reduce(small)
    gw_in = _unpack_cols(jnp.concatenate(
        [fox[:, :3 * FOX_W], gates[:, :FOX_W], dil, gates[:, FOX_W:FOX_W + DIL_W], mq,
         gates[:, FOX_W + DIL_W:], fox[:, 3 * FOX_W:]], axis=1)[None])

    loss = tot[4, 0]
    g_norm, g_mem_norm, g_final, g_b = tot[0:1], tot[1:2], tot[2], tot[3:4, :FOX_HEADS]

    def rows8(*rows):
        rows = [r.reshape(1, -1) for r in rows]
        rows = [_pad_row(r, D) for r in rows]
        return jnp.concatenate(rows + [jnp.zeros((8 - len(rows), D), F32)], axis=0)

    sw = rows8(norm_g, mem_norm_g, final_norm_g, b_forget)
    sm = rows8(m_norm_g, m_mem_norm_g, m_final_norm_g, m_b_forget)
    sv = rows8(v_norm_g, v_mem_norm_g, v_final_norm_g, v_b_forget)
    d_s, m_s, v_s = adamw(sw, tot, sm, sv, 8, "adamw_small")
    d_in, m_in, v_in = adamw(w_in, gw_in, m_w_in, v_w_in, 32, "adamw_w_in")
    d_kv, m_kv, v_kv = adamw(w_mem_kv[0], gw_kv, m_w_mem_kv[0], v_w_mem_kv[0], 128, "adamw_w_kv")
    d_out, m_out, v_out = adamw(w_out[0], gw_out, m_w_out[0], v_w_out[0], 256, "adamw_w_out")

    def small_outs(t):
        return t[0:1], t[3:4, :FOX_HEADS], t[1:2], t[2]

    grads = (g_norm, gw_in, g_b, g_mem_norm, gw_kv[None], gw_out[None], g_final)
    outs = []
    for t, big in ((d_s, (d_in, d_kv, d_out)), (m_s, (m_in, m_kv, m_out)), (v_s, (v_in, v_kv, v_out))):
        n, b, mn, f = small_outs(t)
        outs += [n, big[0], b, mn, big[1][None], big[2][None], f]
    return (loss, grad_x, *grads, *outs)
```

```python
import math

import numpy as np
import jax
import jax.numpy as jnp
from jax import lax
from jax.experimental import pallas as pl
from jax.experimental.pallas import tpu as pltpu

F32 = jnp.float32
BF16 = jnp.bfloat16

D_MODEL = 1024
HEAD_DIM = 64
FOX_HEADS = 12
DIL_HEADS = 12
MEM_HEADS = 4
MEM_HEAD_DIM = 128
MEM_LEN = 256
FOX_W = FOX_HEADS * HEAD_DIM
DIL_W = DIL_HEADS * HEAD_DIM
MEM_W = MEM_HEADS * MEM_HEAD_DIM
MIX_W = FOX_W + DIL_W + MEM_W
DILATIONS = ((128, 1), (512, 4), (2048, 16))
ROPE_THETA = 500000.0
ROPE_DIM = HEAD_DIM // 4
RMS_EPS = 1e-6
NEG_INF = -1e30
IN_W = 4 * FOX_W + FOX_HEADS + 4 * DIL_W + 2 * MEM_W

ADAM_LR = 0.001
ADAM_B1 = 0.9
ADAM_B2 = 0.999
ADAM_EPS = 1e-08
ADAM_WD = 0.01
ADAM_STEP = 10

N_DEV = 8
LANES = 128
PAIR_W = 3 * LANES
TQ = 256
TK = 256

O_FQ, O_FK, O_FV, O_FG = 0, FOX_W, 2 * FOX_W, 3 * FOX_W
O_FLOG = 4 * FOX_W
O_DQ = O_FLOG + FOX_HEADS
O_DK, O_DV, O_DG = O_DQ + DIL_W, O_DQ + 2 * DIL_W, O_DQ + 3 * DIL_W
O_MQ = O_DQ + 4 * DIL_W
O_MG = O_MQ + MEM_W
P_FOX = 0
P_FG = P_FOX + 3 * FOX_W
P_DIL = P_FG + FOX_W
P_DG = P_DIL + 3 * DIL_W
P_MQ = P_DG + DIL_W
P_MG = P_MQ + MEM_W
P_FLOG = P_MG + MEM_W
PW = P_FLOG + LANES
A_FOX = 0
A_FLOG = A_FOX + 3 * FOX_W
PA = A_FLOG + LANES
B_FG = 0
B_DG = B_FG + FOX_W
B_DIL = B_DG + DIL_W
B_MQ = B_DIL + 3 * DIL_W
B_MG = -(-(B_MQ + MEM_W) // MEM_W) * MEM_W
PB = B_MG + MEM_W

VMEM_LIMIT = 56 * 1024 * 1024


def _pack_pieces():
    pieces = []
    for base in (O_FQ, O_DQ):
        seg = []
        for hp in range(FOX_HEADS // 2):
            for part in range(3):
                seg.append((base + part * FOX_W + hp * LANES, LANES))
        pieces.append(seg)
    fox, dil = pieces
    return fox + [(O_FG, FOX_W)] + dil + [(O_DG, DIL_W), (O_MQ, MEM_W), (O_MG, MEM_W), (O_FLOG, FOX_HEADS)]


def _pack_cols(w):
    parts = [w[..., s:s + n] for s, n in _pack_pieces()]
    parts.append(jnp.zeros(w.shape[:-1] + (LANES - FOX_HEADS,), w.dtype))
    return jnp.concatenate(parts, axis=-1)


def _split_cols(wp):
    def cut(start, width):
        return wp[..., start:start + width]

    group_a = jnp.concatenate([cut(P_FOX, 3 * FOX_W), cut(P_FLOG, LANES)], axis=-1)
    pad = jnp.zeros(wp.shape[:-1] + (B_MG - B_MQ - MEM_W,), wp.dtype)
    group_b = jnp.concatenate([cut(P_FG, FOX_W), cut(P_DG, DIL_W), cut(P_DIL, 3 * DIL_W), cut(P_MQ, MEM_W), pad,
                               cut(P_MG, MEM_W)], axis=-1)
    return group_a, group_b


def _unpack_cols(g):
    runs = []
    pos = 0
    for s, n in _pack_pieces():
        runs.append((s, n, pos))
        pos += n
    runs.sort()
    return jnp.concatenate([g[..., p:p + n] for s, n, p in runs], axis=-1)


def _params(sem=None, **kw):
    return pltpu.CompilerParams(dimension_semantics=sem, vmem_limit_bytes=VMEM_LIMIT, **kw)


def _mesh_pos():
    return lax.axis_index("x"), lax.axis_index("y"), lax.axis_index("c")


def _flip(v, d):
    return 1 - v if d else v


_RELATIONS = [(dx, dy, dc) for dx in (0, 1) for dy in (0, 1) for dc in (0, 1)][1:]
_SIBLING_AND_SAME_CORES = [(0, 0, 1), (1, 0, 0), (0, 1, 0), (1, 1, 0)]


def weight_gather(shards):
    n_arr = len(shards)
    rows = [s.shape[0] for s in shards]

    def body(*refs):
        in_refs = refs[:n_arr]
        out_refs = refs[n_arr:2 * n_arr]
        send_sems, recv_sems, local_sems = refs[2 * n_arr:]
        x, y, c = _mesh_pos()
        me, sibling = (x, y, c), (x, y, 1 - c)
        x_nbr, y_nbr, diag = (1 - x, y, c), (x, 1 - y, c), (1 - x, 1 - y, c)
        north = c == 1
        relay_from = (jnp.where(north, 1 - x, x), jnp.where(north, y, 1 - y), c)
        relay_to = (jnp.where(north, x, 1 - x), jnp.where(north, 1 - y, y), c)
        k_from = jnp.where(north, 1, 2)
        k_to = 3 - k_from

        def block(a, pos):
            px, py, pc = pos
            return out_refs[a].at[pl.ds((4 * px + 2 * py + pc) * rows[a], rows[a]), :]

        def copy(a, k, blk, to, src=None):
            return pltpu.make_async_remote_copy(
                src_ref=block(a, blk) if src is None else src, dst_ref=block(a, blk),
                send_sem=send_sems.at[a, k], recv_sem=recv_sems.at[a, k],
                device_id=to, device_id_type=pl.DeviceIdType.MESH)

        started = []
        mine = []
        for a in range(n_arr):
            cp = pltpu.make_async_copy(in_refs[a], block(a, me), local_sems.at[a])
            cp.start()
            mine.append(cp)
            first = [copy(a, 0, me, sibling, src=in_refs[a]), copy(a, 1, me, x_nbr, src=in_refs[a]),
                     copy(a, 2, me, y_nbr, src=in_refs[a])]
            for cp in first:
                cp.start()
            started += first
        for a in range(n_arr):
            copy(a, k_from, relay_from, me).wait_recv()
            second_hop = copy(a, 3, relay_from, relay_to)
            second_hop.start()
            passed = copy(a, 3 + k_from, relay_from, sibling)
            passed.start()
            started += [second_hop, passed]
        for a in range(n_arr):
            copy(a, k_to, relay_to, me).wait_recv()
            passed = copy(a, 3 + k_to, relay_to, sibling)
            passed.start()
            started.append(passed)
        for a in range(n_arr):
            copy(a, 3, diag, me).wait_recv()
            passed = copy(a, 6, diag, sibling)
            passed.start()
            started.append(passed)
        for a in range(n_arr):
            copy(a, 0, sibling, me).wait_recv()
            for k, chip in ((4, x_nbr), (5, y_nbr), (6, diag)):
                copy(a, k, (chip[0], chip[1], 1 - c), me).wait_recv()
        for cp in started:
            cp.wait_send()
        for cp in mine:
            cp.wait()

    any_spec = pl.BlockSpec(memory_space=pl.ANY)
    return pl.pallas_call(
        body, name="weight_gather",
        out_shape=[jax.ShapeDtypeStruct((N_DEV * s.shape[0], s.shape[1]), s.dtype) for s in shards],
        in_specs=[any_spec] * n_arr, out_specs=[any_spec] * n_arr,
        scratch_shapes=[pltpu.SemaphoreType.DMA((n_arr, 7)), pltpu.SemaphoreType.DMA((n_arr, 7)),
                        pltpu.SemaphoreType.DMA((n_arr,))],
    )(*shards)


_OTHER_CHIPS = [(1, 0), (0, 1), (1, 1)]


def small_all_reduce(small):
    vmem_spec = pl.BlockSpec(memory_space=pltpu.VMEM)

    def body(small_ref, tot_ref, land, send_sems, recv_sems):
        x, y, c = _mesh_pos()
        me = 4 * x + 2 * y + c
        land[me] = small_ref[...]
        sends, recvs = [], []
        for j, (dx, dy, dc) in enumerate(_RELATIONS):
            px, py, pc = _flip(x, dx), _flip(y, dy), _flip(c, dc)
            common = dict(send_sem=send_sems.at[j], recv_sem=recv_sems.at[j],
                          device_id=(px, py, pc), device_id_type=pl.DeviceIdType.MESH)
            sends.append(pltpu.make_async_remote_copy(src_ref=small_ref, dst_ref=land.at[me], **common))
            recvs.append(pltpu.make_async_remote_copy(src_ref=small_ref, dst_ref=land.at[4 * px + 2 * py + pc], **common))
        for cp in sends:
            cp.start()
        for cp in recvs:
            cp.wait_recv()
        for cp in sends:
            cp.wait_send()
        tot = land[0]
        for d in range(1, N_DEV):
            tot = tot + land[d]
        tot_ref[...] = tot

    return pl.pallas_call(
        body, name="small_sum", out_shape=jax.ShapeDtypeStruct(small.shape, small.dtype),
        in_specs=[vmem_spec], out_specs=vmem_spec,
        scratch_shapes=[pltpu.VMEM((N_DEV,) + small.shape, small.dtype),
                        pltpu.SemaphoreType.DMA((len(_RELATIONS),)), pltpu.SemaphoreType.DMA((len(_RELATIONS),))],
    )(small)


_HBM = pl.BlockSpec(memory_space=pltpu.HBM)
_SEM = pl.BlockSpec(memory_space=pltpu.SEMAPHORE)
_EFFECT = pltpu.SideEffectType.DATAFLOW_SIDE_EFFECTING


def _early_copies(src_refs, land_refs, send_sems, recv_sems, rows, gather, relations):
    x, y, c = _mesh_pos()
    me = 4 * x + 2 * y + c
    copies = []
    for a in range(len(src_refs)):
        for dx, dy, dc in relations:
            px, py, pc = _flip(x, dx), _flip(y, dy), _flip(c, dc)
            peer = 4 * px + 2 * py + pc
            copies.append(pltpu.make_async_remote_copy(
                src_ref=src_refs[a] if gather else src_refs[a].at[pl.ds(peer * rows[a], rows[a]), :],
                dst_ref=land_refs[a].at[pl.ds(me * rows[a], rows[a]), :],
                send_sem=send_sems[a], recv_sem=recv_sems[a],
                device_id=(px, py, pc), device_id_type=pl.DeviceIdType.MESH))
    return copies


def own_slots(shards, name):
    n = len(shards)
    x, y, c = _mesh_pos()
    me = (4 * x + 2 * y + c).astype(jnp.int32).reshape(1)
    empties = [lax.empty((N_DEV * s.shape[0], s.shape[1]), s.dtype) for s in shards]

    def body(me_ref, *refs):
        for a in range(n):
            refs[2 * n + a][...] = refs[a][...]

    return pl.pallas_call(
        body, name=name,
        grid_spec=pltpu.PrefetchScalarGridSpec(
            num_scalar_prefetch=1, grid=(1,),
            in_specs=[pl.BlockSpec(s.shape, lambda i, w: (0, 0)) for s in shards]
            + [pl.BlockSpec(memory_space=pl.ANY)] * n,
            out_specs=[pl.BlockSpec(s.shape, lambda i, w: (w[0], 0)) for s in shards]),
        out_shape=[jax.ShapeDtypeStruct(e.shape, e.dtype) for e in empties],
        input_output_aliases={1 + n + a: a for a in range(n)},
        compiler_params=_params(("arbitrary",)),
    )(me, *shards, *empties)


def early_exchange_start(srcs, name, gather=False, after=None, relations=_RELATIONS):
    n = len(srcs)
    if gather:
        rows = [s.shape[0] for s in srcs]
        lands = list(own_slots(srcs, name + "_place"))
    else:
        rows = [s.shape[0] // N_DEV for s in srcs]
        lands = [lax.empty(s.shape, s.dtype) for s in srcs]

    extra = [] if after is None else [after]

    def body(*refs):
        src_refs, land_refs = refs[:n], refs[n:2 * n]
        first_sem = 2 * n + len(extra)
        send_sems, recv_sems = refs[first_sem:first_sem + n], refs[first_sem + n:first_sem + 2 * n]
        token = refs[-1]
        for cp in _early_copies(src_refs, land_refs, send_sems, recv_sems, rows, gather, relations):
            cp.start()
        token[...] = jnp.zeros_like(token)

    hbm = lambda a: pltpu.HBM(a.shape, a.dtype)
    outs = pl.pallas_call(
        body, name=name,
        out_shape=[pltpu.SemaphoreType.DMA(())] * (2 * n)
        + [hbm(a) for a in srcs] + [hbm(a) for a in lands] + [jax.ShapeDtypeStruct((8, LANES), F32)],
        in_specs=[_HBM] * (2 * n) + [pl.BlockSpec(memory_space=pl.ANY)] * len(extra),
        out_specs=[_SEM] * (2 * n) + [_HBM] * (2 * n) + [pl.BlockSpec(memory_space=pltpu.VMEM)],
        input_output_aliases={i: 2 * n + i for i in range(2 * n)},
        compiler_params=pltpu.CompilerParams(has_side_effects=_EFFECT),
    )(*[pltpu.with_memory_space_constraint(a, pltpu.HBM) for a in list(srcs) + lands], *extra)
    handle = dict(sems=outs[:2 * n], srcs=outs[2 * n:3 * n], lands=outs[3 * n:4 * n], rows=rows,
                  copies=len(relations))
    return handle, outs[-1]


def early_exchange_wait(handle, after, name):
    n = len(handle["srcs"])
    rows = handle["rows"]

    def body(*refs):
        src_refs, land_refs = refs[:n], refs[n:2 * n]
        send_sems, recv_sems = refs[2 * n:3 * n], refs[3 * n:4 * n]
        x, y, c = _mesh_pos()
        for a in range(n):
            span = pl.ds(0, handle["copies"] * rows[a])
            all_copies = pltpu.make_async_remote_copy(
                src_ref=land_refs[a].at[span, :], dst_ref=land_refs[a].at[span, :],
                send_sem=send_sems[a], recv_sem=recv_sems[a],
                device_id=(x, y, c), device_id_type=pl.DeviceIdType.MESH)
            all_copies.wait_send()
            all_copies.wait_recv()

    hbm = lambda a: pltpu.HBM(a.shape, a.dtype)
    ins = list(handle["srcs"]) + list(handle["lands"])
    outs = pl.pallas_call(
        body, name=name,
        out_shape=[hbm(a) for a in ins],
        in_specs=[_HBM] * (2 * n) + [_SEM] * (2 * n) + [pl.BlockSpec(memory_space=pl.ANY)],
        out_specs=[_HBM] * (2 * n),
        input_output_aliases={i: i for i in range(2 * n)},
        compiler_params=pltpu.CompilerParams(has_side_effects=_EFFECT),
    )(*ins, *handle["sems"], after)
    return outs[:n], outs[n:]


def pass_on_to_sibling(lands, rows, name):
    n = len(lands)

    def body(*refs):
        land_refs = refs[n:2 * n]
        send_sems, recv_sems = refs[2 * n:]
        x, y, c = _mesh_pos()
        copies = []
        for a in range(n):
            for k, (dx, dy) in enumerate(_OTHER_CHIPS):
                slot = 4 * _flip(x, dx) + 2 * _flip(y, dy) + c
                blk = land_refs[a].at[pl.ds(slot * rows[a], rows[a]), :]
                copies.append(pltpu.make_async_remote_copy(
                    src_ref=blk, dst_ref=blk, send_sem=send_sems.at[a, k], recv_sem=recv_sems.at[a, k],
                    device_id=(x, y, 1 - c), device_id_type=pl.DeviceIdType.MESH))
        for cp in copies:
            cp.start()
        for cp in copies:
            cp.wait_recv()
        for cp in copies:
            cp.wait_send()

    any_spec = pl.BlockSpec(memory_space=pl.ANY)
    return pl.pallas_call(
        body, name=name,
        out_shape=[jax.ShapeDtypeStruct(a.shape, a.dtype) for a in lands],
        in_specs=[any_spec] * n, out_specs=[any_spec] * n,
        input_output_aliases={i: i for i in range(n)},
        scratch_shapes=[pltpu.SemaphoreType.DMA((n, len(_OTHER_CHIPS))), pltpu.SemaphoreType.DMA((n, len(_OTHER_CHIPS)))],
    )(*lands)


def slot_sum8(src, land, tr, name):
    rows, cols = land.shape[0] // N_DEV, land.shape[1]
    x, y, c = _mesh_pos()
    me = (4 * x + 2 * y + c).astype(jnp.int32).reshape(1)

    def body(me_ref, src_ref, land_ref, o_ref):
        acc = None
        for d in range(N_DEV):
            term = jnp.where(d == me_ref[0], src_ref[0], land_ref[d]).astype(F32)
            acc = term if acc is None else acc + term
        o_ref[...] = acc

    return pl.pallas_call(
        body, name=name,
        grid_spec=pltpu.PrefetchScalarGridSpec(
            num_scalar_prefetch=1, grid=(rows // tr,),
            in_specs=[pl.BlockSpec((1, tr, cols), lambda i, w: (w[0], i, 0)),
                      pl.BlockSpec((N_DEV, tr, cols), lambda i, w: (0, i, 0))],
            out_specs=pl.BlockSpec((tr, cols), lambda i, w: (i, 0))),
        out_shape=jax.ShapeDtypeStruct((rows, cols), F32),
        compiler_params=_params(("arbitrary",)),
    )(me, src.reshape(N_DEV, rows, cols), land.reshape(N_DEV, rows, cols))


def mm_tn_multi(a_t, bs, tt, name, out_dtype=F32):
    K, T = a_t.shape
    widths = [b.shape[1] for b in bs]
    steps = T // tt

    def body(a_ref, *rest):
        b_refs, o_ref, acc = rest[:-2], rest[-2], rest[-1]

        @pl.when(pl.program_id(0) == 0)
        def _():
            acc[...] = jnp.zeros(acc.shape, F32)

        av = a_ref[...]
        col = 0
        for b_ref, w in zip(b_refs, widths):
            acc[:, col:col + w] += jnp.dot(av, b_ref[...], preferred_element_type=F32)
            col += w

        @pl.when(pl.program_id(0) == steps - 1)
        def _():
            o_ref[...] = acc[...].astype(out_dtype)

    return pl.pallas_call(
        body, name=name, grid=(steps,),
        in_specs=[pl.BlockSpec((K, tt), lambda t: (0, t))] + [pl.BlockSpec((tt, w), lambda t: (t, 0)) for w in widths],
        out_specs=pl.BlockSpec((K, sum(widths)), lambda t: (0, 0)),
        out_shape=jax.ShapeDtypeStruct((K, sum(widths)), out_dtype),
        scratch_shapes=[pltpu.VMEM((K, sum(widths)), F32)],
        compiler_params=_params(("arbitrary",)),
    )(a_t, *bs)


def rms_fwd(x, g, tm, name, with_transpose=False):
    M, K = x.shape

    def body(x_ref, g_ref, o_ref, *t_ref):
        xv = x_ref[...]
        r = lax.rsqrt(jnp.mean(xv * xv, axis=-1, keepdims=True) + RMS_EPS)
        h = ((xv * r) * g_ref[...]).astype(BF16)
        o_ref[...] = h
        if with_transpose:
            t_ref[0][...] = h.T

    out_specs = [pl.BlockSpec((tm, K), lambda i: (i, 0))]
    out_shape = [jax.ShapeDtypeStruct((M, K), BF16)]
    if with_transpose:
        out_specs.append(pl.BlockSpec((K, tm), lambda i: (0, i)))
        out_shape.append(jax.ShapeDtypeStruct((K, M), BF16))
    outs = pl.pallas_call(
        body, name=name, grid=(M // tm,),
        in_specs=[pl.BlockSpec((tm, K), lambda i: (i, 0)), pl.BlockSpec((1, K), lambda i: (0, 0))],
        out_specs=out_specs, out_shape=out_shape,
        compiler_params=_params(("arbitrary",)),
    )(x, g)
    return outs if with_transpose else outs[0]


def rms_bwd(x, g, dh, dres, tm, name):
    M, K = x.shape
    has_res = dres is not None

    def body(*refs):
        if has_res:
            x_ref, g_ref, dh_ref, dres_ref, dx_ref, dg_ref = refs
        else:
            x_ref, g_ref, dh_ref, dx_ref, dg_ref = refs
        xv = x_ref[...]
        r = lax.rsqrt(jnp.mean(xv * xv, axis=-1, keepdims=True) + RMS_EPS)
        xn = xv * r
        dhv = dh_ref[...]
        dxn = dhv * g_ref[...]
        dx = r * (dxn - xn * jnp.mean(dxn * xn, axis=-1, keepdims=True))
        if has_res:
            dx = dx + dres_ref[...]
        dx_ref[...] = dx
        part = jnp.sum(dhv * xn, axis=0, keepdims=True)
        row = lax.broadcasted_iota(jnp.int32, (8, K), 0)
        upd = jnp.where(row == 0, part, 0.0)

        @pl.when(pl.program_id(0) == 0)
        def _():
            dg_ref[...] = upd

        @pl.when(pl.program_id(0) != 0)
        def _():
            dg_ref[...] += upd

    row_spec = pl.BlockSpec((tm, K), lambda i: (i, 0))
    ins = [x, g, dh] + ([dres] if has_res else [])
    in_specs = [row_spec, pl.BlockSpec((1, K), lambda i: (0, 0)), row_spec] + ([row_spec] if has_res else [])
    return pl.pallas_call(
        body, name=name, grid=(M // tm,),
        in_specs=in_specs,
        out_specs=[row_spec, pl.BlockSpec((8, K), lambda i: (0, 0))],
        out_shape=[jax.ShapeDtypeStruct((M, K), F32), jax.ShapeDtypeStruct((8, K), F32)],
        compiler_params=_params(("arbitrary",)),
    )(*ins)


def mm_nn(a, b, tm, tn, name, token=None):
    M, K = a.shape
    N = b.shape[1]
    extra = [] if token is None else [token]

    def body(a_ref, b_ref, *rest):
        rest[-1][...] = jnp.dot(a_ref[...], b_ref[...], preferred_element_type=F32)

    return pl.pallas_call(
        body, name=name, grid=(N // tn, M // tm),
        in_specs=[pl.BlockSpec((tm, K), lambda j, i: (i, 0)), pl.BlockSpec((K, tn), lambda j, i: (0, j))]
        + [pl.BlockSpec(t.shape, lambda j, i: (0, 0)) for t in extra],
        out_specs=pl.BlockSpec((tm, tn), lambda j, i: (i, j)),
        out_shape=jax.ShapeDtypeStruct((M, N), F32),
        compiler_params=_params(("arbitrary", "arbitrary")),
    )(a, b, *extra)


def mm_nt(a, b, tm, tk, name):
    M, K = a.shape
    N = b.shape[0]

    def body(a_ref, b_ref, o_ref):
        part = lax.dot_general(a_ref[...], b_ref[...], (((1,), (1,)), ((), ())), preferred_element_type=F32)

        @pl.when(pl.program_id(1) == 0)
        def _():
            o_ref[...] = part

        @pl.when(pl.program_id(1) != 0)
        def _():
            o_ref[...] += part

    return pl.pallas_call(
        body, name=name, grid=(M // tm, K // tk),
        in_specs=[pl.BlockSpec((tm, tk), lambda i, k: (i, k)), pl.BlockSpec((N, tk), lambda i, k: (0, k))],
        out_specs=pl.BlockSpec((tm, N), lambda i, k: (i, 0)),
        out_shape=jax.ShapeDtypeStruct((M, N), F32),
        compiler_params=_params(("arbitrary", "arbitrary")),
    )(a, b)


def in_proj_bwd_rms(pieces, ws, x, g, dres, tm, token):
    M, N = x.shape

    def body(*refs):
        n = len(pieces)
        p_refs, w_refs = refs[:n], refs[n:n + len(ws)]
        x_ref, g_ref, dres_ref, _, dx_ref, dg_ref = refs[n + len(ws):]
        dh = None
        for p_ref, (arr, group, col) in zip(p_refs, pieces):
            part = lax.dot_general(p_ref[...], w_refs[group][:, col:col + arr.shape[1]], (((1,), (1,)), ((), ())),
                                   preferred_element_type=F32)
            dh = part if dh is None else dh + part
        xv = x_ref[...]
        r = lax.rsqrt(jnp.mean(xv * xv, axis=-1, keepdims=True) + RMS_EPS)
        xn = xv * r
        dxn = dh * g_ref[...]
        dx_ref[...] = r * (dxn - xn * jnp.mean(dxn * xn, axis=-1, keepdims=True)) + dres_ref[...]
        row = lax.broadcasted_iota(jnp.int32, (8, N), 0)
        upd = jnp.where(row == 0, jnp.sum(dh * xn, axis=0, keepdims=True), 0.0)

        @pl.when(pl.program_id(0) == 0)
        def _():
            dg_ref[...] = upd

        @pl.when(pl.program_id(0) != 0)
        def _():
            dg_ref[...] += upd

    row_spec = pl.BlockSpec((tm, N), lambda i: (i, 0))
    return pl.pallas_call(
        body, name="in_proj_bwd", grid=(M // tm,),
        in_specs=[pl.BlockSpec((tm, arr.shape[1]), lambda i: (i, 0)) for arr, _, _ in pieces]
        + [pl.BlockSpec(w.shape, lambda i: (0, 0)) for w in ws]
        + [row_spec, pl.BlockSpec((1, N), lambda i: (0, 0)), row_spec, pl.BlockSpec(token.shape, lambda i: (0, 0))],
        out_specs=[row_spec, pl.BlockSpec((8, N), lambda i: (0, 0))],
        out_shape=[jax.ShapeDtypeStruct((M, N), F32), jax.ShapeDtypeStruct((8, N), F32)],
        compiler_params=_params(("arbitrary",)),
    )(*[arr for arr, _, _ in pieces], *ws, x, g, dres, token)


def _log_sigmoid(z):
    return jnp.minimum(z, 0.0) - jnp.log(1.0 + jnp.exp(-jnp.abs(z)))


def _tri(n, lower):
    r = lax.broadcasted_iota(jnp.int32, (n, n), 0)
    c = lax.broadcasted_iota(jnp.int32, (n, n), 1)
    return jnp.where((r >= c) if lower else (r <= c), 1.0, 0.0).astype(F32)


def fox_gate(proj3, b_pad):
    B, S, _ = proj3.shape
    nblk = S // TK

    def body(f_ref, b_ref, o_ref):
        tri = _tri(TK, True)
        carry = jnp.zeros((1, LANES), F32)
        for n in range(nblk):
            z = f_ref[0, n * TK:(n + 1) * TK, :] + b_ref[...]
            logf = _log_sigmoid(z)
            cs = jnp.dot(tri, logf, preferred_element_type=F32, precision=lax.Precision.HIGHEST) + carry
            carry = cs[TK - 1:TK, :]
            o_ref[0, n * TK:(n + 1) * TK, :] = -cs

    return pl.pallas_call(
        body, name="fox_gate", grid=(B,),
        in_specs=[pl.BlockSpec((1, S, LANES), lambda b: (b, 0, A_FLOG // LANES)),
                  pl.BlockSpec((1, LANES), lambda b: (0, 0))],
        out_specs=pl.BlockSpec((1, S, LANES), lambda b: (b, 0, 0)),
        out_shape=jax.ShapeDtypeStruct((B, S, LANES), F32),
        compiler_params=_params(("arbitrary",)),
    )(proj3, b_pad)


def fox_gate_bwd(drow, dneg, proj3, b_pad):
    B, S, _ = proj3.shape
    nblk = S // TK

    def body(d_ref, r_ref, f_ref, b_ref, o_ref, db_ref):
        tri = _tri(TK, False)
        lane = lax.broadcasted_iota(jnp.int32, (TK, LANES), 1)
        carry = jnp.zeros((1, LANES), F32)
        dbsum = jnp.zeros((1, LANES), F32)
        for n in reversed(range(nblk)):
            dk_side = None
            for hp in range(FOX_HEADS // 2):
                two = jnp.where(lane < 2, r_ref[0, n * TK:(n + 1) * TK, hp * LANES:(hp + 1) * LANES], 0.0)
                two = pltpu.roll(two, 2 * hp, 1) if hp else two
                dk_side = two if dk_side is None else dk_side + two
            dc = jnp.where(lane < FOX_HEADS, d_ref[0, :, n * TK:(n + 1) * TK].T - dk_side, 0.0)
            rs = jnp.dot(tri, dc, preferred_element_type=F32, precision=lax.Precision.HIGHEST) + carry
            carry = rs[0:1, :]
            z = f_ref[0, n * TK:(n + 1) * TK, :] + b_ref[...]
            dz = rs * (1.0 / (1.0 + jnp.exp(z)))
            o_ref[0, n * TK:(n + 1) * TK, :] = dz.astype(BF16)
            dbsum = dbsum + jnp.sum(dz, axis=0, keepdims=True)
        row = lax.broadcasted_iota(jnp.int32, (8, LANES), 0)
        upd = jnp.where(row == 0, dbsum, 0.0)

        @pl.when(pl.program_id(0) == 0)
        def _():
            db_ref[...] = upd

        @pl.when(pl.program_id(0) != 0)
        def _():
            db_ref[...] += upd

    return pl.pallas_call(
        body, name="fox_gate_bwd", grid=(B,),
        in_specs=[pl.BlockSpec((1, LANES, S), lambda b: (b, 0, 0)),
                  pl.BlockSpec((1, S, FOX_W), lambda b: (b, 0, 0)),
                  pl.BlockSpec((1, S, LANES), lambda b: (b, 0, A_FLOG // LANES)),
                  pl.BlockSpec((1, LANES), lambda b: (0, 0))],
        out_specs=[pl.BlockSpec((1, S, LANES), lambda b: (b, 0, 0)), pl.BlockSpec((8, LANES), lambda b: (0, 0))],
        out_shape=[jax.ShapeDtypeStruct((B, S, LANES), BF16), jax.ShapeDtypeStruct((8, LANES), F32)],
        compiler_params=_params(("arbitrary",)),
    )(drow, dneg, proj3, b_pad)


def _rope_tables(S):
    half = ROPE_DIM // 2
    f32 = np.float32
    pos = np.arange(S, dtype=f32)
    inv_freq = f32(1.0) / np.power(f32(ROPE_THETA), np.arange(0, ROPE_DIM, 2, dtype=f32) / f32(ROPE_DIM)).astype(f32)
    ang = (pos[:, None] * inv_freq[None, :]).astype(f32).astype(np.float64)
    cos, sin = np.cos(ang).astype(f32), np.sin(ang).astype(f32)
    one = np.ones((S, HEAD_DIM - ROPE_DIM), f32)
    zero = np.zeros((S, HEAD_DIM - ROPE_DIM), f32)
    zh = np.zeros((S, half), f32)
    c = np.concatenate([cos, cos, one], axis=1)
    s1 = np.concatenate([-sin, zh, zero], axis=1)
    s2 = np.concatenate([zh, sin, zero], axis=1)
    return tuple(jnp.asarray(np.concatenate([t, t], axis=1)) for t in (c, s1, s2))


_HALF_ROPE = ROPE_DIM // 2


def _rope(t, c, s1, s2):
    return t * c + pltpu.roll(t, LANES - _HALF_ROPE, 1) * s1 + pltpu.roll(t, _HALF_ROPE, 1) * s2


def _rope_bwd(d, c, s1, s2):
    return d * c + pltpu.roll(d * s1, _HALF_ROPE, 1) + pltpu.roll(d * s2, LANES - _HALF_ROPE, 1)


def _scale_parts(scale):
    m, _ = math.frexp(scale)
    return (scale, None) if m == 0.5 else (None, scale)


def _log_masks(S, kind):
    nd = 1 if kind == "causal" else S // TQ
    a = np.arange(TQ)[:, None]
    b = np.arange(TK)[None, :]
    out = np.zeros((nd, TQ, TK), np.float32)
    for d in range(nd):
        delta = d * TQ + a - b
        if kind == "causal":
            m = (delta >= 0).astype(np.float64)
        else:
            m = sum(((delta >= 0) & (delta % dil == 0) & (delta <= w)).astype(np.float64) for w, dil in DILATIONS)
        out[d] = np.where(m > 0, np.log(np.maximum(m, 1.0)), NEG_INF)
    return jnp.asarray(out)


def _attn_setup(kind):
    pair = kind != "mem"
    e_dim = HEAD_DIM if pair else MEM_HEAD_DIM
    q_fold, s_scale = _scale_parts(1.0 / math.sqrt(e_dim))
    return dict(pair=pair, col0={"fox": A_FOX, "dil": B_DIL, "mem": B_MQ}[kind],
                n_blocks=FOX_HEADS // 2 if pair else MEM_HEADS, q_fold=q_fold, s_scale=s_scale,
                nh=2 if pair else 1)


def _cat(parts, axis):
    return parts[0] if len(parts) == 1 else jnp.concatenate(parts, axis=axis)


def _log_masks_t(S, kind):
    return jnp.swapaxes(_log_masks(S, kind), 1, 2)


def _head_rows(hh, pair):
    row = lax.broadcasted_iota(jnp.int32, (LANES, 1), 0)
    if not pair:
        return row >= 0
    return (row >= HEAD_DIM * hh) & (row < HEAD_DIM * (hh + 1))


def _attn_t_inputs(kind, src, S, negc_cols, mask, rope, kv):
    cfg = _attn_setup(kind)
    col0 = cfg["col0"]
    ins, in_specs = [], []
    if cfg["pair"]:
        ins.append(src)
        in_specs.append(pl.BlockSpec((1, S, PAIR_W), lambda b, h: (b, 0, col0 // PAIR_W + h)))
    else:
        ins += [src, kv, kv]
        in_specs += [pl.BlockSpec((1, S, LANES), lambda b, h: (b, 0, col0 // LANES + h)),
                     pl.BlockSpec((1, MEM_LEN, LANES), lambda b, h: (b, 0, h)),
                     pl.BlockSpec((1, MEM_LEN, LANES), lambda b, h: (b, 0, MEM_HEADS + h))]
    if negc_cols is not None:
        ins.append(negc_cols)
        in_specs.append(pl.BlockSpec((1, S, LANES), lambda b, h: (b, 0, 0)))
    if mask is not None:
        ins.append(mask)
        in_specs.append(pl.BlockSpec(mask.shape, lambda b, h: (0, 0, 0)))
    if rope is not None:
        ins += list(rope)
        in_specs += [pl.BlockSpec((S, LANES), lambda b, h: (0, 0))] * 3
    return ins, in_specs


def _attn_t_prep(cfg, refs, S, Sk, *, qT2s, ks, vs=None, vTs=None, kTs=None, nb=None):
    pair, nh = cfg["pair"], cfg["nh"]
    lane = lax.broadcasted_iota(jnp.int32, (1, LANES), 1)
    rope_refs = refs["rope"]

    def prep_q(n):
        rows = slice(n * TQ, (n + 1) * TQ)
        q = refs["load_q"](rows)
        if rope_refs is not None:
            q = _rope(q, *[t[rows, :] for t in rope_refs])
        if cfg["q_fold"] is not None:
            q = q * cfg["q_fold"]
        qtb = q.astype(BF16).T
        for hh in range(nh):
            qT2s[n, :, hh * TQ:(hh + 1) * TQ] = jnp.where(_head_rows(hh, pair), qtb, jnp.zeros_like(qtb))

    def prep_kv(n):
        rows = slice(n * TK, (n + 1) * TK)
        k, v = refs["load_kv"](rows)
        if rope_refs is not None:
            k = _rope(k, *[t[rows, :] for t in rope_refs])
        kb = k.astype(BF16)
        vb = v.astype(BF16)
        ks[rows, :] = kb
        if vs is not None:
            vs[rows, :] = vb
        if vTs is not None:
            vTs[n] = vb.T
        if kTs is not None:
            kTs[n] = kb.T
        if nb is not None:
            blk = refs["negc"][0, rows, :]
            for hh in range(nh):
                h = 2 * refs["block"] + hh
                col = jnp.sum(jnp.where(lane == h, blk, 0.0), axis=1, keepdims=True)
                nb[hh, rows, :] = jnp.broadcast_to(col, (TK, LANES))

    for n in range(S // TQ):
        prep_q(n)
    for n in range(Sk // TK):
        prep_kv(n)


def _raw_scores_t(cfg, k, qT2):
    sT = jnp.dot(k, qT2, preferred_element_type=F32)
    if cfg["s_scale"] is not None:
        sT = sT * cfg["s_scale"]
    return sT


def _bias_mask_t(cfg, sT, nb, mask_ref, kc, midx):
    nh = cfg["nh"]
    if nb is None and midx is None:
        return sT
    parts = []
    for hh in range(nh):
        t = sT[:, hh * TQ:(hh + 1) * TQ]
        if nb is not None:
            t = t + jnp.concatenate([nb[hh, kc, :]] * (TQ // LANES), axis=1)
        if midx is not None:
            t = t + mask_ref[midx]
        parts.append(t)
    return _cat(parts, 1)


def _tile_pairs(kind, nq, nk):
    if kind == "mem":
        return [(i, j) for i in range(nq) for j in range(nk)], (lambda i, j: None)
    pairs = [(i, j) for i in range(nq) for j in range(i + 1)]
    if kind == "fox":
        return pairs, (lambda i, j: 0 if j == i else None)
    return pairs, (lambda i, j: i - j)


def attn_fwd(kind, src, S, *, negc_cols=None, mask=None, rope=None, kv=None):
    B = src.shape[0]
    cfg = _attn_setup(kind)
    pair, nh = cfg["pair"], cfg["nh"]
    Sk = S if pair else MEM_LEN
    has_bias, has_rope = negc_cols is not None, rope is not None
    R = nh * TQ
    nq, nk = S // TQ, Sk // TK
    pairs, mask_index = _tile_pairs(kind, nq, nk)

    def body(*refs):
        refs = list(refs)
        if pair:
            qkv_ref = refs.pop(0)
            load_q = lambda rows: qkv_ref[0, rows, 0:LANES]
            load_kv = lambda rows: (qkv_ref[0, rows, LANES:2 * LANES], qkv_ref[0, rows, 2 * LANES:3 * LANES])
        else:
            q_ref, k_ref, v_ref = refs.pop(0), refs.pop(0), refs.pop(0)
            load_q = lambda rows: q_ref[0, rows, :]
            load_kv = lambda rows: (k_ref[0, rows, :], v_ref[0, rows, :])
        negc_ref = refs.pop(0) if has_bias else None
        mask_ref = refs.pop(0) if mask is not None else None
        rope_refs = [refs.pop(0) for _ in range(3)] if has_rope else None
        o_ref, lse_ref, qT2s, ks, vTs, s_a, s_b, p_a, p_b = refs[:9]
        nb = refs[9] if has_bias else None
        _attn_t_prep(cfg, dict(load_q=load_q, load_kv=load_kv, rope=rope_refs, negc=negc_ref,
                               block=pl.program_id(1)), S, Sk, qT2s=qT2s, ks=ks, vTs=vTs, nb=nb)

        def cols(j):
            return slice(j * TK, (j + 1) * TK)

        def scores(i, j):
            return _raw_scores_t(cfg, ks[cols(j), :], qT2s[i])

        def finish(i, m, l, accT):
            oT2 = accT / l
            oT = jnp.where(_head_rows(0, True), oT2[:, 0:TQ], oT2[:, TQ:2 * TQ]) if pair else oT2
            o_ref[0, i * TQ:(i + 1) * TQ, :] = oT.T
            lse_ref[0, 0, i:i + 1, :] = m + jnp.log(l)

        s_bufs, p_bufs = (s_a, s_b), (p_a, p_b)
        s_bufs[0][...] = scores(*pairs[0])
        m = l = accT = None
        for t, (i, j) in enumerate(pairs):
            cur, oth = t % 2, 1 - t % 2
            if t > 0:
                i_prev, j_prev = pairs[t - 1]
                pv = jnp.dot(vTs[j_prev], p_bufs[oth][...], preferred_element_type=F32)
                acc_full = pv if accT is None else accT + pv
            if t + 1 < len(pairs):
                s_bufs[oth][...] = scores(*pairs[t + 1])
            first = j == 0
            if first and t > 0:
                finish(i_prev, m, l, acc_full)
            sT = _bias_mask_t(cfg, s_bufs[cur][...], nb, mask_ref, cols(j), mask_index(i, j))
            m_tile = jnp.max(sT, axis=0, keepdims=True)
            m_new = m_tile if first else jnp.maximum(m, m_tile)
            p = jnp.exp(sT - m_new)
            p_bufs[cur][...] = p.astype(BF16)
            if first:
                l, accT = jnp.sum(p, axis=0, keepdims=True), None
            else:
                alpha = jnp.exp(m - m_new)
                l, accT = alpha * l + jnp.sum(p, axis=0, keepdims=True), acc_full * alpha
            m = m_new
        i_last, j_last = pairs[-1]
        pv = jnp.dot(vTs[j_last], p_bufs[(len(pairs) - 1) % 2][...], preferred_element_type=F32)
        finish(i_last, m, l, pv if accT is None else accT + pv)

    ins, in_specs = _attn_t_inputs(kind, src, S, negc_cols, mask, rope, kv)
    W = cfg["n_blocks"] * LANES
    scratch = [pltpu.VMEM((nq, LANES, R), BF16), pltpu.VMEM((Sk, LANES), BF16), pltpu.VMEM((nk, LANES, TK), BF16),
               pltpu.VMEM((TK, R), F32), pltpu.VMEM((TK, R), F32), pltpu.VMEM((TK, R), BF16), pltpu.VMEM((TK, R), BF16)]
    if has_bias:
        scratch.append(pltpu.VMEM((nh, Sk, LANES), F32))
    return pl.pallas_call(
        body, name=kind + "_attn_fwd", grid=(B, cfg["n_blocks"]),
        in_specs=in_specs,
        out_specs=[pl.BlockSpec((1, S, LANES), lambda b, h: (b, 0, h)),
                   pl.BlockSpec((1, 1, nq, R), lambda b, h: (b, h, 0, 0))],
        out_shape=[jax.ShapeDtypeStruct((B, S, W), F32), jax.ShapeDtypeStruct((B, cfg["n_blocks"], nq, R), F32)],
        scratch_shapes=scratch,
        compiler_params=_params(("arbitrary", "arbitrary")),
    )(*ins)


def attn_bwd(kind, src, do, o, lse, S, *, negc_cols=None, mask=None, rope=None, kv=None, token=None):
    B = src.shape[0]
    cfg = _attn_setup(kind)
    pair, nh, s_scale, q_fold = cfg["pair"], cfg["nh"], cfg["s_scale"], cfg["q_fold"]
    Sk = S if pair else MEM_LEN
    has_bias, has_rope = negc_cols is not None, rope is not None
    R = nh * TQ
    nq, nk = S // TQ, Sk // TK
    pairs, mask_index = _tile_pairs(kind, nq, nk)

    def body(*refs):
        refs = list(refs)
        if pair:
            qkv_ref = refs.pop(0)
            load_q = lambda rows: qkv_ref[0, rows, 0:LANES]
            load_kv = lambda rows: (qkv_ref[0, rows, LANES:2 * LANES], qkv_ref[0, rows, 2 * LANES:3 * LANES])
        else:
            q_ref, k_ref, v_ref = refs.pop(0), refs.pop(0), refs.pop(0)
            load_q = lambda rows: q_ref[0, rows, :]
            load_kv = lambda rows: (k_ref[0, rows, :], v_ref[0, rows, :])
        negc_ref = refs.pop(0) if has_bias else None
        mask_ref = refs.pop(0) if mask is not None else None
        rope_refs = [refs.pop(0) for _ in range(3)] if has_rope else None
        do_ref, o_ref, lse_ref = refs.pop(0), refs.pop(0), refs.pop(0)
        if token is not None:
            refs.pop(0)
        if pair:
            dqkv_ref = refs.pop(0)
            dneg_ref = refs.pop(0) if has_bias else None
            drow_ref = refs.pop(0) if has_bias else None
        else:
            dq_ref, dk_ref, dv_ref = refs.pop(0), refs.pop(0), refs.pop(0)
        qT2s, ks, vs, kTs, doT2s, delta_s, dk_acc, dv_acc = refs[:8]
        bufs_a, bufs_b = refs[8:12], refs[12:16]
        nb, dneg_acc = (refs[16], refs[17]) if has_bias else (None, None)
        lane = lax.broadcasted_iota(jnp.int32, (1, LANES), 1)
        _attn_t_prep(cfg, dict(load_q=load_q, load_kv=load_kv, rope=rope_refs, negc=negc_ref,
                               block=pl.program_id(1)), S, Sk,
                     qT2s=qT2s, ks=ks, vs=vs, kTs=kTs, nb=nb)

        def prep_do(n):
            rows = slice(n * TQ, (n + 1) * TQ)
            doT = do_ref[0, rows, :].astype(BF16).astype(F32).T
            prodT = doT * o_ref[0, rows, :].T
            doTb = doT.astype(BF16)
            for hh in range(nh):
                hm = _head_rows(hh, pair)
                doT2s[n, :, hh * TQ:(hh + 1) * TQ] = jnp.where(hm, doTb, jnp.zeros_like(doTb))
                delta_s[n:n + 1, hh * TQ:(hh + 1) * TQ] = jnp.sum(jnp.where(hm, prodT, 0.0), axis=0, keepdims=True)

        for n in range(nq):
            prep_do(n)
        dk_acc[...] = jnp.zeros(dk_acc.shape, F32)
        dv_acc[...] = jnp.zeros(dv_acc.shape, F32)
        if has_bias:
            dneg_acc[...] = jnp.zeros(dneg_acc.shape, F32)

        def cols(j):
            return slice(j * TK, (j + 1) * TK)

        nt_dims = (((1,), (1,)), ((), ()))

        def first_products(i, j, bufs):
            bufs[0][...] = _raw_scores_t(cfg, ks[cols(j), :], qT2s[i])
            bufs[1][...] = jnp.dot(vs[cols(j), :], doT2s[i], preferred_element_type=F32)

        def last_products(i, j, bufs, dqT2):
            dv_acc[j] += lax.dot_general(doT2s[i], bufs[2][...], nt_dims, preferred_element_type=F32)
            dk_acc[j] += lax.dot_general(qT2s[i], bufs[3][...], nt_dims, preferred_element_type=F32)
            dq = jnp.dot(kTs[j], bufs[3][...], preferred_element_type=F32)
            return dq if dqT2 is None else dqT2 + dq

        def finish_q(i, dqT2, drow):
            rows = slice(i * TQ, (i + 1) * TQ)
            dqT = jnp.where(_head_rows(0, True), dqT2[:, 0:TQ], dqT2[:, TQ:2 * TQ]) if pair else dqT2
            dq = dqT.T
            if q_fold is not None:
                dq = dq * q_fold
            if has_rope:
                dq = _rope_bwd(dq, *[t[rows, :] for t in rope_refs])
            if pair:
                dqkv_ref[0, rows, 0:LANES] = dq.astype(BF16)
            else:
                dq_ref[0, rows, :] = dq.astype(BF16)
            if has_bias:
                drow_ref[0, 0, i:i + 1, :] = drow

        bufs = (bufs_a, bufs_b)
        first_products(*pairs[0], bufs[0])
        dqT2 = drow = None
        for t, (i, j) in enumerate(pairs):
            cur, oth = bufs[t % 2], bufs[1 - t % 2]
            first = j == 0
            if first and t > 0:
                i_prev, j_prev = pairs[t - 1]
                finish_q(i_prev, last_products(i_prev, j_prev, oth, dqT2), drow)
                dqT2 = drow = None
            sT = _bias_mask_t(cfg, cur[0][...], nb, mask_ref, cols(j), mask_index(i, j))
            pT = jnp.exp(sT - lse_ref[0, 0, i:i + 1, :])
            dsT = pT * (cur[1][...] - delta_s[i:i + 1, :])
            if has_bias:
                tile_rows = jnp.sum(dsT, axis=0, keepdims=True)
                drow = tile_rows if drow is None else drow + tile_rows
                for hh in range(nh):
                    part = dsT[:, hh * TQ:hh * TQ + LANES]
                    for u in range(1, TQ // LANES):
                        part = part + dsT[:, hh * TQ + u * LANES:hh * TQ + (u + 1) * LANES]
                    dneg_acc[hh, cols(j), :] += part
            if s_scale is not None:
                dsT = dsT * s_scale
            cur[2][...] = pT.astype(BF16)
            cur[3][...] = dsT.astype(BF16)
            if not first:
                dqT2 = last_products(*pairs[t - 1], oth, dqT2)
            if t + 1 < len(pairs):
                first_products(*pairs[t + 1], oth)
        i_last, j_last = pairs[-1]
        finish_q(i_last, last_products(i_last, j_last, bufs[(len(pairs) - 1) % 2], dqT2), drow)

        for n in range(nk):
            rows = slice(n * TK, (n + 1) * TK)
            dk = dk_acc[n].T
            dv = dv_acc[n].T
            if has_rope:
                dk = _rope_bwd(dk, *[t[rows, :] for t in rope_refs])
            if pair:
                dqkv_ref[0, rows, LANES:2 * LANES] = dk.astype(BF16)
                dqkv_ref[0, rows, 2 * LANES:3 * LANES] = dv.astype(BF16)
            else:
                dk_ref[0, rows, :] = dk.astype(BF16)
                dv_ref[0, rows, :] = dv.astype(BF16)
            if has_bias:
                x0 = jnp.sum(dneg_acc[0, rows, :], axis=1, keepdims=True)
                x1 = jnp.sum(dneg_acc[1, rows, :], axis=1, keepdims=True)
                dneg_ref[0, rows, :] = jnp.where(lane == 0, x0, jnp.where(lane == 1, x1, 0.0))

    ins, in_specs = _attn_t_inputs(kind, src, S, negc_cols, mask, rope, kv)
    row_spec = pl.BlockSpec((1, S, LANES), lambda b, h: (b, 0, h))
    vec_spec = pl.BlockSpec((1, 1, nq, R), lambda b, h: (b, h, 0, 0))
    ins += [do, o, lse]
    in_specs += [row_spec, row_spec, vec_spec]
    if token is not None:
        ins.append(token)
        in_specs.append(pl.BlockSpec(token.shape, lambda b, h: (0, 0)))
    W = cfg["n_blocks"] * LANES
    if pair:
        out_specs = [pl.BlockSpec((1, S, PAIR_W), lambda b, h: (b, 0, h))]
        out_shape = [jax.ShapeDtypeStruct((B, S, 3 * W), BF16)]
        if has_bias:
            out_specs += [row_spec, vec_spec]
            out_shape += [jax.ShapeDtypeStruct((B, S, W), F32), jax.ShapeDtypeStruct((B, cfg["n_blocks"], nq, R), F32)]
    else:
        kv_spec = pl.BlockSpec((1, MEM_LEN, LANES), lambda b, h: (b, 0, h))
        out_specs = [row_spec, kv_spec, kv_spec]
        out_shape = [jax.ShapeDtypeStruct((B, S, W), BF16)] + [jax.ShapeDtypeStruct((B, MEM_LEN, W), BF16)] * 2
    scratch = [pltpu.VMEM((nq, LANES, R), BF16), pltpu.VMEM((Sk, LANES), BF16),
               pltpu.VMEM((Sk, LANES), BF16), pltpu.VMEM((nk, LANES, TK), BF16), pltpu.VMEM((nq, LANES, R), BF16),
               pltpu.VMEM((nq, R), F32), pltpu.VMEM((nk, LANES, TK), F32), pltpu.VMEM((nk, LANES, TK), F32)]
    pair_bufs = [pltpu.VMEM((TK, R), F32), pltpu.VMEM((TK, R), F32), pltpu.VMEM((TK, R), BF16), pltpu.VMEM((TK, R), BF16)]
    scratch += pair_bufs + pair_bufs
    if has_bias:
        scratch += [pltpu.VMEM((nh, Sk, LANES), F32), pltpu.VMEM((nh, Sk, LANES), F32)]
    return pl.pallas_call(
        body, name=kind + "_attn_bwd", grid=(B, cfg["n_blocks"]),
        in_specs=in_specs, out_specs=out_specs, out_shape=out_shape, scratch_shapes=scratch,
        compiler_params=_params(("arbitrary", "arbitrary")),
    )(*ins)


def _sigmoid(g):
    return 1.0 / (1.0 + jnp.exp(-g))


def out_step(proj, o_fox, o_dil, o_mem, w_out, x, target, gf, tm):
    T = x.shape[0]

    def body(fg_ref, dg_ref, mg_ref, of_ref, od_ref, om_ref, w_ref, x_ref, t_ref, gf_ref,
             dx_ref, dof_ref, dod_ref, dom_ref, dfg_ref, ddg_ref, dmg_ref, gw_ref, sm_ref, gw_acc):
        branches = []
        for g_ref, o_ref in ((fg_ref, of_ref), (dg_ref, od_ref), (mg_ref, om_ref)):
            g = g_ref[...]
            sg = _sigmoid(g)
            o = o_ref[...]
            branches.append((g, sg, o))
        ymix = jnp.concatenate([(o * (g * sg)).astype(BF16) for g, sg, o in branches], axis=1)
        x2 = x_ref[...] + jnp.dot(ymix, w_ref[...], preferred_element_type=F32)
        r = lax.rsqrt(jnp.mean(x2 * x2, axis=-1, keepdims=True) + RMS_EPS)
        yn = x2 * r
        err = yn * gf_ref[...] - t_ref[...]
        loss = 0.5 * jnp.sum(jnp.sum(err * err, axis=-1, keepdims=True) / D_MODEL, axis=0, keepdims=True)
        dyf = err / D_MODEL
        dgf = jnp.sum(dyf * yn, axis=0, keepdims=True)
        dyn = dyf * gf_ref[...]
        dx2 = r * (dyn - yn * jnp.mean(dyn * yn, axis=-1, keepdims=True))
        dx_ref[...] = dx2
        dxb = dx2.astype(BF16)
        dmix = lax.dot_general(dxb, w_ref[...], (((1,), (1,)), ((), ())), preferred_element_type=F32)
        col = 0
        for (g, sg, o), do_ref, dgate_ref in zip(branches, (dof_ref, dod_ref, dom_ref), (dfg_ref, ddg_ref, dmg_ref)):
            d = dmix[:, col:col + g.shape[1]]
            col += g.shape[1]
            do_ref[...] = (d * (g * sg)).astype(BF16)
            dgate_ref[...] = (d * o * (sg * (1.0 + g * (1.0 - sg)))).astype(BF16)
        row = lax.broadcasted_iota(jnp.int32, (8, D_MODEL), 0)
        upd = jnp.where(row == 0, dgf, jnp.where(row == 1, loss, 0.0))

        @pl.when(pl.program_id(0) == 0)
        def _():
            sm_ref[...] = jnp.zeros(sm_ref.shape, F32)
            gw_acc[...] = jnp.zeros(gw_acc.shape, F32)

        sm_ref[...] += upd
        gw_acc[...] += lax.dot_general(ymix, dxb, (((0,), (0,)), ((), ())), preferred_element_type=F32)

        @pl.when(pl.program_id(0) == T // tm - 1)
        def _():
            gw_ref[...] = gw_acc[...].astype(BF16)

    def rows(w, col=0):
        return pl.BlockSpec((tm, w), lambda i: (i, col))

    return pl.pallas_call(
        body, name="out_step", grid=(T // tm,),
        in_specs=[rows(FOX_W, B_FG // FOX_W), rows(DIL_W, B_DG // DIL_W), rows(MEM_W, B_MG // MEM_W),
                  rows(FOX_W), rows(DIL_W), rows(MEM_W),
                  pl.BlockSpec((MIX_W, D_MODEL), lambda i: (0, 0)),
                  rows(D_MODEL), rows(D_MODEL), pl.BlockSpec((1, D_MODEL), lambda i: (0, 0))],
        out_specs=[rows(D_MODEL), rows(FOX_W), rows(DIL_W), rows(MEM_W), rows(FOX_W), rows(DIL_W), rows(MEM_W),
                   pl.BlockSpec((MIX_W, D_MODEL), lambda i: (0, 0)), pl.BlockSpec((8, D_MODEL), lambda i: (0, 0))],
        out_shape=[jax.ShapeDtypeStruct((T, D_MODEL), F32), jax.ShapeDtypeStruct((T, FOX_W), BF16),
                   jax.ShapeDtypeStruct((T, DIL_W), BF16), jax.ShapeDtypeStruct((T, MEM_W), BF16),
                   jax.ShapeDtypeStruct((T, FOX_W), BF16), jax.ShapeDtypeStruct((T, DIL_W), BF16),
                   jax.ShapeDtypeStruct((T, MEM_W), BF16), jax.ShapeDtypeStruct((MIX_W, D_MODEL), BF16),
                   jax.ShapeDtypeStruct((8, D_MODEL), F32)],
        scratch_shapes=[pltpu.VMEM((MIX_W, D_MODEL), F32)],
        compiler_params=_params(("arbitrary",)),
    )(proj, proj, proj, o_fox, o_dil, o_mem, w_out, x, target, gf)


def adamw(w, g, m, v, tr, name):
    lead = w.shape[:-2]
    R, C = w.shape[-2:]
    zeros = (0,) * len(lead)

    def body(w_ref, g_ref, m_ref, v_ref, d_ref, mo_ref, vo_ref):
        gv = g_ref[...]
        mn = ADAM_B1 * m_ref[...] + (1.0 - ADAM_B1) * gv
        vn = ADAM_B2 * v_ref[...] + (1.0 - ADAM_B2) * jnp.square(gv)
        m_hat = mn / (1.0 - ADAM_B1 ** ADAM_STEP)
        v_hat = vn / (1.0 - ADAM_B2 ** ADAM_STEP)
        d_ref[...] = -ADAM_LR * (m_hat / (jnp.sqrt(v_hat) + ADAM_EPS) + ADAM_WD * w_ref[...])
        mo_ref[...] = mn
        vo_ref[...] = vn

    spec = pl.BlockSpec((1,) * len(lead) + (tr, C), lambda i: zeros + (i, 0))
    return pl.pallas_call(
        body, name=name, grid=(pl.cdiv(R, tr),),
        in_specs=[spec] * 4, out_specs=[spec] * 3,
        out_shape=[jax.ShapeDtypeStruct(w.shape, F32)] * 3,
        compiler_params=_params(("arbitrary",)),
    )(w, g, m, v)


def adamw_columns_first(w, g, m, v, name):
    N = w.shape[0]
    chunk = 16
    main = N // chunk * chunk

    def body(w_hbm, g_hbm, m_hbm, v_hbm, d_hbm, mo_hbm, vo_hbm, wb, gb, mb, vb, db, mob, vob, sems):
        loads = [pltpu.make_async_copy(h.reshape(N, LANES), b, sems.at[k])
                 for k, (h, b) in enumerate(((w_hbm, wb), (g_hbm, gb), (m_hbm, mb), (v_hbm, vb)))]
        for cp in loads:
            cp.start()
        for cp in loads:
            cp.wait()

        def update(rows):
            gv = gb[rows, :]
            mn = ADAM_B1 * mb[rows, :] + (1.0 - ADAM_B1) * gv
            vn = ADAM_B2 * vb[rows, :] + (1.0 - ADAM_B2) * jnp.square(gv)
            m_hat = mn / (1.0 - ADAM_B1 ** ADAM_STEP)
            v_hat = vn / (1.0 - ADAM_B2 ** ADAM_STEP)
            db[rows, :] = -ADAM_LR * (m_hat / (jnp.sqrt(v_hat) + ADAM_EPS) + ADAM_WD * wb[rows, :])
            mob[rows, :] = mn
            vob[rows, :] = vn

        def step(i, _):
            update(pl.ds(pl.multiple_of(i * chunk, chunk), chunk))
            return 0

        lax.fori_loop(0, main // chunk, step, 0, unroll=4)
        if main < N:
            update(slice(main, N))
        stores = [pltpu.make_async_copy(b, h.reshape(N, LANES), sems.at[k])
                  for k, (h, b) in enumerate(((d_hbm, db), (mo_hbm, mob), (vo_hbm, vob)))]
        for cp in stores:
            cp.start()
        for cp in stores:
            cp.wait()

    any_spec = pl.BlockSpec(memory_space=pl.ANY)
    return pl.pallas_call(
        body, name=name,
        in_specs=[any_spec] * 4, out_specs=[any_spec] * 3,
        out_shape=[jax.ShapeDtypeStruct(w.shape, F32)] * 3,
        scratch_shapes=[pltpu.VMEM((N, LANES), F32)] * 7 + [pltpu.SemaphoreType.DMA((4,))],
        compiler_params=_params(),
    )(w, g, m, v)


def _pad_row(v, width):
    return jnp.concatenate([v, jnp.zeros((1, width - v.shape[1]), v.dtype)], axis=1)


def local_grads(x, mem, norm_g, b_forget, mem_norm_g, final_norm_g, loss_target, w_in_a, first_token, late_weights,
                start_exchange):
    B, S, D = x.shape
    T = B * S
    xt = x.reshape(T, D)
    memt = mem.reshape(B * MEM_LEN, D)
    b_pad = _pad_row(b_forget, LANES)

    h, h_t = rms_fwd(xt, norm_g, 512, "rms_x", with_transpose=True)
    proj_a = mm_nn(h, w_in_a, 512, PA, "in_proj_a", first_token)
    proj_a3 = proj_a.reshape(B, S, PA)

    negc = fox_gate(proj_a3, b_pad)
    causal = _log_masks_t(S, "causal")
    dilated = _log_masks_t(S, "dilated")
    rope = _rope_tables(S)

    o_fox, lse_fox = attn_fwd("fox", proj_a3, S, negc_cols=negc, mask=causal)

    w_in_b, w_kv, w_out = late_weights(o_fox)
    proj_b = mm_nn(h, w_in_b, 512, PB // 2, "in_proj_b")
    proj_b3 = proj_b.reshape(B, S, PB)
    o_dil, lse_dil = attn_fwd("dil", proj_b3, S, mask=dilated, rope=rope)

    mh, mh_t = rms_fwd(memt, mem_norm_g, B * MEM_LEN, "rms_mem", with_transpose=True)
    mkv = mm_nn(mh, w_kv, B * MEM_LEN, 2 * MEM_W, "mem_kv_proj")
    mkv3 = mkv.reshape(B, MEM_LEN, 2 * MEM_W)
    o_mem, lse_mem = attn_fwd("mem", proj_b3, S, kv=mkv3)

    dx2, do_fox, do_dil, do_mem, dfg, ddg, dmg, g_out, small_out = out_step(
        proj_b, o_fox.reshape(T, FOX_W), o_dil.reshape(T, DIL_W), o_mem.reshape(T, MEM_W), w_out,
        xt, loss_target.reshape(T, D), final_norm_g.reshape(1, D), 256)

    gates = [(dfg, 1, B_FG), (ddg, 1, B_DG), (dmg, 1, B_MG)]
    g_gates = mm_tn_multi(h_t, [piece[0] for piece in gates], 1024, "w_in_grad_gates", BF16)
    first, token = start_exchange([g_gates, g_out], "early_exchange_a")

    dqkv_fox, dneg, drow = attn_bwd("fox", proj_a3, do_fox.reshape(B, S, FOX_W), o_fox, lse_fox, S,
                                    negc_cols=negc, mask=causal, token=token)
    drow = drow.reshape(B, FOX_HEADS // 2, S // TQ, 2, TQ).transpose(0, 1, 3, 2, 4).reshape(B, FOX_HEADS, S)
    drow = jnp.pad(drow, ((0, 0), (0, LANES - FOX_HEADS), (0, 0)))
    dflog, db_part = fox_gate_bwd(drow, dneg, proj_a3, b_pad)
    fox = [(dqkv_fox.reshape(T, 3 * FOX_W), 0, A_FOX), (dflog.reshape(T, LANES), 0, A_FLOG)]
    g_fox = mm_tn_multi(h_t, [piece[0] for piece in fox], 1024, "w_in_grad_fox", BF16)
    second, token = start_exchange([g_fox], "early_exchange_b")

    (dqkv_dil,) = attn_bwd("dil", proj_b3, do_dil.reshape(B, S, DIL_W), o_dil, lse_dil, S, mask=dilated, rope=rope,
                           token=token)
    dil = [(dqkv_dil.reshape(T, 3 * DIL_W), 1, B_DIL)]
    g_dil = mm_tn_multi(h_t, [piece[0] for piece in dil], 1024, "w_in_grad_dil", BF16)
    third, token = start_exchange([g_dil], "early_exchange_c")

    dmq, dmk, dmv = attn_bwd("mem", proj_b3, do_mem.reshape(B, S, MEM_W), o_mem, lse_mem, S, kv=mkv3, token=token)
    mq = [(dmq.reshape(T, MEM_W), 1, B_MQ)]
    g_mq = mm_tn_multi(h_t, [piece[0] for piece in mq], 1024, "w_in_grad_mq", BF16)
    dmkv = jnp.concatenate([dmk, dmv], axis=2).reshape(B * MEM_LEN, 2 * MEM_W)
    g_kv = mm_tn_multi(mh_t, [dmkv], B * MEM_LEN, "w_kv_grad", BF16)
    fourth, token = start_exchange([g_mq, g_kv], "early_exchange_d")

    grad_x, dng = in_proj_bwd_rms(gates + fox + dil + mq, (w_in_a, w_in_b), xt, norm_g, dx2, 256, token)
    dmh = mm_nt(dmkv, w_kv, B * MEM_LEN, D, "mem_kv_bwd")
    _, dmng = rms_bwd(memt, mem_norm_g, dmh, None, B * MEM_LEN, "rms_mem_bwd")

    small = jnp.concatenate([dng[0:1], dmng[0:1], small_out[0:1], _pad_row(db_part[0:1], D), small_out[1:2],
                             jnp.zeros((3, D), F32)], axis=0)
    early = [(first, dqkv_fox), (second, dqkv_dil), (third, dmq), (fourth, grad_x)]
    return grad_x.reshape(B, S, D), early, small


def kernel(x, mem, norm_g, w_in, b_forget, mem_norm_g, w_mem_kv, w_out, final_norm_g, loss_target, m_norm_g, m_w_in, m_b_forget, m_mem_norm_g, m_w_mem_kv, m_w_out, m_final_norm_g, v_norm_g, v_w_in, v_b_forget, v_mem_norm_g, v_w_mem_kv, v_w_out, v_final_norm_g):
    D = D_MODEL
    shard_a, shard_b = _split_cols(_pack_cols(w_in).astype(BF16).reshape(w_in.shape[1], PW))
    (w_in_a,) = weight_gather([shard_a])
    gather, gather_token = early_exchange_start(
        [shard_b, w_mem_kv[0].astype(BF16), w_out[0].astype(BF16)], "late_gather", gather=True, after=w_in_a,
        relations=_SIBLING_AND_SAME_CORES)

    def late_weights(after):
        _, gathered = early_exchange_wait(gather, after, "late_gather_wait")
        return pass_on_to_sibling(gathered, gather["rows"], "late_gather_pass")

    grad_x, early, small = local_grads(
        x, mem, norm_g, b_forget, mem_norm_g, final_norm_g, loss_target, w_in_a, gather_token, late_weights,
        early_exchange_start)

    (first, after_first), (second, after_second), (third, after_third), (fourth, after_fourth) = early
    (src_gates, src_out), (land_gates, land_out) = early_exchange_wait(first, after_first, "early_wait_a")
    (src_fox,), (land_fox,) = early_exchange_wait(second, after_second, "early_wait_b")
    (src_dil,), (land_dil,) = early_exchange_wait(third, after_third, "early_wait_c")
    (src_mq, src_kv), (land_mq, land_kv) = early_exchange_wait(fourth, after_fourth, "early_wait_d")
    gates = slot_sum8(src_gates, land_gates, 128, "sum_w_in_gates")
    gw_out = slot_sum8(src_out, land_out, 256, "sum_w_out")
    fox = slot_sum8(src_fox, land_fox, 128, "sum_w_in_fox")
    dil = slot_sum8(src_dil, land_dil, 128, "sum_w_in_dil")
    mq = slot_sum8(src_mq, land_mq, 128, "sum_w_in_mq")
    gw_kv = slot_sum8(src_kv, land_kv, 128, "sum_w_kv")

    tot = small_all_reduce(small)
    gw_in = _unpack_cols(jnp.concatenate(
        [fox[:, :3 * FOX_W], gates[:, :FOX_W], dil, gates[:, FOX_W:FOX_W + DIL_W], mq,
         gates[:, FOX_W + DIL_W:], fox[:, 3 * FOX_W:]], axis=1)[None])

    loss = tot[4, 0]
    g_norm, g_mem_norm, g_final, g_b = tot[0:1], tot[1:2], tot[2], tot[3:4, :FOX_HEADS]

    def rows8(*rows):
        rows = [r.reshape(1, -1) for r in rows]
        rows = [_pad_row(r, D) for r in rows]
        return jnp.concatenate(rows + [jnp.zeros((8 - len(rows), D), F32)], axis=0)

    sw = rows8(norm_g, mem_norm_g, final_norm_g, b_forget)
    sm = rows8(m_norm_g, m_mem_norm_g, m_final_norm_g, m_b_forget)
    sv = rows8(v_norm_g, v_mem_norm_g, v_final_norm_g, v_b_forget)
    d_s, m_s, v_s = adamw(sw, tot, sm, sv, 8, "adamw_small")
    columns_first = lambda a: jnp.transpose(a, (2, 0, 1))
    d_in, m_in, v_in = [jnp.transpose(o, (1, 2, 0)) for o in adamw_columns_first(
        columns_first(w_in), columns_first(gw_in), columns_first(m_w_in), columns_first(v_w_in), "adamw_w_in")]
    d_kv, m_kv, v_kv = adamw(w_mem_kv[0], gw_kv, m_w_mem_kv[0], v_w_mem_kv[0], 128, "adamw_w_kv")
    d_out, m_out, v_out = adamw(w_out[0], gw_out, m_w_out[0], v_w_out[0], 256, "adamw_w_out")

    def small_outs(t):
        return t[0:1], t[3:4, :FOX_HEADS], t[1:2], t[2]

    grads = (g_norm, gw_in, g_b, g_mem_norm, gw_kv[None], gw_out[None], g_final)
    outs = []
    for t, big in ((d_s, (d_in, d_kv, d_out)), (m_s, (m_in, m_kv, m_out)), (v_s, (v_in, v_kv, v_out))):
        n, b, mn, f = small_outs(t)
        outs += [n, big[0], b, mn, big[1][None], big[2][None], f]
    return (loss, grad_x, *grads, *outs)
```

```python
import math

import numpy as np
import jax
import jax.numpy as jnp
from jax import lax
from jax.experimental import pallas as pl
from jax.experimental.pallas import tpu as pltpu

F32 = jnp.float32
BF16 = jnp.bfloat16

D_MODEL = 1024
HEAD_DIM = 64
FOX_HEADS = 12
DIL_HEADS = 12
MEM_HEADS = 4
MEM_HEAD_DIM = 128
MEM_LEN = 256
FOX_W = FOX_HEADS * HEAD_DIM
DIL_W = DIL_HEADS * HEAD_DIM
MEM_W = MEM_HEADS * MEM_HEAD_DIM
MIX_W = FOX_W + DIL_W + MEM_W
DILATIONS = ((128, 1), (512, 4), (2048, 16))
ROPE_THETA = 500000.0
ROPE_DIM = HEAD_DIM // 4
RMS_EPS = 1e-6
NEG_INF = -1e30
IN_W = 4 * FOX_W + FOX_HEADS + 4 * DIL_W + 2 * MEM_W

ADAM_LR = 0.001
ADAM_B1 = 0.9
ADAM_B2 = 0.999
ADAM_EPS = 1e-08
ADAM_WD = 0.01
ADAM_STEP = 10

N_DEV = 8
LANES = 128
PAIR_W = 3 * LANES
TQ = 256
TK = 256

O_FQ, O_FK, O_FV, O_FG = 0, FOX_W, 2 * FOX_W, 3 * FOX_W
O_FLOG = 4 * FOX_W
O_DQ = O_FLOG + FOX_HEADS
O_DK, O_DV, O_DG = O_DQ + DIL_W, O_DQ + 2 * DIL_W, O_DQ + 3 * DIL_W
O_MQ = O_DQ + 4 * DIL_W
O_MG = O_MQ + MEM_W
P_FOX = 0
P_FG = P_FOX + 3 * FOX_W
P_DIL = P_FG + FOX_W
P_DG = P_DIL + 3 * DIL_W
P_MQ = P_DG + DIL_W
P_MG = P_MQ + MEM_W
P_FLOG = P_MG + MEM_W
PW = P_FLOG + LANES
A_FOX = 0
A_FLOG = A_FOX + 3 * FOX_W
PA = A_FLOG + LANES
B_FG = 0
B_DG = B_FG + FOX_W
B_DIL = B_DG + DIL_W
B_MQ = B_DIL + 3 * DIL_W
B_MG = -(-(B_MQ + MEM_W) // MEM_W) * MEM_W
PB = B_MG + MEM_W

VMEM_LIMIT = 56 * 1024 * 1024


def _pack_pieces():
    pieces = []
    for base in (O_FQ, O_DQ):
        seg = []
        for hp in range(FOX_HEADS // 2):
            for part in range(3):
                seg.append((base + part * FOX_W + hp * LANES, LANES))
        pieces.append(seg)
    fox, dil = pieces
    return fox + [(O_FG, FOX_W)] + dil + [(O_DG, DIL_W), (O_MQ, MEM_W), (O_MG, MEM_W), (O_FLOG, FOX_HEADS)]


def _pack_cols(w):
    parts = [w[..., s:s + n] for s, n in _pack_pieces()]
    parts.append(jnp.zeros(w.shape[:-1] + (LANES - FOX_HEADS,), w.dtype))
    return jnp.concatenate(parts, axis=-1)


def _split_cols(wp):
    def cut(start, width):
        return wp[..., start:start + width]

    group_a = jnp.concatenate([cut(P_FOX, 3 * FOX_W), cut(P_FLOG, LANES)], axis=-1)
    pad = jnp.zeros(wp.shape[:-1] + (B_MG - B_MQ - MEM_W,), wp.dtype)
    group_b = jnp.concatenate([cut(P_FG, FOX_W), cut(P_DG, DIL_W), cut(P_DIL, 3 * DIL_W), cut(P_MQ, MEM_W), pad,
                               cut(P_MG, MEM_W)], axis=-1)
    return group_a, group_b


def _unpack_cols(g):
    runs = []
    pos = 0
    for s, n in _pack_pieces():
        runs.append((s, n, pos))
        pos += n
    runs.sort()
    return jnp.concatenate([g[..., p:p + n] for s, n, p in runs], axis=-1)


def _params(sem=None, **kw):
    return pltpu.CompilerParams(dimension_semantics=sem, vmem_limit_bytes=VMEM_LIMIT, **kw)


def _mesh_pos():
    return lax.axis_index("x"), lax.axis_index("y"), lax.axis_index("c")


def _flip(v, d):
    return 1 - v if d else v


_RELATIONS = [(dx, dy, dc) for dx in (0, 1) for dy in (0, 1) for dc in (0, 1)][1:]
_SIBLING_AND_SAME_CORES = [(0, 0, 1), (1, 0, 0), (0, 1, 0), (1, 1, 0)]


def weight_gather(shards):
    n_arr = len(shards)
    rows = [s.shape[0] for s in shards]

    def body(*refs):
        in_refs = refs[:n_arr]
        out_refs = refs[n_arr:2 * n_arr]
        send_sems, recv_sems, local_sems = refs[2 * n_arr:]
        x, y, c = _mesh_pos()
        me, sibling = (x, y, c), (x, y, 1 - c)
        x_nbr, y_nbr, diag = (1 - x, y, c), (x, 1 - y, c), (1 - x, 1 - y, c)
        north = c == 1
        relay_from = (jnp.where(north, 1 - x, x), jnp.where(north, y, 1 - y), c)
        relay_to = (jnp.where(north, x, 1 - x), jnp.where(north, 1 - y, y), c)
        k_from = jnp.where(north, 1, 2)
        k_to = 3 - k_from

        def block(a, pos):
            px, py, pc = pos
            return out_refs[a].at[pl.ds((4 * px + 2 * py + pc) * rows[a], rows[a]), :]

        def copy(a, k, blk, to, src=None):
            return pltpu.make_async_remote_copy(
                src_ref=block(a, blk) if src is None else src, dst_ref=block(a, blk),
                send_sem=send_sems.at[a, k], recv_sem=recv_sems.at[a, k],
                device_id=to, device_id_type=pl.DeviceIdType.MESH)

        started = []
        mine = []
        for a in range(n_arr):
            cp = pltpu.make_async_copy(in_refs[a], block(a, me), local_sems.at[a])
            cp.start()
            mine.append(cp)
            first = [copy(a, 0, me, sibling, src=in_refs[a]), copy(a, 1, me, x_nbr, src=in_refs[a]),
                     copy(a, 2, me, y_nbr, src=in_refs[a])]
            for cp in first:
                cp.start()
            started += first
        for a in range(n_arr):
            copy(a, k_from, relay_from, me).wait_recv()
            second_hop = copy(a, 3, relay_from, relay_to)
            second_hop.start()
            passed = copy(a, 3 + k_from, relay_from, sibling)
            passed.start()
            started += [second_hop, passed]
        for a in range(n_arr):
            copy(a, k_to, relay_to, me).wait_recv()
            passed = copy(a, 3 + k_to, relay_to, sibling)
            passed.start()
            started.append(passed)
        for a in range(n_arr):
            copy(a, 3, diag, me).wait_recv()
            passed = copy(a, 6, diag, sibling)
            passed.start()
            started.append(passed)
        for a in range(n_arr):
            copy(a, 0, sibling, me).wait_recv()
            for k, chip in ((4, x_nbr), (5, y_nbr), (6, diag)):
                copy(a, k, (chip[0], chip[1], 1 - c), me).wait_recv()
        for cp in started:
            cp.wait_send()
        for cp in mine:
            cp.wait()

    any_spec = pl.BlockSpec(memory_space=pl.ANY)
    return pl.pallas_call(
        body, name="weight_gather",
        out_shape=[jax.ShapeDtypeStruct((N_DEV * s.shape[0], s.shape[1]), s.dtype) for s in shards],
        in_specs=[any_spec] * n_arr, out_specs=[any_spec] * n_arr,
        scratch_shapes=[pltpu.SemaphoreType.DMA((n_arr, 7)), pltpu.SemaphoreType.DMA((n_arr, 7)),
                        pltpu.SemaphoreType.DMA((n_arr,))],
    )(*shards)


_OTHER_CHIPS = [(1, 0), (0, 1), (1, 1)]


def small_all_reduce(small):
    vmem_spec = pl.BlockSpec(memory_space=pltpu.VMEM)

    def body(small_ref, tot_ref, land, send_sems, recv_sems):
        x, y, c = _mesh_pos()
        me = 4 * x + 2 * y + c
        land[me] = small_ref[...]
        sends, recvs = [], []
        for j, (dx, dy, dc) in enumerate(_RELATIONS):
            px, py, pc = _flip(x, dx), _flip(y, dy), _flip(c, dc)
            common = dict(send_sem=send_sems.at[j], recv_sem=recv_sems.at[j],
                          device_id=(px, py, pc), device_id_type=pl.DeviceIdType.MESH)
            sends.append(pltpu.make_async_remote_copy(src_ref=small_ref, dst_ref=land.at[me], **common))
            recvs.append(pltpu.make_async_remote_copy(src_ref=small_ref, dst_ref=land.at[4 * px + 2 * py + pc], **common))
        for cp in sends:
            cp.start()
        for cp in recvs:
            cp.wait_recv()
        for cp in sends:
            cp.wait_send()
        tot = land[0]
        for d in range(1, N_DEV):
            tot = tot + land[d]
        tot_ref[...] = tot

    return pl.pallas_call(
        body, name="small_sum", out_shape=jax.ShapeDtypeStruct(small.shape, small.dtype),
        in_specs=[vmem_spec], out_specs=vmem_spec,
        scratch_shapes=[pltpu.VMEM((N_DEV,) + small.shape, small.dtype),
                        pltpu.SemaphoreType.DMA((len(_RELATIONS),)), pltpu.SemaphoreType.DMA((len(_RELATIONS),))],
    )(small)


_HBM = pl.BlockSpec(memory_space=pltpu.HBM)
_SEM = pl.BlockSpec(memory_space=pltpu.SEMAPHORE)
_EFFECT = pltpu.SideEffectType.DATAFLOW_SIDE_EFFECTING


def _early_copies(src_refs, land_refs, send_sems, recv_sems, rows, gather, relations):
    x, y, c = _mesh_pos()
    me = 4 * x + 2 * y + c
    copies = []
    for a in range(len(src_refs)):
        for dx, dy, dc in relations:
            px, py, pc = _flip(x, dx), _flip(y, dy), _flip(c, dc)
            peer = 4 * px + 2 * py + pc
            copies.append(pltpu.make_async_remote_copy(
                src_ref=src_refs[a] if gather else src_refs[a].at[pl.ds(peer * rows[a], rows[a]), :],
                dst_ref=land_refs[a].at[pl.ds(me * rows[a], rows[a]), :],
                send_sem=send_sems[a], recv_sem=recv_sems[a],
                device_id=(px, py, pc), device_id_type=pl.DeviceIdType.MESH))
    return copies


def own_slots(shards, name):
    n = len(shards)
    x, y, c = _mesh_pos()
    me = (4 * x + 2 * y + c).astype(jnp.int32).reshape(1)
    empties = [lax.empty((N_DEV * s.shape[0], s.shape[1]), s.dtype) for s in shards]

    def body(me_ref, *refs):
        for a in range(n):
            refs[2 * n + a][...] = refs[a][...]

    return pl.pallas_call(
        body, name=name,
        grid_spec=pltpu.PrefetchScalarGridSpec(
            num_scalar_prefetch=1, grid=(1,),
            in_specs=[pl.BlockSpec(s.shape, lambda i, w: (0, 0)) for s in shards]
            + [pl.BlockSpec(memory_space=pl.ANY)] * n,
            out_specs=[pl.BlockSpec(s.shape, lambda i, w: (w[0], 0)) for s in shards]),
        out_shape=[jax.ShapeDtypeStruct(e.shape, e.dtype) for e in empties],
        input_output_aliases={1 + n + a: a for a in range(n)},
        compiler_params=_params(("arbitrary",)),
    )(me, *shards, *empties)


def early_exchange_start(srcs, name, gather=False, after=None, relations=_RELATIONS):
    n = len(srcs)
    if gather:
        rows = [s.shape[0] for s in srcs]
        lands = list(own_slots(srcs, name + "_place"))
    else:
        rows = [s.shape[0] // N_DEV for s in srcs]
        lands = [lax.empty(s.shape, s.dtype) for s in srcs]

    extra = [] if after is None else [after]

    def body(*refs):
        src_refs, land_refs = refs[:n], refs[n:2 * n]
        first_sem = 2 * n + len(extra)
        send_sems, recv_sems = refs[first_sem:first_sem + n], refs[first_sem + n:first_sem + 2 * n]
        token = refs[-1]
        for cp in _early_copies(src_refs, land_refs, send_sems, recv_sems, rows, gather, relations):
            cp.start()
        token[...] = jnp.zeros_like(token)

    hbm = lambda a: pltpu.HBM(a.shape, a.dtype)
    outs = pl.pallas_call(
        body, name=name,
        out_shape=[pltpu.SemaphoreType.DMA(())] * (2 * n)
        + [hbm(a) for a in srcs] + [hbm(a) for a in lands] + [jax.ShapeDtypeStruct((8, LANES), F32)],
        in_specs=[_HBM] * (2 * n) + [pl.BlockSpec(memory_space=pl.ANY)] * len(extra),
        out_specs=[_SEM] * (2 * n) + [_HBM] * (2 * n) + [pl.BlockSpec(memory_space=pltpu.VMEM)],
        input_output_aliases={i: 2 * n + i for i in range(2 * n)},
        compiler_params=pltpu.CompilerParams(has_side_effects=_EFFECT),
    )(*[pltpu.with_memory_space_constraint(a, pltpu.HBM) for a in list(srcs) + lands], *extra)
    handle = dict(sems=outs[:2 * n], srcs=outs[2 * n:3 * n], lands=outs[3 * n:4 * n], rows=rows,
                  copies=len(relations))
    return handle, outs[-1]


def early_exchange_wait(handle, after, name):
    n = len(handle["srcs"])
    rows = handle["rows"]

    def body(*refs):
        src_refs, land_refs = refs[:n], refs[n:2 * n]
        send_sems, recv_sems = refs[2 * n:3 * n], refs[3 * n:4 * n]
        x, y, c = _mesh_pos()
        for a in range(n):
            span = pl.ds(0, handle["copies"] * rows[a])
            all_copies = pltpu.make_async_remote_copy(
                src_ref=land_refs[a].at[span, :], dst_ref=land_refs[a].at[span, :],
                send_sem=send_sems[a], recv_sem=recv_sems[a],
                device_id=(x, y, c), device_id_type=pl.DeviceIdType.MESH)
            all_copies.wait_send()
            all_copies.wait_recv()

    hbm = lambda a: pltpu.HBM(a.shape, a.dtype)
    ins = list(handle["srcs"]) + list(handle["lands"])
    outs = pl.pallas_call(
        body, name=name,
        out_shape=[hbm(a) for a in ins],
        in_specs=[_HBM] * (2 * n) + [_SEM] * (2 * n) + [pl.BlockSpec(memory_space=pl.ANY)],
        out_specs=[_HBM] * (2 * n),
        input_output_aliases={i: i for i in range(2 * n)},
        compiler_params=pltpu.CompilerParams(has_side_effects=_EFFECT),
    )(*ins, *handle["sems"], after)
    return outs[:n], outs[n:]


def pass_on_to_sibling(lands, rows, name):
    n = len(lands)

    def body(*refs):
        land_refs = refs[n:2 * n]
        send_sems, recv_sems = refs[2 * n:]
        x, y, c = _mesh_pos()
        copies = []
        for a in range(n):
            for k, (dx, dy) in enumerate(_OTHER_CHIPS):
                slot = 4 * _flip(x, dx) + 2 * _flip(y, dy) + c
                blk = land_refs[a].at[pl.ds(slot * rows[a], rows[a]), :]
                copies.append(pltpu.make_async_remote_copy(
                    src_ref=blk, dst_ref=blk, send_sem=send_sems.at[a, k], recv_sem=recv_sems.at[a, k],
                    device_id=(x, y, 1 - c), device_id_type=pl.DeviceIdType.MESH))
        for cp in copies:
            cp.start()
        for cp in copies:
            cp.wait_recv()
        for cp in copies:
            cp.wait_send()

    any_spec = pl.BlockSpec(memory_space=pl.ANY)
    return pl.pallas_call(
        body, name=name,
        out_shape=[jax.ShapeDtypeStruct(a.shape, a.dtype) for a in lands],
        in_specs=[any_spec] * n, out_specs=[any_spec] * n,
        input_output_aliases={i: i for i in range(n)},
        scratch_shapes=[pltpu.SemaphoreType.DMA((n, len(_OTHER_CHIPS))), pltpu.SemaphoreType.DMA((n, len(_OTHER_CHIPS)))],
    )(*lands)


def slot_sum8(src, land, tr, name):
    rows, cols = land.shape[0] // N_DEV, land.shape[1]
    x, y, c = _mesh_pos()
    me = (4 * x + 2 * y + c).astype(jnp.int32).reshape(1)

    def body(me_ref, src_ref, land_ref, o_ref):
        acc = None
        for d in range(N_DEV):
            term = jnp.where(d == me_ref[0], src_ref[0], land_ref[d]).astype(F32)
            acc = term if acc is None else acc + term
        o_ref[...] = acc

    return pl.pallas_call(
        body, name=name,
        grid_spec=pltpu.PrefetchScalarGridSpec(
            num_scalar_prefetch=1, grid=(rows // tr,),
            in_specs=[pl.BlockSpec((1, tr, cols), lambda i, w: (w[0], i, 0)),
                      pl.BlockSpec((N_DEV, tr, cols), lambda i, w: (0, i, 0))],
            out_specs=pl.BlockSpec((tr, cols), lambda i, w: (i, 0))),
        out_shape=jax.ShapeDtypeStruct((rows, cols), F32),
        compiler_params=_params(("arbitrary",)),
    )(me, src.reshape(N_DEV, rows, cols), land.reshape(N_DEV, rows, cols))


def mm_tn_multi(a_t, bs, tt, name, out_dtype=F32):
    K, T = a_t.shape
    widths = [b.shape[1] for b in bs]
    steps = T // tt

    def body(a_ref, *rest):
        b_refs, o_ref, acc = rest[:-2], rest[-2], rest[-1]

        @pl.when(pl.program_id(0) == 0)
        def _():
            acc[...] = jnp.zeros(acc.shape, F32)

        av = a_ref[...]
        col = 0
        for b_ref, w in zip(b_refs, widths):
            acc[:, col:col + w] += jnp.dot(av, b_ref[...], preferred_element_type=F32)
            col += w

        @pl.when(pl.program_id(0) == steps - 1)
        def _():
            o_ref[...] = acc[...].astype(out_dtype)

    return pl.pallas_call(
        body, name=name, grid=(steps,),
        in_specs=[pl.BlockSpec((K, tt), lambda t: (0, t))] + [pl.BlockSpec((tt, w), lambda t: (t, 0)) for w in widths],
        out_specs=pl.BlockSpec((K, sum(widths)), lambda t: (0, 0)),
        out_shape=jax.ShapeDtypeStruct((K, sum(widths)), out_dtype),
        scratch_shapes=[pltpu.VMEM((K, sum(widths)), F32)],
        compiler_params=_params(("arbitrary",)),
    )(a_t, *bs)


def rms_fwd(x, g, tm, name, with_transpose=False, token=None):
    M, K = x.shape
    extra = [] if token is None else [token]

    def body(x_ref, g_ref, *rest):
        o_ref = rest[len(extra)]
        xv = x_ref[...]
        r = lax.rsqrt(jnp.mean(xv * xv, axis=-1, keepdims=True) + RMS_EPS)
        h = ((xv * r) * g_ref[...]).astype(BF16)
        o_ref[...] = h
        if with_transpose:
            rest[len(extra) + 1][...] = h.T

    out_specs = [pl.BlockSpec((tm, K), lambda i: (i, 0))]
    out_shape = [jax.ShapeDtypeStruct((M, K), BF16)]
    if with_transpose:
        out_specs.append(pl.BlockSpec((K, tm), lambda i: (0, i)))
        out_shape.append(jax.ShapeDtypeStruct((K, M), BF16))
    outs = pl.pallas_call(
        body, name=name, grid=(M // tm,),
        in_specs=[pl.BlockSpec((tm, K), lambda i: (i, 0)), pl.BlockSpec((1, K), lambda i: (0, 0))]
        + [pl.BlockSpec(t.shape, lambda i: (0, 0)) for t in extra],
        out_specs=out_specs, out_shape=out_shape,
        compiler_params=_params(("arbitrary",)),
    )(x, g, *extra)
    return outs if with_transpose else outs[0]


def rms_bwd(x, g, dh, dres, tm, name):
    M, K = x.shape
    has_res = dres is not None

    def body(*refs):
        if has_res:
            x_ref, g_ref, dh_ref, dres_ref, dx_ref, dg_ref = refs
        else:
            x_ref, g_ref, dh_ref, dx_ref, dg_ref = refs
        xv = x_ref[...]
        r = lax.rsqrt(jnp.mean(xv * xv, axis=-1, keepdims=True) + RMS_EPS)
        xn = xv * r
        dhv = dh_ref[...]
        dxn = dhv * g_ref[...]
        dx = r * (dxn - xn * jnp.mean(dxn * xn, axis=-1, keepdims=True))
        if has_res:
            dx = dx + dres_ref[...]
        dx_ref[...] = dx
        part = jnp.sum(dhv * xn, axis=0, keepdims=True)
        row = lax.broadcasted_iota(jnp.int32, (8, K), 0)
        upd = jnp.where(row == 0, part, 0.0)

        @pl.when(pl.program_id(0) == 0)
        def _():
            dg_ref[...] = upd

        @pl.when(pl.program_id(0) != 0)
        def _():
            dg_ref[...] += upd

    row_spec = pl.BlockSpec((tm, K), lambda i: (i, 0))
    ins = [x, g, dh] + ([dres] if has_res else [])
    in_specs = [row_spec, pl.BlockSpec((1, K), lambda i: (0, 0)), row_spec] + ([row_spec] if has_res else [])
    return pl.pallas_call(
        body, name=name, grid=(M // tm,),
        in_specs=in_specs,
        out_specs=[row_spec, pl.BlockSpec((8, K), lambda i: (0, 0))],
        out_shape=[jax.ShapeDtypeStruct((M, K), F32), jax.ShapeDtypeStruct((8, K), F32)],
        compiler_params=_params(("arbitrary",)),
    )(*ins)


def mm_nn(a, b, tm, tn, name, token=None):
    M, K = a.shape
    N = b.shape[1]
    extra = [] if token is None else [token]

    def body(a_ref, b_ref, *rest):
        rest[-1][...] = jnp.dot(a_ref[...], b_ref[...], preferred_element_type=F32)

    return pl.pallas_call(
        body, name=name, grid=(N // tn, M // tm),
        in_specs=[pl.BlockSpec((tm, K), lambda j, i: (i, 0)), pl.BlockSpec((K, tn), lambda j, i: (0, j))]
        + [pl.BlockSpec(t.shape, lambda j, i: (0, 0)) for t in extra],
        out_specs=pl.BlockSpec((tm, tn), lambda j, i: (i, j)),
        out_shape=jax.ShapeDtypeStruct((M, N), F32),
        compiler_params=_params(("arbitrary", "arbitrary")),
    )(a, b, *extra)


def mm_nt(a, b, tm, tk, name):
    M, K = a.shape
    N = b.shape[0]

    def body(a_ref, b_ref, o_ref):
        part = lax.dot_general(a_ref[...], b_ref[...], (((1,), (1,)), ((), ())), preferred_element_type=F32)

        @pl.when(pl.program_id(1) == 0)
        def _():
            o_ref[...] = part

        @pl.when(pl.program_id(1) != 0)
        def _():
            o_ref[...] += part

    return pl.pallas_call(
        body, name=name, grid=(M // tm, K // tk),
        in_specs=[pl.BlockSpec((tm, tk), lambda i, k: (i, k)), pl.BlockSpec((N, tk), lambda i, k: (0, k))],
        out_specs=pl.BlockSpec((tm, N), lambda i, k: (i, 0)),
        out_shape=jax.ShapeDtypeStruct((M, N), F32),
        compiler_params=_params(("arbitrary", "arbitrary")),
    )(a, b)


def in_proj_bwd_rms(pieces, ws, x, g, dres, tm, token):
    M, N = x.shape

    def body(*refs):
        n = len(pieces)
        p_refs, w_refs = refs[:n], refs[n:n + len(ws)]
        x_ref, g_ref, dres_ref, _, dx_ref, dg_ref = refs[n + len(ws):]
        dh = None
        for p_ref, (arr, group, col) in zip(p_refs, pieces):
            part = lax.dot_general(p_ref[...], w_refs[group][:, col:col + arr.shape[1]], (((1,), (1,)), ((), ())),
                                   preferred_element_type=F32)
            dh = part if dh is None else dh + part
        xv = x_ref[...]
        r = lax.rsqrt(jnp.mean(xv * xv, axis=-1, keepdims=True) + RMS_EPS)
        xn = xv * r
        dxn = dh * g_ref[...]
        dx_ref[...] = r * (dxn - xn * jnp.mean(dxn * xn, axis=-1, keepdims=True)) + dres_ref[...]
        row = lax.broadcasted_iota(jnp.int32, (8, N), 0)
        upd = jnp.where(row == 0, jnp.sum(dh * xn, axis=0, keepdims=True), 0.0)

        @pl.when(pl.program_id(0) == 0)
        def _():
            dg_ref[...] = upd

        @pl.when(pl.program_id(0) != 0)
        def _():
            dg_ref[...] += upd

    row_spec = pl.BlockSpec((tm, N), lambda i: (i, 0))
    return pl.pallas_call(
        body, name="in_proj_bwd", grid=(M // tm,),
        in_specs=[pl.BlockSpec((tm, arr.shape[1]), lambda i: (i, 0)) for arr, _, _ in pieces]
        + [pl.BlockSpec(w.shape, lambda i: (0, 0)) for w in ws]
        + [row_spec, pl.BlockSpec((1, N), lambda i: (0, 0)), row_spec, pl.BlockSpec(token.shape, lambda i: (0, 0))],
        out_specs=[row_spec, pl.BlockSpec((8, N), lambda i: (0, 0))],
        out_shape=[jax.ShapeDtypeStruct((M, N), F32), jax.ShapeDtypeStruct((8, N), F32)],
        compiler_params=_params(("arbitrary",)),
    )(*[arr for arr, _, _ in pieces], *ws, x, g, dres, token)


def _log_sigmoid(z):
    return jnp.minimum(z, 0.0) - jnp.log(1.0 + jnp.exp(-jnp.abs(z)))


def _tri(n, lower):
    r = lax.broadcasted_iota(jnp.int32, (n, n), 0)
    c = lax.broadcasted_iota(jnp.int32, (n, n), 1)
    return jnp.where((r >= c) if lower else (r <= c), 1.0, 0.0).astype(F32)


def fox_gate(proj3, b_pad):
    B, S, _ = proj3.shape
    nblk = S // TK

    def body(f_ref, b_ref, o_ref):
        tri = _tri(TK, True)
        carry = jnp.zeros((1, LANES), F32)
        for n in range(nblk):
            z = f_ref[0, n * TK:(n + 1) * TK, :] + b_ref[...]
            logf = _log_sigmoid(z)
            cs = jnp.dot(tri, logf, preferred_element_type=F32, precision=lax.Precision.HIGHEST) + carry
            carry = cs[TK - 1:TK, :]
            o_ref[0, n * TK:(n + 1) * TK, :] = -cs

    return pl.pallas_call(
        body, name="fox_gate", grid=(B,),
        in_specs=[pl.BlockSpec((1, S, LANES), lambda b: (b, 0, A_FLOG // LANES)),
                  pl.BlockSpec((1, LANES), lambda b: (0, 0))],
        out_specs=pl.BlockSpec((1, S, LANES), lambda b: (b, 0, 0)),
        out_shape=jax.ShapeDtypeStruct((B, S, LANES), F32),
        compiler_params=_params(("arbitrary",)),
    )(proj3, b_pad)


def fox_gate_bwd(drow, dneg, proj3, b_pad):
    B, S, _ = proj3.shape
    nblk = S // TK

    def body(d_ref, r_ref, f_ref, b_ref, o_ref, db_ref):
        tri = _tri(TK, False)
        lane = lax.broadcasted_iota(jnp.int32, (TK, LANES), 1)
        carry = jnp.zeros((1, LANES), F32)
        dbsum = jnp.zeros((1, LANES), F32)
        for n in reversed(range(nblk)):
            dk_side = None
            for hp in range(FOX_HEADS // 2):
                two = jnp.where(lane < 2, r_ref[0, n * TK:(n + 1) * TK, hp * LANES:(hp + 1) * LANES], 0.0)
                two = pltpu.roll(two, 2 * hp, 1) if hp else two
                dk_side = two if dk_side is None else dk_side + two
            dc = jnp.where(lane < FOX_HEADS, d_ref[0, :, n * TK:(n + 1) * TK].T - dk_side, 0.0)
            rs = jnp.dot(tri, dc, preferred_element_type=F32, precision=lax.Precision.HIGHEST) + carry
            carry = rs[0:1, :]
            z = f_ref[0, n * TK:(n + 1) * TK, :] + b_ref[...]
            dz = rs * (1.0 / (1.0 + jnp.exp(z)))
            o_ref[0, n * TK:(n + 1) * TK, :] = dz.astype(BF16)
            dbsum = dbsum + jnp.sum(dz, axis=0, keepdims=True)
        row = lax.broadcasted_iota(jnp.int32, (8, LANES), 0)
        upd = jnp.where(row == 0, dbsum, 0.0)

        @pl.when(pl.program_id(0) == 0)
        def _():
            db_ref[...] = upd

        @pl.when(pl.program_id(0) != 0)
        def _():
            db_ref[...] += upd

    return pl.pallas_call(
        body, name="fox_gate_bwd", grid=(B,),
        in_specs=[pl.BlockSpec((1, LANES, S), lambda b: (b, 0, 0)),
                  pl.BlockSpec((1, S, FOX_W), lambda b: (b, 0, 0)),
                  pl.BlockSpec((1, S, LANES), lambda b: (b, 0, A_FLOG // LANES)),
                  pl.BlockSpec((1, LANES), lambda b: (0, 0))],
        out_specs=[pl.BlockSpec((1, S, LANES), lambda b: (b, 0, 0)), pl.BlockSpec((8, LANES), lambda b: (0, 0))],
        out_shape=[jax.ShapeDtypeStruct((B, S, LANES), BF16), jax.ShapeDtypeStruct((8, LANES), F32)],
        compiler_params=_params(("arbitrary",)),
    )(drow, dneg, proj3, b_pad)


def _rope_tables(S):
    half = ROPE_DIM // 2
    f32 = np.float32
    pos = np.arange(S, dtype=f32)
    inv_freq = f32(1.0) / np.power(f32(ROPE_THETA), np.arange(0, ROPE_DIM, 2, dtype=f32) / f32(ROPE_DIM)).astype(f32)
    ang = (pos[:, None] * inv_freq[None, :]).astype(f32).astype(np.float64)
    cos, sin = np.cos(ang).astype(f32), np.sin(ang).astype(f32)
    one = np.ones((S, HEAD_DIM - ROPE_DIM), f32)
    zero = np.zeros((S, HEAD_DIM - ROPE_DIM), f32)
    zh = np.zeros((S, half), f32)
    c = np.concatenate([cos, cos, one], axis=1)
    s1 = np.concatenate([-sin, zh, zero], axis=1)
    s2 = np.concatenate([zh, sin, zero], axis=1)
    return tuple(jnp.asarray(np.concatenate([t, t], axis=1)) for t in (c, s1, s2))


_HALF_ROPE = ROPE_DIM // 2


def _rope(t, c, s1, s2):
    return t * c + pltpu.roll(t, LANES - _HALF_ROPE, 1) * s1 + pltpu.roll(t, _HALF_ROPE, 1) * s2


def _rope_bwd(d, c, s1, s2):
    return d * c + pltpu.roll(d * s1, _HALF_ROPE, 1) + pltpu.roll(d * s2, LANES - _HALF_ROPE, 1)


def _scale_parts(scale):
    m, _ = math.frexp(scale)
    return (scale, None) if m == 0.5 else (None, scale)


def _log_masks(S, kind):
    nd = 1 if kind == "causal" else S // TQ
    a = np.arange(TQ)[:, None]
    b = np.arange(TK)[None, :]
    out = np.zeros((nd, TQ, TK), np.float32)
    for d in range(nd):
        delta = d * TQ + a - b
        if kind == "causal":
            m = (delta >= 0).astype(np.float64)
        else:
            m = sum(((delta >= 0) & (delta % dil == 0) & (delta <= w)).astype(np.float64) for w, dil in DILATIONS)
        out[d] = np.where(m > 0, np.log(np.maximum(m, 1.0)), NEG_INF)
    return jnp.asarray(out)


def _attn_setup(kind):
    pair = kind != "mem"
    e_dim = HEAD_DIM if pair else MEM_HEAD_DIM
    q_fold, s_scale = _scale_parts(1.0 / math.sqrt(e_dim))
    return dict(pair=pair, col0={"fox": A_FOX, "dil": B_DIL, "mem": B_MQ}[kind],
                n_blocks=FOX_HEADS // 2 if pair else MEM_HEADS, q_fold=q_fold, s_scale=s_scale,
                nh=2 if pair else 1)


def _cat(parts, axis):
    return parts[0] if len(parts) == 1 else jnp.concatenate(parts, axis=axis)


def _log_masks_t(S, kind):
    return jnp.swapaxes(_log_masks(S, kind), 1, 2)


def _head_rows(hh, pair):
    row = lax.broadcasted_iota(jnp.int32, (LANES, 1), 0)
    if not pair:
        return row >= 0
    return (row >= HEAD_DIM * hh) & (row < HEAD_DIM * (hh + 1))


def _attn_t_inputs(kind, src, S, negc_cols, mask, rope, kv):
    cfg = _attn_setup(kind)
    col0 = cfg["col0"]
    ins, in_specs = [], []
    if cfg["pair"]:
        ins.append(src)
        in_specs.append(pl.BlockSpec((1, S, PAIR_W), lambda b, h: (b, 0, col0 // PAIR_W + h)))
    else:
        ins += [src, kv, kv]
        in_specs += [pl.BlockSpec((1, S, LANES), lambda b, h: (b, 0, col0 // LANES + h)),
                     pl.BlockSpec((1, MEM_LEN, LANES), lambda b, h: (b, 0, h)),
                     pl.BlockSpec((1, MEM_LEN, LANES), lambda b, h: (b, 0, MEM_HEADS + h))]
    if negc_cols is not None:
        ins.append(negc_cols)
        in_specs.append(pl.BlockSpec((1, S, LANES), lambda b, h: (b, 0, 0)))
    if mask is not None:
        ins.append(mask)
        in_specs.append(pl.BlockSpec(mask.shape, lambda b, h: (0, 0, 0)))
    if rope is not None:
        ins += list(rope)
        in_specs += [pl.BlockSpec((S, LANES), lambda b, h: (0, 0))] * 3
    return ins, in_specs


def _attn_t_prep(cfg, refs, S, Sk, *, qT2s, ks, vs=None, vTs=None, kTs=None, nb=None):
    pair, nh = cfg["pair"], cfg["nh"]
    lane = lax.broadcasted_iota(jnp.int32, (1, LANES), 1)
    rope_refs = refs["rope"]

    def prep_q(n):
        rows = slice(n * TQ, (n + 1) * TQ)
        q = refs["load_q"](rows)
        if rope_refs is not None:
            q = _rope(q, *[t[rows, :] for t in rope_refs])
        if cfg["q_fold"] is not None:
            q = q * cfg["q_fold"]
        qtb = q.astype(BF16).T
        for hh in range(nh):
            qT2s[n, :, hh * TQ:(hh + 1) * TQ] = jnp.where(_head_rows(hh, pair), qtb, jnp.zeros_like(qtb))

    def prep_kv(n):
        rows = slice(n * TK, (n + 1) * TK)
        k, v = refs["load_kv"](rows)
        if rope_refs is not None:
            k = _rope(k, *[t[rows, :] for t in rope_refs])
        kb = k.astype(BF16)
        vb = v.astype(BF16)
        ks[rows, :] = kb
        if vs is not None:
            vs[rows, :] = vb
        if vTs is not None:
            vTs[n] = vb.T
        if kTs is not None:
            kTs[n] = kb.T
        if nb is not None:
            blk = refs["negc"][0, rows, :]
            for hh in range(nh):
                h = 2 * refs["block"] + hh
                col = jnp.sum(jnp.where(lane == h, blk, 0.0), axis=1, keepdims=True)
                nb[hh, rows, :] = jnp.broadcast_to(col, (TK, LANES))

    for n in range(S // TQ):
        prep_q(n)
    for n in range(Sk // TK):
        prep_kv(n)


def _raw_scores_t(cfg, k, qT2):
    sT = jnp.dot(k, qT2, preferred_element_type=F32)
    if cfg["s_scale"] is not None:
        sT = sT * cfg["s_scale"]
    return sT


def _bias_mask_t(cfg, sT, nb, mask_ref, kc, midx):
    nh = cfg["nh"]
    if nb is None and midx is None:
        return sT
    parts = []
    for hh in range(nh):
        t = sT[:, hh * TQ:(hh + 1) * TQ]
        if nb is not None:
            t = t + jnp.concatenate([nb[hh, kc, :]] * (TQ // LANES), axis=1)
        if midx is not None:
            t = t + mask_ref[midx]
        parts.append(t)
    return _cat(parts, 1)


def _tile_pairs(kind, nq, nk):
    if kind == "mem":
        return [(i, j) for i in range(nq) for j in range(nk)], (lambda i, j: None)
    pairs = [(i, j) for i in range(nq) for j in range(i + 1)]
    if kind == "fox":
        return pairs, (lambda i, j: 0 if j == i else None)
    return pairs, (lambda i, j: i - j)


def attn_fwd(kind, src, S, *, negc_cols=None, mask=None, rope=None, kv=None):
    B = src.shape[0]
    cfg = _attn_setup(kind)
    pair, nh = cfg["pair"], cfg["nh"]
    Sk = S if pair else MEM_LEN
    has_bias, has_rope = negc_cols is not None, rope is not None
    R = nh * TQ
    nq, nk = S // TQ, Sk // TK
    pairs, mask_index = _tile_pairs(kind, nq, nk)

    def body(*refs):
        refs = list(refs)
        if pair:
            qkv_ref = refs.pop(0)
            load_q = lambda rows: qkv_ref[0, rows, 0:LANES]
            load_kv = lambda rows: (qkv_ref[0, rows, LANES:2 * LANES], qkv_ref[0, rows, 2 * LANES:3 * LANES])
        else:
            q_ref, k_ref, v_ref = refs.pop(0), refs.pop(0), refs.pop(0)
            load_q = lambda rows: q_ref[0, rows, :]
            load_kv = lambda rows: (k_ref[0, rows, :], v_ref[0, rows, :])
        negc_ref = refs.pop(0) if has_bias else None
        mask_ref = refs.pop(0) if mask is not None else None
        rope_refs = [refs.pop(0) for _ in range(3)] if has_rope else None
        o_ref, lse_ref, qT2s, ks, vTs, s_a, s_b, p_a, p_b = refs[:9]
        nb = refs[9] if has_bias else None
        _attn_t_prep(cfg, dict(load_q=load_q, load_kv=load_kv, rope=rope_refs, negc=negc_ref,
                               block=pl.program_id(1)), S, Sk, qT2s=qT2s, ks=ks, vTs=vTs, nb=nb)

        def cols(j):
            return slice(j * TK, (j + 1) * TK)

        def scores(i, j):
            return _raw_scores_t(cfg, ks[cols(j), :], qT2s[i])

        def finish(i, m, l, accT):
            oT2 = accT / l
            oT = jnp.where(_head_rows(0, True), oT2[:, 0:TQ], oT2[:, TQ:2 * TQ]) if pair else oT2
            o_ref[0, i * TQ:(i + 1) * TQ, :] = oT.T
            lse_ref[0, 0, i:i + 1, :] = m + jnp.log(l)

        s_bufs, p_bufs = (s_a, s_b), (p_a, p_b)
        s_bufs[0][...] = scores(*pairs[0])
        m = l = accT = None
        for t, (i, j) in enumerate(pairs):
            cur, oth = t % 2, 1 - t % 2
            if t > 0:
                i_prev, j_prev = pairs[t - 1]
                pv = jnp.dot(vTs[j_prev], p_bufs[oth][...], preferred_element_type=F32)
                acc_full = pv if accT is None else accT + pv
            if t + 1 < len(pairs):
                s_bufs[oth][...] = scores(*pairs[t + 1])
            first = j == 0
            if first and t > 0:
                finish(i_prev, m, l, acc_full)
            sT = _bias_mask_t(cfg, s_bufs[cur][...], nb, mask_ref, cols(j), mask_index(i, j))
            m_tile = jnp.max(sT, axis=0, keepdims=True)
            m_new = m_tile if first else jnp.maximum(m, m_tile)
            p = jnp.exp(sT - m_new)
            p_bufs[cur][...] = p.astype(BF16)
            if first:
                l, accT = jnp.sum(p, axis=0, keepdims=True), None
            else:
                alpha = jnp.exp(m - m_new)
                l, accT = alpha * l + jnp.sum(p, axis=0, keepdims=True), acc_full * alpha
            m = m_new
        i_last, j_last = pairs[-1]
        pv = jnp.dot(vTs[j_last], p_bufs[(len(pairs) - 1) % 2][...], preferred_element_type=F32)
        finish(i_last, m, l, pv if accT is None else accT + pv)

    ins, in_specs = _attn_t_inputs(kind, src, S, negc_cols, mask, rope, kv)
    W = cfg["n_blocks"] * LANES
    scratch = [pltpu.VMEM((nq, LANES, R), BF16), pltpu.VMEM((Sk, LANES), BF16), pltpu.VMEM((nk, LANES, TK), BF16),
               pltpu.VMEM((TK, R), F32), pltpu.VMEM((TK, R), F32), pltpu.VMEM((TK, R), BF16), pltpu.VMEM((TK, R), BF16)]
    if has_bias:
        scratch.append(pltpu.VMEM((nh, Sk, LANES), F32))
    return pl.pallas_call(
        body, name=kind + "_attn_fwd", grid=(B, cfg["n_blocks"]),
        in_specs=in_specs,
        out_specs=[pl.BlockSpec((1, S, LANES), lambda b, h: (b, 0, h)),
                   pl.BlockSpec((1, 1, nq, R), lambda b, h: (b, h, 0, 0))],
        out_shape=[jax.ShapeDtypeStruct((B, S, W), F32), jax.ShapeDtypeStruct((B, cfg["n_blocks"], nq, R), F32)],
        scratch_shapes=scratch,
        compiler_params=_params(("arbitrary", "arbitrary")),
    )(*ins)


def attn_bwd(kind, src, do, o, lse, S, *, negc_cols=None, mask=None, rope=None, kv=None, token=None):
    B = src.shape[0]
    cfg = _attn_setup(kind)
    pair, nh, s_scale, q_fold = cfg["pair"], cfg["nh"], cfg["s_scale"], cfg["q_fold"]
    Sk = S if pair else MEM_LEN
    has_bias, has_rope = negc_cols is not None, rope is not None
    R = nh * TQ
    nq, nk = S // TQ, Sk // TK
    pairs, mask_index = _tile_pairs(kind, nq, nk)

    def body(*refs):
        refs = list(refs)
        if pair:
            qkv_ref = refs.pop(0)
            load_q = lambda rows: qkv_ref[0, rows, 0:LANES]
            load_kv = lambda rows: (qkv_ref[0, rows, LANES:2 * LANES], qkv_ref[0, rows, 2 * LANES:3 * LANES])
        else:
            q_ref, k_ref, v_ref = refs.pop(0), refs.pop(0), refs.pop(0)
            load_q = lambda rows: q_ref[0, rows, :]
            load_kv = lambda rows: (k_ref[0, rows, :], v_ref[0, rows, :])
        negc_ref = refs.pop(0) if has_bias else None
        mask_ref = refs.pop(0) if mask is not None else None
        rope_refs = [refs.pop(0) for _ in range(3)] if has_rope else None
        do_ref, o_ref, lse_ref = refs.pop(0), refs.pop(0), refs.pop(0)
        if token is not None:
            refs.pop(0)
        if pair:
            dqkv_ref = refs.pop(0)
            dneg_ref = refs.pop(0) if has_bias else None
            drow_ref = refs.pop(0) if has_bias else None
        else:
            dq_ref, dk_ref, dv_ref = refs.pop(0), refs.pop(0), refs.pop(0)
        qT2s, ks, vs, kTs, doT2s, delta_s, dk_acc, dv_acc = refs[:8]
        bufs_a, bufs_b = refs[8:12], refs[12:16]
        nb, dneg_acc = (refs[16], refs[17]) if has_bias else (None, None)
        lane = lax.broadcasted_iota(jnp.int32, (1, LANES), 1)
        _attn_t_prep(cfg, dict(load_q=load_q, load_kv=load_kv, rope=rope_refs, negc=negc_ref,
                               block=pl.program_id(1)), S, Sk,
                     qT2s=qT2s, ks=ks, vs=vs, kTs=kTs, nb=nb)

        def prep_do(n):
            rows = slice(n * TQ, (n + 1) * TQ)
            doT = do_ref[0, rows, :].astype(BF16).astype(F32).T
            prodT = doT * o_ref[0, rows, :].T
            doTb = doT.astype(BF16)
            for hh in range(nh):
                hm = _head_rows(hh, pair)
                doT2s[n, :, hh * TQ:(hh + 1) * TQ] = jnp.where(hm, doTb, jnp.zeros_like(doTb))
                delta_s[n:n + 1, hh * TQ:(hh + 1) * TQ] = jnp.sum(jnp.where(hm, prodT, 0.0), axis=0, keepdims=True)

        for n in range(nq):
            prep_do(n)
        dk_acc[...] = jnp.zeros(dk_acc.shape, F32)
        dv_acc[...] = jnp.zeros(dv_acc.shape, F32)
        if has_bias:
            dneg_acc[...] = jnp.zeros(dneg_acc.shape, F32)

        def cols(j):
            return slice(j * TK, (j + 1) * TK)

        nt_dims = (((1,), (1,)), ((), ()))

        def first_products(i, j, bufs):
            bufs[0][...] = _raw_scores_t(cfg, ks[cols(j), :], qT2s[i])
            bufs[1][...] = jnp.dot(vs[cols(j), :], doT2s[i], preferred_element_type=F32)

        def last_products(i, j, bufs, dqT2):
            dv_acc[j] += lax.dot_general(doT2s[i], bufs[2][...], nt_dims, preferred_element_type=F32)
            dk_acc[j] += lax.dot_general(qT2s[i], bufs[3][...], nt_dims, preferred_element_type=F32)
            dq = jnp.dot(kTs[j], bufs[3][...], preferred_element_type=F32)
            return dq if dqT2 is None else dqT2 + dq

        def finish_q(i, dqT2, drow):
            rows = slice(i * TQ, (i + 1) * TQ)
            dqT = jnp.where(_head_rows(0, True), dqT2[:, 0:TQ], dqT2[:, TQ:2 * TQ]) if pair else dqT2
            dq = dqT.T
            if q_fold is not None:
                dq = dq * q_fold
            if has_rope:
                dq = _rope_bwd(dq, *[t[rows, :] for t in rope_refs])
            if pair:
                dqkv_ref[0, rows, 0:LANES] = dq.astype(BF16)
            else:
                dq_ref[0, rows, :] = dq.astype(BF16)
            if has_bias:
                drow_ref[0, 0, i:i + 1, :] = drow

        bufs = (bufs_a, bufs_b)
        first_products(*pairs[0], bufs[0])
        dqT2 = drow = None
        for t, (i, j) in enumerate(pairs):
            cur, oth = bufs[t % 2], bufs[1 - t % 2]
            first = j == 0
            if first and t > 0:
                i_prev, j_prev = pairs[t - 1]
                finish_q(i_prev, last_products(i_prev, j_prev, oth, dqT2), drow)
                dqT2 = drow = None
            sT = _bias_mask_t(cfg, cur[0][...], nb, mask_ref, cols(j), mask_index(i, j))
            pT = jnp.exp(sT - lse_ref[0, 0, i:i + 1, :])
            dsT = pT * (cur[1][...] - delta_s[i:i + 1, :])
            if has_bias:
                tile_rows = jnp.sum(dsT, axis=0, keepdims=True)
                drow = tile_rows if drow is None else drow + tile_rows
                for hh in range(nh):
                    part = dsT[:, hh * TQ:hh * TQ + LANES]
                    for u in range(1, TQ // LANES):
                        part = part + dsT[:, hh * TQ + u * LANES:hh * TQ + (u + 1) * LANES]
                    dneg_acc[hh, cols(j), :] += part
            if s_scale is not None:
                dsT = dsT * s_scale
            cur[2][...] = pT.astype(BF16)
            cur[3][...] = dsT.astype(BF16)
            if not first:
                dqT2 = last_products(*pairs[t - 1], oth, dqT2)
            if t + 1 < len(pairs):
                first_products(*pairs[t + 1], oth)
        i_last, j_last = pairs[-1]
        finish_q(i_last, last_products(i_last, j_last, bufs[(len(pairs) - 1) % 2], dqT2), drow)

        for n in range(nk):
            rows = slice(n * TK, (n + 1) * TK)
            dk = dk_acc[n].T
            dv = dv_acc[n].T
            if has_rope:
                dk = _rope_bwd(dk, *[t[rows, :] for t in rope_refs])
            if pair:
                dqkv_ref[0, rows, LANES:2 * LANES] = dk.astype(BF16)
                dqkv_ref[0, rows, 2 * LANES:3 * LANES] = dv.astype(BF16)
            else:
                dk_ref[0, rows, :] = dk.astype(BF16)
                dv_ref[0, rows, :] = dv.astype(BF16)
            if has_bias:
                x0 = jnp.sum(dneg_acc[0, rows, :], axis=1, keepdims=True)
                x1 = jnp.sum(dneg_acc[1, rows, :], axis=1, keepdims=True)
                dneg_ref[0, rows, :] = jnp.where(lane == 0, x0, jnp.where(lane == 1, x1, 0.0))

    ins, in_specs = _attn_t_inputs(kind, src, S, negc_cols, mask, rope, kv)
    row_spec = pl.BlockSpec((1, S, LANES), lambda b, h: (b, 0, h))
    vec_spec = pl.BlockSpec((1, 1, nq, R), lambda b, h: (b, h, 0, 0))
    ins += [do, o, lse]
    in_specs += [row_spec, row_spec, vec_spec]
    if token is not None:
        ins.append(token)
        in_specs.append(pl.BlockSpec(token.shape, lambda b, h: (0, 0)))
    W = cfg["n_blocks"] * LANES
    if pair:
        out_specs = [pl.BlockSpec((1, S, PAIR_W), lambda b, h: (b, 0, h))]
        out_shape = [jax.ShapeDtypeStruct((B, S, 3 * W), BF16)]
        if has_bias:
            out_specs += [row_spec, vec_spec]
            out_shape += [jax.ShapeDtypeStruct((B, S, W), F32), jax.ShapeDtypeStruct((B, cfg["n_blocks"], nq, R), F32)]
    else:
        kv_spec = pl.BlockSpec((1, MEM_LEN, LANES), lambda b, h: (b, 0, h))
        out_specs = [row_spec, kv_spec, kv_spec]
        out_shape = [jax.ShapeDtypeStruct((B, S, W), BF16)] + [jax.ShapeDtypeStruct((B, MEM_LEN, W), BF16)] * 2
    scratch = [pltpu.VMEM((nq, LANES, R), BF16), pltpu.VMEM((Sk, LANES), BF16),
               pltpu.VMEM((Sk, LANES), BF16), pltpu.VMEM((nk, LANES, TK), BF16), pltpu.VMEM((nq, LANES, R), BF16),
               pltpu.VMEM((nq, R), F32), pltpu.VMEM((nk, LANES, TK), F32), pltpu.VMEM((nk, LANES, TK), F32)]
    pair_bufs = [pltpu.VMEM((TK, R), F32), pltpu.VMEM((TK, R), F32), pltpu.VMEM((TK, R), BF16), pltpu.VMEM((TK, R), BF16)]
    scratch += pair_bufs + pair_bufs
    if has_bias:
        scratch += [pltpu.VMEM((nh, Sk, LANES), F32), pltpu.VMEM((nh, Sk, LANES), F32)]
    return pl.pallas_call(
        body, name=kind + "_attn_bwd", grid=(B, cfg["n_blocks"]),
        in_specs=in_specs, out_specs=out_specs, out_shape=out_shape, scratch_shapes=scratch,
        compiler_params=_params(("arbitrary", "arbitrary")),
    )(*ins)


def _sigmoid(g):
    return 1.0 / (1.0 + jnp.exp(-g))


def out_step(proj, o_fox, o_dil, o_mem, w_out, x, target, gf, tm):
    T = x.shape[0]

    def body(fg_ref, dg_ref, mg_ref, of_ref, od_ref, om_ref, w_ref, x_ref, t_ref, gf_ref,
             dx_ref, dof_ref, dod_ref, dom_ref, dfg_ref, ddg_ref, dmg_ref, gw_ref, sm_ref, gw_acc):
        branches = []
        for g_ref, o_ref in ((fg_ref, of_ref), (dg_ref, od_ref), (mg_ref, om_ref)):
            g = g_ref[...]
            sg = _sigmoid(g)
            o = o_ref[...]
            branches.append((g, sg, o))
        ymix = jnp.concatenate([(o * (g * sg)).astype(BF16) for g, sg, o in branches], axis=1)
        x2 = x_ref[...] + jnp.dot(ymix, w_ref[...], preferred_element_type=F32)
        r = lax.rsqrt(jnp.mean(x2 * x2, axis=-1, keepdims=True) + RMS_EPS)
        yn = x2 * r
        err = yn * gf_ref[...] - t_ref[...]
        loss = 0.5 * jnp.sum(jnp.sum(err * err, axis=-1, keepdims=True) / D_MODEL, axis=0, keepdims=True)
        dyf = err / D_MODEL
        dgf = jnp.sum(dyf * yn, axis=0, keepdims=True)
        dyn = dyf * gf_ref[...]
        dx2 = r * (dyn - yn * jnp.mean(dyn * yn, axis=-1, keepdims=True))
        dx_ref[...] = dx2
        dxb = dx2.astype(BF16)
        dmix = lax.dot_general(dxb, w_ref[...], (((1,), (1,)), ((), ())), preferred_element_type=F32)
        col = 0
        for (g, sg, o), do_ref, dgate_ref in zip(branches, (dof_ref, dod_ref, dom_ref), (dfg_ref, ddg_ref, dmg_ref)):
            d = dmix[:, col:col + g.shape[1]]
            col += g.shape[1]
            do_ref[...] = (d * (g * sg)).astype(BF16)
            dgate_ref[...] = (d * o * (sg * (1.0 + g * (1.0 - sg)))).astype(BF16)
        row = lax.broadcasted_iota(jnp.int32, (8, D_MODEL), 0)
        upd = jnp.where(row == 0, dgf, jnp.where(row == 1, loss, 0.0))

        @pl.when(pl.program_id(0) == 0)
        def _():
            sm_ref[...] = jnp.zeros(sm_ref.shape, F32)
            gw_acc[...] = jnp.zeros(gw_acc.shape, F32)

        sm_ref[...] += upd
        gw_acc[...] += lax.dot_general(ymix, dxb, (((0,), (0,)), ((), ())), preferred_element_type=F32)

        @pl.when(pl.program_id(0) == T // tm - 1)
        def _():
            gw_ref[...] = gw_acc[...].astype(BF16)

    def rows(w, col=0):
        return pl.BlockSpec((tm, w), lambda i: (i, col))

    return pl.pallas_call(
        body, name="out_step", grid=(T // tm,),
        in_specs=[rows(FOX_W, B_FG // FOX_W), rows(DIL_W, B_DG // DIL_W), rows(MEM_W, B_MG // MEM_W),
                  rows(FOX_W), rows(DIL_W), rows(MEM_W),
                  pl.BlockSpec((MIX_W, D_MODEL), lambda i: (0, 0)),
                  rows(D_MODEL), rows(D_MODEL), pl.BlockSpec((1, D_MODEL), lambda i: (0, 0))],
        out_specs=[rows(D_MODEL), rows(FOX_W), rows(DIL_W), rows(MEM_W), rows(FOX_W), rows(DIL_W), rows(MEM_W),
                   pl.BlockSpec((MIX_W, D_MODEL), lambda i: (0, 0)), pl.BlockSpec((8, D_MODEL), lambda i: (0, 0))],
        out_shape=[jax.ShapeDtypeStruct((T, D_MODEL), F32), jax.ShapeDtypeStruct((T, FOX_W), BF16),
                   jax.ShapeDtypeStruct((T, DIL_W), BF16), jax.ShapeDtypeStruct((T, MEM_W), BF16),
                   jax.ShapeDtypeStruct((T, FOX_W), BF16), jax.ShapeDtypeStruct((T, DIL_W), BF16),
                   jax.ShapeDtypeStruct((T, MEM_W), BF16), jax.ShapeDtypeStruct((MIX_W, D_MODEL), BF16),
                   jax.ShapeDtypeStruct((8, D_MODEL), F32)],
        scratch_shapes=[pltpu.VMEM((MIX_W, D_MODEL), F32)],
        compiler_params=_params(("arbitrary",)),
    )(proj, proj, proj, o_fox, o_dil, o_mem, w_out, x, target, gf)


def adamw(w, g, m, v, tr, name):
    lead = w.shape[:-2]
    R, C = w.shape[-2:]
    zeros = (0,) * len(lead)

    def body(w_ref, g_ref, m_ref, v_ref, d_ref, mo_ref, vo_ref):
        gv = g_ref[...]
        mn = ADAM_B1 * m_ref[...] + (1.0 - ADAM_B1) * gv
        vn = ADAM_B2 * v_ref[...] + (1.0 - ADAM_B2) * jnp.square(gv)
        m_hat = mn / (1.0 - ADAM_B1 ** ADAM_STEP)
        v_hat = vn / (1.0 - ADAM_B2 ** ADAM_STEP)
        d_ref[...] = -ADAM_LR * (m_hat / (jnp.sqrt(v_hat) + ADAM_EPS) + ADAM_WD * w_ref[...])
        mo_ref[...] = mn
        vo_ref[...] = vn

    spec = pl.BlockSpec((1,) * len(lead) + (tr, C), lambda i: zeros + (i, 0))
    return pl.pallas_call(
        body, name=name, grid=(pl.cdiv(R, tr),),
        in_specs=[spec] * 4, out_specs=[spec] * 3,
        out_shape=[jax.ShapeDtypeStruct(w.shape, F32)] * 3,
        compiler_params=_params(("arbitrary",)),
    )(w, g, m, v)


def adamw_columns_first(w, g, m, v, name):
    N = w.shape[0]
    chunk = 16
    main = N // chunk * chunk

    def body(w_hbm, g_hbm, m_hbm, v_hbm, d_hbm, mo_hbm, vo_hbm, wb, gb, mb, vb, db, mob, vob, sems):
        loads = [pltpu.make_async_copy(h.reshape(N, LANES), b, sems.at[k])
                 for k, (h, b) in enumerate(((w_hbm, wb), (g_hbm, gb), (m_hbm, mb), (v_hbm, vb)))]
        for cp in loads:
            cp.start()
        for cp in loads:
            cp.wait()

        def update(rows):
            gv = gb[rows, :]
            mn = ADAM_B1 * mb[rows, :] + (1.0 - ADAM_B1) * gv
            vn = ADAM_B2 * vb[rows, :] + (1.0 - ADAM_B2) * jnp.square(gv)
            m_hat = mn / (1.0 - ADAM_B1 ** ADAM_STEP)
            v_hat = vn / (1.0 - ADAM_B2 ** ADAM_STEP)
            db[rows, :] = -ADAM_LR * (m_hat / (jnp.sqrt(v_hat) + ADAM_EPS) + ADAM_WD * wb[rows, :])
            mob[rows, :] = mn
            vob[rows, :] = vn

        def step(i, _):
            update(pl.ds(pl.multiple_of(i * chunk, chunk), chunk))
            return 0

        lax.fori_loop(0, main // chunk, step, 0, unroll=4)
        if main < N:
            update(slice(main, N))
        stores = [pltpu.make_async_copy(b, h.reshape(N, LANES), sems.at[k])
                  for k, (h, b) in enumerate(((d_hbm, db), (mo_hbm, mob), (vo_hbm, vob)))]
        for cp in stores:
            cp.start()
        for cp in stores:
            cp.wait()

    any_spec = pl.BlockSpec(memory_space=pl.ANY)
    return pl.pallas_call(
        body, name=name,
        in_specs=[any_spec] * 4, out_specs=[any_spec] * 3,
        out_shape=[jax.ShapeDtypeStruct(w.shape, F32)] * 3,
        scratch_shapes=[pltpu.VMEM((N, LANES), F32)] * 7 + [pltpu.SemaphoreType.DMA((4,))],
        compiler_params=_params(),
    )(w, g, m, v)


def _pad_row(v, width):
    return jnp.concatenate([v, jnp.zeros((1, width - v.shape[1]), v.dtype)], axis=1)


def local_grads(x, mem, norm_g, b_forget, mem_norm_g, final_norm_g, loss_target, first_token, first_weights,
                late_weights, start_exchange):
    B, S, D = x.shape
    T = B * S
    xt = x.reshape(T, D)
    memt = mem.reshape(B * MEM_LEN, D)
    b_pad = _pad_row(b_forget, LANES)

    h, h_t = rms_fwd(xt, norm_g, 512, "rms_x", with_transpose=True, token=first_token)
    w_in_a, proj_token = first_weights(h)
    proj_a = mm_nn(h, w_in_a, 512, PA, "in_proj_a", proj_token)
    proj_a3 = proj_a.reshape(B, S, PA)

    negc = fox_gate(proj_a3, b_pad)
    causal = _log_masks_t(S, "causal")
    dilated = _log_masks_t(S, "dilated")
    rope = _rope_tables(S)

    o_fox, lse_fox = attn_fwd("fox", proj_a3, S, negc_cols=negc, mask=causal)

    w_in_b, w_kv, w_out = late_weights(o_fox)
    proj_b = mm_nn(h, w_in_b, 512, PB // 2, "in_proj_b")
    proj_b3 = proj_b.reshape(B, S, PB)
    o_dil, lse_dil = attn_fwd("dil", proj_b3, S, mask=dilated, rope=rope)

    mh, mh_t = rms_fwd(memt, mem_norm_g, B * MEM_LEN, "rms_mem", with_transpose=True)
    mkv = mm_nn(mh, w_kv, B * MEM_LEN, 2 * MEM_W, "mem_kv_proj")
    mkv3 = mkv.reshape(B, MEM_LEN, 2 * MEM_W)
    o_mem, lse_mem = attn_fwd("mem", proj_b3, S, kv=mkv3)

    dx2, do_fox, do_dil, do_mem, dfg, ddg, dmg, g_out, small_out = out_step(
        proj_b, o_fox.reshape(T, FOX_W), o_dil.reshape(T, DIL_W), o_mem.reshape(T, MEM_W), w_out,
        xt, loss_target.reshape(T, D), final_norm_g.reshape(1, D), 256)

    gates = [(dfg, 1, B_FG), (ddg, 1, B_DG), (dmg, 1, B_MG)]
    g_gates = mm_tn_multi(h_t, [piece[0] for piece in gates], 1024, "w_in_grad_gates", BF16)
    first, token = start_exchange([g_gates, g_out], "early_exchange_a")

    dqkv_fox, dneg, drow = attn_bwd("fox", proj_a3, do_fox.reshape(B, S, FOX_W), o_fox, lse_fox, S,
                                    negc_cols=negc, mask=causal, token=token)
    drow = drow.reshape(B, FOX_HEADS // 2, S // TQ, 2, TQ).transpose(0, 1, 3, 2, 4).reshape(B, FOX_HEADS, S)
    drow = jnp.pad(drow, ((0, 0), (0, LANES - FOX_HEADS), (0, 0)))
    dflog, db_part = fox_gate_bwd(drow, dneg, proj_a3, b_pad)
    fox = [(dqkv_fox.reshape(T, 3 * FOX_W), 0, A_FOX), (dflog.reshape(T, LANES), 0, A_FLOG)]
    g_fox = mm_tn_multi(h_t, [piece[0] for piece in fox], 1024, "w_in_grad_fox", BF16)
    second, token = start_exchange([g_fox], "early_exchange_b")

    (dqkv_dil,) = attn_bwd("dil", proj_b3, do_dil.reshape(B, S, DIL_W), o_dil, lse_dil, S, mask=dilated, rope=rope,
                           token=token)
    dil = [(dqkv_dil.reshape(T, 3 * DIL_W), 1, B_DIL)]
    g_dil = mm_tn_multi(h_t, [piece[0] for piece in dil], 1024, "w_in_grad_dil", BF16)
    third, token = start_exchange([g_dil], "early_exchange_c")

    dmq, dmk, dmv = attn_bwd("mem", proj_b3, do_mem.reshape(B, S, MEM_W), o_mem, lse_mem, S, kv=mkv3, token=token)
    mq = [(dmq.reshape(T, MEM_W), 1, B_MQ)]
    g_mq = mm_tn_multi(h_t, [piece[0] for piece in mq], 1024, "w_in_grad_mq", BF16)
    dmkv = jnp.concatenate([dmk, dmv], axis=2).reshape(B * MEM_LEN, 2 * MEM_W)
    g_kv = mm_tn_multi(mh_t, [dmkv], B * MEM_LEN, "w_kv_grad", BF16)
    fourth, token = start_exchange([g_mq, g_kv], "early_exchange_d")

    grad_x, dng = in_proj_bwd_rms(gates + fox + dil + mq, (w_in_a, w_in_b), xt, norm_g, dx2, 256, token)
    dmh = mm_nt(dmkv, w_kv, B * MEM_LEN, D, "mem_kv_bwd")
    _, dmng = rms_bwd(memt, mem_norm_g, dmh, None, B * MEM_LEN, "rms_mem_bwd")

    small = jnp.concatenate([dng[0:1], dmng[0:1], small_out[0:1], _pad_row(db_part[0:1], D), small_out[1:2],
                             jnp.zeros((3, D), F32)], axis=0)
    early = [(first, dqkv_fox), (second, dqkv_dil), (third, dmq), (fourth, grad_x)]
    return grad_x.reshape(B, S, D), early, small


def kernel(x, mem, norm_g, w_in, b_forget, mem_norm_g, w_mem_kv, w_out, final_norm_g, loss_target, m_norm_g, m_w_in, m_b_forget, m_mem_norm_g, m_w_mem_kv, m_w_out, m_final_norm_g, v_norm_g, v_w_in, v_b_forget, v_mem_norm_g, v_w_mem_kv, v_w_out, v_final_norm_g):
    D = D_MODEL
    shard_a, shard_b = _split_cols(_pack_cols(w_in).astype(BF16).reshape(w_in.shape[1], PW))
    gather_a, first_token = early_exchange_start([shard_a], "first_gather", gather=True,
                                                 relations=_SIBLING_AND_SAME_CORES)
    late = {}

    def first_weights(after):
        _, gathered = early_exchange_wait(gather_a, after, "first_gather_wait")
        (w_in_a,) = pass_on_to_sibling(gathered, gather_a["rows"], "first_gather_pass")
        late["gather"], token = early_exchange_start(
            [shard_b, w_mem_kv[0].astype(BF16), w_out[0].astype(BF16)], "late_gather", gather=True, after=w_in_a,
            relations=_SIBLING_AND_SAME_CORES)
        return w_in_a, token

    def late_weights(after):
        _, gathered = early_exchange_wait(late["gather"], after, "late_gather_wait")
        return pass_on_to_sibling(gathered, late["gather"]["rows"], "late_gather_pass")

    grad_x, early, small = local_grads(
        x, mem, norm_g, b_forget, mem_norm_g, final_norm_g, loss_target, first_token, first_weights, late_weights,
        early_exchange_start)

    (first, after_first), (second, after_second), (third, after_third), (fourth, after_fourth) = early
    (src_gates, src_out), (land_gates, land_out) = early_exchange_wait(first, after_first, "early_wait_a")
    (src_fox,), (land_fox,) = early_exchange_wait(second, after_second, "early_wait_b")
    (src_dil,), (land_dil,) = early_exchange_wait(third, after_third, "early_wait_c")
    (src_mq, src_kv), (land_mq, land_kv) = early_exchange_wait(fourth, after_fourth, "early_wait_d")
    gates = slot_sum8(src_gates, land_gates, 128, "sum_w_in_gates")
    gw_out = slot_sum8(src_out, land_out, 256, "sum_w_out")
    fox = slot_sum8(src_fox, land_fox, 128, "sum_w_in_fox")
    dil = slot_sum8(src_dil, land_dil, 128, "sum_w_in_dil")
    mq = slot_sum8(src_mq, land_mq, 128, "sum_w_in_mq")
    gw_kv = slot_sum8(src_kv, land_kv, 128, "sum_w_kv")

    tot = small_all_reduce(small)
    gw_in = _unpack_cols(jnp.concatenate(
        [fox[:, :3 * FOX_W], gates[:, :FOX_W], dil, gates[:, FOX_W:FOX_W + DIL_W], mq,
         gates[:, FOX_W + DIL_W:], fox[:, 3 * FOX_W:]], axis=1)[None])

    loss = tot[4, 0]
    g_norm, g_mem_norm, g_final, g_b = tot[0:1], tot[1:2], tot[2], tot[3:4, :FOX_HEADS]

    def rows8(*rows):
        rows = [r.reshape(1, -1) for r in rows]
        rows = [_pad_row(r, D) for r in rows]
        return jnp.concatenate(rows + [jnp.zeros((8 - len(rows), D), F32)], axis=0)

    sw = rows8(norm_g, mem_norm_g, final_norm_g, b_forget)
    sm = rows8(m_norm_g, m_mem_norm_g, m_final_norm_g, m_b_forget)
    sv = rows8(v_norm_g, v_mem_norm_g, v_final_norm_g, v_b_forget)
    d_s, m_s, v_s = adamw(sw, tot, sm, sv, 8, "adamw_small")
    columns_first = lambda a: jnp.transpose(a, (2, 0, 1))
    d_in, m_in, v_in = [jnp.transpose(o, (1, 2, 0)) for o in adamw_columns_first(
        columns_first(w_in), columns_first(gw_in), columns_first(m_w_in), columns_first(v_w_in), "adamw_w_in")]
    d_kv, m_kv, v_kv = adamw(w_mem_kv[0], gw_kv, m_w_mem_kv[0], v_w_mem_kv[0], 128, "adamw_w_kv")
    d_out, m_out, v_out = adamw(w_out[0], gw_out, m_w_out[0], v_w_out[0], 256, "adamw_w_out")

    def small_outs(t):
        return t[0:1], t[3:4, :FOX_HEADS], t[1:2], t[2]

    grads = (g_norm, gw_in, g_b, g_mem_norm, gw_kv[None], gw_out[None], g_final)
    outs = []
    for t, big in ((d_s, (d_in, d_kv, d_out)), (m_s, (m_in, m_kv, m_out)), (v_s, (v_in, v_kv, v_out))):
        n, b, mn, f = small_outs(t)
        outs += [n, big[0], b, mn, big[1][None], big[2][None], f]
    return (loss, grad_x, *grads, *outs)
```

```python
import math

import numpy as np
import jax
import jax.numpy as jnp
from jax import lax
from jax.experimental import pallas as pl
from jax.experimental.pallas import tpu as pltpu

F32 = jnp.float32
BF16 = jnp.bfloat16

D_MODEL = 1024
HEAD_DIM = 64
FOX_HEADS = 12
DIL_HEADS = 12
MEM_HEADS = 4
MEM_HEAD_DIM = 128
MEM_LEN = 256
FOX_W = FOX_HEADS * HEAD_DIM
DIL_W = DIL_HEADS * HEAD_DIM
MEM_W = MEM_HEADS * MEM_HEAD_DIM
MIX_W = FOX_W + DIL_W + MEM_W
DILATIONS = ((128, 1), (512, 4), (2048, 16))
ROPE_THETA = 500000.0
ROPE_DIM = HEAD_DIM // 4
RMS_EPS = 1e-6
NEG_INF = -1e30
IN_W = 4 * FOX_W + FOX_HEADS + 4 * DIL_W + 2 * MEM_W

ADAM_LR = 0.001
ADAM_B1 = 0.9
ADAM_B2 = 0.999
ADAM_EPS = 1e-08
ADAM_WD = 0.01
ADAM_STEP = 10

N_DEV = 8
LANES = 128
PAIR_W = 3 * LANES
TQ = 256
TK = 256

O_FQ, O_FK, O_FV, O_FG = 0, FOX_W, 2 * FOX_W, 3 * FOX_W
O_FLOG = 4 * FOX_W
O_DQ = O_FLOG + FOX_HEADS
O_DK, O_DV, O_DG = O_DQ + DIL_W, O_DQ + 2 * DIL_W, O_DQ + 3 * DIL_W
O_MQ = O_DQ + 4 * DIL_W
O_MG = O_MQ + MEM_W
P_FOX = 0
P_FG = P_FOX + 3 * FOX_W
P_DIL = P_FG + FOX_W
P_DG = P_DIL + 3 * DIL_W
P_MQ = P_DG + DIL_W
P_MG = P_MQ + MEM_W
P_FLOG = P_MG + MEM_W
PW = P_FLOG + LANES
A_FOX = 0
A_FLOG = A_FOX + 3 * FOX_W
PA = A_FLOG + LANES
B_FG = 0
B_DG = B_FG + FOX_W
B_DIL = B_DG + DIL_W
B_MQ = B_DIL + 3 * DIL_W
B_MG = -(-(B_MQ + MEM_W) // MEM_W) * MEM_W
PB = B_MG + MEM_W

VMEM_LIMIT = 56 * 1024 * 1024


def _pack_pieces():
    pieces = []
    for base in (O_FQ, O_DQ):
        seg = []
        for hp in range(FOX_HEADS // 2):
            for part in range(3):
                seg.append((base + part * FOX_W + hp * LANES, LANES))
        pieces.append(seg)
    fox, dil = pieces
    return fox + [(O_FG, FOX_W)] + dil + [(O_DG, DIL_W), (O_MQ, MEM_W), (O_MG, MEM_W), (O_FLOG, FOX_HEADS)]


def _pack_cols(w):
    parts = [w[..., s:s + n] for s, n in _pack_pieces()]
    parts.append(jnp.zeros(w.shape[:-1] + (LANES - FOX_HEADS,), w.dtype))
    return jnp.concatenate(parts, axis=-1)


def _split_cols(wp):
    def cut(start, width):
        return wp[..., start:start + width]

    group_a = jnp.concatenate([cut(P_FOX, 3 * FOX_W), cut(P_FLOG, LANES)], axis=-1)
    pad = jnp.zeros(wp.shape[:-1] + (B_MG - B_MQ - MEM_W,), wp.dtype)
    group_b = jnp.concatenate([cut(P_FG, FOX_W), cut(P_DG, DIL_W), cut(P_DIL, 3 * DIL_W), cut(P_MQ, MEM_W), pad,
                               cut(P_MG, MEM_W)], axis=-1)
    return group_a, group_b


def _params(sem=None, **kw):
    return pltpu.CompilerParams(dimension_semantics=sem, vmem_limit_bytes=VMEM_LIMIT, **kw)


def _mesh_pos():
    return lax.axis_index("x"), lax.axis_index("y"), lax.axis_index("c")


def _flip(v, d):
    return 1 - v if d else v


_RELATIONS = [(dx, dy, dc) for dx in (0, 1) for dy in (0, 1) for dc in (0, 1)][1:]
_SIBLING_AND_SAME_CORES = [(0, 0, 1), (1, 0, 0), (0, 1, 0), (1, 1, 0)]


_OTHER_CHIPS = [(1, 0), (0, 1), (1, 1)]


def small_all_reduce(small):
    vmem_spec = pl.BlockSpec(memory_space=pltpu.VMEM)

    def body(small_ref, tot_ref, land, send_sems, recv_sems):
        x, y, c = _mesh_pos()
        me = 4 * x + 2 * y + c
        land[me] = small_ref[...]
        sends, recvs = [], []
        for j, (dx, dy, dc) in enumerate(_RELATIONS):
            px, py, pc = _flip(x, dx), _flip(y, dy), _flip(c, dc)
            common = dict(send_sem=send_sems.at[j], recv_sem=recv_sems.at[j],
                          device_id=(px, py, pc), device_id_type=pl.DeviceIdType.MESH)
            sends.append(pltpu.make_async_remote_copy(src_ref=small_ref, dst_ref=land.at[me], **common))
            recvs.append(pltpu.make_async_remote_copy(src_ref=small_ref, dst_ref=land.at[4 * px + 2 * py + pc], **common))
        for cp in sends:
            cp.start()
        for cp in recvs:
            cp.wait_recv()
        for cp in sends:
            cp.wait_send()
        tot = land[0]
        for d in range(1, N_DEV):
            tot = tot + land[d]
        tot_ref[...] = tot

    return pl.pallas_call(
        body, name="small_sum", out_shape=jax.ShapeDtypeStruct(small.shape, small.dtype),
        in_specs=[vmem_spec], out_specs=vmem_spec,
        scratch_shapes=[pltpu.VMEM((N_DEV,) + small.shape, small.dtype),
                        pltpu.SemaphoreType.DMA((len(_RELATIONS),)), pltpu.SemaphoreType.DMA((len(_RELATIONS),))],
    )(small)


_HBM = pl.BlockSpec(memory_space=pltpu.HBM)
_SEM = pl.BlockSpec(memory_space=pltpu.SEMAPHORE)
_EFFECT = pltpu.SideEffectType.DATAFLOW_SIDE_EFFECTING


def _early_copies(src_refs, land_refs, send_sems, recv_sems, rows, gather, relations):
    x, y, c = _mesh_pos()
    me = 4 * x + 2 * y + c
    copies = []
    for a in range(len(src_refs)):
        for dx, dy, dc in relations:
            px, py, pc = _flip(x, dx), _flip(y, dy), _flip(c, dc)
            peer = 4 * px + 2 * py + pc
            copies.append(pltpu.make_async_remote_copy(
                src_ref=src_refs[a] if gather else src_refs[a].at[pl.ds(peer * rows[a], rows[a]), :],
                dst_ref=land_refs[a].at[pl.ds(me * rows[a], rows[a]), :],
                send_sem=send_sems[a], recv_sem=recv_sems[a],
                device_id=(px, py, pc), device_id_type=pl.DeviceIdType.MESH))
    return copies


def own_slots(shards, name):
    n = len(shards)
    x, y, c = _mesh_pos()
    me = (4 * x + 2 * y + c).astype(jnp.int32).reshape(1)
    empties = [lax.empty((N_DEV * s.shape[0], s.shape[1]), s.dtype) for s in shards]

    def body(me_ref, *refs):
        for a in range(n):
            refs[2 * n + a][...] = refs[a][...]

    return pl.pallas_call(
        body, name=name,
        grid_spec=pltpu.PrefetchScalarGridSpec(
            num_scalar_prefetch=1, grid=(1,),
            in_specs=[pl.BlockSpec(s.shape, lambda i, w: (0, 0)) for s in shards]
            + [pl.BlockSpec(memory_space=pl.ANY)] * n,
            out_specs=[pl.BlockSpec(s.shape, lambda i, w: (w[0], 0)) for s in shards]),
        out_shape=[jax.ShapeDtypeStruct(e.shape, e.dtype) for e in empties],
        input_output_aliases={1 + n + a: a for a in range(n)},
        compiler_params=_params(("arbitrary",)),
    )(me, *shards, *empties)


def early_exchange_start(srcs, name, gather=False, after=None, relations=_RELATIONS):
    n = len(srcs)
    if gather:
        rows = [s.shape[0] for s in srcs]
        lands = list(own_slots(srcs, name + "_place"))
    else:
        rows = [s.shape[0] // N_DEV for s in srcs]
        lands = [lax.empty(s.shape, s.dtype) for s in srcs]

    extra = [] if after is None else [after]

    def body(*refs):
        src_refs, land_refs = refs[:n], refs[n:2 * n]
        first_sem = 2 * n + len(extra)
        send_sems, recv_sems = refs[first_sem:first_sem + n], refs[first_sem + n:first_sem + 2 * n]
        token = refs[-1]
        for cp in _early_copies(src_refs, land_refs, send_sems, recv_sems, rows, gather, relations):
            cp.start()
        token[...] = jnp.zeros_like(token)

    hbm = lambda a: pltpu.HBM(a.shape, a.dtype)
    outs = pl.pallas_call(
        body, name=name,
        out_shape=[pltpu.SemaphoreType.DMA(())] * (2 * n)
        + [hbm(a) for a in srcs] + [hbm(a) for a in lands] + [jax.ShapeDtypeStruct((8, LANES), F32)],
        in_specs=[_HBM] * (2 * n) + [pl.BlockSpec(memory_space=pl.ANY)] * len(extra),
        out_specs=[_SEM] * (2 * n) + [_HBM] * (2 * n) + [pl.BlockSpec(memory_space=pltpu.VMEM)],
        input_output_aliases={i: 2 * n + i for i in range(2 * n)},
        compiler_params=pltpu.CompilerParams(has_side_effects=_EFFECT),
    )(*[pltpu.with_memory_space_constraint(a, pltpu.HBM) for a in list(srcs) + lands], *extra)
    handle = dict(sems=outs[:2 * n], srcs=outs[2 * n:3 * n], lands=outs[3 * n:4 * n], rows=rows,
                  copies=len(relations))
    return handle, outs[-1]


def early_exchange_wait(handle, after, name):
    n = len(handle["srcs"])
    rows = handle["rows"]

    def body(*refs):
        src_refs, land_refs = refs[:n], refs[n:2 * n]
        send_sems, recv_sems = refs[2 * n:3 * n], refs[3 * n:4 * n]
        x, y, c = _mesh_pos()
        for a in range(n):
            span = pl.ds(0, handle["copies"] * rows[a])
            all_copies = pltpu.make_async_remote_copy(
                src_ref=land_refs[a].at[span, :], dst_ref=land_refs[a].at[span, :],
                send_sem=send_sems[a], recv_sem=recv_sems[a],
                device_id=(x, y, c), device_id_type=pl.DeviceIdType.MESH)
            all_copies.wait_send()
            all_copies.wait_recv()

    hbm = lambda a: pltpu.HBM(a.shape, a.dtype)
    ins = list(handle["srcs"]) + list(handle["lands"])
    outs = pl.pallas_call(
        body, name=name,
        out_shape=[hbm(a) for a in ins],
        in_specs=[_HBM] * (2 * n) + [_SEM] * (2 * n) + [pl.BlockSpec(memory_space=pl.ANY)],
        out_specs=[_HBM] * (2 * n),
        input_output_aliases={i: i for i in range(2 * n)},
        compiler_params=pltpu.CompilerParams(has_side_effects=_EFFECT),
    )(*ins, *handle["sems"], after)
    return outs[:n], outs[n:]


def pass_on_to_sibling(lands, rows, name):
    n = len(lands)

    def body(*refs):
        land_refs = refs[n:2 * n]
        send_sems, recv_sems = refs[2 * n:]
        x, y, c = _mesh_pos()
        copies = []
        for a in range(n):
            for k, (dx, dy) in enumerate(_OTHER_CHIPS):
                slot = 4 * _flip(x, dx) + 2 * _flip(y, dy) + c
                blk = land_refs[a].at[pl.ds(slot * rows[a], rows[a]), :]
                copies.append(pltpu.make_async_remote_copy(
                    src_ref=blk, dst_ref=blk, send_sem=send_sems.at[a, k], recv_sem=recv_sems.at[a, k],
                    device_id=(x, y, 1 - c), device_id_type=pl.DeviceIdType.MESH))
        for cp in copies:
            cp.start()
        for cp in copies:
            cp.wait_recv()
        for cp in copies:
            cp.wait_send()

    any_spec = pl.BlockSpec(memory_space=pl.ANY)
    return pl.pallas_call(
        body, name=name,
        out_shape=[jax.ShapeDtypeStruct(a.shape, a.dtype) for a in lands],
        in_specs=[any_spec] * n, out_specs=[any_spec] * n,
        input_output_aliases={i: i for i in range(n)},
        scratch_shapes=[pltpu.SemaphoreType.DMA((n, len(_OTHER_CHIPS))), pltpu.SemaphoreType.DMA((n, len(_OTHER_CHIPS)))],
    )(*lands)


def slot_sum8(src, land, tr, name):
    rows, cols = land.shape[0] // N_DEV, land.shape[1]
    x, y, c = _mesh_pos()
    me = (4 * x + 2 * y + c).astype(jnp.int32).reshape(1)

    def body(me_ref, src_ref, land_ref, o_ref):
        acc = None
        for d in range(N_DEV):
            term = jnp.where(d == me_ref[0], src_ref[0], land_ref[d]).astype(F32)
            acc = term if acc is None else acc + term
        o_ref[...] = acc

    return pl.pallas_call(
        body, name=name,
        grid_spec=pltpu.PrefetchScalarGridSpec(
            num_scalar_prefetch=1, grid=(rows // tr,),
            in_specs=[pl.BlockSpec((1, tr, cols), lambda i, w: (w[0], i, 0)),
                      pl.BlockSpec((N_DEV, tr, cols), lambda i, w: (0, i, 0))],
            out_specs=pl.BlockSpec((tr, cols), lambda i, w: (i, 0))),
        out_shape=jax.ShapeDtypeStruct((rows, cols), F32),
        compiler_params=_params(("arbitrary",)),
    )(me, src.reshape(N_DEV, rows, cols), land.reshape(N_DEV, rows, cols))


def w_in_grad_sum(groups, name):
    rows = groups[0][0].shape[0] // N_DEV
    runs, pos = [], 0
    for start, width in _pack_pieces():
        runs.append((start, width, pos))
        pos += width
    n = len(groups)

    def body(*refs):
        src_refs, land_refs = refs[0:2 * n:2], refs[1:2 * n:2]
        g_ref = refs[2 * n]
        own_bufs, land_bufs = refs[2 * n + 1:3 * n + 1], refs[3 * n + 1:4 * n + 1]
        stage, load_sems, store_sems = refs[4 * n + 1:]
        x, y, c = _mesh_pos()
        me = 4 * x + 2 * y + c
        loads = []
        for k in range(n):
            pair = [pltpu.make_async_copy(src_refs[k].at[pl.ds(me * rows, rows), :], own_bufs[k], load_sems.at[k, 0]),
                    pltpu.make_async_copy(land_refs[k], land_bufs[k], load_sems.at[k, 1])]
            for cp in pair:
                cp.start()
            loads.append(pair)
        for k, (_, _, segments) in enumerate(groups):
            for cp in loads[k]:
                cp.wait()
            for first, packed, width in segments:
                for off in range(0, width, LANES):
                    cols = slice(first + off, first + off + LANES)
                    acc = None
                    for d in range(N_DEV):
                        term = jnp.where(d == me, own_bufs[k][:, cols], land_bufs[k][d * rows:(d + 1) * rows, cols])
                        acc = term.astype(F32) if acc is None else acc + term.astype(F32)
                    stage[packed + off:packed + off + LANES, :] = acc.T
        g_rows = g_ref.reshape(IN_W, LANES)
        stores = [pltpu.make_async_copy(stage.at[pl.ds(p, width), :], g_rows.at[pl.ds(start, width), :], store_sems.at[r])
                  for r, (start, width, p) in enumerate(runs)]
        for cp in stores:
            cp.start()
        for cp in stores:
            cp.wait()

    any_spec = pl.BlockSpec(memory_space=pl.ANY)
    operands = [a for src, land, _ in groups for a in (src, land)]
    return pl.pallas_call(
        body, name=name,
        in_specs=[any_spec] * (2 * n), out_specs=any_spec,
        out_shape=jax.ShapeDtypeStruct((IN_W, 1, LANES), F32),
        scratch_shapes=[pltpu.VMEM((rows, src.shape[1]), src.dtype) for src, _, _ in groups]
        + [pltpu.VMEM(land.shape, land.dtype) for _, land, _ in groups]
        + [pltpu.VMEM((PW, LANES), F32), pltpu.SemaphoreType.DMA((n, 2)), pltpu.SemaphoreType.DMA((len(runs),))],
        compiler_params=_params(),
    )(*operands)


def mm_tn_multi(a_t, bs, tt, name, out_dtype=F32):
    K, T = a_t.shape
    widths = [b.shape[1] for b in bs]
    steps = T // tt

    def body(a_ref, *rest):
        b_refs, o_ref, acc = rest[:-2], rest[-2], rest[-1]

        @pl.when(pl.program_id(0) == 0)
        def _():
            acc[...] = jnp.zeros(acc.shape, F32)

        av = a_ref[...]
        col = 0
        for b_ref, w in zip(b_refs, widths):
            acc[:, col:col + w] += jnp.dot(av, b_ref[...], preferred_element_type=F32)
            col += w

        @pl.when(pl.program_id(0) == steps - 1)
        def _():
            o_ref[...] = acc[...].astype(out_dtype)

    return pl.pallas_call(
        body, name=name, grid=(steps,),
        in_specs=[pl.BlockSpec((K, tt), lambda t: (0, t))] + [pl.BlockSpec((tt, w), lambda t: (t, 0)) for w in widths],
        out_specs=pl.BlockSpec((K, sum(widths)), lambda t: (0, 0)),
        out_shape=jax.ShapeDtypeStruct((K, sum(widths)), out_dtype),
        scratch_shapes=[pltpu.VMEM((K, sum(widths)), F32)],
        compiler_params=_params(("arbitrary",)),
    )(a_t, *bs)


def rms_fwd(x, g, tm, name, with_transpose=False, token=None):
    M, K = x.shape
    extra = [] if token is None else [token]

    def body(x_ref, g_ref, *rest):
        o_ref = rest[len(extra)]
        xv = x_ref[...]
        r = lax.rsqrt(jnp.mean(xv * xv, axis=-1, keepdims=True) + RMS_EPS)
        h = ((xv * r) * g_ref[...]).astype(BF16)
        o_ref[...] = h
        if with_transpose:
            rest[len(extra) + 1][...] = h.T

    out_specs = [pl.BlockSpec((tm, K), lambda i: (i, 0))]
    out_shape = [jax.ShapeDtypeStruct((M, K), BF16)]
    if with_transpose:
        out_specs.append(pl.BlockSpec((K, tm), lambda i: (0, i)))
        out_shape.append(jax.ShapeDtypeStruct((K, M), BF16))
    outs = pl.pallas_call(
        body, name=name, grid=(M // tm,),
        in_specs=[pl.BlockSpec((tm, K), lambda i: (i, 0)), pl.BlockSpec((1, K), lambda i: (0, 0))]
        + [pl.BlockSpec(t.shape, lambda i: (0, 0)) for t in extra],
        out_specs=out_specs, out_shape=out_shape,
        compiler_params=_params(("arbitrary",)),
    )(x, g, *extra)
    return outs if with_transpose else outs[0]


def rms_bwd(x, g, dh, dres, tm, name):
    M, K = x.shape
    has_res = dres is not None

    def body(*refs):
        if has_res:
            x_ref, g_ref, dh_ref, dres_ref, dx_ref, dg_ref = refs
        else:
            x_ref, g_ref, dh_ref, dx_ref, dg_ref = refs
        xv = x_ref[...]
        r = lax.rsqrt(jnp.mean(xv * xv, axis=-1, keepdims=True) + RMS_EPS)
        xn = xv * r
        dhv = dh_ref[...]
        dxn = dhv * g_ref[...]
        dx = r * (dxn - xn * jnp.mean(dxn * xn, axis=-1, keepdims=True))
        if has_res:
            dx = dx + dres_ref[...]
        dx_ref[...] = dx
        part = jnp.sum(dhv * xn, axis=0, keepdims=True)
        row = lax.broadcasted_iota(jnp.int32, (8, K), 0)
        upd = jnp.where(row == 0, part, 0.0)

        @pl.when(pl.program_id(0) == 0)
        def _():
            dg_ref[...] = upd

        @pl.when(pl.program_id(0) != 0)
        def _():
            dg_ref[...] += upd

    row_spec = pl.BlockSpec((tm, K), lambda i: (i, 0))
    ins = [x, g, dh] + ([dres] if has_res else [])
    in_specs = [row_spec, pl.BlockSpec((1, K), lambda i: (0, 0)), row_spec] + ([row_spec] if has_res else [])
    return pl.pallas_call(
        body, name=name, grid=(M // tm,),
        in_specs=in_specs,
        out_specs=[row_spec, pl.BlockSpec((8, K), lambda i: (0, 0))],
        out_shape=[jax.ShapeDtypeStruct((M, K), F32), jax.ShapeDtypeStruct((8, K), F32)],
        compiler_params=_params(("arbitrary",)),
    )(*ins)


def mm_nn(a, b, tm, tn, name, token=None):
    M, K = a.shape
    N = b.shape[1]
    extra = [] if token is None else [token]

    def body(a_ref, b_ref, *rest):
        rest[-1][...] = jnp.dot(a_ref[...], b_ref[...], preferred_element_type=F32)

    return pl.pallas_call(
        body, name=name, grid=(N // tn, M // tm),
        in_specs=[pl.BlockSpec((tm, K), lambda j, i: (i, 0)), pl.BlockSpec((K, tn), lambda j, i: (0, j))]
        + [pl.BlockSpec(t.shape, lambda j, i: (0, 0)) for t in extra],
        out_specs=pl.BlockSpec((tm, tn), lambda j, i: (i, j)),
        out_shape=jax.ShapeDtypeStruct((M, N), F32),
        compiler_params=_params(("arbitrary", "arbitrary")),
    )(a, b, *extra)


def mm_nt(a, b, tm, tk, name):
    M, K = a.shape
    N = b.shape[0]

    def body(a_ref, b_ref, o_ref):
        part = lax.dot_general(a_ref[...], b_ref[...], (((1,), (1,)), ((), ())), preferred_element_type=F32)

        @pl.when(pl.program_id(1) == 0)
        def _():
            o_ref[...] = part

        @pl.when(pl.program_id(1) != 0)
        def _():
            o_ref[...] += part

    return pl.pallas_call(
        body, name=name, grid=(M // tm, K // tk),
        in_specs=[pl.BlockSpec((tm, tk), lambda i, k: (i, k)), pl.BlockSpec((N, tk), lambda i, k: (0, k))],
        out_specs=pl.BlockSpec((tm, N), lambda i, k: (i, 0)),
        out_shape=jax.ShapeDtypeStruct((M, N), F32),
        compiler_params=_params(("arbitrary", "arbitrary")),
    )(a, b)


def in_proj_bwd_rms(pieces, ws, x, g, dres, tm, token):
    M, N = x.shape

    def body(*refs):
        n = len(pieces)
        p_refs, w_refs = refs[:n], refs[n:n + len(ws)]
        x_ref, g_ref, dres_ref, _, dx_ref, dg_ref = refs[n + len(ws):]
        dh = None
        for p_ref, (arr, group, col) in zip(p_refs, pieces):
            part = lax.dot_general(p_ref[...], w_refs[group][:, col:col + arr.shape[1]], (((1,), (1,)), ((), ())),
                                   preferred_element_type=F32)
            dh = part if dh is None else dh + part
        xv = x_ref[...]
        r = lax.rsqrt(jnp.mean(xv * xv, axis=-1, keepdims=True) + RMS_EPS)
        xn = xv * r
        dxn = dh * g_ref[...]
        dx_ref[...] = r * (dxn - xn * jnp.mean(dxn * xn, axis=-1, keepdims=True)) + dres_ref[...]
        row = lax.broadcasted_iota(jnp.int32, (8, N), 0)
        upd = jnp.where(row == 0, jnp.sum(dh * xn, axis=0, keepdims=True), 0.0)

        @pl.when(pl.program_id(0) == 0)
        def _():
            dg_ref[...] = upd

        @pl.when(pl.program_id(0) != 0)
        def _():
            dg_ref[...] += upd

    row_spec = pl.BlockSpec((tm, N), lambda i: (i, 0))
    return pl.pallas_call(
        body, name="in_proj_bwd", grid=(M // tm,),
        in_specs=[pl.BlockSpec((tm, arr.shape[1]), lambda i: (i, 0)) for arr, _, _ in pieces]
        + [pl.BlockSpec(w.shape, lambda i: (0, 0)) for w in ws]
        + [row_spec, pl.BlockSpec((1, N), lambda i: (0, 0)), row_spec, pl.BlockSpec(token.shape, lambda i: (0, 0))],
        out_specs=[row_spec, pl.BlockSpec((8, N), lambda i: (0, 0))],
        out_shape=[jax.ShapeDtypeStruct((M, N), F32), jax.ShapeDtypeStruct((8, N), F32)],
        compiler_params=_params(("arbitrary",)),
    )(*[arr for arr, _, _ in pieces], *ws, x, g, dres, token)


def _log_sigmoid(z):
    return jnp.minimum(z, 0.0) - jnp.log(1.0 + jnp.exp(-jnp.abs(z)))


def _tri(n, lower):
    r = lax.broadcasted_iota(jnp.int32, (n, n), 0)
    c = lax.broadcasted_iota(jnp.int32, (n, n), 1)
    return jnp.where((r >= c) if lower else (r <= c), 1.0, 0.0).astype(F32)


def fox_gate(proj3, b_pad):
    B, S, _ = proj3.shape
    nblk = S // TK

    def body(f_ref, b_ref, o_ref):
        tri = _tri(TK, True)
        carry = jnp.zeros((1, LANES), F32)
        for n in range(nblk):
            z = f_ref[0, n * TK:(n + 1) * TK, :] + b_ref[...]
            logf = _log_sigmoid(z)
            cs = jnp.dot(tri, logf, preferred_element_type=F32, precision=lax.Precision.HIGHEST) + carry
            carry = cs[TK - 1:TK, :]
            o_ref[0, n * TK:(n + 1) * TK, :] = -cs

    return pl.pallas_call(
        body, name="fox_gate", grid=(B,),
        in_specs=[pl.BlockSpec((1, S, LANES), lambda b: (b, 0, A_FLOG // LANES)),
                  pl.BlockSpec((1, LANES), lambda b: (0, 0))],
        out_specs=pl.BlockSpec((1, S, LANES), lambda b: (b, 0, 0)),
        out_shape=jax.ShapeDtypeStruct((B, S, LANES), F32),
        compiler_params=_params(("arbitrary",)),
    )(proj3, b_pad)


def fox_gate_bwd(drow, dneg, proj3, b_pad):
    B, S, _ = proj3.shape
    nblk = S // TK

    def body(d_ref, r_ref, f_ref, b_ref, o_ref, db_ref):
        tri = _tri(TK, False)
        lane = lax.broadcasted_iota(jnp.int32, (TK, LANES), 1)
        carry = jnp.zeros((1, LANES), F32)
        dbsum = jnp.zeros((1, LANES), F32)
        for n in reversed(range(nblk)):
            dk_side = None
            for hp in range(FOX_HEADS // 2):
                two = jnp.where(lane < 2, r_ref[0, n * TK:(n + 1) * TK, hp * LANES:(hp + 1) * LANES], 0.0)
                two = pltpu.roll(two, 2 * hp, 1) if hp else two
                dk_side = two if dk_side is None else dk_side + two
            dc = jnp.where(lane < FOX_HEADS, d_ref[0, :, n * TK:(n + 1) * TK].T - dk_side, 0.0)
            rs = jnp.dot(tri, dc, preferred_element_type=F32, precision=lax.Precision.HIGHEST) + carry
            carry = rs[0:1, :]
            z = f_ref[0, n * TK:(n + 1) * TK, :] + b_ref[...]
            dz = rs * (1.0 / (1.0 + jnp.exp(z)))
            o_ref[0, n * TK:(n + 1) * TK, :] = dz.astype(BF16)
            dbsum = dbsum + jnp.sum(dz, axis=0, keepdims=True)
        row = lax.broadcasted_iota(jnp.int32, (8, LANES), 0)
        upd = jnp.where(row == 0, dbsum, 0.0)

        @pl.when(pl.program_id(0) == 0)
        def _():
            db_ref[...] = upd

        @pl.when(pl.program_id(0) != 0)
        def _():
            db_ref[...] += upd

    return pl.pallas_call(
        body, name="fox_gate_bwd", grid=(B,),
        in_specs=[pl.BlockSpec((1, LANES, S), lambda b: (b, 0, 0)),
                  pl.BlockSpec((1, S, FOX_W), lambda b: (b, 0, 0)),
                  pl.BlockSpec((1, S, LANES), lambda b: (b, 0, A_FLOG // LANES)),
                  pl.BlockSpec((1, LANES), lambda b: (0, 0))],
        out_specs=[pl.BlockSpec((1, S, LANES), lambda b: (b, 0, 0)), pl.BlockSpec((8, LANES), lambda b: (0, 0))],
        out_shape=[jax.ShapeDtypeStruct((B, S, LANES), BF16), jax.ShapeDtypeStruct((8, LANES), F32)],
        compiler_params=_params(("arbitrary",)),
    )(drow, dneg, proj3, b_pad)


def _rope_tables(S):
    half = ROPE_DIM // 2
    f32 = np.float32
    pos = np.arange(S, dtype=f32)
    inv_freq = f32(1.0) / np.power(f32(ROPE_THETA), np.arange(0, ROPE_DIM, 2, dtype=f32) / f32(ROPE_DIM)).astype(f32)
    ang = (pos[:, None] * inv_freq[None, :]).astype(f32).astype(np.float64)
    cos, sin = np.cos(ang).astype(f32), np.sin(ang).astype(f32)
    one = np.ones((S, HEAD_DIM - ROPE_DIM), f32)
    zero = np.zeros((S, HEAD_DIM - ROPE_DIM), f32)
    zh = np.zeros((S, half), f32)
    c = np.concatenate([cos, cos, one], axis=1)
    s1 = np.concatenate([-sin, zh, zero], axis=1)
    s2 = np.concatenate([zh, sin, zero], axis=1)
    return tuple(jnp.asarray(np.concatenate([t, t], axis=1)) for t in (c, s1, s2))


_HALF_ROPE = ROPE_DIM // 2


def _rope(t, c, s1, s2):
    return t * c + pltpu.roll(t, LANES - _HALF_ROPE, 1) * s1 + pltpu.roll(t, _HALF_ROPE, 1) * s2


def _rope_bwd(d, c, s1, s2):
    return d * c + pltpu.roll(d * s1, _HALF_ROPE, 1) + pltpu.roll(d * s2, LANES - _HALF_ROPE, 1)


def _scale_parts(scale):
    m, _ = math.frexp(scale)
    return (scale, None) if m == 0.5 else (None, scale)


def _log_masks(S, kind):
    nd = 1 if kind == "causal" else S // TQ
    a = np.arange(TQ)[:, None]
    b = np.arange(TK)[None, :]
    out = np.zeros((nd, TQ, TK), np.float32)
    for d in range(nd):
        delta = d * TQ + a - b
        if kind == "causal":
            m = (delta >= 0).astype(np.float64)
        else:
            m = sum(((delta >= 0) & (delta % dil == 0) & (delta <= w)).astype(np.float64) for w, dil in DILATIONS)
        out[d] = np.where(m > 0, np.log(np.maximum(m, 1.0)), NEG_INF)
    return jnp.asarray(out)


def _attn_setup(kind):
    pair = kind != "mem"
    e_dim = HEAD_DIM if pair else MEM_HEAD_DIM
    q_fold, s_scale = _scale_parts(1.0 / math.sqrt(e_dim))
    return dict(pair=pair, col0={"fox": A_FOX, "dil": B_DIL, "mem": B_MQ}[kind],
                n_blocks=FOX_HEADS // 2 if pair else MEM_HEADS, q_fold=q_fold, s_scale=s_scale,
                nh=2 if pair else 1)


def _cat(parts, axis):
    return parts[0] if len(parts) == 1 else jnp.concatenate(parts, axis=axis)


def _log_masks_t(S, kind):
    return jnp.swapaxes(_log_masks(S, kind), 1, 2)


def _head_rows(hh, pair):
    row = lax.broadcasted_iota(jnp.int32, (LANES, 1), 0)
    if not pair:
        return row >= 0
    return (row >= HEAD_DIM * hh) & (row < HEAD_DIM * (hh + 1))


def _attn_t_inputs(kind, src, S, negc_cols, mask, rope, kv):
    cfg = _attn_setup(kind)
    col0 = cfg["col0"]
    ins, in_specs = [], []
    if cfg["pair"]:
        ins.append(src)
        in_specs.append(pl.BlockSpec((1, S, PAIR_W), lambda b, h: (b, 0, col0 // PAIR_W + h)))
    else:
        ins += [src, kv, kv]
        in_specs += [pl.BlockSpec((1, S, LANES), lambda b, h: (b, 0, col0 // LANES + h)),
                     pl.BlockSpec((1, MEM_LEN, LANES), lambda b, h: (b, 0, h)),
                     pl.BlockSpec((1, MEM_LEN, LANES), lambda b, h: (b, 0, MEM_HEADS + h))]
    if negc_cols is not None:
        ins.append(negc_cols)
        in_specs.append(pl.BlockSpec((1, S, LANES), lambda b, h: (b, 0, 0)))
    if mask is not None:
        ins.append(mask)
        in_specs.append(pl.BlockSpec(mask.shape, lambda b, h: (0, 0, 0)))
    if rope is not None:
        ins += list(rope)
        in_specs += [pl.BlockSpec((S, LANES), lambda b, h: (0, 0))] * 3
    return ins, in_specs


def _attn_t_prep(cfg, refs, S, Sk, *, qT2s, ks, vs=None, vTs=None, kTs=None, nb=None):
    pair, nh = cfg["pair"], cfg["nh"]
    lane = lax.broadcasted_iota(jnp.int32, (1, LANES), 1)
    rope_refs = refs["rope"]

    def prep_q(n):
        rows = slice(n * TQ, (n + 1) * TQ)
        q = refs["load_q"](rows)
        if rope_refs is not None:
            q = _rope(q, *[t[rows, :] for t in rope_refs])
        if cfg["q_fold"] is not None:
            q = q * cfg["q_fold"]
        qtb = q.astype(BF16).T
        for hh in range(nh):
            qT2s[n, :, hh * TQ:(hh + 1) * TQ] = jnp.where(_head_rows(hh, pair), qtb, jnp.zeros_like(qtb))

    def prep_kv(n):
        rows = slice(n * TK, (n + 1) * TK)
        k, v = refs["load_kv"](rows)
        if rope_refs is not None:
            k = _rope(k, *[t[rows, :] for t in rope_refs])
        kb = k.astype(BF16)
        vb = v.astype(BF16)
        ks[rows, :] = kb
        if vs is not None:
            vs[rows, :] = vb
        if vTs is not None:
            vTs[n] = vb.T
        if kTs is not None:
            kTs[n] = kb.T
        if nb is not None:
            blk = refs["negc"][0, rows, :]
            for hh in range(nh):
                h = 2 * refs["block"] + hh
                col = jnp.sum(jnp.where(lane == h, blk, 0.0), axis=1, keepdims=True)
                nb[hh, rows, :] = jnp.broadcast_to(col, (TK, LANES))

    for n in range(S // TQ):
        prep_q(n)
    for n in range(Sk // TK):
        prep_kv(n)


def _raw_scores_t(cfg, k, qT2):
    sT = jnp.dot(k, qT2, preferred_element_type=F32)
    if cfg["s_scale"] is not None:
        sT = sT * cfg["s_scale"]
    return sT


def _bias_mask_t(cfg, sT, nb, mask_ref, kc, midx):
    nh = cfg["nh"]
    if nb is None and midx is None:
        return sT
    parts = []
    for hh in range(nh):
        t = sT[:, hh * TQ:(hh + 1) * TQ]
        if nb is not None:
            t = t + jnp.concatenate([nb[hh, kc, :]] * (TQ // LANES), axis=1)
        if midx is not None:
            t = t + mask_ref[midx]
        parts.append(t)
    return _cat(parts, 1)


def _tile_pairs(kind, nq, nk):
    if kind == "mem":
        return [(i, j) for i in range(nq) for j in range(nk)], (lambda i, j: None)
    pairs = [(i, j) for i in range(nq) for j in range(i + 1)]
    if kind == "fox":
        return pairs, (lambda i, j: 0 if j == i else None)
    return pairs, (lambda i, j: i - j)


def attn_fwd(kind, src, S, *, negc_cols=None, mask=None, rope=None, kv=None):
    B = src.shape[0]
    cfg = _attn_setup(kind)
    pair, nh = cfg["pair"], cfg["nh"]
    Sk = S if pair else MEM_LEN
    has_bias, has_rope = negc_cols is not None, rope is not None
    R = nh * TQ
    nq, nk = S // TQ, Sk // TK
    pairs, mask_index = _tile_pairs(kind, nq, nk)

    def body(*refs):
        refs = list(refs)
        if pair:
            qkv_ref = refs.pop(0)
            load_q = lambda rows: qkv_ref[0, rows, 0:LANES]
            load_kv = lambda rows: (qkv_ref[0, rows, LANES:2 * LANES], qkv_ref[0, rows, 2 * LANES:3 * LANES])
        else:
            q_ref, k_ref, v_ref = refs.pop(0), refs.pop(0), refs.pop(0)
            load_q = lambda rows: q_ref[0, rows, :]
            load_kv = lambda rows: (k_ref[0, rows, :], v_ref[0, rows, :])
        negc_ref = refs.pop(0) if has_bias else None
        mask_ref = refs.pop(0) if mask is not None else None
        rope_refs = [refs.pop(0) for _ in range(3)] if has_rope else None
        o_ref, lse_ref, qT2s, ks, vTs, s_a, s_b, p_a, p_b = refs[:9]
        nb = refs[9] if has_bias else None
        _attn_t_prep(cfg, dict(load_q=load_q, load_kv=load_kv, rope=rope_refs, negc=negc_ref,
                               block=pl.program_id(1)), S, Sk, qT2s=qT2s, ks=ks, vTs=vTs, nb=nb)

        def cols(j):
            return slice(j * TK, (j + 1) * TK)

        def scores(i, j):
            return _raw_scores_t(cfg, ks[cols(j), :], qT2s[i])

        def finish(i, m, l, accT):
            oT2 = accT / l
            oT = jnp.where(_head_rows(0, True), oT2[:, 0:TQ], oT2[:, TQ:2 * TQ]) if pair else oT2
            o_ref[0, i * TQ:(i + 1) * TQ, :] = oT.T
            lse_ref[0, 0, i:i + 1, :] = m + jnp.log(l)

        s_bufs, p_bufs = (s_a, s_b), (p_a, p_b)
        s_bufs[0][...] = scores(*pairs[0])
        m = l = accT = None
        for t, (i, j) in enumerate(pairs):
            cur, oth = t % 2, 1 - t % 2
            if t > 0:
                i_prev, j_prev = pairs[t - 1]
                pv = jnp.dot(vTs[j_prev], p_bufs[oth][...], preferred_element_type=F32)
                acc_full = pv if accT is None else accT + pv
            if t + 1 < len(pairs):
                s_bufs[oth][...] = scores(*pairs[t + 1])
            first = j == 0
            if first and t > 0:
                finish(i_prev, m, l, acc_full)
            sT = _bias_mask_t(cfg, s_bufs[cur][...], nb, mask_ref, cols(j), mask_index(i, j))
            m_tile = jnp.max(sT, axis=0, keepdims=True)
            m_new = m_tile if first else jnp.maximum(m, m_tile)
            p = jnp.exp(sT - m_new)
            p_bufs[cur][...] = p.astype(BF16)
            if first:
                l, accT = jnp.sum(p, axis=0, keepdims=True), None
            else:
                alpha = jnp.exp(m - m_new)
                l, accT = alpha * l + jnp.sum(p, axis=0, keepdims=True), acc_full * alpha
            m = m_new
        i_last, j_last = pairs[-1]
        pv = jnp.dot(vTs[j_last], p_bufs[(len(pairs) - 1) % 2][...], preferred_element_type=F32)
        finish(i_last, m, l, pv if accT is None else accT + pv)

    ins, in_specs = _attn_t_inputs(kind, src, S, negc_cols, mask, rope, kv)
    W = cfg["n_blocks"] * LANES
    scratch = [pltpu.VMEM((nq, LANES, R), BF16), pltpu.VMEM((Sk, LANES), BF16), pltpu.VMEM((nk, LANES, TK), BF16),
               pltpu.VMEM((TK, R), F32), pltpu.VMEM((TK, R), F32), pltpu.VMEM((TK, R), BF16), pltpu.VMEM((TK, R), BF16)]
    if has_bias:
        scratch.append(pltpu.VMEM((nh, Sk, LANES), F32))
    return pl.pallas_call(
        body, name=kind + "_attn_fwd", grid=(B, cfg["n_blocks"]),
        in_specs=in_specs,
        out_specs=[pl.BlockSpec((1, S, LANES), lambda b, h: (b, 0, h)),
                   pl.BlockSpec((1, 1, nq, R), lambda b, h: (b, h, 0, 0))],
        out_shape=[jax.ShapeDtypeStruct((B, S, W), F32), jax.ShapeDtypeStruct((B, cfg["n_blocks"], nq, R), F32)],
        scratch_shapes=scratch,
        compiler_params=_params(("arbitrary", "arbitrary")),
    )(*ins)


def attn_bwd(kind, src, do, o, lse, S, *, negc_cols=None, mask=None, rope=None, kv=None, token=None):
    B = src.shape[0]
    cfg = _attn_setup(kind)
    pair, nh, s_scale, q_fold = cfg["pair"], cfg["nh"], cfg["s_scale"], cfg["q_fold"]
    Sk = S if pair else MEM_LEN
    has_bias, has_rope = negc_cols is not None, rope is not None
    R = nh * TQ
    nq, nk = S // TQ, Sk // TK
    pairs, mask_index = _tile_pairs(kind, nq, nk)

    def body(*refs):
        refs = list(refs)
        if pair:
            qkv_ref = refs.pop(0)
            load_q = lambda rows: qkv_ref[0, rows, 0:LANES]
            load_kv = lambda rows: (qkv_ref[0, rows, LANES:2 * LANES], qkv_ref[0, rows, 2 * LANES:3 * LANES])
        else:
            q_ref, k_ref, v_ref = refs.pop(0), refs.pop(0), refs.pop(0)
            load_q = lambda rows: q_ref[0, rows, :]
            load_kv = lambda rows: (k_ref[0, rows, :], v_ref[0, rows, :])
        negc_ref = refs.pop(0) if has_bias else None
        mask_ref = refs.pop(0) if mask is not None else None
        rope_refs = [refs.pop(0) for _ in range(3)] if has_rope else None
        do_ref, o_ref, lse_ref = refs.pop(0), refs.pop(0), refs.pop(0)
        if token is not None:
            refs.pop(0)
        if pair:
            dqkv_ref = refs.pop(0)
            dneg_ref = refs.pop(0) if has_bias else None
            drow_ref = refs.pop(0) if has_bias else None
        else:
            dq_ref, dk_ref, dv_ref = refs.pop(0), refs.pop(0), refs.pop(0)
        qT2s, ks, vs, kTs, doT2s, delta_s, dk_acc, dv_acc = refs[:8]
        bufs_a, bufs_b = refs[8:12], refs[12:16]
        nb, dneg_acc = (refs[16], refs[17]) if has_bias else (None, None)
        lane = lax.broadcasted_iota(jnp.int32, (1, LANES), 1)
        _attn_t_prep(cfg, dict(load_q=load_q, load_kv=load_kv, rope=rope_refs, negc=negc_ref,
                               block=pl.program_id(1)), S, Sk,
                     qT2s=qT2s, ks=ks, vs=vs, kTs=kTs, nb=nb)

        def prep_do(n):
            rows = slice(n * TQ, (n + 1) * TQ)
            doT = do_ref[0, rows, :].astype(BF16).astype(F32).T
            prodT = doT * o_ref[0, rows, :].T
            doTb = doT.astype(BF16)
            for hh in range(nh):
                hm = _head_rows(hh, pair)
                doT2s[n, :, hh * TQ:(hh + 1) * TQ] = jnp.where(hm, doTb, jnp.zeros_like(doTb))
                delta_s[n:n + 1, hh * TQ:(hh + 1) * TQ] = jnp.sum(jnp.where(hm, prodT, 0.0), axis=0, keepdims=True)

        for n in range(nq):
            prep_do(n)
        dk_acc[...] = jnp.zeros(dk_acc.shape, F32)
        dv_acc[...] = jnp.zeros(dv_acc.shape, F32)
        if has_bias:
            dneg_acc[...] = jnp.zeros(dneg_acc.shape, F32)

        def cols(j):
            return slice(j * TK, (j + 1) * TK)

        nt_dims = (((1,), (1,)), ((), ()))

        def first_products(i, j, bufs):
            bufs[0][...] = _raw_scores_t(cfg, ks[cols(j), :], qT2s[i])
            bufs[1][...] = jnp.dot(vs[cols(j), :], doT2s[i], preferred_element_type=F32)

        def last_products(i, j, bufs, dqT2):
            dv_acc[j] += lax.dot_general(doT2s[i], bufs[2][...], nt_dims, preferred_element_type=F32)
            dk_acc[j] += lax.dot_general(qT2s[i], bufs[3][...], nt_dims, preferred_element_type=F32)
            dq = jnp.dot(kTs[j], bufs[3][...], preferred_element_type=F32)
            return dq if dqT2 is None else dqT2 + dq

        def finish_q(i, dqT2, drow):
            rows = slice(i * TQ, (i + 1) * TQ)
            dqT = jnp.where(_head_rows(0, True), dqT2[:, 0:TQ], dqT2[:, TQ:2 * TQ]) if pair else dqT2
            dq = dqT.T
            if q_fold is not None:
                dq = dq * q_fold
            if has_rope:
                dq = _rope_bwd(dq, *[t[rows, :] for t in rope_refs])
            if pair:
                dqkv_ref[0, rows, 0:LANES] = dq.astype(BF16)
            else:
                dq_ref[0, rows, :] = dq.astype(BF16)
            if has_bias:
                drow_ref[0, 0, i:i + 1, :] = drow

        bufs = (bufs_a, bufs_b)
        first_products(*pairs[0], bufs[0])
        dqT2 = drow = None
        for t, (i, j) in enumerate(pairs):
            cur, oth = bufs[t % 2], bufs[1 - t % 2]
            first = j == 0
            if first and t > 0:
                i_prev, j_prev = pairs[t - 1]
                finish_q(i_prev, last_products(i_prev, j_prev, oth, dqT2), drow)
                dqT2 = drow = None
            sT = _bias_mask_t(cfg, cur[0][...], nb, mask_ref, cols(j), mask_index(i, j))
            pT = jnp.exp(sT - lse_ref[0, 0, i:i + 1, :])
            dsT = pT * (cur[1][...] - delta_s[i:i + 1, :])
            if has_bias:
                tile_rows = jnp.sum(dsT, axis=0, keepdims=True)
                drow = tile_rows if drow is None else drow + tile_rows
                for hh in range(nh):
                    part = dsT[:, hh * TQ:hh * TQ + LANES]
                    for u in range(1, TQ // LANES):
                        part = part + dsT[:, hh * TQ + u * LANES:hh * TQ + (u + 1) * LANES]
                    dneg_acc[hh, cols(j), :] += part
            if s_scale is not None:
                dsT = dsT * s_scale
            cur[2][...] = pT.astype(BF16)
            cur[3][...] = dsT.astype(BF16)
            if not first:
                dqT2 = last_products(*pairs[t - 1], oth, dqT2)
            if t + 1 < len(pairs):
                first_products(*pairs[t + 1], oth)
        i_last, j_last = pairs[-1]
        finish_q(i_last, last_products(i_last, j_last, bufs[(len(pairs) - 1) % 2], dqT2), drow)

        for n in range(nk):
            rows = slice(n * TK, (n + 1) * TK)
            dk = dk_acc[n].T
            dv = dv_acc[n].T
            if has_rope:
                dk = _rope_bwd(dk, *[t[rows, :] for t in rope_refs])
            if pair:
                dqkv_ref[0, rows, LANES:2 * LANES] = dk.astype(BF16)
                dqkv_ref[0, rows, 2 * LANES:3 * LANES] = dv.astype(BF16)
            else:
                dk_ref[0, rows, :] = dk.astype(BF16)
                dv_ref[0, rows, :] = dv.astype(BF16)
            if has_bias:
                x0 = jnp.sum(dneg_acc[0, rows, :], axis=1, keepdims=True)
                x1 = jnp.sum(dneg_acc[1, rows, :], axis=1, keepdims=True)
                dneg_ref[0, rows, :] = jnp.where(lane == 0, x0, jnp.where(lane == 1, x1, 0.0))

    ins, in_specs = _attn_t_inputs(kind, src, S, negc_cols, mask, rope, kv)
    row_spec = pl.BlockSpec((1, S, LANES), lambda b, h: (b, 0, h))
    vec_spec = pl.BlockSpec((1, 1, nq, R), lambda b, h: (b, h, 0, 0))
    ins += [do, o, lse]
    in_specs += [row_spec, row_spec, vec_spec]
    if token is not None:
        ins.append(token)
        in_specs.append(pl.BlockSpec(token.shape, lambda b, h: (0, 0)))
    W = cfg["n_blocks"] * LANES
    if pair:
        out_specs = [pl.BlockSpec((1, S, PAIR_W), lambda b, h: (b, 0, h))]
        out_shape = [jax.ShapeDtypeStruct((B, S, 3 * W), BF16)]
        if has_bias:
            out_specs += [row_spec, vec_spec]
            out_shape += [jax.ShapeDtypeStruct((B, S, W), F32), jax.ShapeDtypeStruct((B, cfg["n_blocks"], nq, R), F32)]
    else:
        kv_spec = pl.BlockSpec((1, MEM_LEN, LANES), lambda b, h: (b, 0, h))
        out_specs = [row_spec, kv_spec, kv_spec]
        out_shape = [jax.ShapeDtypeStruct((B, S, W), BF16)] + [jax.ShapeDtypeStruct((B, MEM_LEN, W), BF16)] * 2
    scratch = [pltpu.VMEM((nq, LANES, R), BF16), pltpu.VMEM((Sk, LANES), BF16),
               pltpu.VMEM((Sk, LANES), BF16), pltpu.VMEM((nk, LANES, TK), BF16), pltpu.VMEM((nq, LANES, R), BF16),
               pltpu.VMEM((nq, R), F32), pltpu.VMEM((nk, LANES, TK), F32), pltpu.VMEM((nk, LANES, TK), F32)]
    pair_bufs = [pltpu.VMEM((TK, R), F32), pltpu.VMEM((TK, R), F32), pltpu.VMEM((TK, R), BF16), pltpu.VMEM((TK, R), BF16)]
    scratch += pair_bufs + pair_bufs
    if has_bias:
        scratch += [pltpu.VMEM((nh, Sk, LANES), F32), pltpu.VMEM((nh, Sk, LANES), F32)]
    return pl.pallas_call(
        body, name=kind + "_attn_bwd", grid=(B, cfg["n_blocks"]),
        in_specs=in_specs, out_specs=out_specs, out_shape=out_shape, scratch_shapes=scratch,
        compiler_params=_params(("arbitrary", "arbitrary")),
    )(*ins)


def _sigmoid(g):
    return 1.0 / (1.0 + jnp.exp(-g))


def out_step(proj, o_fox, o_dil, o_mem, w_out, x, target, gf, tm):
    T = x.shape[0]

    def body(fg_ref, dg_ref, mg_ref, of_ref, od_ref, om_ref, w_ref, x_ref, t_ref, gf_ref,
             dx_ref, dof_ref, dod_ref, dom_ref, dfg_ref, ddg_ref, dmg_ref, gw_ref, sm_ref, gw_acc):
        branches = []
        for g_ref, o_ref in ((fg_ref, of_ref), (dg_ref, od_ref), (mg_ref, om_ref)):
            g = g_ref[...]
            sg = _sigmoid(g)
            o = o_ref[...]
            branches.append((g, sg, o))
        ymix = jnp.concatenate([(o * (g * sg)).astype(BF16) for g, sg, o in branches], axis=1)
        x2 = x_ref[...] + jnp.dot(ymix, w_ref[...], preferred_element_type=F32)
        r = lax.rsqrt(jnp.mean(x2 * x2, axis=-1, keepdims=True) + RMS_EPS)
        yn = x2 * r
        err = yn * gf_ref[...] - t_ref[...]
        loss = 0.5 * jnp.sum(jnp.sum(err * err, axis=-1, keepdims=True) / D_MODEL, axis=0, keepdims=True)
        dyf = err / D_MODEL
        dgf = jnp.sum(dyf * yn, axis=0, keepdims=True)
        dyn = dyf * gf_ref[...]
        dx2 = r * (dyn - yn * jnp.mean(dyn * yn, axis=-1, keepdims=True))
        dx_ref[...] = dx2
        dxb = dx2.astype(BF16)
        dmix = lax.dot_general(dxb, w_ref[...], (((1,), (1,)), ((), ())), preferred_element_type=F32)
        col = 0
        for (g, sg, o), do_ref, dgate_ref in zip(branches, (dof_ref, dod_ref, dom_ref), (dfg_ref, ddg_ref, dmg_ref)):
            d = dmix[:, col:col + g.shape[1]]
            col += g.shape[1]
            do_ref[...] = (d * (g * sg)).astype(BF16)
            dgate_ref[...] = (d * o * (sg * (1.0 + g * (1.0 - sg)))).astype(BF16)
        row = lax.broadcasted_iota(jnp.int32, (8, D_MODEL), 0)
        upd = jnp.where(row == 0, dgf, jnp.where(row == 1, loss, 0.0))

        @pl.when(pl.program_id(0) == 0)
        def _():
            sm_ref[...] = jnp.zeros(sm_ref.shape, F32)
            gw_acc[...] = jnp.zeros(gw_acc.shape, F32)

        sm_ref[...] += upd
        gw_acc[...] += lax.dot_general(ymix, dxb, (((0,), (0,)), ((), ())), preferred_element_type=F32)

        @pl.when(pl.program_id(0) == T // tm - 1)
        def _():
            gw_ref[...] = gw_acc[...].astype(BF16)

    def rows(w, col=0):
        return pl.BlockSpec((tm, w), lambda i: (i, col))

    return pl.pallas_call(
        body, name="out_step", grid=(T // tm,),
        in_specs=[rows(FOX_W, B_FG // FOX_W), rows(DIL_W, B_DG // DIL_W), rows(MEM_W, B_MG // MEM_W),
                  rows(FOX_W), rows(DIL_W), rows(MEM_W),
                  pl.BlockSpec((MIX_W, D_MODEL), lambda i: (0, 0)),
                  rows(D_MODEL), rows(D_MODEL), pl.BlockSpec((1, D_MODEL), lambda i: (0, 0))],
        out_specs=[rows(D_MODEL), rows(FOX_W), rows(DIL_W), rows(MEM_W), rows(FOX_W), rows(DIL_W), rows(MEM_W),
                   pl.BlockSpec((MIX_W, D_MODEL), lambda i: (0, 0)), pl.BlockSpec((8, D_MODEL), lambda i: (0, 0))],
        out_shape=[jax.ShapeDtypeStruct((T, D_MODEL), F32), jax.ShapeDtypeStruct((T, FOX_W), BF16),
                   jax.ShapeDtypeStruct((T, DIL_W), BF16), jax.ShapeDtypeStruct((T, MEM_W), BF16),
                   jax.ShapeDtypeStruct((T, FOX_W), BF16), jax.ShapeDtypeStruct((T, DIL_W), BF16),
                   jax.ShapeDtypeStruct((T, MEM_W), BF16), jax.ShapeDtypeStruct((MIX_W, D_MODEL), BF16),
                   jax.ShapeDtypeStruct((8, D_MODEL), F32)],
        scratch_shapes=[pltpu.VMEM((MIX_W, D_MODEL), F32)],
        compiler_params=_params(("arbitrary",)),
    )(proj, proj, proj, o_fox, o_dil, o_mem, w_out, x, target, gf)


def adamw(w, g, m, v, tr, name):
    lead = w.shape[:-2]
    R, C = w.shape[-2:]
    zeros = (0,) * len(lead)

    def body(w_ref, g_ref, m_ref, v_ref, d_ref, mo_ref, vo_ref):
        gv = g_ref[...]
        mn = ADAM_B1 * m_ref[...] + (1.0 - ADAM_B1) * gv
        vn = ADAM_B2 * v_ref[...] + (1.0 - ADAM_B2) * jnp.square(gv)
        m_hat = mn / (1.0 - ADAM_B1 ** ADAM_STEP)
        v_hat = vn / (1.0 - ADAM_B2 ** ADAM_STEP)
        d_ref[...] = -ADAM_LR * (m_hat / (jnp.sqrt(v_hat) + ADAM_EPS) + ADAM_WD * w_ref[...])
        mo_ref[...] = mn
        vo_ref[...] = vn

    spec = pl.BlockSpec((1,) * len(lead) + (tr, C), lambda i: zeros + (i, 0))
    return pl.pallas_call(
        body, name=name, grid=(pl.cdiv(R, tr),),
        in_specs=[spec] * 4, out_specs=[spec] * 3,
        out_shape=[jax.ShapeDtypeStruct(w.shape, F32)] * 3,
        compiler_params=_params(("arbitrary",)),
    )(w, g, m, v)


def adamw_columns_first(w, g, m, v, name):
    N = w.shape[0]
    chunk = 16
    main = N // chunk * chunk

    def body(w_hbm, g_hbm, m_hbm, v_hbm, d_hbm, mo_hbm, vo_hbm, wb, gb, mb, vb, db, mob, vob, sems):
        loads = [pltpu.make_async_copy(h.reshape(N, LANES), b, sems.at[k])
                 for k, (h, b) in enumerate(((w_hbm, wb), (g_hbm, gb), (m_hbm, mb), (v_hbm, vb)))]
        for cp in loads:
            cp.start()
        for cp in loads:
            cp.wait()

        def update(rows):
            gv = gb[rows, :]
            mn = ADAM_B1 * mb[rows, :] + (1.0 - ADAM_B1) * gv
            vn = ADAM_B2 * vb[rows, :] + (1.0 - ADAM_B2) * jnp.square(gv)
            m_hat = mn / (1.0 - ADAM_B1 ** ADAM_STEP)
            v_hat = vn / (1.0 - ADAM_B2 ** ADAM_STEP)
            db[rows, :] = -ADAM_LR * (m_hat / (jnp.sqrt(v_hat) + ADAM_EPS) + ADAM_WD * wb[rows, :])
            mob[rows, :] = mn
            vob[rows, :] = vn

        def step(i, _):
            update(pl.ds(pl.multiple_of(i * chunk, chunk), chunk))
            return 0

        lax.fori_loop(0, main // chunk, step, 0, unroll=4)
        if main < N:
            update(slice(main, N))
        stores = [pltpu.make_async_copy(b, h.reshape(N, LANES), sems.at[k])
                  for k, (h, b) in enumerate(((d_hbm, db), (mo_hbm, mob), (vo_hbm, vob)))]
        for cp in stores:
            cp.start()
        for cp in stores:
            cp.wait()

    any_spec = pl.BlockSpec(memory_space=pl.ANY)
    return pl.pallas_call(
        body, name=name,
        in_specs=[any_spec] * 4, out_specs=[any_spec] * 3,
        out_shape=[jax.ShapeDtypeStruct(w.shape, F32)] * 3,
        scratch_shapes=[pltpu.VMEM((N, LANES), F32)] * 7 + [pltpu.SemaphoreType.DMA((4,))],
        compiler_params=_params(),
    )(w, g, m, v)


def _pad_row(v, width):
    return jnp.concatenate([v, jnp.zeros((1, width - v.shape[1]), v.dtype)], axis=1)


def local_grads(x, mem, norm_g, b_forget, mem_norm_g, final_norm_g, loss_target, first_token, first_weights,
                late_weights, start_exchange):
    B, S, D = x.shape
    T = B * S
    xt = x.reshape(T, D)
    memt = mem.reshape(B * MEM_LEN, D)
    b_pad = _pad_row(b_forget, LANES)

    h, h_t = rms_fwd(xt, norm_g, 512, "rms_x", with_transpose=True, token=first_token)
    w_in_a, proj_token = first_weights(h)
    proj_a = mm_nn(h, w_in_a, 512, PA, "in_proj_a", proj_token)
    proj_a3 = proj_a.reshape(B, S, PA)

    negc = fox_gate(proj_a3, b_pad)
    causal = _log_masks_t(S, "causal")
    dilated = _log_masks_t(S, "dilated")
    rope = _rope_tables(S)

    o_fox, lse_fox = attn_fwd("fox", proj_a3, S, negc_cols=negc, mask=causal)

    w_in_b, w_kv, w_out = late_weights(o_fox)
    proj_b = mm_nn(h, w_in_b, 512, PB // 2, "in_proj_b")
    proj_b3 = proj_b.reshape(B, S, PB)
    o_dil, lse_dil = attn_fwd("dil", proj_b3, S, mask=dilated, rope=rope)

    mh, mh_t = rms_fwd(memt, mem_norm_g, B * MEM_LEN, "rms_mem", with_transpose=True)
    mkv = mm_nn(mh, w_kv, B * MEM_LEN, 2 * MEM_W, "mem_kv_proj")
    mkv3 = mkv.reshape(B, MEM_LEN, 2 * MEM_W)
    o_mem, lse_mem = attn_fwd("mem", proj_b3, S, kv=mkv3)

    dx2, do_fox, do_dil, do_mem, dfg, ddg, dmg, g_out, small_out = out_step(
        proj_b, o_fox.reshape(T, FOX_W), o_dil.reshape(T, DIL_W), o_mem.reshape(T, MEM_W), w_out,
        xt, loss_target.reshape(T, D), final_norm_g.reshape(1, D), 256)

    gates = [(dfg, 1, B_FG), (ddg, 1, B_DG), (dmg, 1, B_MG)]
    g_gates = mm_tn_multi(h_t, [piece[0] for piece in gates], 1024, "w_in_grad_gates", BF16)
    first, token = start_exchange([g_gates, g_out], "early_exchange_a")

    dqkv_fox, dneg, drow = attn_bwd("fox", proj_a3, do_fox.reshape(B, S, FOX_W), o_fox, lse_fox, S,
                                    negc_cols=negc, mask=causal, token=token)
    drow = drow.reshape(B, FOX_HEADS // 2, S // TQ, 2, TQ).transpose(0, 1, 3, 2, 4).reshape(B, FOX_HEADS, S)
    drow = jnp.pad(drow, ((0, 0), (0, LANES - FOX_HEADS), (0, 0)))
    dflog, db_part = fox_gate_bwd(drow, dneg, proj_a3, b_pad)
    fox = [(dqkv_fox.reshape(T, 3 * FOX_W), 0, A_FOX), (dflog.reshape(T, LANES), 0, A_FLOG)]
    g_fox = mm_tn_multi(h_t, [piece[0] for piece in fox], 1024, "w_in_grad_fox", BF16)
    second, token = start_exchange([g_fox], "early_exchange_b")

    (dqkv_dil,) = attn_bwd("dil", proj_b3, do_dil.reshape(B, S, DIL_W), o_dil, lse_dil, S, mask=dilated, rope=rope,
                           token=token)
    dil = [(dqkv_dil.reshape(T, 3 * DIL_W), 1, B_DIL)]
    g_dil = mm_tn_multi(h_t, [piece[0] for piece in dil], 1024, "w_in_grad_dil", BF16)
    third, token = start_exchange([g_dil], "early_exchange_c")

    dmq, dmk, dmv = attn_bwd("mem", proj_b3, do_mem.reshape(B, S, MEM_W), o_mem, lse_mem, S, kv=mkv3, token=token)
    mq = [(dmq.reshape(T, MEM_W), 1, B_MQ)]
    g_mq = mm_tn_multi(h_t, [piece[0] for piece in mq], 1024, "w_in_grad_mq", BF16)
    dmkv = jnp.concatenate([dmk, dmv], axis=2).reshape(B * MEM_LEN, 2 * MEM_W)
    g_kv = mm_tn_multi(mh_t, [dmkv], B * MEM_LEN, "w_kv_grad", BF16)
    fourth, token = start_exchange([g_mq, g_kv], "early_exchange_d")

    grad_x, dng = in_proj_bwd_rms(gates + fox + dil + mq, (w_in_a, w_in_b), xt, norm_g, dx2, 256, token)
    dmh = mm_nt(dmkv, w_kv, B * MEM_LEN, D, "mem_kv_bwd")
    _, dmng = rms_bwd(memt, mem_norm_g, dmh, None, B * MEM_LEN, "rms_mem_bwd")

    small = jnp.concatenate([dng[0:1], dmng[0:1], small_out[0:1], _pad_row(db_part[0:1], D), small_out[1:2],
                             jnp.zeros((3, D), F32)], axis=0)
    early = [(first, dqkv_fox), (second, dqkv_dil), (third, dmq), (fourth, grad_x)]
    return grad_x.reshape(B, S, D), early, small


def kernel(x, mem, norm_g, w_in, b_forget, mem_norm_g, w_mem_kv, w_out, final_norm_g, loss_target, m_norm_g, m_w_in, m_b_forget, m_mem_norm_g, m_w_mem_kv, m_w_out, m_final_norm_g, v_norm_g, v_w_in, v_b_forget, v_mem_norm_g, v_w_mem_kv, v_w_out, v_final_norm_g):
    D = D_MODEL
    shard_a, shard_b = _split_cols(_pack_cols(w_in).astype(BF16).reshape(w_in.shape[1], PW))
    gather_a, first_token = early_exchange_start([shard_a], "first_gather", gather=True,
                                                 relations=_SIBLING_AND_SAME_CORES)
    late = {}

    def first_weights(after):
        _, gathered = early_exchange_wait(gather_a, after, "first_gather_wait")
        (w_in_a,) = pass_on_to_sibling(gathered, gather_a["rows"], "first_gather_pass")
        late["gather"], token = early_exchange_start(
            [shard_b, w_mem_kv[0].astype(BF16), w_out[0].astype(BF16)], "late_gather", gather=True, after=w_in_a,
            relations=_SIBLING_AND_SAME_CORES)
        return w_in_a, token

    def late_weights(after):
        _, gathered = early_exchange_wait(late["gather"], after, "late_gather_wait")
        return pass_on_to_sibling(gathered, late["gather"]["rows"], "late_gather_pass")

    grad_x, early, small = local_grads(
        x, mem, norm_g, b_forget, mem_norm_g, final_norm_g, loss_target, first_token, first_weights, late_weights,
        early_exchange_start)

    (first, after_first), (second, after_second), (third, after_third), (fourth, after_fourth) = early
    (src_gates, src_out), (land_gates, land_out) = early_exchange_wait(first, after_first, "early_wait_a")
    (src_fox,), (land_fox,) = early_exchange_wait(second, after_second, "early_wait_b")
    (src_dil,), (land_dil,) = early_exchange_wait(third, after_third, "early_wait_c")
    (src_mq, src_kv), (land_mq, land_kv) = early_exchange_wait(fourth, after_fourth, "early_wait_d")
    gw_out = slot_sum8(src_out, land_out, 256, "sum_w_out")
    gw_kv = slot_sum8(src_kv, land_kv, 128, "sum_w_kv")
    gw_in_cols = w_in_grad_sum(
        [(src_fox, land_fox, [(0, P_FOX, 3 * FOX_W), (3 * FOX_W, P_FLOG, LANES)]),
         (src_gates, land_gates, [(0, P_FG, FOX_W), (FOX_W, P_DG, DIL_W), (FOX_W + DIL_W, P_MG, MEM_W)]),
         (src_dil, land_dil, [(0, P_DIL, 3 * DIL_W)]),
         (src_mq, land_mq, [(0, P_MQ, MEM_W)])], "sum_w_in")
    gw_in = jnp.transpose(gw_in_cols, (1, 2, 0))

    tot = small_all_reduce(small)

    loss = tot[4, 0]
    g_norm, g_mem_norm, g_final, g_b = tot[0:1], tot[1:2], tot[2], tot[3:4, :FOX_HEADS]

    def rows8(*rows):
        rows = [r.reshape(1, -1) for r in rows]
        rows = [_pad_row(r, D) for r in rows]
        return jnp.concatenate(rows + [jnp.zeros((8 - len(rows), D), F32)], axis=0)

    sw = rows8(norm_g, mem_norm_g, final_norm_g, b_forget)
    sm = rows8(m_norm_g, m_mem_norm_g, m_final_norm_g, m_b_forget)
    sv = rows8(v_norm_g, v_mem_norm_g, v_final_norm_g, v_b_forget)
    d_s, m_s, v_s = adamw(sw, tot, sm, sv, 8, "adamw_small")
    columns_first = lambda a: jnp.transpose(a, (2, 0, 1))
    d_in, m_in, v_in = [jnp.transpose(o, (1, 2, 0)) for o in adamw_columns_first(
        columns_first(w_in), gw_in_cols, columns_first(m_w_in), columns_first(v_w_in), "adamw_w_in")]
    d_kv, m_kv, v_kv = adamw(w_mem_kv[0], gw_kv, m_w_mem_kv[0], v_w_mem_kv[0], 128, "adamw_w_kv")
    d_out, m_out, v_out = adamw(w_out[0], gw_out, m_w_out[0], v_w_out[0], 256, "adamw_w_out")

    def small_outs(t):
        return t[0:1], t[3:4, :FOX_HEADS], t[1:2], t[2]

    grads = (g_norm, gw_in, g_b, g_mem_norm, gw_kv[None], gw_out[None], g_final)
    outs = []
    for t, big in ((d_s, (d_in, d_kv, d_out)), (m_s, (m_in, m_kv, m_out)), (v_s, (v_in, v_kv, v_out))):
        n, b, mn, f = small_outs(t)
        outs += [n, big[0], b, mn, big[1][None], big[2][None], f]
    return (loss, grad_x, *grads, *outs)
```

```python
import math

import numpy as np
import jax
import jax.numpy as jnp
from jax import lax
from jax.experimental import pallas as pl
from jax.experimental.pallas import tpu as pltpu

F32 = jnp.float32
BF16 = jnp.bfloat16

D_MODEL = 1024
HEAD_DIM = 64
FOX_HEADS = 12
DIL_HEADS = 12
MEM_HEADS = 4
MEM_HEAD_DIM = 128
MEM_LEN = 256
FOX_W = FOX_HEADS * HEAD_DIM
DIL_W = DIL_HEADS * HEAD_DIM
MEM_W = MEM_HEADS * MEM_HEAD_DIM
MIX_W = FOX_W + DIL_W + MEM_W
DILATIONS = ((128, 1), (512, 4), (2048, 16))
ROPE_THETA = 500000.0
ROPE_DIM = HEAD_DIM // 4
RMS_EPS = 1e-6
NEG_INF = -1e30
IN_W = 4 * FOX_W + FOX_HEADS + 4 * DIL_W + 2 * MEM_W

ADAM_LR = 0.001
ADAM_B1 = 0.9
ADAM_B2 = 0.999
ADAM_EPS = 1e-08
ADAM_WD = 0.01
ADAM_STEP = 10

N_DEV = 8
LANES = 128
PAIR_W = 3 * LANES
TQ = 256
TK = 256

O_FQ, O_FK, O_FV, O_FG = 0, FOX_W, 2 * FOX_W, 3 * FOX_W
O_FLOG = 4 * FOX_W
O_DQ = O_FLOG + FOX_HEADS
O_DK, O_DV, O_DG = O_DQ + DIL_W, O_DQ + 2 * DIL_W, O_DQ + 3 * DIL_W
O_MQ = O_DQ + 4 * DIL_W
O_MG = O_MQ + MEM_W
P_FOX = 0
P_FG = P_FOX + 3 * FOX_W
P_DIL = P_FG + FOX_W
P_DG = P_DIL + 3 * DIL_W
P_MQ = P_DG + DIL_W
P_MG = P_MQ + MEM_W
P_FLOG = P_MG + MEM_W
PW = P_FLOG + LANES
A_FOX = 0
A_FLOG = A_FOX + 3 * FOX_W
PA = A_FLOG + LANES
B_FG = 0
B_DG = B_FG + FOX_W
B_DIL = B_DG + DIL_W
B_MQ = B_DIL + 3 * DIL_W
B_MG = -(-(B_MQ + MEM_W) // MEM_W) * MEM_W
PB = B_MG + MEM_W

VMEM_LIMIT = 56 * 1024 * 1024


def _pack_pieces():
    pieces = []
    for base in (O_FQ, O_DQ):
        seg = []
        for hp in range(FOX_HEADS // 2):
            for part in range(3):
                seg.append((base + part * FOX_W + hp * LANES, LANES))
        pieces.append(seg)
    fox, dil = pieces
    return fox + [(O_FG, FOX_W)] + dil + [(O_DG, DIL_W), (O_MQ, MEM_W), (O_MG, MEM_W), (O_FLOG, FOX_HEADS)]


def _pack_cols(w):
    parts = [w[..., s:s + n] for s, n in _pack_pieces()]
    parts.append(jnp.zeros(w.shape[:-1] + (LANES - FOX_HEADS,), w.dtype))
    return jnp.concatenate(parts, axis=-1)


def _split_cols(wp):
    def cut(start, width):
        return wp[..., start:start + width]

    group_a = jnp.concatenate([cut(P_FOX, 3 * FOX_W), cut(P_FLOG, LANES)], axis=-1)
    pad = jnp.zeros(wp.shape[:-1] + (B_MG - B_MQ - MEM_W,), wp.dtype)
    group_b = jnp.concatenate([cut(P_FG, FOX_W), cut(P_DG, DIL_W), cut(P_DIL, 3 * DIL_W), cut(P_MQ, MEM_W), pad,
                               cut(P_MG, MEM_W)], axis=-1)
    return group_a, group_b


def _params(sem=None, **kw):
    return pltpu.CompilerParams(dimension_semantics=sem, vmem_limit_bytes=VMEM_LIMIT, **kw)


def _mesh_pos():
    return lax.axis_index("x"), lax.axis_index("y"), lax.axis_index("c")


def _flip(v, d):
    return 1 - v if d else v


_RELATIONS = [(dx, dy, dc) for dx in (0, 1) for dy in (0, 1) for dc in (0, 1)][1:]
_SIBLING_AND_SAME_CORES = [(0, 0, 1), (1, 0, 0), (0, 1, 0), (1, 1, 0)]


_OTHER_CHIPS = [(1, 0), (0, 1), (1, 1)]


def small_all_reduce(small):
    vmem_spec = pl.BlockSpec(memory_space=pltpu.VMEM)

    def body(small_ref, tot_ref, land, send_sems, recv_sems):
        x, y, c = _mesh_pos()
        me = 4 * x + 2 * y + c
        land[me] = small_ref[...]
        sends, recvs = [], []
        for j, (dx, dy, dc) in enumerate(_RELATIONS):
            px, py, pc = _flip(x, dx), _flip(y, dy), _flip(c, dc)
            common = dict(send_sem=send_sems.at[j], recv_sem=recv_sems.at[j],
                          device_id=(px, py, pc), device_id_type=pl.DeviceIdType.MESH)
            sends.append(pltpu.make_async_remote_copy(src_ref=small_ref, dst_ref=land.at[me], **common))
            recvs.append(pltpu.make_async_remote_copy(src_ref=small_ref, dst_ref=land.at[4 * px + 2 * py + pc], **common))
        for cp in sends:
            cp.start()
        for cp in recvs:
            cp.wait_recv()
        for cp in sends:
            cp.wait_send()
        tot = land[0]
        for d in range(1, N_DEV):
            tot = tot + land[d]
        tot_ref[...] = tot

    return pl.pallas_call(
        body, name="small_sum", out_shape=jax.ShapeDtypeStruct(small.shape, small.dtype),
        in_specs=[vmem_spec], out_specs=vmem_spec,
        scratch_shapes=[pltpu.VMEM((N_DEV,) + small.shape, small.dtype),
                        pltpu.SemaphoreType.DMA((len(_RELATIONS),)), pltpu.SemaphoreType.DMA((len(_RELATIONS),))],
    )(small)


_HBM = pl.BlockSpec(memory_space=pltpu.HBM)
_SEM = pl.BlockSpec(memory_space=pltpu.SEMAPHORE)
_EFFECT = pltpu.SideEffectType.DATAFLOW_SIDE_EFFECTING


def _early_copies(src_refs, land_refs, send_sems, recv_sems, rows, gather, relations):
    x, y, c = _mesh_pos()
    me = 4 * x + 2 * y + c
    copies = []
    for a in range(len(src_refs)):
        for dx, dy, dc in relations:
            px, py, pc = _flip(x, dx), _flip(y, dy), _flip(c, dc)
            peer = 4 * px + 2 * py + pc
            copies.append(pltpu.make_async_remote_copy(
                src_ref=src_refs[a] if gather else src_refs[a].at[pl.ds(peer * rows[a], rows[a]), :],
                dst_ref=land_refs[a].at[pl.ds(me * rows[a], rows[a]), :],
                send_sem=send_sems[a], recv_sem=recv_sems[a],
                device_id=(px, py, pc), device_id_type=pl.DeviceIdType.MESH))
    return copies


def own_slots(shards, name):
    n = len(shards)
    x, y, c = _mesh_pos()
    me = (4 * x + 2 * y + c).astype(jnp.int32).reshape(1)
    empties = [lax.empty((N_DEV * s.shape[0], s.shape[1]), s.dtype) for s in shards]

    def body(me_ref, *refs):
        for a in range(n):
            refs[2 * n + a][...] = refs[a][...]

    return pl.pallas_call(
        body, name=name,
        grid_spec=pltpu.PrefetchScalarGridSpec(
            num_scalar_prefetch=1, grid=(1,),
            in_specs=[pl.BlockSpec(s.shape, lambda i, w: (0, 0)) for s in shards]
            + [pl.BlockSpec(memory_space=pl.ANY)] * n,
            out_specs=[pl.BlockSpec(s.shape, lambda i, w: (w[0], 0)) for s in shards]),
        out_shape=[jax.ShapeDtypeStruct(e.shape, e.dtype) for e in empties],
        input_output_aliases={1 + n + a: a for a in range(n)},
        compiler_params=_params(("arbitrary",)),
    )(me, *shards, *empties)


def early_exchange_start(srcs, name, gather=False, after=None, relations=_RELATIONS):
    n = len(srcs)
    if gather:
        rows = [s.shape[0] for s in srcs]
        lands = list(own_slots(srcs, name + "_place"))
    else:
        rows = [s.shape[0] // N_DEV for s in srcs]
        lands = [lax.empty(s.shape, s.dtype) for s in srcs]

    extra = [] if after is None else [after]

    def body(*refs):
        src_refs, land_refs = refs[:n], refs[n:2 * n]
        first_sem = 2 * n + len(extra)
        send_sems, recv_sems = refs[first_sem:first_sem + n], refs[first_sem + n:first_sem + 2 * n]
        token = refs[-1]
        for cp in _early_copies(src_refs, land_refs, send_sems, recv_sems, rows, gather, relations):
            cp.start()
        token[...] = jnp.zeros_like(token)

    hbm = lambda a: pltpu.HBM(a.shape, a.dtype)
    outs = pl.pallas_call(
        body, name=name,
        out_shape=[pltpu.SemaphoreType.DMA(())] * (2 * n)
        + [hbm(a) for a in srcs] + [hbm(a) for a in lands] + [jax.ShapeDtypeStruct((8, LANES), F32)],
        in_specs=[_HBM] * (2 * n) + [pl.BlockSpec(memory_space=pl.ANY)] * len(extra),
        out_specs=[_SEM] * (2 * n) + [_HBM] * (2 * n) + [pl.BlockSpec(memory_space=pltpu.VMEM)],
        input_output_aliases={i: 2 * n + i for i in range(2 * n)},
        compiler_params=pltpu.CompilerParams(has_side_effects=_EFFECT),
    )(*[pltpu.with_memory_space_constraint(a, pltpu.HBM) for a in list(srcs) + lands], *extra)
    handle = dict(sems=outs[:2 * n], srcs=outs[2 * n:3 * n], lands=outs[3 * n:4 * n], rows=rows,
                  copies=len(relations))
    return handle, outs[-1]


def early_exchange_wait(handle, after, name):
    n = len(handle["srcs"])
    rows = handle["rows"]

    def body(*refs):
        src_refs, land_refs = refs[:n], refs[n:2 * n]
        send_sems, recv_sems = refs[2 * n:3 * n], refs[3 * n:4 * n]
        x, y, c = _mesh_pos()
        for a in range(n):
            span = pl.ds(0, handle["copies"] * rows[a])
            all_copies = pltpu.make_async_remote_copy(
                src_ref=land_refs[a].at[span, :], dst_ref=land_refs[a].at[span, :],
                send_sem=send_sems[a], recv_sem=recv_sems[a],
                device_id=(x, y, c), device_id_type=pl.DeviceIdType.MESH)
            all_copies.wait_send()
            all_copies.wait_recv()

    hbm = lambda a: pltpu.HBM(a.shape, a.dtype)
    ins = list(handle["srcs"]) + list(handle["lands"])
    outs = pl.pallas_call(
        body, name=name,
        out_shape=[hbm(a) for a in ins],
        in_specs=[_HBM] * (2 * n) + [_SEM] * (2 * n) + [pl.BlockSpec(memory_space=pl.ANY)],
        out_specs=[_HBM] * (2 * n),
        input_output_aliases={i: i for i in range(2 * n)},
        compiler_params=pltpu.CompilerParams(has_side_effects=_EFFECT),
    )(*ins, *handle["sems"], after)
    return outs[:n], outs[n:]


def pass_on_to_sibling(lands, rows, name):
    n = len(lands)

    def body(*refs):
        land_refs = refs[n:2 * n]
        send_sems, recv_sems = refs[2 * n:]
        x, y, c = _mesh_pos()
        copies = []
        for a in range(n):
            for k, (dx, dy) in enumerate(_OTHER_CHIPS):
                slot = 4 * _flip(x, dx) + 2 * _flip(y, dy) + c
                blk = land_refs[a].at[pl.ds(slot * rows[a], rows[a]), :]
                copies.append(pltpu.make_async_remote_copy(
                    src_ref=blk, dst_ref=blk, send_sem=send_sems.at[a, k], recv_sem=recv_sems.at[a, k],
                    device_id=(x, y, 1 - c), device_id_type=pl.DeviceIdType.MESH))
        for cp in copies:
            cp.start()
        for cp in copies:
            cp.wait_recv()
        for cp in copies:
            cp.wait_send()

    any_spec = pl.BlockSpec(memory_space=pl.ANY)
    return pl.pallas_call(
        body, name=name,
        out_shape=[jax.ShapeDtypeStruct(a.shape, a.dtype) for a in lands],
        in_specs=[any_spec] * n, out_specs=[any_spec] * n,
        input_output_aliases={i: i for i in range(n)},
        scratch_shapes=[pltpu.SemaphoreType.DMA((n, len(_OTHER_CHIPS))), pltpu.SemaphoreType.DMA((n, len(_OTHER_CHIPS)))],
    )(*lands)


def slot_sum8(src, land, tr, name):
    rows, cols = land.shape[0] // N_DEV, land.shape[1]
    x, y, c = _mesh_pos()
    me = (4 * x + 2 * y + c).astype(jnp.int32).reshape(1)

    def body(me_ref, src_ref, land_ref, o_ref):
        acc = None
        for d in range(N_DEV):
            term = jnp.where(d == me_ref[0], src_ref[0], land_ref[d]).astype(F32)
            acc = term if acc is None else acc + term
        o_ref[...] = acc

    return pl.pallas_call(
        body, name=name,
        grid_spec=pltpu.PrefetchScalarGridSpec(
            num_scalar_prefetch=1, grid=(rows // tr,),
            in_specs=[pl.BlockSpec((1, tr, cols), lambda i, w: (w[0], i, 0)),
                      pl.BlockSpec((N_DEV, tr, cols), lambda i, w: (0, i, 0))],
            out_specs=pl.BlockSpec((tr, cols), lambda i, w: (i, 0))),
        out_shape=jax.ShapeDtypeStruct((rows, cols), F32),
        compiler_params=_params(("arbitrary",)),
    )(me, src.reshape(N_DEV, rows, cols), land.reshape(N_DEV, rows, cols))


def w_in_grad_sum(groups, name):
    rows = groups[0][0].shape[0] // N_DEV
    runs, pos = [], 0
    for start, width in _pack_pieces():
        runs.append((start, width, pos))
        pos += width
    n = len(groups)

    def body(*refs):
        src_refs, land_refs = refs[0:2 * n:2], refs[1:2 * n:2]
        g_ref = refs[2 * n]
        own_bufs, land_bufs = refs[2 * n + 1:3 * n + 1], refs[3 * n + 1:4 * n + 1]
        stage, load_sems, store_sems = refs[4 * n + 1:]
        x, y, c = _mesh_pos()
        me = 4 * x + 2 * y + c
        loads = []
        for k in range(n):
            pair = [pltpu.make_async_copy(src_refs[k].at[pl.ds(me * rows, rows), :], own_bufs[k], load_sems.at[k, 0]),
                    pltpu.make_async_copy(land_refs[k], land_bufs[k], load_sems.at[k, 1])]
            for cp in pair:
                cp.start()
            loads.append(pair)
        for k, (_, _, segments) in enumerate(groups):
            for cp in loads[k]:
                cp.wait()
            for first, packed, width in segments:
                for off in range(0, width, LANES):
                    cols = slice(first + off, first + off + LANES)
                    acc = None
                    for d in range(N_DEV):
                        term = jnp.where(d == me, own_bufs[k][:, cols], land_bufs[k][d * rows:(d + 1) * rows, cols])
                        acc = term.astype(F32) if acc is None else acc + term.astype(F32)
                    stage[packed + off:packed + off + LANES, :] = acc.T
        g_rows = g_ref.reshape(IN_W, LANES)
        stores = [pltpu.make_async_copy(stage.at[pl.ds(p, width), :], g_rows.at[pl.ds(start, width), :], store_sems.at[r])
                  for r, (start, width, p) in enumerate(runs)]
        for cp in stores:
            cp.start()
        for cp in stores:
            cp.wait()

    any_spec = pl.BlockSpec(memory_space=pl.ANY)
    operands = [a for src, land, _ in groups for a in (src, land)]
    return pl.pallas_call(
        body, name=name,
        in_specs=[any_spec] * (2 * n), out_specs=any_spec,
        out_shape=jax.ShapeDtypeStruct((IN_W, 1, LANES), F32),
        scratch_shapes=[pltpu.VMEM((rows, src.shape[1]), src.dtype) for src, _, _ in groups]
        + [pltpu.VMEM(land.shape, land.dtype) for _, land, _ in groups]
        + [pltpu.VMEM((PW, LANES), F32), pltpu.SemaphoreType.DMA((n, 2)), pltpu.SemaphoreType.DMA((len(runs),))],
        compiler_params=_params(),
    )(*operands)


def mm_tn_multi(a_t, bs, tt, name, out_dtype=F32):
    K, T = a_t.shape
    widths = [b.shape[1] for b in bs]
    steps = T // tt

    def body(a_ref, *rest):
        b_refs, o_ref, acc = rest[:-2], rest[-2], rest[-1]

        @pl.when(pl.program_id(0) == 0)
        def _():
            acc[...] = jnp.zeros(acc.shape, F32)

        av = a_ref[...]
        col = 0
        for b_ref, w in zip(b_refs, widths):
            acc[:, col:col + w] += jnp.dot(av, b_ref[...], preferred_element_type=F32)
            col += w

        @pl.when(pl.program_id(0) == steps - 1)
        def _():
            o_ref[...] = acc[...].astype(out_dtype)

    return pl.pallas_call(
        body, name=name, grid=(steps,),
        in_specs=[pl.BlockSpec((K, tt), lambda t: (0, t))] + [pl.BlockSpec((tt, w), lambda t: (t, 0)) for w in widths],
        out_specs=pl.BlockSpec((K, sum(widths)), lambda t: (0, 0)),
        out_shape=jax.ShapeDtypeStruct((K, sum(widths)), out_dtype),
        scratch_shapes=[pltpu.VMEM((K, sum(widths)), F32)],
        compiler_params=_params(("arbitrary",)),
    )(a_t, *bs)


def rms_fwd(x, g, tm, name, with_transpose=False, token=None):
    M, K = x.shape
    extra = [] if token is None else [token]

    def body(x_ref, g_ref, *rest):
        o_ref = rest[len(extra)]
        xv = x_ref[...]
        r = lax.rsqrt(jnp.mean(xv * xv, axis=-1, keepdims=True) + RMS_EPS)
        h = ((xv * r) * g_ref[...]).astype(BF16)
        o_ref[...] = h
        if with_transpose:
            rest[len(extra) + 1][...] = h.T

    out_specs = [pl.BlockSpec((tm, K), lambda i: (i, 0))]
    out_shape = [jax.ShapeDtypeStruct((M, K), BF16)]
    if with_transpose:
        out_specs.append(pl.BlockSpec((K, tm), lambda i: (0, i)))
        out_shape.append(jax.ShapeDtypeStruct((K, M), BF16))
    outs = pl.pallas_call(
        body, name=name, grid=(M // tm,),
        in_specs=[pl.BlockSpec((tm, K), lambda i: (i, 0)), pl.BlockSpec((1, K), lambda i: (0, 0))]
        + [pl.BlockSpec(t.shape, lambda i: (0, 0)) for t in extra],
        out_specs=out_specs, out_shape=out_shape,
        compiler_params=_params(("arbitrary",)),
    )(x, g, *extra)
    return outs if with_transpose else outs[0]


def rms_bwd(x, g, dh, dres, tm, name):
    M, K = x.shape
    has_res = dres is not None

    def body(*refs):
        if has_res:
            x_ref, g_ref, dh_ref, dres_ref, dx_ref, dg_ref = refs
        else:
            x_ref, g_ref, dh_ref, dx_ref, dg_ref = refs
        xv = x_ref[...]
        r = lax.rsqrt(jnp.mean(xv * xv, axis=-1, keepdims=True) + RMS_EPS)
        xn = xv * r
        dhv = dh_ref[...]
        dxn = dhv * g_ref[...]
        dx = r * (dxn - xn * jnp.mean(dxn * xn, axis=-1, keepdims=True))
        if has_res:
            dx = dx + dres_ref[...]
        dx_ref[...] = dx
        part = jnp.sum(dhv * xn, axis=0, keepdims=True)
        row = lax.broadcasted_iota(jnp.int32, (8, K), 0)
        upd = jnp.where(row == 0, part, 0.0)

        @pl.when(pl.program_id(0) == 0)
        def _():
            dg_ref[...] = upd

        @pl.when(pl.program_id(0) != 0)
        def _():
            dg_ref[...] += upd

    row_spec = pl.BlockSpec((tm, K), lambda i: (i, 0))
    ins = [x, g, dh] + ([dres] if has_res else [])
    in_specs = [row_spec, pl.BlockSpec((1, K), lambda i: (0, 0)), row_spec] + ([row_spec] if has_res else [])
    return pl.pallas_call(
        body, name=name, grid=(M // tm,),
        in_specs=in_specs,
        out_specs=[row_spec, pl.BlockSpec((8, K), lambda i: (0, 0))],
        out_shape=[jax.ShapeDtypeStruct((M, K), F32), jax.ShapeDtypeStruct((8, K), F32)],
        compiler_params=_params(("arbitrary",)),
    )(*ins)


def mm_nn(a, b, tm, tn, name, token=None):
    M, K = a.shape
    N = b.shape[1]
    extra = [] if token is None else [token]

    def body(a_ref, b_ref, *rest):
        rest[-1][...] = jnp.dot(a_ref[...], b_ref[...], preferred_element_type=F32)

    return pl.pallas_call(
        body, name=name, grid=(N // tn, M // tm),
        in_specs=[pl.BlockSpec((tm, K), lambda j, i: (i, 0)), pl.BlockSpec((K, tn), lambda j, i: (0, j))]
        + [pl.BlockSpec(t.shape, lambda j, i: (0, 0)) for t in extra],
        out_specs=pl.BlockSpec((tm, tn), lambda j, i: (i, j)),
        out_shape=jax.ShapeDtypeStruct((M, N), F32),
        compiler_params=_params(("arbitrary", "arbitrary")),
    )(a, b, *extra)


def mm_nt(a, b, tm, tk, name):
    M, K = a.shape
    N = b.shape[0]

    def body(a_ref, b_ref, o_ref):
        part = lax.dot_general(a_ref[...], b_ref[...], (((1,), (1,)), ((), ())), preferred_element_type=F32)

        @pl.when(pl.program_id(1) == 0)
        def _():
            o_ref[...] = part

        @pl.when(pl.program_id(1) != 0)
        def _():
            o_ref[...] += part

    return pl.pallas_call(
        body, name=name, grid=(M // tm, K // tk),
        in_specs=[pl.BlockSpec((tm, tk), lambda i, k: (i, k)), pl.BlockSpec((N, tk), lambda i, k: (0, k))],
        out_specs=pl.BlockSpec((tm, N), lambda i, k: (i, 0)),
        out_shape=jax.ShapeDtypeStruct((M, N), F32),
        compiler_params=_params(("arbitrary", "arbitrary")),
    )(a, b)


def in_proj_bwd_rms(pieces, ws, x, g, dres, tm, token):
    M, N = x.shape

    def body(*refs):
        n = len(pieces)
        p_refs, w_refs = refs[:n], refs[n:n + len(ws)]
        x_ref, g_ref, dres_ref, _, dx_ref, dg_ref = refs[n + len(ws):]
        dh = None
        for p_ref, (arr, group, col) in zip(p_refs, pieces):
            part = lax.dot_general(p_ref[...], w_refs[group][:, col:col + arr.shape[1]], (((1,), (1,)), ((), ())),
                                   preferred_element_type=F32)
            dh = part if dh is None else dh + part
        xv = x_ref[...]
        r = lax.rsqrt(jnp.mean(xv * xv, axis=-1, keepdims=True) + RMS_EPS)
        xn = xv * r
        dxn = dh * g_ref[...]
        dx_ref[...] = r * (dxn - xn * jnp.mean(dxn * xn, axis=-1, keepdims=True)) + dres_ref[...]
        row = lax.broadcasted_iota(jnp.int32, (8, N), 0)
        upd = jnp.where(row == 0, jnp.sum(dh * xn, axis=0, keepdims=True), 0.0)

        @pl.when(pl.program_id(0) == 0)
        def _():
            dg_ref[...] = upd

        @pl.when(pl.program_id(0) != 0)
        def _():
            dg_ref[...] += upd

    row_spec = pl.BlockSpec((tm, N), lambda i: (i, 0))
    return pl.pallas_call(
        body, name="in_proj_bwd", grid=(M // tm,),
        in_specs=[pl.BlockSpec((tm, arr.shape[1]), lambda i: (i, 0)) for arr, _, _ in pieces]
        + [pl.BlockSpec(w.shape, lambda i: (0, 0), pipeline_mode=pl.Buffered(1)) for w in ws]
        + [row_spec, pl.BlockSpec((1, N), lambda i: (0, 0)), row_spec, pl.BlockSpec(token.shape, lambda i: (0, 0))],
        out_specs=[row_spec, pl.BlockSpec((8, N), lambda i: (0, 0))],
        out_shape=[jax.ShapeDtypeStruct((M, N), F32), jax.ShapeDtypeStruct((8, N), F32)],
        compiler_params=_params(("arbitrary",)),
    )(*[arr for arr, _, _ in pieces], *ws, x, g, dres, token)


def _log_sigmoid(z):
    return jnp.minimum(z, 0.0) - jnp.log(1.0 + jnp.exp(-jnp.abs(z)))


def _tri(n, lower):
    r = lax.broadcasted_iota(jnp.int32, (n, n), 0)
    c = lax.broadcasted_iota(jnp.int32, (n, n), 1)
    return jnp.where((r >= c) if lower else (r <= c), 1.0, 0.0).astype(F32)


def fox_gate(proj3, b_pad):
    B, S, _ = proj3.shape
    nblk = S // TK

    def body(f_ref, b_ref, o_ref):
        tri = _tri(TK, True)
        carry = jnp.zeros((1, LANES), F32)
        for n in range(nblk):
            z = f_ref[0, n * TK:(n + 1) * TK, :] + b_ref[...]
            logf = _log_sigmoid(z)
            cs = jnp.dot(tri, logf, preferred_element_type=F32, precision=lax.Precision.HIGHEST) + carry
            carry = cs[TK - 1:TK, :]
            o_ref[0, n * TK:(n + 1) * TK, :] = -cs

    return pl.pallas_call(
        body, name="fox_gate", grid=(B,),
        in_specs=[pl.BlockSpec((1, S, LANES), lambda b: (b, 0, A_FLOG // LANES)),
                  pl.BlockSpec((1, LANES), lambda b: (0, 0))],
        out_specs=pl.BlockSpec((1, S, LANES), lambda b: (b, 0, 0)),
        out_shape=jax.ShapeDtypeStruct((B, S, LANES), F32),
        compiler_params=_params(("arbitrary",)),
    )(proj3, b_pad)


def fox_gate_bwd(drow, dneg, proj3, b_pad):
    B, S, _ = proj3.shape
    nblk = S // TK

    def body(d_ref, r_ref, f_ref, b_ref, o_ref, db_ref):
        tri = _tri(TK, False)
        lane = lax.broadcasted_iota(jnp.int32, (TK, LANES), 1)
        carry = jnp.zeros((1, LANES), F32)
        dbsum = jnp.zeros((1, LANES), F32)
        for n in reversed(range(nblk)):
            dk_side = None
            for hp in range(FOX_HEADS // 2):
                two = jnp.where(lane < 2, r_ref[0, n * TK:(n + 1) * TK, hp * LANES:(hp + 1) * LANES], 0.0)
                two = pltpu.roll(two, 2 * hp, 1) if hp else two
                dk_side = two if dk_side is None else dk_side + two
            dc = jnp.where(lane < FOX_HEADS, d_ref[0, :, n * TK:(n + 1) * TK].T - dk_side, 0.0)
            rs = jnp.dot(tri, dc, preferred_element_type=F32, precision=lax.Precision.HIGHEST) + carry
            carry = rs[0:1, :]
            z = f_ref[0, n * TK:(n + 1) * TK, :] + b_ref[...]
            dz = rs * (1.0 / (1.0 + jnp.exp(z)))
            o_ref[0, n * TK:(n + 1) * TK, :] = dz.astype(BF16)
            dbsum = dbsum + jnp.sum(dz, axis=0, keepdims=True)
        row = lax.broadcasted_iota(jnp.int32, (8, LANES), 0)
        upd = jnp.where(row == 0, dbsum, 0.0)

        @pl.when(pl.program_id(0) == 0)
        def _():
            db_ref[...] = upd

        @pl.when(pl.program_id(0) != 0)
        def _():
            db_ref[...] += upd

    return pl.pallas_call(
        body, name="fox_gate_bwd", grid=(B,),
        in_specs=[pl.BlockSpec((1, LANES, S), lambda b: (b, 0, 0)),
                  pl.BlockSpec((1, S, FOX_W), lambda b: (b, 0, 0)),
                  pl.BlockSpec((1, S, LANES), lambda b: (b, 0, A_FLOG // LANES)),
                  pl.BlockSpec((1, LANES), lambda b: (0, 0))],
        out_specs=[pl.BlockSpec((1, S, LANES), lambda b: (b, 0, 0)), pl.BlockSpec((8, LANES), lambda b: (0, 0))],
        out_shape=[jax.ShapeDtypeStruct((B, S, LANES), BF16), jax.ShapeDtypeStruct((8, LANES), F32)],
        compiler_params=_params(("arbitrary",)),
    )(drow, dneg, proj3, b_pad)


def _rope_tables(S):
    half = ROPE_DIM // 2
    f32 = np.float32
    pos = np.arange(S, dtype=f32)
    inv_freq = f32(1.0) / np.power(f32(ROPE_THETA), np.arange(0, ROPE_DIM, 2, dtype=f32) / f32(ROPE_DIM)).astype(f32)
    ang = (pos[:, None] * inv_freq[None, :]).astype(f32).astype(np.float64)
    cos, sin = np.cos(ang).astype(f32), np.sin(ang).astype(f32)
    one = np.ones((S, HEAD_DIM - ROPE_DIM), f32)
    zero = np.zeros((S, HEAD_DIM - ROPE_DIM), f32)
    zh = np.zeros((S, half), f32)
    c = np.concatenate([cos, cos, one], axis=1)
    s1 = np.concatenate([-sin, zh, zero], axis=1)
    s2 = np.concatenate([zh, sin, zero], axis=1)
    return tuple(jnp.asarray(np.concatenate([t, t], axis=1)) for t in (c, s1, s2))


_HALF_ROPE = ROPE_DIM // 2


def _rope(t, c, s1, s2):
    return t * c + pltpu.roll(t, LANES - _HALF_ROPE, 1) * s1 + pltpu.roll(t, _HALF_ROPE, 1) * s2


def _rope_bwd(d, c, s1, s2):
    return d * c + pltpu.roll(d * s1, _HALF_ROPE, 1) + pltpu.roll(d * s2, LANES - _HALF_ROPE, 1)


def _scale_parts(scale):
    m, _ = math.frexp(scale)
    return (scale, None) if m == 0.5 else (None, scale)


def _log_masks(S, kind):
    nd = 1 if kind == "causal" else S // TQ
    a = np.arange(TQ)[:, None]
    b = np.arange(TK)[None, :]
    out = np.zeros((nd, TQ, TK), np.float32)
    for d in range(nd):
        delta = d * TQ + a - b
        if kind == "causal":
            m = (delta >= 0).astype(np.float64)
        else:
            m = sum(((delta >= 0) & (delta % dil == 0) & (delta <= w)).astype(np.float64) for w, dil in DILATIONS)
        out[d] = np.where(m > 0, np.log(np.maximum(m, 1.0)), NEG_INF)
    return jnp.asarray(out)


def _attn_setup(kind):
    pair = kind != "mem"
    e_dim = HEAD_DIM if pair else MEM_HEAD_DIM
    q_fold, s_scale = _scale_parts(1.0 / math.sqrt(e_dim))
    return dict(pair=pair, col0={"fox": A_FOX, "dil": B_DIL, "mem": B_MQ}[kind],
                n_blocks=FOX_HEADS // 2 if pair else MEM_HEADS, q_fold=q_fold, s_scale=s_scale,
                nh=2 if pair else 1)


def _cat(parts, axis):
    return parts[0] if len(parts) == 1 else jnp.concatenate(parts, axis=axis)


def _log_masks_t(S, kind):
    return jnp.swapaxes(_log_masks(S, kind), 1, 2)


def _head_rows(hh, pair):
    row = lax.broadcasted_iota(jnp.int32, (LANES, 1), 0)
    if not pair:
        return row >= 0
    return (row >= HEAD_DIM * hh) & (row < HEAD_DIM * (hh + 1))


def _attn_t_inputs(kind, src, S, negc_cols, mask, rope, kv):
    cfg = _attn_setup(kind)
    col0 = cfg["col0"]
    ins, in_specs = [], []
    if cfg["pair"]:
        ins.append(src)
        in_specs.append(pl.BlockSpec((1, S, PAIR_W), lambda b, h: (b, 0, col0 // PAIR_W + h)))
    else:
        ins += [src, kv, kv]
        in_specs += [pl.BlockSpec((1, S, LANES), lambda b, h: (b, 0, col0 // LANES + h)),
                     pl.BlockSpec((1, MEM_LEN, LANES), lambda b, h: (b, 0, h)),
                     pl.BlockSpec((1, MEM_LEN, LANES), lambda b, h: (b, 0, MEM_HEADS + h))]
    if negc_cols is not None:
        ins.append(negc_cols)
        in_specs.append(pl.BlockSpec((1, S, LANES), lambda b, h: (b, 0, 0)))
    if mask is not None:
        ins.append(mask)
        in_specs.append(pl.BlockSpec(mask.shape, lambda b, h: (0, 0, 0)))
    if rope is not None:
        ins += list(rope)
        in_specs += [pl.BlockSpec((S, LANES), lambda b, h: (0, 0))] * 3
    return ins, in_specs


def _attn_t_prep(cfg, refs, S, Sk, *, qT2s, ks, vs=None, vTs=None, kTs=None, nb=None):
    pair, nh = cfg["pair"], cfg["nh"]
    lane = lax.broadcasted_iota(jnp.int32, (1, LANES), 1)
    rope_refs = refs["rope"]

    def prep_q(n):
        rows = slice(n * TQ, (n + 1) * TQ)
        q = refs["load_q"](rows)
        if rope_refs is not None:
            q = _rope(q, *[t[rows, :] for t in rope_refs])
        if cfg["q_fold"] is not None:
            q = q * cfg["q_fold"]
        qtb = q.astype(BF16).T
        for hh in range(nh):
            qT2s[n, :, hh * TQ:(hh + 1) * TQ] = jnp.where(_head_rows(hh, pair), qtb, jnp.zeros_like(qtb))

    def prep_kv(n):
        rows = slice(n * TK, (n + 1) * TK)
        k, v = refs["load_kv"](rows)
        if rope_refs is not None:
            k = _rope(k, *[t[rows, :] for t in rope_refs])
        kb = k.astype(BF16)
        vb = v.astype(BF16)
        ks[rows, :] = kb
        if vs is not None:
            vs[rows, :] = vb
        if vTs is not None:
            vTs[n] = vb.T
        if kTs is not None:
            kTs[n] = kb.T
        if nb is not None:
            blk = refs["negc"][0, rows, :]
            for hh in range(nh):
                h = 2 * refs["block"] + hh
                col = jnp.sum(jnp.where(lane == h, blk, 0.0), axis=1, keepdims=True)
                nb[hh, rows, :] = jnp.broadcast_to(col, (TK, LANES))

    for n in range(S // TQ):
        prep_q(n)
    for n in range(Sk // TK):
        prep_kv(n)


def _raw_scores_t(cfg, k, qT2):
    sT = jnp.dot(k, qT2, preferred_element_type=F32)
    if cfg["s_scale"] is not None:
        sT = sT * cfg["s_scale"]
    return sT


def _bias_mask_t(cfg, sT, nb, mask_ref, kc, midx):
    nh = cfg["nh"]
    if nb is None and midx is None:
        return sT
    parts = []
    for hh in range(nh):
        t = sT[:, hh * TQ:(hh + 1) * TQ]
        if nb is not None:
            t = t + jnp.concatenate([nb[hh, kc, :]] * (TQ // LANES), axis=1)
        if midx is not None:
            t = t + mask_ref[midx]
        parts.append(t)
    return _cat(parts, 1)


def _tile_pairs(kind, nq, nk):
    if kind == "mem":
        return [(i, j) for i in range(nq) for j in range(nk)], (lambda i, j: None)
    pairs = [(i, j) for i in range(nq) for j in range(i + 1)]
    if kind == "fox":
        return pairs, (lambda i, j: 0 if j == i else None)
    return pairs, (lambda i, j: i - j)


def attn_fwd(kind, src, S, *, negc_cols=None, mask=None, rope=None, kv=None):
    B = src.shape[0]
    cfg = _attn_setup(kind)
    pair, nh = cfg["pair"], cfg["nh"]
    Sk = S if pair else MEM_LEN
    has_bias, has_rope = negc_cols is not None, rope is not None
    R = nh * TQ
    nq, nk = S // TQ, Sk // TK
    pairs, mask_index = _tile_pairs(kind, nq, nk)

    def body(*refs):
        refs = list(refs)
        if pair:
            qkv_ref = refs.pop(0)
            load_q = lambda rows: qkv_ref[0, rows, 0:LANES]
            load_kv = lambda rows: (qkv_ref[0, rows, LANES:2 * LANES], qkv_ref[0, rows, 2 * LANES:3 * LANES])
        else:
            q_ref, k_ref, v_ref = refs.pop(0), refs.pop(0), refs.pop(0)
            load_q = lambda rows: q_ref[0, rows, :]
            load_kv = lambda rows: (k_ref[0, rows, :], v_ref[0, rows, :])
        negc_ref = refs.pop(0) if has_bias else None
        mask_ref = refs.pop(0) if mask is not None else None
        rope_refs = [refs.pop(0) for _ in range(3)] if has_rope else None
        o_ref, lse_ref, qT2s, ks, vTs, s_a, s_b, p_a, p_b = refs[:9]
        nb = refs[9] if has_bias else None
        _attn_t_prep(cfg, dict(load_q=load_q, load_kv=load_kv, rope=rope_refs, negc=negc_ref,
                               block=pl.program_id(1)), S, Sk, qT2s=qT2s, ks=ks, vTs=vTs, nb=nb)

        def cols(j):
            return slice(j * TK, (j + 1) * TK)

        def scores(i, j):
            return _raw_scores_t(cfg, ks[cols(j), :], qT2s[i])

        def finish(i, m, l, accT):
            oT2 = accT / l
            oT = jnp.where(_head_rows(0, True), oT2[:, 0:TQ], oT2[:, TQ:2 * TQ]) if pair else oT2
            o_ref[0, i * TQ:(i + 1) * TQ, :] = oT.T
            lse_ref[0, 0, i:i + 1, :] = m + jnp.log(l)

        s_bufs, p_bufs = (s_a, s_b), (p_a, p_b)
        s_bufs[0][...] = scores(*pairs[0])
        m = l = accT = None
        for t, (i, j) in enumerate(pairs):
            cur, oth = t % 2, 1 - t % 2
            if t > 0:
                i_prev, j_prev = pairs[t - 1]
                pv = jnp.dot(vTs[j_prev], p_bufs[oth][...], preferred_element_type=F32)
                acc_full = pv if accT is None else accT + pv
            if t + 1 < len(pairs):
                s_bufs[oth][...] = scores(*pairs[t + 1])
            first = j == 0
            if first and t > 0:
                finish(i_prev, m, l, acc_full)
            sT = _bias_mask_t(cfg, s_bufs[cur][...], nb, mask_ref, cols(j), mask_index(i, j))
            m_tile = jnp.max(sT, axis=0, keepdims=True)
            m_new = m_tile if first else jnp.maximum(m, m_tile)
            p = jnp.exp(sT - m_new)
            p_bufs[cur][...] = p.astype(BF16)
            if first:
                l, accT = jnp.sum(p, axis=0, keepdims=True), None
            else:
                alpha = jnp.exp(m - m_new)
                l, accT = alpha * l + jnp.sum(p, axis=0, keepdims=True), acc_full * alpha
            m = m_new
        i_last, j_last = pairs[-1]
        pv = jnp.dot(vTs[j_last], p_bufs[(len(pairs) - 1) % 2][...], preferred_element_type=F32)
        finish(i_last, m, l, pv if accT is None else accT + pv)

    ins, in_specs = _attn_t_inputs(kind, src, S, negc_cols, mask, rope, kv)
    W = cfg["n_blocks"] * LANES
    scratch = [pltpu.VMEM((nq, LANES, R), BF16), pltpu.VMEM((Sk, LANES), BF16), pltpu.VMEM((nk, LANES, TK), BF16),
               pltpu.VMEM((TK, R), F32), pltpu.VMEM((TK, R), F32), pltpu.VMEM((TK, R), BF16), pltpu.VMEM((TK, R), BF16)]
    if has_bias:
        scratch.append(pltpu.VMEM((nh, Sk, LANES), F32))
    return pl.pallas_call(
        body, name=kind + "_attn_fwd", grid=(B, cfg["n_blocks"]),
        in_specs=in_specs,
        out_specs=[pl.BlockSpec((1, S, LANES), lambda b, h: (b, 0, h)),
                   pl.BlockSpec((1, 1, nq, R), lambda b, h: (b, h, 0, 0))],
        out_shape=[jax.ShapeDtypeStruct((B, S, W), F32), jax.ShapeDtypeStruct((B, cfg["n_blocks"], nq, R), F32)],
        scratch_shapes=scratch,
        compiler_params=_params(("arbitrary", "arbitrary")),
    )(*ins)


def attn_bwd(kind, src, do, o, lse, S, *, negc_cols=None, mask=None, rope=None, kv=None, token=None):
    B = src.shape[0]
    cfg = _attn_setup(kind)
    pair, nh, s_scale, q_fold = cfg["pair"], cfg["nh"], cfg["s_scale"], cfg["q_fold"]
    Sk = S if pair else MEM_LEN
    has_bias, has_rope = negc_cols is not None, rope is not None
    R = nh * TQ
    nq, nk = S // TQ, Sk // TK
    pairs, mask_index = _tile_pairs(kind, nq, nk)

    def body(*refs):
        refs = list(refs)
        if pair:
            qkv_ref = refs.pop(0)
            load_q = lambda rows: qkv_ref[0, rows, 0:LANES]
            load_kv = lambda rows: (qkv_ref[0, rows, LANES:2 * LANES], qkv_ref[0, rows, 2 * LANES:3 * LANES])
        else:
            q_ref, k_ref, v_ref = refs.pop(0), refs.pop(0), refs.pop(0)
            load_q = lambda rows: q_ref[0, rows, :]
            load_kv = lambda rows: (k_ref[0, rows, :], v_ref[0, rows, :])
        negc_ref = refs.pop(0) if has_bias else None
        mask_ref = refs.pop(0) if mask is not None else None
        rope_refs = [refs.pop(0) for _ in range(3)] if has_rope else None
        do_ref, o_ref, lse_ref = refs.pop(0), refs.pop(0), refs.pop(0)
        if token is not None:
            refs.pop(0)
        if pair:
            dqkv_ref = refs.pop(0)
            dneg_ref = refs.pop(0) if has_bias else None
            drow_ref = refs.pop(0) if has_bias else None
        else:
            dq_ref, dk_ref, dv_ref = refs.pop(0), refs.pop(0), refs.pop(0)
        qT2s, ks, vs, kTs, doT2s, delta_s, dk_acc, dv_acc = refs[:8]
        bufs_a, bufs_b = refs[8:12], refs[12:16]
        nb, dneg_acc = (refs[16], refs[17]) if has_bias else (None, None)
        lane = lax.broadcasted_iota(jnp.int32, (1, LANES), 1)
        _attn_t_prep(cfg, dict(load_q=load_q, load_kv=load_kv, rope=rope_refs, negc=negc_ref,
                               block=pl.program_id(1)), S, Sk,
                     qT2s=qT2s, ks=ks, vs=vs, kTs=kTs, nb=nb)

        def prep_do(n):
            rows = slice(n * TQ, (n + 1) * TQ)
            doT = do_ref[0, rows, :].astype(BF16).astype(F32).T
            prodT = doT * o_ref[0, rows, :].T
            doTb = doT.astype(BF16)
            for hh in range(nh):
                hm = _head_rows(hh, pair)
                doT2s[n, :, hh * TQ:(hh + 1) * TQ] = jnp.where(hm, doTb, jnp.zeros_like(doTb))
                delta_s[n:n + 1, hh * TQ:(hh + 1) * TQ] = jnp.sum(jnp.where(hm, prodT, 0.0), axis=0, keepdims=True)

        for n in range(nq):
            prep_do(n)
        dk_acc[...] = jnp.zeros(dk_acc.shape, F32)
        dv_acc[...] = jnp.zeros(dv_acc.shape, F32)
        if has_bias:
            dneg_acc[...] = jnp.zeros(dneg_acc.shape, F32)

        def cols(j):
            return slice(j * TK, (j + 1) * TK)

        nt_dims = (((1,), (1,)), ((), ()))

        def first_products(i, j, bufs):
            bufs[0][...] = _raw_scores_t(cfg, ks[cols(j), :], qT2s[i])
            bufs[1][...] = jnp.dot(vs[cols(j), :], doT2s[i], preferred_element_type=F32)

        def last_products(i, j, bufs, dqT2):
            dv_acc[j] += lax.dot_general(doT2s[i], bufs[2][...], nt_dims, preferred_element_type=F32)
            dk_acc[j] += lax.dot_general(qT2s[i], bufs[3][...], nt_dims, preferred_element_type=F32)
            dq = jnp.dot(kTs[j], bufs[3][...], preferred_element_type=F32)
            return dq if dqT2 is None else dqT2 + dq

        def finish_q(i, dqT2, drow):
            rows = slice(i * TQ, (i + 1) * TQ)
            dqT = jnp.where(_head_rows(0, True), dqT2[:, 0:TQ], dqT2[:, TQ:2 * TQ]) if pair else dqT2
            dq = dqT.T
            if q_fold is not None:
                dq = dq * q_fold
            if has_rope:
                dq = _rope_bwd(dq, *[t[rows, :] for t in rope_refs])
            if pair:
                dqkv_ref[0, rows, 0:LANES] = dq.astype(BF16)
            else:
                dq_ref[0, rows, :] = dq.astype(BF16)
            if has_bias:
                drow_ref[0, 0, i:i + 1, :] = drow

        bufs = (bufs_a, bufs_b)
        first_products(*pairs[0], bufs[0])
        dqT2 = drow = None
        for t, (i, j) in enumerate(pairs):
            cur, oth = bufs[t % 2], bufs[1 - t % 2]
            first = j == 0
            if first and t > 0:
                i_prev, j_prev = pairs[t - 1]
                finish_q(i_prev, last_products(i_prev, j_prev, oth, dqT2), drow)
                dqT2 = drow = None
            sT = _bias_mask_t(cfg, cur[0][...], nb, mask_ref, cols(j), mask_index(i, j))
            pT = jnp.exp(sT - lse_ref[0, 0, i:i + 1, :])
            dsT = pT * (cur[1][...] - delta_s[i:i + 1, :])
            if has_bias:
                tile_rows = jnp.sum(dsT, axis=0, keepdims=True)
                drow = tile_rows if drow is None else drow + tile_rows
                for hh in range(nh):
                    part = dsT[:, hh * TQ:hh * TQ + LANES]
                    for u in range(1, TQ // LANES):
                        part = part + dsT[:, hh * TQ + u * LANES:hh * TQ + (u + 1) * LANES]
                    dneg_acc[hh, cols(j), :] += part
            if s_scale is not None:
                dsT = dsT * s_scale
            cur[2][...] = pT.astype(BF16)
            cur[3][...] = dsT.astype(BF16)
            if not first:
                dqT2 = last_products(*pairs[t - 1], oth, dqT2)
            if t + 1 < len(pairs):
                first_products(*pairs[t + 1], oth)
        i_last, j_last = pairs[-1]
        finish_q(i_last, last_products(i_last, j_last, bufs[(len(pairs) - 1) % 2], dqT2), drow)

        for n in range(nk):
            rows = slice(n * TK, (n + 1) * TK)
            dk = dk_acc[n].T
            dv = dv_acc[n].T
            if has_rope:
                dk = _rope_bwd(dk, *[t[rows, :] for t in rope_refs])
            if pair:
                dqkv_ref[0, rows, LANES:2 * LANES] = dk.astype(BF16)
                dqkv_ref[0, rows, 2 * LANES:3 * LANES] = dv.astype(BF16)
            else:
                dk_ref[0, rows, :] = dk.astype(BF16)
                dv_ref[0, rows, :] = dv.astype(BF16)
            if has_bias:
                x0 = jnp.sum(dneg_acc[0, rows, :], axis=1, keepdims=True)
                x1 = jnp.sum(dneg_acc[1, rows, :], axis=1, keepdims=True)
                dneg_ref[0, rows, :] = jnp.where(lane == 0, x0, jnp.where(lane == 1, x1, 0.0))

    ins, in_specs = _attn_t_inputs(kind, src, S, negc_cols, mask, rope, kv)
    row_spec = pl.BlockSpec((1, S, LANES), lambda b, h: (b, 0, h))
    vec_spec = pl.BlockSpec((1, 1, nq, R), lambda b, h: (b, h, 0, 0))
    ins += [do, o, lse]
    in_specs += [row_spec, row_spec, vec_spec]
    if token is not None:
        ins.append(token)
        in_specs.append(pl.BlockSpec(token.shape, lambda b, h: (0, 0)))
    W = cfg["n_blocks"] * LANES
    if pair:
        out_specs = [pl.BlockSpec((1, S, PAIR_W), lambda b, h: (b, 0, h))]
        out_shape = [jax.ShapeDtypeStruct((B, S, 3 * W), BF16)]
        if has_bias:
            out_specs += [row_spec, vec_spec]
            out_shape += [jax.ShapeDtypeStruct((B, S, W), F32), jax.ShapeDtypeStruct((B, cfg["n_blocks"], nq, R), F32)]
    else:
        kv_spec = pl.BlockSpec((1, MEM_LEN, LANES), lambda b, h: (b, 0, h))
        out_specs = [row_spec, kv_spec, kv_spec]
        out_shape = [jax.ShapeDtypeStruct((B, S, W), BF16)] + [jax.ShapeDtypeStruct((B, MEM_LEN, W), BF16)] * 2
    scratch = [pltpu.VMEM((nq, LANES, R), BF16), pltpu.VMEM((Sk, LANES), BF16),
               pltpu.VMEM((Sk, LANES), BF16), pltpu.VMEM((nk, LANES, TK), BF16), pltpu.VMEM((nq, LANES, R), BF16),
               pltpu.VMEM((nq, R), F32), pltpu.VMEM((nk, LANES, TK), F32), pltpu.VMEM((nk, LANES, TK), F32)]
    pair_bufs = [pltpu.VMEM((TK, R), F32), pltpu.VMEM((TK, R), F32), pltpu.VMEM((TK, R), BF16), pltpu.VMEM((TK, R), BF16)]
    scratch += pair_bufs + pair_bufs
    if has_bias:
        scratch += [pltpu.VMEM((nh, Sk, LANES), F32), pltpu.VMEM((nh, Sk, LANES), F32)]
    return pl.pallas_call(
        body, name=kind + "_attn_bwd", grid=(B, cfg["n_blocks"]),
        in_specs=in_specs, out_specs=out_specs, out_shape=out_shape, scratch_shapes=scratch,
        compiler_params=_params(("arbitrary", "arbitrary")),
    )(*ins)


def _sigmoid(g):
    return 1.0 / (1.0 + jnp.exp(-g))


def out_step(proj, o_fox, o_dil, o_mem, w_out, x, target, gf, tm):
    T = x.shape[0]

    def body(fg_ref, dg_ref, mg_ref, of_ref, od_ref, om_ref, w_ref, x_ref, t_ref, gf_ref,
             dx_ref, dof_ref, dod_ref, dom_ref, dfg_ref, ddg_ref, dmg_ref, gw_ref, sm_ref, gw_acc):
        branches = []
        for g_ref, o_ref in ((fg_ref, of_ref), (dg_ref, od_ref), (mg_ref, om_ref)):
            g = g_ref[...]
            sg = _sigmoid(g)
            o = o_ref[...]
            branches.append((g, sg, o))
        ymix = jnp.concatenate([(o * (g * sg)).astype(BF16) for g, sg, o in branches], axis=1)
        x2 = x_ref[...] + jnp.dot(ymix, w_ref[...], preferred_element_type=F32)
        r = lax.rsqrt(jnp.mean(x2 * x2, axis=-1, keepdims=True) + RMS_EPS)
        yn = x2 * r
        err = yn * gf_ref[...] - t_ref[...]
        loss = 0.5 * jnp.sum(jnp.sum(err * err, axis=-1, keepdims=True) / D_MODEL, axis=0, keepdims=True)
        dyf = err / D_MODEL
        dgf = jnp.sum(dyf * yn, axis=0, keepdims=True)
        dyn = dyf * gf_ref[...]
        dx2 = r * (dyn - yn * jnp.mean(dyn * yn, axis=-1, keepdims=True))
        dx_ref[...] = dx2
        dxb = dx2.astype(BF16)
        dmix = lax.dot_general(dxb, w_ref[...], (((1,), (1,)), ((), ())), preferred_element_type=F32)
        col = 0
        for (g, sg, o), do_ref, dgate_ref in zip(branches, (dof_ref, dod_ref, dom_ref), (dfg_ref, ddg_ref, dmg_ref)):
            d = dmix[:, col:col + g.shape[1]]
            col += g.shape[1]
            do_ref[...] = (d * (g * sg)).astype(BF16)
            dgate_ref[...] = (d * o * (sg * (1.0 + g * (1.0 - sg)))).astype(BF16)
        row = lax.broadcasted_iota(jnp.int32, (8, D_MODEL), 0)
        upd = jnp.where(row == 0, dgf, jnp.where(row == 1, loss, 0.0))

        @pl.when(pl.program_id(0) == 0)
        def _():
            sm_ref[...] = jnp.zeros(sm_ref.shape, F32)
            gw_acc[...] = jnp.zeros(gw_acc.shape, F32)

        sm_ref[...] += upd
        gw_acc[...] += lax.dot_general(ymix, dxb, (((0,), (0,)), ((), ())), preferred_element_type=F32)

        @pl.when(pl.program_id(0) == T // tm - 1)
        def _():
            gw_ref[...] = gw_acc[...].astype(BF16)

    def rows(w, col=0):
        return pl.BlockSpec((tm, w), lambda i: (i, col))

    return pl.pallas_call(
        body, name="out_step", grid=(T // tm,),
        in_specs=[rows(FOX_W, B_FG // FOX_W), rows(DIL_W, B_DG // DIL_W), rows(MEM_W, B_MG // MEM_W),
                  rows(FOX_W), rows(DIL_W), rows(MEM_W),
                  pl.BlockSpec((MIX_W, D_MODEL), lambda i: (0, 0)),
                  rows(D_MODEL), rows(D_MODEL), pl.BlockSpec((1, D_MODEL), lambda i: (0, 0))],
        out_specs=[rows(D_MODEL), rows(FOX_W), rows(DIL_W), rows(MEM_W), rows(FOX_W), rows(DIL_W), rows(MEM_W),
                   pl.BlockSpec((MIX_W, D_MODEL), lambda i: (0, 0)), pl.BlockSpec((8, D_MODEL), lambda i: (0, 0))],
        out_shape=[jax.ShapeDtypeStruct((T, D_MODEL), F32), jax.ShapeDtypeStruct((T, FOX_W), BF16),
                   jax.ShapeDtypeStruct((T, DIL_W), BF16), jax.ShapeDtypeStruct((T, MEM_W), BF16),
                   jax.ShapeDtypeStruct((T, FOX_W), BF16), jax.ShapeDtypeStruct((T, DIL_W), BF16),
                   jax.ShapeDtypeStruct((T, MEM_W), BF16), jax.ShapeDtypeStruct((MIX_W, D_MODEL), BF16),
                   jax.ShapeDtypeStruct((8, D_MODEL), F32)],
        scratch_shapes=[pltpu.VMEM((MIX_W, D_MODEL), F32)],
        compiler_params=_params(("arbitrary",)),
    )(proj, proj, proj, o_fox, o_dil, o_mem, w_out, x, target, gf)


def adamw(w, g, m, v, tr, name):
    lead = w.shape[:-2]
    R, C = w.shape[-2:]
    zeros = (0,) * len(lead)

    def body(w_ref, g_ref, m_ref, v_ref, d_ref, mo_ref, vo_ref):
        gv = g_ref[...]
        mn = ADAM_B1 * m_ref[...] + (1.0 - ADAM_B1) * gv
        vn = ADAM_B2 * v_ref[...] + (1.0 - ADAM_B2) * jnp.square(gv)
        m_hat = mn / (1.0 - ADAM_B1 ** ADAM_STEP)
        v_hat = vn / (1.0 - ADAM_B2 ** ADAM_STEP)
        d_ref[...] = -ADAM_LR * (m_hat / (jnp.sqrt(v_hat) + ADAM_EPS) + ADAM_WD * w_ref[...])
        mo_ref[...] = mn
        vo_ref[...] = vn

    spec = pl.BlockSpec((1,) * len(lead) + (tr, C), lambda i: zeros + (i, 0))
    return pl.pallas_call(
        body, name=name, grid=(pl.cdiv(R, tr),),
        in_specs=[spec] * 4, out_specs=[spec] * 3,
        out_shape=[jax.ShapeDtypeStruct(w.shape, F32)] * 3,
        compiler_params=_params(("arbitrary",)),
    )(w, g, m, v)


def adamw_columns_first(w, g, m, v, name):
    N = w.shape[0]
    chunk = 16
    main = N // chunk * chunk

    def body(w_hbm, g_hbm, m_hbm, v_hbm, d_hbm, mo_hbm, vo_hbm, wb, gb, mb, vb, db, mob, vob, sems):
        loads = [pltpu.make_async_copy(h.reshape(N, LANES), b, sems.at[k])
                 for k, (h, b) in enumerate(((w_hbm, wb), (g_hbm, gb), (m_hbm, mb), (v_hbm, vb)))]
        for cp in loads:
            cp.start()
        for cp in loads:
            cp.wait()

        def update(rows):
            gv = gb[rows, :]
            mn = ADAM_B1 * mb[rows, :] + (1.0 - ADAM_B1) * gv
            vn = ADAM_B2 * vb[rows, :] + (1.0 - ADAM_B2) * jnp.square(gv)
            m_hat = mn / (1.0 - ADAM_B1 ** ADAM_STEP)
            v_hat = vn / (1.0 - ADAM_B2 ** ADAM_STEP)
            db[rows, :] = -ADAM_LR * (m_hat / (jnp.sqrt(v_hat) + ADAM_EPS) + ADAM_WD * wb[rows, :])
            mob[rows, :] = mn
            vob[rows, :] = vn

        def step(i, _):
            update(pl.ds(pl.multiple_of(i * chunk, chunk), chunk))
            return 0

        lax.fori_loop(0, main // chunk, step, 0, unroll=4)
        if main < N:
            update(slice(main, N))
        stores = [pltpu.make_async_copy(b, h.reshape(N, LANES), sems.at[k])
                  for k, (h, b) in enumerate(((d_hbm, db), (mo_hbm, mob), (vo_hbm, vob)))]
        for cp in stores:
            cp.start()
        for cp in stores:
            cp.wait()

    any_spec = pl.BlockSpec(memory_space=pl.ANY)
    return pl.pallas_call(
        body, name=name,
        in_specs=[any_spec] * 4, out_specs=[any_spec] * 3,
        out_shape=[jax.ShapeDtypeStruct(w.shape, F32)] * 3,
        scratch_shapes=[pltpu.VMEM((N, LANES), F32)] * 7 + [pltpu.SemaphoreType.DMA((4,))],
        compiler_params=_params(),
    )(w, g, m, v)


def _pad_row(v, width):
    return jnp.concatenate([v, jnp.zeros((1, width - v.shape[1]), v.dtype)], axis=1)


def local_grads(x, mem, norm_g, b_forget, mem_norm_g, final_norm_g, loss_target, first_token, first_weights,
                late_weights, start_exchange):
    B, S, D = x.shape
    T = B * S
    xt = x.reshape(T, D)
    memt = mem.reshape(B * MEM_LEN, D)
    b_pad = _pad_row(b_forget, LANES)

    h, h_t = rms_fwd(xt, norm_g, 512, "rms_x", with_transpose=True, token=first_token)
    w_in_a, proj_token = first_weights(h)
    proj_a = mm_nn(h, w_in_a, 1024, PA, "in_proj_a", proj_token)
    proj_a3 = proj_a.reshape(B, S, PA)

    negc = fox_gate(proj_a3, b_pad)
    causal = _log_masks_t(S, "causal")
    dilated = _log_masks_t(S, "dilated")
    rope = _rope_tables(S)

    o_fox, lse_fox = attn_fwd("fox", proj_a3, S, negc_cols=negc, mask=causal)

    w_in_b, w_kv, w_out = late_weights(o_fox)
    proj_b = mm_nn(h, w_in_b, 1024, PB // 2, "in_proj_b")
    proj_b3 = proj_b.reshape(B, S, PB)
    o_dil, lse_dil = attn_fwd("dil", proj_b3, S, mask=dilated, rope=rope)

    mh, mh_t = rms_fwd(memt, mem_norm_g, B * MEM_LEN, "rms_mem", with_transpose=True)
    mkv = mm_nn(mh, w_kv, B * MEM_LEN, 2 * MEM_W, "mem_kv_proj")
    mkv3 = mkv.reshape(B, MEM_LEN, 2 * MEM_W)
    o_mem, lse_mem = attn_fwd("mem", proj_b3, S, kv=mkv3)

    dx2, do_fox, do_dil, do_mem, dfg, ddg, dmg, g_out, small_out = out_step(
        proj_b, o_fox.reshape(T, FOX_W), o_dil.reshape(T, DIL_W), o_mem.reshape(T, MEM_W), w_out,
        xt, loss_target.reshape(T, D), final_norm_g.reshape(1, D), 256)

    gates = [(dfg, 1, B_FG), (ddg, 1, B_DG), (dmg, 1, B_MG)]
    g_gates = mm_tn_multi(h_t, [piece[0] for piece in gates], 1024, "w_in_grad_gates", BF16)
    first, token = start_exchange([g_gates, g_out], "early_exchange_a")

    dqkv_fox, dneg, drow = attn_bwd("fox", proj_a3, do_fox.reshape(B, S, FOX_W), o_fox, lse_fox, S,
                                    negc_cols=negc, mask=causal, token=token)
    drow = drow.reshape(B, FOX_HEADS // 2, S // TQ, 2, TQ).transpose(0, 1, 3, 2, 4).reshape(B, FOX_HEADS, S)
    drow = jnp.pad(drow, ((0, 0), (0, LANES - FOX_HEADS), (0, 0)))
    dflog, db_part = fox_gate_bwd(drow, dneg, proj_a3, b_pad)
    fox = [(dqkv_fox.reshape(T, 3 * FOX_W), 0, A_FOX), (dflog.reshape(T, LANES), 0, A_FLOG)]
    g_fox = mm_tn_multi(h_t, [piece[0] for piece in fox], 1024, "w_in_grad_fox", BF16)
    second, token = start_exchange([g_fox], "early_exchange_b")

    (dqkv_dil,) = attn_bwd("dil", proj_b3, do_dil.reshape(B, S, DIL_W), o_dil, lse_dil, S, mask=dilated, rope=rope,
                           token=token)
    dil = [(dqkv_dil.reshape(T, 3 * DIL_W), 1, B_DIL)]
    g_dil = mm_tn_multi(h_t, [piece[0] for piece in dil], 1024, "w_in_grad_dil", BF16)
    third, token = start_exchange([g_dil], "early_exchange_c")

    dmq, dmk, dmv = attn_bwd("mem", proj_b3, do_mem.reshape(B, S, MEM_W), o_mem, lse_mem, S, kv=mkv3, token=token)
    mq = [(dmq.reshape(T, MEM_W), 1, B_MQ)]
    g_mq = mm_tn_multi(h_t, [piece[0] for piece in mq], 1024, "w_in_grad_mq", BF16)
    dmkv = jnp.concatenate([dmk, dmv], axis=2).reshape(B * MEM_LEN, 2 * MEM_W)
    g_kv = mm_tn_multi(mh_t, [dmkv], B * MEM_LEN, "w_kv_grad", BF16)
    fourth, token = start_exchange([g_mq, g_kv], "early_exchange_d")

    grad_x, dng = in_proj_bwd_rms(gates + fox + dil + mq, (w_in_a, w_in_b), xt, norm_g, dx2, 512, token)
    dmh = mm_nt(dmkv, w_kv, B * MEM_LEN, D, "mem_kv_bwd")
    _, dmng = rms_bwd(memt, mem_norm_g, dmh, None, B * MEM_LEN, "rms_mem_bwd")

    small = jnp.concatenate([dng[0:1], dmng[0:1], small_out[0:1], _pad_row(db_part[0:1], D), small_out[1:2],
                             jnp.zeros((3, D), F32)], axis=0)
    early = [(first, dqkv_fox), (second, dqkv_dil), (third, dmq), (fourth, grad_x)]
    return grad_x.reshape(B, S, D), early, small


def kernel(x, mem, norm_g, w_in, b_forget, mem_norm_g, w_mem_kv, w_out, final_norm_g, loss_target, m_norm_g, m_w_in, m_b_forget, m_mem_norm_g, m_w_mem_kv, m_w_out, m_final_norm_g, v_norm_g, v_w_in, v_b_forget, v_mem_norm_g, v_w_mem_kv, v_w_out, v_final_norm_g):
    D = D_MODEL
    shard_a, shard_b = _split_cols(_pack_cols(w_in).astype(BF16).reshape(w_in.shape[1], PW))
    gather_a, first_token = early_exchange_start([shard_a], "first_gather", gather=True,
                                                 relations=_SIBLING_AND_SAME_CORES)
    late = {}

    def first_weights(after):
        _, gathered = early_exchange_wait(gather_a, after, "first_gather_wait")
        (w_in_a,) = pass_on_to_sibling(gathered, gather_a["rows"], "first_gather_pass")
        late["gather"], token = early_exchange_start(
            [shard_b, w_mem_kv[0].astype(BF16), w_out[0].astype(BF16)], "late_gather", gather=True, after=w_in_a,
            relations=_SIBLING_AND_SAME_CORES)
        return w_in_a, token

    def late_weights(after):
        _, gathered = early_exchange_wait(late["gather"], after, "late_gather_wait")
        return pass_on_to_sibling(gathered, late["gather"]["rows"], "late_gather_pass")

    grad_x, early, small = local_grads(
        x, mem, norm_g, b_forget, mem_norm_g, final_norm_g, loss_target, first_token, first_weights, late_weights,
        early_exchange_start)

    (first, after_first), (second, after_second), (third, after_third), (fourth, after_fourth) = early
    (src_gates, src_out), (land_gates, land_out) = early_exchange_wait(first, after_first, "early_wait_a")
    (src_fox,), (land_fox,) = early_exchange_wait(second, after_second, "early_wait_b")
    (src_dil,), (land_dil,) = early_exchange_wait(third, after_third, "early_wait_c")
    (src_mq, src_kv), (land_mq, land_kv) = early_exchange_wait(fourth, after_fourth, "early_wait_d")
    gw_out = slot_sum8(src_out, land_out, 256, "sum_w_out")
    gw_kv = slot_sum8(src_kv, land_kv, 128, "sum_w_kv")
    gw_in_cols = w_in_grad_sum(
        [(src_fox, land_fox, [(0, P_FOX, 3 * FOX_W), (3 * FOX_W, P_FLOG, LANES)]),
         (src_gates, land_gates, [(0, P_FG, FOX_W), (FOX_W, P_DG, DIL_W), (FOX_W + DIL_W, P_MG, MEM_W)]),
         (src_dil, land_dil, [(0, P_DIL, 3 * DIL_W)]),
         (src_mq, land_mq, [(0, P_MQ, MEM_W)])], "sum_w_in")
    gw_in = jnp.transpose(gw_in_cols, (1, 2, 0))

    tot = small_all_reduce(small)

    loss = tot[4, 0]
    g_norm, g_mem_norm, g_final, g_b = tot[0:1], tot[1:2], tot[2], tot[3:4, :FOX_HEADS]

    def rows8(*rows):
        rows = [r.reshape(1, -1) for r in rows]
        rows = [_pad_row(r, D) for r in rows]
        return jnp.concatenate(rows + [jnp.zeros((8 - len(rows), D), F32)], axis=0)

    sw = rows8(norm_g, mem_norm_g, final_norm_g, b_forget)
    sm = rows8(m_norm_g, m_mem_norm_g, m_final_norm_g, m_b_forget)
    sv = rows8(v_norm_g, v_mem_norm_g, v_final_norm_g, v_b_forget)
    d_s, m_s, v_s = adamw(sw, tot, sm, sv, 8, "adamw_small")
    columns_first = lambda a: jnp.transpose(a, (2, 0, 1))
    d_in, m_in, v_in = [jnp.transpose(o, (1, 2, 0)) for o in adamw_columns_first(
        columns_first(w_in), gw_in_cols, columns_first(m_w_in), columns_first(v_w_in), "adamw_w_in")]
    d_kv, m_kv, v_kv = adamw(w_mem_kv[0], gw_kv, m_w_mem_kv[0], v_w_mem_kv[0], 128, "adamw_w_kv")
    d_out, m_out, v_out = adamw(w_out[0], gw_out, m_w_out[0], v_w_out[0], 256, "adamw_w_out")

    def small_outs(t):
        return t[0:1], t[3:4, :FOX_HEADS], t[1:2], t[2]

    grads = (g_norm, gw_in, g_b, g_mem_norm, gw_kv[None], gw_out[None], g_final)
    outs = []
    for t, big in ((d_s, (d_in, d_kv, d_out)), (m_s, (m_in, m_kv, m_out)), (v_s, (v_in, v_kv, v_out))):
        n, b, mn, f = small_outs(t)
        outs += [n, big[0], b, mn, big[1][None], big[2][None], f]
    return (loss, grad_x, *grads, *outs)
```

```python
import math

import numpy as np
import jax
import jax.numpy as jnp
from jax import lax
from jax.experimental import pallas as pl
from jax.experimental.pallas import tpu as pltpu

F32 = jnp.float32
BF16 = jnp.bfloat16

D_MODEL = 1024
HEAD_DIM = 64
FOX_HEADS = 12
DIL_HEADS = 12
MEM_HEADS = 4
MEM_HEAD_DIM = 128
MEM_LEN = 256
FOX_W = FOX_HEADS * HEAD_DIM
DIL_W = DIL_HEADS * HEAD_DIM
MEM_W = MEM_HEADS * MEM_HEAD_DIM
MIX_W = FOX_W + DIL_W + MEM_W
DILATIONS = ((128, 1), (512, 4), (2048, 16))
ROPE_THETA = 500000.0
ROPE_DIM = HEAD_DIM // 4
RMS_EPS = 1e-6
NEG_INF = -1e30
IN_W = 4 * FOX_W + FOX_HEADS + 4 * DIL_W + 2 * MEM_W

ADAM_LR = 0.001
ADAM_B1 = 0.9
ADAM_B2 = 0.999
ADAM_EPS = 1e-08
ADAM_WD = 0.01
ADAM_STEP = 10

N_DEV = 8
LANES = 128
PAIR_W = 3 * LANES
TQ = 256
TK = 256

O_FQ, O_FK, O_FV, O_FG = 0, FOX_W, 2 * FOX_W, 3 * FOX_W
O_FLOG = 4 * FOX_W
O_DQ = O_FLOG + FOX_HEADS
O_DK, O_DV, O_DG = O_DQ + DIL_W, O_DQ + 2 * DIL_W, O_DQ + 3 * DIL_W
O_MQ = O_DQ + 4 * DIL_W
O_MG = O_MQ + MEM_W
P_FOX = 0
P_FG = P_FOX + 3 * FOX_W
P_DIL = P_FG + FOX_W
P_DG = P_DIL + 3 * DIL_W
P_MQ = P_DG + DIL_W
P_MG = P_MQ + MEM_W
P_FLOG = P_MG + MEM_W
PW = P_FLOG + LANES
A_FOX = 0
A_FLOG = A_FOX + 3 * FOX_W
PA = A_FLOG + LANES
B_FG = 0
B_DG = B_FG + FOX_W
B_DIL = B_DG + DIL_W
B_MQ = B_DIL + 3 * DIL_W
B_MG = -(-(B_MQ + MEM_W) // MEM_W) * MEM_W
PB = B_MG + MEM_W

VMEM_LIMIT = 56 * 1024 * 1024


def _pack_pieces():
    pieces = []
    for base in (O_FQ, O_DQ):
        seg = []
        for hp in range(FOX_HEADS // 2):
            for part in range(3):
                seg.append((base + part * FOX_W + hp * LANES, LANES))
        pieces.append(seg)
    fox, dil = pieces
    return fox + [(O_FG, FOX_W)] + dil + [(O_DG, DIL_W), (O_MQ, MEM_W), (O_MG, MEM_W), (O_FLOG, FOX_HEADS)]


def _pack_cols(w):
    parts = [w[..., s:s + n] for s, n in _pack_pieces()]
    parts.append(jnp.zeros(w.shape[:-1] + (LANES - FOX_HEADS,), w.dtype))
    return jnp.concatenate(parts, axis=-1)


def _split_cols(wp):
    def cut(start, width):
        return wp[..., start:start + width]

    group_a = jnp.concatenate([cut(P_FOX, 3 * FOX_W), cut(P_FLOG, LANES)], axis=-1)
    pad = jnp.zeros(wp.shape[:-1] + (B_MG - B_MQ - MEM_W,), wp.dtype)
    group_b = jnp.concatenate([cut(P_FG, FOX_W), cut(P_DG, DIL_W), cut(P_DIL, 3 * DIL_W), cut(P_MQ, MEM_W), pad,
                               cut(P_MG, MEM_W)], axis=-1)
    return group_a, group_b


def _params(sem=None, **kw):
    return pltpu.CompilerParams(dimension_semantics=sem, vmem_limit_bytes=VMEM_LIMIT, **kw)


def _mesh_pos():
    return lax.axis_index("x"), lax.axis_index("y"), lax.axis_index("c")


def _flip(v, d):
    return 1 - v if d else v


_RELATIONS = [(dx, dy, dc) for dx in (0, 1) for dy in (0, 1) for dc in (0, 1)][1:]
_SIBLING_AND_SAME_CORES = [(0, 0, 1), (1, 0, 0), (0, 1, 0), (1, 1, 0)]


_OTHER_CHIPS = [(1, 0), (0, 1), (1, 1)]


def small_all_reduce(small):
    vmem_spec = pl.BlockSpec(memory_space=pltpu.VMEM)

    def body(small_ref, tot_ref, land, send_sems, recv_sems):
        x, y, c = _mesh_pos()
        me = 4 * x + 2 * y + c
        land[me] = small_ref[...]
        sends, recvs = [], []
        for j, (dx, dy, dc) in enumerate(_RELATIONS):
            px, py, pc = _flip(x, dx), _flip(y, dy), _flip(c, dc)
            common = dict(send_sem=send_sems.at[j], recv_sem=recv_sems.at[j],
                          device_id=(px, py, pc), device_id_type=pl.DeviceIdType.MESH)
            sends.append(pltpu.make_async_remote_copy(src_ref=small_ref, dst_ref=land.at[me], **common))
            recvs.append(pltpu.make_async_remote_copy(src_ref=small_ref, dst_ref=land.at[4 * px + 2 * py + pc], **common))
        for cp in sends:
            cp.start()
        for cp in recvs:
            cp.wait_recv()
        for cp in sends:
            cp.wait_send()
        tot = land[0]
        for d in range(1, N_DEV):
            tot = tot + land[d]
        tot_ref[...] = tot

    return pl.pallas_call(
        body, name="small_sum", out_shape=jax.ShapeDtypeStruct(small.shape, small.dtype),
        in_specs=[vmem_spec], out_specs=vmem_spec,
        scratch_shapes=[pltpu.VMEM((N_DEV,) + small.shape, small.dtype),
                        pltpu.SemaphoreType.DMA((len(_RELATIONS),)), pltpu.SemaphoreType.DMA((len(_RELATIONS),))],
    )(small)


_HBM = pl.BlockSpec(memory_space=pltpu.HBM)
_SEM = pl.BlockSpec(memory_space=pltpu.SEMAPHORE)
_EFFECT = pltpu.SideEffectType.DATAFLOW_SIDE_EFFECTING


def _early_copies(src_refs, land_refs, send_sems, recv_sems, rows, gather, relations):
    x, y, c = _mesh_pos()
    me = 4 * x + 2 * y + c
    copies = []
    for a in range(len(src_refs)):
        for dx, dy, dc in relations:
            px, py, pc = _flip(x, dx), _flip(y, dy), _flip(c, dc)
            peer = 4 * px + 2 * py + pc
            copies.append(pltpu.make_async_remote_copy(
                src_ref=src_refs[a] if gather else src_refs[a].at[pl.ds(peer * rows[a], rows[a]), :],
                dst_ref=land_refs[a].at[pl.ds(me * rows[a], rows[a]), :],
                send_sem=send_sems[a], recv_sem=recv_sems[a],
                device_id=(px, py, pc), device_id_type=pl.DeviceIdType.MESH))
    return copies


def own_slots(shards, name, after=None):
    n = len(shards)
    extra = [] if after is None else [after]
    x, y, c = _mesh_pos()
    me = (4 * x + 2 * y + c).astype(jnp.int32).reshape(1)
    empties = [lax.empty((N_DEV * s.shape[0], s.shape[1]), s.dtype) for s in shards]

    def body(me_ref, *refs):
        for a in range(n):
            refs[2 * n + len(extra) + a][...] = refs[a][...]

    return pl.pallas_call(
        body, name=name,
        grid_spec=pltpu.PrefetchScalarGridSpec(
            num_scalar_prefetch=1, grid=(1,),
            in_specs=[pl.BlockSpec(s.shape, lambda i, w: (0, 0)) for s in shards]
            + [pl.BlockSpec(memory_space=pl.ANY)] * (n + len(extra)),
            out_specs=[pl.BlockSpec(s.shape, lambda i, w: (w[0], 0)) for s in shards]),
        out_shape=[jax.ShapeDtypeStruct(e.shape, e.dtype) for e in empties],
        input_output_aliases={1 + n + a: a for a in range(n)},
        compiler_params=_params(("arbitrary",)),
    )(me, *shards, *empties, *extra)


def early_exchange_start(srcs, name, gather=False, after=None, relations=_RELATIONS, lands=None):
    n = len(srcs)
    if gather:
        rows = [s.shape[0] for s in srcs]
        lands = list(own_slots(srcs, name + "_place") if lands is None else lands)
    else:
        rows = [s.shape[0] // N_DEV for s in srcs]
        lands = [lax.empty(s.shape, s.dtype) for s in srcs]

    extra = [] if after is None else [after]

    def body(*refs):
        src_refs, land_refs = refs[:n], refs[n:2 * n]
        first_sem = 2 * n + len(extra)
        send_sems, recv_sems = refs[first_sem:first_sem + n], refs[first_sem + n:first_sem + 2 * n]
        token = refs[-1]
        for cp in _early_copies(src_refs, land_refs, send_sems, recv_sems, rows, gather, relations):
            cp.start()
        token[...] = jnp.zeros_like(token)

    hbm = lambda a: pltpu.HBM(a.shape, a.dtype)
    outs = pl.pallas_call(
        body, name=name,
        out_shape=[pltpu.SemaphoreType.DMA(())] * (2 * n)
        + [hbm(a) for a in srcs] + [hbm(a) for a in lands] + [jax.ShapeDtypeStruct((8, LANES), F32)],
        in_specs=[_HBM] * (2 * n) + [pl.BlockSpec(memory_space=pl.ANY)] * len(extra),
        out_specs=[_SEM] * (2 * n) + [_HBM] * (2 * n) + [pl.BlockSpec(memory_space=pltpu.VMEM)],
        input_output_aliases={i: 2 * n + i for i in range(2 * n)},
        compiler_params=pltpu.CompilerParams(has_side_effects=_EFFECT),
    )(*[pltpu.with_memory_space_constraint(a, pltpu.HBM) for a in list(srcs) + lands], *extra)
    handle = dict(sems=outs[:2 * n], srcs=outs[2 * n:3 * n], lands=outs[3 * n:4 * n], rows=rows,
                  copies=len(relations))
    return handle, outs[-1]


def early_exchange_wait(handle, after, name):
    n = len(handle["srcs"])
    rows = handle["rows"]
    after = list(after) if isinstance(after, (list, tuple)) else [after]

    def body(*refs):
        src_refs, land_refs = refs[:n], refs[n:2 * n]
        send_sems, recv_sems = refs[2 * n:3 * n], refs[3 * n:4 * n]
        x, y, c = _mesh_pos()
        for a in range(n):
            span = pl.ds(0, handle["copies"] * rows[a])
            all_copies = pltpu.make_async_remote_copy(
                src_ref=land_refs[a].at[span, :], dst_ref=land_refs[a].at[span, :],
                send_sem=send_sems[a], recv_sem=recv_sems[a],
                device_id=(x, y, c), device_id_type=pl.DeviceIdType.MESH)
            all_copies.wait_send()
            all_copies.wait_recv()

    hbm = lambda a: pltpu.HBM(a.shape, a.dtype)
    ins = list(handle["srcs"]) + list(handle["lands"])
    outs = pl.pallas_call(
        body, name=name,
        out_shape=[hbm(a) for a in ins],
        in_specs=[_HBM] * (2 * n) + [_SEM] * (2 * n) + [pl.BlockSpec(memory_space=pl.ANY)] * len(after),
        out_specs=[_HBM] * (2 * n),
        input_output_aliases={i: i for i in range(2 * n)},
        compiler_params=pltpu.CompilerParams(has_side_effects=_EFFECT),
    )(*ins, *handle["sems"], *after)
    return outs[:n], outs[n:]


def pass_on_to_sibling(lands, rows, name):
    n = len(lands)

    def body(*refs):
        land_refs = refs[n:2 * n]
        send_sems, recv_sems = refs[2 * n:]
        x, y, c = _mesh_pos()
        copies = []
        for a in range(n):
            for k, (dx, dy) in enumerate(_OTHER_CHIPS):
                slot = 4 * _flip(x, dx) + 2 * _flip(y, dy) + c
                blk = land_refs[a].at[pl.ds(slot * rows[a], rows[a]), :]
                copies.append(pltpu.make_async_remote_copy(
                    src_ref=blk, dst_ref=blk, send_sem=send_sems.at[a, k], recv_sem=recv_sems.at[a, k],
                    device_id=(x, y, 1 - c), device_id_type=pl.DeviceIdType.MESH))
        for cp in copies:
            cp.start()
        for cp in copies:
            cp.wait_recv()
        for cp in copies:
            cp.wait_send()

    any_spec = pl.BlockSpec(memory_space=pl.ANY)
    return pl.pallas_call(
        body, name=name,
        out_shape=[jax.ShapeDtypeStruct(a.shape, a.dtype) for a in lands],
        in_specs=[any_spec] * n, out_specs=[any_spec] * n,
        input_output_aliases={i: i for i in range(n)},
        scratch_shapes=[pltpu.SemaphoreType.DMA((n, len(_OTHER_CHIPS))), pltpu.SemaphoreType.DMA((n, len(_OTHER_CHIPS)))],
    )(*lands)


def slot_sum8(src, land, tr, name):
    rows, cols = land.shape[0] // N_DEV, land.shape[1]
    x, y, c = _mesh_pos()
    me = (4 * x + 2 * y + c).astype(jnp.int32).reshape(1)

    def body(me_ref, src_ref, land_ref, o_ref):
        acc = None
        for d in range(N_DEV):
            term = jnp.where(d == me_ref[0], src_ref[0], land_ref[d]).astype(F32)
            acc = term if acc is None else acc + term
        o_ref[...] = acc

    return pl.pallas_call(
        body, name=name,
        grid_spec=pltpu.PrefetchScalarGridSpec(
            num_scalar_prefetch=1, grid=(rows // tr,),
            in_specs=[pl.BlockSpec((1, tr, cols), lambda i, w: (w[0], i, 0)),
                      pl.BlockSpec((N_DEV, tr, cols), lambda i, w: (0, i, 0))],
            out_specs=pl.BlockSpec((tr, cols), lambda i, w: (i, 0))),
        out_shape=jax.ShapeDtypeStruct((rows, cols), F32),
        compiler_params=_params(("arbitrary",)),
    )(me, src.reshape(N_DEV, rows, cols), land.reshape(N_DEV, rows, cols))


def w_in_grad_sum(groups, name):
    rows = groups[0][0].shape[0] // N_DEV
    runs, pos = [], 0
    for start, width in _pack_pieces():
        runs.append((start, width, pos))
        pos += width
    n = len(groups)

    def body(*refs):
        src_refs, land_refs = refs[0:2 * n:2], refs[1:2 * n:2]
        g_ref = refs[2 * n]
        own_bufs, land_bufs = refs[2 * n + 1:3 * n + 1], refs[3 * n + 1:4 * n + 1]
        stage, load_sems, store_sems = refs[4 * n + 1:]
        x, y, c = _mesh_pos()
        me = 4 * x + 2 * y + c
        loads = []
        for k in range(n):
            pair = [pltpu.make_async_copy(src_refs[k].at[pl.ds(me * rows, rows), :], own_bufs[k], load_sems.at[k, 0]),
                    pltpu.make_async_copy(land_refs[k], land_bufs[k], load_sems.at[k, 1])]
            for cp in pair:
                cp.start()
            loads.append(pair)
        for k, (_, _, segments) in enumerate(groups):
            for cp in loads[k]:
                cp.wait()
            for first, packed, width in segments:
                for off in range(0, width, LANES):
                    cols = slice(first + off, first + off + LANES)
                    acc = None
                    for d in range(N_DEV):
                        term = jnp.where(d == me, own_bufs[k][:, cols], land_bufs[k][d * rows:(d + 1) * rows, cols])
                        acc = term.astype(F32) if acc is None else acc + term.astype(F32)
                    stage[packed + off:packed + off + LANES, :] = acc.T
        g_rows = g_ref.reshape(IN_W, LANES)
        stores = [pltpu.make_async_copy(stage.at[pl.ds(p, width), :], g_rows.at[pl.ds(start, width), :], store_sems.at[r])
                  for r, (start, width, p) in enumerate(runs)]
        for cp in stores:
            cp.start()
        for cp in stores:
            cp.wait()

    any_spec = pl.BlockSpec(memory_space=pl.ANY)
    operands = [a for src, land, _ in groups for a in (src, land)]
    return pl.pallas_call(
        body, name=name,
        in_specs=[any_spec] * (2 * n), out_specs=any_spec,
        out_shape=jax.ShapeDtypeStruct((IN_W, 1, LANES), F32),
        scratch_shapes=[pltpu.VMEM((rows, src.shape[1]), src.dtype) for src, _, _ in groups]
        + [pltpu.VMEM(land.shape, land.dtype) for _, land, _ in groups]
        + [pltpu.VMEM((PW, LANES), F32), pltpu.SemaphoreType.DMA((n, 2)), pltpu.SemaphoreType.DMA((len(runs),))],
        compiler_params=_params(),
    )(*operands)


def mm_tn_multi(a_t, bs, tt, name, out_dtype=F32):
    K, T = a_t.shape
    widths = [b.shape[1] for b in bs]
    steps = T // tt

    def body(a_ref, *rest):
        b_refs, o_ref, acc = rest[:-2], rest[-2], rest[-1]

        @pl.when(pl.program_id(0) == 0)
        def _():
            acc[...] = jnp.zeros(acc.shape, F32)

        av = a_ref[...]
        col = 0
        for b_ref, w in zip(b_refs, widths):
            acc[:, col:col + w] += jnp.dot(av, b_ref[...], preferred_element_type=F32)
            col += w

        @pl.when(pl.program_id(0) == steps - 1)
        def _():
            o_ref[...] = acc[...].astype(out_dtype)

    return pl.pallas_call(
        body, name=name, grid=(steps,),
        in_specs=[pl.BlockSpec((K, tt), lambda t: (0, t))] + [pl.BlockSpec((tt, w), lambda t: (t, 0)) for w in widths],
        out_specs=pl.BlockSpec((K, sum(widths)), lambda t: (0, 0)),
        out_shape=jax.ShapeDtypeStruct((K, sum(widths)), out_dtype),
        scratch_shapes=[pltpu.VMEM((K, sum(widths)), F32)],
        compiler_params=_params(("arbitrary",)),
    )(a_t, *bs)


def rms_fwd(x, g, tm, name, with_transpose=False, token=None):
    M, K = x.shape
    extra = [] if token is None else [token]

    def body(x_ref, g_ref, *rest):
        o_ref = rest[len(extra)]
        xv = x_ref[...]
        r = lax.rsqrt(jnp.mean(xv * xv, axis=-1, keepdims=True) + RMS_EPS)
        h = ((xv * r) * g_ref[...]).astype(BF16)
        o_ref[...] = h
        if with_transpose:
            rest[len(extra) + 1][...] = h.T

    out_specs = [pl.BlockSpec((tm, K), lambda i: (i, 0))]
    out_shape = [jax.ShapeDtypeStruct((M, K), BF16)]
    if with_transpose:
        out_specs.append(pl.BlockSpec((K, tm), lambda i: (0, i)))
        out_shape.append(jax.ShapeDtypeStruct((K, M), BF16))
    outs = pl.pallas_call(
        body, name=name, grid=(M // tm,),
        in_specs=[pl.BlockSpec((tm, K), lambda i: (i, 0)), pl.BlockSpec((1, K), lambda i: (0, 0))]
        + [pl.BlockSpec(t.shape, lambda i: (0, 0)) for t in extra],
        out_specs=out_specs, out_shape=out_shape,
        compiler_params=_params(("arbitrary",)),
    )(x, g, *extra)
    return outs if with_transpose else outs[0]


def rms_bwd(x, g, dh, dres, tm, name):
    M, K = x.shape
    has_res = dres is not None

    def body(*refs):
        if has_res:
            x_ref, g_ref, dh_ref, dres_ref, dx_ref, dg_ref = refs
        else:
            x_ref, g_ref, dh_ref, dx_ref, dg_ref = refs
        xv = x_ref[...]
        r = lax.rsqrt(jnp.mean(xv * xv, axis=-1, keepdims=True) + RMS_EPS)
        xn = xv * r
        dhv = dh_ref[...]
        dxn = dhv * g_ref[...]
        dx = r * (dxn - xn * jnp.mean(dxn * xn, axis=-1, keepdims=True))
        if has_res:
            dx = dx + dres_ref[...]
        dx_ref[...] = dx
        part = jnp.sum(dhv * xn, axis=0, keepdims=True)
        row = lax.broadcasted_iota(jnp.int32, (8, K), 0)
        upd = jnp.where(row == 0, part, 0.0)

        @pl.when(pl.program_id(0) == 0)
        def _():
            dg_ref[...] = upd

        @pl.when(pl.program_id(0) != 0)
        def _():
            dg_ref[...] += upd

    row_spec = pl.BlockSpec((tm, K), lambda i: (i, 0))
    ins = [x, g, dh] + ([dres] if has_res else [])
    in_specs = [row_spec, pl.BlockSpec((1, K), lambda i: (0, 0)), row_spec] + ([row_spec] if has_res else [])
    return pl.pallas_call(
        body, name=name, grid=(M // tm,),
        in_specs=in_specs,
        out_specs=[row_spec, pl.BlockSpec((8, K), lambda i: (0, 0))],
        out_shape=[jax.ShapeDtypeStruct((M, K), F32), jax.ShapeDtypeStruct((8, K), F32)],
        compiler_params=_params(("arbitrary",)),
    )(*ins)


def mm_nn(a, b, tm, tn, name, token=None):
    M, K = a.shape
    N = b.shape[1]
    extra = [] if token is None else [token]

    def body(a_ref, b_ref, *rest):
        rest[-1][...] = jnp.dot(a_ref[...], b_ref[...], preferred_element_type=F32)

    return pl.pallas_call(
        body, name=name, grid=(N // tn, M // tm),
        in_specs=[pl.BlockSpec((tm, K), lambda j, i: (i, 0)), pl.BlockSpec((K, tn), lambda j, i: (0, j))]
        + [pl.BlockSpec(t.shape, lambda j, i: (0, 0)) for t in extra],
        out_specs=pl.BlockSpec((tm, tn), lambda j, i: (i, j)),
        out_shape=jax.ShapeDtypeStruct((M, N), F32),
        compiler_params=_params(("arbitrary", "arbitrary")),
    )(a, b, *extra)


def mm_nt(a, b, tm, tk, name):
    M, K = a.shape
    N = b.shape[0]

    def body(a_ref, b_ref, o_ref):
        part = lax.dot_general(a_ref[...], b_ref[...], (((1,), (1,)), ((), ())), preferred_element_type=F32)

        @pl.when(pl.program_id(1) == 0)
        def _():
            o_ref[...] = part

        @pl.when(pl.program_id(1) != 0)
        def _():
            o_ref[...] += part

    return pl.pallas_call(
        body, name=name, grid=(M // tm, K // tk),
        in_specs=[pl.BlockSpec((tm, tk), lambda i, k: (i, k)), pl.BlockSpec((N, tk), lambda i, k: (0, k))],
        out_specs=pl.BlockSpec((tm, N), lambda i, k: (i, 0)),
        out_shape=jax.ShapeDtypeStruct((M, N), F32),
        compiler_params=_params(("arbitrary", "arbitrary")),
    )(a, b)


def in_proj_bwd_rms(pieces, ws, x, g, dres, tm, token):
    M, N = x.shape

    def body(*refs):
        n = len(pieces)
        p_refs, w_refs = refs[:n], refs[n:n + len(ws)]
        x_ref, g_ref, dres_ref, _, dx_ref, dg_ref = refs[n + len(ws):]
        dh = None
        for p_ref, (arr, group, col) in zip(p_refs, pieces):
            part = lax.dot_general(p_ref[...], w_refs[group][:, col:col + arr.shape[1]], (((1,), (1,)), ((), ())),
                                   preferred_element_type=F32)
            dh = part if dh is None else dh + part
        xv = x_ref[...]
        r = lax.rsqrt(jnp.mean(xv * xv, axis=-1, keepdims=True) + RMS_EPS)
        xn = xv * r
        dxn = dh * g_ref[...]
        dx_ref[...] = r * (dxn - xn * jnp.mean(dxn * xn, axis=-1, keepdims=True)) + dres_ref[...]
        row = lax.broadcasted_iota(jnp.int32, (8, N), 0)
        upd = jnp.where(row == 0, jnp.sum(dh * xn, axis=0, keepdims=True), 0.0)

        @pl.when(pl.program_id(0) == 0)
        def _():
            dg_ref[...] = upd

        @pl.when(pl.program_id(0) != 0)
        def _():
            dg_ref[...] += upd

    row_spec = pl.BlockSpec((tm, N), lambda i: (i, 0))
    return pl.pallas_call(
        body, name="in_proj_bwd", grid=(M // tm,),
        in_specs=[pl.BlockSpec((tm, arr.shape[1]), lambda i: (i, 0)) for arr, _, _ in pieces]
        + [pl.BlockSpec(w.shape, lambda i: (0, 0), pipeline_mode=pl.Buffered(1)) for w in ws]
        + [row_spec, pl.BlockSpec((1, N), lambda i: (0, 0)), row_spec, pl.BlockSpec(token.shape, lambda i: (0, 0))],
        out_specs=[row_spec, pl.BlockSpec((8, N), lambda i: (0, 0))],
        out_shape=[jax.ShapeDtypeStruct((M, N), F32), jax.ShapeDtypeStruct((8, N), F32)],
        compiler_params=_params(("arbitrary",)),
    )(*[arr for arr, _, _ in pieces], *ws, x, g, dres, token)


def _log_sigmoid(z):
    return jnp.minimum(z, 0.0) - jnp.log(1.0 + jnp.exp(-jnp.abs(z)))


def _tri(n, lower):
    r = lax.broadcasted_iota(jnp.int32, (n, n), 0)
    c = lax.broadcasted_iota(jnp.int32, (n, n), 1)
    return jnp.where((r >= c) if lower else (r <= c), 1.0, 0.0).astype(F32)


def fox_gate(proj3, b_pad):
    B, S, _ = proj3.shape
    nblk = S // TK

    def body(f_ref, b_ref, o_ref):
        tri = _tri(TK, True)
        carry = jnp.zeros((1, LANES), F32)
        for n in range(nblk):
            z = f_ref[0, n * TK:(n + 1) * TK, :] + b_ref[...]
            logf = _log_sigmoid(z)
            cs = jnp.dot(tri, logf, preferred_element_type=F32, precision=lax.Precision.HIGHEST) + carry
            carry = cs[TK - 1:TK, :]
            o_ref[0, n * TK:(n + 1) * TK, :] = -cs

    return pl.pallas_call(
        body, name="fox_gate", grid=(B,),
        in_specs=[pl.BlockSpec((1, S, LANES), lambda b: (b, 0, A_FLOG // LANES)),
                  pl.BlockSpec((1, LANES), lambda b: (0, 0))],
        out_specs=pl.BlockSpec((1, S, LANES), lambda b: (b, 0, 0)),
        out_shape=jax.ShapeDtypeStruct((B, S, LANES), F32),
        compiler_params=_params(("arbitrary",)),
    )(proj3, b_pad)


def fox_gate_bwd(drow, dneg, proj3, b_pad):
    B, S, _ = proj3.shape
    nblk = S // TK

    def body(d_ref, r_ref, f_ref, b_ref, o_ref, db_ref):
        tri = _tri(TK, False)
        lane = lax.broadcasted_iota(jnp.int32, (TK, LANES), 1)
        carry = jnp.zeros((1, LANES), F32)
        dbsum = jnp.zeros((1, LANES), F32)
        for n in reversed(range(nblk)):
            dk_side = None
            for hp in range(FOX_HEADS // 2):
                two = jnp.where(lane < 2, r_ref[0, n * TK:(n + 1) * TK, hp * LANES:(hp + 1) * LANES], 0.0)
                two = pltpu.roll(two, 2 * hp, 1) if hp else two
                dk_side = two if dk_side is None else dk_side + two
            dc = jnp.where(lane < FOX_HEADS, d_ref[0, :, n * TK:(n + 1) * TK].T - dk_side, 0.0)
            rs = jnp.dot(tri, dc, preferred_element_type=F32, precision=lax.Precision.HIGHEST) + carry
            carry = rs[0:1, :]
            z = f_ref[0, n * TK:(n + 1) * TK, :] + b_ref[...]
            dz = rs * (1.0 / (1.0 + jnp.exp(z)))
            o_ref[0, n * TK:(n + 1) * TK, :] = dz.astype(BF16)
            dbsum = dbsum + jnp.sum(dz, axis=0, keepdims=True)
        row = lax.broadcasted_iota(jnp.int32, (8, LANES), 0)
        upd = jnp.where(row == 0, dbsum, 0.0)

        @pl.when(pl.program_id(0) == 0)
        def _():
            db_ref[...] = upd

        @pl.when(pl.program_id(0) != 0)
        def _():
            db_ref[...] += upd

    return pl.pallas_call(
        body, name="fox_gate_bwd", grid=(B,),
        in_specs=[pl.BlockSpec((1, LANES, S), lambda b: (b, 0, 0)),
                  pl.BlockSpec((1, S, FOX_W), lambda b: (b, 0, 0)),
                  pl.BlockSpec((1, S, LANES), lambda b: (b, 0, A_FLOG // LANES)),
                  pl.BlockSpec((1, LANES), lambda b: (0, 0))],
        out_specs=[pl.BlockSpec((1, S, LANES), lambda b: (b, 0, 0)), pl.BlockSpec((8, LANES), lambda b: (0, 0))],
        out_shape=[jax.ShapeDtypeStruct((B, S, LANES), BF16), jax.ShapeDtypeStruct((8, LANES), F32)],
        compiler_params=_params(("arbitrary",)),
    )(drow, dneg, proj3, b_pad)


def _rope_tables(S):
    half = ROPE_DIM // 2
    f32 = np.float32
    pos = np.arange(S, dtype=f32)
    inv_freq = f32(1.0) / np.power(f32(ROPE_THETA), np.arange(0, ROPE_DIM, 2, dtype=f32) / f32(ROPE_DIM)).astype(f32)
    ang = (pos[:, None] * inv_freq[None, :]).astype(f32).astype(np.float64)
    cos, sin = np.cos(ang).astype(f32), np.sin(ang).astype(f32)
    one = np.ones((S, HEAD_DIM - ROPE_DIM), f32)
    zero = np.zeros((S, HEAD_DIM - ROPE_DIM), f32)
    zh = np.zeros((S, half), f32)
    c = np.concatenate([cos, cos, one], axis=1)
    s1 = np.concatenate([-sin, zh, zero], axis=1)
    s2 = np.concatenate([zh, sin, zero], axis=1)
    return tuple(jnp.asarray(np.concatenate([t, t], axis=1)) for t in (c, s1, s2))


_HALF_ROPE = ROPE_DIM // 2


def _rope(t, c, s1, s2):
    return t * c + pltpu.roll(t, LANES - _HALF_ROPE, 1) * s1 + pltpu.roll(t, _HALF_ROPE, 1) * s2


def _rope_bwd(d, c, s1, s2):
    return d * c + pltpu.roll(d * s1, _HALF_ROPE, 1) + pltpu.roll(d * s2, LANES - _HALF_ROPE, 1)


def _scale_parts(scale):
    m, _ = math.frexp(scale)
    return (scale, None) if m == 0.5 else (None, scale)


def _log_masks(S, kind):
    nd = 1 if kind == "causal" else S // TQ
    a = np.arange(TQ)[:, None]
    b = np.arange(TK)[None, :]
    out = np.zeros((nd, TQ, TK), np.float32)
    for d in range(nd):
        delta = d * TQ + a - b
        if kind == "causal":
            m = (delta >= 0).astype(np.float64)
        else:
            m = sum(((delta >= 0) & (delta % dil == 0) & (delta <= w)).astype(np.float64) for w, dil in DILATIONS)
        out[d] = np.where(m > 0, np.log(np.maximum(m, 1.0)), NEG_INF)
    return jnp.asarray(out)


def _attn_setup(kind):
    pair = kind != "mem"
    e_dim = HEAD_DIM if pair else MEM_HEAD_DIM
    q_fold, s_scale = _scale_parts(1.0 / math.sqrt(e_dim))
    return dict(pair=pair, col0={"fox": A_FOX, "dil": B_DIL, "mem": B_MQ}[kind],
                n_blocks=FOX_HEADS // 2 if pair else MEM_HEADS, q_fold=q_fold, s_scale=s_scale,
                nh=2 if pair else 1)


def _cat(parts, axis):
    return parts[0] if len(parts) == 1 else jnp.concatenate(parts, axis=axis)


def _log_masks_t(S, kind):
    return jnp.swapaxes(_log_masks(S, kind), 1, 2)


def _head_rows(hh, pair):
    row = lax.broadcasted_iota(jnp.int32, (LANES, 1), 0)
    if not pair:
        return row >= 0
    return (row >= HEAD_DIM * hh) & (row < HEAD_DIM * (hh + 1))


def _attn_t_inputs(kind, src, S, negc_cols, mask, rope, kv):
    cfg = _attn_setup(kind)
    col0 = cfg["col0"]
    ins, in_specs = [], []
    if cfg["pair"]:
        ins.append(src)
        in_specs.append(pl.BlockSpec((1, S, PAIR_W), lambda b, h: (b, 0, col0 // PAIR_W + h)))
    else:
        ins += [src, kv, kv]
        in_specs += [pl.BlockSpec((1, S, LANES), lambda b, h: (b, 0, col0 // LANES + h)),
                     pl.BlockSpec((1, MEM_LEN, LANES), lambda b, h: (b, 0, h)),
                     pl.BlockSpec((1, MEM_LEN, LANES), lambda b, h: (b, 0, MEM_HEADS + h))]
    if negc_cols is not None:
        ins.append(negc_cols)
        in_specs.append(pl.BlockSpec((1, S, LANES), lambda b, h: (b, 0, 0)))
    if mask is not None:
        ins.append(mask)
        in_specs.append(pl.BlockSpec(mask.shape, lambda b, h: (0, 0, 0)))
    if rope is not None:
        ins += list(rope)
        in_specs += [pl.BlockSpec((S, LANES), lambda b, h: (0, 0))] * 3
    return ins, in_specs


def _attn_t_prep(cfg, refs, S, Sk, *, qT2s, ks, vs=None, vTs=None, kTs=None, nb=None):
    pair, nh = cfg["pair"], cfg["nh"]
    lane = lax.broadcasted_iota(jnp.int32, (1, LANES), 1)
    rope_refs = refs["rope"]

    def prep_q(n):
        rows = slice(n * TQ, (n + 1) * TQ)
        q = refs["load_q"](rows)
        if rope_refs is not None:
            q = _rope(q, *[t[rows, :] for t in rope_refs])
        if cfg["q_fold"] is not None:
            q = q * cfg["q_fold"]
        qtb = q.astype(BF16).T
        for hh in range(nh):
            qT2s[n, :, hh * TQ:(hh + 1) * TQ] = jnp.where(_head_rows(hh, pair), qtb, jnp.zeros_like(qtb))

    def prep_kv(n):
        rows = slice(n * TK, (n + 1) * TK)
        k, v = refs["load_kv"](rows)
        if rope_refs is not None:
            k = _rope(k, *[t[rows, :] for t in rope_refs])
        kb = k.astype(BF16)
        vb = v.astype(BF16)
        ks[rows, :] = kb
        if vs is not None:
            vs[rows, :] = vb
        if vTs is not None:
            vTs[n] = vb.T
        if kTs is not None:
            kTs[n] = kb.T
        if nb is not None:
            blk = refs["negc"][0, rows, :]
            for hh in range(nh):
                h = 2 * refs["block"] + hh
                col = jnp.sum(jnp.where(lane == h, blk, 0.0), axis=1, keepdims=True)
                nb[hh, rows, :] = jnp.broadcast_to(col, (TK, LANES))

    for n in range(S // TQ):
        prep_q(n)
    for n in range(Sk // TK):
        prep_kv(n)


def _raw_scores_t(cfg, k, qT2):
    sT = jnp.dot(k, qT2, preferred_element_type=F32)
    if cfg["s_scale"] is not None:
        sT = sT * cfg["s_scale"]
    return sT


def _bias_mask_t(cfg, sT, nb, mask_ref, kc, midx):
    nh = cfg["nh"]
    if nb is None and midx is None:
        return sT
    parts = []
    for hh in range(nh):
        t = sT[:, hh * TQ:(hh + 1) * TQ]
        if nb is not None:
            t = t + jnp.concatenate([nb[hh, kc, :]] * (TQ // LANES), axis=1)
        if midx is not None:
            t = t + mask_ref[midx]
        parts.append(t)
    return _cat(parts, 1)


def _tile_pairs(kind, nq, nk):
    if kind == "mem":
        return [(i, j) for i in range(nq) for j in range(nk)], (lambda i, j: None)
    pairs = [(i, j) for i in range(nq) for j in range(i + 1)]
    if kind == "fox":
        return pairs, (lambda i, j: 0 if j == i else None)
    return pairs, (lambda i, j: i - j)


def attn_fwd(kind, src, S, *, negc_cols=None, mask=None, rope=None, kv=None):
    B = src.shape[0]
    cfg = _attn_setup(kind)
    pair, nh = cfg["pair"], cfg["nh"]
    Sk = S if pair else MEM_LEN
    has_bias, has_rope = negc_cols is not None, rope is not None
    R = nh * TQ
    nq, nk = S // TQ, Sk // TK
    pairs, mask_index = _tile_pairs(kind, nq, nk)

    def body(*refs):
        refs = list(refs)
        if pair:
            qkv_ref = refs.pop(0)
            load_q = lambda rows: qkv_ref[0, rows, 0:LANES]
            load_kv = lambda rows: (qkv_ref[0, rows, LANES:2 * LANES], qkv_ref[0, rows, 2 * LANES:3 * LANES])
        else:
            q_ref, k_ref, v_ref = refs.pop(0), refs.pop(0), refs.pop(0)
            load_q = lambda rows: q_ref[0, rows, :]
            load_kv = lambda rows: (k_ref[0, rows, :], v_ref[0, rows, :])
        negc_ref = refs.pop(0) if has_bias else None
        mask_ref = refs.pop(0) if mask is not None else None
        rope_refs = [refs.pop(0) for _ in range(3)] if has_rope else None
        o_ref, lse_ref, qT2s, ks, vTs, s_a, s_b, p_a, p_b = refs[:9]
        nb = refs[9] if has_bias else None
        _attn_t_prep(cfg, dict(load_q=load_q, load_kv=load_kv, rope=rope_refs, negc=negc_ref,
                               block=pl.program_id(1)), S, Sk, qT2s=qT2s, ks=ks, vTs=vTs, nb=nb)

        def cols(j):
            return slice(j * TK, (j + 1) * TK)

        def scores(i, j):
            return _raw_scores_t(cfg, ks[cols(j), :], qT2s[i])

        def finish(i, m, l, accT):
            oT2 = accT / l
            oT = jnp.where(_head_rows(0, True), oT2[:, 0:TQ], oT2[:, TQ:2 * TQ]) if pair else oT2
            o_ref[0, i * TQ:(i + 1) * TQ, :] = oT.T
            lse_ref[0, 0, i:i + 1, :] = m + jnp.log(l)

        s_bufs, p_bufs = (s_a, s_b), (p_a, p_b)
        s_bufs[0][...] = scores(*pairs[0])
        m = l = accT = None
        for t, (i, j) in enumerate(pairs):
            cur, oth = t % 2, 1 - t % 2
            if t > 0:
                i_prev, j_prev = pairs[t - 1]
                pv = jnp.dot(vTs[j_prev], p_bufs[oth][...], preferred_element_type=F32)
                acc_full = pv if accT is None else accT + pv
            if t + 1 < len(pairs):
                s_bufs[oth][...] = scores(*pairs[t + 1])
            first = j == 0
            if first and t > 0:
                finish(i_prev, m, l, acc_full)
            sT = _bias_mask_t(cfg, s_bufs[cur][...], nb, mask_ref, cols(j), mask_index(i, j))
            m_tile = jnp.max(sT, axis=0, keepdims=True)
            m_new = m_tile if first else jnp.maximum(m, m_tile)
            p = jnp.exp(sT - m_new)
            p_bufs[cur][...] = p.astype(BF16)
            if first:
                l, accT = jnp.sum(p, axis=0, keepdims=True), None
            else:
                alpha = jnp.exp(m - m_new)
                l, accT = alpha * l + jnp.sum(p, axis=0, keepdims=True), acc_full * alpha
            m = m_new
        i_last, j_last = pairs[-1]
        pv = jnp.dot(vTs[j_last], p_bufs[(len(pairs) - 1) % 2][...], preferred_element_type=F32)
        finish(i_last, m, l, pv if accT is None else accT + pv)

    ins, in_specs = _attn_t_inputs(kind, src, S, negc_cols, mask, rope, kv)
    W = cfg["n_blocks"] * LANES
    scratch = [pltpu.VMEM((nq, LANES, R), BF16), pltpu.VMEM((Sk, LANES), BF16), pltpu.VMEM((nk, LANES, TK), BF16),
               pltpu.VMEM((TK, R), F32), pltpu.VMEM((TK, R), F32), pltpu.VMEM((TK, R), BF16), pltpu.VMEM((TK, R), BF16)]
    if has_bias:
        scratch.append(pltpu.VMEM((nh, Sk, LANES), F32))
    return pl.pallas_call(
        body, name=kind + "_attn_fwd", grid=(B, cfg["n_blocks"]),
        in_specs=in_specs,
        out_specs=[pl.BlockSpec((1, S, LANES), lambda b, h: (b, 0, h)),
                   pl.BlockSpec((1, 1, nq, R), lambda b, h: (b, h, 0, 0))],
        out_shape=[jax.ShapeDtypeStruct((B, S, W), F32), jax.ShapeDtypeStruct((B, cfg["n_blocks"], nq, R), F32)],
        scratch_shapes=scratch,
        compiler_params=_params(("arbitrary", "arbitrary")),
    )(*ins)


def attn_bwd(kind, src, do, o, lse, S, *, negc_cols=None, mask=None, rope=None, kv=None, token=None):
    B = src.shape[0]
    cfg = _attn_setup(kind)
    pair, nh, s_scale, q_fold = cfg["pair"], cfg["nh"], cfg["s_scale"], cfg["q_fold"]
    Sk = S if pair else MEM_LEN
    has_bias, has_rope = negc_cols is not None, rope is not None
    R = nh * TQ
    nq, nk = S // TQ, Sk // TK
    pairs, mask_index = _tile_pairs(kind, nq, nk)

    def body(*refs):
        refs = list(refs)
        if pair:
            qkv_ref = refs.pop(0)
            load_q = lambda rows: qkv_ref[0, rows, 0:LANES]
            load_kv = lambda rows: (qkv_ref[0, rows, LANES:2 * LANES], qkv_ref[0, rows, 2 * LANES:3 * LANES])
        else:
            q_ref, k_ref, v_ref = refs.pop(0), refs.pop(0), refs.pop(0)
            load_q = lambda rows: q_ref[0, rows, :]
            load_kv = lambda rows: (k_ref[0, rows, :], v_ref[0, rows, :])
        negc_ref = refs.pop(0) if has_bias else None
        mask_ref = refs.pop(0) if mask is not None else None
        rope_refs = [refs.pop(0) for _ in range(3)] if has_rope else None
        do_ref, o_ref, lse_ref = refs.pop(0), refs.pop(0), refs.pop(0)
        if token is not None:
            refs.pop(0)
        if pair:
            dqkv_ref = refs.pop(0)
            dneg_ref = refs.pop(0) if has_bias else None
            drow_ref = refs.pop(0) if has_bias else None
        else:
            dq_ref, dk_ref, dv_ref = refs.pop(0), refs.pop(0), refs.pop(0)
        qT2s, ks, vs, kTs, doT2s, delta_s, dk_acc, dv_acc = refs[:8]
        bufs_a, bufs_b = refs[8:12], refs[12:16]
        nb, dneg_acc = (refs[16], refs[17]) if has_bias else (None, None)
        lane = lax.broadcasted_iota(jnp.int32, (1, LANES), 1)
        _attn_t_prep(cfg, dict(load_q=load_q, load_kv=load_kv, rope=rope_refs, negc=negc_ref,
                               block=pl.program_id(1)), S, Sk,
                     qT2s=qT2s, ks=ks, vs=vs, kTs=kTs, nb=nb)

        def prep_do(n):
            rows = slice(n * TQ, (n + 1) * TQ)
            doT = do_ref[0, rows, :].astype(BF16).astype(F32).T
            prodT = doT * o_ref[0, rows, :].T
            doTb = doT.astype(BF16)
            for hh in range(nh):
                hm = _head_rows(hh, pair)
                doT2s[n, :, hh * TQ:(hh + 1) * TQ] = jnp.where(hm, doTb, jnp.zeros_like(doTb))
                delta_s[n:n + 1, hh * TQ:(hh + 1) * TQ] = jnp.sum(jnp.where(hm, prodT, 0.0), axis=0, keepdims=True)

        for n in range(nq):
            prep_do(n)
        dk_acc[...] = jnp.zeros(dk_acc.shape, F32)
        dv_acc[...] = jnp.zeros(dv_acc.shape, F32)
        if has_bias:
            dneg_acc[...] = jnp.zeros(dneg_acc.shape, F32)

        def cols(j):
            return slice(j * TK, (j + 1) * TK)

        nt_dims = (((1,), (1,)), ((), ()))

        def first_products(i, j, bufs):
            bufs[0][...] = _raw_scores_t(cfg, ks[cols(j), :], qT2s[i])
            bufs[1][...] = jnp.dot(vs[cols(j), :], doT2s[i], preferred_element_type=F32)

        def last_products(i, j, bufs, dqT2):
            dv_acc[j] += lax.dot_general(doT2s[i], bufs[2][...], nt_dims, preferred_element_type=F32)
            dk_acc[j] += lax.dot_general(qT2s[i], bufs[3][...], nt_dims, preferred_element_type=F32)
            dq = jnp.dot(kTs[j], bufs[3][...], preferred_element_type=F32)
            return dq if dqT2 is None else dqT2 + dq

        def finish_q(i, dqT2, drow):
            rows = slice(i * TQ, (i + 1) * TQ)
            dqT = jnp.where(_head_rows(0, True), dqT2[:, 0:TQ], dqT2[:, TQ:2 * TQ]) if pair else dqT2
            dq = dqT.T
            if q_fold is not None:
                dq = dq * q_fold
            if has_rope:
                dq = _rope_bwd(dq, *[t[rows, :] for t in rope_refs])
            if pair:
                dqkv_ref[0, rows, 0:LANES] = dq.astype(BF16)
            else:
                dq_ref[0, rows, :] = dq.astype(BF16)
            if has_bias:
                drow_ref[0, 0, i:i + 1, :] = drow

        bufs = (bufs_a, bufs_b)
        first_products(*pairs[0], bufs[0])
        dqT2 = drow = None
        for t, (i, j) in enumerate(pairs):
            cur, oth = bufs[t % 2], bufs[1 - t % 2]
            first = j == 0
            if first and t > 0:
                i_prev, j_prev = pairs[t - 1]
                finish_q(i_prev, last_products(i_prev, j_prev, oth, dqT2), drow)
                dqT2 = drow = None
            sT = _bias_mask_t(cfg, cur[0][...], nb, mask_ref, cols(j), mask_index(i, j))
            pT = jnp.exp(sT - lse_ref[0, 0, i:i + 1, :])
            dsT = pT * (cur[1][...] - delta_s[i:i + 1, :])
            if has_bias:
                tile_rows = jnp.sum(dsT, axis=0, keepdims=True)
                drow = tile_rows if drow is None else drow + tile_rows
                for hh in range(nh):
                    part = dsT[:, hh * TQ:hh * TQ + LANES]
                    for u in range(1, TQ // LANES):
                        part = part + dsT[:, hh * TQ + u * LANES:hh * TQ + (u + 1) * LANES]
                    dneg_acc[hh, cols(j), :] += part
            if s_scale is not None:
                dsT = dsT * s_scale
            cur[2][...] = pT.astype(BF16)
            cur[3][...] = dsT.astype(BF16)
            if not first:
                dqT2 = last_products(*pairs[t - 1], oth, dqT2)
            if t + 1 < len(pairs):
                first_products(*pairs[t + 1], oth)
        i_last, j_last = pairs[-1]
        finish_q(i_last, last_products(i_last, j_last, bufs[(len(pairs) - 1) % 2], dqT2), drow)

        for n in range(nk):
            rows = slice(n * TK, (n + 1) * TK)
            dk = dk_acc[n].T
            dv = dv_acc[n].T
            if has_rope:
                dk = _rope_bwd(dk, *[t[rows, :] for t in rope_refs])
            if pair:
                dqkv_ref[0, rows, LANES:2 * LANES] = dk.astype(BF16)
                dqkv_ref[0, rows, 2 * LANES:3 * LANES] = dv.astype(BF16)
            else:
                dk_ref[0, rows, :] = dk.astype(BF16)
                dv_ref[0, rows, :] = dv.astype(BF16)
            if has_bias:
                x0 = jnp.sum(dneg_acc[0, rows, :], axis=1, keepdims=True)
                x1 = jnp.sum(dneg_acc[1, rows, :], axis=1, keepdims=True)
                dneg_ref[0, rows, :] = jnp.where(lane == 0, x0, jnp.where(lane == 1, x1, 0.0))

    ins, in_specs = _attn_t_inputs(kind, src, S, negc_cols, mask, rope, kv)
    row_spec = pl.BlockSpec((1, S, LANES), lambda b, h: (b, 0, h))
    vec_spec = pl.BlockSpec((1, 1, nq, R), lambda b, h: (b, h, 0, 0))
    ins += [do, o, lse]
    in_specs += [row_spec, row_spec, vec_spec]
    if token is not None:
        ins.append(token)
        in_specs.append(pl.BlockSpec(token.shape, lambda b, h: (0, 0)))
    W = cfg["n_blocks"] * LANES
    if pair:
        out_specs = [pl.BlockSpec((1, S, PAIR_W), lambda b, h: (b, 0, h))]
        out_shape = [jax.ShapeDtypeStruct((B, S, 3 * W), BF16)]
        if has_bias:
            out_specs += [row_spec, vec_spec]
            out_shape += [jax.ShapeDtypeStruct((B, S, W), F32), jax.ShapeDtypeStruct((B, cfg["n_blocks"], nq, R), F32)]
    else:
        kv_spec = pl.BlockSpec((1, MEM_LEN, LANES), lambda b, h: (b, 0, h))
        out_specs = [row_spec, kv_spec, kv_spec]
        out_shape = [jax.ShapeDtypeStruct((B, S, W), BF16)] + [jax.ShapeDtypeStruct((B, MEM_LEN, W), BF16)] * 2
    scratch = [pltpu.VMEM((nq, LANES, R), BF16), pltpu.VMEM((Sk, LANES), BF16),
               pltpu.VMEM((Sk, LANES), BF16), pltpu.VMEM((nk, LANES, TK), BF16), pltpu.VMEM((nq, LANES, R), BF16),
               pltpu.VMEM((nq, R), F32), pltpu.VMEM((nk, LANES, TK), F32), pltpu.VMEM((nk, LANES, TK), F32)]
    pair_bufs = [pltpu.VMEM((TK, R), F32), pltpu.VMEM((TK, R), F32), pltpu.VMEM((TK, R), BF16), pltpu.VMEM((TK, R), BF16)]
    scratch += pair_bufs + pair_bufs
    if has_bias:
        scratch += [pltpu.VMEM((nh, Sk, LANES), F32), pltpu.VMEM((nh, Sk, LANES), F32)]
    return pl.pallas_call(
        body, name=kind + "_attn_bwd", grid=(B, cfg["n_blocks"]),
        in_specs=in_specs, out_specs=out_specs, out_shape=out_shape, scratch_shapes=scratch,
        compiler_params=_params(("arbitrary", "arbitrary")),
    )(*ins)


def _sigmoid(g):
    return 1.0 / (1.0 + jnp.exp(-g))


def out_step(proj, o_fox, o_dil, o_mem, w_out, x, target, gf, tm):
    T = x.shape[0]

    def body(fg_ref, dg_ref, mg_ref, of_ref, od_ref, om_ref, w_ref, x_ref, t_ref, gf_ref,
             dx_ref, dof_ref, dod_ref, dom_ref, dfg_ref, ddg_ref, dmg_ref, gw_ref, sm_ref, gw_acc):
        branches = []
        for g_ref, o_ref in ((fg_ref, of_ref), (dg_ref, od_ref), (mg_ref, om_ref)):
            g = g_ref[...]
            sg = _sigmoid(g)
            o = o_ref[...]
            branches.append((g, sg, o))
        ymix = jnp.concatenate([(o * (g * sg)).astype(BF16) for g, sg, o in branches], axis=1)
        x2 = x_ref[...] + jnp.dot(ymix, w_ref[...], preferred_element_type=F32)
        r = lax.rsqrt(jnp.mean(x2 * x2, axis=-1, keepdims=True) + RMS_EPS)
        yn = x2 * r
        err = yn * gf_ref[...] - t_ref[...]
        loss = 0.5 * jnp.sum(jnp.sum(err * err, axis=-1, keepdims=True) / D_MODEL, axis=0, keepdims=True)
        dyf = err / D_MODEL
        dgf = jnp.sum(dyf * yn, axis=0, keepdims=True)
        dyn = dyf * gf_ref[...]
        dx2 = r * (dyn - yn * jnp.mean(dyn * yn, axis=-1, keepdims=True))
        dx_ref[...] = dx2
        dxb = dx2.astype(BF16)
        dmix = lax.dot_general(dxb, w_ref[...], (((1,), (1,)), ((), ())), preferred_element_type=F32)
        col = 0
        for (g, sg, o), do_ref, dgate_ref in zip(branches, (dof_ref, dod_ref, dom_ref), (dfg_ref, ddg_ref, dmg_ref)):
            d = dmix[:, col:col + g.shape[1]]
            col += g.shape[1]
            do_ref[...] = (d * (g * sg)).astype(BF16)
            dgate_ref[...] = (d * o * (sg * (1.0 + g * (1.0 - sg)))).astype(BF16)
        row = lax.broadcasted_iota(jnp.int32, (8, D_MODEL), 0)
        upd = jnp.where(row == 0, dgf, jnp.where(row == 1, loss, 0.0))

        @pl.when(pl.program_id(0) == 0)
        def _():
            sm_ref[...] = jnp.zeros(sm_ref.shape, F32)
            gw_acc[...] = jnp.zeros(gw_acc.shape, F32)

        sm_ref[...] += upd
        gw_acc[...] += lax.dot_general(ymix, dxb, (((0,), (0,)), ((), ())), preferred_element_type=F32)

        @pl.when(pl.program_id(0) == T // tm - 1)
        def _():
            gw_ref[...] = gw_acc[...].astype(BF16)

    def rows(w, col=0):
        return pl.BlockSpec((tm, w), lambda i: (i, col))

    return pl.pallas_call(
        body, name="out_step", grid=(T // tm,),
        in_specs=[rows(FOX_W, B_FG // FOX_W), rows(DIL_W, B_DG // DIL_W), rows(MEM_W, B_MG // MEM_W),
                  rows(FOX_W), rows(DIL_W), rows(MEM_W),
                  pl.BlockSpec((MIX_W, D_MODEL), lambda i: (0, 0)),
                  rows(D_MODEL), rows(D_MODEL), pl.BlockSpec((1, D_MODEL), lambda i: (0, 0))],
        out_specs=[rows(D_MODEL), rows(FOX_W), rows(DIL_W), rows(MEM_W), rows(FOX_W), rows(DIL_W), rows(MEM_W),
                   pl.BlockSpec((MIX_W, D_MODEL), lambda i: (0, 0)), pl.BlockSpec((8, D_MODEL), lambda i: (0, 0))],
        out_shape=[jax.ShapeDtypeStruct((T, D_MODEL), F32), jax.ShapeDtypeStruct((T, FOX_W), BF16),
                   jax.ShapeDtypeStruct((T, DIL_W), BF16), jax.ShapeDtypeStruct((T, MEM_W), BF16),
                   jax.ShapeDtypeStruct((T, FOX_W), BF16), jax.ShapeDtypeStruct((T, DIL_W), BF16),
                   jax.ShapeDtypeStruct((T, MEM_W), BF16), jax.ShapeDtypeStruct((MIX_W, D_MODEL), BF16),
                   jax.ShapeDtypeStruct((8, D_MODEL), F32)],
        scratch_shapes=[pltpu.VMEM((MIX_W, D_MODEL), F32)],
        compiler_params=_params(("arbitrary",)),
    )(proj, proj, proj, o_fox, o_dil, o_mem, w_out, x, target, gf)


def adamw(w, g, m, v, tr, name):
    lead = w.shape[:-2]
    R, C = w.shape[-2:]
    zeros = (0,) * len(lead)

    def body(w_ref, g_ref, m_ref, v_ref, d_ref, mo_ref, vo_ref):
        gv = g_ref[...]
        mn = ADAM_B1 * m_ref[...] + (1.0 - ADAM_B1) * gv
        vn = ADAM_B2 * v_ref[...] + (1.0 - ADAM_B2) * jnp.square(gv)
        m_hat = mn / (1.0 - ADAM_B1 ** ADAM_STEP)
        v_hat = vn / (1.0 - ADAM_B2 ** ADAM_STEP)
        d_ref[...] = -ADAM_LR * (m_hat / (jnp.sqrt(v_hat) + ADAM_EPS) + ADAM_WD * w_ref[...])
        mo_ref[...] = mn
        vo_ref[...] = vn

    spec = pl.BlockSpec((1,) * len(lead) + (tr, C), lambda i: zeros + (i, 0))
    return pl.pallas_call(
        body, name=name, grid=(pl.cdiv(R, tr),),
        in_specs=[spec] * 4, out_specs=[spec] * 3,
        out_shape=[jax.ShapeDtypeStruct(w.shape, F32)] * 3,
        compiler_params=_params(("arbitrary",)),
    )(w, g, m, v)


def adamw_columns_first(w, g, m, v, name):
    N = w.shape[0]
    chunk = 16
    main = N // chunk * chunk

    def body(w_hbm, g_hbm, m_hbm, v_hbm, d_hbm, mo_hbm, vo_hbm, wb, gb, mb, vb, db, mob, vob, sems):
        loads = [pltpu.make_async_copy(h.reshape(N, LANES), b, sems.at[k])
                 for k, (h, b) in enumerate(((w_hbm, wb), (g_hbm, gb), (m_hbm, mb), (v_hbm, vb)))]
        for cp in loads:
            cp.start()
        for cp in loads:
            cp.wait()

        def update(rows):
            gv = gb[rows, :]
            mn = ADAM_B1 * mb[rows, :] + (1.0 - ADAM_B1) * gv
            vn = ADAM_B2 * vb[rows, :] + (1.0 - ADAM_B2) * jnp.square(gv)
            m_hat = mn / (1.0 - ADAM_B1 ** ADAM_STEP)
            v_hat = vn / (1.0 - ADAM_B2 ** ADAM_STEP)
            db[rows, :] = -ADAM_LR * (m_hat / (jnp.sqrt(v_hat) + ADAM_EPS) + ADAM_WD * wb[rows, :])
            mob[rows, :] = mn
            vob[rows, :] = vn

        def step(i, _):
            update(pl.ds(pl.multiple_of(i * chunk, chunk), chunk))
            return 0

        lax.fori_loop(0, main // chunk, step, 0, unroll=4)
        if main < N:
            update(slice(main, N))
        stores = [pltpu.make_async_copy(b, h.reshape(N, LANES), sems.at[k])
                  for k, (h, b) in enumerate(((d_hbm, db), (mo_hbm, mob), (vo_hbm, vob)))]
        for cp in stores:
            cp.start()
        for cp in stores:
            cp.wait()

    any_spec = pl.BlockSpec(memory_space=pl.ANY)
    return pl.pallas_call(
        body, name=name,
        in_specs=[any_spec] * 4, out_specs=[any_spec] * 3,
        out_shape=[jax.ShapeDtypeStruct(w.shape, F32)] * 3,
        scratch_shapes=[pltpu.VMEM((N, LANES), F32)] * 7 + [pltpu.SemaphoreType.DMA((4,))],
        compiler_params=_params(),
    )(w, g, m, v)


def _pad_row(v, width):
    return jnp.concatenate([v, jnp.zeros((1, width - v.shape[1]), v.dtype)], axis=1)


def local_grads(x, mem, norm_g, b_forget, mem_norm_g, final_norm_g, loss_target, first_token, first_weights,
                late_weights, start_exchange):
    B, S, D = x.shape
    T = B * S
    xt = x.reshape(T, D)
    memt = mem.reshape(B * MEM_LEN, D)
    b_pad = _pad_row(b_forget, LANES)

    h, h_t = rms_fwd(xt, norm_g, 512, "rms_x", with_transpose=True, token=first_token)
    mh, mh_t = rms_fwd(memt, mem_norm_g, B * MEM_LEN, "rms_mem", with_transpose=True, token=first_token)
    w_in_a, proj_token = first_weights([h, mh])
    proj_a = mm_nn(h, w_in_a, 1024, PA, "in_proj_a", proj_token)
    proj_a3 = proj_a.reshape(B, S, PA)

    negc = fox_gate(proj_a3, b_pad)
    causal = _log_masks_t(S, "causal")
    dilated = _log_masks_t(S, "dilated")
    rope = _rope_tables(S)

    o_fox, lse_fox = attn_fwd("fox", proj_a3, S, negc_cols=negc, mask=causal)

    w_in_b, w_kv, w_out = late_weights(o_fox)
    proj_b = mm_nn(h, w_in_b, 1024, PB // 2, "in_proj_b")
    proj_b3 = proj_b.reshape(B, S, PB)
    o_dil, lse_dil = attn_fwd("dil", proj_b3, S, mask=dilated, rope=rope)

    mkv = mm_nn(mh, w_kv, B * MEM_LEN, 2 * MEM_W, "mem_kv_proj")
    mkv3 = mkv.reshape(B, MEM_LEN, 2 * MEM_W)
    o_mem, lse_mem = attn_fwd("mem", proj_b3, S, kv=mkv3)

    dx2, do_fox, do_dil, do_mem, dfg, ddg, dmg, g_out, small_out = out_step(
        proj_b, o_fox.reshape(T, FOX_W), o_dil.reshape(T, DIL_W), o_mem.reshape(T, MEM_W), w_out,
        xt, loss_target.reshape(T, D), final_norm_g.reshape(1, D), 256)

    gates = [(dfg, 1, B_FG), (ddg, 1, B_DG), (dmg, 1, B_MG)]
    g_gates = mm_tn_multi(h_t, [piece[0] for piece in gates], 1024, "w_in_grad_gates", BF16)

    dqkv_fox, dneg, drow = attn_bwd("fox", proj_a3, do_fox.reshape(B, S, FOX_W), o_fox, lse_fox, S,
                                    negc_cols=negc, mask=causal)
    drow = drow.reshape(B, FOX_HEADS // 2, S // TQ, 2, TQ).transpose(0, 1, 3, 2, 4).reshape(B, FOX_HEADS, S)
    drow = jnp.pad(drow, ((0, 0), (0, LANES - FOX_HEADS), (0, 0)))
    dflog, db_part = fox_gate_bwd(drow, dneg, proj_a3, b_pad)
    fox = [(dqkv_fox.reshape(T, 3 * FOX_W), 0, A_FOX), (dflog.reshape(T, LANES), 0, A_FLOG)]
    g_fox = mm_tn_multi(h_t, [piece[0] for piece in fox], 1024, "w_in_grad_fox", BF16)
    first, token = start_exchange([g_gates, g_out, g_fox], "early_exchange_a")

    (dqkv_dil,) = attn_bwd("dil", proj_b3, do_dil.reshape(B, S, DIL_W), o_dil, lse_dil, S, mask=dilated, rope=rope,
                           token=token)
    dil = [(dqkv_dil.reshape(T, 3 * DIL_W), 1, B_DIL)]
    g_dil = mm_tn_multi(h_t, [piece[0] for piece in dil], 1024, "w_in_grad_dil", BF16)
    second, token = start_exchange([g_dil], "early_exchange_b")

    dmq, dmk, dmv = attn_bwd("mem", proj_b3, do_mem.reshape(B, S, MEM_W), o_mem, lse_mem, S, kv=mkv3, token=token)
    mq = [(dmq.reshape(T, MEM_W), 1, B_MQ)]
    g_mq = mm_tn_multi(h_t, [piece[0] for piece in mq], 1024, "w_in_grad_mq", BF16)
    dmkv = jnp.concatenate([dmk, dmv], axis=2).reshape(B * MEM_LEN, 2 * MEM_W)
    g_kv = mm_tn_multi(mh_t, [dmkv], B * MEM_LEN, "w_kv_grad", BF16)
    third, token = start_exchange([g_mq, g_kv], "early_exchange_c")

    grad_x, dng = in_proj_bwd_rms(gates + fox + dil + mq, (w_in_a, w_in_b), xt, norm_g, dx2, 512, token)
    dmh = mm_nt(dmkv, w_kv, B * MEM_LEN, D, "mem_kv_bwd")
    _, dmng = rms_bwd(memt, mem_norm_g, dmh, None, B * MEM_LEN, "rms_mem_bwd")

    small = jnp.concatenate([dng[0:1], dmng[0:1], small_out[0:1], _pad_row(db_part[0:1], D), small_out[1:2],
                             jnp.zeros((3, D), F32)], axis=0)
    early = [(first, dqkv_dil), (second, dmq), (third, grad_x)]
    return grad_x.reshape(B, S, D), early, small


def kernel(x, mem, norm_g, w_in, b_forget, mem_norm_g, w_mem_kv, w_out, final_norm_g, loss_target, m_norm_g, m_w_in, m_b_forget, m_mem_norm_g, m_w_mem_kv, m_w_out, m_final_norm_g, v_norm_g, v_w_in, v_b_forget, v_mem_norm_g, v_w_mem_kv, v_w_out, v_final_norm_g):
    D = D_MODEL
    shard_a, shard_b = _split_cols(_pack_cols(w_in).astype(BF16).reshape(w_in.shape[1], PW))
    gather_a, first_token = early_exchange_start([shard_a], "first_gather", gather=True,
                                                 relations=_SIBLING_AND_SAME_CORES)
    late_shards = [shard_b, w_mem_kv[0].astype(BF16), w_out[0].astype(BF16)]
    late_lands = own_slots(late_shards, "late_gather_place", after=first_token)
    late = {}

    def first_weights(after):
        _, gathered = early_exchange_wait(gather_a, list(after) + [late_lands[0]], "first_gather_wait")
        (w_in_a,) = pass_on_to_sibling(gathered, gather_a["rows"], "first_gather_pass")
        late["gather"], token = early_exchange_start(
            late_shards, "late_gather", gather=True, after=w_in_a, relations=_SIBLING_AND_SAME_CORES,
            lands=late_lands)
        return w_in_a, token

    def late_weights(after):
        _, gathered = early_exchange_wait(late["gather"], after, "late_gather_wait")
        return pass_on_to_sibling(gathered, late["gather"]["rows"], "late_gather_pass")

    grad_x, early, small = local_grads(
        x, mem, norm_g, b_forget, mem_norm_g, final_norm_g, loss_target, first_token, first_weights, late_weights,
        early_exchange_start)

    (first, after_first), (second, after_second), (third, after_third) = early
    (src_gates, src_out, src_fox), (land_gates, land_out, land_fox) = early_exchange_wait(
        first, after_first, "early_wait_a")
    (src_dil,), (land_dil,) = early_exchange_wait(second, after_second, "early_wait_b")
    (src_mq, src_kv), (land_mq, land_kv) = early_exchange_wait(third, after_third, "early_wait_c")
    gw_out = slot_sum8(src_out, land_out, 256, "sum_w_out")
    gw_kv = slot_sum8(src_kv, land_kv, 128, "sum_w_kv")
    gw_in_cols = w_in_grad_sum(
        [(src_fox, land_fox, [(0, P_FOX, 3 * FOX_W), (3 * FOX_W, P_FLOG, LANES)]),
         (src_gates, land_gates, [(0, P_FG, FOX_W), (FOX_W, P_DG, DIL_W), (FOX_W + DIL_W, P_MG, MEM_W)]),
         (src_dil, land_dil, [(0, P_DIL, 3 * DIL_W)]),
         (src_mq, land_mq, [(0, P_MQ, MEM_W)])], "sum_w_in")
    gw_in = jnp.transpose(gw_in_cols, (1, 2, 0))

    tot = small_all_reduce(small)

    loss = tot[4, 0]
    g_norm, g_mem_norm, g_final, g_b = tot[0:1], tot[1:2], tot[2], tot[3:4, :FOX_HEADS]

    def rows8(*rows):
        rows = [r.reshape(1, -1) for r in rows]
        rows = [_pad_row(r, D) for r in rows]
        return jnp.concatenate(rows + [jnp.zeros((8 - len(rows), D), F32)], axis=0)

    sw = rows8(norm_g, mem_norm_g, final_norm_g, b_forget)
    sm = rows8(m_norm_g, m_mem_norm_g, m_final_norm_g, m_b_forget)
    sv = rows8(v_norm_g, v_mem_norm_g, v_final_norm_g, v_b_forget)
    d_s, m_s, v_s = adamw(sw, tot, sm, sv, 8, "adamw_small")
    columns_first = lambda a: jnp.transpose(a, (2, 0, 1))
    d_in, m_in, v_in = [jnp.transpose(o, (1, 2, 0)) for o in adamw_columns_first(
        columns_first(w_in), gw_in_cols, columns_first(m_w_in), columns_first(v_w_in), "adamw_w_in")]
    d_kv, m_kv, v_kv = adamw(w_mem_kv[0], gw_kv, m_w_mem_kv[0], v_w_mem_kv[0], 128, "adamw_w_kv")
    d_out, m_out, v_out = adamw(w_out[0], gw_out, m_w_out[0], v_w_out[0], 256, "adamw_w_out")

    def small_outs(t):
        return t[0:1], t[3:4, :FOX_HEADS], t[1:2], t[2]

    grads = (g_norm, gw_in, g_b, g_mem_norm, gw_kv[None], gw_out[None], g_final)
    outs = []
    for t, big in ((d_s, (d_in, d_kv, d_out)), (m_s, (m_in, m_kv, m_out)), (v_s, (v_in, v_kv, v_out))):
        n, b, mn, f = small_outs(t)
        outs += [n, big[0], b, mn, big[1][None], big[2][None], f]
    return (loss, grad_x, *grads, *outs)
```

```python
import math

import numpy as np
import jax
import jax.numpy as jnp
from jax import lax
from jax.experimental import pallas as pl
from jax.experimental.pallas import tpu as pltpu

F32 = jnp.float32
BF16 = jnp.bfloat16

D_MODEL = 1024
HEAD_DIM = 64
FOX_HEADS = 12
DIL_HEADS = 12
MEM_HEADS = 4
MEM_HEAD_DIM = 128
MEM_LEN = 256
FOX_W = FOX_HEADS * HEAD_DIM
DIL_W = DIL_HEADS * HEAD_DIM
MEM_W = MEM_HEADS * MEM_HEAD_DIM
MIX_W = FOX_W + DIL_W + MEM_W
DILATIONS = ((128, 1), (512, 4), (2048, 16))
ROPE_THETA = 500000.0
ROPE_DIM = HEAD_DIM // 4
RMS_EPS = 1e-6
NEG_INF = -1e30
IN_W = 4 * FOX_W + FOX_HEADS + 4 * DIL_W + 2 * MEM_W

ADAM_LR = 0.001
ADAM_B1 = 0.9
ADAM_B2 = 0.999
ADAM_EPS = 1e-08
ADAM_WD = 0.01
ADAM_STEP = 10

N_DEV = 8
LANES = 128
PAIR_W = 3 * LANES
TQ = 256
TK = 256

O_FQ, O_FK, O_FV, O_FG = 0, FOX_W, 2 * FOX_W, 3 * FOX_W
O_FLOG = 4 * FOX_W
O_DQ = O_FLOG + FOX_HEADS
O_DK, O_DV, O_DG = O_DQ + DIL_W, O_DQ + 2 * DIL_W, O_DQ + 3 * DIL_W
O_MQ = O_DQ + 4 * DIL_W
O_MG = O_MQ + MEM_W
P_FOX = 0
P_FG = P_FOX + 3 * FOX_W
P_DIL = P_FG + FOX_W
P_DG = P_DIL + 3 * DIL_W
P_MQ = P_DG + DIL_W
P_MG = P_MQ + MEM_W
P_FLOG = P_MG + MEM_W
PW = P_FLOG + LANES
A_FOX = 0
A_FLOG = A_FOX + 3 * FOX_W
PA = A_FLOG + LANES
B_FG = 0
B_DG = B_FG + FOX_W
B_DIL = B_DG + DIL_W
B_MQ = B_DIL + 3 * DIL_W
B_MG = -(-(B_MQ + MEM_W) // MEM_W) * MEM_W
PB = B_MG + MEM_W

VMEM_LIMIT = 56 * 1024 * 1024


def _pack_pieces():
    pieces = []
    for base in (O_FQ, O_DQ):
        seg = []
        for hp in range(FOX_HEADS // 2):
            for part in range(3):
                seg.append((base + part * FOX_W + hp * LANES, LANES))
        pieces.append(seg)
    fox, dil = pieces
    return fox + [(O_FG, FOX_W)] + dil + [(O_DG, DIL_W), (O_MQ, MEM_W), (O_MG, MEM_W), (O_FLOG, FOX_HEADS)]


def _pack_cols(w):
    parts = [w[..., s:s + n] for s, n in _pack_pieces()]
    parts.append(jnp.zeros(w.shape[:-1] + (LANES - FOX_HEADS,), w.dtype))
    return jnp.concatenate(parts, axis=-1)


def _split_cols(wp):
    def cut(start, width):
        return wp[..., start:start + width]

    group_a = jnp.concatenate([cut(P_FOX, 3 * FOX_W), cut(P_FLOG, LANES)], axis=-1)
    pad = jnp.zeros(wp.shape[:-1] + (B_MG - B_MQ - MEM_W,), wp.dtype)
    group_b = jnp.concatenate([cut(P_FG, FOX_W), cut(P_DG, DIL_W), cut(P_DIL, 3 * DIL_W), cut(P_MQ, MEM_W), pad,
                               cut(P_MG, MEM_W)], axis=-1)
    return group_a, group_b


def _params(sem=None, **kw):
    return pltpu.CompilerParams(dimension_semantics=sem, vmem_limit_bytes=VMEM_LIMIT, **kw)


def _mesh_pos():
    return lax.axis_index("x"), lax.axis_index("y"), lax.axis_index("c")


def _flip(v, d):
    return 1 - v if d else v


_RELATIONS = [(dx, dy, dc) for dx in (0, 1) for dy in (0, 1) for dc in (0, 1)][1:]
_SIBLING_AND_SAME_CORES = [(0, 0, 1), (1, 0, 0), (0, 1, 0), (1, 1, 0)]


_OTHER_CHIPS = [(1, 0), (0, 1), (1, 1)]


def small_all_reduce(small):
    vmem_spec = pl.BlockSpec(memory_space=pltpu.VMEM)

    def body(small_ref, tot_ref, land, send_sems, recv_sems):
        x, y, c = _mesh_pos()
        me = 4 * x + 2 * y + c
        land[me] = small_ref[...]
        sends, recvs = [], []
        for j, (dx, dy, dc) in enumerate(_RELATIONS):
            px, py, pc = _flip(x, dx), _flip(y, dy), _flip(c, dc)
            common = dict(send_sem=send_sems.at[j], recv_sem=recv_sems.at[j],
                          device_id=(px, py, pc), device_id_type=pl.DeviceIdType.MESH)
            sends.append(pltpu.make_async_remote_copy(src_ref=small_ref, dst_ref=land.at[me], **common))
            recvs.append(pltpu.make_async_remote_copy(src_ref=small_ref, dst_ref=land.at[4 * px + 2 * py + pc], **common))
        for cp in sends:
            cp.start()
        for cp in recvs:
            cp.wait_recv()
        for cp in sends:
            cp.wait_send()
        tot = land[0]
        for d in range(1, N_DEV):
            tot = tot + land[d]
        tot_ref[...] = tot

    return pl.pallas_call(
        body, name="small_sum", out_shape=jax.ShapeDtypeStruct(small.shape, small.dtype),
        in_specs=[vmem_spec], out_specs=vmem_spec,
        scratch_shapes=[pltpu.VMEM((N_DEV,) + small.shape, small.dtype),
                        pltpu.SemaphoreType.DMA((len(_RELATIONS),)), pltpu.SemaphoreType.DMA((len(_RELATIONS),))],
    )(small)


_HBM = pl.BlockSpec(memory_space=pltpu.HBM)
_SEM = pl.BlockSpec(memory_space=pltpu.SEMAPHORE)
_EFFECT = pltpu.SideEffectType.DATAFLOW_SIDE_EFFECTING


def _early_copies(src_refs, land_refs, send_sems, recv_sems, rows, gather, relations):
    x, y, c = _mesh_pos()
    me = 4 * x + 2 * y + c
    copies = []
    for a in range(len(src_refs)):
        for dx, dy, dc in relations:
            px, py, pc = _flip(x, dx), _flip(y, dy), _flip(c, dc)
            peer = 4 * px + 2 * py + pc
            copies.append(pltpu.make_async_remote_copy(
                src_ref=src_refs[a] if gather else src_refs[a].at[pl.ds(peer * rows[a], rows[a]), :],
                dst_ref=land_refs[a].at[pl.ds(me * rows[a], rows[a]), :],
                send_sem=send_sems[a], recv_sem=recv_sems[a],
                device_id=(px, py, pc), device_id_type=pl.DeviceIdType.MESH))
    return copies


def own_slots(shards, name, after=None):
    n = len(shards)
    extra = [] if after is None else [after]
    x, y, c = _mesh_pos()
    me = (4 * x + 2 * y + c).astype(jnp.int32).reshape(1)
    empties = [lax.empty((N_DEV * s.shape[0], s.shape[1]), s.dtype) for s in shards]

    def body(me_ref, *refs):
        for a in range(n):
            refs[2 * n + len(extra) + a][...] = refs[a][...]

    return pl.pallas_call(
        body, name=name,
        grid_spec=pltpu.PrefetchScalarGridSpec(
            num_scalar_prefetch=1, grid=(1,),
            in_specs=[pl.BlockSpec(s.shape, lambda i, w: (0, 0)) for s in shards]
            + [pl.BlockSpec(memory_space=pl.ANY)] * (n + len(extra)),
            out_specs=[pl.BlockSpec(s.shape, lambda i, w: (w[0], 0)) for s in shards]),
        out_shape=[jax.ShapeDtypeStruct(e.shape, e.dtype) for e in empties],
        input_output_aliases={1 + n + a: a for a in range(n)},
        compiler_params=_params(("arbitrary",)),
    )(me, *shards, *empties, *extra)


def early_exchange_start(srcs, name, gather=False, after=None, relations=_RELATIONS, lands=None):
    n = len(srcs)
    if gather:
        rows = [s.shape[0] for s in srcs]
        lands = list(own_slots(srcs, name + "_place") if lands is None else lands)
    else:
        rows = [s.shape[0] // N_DEV for s in srcs]
        lands = [lax.empty(s.shape, s.dtype) for s in srcs]

    extra = [] if after is None else [after]

    def body(*refs):
        src_refs, land_refs = refs[:n], refs[n:2 * n]
        first_sem = 2 * n + len(extra)
        send_sems, recv_sems = refs[first_sem:first_sem + n], refs[first_sem + n:first_sem + 2 * n]
        token = refs[-1]
        for cp in _early_copies(src_refs, land_refs, send_sems, recv_sems, rows, gather, relations):
            cp.start()
        token[...] = jnp.zeros_like(token)

    hbm = lambda a: pltpu.HBM(a.shape, a.dtype)
    outs = pl.pallas_call(
        body, name=name,
        out_shape=[pltpu.SemaphoreType.DMA(())] * (2 * n)
        + [hbm(a) for a in srcs] + [hbm(a) for a in lands] + [jax.ShapeDtypeStruct((8, LANES), F32)],
        in_specs=[_HBM] * (2 * n) + [pl.BlockSpec(memory_space=pl.ANY)] * len(extra),
        out_specs=[_SEM] * (2 * n) + [_HBM] * (2 * n) + [pl.BlockSpec(memory_space=pltpu.VMEM)],
        input_output_aliases={i: 2 * n + i for i in range(2 * n)},
        compiler_params=pltpu.CompilerParams(has_side_effects=_EFFECT),
    )(*[pltpu.with_memory_space_constraint(a, pltpu.HBM) for a in list(srcs) + lands], *extra)
    handle = dict(sems=outs[:2 * n], srcs=outs[2 * n:3 * n], lands=outs[3 * n:4 * n], rows=rows,
                  copies=len(relations))
    return handle, outs[-1]


def early_exchange_wait(handle, after, name):
    n = len(handle["srcs"])
    rows = handle["rows"]
    after = list(after) if isinstance(after, (list, tuple)) else [after]

    def body(*refs):
        src_refs, land_refs = refs[:n], refs[n:2 * n]
        send_sems, recv_sems = refs[2 * n:3 * n], refs[3 * n:4 * n]
        x, y, c = _mesh_pos()
        for a in range(n):
            span = pl.ds(0, handle["copies"] * rows[a])
            all_copies = pltpu.make_async_remote_copy(
                src_ref=land_refs[a].at[span, :], dst_ref=land_refs[a].at[span, :],
                send_sem=send_sems[a], recv_sem=recv_sems[a],
                device_id=(x, y, c), device_id_type=pl.DeviceIdType.MESH)
            all_copies.wait_send()
            all_copies.wait_recv()

    hbm = lambda a: pltpu.HBM(a.shape, a.dtype)
    ins = list(handle["srcs"]) + list(handle["lands"])
    outs = pl.pallas_call(
        body, name=name,
        out_shape=[hbm(a) for a in ins],
        in_specs=[_HBM] * (2 * n) + [_SEM] * (2 * n) + [pl.BlockSpec(memory_space=pl.ANY)] * len(after),
        out_specs=[_HBM] * (2 * n),
        input_output_aliases={i: i for i in range(2 * n)},
        compiler_params=pltpu.CompilerParams(has_side_effects=_EFFECT),
    )(*ins, *handle["sems"], *after)
    return outs[:n], outs[n:]


def pass_on_to_sibling(lands, rows, name):
    n = len(lands)

    def body(*refs):
        land_refs = refs[n:2 * n]
        send_sems, recv_sems = refs[2 * n:]
        x, y, c = _mesh_pos()
        copies = []
        for a in range(n):
            for k, (dx, dy) in enumerate(_OTHER_CHIPS):
                slot = 4 * _flip(x, dx) + 2 * _flip(y, dy) + c
                blk = land_refs[a].at[pl.ds(slot * rows[a], rows[a]), :]
                copies.append(pltpu.make_async_remote_copy(
                    src_ref=blk, dst_ref=blk, send_sem=send_sems.at[a, k], recv_sem=recv_sems.at[a, k],
                    device_id=(x, y, 1 - c), device_id_type=pl.DeviceIdType.MESH))
        for cp in copies:
            cp.start()
        for cp in copies:
            cp.wait_recv()
        for cp in copies:
            cp.wait_send()

    any_spec = pl.BlockSpec(memory_space=pl.ANY)
    return pl.pallas_call(
        body, name=name,
        out_shape=[jax.ShapeDtypeStruct(a.shape, a.dtype) for a in lands],
        in_specs=[any_spec] * n, out_specs=[any_spec] * n,
        input_output_aliases={i: i for i in range(n)},
        scratch_shapes=[pltpu.SemaphoreType.DMA((n, len(_OTHER_CHIPS))), pltpu.SemaphoreType.DMA((n, len(_OTHER_CHIPS)))],
    )(*lands)


def slot_sum8(src, land, tr, name):
    rows, cols = land.shape[0] // N_DEV, land.shape[1]
    x, y, c = _mesh_pos()
    me = (4 * x + 2 * y + c).astype(jnp.int32).reshape(1)

    def body(me_ref, src_ref, land_ref, o_ref):
        acc = None
        for d in range(N_DEV):
            term = jnp.where(d == me_ref[0], src_ref[0], land_ref[d]).astype(F32)
            acc = term if acc is None else acc + term
        o_ref[...] = acc

    return pl.pallas_call(
        body, name=name,
        grid_spec=pltpu.PrefetchScalarGridSpec(
            num_scalar_prefetch=1, grid=(rows // tr,),
            in_specs=[pl.BlockSpec((1, tr, cols), lambda i, w: (w[0], i, 0)),
                      pl.BlockSpec((N_DEV, tr, cols), lambda i, w: (0, i, 0))],
            out_specs=pl.BlockSpec((tr, cols), lambda i, w: (i, 0))),
        out_shape=jax.ShapeDtypeStruct((rows, cols), F32),
        compiler_params=_params(("arbitrary",)),
    )(me, src.reshape(N_DEV, rows, cols), land.reshape(N_DEV, rows, cols))


def w_in_grad_sum(groups, name):
    rows = groups[0][0].shape[0] // N_DEV
    runs, pos = [], 0
    for start, width in _pack_pieces():
        runs.append((start, width, pos))
        pos += width
    n = len(groups)

    def body(*refs):
        src_refs, land_refs = refs[0:2 * n:2], refs[1:2 * n:2]
        g_ref = refs[2 * n]
        own_bufs, land_bufs = refs[2 * n + 1:3 * n + 1], refs[3 * n + 1:4 * n + 1]
        stage, load_sems, store_sems = refs[4 * n + 1:]
        x, y, c = _mesh_pos()
        me = 4 * x + 2 * y + c
        loads = []
        for k in range(n):
            pair = [pltpu.make_async_copy(src_refs[k].at[pl.ds(me * rows, rows), :], own_bufs[k], load_sems.at[k, 0]),
                    pltpu.make_async_copy(land_refs[k], land_bufs[k], load_sems.at[k, 1])]
            for cp in pair:
                cp.start()
            loads.append(pair)
        for k, (_, _, segments) in enumerate(groups):
            for cp in loads[k]:
                cp.wait()
            for first, packed, width in segments:
                for off in range(0, width, LANES):
                    cols = slice(first + off, first + off + LANES)
                    acc = None
                    for d in range(N_DEV):
                        term = jnp.where(d == me, own_bufs[k][:, cols], land_bufs[k][d * rows:(d + 1) * rows, cols])
                        acc = term.astype(F32) if acc is None else acc + term.astype(F32)
                    stage[packed + off:packed + off + LANES, :] = acc.T
        g_rows = g_ref.reshape(IN_W, LANES)
        stores = [pltpu.make_async_copy(stage.at[pl.ds(p, width), :], g_rows.at[pl.ds(start, width), :], store_sems.at[r])
                  for r, (start, width, p) in enumerate(runs)]
        for cp in stores:
            cp.start()
        for cp in stores:
            cp.wait()

    any_spec = pl.BlockSpec(memory_space=pl.ANY)
    operands = [a for src, land, _ in groups for a in (src, land)]
    return pl.pallas_call(
        body, name=name,
        in_specs=[any_spec] * (2 * n), out_specs=any_spec,
        out_shape=jax.ShapeDtypeStruct((IN_W, 1, LANES), F32),
        scratch_shapes=[pltpu.VMEM((rows, src.shape[1]), src.dtype) for src, _, _ in groups]
        + [pltpu.VMEM(land.shape, land.dtype) for _, land, _ in groups]
        + [pltpu.VMEM((PW, LANES), F32), pltpu.SemaphoreType.DMA((n, 2)), pltpu.SemaphoreType.DMA((len(runs),))],
        compiler_params=_params(),
    )(*operands)


def mm_tn_multi(a_t, bs, tt, name, out_dtype=F32):
    K, T = a_t.shape
    widths = [b.shape[1] for b in bs]
    steps = T // tt

    def body(a_ref, *rest):
        b_refs, o_ref, acc = rest[:-2], rest[-2], rest[-1]

        @pl.when(pl.program_id(0) == 0)
        def _():
            acc[...] = jnp.zeros(acc.shape, F32)

        av = a_ref[...]
        col = 0
        for b_ref, w in zip(b_refs, widths):
            acc[:, col:col + w] += jnp.dot(av, b_ref[...], preferred_element_type=F32)
            col += w

        @pl.when(pl.program_id(0) == steps - 1)
        def _():
            o_ref[...] = acc[...].astype(out_dtype)

    return pl.pallas_call(
        body, name=name, grid=(steps,),
        in_specs=[pl.BlockSpec((K, tt), lambda t: (0, t))] + [pl.BlockSpec((tt, w), lambda t: (t, 0)) for w in widths],
        out_specs=pl.BlockSpec((K, sum(widths)), lambda t: (0, 0)),
        out_shape=jax.ShapeDtypeStruct((K, sum(widths)), out_dtype),
        scratch_shapes=[pltpu.VMEM((K, sum(widths)), F32)],
        compiler_params=_params(("arbitrary",)),
    )(a_t, *bs)


def rms_fwd(x, g, tm, name, with_transpose=False, token=None):
    M, K = x.shape
    extra = [] if token is None else [token]

    def body(x_ref, g_ref, *rest):
        o_ref = rest[len(extra)]
        xv = x_ref[...]
        r = lax.rsqrt(jnp.mean(xv * xv, axis=-1, keepdims=True) + RMS_EPS)
        h = ((xv * r) * g_ref[...]).astype(BF16)
        o_ref[...] = h
        if with_transpose:
            rest[len(extra) + 1][...] = h.T

    out_specs = [pl.BlockSpec((tm, K), lambda i: (i, 0))]
    out_shape = [jax.ShapeDtypeStruct((M, K), BF16)]
    if with_transpose:
        out_specs.append(pl.BlockSpec((K, tm), lambda i: (0, i)))
        out_shape.append(jax.ShapeDtypeStruct((K, M), BF16))
    outs = pl.pallas_call(
        body, name=name, grid=(M // tm,),
        in_specs=[pl.BlockSpec((tm, K), lambda i: (i, 0)), pl.BlockSpec((1, K), lambda i: (0, 0))]
        + [pl.BlockSpec(t.shape, lambda i: (0, 0)) for t in extra],
        out_specs=out_specs, out_shape=out_shape,
        compiler_params=_params(("arbitrary",)),
    )(x, g, *extra)
    return outs if with_transpose else outs[0]


def rms_bwd(x, g, dh, dres, tm, name):
    M, K = x.shape
    has_res = dres is not None

    def body(*refs):
        if has_res:
            x_ref, g_ref, dh_ref, dres_ref, dx_ref, dg_ref = refs
        else:
            x_ref, g_ref, dh_ref, dx_ref, dg_ref = refs
        xv = x_ref[...]
        r = lax.rsqrt(jnp.mean(xv * xv, axis=-1, keepdims=True) + RMS_EPS)
        xn = xv * r
        dhv = dh_ref[...]
        dxn = dhv * g_ref[...]
        dx = r * (dxn - xn * jnp.mean(dxn * xn, axis=-1, keepdims=True))
        if has_res:
            dx = dx + dres_ref[...]
        dx_ref[...] = dx
        part = jnp.sum(dhv * xn, axis=0, keepdims=True)
        row = lax.broadcasted_iota(jnp.int32, (8, K), 0)
        upd = jnp.where(row == 0, part, 0.0)

        @pl.when(pl.program_id(0) == 0)
        def _():
            dg_ref[...] = upd

        @pl.when(pl.program_id(0) != 0)
        def _():
            dg_ref[...] += upd

    row_spec = pl.BlockSpec((tm, K), lambda i: (i, 0))
    ins = [x, g, dh] + ([dres] if has_res else [])
    in_specs = [row_spec, pl.BlockSpec((1, K), lambda i: (0, 0)), row_spec] + ([row_spec] if has_res else [])
    return pl.pallas_call(
        body, name=name, grid=(M // tm,),
        in_specs=in_specs,
        out_specs=[row_spec, pl.BlockSpec((8, K), lambda i: (0, 0))],
        out_shape=[jax.ShapeDtypeStruct((M, K), F32), jax.ShapeDtypeStruct((8, K), F32)],
        compiler_params=_params(("arbitrary",)),
    )(*ins)


def mm_nn(a, b, tm, tn, name, token=None):
    M, K = a.shape
    N = b.shape[1]
    extra = [] if token is None else [token]

    def body(a_ref, b_ref, *rest):
        rest[-1][...] = jnp.dot(a_ref[...], b_ref[...], preferred_element_type=F32)

    return pl.pallas_call(
        body, name=name, grid=(N // tn, M // tm),
        in_specs=[pl.BlockSpec((tm, K), lambda j, i: (i, 0)), pl.BlockSpec((K, tn), lambda j, i: (0, j))]
        + [pl.BlockSpec(t.shape, lambda j, i: (0, 0)) for t in extra],
        out_specs=pl.BlockSpec((tm, tn), lambda j, i: (i, j)),
        out_shape=jax.ShapeDtypeStruct((M, N), F32),
        compiler_params=_params(("arbitrary", "arbitrary")),
    )(a, b, *extra)


def mm_nt(a, b, tm, tk, name):
    M, K = a.shape
    N = b.shape[0]

    def body(a_ref, b_ref, o_ref):
        part = lax.dot_general(a_ref[...], b_ref[...], (((1,), (1,)), ((), ())), preferred_element_type=F32)

        @pl.when(pl.program_id(1) == 0)
        def _():
            o_ref[...] = part

        @pl.when(pl.program_id(1) != 0)
        def _():
            o_ref[...] += part

    return pl.pallas_call(
        body, name=name, grid=(M // tm, K // tk),
        in_specs=[pl.BlockSpec((tm, tk), lambda i, k: (i, k)), pl.BlockSpec((N, tk), lambda i, k: (0, k))],
        out_specs=pl.BlockSpec((tm, N), lambda i, k: (i, 0)),
        out_shape=jax.ShapeDtypeStruct((M, N), F32),
        compiler_params=_params(("arbitrary", "arbitrary")),
    )(a, b)


def in_proj_bwd_rms(pieces, ws, x, g, dres, tm, token):
    M, N = x.shape

    def body(*refs):
        n = len(pieces)
        p_refs, w_refs = refs[:n], refs[n:n + len(ws)]
        x_ref, g_ref, dres_ref, _, dx_ref, dg_ref = refs[n + len(ws):]
        dh = None
        for p_ref, (arr, group, col) in zip(p_refs, pieces):
            part = lax.dot_general(p_ref[...], w_refs[group][:, col:col + arr.shape[1]], (((1,), (1,)), ((), ())),
                                   preferred_element_type=F32)
            dh = part if dh is None else dh + part
        xv = x_ref[...]
        r = lax.rsqrt(jnp.mean(xv * xv, axis=-1, keepdims=True) + RMS_EPS)
        xn = xv * r
        dxn = dh * g_ref[...]
        dx_ref[...] = r * (dxn - xn * jnp.mean(dxn * xn, axis=-1, keepdims=True)) + dres_ref[...]
        row = lax.broadcasted_iota(jnp.int32, (8, N), 0)
        upd = jnp.where(row == 0, jnp.sum(dh * xn, axis=0, keepdims=True), 0.0)

        @pl.when(pl.program_id(0) == 0)
        def _():
            dg_ref[...] = upd

        @pl.when(pl.program_id(0) != 0)
        def _():
            dg_ref[...] += upd

    row_spec = pl.BlockSpec((tm, N), lambda i: (i, 0))
    return pl.pallas_call(
        body, name="in_proj_bwd", grid=(M // tm,),
        in_specs=[pl.BlockSpec((tm, arr.shape[1]), lambda i: (i, 0)) for arr, _, _ in pieces]
        + [pl.BlockSpec(w.shape, lambda i: (0, 0), pipeline_mode=pl.Buffered(1)) for w in ws]
        + [row_spec, pl.BlockSpec((1, N), lambda i: (0, 0)), row_spec, pl.BlockSpec(token.shape, lambda i: (0, 0))],
        out_specs=[row_spec, pl.BlockSpec((8, N), lambda i: (0, 0))],
        out_shape=[jax.ShapeDtypeStruct((M, N), F32), jax.ShapeDtypeStruct((8, N), F32)],
        compiler_params=_params(("arbitrary",)),
    )(*[arr for arr, _, _ in pieces], *ws, x, g, dres, token)


def _log_sigmoid(z):
    return jnp.minimum(z, 0.0) - jnp.log(1.0 + jnp.exp(-jnp.abs(z)))


def _tri(n, lower):
    r = lax.broadcasted_iota(jnp.int32, (n, n), 0)
    c = lax.broadcasted_iota(jnp.int32, (n, n), 1)
    return jnp.where((r >= c) if lower else (r <= c), 1.0, 0.0).astype(F32)


def fox_gate(proj3, b_pad):
    B, S, _ = proj3.shape
    nblk = S // TK

    def body(f_ref, b_ref, o_ref):
        tri = _tri(TK, True)
        carry = jnp.zeros((1, LANES), F32)
        for n in range(nblk):
            z = f_ref[0, n * TK:(n + 1) * TK, :] + b_ref[...]
            logf = _log_sigmoid(z)
            cs = jnp.dot(tri, logf, preferred_element_type=F32, precision=lax.Precision.HIGHEST) + carry
            carry = cs[TK - 1:TK, :]
            o_ref[0, n * TK:(n + 1) * TK, :] = -cs

    return pl.pallas_call(
        body, name="fox_gate", grid=(B,),
        in_specs=[pl.BlockSpec((1, S, LANES), lambda b: (b, 0, A_FLOG // LANES)),
                  pl.BlockSpec((1, LANES), lambda b: (0, 0))],
        out_specs=pl.BlockSpec((1, S, LANES), lambda b: (b, 0, 0)),
        out_shape=jax.ShapeDtypeStruct((B, S, LANES), F32),
        compiler_params=_params(("arbitrary",)),
    )(proj3, b_pad)


def fox_gate_bwd(drow, dneg, proj3, b_pad):
    B, S, _ = proj3.shape
    nblk = S // TK

    def body(d_ref, r_ref, f_ref, b_ref, o_ref, db_ref):
        tri = _tri(TK, False)
        lane = lax.broadcasted_iota(jnp.int32, (TK, LANES), 1)
        carry = jnp.zeros((1, LANES), F32)
        dbsum = jnp.zeros((1, LANES), F32)
        for n in reversed(range(nblk)):
            dk_side = None
            for hp in range(FOX_HEADS // 2):
                two = jnp.where(lane < 2, r_ref[0, n * TK:(n + 1) * TK, hp * LANES:(hp + 1) * LANES], 0.0)
                two = pltpu.roll(two, 2 * hp, 1) if hp else two
                dk_side = two if dk_side is None else dk_side + two
            dc = jnp.where(lane < FOX_HEADS, d_ref[0, :, n * TK:(n + 1) * TK].T - dk_side, 0.0)
            rs = jnp.dot(tri, dc, preferred_element_type=F32, precision=lax.Precision.HIGHEST) + carry
            carry = rs[0:1, :]
            z = f_ref[0, n * TK:(n + 1) * TK, :] + b_ref[...]
            dz = rs * (1.0 / (1.0 + jnp.exp(z)))
            o_ref[0, n * TK:(n + 1) * TK, :] = dz.astype(BF16)
            dbsum = dbsum + jnp.sum(dz, axis=0, keepdims=True)
        row = lax.broadcasted_iota(jnp.int32, (8, LANES), 0)
        upd = jnp.where(row == 0, dbsum, 0.0)

        @pl.when(pl.program_id(0) == 0)
        def _():
            db_ref[...] = upd

        @pl.when(pl.program_id(0) != 0)
        def _():
            db_ref[...] += upd

    return pl.pallas_call(
        body, name="fox_gate_bwd", grid=(B,),
        in_specs=[pl.BlockSpec((1, LANES, S), lambda b: (b, 0, 0)),
                  pl.BlockSpec((1, S, FOX_W), lambda b: (b, 0, 0)),
                  pl.BlockSpec((1, S, LANES), lambda b: (b, 0, A_FLOG // LANES)),
                  pl.BlockSpec((1, LANES), lambda b: (0, 0))],
        out_specs=[pl.BlockSpec((1, S, LANES), lambda b: (b, 0, 0)), pl.BlockSpec((8, LANES), lambda b: (0, 0))],
        out_shape=[jax.ShapeDtypeStruct((B, S, LANES), BF16), jax.ShapeDtypeStruct((8, LANES), F32)],
        compiler_params=_params(("arbitrary",)),
    )(drow, dneg, proj3, b_pad)


def _rope_tables(S):
    half = ROPE_DIM // 2
    f32 = np.float32
    pos = np.arange(S, dtype=f32)
    inv_freq = f32(1.0) / np.power(f32(ROPE_THETA), np.arange(0, ROPE_DIM, 2, dtype=f32) / f32(ROPE_DIM)).astype(f32)
    ang = (pos[:, None] * inv_freq[None, :]).astype(f32).astype(np.float64)
    cos, sin = np.cos(ang).astype(f32), np.sin(ang).astype(f32)
    one = np.ones((S, HEAD_DIM - ROPE_DIM), f32)
    zero = np.zeros((S, HEAD_DIM - ROPE_DIM), f32)
    zh = np.zeros((S, half), f32)
    c = np.concatenate([cos, cos, one], axis=1)
    s1 = np.concatenate([-sin, zh, zero], axis=1)
    s2 = np.concatenate([zh, sin, zero], axis=1)
    return tuple(jnp.asarray(np.concatenate([t, t], axis=1)) for t in (c, s1, s2))


_HALF_ROPE = ROPE_DIM // 2


def _rope(t, c, s1, s2):
    return t * c + pltpu.roll(t, LANES - _HALF_ROPE, 1) * s1 + pltpu.roll(t, _HALF_ROPE, 1) * s2


def _rope_bwd(d, c, s1, s2):
    return d * c + pltpu.roll(d * s1, _HALF_ROPE, 1) + pltpu.roll(d * s2, LANES - _HALF_ROPE, 1)


def _scale_parts(scale):
    m, _ = math.frexp(scale)
    return (scale, None) if m == 0.5 else (None, scale)


def _log_masks(S, kind):
    nd = 1 if kind == "causal" else S // TQ
    a = np.arange(TQ)[:, None]
    b = np.arange(TK)[None, :]
    out = np.zeros((nd, TQ, TK), np.float32)
    for d in range(nd):
        delta = d * TQ + a - b
        if kind == "causal":
            m = (delta >= 0).astype(np.float64)
        else:
            m = sum(((delta >= 0) & (delta % dil == 0) & (delta <= w)).astype(np.float64) for w, dil in DILATIONS)
        out[d] = np.where(m > 0, np.log(np.maximum(m, 1.0)), NEG_INF)
    return jnp.asarray(out)


def _attn_setup(kind):
    pair = kind != "mem"
    e_dim = HEAD_DIM if pair else MEM_HEAD_DIM
    q_fold, s_scale = _scale_parts(1.0 / math.sqrt(e_dim))
    return dict(pair=pair, col0={"fox": A_FOX, "dil": B_DIL, "mem": B_MQ}[kind],
                n_blocks=FOX_HEADS // 2 if pair else MEM_HEADS, q_fold=q_fold, s_scale=s_scale,
                nh=2 if pair else 1)


def _cat(parts, axis):
    return parts[0] if len(parts) == 1 else jnp.concatenate(parts, axis=axis)


def _log_masks_t(S, kind):
    return jnp.swapaxes(_log_masks(S, kind), 1, 2)


def _head_rows(hh, pair):
    row = lax.broadcasted_iota(jnp.int32, (LANES, 1), 0)
    if not pair:
        return row >= 0
    return (row >= HEAD_DIM * hh) & (row < HEAD_DIM * (hh + 1))


def _attn_t_inputs(kind, src, S, negc_cols, mask, rope, kv):
    cfg = _attn_setup(kind)
    col0 = cfg["col0"]
    ins, in_specs = [], []
    if cfg["pair"]:
        ins.append(src)
        in_specs.append(pl.BlockSpec((1, S, PAIR_W), lambda b, h: (b, 0, col0 // PAIR_W + h)))
    else:
        ins += [src, kv, kv]
        in_specs += [pl.BlockSpec((1, S, LANES), lambda b, h: (b, 0, col0 // LANES + h)),
                     pl.BlockSpec((1, MEM_LEN, LANES), lambda b, h: (b, 0, h)),
                     pl.BlockSpec((1, MEM_LEN, LANES), lambda b, h: (b, 0, MEM_HEADS + h))]
    if negc_cols is not None:
        ins.append(negc_cols)
        in_specs.append(pl.BlockSpec((1, S, LANES), lambda b, h: (b, 0, 0)))
    if mask is not None:
        ins.append(mask)
        in_specs.append(pl.BlockSpec(mask.shape, lambda b, h: (0, 0, 0)))
    if rope is not None:
        ins += list(rope)
        in_specs += [pl.BlockSpec((S, LANES), lambda b, h: (0, 0))] * 3
    return ins, in_specs


def _attn_t_prep(cfg, refs, S, Sk, *, qT2s, ks, vs=None, vTs=None, kTs=None, nb=None):
    pair, nh = cfg["pair"], cfg["nh"]
    lane = lax.broadcasted_iota(jnp.int32, (1, LANES), 1)
    rope_refs = refs["rope"]

    def prep_q(n):
        rows = slice(n * TQ, (n + 1) * TQ)
        q = refs["load_q"](rows)
        if rope_refs is not None:
            q = _rope(q, *[t[rows, :] for t in rope_refs])
        if cfg["q_fold"] is not None:
            q = q * cfg["q_fold"]
        qtb = q.astype(BF16).T
        for hh in range(nh):
            qT2s[n, :, hh * TQ:(hh + 1) * TQ] = jnp.where(_head_rows(hh, pair), qtb, jnp.zeros_like(qtb))

    def prep_kv(n):
        rows = slice(n * TK, (n + 1) * TK)
        k, v = refs["load_kv"](rows)
        if rope_refs is not None:
            k = _rope(k, *[t[rows, :] for t in rope_refs])
        kb = k.astype(BF16)
        vb = v.astype(BF16)
        ks[rows, :] = kb
        if vs is not None:
            vs[rows, :] = vb
        if vTs is not None:
            vTs[n] = vb.T
        if kTs is not None:
            kTs[n] = kb.T
        if nb is not None:
            blk = refs["negc"][0, rows, :]
            for hh in range(nh):
                h = 2 * refs["block"] + hh
                col = jnp.sum(jnp.where(lane == h, blk, 0.0), axis=1, keepdims=True)
                nb[hh, rows, :] = jnp.broadcast_to(col, (TK, LANES))

    for n in range(S // TQ):
        prep_q(n)
    for n in range(Sk // TK):
        prep_kv(n)


def _raw_scores_t(cfg, k, qT2):
    sT = jnp.dot(k, qT2, preferred_element_type=F32)
    if cfg["s_scale"] is not None:
        sT = sT * cfg["s_scale"]
    return sT


def _bias_mask_t(cfg, sT, nb, mask_ref, kc, midx):
    nh = cfg["nh"]
    if nb is None and midx is None:
        return sT
    parts = []
    for hh in range(nh):
        t = sT[:, hh * TQ:(hh + 1) * TQ]
        if nb is not None:
            t = t + jnp.concatenate([nb[hh, kc, :]] * (TQ // LANES), axis=1)
        if midx is not None:
            t = t + mask_ref[midx]
        parts.append(t)
    return _cat(parts, 1)


def _tile_pairs(kind, nq, nk):
    if kind == "mem":
        return [(i, j) for i in range(nq) for j in range(nk)], (lambda i, j: None)
    pairs = [(i, j) for i in range(nq) for j in range(i + 1)]
    if kind == "fox":
        return pairs, (lambda i, j: 0 if j == i else None)
    return pairs, (lambda i, j: i - j)


def attn_fwd(kind, src, S, *, negc_cols=None, mask=None, rope=None, kv=None):
    B = src.shape[0]
    cfg = _attn_setup(kind)
    pair, nh = cfg["pair"], cfg["nh"]
    Sk = S if pair else MEM_LEN
    has_bias, has_rope = negc_cols is not None, rope is not None
    R = nh * TQ
    nq, nk = S // TQ, Sk // TK
    pairs, mask_index = _tile_pairs(kind, nq, nk)

    def body(*refs):
        refs = list(refs)
        if pair:
            qkv_ref = refs.pop(0)
            load_q = lambda rows: qkv_ref[0, rows, 0:LANES]
            load_kv = lambda rows: (qkv_ref[0, rows, LANES:2 * LANES], qkv_ref[0, rows, 2 * LANES:3 * LANES])
        else:
            q_ref, k_ref, v_ref = refs.pop(0), refs.pop(0), refs.pop(0)
            load_q = lambda rows: q_ref[0, rows, :]
            load_kv = lambda rows: (k_ref[0, rows, :], v_ref[0, rows, :])
        negc_ref = refs.pop(0) if has_bias else None
        mask_ref = refs.pop(0) if mask is not None else None
        rope_refs = [refs.pop(0) for _ in range(3)] if has_rope else None
        o_ref, lse_ref, qT2s, ks, vTs, s_a, s_b, p_a, p_b = refs[:9]
        nb = refs[9] if has_bias else None
        _attn_t_prep(cfg, dict(load_q=load_q, load_kv=load_kv, rope=rope_refs, negc=negc_ref,
                               block=pl.program_id(1)), S, Sk, qT2s=qT2s, ks=ks, vTs=vTs, nb=nb)

        def cols(j):
            return slice(j * TK, (j + 1) * TK)

        def scores(i, j):
            return _raw_scores_t(cfg, ks[cols(j), :], qT2s[i])

        def finish(i, m, l, accT):
            oT2 = accT / l
            oT = jnp.where(_head_rows(0, True), oT2[:, 0:TQ], oT2[:, TQ:2 * TQ]) if pair else oT2
            o_ref[0, i * TQ:(i + 1) * TQ, :] = oT.T
            lse_ref[0, 0, i:i + 1, :] = m + jnp.log(l)

        s_bufs, p_bufs = (s_a, s_b), (p_a, p_b)
        s_bufs[0][...] = scores(*pairs[0])
        m = l = accT = None
        for t, (i, j) in enumerate(pairs):
            cur, oth = t % 2, 1 - t % 2
            if t > 0:
                i_prev, j_prev = pairs[t - 1]
                pv = jnp.dot(vTs[j_prev], p_bufs[oth][...], preferred_element_type=F32)
                acc_full = pv if accT is None else accT + pv
            if t + 1 < len(pairs):
                s_bufs[oth][...] = scores(*pairs[t + 1])
            first = j == 0
            if first and t > 0:
                finish(i_prev, m, l, acc_full)
            sT = _bias_mask_t(cfg, s_bufs[cur][...], nb, mask_ref, cols(j), mask_index(i, j))
            m_tile = jnp.max(sT, axis=0, keepdims=True)
            m_new = m_tile if first else jnp.maximum(m, m_tile)
            p = jnp.exp(sT - m_new)
            p_bufs[cur][...] = p.astype(BF16)
            if first:
                l, accT = jnp.sum(p, axis=0, keepdims=True), None
            else:
                alpha = jnp.exp(m - m_new)
                l, accT = alpha * l + jnp.sum(p, axis=0, keepdims=True), acc_full * alpha
            m = m_new
        i_last, j_last = pairs[-1]
        pv = jnp.dot(vTs[j_last], p_bufs[(len(pairs) - 1) % 2][...], preferred_element_type=F32)
        finish(i_last, m, l, pv if accT is None else accT + pv)

    ins, in_specs = _attn_t_inputs(kind, src, S, negc_cols, mask, rope, kv)
    W = cfg["n_blocks"] * LANES
    scratch = [pltpu.VMEM((nq, LANES, R), BF16), pltpu.VMEM((Sk, LANES), BF16), pltpu.VMEM((nk, LANES, TK), BF16),
               pltpu.VMEM((TK, R), F32), pltpu.VMEM((TK, R), F32), pltpu.VMEM((TK, R), BF16), pltpu.VMEM((TK, R), BF16)]
    if has_bias:
        scratch.append(pltpu.VMEM((nh, Sk, LANES), F32))
    return pl.pallas_call(
        body, name=kind + "_attn_fwd", grid=(B, cfg["n_blocks"]),
        in_specs=in_specs,
        out_specs=[pl.BlockSpec((1, S, LANES), lambda b, h: (b, 0, h)),
                   pl.BlockSpec((1, 1, nq, R), lambda b, h: (b, h, 0, 0))],
        out_shape=[jax.ShapeDtypeStruct((B, S, W), F32), jax.ShapeDtypeStruct((B, cfg["n_blocks"], nq, R), F32)],
        scratch_shapes=scratch,
        compiler_params=_params(("arbitrary", "arbitrary")),
    )(*ins)


def attn_bwd(kind, src, do, o, lse, S, *, negc_cols=None, mask=None, rope=None, kv=None, token=None):
    B = src.shape[0]
    cfg = _attn_setup(kind)
    pair, nh, s_scale, q_fold = cfg["pair"], cfg["nh"], cfg["s_scale"], cfg["q_fold"]
    Sk = S if pair else MEM_LEN
    has_bias, has_rope = negc_cols is not None, rope is not None
    R = nh * TQ
    nq, nk = S // TQ, Sk // TK
    pairs, mask_index = _tile_pairs(kind, nq, nk)

    def body(*refs):
        refs = list(refs)
        if pair:
            qkv_ref = refs.pop(0)
            load_q = lambda rows: qkv_ref[0, rows, 0:LANES]
            load_kv = lambda rows: (qkv_ref[0, rows, LANES:2 * LANES], qkv_ref[0, rows, 2 * LANES:3 * LANES])
        else:
            q_ref, k_ref, v_ref = refs.pop(0), refs.pop(0), refs.pop(0)
            load_q = lambda rows: q_ref[0, rows, :]
            load_kv = lambda rows: (k_ref[0, rows, :], v_ref[0, rows, :])
        negc_ref = refs.pop(0) if has_bias else None
        mask_ref = refs.pop(0) if mask is not None else None
        rope_refs = [refs.pop(0) for _ in range(3)] if has_rope else None
        do_ref, o_ref, lse_ref = refs.pop(0), refs.pop(0), refs.pop(0)
        if token is not None:
            refs.pop(0)
        if pair:
            dqkv_ref = refs.pop(0)
            dneg_ref = refs.pop(0) if has_bias else None
            drow_ref = refs.pop(0) if has_bias else None
        else:
            dq_ref, dk_ref, dv_ref = refs.pop(0), refs.pop(0), refs.pop(0)
        qT2s, ks, vs, kTs, doT2s, delta_s, dk_acc, dv_acc = refs[:8]
        bufs_a, bufs_b = refs[8:12], refs[12:16]
        nb, dneg_acc = (refs[16], refs[17]) if has_bias else (None, None)
        lane = lax.broadcasted_iota(jnp.int32, (1, LANES), 1)
        _attn_t_prep(cfg, dict(load_q=load_q, load_kv=load_kv, rope=rope_refs, negc=negc_ref,
                               block=pl.program_id(1)), S, Sk,
                     qT2s=qT2s, ks=ks, vs=vs, kTs=kTs, nb=nb)

        def prep_do(n):
            rows = slice(n * TQ, (n + 1) * TQ)
            doT = do_ref[0, rows, :].astype(BF16).astype(F32).T
            prodT = doT * o_ref[0, rows, :].T
            doTb = doT.astype(BF16)
            for hh in range(nh):
                hm = _head_rows(hh, pair)
                doT2s[n, :, hh * TQ:(hh + 1) * TQ] = jnp.where(hm, doTb, jnp.zeros_like(doTb))
                delta_s[n:n + 1, hh * TQ:(hh + 1) * TQ] = jnp.sum(jnp.where(hm, prodT, 0.0), axis=0, keepdims=True)

        for n in range(nq):
            prep_do(n)
        dk_acc[...] = jnp.zeros(dk_acc.shape, F32)
        dv_acc[...] = jnp.zeros(dv_acc.shape, F32)
        if has_bias:
            dneg_acc[...] = jnp.zeros(dneg_acc.shape, F32)

        def cols(j):
            return slice(j * TK, (j + 1) * TK)

        nt_dims = (((1,), (1,)), ((), ()))

        def first_products(i, j, bufs):
            bufs[0][...] = _raw_scores_t(cfg, ks[cols(j), :], qT2s[i])
            bufs[1][...] = jnp.dot(vs[cols(j), :], doT2s[i], preferred_element_type=F32)

        def last_products(i, j, bufs, dqT2):
            dv_acc[j] += lax.dot_general(doT2s[i], bufs[2][...], nt_dims, preferred_element_type=F32)
            dk_acc[j] += lax.dot_general(qT2s[i], bufs[3][...], nt_dims, preferred_element_type=F32)
            dq = jnp.dot(kTs[j], bufs[3][...], preferred_element_type=F32)
            return dq if dqT2 is None else dqT2 + dq

        def finish_q(i, dqT2, drow):
            rows = slice(i * TQ, (i + 1) * TQ)
            dqT = jnp.where(_head_rows(0, True), dqT2[:, 0:TQ], dqT2[:, TQ:2 * TQ]) if pair else dqT2
            dq = dqT.T
            if q_fold is not None:
                dq = dq * q_fold
            if has_rope:
                dq = _rope_bwd(dq, *[t[rows, :] for t in rope_refs])
            if pair:
                dqkv_ref[0, rows, 0:LANES] = dq.astype(BF16)
            else:
                dq_ref[0, rows, :] = dq.astype(BF16)
            if has_bias:
                drow_ref[0, 0, i:i + 1, :] = drow

        bufs = (bufs_a, bufs_b)
        first_products(*pairs[0], bufs[0])
        dqT2 = drow = None
        for t, (i, j) in enumerate(pairs):
            cur, oth = bufs[t % 2], bufs[1 - t % 2]
            first = j == 0
            if first and t > 0:
                i_prev, j_prev = pairs[t - 1]
                finish_q(i_prev, last_products(i_prev, j_prev, oth, dqT2), drow)
                dqT2 = drow = None
            sT = _bias_mask_t(cfg, cur[0][...], nb, mask_ref, cols(j), mask_index(i, j))
            pT = jnp.exp(sT - lse_ref[0, 0, i:i + 1, :])
            dsT = pT * (cur[1][...] - delta_s[i:i + 1, :])
            if has_bias:
                tile_rows = jnp.sum(dsT, axis=0, keepdims=True)
                drow = tile_rows if drow is None else drow + tile_rows
                for hh in range(nh):
                    part = dsT[:, hh * TQ:hh * TQ + LANES]
                    for u in range(1, TQ // LANES):
                        part = part + dsT[:, hh * TQ + u * LANES:hh * TQ + (u + 1) * LANES]
                    dneg_acc[hh, cols(j), :] += part
            if s_scale is not None:
                dsT = dsT * s_scale
            cur[2][...] = pT.astype(BF16)
            cur[3][...] = dsT.astype(BF16)
            if not first:
                dqT2 = last_products(*pairs[t - 1], oth, dqT2)
            if t + 1 < len(pairs):
                first_products(*pairs[t + 1], oth)
        i_last, j_last = pairs[-1]
        finish_q(i_last, last_products(i_last, j_last, bufs[(len(pairs) - 1) % 2], dqT2), drow)

        for n in range(nk):
            rows = slice(n * TK, (n + 1) * TK)
            dk = dk_acc[n].T
            dv = dv_acc[n].T
            if has_rope:
                dk = _rope_bwd(dk, *[t[rows, :] for t in rope_refs])
            if pair:
                dqkv_ref[0, rows, LANES:2 * LANES] = dk.astype(BF16)
                dqkv_ref[0, rows, 2 * LANES:3 * LANES] = dv.astype(BF16)
            else:
                dk_ref[0, rows, :] = dk.astype(BF16)
                dv_ref[0, rows, :] = dv.astype(BF16)
            if has_bias:
                x0 = jnp.sum(dneg_acc[0, rows, :], axis=1, keepdims=True)
                x1 = jnp.sum(dneg_acc[1, rows, :], axis=1, keepdims=True)
                dneg_ref[0, rows, :] = jnp.where(lane == 0, x0, jnp.where(lane == 1, x1, 0.0))

    ins, in_specs = _attn_t_inputs(kind, src, S, negc_cols, mask, rope, kv)
    row_spec = pl.BlockSpec((1, S, LANES), lambda b, h: (b, 0, h))
    vec_spec = pl.BlockSpec((1, 1, nq, R), lambda b, h: (b, h, 0, 0))
    ins += [do, o, lse]
    in_specs += [row_spec, row_spec, vec_spec]
    if token is not None:
        ins.append(token)
        in_specs.append(pl.BlockSpec(token.shape, lambda b, h: (0, 0)))
    W = cfg["n_blocks"] * LANES
    if pair:
        out_specs = [pl.BlockSpec((1, S, PAIR_W), lambda b, h: (b, 0, h))]
        out_shape = [jax.ShapeDtypeStruct((B, S, 3 * W), BF16)]
        if has_bias:
            out_specs += [row_spec, vec_spec]
            out_shape += [jax.ShapeDtypeStruct((B, S, W), F32), jax.ShapeDtypeStruct((B, cfg["n_blocks"], nq, R), F32)]
    else:
        kv_spec = pl.BlockSpec((1, MEM_LEN, LANES), lambda b, h: (b, 0, h))
        out_specs = [row_spec, kv_spec, kv_spec]
        out_shape = [jax.ShapeDtypeStruct((B, S, W), BF16)] + [jax.ShapeDtypeStruct((B, MEM_LEN, W), BF16)] * 2
    scratch = [pltpu.VMEM((nq, LANES, R), BF16), pltpu.VMEM((Sk, LANES), BF16),
               pltpu.VMEM((Sk, LANES), BF16), pltpu.VMEM((nk, LANES, TK), BF16), pltpu.VMEM((nq, LANES, R), BF16),
               pltpu.VMEM((nq, R), F32), pltpu.VMEM((nk, LANES, TK), F32), pltpu.VMEM((nk, LANES, TK), F32)]
    pair_bufs = [pltpu.VMEM((TK, R), F32), pltpu.VMEM((TK, R), F32), pltpu.VMEM((TK, R), BF16), pltpu.VMEM((TK, R), BF16)]
    scratch += pair_bufs + pair_bufs
    if has_bias:
        scratch += [pltpu.VMEM((nh, Sk, LANES), F32), pltpu.VMEM((nh, Sk, LANES), F32)]
    return pl.pallas_call(
        body, name=kind + "_attn_bwd", grid=(B, cfg["n_blocks"]),
        in_specs=in_specs, out_specs=out_specs, out_shape=out_shape, scratch_shapes=scratch,
        compiler_params=_params(("arbitrary", "arbitrary")),
    )(*ins)


def _sigmoid(g):
    return 1.0 / (1.0 + jnp.exp(-g))


def out_step(proj, o_fox, o_dil, o_mem, w_out, x, target, gf, tm):
    T = x.shape[0]

    def body(fg_ref, dg_ref, mg_ref, of_ref, od_ref, om_ref, w_ref, x_ref, t_ref, gf_ref,
             dx_ref, dof_ref, dod_ref, dom_ref, dfg_ref, ddg_ref, dmg_ref, gw_ref, sm_ref, gw_acc):
        branches = []
        for g_ref, o_ref in ((fg_ref, of_ref), (dg_ref, od_ref), (mg_ref, om_ref)):
            g = g_ref[...]
            sg = _sigmoid(g)
            o = o_ref[...]
            branches.append((g, sg, o))
        ymix = jnp.concatenate([(o * (g * sg)).astype(BF16) for g, sg, o in branches], axis=1)
        x2 = x_ref[...] + jnp.dot(ymix, w_ref[...], preferred_element_type=F32)
        r = lax.rsqrt(jnp.mean(x2 * x2, axis=-1, keepdims=True) + RMS_EPS)
        yn = x2 * r
        err = yn * gf_ref[...] - t_ref[...]
        loss = 0.5 * jnp.sum(jnp.sum(err * err, axis=-1, keepdims=True) / D_MODEL, axis=0, keepdims=True)
        dyf = err / D_MODEL
        dgf = jnp.sum(dyf * yn, axis=0, keepdims=True)
        dyn = dyf * gf_ref[...]
        dx2 = r * (dyn - yn * jnp.mean(dyn * yn, axis=-1, keepdims=True))
        dx_ref[...] = dx2
        dxb = dx2.astype(BF16)
        dmix = lax.dot_general(dxb, w_ref[...], (((1,), (1,)), ((), ())), preferred_element_type=F32)
        col = 0
        for (g, sg, o), do_ref, dgate_ref in zip(branches, (dof_ref, dod_ref, dom_ref), (dfg_ref, ddg_ref, dmg_ref)):
            d = dmix[:, col:col + g.shape[1]]
            col += g.shape[1]
            do_ref[...] = (d * (g * sg)).astype(BF16)
            dgate_ref[...] = (d * o * (sg * (1.0 + g * (1.0 - sg)))).astype(BF16)
        row = lax.broadcasted_iota(jnp.int32, (8, D_MODEL), 0)
        upd = jnp.where(row == 0, dgf, jnp.where(row == 1, loss, 0.0))

        @pl.when(pl.program_id(0) == 0)
        def _():
            sm_ref[...] = jnp.zeros(sm_ref.shape, F32)
            gw_acc[...] = jnp.zeros(gw_acc.shape, F32)

        sm_ref[...] += upd
        gw_acc[...] += lax.dot_general(ymix, dxb, (((0,), (0,)), ((), ())), preferred_element_type=F32)

        @pl.when(pl.program_id(0) == T // tm - 1)
        def _():
            gw_ref[...] = gw_acc[...].astype(BF16)

    def rows(w, col=0):
        return pl.BlockSpec((tm, w), lambda i: (i, col))

    return pl.pallas_call(
        body, name="out_step", grid=(T // tm,),
        in_specs=[rows(FOX_W, B_FG // FOX_W), rows(DIL_W, B_DG // DIL_W), rows(MEM_W, B_MG // MEM_W),
                  rows(FOX_W), rows(DIL_W), rows(MEM_W),
                  pl.BlockSpec((MIX_W, D_MODEL), lambda i: (0, 0)),
                  rows(D_MODEL), rows(D_MODEL), pl.BlockSpec((1, D_MODEL), lambda i: (0, 0))],
        out_specs=[rows(D_MODEL), rows(FOX_W), rows(DIL_W), rows(MEM_W), rows(FOX_W), rows(DIL_W), rows(MEM_W),
                   pl.BlockSpec((MIX_W, D_MODEL), lambda i: (0, 0)), pl.BlockSpec((8, D_MODEL), lambda i: (0, 0))],
        out_shape=[jax.ShapeDtypeStruct((T, D_MODEL), F32), jax.ShapeDtypeStruct((T, FOX_W), BF16),
                   jax.ShapeDtypeStruct((T, DIL_W), BF16), jax.ShapeDtypeStruct((T, MEM_W), BF16),
                   jax.ShapeDtypeStruct((T, FOX_W), BF16), jax.ShapeDtypeStruct((T, DIL_W), BF16),
                   jax.ShapeDtypeStruct((T, MEM_W), BF16), jax.ShapeDtypeStruct((MIX_W, D_MODEL), BF16),
                   jax.ShapeDtypeStruct((8, D_MODEL), F32)],
        scratch_shapes=[pltpu.VMEM((MIX_W, D_MODEL), F32)],
        compiler_params=_params(("arbitrary",)),
    )(proj, proj, proj, o_fox, o_dil, o_mem, w_out, x, target, gf)


def adamw(w, g, m, v, tr, name):
    lead = w.shape[:-2]
    R, C = w.shape[-2:]
    zeros = (0,) * len(lead)

    def body(w_ref, g_ref, m_ref, v_ref, d_ref, mo_ref, vo_ref):
        gv = g_ref[...]
        mn = ADAM_B1 * m_ref[...] + (1.0 - ADAM_B1) * gv
        vn = ADAM_B2 * v_ref[...] + (1.0 - ADAM_B2) * jnp.square(gv)
        m_hat = mn / (1.0 - ADAM_B1 ** ADAM_STEP)
        v_hat = vn / (1.0 - ADAM_B2 ** ADAM_STEP)
        d_ref[...] = -ADAM_LR * (m_hat / (jnp.sqrt(v_hat) + ADAM_EPS) + ADAM_WD * w_ref[...])
        mo_ref[...] = mn
        vo_ref[...] = vn

    spec = pl.BlockSpec((1,) * len(lead) + (tr, C), lambda i: zeros + (i, 0))
    return pl.pallas_call(
        body, name=name, grid=(pl.cdiv(R, tr),),
        in_specs=[spec] * 4, out_specs=[spec] * 3,
        out_shape=[jax.ShapeDtypeStruct(w.shape, F32)] * 3,
        compiler_params=_params(("arbitrary",)),
    )(w, g, m, v)


def adamw_columns_first(w, g, m, v, name):
    N = w.shape[0]
    parts, step_rows = 4, 16
    size = -(-N // (parts * step_rows)) * step_rows
    bounds = [(p * size, min((p + 1) * size, N)) for p in range(parts)]

    def body(w_hbm, g_hbm, m_hbm, v_hbm, d_hbm, mo_hbm, vo_hbm, wb, gb, mb, vb, db, mob, vob, load_sems, store_sems):
        ins = ((w_hbm, wb), (g_hbm, gb), (m_hbm, mb), (v_hbm, vb))
        outs = ((d_hbm, db), (mo_hbm, mob), (vo_hbm, vob))

        def rows_copy(src, dst, lo, hi, sem):
            return pltpu.make_async_copy(src.at[pl.ds(lo, hi - lo), :], dst.at[pl.ds(lo, hi - lo), :], sem)

        def update(rows):
            gv = gb[rows, :]
            mn = ADAM_B1 * mb[rows, :] + (1.0 - ADAM_B1) * gv
            vn = ADAM_B2 * vb[rows, :] + (1.0 - ADAM_B2) * jnp.square(gv)
            m_hat = mn / (1.0 - ADAM_B1 ** ADAM_STEP)
            v_hat = vn / (1.0 - ADAM_B2 ** ADAM_STEP)
            db[rows, :] = -ADAM_LR * (m_hat / (jnp.sqrt(v_hat) + ADAM_EPS) + ADAM_WD * wb[rows, :])
            mob[rows, :] = mn
            vob[rows, :] = vn

        loads = [[rows_copy(h.reshape(N, LANES), b, lo, hi, load_sems.at[p, k]) for k, (h, b) in enumerate(ins)]
                 for p, (lo, hi) in enumerate(bounds)]
        for part in loads:
            for cp in part:
                cp.start()
        stores = []
        for p, (lo, hi) in enumerate(bounds):
            for cp in loads[p]:
                cp.wait()
            whole = (hi - lo) // step_rows

            def step(i, _, lo=lo):
                update(pl.ds(pl.multiple_of(lo + i * step_rows, step_rows), step_rows))
                return 0

            lax.fori_loop(0, whole, step, 0, unroll=4)
            if lo + whole * step_rows < hi:
                update(slice(lo + whole * step_rows, hi))
            leaving = [rows_copy(b, h.reshape(N, LANES), lo, hi, store_sems.at[p, k]) for k, (h, b) in enumerate(outs)]
            for cp in leaving:
                cp.start()
            stores += leaving
        for cp in stores:
            cp.wait()

    any_spec = pl.BlockSpec(memory_space=pl.ANY)
    return pl.pallas_call(
        body, name=name,
        in_specs=[any_spec] * 4, out_specs=[any_spec] * 3,
        out_shape=[jax.ShapeDtypeStruct(w.shape, F32)] * 3,
        scratch_shapes=[pltpu.VMEM((N, LANES), F32)] * 7
        + [pltpu.SemaphoreType.DMA((parts, 4)), pltpu.SemaphoreType.DMA((parts, 3))],
        compiler_params=_params(),
    )(w, g, m, v)


def _pad_row(v, width):
    return jnp.concatenate([v, jnp.zeros((1, width - v.shape[1]), v.dtype)], axis=1)


def local_grads(x, mem, norm_g, b_forget, mem_norm_g, final_norm_g, loss_target, first_token, first_weights,
                late_weights, start_exchange):
    B, S, D = x.shape
    T = B * S
    xt = x.reshape(T, D)
    memt = mem.reshape(B * MEM_LEN, D)
    b_pad = _pad_row(b_forget, LANES)

    h, h_t = rms_fwd(xt, norm_g, 512, "rms_x", with_transpose=True, token=first_token)
    mh, mh_t = rms_fwd(memt, mem_norm_g, B * MEM_LEN, "rms_mem", with_transpose=True, token=first_token)
    w_in_a, proj_token = first_weights([h, mh])
    proj_a = mm_nn(h, w_in_a, 1024, PA, "in_proj_a", proj_token)
    proj_a3 = proj_a.reshape(B, S, PA)

    negc = fox_gate(proj_a3, b_pad)
    causal = _log_masks_t(S, "causal")
    dilated = _log_masks_t(S, "dilated")
    rope = _rope_tables(S)

    o_fox, lse_fox = attn_fwd("fox", proj_a3, S, negc_cols=negc, mask=causal)

    w_in_b, w_kv, w_out = late_weights(o_fox)
    proj_b = mm_nn(h, w_in_b, 1024, PB // 2, "in_proj_b")
    proj_b3 = proj_b.reshape(B, S, PB)
    o_dil, lse_dil = attn_fwd("dil", proj_b3, S, mask=dilated, rope=rope)

    mkv = mm_nn(mh, w_kv, B * MEM_LEN, 2 * MEM_W, "mem_kv_proj")
    mkv3 = mkv.reshape(B, MEM_LEN, 2 * MEM_W)
    o_mem, lse_mem = attn_fwd("mem", proj_b3, S, kv=mkv3)

    dx2, do_fox, do_dil, do_mem, dfg, ddg, dmg, g_out, small_out = out_step(
        proj_b, o_fox.reshape(T, FOX_W), o_dil.reshape(T, DIL_W), o_mem.reshape(T, MEM_W), w_out,
        xt, loss_target.reshape(T, D), final_norm_g.reshape(1, D), 256)

    gates = [(dfg, 1, B_FG), (ddg, 1, B_DG), (dmg, 1, B_MG)]
    g_gates = mm_tn_multi(h_t, [piece[0] for piece in gates], 1024, "w_in_grad_gates", BF16)

    dqkv_fox, dneg, drow = attn_bwd("fox", proj_a3, do_fox.reshape(B, S, FOX_W), o_fox, lse_fox, S,
                                    negc_cols=negc, mask=causal)
    drow = drow.reshape(B, FOX_HEADS // 2, S // TQ, 2, TQ).transpose(0, 1, 3, 2, 4).reshape(B, FOX_HEADS, S)
    drow = jnp.pad(drow, ((0, 0), (0, LANES - FOX_HEADS), (0, 0)))
    dflog, db_part = fox_gate_bwd(drow, dneg, proj_a3, b_pad)
    fox = [(dqkv_fox.reshape(T, 3 * FOX_W), 0, A_FOX), (dflog.reshape(T, LANES), 0, A_FLOG)]
    g_fox = mm_tn_multi(h_t, [piece[0] for piece in fox], 1024, "w_in_grad_fox", BF16)
    first, token = start_exchange([g_gates, g_out, g_fox], "early_exchange_a")

    (dqkv_dil,) = attn_bwd("dil", proj_b3, do_dil.reshape(B, S, DIL_W), o_dil, lse_dil, S, mask=dilated, rope=rope,
                           token=token)
    dil = [(dqkv_dil.reshape(T, 3 * DIL_W), 1, B_DIL)]
    g_dil = mm_tn_multi(h_t, [piece[0] for piece in dil], 1024, "w_in_grad_dil", BF16)
    second, token = start_exchange([g_dil], "early_exchange_b")

    dmq, dmk, dmv = attn_bwd("mem", proj_b3, do_mem.reshape(B, S, MEM_W), o_mem, lse_mem, S, kv=mkv3, token=token)
    mq = [(dmq.reshape(T, MEM_W), 1, B_MQ)]
    g_mq = mm_tn_multi(h_t, [piece[0] for piece in mq], 1024, "w_in_grad_mq", BF16)
    dmkv = jnp.concatenate([dmk, dmv], axis=2).reshape(B * MEM_LEN, 2 * MEM_W)
    g_kv = mm_tn_multi(mh_t, [dmkv], B * MEM_LEN, "w_kv_grad", BF16)
    third, token = start_exchange([g_mq, g_kv], "early_exchange_c")

    grad_x, dng = in_proj_bwd_rms(gates + fox + dil + mq, (w_in_a, w_in_b), xt, norm_g, dx2, 512, token)
    dmh = mm_nt(dmkv, w_kv, B * MEM_LEN, D, "mem_kv_bwd")
    _, dmng = rms_bwd(memt, mem_norm_g, dmh, None, B * MEM_LEN, "rms_mem_bwd")

    small = jnp.concatenate([dng[0:1], dmng[0:1], small_out[0:1], _pad_row(db_part[0:1], D), small_out[1:2],
                             jnp.zeros((3, D), F32)], axis=0)
    early = [(first, dqkv_dil), (second, dmq), (third, grad_x)]
    return grad_x.reshape(B, S, D), early, small


def kernel(x, mem, norm_g, w_in, b_forget, mem_norm_g, w_mem_kv, w_out, final_norm_g, loss_target, m_norm_g, m_w_in, m_b_forget, m_mem_norm_g, m_w_mem_kv, m_w_out, m_final_norm_g, v_norm_g, v_w_in, v_b_forget, v_mem_norm_g, v_w_mem_kv, v_w_out, v_final_norm_g):
    D = D_MODEL
    shard_a, shard_b = _split_cols(_pack_cols(w_in).astype(BF16).reshape(w_in.shape[1], PW))
    gather_a, first_token = early_exchange_start([shard_a], "first_gather", gather=True,
                                                 relations=_SIBLING_AND_SAME_CORES)
    late_shards = [shard_b, w_mem_kv[0].astype(BF16), w_out[0].astype(BF16)]
    late_lands = own_slots(late_shards, "late_gather_place", after=first_token)
    late = {}

    def first_weights(after):
        _, gathered = early_exchange_wait(gather_a, list(after) + [late_lands[0]], "first_gather_wait")
        (w_in_a,) = pass_on_to_sibling(gathered, gather_a["rows"], "first_gather_pass")
        late["gather"], token = early_exchange_start(
            late_shards, "late_gather", gather=True, after=w_in_a, relations=_SIBLING_AND_SAME_CORES,
            lands=late_lands)
        return w_in_a, token

    def late_weights(after):
        _, gathered = early_exchange_wait(late["gather"], after, "late_gather_wait")
        return pass_on_to_sibling(gathered, late["gather"]["rows"], "late_gather_pass")

    grad_x, early, small = local_grads(
        x, mem, norm_g, b_forget, mem_norm_g, final_norm_g, loss_target, first_token, first_weights, late_weights,
        early_exchange_start)

    (first, after_first), (second, after_second), (third, after_third) = early
    (src_gates, src_out, src_fox), (land_gates, land_out, land_fox) = early_exchange_wait(
        first, after_first, "early_wait_a")
    (src_dil,), (land_dil,) = early_exchange_wait(second, after_second, "early_wait_b")
    (src_mq, src_kv), (land_mq, land_kv) = early_exchange_wait(third, after_third, "early_wait_c")
    gw_out = slot_sum8(src_out, land_out, 256, "sum_w_out")
    gw_kv = slot_sum8(src_kv, land_kv, 128, "sum_w_kv")
    gw_in_cols = w_in_grad_sum(
        [(src_fox, land_fox, [(0, P_FOX, 3 * FOX_W), (3 * FOX_W, P_FLOG, LANES)]),
         (src_gates, land_gates, [(0, P_FG, FOX_W), (FOX_W, P_DG, DIL_W), (FOX_W + DIL_W, P_MG, MEM_W)]),
         (src_dil, land_dil, [(0, P_DIL, 3 * DIL_W)]),
         (src_mq, land_mq, [(0, P_MQ, MEM_W)])], "sum_w_in")
    gw_in = jnp.transpose(gw_in_cols, (1, 2, 0))

    tot = small_all_reduce(small)

    loss = tot[4, 0]
    g_norm, g_mem_norm, g_final, g_b = tot[0:1], tot[1:2], tot[2], tot[3:4, :FOX_HEADS]

    def rows8(*rows):
        rows = [r.reshape(1, -1) for r in rows]
        rows = [_pad_row(r, D) for r in rows]
        return jnp.concatenate(rows + [jnp.zeros((8 - len(rows), D), F32)], axis=0)

    sw = rows8(norm_g, mem_norm_g, final_norm_g, b_forget)
    sm = rows8(m_norm_g, m_mem_norm_g, m_final_norm_g, m_b_forget)
    sv = rows8(v_norm_g, v_mem_norm_g, v_final_norm_g, v_b_forget)
    d_s, m_s, v_s = adamw(sw, tot, sm, sv, 8, "adamw_small")
    columns_first = lambda a: jnp.transpose(a, (2, 0, 1))
    d_in, m_in, v_in = [jnp.transpose(o, (1, 2, 0)) for o in adamw_columns_first(
        columns_first(w_in), gw_in_cols, columns_first(m_w_in), columns_first(v_w_in), "adamw_w_in")]
    d_kv, m_kv, v_kv = adamw(w_mem_kv[0], gw_kv, m_w_mem_kv[0], v_w_mem_kv[0], 128, "adamw_w_kv")
    d_out, m_out, v_out = adamw(w_out[0], gw_out, m_w_out[0], v_w_out[0], 256, "adamw_w_out")

    def small_outs(t):
        return t[0:1], t[3:4, :FOX_HEADS], t[1:2], t[2]

    grads = (g_norm, gw_in, g_b, g_mem_norm, gw_kv[None], gw_out[None], g_final)
    outs = []
    for t, big in ((d_s, (d_in, d_kv, d_out)), (m_s, (m_in, m_kv, m_out)), (v_s, (v_in, v_kv, v_out))):
        n, b, mn, f = small_outs(t)
        outs += [n, big[0], b, mn, big[1][None], big[2][None], f]
    return (loss, grad_x, *grads, *outs)
```

```python
import math

import numpy as np
import jax
import jax.numpy as jnp
from jax import lax
from jax.experimental import pallas as pl
from jax.experimental.pallas import tpu as pltpu

F32 = jnp.float32
BF16 = jnp.bfloat16

D_MODEL = 1024
HEAD_DIM = 64
FOX_HEADS = 12
DIL_HEADS = 12
MEM_HEADS = 4
MEM_HEAD_DIM = 128
MEM_LEN = 256
FOX_W = FOX_HEADS * HEAD_DIM
DIL_W = DIL_HEADS * HEAD_DIM
MEM_W = MEM_HEADS * MEM_HEAD_DIM
MIX_W = FOX_W + DIL_W + MEM_W
DILATIONS = ((128, 1), (512, 4), (2048, 16))
ROPE_THETA = 500000.0
ROPE_DIM = HEAD_DIM // 4
RMS_EPS = 1e-6
NEG_INF = -1e30
IN_W = 4 * FOX_W + FOX_HEADS + 4 * DIL_W + 2 * MEM_W

ADAM_LR = 0.001
ADAM_B1 = 0.9
ADAM_B2 = 0.999
ADAM_EPS = 1e-08
ADAM_WD = 0.01
ADAM_STEP = 10

N_DEV = 8
LANES = 128
PAIR_W = 3 * LANES
TQ = 256
TK = 256

O_FQ, O_FK, O_FV, O_FG = 0, FOX_W, 2 * FOX_W, 3 * FOX_W
O_FLOG = 4 * FOX_W
O_DQ = O_FLOG + FOX_HEADS
O_DK, O_DV, O_DG = O_DQ + DIL_W, O_DQ + 2 * DIL_W, O_DQ + 3 * DIL_W
O_MQ = O_DQ + 4 * DIL_W
O_MG = O_MQ + MEM_W
P_FOX = 0
P_FG = P_FOX + 3 * FOX_W
P_DIL = P_FG + FOX_W
P_DG = P_DIL + 3 * DIL_W
P_MQ = P_DG + DIL_W
P_MG = P_MQ + MEM_W
P_FLOG = P_MG + MEM_W
PW = P_FLOG + LANES
A_FOX = 0
A_FLOG = A_FOX + 3 * FOX_W
PA = A_FLOG + LANES
B_FG = 0
B_DG = B_FG + FOX_W
B_DIL = B_DG + DIL_W
B_MQ = B_DIL + 3 * DIL_W
B_MG = -(-(B_MQ + MEM_W) // MEM_W) * MEM_W
PB = B_MG + MEM_W

VMEM_LIMIT = 56 * 1024 * 1024


def _pack_pieces():
    pieces = []
    for base in (O_FQ, O_DQ):
        seg = []
        for hp in range(FOX_HEADS // 2):
            for part in range(3):
                seg.append((base + part * FOX_W + hp * LANES, LANES))
        pieces.append(seg)
    fox, dil = pieces
    return fox + [(O_FG, FOX_W)] + dil + [(O_DG, DIL_W), (O_MQ, MEM_W), (O_MG, MEM_W), (O_FLOG, FOX_HEADS)]


def _pack_cols(w):
    parts = [w[..., s:s + n] for s, n in _pack_pieces()]
    parts.append(jnp.zeros(w.shape[:-1] + (LANES - FOX_HEADS,), w.dtype))
    return jnp.concatenate(parts, axis=-1)


def _split_cols(wp):
    def cut(start, width):
        return wp[..., start:start + width]

    group_a = jnp.concatenate([cut(P_FOX, 3 * FOX_W), cut(P_FLOG, LANES)], axis=-1)
    pad = jnp.zeros(wp.shape[:-1] + (B_MG - B_MQ - MEM_W,), wp.dtype)
    group_b = jnp.concatenate([cut(P_FG, FOX_W), cut(P_DG, DIL_W), cut(P_DIL, 3 * DIL_W), cut(P_MQ, MEM_W), pad,
                               cut(P_MG, MEM_W)], axis=-1)
    return group_a, group_b


def _params(sem=None, **kw):
    return pltpu.CompilerParams(dimension_semantics=sem, vmem_limit_bytes=VMEM_LIMIT, **kw)


def _mesh_pos():
    return lax.axis_index("x"), lax.axis_index("y"), lax.axis_index("c")


def _flip(v, d):
    return 1 - v if d else v


_RELATIONS = [(dx, dy, dc) for dx in (0, 1) for dy in (0, 1) for dc in (0, 1)][1:]
_SIBLING_AND_SAME_CORES = [(0, 0, 1), (1, 0, 0), (0, 1, 0), (1, 1, 0)]


_OTHER_CHIPS = [(1, 0), (0, 1), (1, 1)]


def small_all_reduce(small):
    vmem_spec = pl.BlockSpec(memory_space=pltpu.VMEM)

    def body(small_ref, tot_ref, land, send_sems, recv_sems):
        x, y, c = _mesh_pos()
        me = 4 * x + 2 * y + c
        land[me] = small_ref[...]
        sends, recvs = [], []
        for j, (dx, dy, dc) in enumerate(_RELATIONS):
            px, py, pc = _flip(x, dx), _flip(y, dy), _flip(c, dc)
            common = dict(send_sem=send_sems.at[j], recv_sem=recv_sems.at[j],
                          device_id=(px, py, pc), device_id_type=pl.DeviceIdType.MESH)
            sends.append(pltpu.make_async_remote_copy(src_ref=small_ref, dst_ref=land.at[me], **common))
            recvs.append(pltpu.make_async_remote_copy(src_ref=small_ref, dst_ref=land.at[4 * px + 2 * py + pc], **common))
        for cp in sends:
            cp.start()
        for cp in recvs:
            cp.wait_recv()
        for cp in sends:
            cp.wait_send()
        tot = land[0]
        for d in range(1, N_DEV):
            tot = tot + land[d]
        tot_ref[...] = tot

    return pl.pallas_call(
        body, name="small_sum", out_shape=jax.ShapeDtypeStruct(small.shape, small.dtype),
        in_specs=[vmem_spec], out_specs=vmem_spec,
        scratch_shapes=[pltpu.VMEM((N_DEV,) + small.shape, small.dtype),
                        pltpu.SemaphoreType.DMA((len(_RELATIONS),)), pltpu.SemaphoreType.DMA((len(_RELATIONS),))],
    )(small)


_HBM = pl.BlockSpec(memory_space=pltpu.HBM)
_SEM = pl.BlockSpec(memory_space=pltpu.SEMAPHORE)
_EFFECT = pltpu.SideEffectType.DATAFLOW_SIDE_EFFECTING


def _early_copies(src_refs, land_refs, send_sems, recv_sems, rows, gather, relations):
    x, y, c = _mesh_pos()
    me = 4 * x + 2 * y + c
    copies = []
    for a in range(len(src_refs)):
        for dx, dy, dc in relations:
            px, py, pc = _flip(x, dx), _flip(y, dy), _flip(c, dc)
            peer = 4 * px + 2 * py + pc
            copies.append(pltpu.make_async_remote_copy(
                src_ref=src_refs[a] if gather else src_refs[a].at[pl.ds(peer * rows[a], rows[a]), :],
                dst_ref=land_refs[a].at[pl.ds(me * rows[a], rows[a]), :],
                send_sem=send_sems[a], recv_sem=recv_sems[a],
                device_id=(px, py, pc), device_id_type=pl.DeviceIdType.MESH))
    return copies


def own_slots(shards, name, after=None):
    n = len(shards)
    extra = [] if after is None else [after]
    x, y, c = _mesh_pos()
    me = (4 * x + 2 * y + c).astype(jnp.int32).reshape(1)
    empties = [lax.empty((N_DEV * s.shape[0], s.shape[1]), s.dtype) for s in shards]

    def body(me_ref, *refs):
        for a in range(n):
            refs[2 * n + len(extra) + a][...] = refs[a][...]

    return pl.pallas_call(
        body, name=name,
        grid_spec=pltpu.PrefetchScalarGridSpec(
            num_scalar_prefetch=1, grid=(1,),
            in_specs=[pl.BlockSpec(s.shape, lambda i, w: (0, 0)) for s in shards]
            + [pl.BlockSpec(memory_space=pl.ANY)] * (n + len(extra)),
            out_specs=[pl.BlockSpec(s.shape, lambda i, w: (w[0], 0)) for s in shards]),
        out_shape=[jax.ShapeDtypeStruct(e.shape, e.dtype) for e in empties],
        input_output_aliases={1 + n + a: a for a in range(n)},
        compiler_params=_params(("arbitrary",)),
    )(me, *shards, *empties, *extra)


def early_exchange_start(srcs, name, gather=False, after=None, relations=_RELATIONS, lands=None):
    n = len(srcs)
    if gather:
        rows = [s.shape[0] for s in srcs]
        lands = list(own_slots(srcs, name + "_place") if lands is None else lands)
    else:
        rows = [s.shape[0] // N_DEV for s in srcs]
        lands = [lax.empty(s.shape, s.dtype) for s in srcs]

    extra = [] if after is None else [after]

    def body(*refs):
        src_refs, land_refs = refs[:n], refs[n:2 * n]
        first_sem = 2 * n + len(extra)
        send_sems, recv_sems = refs[first_sem:first_sem + n], refs[first_sem + n:first_sem + 2 * n]
        token = refs[-1]
        for cp in _early_copies(src_refs, land_refs, send_sems, recv_sems, rows, gather, relations):
            cp.start()
        token[...] = jnp.zeros_like(token)

    hbm = lambda a: pltpu.HBM(a.shape, a.dtype)
    outs = pl.pallas_call(
        body, name=name,
        out_shape=[pltpu.SemaphoreType.DMA(())] * (2 * n)
        + [hbm(a) for a in srcs] + [hbm(a) for a in lands] + [jax.ShapeDtypeStruct((8, LANES), F32)],
        in_specs=[_HBM] * (2 * n) + [pl.BlockSpec(memory_space=pl.ANY)] * len(extra),
        out_specs=[_SEM] * (2 * n) + [_HBM] * (2 * n) + [pl.BlockSpec(memory_space=pltpu.VMEM)],
        input_output_aliases={i: 2 * n + i for i in range(2 * n)},
        compiler_params=pltpu.CompilerParams(has_side_effects=_EFFECT),
    )(*[pltpu.with_memory_space_constraint(a, pltpu.HBM) for a in list(srcs) + lands], *extra)
    handle = dict(sems=outs[:2 * n], srcs=outs[2 * n:3 * n], lands=outs[3 * n:4 * n], rows=rows,
                  copies=len(relations))
    return handle, outs[-1]


def early_exchange_wait(handle, after, name):
    n = len(handle["srcs"])
    rows = handle["rows"]
    after = list(after) if isinstance(after, (list, tuple)) else [after]

    def body(*refs):
        src_refs, land_refs = refs[:n], refs[n:2 * n]
        send_sems, recv_sems = refs[2 * n:3 * n], refs[3 * n:4 * n]
        x, y, c = _mesh_pos()
        for a in range(n):
            span = pl.ds(0, handle["copies"] * rows[a])
            all_copies = pltpu.make_async_remote_copy(
                src_ref=land_refs[a].at[span, :], dst_ref=land_refs[a].at[span, :],
                send_sem=send_sems[a], recv_sem=recv_sems[a],
                device_id=(x, y, c), device_id_type=pl.DeviceIdType.MESH)
            all_copies.wait_send()
            all_copies.wait_recv()

    hbm = lambda a: pltpu.HBM(a.shape, a.dtype)
    ins = list(handle["srcs"]) + list(handle["lands"])
    outs = pl.pallas_call(
        body, name=name,
        out_shape=[hbm(a) for a in ins],
        in_specs=[_HBM] * (2 * n) + [_SEM] * (2 * n) + [pl.BlockSpec(memory_space=pl.ANY)] * len(after),
        out_specs=[_HBM] * (2 * n),
        input_output_aliases={i: i for i in range(2 * n)},
        compiler_params=pltpu.CompilerParams(has_side_effects=_EFFECT),
    )(*ins, *handle["sems"], *after)
    return outs[:n], outs[n:]


def pass_on_to_sibling(lands, rows, name):
    n = len(lands)

    def body(*refs):
        land_refs = refs[n:2 * n]
        send_sems, recv_sems = refs[2 * n:]
        x, y, c = _mesh_pos()
        copies = []
        for a in range(n):
            for k, (dx, dy) in enumerate(_OTHER_CHIPS):
                slot = 4 * _flip(x, dx) + 2 * _flip(y, dy) + c
                blk = land_refs[a].at[pl.ds(slot * rows[a], rows[a]), :]
                copies.append(pltpu.make_async_remote_copy(
                    src_ref=blk, dst_ref=blk, send_sem=send_sems.at[a, k], recv_sem=recv_sems.at[a, k],
                    device_id=(x, y, 1 - c), device_id_type=pl.DeviceIdType.MESH))
        for cp in copies:
            cp.start()
        for cp in copies:
            cp.wait_recv()
        for cp in copies:
            cp.wait_send()

    any_spec = pl.BlockSpec(memory_space=pl.ANY)
    return pl.pallas_call(
        body, name=name,
        out_shape=[jax.ShapeDtypeStruct(a.shape, a.dtype) for a in lands],
        in_specs=[any_spec] * n, out_specs=[any_spec] * n,
        input_output_aliases={i: i for i in range(n)},
        scratch_shapes=[pltpu.SemaphoreType.DMA((n, len(_OTHER_CHIPS))), pltpu.SemaphoreType.DMA((n, len(_OTHER_CHIPS)))],
    )(*lands)


def _adamw_update(w, g, m, v):
    mn = ADAM_B1 * m + (1.0 - ADAM_B1) * g
    vn = ADAM_B2 * v + (1.0 - ADAM_B2) * jnp.square(g)
    m_hat = mn / (1.0 - ADAM_B1 ** ADAM_STEP)
    v_hat = vn / (1.0 - ADAM_B2 ** ADAM_STEP)
    return -ADAM_LR * (m_hat / (jnp.sqrt(v_hat) + ADAM_EPS) + ADAM_WD * w), mn, vn


def slot_sum8_adamw(src, land, w, m, v, tr, name):
    rows, cols = land.shape[0] // N_DEV, land.shape[1]
    x, y, c = _mesh_pos()
    me = (4 * x + 2 * y + c).astype(jnp.int32).reshape(1)

    def body(me_ref, src_ref, land_ref, w_ref, m_ref, v_ref, g_ref, d_ref, mo_ref, vo_ref):
        acc = None
        for d in range(N_DEV):
            term = jnp.where(d == me_ref[0], src_ref[0], land_ref[d]).astype(F32)
            acc = term if acc is None else acc + term
        g_ref[...] = acc
        d_ref[...], mo_ref[...], vo_ref[...] = _adamw_update(w_ref[...], acc, m_ref[...], v_ref[...])

    tile = pl.BlockSpec((tr, cols), lambda i, w: (i, 0))
    return pl.pallas_call(
        body, name=name,
        grid_spec=pltpu.PrefetchScalarGridSpec(
            num_scalar_prefetch=1, grid=(rows // tr,),
            in_specs=[pl.BlockSpec((1, tr, cols), lambda i, w: (w[0], i, 0)),
                      pl.BlockSpec((N_DEV, tr, cols), lambda i, w: (0, i, 0)), tile, tile, tile],
            out_specs=[tile] * 4),
        out_shape=[jax.ShapeDtypeStruct((rows, cols), F32)] * 4,
        compiler_params=_params(("arbitrary",)),
    )(me, src.reshape(N_DEV, rows, cols), land.reshape(N_DEV, rows, cols), w, m, v)


def w_in_grad_sum(groups, name):
    rows = groups[0][0].shape[0] // N_DEV
    runs, pos = [], 0
    for start, width in _pack_pieces():
        runs.append((start, width, pos))
        pos += width
    n = len(groups)

    def body(*refs):
        src_refs, land_refs = refs[0:2 * n:2], refs[1:2 * n:2]
        g_ref = refs[2 * n]
        own_bufs, land_bufs = refs[2 * n + 1:3 * n + 1], refs[3 * n + 1:4 * n + 1]
        stage, load_sems, store_sems = refs[4 * n + 1:]
        x, y, c = _mesh_pos()
        me = 4 * x + 2 * y + c
        loads = []
        for k in range(n):
            pair = [pltpu.make_async_copy(src_refs[k].at[pl.ds(me * rows, rows), :], own_bufs[k], load_sems.at[k, 0]),
                    pltpu.make_async_copy(land_refs[k], land_bufs[k], load_sems.at[k, 1])]
            for cp in pair:
                cp.start()
            loads.append(pair)
        for k, (_, _, segments) in enumerate(groups):
            for cp in loads[k]:
                cp.wait()
            for first, packed, width in segments:
                for off in range(0, width, LANES):
                    cols = slice(first + off, first + off + LANES)
                    acc = None
                    for d in range(N_DEV):
                        term = jnp.where(d == me, own_bufs[k][:, cols], land_bufs[k][d * rows:(d + 1) * rows, cols])
                        acc = term.astype(F32) if acc is None else acc + term.astype(F32)
                    stage[packed + off:packed + off + LANES, :] = acc.T
        g_rows = g_ref.reshape(IN_W, LANES)
        stores = [pltpu.make_async_copy(stage.at[pl.ds(p, width), :], g_rows.at[pl.ds(start, width), :], store_sems.at[r])
                  for r, (start, width, p) in enumerate(runs)]
        for cp in stores:
            cp.start()
        for cp in stores:
            cp.wait()

    any_spec = pl.BlockSpec(memory_space=pl.ANY)
    operands = [a for src, land, _ in groups for a in (src, land)]
    return pl.pallas_call(
        body, name=name,
        in_specs=[any_spec] * (2 * n), out_specs=any_spec,
        out_shape=jax.ShapeDtypeStruct((IN_W, 1, LANES), F32),
        scratch_shapes=[pltpu.VMEM((rows, src.shape[1]), src.dtype) for src, _, _ in groups]
        + [pltpu.VMEM(land.shape, land.dtype) for _, land, _ in groups]
        + [pltpu.VMEM((PW, LANES), F32), pltpu.SemaphoreType.DMA((n, 2)), pltpu.SemaphoreType.DMA((len(runs),))],
        compiler_params=_params(),
    )(*operands)


def mm_tn_multi(a_t, bs, tt, name, out_dtype=F32):
    K, T = a_t.shape
    widths = [b.shape[1] for b in bs]
    steps = T // tt

    def body(a_ref, *rest):
        b_refs, o_ref, acc = rest[:-2], rest[-2], rest[-1]

        @pl.when(pl.program_id(0) == 0)
        def _():
            acc[...] = jnp.zeros(acc.shape, F32)

        av = a_ref[...]
        col = 0
        for b_ref, w in zip(b_refs, widths):
            acc[:, col:col + w] += jnp.dot(av, b_ref[...], preferred_element_type=F32)
            col += w

        @pl.when(pl.program_id(0) == steps - 1)
        def _():
            o_ref[...] = acc[...].astype(out_dtype)

    return pl.pallas_call(
        body, name=name, grid=(steps,),
        in_specs=[pl.BlockSpec((K, tt), lambda t: (0, t))] + [pl.BlockSpec((tt, w), lambda t: (t, 0)) for w in widths],
        out_specs=pl.BlockSpec((K, sum(widths)), lambda t: (0, 0)),
        out_shape=jax.ShapeDtypeStruct((K, sum(widths)), out_dtype),
        scratch_shapes=[pltpu.VMEM((K, sum(widths)), F32)],
        compiler_params=_params(("arbitrary",)),
    )(a_t, *bs)


def rms_fwd(x, g, tm, name, with_transpose=False, token=None):
    M, K = x.shape
    extra = [] if token is None else [token]

    def body(x_ref, g_ref, *rest):
        o_ref = rest[len(extra)]
        xv = x_ref[...]
        r = lax.rsqrt(jnp.mean(xv * xv, axis=-1, keepdims=True) + RMS_EPS)
        h = ((xv * r) * g_ref[...]).astype(BF16)
        o_ref[...] = h
        if with_transpose:
            rest[len(extra) + 1][...] = h.T

    out_specs = [pl.BlockSpec((tm, K), lambda i: (i, 0))]
    out_shape = [jax.ShapeDtypeStruct((M, K), BF16)]
    if with_transpose:
        out_specs.append(pl.BlockSpec((K, tm), lambda i: (0, i)))
        out_shape.append(jax.ShapeDtypeStruct((K, M), BF16))
    outs = pl.pallas_call(
        body, name=name, grid=(M // tm,),
        in_specs=[pl.BlockSpec((tm, K), lambda i: (i, 0)), pl.BlockSpec((1, K), lambda i: (0, 0))]
        + [pl.BlockSpec(t.shape, lambda i: (0, 0)) for t in extra],
        out_specs=out_specs, out_shape=out_shape,
        compiler_params=_params(("arbitrary",)),
    )(x, g, *extra)
    return outs if with_transpose else outs[0]


def rms_bwd(x, g, dh, dres, tm, name):
    M, K = x.shape
    has_res = dres is not None

    def body(*refs):
        if has_res:
            x_ref, g_ref, dh_ref, dres_ref, dx_ref, dg_ref = refs
        else:
            x_ref, g_ref, dh_ref, dx_ref, dg_ref = refs
        xv = x_ref[...]
        r = lax.rsqrt(jnp.mean(xv * xv, axis=-1, keepdims=True) + RMS_EPS)
        xn = xv * r
        dhv = dh_ref[...]
        dxn = dhv * g_ref[...]
        dx = r * (dxn - xn * jnp.mean(dxn * xn, axis=-1, keepdims=True))
        if has_res:
            dx = dx + dres_ref[...]
        dx_ref[...] = dx
        part = jnp.sum(dhv * xn, axis=0, keepdims=True)
        row = lax.broadcasted_iota(jnp.int32, (8, K), 0)
        upd = jnp.where(row == 0, part, 0.0)

        @pl.when(pl.program_id(0) == 0)
        def _():
            dg_ref[...] = upd

        @pl.when(pl.program_id(0) != 0)
        def _():
            dg_ref[...] += upd

    row_spec = pl.BlockSpec((tm, K), lambda i: (i, 0))
    ins = [x, g, dh] + ([dres] if has_res else [])
    in_specs = [row_spec, pl.BlockSpec((1, K), lambda i: (0, 0)), row_spec] + ([row_spec] if has_res else [])
    return pl.pallas_call(
        body, name=name, grid=(M // tm,),
        in_specs=in_specs,
        out_specs=[row_spec, pl.BlockSpec((8, K), lambda i: (0, 0))],
        out_shape=[jax.ShapeDtypeStruct((M, K), F32), jax.ShapeDtypeStruct((8, K), F32)],
        compiler_params=_params(("arbitrary",)),
    )(*ins)


def mm_nn(a, b, tm, tn, name, token=None):
    M, K = a.shape
    N = b.shape[1]
    extra = [] if token is None else [token]

    def body(a_ref, b_ref, *rest):
        rest[-1][...] = jnp.dot(a_ref[...], b_ref[...], preferred_element_type=F32)

    return pl.pallas_call(
        body, name=name, grid=(N // tn, M // tm),
        in_specs=[pl.BlockSpec((tm, K), lambda j, i: (i, 0)), pl.BlockSpec((K, tn), lambda j, i: (0, j))]
        + [pl.BlockSpec(t.shape, lambda j, i: (0, 0)) for t in extra],
        out_specs=pl.BlockSpec((tm, tn), lambda j, i: (i, j)),
        out_shape=jax.ShapeDtypeStruct((M, N), F32),
        compiler_params=_params(("arbitrary", "arbitrary")),
    )(a, b, *extra)


def mm_nt(a, b, tm, tk, name):
    M, K = a.shape
    N = b.shape[0]

    def body(a_ref, b_ref, o_ref):
        part = lax.dot_general(a_ref[...], b_ref[...], (((1,), (1,)), ((), ())), preferred_element_type=F32)

        @pl.when(pl.program_id(1) == 0)
        def _():
            o_ref[...] = part

        @pl.when(pl.program_id(1) != 0)
        def _():
            o_ref[...] += part

    return pl.pallas_call(
        body, name=name, grid=(M // tm, K // tk),
        in_specs=[pl.BlockSpec((tm, tk), lambda i, k: (i, k)), pl.BlockSpec((N, tk), lambda i, k: (0, k))],
        out_specs=pl.BlockSpec((tm, N), lambda i, k: (i, 0)),
        out_shape=jax.ShapeDtypeStruct((M, N), F32),
        compiler_params=_params(("arbitrary", "arbitrary")),
    )(a, b)


def in_proj_bwd_rms(pieces, ws, x, g, dres, tm, token):
    M, N = x.shape

    def body(*refs):
        n = len(pieces)
        p_refs, w_refs = refs[:n], refs[n:n + len(ws)]
        x_ref, g_ref, dres_ref, _, dx_ref, dg_ref = refs[n + len(ws):]
        dh = None
        for p_ref, (arr, group, col) in zip(p_refs, pieces):
            part = lax.dot_general(p_ref[...], w_refs[group][:, col:col + arr.shape[1]], (((1,), (1,)), ((), ())),
                                   preferred_element_type=F32)
            dh = part if dh is None else dh + part
        xv = x_ref[...]
        r = lax.rsqrt(jnp.mean(xv * xv, axis=-1, keepdims=True) + RMS_EPS)
        xn = xv * r
        dxn = dh * g_ref[...]
        dx_ref[...] = r * (dxn - xn * jnp.mean(dxn * xn, axis=-1, keepdims=True)) + dres_ref[...]
        row = lax.broadcasted_iota(jnp.int32, (8, N), 0)
        upd = jnp.where(row == 0, jnp.sum(dh * xn, axis=0, keepdims=True), 0.0)

        @pl.when(pl.program_id(0) == 0)
        def _():
            dg_ref[...] = upd

        @pl.when(pl.program_id(0) != 0)
        def _():
            dg_ref[...] += upd

    row_spec = pl.BlockSpec((tm, N), lambda i: (i, 0))
    return pl.pallas_call(
        body, name="in_proj_bwd", grid=(M // tm,),
        in_specs=[pl.BlockSpec((tm, arr.shape[1]), lambda i: (i, 0)) for arr, _, _ in pieces]
        + [pl.BlockSpec(w.shape, lambda i: (0, 0), pipeline_mode=pl.Buffered(1)) for w in ws]
        + [row_spec, pl.BlockSpec((1, N), lambda i: (0, 0)), row_spec, pl.BlockSpec(token.shape, lambda i: (0, 0))],
        out_specs=[row_spec, pl.BlockSpec((8, N), lambda i: (0, 0))],
        out_shape=[jax.ShapeDtypeStruct((M, N), F32), jax.ShapeDtypeStruct((8, N), F32)],
        compiler_params=_params(("arbitrary",)),
    )(*[arr for arr, _, _ in pieces], *ws, x, g, dres, token)


def _log_sigmoid(z):
    return jnp.minimum(z, 0.0) - jnp.log(1.0 + jnp.exp(-jnp.abs(z)))


def _tri(n, lower):
    r = lax.broadcasted_iota(jnp.int32, (n, n), 0)
    c = lax.broadcasted_iota(jnp.int32, (n, n), 1)
    return jnp.where((r >= c) if lower else (r <= c), 1.0, 0.0).astype(F32)


def fox_gate(proj3, b_pad):
    B, S, _ = proj3.shape
    nblk = S // TK

    def body(f_ref, b_ref, o_ref):
        tri = _tri(TK, True)
        carry = jnp.zeros((1, LANES), F32)
        for n in range(nblk):
            z = f_ref[0, n * TK:(n + 1) * TK, :] + b_ref[...]
            logf = _log_sigmoid(z)
            cs = jnp.dot(tri, logf, preferred_element_type=F32, precision=lax.Precision.HIGHEST) + carry
            carry = cs[TK - 1:TK, :]
            o_ref[0, n * TK:(n + 1) * TK, :] = -cs

    return pl.pallas_call(
        body, name="fox_gate", grid=(B,),
        in_specs=[pl.BlockSpec((1, S, LANES), lambda b: (b, 0, A_FLOG // LANES)),
                  pl.BlockSpec((1, LANES), lambda b: (0, 0))],
        out_specs=pl.BlockSpec((1, S, LANES), lambda b: (b, 0, 0)),
        out_shape=jax.ShapeDtypeStruct((B, S, LANES), F32),
        compiler_params=_params(("arbitrary",)),
    )(proj3, b_pad)


def fox_gate_bwd(drow, dneg, proj3, b_pad):
    B, S, _ = proj3.shape
    nblk = S // TK

    def body(d_ref, r_ref, f_ref, b_ref, o_ref, db_ref):
        tri = _tri(TK, False)
        lane = lax.broadcasted_iota(jnp.int32, (TK, LANES), 1)
        carry = jnp.zeros((1, LANES), F32)
        dbsum = jnp.zeros((1, LANES), F32)
        for n in reversed(range(nblk)):
            dk_side = None
            for hp in range(FOX_HEADS // 2):
                two = jnp.where(lane < 2, r_ref[0, n * TK:(n + 1) * TK, hp * LANES:(hp + 1) * LANES], 0.0)
                two = pltpu.roll(two, 2 * hp, 1) if hp else two
                dk_side = two if dk_side is None else dk_side + two
            dc = jnp.where(lane < FOX_HEADS, d_ref[0, :, n * TK:(n + 1) * TK].T - dk_side, 0.0)
            rs = jnp.dot(tri, dc, preferred_element_type=F32, precision=lax.Precision.HIGHEST) + carry
            carry = rs[0:1, :]
            z = f_ref[0, n * TK:(n + 1) * TK, :] + b_ref[...]
            dz = rs * (1.0 / (1.0 + jnp.exp(z)))
            o_ref[0, n * TK:(n + 1) * TK, :] = dz.astype(BF16)
            dbsum = dbsum + jnp.sum(dz, axis=0, keepdims=True)
        row = lax.broadcasted_iota(jnp.int32, (8, LANES), 0)
        upd = jnp.where(row == 0, dbsum, 0.0)

        @pl.when(pl.program_id(0) == 0)
        def _():
            db_ref[...] = upd

        @pl.when(pl.program_id(0) != 0)
        def _():
            db_ref[...] += upd

    return pl.pallas_call(
        body, name="fox_gate_bwd", grid=(B,),
        in_specs=[pl.BlockSpec((1, LANES, S), lambda b: (b, 0, 0)),
                  pl.BlockSpec((1, S, FOX_W), lambda b: (b, 0, 0)),
                  pl.BlockSpec((1, S, LANES), lambda b: (b, 0, A_FLOG // LANES)),
                  pl.BlockSpec((1, LANES), lambda b: (0, 0))],
        out_specs=[pl.BlockSpec((1, S, LANES), lambda b: (b, 0, 0)), pl.BlockSpec((8, LANES), lambda b: (0, 0))],
        out_shape=[jax.ShapeDtypeStruct((B, S, LANES), BF16), jax.ShapeDtypeStruct((8, LANES), F32)],
        compiler_params=_params(("arbitrary",)),
    )(drow, dneg, proj3, b_pad)


def _rope_tables(S):
    half = ROPE_DIM // 2
    f32 = np.float32
    pos = np.arange(S, dtype=f32)
    inv_freq = f32(1.0) / np.power(f32(ROPE_THETA), np.arange(0, ROPE_DIM, 2, dtype=f32) / f32(ROPE_DIM)).astype(f32)
    ang = (pos[:, None] * inv_freq[None, :]).astype(f32).astype(np.float64)
    cos, sin = np.cos(ang).astype(f32), np.sin(ang).astype(f32)
    one = np.ones((S, HEAD_DIM - ROPE_DIM), f32)
    zero = np.zeros((S, HEAD_DIM - ROPE_DIM), f32)
    zh = np.zeros((S, half), f32)
    c = np.concatenate([cos, cos, one], axis=1)
    s1 = np.concatenate([-sin, zh, zero], axis=1)
    s2 = np.concatenate([zh, sin, zero], axis=1)
    return tuple(jnp.asarray(np.concatenate([t, t], axis=1)) for t in (c, s1, s2))


_HALF_ROPE = ROPE_DIM // 2


def _rope(t, c, s1, s2):
    return t * c + pltpu.roll(t, LANES - _HALF_ROPE, 1) * s1 + pltpu.roll(t, _HALF_ROPE, 1) * s2


def _rope_bwd(d, c, s1, s2):
    return d * c + pltpu.roll(d * s1, _HALF_ROPE, 1) + pltpu.roll(d * s2, LANES - _HALF_ROPE, 1)


def _scale_parts(scale):
    m, _ = math.frexp(scale)
    return (scale, None) if m == 0.5 else (None, scale)


def _log_masks(S, kind):
    nd = 1 if kind == "causal" else S // TQ
    a = np.arange(TQ)[:, None]
    b = np.arange(TK)[None, :]
    out = np.zeros((nd, TQ, TK), np.float32)
    for d in range(nd):
        delta = d * TQ + a - b
        if kind == "causal":
            m = (delta >= 0).astype(np.float64)
        else:
            m = sum(((delta >= 0) & (delta % dil == 0) & (delta <= w)).astype(np.float64) for w, dil in DILATIONS)
        out[d] = np.where(m > 0, np.log(np.maximum(m, 1.0)), NEG_INF)
    return jnp.asarray(out)


def _attn_setup(kind):
    pair = kind != "mem"
    e_dim = HEAD_DIM if pair else MEM_HEAD_DIM
    q_fold, s_scale = _scale_parts(1.0 / math.sqrt(e_dim))
    return dict(pair=pair, col0={"fox": A_FOX, "dil": B_DIL, "mem": B_MQ}[kind],
                n_blocks=FOX_HEADS // 2 if pair else MEM_HEADS, q_fold=q_fold, s_scale=s_scale,
                nh=2 if pair else 1)


def _cat(parts, axis):
    return parts[0] if len(parts) == 1 else jnp.concatenate(parts, axis=axis)


def _log_masks_t(S, kind):
    return jnp.swapaxes(_log_masks(S, kind), 1, 2)


def _head_rows(hh, pair):
    row = lax.broadcasted_iota(jnp.int32, (LANES, 1), 0)
    if not pair:
        return row >= 0
    return (row >= HEAD_DIM * hh) & (row < HEAD_DIM * (hh + 1))


def _attn_t_inputs(kind, src, S, negc_cols, mask, rope, kv):
    cfg = _attn_setup(kind)
    col0 = cfg["col0"]
    ins, in_specs = [], []
    if cfg["pair"]:
        ins.append(src)
        in_specs.append(pl.BlockSpec((1, S, PAIR_W), lambda b, h: (b, 0, col0 // PAIR_W + h)))
    else:
        ins += [src, kv, kv]
        in_specs += [pl.BlockSpec((1, S, LANES), lambda b, h: (b, 0, col0 // LANES + h)),
                     pl.BlockSpec((1, MEM_LEN, LANES), lambda b, h: (b, 0, h)),
                     pl.BlockSpec((1, MEM_LEN, LANES), lambda b, h: (b, 0, MEM_HEADS + h))]
    if negc_cols is not None:
        ins.append(negc_cols)
        in_specs.append(pl.BlockSpec((1, S, LANES), lambda b, h: (b, 0, 0)))
    if mask is not None:
        ins.append(mask)
        in_specs.append(pl.BlockSpec(mask.shape, lambda b, h: (0, 0, 0)))
    if rope is not None:
        ins += list(rope)
        in_specs += [pl.BlockSpec((S, LANES), lambda b, h: (0, 0))] * 3
    return ins, in_specs


def _attn_t_prep(cfg, refs, S, Sk, *, qT2s, ks, vs=None, vTs=None, kTs=None, nb=None):
    pair, nh = cfg["pair"], cfg["nh"]
    lane = lax.broadcasted_iota(jnp.int32, (1, LANES), 1)
    rope_refs = refs["rope"]

    def prep_q(n):
        rows = slice(n * TQ, (n + 1) * TQ)
        q = refs["load_q"](rows)
        if rope_refs is not None:
            q = _rope(q, *[t[rows, :] for t in rope_refs])
        if cfg["q_fold"] is not None:
            q = q * cfg["q_fold"]
        qtb = q.astype(BF16).T
        for hh in range(nh):
            qT2s[n, :, hh * TQ:(hh + 1) * TQ] = jnp.where(_head_rows(hh, pair), qtb, jnp.zeros_like(qtb))

    def prep_kv(n):
        rows = slice(n * TK, (n + 1) * TK)
        k, v = refs["load_kv"](rows)
        if rope_refs is not None:
            k = _rope(k, *[t[rows, :] for t in rope_refs])
        kb = k.astype(BF16)
        vb = v.astype(BF16)
        ks[rows, :] = kb
        if vs is not None:
            vs[rows, :] = vb
        if vTs is not None:
            vTs[n] = vb.T
        if kTs is not None:
            kTs[n] = kb.T
        if nb is not None:
            blk = refs["negc"][0, rows, :]
            for hh in range(nh):
                h = 2 * refs["block"] + hh
                col = jnp.sum(jnp.where(lane == h, blk, 0.0), axis=1, keepdims=True)
                nb[hh, rows, :] = jnp.broadcast_to(col, (TK, LANES))

    for n in range(S // TQ):
        prep_q(n)
    for n in range(Sk // TK):
        prep_kv(n)


def _raw_scores_t(cfg, k, qT2):
    sT = jnp.dot(k, qT2, preferred_element_type=F32)
    if cfg["s_scale"] is not None:
        sT = sT * cfg["s_scale"]
    return sT


def _bias_mask_t(cfg, sT, nb, mask_ref, kc, midx):
    nh = cfg["nh"]
    if nb is None and midx is None:
        return sT
    parts = []
    for hh in range(nh):
        t = sT[:, hh * TQ:(hh + 1) * TQ]
        if nb is not None:
            t = t + jnp.concatenate([nb[hh, kc, :]] * (TQ // LANES), axis=1)
        if midx is not None:
            t = t + mask_ref[midx]
        parts.append(t)
    return _cat(parts, 1)


def _tile_pairs(kind, nq, nk):
    if kind == "mem":
        return [(i, j) for i in range(nq) for j in range(nk)], (lambda i, j: None)
    pairs = [(i, j) for i in range(nq) for j in range(i + 1)]
    if kind == "fox":
        return pairs, (lambda i, j: 0 if j == i else None)
    return pairs, (lambda i, j: i - j)


def attn_fwd(kind, src, S, *, negc_cols=None, mask=None, rope=None, kv=None):
    B = src.shape[0]
    cfg = _attn_setup(kind)
    pair, nh = cfg["pair"], cfg["nh"]
    Sk = S if pair else MEM_LEN
    has_bias, has_rope = negc_cols is not None, rope is not None
    R = nh * TQ
    nq, nk = S // TQ, Sk // TK
    pairs, mask_index = _tile_pairs(kind, nq, nk)

    def body(*refs):
        refs = list(refs)
        if pair:
            qkv_ref = refs.pop(0)
            load_q = lambda rows: qkv_ref[0, rows, 0:LANES]
            load_kv = lambda rows: (qkv_ref[0, rows, LANES:2 * LANES], qkv_ref[0, rows, 2 * LANES:3 * LANES])
        else:
            q_ref, k_ref, v_ref = refs.pop(0), refs.pop(0), refs.pop(0)
            load_q = lambda rows: q_ref[0, rows, :]
            load_kv = lambda rows: (k_ref[0, rows, :], v_ref[0, rows, :])
        negc_ref = refs.pop(0) if has_bias else None
        mask_ref = refs.pop(0) if mask is not None else None
        rope_refs = [refs.pop(0) for _ in range(3)] if has_rope else None
        o_ref, lse_ref, qT2s, ks, vTs, s_a, s_b, p_a, p_b = refs[:9]
        nb = refs[9] if has_bias else None
        _attn_t_prep(cfg, dict(load_q=load_q, load_kv=load_kv, rope=rope_refs, negc=negc_ref,
                               block=pl.program_id(1)), S, Sk, qT2s=qT2s, ks=ks, vTs=vTs, nb=nb)

        def cols(j):
            return slice(j * TK, (j + 1) * TK)

        def scores(i, j):
            return _raw_scores_t(cfg, ks[cols(j), :], qT2s[i])

        def finish(i, m, l, accT):
            oT2 = accT / l
            oT = jnp.where(_head_rows(0, True), oT2[:, 0:TQ], oT2[:, TQ:2 * TQ]) if pair else oT2
            o_ref[0, i * TQ:(i + 1) * TQ, :] = oT.T
            lse_ref[0, 0, i:i + 1, :] = m + jnp.log(l)

        s_bufs, p_bufs = (s_a, s_b), (p_a, p_b)
        s_bufs[0][...] = scores(*pairs[0])
        m = l = accT = None
        for t, (i, j) in enumerate(pairs):
            cur, oth = t % 2, 1 - t % 2
            if t > 0:
                i_prev, j_prev = pairs[t - 1]
                pv = jnp.dot(vTs[j_prev], p_bufs[oth][...], preferred_element_type=F32)
                acc_full = pv if accT is None else accT + pv
            if t + 1 < len(pairs):
                s_bufs[oth][...] = scores(*pairs[t + 1])
            first = j == 0
            if first and t > 0:
                finish(i_prev, m, l, acc_full)
            sT = _bias_mask_t(cfg, s_bufs[cur][...], nb, mask_ref, cols(j), mask_index(i, j))
            m_tile = jnp.max(sT, axis=0, keepdims=True)
            m_new = m_tile if first else jnp.maximum(m, m_tile)
            p = jnp.exp(sT - m_new)
            p_bufs[cur][...] = p.astype(BF16)
            if first:
                l, accT = jnp.sum(p, axis=0, keepdims=True), None
            else:
                alpha = jnp.exp(m - m_new)
                l, accT = alpha * l + jnp.sum(p, axis=0, keepdims=True), acc_full * alpha
            m = m_new
        i_last, j_last = pairs[-1]
        pv = jnp.dot(vTs[j_last], p_bufs[(len(pairs) - 1) % 2][...], preferred_element_type=F32)
        finish(i_last, m, l, pv if accT is None else accT + pv)

    ins, in_specs = _attn_t_inputs(kind, src, S, negc_cols, mask, rope, kv)
    W = cfg["n_blocks"] * LANES
    scratch = [pltpu.VMEM((nq, LANES, R), BF16), pltpu.VMEM((Sk, LANES), BF16), pltpu.VMEM((nk, LANES, TK), BF16),
               pltpu.VMEM((TK, R), F32), pltpu.VMEM((TK, R), F32), pltpu.VMEM((TK, R), BF16), pltpu.VMEM((TK, R), BF16)]
    if has_bias:
        scratch.append(pltpu.VMEM((nh, Sk, LANES), F32))
    return pl.pallas_call(
        body, name=kind + "_attn_fwd", grid=(B, cfg["n_blocks"]),
        in_specs=in_specs,
        out_specs=[pl.BlockSpec((1, S, LANES), lambda b, h: (b, 0, h)),
                   pl.BlockSpec((1, 1, nq, R), lambda b, h: (b, h, 0, 0))],
        out_shape=[jax.ShapeDtypeStruct((B, S, W), F32), jax.ShapeDtypeStruct((B, cfg["n_blocks"], nq, R), F32)],
        scratch_shapes=scratch,
        compiler_params=_params(("arbitrary", "arbitrary")),
    )(*ins)


def attn_bwd(kind, src, do, o, lse, S, *, negc_cols=None, mask=None, rope=None, kv=None, token=None):
    B = src.shape[0]
    cfg = _attn_setup(kind)
    pair, nh, s_scale, q_fold = cfg["pair"], cfg["nh"], cfg["s_scale"], cfg["q_fold"]
    Sk = S if pair else MEM_LEN
    has_bias, has_rope = negc_cols is not None, rope is not None
    R = nh * TQ
    nq, nk = S // TQ, Sk // TK
    pairs, mask_index = _tile_pairs(kind, nq, nk)

    def body(*refs):
        refs = list(refs)
        if pair:
            qkv_ref = refs.pop(0)
            load_q = lambda rows: qkv_ref[0, rows, 0:LANES]
            load_kv = lambda rows: (qkv_ref[0, rows, LANES:2 * LANES], qkv_ref[0, rows, 2 * LANES:3 * LANES])
        else:
            q_ref, k_ref, v_ref = refs.pop(0), refs.pop(0), refs.pop(0)
            load_q = lambda rows: q_ref[0, rows, :]
            load_kv = lambda rows: (k_ref[0, rows, :], v_ref[0, rows, :])
        negc_ref = refs.pop(0) if has_bias else None
        mask_ref = refs.pop(0) if mask is not None else None
        rope_refs = [refs.pop(0) for _ in range(3)] if has_rope else None
        do_ref, o_ref, lse_ref = refs.pop(0), refs.pop(0), refs.pop(0)
        if token is not None:
            refs.pop(0)
        if pair:
            dqkv_ref = refs.pop(0)
            dneg_ref = refs.pop(0) if has_bias else None
            drow_ref = refs.pop(0) if has_bias else None
        else:
            dq_ref, dk_ref, dv_ref = refs.pop(0), refs.pop(0), refs.pop(0)
        qT2s, ks, vs, kTs, doT2s, delta_s, dk_acc, dv_acc = refs[:8]
        bufs_a, bufs_b = refs[8:12], refs[12:16]
        nb, dneg_acc = (refs[16], refs[17]) if has_bias else (None, None)
        lane = lax.broadcasted_iota(jnp.int32, (1, LANES), 1)
        _attn_t_prep(cfg, dict(load_q=load_q, load_kv=load_kv, rope=rope_refs, negc=negc_ref,
                               block=pl.program_id(1)), S, Sk,
                     qT2s=qT2s, ks=ks, vs=vs, kTs=kTs, nb=nb)

        def prep_do(n):
            rows = slice(n * TQ, (n + 1) * TQ)
            doT = do_ref[0, rows, :].astype(BF16).astype(F32).T
            prodT = doT * o_ref[0, rows, :].T
            doTb = doT.astype(BF16)
            for hh in range(nh):
                hm = _head_rows(hh, pair)
                doT2s[n, :, hh * TQ:(hh + 1) * TQ] = jnp.where(hm, doTb, jnp.zeros_like(doTb))
                delta_s[n:n + 1, hh * TQ:(hh + 1) * TQ] = jnp.sum(jnp.where(hm, prodT, 0.0), axis=0, keepdims=True)

        for n in range(nq):
            prep_do(n)
        dk_acc[...] = jnp.zeros(dk_acc.shape, F32)
        dv_acc[...] = jnp.zeros(dv_acc.shape, F32)
        if has_bias:
            dneg_acc[...] = jnp.zeros(dneg_acc.shape, F32)

        def cols(j):
            return slice(j * TK, (j + 1) * TK)

        nt_dims = (((1,), (1,)), ((), ()))

        def first_products(i, j, bufs):
            bufs[0][...] = _raw_scores_t(cfg, ks[cols(j), :], qT2s[i])
            bufs[1][...] = jnp.dot(vs[cols(j), :], doT2s[i], preferred_element_type=F32)

        def last_products(i, j, bufs, dqT2):
            dv_acc[j] += lax.dot_general(doT2s[i], bufs[2][...], nt_dims, preferred_element_type=F32)
            dk_acc[j] += lax.dot_general(qT2s[i], bufs[3][...], nt_dims, preferred_element_type=F32)
            dq = jnp.dot(kTs[j], bufs[3][...], preferred_element_type=F32)
            return dq if dqT2 is None else dqT2 + dq

        def finish_q(i, dqT2, drow):
            rows = slice(i * TQ, (i + 1) * TQ)
            dqT = jnp.where(_head_rows(0, True), dqT2[:, 0:TQ], dqT2[:, TQ:2 * TQ]) if pair else dqT2
            dq = dqT.T
            if q_fold is not None:
                dq = dq * q_fold
            if has_rope:
                dq = _rope_bwd(dq, *[t[rows, :] for t in rope_refs])
            if pair:
                dqkv_ref[0, rows, 0:LANES] = dq.astype(BF16)
            else:
                dq_ref[0, rows, :] = dq.astype(BF16)
            if has_bias:
                drow_ref[0, 0, i:i + 1, :] = drow

        bufs = (bufs_a, bufs_b)
        first_products(*pairs[0], bufs[0])
        dqT2 = drow = None
        for t, (i, j) in enumerate(pairs):
            cur, oth = bufs[t % 2], bufs[1 - t % 2]
            first = j == 0
            if first and t > 0:
                i_prev, j_prev = pairs[t - 1]
                finish_q(i_prev, last_products(i_prev, j_prev, oth, dqT2), drow)
                dqT2 = drow = None
            sT = _bias_mask_t(cfg, cur[0][...], nb, mask_ref, cols(j), mask_index(i, j))
            pT = jnp.exp(sT - lse_ref[0, 0, i:i + 1, :])
            dsT = pT * (cur[1][...] - delta_s[i:i + 1, :])
            if has_bias:
                tile_rows = jnp.sum(dsT, axis=0, keepdims=True)
                drow = tile_rows if drow is None else drow + tile_rows
                for hh in range(nh):
                    part = dsT[:, hh * TQ:hh * TQ + LANES]
                    for u in range(1, TQ // LANES):
                        part = part + dsT[:, hh * TQ + u * LANES:hh * TQ + (u + 1) * LANES]
                    dneg_acc[hh, cols(j), :] += part
            if s_scale is not None:
                dsT = dsT * s_scale
            cur[2][...] = pT.astype(BF16)
            cur[3][...] = dsT.astype(BF16)
            if not first:
                dqT2 = last_products(*pairs[t - 1], oth, dqT2)
            if t + 1 < len(pairs):
                first_products(*pairs[t + 1], oth)
        i_last, j_last = pairs[-1]
        finish_q(i_last, last_products(i_last, j_last, bufs[(len(pairs) - 1) % 2], dqT2), drow)

        for n in range(nk):
            rows = slice(n * TK, (n + 1) * TK)
            dk = dk_acc[n].T
            dv = dv_acc[n].T
            if has_rope:
                dk = _rope_bwd(dk, *[t[rows, :] for t in rope_refs])
            if pair:
                dqkv_ref[0, rows, LANES:2 * LANES] = dk.astype(BF16)
                dqkv_ref[0, rows, 2 * LANES:3 * LANES] = dv.astype(BF16)
            else:
                dk_ref[0, rows, :] = dk.astype(BF16)
                dv_ref[0, rows, :] = dv.astype(BF16)
            if has_bias:
                x0 = jnp.sum(dneg_acc[0, rows, :], axis=1, keepdims=True)
                x1 = jnp.sum(dneg_acc[1, rows, :], axis=1, keepdims=True)
                dneg_ref[0, rows, :] = jnp.where(lane == 0, x0, jnp.where(lane == 1, x1, 0.0))

    ins, in_specs = _attn_t_inputs(kind, src, S, negc_cols, mask, rope, kv)
    row_spec = pl.BlockSpec((1, S, LANES), lambda b, h: (b, 0, h))
    vec_spec = pl.BlockSpec((1, 1, nq, R), lambda b, h: (b, h, 0, 0))
    ins += [do, o, lse]
    in_specs += [row_spec, row_spec, vec_spec]
    if token is not None:
        ins.append(token)
        in_specs.append(pl.BlockSpec(token.shape, lambda b, h: (0, 0)))
    W = cfg["n_blocks"] * LANES
    if pair:
        out_specs = [pl.BlockSpec((1, S, PAIR_W), lambda b, h: (b, 0, h))]
        out_shape = [jax.ShapeDtypeStruct((B, S, 3 * W), BF16)]
        if has_bias:
            out_specs += [row_spec, vec_spec]
            out_shape += [jax.ShapeDtypeStruct((B, S, W), F32), jax.ShapeDtypeStruct((B, cfg["n_blocks"], nq, R), F32)]
    else:
        kv_spec = pl.BlockSpec((1, MEM_LEN, LANES), lambda b, h: (b, 0, h))
        out_specs = [row_spec, kv_spec, kv_spec]
        out_shape = [jax.ShapeDtypeStruct((B, S, W), BF16)] + [jax.ShapeDtypeStruct((B, MEM_LEN, W), BF16)] * 2
    scratch = [pltpu.VMEM((nq, LANES, R), BF16), pltpu.VMEM((Sk, LANES), BF16),
               pltpu.VMEM((Sk, LANES), BF16), pltpu.VMEM((nk, LANES, TK), BF16), pltpu.VMEM((nq, LANES, R), BF16),
               pltpu.VMEM((nq, R), F32), pltpu.VMEM((nk, LANES, TK), F32), pltpu.VMEM((nk, LANES, TK), F32)]
    pair_bufs = [pltpu.VMEM((TK, R), F32), pltpu.VMEM((TK, R), F32), pltpu.VMEM((TK, R), BF16), pltpu.VMEM((TK, R), BF16)]
    scratch += pair_bufs + pair_bufs
    if has_bias:
        scratch += [pltpu.VMEM((nh, Sk, LANES), F32), pltpu.VMEM((nh, Sk, LANES), F32)]
    return pl.pallas_call(
        body, name=kind + "_attn_bwd", grid=(B, cfg["n_blocks"]),
        in_specs=in_specs, out_specs=out_specs, out_shape=out_shape, scratch_shapes=scratch,
        compiler_params=_params(("arbitrary", "arbitrary")),
    )(*ins)


def _sigmoid(g):
    return 1.0 / (1.0 + jnp.exp(-g))


def out_step(proj, o_fox, o_dil, o_mem, w_out, x, target, gf, tm):
    T = x.shape[0]

    def body(fg_ref, dg_ref, mg_ref, of_ref, od_ref, om_ref, w_ref, x_ref, t_ref, gf_ref,
             dx_ref, dof_ref, dod_ref, dom_ref, dfg_ref, ddg_ref, dmg_ref, gw_ref, sm_ref, gw_acc):
        branches = []
        for g_ref, o_ref in ((fg_ref, of_ref), (dg_ref, od_ref), (mg_ref, om_ref)):
            g = g_ref[...]
            sg = _sigmoid(g)
            o = o_ref[...]
            branches.append((g, sg, o))
        ymix = jnp.concatenate([(o * (g * sg)).astype(BF16) for g, sg, o in branches], axis=1)
        x2 = x_ref[...] + jnp.dot(ymix, w_ref[...], preferred_element_type=F32)
        r = lax.rsqrt(jnp.mean(x2 * x2, axis=-1, keepdims=True) + RMS_EPS)
        yn = x2 * r
        err = yn * gf_ref[...] - t_ref[...]
        loss = 0.5 * jnp.sum(jnp.sum(err * err, axis=-1, keepdims=True) / D_MODEL, axis=0, keepdims=True)
        dyf = err / D_MODEL
        dgf = jnp.sum(dyf * yn, axis=0, keepdims=True)
        dyn = dyf * gf_ref[...]
        dx2 = r * (dyn - yn * jnp.mean(dyn * yn, axis=-1, keepdims=True))
        dx_ref[...] = dx2
        dxb = dx2.astype(BF16)
        dmix = lax.dot_general(dxb, w_ref[...], (((1,), (1,)), ((), ())), preferred_element_type=F32)
        col = 0
        for (g, sg, o), do_ref, dgate_ref in zip(branches, (dof_ref, dod_ref, dom_ref), (dfg_ref, ddg_ref, dmg_ref)):
            d = dmix[:, col:col + g.shape[1]]
            col += g.shape[1]
            do_ref[...] = (d * (g * sg)).astype(BF16)
            dgate_ref[...] = (d * o * (sg * (1.0 + g * (1.0 - sg)))).astype(BF16)
        row = lax.broadcasted_iota(jnp.int32, (8, D_MODEL), 0)
        upd = jnp.where(row == 0, dgf, jnp.where(row == 1, loss, 0.0))

        @pl.when(pl.program_id(0) == 0)
        def _():
            sm_ref[...] = jnp.zeros(sm_ref.shape, F32)
            gw_acc[...] = jnp.zeros(gw_acc.shape, F32)

        sm_ref[...] += upd
        gw_acc[...] += lax.dot_general(ymix, dxb, (((0,), (0,)), ((), ())), preferred_element_type=F32)

        @pl.when(pl.program_id(0) == T // tm - 1)
        def _():
            gw_ref[...] = gw_acc[...].astype(BF16)

    def rows(w, col=0):
        return pl.BlockSpec((tm, w), lambda i: (i, col))

    return pl.pallas_call(
        body, name="out_step", grid=(T // tm,),
        in_specs=[rows(FOX_W, B_FG // FOX_W), rows(DIL_W, B_DG // DIL_W), rows(MEM_W, B_MG // MEM_W),
                  rows(FOX_W), rows(DIL_W), rows(MEM_W),
                  pl.BlockSpec((MIX_W, D_MODEL), lambda i: (0, 0)),
                  rows(D_MODEL), rows(D_MODEL), pl.BlockSpec((1, D_MODEL), lambda i: (0, 0))],
        out_specs=[rows(D_MODEL), rows(FOX_W), rows(DIL_W), rows(MEM_W), rows(FOX_W), rows(DIL_W), rows(MEM_W),
                   pl.BlockSpec((MIX_W, D_MODEL), lambda i: (0, 0)), pl.BlockSpec((8, D_MODEL), lambda i: (0, 0))],
        out_shape=[jax.ShapeDtypeStruct((T, D_MODEL), F32), jax.ShapeDtypeStruct((T, FOX_W), BF16),
                   jax.ShapeDtypeStruct((T, DIL_W), BF16), jax.ShapeDtypeStruct((T, MEM_W), BF16),
                   jax.ShapeDtypeStruct((T, FOX_W), BF16), jax.ShapeDtypeStruct((T, DIL_W), BF16),
                   jax.ShapeDtypeStruct((T, MEM_W), BF16), jax.ShapeDtypeStruct((MIX_W, D_MODEL), BF16),
                   jax.ShapeDtypeStruct((8, D_MODEL), F32)],
        scratch_shapes=[pltpu.VMEM((MIX_W, D_MODEL), F32)],
        compiler_params=_params(("arbitrary",)),
    )(proj, proj, proj, o_fox, o_dil, o_mem, w_out, x, target, gf)


def adamw(w, g, m, v, tr, name):
    lead = w.shape[:-2]
    R, C = w.shape[-2:]
    zeros = (0,) * len(lead)

    def body(w_ref, g_ref, m_ref, v_ref, d_ref, mo_ref, vo_ref):
        d_ref[...], mo_ref[...], vo_ref[...] = _adamw_update(w_ref[...], g_ref[...], m_ref[...], v_ref[...])

    spec = pl.BlockSpec((1,) * len(lead) + (tr, C), lambda i: zeros + (i, 0))
    return pl.pallas_call(
        body, name=name, grid=(pl.cdiv(R, tr),),
        in_specs=[spec] * 4, out_specs=[spec] * 3,
        out_shape=[jax.ShapeDtypeStruct(w.shape, F32)] * 3,
        compiler_params=_params(("arbitrary",)),
    )(w, g, m, v)


def adamw_columns_first(w, g, m, v, name):
    N = w.shape[0]
    parts, step_rows = 4, 16
    size = -(-N // (parts * step_rows)) * step_rows
    bounds = [(p * size, min((p + 1) * size, N)) for p in range(parts)]

    def body(w_hbm, g_hbm, m_hbm, v_hbm, d_hbm, mo_hbm, vo_hbm, wb, gb, mb, vb, db, mob, vob, load_sems, store_sems):
        ins = ((w_hbm, wb), (g_hbm, gb), (m_hbm, mb), (v_hbm, vb))
        outs = ((d_hbm, db), (mo_hbm, mob), (vo_hbm, vob))

        def rows_copy(src, dst, lo, hi, sem):
            return pltpu.make_async_copy(src.at[pl.ds(lo, hi - lo), :], dst.at[pl.ds(lo, hi - lo), :], sem)

        def update(rows):
            db[rows, :], mob[rows, :], vob[rows, :] = _adamw_update(wb[rows, :], gb[rows, :], mb[rows, :], vb[rows, :])

        loads = [[rows_copy(h.reshape(N, LANES), b, lo, hi, load_sems.at[p, k]) for k, (h, b) in enumerate(ins)]
                 for p, (lo, hi) in enumerate(bounds)]
        for part in loads:
            for cp in part:
                cp.start()
        stores = []
        for p, (lo, hi) in enumerate(bounds):
            for cp in loads[p]:
                cp.wait()
            whole = (hi - lo) // step_rows

            def step(i, _, lo=lo):
                update(pl.ds(pl.multiple_of(lo + i * step_rows, step_rows), step_rows))
                return 0

            lax.fori_loop(0, whole, step, 0, unroll=4)
            if lo + whole * step_rows < hi:
                update(slice(lo + whole * step_rows, hi))
            leaving = [rows_copy(b, h.reshape(N, LANES), lo, hi, store_sems.at[p, k]) for k, (h, b) in enumerate(outs)]
            for cp in leaving:
                cp.start()
            stores += leaving
        for cp in stores:
            cp.wait()

    any_spec = pl.BlockSpec(memory_space=pl.ANY)
    return pl.pallas_call(
        body, name=name,
        in_specs=[any_spec] * 4, out_specs=[any_spec] * 3,
        out_shape=[jax.ShapeDtypeStruct(w.shape, F32)] * 3,
        scratch_shapes=[pltpu.VMEM((N, LANES), F32)] * 7
        + [pltpu.SemaphoreType.DMA((parts, 4)), pltpu.SemaphoreType.DMA((parts, 3))],
        compiler_params=_params(),
    )(w, g, m, v)


def _pad_row(v, width):
    return jnp.concatenate([v, jnp.zeros((1, width - v.shape[1]), v.dtype)], axis=1)


def local_grads(x, mem, norm_g, b_forget, mem_norm_g, final_norm_g, loss_target, first_token, first_weights,
                late_weights, start_exchange):
    B, S, D = x.shape
    T = B * S
    xt = x.reshape(T, D)
    memt = mem.reshape(B * MEM_LEN, D)
    b_pad = _pad_row(b_forget, LANES)

    h, h_t = rms_fwd(xt, norm_g, 512, "rms_x", with_transpose=True, token=first_token)
    mh, mh_t = rms_fwd(memt, mem_norm_g, B * MEM_LEN, "rms_mem", with_transpose=True, token=first_token)
    w_in_a, proj_token = first_weights([h, mh])
    proj_a = mm_nn(h, w_in_a, 1024, PA, "in_proj_a", proj_token)
    proj_a3 = proj_a.reshape(B, S, PA)

    negc = fox_gate(proj_a3, b_pad)
    causal = _log_masks_t(S, "causal")
    dilated = _log_masks_t(S, "dilated")
    rope = _rope_tables(S)

    o_fox, lse_fox = attn_fwd("fox", proj_a3, S, negc_cols=negc, mask=causal)

    w_in_b, w_kv, w_out = late_weights(o_fox)
    proj_b = mm_nn(h, w_in_b, 1024, PB // 2, "in_proj_b")
    proj_b3 = proj_b.reshape(B, S, PB)
    o_dil, lse_dil = attn_fwd("dil", proj_b3, S, mask=dilated, rope=rope)

    mkv = mm_nn(mh, w_kv, B * MEM_LEN, 2 * MEM_W, "mem_kv_proj")
    mkv3 = mkv.reshape(B, MEM_LEN, 2 * MEM_W)
    o_mem, lse_mem = attn_fwd("mem", proj_b3, S, kv=mkv3)

    dx2, do_fox, do_dil, do_mem, dfg, ddg, dmg, g_out, small_out = out_step(
        proj_b, o_fox.reshape(T, FOX_W), o_dil.reshape(T, DIL_W), o_mem.reshape(T, MEM_W), w_out,
        xt, loss_target.reshape(T, D), final_norm_g.reshape(1, D), 256)

    gates = [(dfg, 1, B_FG), (ddg, 1, B_DG), (dmg, 1, B_MG)]
    g_gates = mm_tn_multi(h_t, [piece[0] for piece in gates], 1024, "w_in_grad_gates", BF16)

    dqkv_fox, dneg, drow = attn_bwd("fox", proj_a3, do_fox.reshape(B, S, FOX_W), o_fox, lse_fox, S,
                                    negc_cols=negc, mask=causal)
    drow = drow.reshape(B, FOX_HEADS // 2, S // TQ, 2, TQ).transpose(0, 1, 3, 2, 4).reshape(B, FOX_HEADS, S)
    drow = jnp.pad(drow, ((0, 0), (0, LANES - FOX_HEADS), (0, 0)))
    dflog, db_part = fox_gate_bwd(drow, dneg, proj_a3, b_pad)
    fox = [(dqkv_fox.reshape(T, 3 * FOX_W), 0, A_FOX), (dflog.reshape(T, LANES), 0, A_FLOG)]
    g_fox = mm_tn_multi(h_t, [piece[0] for piece in fox], 1024, "w_in_grad_fox", BF16)
    first, token = start_exchange([g_gates, g_out, g_fox], "early_exchange_a")

    (dqkv_dil,) = attn_bwd("dil", proj_b3, do_dil.reshape(B, S, DIL_W), o_dil, lse_dil, S, mask=dilated, rope=rope,
                           token=token)
    dil = [(dqkv_dil.reshape(T, 3 * DIL_W), 1, B_DIL)]
    g_dil = mm_tn_multi(h_t, [piece[0] for piece in dil], 1024, "w_in_grad_dil", BF16)
    second, token = start_exchange([g_dil], "early_exchange_b")

    dmq, dmk, dmv = attn_bwd("mem", proj_b3, do_mem.reshape(B, S, MEM_W), o_mem, lse_mem, S, kv=mkv3, token=token)
    mq = [(dmq.reshape(T, MEM_W), 1, B_MQ)]
    g_mq = mm_tn_multi(h_t, [piece[0] for piece in mq], 1024, "w_in_grad_mq", BF16)
    dmkv = jnp.concatenate([dmk, dmv], axis=2).reshape(B * MEM_LEN, 2 * MEM_W)
    g_kv = mm_tn_multi(mh_t, [dmkv], B * MEM_LEN, "w_kv_grad", BF16)
    third, token = start_exchange([g_mq, g_kv], "early_exchange_c")

    grad_x, dng = in_proj_bwd_rms(gates + fox + dil + mq, (w_in_a, w_in_b), xt, norm_g, dx2, 512, token)
    dmh = mm_nt(dmkv, w_kv, B * MEM_LEN, D, "mem_kv_bwd")
    _, dmng = rms_bwd(memt, mem_norm_g, dmh, None, B * MEM_LEN, "rms_mem_bwd")

    small = jnp.concatenate([dng[0:1], dmng[0:1], small_out[0:1], _pad_row(db_part[0:1], D), small_out[1:2],
                             jnp.zeros((3, D), F32)], axis=0)
    early = [(first, dqkv_dil), (second, dmq), (third, grad_x)]
    return grad_x.reshape(B, S, D), early, small


def kernel(x, mem, norm_g, w_in, b_forget, mem_norm_g, w_mem_kv, w_out, final_norm_g, loss_target, m_norm_g, m_w_in, m_b_forget, m_mem_norm_g, m_w_mem_kv, m_w_out, m_final_norm_g, v_norm_g, v_w_in, v_b_forget, v_mem_norm_g, v_w_mem_kv, v_w_out, v_final_norm_g):
    D = D_MODEL
    shard_a, shard_b = _split_cols(_pack_cols(w_in).astype(BF16).reshape(w_in.shape[1], PW))
    gather_a, first_token = early_exchange_start([shard_a], "first_gather", gather=True,
                                                 relations=_SIBLING_AND_SAME_CORES)
    late_shards = [shard_b, w_mem_kv[0].astype(BF16), w_out[0].astype(BF16)]
    late_lands = own_slots(late_shards, "late_gather_place", after=first_token)
    late = {}

    def first_weights(after):
        _, gathered = early_exchange_wait(gather_a, list(after) + [late_lands[0]], "first_gather_wait")
        (w_in_a,) = pass_on_to_sibling(gathered, gather_a["rows"], "first_gather_pass")
        late["gather"], token = early_exchange_start(
            late_shards, "late_gather", gather=True, after=w_in_a, relations=_SIBLING_AND_SAME_CORES,
            lands=late_lands)
        return w_in_a, token

    def late_weights(after):
        _, gathered = early_exchange_wait(late["gather"], after, "late_gather_wait")
        return pass_on_to_sibling(gathered, late["gather"]["rows"], "late_gather_pass")

    grad_x, early, small = local_grads(
        x, mem, norm_g, b_forget, mem_norm_g, final_norm_g, loss_target, first_token, first_weights, late_weights,
        early_exchange_start)

    (first, after_first), (second, after_second), (third, after_third) = early
    (src_gates, src_out, src_fox), (land_gates, land_out, land_fox) = early_exchange_wait(
        first, after_first, "early_wait_a")
    (src_dil,), (land_dil,) = early_exchange_wait(second, after_second, "early_wait_b")
    (src_mq, src_kv), (land_mq, land_kv) = early_exchange_wait(third, after_third, "early_wait_c")
    gw_out, d_out, m_out, v_out = slot_sum8_adamw(src_out, land_out, w_out[0], m_w_out[0], v_w_out[0], 256,
                                                  "sum_adamw_w_out")
    gw_kv, d_kv, m_kv, v_kv = slot_sum8_adamw(src_kv, land_kv, w_mem_kv[0], m_w_mem_kv[0], v_w_mem_kv[0], 128,
                                              "sum_adamw_w_kv")
    gw_in_cols = w_in_grad_sum(
        [(src_fox, land_fox, [(0, P_FOX, 3 * FOX_W), (3 * FOX_W, P_FLOG, LANES)]),
         (src_gates, land_gates, [(0, P_FG, FOX_W), (FOX_W, P_DG, DIL_W), (FOX_W + DIL_W, P_MG, MEM_W)]),
         (src_dil, land_dil, [(0, P_DIL, 3 * DIL_W)]),
         (src_mq, land_mq, [(0, P_MQ, MEM_W)])], "sum_w_in")
    gw_in = jnp.transpose(gw_in_cols, (1, 2, 0))

    tot = small_all_reduce(small)

    loss = tot[4, 0]
    g_norm, g_mem_norm, g_final, g_b = tot[0:1], tot[1:2], tot[2], tot[3:4, :FOX_HEADS]

    def rows8(*rows):
        rows = [r.reshape(1, -1) for r in rows]
        rows = [_pad_row(r, D) for r in rows]
        return jnp.concatenate(rows + [jnp.zeros((8 - len(rows), D), F32)], axis=0)

    sw = rows8(norm_g, mem_norm_g, final_norm_g, b_forget)
    sm = rows8(m_norm_g, m_mem_norm_g, m_final_norm_g, m_b_forget)
    sv = rows8(v_norm_g, v_mem_norm_g, v_final_norm_g, v_b_forget)
    d_s, m_s, v_s = adamw(sw, tot, sm, sv, 8, "adamw_small")
    columns_first = lambda a: jnp.transpose(a, (2, 0, 1))
    d_in, m_in, v_in = [jnp.transpose(o, (1, 2, 0)) for o in adamw_columns_first(
        columns_first(w_in), gw_in_cols, columns_first(m_w_in), columns_first(v_w_in), "adamw_w_in")]

    def small_outs(t):
        return t[0:1], t[3:4, :FOX_HEADS], t[1:2], t[2]

    grads = (g_norm, gw_in, g_b, g_mem_norm, gw_kv[None], gw_out[None], g_final)
    outs = []
    for t, big in ((d_s, (d_in, d_kv, d_out)), (m_s, (m_in, m_kv, m_out)), (v_s, (v_in, v_kv, v_out))):
        n, b, mn, f = small_outs(t)
        outs += [n, big[0], b, mn, big[1][None], big[2][None], f]
    return (loss, grad_x, *grads, *outs)
```

```python
import math

import numpy as np
import jax
import jax.numpy as jnp
from jax import lax
from jax.experimental import pallas as pl
from jax.experimental.pallas import tpu as pltpu

F32 = jnp.float32
BF16 = jnp.bfloat16

D_MODEL = 1024
HEAD_DIM = 64
FOX_HEADS = 12
DIL_HEADS = 12
MEM_HEADS = 4
MEM_HEAD_DIM = 128
MEM_LEN = 256
FOX_W = FOX_HEADS * HEAD_DIM
DIL_W = DIL_HEADS * HEAD_DIM
MEM_W = MEM_HEADS * MEM_HEAD_DIM
MIX_W = FOX_W + DIL_W + MEM_W
DILATIONS = ((128, 1), (512, 4), (2048, 16))
ROPE_THETA = 500000.0
ROPE_DIM = HEAD_DIM // 4
RMS_EPS = 1e-6
NEG_INF = -1e30
IN_W = 4 * FOX_W + FOX_HEADS + 4 * DIL_W + 2 * MEM_W

ADAM_LR = 0.001
ADAM_B1 = 0.9
ADAM_B2 = 0.999
ADAM_EPS = 1e-08
ADAM_WD = 0.01
ADAM_STEP = 10

N_DEV = 8
LANES = 128
PAIR_W = 3 * LANES
TQ = 256
TK = 256

O_FQ, O_FK, O_FV, O_FG = 0, FOX_W, 2 * FOX_W, 3 * FOX_W
O_FLOG = 4 * FOX_W
O_DQ = O_FLOG + FOX_HEADS
O_DK, O_DV, O_DG = O_DQ + DIL_W, O_DQ + 2 * DIL_W, O_DQ + 3 * DIL_W
O_MQ = O_DQ + 4 * DIL_W
O_MG = O_MQ + MEM_W
P_FOX = 0
P_FG = P_FOX + 3 * FOX_W
P_DIL = P_FG + FOX_W
P_DG = P_DIL + 3 * DIL_W
P_MQ = P_DG + DIL_W
P_MG = P_MQ + MEM_W
P_FLOG = P_MG + MEM_W
PW = P_FLOG + LANES
A_FOX = 0
A_FLOG = A_FOX + 3 * FOX_W
PA = A_FLOG + LANES
B_FG = 0
B_DG = B_FG + FOX_W
B_DIL = B_DG + DIL_W
B_MQ = B_DIL + 3 * DIL_W
B_MG = -(-(B_MQ + MEM_W) // MEM_W) * MEM_W
PB = B_MG + MEM_W

VMEM_LIMIT = 56 * 1024 * 1024


def _pack_pieces():
    pieces = []
    for base in (O_FQ, O_DQ):
        seg = []
        for hp in range(FOX_HEADS // 2):
            for part in range(3):
                seg.append((base + part * FOX_W + hp * LANES, LANES))
        pieces.append(seg)
    fox, dil = pieces
    return fox + [(O_FG, FOX_W)] + dil + [(O_DG, DIL_W), (O_MQ, MEM_W), (O_MG, MEM_W), (O_FLOG, FOX_HEADS)]


def _pack_cols(w):
    parts = [w[..., s:s + n] for s, n in _pack_pieces()]
    parts.append(jnp.zeros(w.shape[:-1] + (LANES - FOX_HEADS,), w.dtype))
    return jnp.concatenate(parts, axis=-1)


def _split_cols(wp):
    def cut(start, width):
        return wp[..., start:start + width]

    group_a = jnp.concatenate([cut(P_FOX, 3 * FOX_W), cut(P_FLOG, LANES)], axis=-1)
    pad = jnp.zeros(wp.shape[:-1] + (B_MG - B_MQ - MEM_W,), wp.dtype)
    group_b = jnp.concatenate([cut(P_FG, FOX_W), cut(P_DG, DIL_W), cut(P_DIL, 3 * DIL_W), cut(P_MQ, MEM_W), pad,
                               cut(P_MG, MEM_W)], axis=-1)
    return group_a, group_b


def _params(sem=None, **kw):
    return pltpu.CompilerParams(dimension_semantics=sem, vmem_limit_bytes=VMEM_LIMIT, **kw)


def _mesh_pos():
    return lax.axis_index("x"), lax.axis_index("y"), lax.axis_index("c")


def _flip(v, d):
    return 1 - v if d else v


_RELATIONS = [(dx, dy, dc) for dx in (0, 1) for dy in (0, 1) for dc in (0, 1)][1:]
_SIBLING_AND_SAME_CORES = [(0, 0, 1), (1, 0, 0), (0, 1, 0), (1, 1, 0)]


_OTHER_CHIPS = [(1, 0), (0, 1), (1, 1)]


_HBM = pl.BlockSpec(memory_space=pltpu.HBM)
_SEM = pl.BlockSpec(memory_space=pltpu.SEMAPHORE)
_EFFECT = pltpu.SideEffectType.DATAFLOW_SIDE_EFFECTING


def _early_copies(src_refs, land_refs, send_sems, recv_sems, rows, gather, relations):
    x, y, c = _mesh_pos()
    me = 4 * x + 2 * y + c
    copies = []
    for a in range(len(src_refs)):
        for dx, dy, dc in relations:
            px, py, pc = _flip(x, dx), _flip(y, dy), _flip(c, dc)
            peer = 4 * px + 2 * py + pc
            copies.append(pltpu.make_async_remote_copy(
                src_ref=src_refs[a] if gather else src_refs[a].at[pl.ds(peer * rows[a], rows[a]), :],
                dst_ref=land_refs[a].at[pl.ds(me * rows[a], rows[a]), :],
                send_sem=send_sems[a], recv_sem=recv_sems[a],
                device_id=(px, py, pc), device_id_type=pl.DeviceIdType.MESH))
    return copies


def own_slots(shards, name, after=None):
    n = len(shards)
    extra = [] if after is None else [after]
    x, y, c = _mesh_pos()
    me = (4 * x + 2 * y + c).astype(jnp.int32).reshape(1)
    empties = [lax.empty((N_DEV * s.shape[0], s.shape[1]), s.dtype) for s in shards]

    def body(me_ref, *refs):
        for a in range(n):
            refs[2 * n + len(extra) + a][...] = refs[a][...]

    return pl.pallas_call(
        body, name=name,
        grid_spec=pltpu.PrefetchScalarGridSpec(
            num_scalar_prefetch=1, grid=(1,),
            in_specs=[pl.BlockSpec(s.shape, lambda i, w: (0, 0)) for s in shards]
            + [pl.BlockSpec(memory_space=pl.ANY)] * (n + len(extra)),
            out_specs=[pl.BlockSpec(s.shape, lambda i, w: (w[0], 0)) for s in shards]),
        out_shape=[jax.ShapeDtypeStruct(e.shape, e.dtype) for e in empties],
        input_output_aliases={1 + n + a: a for a in range(n)},
        compiler_params=_params(("arbitrary",)),
    )(me, *shards, *empties, *extra)


def early_exchange_start(srcs, name, gather=False, after=None, relations=_RELATIONS, lands=None):
    n = len(srcs)
    if gather:
        rows = [s.shape[0] for s in srcs]
        lands = list(own_slots(srcs, name + "_place") if lands is None else lands)
    else:
        rows = [s.shape[0] // N_DEV for s in srcs]
        lands = [lax.empty(s.shape, s.dtype) for s in srcs]

    extra = [] if after is None else [after]

    def body(*refs):
        src_refs, land_refs = refs[:n], refs[n:2 * n]
        first_sem = 2 * n + len(extra)
        send_sems, recv_sems = refs[first_sem:first_sem + n], refs[first_sem + n:first_sem + 2 * n]
        token = refs[-1]
        for cp in _early_copies(src_refs, land_refs, send_sems, recv_sems, rows, gather, relations):
            cp.start()
        token[...] = jnp.zeros_like(token)

    hbm = lambda a: pltpu.HBM(a.shape, a.dtype)
    outs = pl.pallas_call(
        body, name=name,
        out_shape=[pltpu.SemaphoreType.DMA(())] * (2 * n)
        + [hbm(a) for a in srcs] + [hbm(a) for a in lands] + [jax.ShapeDtypeStruct((8, LANES), F32)],
        in_specs=[_HBM] * (2 * n) + [pl.BlockSpec(memory_space=pl.ANY)] * len(extra),
        out_specs=[_SEM] * (2 * n) + [_HBM] * (2 * n) + [pl.BlockSpec(memory_space=pltpu.VMEM)],
        input_output_aliases={i: 2 * n + i for i in range(2 * n)},
        compiler_params=pltpu.CompilerParams(has_side_effects=_EFFECT),
    )(*[pltpu.with_memory_space_constraint(a, pltpu.HBM) for a in list(srcs) + lands], *extra)
    handle = dict(sems=outs[:2 * n], srcs=outs[2 * n:3 * n], lands=outs[3 * n:4 * n], rows=rows,
                  copies=len(relations))
    return handle, outs[-1]


def early_exchange_wait(handle, after, name):
    n = len(handle["srcs"])
    rows = handle["rows"]
    after = list(after) if isinstance(after, (list, tuple)) else [after]

    def body(*refs):
        src_refs, land_refs = refs[:n], refs[n:2 * n]
        send_sems, recv_sems = refs[2 * n:3 * n], refs[3 * n:4 * n]
        x, y, c = _mesh_pos()
        for a in range(n):
            span = pl.ds(0, handle["copies"] * rows[a])
            all_copies = pltpu.make_async_remote_copy(
                src_ref=land_refs[a].at[span, :], dst_ref=land_refs[a].at[span, :],
                send_sem=send_sems[a], recv_sem=recv_sems[a],
                device_id=(x, y, c), device_id_type=pl.DeviceIdType.MESH)
            all_copies.wait_send()
            all_copies.wait_recv()

    hbm = lambda a: pltpu.HBM(a.shape, a.dtype)
    ins = list(handle["srcs"]) + list(handle["lands"])
    outs = pl.pallas_call(
        body, name=name,
        out_shape=[hbm(a) for a in ins],
        in_specs=[_HBM] * (2 * n) + [_SEM] * (2 * n) + [pl.BlockSpec(memory_space=pl.ANY)] * len(after),
        out_specs=[_HBM] * (2 * n),
        input_output_aliases={i: i for i in range(2 * n)},
        compiler_params=pltpu.CompilerParams(has_side_effects=_EFFECT),
    )(*ins, *handle["sems"], *after)
    return outs[:n], outs[n:]


def pass_on_to_sibling(lands, rows, name):
    n = len(lands)

    def body(*refs):
        land_refs = refs[n:2 * n]
        send_sems, recv_sems = refs[2 * n:]
        x, y, c = _mesh_pos()
        copies = []
        for a in range(n):
            for k, (dx, dy) in enumerate(_OTHER_CHIPS):
                slot = 4 * _flip(x, dx) + 2 * _flip(y, dy) + c
                blk = land_refs[a].at[pl.ds(slot * rows[a], rows[a]), :]
                copies.append(pltpu.make_async_remote_copy(
                    src_ref=blk, dst_ref=blk, send_sem=send_sems.at[a, k], recv_sem=recv_sems.at[a, k],
                    device_id=(x, y, 1 - c), device_id_type=pl.DeviceIdType.MESH))
        for cp in copies:
            cp.start()
        for cp in copies:
            cp.wait_recv()
        for cp in copies:
            cp.wait_send()

    any_spec = pl.BlockSpec(memory_space=pl.ANY)
    return pl.pallas_call(
        body, name=name,
        out_shape=[jax.ShapeDtypeStruct(a.shape, a.dtype) for a in lands],
        in_specs=[any_spec] * n, out_specs=[any_spec] * n,
        input_output_aliases={i: i for i in range(n)},
        scratch_shapes=[pltpu.SemaphoreType.DMA((n, len(_OTHER_CHIPS))), pltpu.SemaphoreType.DMA((n, len(_OTHER_CHIPS)))],
    )(*lands)


def _adamw_update(w, g, m, v):
    mn = ADAM_B1 * m + (1.0 - ADAM_B1) * g
    vn = ADAM_B2 * v + (1.0 - ADAM_B2) * jnp.square(g)
    m_hat = mn / (1.0 - ADAM_B1 ** ADAM_STEP)
    v_hat = vn / (1.0 - ADAM_B2 ** ADAM_STEP)
    return -ADAM_LR * (m_hat / (jnp.sqrt(v_hat) + ADAM_EPS) + ADAM_WD * w), mn, vn


def slot_sum8_adamw(src, land, w, m, v, tr, name):
    rows, cols = land.shape[0] // N_DEV, land.shape[1]
    x, y, c = _mesh_pos()
    me = (4 * x + 2 * y + c).astype(jnp.int32).reshape(1)

    def body(me_ref, src_ref, land_ref, w_ref, m_ref, v_ref, g_ref, d_ref, mo_ref, vo_ref):
        acc = None
        for d in range(N_DEV):
            term = jnp.where(d == me_ref[0], src_ref[0], land_ref[d]).astype(F32)
            acc = term if acc is None else acc + term
        g_ref[...] = acc
        d_ref[...], mo_ref[...], vo_ref[...] = _adamw_update(w_ref[...], acc, m_ref[...], v_ref[...])

    tile = pl.BlockSpec((tr, cols), lambda i, w: (i, 0))
    return pl.pallas_call(
        body, name=name,
        grid_spec=pltpu.PrefetchScalarGridSpec(
            num_scalar_prefetch=1, grid=(rows // tr,),
            in_specs=[pl.BlockSpec((1, tr, cols), lambda i, w: (w[0], i, 0)),
                      pl.BlockSpec((N_DEV, tr, cols), lambda i, w: (0, i, 0)), tile, tile, tile],
            out_specs=[tile] * 4),
        out_shape=[jax.ShapeDtypeStruct((rows, cols), F32)] * 4,
        compiler_params=_params(("arbitrary",)),
    )(me, src.reshape(N_DEV, rows, cols), land.reshape(N_DEV, rows, cols), w, m, v)


def w_in_grad_sum(groups, name):
    rows = groups[0][0].shape[0] // N_DEV
    runs, pos = [], 0
    for start, width in _pack_pieces():
        runs.append((start, width, pos))
        pos += width
    n = len(groups)

    def body(*refs):
        src_refs, land_refs = refs[0:2 * n:2], refs[1:2 * n:2]
        g_ref = refs[2 * n]
        own_bufs, land_bufs = refs[2 * n + 1:3 * n + 1], refs[3 * n + 1:4 * n + 1]
        stage, load_sems, store_sems = refs[4 * n + 1:]
        x, y, c = _mesh_pos()
        me = 4 * x + 2 * y + c
        loads = []
        for k in range(n):
            pair = [pltpu.make_async_copy(src_refs[k].at[pl.ds(me * rows, rows), :], own_bufs[k], load_sems.at[k, 0]),
                    pltpu.make_async_copy(land_refs[k], land_bufs[k], load_sems.at[k, 1])]
            for cp in pair:
                cp.start()
            loads.append(pair)
        for k, (_, _, segments) in enumerate(groups):
            for cp in loads[k]:
                cp.wait()
            for first, packed, width in segments:
                for off in range(0, width, LANES):
                    cols = slice(first + off, first + off + LANES)
                    acc = None
                    for d in range(N_DEV):
                        term = jnp.where(d == me, own_bufs[k][:, cols], land_bufs[k][d * rows:(d + 1) * rows, cols])
                        acc = term.astype(F32) if acc is None else acc + term.astype(F32)
                    stage[packed + off:packed + off + LANES, :] = acc.T
        g_rows = g_ref.reshape(IN_W, LANES)
        stores = [pltpu.make_async_copy(stage.at[pl.ds(p, width), :], g_rows.at[pl.ds(start, width), :], store_sems.at[r])
                  for r, (start, width, p) in enumerate(runs)]
        for cp in stores:
            cp.start()
        for cp in stores:
            cp.wait()

    any_spec = pl.BlockSpec(memory_space=pl.ANY)
    operands = [a for src, land, _ in groups for a in (src, land)]
    return pl.pallas_call(
        body, name=name,
        in_specs=[any_spec] * (2 * n), out_specs=any_spec,
        out_shape=jax.ShapeDtypeStruct((IN_W, 1, LANES), F32),
        scratch_shapes=[pltpu.VMEM((rows, src.shape[1]), src.dtype) for src, _, _ in groups]
        + [pltpu.VMEM(land.shape, land.dtype) for _, land, _ in groups]
        + [pltpu.VMEM((PW, LANES), F32), pltpu.SemaphoreType.DMA((n, 2)), pltpu.SemaphoreType.DMA((len(runs),))],
        compiler_params=_params(),
    )(*operands)


def mm_tn_multi(a_t, bs, tt, name, out_dtype=F32):
    K, T = a_t.shape
    widths = [b.shape[1] for b in bs]
    steps = T // tt

    def body(a_ref, *rest):
        b_refs, o_ref, acc = rest[:-2], rest[-2], rest[-1]

        @pl.when(pl.program_id(0) == 0)
        def _():
            acc[...] = jnp.zeros(acc.shape, F32)

        av = a_ref[...]
        col = 0
        for b_ref, w in zip(b_refs, widths):
            acc[:, col:col + w] += jnp.dot(av, b_ref[...], preferred_element_type=F32)
            col += w

        @pl.when(pl.program_id(0) == steps - 1)
        def _():
            o_ref[...] = acc[...].astype(out_dtype)

    return pl.pallas_call(
        body, name=name, grid=(steps,),
        in_specs=[pl.BlockSpec((K, tt), lambda t: (0, t))] + [pl.BlockSpec((tt, w), lambda t: (t, 0)) for w in widths],
        out_specs=pl.BlockSpec((K, sum(widths)), lambda t: (0, 0)),
        out_shape=jax.ShapeDtypeStruct((K, sum(widths)), out_dtype),
        scratch_shapes=[pltpu.VMEM((K, sum(widths)), F32)],
        compiler_params=_params(("arbitrary",)),
    )(a_t, *bs)


def rms_fwd(x, g, tm, name, with_transpose=False, token=None):
    M, K = x.shape
    extra = [] if token is None else [token]

    def body(x_ref, g_ref, *rest):
        o_ref = rest[len(extra)]
        xv = x_ref[...]
        r = lax.rsqrt(jnp.mean(xv * xv, axis=-1, keepdims=True) + RMS_EPS)
        h = ((xv * r) * g_ref[...]).astype(BF16)
        o_ref[...] = h
        if with_transpose:
            rest[len(extra) + 1][...] = h.T

    out_specs = [pl.BlockSpec((tm, K), lambda i: (i, 0))]
    out_shape = [jax.ShapeDtypeStruct((M, K), BF16)]
    if with_transpose:
        out_specs.append(pl.BlockSpec((K, tm), lambda i: (0, i)))
        out_shape.append(jax.ShapeDtypeStruct((K, M), BF16))
    outs = pl.pallas_call(
        body, name=name, grid=(M // tm,),
        in_specs=[pl.BlockSpec((tm, K), lambda i: (i, 0)), pl.BlockSpec((1, K), lambda i: (0, 0))]
        + [pl.BlockSpec(t.shape, lambda i: (0, 0)) for t in extra],
        out_specs=out_specs, out_shape=out_shape,
        compiler_params=_params(("arbitrary",)),
    )(x, g, *extra)
    return outs if with_transpose else outs[0]


def rms_bwd(x, g, dh, dres, tm, name):
    M, K = x.shape
    has_res = dres is not None

    def body(*refs):
        if has_res:
            x_ref, g_ref, dh_ref, dres_ref, dx_ref, dg_ref = refs
        else:
            x_ref, g_ref, dh_ref, dx_ref, dg_ref = refs
        xv = x_ref[...]
        r = lax.rsqrt(jnp.mean(xv * xv, axis=-1, keepdims=True) + RMS_EPS)
        xn = xv * r
        dhv = dh_ref[...]
        dxn = dhv * g_ref[...]
        dx = r * (dxn - xn * jnp.mean(dxn * xn, axis=-1, keepdims=True))
        if has_res:
            dx = dx + dres_ref[...]
        dx_ref[...] = dx
        part = jnp.sum(dhv * xn, axis=0, keepdims=True)
        row = lax.broadcasted_iota(jnp.int32, (8, K), 0)
        upd = jnp.where(row == 0, part, 0.0)

        @pl.when(pl.program_id(0) == 0)
        def _():
            dg_ref[...] = upd

        @pl.when(pl.program_id(0) != 0)
        def _():
            dg_ref[...] += upd

    row_spec = pl.BlockSpec((tm, K), lambda i: (i, 0))
    ins = [x, g, dh] + ([dres] if has_res else [])
    in_specs = [row_spec, pl.BlockSpec((1, K), lambda i: (0, 0)), row_spec] + ([row_spec] if has_res else [])
    return pl.pallas_call(
        body, name=name, grid=(M // tm,),
        in_specs=in_specs,
        out_specs=[row_spec, pl.BlockSpec((8, K), lambda i: (0, 0))],
        out_shape=[jax.ShapeDtypeStruct((M, K), F32), jax.ShapeDtypeStruct((8, K), F32)],
        compiler_params=_params(("arbitrary",)),
    )(*ins)


def mm_nn(a, b, tm, tn, name, token=None):
    M, K = a.shape
    N = b.shape[1]
    extra = [] if token is None else [token]

    def body(a_ref, b_ref, *rest):
        rest[-1][...] = jnp.dot(a_ref[...], b_ref[...], preferred_element_type=F32)

    return pl.pallas_call(
        body, name=name, grid=(N // tn, M // tm),
        in_specs=[pl.BlockSpec((tm, K), lambda j, i: (i, 0)), pl.BlockSpec((K, tn), lambda j, i: (0, j))]
        + [pl.BlockSpec(t.shape, lambda j, i: (0, 0)) for t in extra],
        out_specs=pl.BlockSpec((tm, tn), lambda j, i: (i, j)),
        out_shape=jax.ShapeDtypeStruct((M, N), F32),
        compiler_params=_params(("arbitrary", "arbitrary")),
    )(a, b, *extra)


def mm_nt(a, b, tm, tk, name):
    M, K = a.shape
    N = b.shape[0]

    def body(a_ref, b_ref, o_ref):
        part = lax.dot_general(a_ref[...], b_ref[...], (((1,), (1,)), ((), ())), preferred_element_type=F32)

        @pl.when(pl.program_id(1) == 0)
        def _():
            o_ref[...] = part

        @pl.when(pl.program_id(1) != 0)
        def _():
            o_ref[...] += part

    return pl.pallas_call(
        body, name=name, grid=(M // tm, K // tk),
        in_specs=[pl.BlockSpec((tm, tk), lambda i, k: (i, k)), pl.BlockSpec((N, tk), lambda i, k: (0, k))],
        out_specs=pl.BlockSpec((tm, N), lambda i, k: (i, 0)),
        out_shape=jax.ShapeDtypeStruct((M, N), F32),
        compiler_params=_params(("arbitrary", "arbitrary")),
    )(a, b)


def in_proj_bwd_rms(pieces, ws, x, g, dres, tm, token):
    M, N = x.shape

    def body(*refs):
        n = len(pieces)
        p_refs, w_refs = refs[:n], refs[n:n + len(ws)]
        x_ref, g_ref, dres_ref, _, dx_ref, dg_ref = refs[n + len(ws):]
        dh = None
        for p_ref, (arr, group, col) in zip(p_refs, pieces):
            part = lax.dot_general(p_ref[...], w_refs[group][:, col:col + arr.shape[1]], (((1,), (1,)), ((), ())),
                                   preferred_element_type=F32)
            dh = part if dh is None else dh + part
        xv = x_ref[...]
        r = lax.rsqrt(jnp.mean(xv * xv, axis=-1, keepdims=True) + RMS_EPS)
        xn = xv * r
        dxn = dh * g_ref[...]
        dx_ref[...] = r * (dxn - xn * jnp.mean(dxn * xn, axis=-1, keepdims=True)) + dres_ref[...]
        row = lax.broadcasted_iota(jnp.int32, (8, N), 0)
        upd = jnp.where(row == 0, jnp.sum(dh * xn, axis=0, keepdims=True), 0.0)

        @pl.when(pl.program_id(0) == 0)
        def _():
            dg_ref[...] = upd

        @pl.when(pl.program_id(0) != 0)
        def _():
            dg_ref[...] += upd

    row_spec = pl.BlockSpec((tm, N), lambda i: (i, 0))
    return pl.pallas_call(
        body, name="in_proj_bwd", grid=(M // tm,),
        in_specs=[pl.BlockSpec((tm, arr.shape[1]), lambda i: (i, 0)) for arr, _, _ in pieces]
        + [pl.BlockSpec(w.shape, lambda i: (0, 0), pipeline_mode=pl.Buffered(1)) for w in ws]
        + [row_spec, pl.BlockSpec((1, N), lambda i: (0, 0)), row_spec, pl.BlockSpec(token.shape, lambda i: (0, 0))],
        out_specs=[row_spec, pl.BlockSpec((8, N), lambda i: (0, 0))],
        out_shape=[jax.ShapeDtypeStruct((M, N), F32), jax.ShapeDtypeStruct((8, N), F32)],
        compiler_params=_params(("arbitrary",)),
    )(*[arr for arr, _, _ in pieces], *ws, x, g, dres, token)


def _log_sigmoid(z):
    return jnp.minimum(z, 0.0) - jnp.log(1.0 + jnp.exp(-jnp.abs(z)))


def _tri(n, lower):
    r = lax.broadcasted_iota(jnp.int32, (n, n), 0)
    c = lax.broadcasted_iota(jnp.int32, (n, n), 1)
    return jnp.where((r >= c) if lower else (r <= c), 1.0, 0.0).astype(F32)


def fox_gate(proj3, b_pad):
    B, S, _ = proj3.shape
    nblk = S // TK

    def body(f_ref, b_ref, o_ref):
        tri = _tri(TK, True)
        carry = jnp.zeros((1, LANES), F32)
        for n in range(nblk):
            z = f_ref[0, n * TK:(n + 1) * TK, :] + b_ref[...]
            logf = _log_sigmoid(z)
            cs = jnp.dot(tri, logf, preferred_element_type=F32, precision=lax.Precision.HIGHEST) + carry
            carry = cs[TK - 1:TK, :]
            o_ref[0, n * TK:(n + 1) * TK, :] = -cs

    return pl.pallas_call(
        body, name="fox_gate", grid=(B,),
        in_specs=[pl.BlockSpec((1, S, LANES), lambda b: (b, 0, A_FLOG // LANES)),
                  pl.BlockSpec((1, LANES), lambda b: (0, 0))],
        out_specs=pl.BlockSpec((1, S, LANES), lambda b: (b, 0, 0)),
        out_shape=jax.ShapeDtypeStruct((B, S, LANES), F32),
        compiler_params=_params(("arbitrary",)),
    )(proj3, b_pad)


def fox_gate_bwd(drow, dneg, proj3, b_pad):
    B, S, _ = proj3.shape
    nblk = S // TK

    def body(d_ref, r_ref, f_ref, b_ref, o_ref, db_ref):
        tri = _tri(TK, False)
        lane = lax.broadcasted_iota(jnp.int32, (TK, LANES), 1)
        carry = jnp.zeros((1, LANES), F32)
        dbsum = jnp.zeros((1, LANES), F32)
        for n in reversed(range(nblk)):
            dk_side = None
            for hp in range(FOX_HEADS // 2):
                two = jnp.where(lane < 2, r_ref[0, n * TK:(n + 1) * TK, hp * LANES:(hp + 1) * LANES], 0.0)
                two = pltpu.roll(two, 2 * hp, 1) if hp else two
                dk_side = two if dk_side is None else dk_side + two
            dc = jnp.where(lane < FOX_HEADS, d_ref[0, :, n * TK:(n + 1) * TK].T - dk_side, 0.0)
            rs = jnp.dot(tri, dc, preferred_element_type=F32, precision=lax.Precision.HIGHEST) + carry
            carry = rs[0:1, :]
            z = f_ref[0, n * TK:(n + 1) * TK, :] + b_ref[...]
            dz = rs * (1.0 / (1.0 + jnp.exp(z)))
            o_ref[0, n * TK:(n + 1) * TK, :] = dz.astype(BF16)
            dbsum = dbsum + jnp.sum(dz, axis=0, keepdims=True)
        row = lax.broadcasted_iota(jnp.int32, (8, LANES), 0)
        upd = jnp.where(row == 0, dbsum, 0.0)

        @pl.when(pl.program_id(0) == 0)
        def _():
            db_ref[...] = upd

        @pl.when(pl.program_id(0) != 0)
        def _():
            db_ref[...] += upd

    return pl.pallas_call(
        body, name="fox_gate_bwd", grid=(B,),
        in_specs=[pl.BlockSpec((1, LANES, S), lambda b: (b, 0, 0)),
                  pl.BlockSpec((1, S, FOX_W), lambda b: (b, 0, 0)),
                  pl.BlockSpec((1, S, LANES), lambda b: (b, 0, A_FLOG // LANES)),
                  pl.BlockSpec((1, LANES), lambda b: (0, 0))],
        out_specs=[pl.BlockSpec((1, S, LANES), lambda b: (b, 0, 0)), pl.BlockSpec((8, LANES), lambda b: (0, 0))],
        out_shape=[jax.ShapeDtypeStruct((B, S, LANES), BF16), jax.ShapeDtypeStruct((8, LANES), F32)],
        compiler_params=_params(("arbitrary",)),
    )(drow, dneg, proj3, b_pad)


def _rope_tables(S):
    half = ROPE_DIM // 2
    f32 = np.float32
    pos = np.arange(S, dtype=f32)
    inv_freq = f32(1.0) / np.power(f32(ROPE_THETA), np.arange(0, ROPE_DIM, 2, dtype=f32) / f32(ROPE_DIM)).astype(f32)
    ang = (pos[:, None] * inv_freq[None, :]).astype(f32).astype(np.float64)
    cos, sin = np.cos(ang).astype(f32), np.sin(ang).astype(f32)
    one = np.ones((S, HEAD_DIM - ROPE_DIM), f32)
    zero = np.zeros((S, HEAD_DIM - ROPE_DIM), f32)
    zh = np.zeros((S, half), f32)
    c = np.concatenate([cos, cos, one], axis=1)
    s1 = np.concatenate([-sin, zh, zero], axis=1)
    s2 = np.concatenate([zh, sin, zero], axis=1)
    return tuple(jnp.asarray(np.concatenate([t, t], axis=1)) for t in (c, s1, s2))


_HALF_ROPE = ROPE_DIM // 2


def _rope(t, c, s1, s2):
    return t * c + pltpu.roll(t, LANES - _HALF_ROPE, 1) * s1 + pltpu.roll(t, _HALF_ROPE, 1) * s2


def _rope_bwd(d, c, s1, s2):
    return d * c + pltpu.roll(d * s1, _HALF_ROPE, 1) + pltpu.roll(d * s2, LANES - _HALF_ROPE, 1)


def _scale_parts(scale):
    m, _ = math.frexp(scale)
    return (scale, None) if m == 0.5 else (None, scale)


def _log_masks(S, kind):
    nd = 1 if kind == "causal" else S // TQ
    a = np.arange(TQ)[:, None]
    b = np.arange(TK)[None, :]
    out = np.zeros((nd, TQ, TK), np.float32)
    for d in range(nd):
        delta = d * TQ + a - b
        if kind == "causal":
            m = (delta >= 0).astype(np.float64)
        else:
            m = sum(((delta >= 0) & (delta % dil == 0) & (delta <= w)).astype(np.float64) for w, dil in DILATIONS)
        out[d] = np.where(m > 0, np.log(np.maximum(m, 1.0)), NEG_INF)
    return jnp.asarray(out)


def _attn_setup(kind):
    pair = kind != "mem"
    e_dim = HEAD_DIM if pair else MEM_HEAD_DIM
    q_fold, s_scale = _scale_parts(1.0 / math.sqrt(e_dim))
    return dict(pair=pair, col0={"fox": A_FOX, "dil": B_DIL, "mem": B_MQ}[kind],
                n_blocks=FOX_HEADS // 2 if pair else MEM_HEADS, q_fold=q_fold, s_scale=s_scale,
                nh=2 if pair else 1)


def _cat(parts, axis):
    return parts[0] if len(parts) == 1 else jnp.concatenate(parts, axis=axis)


def _log_masks_t(S, kind):
    return jnp.swapaxes(_log_masks(S, kind), 1, 2)


def _head_rows(hh, pair):
    row = lax.broadcasted_iota(jnp.int32, (LANES, 1), 0)
    if not pair:
        return row >= 0
    return (row >= HEAD_DIM * hh) & (row < HEAD_DIM * (hh + 1))


def _attn_t_inputs(kind, src, S, negc_cols, mask, rope, kv):
    cfg = _attn_setup(kind)
    col0 = cfg["col0"]
    ins, in_specs = [], []
    if cfg["pair"]:
        ins.append(src)
        in_specs.append(pl.BlockSpec((1, S, PAIR_W), lambda b, h: (b, 0, col0 // PAIR_W + h)))
    else:
        ins += [src, kv, kv]
        in_specs += [pl.BlockSpec((1, S, LANES), lambda b, h: (b, 0, col0 // LANES + h)),
                     pl.BlockSpec((1, MEM_LEN, LANES), lambda b, h: (b, 0, h)),
                     pl.BlockSpec((1, MEM_LEN, LANES), lambda b, h: (b, 0, MEM_HEADS + h))]
    if negc_cols is not None:
        ins.append(negc_cols)
        in_specs.append(pl.BlockSpec((1, S, LANES), lambda b, h: (b, 0, 0)))
    if mask is not None:
        ins.append(mask)
        in_specs.append(pl.BlockSpec(mask.shape, lambda b, h: (0, 0, 0)))
    if rope is not None:
        ins += list(rope)
        in_specs += [pl.BlockSpec((S, LANES), lambda b, h: (0, 0))] * 3
    return ins, in_specs


def _attn_t_prep(cfg, refs, S, Sk, *, qT2s, ks, vs=None, vTs=None, kTs=None, nb=None):
    pair, nh = cfg["pair"], cfg["nh"]
    lane = lax.broadcasted_iota(jnp.int32, (1, LANES), 1)
    rope_refs = refs["rope"]

    def prep_q(n):
        rows = slice(n * TQ, (n + 1) * TQ)
        q = refs["load_q"](rows)
        if rope_refs is not None:
            q = _rope(q, *[t[rows, :] for t in rope_refs])
        if cfg["q_fold"] is not None:
            q = q * cfg["q_fold"]
        qtb = q.astype(BF16).T
        for hh in range(nh):
            qT2s[n, :, hh * TQ:(hh + 1) * TQ] = jnp.where(_head_rows(hh, pair), qtb, jnp.zeros_like(qtb))

    def prep_kv(n):
        rows = slice(n * TK, (n + 1) * TK)
        k, v = refs["load_kv"](rows)
        if rope_refs is not None:
            k = _rope(k, *[t[rows, :] for t in rope_refs])
        kb = k.astype(BF16)
        vb = v.astype(BF16)
        ks[rows, :] = kb
        if vs is not None:
            vs[rows, :] = vb
        if vTs is not None:
            vTs[n] = vb.T
        if kTs is not None:
            kTs[n] = kb.T
        if nb is not None:
            blk = refs["negc"][0, rows, :]
            for hh in range(nh):
                h = 2 * refs["block"] + hh
                col = jnp.sum(jnp.where(lane == h, blk, 0.0), axis=1, keepdims=True)
                nb[hh, rows, :] = jnp.broadcast_to(col, (TK, LANES))

    for n in range(S // TQ):
        prep_q(n)
    for n in range(Sk // TK):
        prep_kv(n)


def _raw_scores_t(cfg, k, qT2):
    sT = jnp.dot(k, qT2, preferred_element_type=F32)
    if cfg["s_scale"] is not None:
        sT = sT * cfg["s_scale"]
    return sT


def _bias_mask_t(cfg, sT, nb, mask_ref, kc, midx):
    nh = cfg["nh"]
    if nb is None and midx is None:
        return sT
    parts = []
    for hh in range(nh):
        t = sT[:, hh * TQ:(hh + 1) * TQ]
        if nb is not None:
            t = t + jnp.concatenate([nb[hh, kc, :]] * (TQ // LANES), axis=1)
        if midx is not None:
            t = t + mask_ref[midx]
        parts.append(t)
    return _cat(parts, 1)


def _tile_pairs(kind, nq, nk):
    if kind == "mem":
        return [(i, j) for i in range(nq) for j in range(nk)], (lambda i, j: None)
    pairs = [(i, j) for i in range(nq) for j in range(i + 1)]
    if kind == "fox":
        return pairs, (lambda i, j: 0 if j == i else None)
    return pairs, (lambda i, j: i - j)


def attn_fwd(kind, src, S, *, negc_cols=None, mask=None, rope=None, kv=None):
    B = src.shape[0]
    cfg = _attn_setup(kind)
    pair, nh = cfg["pair"], cfg["nh"]
    Sk = S if pair else MEM_LEN
    has_bias, has_rope = negc_cols is not None, rope is not None
    R = nh * TQ
    nq, nk = S // TQ, Sk // TK
    pairs, mask_index = _tile_pairs(kind, nq, nk)

    def body(*refs):
        refs = list(refs)
        if pair:
            qkv_ref = refs.pop(0)
            load_q = lambda rows: qkv_ref[0, rows, 0:LANES]
            load_kv = lambda rows: (qkv_ref[0, rows, LANES:2 * LANES], qkv_ref[0, rows, 2 * LANES:3 * LANES])
        else:
            q_ref, k_ref, v_ref = refs.pop(0), refs.pop(0), refs.pop(0)
            load_q = lambda rows: q_ref[0, rows, :]
            load_kv = lambda rows: (k_ref[0, rows, :], v_ref[0, rows, :])
        negc_ref = refs.pop(0) if has_bias else None
        mask_ref = refs.pop(0) if mask is not None else None
        rope_refs = [refs.pop(0) for _ in range(3)] if has_rope else None
        o_ref, lse_ref, qT2s, ks, vTs, s_a, s_b, p_a, p_b = refs[:9]
        nb = refs[9] if has_bias else None
        _attn_t_prep(cfg, dict(load_q=load_q, load_kv=load_kv, rope=rope_refs, negc=negc_ref,
                               block=pl.program_id(1)), S, Sk, qT2s=qT2s, ks=ks, vTs=vTs, nb=nb)

        def cols(j):
            return slice(j * TK, (j + 1) * TK)

        def scores(i, j):
            return _raw_scores_t(cfg, ks[cols(j), :], qT2s[i])

        def finish(i, m, l, accT):
            oT2 = accT / l
            oT = jnp.where(_head_rows(0, True), oT2[:, 0:TQ], oT2[:, TQ:2 * TQ]) if pair else oT2
            o_ref[0, i * TQ:(i + 1) * TQ, :] = oT.T
            lse_ref[0, 0, i:i + 1, :] = m + jnp.log(l)

        s_bufs, p_bufs = (s_a, s_b), (p_a, p_b)
        s_bufs[0][...] = scores(*pairs[0])
        m = l = accT = None
        for t, (i, j) in enumerate(pairs):
            cur, oth = t % 2, 1 - t % 2
            if t > 0:
                i_prev, j_prev = pairs[t - 1]
                pv = jnp.dot(vTs[j_prev], p_bufs[oth][...], preferred_element_type=F32)
                acc_full = pv if accT is None else accT + pv
            if t + 1 < len(pairs):
                s_bufs[oth][...] = scores(*pairs[t + 1])
            first = j == 0
            if first and t > 0:
                finish(i_prev, m, l, acc_full)
            sT = _bias_mask_t(cfg, s_bufs[cur][...], nb, mask_ref, cols(j), mask_index(i, j))
            m_tile = jnp.max(sT, axis=0, keepdims=True)
            m_new = m_tile if first else jnp.maximum(m, m_tile)
            p = jnp.exp(sT - m_new)
            p_bufs[cur][...] = p.astype(BF16)
            if first:
                l, accT = jnp.sum(p, axis=0, keepdims=True), None
            else:
                alpha = jnp.exp(m - m_new)
                l, accT = alpha * l + jnp.sum(p, axis=0, keepdims=True), acc_full * alpha
            m = m_new
        i_last, j_last = pairs[-1]
        pv = jnp.dot(vTs[j_last], p_bufs[(len(pairs) - 1) % 2][...], preferred_element_type=F32)
        finish(i_last, m, l, pv if accT is None else accT + pv)

    ins, in_specs = _attn_t_inputs(kind, src, S, negc_cols, mask, rope, kv)
    W = cfg["n_blocks"] * LANES
    scratch = [pltpu.VMEM((nq, LANES, R), BF16), pltpu.VMEM((Sk, LANES), BF16), pltpu.VMEM((nk, LANES, TK), BF16),
               pltpu.VMEM((TK, R), F32), pltpu.VMEM((TK, R), F32), pltpu.VMEM((TK, R), BF16), pltpu.VMEM((TK, R), BF16)]
    if has_bias:
        scratch.append(pltpu.VMEM((nh, Sk, LANES), F32))
    return pl.pallas_call(
        body, name=kind + "_attn_fwd", grid=(B, cfg["n_blocks"]),
        in_specs=in_specs,
        out_specs=[pl.BlockSpec((1, S, LANES), lambda b, h: (b, 0, h)),
                   pl.BlockSpec((1, 1, nq, R), lambda b, h: (b, h, 0, 0))],
        out_shape=[jax.ShapeDtypeStruct((B, S, W), F32), jax.ShapeDtypeStruct((B, cfg["n_blocks"], nq, R), F32)],
        scratch_shapes=scratch,
        compiler_params=_params(("arbitrary", "arbitrary")),
    )(*ins)


def attn_bwd(kind, src, do, o, lse, S, *, negc_cols=None, mask=None, rope=None, kv=None, token=None):
    B = src.shape[0]
    cfg = _attn_setup(kind)
    pair, nh, s_scale, q_fold = cfg["pair"], cfg["nh"], cfg["s_scale"], cfg["q_fold"]
    Sk = S if pair else MEM_LEN
    has_bias, has_rope = negc_cols is not None, rope is not None
    R = nh * TQ
    nq, nk = S // TQ, Sk // TK
    pairs, mask_index = _tile_pairs(kind, nq, nk)

    def body(*refs):
        refs = list(refs)
        if pair:
            qkv_ref = refs.pop(0)
            load_q = lambda rows: qkv_ref[0, rows, 0:LANES]
            load_kv = lambda rows: (qkv_ref[0, rows, LANES:2 * LANES], qkv_ref[0, rows, 2 * LANES:3 * LANES])
        else:
            q_ref, k_ref, v_ref = refs.pop(0), refs.pop(0), refs.pop(0)
            load_q = lambda rows: q_ref[0, rows, :]
            load_kv = lambda rows: (k_ref[0, rows, :], v_ref[0, rows, :])
        negc_ref = refs.pop(0) if has_bias else None
        mask_ref = refs.pop(0) if mask is not None else None
        rope_refs = [refs.pop(0) for _ in range(3)] if has_rope else None
        do_ref, o_ref, lse_ref = refs.pop(0), refs.pop(0), refs.pop(0)
        if token is not None:
            refs.pop(0)
        if pair:
            dqkv_ref = refs.pop(0)
            dneg_ref = refs.pop(0) if has_bias else None
            drow_ref = refs.pop(0) if has_bias else None
        else:
            dq_ref, dk_ref, dv_ref = refs.pop(0), refs.pop(0), refs.pop(0)
        qT2s, ks, vs, kTs, doT2s, delta_s, dk_acc, dv_acc = refs[:8]
        bufs_a, bufs_b = refs[8:12], refs[12:16]
        nb, dneg_acc = (refs[16], refs[17]) if has_bias else (None, None)
        lane = lax.broadcasted_iota(jnp.int32, (1, LANES), 1)
        _attn_t_prep(cfg, dict(load_q=load_q, load_kv=load_kv, rope=rope_refs, negc=negc_ref,
                               block=pl.program_id(1)), S, Sk,
                     qT2s=qT2s, ks=ks, vs=vs, kTs=kTs, nb=nb)

        def prep_do(n):
            rows = slice(n * TQ, (n + 1) * TQ)
            doT = do_ref[0, rows, :].astype(BF16).astype(F32).T
            prodT = doT * o_ref[0, rows, :].T
            doTb = doT.astype(BF16)
            for hh in range(nh):
                hm = _head_rows(hh, pair)
                doT2s[n, :, hh * TQ:(hh + 1) * TQ] = jnp.where(hm, doTb, jnp.zeros_like(doTb))
                delta_s[n:n + 1, hh * TQ:(hh + 1) * TQ] = jnp.sum(jnp.where(hm, prodT, 0.0), axis=0, keepdims=True)

        for n in range(nq):
            prep_do(n)
        dk_acc[...] = jnp.zeros(dk_acc.shape, F32)
        dv_acc[...] = jnp.zeros(dv_acc.shape, F32)
        if has_bias:
            dneg_acc[...] = jnp.zeros(dneg_acc.shape, F32)

        def cols(j):
            return slice(j * TK, (j + 1) * TK)

        nt_dims = (((1,), (1,)), ((), ()))

        def first_products(i, j, bufs):
            bufs[0][...] = _raw_scores_t(cfg, ks[cols(j), :], qT2s[i])
            bufs[1][...] = jnp.dot(vs[cols(j), :], doT2s[i], preferred_element_type=F32)

        def last_products(i, j, bufs, dqT2):
            dv_acc[j] += lax.dot_general(doT2s[i], bufs[2][...], nt_dims, preferred_element_type=F32)
            dk_acc[j] += lax.dot_general(qT2s[i], bufs[3][...], nt_dims, preferred_element_type=F32)
            dq = jnp.dot(kTs[j], bufs[3][...], preferred_element_type=F32)
            return dq if dqT2 is None else dqT2 + dq

        def finish_q(i, dqT2, drow):
            rows = slice(i * TQ, (i + 1) * TQ)
            dqT = jnp.where(_head_rows(0, True), dqT2[:, 0:TQ], dqT2[:, TQ:2 * TQ]) if pair else dqT2
            dq = dqT.T
            if q_fold is not None:
                dq = dq * q_fold
            if has_rope:
                dq = _rope_bwd(dq, *[t[rows, :] for t in rope_refs])
            if pair:
                dqkv_ref[0, rows, 0:LANES] = dq.astype(BF16)
            else:
                dq_ref[0, rows, :] = dq.astype(BF16)
            if has_bias:
                drow_ref[0, 0, i:i + 1, :] = drow

        bufs = (bufs_a, bufs_b)
        first_products(*pairs[0], bufs[0])
        dqT2 = drow = None
        for t, (i, j) in enumerate(pairs):
            cur, oth = bufs[t % 2], bufs[1 - t % 2]
            first = j == 0
            if first and t > 0:
                i_prev, j_prev = pairs[t - 1]
                finish_q(i_prev, last_products(i_prev, j_prev, oth, dqT2), drow)
                dqT2 = drow = None
            sT = _bias_mask_t(cfg, cur[0][...], nb, mask_ref, cols(j), mask_index(i, j))
            pT = jnp.exp(sT - lse_ref[0, 0, i:i + 1, :])
            dsT = pT * (cur[1][...] - delta_s[i:i + 1, :])
            if has_bias:
                tile_rows = jnp.sum(dsT, axis=0, keepdims=True)
                drow = tile_rows if drow is None else drow + tile_rows
                for hh in range(nh):
                    part = dsT[:, hh * TQ:hh * TQ + LANES]
                    for u in range(1, TQ // LANES):
                        part = part + dsT[:, hh * TQ + u * LANES:hh * TQ + (u + 1) * LANES]
                    dneg_acc[hh, cols(j), :] += part
            if s_scale is not None:
                dsT = dsT * s_scale
            cur[2][...] = pT.astype(BF16)
            cur[3][...] = dsT.astype(BF16)
            if not first:
                dqT2 = last_products(*pairs[t - 1], oth, dqT2)
            if t + 1 < len(pairs):
                first_products(*pairs[t + 1], oth)
        i_last, j_last = pairs[-1]
        finish_q(i_last, last_products(i_last, j_last, bufs[(len(pairs) - 1) % 2], dqT2), drow)

        for n in range(nk):
            rows = slice(n * TK, (n + 1) * TK)
            dk = dk_acc[n].T
            dv = dv_acc[n].T
            if has_rope:
                dk = _rope_bwd(dk, *[t[rows, :] for t in rope_refs])
            if pair:
                dqkv_ref[0, rows, LANES:2 * LANES] = dk.astype(BF16)
                dqkv_ref[0, rows, 2 * LANES:3 * LANES] = dv.astype(BF16)
            else:
                dk_ref[0, rows, :] = dk.astype(BF16)
                dv_ref[0, rows, :] = dv.astype(BF16)
            if has_bias:
                x0 = jnp.sum(dneg_acc[0, rows, :], axis=1, keepdims=True)
                x1 = jnp.sum(dneg_acc[1, rows, :], axis=1, keepdims=True)
                dneg_ref[0, rows, :] = jnp.where(lane == 0, x0, jnp.where(lane == 1, x1, 0.0))

    ins, in_specs = _attn_t_inputs(kind, src, S, negc_cols, mask, rope, kv)
    row_spec = pl.BlockSpec((1, S, LANES), lambda b, h: (b, 0, h))
    vec_spec = pl.BlockSpec((1, 1, nq, R), lambda b, h: (b, h, 0, 0))
    ins += [do, o, lse]
    in_specs += [row_spec, row_spec, vec_spec]
    if token is not None:
        ins.append(token)
        in_specs.append(pl.BlockSpec(token.shape, lambda b, h: (0, 0)))
    W = cfg["n_blocks"] * LANES
    if pair:
        out_specs = [pl.BlockSpec((1, S, PAIR_W), lambda b, h: (b, 0, h))]
        out_shape = [jax.ShapeDtypeStruct((B, S, 3 * W), BF16)]
        if has_bias:
            out_specs += [row_spec, vec_spec]
            out_shape += [jax.ShapeDtypeStruct((B, S, W), F32), jax.ShapeDtypeStruct((B, cfg["n_blocks"], nq, R), F32)]
    else:
        kv_spec = pl.BlockSpec((1, MEM_LEN, LANES), lambda b, h: (b, 0, h))
        out_specs = [row_spec, kv_spec, kv_spec]
        out_shape = [jax.ShapeDtypeStruct((B, S, W), BF16)] + [jax.ShapeDtypeStruct((B, MEM_LEN, W), BF16)] * 2
    scratch = [pltpu.VMEM((nq, LANES, R), BF16), pltpu.VMEM((Sk, LANES), BF16),
               pltpu.VMEM((Sk, LANES), BF16), pltpu.VMEM((nk, LANES, TK), BF16), pltpu.VMEM((nq, LANES, R), BF16),
               pltpu.VMEM((nq, R), F32), pltpu.VMEM((nk, LANES, TK), F32), pltpu.VMEM((nk, LANES, TK), F32)]
    pair_bufs = [pltpu.VMEM((TK, R), F32), pltpu.VMEM((TK, R), F32), pltpu.VMEM((TK, R), BF16), pltpu.VMEM((TK, R), BF16)]
    scratch += pair_bufs + pair_bufs
    if has_bias:
        scratch += [pltpu.VMEM((nh, Sk, LANES), F32), pltpu.VMEM((nh, Sk, LANES), F32)]
    return pl.pallas_call(
        body, name=kind + "_attn_bwd", grid=(B, cfg["n_blocks"]),
        in_specs=in_specs, out_specs=out_specs, out_shape=out_shape, scratch_shapes=scratch,
        compiler_params=_params(("arbitrary", "arbitrary")),
    )(*ins)


def _sigmoid(g):
    return 1.0 / (1.0 + jnp.exp(-g))


def out_step(proj, o_fox, o_dil, o_mem, w_out, x, target, gf, tm):
    T = x.shape[0]

    def body(fg_ref, dg_ref, mg_ref, of_ref, od_ref, om_ref, w_ref, x_ref, t_ref, gf_ref,
             dx_ref, dof_ref, dod_ref, dom_ref, dfg_ref, ddg_ref, dmg_ref, gw_ref, sm_ref, gw_acc):
        branches = []
        for g_ref, o_ref in ((fg_ref, of_ref), (dg_ref, od_ref), (mg_ref, om_ref)):
            g = g_ref[...]
            sg = _sigmoid(g)
            o = o_ref[...]
            branches.append((g, sg, o))
        ymix = jnp.concatenate([(o * (g * sg)).astype(BF16) for g, sg, o in branches], axis=1)
        x2 = x_ref[...] + jnp.dot(ymix, w_ref[...], preferred_element_type=F32)
        r = lax.rsqrt(jnp.mean(x2 * x2, axis=-1, keepdims=True) + RMS_EPS)
        yn = x2 * r
        err = yn * gf_ref[...] - t_ref[...]
        loss = 0.5 * jnp.sum(jnp.sum(err * err, axis=-1, keepdims=True) / D_MODEL, axis=0, keepdims=True)
        dyf = err / D_MODEL
        dgf = jnp.sum(dyf * yn, axis=0, keepdims=True)
        dyn = dyf * gf_ref[...]
        dx2 = r * (dyn - yn * jnp.mean(dyn * yn, axis=-1, keepdims=True))
        dx_ref[...] = dx2
        dxb = dx2.astype(BF16)
        dmix = lax.dot_general(dxb, w_ref[...], (((1,), (1,)), ((), ())), preferred_element_type=F32)
        col = 0
        for (g, sg, o), do_ref, dgate_ref in zip(branches, (dof_ref, dod_ref, dom_ref), (dfg_ref, ddg_ref, dmg_ref)):
            d = dmix[:, col:col + g.shape[1]]
            col += g.shape[1]
            do_ref[...] = (d * (g * sg)).astype(BF16)
            dgate_ref[...] = (d * o * (sg * (1.0 + g * (1.0 - sg)))).astype(BF16)
        row = lax.broadcasted_iota(jnp.int32, (8, D_MODEL), 0)
        upd = jnp.where(row == 0, dgf, jnp.where(row == 1, loss, 0.0))

        @pl.when(pl.program_id(0) == 0)
        def _():
            sm_ref[...] = jnp.zeros(sm_ref.shape, F32)
            gw_acc[...] = jnp.zeros(gw_acc.shape, F32)

        sm_ref[...] += upd
        gw_acc[...] += lax.dot_general(ymix, dxb, (((0,), (0,)), ((), ())), preferred_element_type=F32)

        @pl.when(pl.program_id(0) == T // tm - 1)
        def _():
            gw_ref[...] = gw_acc[...].astype(BF16)

    def rows(w, col=0):
        return pl.BlockSpec((tm, w), lambda i: (i, col))

    return pl.pallas_call(
        body, name="out_step", grid=(T // tm,),
        in_specs=[rows(FOX_W, B_FG // FOX_W), rows(DIL_W, B_DG // DIL_W), rows(MEM_W, B_MG // MEM_W),
                  rows(FOX_W), rows(DIL_W), rows(MEM_W),
                  pl.BlockSpec((MIX_W, D_MODEL), lambda i: (0, 0)),
                  rows(D_MODEL), rows(D_MODEL), pl.BlockSpec((1, D_MODEL), lambda i: (0, 0))],
        out_specs=[rows(D_MODEL), rows(FOX_W), rows(DIL_W), rows(MEM_W), rows(FOX_W), rows(DIL_W), rows(MEM_W),
                   pl.BlockSpec((MIX_W, D_MODEL), lambda i: (0, 0)), pl.BlockSpec((8, D_MODEL), lambda i: (0, 0))],
        out_shape=[jax.ShapeDtypeStruct((T, D_MODEL), F32), jax.ShapeDtypeStruct((T, FOX_W), BF16),
                   jax.ShapeDtypeStruct((T, DIL_W), BF16), jax.ShapeDtypeStruct((T, MEM_W), BF16),
                   jax.ShapeDtypeStruct((T, FOX_W), BF16), jax.ShapeDtypeStruct((T, DIL_W), BF16),
                   jax.ShapeDtypeStruct((T, MEM_W), BF16), jax.ShapeDtypeStruct((MIX_W, D_MODEL), BF16),
                   jax.ShapeDtypeStruct((8, D_MODEL), F32)],
        scratch_shapes=[pltpu.VMEM((MIX_W, D_MODEL), F32)],
        compiler_params=_params(("arbitrary",)),
    )(proj, proj, proj, o_fox, o_dil, o_mem, w_out, x, target, gf)


def adamw_columns_first(w, g, m, v, name):
    N = w.shape[0]
    parts, step_rows = 4, 16
    size = -(-N // (parts * step_rows)) * step_rows
    bounds = [(p * size, min((p + 1) * size, N)) for p in range(parts)]

    def body(w_hbm, g_hbm, m_hbm, v_hbm, d_hbm, mo_hbm, vo_hbm, wb, gb, mb, vb, db, mob, vob, load_sems, store_sems):
        ins = ((w_hbm, wb), (g_hbm, gb), (m_hbm, mb), (v_hbm, vb))
        outs = ((d_hbm, db), (mo_hbm, mob), (vo_hbm, vob))

        def rows_copy(src, dst, lo, hi, sem):
            return pltpu.make_async_copy(src.at[pl.ds(lo, hi - lo), :], dst.at[pl.ds(lo, hi - lo), :], sem)

        def update(rows):
            db[rows, :], mob[rows, :], vob[rows, :] = _adamw_update(wb[rows, :], gb[rows, :], mb[rows, :], vb[rows, :])

        loads = [[rows_copy(h.reshape(N, LANES), b, lo, hi, load_sems.at[p, k]) for k, (h, b) in enumerate(ins)]
                 for p, (lo, hi) in enumerate(bounds)]
        for part in loads:
            for cp in part:
                cp.start()
        stores = []
        for p, (lo, hi) in enumerate(bounds):
            for cp in loads[p]:
                cp.wait()
            whole = (hi - lo) // step_rows

            def step(i, _, lo=lo):
                update(pl.ds(pl.multiple_of(lo + i * step_rows, step_rows), step_rows))
                return 0

            lax.fori_loop(0, whole, step, 0, unroll=4)
            if lo + whole * step_rows < hi:
                update(slice(lo + whole * step_rows, hi))
            leaving = [rows_copy(b, h.reshape(N, LANES), lo, hi, store_sems.at[p, k]) for k, (h, b) in enumerate(outs)]
            for cp in leaving:
                cp.start()
            stores += leaving
        for cp in stores:
            cp.wait()

    any_spec = pl.BlockSpec(memory_space=pl.ANY)
    return pl.pallas_call(
        body, name=name,
        in_specs=[any_spec] * 4, out_specs=[any_spec] * 3,
        out_shape=[jax.ShapeDtypeStruct(w.shape, F32)] * 3,
        scratch_shapes=[pltpu.VMEM((N, LANES), F32)] * 7
        + [pltpu.SemaphoreType.DMA((parts, 4)), pltpu.SemaphoreType.DMA((parts, 3))],
        compiler_params=_params(),
    )(w, g, m, v)


def _pad_row(v, width):
    return jnp.concatenate([v, jnp.zeros((1, width - v.shape[1]), v.dtype)], axis=1)


def local_grads(x, mem, norm_g, b_forget, mem_norm_g, final_norm_g, loss_target, first_token, first_weights,
                late_weights, start_exchange):
    B, S, D = x.shape
    T = B * S
    xt = x.reshape(T, D)
    memt = mem.reshape(B * MEM_LEN, D)
    b_pad = _pad_row(b_forget, LANES)

    h, h_t = rms_fwd(xt, norm_g, 512, "rms_x", with_transpose=True, token=first_token)
    mh, mh_t = rms_fwd(memt, mem_norm_g, B * MEM_LEN, "rms_mem", with_transpose=True, token=first_token)
    w_in_a, proj_token = first_weights([h, mh])
    proj_a = mm_nn(h, w_in_a, 1024, PA, "in_proj_a", proj_token)
    proj_a3 = proj_a.reshape(B, S, PA)

    negc = fox_gate(proj_a3, b_pad)
    causal = _log_masks_t(S, "causal")
    dilated = _log_masks_t(S, "dilated")
    rope = _rope_tables(S)

    o_fox, lse_fox = attn_fwd("fox", proj_a3, S, negc_cols=negc, mask=causal)

    w_in_b, w_kv, w_out = late_weights(o_fox)
    proj_b = mm_nn(h, w_in_b, 1024, PB // 2, "in_proj_b")
    proj_b3 = proj_b.reshape(B, S, PB)
    o_dil, lse_dil = attn_fwd("dil", proj_b3, S, mask=dilated, rope=rope)

    mkv = mm_nn(mh, w_kv, B * MEM_LEN, 2 * MEM_W, "mem_kv_proj")
    mkv3 = mkv.reshape(B, MEM_LEN, 2 * MEM_W)
    o_mem, lse_mem = attn_fwd("mem", proj_b3, S, kv=mkv3)

    dx2, do_fox, do_dil, do_mem, dfg, ddg, dmg, g_out, small_out = out_step(
        proj_b, o_fox.reshape(T, FOX_W), o_dil.reshape(T, DIL_W), o_mem.reshape(T, MEM_W), w_out,
        xt, loss_target.reshape(T, D), final_norm_g.reshape(1, D), 256)

    gates = [(dfg, 1, B_FG), (ddg, 1, B_DG), (dmg, 1, B_MG)]
    g_gates = mm_tn_multi(h_t, [piece[0] for piece in gates], 1024, "w_in_grad_gates", BF16)

    dqkv_fox, dneg, drow = attn_bwd("fox", proj_a3, do_fox.reshape(B, S, FOX_W), o_fox, lse_fox, S,
                                    negc_cols=negc, mask=causal)
    drow = drow.reshape(B, FOX_HEADS // 2, S // TQ, 2, TQ).transpose(0, 1, 3, 2, 4).reshape(B, FOX_HEADS, S)
    drow = jnp.pad(drow, ((0, 0), (0, LANES - FOX_HEADS), (0, 0)))
    dflog, db_part = fox_gate_bwd(drow, dneg, proj_a3, b_pad)
    fox = [(dqkv_fox.reshape(T, 3 * FOX_W), 0, A_FOX), (dflog.reshape(T, LANES), 0, A_FLOG)]
    g_fox = mm_tn_multi(h_t, [piece[0] for piece in fox], 1024, "w_in_grad_fox", BF16)
    first, token = start_exchange([g_gates, g_out, g_fox], "early_exchange_a")

    (dqkv_dil,) = attn_bwd("dil", proj_b3, do_dil.reshape(B, S, DIL_W), o_dil, lse_dil, S, mask=dilated, rope=rope,
                           token=token)
    dil = [(dqkv_dil.reshape(T, 3 * DIL_W), 1, B_DIL)]
    g_dil = mm_tn_multi(h_t, [piece[0] for piece in dil], 1024, "w_in_grad_dil", BF16)
    second, token = start_exchange([g_dil], "early_exchange_b")

    dmq, dmk, dmv = attn_bwd("mem", proj_b3, do_mem.reshape(B, S, MEM_W), o_mem, lse_mem, S, kv=mkv3, token=token)
    mq = [(dmq.reshape(T, MEM_W), 1, B_MQ)]
    g_mq = mm_tn_multi(h_t, [piece[0] for piece in mq], 1024, "w_in_grad_mq", BF16)
    dmkv = jnp.concatenate([dmk, dmv], axis=2).reshape(B * MEM_LEN, 2 * MEM_W)
    g_kv = mm_tn_multi(mh_t, [dmkv], B * MEM_LEN, "w_kv_grad", BF16)
    third, token = start_exchange([g_mq, g_kv], "early_exchange_c")

    grad_x, dng = in_proj_bwd_rms(gates + fox + dil + mq, (w_in_a, w_in_b), xt, norm_g, dx2, 512, token)
    dmh = mm_nt(dmkv, w_kv, B * MEM_LEN, D, "mem_kv_bwd")
    _, dmng = rms_bwd(memt, mem_norm_g, dmh, None, B * MEM_LEN, "rms_mem_bwd")

    small = jnp.concatenate([dng[0:1], dmng[0:1], small_out[0:1], _pad_row(db_part[0:1], D), small_out[1:2],
                             jnp.zeros((3, D), F32)], axis=0)
    early = [(first, dqkv_dil), (second, dmq), (third, grad_x)]
    return grad_x.reshape(B, S, D), early, small


def kernel(x, mem, norm_g, w_in, b_forget, mem_norm_g, w_mem_kv, w_out, final_norm_g, loss_target, m_norm_g, m_w_in, m_b_forget, m_mem_norm_g, m_w_mem_kv, m_w_out, m_final_norm_g, v_norm_g, v_w_in, v_b_forget, v_mem_norm_g, v_w_mem_kv, v_w_out, v_final_norm_g):
    D = D_MODEL
    shard_a, shard_b = _split_cols(_pack_cols(w_in).astype(BF16).reshape(w_in.shape[1], PW))
    gather_a, first_token = early_exchange_start([shard_a], "first_gather", gather=True,
                                                 relations=_SIBLING_AND_SAME_CORES)
    late_shards = [shard_b, w_mem_kv[0].astype(BF16), w_out[0].astype(BF16)]
    late_lands = own_slots(late_shards, "late_gather_place", after=first_token)
    late = {}

    def first_weights(after):
        _, gathered = early_exchange_wait(gather_a, list(after) + [late_lands[0]], "first_gather_wait")
        (w_in_a,) = pass_on_to_sibling(gathered, gather_a["rows"], "first_gather_pass")
        late["gather"], token = early_exchange_start(
            late_shards, "late_gather", gather=True, after=w_in_a, relations=_SIBLING_AND_SAME_CORES,
            lands=late_lands)
        return w_in_a, token

    def late_weights(after):
        _, gathered = early_exchange_wait(late["gather"], after, "late_gather_wait")
        return pass_on_to_sibling(gathered, late["gather"]["rows"], "late_gather_pass")

    grad_x, early, small = local_grads(
        x, mem, norm_g, b_forget, mem_norm_g, final_norm_g, loss_target, first_token, first_weights, late_weights,
        early_exchange_start)

    (first, after_first), (second, after_second), (third, after_third) = early
    small_handle, small_token = early_exchange_start([jnp.tile(small, (N_DEV, 1))], "small_exchange")
    (src_gates, src_out, src_fox), (land_gates, land_out, land_fox) = early_exchange_wait(
        first, [after_first, small_token], "early_wait_a")
    (src_dil,), (land_dil,) = early_exchange_wait(second, after_second, "early_wait_b")
    (src_mq, src_kv), (land_mq, land_kv) = early_exchange_wait(third, after_third, "early_wait_c")
    gw_out, d_out, m_out, v_out = slot_sum8_adamw(src_out, land_out, w_out[0], m_w_out[0], v_w_out[0], 256,
                                                  "sum_adamw_w_out")
    gw_kv, d_kv, m_kv, v_kv = slot_sum8_adamw(src_kv, land_kv, w_mem_kv[0], m_w_mem_kv[0], v_w_mem_kv[0], 128,
                                              "sum_adamw_w_kv")
    gw_in_cols = w_in_grad_sum(
        [(src_fox, land_fox, [(0, P_FOX, 3 * FOX_W), (3 * FOX_W, P_FLOG, LANES)]),
         (src_gates, land_gates, [(0, P_FG, FOX_W), (FOX_W, P_DG, DIL_W), (FOX_W + DIL_W, P_MG, MEM_W)]),
         (src_dil, land_dil, [(0, P_DIL, 3 * DIL_W)]),
         (src_mq, land_mq, [(0, P_MQ, MEM_W)])], "sum_w_in")
    gw_in = jnp.transpose(gw_in_cols, (1, 2, 0))

    def rows8(*rows):
        rows = [r.reshape(1, -1) for r in rows]
        rows = [_pad_row(r, D) for r in rows]
        return jnp.concatenate(rows + [jnp.zeros((8 - len(rows), D), F32)], axis=0)

    sw = rows8(norm_g, mem_norm_g, final_norm_g, b_forget)
    sm = rows8(m_norm_g, m_mem_norm_g, m_final_norm_g, m_b_forget)
    sv = rows8(v_norm_g, v_mem_norm_g, v_final_norm_g, v_b_forget)
    (src_small,), (land_small,) = early_exchange_wait(small_handle, gw_in_cols, "small_wait")
    tot, d_s, m_s, v_s = slot_sum8_adamw(src_small, land_small, sw, sm, sv, 8, "sum_adamw_small")
    loss = tot[4, 0]
    g_norm, g_mem_norm, g_final, g_b = tot[0:1], tot[1:2], tot[2], tot[3:4, :FOX_HEADS]
    columns_first = lambda a: jnp.transpose(a, (2, 0, 1))
    d_in, m_in, v_in = [jnp.transpose(o, (1, 2, 0)) for o in adamw_columns_first(
        columns_first(w_in), gw_in_cols, columns_first(m_w_in), columns_first(v_w_in), "adamw_w_in")]

    def small_outs(t):
        return t[0:1], t[3:4, :FOX_HEADS], t[1:2], t[2]

    grads = (g_norm, gw_in, g_b, g_mem_norm, gw_kv[None], gw_out[None], g_final)
    outs = []
    for t, big in ((d_s, (d_in, d_kv, d_out)), (m_s, (m_in, m_kv, m_out)), (v_s, (v_in, v_kv, v_out))):
        n, b, mn, f = small_outs(t)
        outs += [n, big[0], b, mn, big[1][None], big[2][None], f]
    return (loss, grad_x, *grads, *outs)
```

```python
import math

import numpy as np
import jax
import jax.numpy as jnp
from jax import lax
from jax.experimental import pallas as pl
from jax.experimental.pallas import tpu as pltpu

F32 = jnp.float32
BF16 = jnp.bfloat16

D_MODEL = 1024
HEAD_DIM = 64
FOX_HEADS = 12
DIL_HEADS = 12
MEM_HEADS = 4
MEM_HEAD_DIM = 128
MEM_LEN = 256
FOX_W = FOX_HEADS * HEAD_DIM
DIL_W = DIL_HEADS * HEAD_DIM
MEM_W = MEM_HEADS * MEM_HEAD_DIM
MIX_W = FOX_W + DIL_W + MEM_W
DILATIONS = ((128, 1), (512, 4), (2048, 16))
ROPE_THETA = 500000.0
ROPE_DIM = HEAD_DIM // 4
RMS_EPS = 1e-6
NEG_INF = -1e30
IN_W = 4 * FOX_W + FOX_HEADS + 4 * DIL_W + 2 * MEM_W

ADAM_LR = 0.001
ADAM_B1 = 0.9
ADAM_B2 = 0.999
ADAM_EPS = 1e-08
ADAM_WD = 0.01
ADAM_STEP = 10

N_DEV = 8
LANES = 128
PAIR_W = 3 * LANES
TQ = 256
TK = 256

O_FQ, O_FK, O_FV, O_FG = 0, FOX_W, 2 * FOX_W, 3 * FOX_W
O_FLOG = 4 * FOX_W
O_DQ = O_FLOG + FOX_HEADS
O_DK, O_DV, O_DG = O_DQ + DIL_W, O_DQ + 2 * DIL_W, O_DQ + 3 * DIL_W
O_MQ = O_DQ + 4 * DIL_W
O_MG = O_MQ + MEM_W
P_FOX = 0
P_FG = P_FOX + 3 * FOX_W
P_DIL = P_FG + FOX_W
P_DG = P_DIL + 3 * DIL_W
P_MQ = P_DG + DIL_W
P_MG = P_MQ + MEM_W
P_FLOG = P_MG + MEM_W
PW = P_FLOG + LANES
A_FOX = 0
A_FLOG = A_FOX + 3 * FOX_W
PA = A_FLOG + LANES
B_FG = 0
B_DG = B_FG + FOX_W
B_DIL = B_DG + DIL_W
B_MQ = B_DIL + 3 * DIL_W
B_MG = -(-(B_MQ + MEM_W) // MEM_W) * MEM_W
PB = B_MG + MEM_W

VMEM_LIMIT = 56 * 1024 * 1024


def _pack_pieces():
    pieces = []
    for base in (O_FQ, O_DQ):
        seg = []
        for hp in range(FOX_HEADS // 2):
            for part in range(3):
                seg.append((base + part * FOX_W + hp * LANES, LANES))
        pieces.append(seg)
    fox, dil = pieces
    return fox + [(O_FG, FOX_W)] + dil + [(O_DG, DIL_W), (O_MQ, MEM_W), (O_MG, MEM_W), (O_FLOG, FOX_HEADS)]


def _pack_cols(w):
    parts = [w[..., s:s + n] for s, n in _pack_pieces()]
    parts.append(jnp.zeros(w.shape[:-1] + (LANES - FOX_HEADS,), w.dtype))
    return jnp.concatenate(parts, axis=-1)


def _split_cols(wp):
    def cut(start, width):
        return wp[..., start:start + width]

    group_a = jnp.concatenate([cut(P_FOX, 3 * FOX_W), cut(P_FLOG, LANES)], axis=-1)
    pad = jnp.zeros(wp.shape[:-1] + (B_MG - B_MQ - MEM_W,), wp.dtype)
    group_b = jnp.concatenate([cut(P_FG, FOX_W), cut(P_DG, DIL_W), cut(P_DIL, 3 * DIL_W), cut(P_MQ, MEM_W), pad,
                               cut(P_MG, MEM_W)], axis=-1)
    return group_a, group_b


def _params(sem=None, **kw):
    return pltpu.CompilerParams(dimension_semantics=sem, vmem_limit_bytes=VMEM_LIMIT, **kw)


def _mesh_pos():
    return lax.axis_index("x"), lax.axis_index("y"), lax.axis_index("c")


def _flip(v, d):
    return 1 - v if d else v


_RELATIONS = [(dx, dy, dc) for dx in (0, 1) for dy in (0, 1) for dc in (0, 1)][1:]
_SIBLING_AND_SAME_CORES = [(0, 0, 1), (1, 0, 0), (0, 1, 0), (1, 1, 0)]


_OTHER_CHIPS = [(1, 0), (0, 1), (1, 1)]


_HBM = pl.BlockSpec(memory_space=pltpu.HBM)
_SEM = pl.BlockSpec(memory_space=pltpu.SEMAPHORE)
_EFFECT = pltpu.SideEffectType.DATAFLOW_SIDE_EFFECTING


def _early_copies(src_refs, land_refs, send_sems, recv_sems, rows, gather, relations):
    x, y, c = _mesh_pos()
    me = 4 * x + 2 * y + c
    copies = []
    for a in range(len(src_refs)):
        for dx, dy, dc in relations:
            px, py, pc = _flip(x, dx), _flip(y, dy), _flip(c, dc)
            peer = 4 * px + 2 * py + pc
            copies.append(pltpu.make_async_remote_copy(
                src_ref=src_refs[a] if gather else src_refs[a].at[pl.ds(peer * rows[a], rows[a]), :],
                dst_ref=land_refs[a].at[pl.ds(me * rows[a], rows[a]), :],
                send_sem=send_sems[a], recv_sem=recv_sems[a],
                device_id=(px, py, pc), device_id_type=pl.DeviceIdType.MESH))
    return copies


def own_slots(shards, name, after=None):
    n = len(shards)
    extra = [] if after is None else [after]
    x, y, c = _mesh_pos()
    me = (4 * x + 2 * y + c).astype(jnp.int32).reshape(1)
    empties = [lax.empty((N_DEV * s.shape[0], s.shape[1]), s.dtype) for s in shards]

    def body(me_ref, *refs):
        for a in range(n):
            refs[2 * n + len(extra) + a][...] = refs[a][...]

    return pl.pallas_call(
        body, name=name,
        grid_spec=pltpu.PrefetchScalarGridSpec(
            num_scalar_prefetch=1, grid=(1,),
            in_specs=[pl.BlockSpec(s.shape, lambda i, w: (0, 0)) for s in shards]
            + [pl.BlockSpec(memory_space=pl.ANY)] * (n + len(extra)),
            out_specs=[pl.BlockSpec(s.shape, lambda i, w: (w[0], 0)) for s in shards]),
        out_shape=[jax.ShapeDtypeStruct(e.shape, e.dtype) for e in empties],
        input_output_aliases={1 + n + a: a for a in range(n)},
        compiler_params=_params(("arbitrary",)),
    )(me, *shards, *empties, *extra)


def early_exchange_start(srcs, name, gather=False, after=None, relations=_RELATIONS, lands=None):
    n = len(srcs)
    if gather:
        rows = [s.shape[0] for s in srcs]
        lands = list(own_slots(srcs, name + "_place") if lands is None else lands)
    else:
        rows = [s.shape[0] // N_DEV for s in srcs]
        lands = [lax.empty(s.shape, s.dtype) for s in srcs]

    extra = [] if after is None else [after]

    def body(*refs):
        src_refs, land_refs = refs[:n], refs[n:2 * n]
        first_sem = 2 * n + len(extra)
        send_sems, recv_sems = refs[first_sem:first_sem + n], refs[first_sem + n:first_sem + 2 * n]
        token = refs[-1]
        for cp in _early_copies(src_refs, land_refs, send_sems, recv_sems, rows, gather, relations):
            cp.start()
        token[...] = jnp.zeros_like(token)

    hbm = lambda a: pltpu.HBM(a.shape, a.dtype)
    outs = pl.pallas_call(
        body, name=name,
        out_shape=[pltpu.SemaphoreType.DMA(())] * (2 * n)
        + [hbm(a) for a in srcs] + [hbm(a) for a in lands] + [jax.ShapeDtypeStruct((8, LANES), F32)],
        in_specs=[_HBM] * (2 * n) + [pl.BlockSpec(memory_space=pl.ANY)] * len(extra),
        out_specs=[_SEM] * (2 * n) + [_HBM] * (2 * n) + [pl.BlockSpec(memory_space=pltpu.VMEM)],
        input_output_aliases={i: 2 * n + i for i in range(2 * n)},
        compiler_params=pltpu.CompilerParams(has_side_effects=_EFFECT),
    )(*[pltpu.with_memory_space_constraint(a, pltpu.HBM) for a in list(srcs) + lands], *extra)
    handle = dict(sems=outs[:2 * n], srcs=outs[2 * n:3 * n], lands=outs[3 * n:4 * n], rows=rows,
                  copies=len(relations))
    return handle, outs[-1]


def early_exchange_wait(handle, after, name):
    n = len(handle["srcs"])
    rows = handle["rows"]
    after = list(after) if isinstance(after, (list, tuple)) else [after]

    def body(*refs):
        src_refs, land_refs = refs[:n], refs[n:2 * n]
        send_sems, recv_sems = refs[2 * n:3 * n], refs[3 * n:4 * n]
        x, y, c = _mesh_pos()
        for a in range(n):
            span = pl.ds(0, handle["copies"] * rows[a])
            all_copies = pltpu.make_async_remote_copy(
                src_ref=land_refs[a].at[span, :], dst_ref=land_refs[a].at[span, :],
                send_sem=send_sems[a], recv_sem=recv_sems[a],
                device_id=(x, y, c), device_id_type=pl.DeviceIdType.MESH)
            all_copies.wait_send()
            all_copies.wait_recv()

    hbm = lambda a: pltpu.HBM(a.shape, a.dtype)
    ins = list(handle["srcs"]) + list(handle["lands"])
    outs = pl.pallas_call(
        body, name=name,
        out_shape=[hbm(a) for a in ins],
        in_specs=[_HBM] * (2 * n) + [_SEM] * (2 * n) + [pl.BlockSpec(memory_space=pl.ANY)] * len(after),
        out_specs=[_HBM] * (2 * n),
        input_output_aliases={i: i for i in range(2 * n)},
        compiler_params=pltpu.CompilerParams(has_side_effects=_EFFECT),
    )(*ins, *handle["sems"], *after)
    return outs[:n], outs[n:]


def pass_on_to_sibling(lands, rows, name):
    n = len(lands)

    def body(*refs):
        land_refs = refs[n:2 * n]
        send_sems, recv_sems = refs[2 * n:]
        x, y, c = _mesh_pos()
        copies = []
        for a in range(n):
            for k, (dx, dy) in enumerate(_OTHER_CHIPS):
                slot = 4 * _flip(x, dx) + 2 * _flip(y, dy) + c
                blk = land_refs[a].at[pl.ds(slot * rows[a], rows[a]), :]
                copies.append(pltpu.make_async_remote_copy(
                    src_ref=blk, dst_ref=blk, send_sem=send_sems.at[a, k], recv_sem=recv_sems.at[a, k],
                    device_id=(x, y, 1 - c), device_id_type=pl.DeviceIdType.MESH))
        for cp in copies:
            cp.start()
        for cp in copies:
            cp.wait_recv()
        for cp in copies:
            cp.wait_send()

    any_spec = pl.BlockSpec(memory_space=pl.ANY)
    return pl.pallas_call(
        body, name=name,
        out_shape=[jax.ShapeDtypeStruct(a.shape, a.dtype) for a in lands],
        in_specs=[any_spec] * n, out_specs=[any_spec] * n,
        input_output_aliases={i: i for i in range(n)},
        scratch_shapes=[pltpu.SemaphoreType.DMA((n, len(_OTHER_CHIPS))), pltpu.SemaphoreType.DMA((n, len(_OTHER_CHIPS)))],
    )(*lands)


def _adamw_update(w, g, m, v):
    mn = ADAM_B1 * m + (1.0 - ADAM_B1) * g
    vn = ADAM_B2 * v + (1.0 - ADAM_B2) * jnp.square(g)
    m_hat = mn / (1.0 - ADAM_B1 ** ADAM_STEP)
    v_hat = vn / (1.0 - ADAM_B2 ** ADAM_STEP)
    return -ADAM_LR * (m_hat / (jnp.sqrt(v_hat) + ADAM_EPS) + ADAM_WD * w), mn, vn


def slot_sum8_adamw(src, land, w, m, v, tr, name):
    rows, cols = land.shape[0] // N_DEV, land.shape[1]
    x, y, c = _mesh_pos()
    me = (4 * x + 2 * y + c).astype(jnp.int32).reshape(1)

    def body(me_ref, src_ref, land_ref, w_ref, m_ref, v_ref, g_ref, d_ref, mo_ref, vo_ref):
        acc = None
        for d in range(N_DEV):
            term = jnp.where(d == me_ref[0], src_ref[0], land_ref[d]).astype(F32)
            acc = term if acc is None else acc + term
        g_ref[...] = acc
        d_ref[...], mo_ref[...], vo_ref[...] = _adamw_update(w_ref[...], acc, m_ref[...], v_ref[...])

    tile = pl.BlockSpec((tr, cols), lambda i, w: (i, 0))
    return pl.pallas_call(
        body, name=name,
        grid_spec=pltpu.PrefetchScalarGridSpec(
            num_scalar_prefetch=1, grid=(rows // tr,),
            in_specs=[pl.BlockSpec((1, tr, cols), lambda i, w: (w[0], i, 0)),
                      pl.BlockSpec((N_DEV, tr, cols), lambda i, w: (0, i, 0)), tile, tile, tile],
            out_specs=[tile] * 4),
        out_shape=[jax.ShapeDtypeStruct((rows, cols), F32)] * 4,
        compiler_params=_params(("arbitrary",)),
    )(me, src.reshape(N_DEV, rows, cols), land.reshape(N_DEV, rows, cols), w, m, v)


def w_in_grad_sum(groups, name):
    rows = groups[0][0].shape[0] // N_DEV
    runs, pos = [], 0
    for start, width in _pack_pieces():
        runs.append((start, width, pos))
        pos += width
    n = len(groups)

    def body(*refs):
        src_refs, land_refs = refs[0:2 * n:2], refs[1:2 * n:2]
        g_ref = refs[2 * n]
        own_bufs, land_bufs = refs[2 * n + 1:3 * n + 1], refs[3 * n + 1:4 * n + 1]
        stage, load_sems, store_sems = refs[4 * n + 1:]
        x, y, c = _mesh_pos()
        me = 4 * x + 2 * y + c
        loads = []
        for k in range(n):
            pair = [pltpu.make_async_copy(src_refs[k].at[pl.ds(me * rows, rows), :], own_bufs[k], load_sems.at[k, 0]),
                    pltpu.make_async_copy(land_refs[k], land_bufs[k], load_sems.at[k, 1])]
            for cp in pair:
                cp.start()
            loads.append(pair)
        for k, (_, _, segments) in enumerate(groups):
            for cp in loads[k]:
                cp.wait()
            for first, packed, width in segments:
                for off in range(0, width, LANES):
                    cols = slice(first + off, first + off + LANES)
                    acc = None
                    for d in range(N_DEV):
                        term = jnp.where(d == me, own_bufs[k][:, cols], land_bufs[k][d * rows:(d + 1) * rows, cols])
                        acc = term.astype(F32) if acc is None else acc + term.astype(F32)
                    stage[packed + off:packed + off + LANES, :] = acc.T
        g_rows = g_ref.reshape(IN_W, LANES)
        stores = [pltpu.make_async_copy(stage.at[pl.ds(p, width), :], g_rows.at[pl.ds(start, width), :], store_sems.at[r])
                  for r, (start, width, p) in enumerate(runs)]
        for cp in stores:
            cp.start()
        for cp in stores:
            cp.wait()

    any_spec = pl.BlockSpec(memory_space=pl.ANY)
    operands = [a for src, land, _ in groups for a in (src, land)]
    return pl.pallas_call(
        body, name=name,
        in_specs=[any_spec] * (2 * n), out_specs=any_spec,
        out_shape=jax.ShapeDtypeStruct((IN_W, 1, LANES), F32),
        scratch_shapes=[pltpu.VMEM((rows, src.shape[1]), src.dtype) for src, _, _ in groups]
        + [pltpu.VMEM(land.shape, land.dtype) for _, land, _ in groups]
        + [pltpu.VMEM((PW, LANES), F32), pltpu.SemaphoreType.DMA((n, 2)), pltpu.SemaphoreType.DMA((len(runs),))],
        compiler_params=_params(),
    )(*operands)


def mm_tn_multi(a_t, bs, tt, name, out_dtype=F32):
    K, T = a_t.shape
    widths = [b.shape[1] for b in bs]
    steps = T // tt

    def body(a_ref, *rest):
        b_refs, o_ref, acc = rest[:-2], rest[-2], rest[-1]

        @pl.when(pl.program_id(0) == 0)
        def _():
            acc[...] = jnp.zeros(acc.shape, F32)

        av = a_ref[...]
        col = 0
        for b_ref, w in zip(b_refs, widths):
            acc[:, col:col + w] += jnp.dot(av, b_ref[...], preferred_element_type=F32)
            col += w

        @pl.when(pl.program_id(0) == steps - 1)
        def _():
            o_ref[...] = acc[...].astype(out_dtype)

    return pl.pallas_call(
        body, name=name, grid=(steps,),
        in_specs=[pl.BlockSpec((K, tt), lambda t: (0, t))] + [pl.BlockSpec((tt, w), lambda t: (t, 0)) for w in widths],
        out_specs=pl.BlockSpec((K, sum(widths)), lambda t: (0, 0)),
        out_shape=jax.ShapeDtypeStruct((K, sum(widths)), out_dtype),
        scratch_shapes=[pltpu.VMEM((K, sum(widths)), F32)],
        compiler_params=_params(("arbitrary",)),
    )(a_t, *bs)


def rms_fwd(x, g, tm, name, with_transpose=False, token=None):
    M, K = x.shape
    extra = [] if token is None else [token]

    def body(x_ref, g_ref, *rest):
        o_ref = rest[len(extra)]
        xv = x_ref[...]
        r = lax.rsqrt(jnp.mean(xv * xv, axis=-1, keepdims=True) + RMS_EPS)
        h = ((xv * r) * g_ref[...]).astype(BF16)
        o_ref[...] = h
        if with_transpose:
            rest[len(extra) + 1][...] = h.T

    out_specs = [pl.BlockSpec((tm, K), lambda i: (i, 0))]
    out_shape = [jax.ShapeDtypeStruct((M, K), BF16)]
    if with_transpose:
        out_specs.append(pl.BlockSpec((K, tm), lambda i: (0, i)))
        out_shape.append(jax.ShapeDtypeStruct((K, M), BF16))
    outs = pl.pallas_call(
        body, name=name, grid=(M // tm,),
        in_specs=[pl.BlockSpec((tm, K), lambda i: (i, 0)), pl.BlockSpec((1, K), lambda i: (0, 0))]
        + [pl.BlockSpec(t.shape, lambda i: (0, 0)) for t in extra],
        out_specs=out_specs, out_shape=out_shape,
        compiler_params=_params(("arbitrary",)),
    )(x, g, *extra)
    return outs if with_transpose else outs[0]


def mm_nn(a, b, tm, tn, name, token=None):
    M, K = a.shape
    N = b.shape[1]
    extra = [] if token is None else [token]

    def body(a_ref, b_ref, *rest):
        rest[-1][...] = jnp.dot(a_ref[...], b_ref[...], preferred_element_type=F32)

    return pl.pallas_call(
        body, name=name, grid=(N // tn, M // tm),
        in_specs=[pl.BlockSpec((tm, K), lambda j, i: (i, 0)), pl.BlockSpec((K, tn), lambda j, i: (0, j))]
        + [pl.BlockSpec(t.shape, lambda j, i: (0, 0)) for t in extra],
        out_specs=pl.BlockSpec((tm, tn), lambda j, i: (i, j)),
        out_shape=jax.ShapeDtypeStruct((M, N), F32),
        compiler_params=_params(("arbitrary", "arbitrary")),
    )(a, b, *extra)


def proj_bwd_rms(pieces, ws, x, g, dres, tm, token, name):
    M, N = x.shape
    residual = [] if dres is None else [dres]

    def body(*refs):
        n = len(pieces)
        p_refs, w_refs = refs[:n], refs[n:n + len(ws)]
        x_ref, g_ref, *dres_ref = refs[n + len(ws):n + len(ws) + 2 + len(residual)]
        dx_ref, dg_ref = refs[-2:]
        dh = None
        for p_ref, (arr, group, col) in zip(p_refs, pieces):
            part = lax.dot_general(p_ref[...], w_refs[group][:, col:col + arr.shape[1]], (((1,), (1,)), ((), ())),
                                   preferred_element_type=F32)
            dh = part if dh is None else dh + part
        xv = x_ref[...]
        r = lax.rsqrt(jnp.mean(xv * xv, axis=-1, keepdims=True) + RMS_EPS)
        xn = xv * r
        dxn = dh * g_ref[...]
        dx = r * (dxn - xn * jnp.mean(dxn * xn, axis=-1, keepdims=True))
        dx_ref[...] = dx + dres_ref[0][...] if residual else dx
        row = lax.broadcasted_iota(jnp.int32, (8, N), 0)
        upd = jnp.where(row == 0, jnp.sum(dh * xn, axis=0, keepdims=True), 0.0)

        @pl.when(pl.program_id(0) == 0)
        def _():
            dg_ref[...] = upd

        @pl.when(pl.program_id(0) != 0)
        def _():
            dg_ref[...] += upd

    row_spec = pl.BlockSpec((tm, N), lambda i: (i, 0))
    return pl.pallas_call(
        body, name=name, grid=(M // tm,),
        in_specs=[pl.BlockSpec((tm, arr.shape[1]), lambda i: (i, 0)) for arr, _, _ in pieces]
        + [pl.BlockSpec(w.shape, lambda i: (0, 0), pipeline_mode=pl.Buffered(1)) for w in ws]
        + [row_spec, pl.BlockSpec((1, N), lambda i: (0, 0))] + [row_spec] * len(residual)
        + [pl.BlockSpec(token.shape, lambda i: (0, 0))],
        out_specs=[row_spec, pl.BlockSpec((8, N), lambda i: (0, 0))],
        out_shape=[jax.ShapeDtypeStruct((M, N), F32), jax.ShapeDtypeStruct((8, N), F32)],
        compiler_params=_params(("arbitrary",)),
    )(*[arr for arr, _, _ in pieces], *ws, x, g, *residual, token)


def _log_sigmoid(z):
    return jnp.minimum(z, 0.0) - jnp.log(1.0 + jnp.exp(-jnp.abs(z)))


def _tri(n, lower):
    r = lax.broadcasted_iota(jnp.int32, (n, n), 0)
    c = lax.broadcasted_iota(jnp.int32, (n, n), 1)
    return jnp.where((r >= c) if lower else (r <= c), 1.0, 0.0).astype(F32)


def fox_gate(proj3, b_pad):
    B, S, _ = proj3.shape
    nblk = S // TK

    def body(f_ref, b_ref, o_ref):
        tri = _tri(TK, True)
        carry = jnp.zeros((1, LANES), F32)
        for n in range(nblk):
            z = f_ref[0, n * TK:(n + 1) * TK, :] + b_ref[...]
            logf = _log_sigmoid(z)
            cs = jnp.dot(tri, logf, preferred_element_type=F32, precision=lax.Precision.HIGHEST) + carry
            carry = cs[TK - 1:TK, :]
            o_ref[0, n * TK:(n + 1) * TK, :] = -cs

    return pl.pallas_call(
        body, name="fox_gate", grid=(B,),
        in_specs=[pl.BlockSpec((1, S, LANES), lambda b: (b, 0, A_FLOG // LANES)),
                  pl.BlockSpec((1, LANES), lambda b: (0, 0))],
        out_specs=pl.BlockSpec((1, S, LANES), lambda b: (b, 0, 0)),
        out_shape=jax.ShapeDtypeStruct((B, S, LANES), F32),
        compiler_params=_params(("arbitrary",)),
    )(proj3, b_pad)


def fox_gate_bwd(drow, dneg, proj3, b_pad):
    B, S, _ = proj3.shape
    nblk = S // TK

    def body(d_ref, r_ref, f_ref, b_ref, o_ref, db_ref):
        tri = _tri(TK, False)
        lane = lax.broadcasted_iota(jnp.int32, (TK, LANES), 1)
        carry = jnp.zeros((1, LANES), F32)
        dbsum = jnp.zeros((1, LANES), F32)
        for n in reversed(range(nblk)):
            dk_side = None
            for hp in range(FOX_HEADS // 2):
                two = jnp.where(lane < 2, r_ref[0, n * TK:(n + 1) * TK, hp * LANES:(hp + 1) * LANES], 0.0)
                two = pltpu.roll(two, 2 * hp, 1) if hp else two
                dk_side = two if dk_side is None else dk_side + two
            dc = jnp.where(lane < FOX_HEADS, d_ref[0, :, n * TK:(n + 1) * TK].T - dk_side, 0.0)
            rs = jnp.dot(tri, dc, preferred_element_type=F32, precision=lax.Precision.HIGHEST) + carry
            carry = rs[0:1, :]
            z = f_ref[0, n * TK:(n + 1) * TK, :] + b_ref[...]
            dz = rs * (1.0 / (1.0 + jnp.exp(z)))
            o_ref[0, n * TK:(n + 1) * TK, :] = dz.astype(BF16)
            dbsum = dbsum + jnp.sum(dz, axis=0, keepdims=True)
        row = lax.broadcasted_iota(jnp.int32, (8, LANES), 0)
        upd = jnp.where(row == 0, dbsum, 0.0)

        @pl.when(pl.program_id(0) == 0)
        def _():
            db_ref[...] = upd

        @pl.when(pl.program_id(0) != 0)
        def _():
            db_ref[...] += upd

    return pl.pallas_call(
        body, name="fox_gate_bwd", grid=(B,),
        in_specs=[pl.BlockSpec((1, LANES, S), lambda b: (b, 0, 0)),
                  pl.BlockSpec((1, S, FOX_W), lambda b: (b, 0, 0)),
                  pl.BlockSpec((1, S, LANES), lambda b: (b, 0, A_FLOG // LANES)),
                  pl.BlockSpec((1, LANES), lambda b: (0, 0))],
        out_specs=[pl.BlockSpec((1, S, LANES), lambda b: (b, 0, 0)), pl.BlockSpec((8, LANES), lambda b: (0, 0))],
        out_shape=[jax.ShapeDtypeStruct((B, S, LANES), BF16), jax.ShapeDtypeStruct((8, LANES), F32)],
        compiler_params=_params(("arbitrary",)),
    )(drow, dneg, proj3, b_pad)


def _rope_tables(S):
    half = ROPE_DIM // 2
    f32 = np.float32
    pos = np.arange(S, dtype=f32)
    inv_freq = f32(1.0) / np.power(f32(ROPE_THETA), np.arange(0, ROPE_DIM, 2, dtype=f32) / f32(ROPE_DIM)).astype(f32)
    ang = (pos[:, None] * inv_freq[None, :]).astype(f32).astype(np.float64)
    cos, sin = np.cos(ang).astype(f32), np.sin(ang).astype(f32)
    one = np.ones((S, HEAD_DIM - ROPE_DIM), f32)
    zero = np.zeros((S, HEAD_DIM - ROPE_DIM), f32)
    zh = np.zeros((S, half), f32)
    c = np.concatenate([cos, cos, one], axis=1)
    s1 = np.concatenate([-sin, zh, zero], axis=1)
    s2 = np.concatenate([zh, sin, zero], axis=1)
    return tuple(jnp.asarray(np.concatenate([t, t], axis=1)) for t in (c, s1, s2))


_HALF_ROPE = ROPE_DIM // 2


def _rope(t, c, s1, s2):
    return t * c + pltpu.roll(t, LANES - _HALF_ROPE, 1) * s1 + pltpu.roll(t, _HALF_ROPE, 1) * s2


def _rope_bwd(d, c, s1, s2):
    return d * c + pltpu.roll(d * s1, _HALF_ROPE, 1) + pltpu.roll(d * s2, LANES - _HALF_ROPE, 1)


def _scale_parts(scale):
    m, _ = math.frexp(scale)
    return (scale, None) if m == 0.5 else (None, scale)


def _log_masks(S, kind):
    nd = 1 if kind == "causal" else S // TQ
    a = np.arange(TQ)[:, None]
    b = np.arange(TK)[None, :]
    out = np.zeros((nd, TQ, TK), np.float32)
    for d in range(nd):
        delta = d * TQ + a - b
        if kind == "causal":
            m = (delta >= 0).astype(np.float64)
        else:
            m = sum(((delta >= 0) & (delta % dil == 0) & (delta <= w)).astype(np.float64) for w, dil in DILATIONS)
        out[d] = np.where(m > 0, np.log(np.maximum(m, 1.0)), NEG_INF)
    return jnp.asarray(out)


def _attn_setup(kind):
    pair = kind != "mem"
    e_dim = HEAD_DIM if pair else MEM_HEAD_DIM
    q_fold, s_scale = _scale_parts(1.0 / math.sqrt(e_dim))
    return dict(pair=pair, col0={"fox": A_FOX, "dil": B_DIL, "mem": B_MQ}[kind],
                n_blocks=FOX_HEADS // 2 if pair else MEM_HEADS, q_fold=q_fold, s_scale=s_scale,
                nh=2 if pair else 1)


def _cat(parts, axis):
    return parts[0] if len(parts) == 1 else jnp.concatenate(parts, axis=axis)


def _log_masks_t(S, kind):
    return jnp.swapaxes(_log_masks(S, kind), 1, 2)


def _head_rows(hh, pair):
    row = lax.broadcasted_iota(jnp.int32, (LANES, 1), 0)
    if not pair:
        return row >= 0
    return (row >= HEAD_DIM * hh) & (row < HEAD_DIM * (hh + 1))


def _attn_t_inputs(kind, src, S, negc_cols, mask, rope, kv):
    cfg = _attn_setup(kind)
    col0 = cfg["col0"]
    ins, in_specs = [], []
    if cfg["pair"]:
        ins.append(src)
        in_specs.append(pl.BlockSpec((1, S, PAIR_W), lambda b, h: (b, 0, col0 // PAIR_W + h)))
    else:
        ins += [src, kv, kv]
        in_specs += [pl.BlockSpec((1, S, LANES), lambda b, h: (b, 0, col0 // LANES + h)),
                     pl.BlockSpec((1, MEM_LEN, LANES), lambda b, h: (b, 0, h)),
                     pl.BlockSpec((1, MEM_LEN, LANES), lambda b, h: (b, 0, MEM_HEADS + h))]
    if negc_cols is not None:
        ins.append(negc_cols)
        in_specs.append(pl.BlockSpec((1, S, LANES), lambda b, h: (b, 0, 0)))
    if mask is not None:
        ins.append(mask)
        in_specs.append(pl.BlockSpec(mask.shape, lambda b, h: (0, 0, 0)))
    if rope is not None:
        ins += list(rope)
        in_specs += [pl.BlockSpec((S, LANES), lambda b, h: (0, 0))] * 3
    return ins, in_specs


def _attn_t_prep(cfg, refs, S, Sk, *, qT2s, ks, vs=None, vTs=None, kTs=None, nb=None):
    pair, nh = cfg["pair"], cfg["nh"]
    lane = lax.broadcasted_iota(jnp.int32, (1, LANES), 1)
    rope_refs = refs["rope"]

    def prep_q(n):
        rows = slice(n * TQ, (n + 1) * TQ)
        q = refs["load_q"](rows)
        if rope_refs is not None:
            q = _rope(q, *[t[rows, :] for t in rope_refs])
        if cfg["q_fold"] is not None:
            q = q * cfg["q_fold"]
        qtb = q.astype(BF16).T
        for hh in range(nh):
            qT2s[n, :, hh * TQ:(hh + 1) * TQ] = jnp.where(_head_rows(hh, pair), qtb, jnp.zeros_like(qtb))

    def prep_kv(n):
        rows = slice(n * TK, (n + 1) * TK)
        k, v = refs["load_kv"](rows)
        if rope_refs is not None:
            k = _rope(k, *[t[rows, :] for t in rope_refs])
        kb = k.astype(BF16)
        vb = v.astype(BF16)
        ks[rows, :] = kb
        if vs is not None:
            vs[rows, :] = vb
        if vTs is not None:
            vTs[n] = vb.T
        if kTs is not None:
            kTs[n] = kb.T
        if nb is not None:
            blk = refs["negc"][0, rows, :]
            for hh in range(nh):
                h = 2 * refs["block"] + hh
                col = jnp.sum(jnp.where(lane == h, blk, 0.0), axis=1, keepdims=True)
                nb[hh, rows, :] = jnp.broadcast_to(col, (TK, LANES))

    for n in range(S // TQ):
        prep_q(n)
    for n in range(Sk // TK):
        prep_kv(n)


def _raw_scores_t(cfg, k, qT2):
    sT = jnp.dot(k, qT2, preferred_element_type=F32)
    if cfg["s_scale"] is not None:
        sT = sT * cfg["s_scale"]
    return sT


def _bias_mask_t(cfg, sT, nb, mask_ref, kc, midx):
    nh = cfg["nh"]
    if nb is None and midx is None:
        return sT
    parts = []
    for hh in range(nh):
        t = sT[:, hh * TQ:(hh + 1) * TQ]
        if nb is not None:
            t = t + jnp.concatenate([nb[hh, kc, :]] * (TQ // LANES), axis=1)
        if midx is not None:
            t = t + mask_ref[midx]
        parts.append(t)
    return _cat(parts, 1)


def _tile_pairs(kind, nq, nk):
    if kind == "mem":
        return [(i, j) for i in range(nq) for j in range(nk)], (lambda i, j: None)
    pairs = [(i, j) for i in range(nq) for j in range(i + 1)]
    if kind == "fox":
        return pairs, (lambda i, j: 0 if j == i else None)
    return pairs, (lambda i, j: i - j)


def attn_fwd(kind, src, S, *, negc_cols=None, mask=None, rope=None, kv=None):
    B = src.shape[0]
    cfg = _attn_setup(kind)
    pair, nh = cfg["pair"], cfg["nh"]
    Sk = S if pair else MEM_LEN
    has_bias, has_rope = negc_cols is not None, rope is not None
    R = nh * TQ
    nq, nk = S // TQ, Sk // TK
    pairs, mask_index = _tile_pairs(kind, nq, nk)

    def body(*refs):
        refs = list(refs)
        if pair:
            qkv_ref = refs.pop(0)
            load_q = lambda rows: qkv_ref[0, rows, 0:LANES]
            load_kv = lambda rows: (qkv_ref[0, rows, LANES:2 * LANES], qkv_ref[0, rows, 2 * LANES:3 * LANES])
        else:
            q_ref, k_ref, v_ref = refs.pop(0), refs.pop(0), refs.pop(0)
            load_q = lambda rows: q_ref[0, rows, :]
            load_kv = lambda rows: (k_ref[0, rows, :], v_ref[0, rows, :])
        negc_ref = refs.pop(0) if has_bias else None
        mask_ref = refs.pop(0) if mask is not None else None
        rope_refs = [refs.pop(0) for _ in range(3)] if has_rope else None
        o_ref, lse_ref, qT2s, ks, vTs, s_a, s_b, p_a, p_b = refs[:9]
        nb = refs[9] if has_bias else None
        _attn_t_prep(cfg, dict(load_q=load_q, load_kv=load_kv, rope=rope_refs, negc=negc_ref,
                               block=pl.program_id(1)), S, Sk, qT2s=qT2s, ks=ks, vTs=vTs, nb=nb)

        def cols(j):
            return slice(j * TK, (j + 1) * TK)

        def scores(i, j):
            return _raw_scores_t(cfg, ks[cols(j), :], qT2s[i])

        def finish(i, m, l, accT):
            oT2 = accT / l
            oT = jnp.where(_head_rows(0, True), oT2[:, 0:TQ], oT2[:, TQ:2 * TQ]) if pair else oT2
            o_ref[0, i * TQ:(i + 1) * TQ, :] = oT.T
            lse_ref[0, 0, i:i + 1, :] = m + jnp.log(l)

        s_bufs, p_bufs = (s_a, s_b), (p_a, p_b)
        s_bufs[0][...] = scores(*pairs[0])
        m = l = accT = None
        for t, (i, j) in enumerate(pairs):
            cur, oth = t % 2, 1 - t % 2
            if t > 0:
                i_prev, j_prev = pairs[t - 1]
                pv = jnp.dot(vTs[j_prev], p_bufs[oth][...], preferred_element_type=F32)
                acc_full = pv if accT is None else accT + pv
            if t + 1 < len(pairs):
                s_bufs[oth][...] = scores(*pairs[t + 1])
            first = j == 0
            if first and t > 0:
                finish(i_prev, m, l, acc_full)
            sT = _bias_mask_t(cfg, s_bufs[cur][...], nb, mask_ref, cols(j), mask_index(i, j))
            m_tile = jnp.max(sT, axis=0, keepdims=True)
            m_new = m_tile if first else jnp.maximum(m, m_tile)
            p = jnp.exp(sT - m_new)
            p_bufs[cur][...] = p.astype(BF16)
            if first:
                l, accT = jnp.sum(p, axis=0, keepdims=True), None
            else:
                alpha = jnp.exp(m - m_new)
                l, accT = alpha * l + jnp.sum(p, axis=0, keepdims=True), acc_full * alpha
            m = m_new
        i_last, j_last = pairs[-1]
        pv = jnp.dot(vTs[j_last], p_bufs[(len(pairs) - 1) % 2][...], preferred_element_type=F32)
        finish(i_last, m, l, pv if accT is None else accT + pv)

    ins, in_specs = _attn_t_inputs(kind, src, S, negc_cols, mask, rope, kv)
    W = cfg["n_blocks"] * LANES
    scratch = [pltpu.VMEM((nq, LANES, R), BF16), pltpu.VMEM((Sk, LANES), BF16), pltpu.VMEM((nk, LANES, TK), BF16),
               pltpu.VMEM((TK, R), F32), pltpu.VMEM((TK, R), F32), pltpu.VMEM((TK, R), BF16), pltpu.VMEM((TK, R), BF16)]
    if has_bias:
        scratch.append(pltpu.VMEM((nh, Sk, LANES), F32))
    return pl.pallas_call(
        body, name=kind + "_attn_fwd", grid=(B, cfg["n_blocks"]),
        in_specs=in_specs,
        out_specs=[pl.BlockSpec((1, S, LANES), lambda b, h: (b, 0, h)),
                   pl.BlockSpec((1, 1, nq, R), lambda b, h: (b, h, 0, 0))],
        out_shape=[jax.ShapeDtypeStruct((B, S, W), F32), jax.ShapeDtypeStruct((B, cfg["n_blocks"], nq, R), F32)],
        scratch_shapes=scratch,
        compiler_params=_params(("arbitrary", "arbitrary")),
    )(*ins)


def attn_bwd(kind, src, do, o, lse, S, *, negc_cols=None, mask=None, rope=None, kv=None, token=None):
    B = src.shape[0]
    cfg = _attn_setup(kind)
    pair, nh, s_scale, q_fold = cfg["pair"], cfg["nh"], cfg["s_scale"], cfg["q_fold"]
    Sk = S if pair else MEM_LEN
    has_bias, has_rope = negc_cols is not None, rope is not None
    R = nh * TQ
    nq, nk = S // TQ, Sk // TK
    pairs, mask_index = _tile_pairs(kind, nq, nk)

    def body(*refs):
        refs = list(refs)
        if pair:
            qkv_ref = refs.pop(0)
            load_q = lambda rows: qkv_ref[0, rows, 0:LANES]
            load_kv = lambda rows: (qkv_ref[0, rows, LANES:2 * LANES], qkv_ref[0, rows, 2 * LANES:3 * LANES])
        else:
            q_ref, k_ref, v_ref = refs.pop(0), refs.pop(0), refs.pop(0)
            load_q = lambda rows: q_ref[0, rows, :]
            load_kv = lambda rows: (k_ref[0, rows, :], v_ref[0, rows, :])
        negc_ref = refs.pop(0) if has_bias else None
        mask_ref = refs.pop(0) if mask is not None else None
        rope_refs = [refs.pop(0) for _ in range(3)] if has_rope else None
        do_ref, o_ref, lse_ref = refs.pop(0), refs.pop(0), refs.pop(0)
        if token is not None:
            refs.pop(0)
        if pair:
            dqkv_ref = refs.pop(0)
            dneg_ref = refs.pop(0) if has_bias else None
            drow_ref = refs.pop(0) if has_bias else None
        else:
            dq_ref, dk_ref, dv_ref = refs.pop(0), refs.pop(0), refs.pop(0)
        qT2s, ks, vs, kTs, doT2s, delta_s, dk_acc, dv_acc = refs[:8]
        bufs_a, bufs_b = refs[8:12], refs[12:16]
        nb, dneg_acc = (refs[16], refs[17]) if has_bias else (None, None)
        lane = lax.broadcasted_iota(jnp.int32, (1, LANES), 1)
        _attn_t_prep(cfg, dict(load_q=load_q, load_kv=load_kv, rope=rope_refs, negc=negc_ref,
                               block=pl.program_id(1)), S, Sk,
                     qT2s=qT2s, ks=ks, vs=vs, kTs=kTs, nb=nb)

        def prep_do(n):
            rows = slice(n * TQ, (n + 1) * TQ)
            doT = do_ref[0, rows, :].astype(BF16).astype(F32).T
            prodT = doT * o_ref[0, rows, :].T
            doTb = doT.astype(BF16)
            for hh in range(nh):
                hm = _head_rows(hh, pair)
                doT2s[n, :, hh * TQ:(hh + 1) * TQ] = jnp.where(hm, doTb, jnp.zeros_like(doTb))
                delta_s[n:n + 1, hh * TQ:(hh + 1) * TQ] = jnp.sum(jnp.where(hm, prodT, 0.0), axis=0, keepdims=True)

        for n in range(nq):
            prep_do(n)
        dk_acc[...] = jnp.zeros(dk_acc.shape, F32)
        dv_acc[...] = jnp.zeros(dv_acc.shape, F32)
        if has_bias:
            dneg_acc[...] = jnp.zeros(dneg_acc.shape, F32)

        def cols(j):
            return slice(j * TK, (j + 1) * TK)

        nt_dims = (((1,), (1,)), ((), ()))

        def first_products(i, j, bufs):
            bufs[0][...] = _raw_scores_t(cfg, ks[cols(j), :], qT2s[i])
            bufs[1][...] = jnp.dot(vs[cols(j), :], doT2s[i], preferred_element_type=F32)

        def last_products(i, j, bufs, dqT2):
            dv_acc[j] += lax.dot_general(doT2s[i], bufs[2][...], nt_dims, preferred_element_type=F32)
            dk_acc[j] += lax.dot_general(qT2s[i], bufs[3][...], nt_dims, preferred_element_type=F32)
            dq = jnp.dot(kTs[j], bufs[3][...], preferred_element_type=F32)
            return dq if dqT2 is None else dqT2 + dq

        def finish_q(i, dqT2, drow):
            rows = slice(i * TQ, (i + 1) * TQ)
            dqT = jnp.where(_head_rows(0, True), dqT2[:, 0:TQ], dqT2[:, TQ:2 * TQ]) if pair else dqT2
            dq = dqT.T
            if q_fold is not None:
                dq = dq * q_fold
            if has_rope:
                dq = _rope_bwd(dq, *[t[rows, :] for t in rope_refs])
            if pair:
                dqkv_ref[0, rows, 0:LANES] = dq.astype(BF16)
            else:
                dq_ref[0, rows, :] = dq.astype(BF16)
            if has_bias:
                drow_ref[0, 0, i:i + 1, :] = drow

        bufs = (bufs_a, bufs_b)
        first_products(*pairs[0], bufs[0])
        dqT2 = drow = None
        for t, (i, j) in enumerate(pairs):
            cur, oth = bufs[t % 2], bufs[1 - t % 2]
            first = j == 0
            if first and t > 0:
                i_prev, j_prev = pairs[t - 1]
                finish_q(i_prev, last_products(i_prev, j_prev, oth, dqT2), drow)
                dqT2 = drow = None
            sT = _bias_mask_t(cfg, cur[0][...], nb, mask_ref, cols(j), mask_index(i, j))
            pT = jnp.exp(sT - lse_ref[0, 0, i:i + 1, :])
            dsT = pT * (cur[1][...] - delta_s[i:i + 1, :])
            if has_bias:
                tile_rows = jnp.sum(dsT, axis=0, keepdims=True)
                drow = tile_rows if drow is None else drow + tile_rows
                for hh in range(nh):
                    part = dsT[:, hh * TQ:hh * TQ + LANES]
                    for u in range(1, TQ // LANES):
                        part = part + dsT[:, hh * TQ + u * LANES:hh * TQ + (u + 1) * LANES]
                    dneg_acc[hh, cols(j), :] += part
            if s_scale is not None:
                dsT = dsT * s_scale
            cur[2][...] = pT.astype(BF16)
            cur[3][...] = dsT.astype(BF16)
            if not first:
                dqT2 = last_products(*pairs[t - 1], oth, dqT2)
            if t + 1 < len(pairs):
                first_products(*pairs[t + 1], oth)
        i_last, j_last = pairs[-1]
        finish_q(i_last, last_products(i_last, j_last, bufs[(len(pairs) - 1) % 2], dqT2), drow)

        for n in range(nk):
            rows = slice(n * TK, (n + 1) * TK)
            dk = dk_acc[n].T
            dv = dv_acc[n].T
            if has_rope:
                dk = _rope_bwd(dk, *[t[rows, :] for t in rope_refs])
            if pair:
                dqkv_ref[0, rows, LANES:2 * LANES] = dk.astype(BF16)
                dqkv_ref[0, rows, 2 * LANES:3 * LANES] = dv.astype(BF16)
            else:
                dk_ref[0, rows, :] = dk.astype(BF16)
                dv_ref[0, rows, :] = dv.astype(BF16)
            if has_bias:
                x0 = jnp.sum(dneg_acc[0, rows, :], axis=1, keepdims=True)
                x1 = jnp.sum(dneg_acc[1, rows, :], axis=1, keepdims=True)
                dneg_ref[0, rows, :] = jnp.where(lane == 0, x0, jnp.where(lane == 1, x1, 0.0))

    ins, in_specs = _attn_t_inputs(kind, src, S, negc_cols, mask, rope, kv)
    row_spec = pl.BlockSpec((1, S, LANES), lambda b, h: (b, 0, h))
    vec_spec = pl.BlockSpec((1, 1, nq, R), lambda b, h: (b, h, 0, 0))
    ins += [do, o, lse]
    in_specs += [row_spec, row_spec, vec_spec]
    if token is not None:
        ins.append(token)
        in_specs.append(pl.BlockSpec(token.shape, lambda b, h: (0, 0)))
    W = cfg["n_blocks"] * LANES
    if pair:
        out_specs = [pl.BlockSpec((1, S, PAIR_W), lambda b, h: (b, 0, h))]
        out_shape = [jax.ShapeDtypeStruct((B, S, 3 * W), BF16)]
        if has_bias:
            out_specs += [row_spec, vec_spec]
            out_shape += [jax.ShapeDtypeStruct((B, S, W), F32), jax.ShapeDtypeStruct((B, cfg["n_blocks"], nq, R), F32)]
    else:
        kv_spec = pl.BlockSpec((1, MEM_LEN, LANES), lambda b, h: (b, 0, h))
        out_specs = [row_spec, kv_spec, kv_spec]
        out_shape = [jax.ShapeDtypeStruct((B, S, W), BF16)] + [jax.ShapeDtypeStruct((B, MEM_LEN, W), BF16)] * 2
    scratch = [pltpu.VMEM((nq, LANES, R), BF16), pltpu.VMEM((Sk, LANES), BF16),
               pltpu.VMEM((Sk, LANES), BF16), pltpu.VMEM((nk, LANES, TK), BF16), pltpu.VMEM((nq, LANES, R), BF16),
               pltpu.VMEM((nq, R), F32), pltpu.VMEM((nk, LANES, TK), F32), pltpu.VMEM((nk, LANES, TK), F32)]
    pair_bufs = [pltpu.VMEM((TK, R), F32), pltpu.VMEM((TK, R), F32), pltpu.VMEM((TK, R), BF16), pltpu.VMEM((TK, R), BF16)]
    scratch += pair_bufs + pair_bufs
    if has_bias:
        scratch += [pltpu.VMEM((nh, Sk, LANES), F32), pltpu.VMEM((nh, Sk, LANES), F32)]
    return pl.pallas_call(
        body, name=kind + "_attn_bwd", grid=(B, cfg["n_blocks"]),
        in_specs=in_specs, out_specs=out_specs, out_shape=out_shape, scratch_shapes=scratch,
        compiler_params=_params(("arbitrary", "arbitrary")),
    )(*ins)


def _sigmoid(g):
    return 1.0 / (1.0 + jnp.exp(-g))


def out_step(proj, o_fox, o_dil, o_mem, w_out, x, target, gf, tm):
    T = x.shape[0]

    def body(fg_ref, dg_ref, mg_ref, of_ref, od_ref, om_ref, w_ref, x_ref, t_ref, gf_ref,
             dx_ref, dof_ref, dod_ref, dom_ref, dfg_ref, ddg_ref, dmg_ref, gw_ref, sm_ref, gw_acc):
        branches = []
        for g_ref, o_ref in ((fg_ref, of_ref), (dg_ref, od_ref), (mg_ref, om_ref)):
            g = g_ref[...]
            sg = _sigmoid(g)
            o = o_ref[...]
            branches.append((g, sg, o))
        ymix = jnp.concatenate([(o * (g * sg)).astype(BF16) for g, sg, o in branches], axis=1)
        x2 = x_ref[...] + jnp.dot(ymix, w_ref[...], preferred_element_type=F32)
        r = lax.rsqrt(jnp.mean(x2 * x2, axis=-1, keepdims=True) + RMS_EPS)
        yn = x2 * r
        err = yn * gf_ref[...] - t_ref[...]
        loss = 0.5 * jnp.sum(jnp.sum(err * err, axis=-1, keepdims=True) / D_MODEL, axis=0, keepdims=True)
        dyf = err / D_MODEL
        dgf = jnp.sum(dyf * yn, axis=0, keepdims=True)
        dyn = dyf * gf_ref[...]
        dx2 = r * (dyn - yn * jnp.mean(dyn * yn, axis=-1, keepdims=True))
        dx_ref[...] = dx2
        dxb = dx2.astype(BF16)
        dmix = lax.dot_general(dxb, w_ref[...], (((1,), (1,)), ((), ())), preferred_element_type=F32)
        col = 0
        for (g, sg, o), do_ref, dgate_ref in zip(branches, (dof_ref, dod_ref, dom_ref), (dfg_ref, ddg_ref, dmg_ref)):
            d = dmix[:, col:col + g.shape[1]]
            col += g.shape[1]
            do_ref[...] = (d * (g * sg)).astype(BF16)
            dgate_ref[...] = (d * o * (sg * (1.0 + g * (1.0 - sg)))).astype(BF16)
        row = lax.broadcasted_iota(jnp.int32, (8, D_MODEL), 0)
        upd = jnp.where(row == 0, dgf, jnp.where(row == 1, loss, 0.0))

        @pl.when(pl.program_id(0) == 0)
        def _():
            sm_ref[...] = jnp.zeros(sm_ref.shape, F32)
            gw_acc[...] = jnp.zeros(gw_acc.shape, F32)

        sm_ref[...] += upd
        gw_acc[...] += lax.dot_general(ymix, dxb, (((0,), (0,)), ((), ())), preferred_element_type=F32)

        @pl.when(pl.program_id(0) == T // tm - 1)
        def _():
            gw_ref[...] = gw_acc[...].astype(BF16)

    def rows(w, col=0):
        return pl.BlockSpec((tm, w), lambda i: (i, col))

    return pl.pallas_call(
        body, name="out_step", grid=(T // tm,),
        in_specs=[rows(FOX_W, B_FG // FOX_W), rows(DIL_W, B_DG // DIL_W), rows(MEM_W, B_MG // MEM_W),
                  rows(FOX_W), rows(DIL_W), rows(MEM_W),
                  pl.BlockSpec((MIX_W, D_MODEL), lambda i: (0, 0)),
                  rows(D_MODEL), rows(D_MODEL), pl.BlockSpec((1, D_MODEL), lambda i: (0, 0))],
        out_specs=[rows(D_MODEL), rows(FOX_W), rows(DIL_W), rows(MEM_W), rows(FOX_W), rows(DIL_W), rows(MEM_W),
                   pl.BlockSpec((MIX_W, D_MODEL), lambda i: (0, 0)), pl.BlockSpec((8, D_MODEL), lambda i: (0, 0))],
        out_shape=[jax.ShapeDtypeStruct((T, D_MODEL), F32), jax.ShapeDtypeStruct((T, FOX_W), BF16),
                   jax.ShapeDtypeStruct((T, DIL_W), BF16), jax.ShapeDtypeStruct((T, MEM_W), BF16),
                   jax.ShapeDtypeStruct((T, FOX_W), BF16), jax.ShapeDtypeStruct((T, DIL_W), BF16),
                   jax.ShapeDtypeStruct((T, MEM_W), BF16), jax.ShapeDtypeStruct((MIX_W, D_MODEL), BF16),
                   jax.ShapeDtypeStruct((8, D_MODEL), F32)],
        scratch_shapes=[pltpu.VMEM((MIX_W, D_MODEL), F32)],
        compiler_params=_params(("arbitrary",)),
    )(proj, proj, proj, o_fox, o_dil, o_mem, w_out, x, target, gf)


def adamw_columns_first(w, g, m, v, name):
    N = w.shape[0]
    parts, step_rows = 4, 16
    size = -(-N // (parts * step_rows)) * step_rows
    bounds = [(p * size, min((p + 1) * size, N)) for p in range(parts)]

    def body(w_hbm, g_hbm, m_hbm, v_hbm, d_hbm, mo_hbm, vo_hbm, wb, gb, mb, vb, db, mob, vob, load_sems, store_sems):
        ins = ((w_hbm, wb), (g_hbm, gb), (m_hbm, mb), (v_hbm, vb))
        outs = ((d_hbm, db), (mo_hbm, mob), (vo_hbm, vob))

        def rows_copy(src, dst, lo, hi, sem):
            return pltpu.make_async_copy(src.at[pl.ds(lo, hi - lo), :], dst.at[pl.ds(lo, hi - lo), :], sem)

        def update(rows):
            db[rows, :], mob[rows, :], vob[rows, :] = _adamw_update(wb[rows, :], gb[rows, :], mb[rows, :], vb[rows, :])

        loads = [[rows_copy(h.reshape(N, LANES), b, lo, hi, load_sems.at[p, k]) for k, (h, b) in enumerate(ins)]
                 for p, (lo, hi) in enumerate(bounds)]
        for part in loads:
            for cp in part:
                cp.start()
        stores = []
        for p, (lo, hi) in enumerate(bounds):
            for cp in loads[p]:
                cp.wait()
            whole = (hi - lo) // step_rows

            def step(i, _, lo=lo):
                update(pl.ds(pl.multiple_of(lo + i * step_rows, step_rows), step_rows))
                return 0

            lax.fori_loop(0, whole, step, 0, unroll=4)
            if lo + whole * step_rows < hi:
                update(slice(lo + whole * step_rows, hi))
            leaving = [rows_copy(b, h.reshape(N, LANES), lo, hi, store_sems.at[p, k]) for k, (h, b) in enumerate(outs)]
            for cp in leaving:
                cp.start()
            stores += leaving
        for cp in stores:
            cp.wait()

    any_spec = pl.BlockSpec(memory_space=pl.ANY)
    return pl.pallas_call(
        body, name=name,
        in_specs=[any_spec] * 4, out_specs=[any_spec] * 3,
        out_shape=[jax.ShapeDtypeStruct(w.shape, F32)] * 3,
        scratch_shapes=[pltpu.VMEM((N, LANES), F32)] * 7
        + [pltpu.SemaphoreType.DMA((parts, 4)), pltpu.SemaphoreType.DMA((parts, 3))],
        compiler_params=_params(),
    )(w, g, m, v)


def _pad_row(v, width):
    return jnp.concatenate([v, jnp.zeros((1, width - v.shape[1]), v.dtype)], axis=1)


def local_grads(x, mem, norm_g, b_forget, mem_norm_g, final_norm_g, loss_target, first_token, first_weights,
                late_weights, start_exchange):
    B, S, D = x.shape
    T = B * S
    xt = x.reshape(T, D)
    memt = mem.reshape(B * MEM_LEN, D)
    b_pad = _pad_row(b_forget, LANES)

    h, h_t = rms_fwd(xt, norm_g, 512, "rms_x", with_transpose=True, token=first_token)
    mh, mh_t = rms_fwd(memt, mem_norm_g, B * MEM_LEN, "rms_mem", with_transpose=True, token=first_token)
    w_in_a, proj_token = first_weights([h, mh])
    proj_a = mm_nn(h, w_in_a, 1024, PA, "in_proj_a", proj_token)
    proj_a3 = proj_a.reshape(B, S, PA)

    negc = fox_gate(proj_a3, b_pad)
    causal = _log_masks_t(S, "causal")
    dilated = _log_masks_t(S, "dilated")
    rope = _rope_tables(S)

    o_fox, lse_fox = attn_fwd("fox", proj_a3, S, negc_cols=negc, mask=causal)

    w_in_b, w_kv, w_out = late_weights(o_fox)
    proj_b = mm_nn(h, w_in_b, 1024, PB // 2, "in_proj_b")
    proj_b3 = proj_b.reshape(B, S, PB)
    o_dil, lse_dil = attn_fwd("dil", proj_b3, S, mask=dilated, rope=rope)

    mkv = mm_nn(mh, w_kv, B * MEM_LEN, 2 * MEM_W, "mem_kv_proj")
    mkv3 = mkv.reshape(B, MEM_LEN, 2 * MEM_W)
    o_mem, lse_mem = attn_fwd("mem", proj_b3, S, kv=mkv3)

    dx2, do_fox, do_dil, do_mem, dfg, ddg, dmg, g_out, small_out = out_step(
        proj_b, o_fox.reshape(T, FOX_W), o_dil.reshape(T, DIL_W), o_mem.reshape(T, MEM_W), w_out,
        xt, loss_target.reshape(T, D), final_norm_g.reshape(1, D), 256)

    gates = [(dfg, 1, B_FG), (ddg, 1, B_DG), (dmg, 1, B_MG)]
    g_gates = mm_tn_multi(h_t, [piece[0] for piece in gates], 1024, "w_in_grad_gates", BF16)

    dqkv_fox, dneg, drow = attn_bwd("fox", proj_a3, do_fox.reshape(B, S, FOX_W), o_fox, lse_fox, S,
                                    negc_cols=negc, mask=causal)
    drow = drow.reshape(B, FOX_HEADS // 2, S // TQ, 2, TQ).transpose(0, 1, 3, 2, 4).reshape(B, FOX_HEADS, S)
    drow = jnp.pad(drow, ((0, 0), (0, LANES - FOX_HEADS), (0, 0)))
    dflog, db_part = fox_gate_bwd(drow, dneg, proj_a3, b_pad)
    fox = [(dqkv_fox.reshape(T, 3 * FOX_W), 0, A_FOX), (dflog.reshape(T, LANES), 0, A_FLOG)]
    g_fox = mm_tn_multi(h_t, [piece[0] for piece in fox], 1024, "w_in_grad_fox", BF16)
    first, token = start_exchange([g_gates, g_out, g_fox], "early_exchange_a")

    (dqkv_dil,) = attn_bwd("dil", proj_b3, do_dil.reshape(B, S, DIL_W), o_dil, lse_dil, S, mask=dilated, rope=rope,
                           token=token)
    dil = [(dqkv_dil.reshape(T, 3 * DIL_W), 1, B_DIL)]
    g_dil = mm_tn_multi(h_t, [piece[0] for piece in dil], 1024, "w_in_grad_dil", BF16)
    second, token = start_exchange([g_dil], "early_exchange_b")

    dmq, dmk, dmv = attn_bwd("mem", proj_b3, do_mem.reshape(B, S, MEM_W), o_mem, lse_mem, S, kv=mkv3, token=token)
    mq = [(dmq.reshape(T, MEM_W), 1, B_MQ)]
    g_mq = mm_tn_multi(h_t, [piece[0] for piece in mq], 1024, "w_in_grad_mq", BF16)
    dmkv = jnp.concatenate([dmk, dmv], axis=2).reshape(B * MEM_LEN, 2 * MEM_W)
    g_kv = mm_tn_multi(mh_t, [dmkv], B * MEM_LEN, "w_kv_grad", BF16)
    third, token = start_exchange([g_mq, g_kv], "early_exchange_c")

    grad_x, dng = proj_bwd_rms(gates + fox + dil + mq, (w_in_a, w_in_b), xt, norm_g, dx2, 512, token,
                               "in_proj_bwd")
    _, dmng = proj_bwd_rms([(dmkv, 0, 0)], (w_kv,), memt, mem_norm_g, None, B * MEM_LEN, token, "mem_kv_bwd")

    small = jnp.concatenate([dng[0:1], dmng[0:1], small_out[0:1], _pad_row(db_part[0:1], D), small_out[1:2],
                             jnp.zeros((3, D), F32)], axis=0)
    early = [(first, dqkv_dil), (second, dmq), (third, grad_x)]
    return grad_x.reshape(B, S, D), early, small


def kernel(x, mem, norm_g, w_in, b_forget, mem_norm_g, w_mem_kv, w_out, final_norm_g, loss_target, m_norm_g, m_w_in, m_b_forget, m_mem_norm_g, m_w_mem_kv, m_w_out, m_final_norm_g, v_norm_g, v_w_in, v_b_forget, v_mem_norm_g, v_w_mem_kv, v_w_out, v_final_norm_g):
    D = D_MODEL
    shard_a, shard_b = _split_cols(_pack_cols(w_in).astype(BF16).reshape(w_in.shape[1], PW))
    gather_a, first_token = early_exchange_start([shard_a], "first_gather", gather=True,
                                                 relations=_SIBLING_AND_SAME_CORES)
    late_shards = [shard_b, w_mem_kv[0].astype(BF16), w_out[0].astype(BF16)]
    late_lands = own_slots(late_shards, "late_gather_place", after=first_token)
    late = {}

    def first_weights(after):
        _, gathered = early_exchange_wait(gather_a, list(after) + [late_lands[0]], "first_gather_wait")
        (w_in_a,) = pass_on_to_sibling(gathered, gather_a["rows"], "first_gather_pass")
        late["gather"], token = early_exchange_start(
            late_shards, "late_gather", gather=True, after=w_in_a, relations=_SIBLING_AND_SAME_CORES,
            lands=late_lands)
        return w_in_a, token

    def late_weights(after):
        _, gathered = early_exchange_wait(late["gather"], after, "late_gather_wait")
        return pass_on_to_sibling(gathered, late["gather"]["rows"], "late_gather_pass")

    grad_x, early, small = local_grads(
        x, mem, norm_g, b_forget, mem_norm_g, final_norm_g, loss_target, first_token, first_weights, late_weights,
        early_exchange_start)

    (first, after_first), (second, after_second), (third, after_third) = early
    small_handle, small_token = early_exchange_start([jnp.tile(small, (N_DEV, 1))], "small_exchange")
    (src_gates, src_out, src_fox), (land_gates, land_out, land_fox) = early_exchange_wait(
        first, [after_first, small_token], "early_wait_a")
    (src_dil,), (land_dil,) = early_exchange_wait(second, after_second, "early_wait_b")
    (src_mq, src_kv), (land_mq, land_kv) = early_exchange_wait(third, after_third, "early_wait_c")
    gw_out, d_out, m_out, v_out = slot_sum8_adamw(src_out, land_out, w_out[0], m_w_out[0], v_w_out[0], 256,
                                                  "sum_adamw_w_out")
    gw_kv, d_kv, m_kv, v_kv = slot_sum8_adamw(src_kv, land_kv, w_mem_kv[0], m_w_mem_kv[0], v_w_mem_kv[0], 128,
                                              "sum_adamw_w_kv")
    gw_in_cols = w_in_grad_sum(
        [(src_fox, land_fox, [(0, P_FOX, 3 * FOX_W), (3 * FOX_W, P_FLOG, LANES)]),
         (src_gates, land_gates, [(0, P_FG, FOX_W), (FOX_W, P_DG, DIL_W), (FOX_W + DIL_W, P_MG, MEM_W)]),
         (src_dil, land_dil, [(0, P_DIL, 3 * DIL_W)]),
         (src_mq, land_mq, [(0, P_MQ, MEM_W)])], "sum_w_in")
    gw_in = jnp.transpose(gw_in_cols, (1, 2, 0))

    def rows8(*rows):
        rows = [r.reshape(1, -1) for r in rows]
        rows = [_pad_row(r, D) for r in rows]
        return jnp.concatenate(rows + [jnp.zeros((8 - len(rows), D), F32)], axis=0)

    sw = rows8(norm_g, mem_norm_g, final_norm_g, b_forget)
    sm = rows8(m_norm_g, m_mem_norm_g, m_final_norm_g, m_b_forget)
    sv = rows8(v_norm_g, v_mem_norm_g, v_final_norm_g, v_b_forget)
    (src_small,), (land_small,) = early_exchange_wait(small_handle, gw_in_cols, "small_wait")
    tot, d_s, m_s, v_s = slot_sum8_adamw(src_small, land_small, sw, sm, sv, 8, "sum_adamw_small")
    loss = tot[4, 0]
    g_norm, g_mem_norm, g_final, g_b = tot[0:1], tot[1:2], tot[2], tot[3:4, :FOX_HEADS]
    columns_first = lambda a: jnp.transpose(a, (2, 0, 1))
    d_in, m_in, v_in = [jnp.transpose(o, (1, 2, 0)) for o in adamw_columns_first(
        columns_first(w_in), gw_in_cols, columns_first(m_w_in), columns_first(v_w_in), "adamw_w_in")]

    def small_outs(t):
        return t[0:1], t[3:4, :FOX_HEADS], t[1:2], t[2]

    grads = (g_norm, gw_in, g_b, g_mem_norm, gw_kv[None], gw_out[None], g_final)
    outs = []
    for t, big in ((d_s, (d_in, d_kv, d_out)), (m_s, (m_in, m_kv, m_out)), (v_s, (v_in, v_kv, v_out))):
        n, b, mn, f = small_outs(t)
        outs += [n, big[0], b, mn, big[1][None], big[2][None], f]
    return (loss, grad_x, *grads, *outs)
```

```python
import math

import numpy as np
import jax
import jax.numpy as jnp
from jax import lax
from jax.experimental import pallas as pl
from jax.experimental.pallas import tpu as pltpu

F32 = jnp.float32
BF16 = jnp.bfloat16

D_MODEL = 1024
HEAD_DIM = 64
FOX_HEADS = 12
DIL_HEADS = 12
MEM_HEADS = 4
MEM_HEAD_DIM = 128
MEM_LEN = 256
FOX_W = FOX_HEADS * HEAD_DIM
DIL_W = DIL_HEADS * HEAD_DIM
MEM_W = MEM_HEADS * MEM_HEAD_DIM
MIX_W = FOX_W + DIL_W + MEM_W
DILATIONS = ((128, 1), (512, 4), (2048, 16))
ROPE_THETA = 500000.0
ROPE_DIM = HEAD_DIM // 4
RMS_EPS = 1e-6
NEG_INF = -1e30
IN_W = 4 * FOX_W + FOX_HEADS + 4 * DIL_W + 2 * MEM_W

ADAM_LR = 0.001
ADAM_B1 = 0.9
ADAM_B2 = 0.999
ADAM_EPS = 1e-08
ADAM_WD = 0.01
ADAM_STEP = 10

N_DEV = 8
LANES = 128
PAIR_W = 3 * LANES
TQ = 256
TK = 256

O_FQ, O_FK, O_FV, O_FG = 0, FOX_W, 2 * FOX_W, 3 * FOX_W
O_FLOG = 4 * FOX_W
O_DQ = O_FLOG + FOX_HEADS
O_DK, O_DV, O_DG = O_DQ + DIL_W, O_DQ + 2 * DIL_W, O_DQ + 3 * DIL_W
O_MQ = O_DQ + 4 * DIL_W
O_MG = O_MQ + MEM_W
P_FOX = 0
P_FG = P_FOX + 3 * FOX_W
P_DIL = P_FG + FOX_W
P_DG = P_DIL + 3 * DIL_W
P_MQ = P_DG + DIL_W
P_MG = P_MQ + MEM_W
P_FLOG = P_MG + MEM_W
PW = P_FLOG + LANES
A_FOX = 0
A_FLOG = A_FOX + 3 * FOX_W
PA = A_FLOG + LANES
B_FG = 0
B_DG = B_FG + FOX_W
B_DIL = B_DG + DIL_W
B_MQ = B_DIL + 3 * DIL_W
B_MG = -(-(B_MQ + MEM_W) // MEM_W) * MEM_W
PB = B_MG + MEM_W

VMEM_LIMIT = 56 * 1024 * 1024


def _pack_pieces():
    pieces = []
    for base in (O_FQ, O_DQ):
        seg = []
        for hp in range(FOX_HEADS // 2):
            for part in range(3):
                seg.append((base + part * FOX_W + hp * LANES, LANES))
        pieces.append(seg)
    fox, dil = pieces
    return fox + [(O_FG, FOX_W)] + dil + [(O_DG, DIL_W), (O_MQ, MEM_W), (O_MG, MEM_W), (O_FLOG, FOX_HEADS)]


def _pack_cols(w):
    parts = [w[..., s:s + n] for s, n in _pack_pieces()]
    parts.append(jnp.zeros(w.shape[:-1] + (LANES - FOX_HEADS,), w.dtype))
    return jnp.concatenate(parts, axis=-1)


def _split_cols(wp):
    def cut(start, width):
        return wp[..., start:start + width]

    group_a = jnp.concatenate([cut(P_FOX, 3 * FOX_W), cut(P_FLOG, LANES)], axis=-1)
    pad = jnp.zeros(wp.shape[:-1] + (B_MG - B_MQ - MEM_W,), wp.dtype)
    group_b = jnp.concatenate([cut(P_FG, FOX_W), cut(P_DG, DIL_W), cut(P_DIL, 3 * DIL_W), cut(P_MQ, MEM_W), pad,
                               cut(P_MG, MEM_W)], axis=-1)
    return group_a, group_b


def _params(sem=None, **kw):
    return pltpu.CompilerParams(dimension_semantics=sem, vmem_limit_bytes=VMEM_LIMIT, **kw)


def _mesh_pos():
    return lax.axis_index("x"), lax.axis_index("y"), lax.axis_index("c")


def _flip(v, d):
    return 1 - v if d else v


_RELATIONS = [(dx, dy, dc) for dx in (0, 1) for dy in (0, 1) for dc in (0, 1)][1:]
_SIBLING_AND_SAME_CORES = [(0, 0, 1), (1, 0, 0), (0, 1, 0), (1, 1, 0)]


_OTHER_CHIPS = [(1, 0), (0, 1), (1, 1)]


_HBM = pl.BlockSpec(memory_space=pltpu.HBM)
_SEM = pl.BlockSpec(memory_space=pltpu.SEMAPHORE)
_EFFECT = pltpu.SideEffectType.DATAFLOW_SIDE_EFFECTING


def _early_copies(src_refs, land_refs, send_sems, recv_sems, rows, gather, relations):
    x, y, c = _mesh_pos()
    me = 4 * x + 2 * y + c
    copies = []
    for a in range(len(src_refs)):
        for dx, dy, dc in relations:
            px, py, pc = _flip(x, dx), _flip(y, dy), _flip(c, dc)
            peer = 4 * px + 2 * py + pc
            copies.append(pltpu.make_async_remote_copy(
                src_ref=src_refs[a] if gather else src_refs[a].at[pl.ds(peer * rows[a], rows[a]), :],
                dst_ref=land_refs[a].at[pl.ds(me * rows[a], rows[a]), :],
                send_sem=send_sems[a], recv_sem=recv_sems[a],
                device_id=(px, py, pc), device_id_type=pl.DeviceIdType.MESH))
    return copies


def own_slots(shards, name, after=None):
    n = len(shards)
    extra = [] if after is None else [after]
    x, y, c = _mesh_pos()
    me = (4 * x + 2 * y + c).astype(jnp.int32).reshape(1)
    empties = [lax.empty((N_DEV * s.shape[0], s.shape[1]), s.dtype) for s in shards]

    def body(me_ref, *refs):
        for a in range(n):
            refs[2 * n + len(extra) + a][...] = refs[a][...]

    return pl.pallas_call(
        body, name=name,
        grid_spec=pltpu.PrefetchScalarGridSpec(
            num_scalar_prefetch=1, grid=(1,),
            in_specs=[pl.BlockSpec(s.shape, lambda i, w: (0, 0)) for s in shards]
            + [pl.BlockSpec(memory_space=pl.ANY)] * (n + len(extra)),
            out_specs=[pl.BlockSpec(s.shape, lambda i, w: (w[0], 0)) for s in shards]),
        out_shape=[jax.ShapeDtypeStruct(e.shape, e.dtype) for e in empties],
        input_output_aliases={1 + n + a: a for a in range(n)},
        compiler_params=_params(("arbitrary",)),
    )(me, *shards, *empties, *extra)


def early_exchange_start(srcs, name, gather=False, after=None, relations=_RELATIONS, lands=None):
    n = len(srcs)
    if gather:
        rows = [s.shape[0] for s in srcs]
        lands = list(own_slots(srcs, name + "_place") if lands is None else lands)
    else:
        rows = [s.shape[0] // N_DEV for s in srcs]
        lands = [lax.empty(s.shape, s.dtype) for s in srcs]

    extra = [] if after is None else [after]

    def body(*refs):
        src_refs, land_refs = refs[:n], refs[n:2 * n]
        first_sem = 2 * n + len(extra)
        send_sems, recv_sems = refs[first_sem:first_sem + n], refs[first_sem + n:first_sem + 2 * n]
        token = refs[-1]
        for cp in _early_copies(src_refs, land_refs, send_sems, recv_sems, rows, gather, relations):
            cp.start()
        token[...] = jnp.zeros_like(token)

    hbm = lambda a: pltpu.HBM(a.shape, a.dtype)
    outs = pl.pallas_call(
        body, name=name,
        out_shape=[pltpu.SemaphoreType.DMA(())] * (2 * n)
        + [hbm(a) for a in srcs] + [hbm(a) for a in lands] + [jax.ShapeDtypeStruct((8, LANES), F32)],
        in_specs=[_HBM] * (2 * n) + [pl.BlockSpec(memory_space=pl.ANY)] * len(extra),
        out_specs=[_SEM] * (2 * n) + [_HBM] * (2 * n) + [pl.BlockSpec(memory_space=pltpu.VMEM)],
        input_output_aliases={i: 2 * n + i for i in range(2 * n)},
        compiler_params=pltpu.CompilerParams(has_side_effects=_EFFECT),
    )(*[pltpu.with_memory_space_constraint(a, pltpu.HBM) for a in list(srcs) + lands], *extra)
    handle = dict(sems=outs[:2 * n], srcs=outs[2 * n:3 * n], lands=outs[3 * n:4 * n], rows=rows,
                  copies=len(relations))
    return handle, outs[-1]


def early_exchange_wait(handle, after, name):
    n = len(handle["srcs"])
    rows = handle["rows"]
    after = list(after) if isinstance(after, (list, tuple)) else [after]

    def body(*refs):
        src_refs, land_refs = refs[:n], refs[n:2 * n]
        send_sems, recv_sems = refs[2 * n:3 * n], refs[3 * n:4 * n]
        x, y, c = _mesh_pos()
        for a in range(n):
            span = pl.ds(0, handle["copies"] * rows[a])
            all_copies = pltpu.make_async_remote_copy(
                src_ref=land_refs[a].at[span, :], dst_ref=land_refs[a].at[span, :],
                send_sem=send_sems[a], recv_sem=recv_sems[a],
                device_id=(x, y, c), device_id_type=pl.DeviceIdType.MESH)
            all_copies.wait_send()
            all_copies.wait_recv()

    hbm = lambda a: pltpu.HBM(a.shape, a.dtype)
    ins = list(handle["srcs"]) + list(handle["lands"])
    outs = pl.pallas_call(
        body, name=name,
        out_shape=[hbm(a) for a in ins],
        in_specs=[_HBM] * (2 * n) + [_SEM] * (2 * n) + [pl.BlockSpec(memory_space=pl.ANY)] * len(after),
        out_specs=[_HBM] * (2 * n),
        input_output_aliases={i: i for i in range(2 * n)},
        compiler_params=pltpu.CompilerParams(has_side_effects=_EFFECT),
    )(*ins, *handle["sems"], *after)
    return outs[:n], outs[n:]


def pass_on_to_sibling(lands, rows, name):
    n = len(lands)

    def body(*refs):
        land_refs = refs[n:2 * n]
        send_sems, recv_sems = refs[2 * n:]
        x, y, c = _mesh_pos()
        copies = []
        for a in range(n):
            for k, (dx, dy) in enumerate(_OTHER_CHIPS):
                slot = 4 * _flip(x, dx) + 2 * _flip(y, dy) + c
                blk = land_refs[a].at[pl.ds(slot * rows[a], rows[a]), :]
                copies.append(pltpu.make_async_remote_copy(
                    src_ref=blk, dst_ref=blk, send_sem=send_sems.at[a, k], recv_sem=recv_sems.at[a, k],
                    device_id=(x, y, 1 - c), device_id_type=pl.DeviceIdType.MESH))
        for cp in copies:
            cp.start()
        for cp in copies:
            cp.wait_recv()
        for cp in copies:
            cp.wait_send()

    any_spec = pl.BlockSpec(memory_space=pl.ANY)
    return pl.pallas_call(
        body, name=name,
        out_shape=[jax.ShapeDtypeStruct(a.shape, a.dtype) for a in lands],
        in_specs=[any_spec] * n, out_specs=[any_spec] * n,
        input_output_aliases={i: i for i in range(n)},
        scratch_shapes=[pltpu.SemaphoreType.DMA((n, len(_OTHER_CHIPS))), pltpu.SemaphoreType.DMA((n, len(_OTHER_CHIPS)))],
    )(*lands)


def _adamw_update(w, g, m, v):
    mn = ADAM_B1 * m + (1.0 - ADAM_B1) * g
    vn = ADAM_B2 * v + (1.0 - ADAM_B2) * jnp.square(g)
    m_hat = mn / (1.0 - ADAM_B1 ** ADAM_STEP)
    v_hat = vn / (1.0 - ADAM_B2 ** ADAM_STEP)
    return -ADAM_LR * (m_hat / (jnp.sqrt(v_hat) + ADAM_EPS) + ADAM_WD * w), mn, vn


def slot_sum8_adamw(src, land, w, m, v, tr, name):
    rows, cols = land.shape[0] // N_DEV, land.shape[1]
    x, y, c = _mesh_pos()
    me = (4 * x + 2 * y + c).astype(jnp.int32).reshape(1)

    def body(me_ref, src_ref, land_ref, w_ref, m_ref, v_ref, g_ref, d_ref, mo_ref, vo_ref):
        acc = None
        for d in range(N_DEV):
            term = jnp.where(d == me_ref[0], src_ref[0], land_ref[d]).astype(F32)
            acc = term if acc is None else acc + term
        g_ref[...] = acc
        d_ref[...], mo_ref[...], vo_ref[...] = _adamw_update(w_ref[...], acc, m_ref[...], v_ref[...])

    tile = pl.BlockSpec((tr, cols), lambda i, w: (i, 0))
    return pl.pallas_call(
        body, name=name,
        grid_spec=pltpu.PrefetchScalarGridSpec(
            num_scalar_prefetch=1, grid=(rows // tr,),
            in_specs=[pl.BlockSpec((1, tr, cols), lambda i, w: (w[0], i, 0)),
                      pl.BlockSpec((N_DEV, tr, cols), lambda i, w: (0, i, 0)), tile, tile, tile],
            out_specs=[tile] * 4),
        out_shape=[jax.ShapeDtypeStruct((rows, cols), F32)] * 4,
        compiler_params=_params(("arbitrary",)),
    )(me, src.reshape(N_DEV, rows, cols), land.reshape(N_DEV, rows, cols), w, m, v)


def w_in_grad_sum(groups, name):
    rows = groups[0][0].shape[0] // N_DEV
    runs, pos = [], 0
    for start, width in _pack_pieces():
        runs.append((start, width, pos))
        pos += width
    n = len(groups)

    def body(*refs):
        src_refs, land_refs = refs[0:2 * n:2], refs[1:2 * n:2]
        g_ref = refs[2 * n]
        own_bufs, land_bufs = refs[2 * n + 1:3 * n + 1], refs[3 * n + 1:4 * n + 1]
        stage, load_sems, store_sems = refs[4 * n + 1:]
        x, y, c = _mesh_pos()
        me = 4 * x + 2 * y + c
        loads = []
        for k in range(n):
            pair = [pltpu.make_async_copy(src_refs[k].at[pl.ds(me * rows, rows), :], own_bufs[k], load_sems.at[k, 0]),
                    pltpu.make_async_copy(land_refs[k], land_bufs[k], load_sems.at[k, 1])]
            for cp in pair:
                cp.start()
            loads.append(pair)
        for k, (_, _, segments) in enumerate(groups):
            for cp in loads[k]:
                cp.wait()
            for first, packed, width in segments:
                for off in range(0, width, LANES):
                    cols = slice(first + off, first + off + LANES)
                    acc = None
                    for d in range(N_DEV):
                        term = jnp.where(d == me, own_bufs[k][:, cols], land_bufs[k][d * rows:(d + 1) * rows, cols])
                        acc = term.astype(F32) if acc is None else acc + term.astype(F32)
                    stage[packed + off:packed + off + LANES, :] = acc.T
        g_rows = g_ref.reshape(IN_W, LANES)
        stores = [pltpu.make_async_copy(stage.at[pl.ds(p, width), :], g_rows.at[pl.ds(start, width), :], store_sems.at[r])
                  for r, (start, width, p) in enumerate(runs)]
        for r, cp in enumerate(stores):
            cp.start(priority=r % 2)
        for cp in stores:
            cp.wait()

    any_spec = pl.BlockSpec(memory_space=pl.ANY)
    operands = [a for src, land, _ in groups for a in (src, land)]
    return pl.pallas_call(
        body, name=name,
        in_specs=[any_spec] * (2 * n), out_specs=any_spec,
        out_shape=jax.ShapeDtypeStruct((IN_W, 1, LANES), F32),
        scratch_shapes=[pltpu.VMEM((rows, src.shape[1]), src.dtype) for src, _, _ in groups]
        + [pltpu.VMEM(land.shape, land.dtype) for _, land, _ in groups]
        + [pltpu.VMEM((PW, LANES), F32), pltpu.SemaphoreType.DMA((n, 2)), pltpu.SemaphoreType.DMA((len(runs),))],
        compiler_params=_params(),
    )(*operands)


def mm_tn_multi(a_t, bs, tt, name, out_dtype=F32):
    K, T = a_t.shape
    widths = [b.shape[1] for b in bs]
    steps = T // tt

    def body(a_ref, *rest):
        b_refs, o_ref, acc = rest[:-2], rest[-2], rest[-1]

        @pl.when(pl.program_id(0) == 0)
        def _():
            acc[...] = jnp.zeros(acc.shape, F32)

        av = a_ref[...]
        col = 0
        for b_ref, w in zip(b_refs, widths):
            acc[:, col:col + w] += jnp.dot(av, b_ref[...], preferred_element_type=F32)
            col += w

        @pl.when(pl.program_id(0) == steps - 1)
        def _():
            o_ref[...] = acc[...].astype(out_dtype)

    return pl.pallas_call(
        body, name=name, grid=(steps,),
        in_specs=[pl.BlockSpec((K, tt), lambda t: (0, t))] + [pl.BlockSpec((tt, w), lambda t: (t, 0)) for w in widths],
        out_specs=pl.BlockSpec((K, sum(widths)), lambda t: (0, 0)),
        out_shape=jax.ShapeDtypeStruct((K, sum(widths)), out_dtype),
        scratch_shapes=[pltpu.VMEM((K, sum(widths)), F32)],
        compiler_params=_params(("arbitrary",)),
    )(a_t, *bs)


def rms_fwd(x, g, tm, name, with_transpose=False, token=None):
    M, K = x.shape
    extra = [] if token is None else [token]

    def body(x_ref, g_ref, *rest):
        o_ref = rest[len(extra)]
        xv = x_ref[...]
        r = lax.rsqrt(jnp.mean(xv * xv, axis=-1, keepdims=True) + RMS_EPS)
        h = ((xv * r) * g_ref[...]).astype(BF16)
        o_ref[...] = h
        if with_transpose:
            rest[len(extra) + 1][...] = h.T

    out_specs = [pl.BlockSpec((tm, K), lambda i: (i, 0))]
    out_shape = [jax.ShapeDtypeStruct((M, K), BF16)]
    if with_transpose:
        out_specs.append(pl.BlockSpec((K, tm), lambda i: (0, i)))
        out_shape.append(jax.ShapeDtypeStruct((K, M), BF16))
    outs = pl.pallas_call(
        body, name=name, grid=(M // tm,),
        in_specs=[pl.BlockSpec((tm, K), lambda i: (i, 0)), pl.BlockSpec((1, K), lambda i: (0, 0))]
        + [pl.BlockSpec(t.shape, lambda i: (0, 0)) for t in extra],
        out_specs=out_specs, out_shape=out_shape,
        compiler_params=_params(("arbitrary",)),
    )(x, g, *extra)
    return outs if with_transpose else outs[0]


def mm_nn(a, b, tm, tn, name, token=None):
    M, K = a.shape
    N = b.shape[1]
    extra = [] if token is None else [token]

    def body(a_ref, b_ref, *rest):
        rest[-1][...] = jnp.dot(a_ref[...], b_ref[...], preferred_element_type=F32)

    return pl.pallas_call(
        body, name=name, grid=(N // tn, M // tm),
        in_specs=[pl.BlockSpec((tm, K), lambda j, i: (i, 0)), pl.BlockSpec((K, tn), lambda j, i: (0, j))]
        + [pl.BlockSpec(t.shape, lambda j, i: (0, 0)) for t in extra],
        out_specs=pl.BlockSpec((tm, tn), lambda j, i: (i, j)),
        out_shape=jax.ShapeDtypeStruct((M, N), F32),
        compiler_params=_params(("arbitrary", "arbitrary")),
    )(a, b, *extra)


def proj_bwd_rms(pieces, ws, x, g, dres, tm, token, name):
    M, N = x.shape
    residual = [] if dres is None else [dres]

    def body(*refs):
        n = len(pieces)
        p_refs, w_refs = refs[:n], refs[n:n + len(ws)]
        x_ref, g_ref, *dres_ref = refs[n + len(ws):n + len(ws) + 2 + len(residual)]
        dx_ref, dg_ref = refs[-2:]
        dh = None
        for p_ref, (arr, group, col) in zip(p_refs, pieces):
            part = lax.dot_general(p_ref[...], w_refs[group][:, col:col + arr.shape[1]], (((1,), (1,)), ((), ())),
                                   preferred_element_type=F32)
            dh = part if dh is None else dh + part
        xv = x_ref[...]
        r = lax.rsqrt(jnp.mean(xv * xv, axis=-1, keepdims=True) + RMS_EPS)
        xn = xv * r
        dxn = dh * g_ref[...]
        dx = r * (dxn - xn * jnp.mean(dxn * xn, axis=-1, keepdims=True))
        dx_ref[...] = dx + dres_ref[0][...] if residual else dx
        row = lax.broadcasted_iota(jnp.int32, (8, N), 0)
        upd = jnp.where(row == 0, jnp.sum(dh * xn, axis=0, keepdims=True), 0.0)

        @pl.when(pl.program_id(0) == 0)
        def _():
            dg_ref[...] = upd

        @pl.when(pl.program_id(0) != 0)
        def _():
            dg_ref[...] += upd

    row_spec = pl.BlockSpec((tm, N), lambda i: (i, 0))
    return pl.pallas_call(
        body, name=name, grid=(M // tm,),
        in_specs=[pl.BlockSpec((tm, arr.shape[1]), lambda i: (i, 0)) for arr, _, _ in pieces]
        + [pl.BlockSpec(w.shape, lambda i: (0, 0), pipeline_mode=pl.Buffered(1)) for w in ws]
        + [row_spec, pl.BlockSpec((1, N), lambda i: (0, 0))] + [row_spec] * len(residual)
        + [pl.BlockSpec(token.shape, lambda i: (0, 0))],
        out_specs=[row_spec, pl.BlockSpec((8, N), lambda i: (0, 0))],
        out_shape=[jax.ShapeDtypeStruct((M, N), F32), jax.ShapeDtypeStruct((8, N), F32)],
        compiler_params=_params(("arbitrary",)),
    )(*[arr for arr, _, _ in pieces], *ws, x, g, *residual, token)


def _log_sigmoid(z):
    return jnp.minimum(z, 0.0) - jnp.log(1.0 + jnp.exp(-jnp.abs(z)))


def _tri(n, lower):
    r = lax.broadcasted_iota(jnp.int32, (n, n), 0)
    c = lax.broadcasted_iota(jnp.int32, (n, n), 1)
    return jnp.where((r >= c) if lower else (r <= c), 1.0, 0.0).astype(F32)


def fox_gate(proj3, b_pad):
    B, S, _ = proj3.shape
    nblk = S // TK

    def body(f_ref, b_ref, o_ref):
        tri = _tri(TK, True)
        carry = jnp.zeros((1, LANES), F32)
        for n in range(nblk):
            z = f_ref[0, n * TK:(n + 1) * TK, :] + b_ref[...]
            logf = _log_sigmoid(z)
            cs = jnp.dot(tri, logf, preferred_element_type=F32, precision=lax.Precision.HIGHEST) + carry
            carry = cs[TK - 1:TK, :]
            o_ref[0, n * TK:(n + 1) * TK, :] = -cs

    return pl.pallas_call(
        body, name="fox_gate", grid=(B,),
        in_specs=[pl.BlockSpec((1, S, LANES), lambda b: (b, 0, A_FLOG // LANES)),
                  pl.BlockSpec((1, LANES), lambda b: (0, 0))],
        out_specs=pl.BlockSpec((1, S, LANES), lambda b: (b, 0, 0)),
        out_shape=jax.ShapeDtypeStruct((B, S, LANES), F32),
        compiler_params=_params(("arbitrary",)),
    )(proj3, b_pad)


def fox_gate_bwd(drow, dneg, proj3, b_pad):
    B, S, _ = proj3.shape
    nblk = S // TK

    def body(d_ref, r_ref, f_ref, b_ref, o_ref, db_ref):
        tri = _tri(TK, False)
        lane = lax.broadcasted_iota(jnp.int32, (TK, LANES), 1)
        carry = jnp.zeros((1, LANES), F32)
        dbsum = jnp.zeros((1, LANES), F32)
        for n in reversed(range(nblk)):
            dk_side = None
            for hp in range(FOX_HEADS // 2):
                two = jnp.where(lane < 2, r_ref[0, n * TK:(n + 1) * TK, hp * LANES:(hp + 1) * LANES], 0.0)
                two = pltpu.roll(two, 2 * hp, 1) if hp else two
                dk_side = two if dk_side is None else dk_side + two
            dc = jnp.where(lane < FOX_HEADS, d_ref[0, :, n * TK:(n + 1) * TK].T - dk_side, 0.0)
            rs = jnp.dot(tri, dc, preferred_element_type=F32, precision=lax.Precision.HIGHEST) + carry
            carry = rs[0:1, :]
            z = f_ref[0, n * TK:(n + 1) * TK, :] + b_ref[...]
            dz = rs * (1.0 / (1.0 + jnp.exp(z)))
            o_ref[0, n * TK:(n + 1) * TK, :] = dz.astype(BF16)
            dbsum = dbsum + jnp.sum(dz, axis=0, keepdims=True)
        row = lax.broadcasted_iota(jnp.int32, (8, LANES), 0)
        upd = jnp.where(row == 0, dbsum, 0.0)

        @pl.when(pl.program_id(0) == 0)
        def _():
            db_ref[...] = upd

        @pl.when(pl.program_id(0) != 0)
        def _():
            db_ref[...] += upd

    return pl.pallas_call(
        body, name="fox_gate_bwd", grid=(B,),
        in_specs=[pl.BlockSpec((1, LANES, S), lambda b: (b, 0, 0)),
                  pl.BlockSpec((1, S, FOX_W), lambda b: (b, 0, 0)),
                  pl.BlockSpec((1, S, LANES), lambda b: (b, 0, A_FLOG // LANES)),
                  pl.BlockSpec((1, LANES), lambda b: (0, 0))],
        out_specs=[pl.BlockSpec((1, S, LANES), lambda b: (b, 0, 0)), pl.BlockSpec((8, LANES), lambda b: (0, 0))],
        out_shape=[jax.ShapeDtypeStruct((B, S, LANES), BF16), jax.ShapeDtypeStruct((8, LANES), F32)],
        compiler_params=_params(("arbitrary",)),
    )(drow, dneg, proj3, b_pad)


def _rope_tables(S):
    half = ROPE_DIM // 2
    f32 = np.float32
    pos = np.arange(S, dtype=f32)
    inv_freq = f32(1.0) / np.power(f32(ROPE_THETA), np.arange(0, ROPE_DIM, 2, dtype=f32) / f32(ROPE_DIM)).astype(f32)
    ang = (pos[:, None] * inv_freq[None, :]).astype(f32).astype(np.float64)
    cos, sin = np.cos(ang).astype(f32), np.sin(ang).astype(f32)
    one = np.ones((S, HEAD_DIM - ROPE_DIM), f32)
    zero = np.zeros((S, HEAD_DIM - ROPE_DIM), f32)
    zh = np.zeros((S, half), f32)
    c = np.concatenate([cos, cos, one], axis=1)
    s1 = np.concatenate([-sin, zh, zero], axis=1)
    s2 = np.concatenate([zh, sin, zero], axis=1)
    return tuple(jnp.asarray(np.concatenate([t, t], axis=1)) for t in (c, s1, s2))


_HALF_ROPE = ROPE_DIM // 2


def _rope(t, c, s1, s2):
    return t * c + pltpu.roll(t, LANES - _HALF_ROPE, 1) * s1 + pltpu.roll(t, _HALF_ROPE, 1) * s2


def _rope_bwd(d, c, s1, s2):
    return d * c + pltpu.roll(d * s1, _HALF_ROPE, 1) + pltpu.roll(d * s2, LANES - _HALF_ROPE, 1)


def _scale_parts(scale):
    m, _ = math.frexp(scale)
    return (scale, None) if m == 0.5 else (None, scale)


def _log_masks(S, kind):
    nd = 1 if kind == "causal" else S // TQ
    a = np.arange(TQ)[:, None]
    b = np.arange(TK)[None, :]
    out = np.zeros((nd, TQ, TK), np.float32)
    for d in range(nd):
        delta = d * TQ + a - b
        if kind == "causal":
            m = (delta >= 0).astype(np.float64)
        else:
            m = sum(((delta >= 0) & (delta % dil == 0) & (delta <= w)).astype(np.float64) for w, dil in DILATIONS)
        out[d] = np.where(m > 0, np.log(np.maximum(m, 1.0)), NEG_INF)
    return jnp.asarray(out)


def _attn_setup(kind):
    pair = kind != "mem"
    e_dim = HEAD_DIM if pair else MEM_HEAD_DIM
    q_fold, s_scale = _scale_parts(1.0 / math.sqrt(e_dim))
    return dict(pair=pair, col0={"fox": A_FOX, "dil": B_DIL, "mem": B_MQ}[kind],
                n_blocks=FOX_HEADS // 2 if pair else MEM_HEADS, q_fold=q_fold, s_scale=s_scale,
                nh=2 if pair else 1)


def _cat(parts, axis):
    return parts[0] if len(parts) == 1 else jnp.concatenate(parts, axis=axis)


def _log_masks_t(S, kind):
    return jnp.swapaxes(_log_masks(S, kind), 1, 2)


def _head_rows(hh, pair):
    row = lax.broadcasted_iota(jnp.int32, (LANES, 1), 0)
    if not pair:
        return row >= 0
    return (row >= HEAD_DIM * hh) & (row < HEAD_DIM * (hh + 1))


def _attn_t_inputs(kind, src, S, negc_cols, mask, rope, kv):
    cfg = _attn_setup(kind)
    col0 = cfg["col0"]
    ins, in_specs = [], []
    if cfg["pair"]:
        ins.append(src)
        in_specs.append(pl.BlockSpec((1, S, PAIR_W), lambda b, h: (b, 0, col0 // PAIR_W + h)))
    else:
        ins += [src, kv, kv]
        in_specs += [pl.BlockSpec((1, S, LANES), lambda b, h: (b, 0, col0 // LANES + h)),
                     pl.BlockSpec((1, MEM_LEN, LANES), lambda b, h: (b, 0, h)),
                     pl.BlockSpec((1, MEM_LEN, LANES), lambda b, h: (b, 0, MEM_HEADS + h))]
    if negc_cols is not None:
        ins.append(negc_cols)
        in_specs.append(pl.BlockSpec((1, S, LANES), lambda b, h: (b, 0, 0)))
    if mask is not None:
        ins.append(mask)
        in_specs.append(pl.BlockSpec(mask.shape, lambda b, h: (0, 0, 0)))
    if rope is not None:
        ins += list(rope)
        in_specs += [pl.BlockSpec((S, LANES), lambda b, h: (0, 0))] * 3
    return ins, in_specs


def _attn_t_prep(cfg, refs, S, Sk, *, qT2s, ks, vs=None, vTs=None, kTs=None, nb=None):
    pair, nh = cfg["pair"], cfg["nh"]
    lane = lax.broadcasted_iota(jnp.int32, (1, LANES), 1)
    rope_refs = refs["rope"]

    def prep_q(n):
        rows = slice(n * TQ, (n + 1) * TQ)
        q = refs["load_q"](rows)
        if rope_refs is not None:
            q = _rope(q, *[t[rows, :] for t in rope_refs])
        if cfg["q_fold"] is not None:
            q = q * cfg["q_fold"]
        qtb = q.astype(BF16).T
        for hh in range(nh):
            qT2s[n, :, hh * TQ:(hh + 1) * TQ] = jnp.where(_head_rows(hh, pair), qtb, jnp.zeros_like(qtb))

    def prep_kv(n):
        rows = slice(n * TK, (n + 1) * TK)
        k, v = refs["load_kv"](rows)
        if rope_refs is not None:
            k = _rope(k, *[t[rows, :] for t in rope_refs])
        kb = k.astype(BF16)
        vb = v.astype(BF16)
        ks[rows, :] = kb
        if vs is not None:
            vs[rows, :] = vb
        if vTs is not None:
            vTs[n] = vb.T
        if kTs is not None:
            kTs[n] = kb.T
        if nb is not None:
            blk = refs["negc"][0, rows, :]
            for hh in range(nh):
                h = 2 * refs["block"] + hh
                col = jnp.sum(jnp.where(lane == h, blk, 0.0), axis=1, keepdims=True)
                nb[hh, rows, :] = jnp.broadcast_to(col, (TK, LANES))

    for n in range(S // TQ):
        prep_q(n)
    for n in range(Sk // TK):
        prep_kv(n)


def _raw_scores_t(cfg, k, qT2):
    sT = jnp.dot(k, qT2, preferred_element_type=F32)
    if cfg["s_scale"] is not None:
        sT = sT * cfg["s_scale"]
    return sT


def _bias_mask_t(cfg, sT, nb, mask_ref, kc, midx):
    nh = cfg["nh"]
    if nb is None and midx is None:
        return sT
    parts = []
    for hh in range(nh):
        t = sT[:, hh * TQ:(hh + 1) * TQ]
        if nb is not None:
            t = t + jnp.concatenate([nb[hh, kc, :]] * (TQ // LANES), axis=1)
        if midx is not None:
            t = t + mask_ref[midx]
        parts.append(t)
    return _cat(parts, 1)


def _tile_pairs(kind, nq, nk):
    if kind == "mem":
        return [(i, j) for i in range(nq) for j in range(nk)], (lambda i, j: None)
    pairs = [(i, j) for i in range(nq) for j in range(i + 1)]
    if kind == "fox":
        return pairs, (lambda i, j: 0 if j == i else None)
    return pairs, (lambda i, j: i - j)


def attn_fwd(kind, src, S, *, negc_cols=None, mask=None, rope=None, kv=None):
    B = src.shape[0]
    cfg = _attn_setup(kind)
    pair, nh = cfg["pair"], cfg["nh"]
    Sk = S if pair else MEM_LEN
    has_bias, has_rope = negc_cols is not None, rope is not None
    R = nh * TQ
    nq, nk = S // TQ, Sk // TK
    pairs, mask_index = _tile_pairs(kind, nq, nk)

    def body(*refs):
        refs = list(refs)
        if pair:
            qkv_ref = refs.pop(0)
            load_q = lambda rows: qkv_ref[0, rows, 0:LANES]
            load_kv = lambda rows: (qkv_ref[0, rows, LANES:2 * LANES], qkv_ref[0, rows, 2 * LANES:3 * LANES])
        else:
            q_ref, k_ref, v_ref = refs.pop(0), refs.pop(0), refs.pop(0)
            load_q = lambda rows: q_ref[0, rows, :]
            load_kv = lambda rows: (k_ref[0, rows, :], v_ref[0, rows, :])
        negc_ref = refs.pop(0) if has_bias else None
        mask_ref = refs.pop(0) if mask is not None else None
        rope_refs = [refs.pop(0) for _ in range(3)] if has_rope else None
        o_ref, lse_ref, qT2s, ks, vTs, s_a, s_b, p_a, p_b = refs[:9]
        nb = refs[9] if has_bias else None
        _attn_t_prep(cfg, dict(load_q=load_q, load_kv=load_kv, rope=rope_refs, negc=negc_ref,
                               block=pl.program_id(1)), S, Sk, qT2s=qT2s, ks=ks, vTs=vTs, nb=nb)

        def cols(j):
            return slice(j * TK, (j + 1) * TK)

        def scores(i, j):
            return _raw_scores_t(cfg, ks[cols(j), :], qT2s[i])

        def finish(i, m, l, accT):
            oT2 = accT / l
            oT = jnp.where(_head_rows(0, True), oT2[:, 0:TQ], oT2[:, TQ:2 * TQ]) if pair else oT2
            o_ref[0, i * TQ:(i + 1) * TQ, :] = oT.T
            lse_ref[0, 0, i:i + 1, :] = m + jnp.log(l)

        s_bufs, p_bufs = (s_a, s_b), (p_a, p_b)
        s_bufs[0][...] = scores(*pairs[0])
        m = l = accT = None
        for t, (i, j) in enumerate(pairs):
            cur, oth = t % 2, 1 - t % 2
            if t > 0:
                i_prev, j_prev = pairs[t - 1]
                pv = jnp.dot(vTs[j_prev], p_bufs[oth][...], preferred_element_type=F32)
                acc_full = pv if accT is None else accT + pv
            if t + 1 < len(pairs):
                s_bufs[oth][...] = scores(*pairs[t + 1])
            first = j == 0
            if first and t > 0:
                finish(i_prev, m, l, acc_full)
            sT = _bias_mask_t(cfg, s_bufs[cur][...], nb, mask_ref, cols(j), mask_index(i, j))
            m_tile = jnp.max(sT, axis=0, keepdims=True)
            m_new = m_tile if first else jnp.maximum(m, m_tile)
            p = jnp.exp(sT - m_new)
            p_bufs[cur][...] = p.astype(BF16)
            if first:
                l, accT = jnp.sum(p, axis=0, keepdims=True), None
            else:
                alpha = jnp.exp(m - m_new)
                l, accT = alpha * l + jnp.sum(p, axis=0, keepdims=True), acc_full * alpha
            m = m_new
        i_last, j_last = pairs[-1]
        pv = jnp.dot(vTs[j_last], p_bufs[(len(pairs) - 1) % 2][...], preferred_element_type=F32)
        finish(i_last, m, l, pv if accT is None else accT + pv)

    ins, in_specs = _attn_t_inputs(kind, src, S, negc_cols, mask, rope, kv)
    W = cfg["n_blocks"] * LANES
    scratch = [pltpu.VMEM((nq, LANES, R), BF16), pltpu.VMEM((Sk, LANES), BF16), pltpu.VMEM((nk, LANES, TK), BF16),
               pltpu.VMEM((TK, R), F32), pltpu.VMEM((TK, R), F32), pltpu.VMEM((TK, R), BF16), pltpu.VMEM((TK, R), BF16)]
    if has_bias:
        scratch.append(pltpu.VMEM((nh, Sk, LANES), F32))
    return pl.pallas_call(
        body, name=kind + "_attn_fwd", grid=(B, cfg["n_blocks"]),
        in_specs=in_specs,
        out_specs=[pl.BlockSpec((1, S, LANES), lambda b, h: (b, 0, h)),
                   pl.BlockSpec((1, 1, nq, R), lambda b, h: (b, h, 0, 0))],
        out_shape=[jax.ShapeDtypeStruct((B, S, W), F32), jax.ShapeDtypeStruct((B, cfg["n_blocks"], nq, R), F32)],
        scratch_shapes=scratch,
        compiler_params=_params(("arbitrary", "arbitrary")),
    )(*ins)


def attn_bwd(kind, src, do, o, lse, S, *, negc_cols=None, mask=None, rope=None, kv=None, token=None):
    B = src.shape[0]
    cfg = _attn_setup(kind)
    pair, nh, s_scale, q_fold = cfg["pair"], cfg["nh"], cfg["s_scale"], cfg["q_fold"]
    Sk = S if pair else MEM_LEN
    has_bias, has_rope = negc_cols is not None, rope is not None
    R = nh * TQ
    nq, nk = S // TQ, Sk // TK
    pairs, mask_index = _tile_pairs(kind, nq, nk)

    def body(*refs):
        refs = list(refs)
        if pair:
            qkv_ref = refs.pop(0)
            load_q = lambda rows: qkv_ref[0, rows, 0:LANES]
            load_kv = lambda rows: (qkv_ref[0, rows, LANES:2 * LANES], qkv_ref[0, rows, 2 * LANES:3 * LANES])
        else:
            q_ref, k_ref, v_ref = refs.pop(0), refs.pop(0), refs.pop(0)
            load_q = lambda rows: q_ref[0, rows, :]
            load_kv = lambda rows: (k_ref[0, rows, :], v_ref[0, rows, :])
        negc_ref = refs.pop(0) if has_bias else None
        mask_ref = refs.pop(0) if mask is not None else None
        rope_refs = [refs.pop(0) for _ in range(3)] if has_rope else None
        do_ref, o_ref, lse_ref = refs.pop(0), refs.pop(0), refs.pop(0)
        if token is not None:
            refs.pop(0)
        if pair:
            dqkv_ref = refs.pop(0)
            dneg_ref = refs.pop(0) if has_bias else None
            drow_ref = refs.pop(0) if has_bias else None
        else:
            dq_ref, dk_ref, dv_ref = refs.pop(0), refs.pop(0), refs.pop(0)
        qT2s, ks, vs, kTs, doT2s, delta_s, dk_acc, dv_acc = refs[:8]
        bufs_a, bufs_b = refs[8:12], refs[12:16]
        nb, dneg_acc = (refs[16], refs[17]) if has_bias else (None, None)
        lane = lax.broadcasted_iota(jnp.int32, (1, LANES), 1)
        _attn_t_prep(cfg, dict(load_q=load_q, load_kv=load_kv, rope=rope_refs, negc=negc_ref,
                               block=pl.program_id(1)), S, Sk,
                     qT2s=qT2s, ks=ks, vs=vs, kTs=kTs, nb=nb)

        def prep_do(n):
            rows = slice(n * TQ, (n + 1) * TQ)
            doT = do_ref[0, rows, :].astype(BF16).astype(F32).T
            prodT = doT * o_ref[0, rows, :].T
            doTb = doT.astype(BF16)
            for hh in range(nh):
                hm = _head_rows(hh, pair)
                doT2s[n, :, hh * TQ:(hh + 1) * TQ] = jnp.where(hm, doTb, jnp.zeros_like(doTb))
                delta_s[n:n + 1, hh * TQ:(hh + 1) * TQ] = jnp.sum(jnp.where(hm, prodT, 0.0), axis=0, keepdims=True)

        for n in range(nq):
            prep_do(n)
        dk_acc[...] = jnp.zeros(dk_acc.shape, F32)
        dv_acc[...] = jnp.zeros(dv_acc.shape, F32)
        if has_bias:
            dneg_acc[...] = jnp.zeros(dneg_acc.shape, F32)

        def cols(j):
            return slice(j * TK, (j + 1) * TK)

        nt_dims = (((1,), (1,)), ((), ()))

        def first_products(i, j, bufs):
            bufs[0][...] = _raw_scores_t(cfg, ks[cols(j), :], qT2s[i])
            bufs[1][...] = jnp.dot(vs[cols(j), :], doT2s[i], preferred_element_type=F32)

        def last_products(i, j, bufs, dqT2):
            dv_acc[j] += lax.dot_general(doT2s[i], bufs[2][...], nt_dims, preferred_element_type=F32)
            dk_acc[j] += lax.dot_general(qT2s[i], bufs[3][...], nt_dims, preferred_element_type=F32)
            dq = jnp.dot(kTs[j], bufs[3][...], preferred_element_type=F32)
            return dq if dqT2 is None else dqT2 + dq

        def finish_q(i, dqT2, drow):
            rows = slice(i * TQ, (i + 1) * TQ)
            dqT = jnp.where(_head_rows(0, True), dqT2[:, 0:TQ], dqT2[:, TQ:2 * TQ]) if pair else dqT2
            dq = dqT.T
            if q_fold is not None:
                dq = dq * q_fold
            if has_rope:
                dq = _rope_bwd(dq, *[t[rows, :] for t in rope_refs])
            if pair:
                dqkv_ref[0, rows, 0:LANES] = dq.astype(BF16)
            else:
                dq_ref[0, rows, :] = dq.astype(BF16)
            if has_bias:
                drow_ref[0, 0, i:i + 1, :] = drow

        bufs = (bufs_a, bufs_b)
        first_products(*pairs[0], bufs[0])
        dqT2 = drow = None
        for t, (i, j) in enumerate(pairs):
            cur, oth = bufs[t % 2], bufs[1 - t % 2]
            first = j == 0
            if first and t > 0:
                i_prev, j_prev = pairs[t - 1]
                finish_q(i_prev, last_products(i_prev, j_prev, oth, dqT2), drow)
                dqT2 = drow = None
            sT = _bias_mask_t(cfg, cur[0][...], nb, mask_ref, cols(j), mask_index(i, j))
            pT = jnp.exp(sT - lse_ref[0, 0, i:i + 1, :])
            dsT = pT * (cur[1][...] - delta_s[i:i + 1, :])
            if has_bias:
                tile_rows = jnp.sum(dsT, axis=0, keepdims=True)
                drow = tile_rows if drow is None else drow + tile_rows
                for hh in range(nh):
                    part = dsT[:, hh * TQ:hh * TQ + LANES]
                    for u in range(1, TQ // LANES):
                        part = part + dsT[:, hh * TQ + u * LANES:hh * TQ + (u + 1) * LANES]
                    dneg_acc[hh, cols(j), :] += part
            if s_scale is not None:
                dsT = dsT * s_scale
            cur[2][...] = pT.astype(BF16)
            cur[3][...] = dsT.astype(BF16)
            if not first:
                dqT2 = last_products(*pairs[t - 1], oth, dqT2)
            if t + 1 < len(pairs):
                first_products(*pairs[t + 1], oth)
        i_last, j_last = pairs[-1]
        finish_q(i_last, last_products(i_last, j_last, bufs[(len(pairs) - 1) % 2], dqT2), drow)

        for n in range(nk):
            rows = slice(n * TK, (n + 1) * TK)
            dk = dk_acc[n].T
            dv = dv_acc[n].T
            if has_rope:
                dk = _rope_bwd(dk, *[t[rows, :] for t in rope_refs])
            if pair:
                dqkv_ref[0, rows, LANES:2 * LANES] = dk.astype(BF16)
                dqkv_ref[0, rows, 2 * LANES:3 * LANES] = dv.astype(BF16)
            else:
                dk_ref[0, rows, :] = dk.astype(BF16)
                dv_ref[0, rows, :] = dv.astype(BF16)
            if has_bias:
                x0 = jnp.sum(dneg_acc[0, rows, :], axis=1, keepdims=True)
                x1 = jnp.sum(dneg_acc[1, rows, :], axis=1, keepdims=True)
                dneg_ref[0, rows, :] = jnp.where(lane == 0, x0, jnp.where(lane == 1, x1, 0.0))

    ins, in_specs = _attn_t_inputs(kind, src, S, negc_cols, mask, rope, kv)
    row_spec = pl.BlockSpec((1, S, LANES), lambda b, h: (b, 0, h))
    vec_spec = pl.BlockSpec((1, 1, nq, R), lambda b, h: (b, h, 0, 0))
    ins += [do, o, lse]
    in_specs += [row_spec, row_spec, vec_spec]
    if token is not None:
        ins.append(token)
        in_specs.append(pl.BlockSpec(token.shape, lambda b, h: (0, 0)))
    W = cfg["n_blocks"] * LANES
    if pair:
        out_specs = [pl.BlockSpec((1, S, PAIR_W), lambda b, h: (b, 0, h))]
        out_shape = [jax.ShapeDtypeStruct((B, S, 3 * W), BF16)]
        if has_bias:
            out_specs += [row_spec, vec_spec]
            out_shape += [jax.ShapeDtypeStruct((B, S, W), F32), jax.ShapeDtypeStruct((B, cfg["n_blocks"], nq, R), F32)]
    else:
        kv_spec = pl.BlockSpec((1, MEM_LEN, LANES), lambda b, h: (b, 0, h))
        out_specs = [row_spec, kv_spec, kv_spec]
        out_shape = [jax.ShapeDtypeStruct((B, S, W), BF16)] + [jax.ShapeDtypeStruct((B, MEM_LEN, W), BF16)] * 2
    scratch = [pltpu.VMEM((nq, LANES, R), BF16), pltpu.VMEM((Sk, LANES), BF16),
               pltpu.VMEM((Sk, LANES), BF16), pltpu.VMEM((nk, LANES, TK), BF16), pltpu.VMEM((nq, LANES, R), BF16),
               pltpu.VMEM((nq, R), F32), pltpu.VMEM((nk, LANES, TK), F32), pltpu.VMEM((nk, LANES, TK), F32)]
    pair_bufs = [pltpu.VMEM((TK, R), F32), pltpu.VMEM((TK, R), F32), pltpu.VMEM((TK, R), BF16), pltpu.VMEM((TK, R), BF16)]
    scratch += pair_bufs + pair_bufs
    if has_bias:
        scratch += [pltpu.VMEM((nh, Sk, LANES), F32), pltpu.VMEM((nh, Sk, LANES), F32)]
    return pl.pallas_call(
        body, name=kind + "_attn_bwd", grid=(B, cfg["n_blocks"]),
        in_specs=in_specs, out_specs=out_specs, out_shape=out_shape, scratch_shapes=scratch,
        compiler_params=_params(("arbitrary", "arbitrary")),
    )(*ins)


def _sigmoid(g):
    return 1.0 / (1.0 + jnp.exp(-g))


def out_step(proj, o_fox, o_dil, o_mem, w_out, x, target, gf, tm):
    T = x.shape[0]

    def body(fg_ref, dg_ref, mg_ref, of_ref, od_ref, om_ref, w_ref, x_ref, t_ref, gf_ref,
             dx_ref, dof_ref, dod_ref, dom_ref, dfg_ref, ddg_ref, dmg_ref, gw_ref, sm_ref, gw_acc):
        branches = []
        for g_ref, o_ref in ((fg_ref, of_ref), (dg_ref, od_ref), (mg_ref, om_ref)):
            g = g_ref[...]
            sg = _sigmoid(g)
            o = o_ref[...]
            branches.append((g, sg, o))
        ymix = jnp.concatenate([(o * (g * sg)).astype(BF16) for g, sg, o in branches], axis=1)
        x2 = x_ref[...] + jnp.dot(ymix, w_ref[...], preferred_element_type=F32)
        r = lax.rsqrt(jnp.mean(x2 * x2, axis=-1, keepdims=True) + RMS_EPS)
        yn = x2 * r
        err = yn * gf_ref[...] - t_ref[...]
        loss = 0.5 * jnp.sum(jnp.sum(err * err, axis=-1, keepdims=True) / D_MODEL, axis=0, keepdims=True)
        dyf = err / D_MODEL
        dgf = jnp.sum(dyf * yn, axis=0, keepdims=True)
        dyn = dyf * gf_ref[...]
        dx2 = r * (dyn - yn * jnp.mean(dyn * yn, axis=-1, keepdims=True))
        dx_ref[...] = dx2
        dxb = dx2.astype(BF16)
        dmix = lax.dot_general(dxb, w_ref[...], (((1,), (1,)), ((), ())), preferred_element_type=F32)
        col = 0
        for (g, sg, o), do_ref, dgate_ref in zip(branches, (dof_ref, dod_ref, dom_ref), (dfg_ref, ddg_ref, dmg_ref)):
            d = dmix[:, col:col + g.shape[1]]
            col += g.shape[1]
            do_ref[...] = (d * (g * sg)).astype(BF16)
            dgate_ref[...] = (d * o * (sg * (1.0 + g * (1.0 - sg)))).astype(BF16)
        row = lax.broadcasted_iota(jnp.int32, (8, D_MODEL), 0)
        upd = jnp.where(row == 0, dgf, jnp.where(row == 1, loss, 0.0))

        @pl.when(pl.program_id(0) == 0)
        def _():
            sm_ref[...] = jnp.zeros(sm_ref.shape, F32)
            gw_acc[...] = jnp.zeros(gw_acc.shape, F32)

        sm_ref[...] += upd
        gw_acc[...] += lax.dot_general(ymix, dxb, (((0,), (0,)), ((), ())), preferred_element_type=F32)

        @pl.when(pl.program_id(0) == T // tm - 1)
        def _():
            gw_ref[...] = gw_acc[...].astype(BF16)

    def rows(w, col=0):
        return pl.BlockSpec((tm, w), lambda i: (i, col))

    return pl.pallas_call(
        body, name="out_step", grid=(T // tm,),
        in_specs=[rows(FOX_W, B_FG // FOX_W), rows(DIL_W, B_DG // DIL_W), rows(MEM_W, B_MG // MEM_W),
                  rows(FOX_W), rows(DIL_W), rows(MEM_W),
                  pl.BlockSpec((MIX_W, D_MODEL), lambda i: (0, 0)),
                  rows(D_MODEL), rows(D_MODEL), pl.BlockSpec((1, D_MODEL), lambda i: (0, 0))],
        out_specs=[rows(D_MODEL), rows(FOX_W), rows(DIL_W), rows(MEM_W), rows(FOX_W), rows(DIL_W), rows(MEM_W),
                   pl.BlockSpec((MIX_W, D_MODEL), lambda i: (0, 0)), pl.BlockSpec((8, D_MODEL), lambda i: (0, 0))],
        out_shape=[jax.ShapeDtypeStruct((T, D_MODEL), F32), jax.ShapeDtypeStruct((T, FOX_W), BF16),
                   jax.ShapeDtypeStruct((T, DIL_W), BF16), jax.ShapeDtypeStruct((T, MEM_W), BF16),
                   jax.ShapeDtypeStruct((T, FOX_W), BF16), jax.ShapeDtypeStruct((T, DIL_W), BF16),
                   jax.ShapeDtypeStruct((T, MEM_W), BF16), jax.ShapeDtypeStruct((MIX_W, D_MODEL), BF16),
                   jax.ShapeDtypeStruct((8, D_MODEL), F32)],
        scratch_shapes=[pltpu.VMEM((MIX_W, D_MODEL), F32)],
        compiler_params=_params(("arbitrary",)),
    )(proj, proj, proj, o_fox, o_dil, o_mem, w_out, x, target, gf)


def adamw_columns_first(w, g, m, v, name):
    N = w.shape[0]
    parts, step_rows = 4, 16
    size = -(-N // (parts * step_rows)) * step_rows
    bounds = [(p * size, min((p + 1) * size, N)) for p in range(parts)]

    def body(w_hbm, g_hbm, m_hbm, v_hbm, d_hbm, mo_hbm, vo_hbm, wb, gb, mb, vb, db, mob, vob, load_sems, store_sems):
        ins = ((w_hbm, wb), (g_hbm, gb), (m_hbm, mb), (v_hbm, vb))
        outs = ((d_hbm, db), (mo_hbm, mob), (vo_hbm, vob))

        def rows_copy(src, dst, lo, hi, sem):
            return pltpu.make_async_copy(src.at[pl.ds(lo, hi - lo), :], dst.at[pl.ds(lo, hi - lo), :], sem)

        def update(rows):
            db[rows, :], mob[rows, :], vob[rows, :] = _adamw_update(wb[rows, :], gb[rows, :], mb[rows, :], vb[rows, :])

        loads = [[rows_copy(h.reshape(N, LANES), b, lo, hi, load_sems.at[p, k]) for k, (h, b) in enumerate(ins)]
                 for p, (lo, hi) in enumerate(bounds)]
        for part in loads:
            for cp in part:
                cp.start()
        stores = []
        for p, (lo, hi) in enumerate(bounds):
            for cp in loads[p]:
                cp.wait()
            whole = (hi - lo) // step_rows

            def step(i, _, lo=lo):
                update(pl.ds(pl.multiple_of(lo + i * step_rows, step_rows), step_rows))
                return 0

            lax.fori_loop(0, whole, step, 0, unroll=4)
            if lo + whole * step_rows < hi:
                update(slice(lo + whole * step_rows, hi))
            leaving = [rows_copy(b, h.reshape(N, LANES), lo, hi, store_sems.at[p, k]) for k, (h, b) in enumerate(outs)]
            for cp in leaving:
                cp.start()
            stores += leaving
        for cp in stores:
            cp.wait()

    any_spec = pl.BlockSpec(memory_space=pl.ANY)
    return pl.pallas_call(
        body, name=name,
        in_specs=[any_spec] * 4, out_specs=[any_spec] * 3,
        out_shape=[jax.ShapeDtypeStruct(w.shape, F32)] * 3,
        scratch_shapes=[pltpu.VMEM((N, LANES), F32)] * 7
        + [pltpu.SemaphoreType.DMA((parts, 4)), pltpu.SemaphoreType.DMA((parts, 3))],
        compiler_params=_params(),
    )(w, g, m, v)


def _pad_row(v, width):
    return jnp.concatenate([v, jnp.zeros((1, width - v.shape[1]), v.dtype)], axis=1)


def local_grads(x, mem, norm_g, b_forget, mem_norm_g, final_norm_g, loss_target, first_token, first_weights,
                late_weights, start_exchange):
    B, S, D = x.shape
    T = B * S
    xt = x.reshape(T, D)
    memt = mem.reshape(B * MEM_LEN, D)
    b_pad = _pad_row(b_forget, LANES)

    h, h_t = rms_fwd(xt, norm_g, 512, "rms_x", with_transpose=True, token=first_token)
    mh, mh_t = rms_fwd(memt, mem_norm_g, B * MEM_LEN, "rms_mem", with_transpose=True, token=first_token)
    w_in_a, proj_token = first_weights([h, mh])
    proj_a = mm_nn(h, w_in_a, 1024, PA, "in_proj_a", proj_token)
    proj_a3 = proj_a.reshape(B, S, PA)

    negc = fox_gate(proj_a3, b_pad)
    causal = _log_masks_t(S, "causal")
    dilated = _log_masks_t(S, "dilated")
    rope = _rope_tables(S)

    o_fox, lse_fox = attn_fwd("fox", proj_a3, S, negc_cols=negc, mask=causal)

    w_in_b, w_kv, w_out = late_weights(o_fox)
    proj_b = mm_nn(h, w_in_b, 1024, PB // 2, "in_proj_b")
    proj_b3 = proj_b.reshape(B, S, PB)
    o_dil, lse_dil = attn_fwd("dil", proj_b3, S, mask=dilated, rope=rope)

    mkv = mm_nn(mh, w_kv, B * MEM_LEN, 2 * MEM_W, "mem_kv_proj")
    mkv3 = mkv.reshape(B, MEM_LEN, 2 * MEM_W)
    o_mem, lse_mem = attn_fwd("mem", proj_b3, S, kv=mkv3)

    dx2, do_fox, do_dil, do_mem, dfg, ddg, dmg, g_out, small_out = out_step(
        proj_b, o_fox.reshape(T, FOX_W), o_dil.reshape(T, DIL_W), o_mem.reshape(T, MEM_W), w_out,
        xt, loss_target.reshape(T, D), final_norm_g.reshape(1, D), 256)

    gates = [(dfg, 1, B_FG), (ddg, 1, B_DG), (dmg, 1, B_MG)]
    g_gates = mm_tn_multi(h_t, [piece[0] for piece in gates], 1024, "w_in_grad_gates", BF16)

    dqkv_fox, dneg, drow = attn_bwd("fox", proj_a3, do_fox.reshape(B, S, FOX_W), o_fox, lse_fox, S,
                                    negc_cols=negc, mask=causal)
    drow = drow.reshape(B, FOX_HEADS // 2, S // TQ, 2, TQ).transpose(0, 1, 3, 2, 4).reshape(B, FOX_HEADS, S)
    drow = jnp.pad(drow, ((0, 0), (0, LANES - FOX_HEADS), (0, 0)))
    dflog, db_part = fox_gate_bwd(drow, dneg, proj_a3, b_pad)
    fox = [(dqkv_fox.reshape(T, 3 * FOX_W), 0, A_FOX), (dflog.reshape(T, LANES), 0, A_FLOG)]
    g_fox = mm_tn_multi(h_t, [piece[0] for piece in fox], 1024, "w_in_grad_fox", BF16)
    first, token = start_exchange([g_gates, g_out, g_fox], "early_exchange_a")

    (dqkv_dil,) = attn_bwd("dil", proj_b3, do_dil.reshape(B, S, DIL_W), o_dil, lse_dil, S, mask=dilated, rope=rope,
                           token=token)
    dil = [(dqkv_dil.reshape(T, 3 * DIL_W), 1, B_DIL)]
    g_dil = mm_tn_multi(h_t, [piece[0] for piece in dil], 1024, "w_in_grad_dil", BF16)
    second, token = start_exchange([g_dil], "early_exchange_b")

    dmq, dmk, dmv = attn_bwd("mem", proj_b3, do_mem.reshape(B, S, MEM_W), o_mem, lse_mem, S, kv=mkv3, token=token)
    mq = [(dmq.reshape(T, MEM_W), 1, B_MQ)]
    g_mq = mm_tn_multi(h_t, [piece[0] for piece in mq], 1024, "w_in_grad_mq", BF16)
    dmkv = jnp.concatenate([dmk, dmv], axis=2).reshape(B * MEM_LEN, 2 * MEM_W)
    g_kv = mm_tn_multi(mh_t, [dmkv], B * MEM_LEN, "w_kv_grad", BF16)
    third, token = start_exchange([g_mq, g_kv], "early_exchange_c")

    grad_x, dng = proj_bwd_rms(gates + fox + dil + mq, (w_in_a, w_in_b), xt, norm_g, dx2, 512, token,
                               "in_proj_bwd")
    _, dmng = proj_bwd_rms([(dmkv, 0, 0)], (w_kv,), memt, mem_norm_g, None, B * MEM_LEN, token, "mem_kv_bwd")

    small = jnp.concatenate([dng[0:1], dmng[0:1], small_out[0:1], _pad_row(db_part[0:1], D), small_out[1:2],
                             jnp.zeros((3, D), F32)], axis=0)
    early = [(first, dqkv_dil), (second, dmq), (third, grad_x)]
    return grad_x.reshape(B, S, D), early, small


def kernel(x, mem, norm_g, w_in, b_forget, mem_norm_g, w_mem_kv, w_out, final_norm_g, loss_target, m_norm_g, m_w_in, m_b_forget, m_mem_norm_g, m_w_mem_kv, m_w_out, m_final_norm_g, v_norm_g, v_w_in, v_b_forget, v_mem_norm_g, v_w_mem_kv, v_w_out, v_final_norm_g):
    D = D_MODEL
    shard_a, shard_b = _split_cols(_pack_cols(w_in).astype(BF16).reshape(w_in.shape[1], PW))
    gather_a, first_token = early_exchange_start([shard_a], "first_gather", gather=True,
                                                 relations=_SIBLING_AND_SAME_CORES)
    late_shards = [shard_b, w_mem_kv[0].astype(BF16), w_out[0].astype(BF16)]
    late_lands = own_slots(late_shards, "late_gather_place", after=first_token)
    late = {}

    def first_weights(after):
        _, gathered = early_exchange_wait(gather_a, list(after) + [late_lands[0]], "first_gather_wait")
        (w_in_a,) = pass_on_to_sibling(gathered, gather_a["rows"], "first_gather_pass")
        late["gather"], token = early_exchange_start(
            late_shards, "late_gather", gather=True, after=w_in_a, relations=_SIBLING_AND_SAME_CORES,
            lands=late_lands)
        return w_in_a, token

    def late_weights(after):
        _, gathered = early_exchange_wait(late["gather"], after, "late_gather_wait")
        return pass_on_to_sibling(gathered, late["gather"]["rows"], "late_gather_pass")

    grad_x, early, small = local_grads(
        x, mem, norm_g, b_forget, mem_norm_g, final_norm_g, loss_target, first_token, first_weights, late_weights,
        early_exchange_start)

    (first, after_first), (second, after_second), (third, after_third) = early
    small_handle, small_token = early_exchange_start([jnp.tile(small, (N_DEV, 1))], "small_exchange")
    (src_gates, src_out, src_fox), (land_gates, land_out, land_fox) = early_exchange_wait(
        first, [after_first, small_token], "early_wait_a")
    (src_dil,), (land_dil,) = early_exchange_wait(second, after_second, "early_wait_b")
    (src_mq, src_kv), (land_mq, land_kv) = early_exchange_wait(third, after_third, "early_wait_c")
    gw_out, d_out, m_out, v_out = slot_sum8_adamw(src_out, land_out, w_out[0], m_w_out[0], v_w_out[0], 256,
                                                  "sum_adamw_w_out")
    gw_kv, d_kv, m_kv, v_kv = slot_sum8_adamw(src_kv, land_kv, w_mem_kv[0], m_w_mem_kv[0], v_w_mem_kv[0], 128,
                                              "sum_adamw_w_kv")
    gw_in_cols = w_in_grad_sum(
        [(src_fox, land_fox, [(0, P_FOX, 3 * FOX_W), (3 * FOX_W, P_FLOG, LANES)]),
         (src_gates, land_gates, [(0, P_FG, FOX_W), (FOX_W, P_DG, DIL_W), (FOX_W + DIL_W, P_MG, MEM_W)]),
         (src_dil, land_dil, [(0, P_DIL, 3 * DIL_W)]),
         (src_mq, land_mq, [(0, P_MQ, MEM_W)])], "sum_w_in")
    gw_in = jnp.transpose(gw_in_cols, (1, 2, 0))

    def rows8(*rows):
        rows = [r.reshape(1, -1) for r in rows]
        rows = [_pad_row(r, D) for r in rows]
        return jnp.concatenate(rows + [jnp.zeros((8 - len(rows), D), F32)], axis=0)

    sw = rows8(norm_g, mem_norm_g, final_norm_g, b_forget)
    sm = rows8(m_norm_g, m_mem_norm_g, m_final_norm_g, m_b_forget)
    sv = rows8(v_norm_g, v_mem_norm_g, v_final_norm_g, v_b_forget)
    (src_small,), (land_small,) = early_exchange_wait(small_handle, gw_in_cols, "small_wait")
    tot, d_s, m_s, v_s = slot_sum8_adamw(src_small, land_small, sw, sm, sv, 8, "sum_adamw_small")
    loss = tot[4, 0]
    g_norm, g_mem_norm, g_final, g_b = tot[0:1], tot[1:2], tot[2], tot[3:4, :FOX_HEADS]
    columns_first = lambda a: jnp.transpose(a, (2, 0, 1))
    d_in, m_in, v_in = [jnp.transpose(o, (1, 2, 0)) for o in adamw_columns_first(
        columns_first(w_in), gw_in_cols, columns_first(m_w_in), columns_first(v_w_in), "adamw_w_in")]

    def small_outs(t):
        return t[0:1], t[3:4, :FOX_HEADS], t[1:2], t[2]

    grads = (g_norm, gw_in, g_b, g_mem_norm, gw_kv[None], gw_out[None], g_final)
    outs = []
    for t, big in ((d_s, (d_in, d_kv, d_out)), (m_s, (m_in, m_kv, m_out)), (v_s, (v_in, v_kv, v_out))):
        n, b, mn, f = small_outs(t)
        outs += [n, big[0], b, mn, big[1][None], big[2][None], f]
    return (loss, grad_x, *grads, *outs)
```

```python
import math

import numpy as np
import jax
import jax.numpy as jnp
from jax import lax
from jax.experimental import pallas as pl
from jax.experimental.pallas import tpu as pltpu

F32 = jnp.float32
BF16 = jnp.bfloat16

D_MODEL = 1024
HEAD_DIM = 64
FOX_HEADS = 12
DIL_HEADS = 12
MEM_HEADS = 4
MEM_HEAD_DIM = 128
MEM_LEN = 256
FOX_W = FOX_HEADS * HEAD_DIM
DIL_W = DIL_HEADS * HEAD_DIM
MEM_W = MEM_HEADS * MEM_HEAD_DIM
MIX_W = FOX_W + DIL_W + MEM_W
DILATIONS = ((128, 1), (512, 4), (2048, 16))
ROPE_THETA = 500000.0
ROPE_DIM = HEAD_DIM // 4
RMS_EPS = 1e-6
NEG_INF = -1e30
IN_W = 4 * FOX_W + FOX_HEADS + 4 * DIL_W + 2 * MEM_W

ADAM_LR = 0.001
ADAM_B1 = 0.9
ADAM_B2 = 0.999
ADAM_EPS = 1e-08
ADAM_WD = 0.01
ADAM_STEP = 10

N_DEV = 8
LANES = 128
PAIR_W = 3 * LANES
TQ = 256
TK = 256

O_FQ, O_FK, O_FV, O_FG = 0, FOX_W, 2 * FOX_W, 3 * FOX_W
O_FLOG = 4 * FOX_W
O_DQ = O_FLOG + FOX_HEADS
O_DK, O_DV, O_DG = O_DQ + DIL_W, O_DQ + 2 * DIL_W, O_DQ + 3 * DIL_W
O_MQ = O_DQ + 4 * DIL_W
O_MG = O_MQ + MEM_W
P_FOX = 0
P_FG = P_FOX + 3 * FOX_W
P_DIL = P_FG + FOX_W
P_DG = P_DIL + 3 * DIL_W
P_MQ = P_DG + DIL_W
P_MG = P_MQ + MEM_W
P_FLOG = P_MG + MEM_W
PW = P_FLOG + LANES
A_FOX = 0
A_FLOG = A_FOX + 3 * FOX_W
PA = A_FLOG + LANES
B_FG = 0
B_DG = B_FG + FOX_W
B_DIL = B_DG + DIL_W
B_MQ = B_DIL + 3 * DIL_W
B_MG = -(-(B_MQ + MEM_W) // MEM_W) * MEM_W
PB = B_MG + MEM_W

VMEM_LIMIT = 56 * 1024 * 1024


def _pack_pieces():
    pieces = []
    for base in (O_FQ, O_DQ):
        seg = []
        for hp in range(FOX_HEADS // 2):
            for part in range(3):
                seg.append((base + part * FOX_W + hp * LANES, LANES))
        pieces.append(seg)
    fox, dil = pieces
    return fox + [(O_FG, FOX_W)] + dil + [(O_DG, DIL_W), (O_MQ, MEM_W), (O_MG, MEM_W), (O_FLOG, FOX_HEADS)]


def _pack_cols(w):
    parts = [w[..., s:s + n] for s, n in _pack_pieces()]
    parts.append(jnp.zeros(w.shape[:-1] + (LANES - FOX_HEADS,), w.dtype))
    return jnp.concatenate(parts, axis=-1)


def _split_cols(wp):
    def cut(start, width):
        return wp[..., start:start + width]

    group_a = jnp.concatenate([cut(P_FOX, 3 * FOX_W), cut(P_FLOG, LANES)], axis=-1)
    pad = jnp.zeros(wp.shape[:-1] + (B_MG - B_MQ - MEM_W,), wp.dtype)
    group_b = jnp.concatenate([cut(P_FG, FOX_W), cut(P_DG, DIL_W), cut(P_DIL, 3 * DIL_W), cut(P_MQ, MEM_W), pad,
                               cut(P_MG, MEM_W)], axis=-1)
    return group_a, group_b


def _params(sem=None, **kw):
    return pltpu.CompilerParams(dimension_semantics=sem, vmem_limit_bytes=VMEM_LIMIT, **kw)


def _mesh_pos():
    return lax.axis_index("x"), lax.axis_index("y"), lax.axis_index("c")


def _flip(v, d):
    return 1 - v if d else v


_RELATIONS = [(dx, dy, dc) for dx in (0, 1) for dy in (0, 1) for dc in (0, 1)][1:]
_SIBLING_AND_SAME_CORES = [(0, 0, 1), (1, 0, 0), (0, 1, 0), (1, 1, 0)]


_OTHER_CHIPS = [(1, 0), (0, 1), (1, 1)]


_HBM = pl.BlockSpec(memory_space=pltpu.HBM)
_SEM = pl.BlockSpec(memory_space=pltpu.SEMAPHORE)
_EFFECT = pltpu.SideEffectType.DATAFLOW_SIDE_EFFECTING


def _early_copies(src_refs, land_refs, send_sems, recv_sems, rows, gather, relations):
    x, y, c = _mesh_pos()
    me = 4 * x + 2 * y + c
    copies = []
    for a in range(len(src_refs)):
        for dx, dy, dc in relations:
            px, py, pc = _flip(x, dx), _flip(y, dy), _flip(c, dc)
            peer = 4 * px + 2 * py + pc
            copies.append(pltpu.make_async_remote_copy(
                src_ref=src_refs[a] if gather else src_refs[a].at[pl.ds(peer * rows[a], rows[a]), :],
                dst_ref=land_refs[a].at[pl.ds(me * rows[a], rows[a]), :],
                send_sem=send_sems[a], recv_sem=recv_sems[a],
                device_id=(px, py, pc), device_id_type=pl.DeviceIdType.MESH))
    return copies


def own_slots(shards, name, after=None):
    n = len(shards)
    extra = [] if after is None else [after]
    x, y, c = _mesh_pos()
    me = (4 * x + 2 * y + c).astype(jnp.int32).reshape(1)
    empties = [lax.empty((N_DEV * s.shape[0], s.shape[1]), s.dtype) for s in shards]

    def body(me_ref, *refs):
        for a in range(n):
            refs[2 * n + len(extra) + a][...] = refs[a][...]

    return pl.pallas_call(
        body, name=name,
        grid_spec=pltpu.PrefetchScalarGridSpec(
            num_scalar_prefetch=1, grid=(1,),
            in_specs=[pl.BlockSpec(s.shape, lambda i, w: (0, 0)) for s in shards]
            + [pl.BlockSpec(memory_space=pl.ANY)] * (n + len(extra)),
            out_specs=[pl.BlockSpec(s.shape, lambda i, w: (w[0], 0)) for s in shards]),
        out_shape=[jax.ShapeDtypeStruct(e.shape, e.dtype) for e in empties],
        input_output_aliases={1 + n + a: a for a in range(n)},
        compiler_params=_params(("arbitrary",)),
    )(me, *shards, *empties, *extra)


def early_exchange_start(srcs, name, gather=False, after=None, relations=_RELATIONS, lands=None):
    n = len(srcs)
    if gather:
        rows = [s.shape[0] for s in srcs]
        lands = list(own_slots(srcs, name + "_place") if lands is None else lands)
    else:
        rows = [s.shape[0] // N_DEV for s in srcs]
        lands = [lax.empty(s.shape, s.dtype) for s in srcs]

    extra = [] if after is None else [after]

    def body(*refs):
        src_refs, land_refs = refs[:n], refs[n:2 * n]
        first_sem = 2 * n + len(extra)
        send_sems, recv_sems = refs[first_sem:first_sem + n], refs[first_sem + n:first_sem + 2 * n]
        token = refs[-1]
        for cp in _early_copies(src_refs, land_refs, send_sems, recv_sems, rows, gather, relations):
            cp.start()
        token[...] = jnp.zeros_like(token)

    hbm = lambda a: pltpu.HBM(a.shape, a.dtype)
    outs = pl.pallas_call(
        body, name=name,
        out_shape=[pltpu.SemaphoreType.DMA(())] * (2 * n)
        + [hbm(a) for a in srcs] + [hbm(a) for a in lands] + [jax.ShapeDtypeStruct((8, LANES), F32)],
        in_specs=[_HBM] * (2 * n) + [pl.BlockSpec(memory_space=pl.ANY)] * len(extra),
        out_specs=[_SEM] * (2 * n) + [_HBM] * (2 * n) + [pl.BlockSpec(memory_space=pltpu.VMEM)],
        input_output_aliases={i: 2 * n + i for i in range(2 * n)},
        compiler_params=pltpu.CompilerParams(has_side_effects=_EFFECT),
    )(*[pltpu.with_memory_space_constraint(a, pltpu.HBM) for a in list(srcs) + lands], *extra)
    handle = dict(sems=outs[:2 * n], srcs=outs[2 * n:3 * n], lands=outs[3 * n:4 * n], rows=rows,
                  copies=len(relations))
    return handle, outs[-1]


def early_exchange_wait(handle, after, name):
    n = len(handle["srcs"])
    rows = handle["rows"]
    after = list(after) if isinstance(after, (list, tuple)) else [after]

    def body(*refs):
        src_refs, land_refs = refs[:n], refs[n:2 * n]
        send_sems, recv_sems = refs[2 * n:3 * n], refs[3 * n:4 * n]
        x, y, c = _mesh_pos()
        for a in range(n):
            span = pl.ds(0, handle["copies"] * rows[a])
            all_copies = pltpu.make_async_remote_copy(
                src_ref=land_refs[a].at[span, :], dst_ref=land_refs[a].at[span, :],
                send_sem=send_sems[a], recv_sem=recv_sems[a],
                device_id=(x, y, c), device_id_type=pl.DeviceIdType.MESH)
            all_copies.wait_send()
            all_copies.wait_recv()

    hbm = lambda a: pltpu.HBM(a.shape, a.dtype)
    ins = list(handle["srcs"]) + list(handle["lands"])
    outs = pl.pallas_call(
        body, name=name,
        out_shape=[hbm(a) for a in ins],
        in_specs=[_HBM] * (2 * n) + [_SEM] * (2 * n) + [pl.BlockSpec(memory_space=pl.ANY)] * len(after),
        out_specs=[_HBM] * (2 * n),
        input_output_aliases={i: i for i in range(2 * n)},
        compiler_params=pltpu.CompilerParams(has_side_effects=_EFFECT),
    )(*ins, *handle["sems"], *after)
    return outs[:n], outs[n:]


def pass_on_to_sibling(lands, rows, name):
    n = len(lands)

    def body(*refs):
        land_refs = refs[n:2 * n]
        send_sems, recv_sems = refs[2 * n:]
        x, y, c = _mesh_pos()
        copies = []
        for a in range(n):
            for k, (dx, dy) in enumerate(_OTHER_CHIPS):
                slot = 4 * _flip(x, dx) + 2 * _flip(y, dy) + c
                blk = land_refs[a].at[pl.ds(slot * rows[a], rows[a]), :]
                copies.append(pltpu.make_async_remote_copy(
                    src_ref=blk, dst_ref=blk, send_sem=send_sems.at[a, k], recv_sem=recv_sems.at[a, k],
                    device_id=(x, y, 1 - c), device_id_type=pl.DeviceIdType.MESH))
        for cp in copies:
            cp.start()
        for cp in copies:
            cp.wait_recv()
        for cp in copies:
            cp.wait_send()

    any_spec = pl.BlockSpec(memory_space=pl.ANY)
    return pl.pallas_call(
        body, name=name,
        out_shape=[jax.ShapeDtypeStruct(a.shape, a.dtype) for a in lands],
        in_specs=[any_spec] * n, out_specs=[any_spec] * n,
        input_output_aliases={i: i for i in range(n)},
        scratch_shapes=[pltpu.SemaphoreType.DMA((n, len(_OTHER_CHIPS))), pltpu.SemaphoreType.DMA((n, len(_OTHER_CHIPS)))],
    )(*lands)


def _adamw_update(w, g, m, v):
    mn = ADAM_B1 * m + (1.0 - ADAM_B1) * g
    vn = ADAM_B2 * v + (1.0 - ADAM_B2) * jnp.square(g)
    m_hat = mn / (1.0 - ADAM_B1 ** ADAM_STEP)
    v_hat = vn / (1.0 - ADAM_B2 ** ADAM_STEP)
    return -ADAM_LR * (m_hat / (jnp.sqrt(v_hat) + ADAM_EPS) + ADAM_WD * w), mn, vn


def slot_sum8_adamw(src, land, w, m, v, tr, name):
    rows, cols = land.shape[0] // N_DEV, land.shape[1]
    x, y, c = _mesh_pos()
    me = (4 * x + 2 * y + c).astype(jnp.int32).reshape(1)

    def body(me_ref, src_ref, land_ref, w_ref, m_ref, v_ref, g_ref, d_ref, mo_ref, vo_ref):
        acc = None
        for d in range(N_DEV):
            term = jnp.where(d == me_ref[0], src_ref[0], land_ref[d]).astype(F32)
            acc = term if acc is None else acc + term
        g_ref[...] = acc
        d_ref[...], mo_ref[...], vo_ref[...] = _adamw_update(w_ref[...], acc, m_ref[...], v_ref[...])

    tile = pl.BlockSpec((tr, cols), lambda i, w: (i, 0))
    return pl.pallas_call(
        body, name=name,
        grid_spec=pltpu.PrefetchScalarGridSpec(
            num_scalar_prefetch=1, grid=(rows // tr,),
            in_specs=[pl.BlockSpec((1, tr, cols), lambda i, w: (w[0], i, 0)),
                      pl.BlockSpec((N_DEV, tr, cols), lambda i, w: (0, i, 0)), tile, tile, tile],
            out_specs=[tile] * 4),
        out_shape=[jax.ShapeDtypeStruct((rows, cols), F32)] * 4,
        compiler_params=_params(("arbitrary",)),
    )(me, src.reshape(N_DEV, rows, cols), land.reshape(N_DEV, rows, cols), w, m, v)


def w_in_grad_sum(groups, name):
    rows = groups[0][0].shape[0] // N_DEV
    runs, pos = [], 0
    for start, width in _pack_pieces():
        runs.append((start, width, pos))
        pos += width
    n = len(groups)

    def body(*refs):
        src_refs, land_refs = refs[0:2 * n:2], refs[1:2 * n:2]
        g_ref = refs[2 * n]
        own_bufs, land_bufs = refs[2 * n + 1:3 * n + 1], refs[3 * n + 1:4 * n + 1]
        stage, load_sems, store_sems = refs[4 * n + 1:]
        x, y, c = _mesh_pos()
        me = 4 * x + 2 * y + c
        loads = []
        for k in range(n):
            pair = [pltpu.make_async_copy(src_refs[k].at[pl.ds(me * rows, rows), :], own_bufs[k], load_sems.at[k, 0]),
                    pltpu.make_async_copy(land_refs[k], land_bufs[k], load_sems.at[k, 1])]
            for cp in pair:
                cp.start()
            loads.append(pair)
        for k, (_, _, segments) in enumerate(groups):
            for cp in loads[k]:
                cp.wait()
            for first, packed, width in segments:
                for off in range(0, width, LANES):
                    cols = slice(first + off, first + off + LANES)
                    acc = None
                    for d in range(N_DEV):
                        term = jnp.where(d == me, own_bufs[k][:, cols], land_bufs[k][d * rows:(d + 1) * rows, cols])
                        acc = term.astype(F32) if acc is None else acc + term.astype(F32)
                    stage[packed + off:packed + off + LANES, :] = acc.T
        g_rows = g_ref.reshape(IN_W, LANES)
        stores = [pltpu.make_async_copy(stage.at[pl.ds(p, width), :], g_rows.at[pl.ds(start, width), :], store_sems.at[r])
                  for r, (start, width, p) in enumerate(runs)]
        for cp in stores:
            cp.start()
        for cp in stores:
            cp.wait()

    any_spec = pl.BlockSpec(memory_space=pl.ANY)
    operands = [a for src, land, _ in groups for a in (src, land)]
    return pl.pallas_call(
        body, name=name,
        in_specs=[any_spec] * (2 * n), out_specs=any_spec,
        out_shape=jax.ShapeDtypeStruct((IN_W, 1, LANES), F32),
        scratch_shapes=[pltpu.VMEM((rows, src.shape[1]), src.dtype) for src, _, _ in groups]
        + [pltpu.VMEM(land.shape, land.dtype) for _, land, _ in groups]
        + [pltpu.VMEM((PW, LANES), F32), pltpu.SemaphoreType.DMA((n, 2)), pltpu.SemaphoreType.DMA((len(runs),))],
        compiler_params=_params(),
    )(*operands)


def mm_tn_multi(a_t, bs, tt, name, out_dtype=F32):
    K, T = a_t.shape
    widths = [b.shape[1] for b in bs]
    steps = T // tt

    def body(a_ref, *rest):
        b_refs, o_ref, acc = rest[:-2], rest[-2], rest[-1]

        @pl.when(pl.program_id(0) == 0)
        def _():
            acc[...] = jnp.zeros(acc.shape, F32)

        av = a_ref[...]
        col = 0
        for b_ref, w in zip(b_refs, widths):
            acc[:, col:col + w] += jnp.dot(av, b_ref[...], preferred_element_type=F32)
            col += w

        @pl.when(pl.program_id(0) == steps - 1)
        def _():
            o_ref[...] = acc[...].astype(out_dtype)

    return pl.pallas_call(
        body, name=name, grid=(steps,),
        in_specs=[pl.BlockSpec((K, tt), lambda t: (0, t))] + [pl.BlockSpec((tt, w), lambda t: (t, 0)) for w in widths],
        out_specs=pl.BlockSpec((K, sum(widths)), lambda t: (0, 0)),
        out_shape=jax.ShapeDtypeStruct((K, sum(widths)), out_dtype),
        scratch_shapes=[pltpu.VMEM((K, sum(widths)), F32)],
        compiler_params=_params(("arbitrary",)),
    )(a_t, *bs)


def rms_fwd(x, g, tm, name, with_transpose=False, token=None):
    M, K = x.shape
    extra = [] if token is None else [token]

    def body(x_ref, g_ref, *rest):
        o_ref = rest[len(extra)]
        xv = x_ref[...]
        r = lax.rsqrt(jnp.mean(xv * xv, axis=-1, keepdims=True) + RMS_EPS)
        h = ((xv * r) * g_ref[...]).astype(BF16)
        o_ref[...] = h
        if with_transpose:
            rest[len(extra) + 1][...] = h.T

    out_specs = [pl.BlockSpec((tm, K), lambda i: (i, 0))]
    out_shape = [jax.ShapeDtypeStruct((M, K), BF16)]
    if with_transpose:
        out_specs.append(pl.BlockSpec((K, tm), lambda i: (0, i)))
        out_shape.append(jax.ShapeDtypeStruct((K, M), BF16))
    outs = pl.pallas_call(
        body, name=name, grid=(M // tm,),
        in_specs=[pl.BlockSpec((tm, K), lambda i: (i, 0)), pl.BlockSpec((1, K), lambda i: (0, 0))]
        + [pl.BlockSpec(t.shape, lambda i: (0, 0)) for t in extra],
        out_specs=out_specs, out_shape=out_shape,
        compiler_params=_params(("arbitrary",)),
    )(x, g, *extra)
    return outs if with_transpose else outs[0]


def mm_nn(a, b, tm, tn, name, token=None):
    M, K = a.shape
    N = b.shape[1]
    extra = [] if token is None else [token]

    def body(a_ref, b_ref, *rest):
        rest[-1][...] = jnp.dot(a_ref[...], b_ref[...], preferred_element_type=F32)

    return pl.pallas_call(
        body, name=name, grid=(N // tn, M // tm),
        in_specs=[pl.BlockSpec((tm, K), lambda j, i: (i, 0)), pl.BlockSpec((K, tn), lambda j, i: (0, j))]
        + [pl.BlockSpec(t.shape, lambda j, i: (0, 0)) for t in extra],
        out_specs=pl.BlockSpec((tm, tn), lambda j, i: (i, j)),
        out_shape=jax.ShapeDtypeStruct((M, N), F32),
        compiler_params=_params(("arbitrary", "arbitrary")),
    )(a, b, *extra)


def proj_bwd_rms(pieces, ws, x, g, dres, tm, token, name):
    M, N = x.shape
    residual = [] if dres is None else [dres]

    def body(*refs):
        n = len(pieces)
        p_refs, w_refs = refs[:n], refs[n:n + len(ws)]
        x_ref, g_ref, *dres_ref = refs[n + len(ws):n + len(ws) + 2 + len(residual)]
        dx_ref, dg_ref = refs[-2:]
        dh = None
        for p_ref, (arr, group, col) in zip(p_refs, pieces):
            part = lax.dot_general(p_ref[...], w_refs[group][:, col:col + arr.shape[1]], (((1,), (1,)), ((), ())),
                                   preferred_element_type=F32)
            dh = part if dh is None else dh + part
        xv = x_ref[...]
        r = lax.rsqrt(jnp.mean(xv * xv, axis=-1, keepdims=True) + RMS_EPS)
        xn = xv * r
        dxn = dh * g_ref[...]
        dx = r * (dxn - xn * jnp.mean(dxn * xn, axis=-1, keepdims=True))
        dx_ref[...] = dx + dres_ref[0][...] if residual else dx
        row = lax.broadcasted_iota(jnp.int32, (8, N), 0)
        upd = jnp.where(row == 0, jnp.sum(dh * xn, axis=0, keepdims=True), 0.0)

        @pl.when(pl.program_id(0) == 0)
        def _():
            dg_ref[...] = upd

        @pl.when(pl.program_id(0) != 0)
        def _():
            dg_ref[...] += upd

    row_spec = pl.BlockSpec((tm, N), lambda i: (i, 0))
    return pl.pallas_call(
        body, name=name, grid=(M // tm,),
        in_specs=[pl.BlockSpec((tm, arr.shape[1]), lambda i: (i, 0)) for arr, _, _ in pieces]
        + [pl.BlockSpec(w.shape, lambda i: (0, 0), pipeline_mode=pl.Buffered(1)) for w in ws]
        + [row_spec, pl.BlockSpec((1, N), lambda i: (0, 0))] + [row_spec] * len(residual)
        + [pl.BlockSpec(token.shape, lambda i: (0, 0))],
        out_specs=[row_spec, pl.BlockSpec((8, N), lambda i: (0, 0))],
        out_shape=[jax.ShapeDtypeStruct((M, N), F32), jax.ShapeDtypeStruct((8, N), F32)],
        compiler_params=_params(("arbitrary",)),
    )(*[arr for arr, _, _ in pieces], *ws, x, g, *residual, token)


def _log_sigmoid(z):
    return jnp.minimum(z, 0.0) - jnp.log(1.0 + jnp.exp(-jnp.abs(z)))


def _tri(n, lower):
    r = lax.broadcasted_iota(jnp.int32, (n, n), 0)
    c = lax.broadcasted_iota(jnp.int32, (n, n), 1)
    return jnp.where((r >= c) if lower else (r <= c), 1.0, 0.0).astype(F32)


def fox_gate(proj3, b_pad):
    B, S, _ = proj3.shape
    nblk = S // TK

    def body(f_ref, b_ref, o_ref):
        tri = _tri(TK, True)
        carry = jnp.zeros((1, LANES), F32)
        for n in range(nblk):
            z = f_ref[0, n * TK:(n + 1) * TK, :] + b_ref[...]
            logf = _log_sigmoid(z)
            cs = jnp.dot(tri, logf, preferred_element_type=F32, precision=lax.Precision.HIGHEST) + carry
            carry = cs[TK - 1:TK, :]
            o_ref[0, n * TK:(n + 1) * TK, :] = -cs

    return pl.pallas_call(
        body, name="fox_gate", grid=(B,),
        in_specs=[pl.BlockSpec((1, S, LANES), lambda b: (b, 0, A_FLOG // LANES)),
                  pl.BlockSpec((1, LANES), lambda b: (0, 0))],
        out_specs=pl.BlockSpec((1, S, LANES), lambda b: (b, 0, 0)),
        out_shape=jax.ShapeDtypeStruct((B, S, LANES), F32),
        compiler_params=_params(("arbitrary",)),
    )(proj3, b_pad)


def fox_gate_bwd(drow, dneg, proj3, b_pad):
    B, S, _ = proj3.shape
    nblk = S // TK

    def body(d_ref, r_ref, f_ref, b_ref, o_ref, db_ref):
        tri = _tri(TK, False)
        lane = lax.broadcasted_iota(jnp.int32, (TK, LANES), 1)
        carry = jnp.zeros((1, LANES), F32)
        dbsum = jnp.zeros((1, LANES), F32)
        for n in reversed(range(nblk)):
            dk_side = None
            for hp in range(FOX_HEADS // 2):
                two = jnp.where(lane < 2, r_ref[0, n * TK:(n + 1) * TK, hp * LANES:(hp + 1) * LANES], 0.0)
                two = pltpu.roll(two, 2 * hp, 1) if hp else two
                dk_side = two if dk_side is None else dk_side + two
            dc = jnp.where(lane < FOX_HEADS, d_ref[0, :, n * TK:(n + 1) * TK].T - dk_side, 0.0)
            rs = jnp.dot(tri, dc, preferred_element_type=F32, precision=lax.Precision.HIGHEST) + carry
            carry = rs[0:1, :]
            z = f_ref[0, n * TK:(n + 1) * TK, :] + b_ref[...]
            dz = rs * (1.0 / (1.0 + jnp.exp(z)))
            o_ref[0, n * TK:(n + 1) * TK, :] = dz.astype(BF16)
            dbsum = dbsum + jnp.sum(dz, axis=0, keepdims=True)
        row = lax.broadcasted_iota(jnp.int32, (8, LANES), 0)
        upd = jnp.where(row == 0, dbsum, 0.0)

        @pl.when(pl.program_id(0) == 0)
        def _():
            db_ref[...] = upd

        @pl.when(pl.program_id(0) != 0)
        def _():
            db_ref[...] += upd

    return pl.pallas_call(
        body, name="fox_gate_bwd", grid=(B,),
        in_specs=[pl.BlockSpec((1, LANES, S), lambda b: (b, 0, 0)),
                  pl.BlockSpec((1, S, FOX_W), lambda b: (b, 0, 0)),
                  pl.BlockSpec((1, S, LANES), lambda b: (b, 0, A_FLOG // LANES)),
                  pl.BlockSpec((1, LANES), lambda b: (0, 0))],
        out_specs=[pl.BlockSpec((1, S, LANES), lambda b: (b, 0, 0)), pl.BlockSpec((8, LANES), lambda b: (0, 0))],
        out_shape=[jax.ShapeDtypeStruct((B, S, LANES), BF16), jax.ShapeDtypeStruct((8, LANES), F32)],
        compiler_params=_params(("arbitrary",)),
    )(drow, dneg, proj3, b_pad)


def _rope_tables(S):
    half = ROPE_DIM // 2
    f32 = np.float32
    pos = np.arange(S, dtype=f32)
    inv_freq = f32(1.0) / np.power(f32(ROPE_THETA), np.arange(0, ROPE_DIM, 2, dtype=f32) / f32(ROPE_DIM)).astype(f32)
    ang = (pos[:, None] * inv_freq[None, :]).astype(f32).astype(np.float64)
    cos, sin = np.cos(ang).astype(f32), np.sin(ang).astype(f32)
    one = np.ones((S, HEAD_DIM - ROPE_DIM), f32)
    zero = np.zeros((S, HEAD_DIM - ROPE_DIM), f32)
    zh = np.zeros((S, half), f32)
    c = np.concatenate([cos, cos, one], axis=1)
    s1 = np.concatenate([-sin, zh, zero], axis=1)
    s2 = np.concatenate([zh, sin, zero], axis=1)
    return tuple(jnp.asarray(np.concatenate([t, t], axis=1)) for t in (c, s1, s2))


_HALF_ROPE = ROPE_DIM // 2


def _rope(t, c, s1, s2):
    return t * c + pltpu.roll(t, LANES - _HALF_ROPE, 1) * s1 + pltpu.roll(t, _HALF_ROPE, 1) * s2


def _rope_bwd(d, c, s1, s2):
    return d * c + pltpu.roll(d * s1, _HALF_ROPE, 1) + pltpu.roll(d * s2, LANES - _HALF_ROPE, 1)


def _scale_parts(scale):
    m, _ = math.frexp(scale)
    return (scale, None) if m == 0.5 else (None, scale)


def _log_masks(S, kind):
    nd = 1 if kind == "causal" else S // TQ
    a = np.arange(TQ)[:, None]
    b = np.arange(TK)[None, :]
    out = np.zeros((nd, TQ, TK), np.float32)
    for d in range(nd):
        delta = d * TQ + a - b
        if kind == "causal":
            m = (delta >= 0).astype(np.float64)
        else:
            m = sum(((delta >= 0) & (delta % dil == 0) & (delta <= w)).astype(np.float64) for w, dil in DILATIONS)
        out[d] = np.where(m > 0, np.log(np.maximum(m, 1.0)), NEG_INF)
    return jnp.asarray(out)


def _attn_setup(kind):
    pair = kind != "mem"
    e_dim = HEAD_DIM if pair else MEM_HEAD_DIM
    q_fold, s_scale = _scale_parts(1.0 / math.sqrt(e_dim))
    return dict(pair=pair, col0={"fox": A_FOX, "dil": B_DIL, "mem": B_MQ}[kind],
                n_blocks=FOX_HEADS // 2 if pair else MEM_HEADS, q_fold=q_fold, s_scale=s_scale,
                nh=2 if pair else 1)


def _cat(parts, axis):
    return parts[0] if len(parts) == 1 else jnp.concatenate(parts, axis=axis)


def _log_masks_t(S, kind):
    return jnp.swapaxes(_log_masks(S, kind), 1, 2)


def _head_rows(hh, pair):
    row = lax.broadcasted_iota(jnp.int32, (LANES, 1), 0)
    if not pair:
        return row >= 0
    return (row >= HEAD_DIM * hh) & (row < HEAD_DIM * (hh + 1))


def _attn_t_inputs(kind, src, S, negc_cols, mask, rope, kv):
    cfg = _attn_setup(kind)
    col0 = cfg["col0"]
    ins, in_specs = [], []
    if cfg["pair"]:
        ins.append(src)
        in_specs.append(pl.BlockSpec((1, S, PAIR_W), lambda b, h: (b, 0, col0 // PAIR_W + h)))
    else:
        ins += [src, kv, kv]
        in_specs += [pl.BlockSpec((1, S, LANES), lambda b, h: (b, 0, col0 // LANES + h)),
                     pl.BlockSpec((1, MEM_LEN, LANES), lambda b, h: (b, 0, h)),
                     pl.BlockSpec((1, MEM_LEN, LANES), lambda b, h: (b, 0, MEM_HEADS + h))]
    if negc_cols is not None:
        ins.append(negc_cols)
        in_specs.append(pl.BlockSpec((1, S, LANES), lambda b, h: (b, 0, 0)))
    if mask is not None:
        ins.append(mask)
        in_specs.append(pl.BlockSpec(mask.shape, lambda b, h: (0, 0, 0)))
    if rope is not None:
        ins += list(rope)
        in_specs += [pl.BlockSpec((S, LANES), lambda b, h: (0, 0))] * 3
    return ins, in_specs


def _attn_t_prep(cfg, refs, S, Sk, *, qT2s, ks, vs=None, vTs=None, kTs=None, nb=None):
    pair, nh = cfg["pair"], cfg["nh"]
    lane = lax.broadcasted_iota(jnp.int32, (1, LANES), 1)
    rope_refs = refs["rope"]

    def prep_q(n):
        rows = slice(n * TQ, (n + 1) * TQ)
        q = refs["load_q"](rows)
        if rope_refs is not None:
            q = _rope(q, *[t[rows, :] for t in rope_refs])
        if cfg["q_fold"] is not None:
            q = q * cfg["q_fold"]
        qtb = q.astype(BF16).T
        for hh in range(nh):
            qT2s[n, :, hh * TQ:(hh + 1) * TQ] = jnp.where(_head_rows(hh, pair), qtb, jnp.zeros_like(qtb))

    def prep_kv(n):
        rows = slice(n * TK, (n + 1) * TK)
        k, v = refs["load_kv"](rows)
        if rope_refs is not None:
            k = _rope(k, *[t[rows, :] for t in rope_refs])
        kb = k.astype(BF16)
        vb = v.astype(BF16)
        ks[rows, :] = kb
        if vs is not None:
            vs[rows, :] = vb
        if vTs is not None:
            vTs[n] = vb.T
        if kTs is not None:
            kTs[n] = kb.T
        if nb is not None:
            blk = refs["negc"][0, rows, :]
            for hh in range(nh):
                h = 2 * refs["block"] + hh
                col = jnp.sum(jnp.where(lane == h, blk, 0.0), axis=1, keepdims=True)
                nb[hh, rows, :] = jnp.broadcast_to(col, (TK, LANES))

    for n in range(S // TQ):
        prep_q(n)
    for n in range(Sk // TK):
        prep_kv(n)


def _raw_scores_t(cfg, k, qT2):
    sT = jnp.dot(k, qT2, preferred_element_type=F32)
    if cfg["s_scale"] is not None:
        sT = sT * cfg["s_scale"]
    return sT


def _bias_mask_t(cfg, sT, nb, mask_ref, kc, midx):
    nh = cfg["nh"]
    if nb is None and midx is None:
        return sT
    parts = []
    for hh in range(nh):
        t = sT[:, hh * TQ:(hh + 1) * TQ]
        if nb is not None:
            t = t + jnp.concatenate([nb[hh, kc, :]] * (TQ // LANES), axis=1)
        if midx is not None:
            t = t + mask_ref[midx]
        parts.append(t)
    return _cat(parts, 1)


def _tile_pairs(kind, nq, nk):
    if kind == "mem":
        return [(i, j) for i in range(nq) for j in range(nk)], (lambda i, j: None)
    pairs = [(i, j) for i in range(nq) for j in range(i + 1)]
    if kind == "fox":
        return pairs, (lambda i, j: 0 if j == i else None)
    return pairs, (lambda i, j: i - j)


def attn_fwd(kind, src, S, *, negc_cols=None, mask=None, rope=None, kv=None):
    B = src.shape[0]
    cfg = _attn_setup(kind)
    pair, nh = cfg["pair"], cfg["nh"]
    Sk = S if pair else MEM_LEN
    has_bias, has_rope = negc_cols is not None, rope is not None
    R = nh * TQ
    nq, nk = S // TQ, Sk // TK
    pairs, mask_index = _tile_pairs(kind, nq, nk)

    def body(*refs):
        refs = list(refs)
        if pair:
            qkv_ref = refs.pop(0)
            load_q = lambda rows: qkv_ref[0, rows, 0:LANES]
            load_kv = lambda rows: (qkv_ref[0, rows, LANES:2 * LANES], qkv_ref[0, rows, 2 * LANES:3 * LANES])
        else:
            q_ref, k_ref, v_ref = refs.pop(0), refs.pop(0), refs.pop(0)
            load_q = lambda rows: q_ref[0, rows, :]
            load_kv = lambda rows: (k_ref[0, rows, :], v_ref[0, rows, :])
        negc_ref = refs.pop(0) if has_bias else None
        mask_ref = refs.pop(0) if mask is not None else None
        rope_refs = [refs.pop(0) for _ in range(3)] if has_rope else None
        o_ref, lse_ref, qT2s, ks, vTs, s_a, s_b, p_a, p_b = refs[:9]
        nb = refs[9] if has_bias else None
        _attn_t_prep(cfg, dict(load_q=load_q, load_kv=load_kv, rope=rope_refs, negc=negc_ref,
                               block=pl.program_id(1)), S, Sk, qT2s=qT2s, ks=ks, vTs=vTs, nb=nb)

        def cols(j):
            return slice(j * TK, (j + 1) * TK)

        def scores(i, j):
            return _raw_scores_t(cfg, ks[cols(j), :], qT2s[i])

        def finish(i, m, l, accT):
            oT2 = accT / l
            oT = jnp.where(_head_rows(0, True), oT2[:, 0:TQ], oT2[:, TQ:2 * TQ]) if pair else oT2
            o_ref[0, i * TQ:(i + 1) * TQ, :] = oT.T
            lse_ref[0, 0, i:i + 1, :] = m + jnp.log(l)

        s_bufs, p_bufs = (s_a, s_b), (p_a, p_b)
        s_bufs[0][...] = scores(*pairs[0])
        m = l = accT = None
        for t, (i, j) in enumerate(pairs):
            cur, oth = t % 2, 1 - t % 2
            if t > 0:
                i_prev, j_prev = pairs[t - 1]
                pv = jnp.dot(vTs[j_prev], p_bufs[oth][...], preferred_element_type=F32)
                acc_full = pv if accT is None else accT + pv
            if t + 1 < len(pairs):
                s_bufs[oth][...] = scores(*pairs[t + 1])
            first = j == 0
            if first and t > 0:
                finish(i_prev, m, l, acc_full)
            sT = _bias_mask_t(cfg, s_bufs[cur][...], nb, mask_ref, cols(j), mask_index(i, j))
            m_tile = jnp.max(sT, axis=0, keepdims=True)
            m_new = m_tile if first else jnp.maximum(m, m_tile)
            p = jnp.exp(sT - m_new)
            p_bufs[cur][...] = p.astype(BF16)
            if first:
                l, accT = jnp.sum(p, axis=0, keepdims=True), None
            else:
                alpha = jnp.exp(m - m_new)
                l, accT = alpha * l + jnp.sum(p, axis=0, keepdims=True), acc_full * alpha
            m = m_new
        i_last, j_last = pairs[-1]
        pv = jnp.dot(vTs[j_last], p_bufs[(len(pairs) - 1) % 2][...], preferred_element_type=F32)
        finish(i_last, m, l, pv if accT is None else accT + pv)

    ins, in_specs = _attn_t_inputs(kind, src, S, negc_cols, mask, rope, kv)
    W = cfg["n_blocks"] * LANES
    scratch = [pltpu.VMEM((nq, LANES, R), BF16), pltpu.VMEM((Sk, LANES), BF16), pltpu.VMEM((nk, LANES, TK), BF16),
               pltpu.VMEM((TK, R), F32), pltpu.VMEM((TK, R), F32), pltpu.VMEM((TK, R), BF16), pltpu.VMEM((TK, R), BF16)]
    if has_bias:
        scratch.append(pltpu.VMEM((nh, Sk, LANES), F32))
    return pl.pallas_call(
        body, name=kind + "_attn_fwd", grid=(B, cfg["n_blocks"]),
        in_specs=in_specs,
        out_specs=[pl.BlockSpec((1, S, LANES), lambda b, h: (b, 0, h)),
                   pl.BlockSpec((1, 1, nq, R), lambda b, h: (b, h, 0, 0))],
        out_shape=[jax.ShapeDtypeStruct((B, S, W), F32), jax.ShapeDtypeStruct((B, cfg["n_blocks"], nq, R), F32)],
        scratch_shapes=scratch,
        compiler_params=_params(("arbitrary", "arbitrary")),
    )(*ins)


def attn_bwd(kind, src, do, o, lse, S, *, negc_cols=None, mask=None, rope=None, kv=None, token=None):
    B = src.shape[0]
    cfg = _attn_setup(kind)
    pair, nh, s_scale, q_fold = cfg["pair"], cfg["nh"], cfg["s_scale"], cfg["q_fold"]
    Sk = S if pair else MEM_LEN
    has_bias, has_rope = negc_cols is not None, rope is not None
    R = nh * TQ
    nq, nk = S // TQ, Sk // TK
    pairs, mask_index = _tile_pairs(kind, nq, nk)

    def body(*refs):
        refs = list(refs)
        if pair:
            qkv_ref = refs.pop(0)
            load_q = lambda rows: qkv_ref[0, rows, 0:LANES]
            load_kv = lambda rows: (qkv_ref[0, rows, LANES:2 * LANES], qkv_ref[0, rows, 2 * LANES:3 * LANES])
        else:
            q_ref, k_ref, v_ref = refs.pop(0), refs.pop(0), refs.pop(0)
            load_q = lambda rows: q_ref[0, rows, :]
            load_kv = lambda rows: (k_ref[0, rows, :], v_ref[0, rows, :])
        negc_ref = refs.pop(0) if has_bias else None
        mask_ref = refs.pop(0) if mask is not None else None
        rope_refs = [refs.pop(0) for _ in range(3)] if has_rope else None
        do_ref, o_ref, lse_ref = refs.pop(0), refs.pop(0), refs.pop(0)
        if token is not None:
            refs.pop(0)
        if pair:
            dqkv_ref = refs.pop(0)
            dneg_ref = refs.pop(0) if has_bias else None
            drow_ref = refs.pop(0) if has_bias else None
        else:
            dq_ref, dk_ref, dv_ref = refs.pop(0), refs.pop(0), refs.pop(0)
        qT2s, ks, vs, kTs, doT2s, delta_s, dk_acc, dv_acc = refs[:8]
        bufs_a, bufs_b = refs[8:12], refs[12:16]
        nb, dneg_acc = (refs[16], refs[17]) if has_bias else (None, None)
        lane = lax.broadcasted_iota(jnp.int32, (1, LANES), 1)
        _attn_t_prep(cfg, dict(load_q=load_q, load_kv=load_kv, rope=rope_refs, negc=negc_ref,
                               block=pl.program_id(1)), S, Sk,
                     qT2s=qT2s, ks=ks, vs=vs, kTs=kTs, nb=nb)

        def prep_do(n):
            rows = slice(n * TQ, (n + 1) * TQ)
            doT = do_ref[0, rows, :].astype(BF16).astype(F32).T
            prodT = doT * o_ref[0, rows, :].T
            doTb = doT.astype(BF16)
            for hh in range(nh):
                hm = _head_rows(hh, pair)
                doT2s[n, :, hh * TQ:(hh + 1) * TQ] = jnp.where(hm, doTb, jnp.zeros_like(doTb))
                delta_s[n:n + 1, hh * TQ:(hh + 1) * TQ] = jnp.sum(jnp.where(hm, prodT, 0.0), axis=0, keepdims=True)

        for n in range(nq):
            prep_do(n)
        dk_acc[...] = jnp.zeros(dk_acc.shape, F32)
        dv_acc[...] = jnp.zeros(dv_acc.shape, F32)
        if has_bias:
            dneg_acc[...] = jnp.zeros(dneg_acc.shape, F32)

        def cols(j):
            return slice(j * TK, (j + 1) * TK)

        nt_dims = (((1,), (1,)), ((), ()))

        def first_products(i, j, bufs):
            bufs[0][...] = _raw_scores_t(cfg, ks[cols(j), :], qT2s[i])
            bufs[1][...] = jnp.dot(vs[cols(j), :], doT2s[i], preferred_element_type=F32)

        def last_products(i, j, bufs, dqT2):
            dv_acc[j] += lax.dot_general(doT2s[i], bufs[2][...], nt_dims, preferred_element_type=F32)
            dk_acc[j] += lax.dot_general(qT2s[i], bufs[3][...], nt_dims, preferred_element_type=F32)
            dq = jnp.dot(kTs[j], bufs[3][...], preferred_element_type=F32)
            return dq if dqT2 is None else dqT2 + dq

        def finish_q(i, dqT2, drow):
            rows = slice(i * TQ, (i + 1) * TQ)
            dqT = jnp.where(_head_rows(0, True), dqT2[:, 0:TQ], dqT2[:, TQ:2 * TQ]) if pair else dqT2
            dq = dqT.T
            if q_fold is not None:
                dq = dq * q_fold
            if has_rope:
                dq = _rope_bwd(dq, *[t[rows, :] for t in rope_refs])
            if pair:
                dqkv_ref[0, rows, 0:LANES] = dq.astype(BF16)
            else:
                dq_ref[0, rows, :] = dq.astype(BF16)
            if has_bias:
                drow_ref[0, 0, i:i + 1, :] = drow

        bufs = (bufs_a, bufs_b)
        first_products(*pairs[0], bufs[0])
        dqT2 = drow = None
        for t, (i, j) in enumerate(pairs):
            cur, oth = bufs[t % 2], bufs[1 - t % 2]
            first = j == 0
            if first and t > 0:
                i_prev, j_prev = pairs[t - 1]
                finish_q(i_prev, last_products(i_prev, j_prev, oth, dqT2), drow)
                dqT2 = drow = None
            sT = _bias_mask_t(cfg, cur[0][...], nb, mask_ref, cols(j), mask_index(i, j))
            pT = jnp.exp(sT - lse_ref[0, 0, i:i + 1, :])
            dsT = pT * (cur[1][...] - delta_s[i:i + 1, :])
            if has_bias:
                tile_rows = jnp.sum(dsT, axis=0, keepdims=True)
                drow = tile_rows if drow is None else drow + tile_rows
                for hh in range(nh):
                    part = dsT[:, hh * TQ:hh * TQ + LANES]
                    for u in range(1, TQ // LANES):
                        part = part + dsT[:, hh * TQ + u * LANES:hh * TQ + (u + 1) * LANES]
                    dneg_acc[hh, cols(j), :] += part
            if s_scale is not None:
                dsT = dsT * s_scale
            cur[2][...] = pT.astype(BF16)
            cur[3][...] = dsT.astype(BF16)
            if not first:
                dqT2 = last_products(*pairs[t - 1], oth, dqT2)
            if t + 1 < len(pairs):
                first_products(*pairs[t + 1], oth)
        i_last, j_last = pairs[-1]
        finish_q(i_last, last_products(i_last, j_last, bufs[(len(pairs) - 1) % 2], dqT2), drow)

        for n in range(nk):
            rows = slice(n * TK, (n + 1) * TK)
            dk = dk_acc[n].T
            dv = dv_acc[n].T
            if has_rope:
                dk = _rope_bwd(dk, *[t[rows, :] for t in rope_refs])
            if pair:
                dqkv_ref[0, rows, LANES:2 * LANES] = dk.astype(BF16)
                dqkv_ref[0, rows, 2 * LANES:3 * LANES] = dv.astype(BF16)
            else:
                dk_ref[0, rows, :] = dk.astype(BF16)
                dv_ref[0, rows, :] = dv.astype(BF16)
            if has_bias:
                x0 = jnp.sum(dneg_acc[0, rows, :], axis=1, keepdims=True)
                x1 = jnp.sum(dneg_acc[1, rows, :], axis=1, keepdims=True)
                dneg_ref[0, rows, :] = jnp.where(lane == 0, x0, jnp.where(lane == 1, x1, 0.0))

    ins, in_specs = _attn_t_inputs(kind, src, S, negc_cols, mask, rope, kv)
    row_spec = pl.BlockSpec((1, S, LANES), lambda b, h: (b, 0, h))
    vec_spec = pl.BlockSpec((1, 1, nq, R), lambda b, h: (b, h, 0, 0))
    ins += [do, o, lse]
    in_specs += [row_spec, row_spec, vec_spec]
    if token is not None:
        ins.append(token)
        in_specs.append(pl.BlockSpec(token.shape, lambda b, h: (0, 0)))
    W = cfg["n_blocks"] * LANES
    if pair:
        out_specs = [pl.BlockSpec((1, S, PAIR_W), lambda b, h: (b, 0, h))]
        out_shape = [jax.ShapeDtypeStruct((B, S, 3 * W), BF16)]
        if has_bias:
            out_specs += [row_spec, vec_spec]
            out_shape += [jax.ShapeDtypeStruct((B, S, W), F32), jax.ShapeDtypeStruct((B, cfg["n_blocks"], nq, R), F32)]
    else:
        kv_spec = pl.BlockSpec((1, MEM_LEN, LANES), lambda b, h: (b, 0, h))
        out_specs = [row_spec, kv_spec, kv_spec]
        out_shape = [jax.ShapeDtypeStruct((B, S, W), BF16)] + [jax.ShapeDtypeStruct((B, MEM_LEN, W), BF16)] * 2
    scratch = [pltpu.VMEM((nq, LANES, R), BF16), pltpu.VMEM((Sk, LANES), BF16),
               pltpu.VMEM((Sk, LANES), BF16), pltpu.VMEM((nk, LANES, TK), BF16), pltpu.VMEM((nq, LANES, R), BF16),
               pltpu.VMEM((nq, R), F32), pltpu.VMEM((nk, LANES, TK), F32), pltpu.VMEM((nk, LANES, TK), F32)]
    pair_bufs = [pltpu.VMEM((TK, R), F32), pltpu.VMEM((TK, R), F32), pltpu.VMEM((TK, R), BF16), pltpu.VMEM((TK, R), BF16)]
    scratch += pair_bufs + pair_bufs
    if has_bias:
        scratch += [pltpu.VMEM((nh, Sk, LANES), F32), pltpu.VMEM((nh, Sk, LANES), F32)]
    return pl.pallas_call(
        body, name=kind + "_attn_bwd", grid=(B, cfg["n_blocks"]),
        in_specs=in_specs, out_specs=out_specs, out_shape=out_shape, scratch_shapes=scratch,
        compiler_params=_params(("arbitrary", "arbitrary")),
    )(*ins)


def _sigmoid(g):
    return 1.0 / (1.0 + jnp.exp(-g))


def out_step(proj, o_fox, o_dil, o_mem, w_out, x, target, gf, tm):
    T = x.shape[0]

    def body(fg_ref, dg_ref, mg_ref, of_ref, od_ref, om_ref, w_ref, x_ref, t_ref, gf_ref,
             dx_ref, dof_ref, dod_ref, dom_ref, dfg_ref, ddg_ref, dmg_ref, gw_ref, sm_ref, gw_acc):
        branches = []
        for g_ref, o_ref in ((fg_ref, of_ref), (dg_ref, od_ref), (mg_ref, om_ref)):
            g = g_ref[...]
            sg = _sigmoid(g)
            o = o_ref[...]
            branches.append((g, sg, o))
        ymix = jnp.concatenate([(o * (g * sg)).astype(BF16) for g, sg, o in branches], axis=1)
        x2 = x_ref[...] + jnp.dot(ymix, w_ref[...], preferred_element_type=F32)
        r = lax.rsqrt(jnp.mean(x2 * x2, axis=-1, keepdims=True) + RMS_EPS)
        yn = x2 * r
        err = yn * gf_ref[...] - t_ref[...]
        loss = 0.5 * jnp.sum(jnp.sum(err * err, axis=-1, keepdims=True) / D_MODEL, axis=0, keepdims=True)
        dyf = err / D_MODEL
        dgf = jnp.sum(dyf * yn, axis=0, keepdims=True)
        dyn = dyf * gf_ref[...]
        dx2 = r * (dyn - yn * jnp.mean(dyn * yn, axis=-1, keepdims=True))
        dx_ref[...] = dx2
        dxb = dx2.astype(BF16)
        dmix = lax.dot_general(dxb, w_ref[...], (((1,), (1,)), ((), ())), preferred_element_type=F32)
        col = 0
        for (g, sg, o), do_ref, dgate_ref in zip(branches, (dof_ref, dod_ref, dom_ref), (dfg_ref, ddg_ref, dmg_ref)):
            d = dmix[:, col:col + g.shape[1]]
            col += g.shape[1]
            do_ref[...] = (d * (g * sg)).astype(BF16)
            dgate_ref[...] = (d * o * (sg * (1.0 + g * (1.0 - sg)))).astype(BF16)
        row = lax.broadcasted_iota(jnp.int32, (8, D_MODEL), 0)
        upd = jnp.where(row == 0, dgf, jnp.where(row == 1, loss, 0.0))

        @pl.when(pl.program_id(0) == 0)
        def _():
            sm_ref[...] = jnp.zeros(sm_ref.shape, F32)
            gw_acc[...] = jnp.zeros(gw_acc.shape, F32)

        sm_ref[...] += upd
        gw_acc[...] += lax.dot_general(ymix, dxb, (((0,), (0,)), ((), ())), preferred_element_type=F32)

        @pl.when(pl.program_id(0) == T // tm - 1)
        def _():
            gw_ref[...] = gw_acc[...].astype(BF16)

    def rows(w, col=0):
        return pl.BlockSpec((tm, w), lambda i: (i, col))

    return pl.pallas_call(
        body, name="out_step", grid=(T // tm,),
        in_specs=[rows(FOX_W, B_FG // FOX_W), rows(DIL_W, B_DG // DIL_W), rows(MEM_W, B_MG // MEM_W),
                  rows(FOX_W), rows(DIL_W), rows(MEM_W),
                  pl.BlockSpec((MIX_W, D_MODEL), lambda i: (0, 0)),
                  rows(D_MODEL), rows(D_MODEL), pl.BlockSpec((1, D_MODEL), lambda i: (0, 0))],
        out_specs=[rows(D_MODEL), rows(FOX_W), rows(DIL_W), rows(MEM_W), rows(FOX_W), rows(DIL_W), rows(MEM_W),
                   pl.BlockSpec((MIX_W, D_MODEL), lambda i: (0, 0)), pl.BlockSpec((8, D_MODEL), lambda i: (0, 0))],
        out_shape=[jax.ShapeDtypeStruct((T, D_MODEL), F32), jax.ShapeDtypeStruct((T, FOX_W), BF16),
                   jax.ShapeDtypeStruct((T, DIL_W), BF16), jax.ShapeDtypeStruct((T, MEM_W), BF16),
                   jax.ShapeDtypeStruct((T, FOX_W), BF16), jax.ShapeDtypeStruct((T, DIL_W), BF16),
                   jax.ShapeDtypeStruct((T, MEM_W), BF16), jax.ShapeDtypeStruct((MIX_W, D_MODEL), BF16),
                   jax.ShapeDtypeStruct((8, D_MODEL), F32)],
        scratch_shapes=[pltpu.VMEM((MIX_W, D_MODEL), F32)],
        compiler_params=_params(("arbitrary",)),
    )(proj, proj, proj, o_fox, o_dil, o_mem, w_out, x, target, gf)


def adamw_columns_first(w, g, m, v, name):
    N = w.shape[0]
    parts, step_rows = 4, 16
    size = -(-N // (parts * step_rows)) * step_rows
    bounds = [(p * size, min((p + 1) * size, N)) for p in range(parts)]

    def body(w_hbm, g_hbm, m_hbm, v_hbm, d_hbm, mo_hbm, vo_hbm, wb, gb, mb, vb, db, mob, vob, load_sems, store_sems):
        ins = ((w_hbm, wb), (g_hbm, gb), (m_hbm, mb), (v_hbm, vb))
        outs = ((d_hbm, db), (mo_hbm, mob), (vo_hbm, vob))

        def rows_copy(src, dst, lo, hi, sem):
            return pltpu.make_async_copy(src.at[pl.ds(lo, hi - lo), :], dst.at[pl.ds(lo, hi - lo), :], sem)

        def update(rows):
            db[rows, :], mob[rows, :], vob[rows, :] = _adamw_update(wb[rows, :], gb[rows, :], mb[rows, :], vb[rows, :])

        loads = [[rows_copy(h.reshape(N, LANES), b, lo, hi, load_sems.at[p, k]) for k, (h, b) in enumerate(ins)]
                 for p, (lo, hi) in enumerate(bounds)]
        for part in loads:
            for cp in part:
                cp.start()
        stores = []
        for p, (lo, hi) in enumerate(bounds):
            for cp in loads[p]:
                cp.wait()
            whole = (hi - lo) // step_rows

            def step(i, _, lo=lo):
                update(pl.ds(pl.multiple_of(lo + i * step_rows, step_rows), step_rows))
                return 0

            lax.fori_loop(0, whole, step, 0, unroll=4)
            if lo + whole * step_rows < hi:
                update(slice(lo + whole * step_rows, hi))
            leaving = [rows_copy(b, h.reshape(N, LANES), lo, hi, store_sems.at[p, k]) for k, (h, b) in enumerate(outs)]
            for cp in leaving:
                cp.start()
            stores += leaving
        for cp in stores:
            cp.wait()

    any_spec = pl.BlockSpec(memory_space=pl.ANY)
    return pl.pallas_call(
        body, name=name,
        in_specs=[any_spec] * 4, out_specs=[any_spec] * 3,
        out_shape=[jax.ShapeDtypeStruct(w.shape, F32)] * 3,
        scratch_shapes=[pltpu.VMEM((N, LANES), F32)] * 7
        + [pltpu.SemaphoreType.DMA((parts, 4)), pltpu.SemaphoreType.DMA((parts, 3))],
        compiler_params=_params(),
    )(w, g, m, v)


def _pad_row(v, width):
    return jnp.concatenate([v, jnp.zeros((1, width - v.shape[1]), v.dtype)], axis=1)


def local_grads(x, mem, norm_g, b_forget, mem_norm_g, final_norm_g, loss_target, first_token, first_weights,
                late_weights, start_exchange):
    B, S, D = x.shape
    T = B * S
    xt = x.reshape(T, D)
    memt = mem.reshape(B * MEM_LEN, D)
    b_pad = _pad_row(b_forget, LANES)

    h, h_t = rms_fwd(xt, norm_g, 512, "rms_x", with_transpose=True, token=first_token)
    mh, mh_t = rms_fwd(memt, mem_norm_g, B * MEM_LEN, "rms_mem", with_transpose=True, token=first_token)
    w_in_a, proj_token = first_weights([h, mh])
    proj_a = mm_nn(h, w_in_a, 1024, PA, "in_proj_a", proj_token)
    proj_a3 = proj_a.reshape(B, S, PA)

    negc = fox_gate(proj_a3, b_pad)
    causal = _log_masks_t(S, "causal")
    dilated = _log_masks_t(S, "dilated")
    rope = _rope_tables(S)

    o_fox, lse_fox = attn_fwd("fox", proj_a3, S, negc_cols=negc, mask=causal)

    w_in_b, w_kv, w_out = late_weights(o_fox)
    proj_b = mm_nn(h, w_in_b, 1024, PB // 2, "in_proj_b")
    proj_b3 = proj_b.reshape(B, S, PB)
    o_dil, lse_dil = attn_fwd("dil", proj_b3, S, mask=dilated, rope=rope)

    mkv = mm_nn(mh, w_kv, B * MEM_LEN, 2 * MEM_W, "mem_kv_proj")
    mkv3 = mkv.reshape(B, MEM_LEN, 2 * MEM_W)
    o_mem, lse_mem = attn_fwd("mem", proj_b3, S, kv=mkv3)

    dx2, do_fox, do_dil, do_mem, dfg, ddg, dmg, g_out, small_out = out_step(
        proj_b, o_fox.reshape(T, FOX_W), o_dil.reshape(T, DIL_W), o_mem.reshape(T, MEM_W), w_out,
        xt, loss_target.reshape(T, D), final_norm_g.reshape(1, D), 256)

    gates = [(dfg, 1, B_FG), (ddg, 1, B_DG), (dmg, 1, B_MG)]
    g_gates = mm_tn_multi(h_t, [piece[0] for piece in gates], 1024, "w_in_grad_gates", BF16)

    dqkv_fox, dneg, drow = attn_bwd("fox", proj_a3, do_fox.reshape(B, S, FOX_W), o_fox, lse_fox, S,
                                    negc_cols=negc, mask=causal)
    drow = drow.reshape(B, FOX_HEADS // 2, S // TQ, 2, TQ).transpose(0, 1, 3, 2, 4).reshape(B, FOX_HEADS, S)
    drow = jnp.pad(drow, ((0, 0), (0, LANES - FOX_HEADS), (0, 0)))
    dflog, db_part = fox_gate_bwd(drow, dneg, proj_a3, b_pad)
    fox = [(dqkv_fox.reshape(T, 3 * FOX_W), 0, A_FOX), (dflog.reshape(T, LANES), 0, A_FLOG)]
    g_fox = mm_tn_multi(h_t, [piece[0] for piece in fox], 1024, "w_in_grad_fox", BF16)
    first, token = start_exchange([g_gates, g_out, g_fox], "early_exchange_a")

    (dqkv_dil,) = attn_bwd("dil", proj_b3, do_dil.reshape(B, S, DIL_W), o_dil, lse_dil, S, mask=dilated, rope=rope,
                           token=token)
    dil = [(dqkv_dil.reshape(T, 3 * DIL_W), 1, B_DIL)]
    g_dil = mm_tn_multi(h_t, [piece[0] for piece in dil], 1024, "w_in_grad_dil", BF16)
    second, token = start_exchange([g_dil], "early_exchange_b")

    dmq, dmk, dmv = attn_bwd("mem", proj_b3, do_mem.reshape(B, S, MEM_W), o_mem, lse_mem, S, kv=mkv3, token=token)
    mq = [(dmq.reshape(T, MEM_W), 1, B_MQ)]
    g_mq = mm_tn_multi(h_t, [piece[0] for piece in mq], 1024, "w_in_grad_mq", BF16)
    dmkv = jnp.concatenate([dmk, dmv], axis=2).reshape(B * MEM_LEN, 2 * MEM_W)
    g_kv = mm_tn_multi(mh_t, [dmkv], B * MEM_LEN, "w_kv_grad", BF16)
    third, token = start_exchange([g_mq, g_kv], "early_exchange_c")

    grad_x, dng = proj_bwd_rms(gates + fox + dil + mq, (w_in_a, w_in_b), xt, norm_g, dx2, 512, token,
                               "in_proj_bwd")
    _, dmng = proj_bwd_rms([(dmkv, 0, 0)], (w_kv,), memt, mem_norm_g, None, B * MEM_LEN, token, "mem_kv_bwd")

    small = jnp.concatenate([dng[0:1], dmng[0:1], small_out[0:1], _pad_row(db_part[0:1], D), small_out[1:2],
                             jnp.zeros((3, D), F32)], axis=0)
    early = [(first, dqkv_dil), (second, dmq), (third, grad_x)]
    return grad_x.reshape(B, S, D), early, small


def kernel(x, mem, norm_g, w_in, b_forget, mem_norm_g, w_mem_kv, w_out, final_norm_g, loss_target, m_norm_g, m_w_in, m_b_forget, m_mem_norm_g, m_w_mem_kv, m_w_out, m_final_norm_g, v_norm_g, v_w_in, v_b_forget, v_mem_norm_g, v_w_mem_kv, v_w_out, v_final_norm_g):
    D = D_MODEL
    shard_a, shard_b = _split_cols(_pack_cols(w_in).astype(BF16).reshape(w_in.shape[1], PW))
    gather_a, first_token = early_exchange_start([shard_a], "first_gather", gather=True,
                                                 relations=_SIBLING_AND_SAME_CORES)
    late_shards = [shard_b, w_mem_kv[0].astype(BF16), w_out[0].astype(BF16)]
    late_lands = own_slots(late_shards, "late_gather_place", after=first_token)
    late = {}

    def first_weights(after):
        _, gathered = early_exchange_wait(gather_a, list(after) + [late_lands[0]], "first_gather_wait")
        (w_in_a,) = pass_on_to_sibling(gathered, gather_a["rows"], "first_gather_pass")
        late["gather"], token = early_exchange_start(
            late_shards, "late_gather", gather=True, after=w_in_a, relations=_SIBLING_AND_SAME_CORES,
            lands=late_lands)
        return w_in_a, token

    def late_weights(after):
        _, gathered = early_exchange_wait(late["gather"], after, "late_gather_wait")
        return pass_on_to_sibling(gathered, late["gather"]["rows"], "late_gather_pass")

    grad_x, early, small = local_grads(
        x, mem, norm_g, b_forget, mem_norm_g, final_norm_g, loss_target, first_token, first_weights, late_weights,
        early_exchange_start)

    (first, after_first), (second, after_second), (third, after_third) = early
    small_handle, small_token = early_exchange_start([jnp.tile(small, (N_DEV, 1))], "small_exchange")
    (src_gates, src_out, src_fox), (land_gates, land_out, land_fox) = early_exchange_wait(
        first, [after_first, small_token], "early_wait_a")
    (src_dil,), (land_dil,) = early_exchange_wait(second, after_second, "early_wait_b")
    (src_mq, src_kv), (land_mq, land_kv) = early_exchange_wait(third, after_third, "early_wait_c")
    gw_out, d_out, m_out, v_out = slot_sum8_adamw(src_out, land_out, w_out[0], m_w_out[0], v_w_out[0], 256,
                                                  "sum_adamw_w_out")
    gw_kv, d_kv, m_kv, v_kv = slot_sum8_adamw(src_kv, land_kv, w_mem_kv[0], m_w_mem_kv[0], v_w_mem_kv[0], 128,
                                              "sum_adamw_w_kv")
    gw_in_cols = w_in_grad_sum(
        [(src_mq, land_mq, [(0, P_MQ, MEM_W)]),
         (src_gates, land_gates, [(0, P_FG, FOX_W), (FOX_W, P_DG, DIL_W), (FOX_W + DIL_W, P_MG, MEM_W)]),
         (src_dil, land_dil, [(0, P_DIL, 3 * DIL_W)]),
         (src_fox, land_fox, [(0, P_FOX, 3 * FOX_W), (3 * FOX_W, P_FLOG, LANES)])], "sum_w_in")
    gw_in = jnp.transpose(gw_in_cols, (1, 2, 0))

    def rows8(*rows):
        rows = [r.reshape(1, -1) for r in rows]
        rows = [_pad_row(r, D) for r in rows]
        return jnp.concatenate(rows + [jnp.zeros((8 - len(rows), D), F32)], axis=0)

    sw = rows8(norm_g, mem_norm_g, final_norm_g, b_forget)
    sm = rows8(m_norm_g, m_mem_norm_g, m_final_norm_g, m_b_forget)
    sv = rows8(v_norm_g, v_mem_norm_g, v_final_norm_g, v_b_forget)
    (src_small,), (land_small,) = early_exchange_wait(small_handle, gw_in_cols, "small_wait")
    tot, d_s, m_s, v_s = slot_sum8_adamw(src_small, land_small, sw, sm, sv, 8, "sum_adamw_small")
    loss = tot[4, 0]
    g_norm, g_mem_norm, g_final, g_b = tot[0:1], tot[1:2], tot[2], tot[3:4, :FOX_HEADS]
    columns_first = lambda a: jnp.transpose(a, (2, 0, 1))
    d_in, m_in, v_in = [jnp.transpose(o, (1, 2, 0)) for o in adamw_columns_first(
        columns_first(w_in), gw_in_cols, columns_first(m_w_in), columns_first(v_w_in), "adamw_w_in")]

    def small_outs(t):
        return t[0:1], t[3:4, :FOX_HEADS], t[1:2], t[2]

    grads = (g_norm, gw_in, g_b, g_mem_norm, gw_kv[None], gw_out[None], g_final)
    outs = []
    for t, big in ((d_s, (d_in, d_kv, d_out)), (m_s, (m_in, m_kv, m_out)), (v_s, (v_in, v_kv, v_out))):
        n, b, mn, f = small_outs(t)
        outs += [n, big[0], b, mn, big[1][None], big[2][None], f]
    return (loss, grad_x, *grads, *outs)
```
